```python
import math
import jax, jax.numpy as jnp
from jax import lax
import numpy as np

D_MODEL = 1024
BATCH = 8
SEQ = 8192
DEPTH = 1

GRID_W = 64
CTX_LEN = 256
S5_WIDTH = 512
S5_GROUP = 16
S5_GROUPS = S5_WIDTH // S5_GROUP
S5_STATE = 64
DT_MIN = 1e-3
DT_MAX = 1e-1
RET_WIDTH = D_MODEL - S5_WIDTH
RET_HEADS = 4
RET_HEAD_DIM = RET_WIDTH // RET_HEADS
RET_CHUNK = 128
ROPE_THETA = 10000.0
IN_COLS = S5_WIDTH + 4 * RET_WIDTH
D_FF = 2816
CONV_W = 3
NORM_EPS = 1e-6

kernel_name = "hybrid_s5_retention_convglu_dit_layer"


def rms_norm(t, w):
    tf = t.astype(jnp.float32)
    y = tf * lax.rsqrt(jnp.mean(tf * tf, axis=-1, keepdims=True) + NORM_EPS)
    return (y * w.astype(jnp.float32)).astype(t.dtype)


def adaln(cond, w_mod, b_mod):
    m = jax.nn.silu(cond) @ w_mod + b_mod
    return [t.reshape(-1, 1, D_MODEL) for t in jnp.split(m, 6, axis=-1)]


def modulate(h, shift, scale):
    return h * (1.0 + scale) + shift


def rope_2d(t):
    n_tok = t.shape[1]
    rows = n_tok // GRID_W
    row = jnp.repeat(jnp.arange(rows, dtype=jnp.float32), GRID_W)
    col = jnp.tile(jnp.arange(GRID_W, dtype=jnp.float32), rows)
    n_freq = RET_HEAD_DIM // 4
    inv_freq = ROPE_THETA ** (-jnp.arange(n_freq, dtype=jnp.float32) / n_freq)
    ang = jnp.concatenate([row[:, None] * inv_freq, col[:, None] * inv_freq], axis=-1)
    cos = jnp.cos(ang)[None, :, None, :]
    sin = jnp.sin(ang)[None, :, None, :]
    tf = t.astype(jnp.float32)
    t1, t2 = tf[..., 0::2], tf[..., 1::2]
    out = jnp.stack([t1 * cos - t2 * sin, t1 * sin + t2 * cos], axis=-1).reshape(t.shape)
    return out.astype(t.dtype)


def split_projection(p, rotate):
    b, n_tok, _ = p.shape
    u = p[..., :S5_WIDTH].reshape(b, n_tok, S5_GROUPS, S5_GROUP)
    q, k, v, g = jnp.split(p[..., S5_WIDTH:], 4, axis=-1)
    q = q.reshape(b, n_tok, RET_HEADS, RET_HEAD_DIM)
    k = k.reshape(b, n_tok, RET_HEADS, RET_HEAD_DIM) * (RET_HEAD_DIM ** -0.5)
    v = v.reshape(b, n_tok, RET_HEADS, RET_HEAD_DIM)
    if rotate:
        q = rope_2d(q)
        k = rope_2d(k)
    return (u, q.transpose(0, 2, 1, 3), k.transpose(0, 2, 1, 3), v.transpose(0, 2, 1, 3), g)


def s5_discretize(lam_re, lam_im, log_step, b_re, b_im):
    lam = lax.complex(lam_re.astype(jnp.float32), lam_im.astype(jnp.float32))
    step = jnp.exp(log_step.astype(jnp.float32))[:, None]
    lam_bar = jnp.exp(lam * step)
    b_mat = lax.complex(b_re.astype(jnp.float32), b_im.astype(jnp.float32))
    b_bar = ((lam_bar - 1.0) / lam)[..., None] * b_mat
    return lam_bar, b_bar


def _linear_combine(left, right):
    a_l, b_l = left
    a_r, b_r = right
    return a_r * a_l, a_r * b_l + b_r


def s5_scan(u, lam_bar, b_bar, h0, reverse):
    bu = jnp.einsum('gnp,blgp->blgn', b_bar, u.astype(jnp.float32).astype(jnp.complex64))
    if reverse:
        bu = jnp.flip(bu, axis=1)
    bu = bu.at[:, 0].add(lam_bar * h0)
    a = jnp.broadcast_to(lam_bar, bu.shape)
    _, h = lax.associative_scan(_linear_combine, (a, bu), axis=1)
    if reverse:
        h = jnp.flip(h, axis=1)
    return h


def s5_readout(u, h_f, h_b, c_re, c_im, d, w_glu, b_glu):
    b, n_tok = u.shape[0], u.shape[1]
    c_mat = lax.complex(c_re.astype(jnp.float32), c_im.astype(jnp.float32))
    y = jnp.real(jnp.einsum('gpn,blgn->blgp', c_mat, h_f + h_b))
    y = y + d.astype(jnp.float32).reshape(S5_GROUPS, S5_GROUP) * u.astype(jnp.float32)
    y = jax.nn.gelu(y.reshape(b, n_tok, S5_WIDTH))
    return y * jax.nn.sigmoid(y @ w_glu.astype(jnp.float32) + b_glu.astype(jnp.float32))


def retention_chunkwise(q, k, v, log_decay, r0, strict):
    b, h, n_tok, dk = q.shape
    dv = v.shape[-1]
    n_chunks = n_tok // RET_CHUNK
    ld = log_decay.astype(jnp.float32)
    qc = q.astype(jnp.float32).reshape(b, h, n_chunks, RET_CHUNK, dk)
    kc = k.astype(jnp.float32).reshape(b, h, n_chunks, RET_CHUNK, dk)
    vc = v.astype(jnp.float32).reshape(b, h, n_chunks, RET_CHUNK, dv)
    pos = jnp.arange(RET_CHUNK, dtype=jnp.float32)
    diff = pos[:, None] - pos[None, :]
    keep = diff > 0 if strict else diff >= 0
    intra_decay = jnp.where(keep, jnp.exp(ld[:, None, None] * jnp.maximum(diff, 0.0)), 0.0)
    scores = jnp.einsum('bhncd,bhnmd->bhncm', qc, kc) * intra_decay[None, :, None]
    intra = jnp.einsum('bhncm,bhnme->bhnce', scores, vc)
    zeta = jnp.exp(ld[:, None] * (RET_CHUNK - 1.0 - pos))
    chunk_kv = jnp.einsum('bhnmd,bhnme->nbhde', kc * zeta[None, :, None, :, None], vc)
    chunk_decay = jnp.exp(ld * RET_CHUNK)[None, :, None, None]

    def step(state, kv):
        return chunk_decay * state + kv, state

    _, r_prev = lax.scan(step, r0.astype(jnp.float32), chunk_kv)
    xi = jnp.exp(ld[:, None] * (pos + 1.0))
    cross = jnp.einsum('bhncd,nbhde->bhnce', qc * xi[None, :, None, :, None], r_prev)
    return (intra + cross).reshape(b, h, n_tok, dv)


def retention_final_state(k, v, log_decay):
    n_tok = k.shape[2]
    ld = log_decay.astype(jnp.float32)
    w = jnp.exp(ld[:, None] * (n_tok - 1.0 - jnp.arange(n_tok, dtype=jnp.float32)))
    return jnp.einsum('bhld,bhle->bhde', k.astype(jnp.float32) * w[None, :, :, None], v.astype(jnp.float32))


def retention_mixer(q, k, v, g, ld_f, ld_b, r0_f, r0_b):
    out_f = retention_chunkwise(q, k, v, ld_f, r0_f, strict=False)
    out_b = jnp.flip(retention_chunkwise(jnp.flip(q, 2), jnp.flip(k, 2), jnp.flip(v, 2), ld_b, r0_b, strict=True), 2)
    y = out_f + out_b
    mu = jnp.mean(y, axis=-1, keepdims=True)
    var = jnp.mean((y - mu) ** 2, axis=-1, keepdims=True)
    y = (y - mu) * lax.rsqrt(var + NORM_EPS)
    b, h, n_tok, dv = y.shape
    y = y.transpose(0, 2, 1, 3).reshape(b, n_tok, h * dv)
    return jax.nn.silu(g.astype(jnp.float32)) * y


def conv_ffn(h, w_up, conv_w, conv_b, w_down):
    a, g = jnp.split(h @ w_up, 2, axis=-1)
    n_tok = g.shape[1]
    half = CONV_W // 2
    gp = jnp.pad(g, ((0, 0), (half, half), (0, 0)))
    g_conv = conv_b + gp[:, 0:n_tok] * conv_w[0]
    for j in range(1, CONV_W):
        g_conv = g_conv + gp[:, j:j + n_tok] * conv_w[j]
    return (jax.nn.gelu(g_conv) * a) @ w_down


def _fwd_setup_inputs(seed: int = 0) -> dict:
    key = jax.random.key(seed)
    ks = jax.random.split(key, 32)
    f32 = jnp.float32

    def nrm(k, shape, s):
        return s * jax.random.normal(k, shape, f32)

    gshape = (DEPTH, S5_GROUPS, S5_STATE)
    lam_re = -0.5 * jnp.ones(gshape, f32)
    lam_im = math.pi * jnp.broadcast_to(jnp.arange(S5_STATE, dtype=f32), gshape)

    def log_dt(k):
        return math.log(DT_MIN) + jax.random.uniform(k, (DEPTH, S5_GROUPS), f32) * (math.log(DT_MAX) - math.log(DT_MIN))

    base_decay = jnp.log(1.0 - 2.0 ** (-5.0 - jnp.arange(RET_HEADS, dtype=f32)))
    return {
        "x": nrm(ks[0], (BATCH, SEQ, D_MODEL), 1.0),
        "c": nrm(ks[1], (BATCH, D_MODEL), 1.0),
        "ctx": nrm(ks[2], (BATCH, CTX_LEN, D_MODEL), 1.0),
        "c_ctx": nrm(ks[3], (D_MODEL,), 1.0),
        "w_mod": nrm(ks[4], (DEPTH, D_MODEL, 6 * D_MODEL), 0.5 * D_MODEL ** -0.5),
        "b_mod": nrm(ks[5], (DEPTH, 6 * D_MODEL), 0.01),
        "norm1_w": 1.0 + nrm(ks[6], (DEPTH, D_MODEL), 0.02),
        "w_in": nrm(ks[7], (DEPTH, D_MODEL, IN_COLS), D_MODEL ** -0.5),
        "s5_lambda_re_f": lam_re + nrm(ks[8], gshape, 0.01),
        "s5_lambda_im_f": lam_im + nrm(ks[9], gshape, 0.01),
        "s5_log_step_f": log_dt(ks[10]),
        "s5_lambda_re_b": lam_re + nrm(ks[11], gshape, 0.01),
        "s5_lambda_im_b": lam_im + nrm(ks[12], gshape, 0.01),
        "s5_log_step_b": log_dt(ks[13]),
        "s5_b_re": nrm(ks[14], (DEPTH, S5_GROUPS, S5_STATE, S5_GROUP), (2.0 * S5_GROUP) ** -0.5),
        "s5_b_im": nrm(ks[15], (DEPTH, S5_GROUPS, S5_STATE, S5_GROUP), (2.0 * S5_GROUP) ** -0.5),
        "s5_c_re": nrm(ks[16], (DEPTH, S5_GROUPS, S5_GROUP, S5_STATE), 0.5),
        "s5_c_im": nrm(ks[17], (DEPTH, S5_GROUPS, S5_GROUP, S5_STATE), 0.5),
        "s5_d": nrm(ks[18], (DEPTH, S5_WIDTH), 0.5),
        "s5_w_glu": nrm(ks[19], (DEPTH, S5_WIDTH, S5_WIDTH), S5_WIDTH ** -0.5),
        "s5_b_glu": nrm(ks[20], (DEPTH, S5_WIDTH), 0.01),
        "ret_log_decay_f": base_decay * jnp.exp(nrm(ks[21], (DEPTH, RET_HEADS), 0.05)),
        "ret_log_decay_b": base_decay * jnp.exp(nrm(ks[22], (DEPTH, RET_HEADS), 0.05)),
        "w_out": nrm(ks[23], (DEPTH, D_MODEL, D_MODEL), D_MODEL ** -0.5),
        "norm2_w": 1.0 + nrm(ks[24], (DEPTH, D_MODEL), 0.02),
        "w_up": nrm(ks[25], (DEPTH, D_MODEL, 2 * D_FF), D_MODEL ** -0.5),
        "conv_w": nrm(ks[26], (DEPTH, CONV_W, D_FF), CONV_W ** -0.5),
        "conv_b": nrm(ks[27], (DEPTH, D_FF), 0.01),
        "w_down": nrm(ks[28], (DEPTH, D_FF, D_MODEL), D_FF ** -0.5),
        "final_norm_w": 1.0 + nrm(ks[29], (D_MODEL,), 0.02),
    }


def _fwd_reference(x, c, ctx, c_ctx, w_mod, b_mod, norm1_w, w_in,
              s5_lambda_re_f, s5_lambda_im_f, s5_log_step_f,
              s5_lambda_re_b, s5_lambda_im_b, s5_log_step_b,
              s5_b_re, s5_b_im, s5_c_re, s5_c_im, s5_d, s5_w_glu, s5_b_glu,
              ret_log_decay_f, ret_log_decay_b, w_out,
              norm2_w, w_up, conv_w, conv_b, w_down, final_norm_w):
    batch = x.shape[0]
    zero_s5 = jnp.zeros((batch, S5_GROUPS, S5_STATE), jnp.complex64)
    zero_ret = jnp.zeros((batch, RET_HEADS, RET_HEAD_DIM, RET_HEAD_DIM), jnp.float32)
    for layer in range(DEPTH):
        mx = adaln(c, w_mod[layer], b_mod[layer])
        mc = adaln(c_ctx, w_mod[layer], b_mod[layer])
        hx = modulate(rms_norm(x, norm1_w[layer]), mx[0], mx[1])
        hc = modulate(rms_norm(ctx, norm1_w[layer]), mc[0], mc[1])
        ux, qx, kx, vx, gx = split_projection(hx @ w_in[layer], rotate=True)
        uc, qc, kc, vc, gc = split_projection(hc @ w_in[layer], rotate=False)
        lam_f, bbar_f = s5_discretize(s5_lambda_re_f[layer], s5_lambda_im_f[layer], s5_log_step_f[layer], s5_b_re[layer], s5_b_im[layer])
        lam_b, bbar_b = s5_discretize(s5_lambda_re_b[layer], s5_lambda_im_b[layer], s5_log_step_b[layer], s5_b_re[layer], s5_b_im[layer])
        hc_f = s5_scan(uc, lam_f, bbar_f, zero_s5, reverse=False)
        hc_b = s5_scan(uc, lam_b, bbar_b, zero_s5, reverse=True)
        rc_f = retention_final_state(kc, vc, ret_log_decay_f[layer])
        rc_b = retention_final_state(jnp.flip(kc, 2), jnp.flip(vc, 2), ret_log_decay_b[layer])
        hx_f = s5_scan(ux, lam_f, bbar_f, hc_f[:, -1], reverse=False)
        hx_b = s5_scan(ux, lam_b, bbar_b, hc_b[:, 0], reverse=True)
        s5_x = s5_readout(ux, hx_f, hx_b, s5_c_re[layer], s5_c_im[layer], s5_d[layer], s5_w_glu[layer], s5_b_glu[layer])
        ret_x = retention_mixer(qx, kx, vx, gx, ret_log_decay_f[layer], ret_log_decay_b[layer], rc_f, rc_b)
        mix_x = jnp.concatenate([s5_x, ret_x], axis=-1).astype(x.dtype) @ w_out[layer]
        x = x + mx[2] * mix_x
        hx2 = modulate(rms_norm(x, norm2_w[layer]), mx[3], mx[4])
        x = x + mx[5] * conv_ffn(hx2, w_up[layer], conv_w[layer], conv_b[layer], w_down[layer])
        if layer < DEPTH - 1:
            s5_c = s5_readout(uc, hc_f, hc_b, s5_c_re[layer], s5_c_im[layer], s5_d[layer], s5_w_glu[layer], s5_b_glu[layer])
            ret_c = retention_mixer(qc, kc, vc, gc, ret_log_decay_f[layer], ret_log_decay_b[layer], zero_ret, zero_ret)
            ctx = ctx + mc[2] * (jnp.concatenate([s5_c, ret_c], axis=-1).astype(ctx.dtype) @ w_out[layer])
            hc2 = modulate(rms_norm(ctx, norm2_w[layer]), mc[3], mc[4])
            ctx = ctx + mc[5] * conv_ffn(hc2, w_up[layer], conv_w[layer], conv_b[layer], w_down[layer])
    return rms_norm(x, final_norm_w)


import jax as _jax
import jax.numpy as _jnp

TWIN_FORMAT = 'train_step'
FWD_PARAMS = ['x', 'c', 'ctx', 'c_ctx', 'w_mod', 'b_mod', 'norm1_w', 'w_in', 's5_lambda_re_f', 's5_lambda_im_f', 's5_log_step_f', 's5_lambda_re_b', 's5_lambda_im_b', 's5_log_step_b', 's5_b_re', 's5_b_im', 's5_c_re', 's5_c_im', 's5_d', 's5_w_glu', 's5_b_glu', 'ret_log_decay_f', 'ret_log_decay_b', 'w_out', 'norm2_w', 'w_up', 'conv_w', 'conv_b', 'w_down', 'final_norm_w']
TWIN_WEIGHTS = ['c_ctx', 'w_mod', 'b_mod', 'norm1_w', 'w_in', 's5_lambda_re_f', 's5_lambda_im_f', 's5_log_step_f', 's5_lambda_re_b', 's5_lambda_im_b', 's5_log_step_b', 's5_b_re', 's5_b_im', 's5_c_re', 's5_c_im', 's5_d', 's5_w_glu', 's5_b_glu', 'ret_log_decay_f', 'ret_log_decay_b', 'w_out', 'norm2_w', 'w_up', 'conv_w', 'conv_b', 'w_down', 'final_norm_w']
TWIN_DIFF_INPUT = 'x'
TWIN_INPUTS = ['x', 'c', 'ctx', 'c_ctx', 'w_mod', 'b_mod', 'norm1_w', 'w_in', 's5_lambda_re_f', 's5_lambda_im_f', 's5_log_step_f', 's5_lambda_re_b', 's5_lambda_im_b', 's5_log_step_b', 's5_b_re', 's5_b_im', 's5_c_re', 's5_c_im', 's5_d', 's5_w_glu', 's5_b_glu', 'ret_log_decay_f', 'ret_log_decay_b', 'w_out', 'norm2_w', 'w_up', 'conv_w', 'conv_b', 'w_down', 'final_norm_w', 'loss_target', 'm_c_ctx', 'm_w_mod', 'm_b_mod', 'm_norm1_w', 'm_w_in', 'm_s5_lambda_re_f', 'm_s5_lambda_im_f', 'm_s5_log_step_f', 'm_s5_lambda_re_b', 'm_s5_lambda_im_b', 'm_s5_log_step_b', 'm_s5_b_re', 'm_s5_b_im', 'm_s5_c_re', 'm_s5_c_im', 'm_s5_d', 'm_s5_w_glu', 'm_s5_b_glu', 'm_ret_log_decay_f', 'm_ret_log_decay_b', 'm_w_out', 'm_norm2_w', 'm_w_up', 'm_conv_w', 'm_conv_b', 'm_w_down', 'm_final_norm_w', 'v_c_ctx', 'v_w_mod', 'v_b_mod', 'v_norm1_w', 'v_w_in', 'v_s5_lambda_re_f', 'v_s5_lambda_im_f', 'v_s5_log_step_f', 'v_s5_lambda_re_b', 'v_s5_lambda_im_b', 'v_s5_log_step_b', 'v_s5_b_re', 'v_s5_b_im', 'v_s5_c_re', 'v_s5_c_im', 'v_s5_d', 'v_s5_w_glu', 'v_s5_b_glu', 'v_ret_log_decay_f', 'v_ret_log_decay_b', 'v_w_out', 'v_norm2_w', 'v_w_up', 'v_conv_w', 'v_conv_b', 'v_w_down', 'v_final_norm_w']
TWIN_OUTPUTS = ['loss', 'grad_x', 'grad_c_ctx', 'grad_w_mod', 'grad_b_mod', 'grad_norm1_w', 'grad_w_in', 'grad_s5_lambda_re_f', 'grad_s5_lambda_im_f', 'grad_s5_log_step_f', 'grad_s5_lambda_re_b', 'grad_s5_lambda_im_b', 'grad_s5_log_step_b', 'grad_s5_b_re', 'grad_s5_b_im', 'grad_s5_c_re', 'grad_s5_c_im', 'grad_s5_d', 'grad_s5_w_glu', 'grad_s5_b_glu', 'grad_ret_log_decay_f', 'grad_ret_log_decay_b', 'grad_w_out', 'grad_norm2_w', 'grad_w_up', 'grad_conv_w', 'grad_conv_b', 'grad_w_down', 'grad_final_norm_w', 'delta_c_ctx', 'delta_w_mod', 'delta_b_mod', 'delta_norm1_w', 'delta_w_in', 'delta_s5_lambda_re_f', 'delta_s5_lambda_im_f', 'delta_s5_log_step_f', 'delta_s5_lambda_re_b', 'delta_s5_lambda_im_b', 'delta_s5_log_step_b', 'delta_s5_b_re', 'delta_s5_b_im', 'delta_s5_c_re', 'delta_s5_c_im', 'delta_s5_d', 'delta_s5_w_glu', 'delta_s5_b_glu', 'delta_ret_log_decay_f', 'delta_ret_log_decay_b', 'delta_w_out', 'delta_norm2_w', 'delta_w_up', 'delta_conv_w', 'delta_conv_b', 'delta_w_down', 'delta_final_norm_w', 'new_m_c_ctx', 'new_m_w_mod', 'new_m_b_mod', 'new_m_norm1_w', 'new_m_w_in', 'new_m_s5_lambda_re_f', 'new_m_s5_lambda_im_f', 'new_m_s5_log_step_f', 'new_m_s5_lambda_re_b', 'new_m_s5_lambda_im_b', 'new_m_s5_log_step_b', 'new_m_s5_b_re', 'new_m_s5_b_im', 'new_m_s5_c_re', 'new_m_s5_c_im', 'new_m_s5_d', 'new_m_s5_w_glu', 'new_m_s5_b_glu', 'new_m_ret_log_decay_f', 'new_m_ret_log_decay_b', 'new_m_w_out', 'new_m_norm2_w', 'new_m_w_up', 'new_m_conv_w', 'new_m_conv_b', 'new_m_w_down', 'new_m_final_norm_w', 'new_v_c_ctx', 'new_v_w_mod', 'new_v_b_mod', 'new_v_norm1_w', 'new_v_w_in', 'new_v_s5_lambda_re_f', 'new_v_s5_lambda_im_f', 'new_v_s5_log_step_f', 'new_v_s5_lambda_re_b', 'new_v_s5_lambda_im_b', 'new_v_s5_log_step_b', 'new_v_s5_b_re', 'new_v_s5_b_im', 'new_v_s5_c_re', 'new_v_s5_c_im', 'new_v_s5_d', 'new_v_s5_w_glu', 'new_v_s5_b_glu', 'new_v_ret_log_decay_f', 'new_v_ret_log_decay_b', 'new_v_w_out', 'new_v_norm2_w', 'new_v_w_up', 'new_v_conv_w', 'new_v_conv_b', 'new_v_w_down', 'new_v_final_norm_w']
TWIN_LEAF_KINDS = {'loss': 'loss', 'grad_x': 'grad_x', 'grad_c_ctx': 'grad_w', 'grad_w_mod': 'grad_w', 'grad_b_mod': 'grad_w', 'grad_norm1_w': 'grad_w', 'grad_w_in': 'grad_w', 'grad_s5_lambda_re_f': 'grad_w', 'grad_s5_lambda_im_f': 'grad_w', 'grad_s5_log_step_f': 'grad_w', 'grad_s5_lambda_re_b': 'grad_w', 'grad_s5_lambda_im_b': 'grad_w', 'grad_s5_log_step_b': 'grad_w', 'grad_s5_b_re': 'grad_w', 'grad_s5_b_im': 'grad_w', 'grad_s5_c_re': 'grad_w', 'grad_s5_c_im': 'grad_w', 'grad_s5_d': 'grad_w', 'grad_s5_w_glu': 'grad_w', 'grad_s5_b_glu': 'grad_w', 'grad_ret_log_decay_f': 'grad_w', 'grad_ret_log_decay_b': 'grad_w', 'grad_w_out': 'grad_w', 'grad_norm2_w': 'grad_w', 'grad_w_up': 'grad_w', 'grad_conv_w': 'grad_w', 'grad_conv_b': 'grad_w', 'grad_w_down': 'grad_w', 'grad_final_norm_w': 'grad_w', 'delta_c_ctx': 'delta_w', 'delta_w_mod': 'delta_w', 'delta_b_mod': 'delta_w', 'delta_norm1_w': 'delta_w', 'delta_w_in': 'delta_w', 'delta_s5_lambda_re_f': 'delta_w', 'delta_s5_lambda_im_f': 'delta_w', 'delta_s5_log_step_f': 'delta_w', 'delta_s5_lambda_re_b': 'delta_w', 'delta_s5_lambda_im_b': 'delta_w', 'delta_s5_log_step_b': 'delta_w', 'delta_s5_b_re': 'delta_w', 'delta_s5_b_im': 'delta_w', 'delta_s5_c_re': 'delta_w', 'delta_s5_c_im': 'delta_w', 'delta_s5_d': 'delta_w', 'delta_s5_w_glu': 'delta_w', 'delta_s5_b_glu': 'delta_w', 'delta_ret_log_decay_f': 'delta_w', 'delta_ret_log_decay_b': 'delta_w', 'delta_w_out': 'delta_w', 'delta_norm2_w': 'delta_w', 'delta_w_up': 'delta_w', 'delta_conv_w': 'delta_w', 'delta_conv_b': 'delta_w', 'delta_w_down': 'delta_w', 'delta_final_norm_w': 'delta_w', 'new_m_c_ctx': 'new_m', 'new_m_w_mod': 'new_m', 'new_m_b_mod': 'new_m', 'new_m_norm1_w': 'new_m', 'new_m_w_in': 'new_m', 'new_m_s5_lambda_re_f': 'new_m', 'new_m_s5_lambda_im_f': 'new_m', 'new_m_s5_log_step_f': 'new_m', 'new_m_s5_lambda_re_b': 'new_m', 'new_m_s5_lambda_im_b': 'new_m', 'new_m_s5_log_step_b': 'new_m', 'new_m_s5_b_re': 'new_m', 'new_m_s5_b_im': 'new_m', 'new_m_s5_c_re': 'new_m', 'new_m_s5_c_im': 'new_m', 'new_m_s5_d': 'new_m', 'new_m_s5_w_glu': 'new_m', 'new_m_s5_b_glu': 'new_m', 'new_m_ret_log_decay_f': 'new_m', 'new_m_ret_log_decay_b': 'new_m', 'new_m_w_out': 'new_m', 'new_m_norm2_w': 'new_m', 'new_m_w_up': 'new_m', 'new_m_conv_w': 'new_m', 'new_m_conv_b': 'new_m', 'new_m_w_down': 'new_m', 'new_m_final_norm_w': 'new_m', 'new_v_c_ctx': 'new_v', 'new_v_w_mod': 'new_v', 'new_v_b_mod': 'new_v', 'new_v_norm1_w': 'new_v', 'new_v_w_in': 'new_v', 'new_v_s5_lambda_re_f': 'new_v', 'new_v_s5_lambda_im_f': 'new_v', 'new_v_s5_log_step_f': 'new_v', 'new_v_s5_lambda_re_b': 'new_v', 'new_v_s5_lambda_im_b': 'new_v', 'new_v_s5_log_step_b': 'new_v', 'new_v_s5_b_re': 'new_v', 'new_v_s5_b_im': 'new_v', 'new_v_s5_c_re': 'new_v', 'new_v_s5_c_im': 'new_v', 'new_v_s5_d': 'new_v', 'new_v_s5_w_glu': 'new_v', 'new_v_s5_b_glu': 'new_v', 'new_v_ret_log_decay_f': 'new_v', 'new_v_ret_log_decay_b': 'new_v', 'new_v_w_out': 'new_v', 'new_v_norm2_w': 'new_v', 'new_v_w_up': 'new_v', 'new_v_conv_w': 'new_v', 'new_v_conv_b': 'new_v', 'new_v_w_down': 'new_v', 'new_v_final_norm_w': 'new_v'}


def _forward(args):
    return _fwd_reference(*[args[k] for k in FWD_PARAMS])


def _output_shape():
    def fwd():
        inp = _fwd_setup_inputs(0)
        return _fwd_reference(*[inp[k] for k in FWD_PARAMS])
    out = _jax.eval_shape(fwd)
    return out.shape, out.dtype

N_MICROBATCH = 1
ADAM_LR = 0.001
ADAM_B1 = 0.9
ADAM_B2 = 0.999
ADAM_EPS = 1e-08
ADAM_WD = 0.01
ADAM_STEP = 10
PER_EXAMPLE_BATCH_AXIS = {'x': 0, 'c': 0, 'ctx': 0, 'loss_target': 0}
SHARED_INPUTS = []
_WEIGHT_DTYPES = {'c_ctx': _jnp.float32, 'w_mod': _jnp.float32, 'b_mod': _jnp.float32, 'norm1_w': _jnp.float32, 'w_in': _jnp.float32, 's5_lambda_re_f': _jnp.float32, 's5_lambda_im_f': _jnp.float32, 's5_log_step_f': _jnp.float32, 's5_lambda_re_b': _jnp.float32, 's5_lambda_im_b': _jnp.float32, 's5_log_step_b': _jnp.float32, 's5_b_re': _jnp.float32, 's5_b_im': _jnp.float32, 's5_c_re': _jnp.float32, 's5_c_im': _jnp.float32, 's5_d': _jnp.float32, 's5_w_glu': _jnp.float32, 's5_b_glu': _jnp.float32, 'ret_log_decay_f': _jnp.float32, 'ret_log_decay_b': _jnp.float32, 'w_out': _jnp.float32, 'norm2_w': _jnp.float32, 'w_up': _jnp.float32, 'conv_w': _jnp.float32, 'conv_b': _jnp.float32, 'w_down': _jnp.float32, 'final_norm_w': _jnp.float32}
MOMENT_SCALE = {'c_ctx': 1.891646e-02, 'w_mod': 7.743245e-02, 'b_mod': 1.291798e-01, 'norm1_w': 7.301217e-02, 'w_in': 5.057356e-02, 's5_lambda_re_f': 1.080831e-02, 's5_lambda_im_f': 8.429072e-03, 's5_log_step_f': 4.314213e+00, 's5_lambda_re_b': 1.314249e-02, 's5_lambda_im_b': 1.212715e-02, 's5_log_step_b': 5.568236e+00, 's5_b_re': 1.279262e-02, 's5_b_im': 1.033140e-02, 's5_c_re': 4.080249e-03, 's5_c_im': 3.982257e-03, 's5_d': 3.280705e-02, 's5_w_glu': 7.136155e-03, 's5_b_glu': 1.195911e-02, 'ret_log_decay_f': 5.038331e+01, 'ret_log_decay_b': 7.362655e+01, 'w_out': 3.943641e-02, 'norm2_w': 7.946726e-02, 'w_up': 3.545084e-02, 'conv_w': 3.648263e-02, 'conv_b': 3.263875e-02, 'w_down': 5.782577e-02, 'final_norm_w': 6.399227e+01}


def _to_microbatches(a, axis):
    t = _jnp.moveaxis(a, axis, 0)
    t = t.reshape((N_MICROBATCH, t.shape[0] // N_MICROBATCH) + t.shape[1:])
    return _jnp.moveaxis(t, 1, axis + 1)


def setup_inputs(seed: int = 0) -> dict:
    inp = _fwd_setup_inputs(seed)
    key = _jax.random.fold_in(_jax.random.key(seed), 7919)
    shape, _ = _output_shape()
    out = dict(inp)
    out["loss_target"] = _jax.random.normal(_jax.random.fold_in(key, 0), shape, _jnp.float32)
    for i, name in enumerate(TWIN_WEIGHTS):
        w = inp[name].astype(_jnp.float32)
        if MOMENT_SCALE is None:
            s = _jnp.sqrt(_jnp.mean(_jnp.square(w)) + 1e-30)
        else:
            s = MOMENT_SCALE[name]
        km, kv = _jax.random.split(_jax.random.fold_in(key, i + 1))
        out[name] = w
        out["m_" + name] = s * _jax.random.normal(km, w.shape, _jnp.float32)
        out["v_" + name] = (s * s) * _jax.random.uniform(kv, w.shape, _jnp.float32, 0.5, 1.5)
    if N_MICROBATCH > 1:
        for name, axis in PER_EXAMPLE_BATCH_AXIS.items():
            out[name] = _to_microbatches(out[name], axis)
    return {'x': out['x'], 'c': out['c'], 'ctx': out['ctx'], 'c_ctx': out['c_ctx'], 'w_mod': out['w_mod'], 'b_mod': out['b_mod'], 'norm1_w': out['norm1_w'], 'w_in': out['w_in'], 's5_lambda_re_f': out['s5_lambda_re_f'], 's5_lambda_im_f': out['s5_lambda_im_f'], 's5_log_step_f': out['s5_log_step_f'], 's5_lambda_re_b': out['s5_lambda_re_b'], 's5_lambda_im_b': out['s5_lambda_im_b'], 's5_log_step_b': out['s5_log_step_b'], 's5_b_re': out['s5_b_re'], 's5_b_im': out['s5_b_im'], 's5_c_re': out['s5_c_re'], 's5_c_im': out['s5_c_im'], 's5_d': out['s5_d'], 's5_w_glu': out['s5_w_glu'], 's5_b_glu': out['s5_b_glu'], 'ret_log_decay_f': out['ret_log_decay_f'], 'ret_log_decay_b': out['ret_log_decay_b'], 'w_out': out['w_out'], 'norm2_w': out['norm2_w'], 'w_up': out['w_up'], 'conv_w': out['conv_w'], 'conv_b': out['conv_b'], 'w_down': out['w_down'], 'final_norm_w': out['final_norm_w'], 'loss_target': out['loss_target'], 'm_c_ctx': out['m_c_ctx'], 'm_w_mod': out['m_w_mod'], 'm_b_mod': out['m_b_mod'], 'm_norm1_w': out['m_norm1_w'], 'm_w_in': out['m_w_in'], 'm_s5_lambda_re_f': out['m_s5_lambda_re_f'], 'm_s5_lambda_im_f': out['m_s5_lambda_im_f'], 'm_s5_log_step_f': out['m_s5_log_step_f'], 'm_s5_lambda_re_b': out['m_s5_lambda_re_b'], 'm_s5_lambda_im_b': out['m_s5_lambda_im_b'], 'm_s5_log_step_b': out['m_s5_log_step_b'], 'm_s5_b_re': out['m_s5_b_re'], 'm_s5_b_im': out['m_s5_b_im'], 'm_s5_c_re': out['m_s5_c_re'], 'm_s5_c_im': out['m_s5_c_im'], 'm_s5_d': out['m_s5_d'], 'm_s5_w_glu': out['m_s5_w_glu'], 'm_s5_b_glu': out['m_s5_b_glu'], 'm_ret_log_decay_f': out['m_ret_log_decay_f'], 'm_ret_log_decay_b': out['m_ret_log_decay_b'], 'm_w_out': out['m_w_out'], 'm_norm2_w': out['m_norm2_w'], 'm_w_up': out['m_w_up'], 'm_conv_w': out['m_conv_w'], 'm_conv_b': out['m_conv_b'], 'm_w_down': out['m_w_down'], 'm_final_norm_w': out['m_final_norm_w'], 'v_c_ctx': out['v_c_ctx'], 'v_w_mod': out['v_w_mod'], 'v_b_mod': out['v_b_mod'], 'v_norm1_w': out['v_norm1_w'], 'v_w_in': out['v_w_in'], 'v_s5_lambda_re_f': out['v_s5_lambda_re_f'], 'v_s5_lambda_im_f': out['v_s5_lambda_im_f'], 'v_s5_log_step_f': out['v_s5_log_step_f'], 'v_s5_lambda_re_b': out['v_s5_lambda_re_b'], 'v_s5_lambda_im_b': out['v_s5_lambda_im_b'], 'v_s5_log_step_b': out['v_s5_log_step_b'], 'v_s5_b_re': out['v_s5_b_re'], 'v_s5_b_im': out['v_s5_b_im'], 'v_s5_c_re': out['v_s5_c_re'], 'v_s5_c_im': out['v_s5_c_im'], 'v_s5_d': out['v_s5_d'], 'v_s5_w_glu': out['v_s5_w_glu'], 'v_s5_b_glu': out['v_s5_b_glu'], 'v_ret_log_decay_f': out['v_ret_log_decay_f'], 'v_ret_log_decay_b': out['v_ret_log_decay_b'], 'v_w_out': out['v_w_out'], 'v_norm2_w': out['v_norm2_w'], 'v_w_up': out['v_w_up'], 'v_conv_w': out['v_conv_w'], 'v_conv_b': out['v_conv_b'], 'v_w_down': out['v_w_down'], 'v_final_norm_w': out['v_final_norm_w']}


def _loss(weights, diff, rest, loss_target):
    with _jax.named_scope("forward"):
        args = {**rest, TWIN_DIFF_INPUT: diff, **{k: w.astype(_WEIGHT_DTYPES[k]) for k, w in weights.items()}}
        y = _forward(args)
    with _jax.named_scope("loss_head"):
        err = _jnp.square(y.astype(_jnp.float32) - loss_target)
        return 0.5 * _jnp.sum(_jnp.mean(err, axis=-1)) if err.ndim else 0.5 * err


def _adamw(w, g, m, v):
    m = ADAM_B1 * m + (1.0 - ADAM_B1) * g
    v = ADAM_B2 * v + (1.0 - ADAM_B2) * _jnp.square(g)
    m_hat = m / (1.0 - ADAM_B1 ** ADAM_STEP)
    v_hat = v / (1.0 - ADAM_B2 ** ADAM_STEP)
    delta = -ADAM_LR * (m_hat / (_jnp.sqrt(v_hat) + ADAM_EPS) + ADAM_WD * w)
    return delta, m, v


def reference(x, c, ctx, c_ctx, w_mod, b_mod, norm1_w, w_in, s5_lambda_re_f, s5_lambda_im_f, s5_log_step_f, s5_lambda_re_b, s5_lambda_im_b, s5_log_step_b, s5_b_re, s5_b_im, s5_c_re, s5_c_im, s5_d, s5_w_glu, s5_b_glu, ret_log_decay_f, ret_log_decay_b, w_out, norm2_w, w_up, conv_w, conv_b, w_down, final_norm_w, loss_target, m_c_ctx, m_w_mod, m_b_mod, m_norm1_w, m_w_in, m_s5_lambda_re_f, m_s5_lambda_im_f, m_s5_log_step_f, m_s5_lambda_re_b, m_s5_lambda_im_b, m_s5_log_step_b, m_s5_b_re, m_s5_b_im, m_s5_c_re, m_s5_c_im, m_s5_d, m_s5_w_glu, m_s5_b_glu, m_ret_log_decay_f, m_ret_log_decay_b, m_w_out, m_norm2_w, m_w_up, m_conv_w, m_conv_b, m_w_down, m_final_norm_w, v_c_ctx, v_w_mod, v_b_mod, v_norm1_w, v_w_in, v_s5_lambda_re_f, v_s5_lambda_im_f, v_s5_log_step_f, v_s5_lambda_re_b, v_s5_lambda_im_b, v_s5_log_step_b, v_s5_b_re, v_s5_b_im, v_s5_c_re, v_s5_c_im, v_s5_d, v_s5_w_glu, v_s5_b_glu, v_ret_log_decay_f, v_ret_log_decay_b, v_w_out, v_norm2_w, v_w_up, v_conv_w, v_conv_b, v_w_down, v_final_norm_w):
    given = dict(x=x, c=c, ctx=ctx, c_ctx=c_ctx, w_mod=w_mod, b_mod=b_mod, norm1_w=norm1_w, w_in=w_in, s5_lambda_re_f=s5_lambda_re_f, s5_lambda_im_f=s5_lambda_im_f, s5_log_step_f=s5_log_step_f, s5_lambda_re_b=s5_lambda_re_b, s5_lambda_im_b=s5_lambda_im_b, s5_log_step_b=s5_log_step_b, s5_b_re=s5_b_re, s5_b_im=s5_b_im, s5_c_re=s5_c_re, s5_c_im=s5_c_im, s5_d=s5_d, s5_w_glu=s5_w_glu, s5_b_glu=s5_b_glu, ret_log_decay_f=ret_log_decay_f, ret_log_decay_b=ret_log_decay_b, w_out=w_out, norm2_w=norm2_w, w_up=w_up, conv_w=conv_w, conv_b=conv_b, w_down=w_down, final_norm_w=final_norm_w, loss_target=loss_target, m_c_ctx=m_c_ctx, m_w_mod=m_w_mod, m_b_mod=m_b_mod, m_norm1_w=m_norm1_w, m_w_in=m_w_in, m_s5_lambda_re_f=m_s5_lambda_re_f, m_s5_lambda_im_f=m_s5_lambda_im_f, m_s5_log_step_f=m_s5_log_step_f, m_s5_lambda_re_b=m_s5_lambda_re_b, m_s5_lambda_im_b=m_s5_lambda_im_b, m_s5_log_step_b=m_s5_log_step_b, m_s5_b_re=m_s5_b_re, m_s5_b_im=m_s5_b_im, m_s5_c_re=m_s5_c_re, m_s5_c_im=m_s5_c_im, m_s5_d=m_s5_d, m_s5_w_glu=m_s5_w_glu, m_s5_b_glu=m_s5_b_glu, m_ret_log_decay_f=m_ret_log_decay_f, m_ret_log_decay_b=m_ret_log_decay_b, m_w_out=m_w_out, m_norm2_w=m_norm2_w, m_w_up=m_w_up, m_conv_w=m_conv_w, m_conv_b=m_conv_b, m_w_down=m_w_down, m_final_norm_w=m_final_norm_w, v_c_ctx=v_c_ctx, v_w_mod=v_w_mod, v_b_mod=v_b_mod, v_norm1_w=v_norm1_w, v_w_in=v_w_in, v_s5_lambda_re_f=v_s5_lambda_re_f, v_s5_lambda_im_f=v_s5_lambda_im_f, v_s5_log_step_f=v_s5_log_step_f, v_s5_lambda_re_b=v_s5_lambda_re_b, v_s5_lambda_im_b=v_s5_lambda_im_b, v_s5_log_step_b=v_s5_log_step_b, v_s5_b_re=v_s5_b_re, v_s5_b_im=v_s5_b_im, v_s5_c_re=v_s5_c_re, v_s5_c_im=v_s5_c_im, v_s5_d=v_s5_d, v_s5_w_glu=v_s5_w_glu, v_s5_b_glu=v_s5_b_glu, v_ret_log_decay_f=v_ret_log_decay_f, v_ret_log_decay_b=v_ret_log_decay_b, v_w_out=v_w_out, v_norm2_w=v_norm2_w, v_w_up=v_w_up, v_conv_w=v_conv_w, v_conv_b=v_conv_b, v_w_down=v_w_down, v_final_norm_w=v_final_norm_w)
    weights = {n: given[n] for n in TWIN_WEIGHTS}
    shared = {n: given[n] for n in SHARED_INPUTS}
    per_example = {n: given[n] for n in ['x', 'c', 'ctx']}
    grad_fn = _jax.value_and_grad(_loss, argnums=(0, 1))

    def one_microbatch(ex, loss_target):
        ex = dict(ex)
        diff = ex.pop(TWIN_DIFF_INPUT)
        return grad_fn(weights, diff, {**shared, **ex}, loss_target)

    if N_MICROBATCH == 1:
        loss, (grad_w, grad_x) = one_microbatch(per_example, given["loss_target"])
    else:
        def body(carry, xs):
            loss_sum, grad_sum = carry
            l_k, (gw_k, gx_k) = one_microbatch(xs[0], xs[1])
            with _jax.named_scope("update"):
                return (loss_sum + l_k, _jax.tree.map(_jnp.add, grad_sum, gw_k)), gx_k

        init = (_jnp.zeros((), _jnp.float32), _jax.tree.map(_jnp.zeros_like, weights))
        (loss, grad_w), grad_x = _jax.lax.scan(body, init, (per_example, given["loss_target"]))
    with _jax.named_scope("update"):
        delta_w, new_m, new_v = {}, {}, {}
        for n in TWIN_WEIGHTS:
            delta_w[n], new_m[n], new_v[n] = _adamw(weights[n], grad_w[n], given["m_" + n], given["v_" + n])
    return (loss, grad_x, *[grad_w[n] for n in TWIN_WEIGHTS], *[delta_w[n] for n in TWIN_WEIGHTS],
            *[new_m[n] for n in TWIN_WEIGHTS], *[new_v[n] for n in TWIN_WEIGHTS])
```

```python
import functools
import math

import jax
import jax.numpy as jnp
from jax import lax
from jax.experimental import pallas as pl
from jax.experimental.pallas import tpu as pltpu

F32 = jnp.float32
BF16 = jnp.bfloat16

D_MODEL = 1024
S5_WIDTH = 512
S5_GROUPS = 32
S5_GROUP = 16
S5_STATE = 64
RET_WIDTH = 512
RET_HEADS = 4
RET_DH = 128
RET_CHUNK = 128
GRID_W = 64
ROPE_THETA = 10000.0
D_FF = 2816
NORM_EPS = 1e-6
IN_COLS = S5_WIDTH + 4 * RET_WIDTH

S5_T = 16
S5_NB = 4
S5_BW = S5_T * 128
S5_SW = 8 * 2 * S5_STATE

ADAM_LR, ADAM_B1, ADAM_B2, ADAM_EPS, ADAM_WD, ADAM_STEP = 0.001, 0.9, 0.999, 1e-08, 0.01, 10

VMEM_LIMIT = 56 * 1024 * 1024
MESH_ID = pl.DeviceIdType.MESH


def _params(sem=None):
    return pltpu.CompilerParams(dimension_semantics=sem, vmem_limit_bytes=VMEM_LIMIT)


def _full(shape):
    n = len(shape)
    return pl.BlockSpec(shape, lambda *_: (0,) * n)


def _dot(a, b):
    return jnp.dot(a, b, preferred_element_type=F32)


def _dot_nt(a, b):
    return lax.dot_general(a, b, (((1,), (1,)), ((), ())), preferred_element_type=F32)


def _dot_tn(a, b):
    return lax.dot_general(a, b, (((0,), (0,)), ((), ())), preferred_element_type=F32)


def _dot_hi(a, b):
    return jnp.dot(a, b, preferred_element_type=F32, precision=lax.Precision.HIGHEST)


def _dot_nt_hi(a, b):
    return lax.dot_general(a, b, (((1,), (1,)), ((), ())), preferred_element_type=F32,
                           precision=lax.Precision.HIGHEST)


def _gelu(x):
    return 0.5 * x * (1.0 + jnp.tanh(0.7978845608028654 * (x + 0.044715 * (x * x * x))))


def _sigmoid(x):
    return 1.0 / (1.0 + jnp.exp(-x))


def _silu(x):
    return x * _sigmoid(x)


def _rms_mod(x, nw, sh, sc):
    r = lax.rsqrt(jnp.mean(x * x, axis=-1, keepdims=True) + NORM_EPS)
    return (x * r * nw) * (1.0 + sc) + sh


def _rms(x, nw):
    r = lax.rsqrt(jnp.mean(x * x, axis=-1, keepdims=True) + NORM_EPS)
    return x * r * nw


def _head_norm_gate(y, g):
    mu = jnp.mean(y, axis=-1, keepdims=True)
    yc = y - mu
    var = jnp.mean(yc * yc, axis=-1, keepdims=True)
    return _silu(g) * (yc * lax.rsqrt(var + NORM_EPS))


def _swap_pairs(t):
    lane = lax.broadcasted_iota(jnp.int32, t.shape, 1)
    return jnp.where(lane % 2 == 0, pltpu.roll(t, RET_DH - 1, 1), pltpu.roll(t, 1, 1))


def _rope(t, cos_t, sin_t):
    return t * cos_t + _swap_pairs(t) * sin_t


def _rope_t(dt, cos_t, sin_t):
    return dt * cos_t + _swap_pairs(dt * sin_t)


def _pick(n, prefs):
    for p in prefs:
        if n % p == 0:
            return p
    return n


def _mm(a, w, *, nt=False, out_dtype=F32, name):
    m, k = a.shape
    n = w.shape[0] if nt else w.shape[1]
    tm = _pick(m, (512, 256, 128))
    tn = _pick(n, (1408, 1024, 1280, 512))

    def body(a_ref, w_ref, o_ref):
        f = _dot_nt if nt else _dot
        o_ref[...] = f(a_ref[...], w_ref[...]).astype(out_dtype)

    w_spec = pl.BlockSpec((tn, k), lambda j, i: (j, 0)) if nt else pl.BlockSpec((k, tn), lambda j, i: (0, j))
    return pl.pallas_call(
        body, name=name, grid=(n // tn, m // tm),
        in_specs=[pl.BlockSpec((tm, k), lambda j, i: (i, 0)), w_spec],
        out_specs=pl.BlockSpec((tm, tn), lambda j, i: (i, j)),
        out_shape=jax.ShapeDtypeStruct((m, n), out_dtype),
        compiler_params=_params(("parallel", "parallel")),
    )(a, w)


def _mm_tn(a, b, *, name):
    m, k = a.shape
    n = b.shape[1]
    tm = _pick(m, (512, 256, 128))
    tn = _pick(n, (1408, 1024, 1280, 512))

    def body(a_ref, b_ref, o_ref):
        @pl.when(pl.program_id(1) == 0)
        def _():
            o_ref[...] = jnp.zeros_like(o_ref)
        o_ref[...] += _dot_tn(a_ref[...], b_ref[...])

    return pl.pallas_call(
        body, name=name, grid=(n // tn, m // tm),
        in_specs=[pl.BlockSpec((tm, k), lambda j, i: (i, 0)), pl.BlockSpec((tm, tn), lambda j, i: (i, j))],
        out_specs=pl.BlockSpec((k, tn), lambda j, i: (0, j)),
        out_shape=jax.ShapeDtypeStruct((k, n), F32),
        compiler_params=_params(("parallel", "arbitrary")),
    )(a, b)


TOK_TILE = 256


def _norm_inproj(x, ctx, n1w, mod4, w_in_b):
    l, lc = x.shape[0], ctx.shape[0]
    tm = TOK_TILE
    nct = lc // tm
    la = l + lc

    def body(x_ref, c_ref, nw_ref, mod_ref, w_ref, p_ref, h_ref):
        is_ctx = pl.program_id(0) < nct
        xt = jnp.where(is_ctx, c_ref[...], x_ref[...])
        sh = jnp.where(is_ctx, mod_ref[0:1, :], mod_ref[2:3, :])
        sc = jnp.where(is_ctx, mod_ref[1:2, :], mod_ref[3:4, :])
        hb = _rms_mod(xt, nw_ref[...], sh, sc).astype(BF16)
        h_ref[...] = hb
        p_ref[...] = _dot(hb, w_ref[...])

    return pl.pallas_call(
        body, name="norm_inproj", grid=(la // tm,),
        in_specs=[pl.BlockSpec((tm, D_MODEL), lambda i: (jnp.maximum(i - nct, 0), 0)),
                  pl.BlockSpec((tm, D_MODEL), lambda i: (jnp.minimum(i, nct - 1), 0)),
                  _full((1, D_MODEL)), _full((4, D_MODEL)), _full((D_MODEL, IN_COLS))],
        out_specs=[pl.BlockSpec((tm, IN_COLS), lambda i: (i, 0)), pl.BlockSpec((tm, D_MODEL), lambda i: (i, 0))],
        out_shape=[jax.ShapeDtypeStruct((la, IN_COLS), F32), jax.ShapeDtypeStruct((la, D_MODEL), BF16)],
        compiler_params=_params(("parallel",)),
    )(x, ctx, n1w, mod4, w_in_b)


def _norm_inproj_bwd(x, ctx, n1w, mod4, dh1, dx1):
    l, lc = x.shape[0], ctx.shape[0]
    tm = TOK_TILE
    nct = lc // tm
    la = l + lc

    def body(x_ref, c_ref, nw_ref, mod_ref, dh_ref, dx1_ref, gx_ref, acc_ref):
        i = pl.program_id(0)
        is_ctx = i < nct

        @pl.when(i == 0)
        def _():
            acc_ref[...] = jnp.zeros_like(acc_ref)

        xt = jnp.where(is_ctx, c_ref[...], x_ref[...])
        sh = jnp.where(is_ctx, mod_ref[0:1, :], mod_ref[2:3, :])
        sc = jnp.where(is_ctx, mod_ref[1:2, :], mod_ref[3:4, :])
        _, vjp = jax.vjp(_rms_mod, xt, nw_ref[...], sh, sc)
        dx, dnw, dsh, dsc = vjp(dh_ref[...])
        gx_ref[...] = dx + dx1_ref[...]
        cf = jnp.where(is_ctx, 1.0, 0.0)
        acc_ref[0:1, :] += dnw
        acc_ref[1:2, :] += cf * dsh
        acc_ref[2:3, :] += cf * dsc
        acc_ref[3:4, :] += (1.0 - cf) * dsh
        acc_ref[4:5, :] += (1.0 - cf) * dsc

    return pl.pallas_call(
        body, name="norm_inproj_bwd", grid=(la // tm,),
        in_specs=[pl.BlockSpec((tm, D_MODEL), lambda i: (jnp.maximum(i - nct, 0), 0)),
                  pl.BlockSpec((tm, D_MODEL), lambda i: (jnp.minimum(i, nct - 1), 0)),
                  _full((1, D_MODEL)), _full((4, D_MODEL)),
                  pl.BlockSpec((tm, D_MODEL), lambda i: (i, 0)),
                  pl.BlockSpec((tm, D_MODEL), lambda i: (jnp.maximum(i - nct, 0), 0))],
        out_specs=[pl.BlockSpec((tm, D_MODEL), lambda i: (jnp.maximum(i - nct, 0), 0)), _full((8, D_MODEL))],
        out_shape=[jax.ShapeDtypeStruct((l, D_MODEL), F32), jax.ShapeDtypeStruct((8, D_MODEL), F32)],
        compiler_params=_params(("arbitrary",)),
    )(x, ctx, n1w, mod4, dh1, dx1)


def _iota2(shape, dim):
    return lax.broadcasted_iota(jnp.int32, shape, dim)


def _group_mask(rows, cols, row_div, col_div):
    return jnp.where(_iota2((rows, cols), 0) // row_div == _iota2((rows, cols), 1) // col_div, 1.0, 0.0).astype(F32)


def _s5_gen_dir(lre, lim, lst, b_re, b_im, c_re, c_im):
    step = jnp.exp(lst)
    mag = jnp.exp(lre * step)
    ar = mag * jnp.cos(lim * step)
    ai = mag * jnp.sin(lim * step)
    den = lre * lre + lim * lim
    xr = ar - 1.0
    cr = (xr * lre + ai * lim) / den
    ci = (ai * lre - xr * lim) / den
    rexp = _group_mask(128, 8, S5_GROUP, 1)
    are, aie = _dot_hi(rexp, ar), _dot_hi(rexp, ai)
    cre, cie = _dot_hi(rexp, cr), _dot_hi(rexp, ci)
    bbr = cre * b_re - cie * b_im
    bbi = cre * b_im + cie * b_re
    gmask = _group_mask(128, 128, S5_GROUP, S5_GROUP)
    pr, pi = jnp.ones_like(are), jnp.zeros_like(are)
    xs, ys = [], []
    for t in range(S5_T + 1):
        if t < S5_T:
            xs.append(jnp.concatenate([bbr * pr - bbi * pi, bbr * pi + bbi * pr], axis=1))
        ys.append(jnp.concatenate([c_re * pr - c_im * pi, -(c_re * pi + c_im * pr)], axis=1))
        pr, pi = pr * are - pi * aie, pr * aie + pi * are
    gs = [_dot_nt_hi(x_t, ys[0]) * gmask for x_t in xs]
    r16, i16 = ar, ai
    for _ in range(4):
        r16, i16 = r16 * r16 - i16 * i16, 2.0 * r16 * i16
    return xs, ys, gs, jnp.concatenate([r16, i16], axis=1)


def _s5_expand(z):
    return jnp.concatenate([z] * 8, axis=1) * _group_mask(128, S5_SW, S5_GROUP, 128)


def _s5_contract(z):
    zm = z * _group_mask(128, S5_SW, S5_GROUP, 128)
    acc = zm[:, 0:128]
    for k in range(1, 8):
        acc = acc + zm[:, 128 * k:128 * (k + 1)]
    return acc


def _s5_param_specs():
    blk3 = lambda r, c: pl.BlockSpec((1, 1, r, c), lambda b, j: (0, b, 0, 0))
    dir3 = lambda r, c: pl.BlockSpec((2, 1, r, c), lambda b, j: (0, b, 0, 0))
    return [dir3(8, S5_STATE), dir3(8, S5_STATE), dir3(8, 1), blk3(128, S5_STATE), blk3(128, S5_STATE),
            blk3(128, S5_STATE), blk3(128, S5_STATE), blk3(1, 128)]


def _s5_gen(lre, lim, lst, b_re, b_im, c_re, c_im, dvec):
    def body(lre_ref, lim_ref, lst_ref, bre_ref, bim_ref, cre_ref, cim_ref, d_ref,
             kb_ref, wst_ref, wout_ref, a16_ref, x_scr, y_scr, g_scr):
        j = pl.program_id(1)

        @pl.when(j == 0)
        def _():
            eye = _group_mask(128, 128, 1, 1)
            g0 = eye * d_ref[0, 0]
            for dr in range(2):
                xs, ys, gs, a16 = _s5_gen_dir(lre_ref[dr, 0], lim_ref[dr, 0], lst_ref[dr, 0], bre_ref[0, 0],
                                              bim_ref[0, 0], cre_ref[0, 0], cim_ref[0, 0])
                a16_ref[0, dr] = a16
                for t in range(S5_T):
                    x_scr[dr, t] = xs[t]
                for t in range(S5_T + 1):
                    y_scr[dr, t] = ys[t]
                g0 = g0 + gs[0]
                for t in range(1, S5_T):
                    g_scr[(S5_T - 1) + t if dr == 0 else (S5_T - 1) - t] = gs[t]
            g_scr[S5_T - 1] = g0

        for i in range(S5_T):
            kb_ref[0, :, 128 * i:128 * (i + 1)] = g_scr[i - j + (S5_T - 1)].astype(BF16)
        wst_ref[0, 0] = _s5_expand(x_scr[0, S5_T - 1 - j]).astype(BF16)
        wst_ref[0, 1] = _s5_expand(x_scr[1, j]).astype(BF16)
        wout_ref[0, 0] = _s5_expand(y_scr[0, j + 1]).astype(BF16)
        wout_ref[0, 1] = _s5_expand(y_scr[1, S5_T - j]).astype(BF16)

    return pl.pallas_call(
        body, name="s5_gen", grid=(S5_NB, S5_T),
        in_specs=_s5_param_specs(),
        out_specs=[pl.BlockSpec((1, 128, S5_BW), lambda b, j: (b, j, 0)),
                   pl.BlockSpec((1, 2, 128, S5_SW), lambda b, j: (b, 0, j, 0)),
                   pl.BlockSpec((1, 2, 128, S5_SW), lambda b, j: (b, 0, j, 0)),
                   pl.BlockSpec((1, 2, 8, 128), lambda b, j: (b, 0, 0, 0))],
        out_shape=[jax.ShapeDtypeStruct((S5_NB, S5_BW, S5_BW), BF16),
                   jax.ShapeDtypeStruct((S5_NB, 2, S5_BW, S5_SW), BF16),
                   jax.ShapeDtypeStruct((S5_NB, 2, S5_BW, S5_SW), BF16),
                   jax.ShapeDtypeStruct((S5_NB, 2, 8, 128), F32)],
        scratch_shapes=[pltpu.VMEM((2, S5_T, 128, 128), F32), pltpu.VMEM((2, S5_T + 1, 128, 128), F32),
                        pltpu.VMEM((2 * S5_T - 1, 128, 128), F32)],
        compiler_params=_params(("parallel", "arbitrary")),
    )(lre, lim, lst, b_re, b_im, c_re, c_im, dvec)


def _s5_gen_bwd(lre, lim, lst, b_re, b_im, c_re, c_im, dvec, dkb, dwst, dwout, da16):
    def body(lre_ref, lim_ref, lst_ref, bre_ref, bim_ref, cre_ref, cim_ref, d_ref,
             dkb_ref, dwst_ref, dwout_ref, da16_ref,
             glre_ref, glim_ref, glst_ref, gbre_ref, gbim_ref, gcre_ref, gcim_ref, gd_ref,
             dx_scr, dy_scr, dg_scr):
        j = pl.program_id(1)

        @pl.when(j == 0)
        def _():
            dg_scr[...] = jnp.zeros_like(dg_scr)
            dy_scr[0, 0] = jnp.zeros((128, 128), F32)
            dy_scr[1, 0] = jnp.zeros((128, 128), F32)

        for i in range(S5_T):
            dg_scr[i - j + (S5_T - 1)] += dkb_ref[0, :, 128 * i:128 * (i + 1)]
        dx_scr[0, S5_T - 1 - j] = _s5_contract(dwst_ref[0, 0])
        dx_scr[1, j] = _s5_contract(dwst_ref[0, 1])
        dy_scr[0, j + 1] = _s5_contract(dwout_ref[0, 0])
        dy_scr[1, S5_T - j] = _s5_contract(dwout_ref[0, 1])

        @pl.when(j == S5_T - 1)
        def _():
            eye = _group_mask(128, 128, 1, 1)
            gd_ref[0, 0] = jnp.sum(dg_scr[S5_T - 1] * eye, axis=0, keepdims=True)
            gb = [None, None, None, None]
            for dr in range(2):
                args = (lre_ref[dr, 0], lim_ref[dr, 0], lst_ref[dr, 0], bre_ref[0, 0], bim_ref[0, 0],
                        cre_ref[0, 0], cim_ref[0, 0])
                _, vjp = jax.vjp(_s5_gen_dir, *args)
                dxs = [dx_scr[dr, t] for t in range(S5_T)]
                dys = [dy_scr[dr, t] for t in range(S5_T + 1)]
                dgs = [dg_scr[(S5_T - 1) + t if dr == 0 else (S5_T - 1) - t] for t in range(S5_T)]
                g = vjp((dxs, dys, dgs, da16_ref[0, dr]))
                glre_ref[dr, 0] = g[0]
                glim_ref[dr, 0] = g[1]
                glst_ref[dr, 0] = g[2]
                for q in range(4):
                    gb[q] = g[3 + q] if gb[q] is None else gb[q] + g[3 + q]
            gbre_ref[0, 0] = gb[0]
            gbim_ref[0, 0] = gb[1]
            gcre_ref[0, 0] = gb[2]
            gcim_ref[0, 0] = gb[3]

    shp = lambda a: jax.ShapeDtypeStruct(a.shape, F32)
    return pl.pallas_call(
        body, name="s5_gen_bwd", grid=(S5_NB, S5_T),
        in_specs=_s5_param_specs() + [
            pl.BlockSpec((1, 128, S5_BW), lambda b, j: (b, j, 0)),
            pl.BlockSpec((1, 2, 128, S5_SW), lambda b, j: (b, 0, j, 0)),
            pl.BlockSpec((1, 2, 128, S5_SW), lambda b, j: (b, 0, j, 0)),
            pl.BlockSpec((1, 2, 8, 128), lambda b, j: (b, 0, 0, 0))],
        out_specs=_s5_param_specs(),
        out_shape=[shp(lre), shp(lim), shp(lst), shp(b_re), shp(b_im), shp(c_re), shp(c_im), shp(dvec)],
        scratch_shapes=[pltpu.VMEM((2, S5_T, 128, 128), F32), pltpu.VMEM((2, S5_T + 1, 128, 128), F32),
                        pltpu.VMEM((2 * S5_T - 1, 128, 128), F32)],
        compiler_params=_params(("parallel", "arbitrary")),
    )(lre, lim, lst, b_re, b_im, c_re, c_im, dvec, dkb, dwst, dwout, da16)


def _s5_ucat(u_ref, lo=0, hi=S5_T):
    return jnp.concatenate([u_ref[:, j, :] for j in range(lo, hi)], axis=1).astype(BF16)


def _s5_state(p3, wst):
    cn = p3.shape[0]

    def body(u_ref, w_ref, o_ref):
        u = _s5_ucat(u_ref)
        o_ref[0] = _dot(u, w_ref[0, 0])
        o_ref[1] = _dot(u, w_ref[0, 1])

    return pl.pallas_call(
        body, name="s5_state", grid=(S5_NB,),
        in_specs=[pl.BlockSpec((cn, S5_T, 128), lambda b: (0, 0, b)),
                  pl.BlockSpec((1, 2, S5_BW, S5_SW), lambda b: (b, 0, 0, 0))],
        out_specs=pl.BlockSpec((2, cn, S5_SW), lambda b: (0, 0, b)),
        out_shape=jax.ShapeDtypeStruct((2, cn, S5_NB * S5_SW), F32),
        compiler_params=_params(("parallel",)),
    )(p3, wst)


def _s5_a_forms(a):
    ra = pltpu.roll(a, S5_STATE, 1)
    low = _iota2(a.shape, 1) < S5_STATE
    return jnp.where(low, a, ra), jnp.where(low, -ra, a)


def _s5_scan(sloc, a16, ncc):
    cn = sloc.shape[1]

    def body(s_ref, a_ref, h_ref):
        for dr in range(2):
            arr, aii = _s5_a_forms(a_ref[dr])

            def step(s, h, dr=dr, arr=arr, aii=aii):
                c = s if dr == 0 else jnp.where(s < ncc, ncc - 1 - s, cn - 1 - (s - ncc))
                h_ref[dr, c] = h
                return h * arr + pltpu.roll(h, S5_STATE, 1) * aii + s_ref[dr, c]

            lax.fori_loop(0, cn, step, jnp.zeros((S5_GROUPS, 128), F32))

    return pl.pallas_call(
        body, name="s5_scan",
        out_shape=jax.ShapeDtypeStruct(sloc.shape, F32),
        compiler_params=_params(),
    )(sloc, a16)


def _s5_scan_bwd(e, hs, a16, ncc):
    cn = e.shape[1]

    def body(e_ref, h_ref, a_ref, ds_ref, da_ref):
        for dr in range(2):
            arr, aii = _s5_a_forms(a_ref[dr])
            low = _iota2((S5_GROUPS, 128), 1) < S5_STATE

            def step(s, carry, dr=dr, arr=arr, aii=aii, low=low):
                g, da = carry
                r = cn - 1 - s
                c = r if dr == 0 else jnp.where(r < ncc, ncc - 1 - r, cn - 1 - (r - ncc))
                ds_ref[dr, c] = g
                h = h_ref[dr, c]
                gh = g * h
                grh = g * pltpu.roll(h, S5_STATE, 1)
                da = da + jnp.where(low, gh + pltpu.roll(gh, S5_STATE, 1), grh - pltpu.roll(grh, S5_STATE, 1))
                g = e_ref[dr, c] + g * arr - pltpu.roll(g, S5_STATE, 1) * aii
                return g, da

            zero = jnp.zeros((S5_GROUPS, 128), F32)
            _, da = lax.fori_loop(0, cn, step, (zero, zero))
            da_ref[dr] = da

    return pl.pallas_call(
        body, name="s5_scan_bwd",
        out_shape=[jax.ShapeDtypeStruct(e.shape, F32), jax.ShapeDtypeStruct((2, S5_GROUPS, 128), F32)],
        compiler_params=_params(),
    )(e, hs, a16)


def _s5_out(p3, kb, h2, wout):
    cn = p3.shape[0]
    half = S5_T // 2

    def body(u_ref, k_ref, h_ref, w_ref, y_ref):
        u = _s5_ucat(u_ref)
        y = _dot(u, k_ref[0])
        y = y + _dot_nt(h_ref[0].astype(BF16), w_ref[0, 0])
        y = y + _dot_nt(h_ref[1].astype(BF16), w_ref[0, 1])
        for i in range(half):
            y_ref[:, i, :] = y[:, 128 * i:128 * (i + 1)]

    return pl.pallas_call(
        body, name="s5_out", grid=(S5_NB, 2),
        in_specs=[pl.BlockSpec((cn, S5_T, 128), lambda b, q: (0, 0, b)),
                  pl.BlockSpec((1, S5_BW, S5_BW // 2), lambda b, q: (b, 0, q)),
                  pl.BlockSpec((2, cn, S5_SW), lambda b, q: (0, 0, b)),
                  pl.BlockSpec((1, 2, S5_BW // 2, S5_SW), lambda b, q: (b, 0, q, 0))],
        out_specs=pl.BlockSpec((cn, half, 128), lambda b, q: (0, q, b)),
        out_shape=jax.ShapeDtypeStruct((cn, S5_T, S5_WIDTH), F32),
        compiler_params=_params(("parallel", "parallel")),
    )(p3, kb, h2, wout)


def _s5_bwd_h(dy3, wout):
    cn = dy3.shape[0]

    def body(d_ref, w_ref, e_ref):
        d = _s5_ucat(d_ref)
        e_ref[0] = _dot(d, w_ref[0, 0])
        e_ref[1] = _dot(d, w_ref[0, 1])

    return pl.pallas_call(
        body, name="s5_bwd_h", grid=(S5_NB,),
        in_specs=[pl.BlockSpec((cn, S5_T, 128), lambda b: (0, 0, b)),
                  pl.BlockSpec((1, 2, S5_BW, S5_SW), lambda b: (b, 0, 0, 0))],
        out_specs=pl.BlockSpec((2, cn, S5_SW), lambda b: (0, 0, b)),
        out_shape=jax.ShapeDtypeStruct((2, cn, S5_NB * S5_SW), F32),
        compiler_params=_params(("parallel",)),
    )(dy3, wout)


def _s5_bwd_u(dy3, kb, ds2, wst):
    cn = dy3.shape[0]
    half = S5_T // 2

    def body(d_ref, k_ref, s_ref, w_ref, o_ref):
        d = _s5_ucat(d_ref)
        du = _dot_nt(d, k_ref[0])
        du = du + _dot_nt(s_ref[0].astype(BF16), w_ref[0, 0])
        du = du + _dot_nt(s_ref[1].astype(BF16), w_ref[0, 1])
        for j in range(half):
            o_ref[:, j, :] = du[:, 128 * j:128 * (j + 1)]

    return pl.pallas_call(
        body, name="s5_bwd_u", grid=(S5_NB, 2),
        in_specs=[pl.BlockSpec((cn, S5_T, 128), lambda b, q: (0, 0, b)),
                  pl.BlockSpec((1, S5_BW // 2, S5_BW), lambda b, q: (b, q, 0)),
                  pl.BlockSpec((2, cn, S5_SW), lambda b, q: (0, 0, b)),
                  pl.BlockSpec((1, 2, S5_BW // 2, S5_SW), lambda b, q: (b, 0, q, 0))],
        out_specs=pl.BlockSpec((cn, half, 128), lambda b, q: (0, q, b)),
        out_shape=jax.ShapeDtypeStruct((cn, S5_T, S5_WIDTH), F32),
        compiler_params=_params(("parallel", "parallel")),
    )(dy3, kb, ds2, wst)


def _s5_bwd_kb(p3, dy3):
    cn = p3.shape[0]
    half = S5_T // 2

    def body(u_ref, d_ref, o_ref):
        o_ref[0] = _dot_tn(_s5_ucat(u_ref), _s5_ucat(d_ref, 0, half))

    return pl.pallas_call(
        body, name="s5_bwd_kb", grid=(S5_NB, 2),
        in_specs=[pl.BlockSpec((cn, S5_T, 128), lambda b, q: (0, 0, b)),
                  pl.BlockSpec((cn, half, 128), lambda b, q: (0, q, b))],
        out_specs=pl.BlockSpec((1, S5_BW, S5_BW // 2), lambda b, q: (b, 0, q)),
        out_shape=jax.ShapeDtypeStruct((S5_NB, S5_BW, S5_BW), F32),
        compiler_params=_params(("parallel", "parallel")),
    )(p3, dy3)


def _s5_bwd_w(p3, dy3, ds2, h2):
    cn = p3.shape[0]

    def body(u_ref, d_ref, s_ref, h_ref, ws_ref, wo_ref):
        ws_ref[0, 0] = _dot_tn(_s5_ucat(u_ref), s_ref[0].astype(BF16))
        wo_ref[0, 0] = _dot_tn(_s5_ucat(d_ref), h_ref[0].astype(BF16))

    hw = S5_SW // 2
    w_spec = pl.BlockSpec((1, 1, S5_BW, hw), lambda b, q, r: (b, q, 0, r))
    s_spec = pl.BlockSpec((1, cn, hw), lambda b, q, r: (q, 0, 2 * b + r))
    u_spec = pl.BlockSpec((cn, S5_T, 128), lambda b, q, r: (0, 0, b))
    w_shape = jax.ShapeDtypeStruct((S5_NB, 2, S5_BW, S5_SW), F32)
    return pl.pallas_call(
        body, name="s5_bwd_w", grid=(S5_NB, 2, 2),
        in_specs=[u_spec, u_spec, s_spec, s_spec],
        out_specs=[w_spec, w_spec],
        out_shape=[w_shape, w_shape],
        compiler_params=_params(("parallel", "parallel", "parallel")),
    )(p3, dy3, ds2, h2)


def _s5_glu(y_all, w_glu_b, b_glu, nct):
    la = y_all.shape[0]
    tm = TOK_TILE
    l = la - nct * tm

    def body(y_ref, w_ref, b_ref, o_ref):
        yg = _gelu(y_ref[...])
        z = _dot(yg.astype(BF16), w_ref[...]) + b_ref[...]
        o_ref[...] = (yg * _sigmoid(z)).astype(BF16)

    return pl.pallas_call(
        body, name="s5_glu", grid=(l // tm,),
        in_specs=[pl.BlockSpec((tm, S5_WIDTH), lambda i: (i + nct, 0)),
                  _full((S5_WIDTH, S5_WIDTH)), _full((1, S5_WIDTH))],
        out_specs=pl.BlockSpec((tm, S5_WIDTH), lambda i: (i, 0)),
        out_shape=jax.ShapeDtypeStruct((l, S5_WIDTH), BF16),
        compiler_params=_params(("parallel",)),
    )(y_all, w_glu_b, b_glu)


def _s5_glu_bwd(y_all, dmix, w_glu_b, b_glu, nct):
    la = y_all.shape[0]
    tm = TOK_TILE

    def body(y_ref, d_ref, w_ref, b_ref, dy_ref, gw_ref, gb_ref):
        i = pl.program_id(0)

        @pl.when(i == 0)
        def _():
            gw_ref[...] = jnp.zeros_like(gw_ref)
            gb_ref[...] = jnp.zeros_like(gb_ref)

        @pl.when(i < nct)
        def _():
            dy_ref[...] = jnp.zeros_like(dy_ref)

        @pl.when(i >= nct)
        def _():
            y = y_ref[...]
            yg, gelu_vjp = jax.vjp(_gelu, y)
            ygb = yg.astype(BF16)
            sg = _sigmoid(_dot(ygb, w_ref[...]) + b_ref[...])
            ds = d_ref[...]
            dz = ds * yg * sg * (1.0 - sg)
            dzb = dz.astype(BF16)
            dyg = ds * sg + _dot_nt(dzb, w_ref[...])
            dy_ref[...] = gelu_vjp(dyg)[0]
            gw_ref[...] += _dot_tn(ygb, dzb)
            gb_ref[...] += jnp.sum(dz, axis=0, keepdims=True)

    return pl.pallas_call(
        body, name="s5_glu_bwd", grid=(la // tm,),
        in_specs=[pl.BlockSpec((tm, S5_WIDTH), lambda i: (i, 0)),
                  pl.BlockSpec((tm, S5_WIDTH), lambda i: (jnp.maximum(i - nct, 0), 0)),
                  _full((S5_WIDTH, S5_WIDTH)), _full((1, S5_WIDTH))],
        out_specs=[pl.BlockSpec((tm, S5_WIDTH), lambda i: (i, 0)), _full((S5_WIDTH, S5_WIDTH)),
                   _full((1, S5_WIDTH))],
        out_shape=[jax.ShapeDtypeStruct((la, S5_WIDTH), F32), jax.ShapeDtypeStruct((S5_WIDTH, S5_WIDTH), F32),
                   jax.ShapeDtypeStruct((1, S5_WIDTH), F32)],
        compiler_params=_params(("arbitrary",)),
    )(y_all, dmix, w_glu_b, b_glu)


K_SCALE = RET_DH ** -0.5
Q_COL, K_COL, V_COL, G_COL = 4, 8, 12, 16


def _ret_chunk_of(step, ncc, nch, rev):
    if not rev:
        return step
    return jnp.where(step < ncc, ncc - 1 - step, nch - 1 - (step - ncc))


def _ret_decay(ld, rev):
    c = _iota2((RET_CHUNK, RET_CHUNK), 0).astype(F32)
    m = _iota2((RET_CHUNK, RET_CHUNK), 1).astype(F32)
    diff = (m - c) if rev else (c - m)
    keep = (diff > 0) if rev else (diff >= 0)
    expo = jnp.maximum(diff, 0.0)
    dm = jnp.where(keep, jnp.exp(ld * expo), 0.0)
    pos = _iota2((RET_CHUNK, 1), 0).astype(F32)
    xi_e = (RET_CHUNK - pos) if rev else (pos + 1.0)
    zeta_e = pos if rev else (RET_CHUNK - 1.0 - pos)
    return dm, expo, jnp.exp(ld * xi_e), xi_e, jnp.exp(ld * zeta_e), zeta_e


def _ret_scan(p_all, cos_t, sin_t, ld, ncc, rev, name):
    la = p_all.shape[0]
    nch = la // RET_CHUNK

    def body(ld_ref, q_ref, k_ref, v_ref, cos_ref, sin_ref, o_ref, ss_ref, s_scr):
        h = pl.program_id(0)

        @pl.when(pl.program_id(1) == 0)
        def _():
            s_scr[...] = jnp.zeros_like(s_scr)

        ldh = ld_ref[h]
        dm, _, xi, _, zeta, _ = _ret_decay(ldh, rev)
        cs, sn = cos_ref[...], sin_ref[...]
        q = _rope(q_ref[...], cs, sn)
        k = _rope(k_ref[...] * K_SCALE, cs, sn)
        vb = v_ref[...].astype(BF16)
        s = s_scr[...]
        ss_ref[0, 0] = s
        sc = (_dot_nt(q.astype(BF16), k.astype(BF16)) * dm).astype(BF16)
        o_ref[...] = _dot(sc, vb) + _dot((q * xi).astype(BF16), s.astype(BF16))
        s_scr[...] = jnp.exp(ldh * RET_CHUNK) * s + _dot_tn((k * zeta).astype(BF16), vb)

    def col(cb):
        return pl.BlockSpec((RET_CHUNK, RET_DH), lambda h, n: (_ret_chunk_of(n, ncc, nch, rev), cb + h))

    tab = pl.BlockSpec((RET_CHUNK, RET_DH), lambda h, n: (_ret_chunk_of(n, ncc, nch, rev), 0))
    return pl.pallas_call(
        body, name=name, grid=(RET_HEADS, nch),
        in_specs=[pl.BlockSpec(memory_space=pltpu.SMEM), col(Q_COL), col(K_COL), col(V_COL), tab, tab],
        out_specs=[pl.BlockSpec((RET_CHUNK, RET_DH), lambda h, n: (_ret_chunk_of(n, ncc, nch, rev), h)),
                   pl.BlockSpec((1, 1, RET_DH, RET_DH), lambda h, n: (h, n, 0, 0))],
        out_shape=[jax.ShapeDtypeStruct((la, RET_WIDTH), F32),
                   jax.ShapeDtypeStruct((RET_HEADS, nch, RET_DH, RET_DH), F32)],
        scratch_shapes=[pltpu.VMEM((RET_DH, RET_DH), F32)],
        compiler_params=_params(("parallel", "arbitrary")),
    )(ld, p_all, p_all, p_all, cos_t, sin_t)


def _ret_scan_bwd(p_all, cos_t, sin_t, ld, ss, dy_all, ncc, rev, name):
    la = p_all.shape[0]
    nch = la // RET_CHUNK

    def body(ld_ref, q_ref, k_ref, v_ref, cos_ref, sin_ref, ss_ref, do_ref,
             dq_ref, dk_ref, dv_ref, dld_ref, ds_scr):
        h = pl.program_id(0)

        @pl.when(pl.program_id(1) == 0)
        def _():
            ds_scr[...] = jnp.zeros_like(ds_scr)
            dld_ref[...] = jnp.zeros_like(dld_ref)

        ldh = ld_ref[h]
        dm, expo, xi, xi_e, zeta, zeta_e = _ret_decay(ldh, rev)
        gc = jnp.exp(ldh * RET_CHUNK)
        cs, sn = cos_ref[...], sin_ref[...]
        q = _rope(q_ref[...], cs, sn)
        k = _rope(k_ref[...] * K_SCALE, cs, sn)
        qb, kb, vb = q.astype(BF16), k.astype(BF16), v_ref[...].astype(BF16)
        s = ss_ref[0, 0]
        sb = s.astype(BF16)
        ds_in = ds_scr[...]
        dsb = ds_in.astype(BF16)
        dob = do_ref[...].astype(BF16)
        qk = _dot_nt(qb, kb)
        dsv = _dot_nt(dob, vb)
        dsc = (dsv * dm).astype(BF16)
        scb = (qk * dm).astype(BF16)
        dos = _dot_nt(dob, sb)
        vds = _dot_nt(vb, dsb)
        kz = (k * zeta).astype(BF16)
        dq_ref[...] = _dot(dsc, kb) + dos * xi
        dk_ref[...] = _dot_tn(dsc, qb) + vds * zeta
        dv_ref[...] = _dot_tn(scb, dob) + _dot(kz, dsb)
        ds_scr[...] = _dot_tn((q * xi).astype(BF16), dob) + gc * ds_in
        dld = (jnp.sum(dsv * qk * dm * expo) + jnp.sum(q * dos * (xi * xi_e)) + jnp.sum(k * vds * (zeta * zeta_e))
               + RET_CHUNK * gc * jnp.sum(s * ds_in))
        dld_ref[...] += dld

    def chunk(n):
        return _ret_chunk_of(nch - 1 - n, ncc, nch, rev)

    def col(cb):
        return pl.BlockSpec((RET_CHUNK, RET_DH), lambda h, n: (chunk(n), cb + h))

    tab = pl.BlockSpec((RET_CHUNK, RET_DH), lambda h, n: (chunk(n), 0))
    out = pl.BlockSpec((RET_CHUNK, RET_DH), lambda h, n: (chunk(n), h))
    shp = jax.ShapeDtypeStruct((la, RET_WIDTH), F32)
    return pl.pallas_call(
        body, name=name, grid=(RET_HEADS, nch),
        in_specs=[pl.BlockSpec(memory_space=pltpu.SMEM), col(Q_COL), col(K_COL), col(V_COL), tab, tab,
                  pl.BlockSpec((1, 1, RET_DH, RET_DH), lambda h, n: (h, nch - 1 - n, 0, 0)), out],
        out_specs=[out, out, out, pl.BlockSpec((1, 8, 128), lambda h, n: (h, 0, 0))],
        out_shape=[shp, shp, shp, jax.ShapeDtypeStruct((RET_HEADS, 8, 128), F32)],
        scratch_shapes=[pltpu.VMEM((RET_DH, RET_DH), F32)],
        compiler_params=_params(("parallel", "arbitrary")),
    )(ld, p_all, p_all, p_all, cos_t, sin_t, ss, dy_all)


def _ret_gate(of, ob, p_all, nct):
    la = of.shape[0]
    tm = TOK_TILE
    l = la - nct * tm

    def body(of_ref, ob_ref, g_ref, r_ref, y_ref):
        y = of_ref[...] + ob_ref[...]
        y_ref[...] = y
        for h in range(RET_HEADS):
            sl = slice(RET_DH * h, RET_DH * (h + 1))
            r_ref[:, sl] = _head_norm_gate(y[:, sl], g_ref[:, sl]).astype(BF16)

    row = pl.BlockSpec((tm, RET_WIDTH), lambda i: (i + nct, 0))
    out = pl.BlockSpec((tm, RET_WIDTH), lambda i: (i, 0))
    return pl.pallas_call(
        body, name="ret_gate", grid=(l // tm,),
        in_specs=[row, row, pl.BlockSpec((tm, RET_WIDTH), lambda i: (i + nct, G_COL // 4))],
        out_specs=[out, out],
        out_shape=[jax.ShapeDtypeStruct((l, RET_WIDTH), BF16), jax.ShapeDtypeStruct((l, RET_WIDTH), F32)],
        compiler_params=_params(("parallel",)),
    )(of, ob, p_all)


def _ret_gate_bwd(y_ret, p_all, dmix, nct):
    la = p_all.shape[0]
    tm = TOK_TILE

    def body(y_ref, g_ref, d_ref, dy_ref, dg_ref):
        i = pl.program_id(0)

        @pl.when(i < nct)
        def _():
            dy_ref[...] = jnp.zeros_like(dy_ref)
            dg_ref[...] = jnp.zeros_like(dg_ref)

        @pl.when(i >= nct)
        def _():
            for h in range(RET_HEADS):
                sl = slice(RET_DH * h, RET_DH * (h + 1))
                _, vjp = jax.vjp(_head_norm_gate, y_ref[:, sl], g_ref[:, sl])
                dy, dg = vjp(d_ref[:, sl])
                dy_ref[:, sl] = dy
                dg_ref[:, sl] = dg

    xrow = lambda cb: pl.BlockSpec((tm, RET_WIDTH), lambda i: (jnp.maximum(i - nct, 0), cb))
    out = pl.BlockSpec((tm, RET_WIDTH), lambda i: (i, 0))
    shp = jax.ShapeDtypeStruct((la, RET_WIDTH), F32)
    return pl.pallas_call(
        body, name="ret_gate_bwd", grid=(la // tm,),
        in_specs=[xrow(0), pl.BlockSpec((tm, RET_WIDTH), lambda i: (i, G_COL // 4)), xrow(1)],
        out_specs=[out, out], out_shape=[shp, shp],
        compiler_params=_params(("parallel",)),
    )(y_ret, p_all, dmix)


def _ret_qkv_grad(dqf, dkf, dvf, dqb, dkb, dvb, du, dg, cos_t, sin_t):
    la = dqf.shape[0]
    tm = TOK_TILE

    def body(dqf_ref, dkf_ref, dvf_ref, dqb_ref, dkb_ref, dvb_ref, du_ref, dg_ref, cos_ref, sin_ref, dp_ref):
        cs, sn = cos_ref[...], sin_ref[...]
        dp_ref[:, 0:S5_WIDTH] = du_ref[...].astype(BF16)
        for h in range(RET_HEADS):
            sl = slice(RET_DH * h, RET_DH * (h + 1))
            dq = _rope_t(dqf_ref[:, sl] + dqb_ref[:, sl], cs, sn)
            dk = _rope_t(dkf_ref[:, sl] + dkb_ref[:, sl], cs, sn) * K_SCALE
            dp_ref[:, 128 * (Q_COL + h):128 * (Q_COL + h + 1)] = dq.astype(BF16)
            dp_ref[:, 128 * (K_COL + h):128 * (K_COL + h + 1)] = dk.astype(BF16)
        dp_ref[:, 128 * V_COL:128 * G_COL] = (dvf_ref[...] + dvb_ref[...]).astype(BF16)
        dp_ref[:, 128 * G_COL:IN_COLS] = dg_ref[...].astype(BF16)

    row = pl.BlockSpec((tm, RET_WIDTH), lambda i: (i, 0))
    tab = pl.BlockSpec((tm, RET_DH), lambda i: (i, 0))
    return pl.pallas_call(
        body, name="ret_qkv_grad", grid=(la // tm,),
        in_specs=[row] * 8 + [tab, tab],
        out_specs=pl.BlockSpec((tm, IN_COLS), lambda i: (i, 0)),
        out_shape=jax.ShapeDtypeStruct((la, IN_COLS), BF16),
        compiler_params=_params(("parallel",)),
    )(dqf, dkf, dvf, dqb, dkb, dvb, du, dg, cos_t, sin_t)


def _outproj_up(x, s5x, retx, w_out_b, mod3, n2w, w_up_b):
    l = x.shape[0]
    tm = TOK_TILE

    def body(x_ref, s_ref, r_ref, wo_ref, mod_ref, nw_ref, wu_ref, x1_ref, mix_ref, h2_ref, up_ref):
        mix = _dot(s_ref[...], wo_ref[0:S5_WIDTH, :]) + _dot(r_ref[...], wo_ref[S5_WIDTH:D_MODEL, :])
        mix_ref[...] = mix
        x1 = x_ref[...] + mod_ref[0:1, :] * mix
        x1_ref[...] = x1
        h2 = _rms_mod(x1, nw_ref[...], mod_ref[1:2, :], mod_ref[2:3, :]).astype(BF16)
        h2_ref[...] = h2
        up_ref[...] = _dot(h2, wu_ref[...])

    row = lambda w: pl.BlockSpec((tm, w), lambda i: (i, 0))
    return pl.pallas_call(
        body, name="outproj_up", grid=(l // tm,),
        in_specs=[row(D_MODEL), row(S5_WIDTH), row(RET_WIDTH), _full((D_MODEL, D_MODEL)), _full((3, D_MODEL)),
                  _full((1, D_MODEL)), _full((D_MODEL, 2 * D_FF))],
        out_specs=[row(D_MODEL), row(D_MODEL), row(D_MODEL), row(2 * D_FF)],
        out_shape=[jax.ShapeDtypeStruct((l, D_MODEL), F32), jax.ShapeDtypeStruct((l, D_MODEL), F32),
                   jax.ShapeDtypeStruct((l, D_MODEL), BF16), jax.ShapeDtypeStruct((l, 2 * D_FF), F32)],
        compiler_params=_params(("parallel",)),
    )(x, s5x, retx, w_out_b, mod3, n2w, w_up_b)


HALO = 8


def _conv_taps(g, prev_row, next_row):
    t = g.shape[0]
    r = _iota2(g.shape, 0)
    gprev = jnp.where(r == 0, prev_row, pltpu.roll(g, 1, 0))
    gnext = jnp.where(r == t - 1, next_row, pltpu.roll(g, t - 1, 0))
    return gprev, gnext


def _ffn_loss(up, x1, conv_w, conv_b, w_down_b, gate, fnw, tgt):
    l = x1.shape[0]
    tm = TOK_TILE
    nt = l // tm
    hb = tm // HALO

    def body(up_a, up_g, hp_ref, hn_ref, x1_ref, cw_ref, cb_ref, wd_ref, gate_ref, fn_ref, tgt_ref,
             act_ref, dx2_ref, ddn_ref, acc_ref):
        i = pl.program_id(0)

        @pl.when(i == 0)
        def _():
            acc_ref[...] = jnp.zeros_like(acc_ref)

        g = up_g[...]
        prev_row = jnp.where(i == 0, 0.0, hp_ref[HALO - 1:HALO, :])
        next_row = jnp.where(i == nt - 1, 0.0, hn_ref[0:1, :])
        gprev, gnext = _conv_taps(g, prev_row, next_row)
        gc = cb_ref[...] + gprev * cw_ref[0:1, :] + g * cw_ref[1:2, :] + gnext * cw_ref[2:3, :]
        act = (_gelu(gc) * up_a[...]).astype(BF16)
        act_ref[...] = act
        dn = _dot(act, wd_ref[...])
        x2 = x1_ref[...] + gate_ref[...] * dn
        y, vjp = jax.vjp(_rms, x2, fn_ref[...])
        err = y - tgt_ref[...]
        dx2, dfn = vjp(err * (1.0 / D_MODEL))
        dx2_ref[...] = dx2
        ddn_ref[...] = (dx2 * gate_ref[...]).astype(BF16)
        acc_ref[0:1, :] += dfn
        acc_ref[1:2, :] += jnp.sum(dx2 * dn, axis=0, keepdims=True)
        acc_ref[2:3, :] += (0.5 / D_MODEL) * jnp.sum(err * err)

    row = lambda w: pl.BlockSpec((tm, w), lambda i: (i, 0))
    last = l // HALO - 1
    return pl.pallas_call(
        body, name="ffn_loss", grid=(nt,),
        in_specs=[pl.BlockSpec((tm, D_FF), lambda i: (i, 0)), pl.BlockSpec((tm, D_FF), lambda i: (i, 1)),
                  pl.BlockSpec((HALO, D_FF), lambda i: (jnp.maximum(i * hb - 1, 0), 1)),
                  pl.BlockSpec((HALO, D_FF), lambda i: (jnp.minimum((i + 1) * hb, last), 1)),
                  row(D_MODEL), _full((3, D_FF)), _full((1, D_FF)), _full((D_FF, D_MODEL)),
                  _full((1, D_MODEL)), _full((1, D_MODEL)), row(D_MODEL)],
        out_specs=[row(D_FF), row(D_MODEL), row(D_MODEL), _full((8, D_MODEL))],
        out_shape=[jax.ShapeDtypeStruct((l, D_FF), BF16), jax.ShapeDtypeStruct((l, D_MODEL), F32),
                   jax.ShapeDtypeStruct((l, D_MODEL), BF16), jax.ShapeDtypeStruct((8, D_MODEL), F32)],
        compiler_params=_params(("arbitrary",)),
    )(up, up, up, up, x1, conv_w, conv_b, w_down_b, gate, fnw, tgt)


def _convglu_bwd(up, dact, conv_w, conv_b):
    l = up.shape[0]
    tm = 128
    nt = l // tm
    hb = tm // HALO
    te = tm + 2 * HALO

    def body(a_ref, ap_ref, an_ref, g_ref, gp_ref, gn_ref, d_ref, dp_ref, dn_ref, cw_ref, cb_ref,
             dup_ref, acc_ref):
        i = pl.program_id(0)

        @pl.when(i == 0)
        def _():
            acc_ref[...] = jnp.zeros_like(acc_ref)

        row = _iota2((te, D_FF), 0) + (i * tm - HALO)
        valid = (row >= 0) & (row < l)

        def ext(p, c, n):
            return jnp.where(valid, jnp.concatenate([p[...], c[...], n[...]], axis=0), 0.0)

        ae, ge, de = ext(ap_ref, a_ref, an_ref), ext(gp_ref, g_ref, gn_ref), ext(dp_ref, d_ref, dn_ref)
        gprev = pltpu.roll(ge, 1, 0)
        gnext = pltpu.roll(ge, te - 1, 0)
        w0, w1, w2 = cw_ref[0:1, :], cw_ref[1:2, :], cw_ref[2:3, :]
        gce = cb_ref[...] + gprev * w0 + ge * w1 + gnext * w2
        _, vjp = jax.vjp(lambda a, gc: _gelu(gc) * a, ae, gce)
        dae, dgce = vjp(de)
        dge = dgce * w1 + pltpu.roll(dgce, te - 1, 0) * w0 + pltpu.roll(dgce, 1, 0) * w2
        mid = slice(HALO, HALO + tm)
        dup_ref[:, 0:D_FF] = dae[mid].astype(BF16)
        dup_ref[:, D_FF:2 * D_FF] = dge[mid].astype(BF16)
        dgc = dgce[mid]
        acc_ref[0:1, :] += jnp.sum(dgc * gprev[mid], axis=0, keepdims=True)
        acc_ref[1:2, :] += jnp.sum(dgc * ge[mid], axis=0, keepdims=True)
        acc_ref[2:3, :] += jnp.sum(dgc * gnext[mid], axis=0, keepdims=True)
        acc_ref[3:4, :] += jnp.sum(dgc, axis=0, keepdims=True)

    last = l // HALO - 1

    def trio(cb):
        return [pl.BlockSpec((tm, D_FF), lambda i: (i, cb)),
                pl.BlockSpec((HALO, D_FF), lambda i: (jnp.maximum(i * hb - 1, 0), cb)),
                pl.BlockSpec((HALO, D_FF), lambda i: (jnp.minimum((i + 1) * hb, last), cb))]

    return pl.pallas_call(
        body, name="convglu_bwd", grid=(nt,),
        in_specs=trio(0) + trio(1) + trio(0) + [_full((3, D_FF)), _full((1, D_FF))],
        out_specs=[pl.BlockSpec((tm, 2 * D_FF), lambda i: (i, 0)), _full((8, D_FF))],
        out_shape=[jax.ShapeDtypeStruct((l, 2 * D_FF), BF16), jax.ShapeDtypeStruct((8, D_FF), F32)],
        compiler_params=_params(("arbitrary",)),
    )(up, up, up, up, up, up, dact, dact, dact, conv_w, conv_b)


def _norm2_bwd(x1, dh2, dx2, mix, mod3, n2w):
    l = x1.shape[0]
    tm = TOK_TILE

    def body(x1_ref, dh_ref, dx2_ref, mix_ref, mod_ref, nw_ref, dx1_ref, dmix_ref, acc_ref):
        @pl.when(pl.program_id(0) == 0)
        def _():
            acc_ref[...] = jnp.zeros_like(acc_ref)

        _, vjp = jax.vjp(_rms_mod, x1_ref[...], nw_ref[...], mod_ref[1:2, :], mod_ref[2:3, :])
        dx, dnw, dsh, dsc = vjp(dh_ref[...])
        dx1 = dx + dx2_ref[...]
        dx1_ref[...] = dx1
        dmix_ref[...] = (dx1 * mod_ref[0:1, :]).astype(BF16)
        acc_ref[0:1, :] += dnw
        acc_ref[1:2, :] += jnp.sum(dx1 * mix_ref[...], axis=0, keepdims=True)
        acc_ref[2:3, :] += dsh
        acc_ref[3:4, :] += dsc

    row = pl.BlockSpec((tm, D_MODEL), lambda i: (i, 0))
    return pl.pallas_call(
        body, name="norm2_bwd", grid=(l // tm,),
        in_specs=[row, row, row, row, _full((3, D_MODEL)), _full((1, D_MODEL))],
        out_specs=[row, row, _full((8, D_MODEL))],
        out_shape=[jax.ShapeDtypeStruct((l, D_MODEL), F32), jax.ShapeDtypeStruct((l, D_MODEL), BF16),
                   jax.ShapeDtypeStruct((8, D_MODEL), F32)],
        compiler_params=_params(("arbitrary",)),
    )(x1, dh2, dx2, mix, mod3, n2w)


MOD_ROWS = 16
MOD_COLS = 6 * D_MODEL // 4


def _mod_fwd(c_all, c_ctx, w_mod_b, b_loc):
    def body(c_ref, cc_ref, w_ref, b_ref, m_ref, s_ref):
        cond = jnp.concatenate([c_ref[...], jnp.broadcast_to(cc_ref[...], (8, D_MODEL))], axis=0)
        s = _silu(cond).astype(BF16)
        s_ref[...] = s
        m_ref[...] = _dot(s, w_ref[...]) + b_ref[...]

    return pl.pallas_call(
        body, name="mod_fwd",
        out_shape=[jax.ShapeDtypeStruct((MOD_ROWS, MOD_COLS), F32), jax.ShapeDtypeStruct((MOD_ROWS, D_MODEL), BF16)],
        compiler_params=_params(),
    )(c_all, c_ctx, w_mod_b, b_loc)


def _mod_bwd_sum(dm_all):
    def body(d_ref, dm_ref, gb_ref):
        rows = [d_ref[k, 0:1, :] for k in range(8)]
        ctx_sum = d_ref[0, 1:2, :]
        for k in range(1, 8):
            ctx_sum = ctx_sum + d_ref[k, 1:2, :]
        gb = ctx_sum
        for k in range(8):
            gb = gb + rows[k]
        gb_ref[...] = gb
        dm_ref[...] = jnp.concatenate(rows + [ctx_sum] + [jnp.zeros((7, 6 * D_MODEL), F32)], axis=0)

    return pl.pallas_call(
        body, name="mod_bwd_sum",
        out_shape=[jax.ShapeDtypeStruct((MOD_ROWS, 6 * D_MODEL), F32), jax.ShapeDtypeStruct((1, 6 * D_MODEL), F32)],
        compiler_params=_params(),
    )(dm_all)


def _mod_bwd_w(dm_loc, s_b, c_ctx, w_mod_b):
    def body(d_ref, s_ref, cc_ref, w_ref, gw_ref, gc_ref):
        db = d_ref[...].astype(BF16)
        gw_ref[...] = _dot_tn(s_ref[...], db)
        ds = _dot_nt(db, w_ref[...])
        _, vjp = jax.vjp(_silu, cc_ref[...])
        gc_ref[...] = jnp.broadcast_to(vjp(ds[8:9, :])[0], (8, D_MODEL))

    return pl.pallas_call(
        body, name="mod_bwd_w",
        out_shape=[jax.ShapeDtypeStruct((D_MODEL, MOD_COLS), F32), jax.ShapeDtypeStruct((8, D_MODEL), F32)],
        compiler_params=_params(),
    )(dm_loc, s_b, c_ctx, w_mod_b)


def _adamw(w, g, m, v, name):
    r, c = w.shape
    tr = _pick(r, (256, 128, 64, 32, 16, 8))
    bc1 = 1.0 - ADAM_B1 ** ADAM_STEP
    bc2 = 1.0 - ADAM_B2 ** ADAM_STEP

    def body(w_ref, g_ref, m_ref, v_ref, d_ref, nm_ref, nv_ref):
        gg = g_ref[...]
        nm = ADAM_B1 * m_ref[...] + (1.0 - ADAM_B1) * gg
        nv = ADAM_B2 * v_ref[...] + (1.0 - ADAM_B2) * (gg * gg)
        nm_ref[...] = nm
        nv_ref[...] = nv
        d_ref[...] = -ADAM_LR * ((nm / bc1) / (jnp.sqrt(nv / bc2) + ADAM_EPS) + ADAM_WD * w_ref[...])

    blk = pl.BlockSpec((tr, c), lambda i: (i, 0))
    shp = jax.ShapeDtypeStruct((r, c), F32)
    return pl.pallas_call(
        body, name=name, grid=(r // tr,), in_specs=[blk] * 4, out_specs=[blk] * 3, out_shape=[shp] * 3,
        compiler_params=_params(("parallel",)),
    )(w, g, m, v)


def _sum_slots(a, name):
    n, r, c = a.shape
    tr = _pick(r, (376, 256, 208, 128, 64, 32, 16, 8))

    def body(a_ref, o_ref):
        acc = a_ref[0]
        for k in range(1, n):
            acc = acc + a_ref[k]
        o_ref[...] = acc

    return pl.pallas_call(
        body, name=name, grid=(r // tr,),
        in_specs=[pl.BlockSpec((n, tr, c), lambda i: (0, i, 0))],
        out_specs=pl.BlockSpec((tr, c), lambda i: (i, 0)),
        out_shape=jax.ShapeDtypeStruct((r, c), F32),
        compiler_params=_params(("parallel",)),
    )(a)


def _mesh_pos():
    return lax.axis_index("x"), lax.axis_index("y"), lax.axis_index("c")


def _all_gather8(v, name):
    m_per, n = v.shape

    def body(x_ref, out_ref, send_sems, recv_sems, local_sem):
        x, y, c = _mesh_pos()
        me, sibling = (x, y, c), (x, y, 1 - c)
        chips = [(1 - x, y), (x, 1 - y), (1 - x, 1 - y)]

        def rows(px, py, pc):
            return out_ref.at[pl.ds((4 * px + 2 * py + pc) * m_per, m_per), :]

        def copy(k, block, to, src=None):
            return pltpu.make_async_remote_copy(
                src_ref=rows(*block) if src is None else src, dst_ref=rows(*block),
                send_sem=send_sems.at[k], recv_sem=recv_sems.at[k], device_id=to, device_id_type=MESH_ID)

        mine = pltpu.make_async_copy(x_ref, rows(*me), local_sem)
        mine.start()
        first = [copy(0, me, sibling, src=x_ref)]
        first += [copy(1 + j, me, (*chip, c), src=x_ref) for j, chip in enumerate(chips)]
        for cp in first:
            cp.start()
        passed = [copy(4 + j, (*chip, c), sibling) for j, chip in enumerate(chips)]
        for j, chip in enumerate(chips):
            copy(1 + j, (*chip, c), me).wait_recv()
            passed[j].start()
        copy(0, sibling, me).wait_recv()
        for j, chip in enumerate(chips):
            copy(4 + j, (*chip, 1 - c), me).wait_recv()
        for cp in first + passed:
            cp.wait_send()
        mine.wait()

    return pl.pallas_call(
        body, name=name,
        out_shape=jax.ShapeDtypeStruct((8 * m_per, n), v.dtype),
        in_specs=[pl.BlockSpec(memory_space=pltpu.VMEM)],
        out_specs=pl.BlockSpec(memory_space=pltpu.VMEM),
        scratch_shapes=[pltpu.SemaphoreType.DMA((7,)), pltpu.SemaphoreType.DMA((7,)), pltpu.SemaphoreType.DMA],
        compiler_params=_params(),
    )(v)


def _sib_swap(src, name):
    def body(src_ref, dst_ref, send_sem, recv_sem):
        x, y, c = _mesh_pos()
        cp = pltpu.make_async_remote_copy(src_ref=src_ref, dst_ref=dst_ref, send_sem=send_sem, recv_sem=recv_sem,
                                          device_id=(x, y, 1 - c), device_id_type=MESH_ID)
        cp.start()
        cp.wait()

    return pl.pallas_call(
        body, name=name,
        out_shape=jax.ShapeDtypeStruct(src.shape, src.dtype),
        in_specs=[pl.BlockSpec(memory_space=pl.ANY)],
        out_specs=pl.BlockSpec(memory_space=pl.ANY),
        scratch_shapes=[pltpu.SemaphoreType.DMA, pltpu.SemaphoreType.DMA],
        compiler_params=_params(),
    )(src)


def _chip_exchange(src, name, same_src=False):
    shape = src.shape if same_src else src.shape[1:]

    def body(src_ref, dst_ref, send_sems, recv_sems, local_sem):
        x, y, c = _mesh_pos()
        me = 2 * x + y
        peers = [(x, 1 - y), (1 - x, y), (1 - x, 1 - y)]
        mine = pltpu.make_async_copy(src_ref if same_src else src_ref.at[me], dst_ref.at[me], local_sem)
        mine.start()
        copies = []
        for k, (px, py) in enumerate(peers):
            to = 2 * px + py
            copies.append(pltpu.make_async_remote_copy(
                src_ref=src_ref if same_src else src_ref.at[to], dst_ref=dst_ref.at[me],
                send_sem=send_sems.at[k], recv_sem=recv_sems.at[k], device_id=(px, py, c), device_id_type=MESH_ID))
        for cp in copies:
            cp.start()
        for k, (px, py) in enumerate(peers):
            frm = 2 * px + py
            pltpu.make_async_remote_copy(
                src_ref=src_ref if same_src else src_ref.at[frm], dst_ref=dst_ref.at[frm],
                send_sem=send_sems.at[k], recv_sem=recv_sems.at[k], device_id=(px, py, c),
                device_id_type=MESH_ID).wait_recv()
        for cp in copies:
            cp.wait_send()
        mine.wait()

    return pl.pallas_call(
        body, name=name,
        out_shape=jax.ShapeDtypeStruct((4,) + tuple(shape), src.dtype),
        in_specs=[pl.BlockSpec(memory_space=pl.ANY)],
        out_specs=pl.BlockSpec(memory_space=pl.ANY),
        scratch_shapes=[pltpu.SemaphoreType.DMA((3,)), pltpu.SemaphoreType.DMA((3,)), pltpu.SemaphoreType.DMA],
        compiler_params=_params(),
    )(src)


def _rope_tables(l, lc):
    rows = l // GRID_W
    row = jnp.repeat(jnp.arange(rows, dtype=F32), GRID_W)
    col = jnp.tile(jnp.arange(GRID_W, dtype=F32), rows)
    n_freq = RET_DH // 4
    inv_freq = ROPE_THETA ** (-jnp.arange(n_freq, dtype=F32) / n_freq)
    ang = jnp.concatenate([row[:, None] * inv_freq, col[:, None] * inv_freq], axis=-1)
    cos_t = jnp.repeat(jnp.cos(ang), 2, axis=-1)
    sin_t = jnp.repeat(jnp.sin(ang), 2, axis=-1) * jnp.tile(jnp.array([-1.0, 1.0], F32), RET_DH // 2)
    cos_t = jnp.concatenate([jnp.ones((lc, RET_DH), F32), cos_t], axis=0)
    sin_t = jnp.concatenate([jnp.zeros((lc, RET_DH), F32), sin_t], axis=0)
    return cos_t, sin_t


def _s5_pack(a):
    blk = lambda t: t.reshape(1, S5_NB, 128, S5_STATE)
    lre = jnp.stack([a["s5_lambda_re_f"][0], a["s5_lambda_re_b"][0]]).reshape(2, S5_NB, 8, S5_STATE)
    lim = jnp.stack([a["s5_lambda_im_f"][0], a["s5_lambda_im_b"][0]]).reshape(2, S5_NB, 8, S5_STATE)
    lst = jnp.stack([a["s5_log_step_f"][0], a["s5_log_step_b"][0]]).reshape(2, S5_NB, 8, 1)
    b_re = blk(a["s5_b_re"][0].transpose(0, 2, 1))
    b_im = blk(a["s5_b_im"][0].transpose(0, 2, 1))
    return (lre, lim, lst, b_re, b_im, blk(a["s5_c_re"][0]), blk(a["s5_c_im"][0]),
            a["s5_d"].reshape(1, S5_NB, 1, 128))


def _s5_unpack(g):
    glre, glim, glst, gbre, gbim, gcre, gcim, gd = g
    unb = lambda t: t.reshape(S5_GROUPS, S5_GROUP, S5_STATE).transpose(0, 2, 1)[None]
    return {
        "s5_lambda_re_f": glre[0].reshape(1, S5_GROUPS, S5_STATE), "s5_lambda_re_b": glre[1].reshape(1, S5_GROUPS, S5_STATE),
        "s5_lambda_im_f": glim[0].reshape(1, S5_GROUPS, S5_STATE), "s5_lambda_im_b": glim[1].reshape(1, S5_GROUPS, S5_STATE),
        "s5_log_step_f": glst[0].reshape(1, S5_GROUPS), "s5_log_step_b": glst[1].reshape(1, S5_GROUPS),
        "s5_b_re": unb(gbre), "s5_b_im": unb(gbim),
        "s5_c_re": gcre.reshape(1, S5_GROUPS, S5_GROUP, S5_STATE), "s5_c_im": gcim.reshape(1, S5_GROUPS, S5_GROUP, S5_STATE),
        "s5_d": gd.reshape(1, S5_WIDTH),
    }


def _local_step(a, wb, mx, mc, conv_w):
    x, ctx, tgt = a["x"][0], a["ctx"][0], a["loss_target"][0]
    l, lc = x.shape[0], ctx.shape[0]
    la = l + lc
    nct, ncc, nrc, cn = lc // TOK_TILE, lc // S5_T, lc // RET_CHUNK, la // S5_T
    n1w, n2w, fnw = a["norm1_w"], a["norm2_w"], a["final_norm_w"].reshape(1, D_MODEL)
    conv_b, b_glu = a["conv_b"], a["s5_b_glu"]
    ld_f, ld_b = a["ret_log_decay_f"][0], a["ret_log_decay_b"][0]
    mod4 = jnp.concatenate([mc[0:2], mx[0:2]], axis=0)
    mod3 = mx[2:5]
    gate5 = mx[5:6]
    cos_t, sin_t = _rope_tables(l, lc)
    s5p = _s5_pack(a)

    p_all, h1b = _norm_inproj(x, ctx, n1w, mod4, wb["w_in"])
    p3 = p_all.reshape(cn, S5_T, IN_COLS)
    kb, wst, wout, a16 = _s5_gen(*s5p)
    sloc = _s5_state(p3, wst)
    a16s = a16.transpose(1, 0, 2, 3).reshape(2, S5_GROUPS, 128)
    hs = _s5_scan(sloc.reshape(2, cn, S5_GROUPS, 128), a16s, ncc)
    h2 = hs.reshape(2, cn, S5_NB * S5_SW)
    y_all = _s5_out(p3, kb, h2, wout).reshape(la, S5_WIDTH)
    s5x = _s5_glu(y_all, wb["s5_w_glu"], b_glu, nct)
    of, ssf = _ret_scan(p_all, cos_t, sin_t, ld_f, nrc, False, "ret_scan_f")
    ob, ssb = _ret_scan(p_all, cos_t, sin_t, ld_b, nrc, True, "ret_scan_b")
    retx, y_ret = _ret_gate(of, ob, p_all, nct)
    x1, mix, h2b, up = _outproj_up(x, s5x, retx, wb["w_out"], mod3, n2w, wb["w_up"])
    act, dx2, ddn, acc_f = _ffn_loss(up, x1, conv_w, conv_b, wb["w_down"], gate5, fnw, tgt)

    g = {}
    dact = _mm(ddn, wb["w_down"], nt=True, name="dact")
    g["w_down"] = _mm_tn(act, ddn, name="gw_down")
    dup, acc_c = _convglu_bwd(up, dact, conv_w, conv_b)
    dh2 = _mm(dup, wb["w_up"], nt=True, name="dh2")
    g["w_up"] = _mm_tn(h2b, dup, name="gw_up")
    dx1, dmixb, acc_2 = _norm2_bwd(x1, dh2, dx2, mix, mod3, n2w)
    dmix = _mm(dmixb, wb["w_out"], nt=True, name="dmix")
    g["w_out"] = jnp.concatenate([_mm_tn(s5x, dmixb, name="gw_out_s5"), _mm_tn(retx, dmixb, name="gw_out_ret")], axis=0)

    dy_s5, g["s5_w_glu"], g["s5_b_glu"] = _s5_glu_bwd(y_all, dmix, wb["s5_w_glu"], b_glu, nct)
    dy3 = dy_s5.reshape(cn, S5_T, S5_WIDTH)
    e = _s5_bwd_h(dy3, wout)
    ds, da16 = _s5_scan_bwd(e.reshape(2, cn, S5_GROUPS, 128), hs, a16s, ncc)
    ds2 = ds.reshape(2, cn, S5_NB * S5_SW)
    du = _s5_bwd_u(dy3, kb, ds2, wst).reshape(la, S5_WIDTH)
    dkb = _s5_bwd_kb(p3, dy3)
    dwst, dwout = _s5_bwd_w(p3, dy3, ds2, h2)
    da16p = da16.reshape(2, S5_NB, 8, 128).transpose(1, 0, 2, 3)
    g.update(_s5_unpack(_s5_gen_bwd(*s5p, dkb, dwst, dwout, da16p)))

    dy_ret, dg = _ret_gate_bwd(y_ret, p_all, dmix, nct)
    dqf, dkf, dvf, dldf = _ret_scan_bwd(p_all, cos_t, sin_t, ld_f, ssf, dy_ret, nrc, False, "ret_scan_f_bwd")
    dqb, dkb_, dvb, dldb = _ret_scan_bwd(p_all, cos_t, sin_t, ld_b, ssb, dy_ret, nrc, True, "ret_scan_b_bwd")
    g["ret_log_decay_f"] = dldf[:, 0, 0].reshape(1, RET_HEADS)
    g["ret_log_decay_b"] = dldb[:, 0, 0].reshape(1, RET_HEADS)
    dp = _ret_qkv_grad(dqf, dkf, dvf, dqb, dkb_, dvb, du, dg, cos_t, sin_t)
    dh1 = _mm(dp, wb["w_in"], nt=True, name="dh1")
    g["w_in"] = _mm_tn(h1b, dp, name="gw_in")
    grad_x, acc_1 = _norm_inproj_bwd(x, ctx, n1w, mod4, dh1, dx1)

    g["norm1_w"], g["norm2_w"], g["final_norm_w"] = acc_1[0:1], acc_2[0:1], acc_f[0]
    g["conv_w"], g["conv_b"] = acc_c[0:3], acc_c[3:4]
    zero = jnp.zeros((1, D_MODEL), F32)
    dmx = jnp.concatenate([acc_1[3:5], acc_2[1:2], acc_2[2:4], acc_f[1:2]], axis=0)
    dmc = jnp.concatenate([acc_1[1:3], zero, zero, zero, zero], axis=0)
    return acc_f[2, 0], grad_x, g, dmx, dmc


WEIGHT_NAMES = ("c_ctx", "w_mod", "b_mod", "norm1_w", "w_in", "s5_lambda_re_f", "s5_lambda_im_f", "s5_log_step_f",
                "s5_lambda_re_b", "s5_lambda_im_b", "s5_log_step_b", "s5_b_re", "s5_b_im", "s5_c_re", "s5_c_im",
                "s5_d", "s5_w_glu", "s5_b_glu", "ret_log_decay_f", "ret_log_decay_b", "w_out", "norm2_w", "w_up",
                "conv_w", "conv_b", "w_down", "final_norm_w")
BIG_NAMES = ("w_in", "w_out", "w_up", "w_down")
SMALL_NAMES = ("norm1_w", "norm2_w", "final_norm_w", "conv_b", "conv_w", "s5_lambda_re_f", "s5_lambda_im_f",
               "s5_log_step_f", "s5_lambda_re_b", "s5_lambda_im_b", "s5_log_step_b", "s5_b_re", "s5_b_im", "s5_c_re",
               "s5_c_im", "s5_d", "s5_w_glu", "s5_b_glu", "ret_log_decay_f", "ret_log_decay_b")
ROW = 1024
N_CHIPS = 4


def _rows(t):
    return t.reshape(-1, ROW)


def _pack_rows(parts):
    flat = jnp.concatenate([p.reshape(-1) for p in parts])
    n = flat.shape[0]
    rows = -(-n // (8 * ROW)) * 8
    return jnp.pad(flat, (0, rows * ROW - n)).reshape(rows, ROW)


def _unpack_rows(packed, shapes):
    flat = packed.reshape(-1)
    out, off = [], 0
    for s in shapes:
        n = math.prod(s)
        out.append(flat[off:off + n].reshape(s))
        off += n
    return out


def _col_shards(t, n):
    r = t.shape[0]
    return t.reshape(r, n, -1).transpose(1, 0, 2)


def _by_core(ci, mine, other):
    return jnp.concatenate([jnp.where(ci == 0, mine, other), jnp.where(ci == 0, other, mine)], axis=-2)


def _step(a):
    xi, yi, ci = _mesh_pos()
    chip = 2 * xi + yi
    dev = 2 * chip + ci

    cw_loc = a["conv_w"].reshape(-1)
    small_in = jnp.concatenate([a["c"].reshape(-1), jnp.pad(cw_loc, (0, 24 * 128 - cw_loc.shape[0]))]).reshape(32, 128)
    sg = _all_gather8(small_in, "gather_cond").reshape(8, 32, 128)
    c_all = sg[:, 0:8].reshape(8, D_MODEL)
    conv_w = sg[0::2, 8:32].reshape(N_CHIPS, -1)[:, :cw_loc.shape[0]].reshape(N_CHIPS, 3, -1)
    conv_w = conv_w.transpose(1, 0, 2).reshape(3, D_FF)

    w_rows = [_rows(a["w_in"]), _rows(a["w_out"]), _rows(a["w_up"]), _rows(a["w_down"]), _rows(a["s5_w_glu"])]
    n_rows = [w.shape[0] for w in w_rows]
    packed = jnp.concatenate(w_rows, axis=0).astype(BF16)
    hr = packed.shape[0] // 2
    got = _chip_exchange(lax.dynamic_slice_in_dim(packed, ci * hr, hr, 0), "gather_w_chips", same_src=True)
    full = _by_core(ci, got, _sib_swap(got, "gather_w_sib"))
    offs = [0]
    for r in n_rows:
        offs.append(offs[-1] + r)
    part = lambda k: full[:, offs[k]:offs[k + 1]]
    wb = {
        "w_in": part(0).reshape(N_CHIPS, D_MODEL, -1).transpose(1, 0, 2).reshape(D_MODEL, IN_COLS),
        "w_out": part(1).reshape(D_MODEL, D_MODEL),
        "w_up": part(2).reshape(N_CHIPS, D_MODEL, -1).transpose(1, 0, 2).reshape(D_MODEL, 2 * D_FF),
        "w_down": part(3).reshape(D_FF, D_MODEL),
        "s5_w_glu": part(4).reshape(S5_WIDTH, S5_WIDTH),
    }

    w_mod_b = a["w_mod"][0].astype(BF16)
    c_ctx = a["c_ctx"].reshape(1, D_MODEL)
    b_loc = lax.dynamic_slice_in_dim(a["b_mod"], chip * MOD_COLS, MOD_COLS, 1)
    m_loc, s_b = _mod_fwd(c_all, c_ctx, w_mod_b, b_loc)
    mg = _all_gather8(m_loc, "gather_mod").reshape(8, MOD_ROWS, MOD_COLS)
    m_full = mg[0::2].transpose(1, 0, 2).reshape(MOD_ROWS, 6 * D_MODEL)
    mx = lax.dynamic_slice_in_dim(m_full, dev, 1, 0).reshape(6, D_MODEL)
    mc = m_full[8].reshape(6, D_MODEL)

    loss_part, grad_x, g, dmx, dmc = _local_step(a, wb, mx, mc, conv_w)
    loss = lax.psum(loss_part, ("x", "y", "c"))

    dm_pair = jnp.concatenate([dmx.reshape(1, -1), dmc.reshape(1, -1), jnp.zeros((6, 6 * D_MODEL), F32)], axis=0)
    dm_all = _all_gather8(dm_pair, "gather_dmod").reshape(8, 8, 6 * D_MODEL)
    dm16, gb_mod = _mod_bwd_sum(dm_all)
    dm_loc = lax.dynamic_slice_in_dim(dm16, chip * MOD_COLS, MOD_COLS, 1)
    gw_mod, gcc = _mod_bwd_w(dm_loc, s_b, c_ctx, w_mod_b)

    small_parts = [g[n] for n in SMALL_NAMES] + [gcc[0]]
    small_shapes = [p.shape for p in small_parts]
    sp = _pack_rows(small_parts)
    tot = _sum_slots(_all_gather8(sp, "gather_small_grads").reshape(8, sp.shape[0], ROW), "sum_small_grads")
    small = dict(zip(SMALL_NAMES + ("c_ctx",), _unpack_rows(tot, small_shapes)))
    grads = {n: small[n].reshape(a[n].shape) for n in SMALL_NAMES if n not in ("conv_w", "s5_w_glu")}
    grads["c_ctx"] = (0.5 * small["c_ctx"]).reshape(a["c_ctx"].shape)
    grads["conv_w"] = lax.dynamic_slice_in_dim(small["conv_w"], chip * (D_FF // N_CHIPS), D_FF // N_CHIPS, 1)[None]
    grads["s5_w_glu"] = lax.dynamic_slice_in_dim(small["s5_w_glu"], chip * (S5_WIDTH // N_CHIPS), S5_WIDTH // N_CHIPS, 0)[None]
    grads["b_mod"] = gb_mod
    grads["w_mod"] = gw_mod[None]

    gsh = jnp.concatenate([_col_shards(g["w_in"], N_CHIPS).reshape(N_CHIPS, -1, ROW),
                           g["w_out"].reshape(N_CHIPS, -1, ROW),
                           _col_shards(g["w_up"], N_CHIPS).reshape(N_CHIPS, -1, ROW),
                           g["w_down"].reshape(N_CHIPS, -1, ROW)], axis=1)
    gh = gsh.shape[1] // 2
    keep = lax.dynamic_slice_in_dim(gsh, ci * gh, gh, 1)
    send = lax.dynamic_slice_in_dim(gsh, (1 - ci) * gh, gh, 1)
    pair = _sum_slots(jnp.stack([keep, _sib_swap(send, "rs_sib")]).reshape(2, N_CHIPS * gh, ROW), "rs_pair_sum")
    red = _sum_slots(_chip_exchange(pair.reshape(N_CHIPS, gh, ROW), "rs_chips"), "rs_chip_sum")
    gbig = _by_core(ci, red, _sib_swap(red, "rs_back"))
    big_rows = [_rows(a[n]).shape[0] for n in BIG_NAMES]

    def split_big(t):
        out, off = {}, 0
        for n, r in zip(BIG_NAMES, big_rows):
            out[n] = t[off:off + r].reshape(a[n].shape)
            off += r
        return out

    grads.update(split_big(gbig))

    delta, new_m, new_v = {}, {}, {}
    pk = lambda pre: jnp.concatenate([_rows(a[pre + n]) for n in BIG_NAMES], axis=0)
    for dst, t in zip((delta, new_m, new_v), _adamw(pk(""), gbig, pk("m_"), pk("v_"), "adamw_big")):
        dst.update(split_big(t))
    for dst, t in zip((delta, new_m, new_v),
                      _adamw(a["w_mod"][0], gw_mod, a["m_w_mod"][0], a["v_w_mod"][0], "adamw_mod")):
        dst["w_mod"] = t[None]
    rest = [n for n in WEIGHT_NAMES if n not in BIG_NAMES and n != "w_mod"]
    shapes = [a[n].shape for n in rest]
    pr = lambda pre: _pack_rows([a[pre + n] for n in rest])
    for dst, t in zip((delta, new_m, new_v),
                      _adamw(pr(""), _pack_rows([grads[n] for n in rest]), pr("m_"), pr("v_"), "adamw_small")):
        dst.update(zip(rest, _unpack_rows(t, shapes)))

    return (loss, grad_x[None], *[grads[n] for n in WEIGHT_NAMES], *[delta[n] for n in WEIGHT_NAMES],
            *[new_m[n] for n in WEIGHT_NAMES], *[new_v[n] for n in WEIGHT_NAMES])


def kernel(x, c, ctx, c_ctx, w_mod, b_mod, norm1_w, w_in, s5_lambda_re_f, s5_lambda_im_f, s5_log_step_f, s5_lambda_re_b, s5_lambda_im_b, s5_log_step_b, s5_b_re, s5_b_im, s5_c_re, s5_c_im, s5_d, s5_w_glu, s5_b_glu, ret_log_decay_f, ret_log_decay_b, w_out, norm2_w, w_up, conv_w, conv_b, w_down, final_norm_w, loss_target, m_c_ctx, m_w_mod, m_b_mod, m_norm1_w, m_w_in, m_s5_lambda_re_f, m_s5_lambda_im_f, m_s5_log_step_f, m_s5_lambda_re_b, m_s5_lambda_im_b, m_s5_log_step_b, m_s5_b_re, m_s5_b_im, m_s5_c_re, m_s5_c_im, m_s5_d, m_s5_w_glu, m_s5_b_glu, m_ret_log_decay_f, m_ret_log_decay_b, m_w_out, m_norm2_w, m_w_up, m_conv_w, m_conv_b, m_w_down, m_final_norm_w, v_c_ctx, v_w_mod, v_b_mod, v_norm1_w, v_w_in, v_s5_lambda_re_f, v_s5_lambda_im_f, v_s5_log_step_f, v_s5_lambda_re_b, v_s5_lambda_im_b, v_s5_log_step_b, v_s5_b_re, v_s5_b_im, v_s5_c_re, v_s5_c_im, v_s5_d, v_s5_w_glu, v_s5_b_glu, v_ret_log_decay_f, v_ret_log_decay_b, v_w_out, v_norm2_w, v_w_up, v_conv_w, v_conv_b, v_w_down, v_final_norm_w):
    return _step(dict(locals()))
```

```python
import functools
import math

import jax
import jax.numpy as jnp
from jax import lax
from jax.experimental import pallas as pl
from jax.experimental.pallas import tpu as pltpu

F32 = jnp.float32
BF16 = jnp.bfloat16

D_MODEL = 1024
S5_WIDTH = 512
S5_GROUPS = 32
S5_GROUP = 16
S5_STATE = 64
RET_WIDTH = 512
RET_HEADS = 4
RET_DH = 128
RET_CHUNK = 128
GRID_W = 64
ROPE_THETA = 10000.0
D_FF = 2816
NORM_EPS = 1e-6
IN_COLS = S5_WIDTH + 4 * RET_WIDTH

S5_T = 16
S5_NB = 4
S5_BW = S5_T * 128
S5_SW = 8 * 2 * S5_STATE

ADAM_LR, ADAM_B1, ADAM_B2, ADAM_EPS, ADAM_WD, ADAM_STEP = 0.001, 0.9, 0.999, 1e-08, 0.01, 10

VMEM_LIMIT = 56 * 1024 * 1024
MESH_ID = pl.DeviceIdType.MESH


def _params(sem=None):
    return pltpu.CompilerParams(dimension_semantics=sem, vmem_limit_bytes=VMEM_LIMIT)


def _full(shape):
    n = len(shape)
    return pl.BlockSpec(shape, lambda *_: (0,) * n)


def _dot(a, b):
    return jnp.dot(a, b, preferred_element_type=F32)


def _dot_nt(a, b):
    return lax.dot_general(a, b, (((1,), (1,)), ((), ())), preferred_element_type=F32)


def _dot_tn(a, b):
    return lax.dot_general(a, b, (((0,), (0,)), ((), ())), preferred_element_type=F32)


def _dot_hi(a, b):
    return jnp.dot(a, b, preferred_element_type=F32, precision=lax.Precision.HIGHEST)


def _dot_nt_hi(a, b):
    return lax.dot_general(a, b, (((1,), (1,)), ((), ())), preferred_element_type=F32,
                           precision=lax.Precision.HIGHEST)


def _gelu(x):
    return 0.5 * x * (1.0 + jnp.tanh(0.7978845608028654 * (x + 0.044715 * (x * x * x))))


def _sigmoid(x):
    return 1.0 / (1.0 + jnp.exp(-x))


def _silu(x):
    return x * _sigmoid(x)


def _rms_mod(x, nw, sh, sc):
    r = lax.rsqrt(jnp.mean(x * x, axis=-1, keepdims=True) + NORM_EPS)
    return (x * r * nw) * (1.0 + sc) + sh


def _rms(x, nw):
    r = lax.rsqrt(jnp.mean(x * x, axis=-1, keepdims=True) + NORM_EPS)
    return x * r * nw


def _head_norm_gate(y, g):
    mu = jnp.mean(y, axis=-1, keepdims=True)
    yc = y - mu
    var = jnp.mean(yc * yc, axis=-1, keepdims=True)
    return _silu(g) * (yc * lax.rsqrt(var + NORM_EPS))


def _swap_pairs(t):
    lane = lax.broadcasted_iota(jnp.int32, t.shape, 1)
    return jnp.where(lane % 2 == 0, pltpu.roll(t, RET_DH - 1, 1), pltpu.roll(t, 1, 1))


def _rope(t, cos_t, sin_t):
    return t * cos_t + _swap_pairs(t) * sin_t


def _rope_t(dt, cos_t, sin_t):
    return dt * cos_t + _swap_pairs(dt * sin_t)


def _pick(n, prefs):
    for p in prefs:
        if n % p == 0:
            return p
    return n


def _mm(a, w, *, nt=False, out_dtype=F32, name):
    m, k = a.shape
    n = w.shape[0] if nt else w.shape[1]
    tm = _pick(m, (512, 256, 128))
    tn = _pick(n, (1408, 1024, 1280, 512))

    def body(a_ref, w_ref, o_ref):
        f = _dot_nt if nt else _dot
        o_ref[...] = f(a_ref[...], w_ref[...]).astype(out_dtype)

    w_spec = pl.BlockSpec((tn, k), lambda j, i: (j, 0)) if nt else pl.BlockSpec((k, tn), lambda j, i: (0, j))
    return pl.pallas_call(
        body, name=name, grid=(n // tn, m // tm),
        in_specs=[pl.BlockSpec((tm, k), lambda j, i: (i, 0)), w_spec],
        out_specs=pl.BlockSpec((tm, tn), lambda j, i: (i, j)),
        out_shape=jax.ShapeDtypeStruct((m, n), out_dtype),
        compiler_params=_params(("parallel", "parallel")),
    )(a, w)


def _mm_tn(a, b, *, name):
    m, k = a.shape
    n = b.shape[1]
    tm = _pick(m, (512, 256, 128))
    tn = _pick(n, (1408, 1024, 1280, 512))

    def body(a_ref, b_ref, o_ref):
        @pl.when(pl.program_id(1) == 0)
        def _():
            o_ref[...] = jnp.zeros_like(o_ref)
        o_ref[...] += _dot_tn(a_ref[...], b_ref[...])

    return pl.pallas_call(
        body, name=name, grid=(n // tn, m // tm),
        in_specs=[pl.BlockSpec((tm, k), lambda j, i: (i, 0)), pl.BlockSpec((tm, tn), lambda j, i: (i, j))],
        out_specs=pl.BlockSpec((k, tn), lambda j, i: (0, j)),
        out_shape=jax.ShapeDtypeStruct((k, n), F32),
        compiler_params=_params(("parallel", "arbitrary")),
    )(a, b)


TOK_TILE = 256


def _norm_inproj(x, ctx, n1w, mod4, w_in_b):
    l, lc = x.shape[0], ctx.shape[0]
    tm = TOK_TILE
    nct = lc // tm
    la = l + lc

    def body(x_ref, c_ref, nw_ref, mod_ref, w_ref, p_ref, h_ref):
        is_ctx = pl.program_id(0) < nct
        xt = jnp.where(is_ctx, c_ref[...], x_ref[...])
        sh = jnp.where(is_ctx, mod_ref[0:1, :], mod_ref[2:3, :])
        sc = jnp.where(is_ctx, mod_ref[1:2, :], mod_ref[3:4, :])
        hb = _rms_mod(xt, nw_ref[...], sh, sc).astype(BF16)
        h_ref[...] = hb
        p_ref[...] = _dot(hb, w_ref[...])

    return pl.pallas_call(
        body, name="norm_inproj", grid=(la // tm,),
        in_specs=[pl.BlockSpec((tm, D_MODEL), lambda i: (jnp.maximum(i - nct, 0), 0)),
                  pl.BlockSpec((tm, D_MODEL), lambda i: (jnp.minimum(i, nct - 1), 0)),
                  _full((1, D_MODEL)), _full((4, D_MODEL)), _full((D_MODEL, IN_COLS))],
        out_specs=[pl.BlockSpec((tm, IN_COLS), lambda i: (i, 0)), pl.BlockSpec((tm, D_MODEL), lambda i: (i, 0))],
        out_shape=[jax.ShapeDtypeStruct((la, IN_COLS), F32), jax.ShapeDtypeStruct((la, D_MODEL), BF16)],
        compiler_params=_params(("parallel",)),
    )(x, ctx, n1w, mod4, w_in_b)


def _norm_inproj_bwd(x, ctx, n1w, mod4, dh1, dx1):
    l, lc = x.shape[0], ctx.shape[0]
    tm = TOK_TILE
    nct = lc // tm
    la = l + lc

    def body(x_ref, c_ref, nw_ref, mod_ref, dh_ref, dx1_ref, gx_ref, acc_ref):
        i = pl.program_id(0)
        is_ctx = i < nct

        @pl.when(i == 0)
        def _():
            acc_ref[...] = jnp.zeros_like(acc_ref)

        xt = jnp.where(is_ctx, c_ref[...], x_ref[...])
        sh = jnp.where(is_ctx, mod_ref[0:1, :], mod_ref[2:3, :])
        sc = jnp.where(is_ctx, mod_ref[1:2, :], mod_ref[3:4, :])
        _, vjp = jax.vjp(_rms_mod, xt, nw_ref[...], sh, sc)
        dx, dnw, dsh, dsc = vjp(dh_ref[...])
        gx_ref[...] = dx + dx1_ref[...]
        cf = jnp.where(is_ctx, 1.0, 0.0)
        acc_ref[0:1, :] += dnw
        acc_ref[1:2, :] += cf * dsh
        acc_ref[2:3, :] += cf * dsc
        acc_ref[3:4, :] += (1.0 - cf) * dsh
        acc_ref[4:5, :] += (1.0 - cf) * dsc

    return pl.pallas_call(
        body, name="norm_inproj_bwd", grid=(la // tm,),
        in_specs=[pl.BlockSpec((tm, D_MODEL), lambda i: (jnp.maximum(i - nct, 0), 0)),
                  pl.BlockSpec((tm, D_MODEL), lambda i: (jnp.minimum(i, nct - 1), 0)),
                  _full((1, D_MODEL)), _full((4, D_MODEL)),
                  pl.BlockSpec((tm, D_MODEL), lambda i: (i, 0)),
                  pl.BlockSpec((tm, D_MODEL), lambda i: (jnp.maximum(i - nct, 0), 0))],
        out_specs=[pl.BlockSpec((tm, D_MODEL), lambda i: (jnp.maximum(i - nct, 0), 0)), _full((8, D_MODEL))],
        out_shape=[jax.ShapeDtypeStruct((l, D_MODEL), F32), jax.ShapeDtypeStruct((8, D_MODEL), F32)],
        compiler_params=_params(("arbitrary",)),
    )(x, ctx, n1w, mod4, dh1, dx1)


def _iota2(shape, dim):
    return lax.broadcasted_iota(jnp.int32, shape, dim)


def _group_mask(rows, cols, row_div, col_div):
    return jnp.where(_iota2((rows, cols), 0) // row_div == _iota2((rows, cols), 1) // col_div, 1.0, 0.0).astype(F32)


def _s5_gen_dir(lre, lim, lst, b_re, b_im, c_re, c_im):
    step = jnp.exp(lst)
    mag = jnp.exp(lre * step)
    ar = mag * jnp.cos(lim * step)
    ai = mag * jnp.sin(lim * step)
    den = lre * lre + lim * lim
    xr = ar - 1.0
    cr = (xr * lre + ai * lim) / den
    ci = (ai * lre - xr * lim) / den
    rexp = _group_mask(128, 8, S5_GROUP, 1)
    are, aie = _dot_hi(rexp, ar), _dot_hi(rexp, ai)
    cre, cie = _dot_hi(rexp, cr), _dot_hi(rexp, ci)
    bbr = cre * b_re - cie * b_im
    bbi = cre * b_im + cie * b_re
    gmask = _group_mask(128, 128, S5_GROUP, S5_GROUP)
    pr, pi = jnp.ones_like(are), jnp.zeros_like(are)
    xs, ys = [], []
    for t in range(S5_T + 1):
        if t < S5_T:
            xs.append(jnp.concatenate([bbr * pr - bbi * pi, bbr * pi + bbi * pr], axis=1))
        ys.append(jnp.concatenate([c_re * pr - c_im * pi, -(c_re * pi + c_im * pr)], axis=1))
        pr, pi = pr * are - pi * aie, pr * aie + pi * are
    gs = [_dot_nt_hi(x_t, ys[0]) * gmask for x_t in xs]
    r16, i16 = ar, ai
    for _ in range(4):
        r16, i16 = r16 * r16 - i16 * i16, 2.0 * r16 * i16
    return xs, ys, gs, jnp.concatenate([r16, i16], axis=1)


def _s5_expand(z):
    return jnp.concatenate([z] * 8, axis=1) * _group_mask(128, S5_SW, S5_GROUP, 128)


def _s5_contract(z):
    zm = z * _group_mask(128, S5_SW, S5_GROUP, 128)
    acc = zm[:, 0:128]
    for k in range(1, 8):
        acc = acc + zm[:, 128 * k:128 * (k + 1)]
    return acc


def _s5_param_specs():
    blk3 = lambda r, c: pl.BlockSpec((1, 1, r, c), lambda b, j: (0, b, 0, 0))
    dir3 = lambda r, c: pl.BlockSpec((2, 1, r, c), lambda b, j: (0, b, 0, 0))
    return [dir3(8, S5_STATE), dir3(8, S5_STATE), dir3(8, 1), blk3(128, S5_STATE), blk3(128, S5_STATE),
            blk3(128, S5_STATE), blk3(128, S5_STATE), blk3(1, 128)]


def _s5_gen(lre, lim, lst, b_re, b_im, c_re, c_im, dvec):
    def body(lre_ref, lim_ref, lst_ref, bre_ref, bim_ref, cre_ref, cim_ref, d_ref,
             kb_ref, wst_ref, wout_ref, a16_ref, x_scr, y_scr, g_scr):
        j = pl.program_id(1)

        @pl.when(j == 0)
        def _():
            eye = _group_mask(128, 128, 1, 1)
            g0 = eye * d_ref[0, 0]
            for dr in range(2):
                xs, ys, gs, a16 = _s5_gen_dir(lre_ref[dr, 0], lim_ref[dr, 0], lst_ref[dr, 0], bre_ref[0, 0],
                                              bim_ref[0, 0], cre_ref[0, 0], cim_ref[0, 0])
                a16_ref[0, dr] = a16
                for t in range(S5_T):
                    x_scr[dr, t] = xs[t]
                for t in range(S5_T + 1):
                    y_scr[dr, t] = ys[t]
                g0 = g0 + gs[0]
                for t in range(1, S5_T):
                    g_scr[(S5_T - 1) + t if dr == 0 else (S5_T - 1) - t] = gs[t]
            g_scr[S5_T - 1] = g0

        for i in range(S5_T):
            kb_ref[0, :, 128 * i:128 * (i + 1)] = g_scr[i - j + (S5_T - 1)].astype(BF16)
        wst_ref[0, 0] = _s5_expand(x_scr[0, S5_T - 1 - j]).astype(BF16)
        wst_ref[0, 1] = _s5_expand(x_scr[1, j]).astype(BF16)
        wout_ref[0, 0] = _s5_expand(y_scr[0, j + 1]).astype(BF16)
        wout_ref[0, 1] = _s5_expand(y_scr[1, S5_T - j]).astype(BF16)

    return pl.pallas_call(
        body, name="s5_gen", grid=(S5_NB, S5_T),
        in_specs=_s5_param_specs(),
        out_specs=[pl.BlockSpec((1, 128, S5_BW), lambda b, j: (b, j, 0)),
                   pl.BlockSpec((1, 2, 128, S5_SW), lambda b, j: (b, 0, j, 0)),
                   pl.BlockSpec((1, 2, 128, S5_SW), lambda b, j: (b, 0, j, 0)),
                   pl.BlockSpec((1, 2, 8, 128), lambda b, j: (b, 0, 0, 0))],
        out_shape=[jax.ShapeDtypeStruct((S5_NB, S5_BW, S5_BW), BF16),
                   jax.ShapeDtypeStruct((S5_NB, 2, S5_BW, S5_SW), BF16),
                   jax.ShapeDtypeStruct((S5_NB, 2, S5_BW, S5_SW), BF16),
                   jax.ShapeDtypeStruct((S5_NB, 2, 8, 128), F32)],
        scratch_shapes=[pltpu.VMEM((2, S5_T, 128, 128), F32), pltpu.VMEM((2, S5_T + 1, 128, 128), F32),
                        pltpu.VMEM((2 * S5_T - 1, 128, 128), F32)],
        compiler_params=_params(("parallel", "arbitrary")),
    )(lre, lim, lst, b_re, b_im, c_re, c_im, dvec)


def _s5_gen_bwd(lre, lim, lst, b_re, b_im, c_re, c_im, dvec, dkb, dwst, dwout, da16):
    def body(lre_ref, lim_ref, lst_ref, bre_ref, bim_ref, cre_ref, cim_ref, d_ref,
             dkb_ref, dwst_ref, dwout_ref, da16_ref,
             glre_ref, glim_ref, glst_ref, gbre_ref, gbim_ref, gcre_ref, gcim_ref, gd_ref,
             dx_scr, dy_scr, dg_scr):
        j = pl.program_id(1)

        @pl.when(j == 0)
        def _():
            dg_scr[...] = jnp.zeros_like(dg_scr)
            dy_scr[0, 0] = jnp.zeros((128, 128), F32)
            dy_scr[1, 0] = jnp.zeros((128, 128), F32)

        for i in range(S5_T):
            dg_scr[i - j + (S5_T - 1)] += dkb_ref[0, :, 128 * i:128 * (i + 1)]
        dx_scr[0, S5_T - 1 - j] = _s5_contract(dwst_ref[0, 0])
        dx_scr[1, j] = _s5_contract(dwst_ref[0, 1])
        dy_scr[0, j + 1] = _s5_contract(dwout_ref[0, 0])
        dy_scr[1, S5_T - j] = _s5_contract(dwout_ref[0, 1])

        @pl.when(j == S5_T - 1)
        def _():
            eye = _group_mask(128, 128, 1, 1)
            gd_ref[0, 0] = jnp.sum(dg_scr[S5_T - 1] * eye, axis=0, keepdims=True)
            gb = [None, None, None, None]
            for dr in range(2):
                args = (lre_ref[dr, 0], lim_ref[dr, 0], lst_ref[dr, 0], bre_ref[0, 0], bim_ref[0, 0],
                        cre_ref[0, 0], cim_ref[0, 0])
                _, vjp = jax.vjp(_s5_gen_dir, *args)
                dxs = [dx_scr[dr, t] for t in range(S5_T)]
                dys = [dy_scr[dr, t] for t in range(S5_T + 1)]
                dgs = [dg_scr[(S5_T - 1) + t if dr == 0 else (S5_T - 1) - t] for t in range(S5_T)]
                g = vjp((dxs, dys, dgs, da16_ref[0, dr]))
                glre_ref[dr, 0] = g[0]
                glim_ref[dr, 0] = g[1]
                glst_ref[dr, 0] = g[2]
                for q in range(4):
                    gb[q] = g[3 + q] if gb[q] is None else gb[q] + g[3 + q]
            gbre_ref[0, 0] = gb[0]
            gbim_ref[0, 0] = gb[1]
            gcre_ref[0, 0] = gb[2]
            gcim_ref[0, 0] = gb[3]

    shp = lambda a: jax.ShapeDtypeStruct(a.shape, F32)
    return pl.pallas_call(
        body, name="s5_gen_bwd", grid=(S5_NB, S5_T),
        in_specs=_s5_param_specs() + [
            pl.BlockSpec((1, 128, S5_BW), lambda b, j: (b, j, 0)),
            pl.BlockSpec((1, 2, 128, S5_SW), lambda b, j: (b, 0, j, 0)),
            pl.BlockSpec((1, 2, 128, S5_SW), lambda b, j: (b, 0, j, 0)),
            pl.BlockSpec((1, 2, 8, 128), lambda b, j: (b, 0, 0, 0))],
        out_specs=_s5_param_specs(),
        out_shape=[shp(lre), shp(lim), shp(lst), shp(b_re), shp(b_im), shp(c_re), shp(c_im), shp(dvec)],
        scratch_shapes=[pltpu.VMEM((2, S5_T, 128, 128), F32), pltpu.VMEM((2, S5_T + 1, 128, 128), F32),
                        pltpu.VMEM((2 * S5_T - 1, 128, 128), F32)],
        compiler_params=_params(("parallel", "arbitrary")),
    )(lre, lim, lst, b_re, b_im, c_re, c_im, dvec, dkb, dwst, dwout, da16)


def _s5_ucat(u_ref, lo=0, hi=S5_T):
    return jnp.concatenate([u_ref[:, j, :] for j in range(lo, hi)], axis=1).astype(BF16)


def _s5_put_groups(o_ref, dr, val):
    for gi in range(8):
        o_ref[dr, :, gi, :] = val[:, 128 * gi:128 * (gi + 1)]


def _s5_get_groups(s_ref, dr, n=8):
    return jnp.concatenate([s_ref[dr, :, gi, :] for gi in range(n)], axis=1).astype(BF16)


def _s5_state(p3, wst):
    cn = p3.shape[0]

    def body(u_ref, w_ref, o_ref):
        u = _s5_ucat(u_ref)
        _s5_put_groups(o_ref, 0, _dot(u, w_ref[0, 0]))
        _s5_put_groups(o_ref, 1, _dot(u, w_ref[0, 1]))

    return pl.pallas_call(
        body, name="s5_state", grid=(S5_NB,),
        in_specs=[pl.BlockSpec((cn, S5_T, 128), lambda b: (0, 0, b)),
                  pl.BlockSpec((1, 2, S5_BW, S5_SW), lambda b: (b, 0, 0, 0))],
        out_specs=pl.BlockSpec((2, cn, 8, 128), lambda b: (0, 0, b, 0)),
        out_shape=jax.ShapeDtypeStruct((2, cn, S5_GROUPS, 128), F32),
        compiler_params=_params(("parallel",)),
    )(p3, wst)


def _s5_a_forms(a):
    ra = pltpu.roll(a, S5_STATE, 1)
    low = _iota2(a.shape, 1) < S5_STATE
    return jnp.where(low, a, ra), jnp.where(low, -ra, a)


def _s5_scan(sloc, a16, ncc):
    cn = sloc.shape[1]

    def body(s_ref, a_ref, h_ref):
        forms = [_s5_a_forms(a_ref[dr]) for dr in range(2)]

        def step(s, hs):
            out = []
            for dr in range(2):
                arr, aii = forms[dr]
                c = s if dr == 0 else jnp.where(s < ncc, ncc - 1 - s, cn - 1 - (s - ncc))
                h_ref[dr, c] = hs[dr]
                out.append(hs[dr] * arr + pltpu.roll(hs[dr], S5_STATE, 1) * aii + s_ref[dr, c])
            return tuple(out)

        zero = jnp.zeros((S5_GROUPS, 128), F32)
        lax.fori_loop(0, cn, step, (zero, zero), unroll=4)

    return pl.pallas_call(
        body, name="s5_scan",
        out_shape=jax.ShapeDtypeStruct(sloc.shape, F32),
        compiler_params=_params(),
    )(sloc, a16)


def _s5_scan_bwd(e, hs, a16, ncc):
    cn = e.shape[1]

    def body(e_ref, h_ref, a_ref, ds_ref, da_ref):
        forms = [_s5_a_forms(a_ref[dr]) for dr in range(2)]
        low = _iota2((S5_GROUPS, 128), 1) < S5_STATE

        def step(s, carry):
            out = []
            r = cn - 1 - s
            for dr in range(2):
                arr, aii = forms[dr]
                g, da = carry[dr]
                c = r if dr == 0 else jnp.where(r < ncc, ncc - 1 - r, cn - 1 - (r - ncc))
                ds_ref[dr, c] = g
                h = h_ref[dr, c]
                gh = g * h
                grh = g * pltpu.roll(h, S5_STATE, 1)
                da = da + jnp.where(low, gh + pltpu.roll(gh, S5_STATE, 1), grh - pltpu.roll(grh, S5_STATE, 1))
                g = e_ref[dr, c] + g * arr - pltpu.roll(g, S5_STATE, 1) * aii
                out.append((g, da))
            return tuple(out)

        zero = jnp.zeros((S5_GROUPS, 128), F32)
        res = lax.fori_loop(0, cn, step, ((zero, zero), (zero, zero)), unroll=4)
        da_ref[0] = res[0][1]
        da_ref[1] = res[1][1]

    return pl.pallas_call(
        body, name="s5_scan_bwd",
        out_shape=[jax.ShapeDtypeStruct(e.shape, F32), jax.ShapeDtypeStruct((2, S5_GROUPS, 128), F32)],
        compiler_params=_params(),
    )(e, hs, a16)


def _s5_out(p3, kb, h2, wout):
    cn = p3.shape[0]
    half = S5_T // 2

    def body(u_ref, k_ref, h_ref, w_ref, y_ref):
        u = _s5_ucat(u_ref)
        y = _dot(u, k_ref[0])
        y = y + _dot_nt(_s5_get_groups(h_ref, 0), w_ref[0, 0])
        y = y + _dot_nt(_s5_get_groups(h_ref, 1), w_ref[0, 1])
        for i in range(half):
            y_ref[:, i, :] = y[:, 128 * i:128 * (i + 1)]

    return pl.pallas_call(
        body, name="s5_out", grid=(S5_NB, 2),
        in_specs=[pl.BlockSpec((cn, S5_T, 128), lambda b, q: (0, 0, b)),
                  pl.BlockSpec((1, S5_BW, S5_BW // 2), lambda b, q: (b, 0, q)),
                  pl.BlockSpec((2, cn, 8, 128), lambda b, q: (0, 0, b, 0)),
                  pl.BlockSpec((1, 2, S5_BW // 2, S5_SW), lambda b, q: (b, 0, q, 0))],
        out_specs=pl.BlockSpec((cn, half, 128), lambda b, q: (0, q, b)),
        out_shape=jax.ShapeDtypeStruct((cn, S5_T, S5_WIDTH), F32),
        compiler_params=_params(("parallel", "parallel")),
    )(p3, kb, h2, wout)


def _s5_bwd_h(dy3, wout):
    cn = dy3.shape[0]

    def body(d_ref, w_ref, e_ref):
        d = _s5_ucat(d_ref)
        _s5_put_groups(e_ref, 0, _dot(d, w_ref[0, 0]))
        _s5_put_groups(e_ref, 1, _dot(d, w_ref[0, 1]))

    return pl.pallas_call(
        body, name="s5_bwd_h", grid=(S5_NB,),
        in_specs=[pl.BlockSpec((cn, S5_T, 128), lambda b: (0, 0, b)),
                  pl.BlockSpec((1, 2, S5_BW, S5_SW), lambda b: (b, 0, 0, 0))],
        out_specs=pl.BlockSpec((2, cn, 8, 128), lambda b: (0, 0, b, 0)),
        out_shape=jax.ShapeDtypeStruct((2, cn, S5_GROUPS, 128), F32),
        compiler_params=_params(("parallel",)),
    )(dy3, wout)


def _s5_bwd_u(dy3, kb, ds2, wst):
    cn = dy3.shape[0]
    half = S5_T // 2

    def body(d_ref, k_ref, s_ref, w_ref, o_ref):
        d = _s5_ucat(d_ref)
        du = _dot_nt(d, k_ref[0])
        du = du + _dot_nt(_s5_get_groups(s_ref, 0), w_ref[0, 0])
        du = du + _dot_nt(_s5_get_groups(s_ref, 1), w_ref[0, 1])
        for j in range(half):
            o_ref[:, j, :] = du[:, 128 * j:128 * (j + 1)]

    return pl.pallas_call(
        body, name="s5_bwd_u", grid=(S5_NB, 2),
        in_specs=[pl.BlockSpec((cn, S5_T, 128), lambda b, q: (0, 0, b)),
                  pl.BlockSpec((1, S5_BW // 2, S5_BW), lambda b, q: (b, q, 0)),
                  pl.BlockSpec((2, cn, 8, 128), lambda b, q: (0, 0, b, 0)),
                  pl.BlockSpec((1, 2, S5_BW // 2, S5_SW), lambda b, q: (b, 0, q, 0))],
        out_specs=pl.BlockSpec((cn, half, 128), lambda b, q: (0, q, b)),
        out_shape=jax.ShapeDtypeStruct((cn, S5_T, S5_WIDTH), F32),
        compiler_params=_params(("parallel", "parallel")),
    )(dy3, kb, ds2, wst)


def _s5_bwd_kb(p3, dy3):
    cn = p3.shape[0]
    half = S5_T // 2

    def body(u_ref, d_ref, o_ref):
        o_ref[0] = _dot_tn(_s5_ucat(u_ref), _s5_ucat(d_ref, 0, half))

    return pl.pallas_call(
        body, name="s5_bwd_kb", grid=(S5_NB, 2),
        in_specs=[pl.BlockSpec((cn, S5_T, 128), lambda b, q: (0, 0, b)),
                  pl.BlockSpec((cn, half, 128), lambda b, q: (0, q, b))],
        out_specs=pl.BlockSpec((1, S5_BW, S5_BW // 2), lambda b, q: (b, 0, q)),
        out_shape=jax.ShapeDtypeStruct((S5_NB, S5_BW, S5_BW), F32),
        compiler_params=_params(("parallel", "parallel")),
    )(p3, dy3)


def _s5_bwd_w(u3, st, name):
    cn = u3.shape[0]

    def body(u_ref, s_ref, w_ref):
        w_ref[0, 0] = _dot_tn(_s5_ucat(u_ref), _s5_get_groups(s_ref, 0))

    return pl.pallas_call(
        body, name=name, grid=(S5_NB, 2),
        in_specs=[pl.BlockSpec((cn, S5_T, 128), lambda b, q: (0, 0, b)),
                  pl.BlockSpec((1, cn, 8, 128), lambda b, q: (q, 0, b, 0))],
        out_specs=pl.BlockSpec((1, 1, S5_BW, S5_SW), lambda b, q: (b, q, 0, 0)),
        out_shape=jax.ShapeDtypeStruct((S5_NB, 2, S5_BW, S5_SW), F32),
        compiler_params=_params(("parallel", "parallel")),
    )(u3, st)


def _s5_glu(y_all, w_glu_b, b_glu, nct):
    la = y_all.shape[0]
    tm = TOK_TILE
    l = la - nct * tm

    def body(y_ref, w_ref, b_ref, o_ref):
        yg = _gelu(y_ref[...])
        z = _dot(yg.astype(BF16), w_ref[...]) + b_ref[...]
        o_ref[...] = (yg * _sigmoid(z)).astype(BF16)

    return pl.pallas_call(
        body, name="s5_glu", grid=(l // tm,),
        in_specs=[pl.BlockSpec((tm, S5_WIDTH), lambda i: (i + nct, 0)),
                  _full((S5_WIDTH, S5_WIDTH)), _full((1, S5_WIDTH))],
        out_specs=pl.BlockSpec((tm, S5_WIDTH), lambda i: (i, 0)),
        out_shape=jax.ShapeDtypeStruct((l, S5_WIDTH), BF16),
        compiler_params=_params(("parallel",)),
    )(y_all, w_glu_b, b_glu)


def _s5_glu_bwd(y_all, dmix, w_glu_b, b_glu, nct):
    la = y_all.shape[0]
    tm = TOK_TILE

    def body(y_ref, d_ref, w_ref, b_ref, dy_ref, gw_ref, gb_ref):
        i = pl.program_id(0)

        @pl.when(i == 0)
        def _():
            gw_ref[...] = jnp.zeros_like(gw_ref)
            gb_ref[...] = jnp.zeros_like(gb_ref)

        @pl.when(i < nct)
        def _():
            dy_ref[...] = jnp.zeros_like(dy_ref)

        @pl.when(i >= nct)
        def _():
            y = y_ref[...]
            yg, gelu_vjp = jax.vjp(_gelu, y)
            ygb = yg.astype(BF16)
            sg = _sigmoid(_dot(ygb, w_ref[...]) + b_ref[...])
            ds = d_ref[...]
            dz = ds * yg * sg * (1.0 - sg)
            dzb = dz.astype(BF16)
            dyg = ds * sg + _dot_nt(dzb, w_ref[...])
            dy_ref[...] = gelu_vjp(dyg)[0]
            gw_ref[...] += _dot_tn(ygb, dzb)
            gb_ref[...] += jnp.sum(dz, axis=0, keepdims=True)

    return pl.pallas_call(
        body, name="s5_glu_bwd", grid=(la // tm,),
        in_specs=[pl.BlockSpec((tm, S5_WIDTH), lambda i: (i, 0)),
                  pl.BlockSpec((tm, S5_WIDTH), lambda i: (jnp.maximum(i - nct, 0), 0)),
                  _full((S5_WIDTH, S5_WIDTH)), _full((1, S5_WIDTH))],
        out_specs=[pl.BlockSpec((tm, S5_WIDTH), lambda i: (i, 0)), _full((S5_WIDTH, S5_WIDTH)),
                   _full((1, S5_WIDTH))],
        out_shape=[jax.ShapeDtypeStruct((la, S5_WIDTH), F32), jax.ShapeDtypeStruct((S5_WIDTH, S5_WIDTH), F32),
                   jax.ShapeDtypeStruct((1, S5_WIDTH), F32)],
        compiler_params=_params(("arbitrary",)),
    )(y_all, dmix, w_glu_b, b_glu)


K_SCALE = RET_DH ** -0.5
Q_COL, K_COL, V_COL, G_COL = 4, 8, 12, 16


def _ret_chunk_of(step, ncc, nch, rev):
    if not rev:
        return step
    return jnp.where(step < ncc, ncc - 1 - step, nch - 1 - (step - ncc))


def _ret_decay(ld, rev):
    c = _iota2((RET_CHUNK, RET_CHUNK), 0).astype(F32)
    m = _iota2((RET_CHUNK, RET_CHUNK), 1).astype(F32)
    diff = (m - c) if rev else (c - m)
    keep = (diff > 0) if rev else (diff >= 0)
    expo = jnp.maximum(diff, 0.0)
    dm = jnp.where(keep, jnp.exp(ld * expo), 0.0)
    pos = _iota2((RET_CHUNK, 1), 0).astype(F32)
    xi_e = (RET_CHUNK - pos) if rev else (pos + 1.0)
    zeta_e = pos if rev else (RET_CHUNK - 1.0 - pos)
    return dm, expo, jnp.exp(ld * xi_e), xi_e, jnp.exp(ld * zeta_e), zeta_e


def _ret_specs(nch, ncc, rev, step_of):
    chunk = lambda n: _ret_chunk_of(step_of(n), ncc, nch, rev)
    cols = [pl.BlockSpec((RET_CHUNK, RET_WIDTH), functools.partial(lambda n, cb: (chunk(n), cb), cb=cb))
            for cb in (1, 2, 3)]
    tab = pl.BlockSpec((RET_CHUNK, RET_DH), lambda n: (chunk(n), 0))
    return cols + [tab, tab], pl.BlockSpec((RET_CHUNK, RET_WIDTH), lambda n: (chunk(n), 0))


def _ret_scan(p_all, cos_t, sin_t, ld2, ncc):
    la = p_all.shape[0]
    nch = la // RET_CHUNK

    def body(ld_ref, qf, kf, vf, cf, sf, qb, kb, vb, cb, sb, of_ref, ob_ref, ssf_ref, ssb_ref, s_scr):
        @pl.when(pl.program_id(0) == 0)
        def _():
            s_scr[...] = jnp.zeros_like(s_scr)

        for dr, (q_ref, k_ref, v_ref, c_ref, n_ref, o_ref, ss_ref) in enumerate(
                ((qf, kf, vf, cf, sf, of_ref, ssf_ref), (qb, kb, vb, cb, sb, ob_ref, ssb_ref))):
            cs, sn = c_ref[...], n_ref[...]
            for h in range(RET_HEADS):
                sl = slice(RET_DH * h, RET_DH * (h + 1))
                ldh = ld_ref[dr, h]
                dm, _, xi, _, zeta, _ = _ret_decay(ldh, dr == 1)
                q = _rope(q_ref[:, sl], cs, sn)
                k = _rope(k_ref[:, sl] * K_SCALE, cs, sn)
                vh = v_ref[:, sl].astype(BF16)
                s = s_scr[dr, h]
                ss_ref[0, h] = s
                sc = (_dot_nt(q.astype(BF16), k.astype(BF16)) * dm).astype(BF16)
                o_ref[:, sl] = _dot(sc, vh) + _dot((q * xi).astype(BF16), s.astype(BF16))
                s_scr[dr, h] = jnp.exp(ldh * RET_CHUNK) * s + _dot_tn((k * zeta).astype(BF16), vh)

    in_f, out_f = _ret_specs(nch, ncc, False, lambda n: n)
    in_b, out_b = _ret_specs(nch, ncc, True, lambda n: n)
    ss_spec = pl.BlockSpec((1, RET_HEADS, RET_DH, RET_DH), lambda n: (n, 0, 0, 0))
    o_shape = jax.ShapeDtypeStruct((la, RET_WIDTH), F32)
    ss_shape = jax.ShapeDtypeStruct((nch, RET_HEADS, RET_DH, RET_DH), F32)
    return pl.pallas_call(
        body, name="ret_scan", grid=(nch,),
        in_specs=[pl.BlockSpec(memory_space=pltpu.SMEM)] + in_f + in_b,
        out_specs=[out_f, out_b, ss_spec, ss_spec],
        out_shape=[o_shape, o_shape, ss_shape, ss_shape],
        scratch_shapes=[pltpu.VMEM((2, RET_HEADS, RET_DH, RET_DH), F32)],
        compiler_params=_params(("arbitrary",)),
    )(ld2, p_all, p_all, p_all, cos_t, sin_t, p_all, p_all, p_all, cos_t, sin_t)


def _ret_scan_bwd(p_all, cos_t, sin_t, ld2, ssf, ssb, dy_all, ncc):
    la = p_all.shape[0]
    nch = la // RET_CHUNK

    def body(ld_ref, qf, kf, vf, cf, sf, dof, ssf_ref, qb, kb, vb, cb, sb, dob_, ssb_ref,
             dqf, dkf, dvf, dqb, dkb, dvb, dld_ref, ds_scr):
        @pl.when(pl.program_id(0) == 0)
        def _():
            ds_scr[...] = jnp.zeros_like(ds_scr)
            dld_ref[...] = jnp.zeros_like(dld_ref)

        for dr, (q_ref, k_ref, v_ref, c_ref, n_ref, do_ref, ss_ref, dq_ref, dk_ref, dv_ref) in enumerate(
                ((qf, kf, vf, cf, sf, dof, ssf_ref, dqf, dkf, dvf), (qb, kb, vb, cb, sb, dob_, ssb_ref, dqb, dkb, dvb))):
            cs, sn = c_ref[...], n_ref[...]
            for h in range(RET_HEADS):
                sl = slice(RET_DH * h, RET_DH * (h + 1))
                ldh = ld_ref[dr, h]
                dm, expo, xi, xi_e, zeta, zeta_e = _ret_decay(ldh, dr == 1)
                gc = jnp.exp(ldh * RET_CHUNK)
                q = _rope(q_ref[:, sl], cs, sn)
                k = _rope(k_ref[:, sl] * K_SCALE, cs, sn)
                q16, k16, v16 = q.astype(BF16), k.astype(BF16), v_ref[:, sl].astype(BF16)
                s = ss_ref[0, h]
                s16 = s.astype(BF16)
                ds_in = ds_scr[dr, h]
                ds16 = ds_in.astype(BF16)
                do16 = do_ref[:, sl].astype(BF16)
                qk = _dot_nt(q16, k16)
                dsv = _dot_nt(do16, v16)
                dsc = (dsv * dm).astype(BF16)
                sc16 = (qk * dm).astype(BF16)
                dos = _dot_nt(do16, s16)
                vds = _dot_nt(v16, ds16)
                dq_ref[:, sl] = _dot(dsc, k16) + dos * xi
                dk_ref[:, sl] = _dot_tn(dsc, q16) + vds * zeta
                dv_ref[:, sl] = _dot_tn(sc16, do16) + _dot((k * zeta).astype(BF16), ds16)
                ds_scr[dr, h] = _dot_tn((q * xi).astype(BF16), do16) + gc * ds_in
                dld = (jnp.sum(dsv * qk * dm * expo) + jnp.sum(q * dos * (xi * xi_e))
                       + jnp.sum(k * vds * (zeta * zeta_e)) + RET_CHUNK * gc * jnp.sum(s * ds_in))
                dld_ref[dr, h] += dld

    back = lambda n: nch - 1 - n
    in_f, out_f = _ret_specs(nch, ncc, False, back)
    in_b, out_b = _ret_specs(nch, ncc, True, back)
    ss_spec = pl.BlockSpec((1, RET_HEADS, RET_DH, RET_DH), lambda n: (nch - 1 - n, 0, 0, 0))
    shp = jax.ShapeDtypeStruct((la, RET_WIDTH), F32)
    return pl.pallas_call(
        body, name="ret_scan_bwd", grid=(nch,),
        in_specs=[pl.BlockSpec(memory_space=pltpu.SMEM)] + in_f + [out_f, ss_spec] + in_b + [out_b, ss_spec],
        out_specs=[out_f, out_f, out_f, out_b, out_b, out_b, _full((2, RET_HEADS, 8, 128))],
        out_shape=[shp] * 6 + [jax.ShapeDtypeStruct((2, RET_HEADS, 8, 128), F32)],
        scratch_shapes=[pltpu.VMEM((2, RET_HEADS, RET_DH, RET_DH), F32)],
        compiler_params=_params(("arbitrary",)),
    )(ld2, p_all, p_all, p_all, cos_t, sin_t, dy_all, ssf, p_all, p_all, p_all, cos_t, sin_t, dy_all, ssb)


def _ret_gate(of, ob, p_all, nct):
    la = of.shape[0]
    tm = TOK_TILE
    l = la - nct * tm

    def body(of_ref, ob_ref, g_ref, r_ref, y_ref):
        y = of_ref[...] + ob_ref[...]
        y_ref[...] = y
        for h in range(RET_HEADS):
            sl = slice(RET_DH * h, RET_DH * (h + 1))
            r_ref[:, sl] = _head_norm_gate(y[:, sl], g_ref[:, sl]).astype(BF16)

    row = pl.BlockSpec((tm, RET_WIDTH), lambda i: (i + nct, 0))
    out = pl.BlockSpec((tm, RET_WIDTH), lambda i: (i, 0))
    return pl.pallas_call(
        body, name="ret_gate", grid=(l // tm,),
        in_specs=[row, row, pl.BlockSpec((tm, RET_WIDTH), lambda i: (i + nct, G_COL // 4))],
        out_specs=[out, out],
        out_shape=[jax.ShapeDtypeStruct((l, RET_WIDTH), BF16), jax.ShapeDtypeStruct((l, RET_WIDTH), F32)],
        compiler_params=_params(("parallel",)),
    )(of, ob, p_all)


def _ret_gate_bwd(y_ret, p_all, dmix, nct):
    la = p_all.shape[0]
    tm = TOK_TILE

    def body(y_ref, g_ref, d_ref, dy_ref, dg_ref):
        i = pl.program_id(0)

        @pl.when(i < nct)
        def _():
            dy_ref[...] = jnp.zeros_like(dy_ref)
            dg_ref[...] = jnp.zeros_like(dg_ref)

        @pl.when(i >= nct)
        def _():
            for h in range(RET_HEADS):
                sl = slice(RET_DH * h, RET_DH * (h + 1))
                _, vjp = jax.vjp(_head_norm_gate, y_ref[:, sl], g_ref[:, sl])
                dy, dg = vjp(d_ref[:, sl])
                dy_ref[:, sl] = dy
                dg_ref[:, sl] = dg

    xrow = lambda cb: pl.BlockSpec((tm, RET_WIDTH), lambda i: (jnp.maximum(i - nct, 0), cb))
    out = pl.BlockSpec((tm, RET_WIDTH), lambda i: (i, 0))
    shp = jax.ShapeDtypeStruct((la, RET_WIDTH), F32)
    return pl.pallas_call(
        body, name="ret_gate_bwd", grid=(la // tm,),
        in_specs=[xrow(0), pl.BlockSpec((tm, RET_WIDTH), lambda i: (i, G_COL // 4)), xrow(1)],
        out_specs=[out, out], out_shape=[shp, shp],
        compiler_params=_params(("parallel",)),
    )(y_ret, p_all, dmix)


def _ret_qkv_grad(dqf, dkf, dvf, dqb, dkb, dvb, du, dg, cos_t, sin_t):
    la = dqf.shape[0]
    tm = TOK_TILE

    def body(dqf_ref, dkf_ref, dvf_ref, dqb_ref, dkb_ref, dvb_ref, du_ref, dg_ref, cos_ref, sin_ref, dp_ref):
        cs, sn = cos_ref[...], sin_ref[...]
        dp_ref[:, 0:S5_WIDTH] = du_ref[...].astype(BF16)
        for h in range(RET_HEADS):
            sl = slice(RET_DH * h, RET_DH * (h + 1))
            dq = _rope_t(dqf_ref[:, sl] + dqb_ref[:, sl], cs, sn)
            dk = _rope_t(dkf_ref[:, sl] + dkb_ref[:, sl], cs, sn) * K_SCALE
            dp_ref[:, 128 * (Q_COL + h):128 * (Q_COL + h + 1)] = dq.astype(BF16)
            dp_ref[:, 128 * (K_COL + h):128 * (K_COL + h + 1)] = dk.astype(BF16)
        dp_ref[:, 128 * V_COL:128 * G_COL] = (dvf_ref[...] + dvb_ref[...]).astype(BF16)
        dp_ref[:, 128 * G_COL:IN_COLS] = dg_ref[...].astype(BF16)

    row = pl.BlockSpec((tm, RET_WIDTH), lambda i: (i, 0))
    tab = pl.BlockSpec((tm, RET_DH), lambda i: (i, 0))
    return pl.pallas_call(
        body, name="ret_qkv_grad", grid=(la // tm,),
        in_specs=[row] * 8 + [tab, tab],
        out_specs=pl.BlockSpec((tm, IN_COLS), lambda i: (i, 0)),
        out_shape=jax.ShapeDtypeStruct((la, IN_COLS), BF16),
        compiler_params=_params(("parallel",)),
    )(dqf, dkf, dvf, dqb, dkb, dvb, du, dg, cos_t, sin_t)


def _outproj_up(x, s5x, retx, w_out_b, mod3, n2w, w_up_b):
    l = x.shape[0]
    tm = TOK_TILE

    def body(x_ref, s_ref, r_ref, wo_ref, mod_ref, nw_ref, wu_ref, x1_ref, mix_ref, h2_ref, up_ref):
        mix = _dot(s_ref[...], wo_ref[0:S5_WIDTH, :]) + _dot(r_ref[...], wo_ref[S5_WIDTH:D_MODEL, :])
        mix_ref[...] = mix
        x1 = x_ref[...] + mod_ref[0:1, :] * mix
        x1_ref[...] = x1
        h2 = _rms_mod(x1, nw_ref[...], mod_ref[1:2, :], mod_ref[2:3, :]).astype(BF16)
        h2_ref[...] = h2
        up_ref[...] = _dot(h2, wu_ref[...])

    row = lambda w: pl.BlockSpec((tm, w), lambda i: (i, 0))
    return pl.pallas_call(
        body, name="outproj_up", grid=(l // tm,),
        in_specs=[row(D_MODEL), row(S5_WIDTH), row(RET_WIDTH), _full((D_MODEL, D_MODEL)), _full((3, D_MODEL)),
                  _full((1, D_MODEL)), _full((D_MODEL, 2 * D_FF))],
        out_specs=[row(D_MODEL), row(D_MODEL), row(D_MODEL), row(2 * D_FF)],
        out_shape=[jax.ShapeDtypeStruct((l, D_MODEL), F32), jax.ShapeDtypeStruct((l, D_MODEL), F32),
                   jax.ShapeDtypeStruct((l, D_MODEL), BF16), jax.ShapeDtypeStruct((l, 2 * D_FF), F32)],
        compiler_params=_params(("parallel",)),
    )(x, s5x, retx, w_out_b, mod3, n2w, w_up_b)


HALO = 8


def _conv_taps(g, prev_row, next_row):
    t = g.shape[0]
    r = _iota2(g.shape, 0)
    gprev = jnp.where(r == 0, prev_row, pltpu.roll(g, 1, 0))
    gnext = jnp.where(r == t - 1, next_row, pltpu.roll(g, t - 1, 0))
    return gprev, gnext


def _ffn_loss(up, x1, conv_w, conv_b, w_down_b, gate, fnw, tgt):
    l = x1.shape[0]
    tm = TOK_TILE
    nt = l // tm
    hb = tm // HALO

    def body(up_a, up_g, hp_ref, hn_ref, x1_ref, cw_ref, cb_ref, wd_ref, gate_ref, fn_ref, tgt_ref,
             act_ref, dx2_ref, ddn_ref, acc_ref):
        i = pl.program_id(0)

        @pl.when(i == 0)
        def _():
            acc_ref[...] = jnp.zeros_like(acc_ref)

        g = up_g[...]
        prev_row = jnp.where(i == 0, 0.0, hp_ref[HALO - 1:HALO, :])
        next_row = jnp.where(i == nt - 1, 0.0, hn_ref[0:1, :])
        gprev, gnext = _conv_taps(g, prev_row, next_row)
        gc = cb_ref[...] + gprev * cw_ref[0:1, :] + g * cw_ref[1:2, :] + gnext * cw_ref[2:3, :]
        act = (_gelu(gc) * up_a[...]).astype(BF16)
        act_ref[...] = act
        dn = _dot(act, wd_ref[...])
        x2 = x1_ref[...] + gate_ref[...] * dn
        y, vjp = jax.vjp(_rms, x2, fn_ref[...])
        err = y - tgt_ref[...]
        dx2, dfn = vjp(err * (1.0 / D_MODEL))
        dx2_ref[...] = dx2
        ddn_ref[...] = (dx2 * gate_ref[...]).astype(BF16)
        acc_ref[0:1, :] += dfn
        acc_ref[1:2, :] += jnp.sum(dx2 * dn, axis=0, keepdims=True)
        acc_ref[2:3, :] += (0.5 / D_MODEL) * jnp.sum(err * err)

    row = lambda w: pl.BlockSpec((tm, w), lambda i: (i, 0))
    last = l // HALO - 1
    return pl.pallas_call(
        body, name="ffn_loss", grid=(nt,),
        in_specs=[pl.BlockSpec((tm, D_FF), lambda i: (i, 0)), pl.BlockSpec((tm, D_FF), lambda i: (i, 1)),
                  pl.BlockSpec((HALO, D_FF), lambda i: (jnp.maximum(i * hb - 1, 0), 1)),
                  pl.BlockSpec((HALO, D_FF), lambda i: (jnp.minimum((i + 1) * hb, last), 1)),
                  row(D_MODEL), _full((3, D_FF)), _full((1, D_FF)), _full((D_FF, D_MODEL)),
                  _full((1, D_MODEL)), _full((1, D_MODEL)), row(D_MODEL)],
        out_specs=[row(D_FF), row(D_MODEL), row(D_MODEL), _full((8, D_MODEL))],
        out_shape=[jax.ShapeDtypeStruct((l, D_FF), BF16), jax.ShapeDtypeStruct((l, D_MODEL), F32),
                   jax.ShapeDtypeStruct((l, D_MODEL), BF16), jax.ShapeDtypeStruct((8, D_MODEL), F32)],
        compiler_params=_params(("arbitrary",)),
    )(up, up, up, up, x1, conv_w, conv_b, w_down_b, gate, fnw, tgt)


def _convglu_bwd(up, dact, conv_w, conv_b):
    l = up.shape[0]
    tm = 128
    nt = l // tm
    hb = tm // HALO
    te = tm + 2 * HALO

    def body(a_ref, ap_ref, an_ref, g_ref, gp_ref, gn_ref, d_ref, dp_ref, dn_ref, cw_ref, cb_ref,
             dup_ref, acc_ref):
        i = pl.program_id(0)

        @pl.when(i == 0)
        def _():
            acc_ref[...] = jnp.zeros_like(acc_ref)

        row = _iota2((te, D_FF), 0) + (i * tm - HALO)
        valid = (row >= 0) & (row < l)

        def ext(p, c, n):
            return jnp.where(valid, jnp.concatenate([p[...], c[...], n[...]], axis=0), 0.0)

        ae, ge, de = ext(ap_ref, a_ref, an_ref), ext(gp_ref, g_ref, gn_ref), ext(dp_ref, d_ref, dn_ref)
        gprev = pltpu.roll(ge, 1, 0)
        gnext = pltpu.roll(ge, te - 1, 0)
        w0, w1, w2 = cw_ref[0:1, :], cw_ref[1:2, :], cw_ref[2:3, :]
        gce = cb_ref[...] + gprev * w0 + ge * w1 + gnext * w2
        _, vjp = jax.vjp(lambda a, gc: _gelu(gc) * a, ae, gce)
        dae, dgce = vjp(de)
        dge = dgce * w1 + pltpu.roll(dgce, te - 1, 0) * w0 + pltpu.roll(dgce, 1, 0) * w2
        mid = slice(HALO, HALO + tm)
        dup_ref[:, 0:D_FF] = dae[mid].astype(BF16)
        dup_ref[:, D_FF:2 * D_FF] = dge[mid].astype(BF16)
        dgc = dgce[mid]
        acc_ref[0:1, :] += jnp.sum(dgc * gprev[mid], axis=0, keepdims=True)
        acc_ref[1:2, :] += jnp.sum(dgc * ge[mid], axis=0, keepdims=True)
        acc_ref[2:3, :] += jnp.sum(dgc * gnext[mid], axis=0, keepdims=True)
        acc_ref[3:4, :] += jnp.sum(dgc, axis=0, keepdims=True)

    last = l // HALO - 1

    def trio(cb):
        return [pl.BlockSpec((tm, D_FF), lambda i: (i, cb)),
                pl.BlockSpec((HALO, D_FF), lambda i: (jnp.maximum(i * hb - 1, 0), cb)),
                pl.BlockSpec((HALO, D_FF), lambda i: (jnp.minimum((i + 1) * hb, last), cb))]

    return pl.pallas_call(
        body, name="convglu_bwd", grid=(nt,),
        in_specs=trio(0) + trio(1) + trio(0) + [_full((3, D_FF)), _full((1, D_FF))],
        out_specs=[pl.BlockSpec((tm, 2 * D_FF), lambda i: (i, 0)), _full((8, D_FF))],
        out_shape=[jax.ShapeDtypeStruct((l, 2 * D_FF), BF16), jax.ShapeDtypeStruct((8, D_FF), F32)],
        compiler_params=_params(("arbitrary",)),
    )(up, up, up, up, up, up, dact, dact, dact, conv_w, conv_b)


def _norm2_bwd(x1, dh2, dx2, mix, mod3, n2w):
    l = x1.shape[0]
    tm = TOK_TILE

    def body(x1_ref, dh_ref, dx2_ref, mix_ref, mod_ref, nw_ref, dx1_ref, dmix_ref, acc_ref):
        @pl.when(pl.program_id(0) == 0)
        def _():
            acc_ref[...] = jnp.zeros_like(acc_ref)

        _, vjp = jax.vjp(_rms_mod, x1_ref[...], nw_ref[...], mod_ref[1:2, :], mod_ref[2:3, :])
        dx, dnw, dsh, dsc = vjp(dh_ref[...])
        dx1 = dx + dx2_ref[...]
        dx1_ref[...] = dx1
        dmix_ref[...] = (dx1 * mod_ref[0:1, :]).astype(BF16)
        acc_ref[0:1, :] += dnw
        acc_ref[1:2, :] += jnp.sum(dx1 * mix_ref[...], axis=0, keepdims=True)
        acc_ref[2:3, :] += dsh
        acc_ref[3:4, :] += dsc

    row = pl.BlockSpec((tm, D_MODEL), lambda i: (i, 0))
    return pl.pallas_call(
        body, name="norm2_bwd", grid=(l // tm,),
        in_specs=[row, row, row, row, _full((3, D_MODEL)), _full((1, D_MODEL))],
        out_specs=[row, row, _full((8, D_MODEL))],
        out_shape=[jax.ShapeDtypeStruct((l, D_MODEL), F32), jax.ShapeDtypeStruct((l, D_MODEL), BF16),
                   jax.ShapeDtypeStruct((8, D_MODEL), F32)],
        compiler_params=_params(("arbitrary",)),
    )(x1, dh2, dx2, mix, mod3, n2w)


MOD_ROWS = 16
MOD_COLS = 6 * D_MODEL // 4


def _mod_fwd(c_all, c_ctx, w_mod_b, b_loc):
    def body(c_ref, cc_ref, w_ref, b_ref, m_ref, s_ref):
        cond = jnp.concatenate([c_ref[...], jnp.broadcast_to(cc_ref[...], (8, D_MODEL))], axis=0)
        s = _silu(cond).astype(BF16)
        s_ref[...] = s
        m_ref[...] = _dot(s, w_ref[...]) + b_ref[...]

    return pl.pallas_call(
        body, name="mod_fwd",
        out_shape=[jax.ShapeDtypeStruct((MOD_ROWS, MOD_COLS), F32), jax.ShapeDtypeStruct((MOD_ROWS, D_MODEL), BF16)],
        compiler_params=_params(),
    )(c_all, c_ctx, w_mod_b, b_loc)


def _mod_bwd_sum(dm_all):
    def body(d_ref, dm_ref, gb_ref):
        rows = [d_ref[k, 0:1, :] for k in range(8)]
        ctx_sum = d_ref[0, 1:2, :]
        for k in range(1, 8):
            ctx_sum = ctx_sum + d_ref[k, 1:2, :]
        gb = ctx_sum
        for k in range(8):
            gb = gb + rows[k]
        gb_ref[...] = gb
        dm_ref[...] = jnp.concatenate(rows + [ctx_sum] + [jnp.zeros((7, 6 * D_MODEL), F32)], axis=0)

    return pl.pallas_call(
        body, name="mod_bwd_sum",
        out_shape=[jax.ShapeDtypeStruct((MOD_ROWS, 6 * D_MODEL), F32), jax.ShapeDtypeStruct((1, 6 * D_MODEL), F32)],
        compiler_params=_params(),
    )(dm_all)


def _mod_bwd_w(dm_loc, s_b, c_ctx, w_mod_b):
    def body(d_ref, s_ref, cc_ref, w_ref, gw_ref, gc_ref):
        db = d_ref[...].astype(BF16)
        gw_ref[...] = _dot_tn(s_ref[...], db)
        ds = _dot_nt(db, w_ref[...])
        _, vjp = jax.vjp(_silu, cc_ref[...])
        gc_ref[...] = jnp.broadcast_to(vjp(ds[8:9, :])[0], (8, D_MODEL))

    return pl.pallas_call(
        body, name="mod_bwd_w",
        out_shape=[jax.ShapeDtypeStruct((D_MODEL, MOD_COLS), F32), jax.ShapeDtypeStruct((8, D_MODEL), F32)],
        compiler_params=_params(),
    )(dm_loc, s_b, c_ctx, w_mod_b)


def _adamw(w, g, m, v, name):
    r, c = w.shape
    tr = _pick(r, (256, 128, 64, 32, 16, 8))
    bc1 = 1.0 - ADAM_B1 ** ADAM_STEP
    bc2 = 1.0 - ADAM_B2 ** ADAM_STEP

    def body(w_ref, g_ref, m_ref, v_ref, d_ref, nm_ref, nv_ref):
        gg = g_ref[...]
        nm = ADAM_B1 * m_ref[...] + (1.0 - ADAM_B1) * gg
        nv = ADAM_B2 * v_ref[...] + (1.0 - ADAM_B2) * (gg * gg)
        nm_ref[...] = nm
        nv_ref[...] = nv
        d_ref[...] = -ADAM_LR * ((nm / bc1) / (jnp.sqrt(nv / bc2) + ADAM_EPS) + ADAM_WD * w_ref[...])

    blk = pl.BlockSpec((tr, c), lambda i: (i, 0))
    shp = jax.ShapeDtypeStruct((r, c), F32)
    return pl.pallas_call(
        body, name=name, grid=(r // tr,), in_specs=[blk] * 4, out_specs=[blk] * 3, out_shape=[shp] * 3,
        compiler_params=_params(("parallel",)),
    )(w, g, m, v)


def _sum_slots(a, name):
    n, r, c = a.shape
    tr = _pick(r, (376, 256, 208, 128, 64, 32, 16, 8))

    def body(a_ref, o_ref):
        acc = a_ref[0]
        for k in range(1, n):
            acc = acc + a_ref[k]
        o_ref[...] = acc

    return pl.pallas_call(
        body, name=name, grid=(r // tr,),
        in_specs=[pl.BlockSpec((n, tr, c), lambda i: (0, i, 0))],
        out_specs=pl.BlockSpec((tr, c), lambda i: (i, 0)),
        out_shape=jax.ShapeDtypeStruct((r, c), F32),
        compiler_params=_params(("parallel",)),
    )(a)


def _mesh_pos():
    return lax.axis_index("x"), lax.axis_index("y"), lax.axis_index("c")


def _all_gather8(v, name):
    m_per, n = v.shape

    def body(x_ref, out_ref, send_sems, recv_sems, local_sem):
        x, y, c = _mesh_pos()
        me, sibling = (x, y, c), (x, y, 1 - c)
        chips = [(1 - x, y), (x, 1 - y), (1 - x, 1 - y)]

        def rows(px, py, pc):
            return out_ref.at[pl.ds((4 * px + 2 * py + pc) * m_per, m_per), :]

        def copy(k, block, to, src=None):
            return pltpu.make_async_remote_copy(
                src_ref=rows(*block) if src is None else src, dst_ref=rows(*block),
                send_sem=send_sems.at[k], recv_sem=recv_sems.at[k], device_id=to, device_id_type=MESH_ID)

        mine = pltpu.make_async_copy(x_ref, rows(*me), local_sem)
        mine.start()
        first = [copy(0, me, sibling, src=x_ref)]
        first += [copy(1 + j, me, (*chip, c), src=x_ref) for j, chip in enumerate(chips)]
        for cp in first:
            cp.start()
        passed = [copy(4 + j, (*chip, c), sibling) for j, chip in enumerate(chips)]
        for j, chip in enumerate(chips):
            copy(1 + j, (*chip, c), me).wait_recv()
            passed[j].start()
        copy(0, sibling, me).wait_recv()
        for j, chip in enumerate(chips):
            copy(4 + j, (*chip, 1 - c), me).wait_recv()
        for cp in first + passed:
            cp.wait_send()
        mine.wait()

    return pl.pallas_call(
        body, name=name,
        out_shape=jax.ShapeDtypeStruct((8 * m_per, n), v.dtype),
        in_specs=[pl.BlockSpec(memory_space=pltpu.VMEM)],
        out_specs=pl.BlockSpec(memory_space=pltpu.VMEM),
        scratch_shapes=[pltpu.SemaphoreType.DMA((7,)), pltpu.SemaphoreType.DMA((7,)), pltpu.SemaphoreType.DMA],
        compiler_params=_params(),
    )(v)


def _sib_swap(src, name):
    def body(src_ref, dst_ref, send_sem, recv_sem):
        x, y, c = _mesh_pos()
        cp = pltpu.make_async_remote_copy(src_ref=src_ref, dst_ref=dst_ref, send_sem=send_sem, recv_sem=recv_sem,
                                          device_id=(x, y, 1 - c), device_id_type=MESH_ID)
        cp.start()
        cp.wait()

    return pl.pallas_call(
        body, name=name,
        out_shape=jax.ShapeDtypeStruct(src.shape, src.dtype),
        in_specs=[pl.BlockSpec(memory_space=pl.ANY)],
        out_specs=pl.BlockSpec(memory_space=pl.ANY),
        scratch_shapes=[pltpu.SemaphoreType.DMA, pltpu.SemaphoreType.DMA],
        compiler_params=_params(),
    )(src)


def _chip_exchange(src, name, same_src=False):
    shape = src.shape if same_src else src.shape[1:]

    def body(src_ref, dst_ref, send_sems, recv_sems, local_sem):
        x, y, c = _mesh_pos()
        me = 2 * x + y
        peers = [(x, 1 - y), (1 - x, y), (1 - x, 1 - y)]
        mine = pltpu.make_async_copy(src_ref if same_src else src_ref.at[me], dst_ref.at[me], local_sem)
        mine.start()
        copies = []
        for k, (px, py) in enumerate(peers):
            to = 2 * px + py
            copies.append(pltpu.make_async_remote_copy(
                src_ref=src_ref if same_src else src_ref.at[to], dst_ref=dst_ref.at[me],
                send_sem=send_sems.at[k], recv_sem=recv_sems.at[k], device_id=(px, py, c), device_id_type=MESH_ID))
        for cp in copies:
            cp.start()
        for k, (px, py) in enumerate(peers):
            frm = 2 * px + py
            pltpu.make_async_remote_copy(
                src_ref=src_ref if same_src else src_ref.at[frm], dst_ref=dst_ref.at[frm],
                send_sem=send_sems.at[k], recv_sem=recv_sems.at[k], device_id=(px, py, c),
                device_id_type=MESH_ID).wait_recv()
        for cp in copies:
            cp.wait_send()
        mine.wait()

    return pl.pallas_call(
        body, name=name,
        out_shape=jax.ShapeDtypeStruct((4,) + tuple(shape), src.dtype),
        in_specs=[pl.BlockSpec(memory_space=pl.ANY)],
        out_specs=pl.BlockSpec(memory_space=pl.ANY),
        scratch_shapes=[pltpu.SemaphoreType.DMA((3,)), pltpu.SemaphoreType.DMA((3,)), pltpu.SemaphoreType.DMA],
        compiler_params=_params(),
    )(src)


def _rope_tables(l, lc):
    rows = l // GRID_W
    row = jnp.repeat(jnp.arange(rows, dtype=F32), GRID_W)
    col = jnp.tile(jnp.arange(GRID_W, dtype=F32), rows)
    n_freq = RET_DH // 4
    inv_freq = ROPE_THETA ** (-jnp.arange(n_freq, dtype=F32) / n_freq)
    ang = jnp.concatenate([row[:, None] * inv_freq, col[:, None] * inv_freq], axis=-1)
    cos_t = jnp.repeat(jnp.cos(ang), 2, axis=-1)
    sin_t = jnp.repeat(jnp.sin(ang), 2, axis=-1) * jnp.tile(jnp.array([-1.0, 1.0], F32), RET_DH // 2)
    cos_t = jnp.concatenate([jnp.ones((lc, RET_DH), F32), cos_t], axis=0)
    sin_t = jnp.concatenate([jnp.zeros((lc, RET_DH), F32), sin_t], axis=0)
    return cos_t, sin_t


def _s5_pack(a):
    blk = lambda t: t.reshape(1, S5_NB, 128, S5_STATE)
    lre = jnp.stack([a["s5_lambda_re_f"][0], a["s5_lambda_re_b"][0]]).reshape(2, S5_NB, 8, S5_STATE)
    lim = jnp.stack([a["s5_lambda_im_f"][0], a["s5_lambda_im_b"][0]]).reshape(2, S5_NB, 8, S5_STATE)
    lst = jnp.stack([a["s5_log_step_f"][0], a["s5_log_step_b"][0]]).reshape(2, S5_NB, 8, 1)
    b_re = blk(a["s5_b_re"][0].transpose(0, 2, 1))
    b_im = blk(a["s5_b_im"][0].transpose(0, 2, 1))
    return (lre, lim, lst, b_re, b_im, blk(a["s5_c_re"][0]), blk(a["s5_c_im"][0]),
            a["s5_d"].reshape(1, S5_NB, 1, 128))


def _s5_unpack(g):
    glre, glim, glst, gbre, gbim, gcre, gcim, gd = g
    unb = lambda t: t.reshape(S5_GROUPS, S5_GROUP, S5_STATE).transpose(0, 2, 1)[None]
    return {
        "s5_lambda_re_f": glre[0].reshape(1, S5_GROUPS, S5_STATE), "s5_lambda_re_b": glre[1].reshape(1, S5_GROUPS, S5_STATE),
        "s5_lambda_im_f": glim[0].reshape(1, S5_GROUPS, S5_STATE), "s5_lambda_im_b": glim[1].reshape(1, S5_GROUPS, S5_STATE),
        "s5_log_step_f": glst[0].reshape(1, S5_GROUPS), "s5_log_step_b": glst[1].reshape(1, S5_GROUPS),
        "s5_b_re": unb(gbre), "s5_b_im": unb(gbim),
        "s5_c_re": gcre.reshape(1, S5_GROUPS, S5_GROUP, S5_STATE), "s5_c_im": gcim.reshape(1, S5_GROUPS, S5_GROUP, S5_STATE),
        "s5_d": gd.reshape(1, S5_WIDTH),
    }


def _local_step(a, wb, mx, mc, conv_w):
    x, ctx, tgt = a["x"][0], a["ctx"][0], a["loss_target"][0]
    l, lc = x.shape[0], ctx.shape[0]
    la = l + lc
    nct, ncc, nrc, cn = lc // TOK_TILE, lc // S5_T, lc // RET_CHUNK, la // S5_T
    n1w, n2w, fnw = a["norm1_w"], a["norm2_w"], a["final_norm_w"].reshape(1, D_MODEL)
    conv_b, b_glu = a["conv_b"], a["s5_b_glu"]
    ld2 = jnp.concatenate([a["ret_log_decay_f"], a["ret_log_decay_b"]], axis=0)
    mod4 = jnp.concatenate([mc[0:2], mx[0:2]], axis=0)
    mod3 = mx[2:5]
    gate5 = mx[5:6]
    cos_t, sin_t = _rope_tables(l, lc)
    s5p = _s5_pack(a)

    p_all, h1b = _norm_inproj(x, ctx, n1w, mod4, wb["w_in"])
    p3 = p_all.reshape(cn, S5_T, IN_COLS)
    kb, wst, wout, a16 = _s5_gen(*s5p)
    sloc = _s5_state(p3, wst)
    a16s = a16.transpose(1, 0, 2, 3).reshape(2, S5_GROUPS, 128)
    hs = _s5_scan(sloc, a16s, ncc)
    y_all = _s5_out(p3, kb, hs, wout).reshape(la, S5_WIDTH)
    s5x = _s5_glu(y_all, wb["s5_w_glu"], b_glu, nct)
    of, ob, ssf, ssb = _ret_scan(p_all, cos_t, sin_t, ld2, nrc)
    retx, y_ret = _ret_gate(of, ob, p_all, nct)
    x1, mix, h2b, up = _outproj_up(x, s5x, retx, wb["w_out"], mod3, n2w, wb["w_up"])
    act, dx2, ddn, acc_f = _ffn_loss(up, x1, conv_w, conv_b, wb["w_down"], gate5, fnw, tgt)

    g = {}
    dact = _mm(ddn, wb["w_down"], nt=True, name="dact")
    g["w_down"] = _mm_tn(act, ddn, name="gw_down")
    dup, acc_c = _convglu_bwd(up, dact, conv_w, conv_b)
    dh2 = _mm(dup, wb["w_up"], nt=True, name="dh2")
    g["w_up"] = _mm_tn(h2b, dup, name="gw_up")
    dx1, dmixb, acc_2 = _norm2_bwd(x1, dh2, dx2, mix, mod3, n2w)
    dmix = _mm(dmixb, wb["w_out"], nt=True, name="dmix")
    g["w_out"] = jnp.concatenate([_mm_tn(s5x, dmixb, name="gw_out_s5"), _mm_tn(retx, dmixb, name="gw_out_ret")], axis=0)

    dy_s5, g["s5_w_glu"], g["s5_b_glu"] = _s5_glu_bwd(y_all, dmix, wb["s5_w_glu"], b_glu, nct)
    dy3 = dy_s5.reshape(cn, S5_T, S5_WIDTH)
    e = _s5_bwd_h(dy3, wout)
    ds, da16 = _s5_scan_bwd(e, hs, a16s, ncc)
    du = _s5_bwd_u(dy3, kb, ds, wst).reshape(la, S5_WIDTH)
    dkb = _s5_bwd_kb(p3, dy3)
    dwst = _s5_bwd_w(p3, ds, "s5_bwd_wst")
    dwout = _s5_bwd_w(dy3, hs, "s5_bwd_wout")
    da16p = da16.reshape(2, S5_NB, 8, 128).transpose(1, 0, 2, 3)
    g.update(_s5_unpack(_s5_gen_bwd(*s5p, dkb, dwst, dwout, da16p)))

    dy_ret, dg = _ret_gate_bwd(y_ret, p_all, dmix, nct)
    dqf, dkf, dvf, dqb, dkb_, dvb, dld = _ret_scan_bwd(p_all, cos_t, sin_t, ld2, ssf, ssb, dy_ret, nrc)
    g["ret_log_decay_f"] = dld[0, :, 0, 0].reshape(1, RET_HEADS)
    g["ret_log_decay_b"] = dld[1, :, 0, 0].reshape(1, RET_HEADS)
    dp = _ret_qkv_grad(dqf, dkf, dvf, dqb, dkb_, dvb, du, dg, cos_t, sin_t)
    dh1 = _mm(dp, wb["w_in"], nt=True, name="dh1")
    g["w_in"] = _mm_tn(h1b, dp, name="gw_in")
    grad_x, acc_1 = _norm_inproj_bwd(x, ctx, n1w, mod4, dh1, dx1)

    g["norm1_w"], g["norm2_w"], g["final_norm_w"] = acc_1[0:1], acc_2[0:1], acc_f[0]
    g["conv_w"], g["conv_b"] = acc_c[0:3], acc_c[3:4]
    zero = jnp.zeros((1, D_MODEL), F32)
    dmx = jnp.concatenate([acc_1[3:5], acc_2[1:2], acc_2[2:4], acc_f[1:2]], axis=0)
    dmc = jnp.concatenate([acc_1[1:3], zero, zero, zero, zero], axis=0)
    return acc_f[2, 0], grad_x, g, dmx, dmc


WEIGHT_NAMES = ("c_ctx", "w_mod", "b_mod", "norm1_w", "w_in", "s5_lambda_re_f", "s5_lambda_im_f", "s5_log_step_f",
                "s5_lambda_re_b", "s5_lambda_im_b", "s5_log_step_b", "s5_b_re", "s5_b_im", "s5_c_re", "s5_c_im",
                "s5_d", "s5_w_glu", "s5_b_glu", "ret_log_decay_f", "ret_log_decay_b", "w_out", "norm2_w", "w_up",
                "conv_w", "conv_b", "w_down", "final_norm_w")
BIG_NAMES = ("w_in", "w_out", "w_up", "w_down")
SMALL_NAMES = ("norm1_w", "norm2_w", "final_norm_w", "conv_b", "conv_w", "s5_lambda_re_f", "s5_lambda_im_f",
               "s5_log_step_f", "s5_lambda_re_b", "s5_lambda_im_b", "s5_log_step_b", "s5_b_re", "s5_b_im", "s5_c_re",
               "s5_c_im", "s5_d", "s5_w_glu", "s5_b_glu", "ret_log_decay_f", "ret_log_decay_b")
ROW = 1024
N_CHIPS = 4


def _rows(t):
    return t.reshape(-1, ROW)


def _pack_rows(parts):
    flat = jnp.concatenate([p.reshape(-1) for p in parts])
    n = flat.shape[0]
    rows = -(-n // (8 * ROW)) * 8
    return jnp.pad(flat, (0, rows * ROW - n)).reshape(rows, ROW)


def _unpack_rows(packed, shapes):
    flat = packed.reshape(-1)
    out, off = [], 0
    for s in shapes:
        n = math.prod(s)
        out.append(flat[off:off + n].reshape(s))
        off += n
    return out


def _col_shards(t, n):
    r = t.shape[0]
    return t.reshape(r, n, -1).transpose(1, 0, 2)


def _by_core(ci, mine, other):
    return jnp.concatenate([jnp.where(ci == 0, mine, other), jnp.where(ci == 0, other, mine)], axis=-2)


def _step(a):
    xi, yi, ci = _mesh_pos()
    chip = 2 * xi + yi
    dev = 2 * chip + ci

    cw_loc = a["conv_w"].reshape(-1)
    small_in = jnp.concatenate([a["c"].reshape(-1), jnp.pad(cw_loc, (0, 24 * 128 - cw_loc.shape[0]))]).reshape(32, 128)
    sg = _all_gather8(small_in, "gather_cond").reshape(8, 32, 128)
    c_all = sg[:, 0:8].reshape(8, D_MODEL)
    conv_w = sg[0::2, 8:32].reshape(N_CHIPS, -1)[:, :cw_loc.shape[0]].reshape(N_CHIPS, 3, -1)
    conv_w = conv_w.transpose(1, 0, 2).reshape(3, D_FF)

    w_rows = [_rows(a["w_in"]), _rows(a["w_out"]), _rows(a["w_up"]), _rows(a["w_down"]), _rows(a["s5_w_glu"])]
    n_rows = [w.shape[0] for w in w_rows]
    packed = jnp.concatenate(w_rows, axis=0).astype(BF16)
    hr = packed.shape[0] // 2
    got = _chip_exchange(lax.dynamic_slice_in_dim(packed, ci * hr, hr, 0), "gather_w_chips", same_src=True)
    full = _by_core(ci, got, _sib_swap(got, "gather_w_sib"))
    offs = [0]
    for r in n_rows:
        offs.append(offs[-1] + r)
    part = lambda k: full[:, offs[k]:offs[k + 1]]
    wb = {
        "w_in": part(0).reshape(N_CHIPS, D_MODEL, -1).transpose(1, 0, 2).reshape(D_MODEL, IN_COLS),
        "w_out": part(1).reshape(D_MODEL, D_MODEL),
        "w_up": part(2).reshape(N_CHIPS, D_MODEL, -1).transpose(1, 0, 2).reshape(D_MODEL, 2 * D_FF),
        "w_down": part(3).reshape(D_FF, D_MODEL),
        "s5_w_glu": part(4).reshape(S5_WIDTH, S5_WIDTH),
    }

    w_mod_b = a["w_mod"][0].astype(BF16)
    c_ctx = a["c_ctx"].reshape(1, D_MODEL)
    b_loc = lax.dynamic_slice_in_dim(a["b_mod"], chip * MOD_COLS, MOD_COLS, 1)
    m_loc, s_b = _mod_fwd(c_all, c_ctx, w_mod_b, b_loc)
    mg = _all_gather8(m_loc, "gather_mod").reshape(8, MOD_ROWS, MOD_COLS)
    m_full = mg[0::2].transpose(1, 0, 2).reshape(MOD_ROWS, 6 * D_MODEL)
    mx = lax.dynamic_slice_in_dim(m_full, dev, 1, 0).reshape(6, D_MODEL)
    mc = m_full[8].reshape(6, D_MODEL)

    loss_part, grad_x, g, dmx, dmc = _local_step(a, wb, mx, mc, conv_w)
    loss = lax.psum(loss_part, ("x", "y", "c"))

    dm_pair = jnp.concatenate([dmx.reshape(1, -1), dmc.reshape(1, -1), jnp.zeros((6, 6 * D_MODEL), F32)], axis=0)
    dm_all = _all_gather8(dm_pair, "gather_dmod").reshape(8, 8, 6 * D_MODEL)
    dm16, gb_mod = _mod_bwd_sum(dm_all)
    dm_loc = lax.dynamic_slice_in_dim(dm16, chip * MOD_COLS, MOD_COLS, 1)
    gw_mod, gcc = _mod_bwd_w(dm_loc, s_b, c_ctx, w_mod_b)

    small_parts = [g[n] for n in SMALL_NAMES] + [gcc[0]]
    small_shapes = [p.shape for p in small_parts]
    sp = _pack_rows(small_parts)
    tot = _sum_slots(_all_gather8(sp, "gather_small_grads").reshape(8, sp.shape[0], ROW), "sum_small_grads")
    small = dict(zip(SMALL_NAMES + ("c_ctx",), _unpack_rows(tot, small_shapes)))
    grads = {n: small[n].reshape(a[n].shape) for n in SMALL_NAMES if n not in ("conv_w", "s5_w_glu")}
    grads["c_ctx"] = (0.5 * small["c_ctx"]).reshape(a["c_ctx"].shape)
    grads["conv_w"] = lax.dynamic_slice_in_dim(small["conv_w"], chip * (D_FF // N_CHIPS), D_FF // N_CHIPS, 1)[None]
    grads["s5_w_glu"] = lax.dynamic_slice_in_dim(small["s5_w_glu"], chip * (S5_WIDTH // N_CHIPS), S5_WIDTH // N_CHIPS, 0)[None]
    grads["b_mod"] = gb_mod
    grads["w_mod"] = gw_mod[None]

    gsh = jnp.concatenate([_col_shards(g["w_in"], N_CHIPS).reshape(N_CHIPS, -1, ROW),
                           g["w_out"].reshape(N_CHIPS, -1, ROW),
                           _col_shards(g["w_up"], N_CHIPS).reshape(N_CHIPS, -1, ROW),
                           g["w_down"].reshape(N_CHIPS, -1, ROW)], axis=1)
    gh = gsh.shape[1] // 2
    keep = lax.dynamic_slice_in_dim(gsh, ci * gh, gh, 1)
    send = lax.dynamic_slice_in_dim(gsh, (1 - ci) * gh, gh, 1)
    pair = _sum_slots(jnp.stack([keep, _sib_swap(send, "rs_sib")]).reshape(2, N_CHIPS * gh, ROW), "rs_pair_sum")
    red = _sum_slots(_chip_exchange(pair.reshape(N_CHIPS, gh, ROW), "rs_chips"), "rs_chip_sum")
    gbig = _by_core(ci, red, _sib_swap(red, "rs_back"))
    big_rows = [_rows(a[n]).shape[0] for n in BIG_NAMES]

    def split_big(t):
        out, off = {}, 0
        for n, r in zip(BIG_NAMES, big_rows):
            out[n] = t[off:off + r].reshape(a[n].shape)
            off += r
        return out

    grads.update(split_big(gbig))

    delta, new_m, new_v = {}, {}, {}
    pk = lambda pre: jnp.concatenate([_rows(a[pre + n]) for n in BIG_NAMES], axis=0)
    for dst, t in zip((delta, new_m, new_v), _adamw(pk(""), gbig, pk("m_"), pk("v_"), "adamw_big")):
        dst.update(split_big(t))
    for dst, t in zip((delta, new_m, new_v),
                      _adamw(a["w_mod"][0], gw_mod, a["m_w_mod"][0], a["v_w_mod"][0], "adamw_mod")):
        dst["w_mod"] = t[None]
    rest = [n for n in WEIGHT_NAMES if n not in BIG_NAMES and n != "w_mod"]
    shapes = [a[n].shape for n in rest]
    pr = lambda pre: _pack_rows([a[pre + n] for n in rest])
    for dst, t in zip((delta, new_m, new_v),
                      _adamw(pr(""), _pack_rows([grads[n] for n in rest]), pr("m_"), pr("v_"), "adamw_small")):
        dst.update(zip(rest, _unpack_rows(t, shapes)))

    return (loss, grad_x[None], *[grads[n] for n in WEIGHT_NAMES], *[delta[n] for n in WEIGHT_NAMES],
            *[new_m[n] for n in WEIGHT_NAMES], *[new_v[n] for n in WEIGHT_NAMES])


def kernel(x, c, ctx, c_ctx, w_mod, b_mod, norm1_w, w_in, s5_lambda_re_f, s5_lambda_im_f, s5_log_step_f, s5_lambda_re_b, s5_lambda_im_b, s5_log_step_b, s5_b_re, s5_b_im, s5_c_re, s5_c_im, s5_d, s5_w_glu, s5_b_glu, ret_log_decay_f, ret_log_decay_b, w_out, norm2_w, w_up, conv_w, conv_b, w_down, final_norm_w, loss_target, m_c_ctx, m_w_mod, m_b_mod, m_norm1_w, m_w_in, m_s5_lambda_re_f, m_s5_lambda_im_f, m_s5_log_step_f, m_s5_lambda_re_b, m_s5_lambda_im_b, m_s5_log_step_b, m_s5_b_re, m_s5_b_im, m_s5_c_re, m_s5_c_im, m_s5_d, m_s5_w_glu, m_s5_b_glu, m_ret_log_decay_f, m_ret_log_decay_b, m_w_out, m_norm2_w, m_w_up, m_conv_w, m_conv_b, m_w_down, m_final_norm_w, v_c_ctx, v_w_mod, v_b_mod, v_norm1_w, v_w_in, v_s5_lambda_re_f, v_s5_lambda_im_f, v_s5_log_step_f, v_s5_lambda_re_b, v_s5_lambda_im_b, v_s5_log_step_b, v_s5_b_re, v_s5_b_im, v_s5_c_re, v_s5_c_im, v_s5_d, v_s5_w_glu, v_s5_b_glu, v_ret_log_decay_f, v_ret_log_decay_b, v_w_out, v_norm2_w, v_w_up, v_conv_w, v_conv_b, v_w_down, v_final_norm_w):
    return _step(dict(locals()))
```

```python
import functools
import math

import jax
import jax.numpy as jnp
from jax import lax
from jax.experimental import pallas as pl
from jax.experimental.pallas import tpu as pltpu

F32 = jnp.float32
BF16 = jnp.bfloat16

D_MODEL = 1024
S5_WIDTH = 512
S5_GROUPS = 32
S5_GROUP = 16
S5_STATE = 64
RET_WIDTH = 512
RET_HEADS = 4
RET_DH = 128
RET_CHUNK = 128
GRID_W = 64
ROPE_THETA = 10000.0
D_FF = 2816
NORM_EPS = 1e-6
IN_COLS = S5_WIDTH + 4 * RET_WIDTH

S5_T = 16
S5_NB = 4
S5_BW = S5_T * 128
S5_SW = 8 * 2 * S5_STATE

ADAM_LR, ADAM_B1, ADAM_B2, ADAM_EPS, ADAM_WD, ADAM_STEP = 0.001, 0.9, 0.999, 1e-08, 0.01, 10

VMEM_LIMIT = 56 * 1024 * 1024
MESH_ID = pl.DeviceIdType.MESH


def _params(sem=None):
    return pltpu.CompilerParams(dimension_semantics=sem, vmem_limit_bytes=VMEM_LIMIT)


def _full(shape):
    n = len(shape)
    return pl.BlockSpec(shape, lambda *_: (0,) * n)


def _dot(a, b):
    return jnp.dot(a, b, preferred_element_type=F32)


def _dot_nt(a, b):
    return lax.dot_general(a, b, (((1,), (1,)), ((), ())), preferred_element_type=F32)


def _dot_tn(a, b):
    return lax.dot_general(a, b, (((0,), (0,)), ((), ())), preferred_element_type=F32)


def _dot_hi(a, b):
    return jnp.dot(a, b, preferred_element_type=F32, precision=lax.Precision.HIGHEST)


def _dot_nt_hi(a, b):
    return lax.dot_general(a, b, (((1,), (1,)), ((), ())), preferred_element_type=F32,
                           precision=lax.Precision.HIGHEST)


def _gelu(x):
    return 0.5 * x * (1.0 + jnp.tanh(0.7978845608028654 * (x + 0.044715 * (x * x * x))))


def _sigmoid(x):
    return 1.0 / (1.0 + jnp.exp(-x))


def _silu(x):
    return x * _sigmoid(x)


def _rms_mod(x, nw, sh, sc):
    r = lax.rsqrt(jnp.mean(x * x, axis=-1, keepdims=True) + NORM_EPS)
    return (x * r * nw) * (1.0 + sc) + sh


def _rms(x, nw):
    r = lax.rsqrt(jnp.mean(x * x, axis=-1, keepdims=True) + NORM_EPS)
    return x * r * nw


def _head_norm_gate(y, g):
    mu = jnp.mean(y, axis=-1, keepdims=True)
    yc = y - mu
    var = jnp.mean(yc * yc, axis=-1, keepdims=True)
    return _silu(g) * (yc * lax.rsqrt(var + NORM_EPS))


def _swap_pairs(t):
    lane = lax.broadcasted_iota(jnp.int32, t.shape, 1)
    return jnp.where(lane % 2 == 0, pltpu.roll(t, RET_DH - 1, 1), pltpu.roll(t, 1, 1))


def _rope(t, cos_t, sin_t):
    return t * cos_t + _swap_pairs(t) * sin_t


def _rope_t(dt, cos_t, sin_t):
    return dt * cos_t + _swap_pairs(dt * sin_t)


def _pick(n, prefs):
    for p in prefs:
        if n % p == 0:
            return p
    return n


def _mm(a, w, *, nt=False, out_dtype=F32, name):
    m, k = a.shape
    n = w.shape[0] if nt else w.shape[1]
    tm = _pick(m, (512, 256, 128))
    tn = _pick(n, (1408, 1024, 1280, 512))

    def body(a_ref, w_ref, o_ref):
        f = _dot_nt if nt else _dot
        o_ref[...] = f(a_ref[...], w_ref[...]).astype(out_dtype)

    w_spec = pl.BlockSpec((tn, k), lambda j, i: (j, 0)) if nt else pl.BlockSpec((k, tn), lambda j, i: (0, j))
    return pl.pallas_call(
        body, name=name, grid=(n // tn, m // tm),
        in_specs=[pl.BlockSpec((tm, k), lambda j, i: (i, 0)), w_spec],
        out_specs=pl.BlockSpec((tm, tn), lambda j, i: (i, j)),
        out_shape=jax.ShapeDtypeStruct((m, n), out_dtype),
        compiler_params=_params(("parallel", "parallel")),
    )(a, w)


def _mm_tn(a, b, *, name):
    m, k = a.shape
    n = b.shape[1]
    tm = _pick(m, (512, 256, 128))
    tn = _pick(n, (1408, 1024, 1280, 512))

    def body(a_ref, b_ref, o_ref):
        @pl.when(pl.program_id(1) == 0)
        def _():
            o_ref[...] = jnp.zeros_like(o_ref)
        o_ref[...] += _dot_tn(a_ref[...], b_ref[...])

    return pl.pallas_call(
        body, name=name, grid=(n // tn, m // tm),
        in_specs=[pl.BlockSpec((tm, k), lambda j, i: (i, 0)), pl.BlockSpec((tm, tn), lambda j, i: (i, j))],
        out_specs=pl.BlockSpec((k, tn), lambda j, i: (0, j)),
        out_shape=jax.ShapeDtypeStruct((k, n), F32),
        compiler_params=_params(("parallel", "arbitrary")),
    )(a, b)


TOK_TILE = 256


def _norm_inproj(x, ctx, n1w, mod4, w_in_b):
    l, lc = x.shape[0], ctx.shape[0]
    tm = TOK_TILE
    nct = lc // tm
    la = l + lc

    def body(x_ref, c_ref, nw_ref, mod_ref, w_ref, p_ref, h_ref):
        is_ctx = pl.program_id(0) < nct
        xt = jnp.where(is_ctx, c_ref[...], x_ref[...])
        sh = jnp.where(is_ctx, mod_ref[0:1, :], mod_ref[2:3, :])
        sc = jnp.where(is_ctx, mod_ref[1:2, :], mod_ref[3:4, :])
        hb = _rms_mod(xt, nw_ref[...], sh, sc).astype(BF16)
        h_ref[...] = hb
        p_ref[...] = _dot(hb, w_ref[...])

    return pl.pallas_call(
        body, name="norm_inproj", grid=(la // tm,),
        in_specs=[pl.BlockSpec((tm, D_MODEL), lambda i: (jnp.maximum(i - nct, 0), 0)),
                  pl.BlockSpec((tm, D_MODEL), lambda i: (jnp.minimum(i, nct - 1), 0)),
                  _full((1, D_MODEL)), _full((4, D_MODEL)), _full((D_MODEL, IN_COLS))],
        out_specs=[pl.BlockSpec((tm, IN_COLS), lambda i: (i, 0)), pl.BlockSpec((tm, D_MODEL), lambda i: (i, 0))],
        out_shape=[jax.ShapeDtypeStruct((la, IN_COLS), F32), jax.ShapeDtypeStruct((la, D_MODEL), BF16)],
        compiler_params=_params(("parallel",)),
    )(x, ctx, n1w, mod4, w_in_b)


def _norm_inproj_bwd(x, ctx, n1w, mod4, dh1, dx1):
    l, lc = x.shape[0], ctx.shape[0]
    tm = TOK_TILE
    nct = lc // tm
    la = l + lc

    def body(x_ref, c_ref, nw_ref, mod_ref, dh_ref, dx1_ref, gx_ref, acc_ref):
        i = pl.program_id(0)
        is_ctx = i < nct

        @pl.when(i == 0)
        def _():
            acc_ref[...] = jnp.zeros_like(acc_ref)

        xt = jnp.where(is_ctx, c_ref[...], x_ref[...])
        sh = jnp.where(is_ctx, mod_ref[0:1, :], mod_ref[2:3, :])
        sc = jnp.where(is_ctx, mod_ref[1:2, :], mod_ref[3:4, :])
        _, vjp = jax.vjp(_rms_mod, xt, nw_ref[...], sh, sc)
        dx, dnw, dsh, dsc = vjp(dh_ref[...])
        gx_ref[...] = dx + dx1_ref[...]
        cf = jnp.where(is_ctx, 1.0, 0.0)
        acc_ref[0:1, :] += dnw
        acc_ref[1:2, :] += cf * dsh
        acc_ref[2:3, :] += cf * dsc
        acc_ref[3:4, :] += (1.0 - cf) * dsh
        acc_ref[4:5, :] += (1.0 - cf) * dsc

    return pl.pallas_call(
        body, name="norm_inproj_bwd", grid=(la // tm,),
        in_specs=[pl.BlockSpec((tm, D_MODEL), lambda i: (jnp.maximum(i - nct, 0), 0)),
                  pl.BlockSpec((tm, D_MODEL), lambda i: (jnp.minimum(i, nct - 1), 0)),
                  _full((1, D_MODEL)), _full((4, D_MODEL)),
                  pl.BlockSpec((tm, D_MODEL), lambda i: (i, 0)),
                  pl.BlockSpec((tm, D_MODEL), lambda i: (jnp.maximum(i - nct, 0), 0))],
        out_specs=[pl.BlockSpec((tm, D_MODEL), lambda i: (jnp.maximum(i - nct, 0), 0)), _full((8, D_MODEL))],
        out_shape=[jax.ShapeDtypeStruct((l, D_MODEL), F32), jax.ShapeDtypeStruct((8, D_MODEL), F32)],
        compiler_params=_params(("arbitrary",)),
    )(x, ctx, n1w, mod4, dh1, dx1)


def _iota2(shape, dim):
    return lax.broadcasted_iota(jnp.int32, shape, dim)


def _group_mask(rows, cols, row_div, col_div):
    return jnp.where(_iota2((rows, cols), 0) // row_div == _iota2((rows, cols), 1) // col_div, 1.0, 0.0).astype(F32)


def _s5_gen_dir(lre, lim, lst, b_re, b_im, c_re, c_im):
    step = jnp.exp(lst)
    mag = jnp.exp(lre * step)
    ar = mag * jnp.cos(lim * step)
    ai = mag * jnp.sin(lim * step)
    den = lre * lre + lim * lim
    xr = ar - 1.0
    cr = (xr * lre + ai * lim) / den
    ci = (ai * lre - xr * lim) / den
    rexp = _group_mask(128, 8, S5_GROUP, 1)
    are, aie = _dot_hi(rexp, ar), _dot_hi(rexp, ai)
    cre, cie = _dot_hi(rexp, cr), _dot_hi(rexp, ci)
    bbr = cre * b_re - cie * b_im
    bbi = cre * b_im + cie * b_re
    gmask = _group_mask(128, 128, S5_GROUP, S5_GROUP)
    pr, pi = jnp.ones_like(are), jnp.zeros_like(are)
    xs, ys = [], []
    for t in range(S5_T + 1):
        if t < S5_T:
            xs.append(jnp.concatenate([bbr * pr - bbi * pi, bbr * pi + bbi * pr], axis=1))
        ys.append(jnp.concatenate([c_re * pr - c_im * pi, -(c_re * pi + c_im * pr)], axis=1))
        pr, pi = pr * are - pi * aie, pr * aie + pi * are
    gs = [_dot_nt_hi(x_t, ys[0]) * gmask for x_t in xs]
    r16, i16 = ar, ai
    for _ in range(4):
        r16, i16 = r16 * r16 - i16 * i16, 2.0 * r16 * i16
    return xs, ys, gs, jnp.concatenate([r16, i16], axis=1)


def _s5_expand(z):
    return jnp.concatenate([z] * 8, axis=1) * _group_mask(128, S5_SW, S5_GROUP, 128)


def _s5_contract(z):
    zm = z * _group_mask(128, S5_SW, S5_GROUP, 128)
    acc = zm[:, 0:128]
    for k in range(1, 8):
        acc = acc + zm[:, 128 * k:128 * (k + 1)]
    return acc


def _s5_param_specs():
    blk3 = lambda r, c: pl.BlockSpec((1, 1, r, c), lambda b, j: (0, b, 0, 0))
    dir3 = lambda r, c: pl.BlockSpec((2, 1, r, c), lambda b, j: (0, b, 0, 0))
    return [dir3(8, S5_STATE), dir3(8, S5_STATE), dir3(8, 1), blk3(128, S5_STATE), blk3(128, S5_STATE),
            blk3(128, S5_STATE), blk3(128, S5_STATE), blk3(1, 128)]


def _s5_gen(lre, lim, lst, b_re, b_im, c_re, c_im, dvec):
    def body(lre_ref, lim_ref, lst_ref, bre_ref, bim_ref, cre_ref, cim_ref, d_ref,
             kb_ref, wst_ref, wout_ref, a16_ref, x_scr, y_scr, g_scr):
        j = pl.program_id(1)

        @pl.when(j == 0)
        def _():
            eye = _group_mask(128, 128, 1, 1)
            g0 = eye * d_ref[0, 0]
            for dr in range(2):
                xs, ys, gs, a16 = _s5_gen_dir(lre_ref[dr, 0], lim_ref[dr, 0], lst_ref[dr, 0], bre_ref[0, 0],
                                              bim_ref[0, 0], cre_ref[0, 0], cim_ref[0, 0])
                a16_ref[0, dr] = a16
                for t in range(S5_T):
                    x_scr[dr, t] = xs[t]
                for t in range(S5_T + 1):
                    y_scr[dr, t] = ys[t]
                g0 = g0 + gs[0]
                for t in range(1, S5_T):
                    g_scr[(S5_T - 1) + t if dr == 0 else (S5_T - 1) - t] = gs[t]
            g_scr[S5_T - 1] = g0

        for i in range(S5_T):
            kb_ref[0, :, 128 * i:128 * (i + 1)] = g_scr[i - j + (S5_T - 1)].astype(BF16)
        wst_ref[0, 0] = _s5_expand(x_scr[0, S5_T - 1 - j]).astype(BF16)
        wst_ref[0, 1] = _s5_expand(x_scr[1, j]).astype(BF16)
        wout_ref[0, 0] = _s5_expand(y_scr[0, j + 1]).astype(BF16)
        wout_ref[0, 1] = _s5_expand(y_scr[1, S5_T - j]).astype(BF16)

    return pl.pallas_call(
        body, name="s5_gen", grid=(S5_NB, S5_T),
        in_specs=_s5_param_specs(),
        out_specs=[pl.BlockSpec((1, 128, S5_BW), lambda b, j: (b, j, 0)),
                   pl.BlockSpec((1, 2, 128, S5_SW), lambda b, j: (b, 0, j, 0)),
                   pl.BlockSpec((1, 2, 128, S5_SW), lambda b, j: (b, 0, j, 0)),
                   pl.BlockSpec((1, 2, 8, 128), lambda b, j: (b, 0, 0, 0))],
        out_shape=[jax.ShapeDtypeStruct((S5_NB, S5_BW, S5_BW), BF16),
                   jax.ShapeDtypeStruct((S5_NB, 2, S5_BW, S5_SW), BF16),
                   jax.ShapeDtypeStruct((S5_NB, 2, S5_BW, S5_SW), BF16),
                   jax.ShapeDtypeStruct((S5_NB, 2, 8, 128), F32)],
        scratch_shapes=[pltpu.VMEM((2, S5_T, 128, 128), F32), pltpu.VMEM((2, S5_T + 1, 128, 128), F32),
                        pltpu.VMEM((2 * S5_T - 1, 128, 128), F32)],
        compiler_params=_params(("parallel", "arbitrary")),
    )(lre, lim, lst, b_re, b_im, c_re, c_im, dvec)


def _s5_gen_bwd(lre, lim, lst, b_re, b_im, c_re, c_im, dvec, dkb, dwst, dwout, da16):
    def body(lre_ref, lim_ref, lst_ref, bre_ref, bim_ref, cre_ref, cim_ref, d_ref,
             dkb_ref, dwst_ref, dwout_ref, da16_ref,
             glre_ref, glim_ref, glst_ref, gbre_ref, gbim_ref, gcre_ref, gcim_ref, gd_ref,
             dx_scr, dy_scr, dg_scr):
        j = pl.program_id(1)

        @pl.when(j == 0)
        def _():
            dg_scr[...] = jnp.zeros_like(dg_scr)
            dy_scr[0, 0] = jnp.zeros((128, 128), F32)
            dy_scr[1, 0] = jnp.zeros((128, 128), F32)

        for i in range(S5_T):
            dg_scr[i - j + (S5_T - 1)] += dkb_ref[0, :, 128 * i:128 * (i + 1)]
        dx_scr[0, S5_T - 1 - j] = _s5_contract(dwst_ref[0, 0])
        dx_scr[1, j] = _s5_contract(dwst_ref[0, 1])
        dy_scr[0, j + 1] = _s5_contract(dwout_ref[0, 0])
        dy_scr[1, S5_T - j] = _s5_contract(dwout_ref[0, 1])

        @pl.when(j == S5_T - 1)
        def _():
            eye = _group_mask(128, 128, 1, 1)
            gd_ref[0, 0] = jnp.sum(dg_scr[S5_T - 1] * eye, axis=0, keepdims=True)
            gb = [None, None, None, None]
            for dr in range(2):
                args = (lre_ref[dr, 0], lim_ref[dr, 0], lst_ref[dr, 0], bre_ref[0, 0], bim_ref[0, 0],
                        cre_ref[0, 0], cim_ref[0, 0])
                _, vjp = jax.vjp(_s5_gen_dir, *args)
                dxs = [dx_scr[dr, t] for t in range(S5_T)]
                dys = [dy_scr[dr, t] for t in range(S5_T + 1)]
                dgs = [dg_scr[(S5_T - 1) + t if dr == 0 else (S5_T - 1) - t] for t in range(S5_T)]
                g = vjp((dxs, dys, dgs, da16_ref[0, dr]))
                glre_ref[dr, 0] = g[0]
                glim_ref[dr, 0] = g[1]
                glst_ref[dr, 0] = g[2]
                for q in range(4):
                    gb[q] = g[3 + q] if gb[q] is None else gb[q] + g[3 + q]
            gbre_ref[0, 0] = gb[0]
            gbim_ref[0, 0] = gb[1]
            gcre_ref[0, 0] = gb[2]
            gcim_ref[0, 0] = gb[3]

    shp = lambda a: jax.ShapeDtypeStruct(a.shape, F32)
    return pl.pallas_call(
        body, name="s5_gen_bwd", grid=(S5_NB, S5_T),
        in_specs=_s5_param_specs() + [
            pl.BlockSpec((1, 128, S5_BW), lambda b, j: (b, j, 0)),
            pl.BlockSpec((1, 2, 128, S5_SW), lambda b, j: (b, 0, j, 0)),
            pl.BlockSpec((1, 2, 128, S5_SW), lambda b, j: (b, 0, j, 0)),
            pl.BlockSpec((1, 2, 8, 128), lambda b, j: (b, 0, 0, 0))],
        out_specs=_s5_param_specs(),
        out_shape=[shp(lre), shp(lim), shp(lst), shp(b_re), shp(b_im), shp(c_re), shp(c_im), shp(dvec)],
        scratch_shapes=[pltpu.VMEM((2, S5_T, 128, 128), F32), pltpu.VMEM((2, S5_T + 1, 128, 128), F32),
                        pltpu.VMEM((2 * S5_T - 1, 128, 128), F32)],
        compiler_params=_params(("parallel", "arbitrary")),
    )(lre, lim, lst, b_re, b_im, c_re, c_im, dvec, dkb, dwst, dwout, da16)


def _s5_ucat(u_ref, lo=0, hi=S5_T):
    return jnp.concatenate([u_ref[:, j, :] for j in range(lo, hi)], axis=1).astype(BF16)


def _s5_put_groups(o_ref, dr, val):
    for gi in range(8):
        o_ref[dr, :, gi, :] = val[:, 128 * gi:128 * (gi + 1)]


def _s5_get_groups(s_ref, dr, n=8):
    return jnp.concatenate([s_ref[dr, :, gi, :] for gi in range(n)], axis=1).astype(BF16)


def _s5_state(p3, wst):
    cn = p3.shape[0]

    def body(u_ref, w_ref, o_ref):
        u = _s5_ucat(u_ref)
        _s5_put_groups(o_ref, 0, _dot(u, w_ref[0, 0]))
        _s5_put_groups(o_ref, 1, _dot(u, w_ref[0, 1]))

    return pl.pallas_call(
        body, name="s5_state", grid=(S5_NB,),
        in_specs=[pl.BlockSpec((cn, S5_T, 128), lambda b: (0, 0, b)),
                  pl.BlockSpec((1, 2, S5_BW, S5_SW), lambda b: (b, 0, 0, 0))],
        out_specs=pl.BlockSpec((2, cn, 8, 128), lambda b: (0, 0, b, 0)),
        out_shape=jax.ShapeDtypeStruct((2, cn, S5_GROUPS, 128), F32),
        compiler_params=_params(("parallel",)),
    )(p3, wst)


def _s5_a_forms(a):
    ra = pltpu.roll(a, S5_STATE, 1)
    low = _iota2(a.shape, 1) < S5_STATE
    return jnp.where(low, a, ra), jnp.where(low, -ra, a)


def _s5_scan(sloc, a16, ncc):
    cn = sloc.shape[1]

    def body(s_ref, a_ref, h_ref):
        forms = [_s5_a_forms(a_ref[dr]) for dr in range(2)]

        def step(s, hs):
            out = []
            for dr in range(2):
                arr, aii = forms[dr]
                c = s if dr == 0 else jnp.where(s < ncc, ncc - 1 - s, cn - 1 - (s - ncc))
                h_ref[dr, c] = hs[dr]
                out.append(hs[dr] * arr + pltpu.roll(hs[dr], S5_STATE, 1) * aii + s_ref[dr, c])
            return tuple(out)

        zero = jnp.zeros((S5_GROUPS, 128), F32)
        lax.fori_loop(0, cn, step, (zero, zero), unroll=4)

    return pl.pallas_call(
        body, name="s5_scan",
        out_shape=jax.ShapeDtypeStruct(sloc.shape, F32),
        compiler_params=_params(),
    )(sloc, a16)


def _s5_scan_bwd(e, hs, a16, ncc):
    cn = e.shape[1]

    def body(e_ref, h_ref, a_ref, ds_ref, da_ref):
        forms = [_s5_a_forms(a_ref[dr]) for dr in range(2)]
        low = _iota2((S5_GROUPS, 128), 1) < S5_STATE

        def step(s, carry):
            out = []
            r = cn - 1 - s
            for dr in range(2):
                arr, aii = forms[dr]
                g, da = carry[dr]
                c = r if dr == 0 else jnp.where(r < ncc, ncc - 1 - r, cn - 1 - (r - ncc))
                ds_ref[dr, c] = g
                h = h_ref[dr, c]
                gh = g * h
                grh = g * pltpu.roll(h, S5_STATE, 1)
                da = da + jnp.where(low, gh + pltpu.roll(gh, S5_STATE, 1), grh - pltpu.roll(grh, S5_STATE, 1))
                g = e_ref[dr, c] + g * arr - pltpu.roll(g, S5_STATE, 1) * aii
                out.append((g, da))
            return tuple(out)

        zero = jnp.zeros((S5_GROUPS, 128), F32)
        res = lax.fori_loop(0, cn, step, ((zero, zero), (zero, zero)), unroll=4)
        da_ref[0] = res[0][1]
        da_ref[1] = res[1][1]

    return pl.pallas_call(
        body, name="s5_scan_bwd",
        out_shape=[jax.ShapeDtypeStruct(e.shape, F32), jax.ShapeDtypeStruct((2, S5_GROUPS, 128), F32)],
        compiler_params=_params(),
    )(e, hs, a16)


def _s5_out(p3, kb, h2, wout):
    cn = p3.shape[0]
    half = S5_T // 2

    def body(u_ref, k_ref, h_ref, w_ref, y_ref):
        u = _s5_ucat(u_ref)
        y = _dot(u, k_ref[0])
        y = y + _dot_nt(_s5_get_groups(h_ref, 0), w_ref[0, 0])
        y = y + _dot_nt(_s5_get_groups(h_ref, 1), w_ref[0, 1])
        for i in range(half):
            y_ref[:, i, :] = y[:, 128 * i:128 * (i + 1)]

    return pl.pallas_call(
        body, name="s5_out", grid=(S5_NB, 2),
        in_specs=[pl.BlockSpec((cn, S5_T, 128), lambda b, q: (0, 0, b)),
                  pl.BlockSpec((1, S5_BW, S5_BW // 2), lambda b, q: (b, 0, q)),
                  pl.BlockSpec((2, cn, 8, 128), lambda b, q: (0, 0, b, 0)),
                  pl.BlockSpec((1, 2, S5_BW // 2, S5_SW), lambda b, q: (b, 0, q, 0))],
        out_specs=pl.BlockSpec((cn, half, 128), lambda b, q: (0, q, b)),
        out_shape=jax.ShapeDtypeStruct((cn, S5_T, S5_WIDTH), F32),
        compiler_params=_params(("parallel", "parallel")),
    )(p3, kb, h2, wout)


def _s5_bwd_h(dy3, wout):
    cn = dy3.shape[0]

    def body(d_ref, w_ref, e_ref):
        d = _s5_ucat(d_ref)
        _s5_put_groups(e_ref, 0, _dot(d, w_ref[0, 0]))
        _s5_put_groups(e_ref, 1, _dot(d, w_ref[0, 1]))

    return pl.pallas_call(
        body, name="s5_bwd_h", grid=(S5_NB,),
        in_specs=[pl.BlockSpec((cn, S5_T, 128), lambda b: (0, 0, b)),
                  pl.BlockSpec((1, 2, S5_BW, S5_SW), lambda b: (b, 0, 0, 0))],
        out_specs=pl.BlockSpec((2, cn, 8, 128), lambda b: (0, 0, b, 0)),
        out_shape=jax.ShapeDtypeStruct((2, cn, S5_GROUPS, 128), F32),
        compiler_params=_params(("parallel",)),
    )(dy3, wout)


def _s5_bwd_u(dy3, kb, ds2, wst):
    cn = dy3.shape[0]
    half = S5_T // 2

    def body(d_ref, k_ref, s_ref, w_ref, o_ref):
        d = _s5_ucat(d_ref)
        du = _dot_nt(d, k_ref[0])
        du = du + _dot_nt(_s5_get_groups(s_ref, 0), w_ref[0, 0])
        du = du + _dot_nt(_s5_get_groups(s_ref, 1), w_ref[0, 1])
        for j in range(half):
            o_ref[:, j, :] = du[:, 128 * j:128 * (j + 1)]

    return pl.pallas_call(
        body, name="s5_bwd_u", grid=(S5_NB, 2),
        in_specs=[pl.BlockSpec((cn, S5_T, 128), lambda b, q: (0, 0, b)),
                  pl.BlockSpec((1, S5_BW // 2, S5_BW), lambda b, q: (b, q, 0)),
                  pl.BlockSpec((2, cn, 8, 128), lambda b, q: (0, 0, b, 0)),
                  pl.BlockSpec((1, 2, S5_BW // 2, S5_SW), lambda b, q: (b, 0, q, 0))],
        out_specs=pl.BlockSpec((cn, half, 128), lambda b, q: (0, q, b)),
        out_shape=jax.ShapeDtypeStruct((cn, S5_T, S5_WIDTH), F32),
        compiler_params=_params(("parallel", "parallel")),
    )(dy3, kb, ds2, wst)


def _s5_bwd_kb(p3, dy3):
    cn = p3.shape[0]
    half = S5_T // 2

    def body(u_ref, d_ref, o_ref):
        o_ref[0] = _dot_tn(_s5_ucat(u_ref), _s5_ucat(d_ref, 0, half))

    return pl.pallas_call(
        body, name="s5_bwd_kb", grid=(S5_NB, 2),
        in_specs=[pl.BlockSpec((cn, S5_T, 128), lambda b, q: (0, 0, b)),
                  pl.BlockSpec((cn, half, 128), lambda b, q: (0, q, b))],
        out_specs=pl.BlockSpec((1, S5_BW, S5_BW // 2), lambda b, q: (b, 0, q)),
        out_shape=jax.ShapeDtypeStruct((S5_NB, S5_BW, S5_BW), F32),
        compiler_params=_params(("parallel", "parallel")),
    )(p3, dy3)


def _s5_bwd_w(u3, st, name):
    cn = u3.shape[0]

    def body(u_ref, s_ref, w_ref):
        w_ref[0, 0] = _dot_tn(_s5_ucat(u_ref), _s5_get_groups(s_ref, 0))

    return pl.pallas_call(
        body, name=name, grid=(S5_NB, 2),
        in_specs=[pl.BlockSpec((cn, S5_T, 128), lambda b, q: (0, 0, b)),
                  pl.BlockSpec((1, cn, 8, 128), lambda b, q: (q, 0, b, 0))],
        out_specs=pl.BlockSpec((1, 1, S5_BW, S5_SW), lambda b, q: (b, q, 0, 0)),
        out_shape=jax.ShapeDtypeStruct((S5_NB, 2, S5_BW, S5_SW), F32),
        compiler_params=_params(("parallel", "parallel")),
    )(u3, st)


def _s5_glu(y_all, w_glu_b, b_glu, nct):
    la = y_all.shape[0]
    tm = TOK_TILE
    l = la - nct * tm

    def body(y_ref, w_ref, b_ref, o_ref):
        yg = _gelu(y_ref[...])
        z = _dot(yg.astype(BF16), w_ref[...]) + b_ref[...]
        o_ref[...] = (yg * _sigmoid(z)).astype(BF16)

    return pl.pallas_call(
        body, name="s5_glu", grid=(l // tm,),
        in_specs=[pl.BlockSpec((tm, S5_WIDTH), lambda i: (i + nct, 0)),
                  _full((S5_WIDTH, S5_WIDTH)), _full((1, S5_WIDTH))],
        out_specs=pl.BlockSpec((tm, S5_WIDTH), lambda i: (i, 0)),
        out_shape=jax.ShapeDtypeStruct((l, S5_WIDTH), BF16),
        compiler_params=_params(("parallel",)),
    )(y_all, w_glu_b, b_glu)


def _s5_glu_bwd(y_all, dmix, w_glu_b, b_glu, nct):
    la = y_all.shape[0]
    tm = TOK_TILE

    def body(y_ref, d_ref, w_ref, b_ref, dy_ref, gw_ref, gb_ref):
        i = pl.program_id(0)

        @pl.when(i == 0)
        def _():
            gw_ref[...] = jnp.zeros_like(gw_ref)
            gb_ref[...] = jnp.zeros_like(gb_ref)

        @pl.when(i < nct)
        def _():
            dy_ref[...] = jnp.zeros_like(dy_ref)

        @pl.when(i >= nct)
        def _():
            y = y_ref[...]
            yg, gelu_vjp = jax.vjp(_gelu, y)
            ygb = yg.astype(BF16)
            sg = _sigmoid(_dot(ygb, w_ref[...]) + b_ref[...])
            ds = d_ref[...]
            dz = ds * yg * sg * (1.0 - sg)
            dzb = dz.astype(BF16)
            dyg = ds * sg + _dot_nt(dzb, w_ref[...])
            dy_ref[...] = gelu_vjp(dyg)[0]
            gw_ref[...] += _dot_tn(ygb, dzb)
            gb_ref[...] += jnp.sum(dz, axis=0, keepdims=True)

    return pl.pallas_call(
        body, name="s5_glu_bwd", grid=(la // tm,),
        in_specs=[pl.BlockSpec((tm, S5_WIDTH), lambda i: (i, 0)),
                  pl.BlockSpec((tm, S5_WIDTH), lambda i: (jnp.maximum(i - nct, 0), 0)),
                  _full((S5_WIDTH, S5_WIDTH)), _full((1, S5_WIDTH))],
        out_specs=[pl.BlockSpec((tm, S5_WIDTH), lambda i: (i, 0)), _full((S5_WIDTH, S5_WIDTH)),
                   _full((1, S5_WIDTH))],
        out_shape=[jax.ShapeDtypeStruct((la, S5_WIDTH), F32), jax.ShapeDtypeStruct((S5_WIDTH, S5_WIDTH), F32),
                   jax.ShapeDtypeStruct((1, S5_WIDTH), F32)],
        compiler_params=_params(("arbitrary",)),
    )(y_all, dmix, w_glu_b, b_glu)


K_SCALE = RET_DH ** -0.5
Q_COL, K_COL, V_COL, G_COL = 4, 8, 12, 16


def _ret_chunk_of(step, ncc, nch, rev):
    if not rev:
        return step
    return jnp.where(step < ncc, ncc - 1 - step, nch - 1 - (step - ncc))


def _ret_decay(ld, rev):
    c = _iota2((RET_CHUNK, RET_CHUNK), 0).astype(F32)
    m = _iota2((RET_CHUNK, RET_CHUNK), 1).astype(F32)
    diff = (m - c) if rev else (c - m)
    keep = (diff > 0) if rev else (diff >= 0)
    expo = jnp.maximum(diff, 0.0)
    dm = jnp.where(keep, jnp.exp(ld * expo), 0.0)
    pos = _iota2((RET_CHUNK, 1), 0).astype(F32)
    xi_e = (RET_CHUNK - pos) if rev else (pos + 1.0)
    zeta_e = pos if rev else (RET_CHUNK - 1.0 - pos)
    return dm, expo, jnp.exp(ld * xi_e), xi_e, jnp.exp(ld * zeta_e), zeta_e


def _ret_specs(nch, ncc, rev, step_of):
    chunk = lambda n: _ret_chunk_of(step_of(n), ncc, nch, rev)
    cols = [pl.BlockSpec((RET_CHUNK, RET_WIDTH), functools.partial(lambda n, cb: (chunk(n), cb), cb=cb))
            for cb in (1, 2, 3)]
    tab = pl.BlockSpec((RET_CHUNK, RET_DH), lambda n: (chunk(n), 0))
    return cols + [tab, tab], pl.BlockSpec((RET_CHUNK, RET_WIDTH), lambda n: (chunk(n), 0))


def _ret_scan(p_all, cos_t, sin_t, ld2, ncc):
    la = p_all.shape[0]
    nch = la // RET_CHUNK

    def body(ld_ref, qf, kf, vf, cf, sf, qb, kb, vb, cb, sb, of_ref, ob_ref, ssf_ref, ssb_ref, s_scr):
        @pl.when(pl.program_id(0) == 0)
        def _():
            s_scr[...] = jnp.zeros_like(s_scr)

        for dr, (q_ref, k_ref, v_ref, c_ref, n_ref, o_ref, ss_ref) in enumerate(
                ((qf, kf, vf, cf, sf, of_ref, ssf_ref), (qb, kb, vb, cb, sb, ob_ref, ssb_ref))):
            cs, sn = c_ref[...], n_ref[...]
            for h in range(RET_HEADS):
                sl = slice(RET_DH * h, RET_DH * (h + 1))
                ldh = ld_ref[dr, h]
                dm, _, xi, _, zeta, _ = _ret_decay(ldh, dr == 1)
                q = _rope(q_ref[:, sl], cs, sn)
                k = _rope(k_ref[:, sl] * K_SCALE, cs, sn)
                vh = v_ref[:, sl].astype(BF16)
                s = s_scr[dr, h]
                ss_ref[0, h] = s
                sc = (_dot_nt(q.astype(BF16), k.astype(BF16)) * dm).astype(BF16)
                o_ref[:, sl] = _dot(sc, vh) + _dot((q * xi).astype(BF16), s.astype(BF16))
                s_scr[dr, h] = jnp.exp(ldh * RET_CHUNK) * s + _dot_tn((k * zeta).astype(BF16), vh)

    in_f, out_f = _ret_specs(nch, ncc, False, lambda n: n)
    in_b, out_b = _ret_specs(nch, ncc, True, lambda n: n)
    ss_spec = pl.BlockSpec((1, RET_HEADS, RET_DH, RET_DH), lambda n: (n, 0, 0, 0))
    o_shape = jax.ShapeDtypeStruct((la, RET_WIDTH), F32)
    ss_shape = jax.ShapeDtypeStruct((nch, RET_HEADS, RET_DH, RET_DH), F32)
    return pl.pallas_call(
        body, name="ret_scan", grid=(nch,),
        in_specs=[pl.BlockSpec(memory_space=pltpu.SMEM)] + in_f + in_b,
        out_specs=[out_f, out_b, ss_spec, ss_spec],
        out_shape=[o_shape, o_shape, ss_shape, ss_shape],
        scratch_shapes=[pltpu.VMEM((2, RET_HEADS, RET_DH, RET_DH), F32)],
        compiler_params=_params(("arbitrary",)),
    )(ld2, p_all, p_all, p_all, cos_t, sin_t, p_all, p_all, p_all, cos_t, sin_t)


def _ret_scan_bwd(p_all, cos_t, sin_t, ld2, ssf, ssb, dy_all, ncc):
    la = p_all.shape[0]
    nch = la // RET_CHUNK

    def body(ld_ref, qf, kf, vf, cf, sf, dof, ssf_ref, qb, kb, vb, cb, sb, dob_, ssb_ref,
             dqf, dkf, dvf, dqb, dkb, dvb, dld_ref, ds_scr):
        @pl.when(pl.program_id(0) == 0)
        def _():
            ds_scr[...] = jnp.zeros_like(ds_scr)
            dld_ref[...] = jnp.zeros_like(dld_ref)

        for dr, (q_ref, k_ref, v_ref, c_ref, n_ref, do_ref, ss_ref, dq_ref, dk_ref, dv_ref) in enumerate(
                ((qf, kf, vf, cf, sf, dof, ssf_ref, dqf, dkf, dvf), (qb, kb, vb, cb, sb, dob_, ssb_ref, dqb, dkb, dvb))):
            cs, sn = c_ref[...], n_ref[...]
            for h in range(RET_HEADS):
                sl = slice(RET_DH * h, RET_DH * (h + 1))
                ldh = ld_ref[dr, h]
                dm, expo, xi, xi_e, zeta, zeta_e = _ret_decay(ldh, dr == 1)
                gc = jnp.exp(ldh * RET_CHUNK)
                q = _rope(q_ref[:, sl], cs, sn)
                k = _rope(k_ref[:, sl] * K_SCALE, cs, sn)
                q16, k16, v16 = q.astype(BF16), k.astype(BF16), v_ref[:, sl].astype(BF16)
                s = ss_ref[0, h]
                s16 = s.astype(BF16)
                ds_in = ds_scr[dr, h]
                ds16 = ds_in.astype(BF16)
                do16 = do_ref[:, sl].astype(BF16)
                qk = _dot_nt(q16, k16)
                dsv = _dot_nt(do16, v16)
                dsc = (dsv * dm).astype(BF16)
                sc16 = (qk * dm).astype(BF16)
                dos = _dot_nt(do16, s16)
                vds = _dot_nt(v16, ds16)
                dq_ref[:, sl] = _dot(dsc, k16) + dos * xi
                dk_ref[:, sl] = _dot_tn(dsc, q16) + vds * zeta
                dv_ref[:, sl] = _dot_tn(sc16, do16) + _dot((k * zeta).astype(BF16), ds16)
                ds_scr[dr, h] = _dot_tn((q * xi).astype(BF16), do16) + gc * ds_in
                dld = (jnp.sum(dsv * qk * dm * expo) + jnp.sum(q * dos * (xi * xi_e))
                       + jnp.sum(k * vds * (zeta * zeta_e)) + RET_CHUNK * gc * jnp.sum(s * ds_in))
                dld_ref[dr, h] += dld

    back = lambda n: nch - 1 - n
    in_f, out_f = _ret_specs(nch, ncc, False, back)
    in_b, out_b = _ret_specs(nch, ncc, True, back)
    ss_spec = pl.BlockSpec((1, RET_HEADS, RET_DH, RET_DH), lambda n: (nch - 1 - n, 0, 0, 0))
    shp = jax.ShapeDtypeStruct((la, RET_WIDTH), F32)
    return pl.pallas_call(
        body, name="ret_scan_bwd", grid=(nch,),
        in_specs=[pl.BlockSpec(memory_space=pltpu.SMEM)] + in_f + [out_f, ss_spec] + in_b + [out_b, ss_spec],
        out_specs=[out_f, out_f, out_f, out_b, out_b, out_b, _full((2, RET_HEADS, 8, 128))],
        out_shape=[shp] * 6 + [jax.ShapeDtypeStruct((2, RET_HEADS, 8, 128), F32)],
        scratch_shapes=[pltpu.VMEM((2, RET_HEADS, RET_DH, RET_DH), F32)],
        compiler_params=_params(("arbitrary",)),
    )(ld2, p_all, p_all, p_all, cos_t, sin_t, dy_all, ssf, p_all, p_all, p_all, cos_t, sin_t, dy_all, ssb)


def _ret_gate(of, ob, p_all, nct):
    la = of.shape[0]
    tm = TOK_TILE
    l = la - nct * tm

    def body(of_ref, ob_ref, g_ref, r_ref, y_ref):
        y = of_ref[...] + ob_ref[...]
        y_ref[...] = y
        for h in range(RET_HEADS):
            sl = slice(RET_DH * h, RET_DH * (h + 1))
            r_ref[:, sl] = _head_norm_gate(y[:, sl], g_ref[:, sl]).astype(BF16)

    row = pl.BlockSpec((tm, RET_WIDTH), lambda i: (i + nct, 0))
    out = pl.BlockSpec((tm, RET_WIDTH), lambda i: (i, 0))
    return pl.pallas_call(
        body, name="ret_gate", grid=(l // tm,),
        in_specs=[row, row, pl.BlockSpec((tm, RET_WIDTH), lambda i: (i + nct, G_COL // 4))],
        out_specs=[out, out],
        out_shape=[jax.ShapeDtypeStruct((l, RET_WIDTH), BF16), jax.ShapeDtypeStruct((l, RET_WIDTH), F32)],
        compiler_params=_params(("parallel",)),
    )(of, ob, p_all)


def _ret_gate_bwd(y_ret, p_all, dmix, nct):
    la = p_all.shape[0]
    tm = TOK_TILE

    def body(y_ref, g_ref, d_ref, dy_ref, dg_ref):
        i = pl.program_id(0)

        @pl.when(i < nct)
        def _():
            dy_ref[...] = jnp.zeros_like(dy_ref)
            dg_ref[...] = jnp.zeros_like(dg_ref)

        @pl.when(i >= nct)
        def _():
            for h in range(RET_HEADS):
                sl = slice(RET_DH * h, RET_DH * (h + 1))
                _, vjp = jax.vjp(_head_norm_gate, y_ref[:, sl], g_ref[:, sl])
                dy, dg = vjp(d_ref[:, sl])
                dy_ref[:, sl] = dy
                dg_ref[:, sl] = dg

    xrow = lambda cb: pl.BlockSpec((tm, RET_WIDTH), lambda i: (jnp.maximum(i - nct, 0), cb))
    out = pl.BlockSpec((tm, RET_WIDTH), lambda i: (i, 0))
    shp = jax.ShapeDtypeStruct((la, RET_WIDTH), F32)
    return pl.pallas_call(
        body, name="ret_gate_bwd", grid=(la // tm,),
        in_specs=[xrow(0), pl.BlockSpec((tm, RET_WIDTH), lambda i: (i, G_COL // 4)), xrow(1)],
        out_specs=[out, out], out_shape=[shp, shp],
        compiler_params=_params(("parallel",)),
    )(y_ret, p_all, dmix)


def _ret_qkv_grad(dqf, dkf, dvf, dqb, dkb, dvb, du, dg, cos_t, sin_t):
    la = dqf.shape[0]
    tm = TOK_TILE

    def body(dqf_ref, dkf_ref, dvf_ref, dqb_ref, dkb_ref, dvb_ref, du_ref, dg_ref, cos_ref, sin_ref, dp_ref):
        cs, sn = cos_ref[...], sin_ref[...]
        dp_ref[:, 0:S5_WIDTH] = du_ref[...].astype(BF16)
        for h in range(RET_HEADS):
            sl = slice(RET_DH * h, RET_DH * (h + 1))
            dq = _rope_t(dqf_ref[:, sl] + dqb_ref[:, sl], cs, sn)
            dk = _rope_t(dkf_ref[:, sl] + dkb_ref[:, sl], cs, sn) * K_SCALE
            dp_ref[:, 128 * (Q_COL + h):128 * (Q_COL + h + 1)] = dq.astype(BF16)
            dp_ref[:, 128 * (K_COL + h):128 * (K_COL + h + 1)] = dk.astype(BF16)
        dp_ref[:, 128 * V_COL:128 * G_COL] = (dvf_ref[...] + dvb_ref[...]).astype(BF16)
        dp_ref[:, 128 * G_COL:IN_COLS] = dg_ref[...].astype(BF16)

    row = pl.BlockSpec((tm, RET_WIDTH), lambda i: (i, 0))
    tab = pl.BlockSpec((tm, RET_DH), lambda i: (i, 0))
    return pl.pallas_call(
        body, name="ret_qkv_grad", grid=(la // tm,),
        in_specs=[row] * 8 + [tab, tab],
        out_specs=pl.BlockSpec((tm, IN_COLS), lambda i: (i, 0)),
        out_shape=jax.ShapeDtypeStruct((la, IN_COLS), BF16),
        compiler_params=_params(("parallel",)),
    )(dqf, dkf, dvf, dqb, dkb, dvb, du, dg, cos_t, sin_t)


def _outproj_up(x, s5x, retx, w_out_b, mod3, n2w, w_up_b):
    l = x.shape[0]
    tm = TOK_TILE

    def body(x_ref, s_ref, r_ref, wo_ref, mod_ref, nw_ref, wu_ref, x1_ref, mix_ref, h2_ref, up_ref):
        mix = _dot(s_ref[...], wo_ref[0:S5_WIDTH, :]) + _dot(r_ref[...], wo_ref[S5_WIDTH:D_MODEL, :])
        mix_ref[...] = mix
        x1 = x_ref[...] + mod_ref[0:1, :] * mix
        x1_ref[...] = x1
        h2 = _rms_mod(x1, nw_ref[...], mod_ref[1:2, :], mod_ref[2:3, :]).astype(BF16)
        h2_ref[...] = h2
        up_ref[...] = _dot(h2, wu_ref[...])

    row = lambda w: pl.BlockSpec((tm, w), lambda i: (i, 0))
    return pl.pallas_call(
        body, name="outproj_up", grid=(l // tm,),
        in_specs=[row(D_MODEL), row(S5_WIDTH), row(RET_WIDTH), _full((D_MODEL, D_MODEL)), _full((3, D_MODEL)),
                  _full((1, D_MODEL)), _full((D_MODEL, 2 * D_FF))],
        out_specs=[row(D_MODEL), row(D_MODEL), row(D_MODEL), row(2 * D_FF)],
        out_shape=[jax.ShapeDtypeStruct((l, D_MODEL), F32), jax.ShapeDtypeStruct((l, D_MODEL), F32),
                   jax.ShapeDtypeStruct((l, D_MODEL), BF16), jax.ShapeDtypeStruct((l, 2 * D_FF), F32)],
        compiler_params=_params(("parallel",)),
    )(x, s5x, retx, w_out_b, mod3, n2w, w_up_b)


HALO = 8


def _conv_taps(g, prev_row, next_row):
    t = g.shape[0]
    r = _iota2(g.shape, 0)
    gprev = jnp.where(r == 0, prev_row, pltpu.roll(g, 1, 0))
    gnext = jnp.where(r == t - 1, next_row, pltpu.roll(g, t - 1, 0))
    return gprev, gnext


def _ffn_loss(up, x1, conv_w, conv_b, w_down_b, gate, fnw, tgt):
    l = x1.shape[0]
    tm = TOK_TILE
    nt = l // tm
    hb = tm // HALO

    def body(up_a, up_g, hp_ref, hn_ref, x1_ref, cw_ref, cb_ref, wd_ref, gate_ref, fn_ref, tgt_ref,
             act_ref, dx2_ref, ddn_ref, acc_ref):
        i = pl.program_id(0)

        @pl.when(i == 0)
        def _():
            acc_ref[...] = jnp.zeros_like(acc_ref)

        g = up_g[...]
        prev_row = jnp.where(i == 0, 0.0, hp_ref[HALO - 1:HALO, :])
        next_row = jnp.where(i == nt - 1, 0.0, hn_ref[0:1, :])
        gprev, gnext = _conv_taps(g, prev_row, next_row)
        gc = cb_ref[...] + gprev * cw_ref[0:1, :] + g * cw_ref[1:2, :] + gnext * cw_ref[2:3, :]
        act = (_gelu(gc) * up_a[...]).astype(BF16)
        act_ref[...] = act
        dn = _dot(act, wd_ref[...])
        x2 = x1_ref[...] + gate_ref[...] * dn
        y, vjp = jax.vjp(_rms, x2, fn_ref[...])
        err = y - tgt_ref[...]
        dx2, dfn = vjp(err * (1.0 / D_MODEL))
        dx2_ref[...] = dx2
        ddn_ref[...] = (dx2 * gate_ref[...]).astype(BF16)
        acc_ref[0:1, :] += dfn
        acc_ref[1:2, :] += jnp.sum(dx2 * dn, axis=0, keepdims=True)
        acc_ref[2:3, :] += (0.5 / D_MODEL) * jnp.sum(err * err)

    row = lambda w: pl.BlockSpec((tm, w), lambda i: (i, 0))
    last = l // HALO - 1
    return pl.pallas_call(
        body, name="ffn_loss", grid=(nt,),
        in_specs=[pl.BlockSpec((tm, D_FF), lambda i: (i, 0)), pl.BlockSpec((tm, D_FF), lambda i: (i, 1)),
                  pl.BlockSpec((HALO, D_FF), lambda i: (jnp.maximum(i * hb - 1, 0), 1)),
                  pl.BlockSpec((HALO, D_FF), lambda i: (jnp.minimum((i + 1) * hb, last), 1)),
                  row(D_MODEL), _full((3, D_FF)), _full((1, D_FF)), _full((D_FF, D_MODEL)),
                  _full((1, D_MODEL)), _full((1, D_MODEL)), row(D_MODEL)],
        out_specs=[row(D_FF), row(D_MODEL), row(D_MODEL), _full((8, D_MODEL))],
        out_shape=[jax.ShapeDtypeStruct((l, D_FF), BF16), jax.ShapeDtypeStruct((l, D_MODEL), F32),
                   jax.ShapeDtypeStruct((l, D_MODEL), BF16), jax.ShapeDtypeStruct((8, D_MODEL), F32)],
        compiler_params=_params(("arbitrary",)),
    )(up, up, up, up, x1, conv_w, conv_b, w_down_b, gate, fnw, tgt)


def _convglu_bwd(up, dact, conv_w, conv_b):
    l = up.shape[0]
    tm = 128
    nt = l // tm
    hb = tm // HALO
    te = tm + 2 * HALO

    def body(a_ref, ap_ref, an_ref, g_ref, gp_ref, gn_ref, d_ref, dp_ref, dn_ref, cw_ref, cb_ref,
             dup_ref, acc_ref):
        i = pl.program_id(0)

        @pl.when(i == 0)
        def _():
            acc_ref[...] = jnp.zeros_like(acc_ref)

        row = _iota2((te, D_FF), 0) + (i * tm - HALO)
        valid = (row >= 0) & (row < l)

        def ext(p, c, n):
            return jnp.where(valid, jnp.concatenate([p[...], c[...], n[...]], axis=0), 0.0)

        ae, ge, de = ext(ap_ref, a_ref, an_ref), ext(gp_ref, g_ref, gn_ref), ext(dp_ref, d_ref, dn_ref)
        gprev = pltpu.roll(ge, 1, 0)
        gnext = pltpu.roll(ge, te - 1, 0)
        w0, w1, w2 = cw_ref[0:1, :], cw_ref[1:2, :], cw_ref[2:3, :]
        gce = cb_ref[...] + gprev * w0 + ge * w1 + gnext * w2
        _, vjp = jax.vjp(lambda a, gc: _gelu(gc) * a, ae, gce)
        dae, dgce = vjp(de)
        dge = dgce * w1 + pltpu.roll(dgce, te - 1, 0) * w0 + pltpu.roll(dgce, 1, 0) * w2
        mid = slice(HALO, HALO + tm)
        dup_ref[:, 0:D_FF] = dae[mid].astype(BF16)
        dup_ref[:, D_FF:2 * D_FF] = dge[mid].astype(BF16)
        dgc = dgce[mid]
        acc_ref[0:1, :] += jnp.sum(dgc * gprev[mid], axis=0, keepdims=True)
        acc_ref[1:2, :] += jnp.sum(dgc * ge[mid], axis=0, keepdims=True)
        acc_ref[2:3, :] += jnp.sum(dgc * gnext[mid], axis=0, keepdims=True)
        acc_ref[3:4, :] += jnp.sum(dgc, axis=0, keepdims=True)

    last = l // HALO - 1

    def trio(cb):
        return [pl.BlockSpec((tm, D_FF), lambda i: (i, cb)),
                pl.BlockSpec((HALO, D_FF), lambda i: (jnp.maximum(i * hb - 1, 0), cb)),
                pl.BlockSpec((HALO, D_FF), lambda i: (jnp.minimum((i + 1) * hb, last), cb))]

    return pl.pallas_call(
        body, name="convglu_bwd", grid=(nt,),
        in_specs=trio(0) + trio(1) + trio(0) + [_full((3, D_FF)), _full((1, D_FF))],
        out_specs=[pl.BlockSpec((tm, 2 * D_FF), lambda i: (i, 0)), _full((8, D_FF))],
        out_shape=[jax.ShapeDtypeStruct((l, 2 * D_FF), BF16), jax.ShapeDtypeStruct((8, D_FF), F32)],
        compiler_params=_params(("arbitrary",)),
    )(up, up, up, up, up, up, dact, dact, dact, conv_w, conv_b)


def _norm2_bwd(x1, dh2, dx2, mix, mod3, n2w):
    l = x1.shape[0]
    tm = TOK_TILE

    def body(x1_ref, dh_ref, dx2_ref, mix_ref, mod_ref, nw_ref, dx1_ref, dmix_ref, acc_ref):
        @pl.when(pl.program_id(0) == 0)
        def _():
            acc_ref[...] = jnp.zeros_like(acc_ref)

        _, vjp = jax.vjp(_rms_mod, x1_ref[...], nw_ref[...], mod_ref[1:2, :], mod_ref[2:3, :])
        dx, dnw, dsh, dsc = vjp(dh_ref[...])
        dx1 = dx + dx2_ref[...]
        dx1_ref[...] = dx1
        dmix_ref[...] = (dx1 * mod_ref[0:1, :]).astype(BF16)
        acc_ref[0:1, :] += dnw
        acc_ref[1:2, :] += jnp.sum(dx1 * mix_ref[...], axis=0, keepdims=True)
        acc_ref[2:3, :] += dsh
        acc_ref[3:4, :] += dsc

    row = pl.BlockSpec((tm, D_MODEL), lambda i: (i, 0))
    return pl.pallas_call(
        body, name="norm2_bwd", grid=(l // tm,),
        in_specs=[row, row, row, row, _full((3, D_MODEL)), _full((1, D_MODEL))],
        out_specs=[row, row, _full((8, D_MODEL))],
        out_shape=[jax.ShapeDtypeStruct((l, D_MODEL), F32), jax.ShapeDtypeStruct((l, D_MODEL), BF16),
                   jax.ShapeDtypeStruct((8, D_MODEL), F32)],
        compiler_params=_params(("arbitrary",)),
    )(x1, dh2, dx2, mix, mod3, n2w)


MOD_ROWS = 16
MOD_COLS = 6 * D_MODEL // 4


def _mod_fwd(c_all, c_ctx, w_mod_b, b_loc):
    def body(c_ref, cc_ref, w_ref, b_ref, m_ref, s_ref):
        cond = jnp.concatenate([c_ref[...], jnp.broadcast_to(cc_ref[...], (8, D_MODEL))], axis=0)
        s = _silu(cond).astype(BF16)
        s_ref[...] = s
        m_ref[...] = _dot(s, w_ref[...]) + b_ref[...]

    return pl.pallas_call(
        body, name="mod_fwd",
        out_shape=[jax.ShapeDtypeStruct((MOD_ROWS, MOD_COLS), F32), jax.ShapeDtypeStruct((MOD_ROWS, D_MODEL), BF16)],
        compiler_params=_params(),
    )(c_all, c_ctx, w_mod_b, b_loc)


def _mod_bwd_sum(dm_all):
    def body(d_ref, dm_ref, gb_ref):
        rows = [d_ref[k, 0:1, :] for k in range(8)]
        ctx_sum = d_ref[0, 1:2, :]
        for k in range(1, 8):
            ctx_sum = ctx_sum + d_ref[k, 1:2, :]
        gb = ctx_sum
        for k in range(8):
            gb = gb + rows[k]
        gb_ref[...] = gb
        dm_ref[...] = jnp.concatenate(rows + [ctx_sum] + [jnp.zeros((7, 6 * D_MODEL), F32)], axis=0)

    return pl.pallas_call(
        body, name="mod_bwd_sum",
        out_shape=[jax.ShapeDtypeStruct((MOD_ROWS, 6 * D_MODEL), F32), jax.ShapeDtypeStruct((1, 6 * D_MODEL), F32)],
        compiler_params=_params(),
    )(dm_all)


def _mod_bwd_w(dm_loc, s_b, c_ctx, w_mod_b):
    def body(d_ref, s_ref, cc_ref, w_ref, gw_ref, gc_ref):
        db = d_ref[...].astype(BF16)
        gw_ref[...] = _dot_tn(s_ref[...], db)
        ds = _dot_nt(db, w_ref[...])
        _, vjp = jax.vjp(_silu, cc_ref[...])
        gc_ref[...] = jnp.broadcast_to(vjp(ds[8:9, :])[0], (8, D_MODEL))

    return pl.pallas_call(
        body, name="mod_bwd_w",
        out_shape=[jax.ShapeDtypeStruct((D_MODEL, MOD_COLS), F32), jax.ShapeDtypeStruct((8, D_MODEL), F32)],
        compiler_params=_params(),
    )(dm_loc, s_b, c_ctx, w_mod_b)


def _adamw(w, g, m, v, name):
    r, c = w.shape
    tr = _pick(r, (256, 128, 64, 32, 16, 8))
    bc1 = 1.0 - ADAM_B1 ** ADAM_STEP
    bc2 = 1.0 - ADAM_B2 ** ADAM_STEP

    def body(w_ref, g_ref, m_ref, v_ref, d_ref, nm_ref, nv_ref):
        gg = g_ref[...]
        nm = ADAM_B1 * m_ref[...] + (1.0 - ADAM_B1) * gg
        nv = ADAM_B2 * v_ref[...] + (1.0 - ADAM_B2) * (gg * gg)
        nm_ref[...] = nm
        nv_ref[...] = nv
        d_ref[...] = -ADAM_LR * ((nm / bc1) / (jnp.sqrt(nv / bc2) + ADAM_EPS) + ADAM_WD * w_ref[...])

    blk = pl.BlockSpec((tr, c), lambda i: (i, 0))
    shp = jax.ShapeDtypeStruct((r, c), F32)
    return pl.pallas_call(
        body, name=name, grid=(r // tr,), in_specs=[blk] * 4, out_specs=[blk] * 3, out_shape=[shp] * 3,
        compiler_params=_params(("parallel",)),
    )(w, g, m, v)


def _sum_slots(a, name):
    n, r, c = a.shape
    tr = _pick(r, (376, 256, 208, 128, 64, 32, 16, 8))

    def body(a_ref, o_ref):
        acc = a_ref[0].astype(F32)
        for k in range(1, n):
            acc = acc + a_ref[k].astype(F32)
        o_ref[...] = acc

    return pl.pallas_call(
        body, name=name, grid=(r // tr,),
        in_specs=[pl.BlockSpec((n, tr, c), lambda i: (0, i, 0))],
        out_specs=pl.BlockSpec((tr, c), lambda i: (i, 0)),
        out_shape=jax.ShapeDtypeStruct((r, c), F32),
        compiler_params=_params(("parallel",)),
    )(a)


def _mesh_pos():
    return lax.axis_index("x"), lax.axis_index("y"), lax.axis_index("c")


def _all_gather8(v, name):
    m_per, n = v.shape

    def body(x_ref, out_ref, send_sems, recv_sems, local_sem):
        x, y, c = _mesh_pos()
        me, sibling = (x, y, c), (x, y, 1 - c)
        chips = [(1 - x, y), (x, 1 - y), (1 - x, 1 - y)]

        def rows(px, py, pc):
            return out_ref.at[pl.ds((4 * px + 2 * py + pc) * m_per, m_per), :]

        def copy(k, block, to, src=None):
            return pltpu.make_async_remote_copy(
                src_ref=rows(*block) if src is None else src, dst_ref=rows(*block),
                send_sem=send_sems.at[k], recv_sem=recv_sems.at[k], device_id=to, device_id_type=MESH_ID)

        mine = pltpu.make_async_copy(x_ref, rows(*me), local_sem)
        mine.start()
        first = [copy(0, me, sibling, src=x_ref)]
        first += [copy(1 + j, me, (*chip, c), src=x_ref) for j, chip in enumerate(chips)]
        for cp in first:
            cp.start()
        passed = [copy(4 + j, (*chip, c), sibling) for j, chip in enumerate(chips)]
        for j, chip in enumerate(chips):
            copy(1 + j, (*chip, c), me).wait_recv()
            passed[j].start()
        copy(0, sibling, me).wait_recv()
        for j, chip in enumerate(chips):
            copy(4 + j, (*chip, 1 - c), me).wait_recv()
        for cp in first + passed:
            cp.wait_send()
        mine.wait()

    return pl.pallas_call(
        body, name=name,
        out_shape=jax.ShapeDtypeStruct((8 * m_per, n), v.dtype),
        in_specs=[pl.BlockSpec(memory_space=pltpu.VMEM)],
        out_specs=pl.BlockSpec(memory_space=pltpu.VMEM),
        scratch_shapes=[pltpu.SemaphoreType.DMA((7,)), pltpu.SemaphoreType.DMA((7,)), pltpu.SemaphoreType.DMA],
        compiler_params=_params(),
    )(v)


ANY = pl.BlockSpec(memory_space=pl.ANY)
PEER_CHIPS = lambda x, y: [(x, 1 - y), (1 - x, y), (1 - x, 1 - y)]


def _shard_region(ref, kind, k, rl, cl, r0, nr, c0, nc):
    if kind == "col":
        return ref.at[pl.ds(r0, nr), pl.ds(k * cl + c0, nc)]
    return ref.at[pl.ds(k * rl + r0, nr), pl.ds(c0, nc)]


def _gather_weights(shards, kinds):
    n = len(shards)
    fulls = [jax.ShapeDtypeStruct((s.shape[0], 4 * s.shape[1]) if k == "col" else (4 * s.shape[0], s.shape[1]), s.dtype)
             for s, k in zip(shards, kinds)]

    def body(*refs):
        locs, outs = refs[:n], refs[n:2 * n]
        send_sems, recv_sems, local_sems = refs[2 * n:]
        x, y, c = _mesh_pos()
        me = 2 * x + y
        peers = PEER_CHIPS(x, y)
        waits, sends = [], []
        for a in range(n):
            rl, cl = shards[a].shape
            rh = rl // 2
            reg = functools.partial(_shard_region, outs[a], kinds[a], rl=rl, cl=cl, c0=0, nc=cl)
            mine = pltpu.make_async_copy(locs[a], reg(k=me, r0=0, nr=rl), local_sems.at[a])
            mine.start()
            waits.append(mine)
            for j, (px, py) in enumerate(peers):
                cp = pltpu.make_async_remote_copy(
                    src_ref=locs[a].at[pl.ds(c * rh, rh), :], dst_ref=reg(k=me, r0=c * rh, nr=rh),
                    send_sem=send_sems.at[a, j], recv_sem=recv_sems.at[a, j],
                    device_id=(px, py, c), device_id_type=MESH_ID)
                cp.start()
                sends.append(cp)
        for a in range(n):
            rl, cl = shards[a].shape
            rh = rl // 2
            reg = functools.partial(_shard_region, outs[a], kinds[a], rl=rl, cl=cl, c0=0, nc=cl)
            for j, (px, py) in enumerate(peers):
                got = reg(k=2 * px + py, r0=c * rh, nr=rh)
                pltpu.make_async_remote_copy(src_ref=got, dst_ref=got, send_sem=send_sems.at[a, j],
                                             recv_sem=recv_sems.at[a, j], device_id=(px, py, c),
                                             device_id_type=MESH_ID).wait_recv()
                fwd = pltpu.make_async_remote_copy(src_ref=got, dst_ref=got, send_sem=send_sems.at[a, 3 + j],
                                                   recv_sem=recv_sems.at[a, 3 + j], device_id=(x, y, 1 - c),
                                                   device_id_type=MESH_ID)
                fwd.start()
                sends.append(fwd)
        for a in range(n):
            rl, cl = shards[a].shape
            rh = rl // 2
            reg = functools.partial(_shard_region, outs[a], kinds[a], rl=rl, cl=cl, c0=0, nc=cl)
            for j, (px, py) in enumerate(peers):
                got = reg(k=2 * px + py, r0=(1 - c) * rh, nr=rh)
                pltpu.make_async_remote_copy(src_ref=got, dst_ref=got, send_sem=send_sems.at[a, 3 + j],
                                             recv_sem=recv_sems.at[a, 3 + j], device_id=(x, y, 1 - c),
                                             device_id_type=MESH_ID).wait_recv()
        for cp in sends:
            cp.wait_send()
        for cp in waits:
            cp.wait()

    return pl.pallas_call(
        body, name="gather_weights", out_shape=fulls, in_specs=[ANY] * n, out_specs=[ANY] * n,
        scratch_shapes=[pltpu.SemaphoreType.DMA((n, 6)), pltpu.SemaphoreType.DMA((n, 6)), pltpu.SemaphoreType.DMA((n,))],
        compiler_params=_params(),
    )(*shards)


def _half(kind, r, c):
    return (r // 2, c) if kind == "col" else (r, c // 2)


def _half_of(ref, kind, which):
    r, c = ref.shape
    hr, hc = _half(kind, r, c)
    return ref.at[pl.ds(which * hr, hr), :] if kind == "col" else ref.at[:, pl.ds(which * hc, hc)]


def _rs_sibling(grads, kinds):
    n = len(grads)

    def body(*refs):
        srcs, dsts = refs[:n], refs[n:2 * n]
        send_sems, recv_sems = refs[2 * n:]
        x, y, c = _mesh_pos()
        cps = [pltpu.make_async_remote_copy(src_ref=_half_of(srcs[a], kinds[a], 1 - c), dst_ref=dsts[a],
                                            send_sem=send_sems.at[a], recv_sem=recv_sems.at[a],
                                            device_id=(x, y, 1 - c), device_id_type=MESH_ID) for a in range(n)]
        for cp in cps:
            cp.start()
        for cp in cps:
            cp.wait()

    return pl.pallas_call(
        body, name="rs_sibling",
        out_shape=[jax.ShapeDtypeStruct(_half(k, *g.shape), g.dtype) for g, k in zip(grads, kinds)],
        in_specs=[ANY] * n, out_specs=[ANY] * n,
        scratch_shapes=[pltpu.SemaphoreType.DMA((n,)), pltpu.SemaphoreType.DMA((n,))],
        compiler_params=_params(),
    )(*grads)


def _pair_sum(gf, rv, kind, ci, name):
    r, c = rv.shape
    tr = _pick(r, (128, 64, 32, 16, 8))
    nt = r // tr

    def body(ci_ref, g_ref, r_ref, o_ref):
        o_ref[...] = (g_ref[...] + r_ref[...]).astype(BF16)

    g_map = (lambda i, ci_ref: (ci_ref[0] * nt + i, 0)) if kind == "col" else (lambda i, ci_ref: (i, ci_ref[0]))
    blk = pl.BlockSpec((tr, c), lambda i, ci_ref: (i, 0))
    return pl.pallas_call(
        body, name=name,
        grid_spec=pltpu.PrefetchScalarGridSpec(num_scalar_prefetch=1, grid=(nt,),
                                               in_specs=[pl.BlockSpec((tr, c), g_map), blk], out_specs=blk),
        out_shape=jax.ShapeDtypeStruct((r, c), BF16),
        compiler_params=_params(("parallel",)),
    )(ci.reshape(1), gf, rv)


def _rs_chips(pairs, kinds):
    n = len(pairs)
    shapes = []
    for p, k in zip(pairs, kinds):
        r, c = p.shape
        shapes.append((r, c // 4) if k == "col" else (r // 4, c))

    def body(*refs):
        srcs, dsts = refs[:n], refs[n:2 * n]
        send_sems, recv_sems, local_sems = refs[2 * n:]
        x, y, c = _mesh_pos()
        me = 2 * x + y
        peers = PEER_CHIPS(x, y)
        cps, mine = [], []
        for a in range(n):
            rl, cl = shapes[a]
            reg = functools.partial(_shard_region, srcs[a], kinds[a], rl=rl, cl=cl, r0=0, nr=rl, c0=0, nc=cl)
            mine.append(pltpu.make_async_copy(reg(k=me), dsts[a].at[me], local_sems.at[a]))
            mine[-1].start()
            for j, (px, py) in enumerate(peers):
                cps.append(pltpu.make_async_remote_copy(
                    src_ref=reg(k=2 * px + py), dst_ref=dsts[a].at[me], send_sem=send_sems.at[a, j],
                    recv_sem=recv_sems.at[a, j], device_id=(px, py, c), device_id_type=MESH_ID))
                cps[-1].start()
        for a in range(n):
            for j, (px, py) in enumerate(peers):
                slot = dsts[a].at[2 * px + py]
                pltpu.make_async_remote_copy(src_ref=slot, dst_ref=slot, send_sem=send_sems.at[a, j],
                                             recv_sem=recv_sems.at[a, j], device_id=(px, py, c),
                                             device_id_type=MESH_ID).wait_recv()
        for cp in cps:
            cp.wait_send()
        for cp in mine:
            cp.wait()

    return pl.pallas_call(
        body, name="rs_chips",
        out_shape=[jax.ShapeDtypeStruct((4,) + s, p.dtype) for s, p in zip(shapes, pairs)],
        in_specs=[ANY] * n, out_specs=[ANY] * n,
        scratch_shapes=[pltpu.SemaphoreType.DMA((n, 3)), pltpu.SemaphoreType.DMA((n, 3)), pltpu.SemaphoreType.DMA((n,))],
        compiler_params=_params(),
    )(*pairs)


def _rs_back(reds, kinds):
    n = len(reds)
    fulls = [(2 * r.shape[0], r.shape[1]) if k == "col" else (r.shape[0], 2 * r.shape[1]) for r, k in zip(reds, kinds)]

    def body(*refs):
        srcs, dsts = refs[:n], refs[n:2 * n]
        send_sems, recv_sems, local_sems = refs[2 * n:]
        x, y, c = _mesh_pos()
        cps, mine = [], []
        for a in range(n):
            mine.append(pltpu.make_async_copy(srcs[a], _half_of(dsts[a], kinds[a], c), local_sems.at[a]))
            mine[-1].start()
            cps.append(pltpu.make_async_remote_copy(
                src_ref=srcs[a], dst_ref=_half_of(dsts[a], kinds[a], c), send_sem=send_sems.at[a],
                recv_sem=recv_sems.at[a], device_id=(x, y, 1 - c), device_id_type=MESH_ID))
            cps[-1].start()
        for a in range(n):
            other = _half_of(dsts[a], kinds[a], 1 - c)
            pltpu.make_async_remote_copy(src_ref=other, dst_ref=other, send_sem=send_sems.at[a],
                                         recv_sem=recv_sems.at[a], device_id=(x, y, 1 - c),
                                         device_id_type=MESH_ID).wait_recv()
        for cp in cps:
            cp.wait_send()
        for cp in mine:
            cp.wait()

    return pl.pallas_call(
        body, name="rs_back",
        out_shape=[jax.ShapeDtypeStruct(s, r.dtype) for s, r in zip(fulls, reds)],
        in_specs=[ANY] * n, out_specs=[ANY] * n,
        scratch_shapes=[pltpu.SemaphoreType.DMA((n,)), pltpu.SemaphoreType.DMA((n,)), pltpu.SemaphoreType.DMA((n,))],
        compiler_params=_params(),
    )(*reds)


def _rope_tables(l, lc):
    rows = l // GRID_W
    row = jnp.repeat(jnp.arange(rows, dtype=F32), GRID_W)
    col = jnp.tile(jnp.arange(GRID_W, dtype=F32), rows)
    n_freq = RET_DH // 4
    inv_freq = ROPE_THETA ** (-jnp.arange(n_freq, dtype=F32) / n_freq)
    ang = jnp.concatenate([row[:, None] * inv_freq, col[:, None] * inv_freq], axis=-1)
    cos_t = jnp.repeat(jnp.cos(ang), 2, axis=-1)
    sin_t = jnp.repeat(jnp.sin(ang), 2, axis=-1) * jnp.tile(jnp.array([-1.0, 1.0], F32), RET_DH // 2)
    cos_t = jnp.concatenate([jnp.ones((lc, RET_DH), F32), cos_t], axis=0)
    sin_t = jnp.concatenate([jnp.zeros((lc, RET_DH), F32), sin_t], axis=0)
    return cos_t, sin_t


def _s5_pack(a):
    blk = lambda t: t.reshape(1, S5_NB, 128, S5_STATE)
    lre = jnp.stack([a["s5_lambda_re_f"][0], a["s5_lambda_re_b"][0]]).reshape(2, S5_NB, 8, S5_STATE)
    lim = jnp.stack([a["s5_lambda_im_f"][0], a["s5_lambda_im_b"][0]]).reshape(2, S5_NB, 8, S5_STATE)
    lst = jnp.stack([a["s5_log_step_f"][0], a["s5_log_step_b"][0]]).reshape(2, S5_NB, 8, 1)
    b_re = blk(a["s5_b_re"][0].transpose(0, 2, 1))
    b_im = blk(a["s5_b_im"][0].transpose(0, 2, 1))
    return (lre, lim, lst, b_re, b_im, blk(a["s5_c_re"][0]), blk(a["s5_c_im"][0]),
            a["s5_d"].reshape(1, S5_NB, 1, 128))


def _s5_unpack(g):
    glre, glim, glst, gbre, gbim, gcre, gcim, gd = g
    unb = lambda t: t.reshape(S5_GROUPS, S5_GROUP, S5_STATE).transpose(0, 2, 1)[None]
    return {
        "s5_lambda_re_f": glre[0].reshape(1, S5_GROUPS, S5_STATE), "s5_lambda_re_b": glre[1].reshape(1, S5_GROUPS, S5_STATE),
        "s5_lambda_im_f": glim[0].reshape(1, S5_GROUPS, S5_STATE), "s5_lambda_im_b": glim[1].reshape(1, S5_GROUPS, S5_STATE),
        "s5_log_step_f": glst[0].reshape(1, S5_GROUPS), "s5_log_step_b": glst[1].reshape(1, S5_GROUPS),
        "s5_b_re": unb(gbre), "s5_b_im": unb(gbim),
        "s5_c_re": gcre.reshape(1, S5_GROUPS, S5_GROUP, S5_STATE), "s5_c_im": gcim.reshape(1, S5_GROUPS, S5_GROUP, S5_STATE),
        "s5_d": gd.reshape(1, S5_WIDTH),
    }


def _local_step(a, wb, mx, mc, conv_w):
    x, ctx, tgt = a["x"][0], a["ctx"][0], a["loss_target"][0]
    l, lc = x.shape[0], ctx.shape[0]
    la = l + lc
    nct, ncc, nrc, cn = lc // TOK_TILE, lc // S5_T, lc // RET_CHUNK, la // S5_T
    n1w, n2w, fnw = a["norm1_w"], a["norm2_w"], a["final_norm_w"].reshape(1, D_MODEL)
    conv_b, b_glu = a["conv_b"], a["s5_b_glu"]
    ld2 = jnp.concatenate([a["ret_log_decay_f"], a["ret_log_decay_b"]], axis=0)
    mod4 = jnp.concatenate([mc[0:2], mx[0:2]], axis=0)
    mod3 = mx[2:5]
    gate5 = mx[5:6]
    cos_t, sin_t = _rope_tables(l, lc)
    s5p = _s5_pack(a)

    p_all, h1b = _norm_inproj(x, ctx, n1w, mod4, wb["w_in"])
    p3 = p_all.reshape(cn, S5_T, IN_COLS)
    kb, wst, wout, a16 = _s5_gen(*s5p)
    sloc = _s5_state(p3, wst)
    a16s = a16.transpose(1, 0, 2, 3).reshape(2, S5_GROUPS, 128)
    hs = _s5_scan(sloc, a16s, ncc)
    y_all = _s5_out(p3, kb, hs, wout).reshape(la, S5_WIDTH)
    s5x = _s5_glu(y_all, wb["s5_w_glu"], b_glu, nct)
    of, ob, ssf, ssb = _ret_scan(p_all, cos_t, sin_t, ld2, nrc)
    retx, y_ret = _ret_gate(of, ob, p_all, nct)
    x1, mix, h2b, up = _outproj_up(x, s5x, retx, wb["w_out"], mod3, n2w, wb["w_up"])
    act, dx2, ddn, acc_f = _ffn_loss(up, x1, conv_w, conv_b, wb["w_down"], gate5, fnw, tgt)

    g = {}
    dact = _mm(ddn, wb["w_down"], nt=True, name="dact")
    g["w_down"] = _mm_tn(act, ddn, name="gw_down")
    dup, acc_c = _convglu_bwd(up, dact, conv_w, conv_b)
    dh2 = _mm(dup, wb["w_up"], nt=True, name="dh2")
    g["w_up"] = _mm_tn(h2b, dup, name="gw_up")
    dx1, dmixb, acc_2 = _norm2_bwd(x1, dh2, dx2, mix, mod3, n2w)
    dmix = _mm(dmixb, wb["w_out"], nt=True, name="dmix")
    g["w_out"] = jnp.concatenate([_mm_tn(s5x, dmixb, name="gw_out_s5"), _mm_tn(retx, dmixb, name="gw_out_ret")], axis=0)

    dy_s5, g["s5_w_glu"], g["s5_b_glu"] = _s5_glu_bwd(y_all, dmix, wb["s5_w_glu"], b_glu, nct)
    dy3 = dy_s5.reshape(cn, S5_T, S5_WIDTH)
    e = _s5_bwd_h(dy3, wout)
    ds, da16 = _s5_scan_bwd(e, hs, a16s, ncc)
    du = _s5_bwd_u(dy3, kb, ds, wst).reshape(la, S5_WIDTH)
    dkb = _s5_bwd_kb(p3, dy3)
    dwst = _s5_bwd_w(p3, ds, "s5_bwd_wst")
    dwout = _s5_bwd_w(dy3, hs, "s5_bwd_wout")
    da16p = da16.reshape(2, S5_NB, 8, 128).transpose(1, 0, 2, 3)
    g.update(_s5_unpack(_s5_gen_bwd(*s5p, dkb, dwst, dwout, da16p)))

    dy_ret, dg = _ret_gate_bwd(y_ret, p_all, dmix, nct)
    dqf, dkf, dvf, dqb, dkb_, dvb, dld = _ret_scan_bwd(p_all, cos_t, sin_t, ld2, ssf, ssb, dy_ret, nrc)
    g["ret_log_decay_f"] = dld[0, :, 0, 0].reshape(1, RET_HEADS)
    g["ret_log_decay_b"] = dld[1, :, 0, 0].reshape(1, RET_HEADS)
    dp = _ret_qkv_grad(dqf, dkf, dvf, dqb, dkb_, dvb, du, dg, cos_t, sin_t)
    dh1 = _mm(dp, wb["w_in"], nt=True, name="dh1")
    g["w_in"] = _mm_tn(h1b, dp, name="gw_in")
    grad_x, acc_1 = _norm_inproj_bwd(x, ctx, n1w, mod4, dh1, dx1)

    g["norm1_w"], g["norm2_w"], g["final_norm_w"] = acc_1[0:1], acc_2[0:1], acc_f[0]
    g["conv_w"], g["conv_b"] = acc_c[0:3], acc_c[3:4]
    zero = jnp.zeros((1, D_MODEL), F32)
    dmx = jnp.concatenate([acc_1[3:5], acc_2[1:2], acc_2[2:4], acc_f[1:2]], axis=0)
    dmc = jnp.concatenate([acc_1[1:3], zero, zero, zero, zero], axis=0)
    return acc_f[2, 0], grad_x, g, dmx, dmc


WEIGHT_NAMES = ("c_ctx", "w_mod", "b_mod", "norm1_w", "w_in", "s5_lambda_re_f", "s5_lambda_im_f", "s5_log_step_f",
                "s5_lambda_re_b", "s5_lambda_im_b", "s5_log_step_b", "s5_b_re", "s5_b_im", "s5_c_re", "s5_c_im",
                "s5_d", "s5_w_glu", "s5_b_glu", "ret_log_decay_f", "ret_log_decay_b", "w_out", "norm2_w", "w_up",
                "conv_w", "conv_b", "w_down", "final_norm_w")
BIG_NAMES = ("w_in", "w_out", "w_up", "w_down", "s5_w_glu")
BIG_KINDS = ("col", "row", "col", "row", "row")
SMALL_NAMES = ("norm1_w", "norm2_w", "final_norm_w", "conv_b", "conv_w", "s5_lambda_re_f", "s5_lambda_im_f",
               "s5_log_step_f", "s5_lambda_re_b", "s5_lambda_im_b", "s5_log_step_b", "s5_b_re", "s5_b_im", "s5_c_re",
               "s5_c_im", "s5_d", "s5_b_glu", "ret_log_decay_f", "ret_log_decay_b")
ROW = 1024
N_CHIPS = 4


def _pack_rows(parts):
    flat = jnp.concatenate([p.reshape(-1) for p in parts])
    n = flat.shape[0]
    rows = -(-n // (8 * ROW)) * 8
    return jnp.pad(flat, (0, rows * ROW - n)).reshape(rows, ROW)


def _unpack_rows(packed, shapes):
    flat = packed.reshape(-1)
    out, off = [], 0
    for s in shapes:
        n = math.prod(s)
        out.append(flat[off:off + n].reshape(s))
        off += n
    return out


def _step(a):
    xi, yi, ci = _mesh_pos()
    chip = 2 * xi + yi
    dev = 2 * chip + ci

    cw_loc = a["conv_w"].reshape(-1)
    small_in = jnp.concatenate([a["c"].reshape(-1), jnp.pad(cw_loc, (0, 24 * 128 - cw_loc.shape[0]))]).reshape(32, 128)
    sg = _all_gather8(small_in, "gather_cond").reshape(8, 32, 128)
    c_all = sg[:, 0:8].reshape(8, D_MODEL)
    conv_w = sg[0::2, 8:32].reshape(N_CHIPS, -1)[:, :cw_loc.shape[0]].reshape(N_CHIPS, 3, -1)
    conv_w = conv_w.transpose(1, 0, 2).reshape(3, D_FF)

    wb = dict(zip(BIG_NAMES, _gather_weights([a[n][0].astype(BF16) for n in BIG_NAMES], BIG_KINDS)))

    w_mod_b = a["w_mod"][0].astype(BF16)
    c_ctx = a["c_ctx"].reshape(1, D_MODEL)
    b_loc = lax.dynamic_slice_in_dim(a["b_mod"], chip * MOD_COLS, MOD_COLS, 1)
    m_loc, s_b = _mod_fwd(c_all, c_ctx, w_mod_b, b_loc)
    mg = _all_gather8(m_loc, "gather_mod").reshape(8, MOD_ROWS, MOD_COLS)
    m_full = mg[0::2].transpose(1, 0, 2).reshape(MOD_ROWS, 6 * D_MODEL)
    mx = lax.dynamic_slice_in_dim(m_full, dev, 1, 0).reshape(6, D_MODEL)
    mc = m_full[8].reshape(6, D_MODEL)

    loss_part, grad_x, g, dmx, dmc = _local_step(a, wb, mx, mc, conv_w)
    loss = lax.psum(loss_part, ("x", "y", "c"))

    dm_pair = jnp.concatenate([dmx.reshape(1, -1), dmc.reshape(1, -1), jnp.zeros((6, 6 * D_MODEL), F32)], axis=0)
    dm_all = _all_gather8(dm_pair, "gather_dmod").reshape(8, 8, 6 * D_MODEL)
    dm16, gb_mod = _mod_bwd_sum(dm_all)
    dm_loc = lax.dynamic_slice_in_dim(dm16, chip * MOD_COLS, MOD_COLS, 1)
    gw_mod, gcc = _mod_bwd_w(dm_loc, s_b, c_ctx, w_mod_b)

    small_parts = [g[n] for n in SMALL_NAMES] + [gcc[0]]
    small_shapes = [p.shape for p in small_parts]
    sp = _pack_rows(small_parts)
    tot = _sum_slots(_all_gather8(sp, "gather_small_grads").reshape(8, sp.shape[0], ROW), "sum_small_grads")
    small = dict(zip(SMALL_NAMES + ("c_ctx",), _unpack_rows(tot, small_shapes)))
    grads = {n: small[n].reshape(a[n].shape) for n in SMALL_NAMES if n != "conv_w"}
    grads["c_ctx"] = (0.5 * small["c_ctx"]).reshape(a["c_ctx"].shape)
    grads["conv_w"] = lax.dynamic_slice_in_dim(small["conv_w"], chip * (D_FF // N_CHIPS), D_FF // N_CHIPS, 1)[None]
    grads["b_mod"] = gb_mod
    grads["w_mod"] = gw_mod[None]

    gfull = [g[n] for n in BIG_NAMES]
    sib = _rs_sibling(gfull, BIG_KINDS)
    pairs = [_pair_sum(gf, rv, k, ci, "rs_pair_" + n) for gf, rv, k, n in zip(gfull, sib, BIG_KINDS, BIG_NAMES)]
    reds = [_sum_slots(t, "rs_sum_" + n) for t, n in zip(_rs_chips(pairs, BIG_KINDS), BIG_NAMES)]
    for n, t in zip(BIG_NAMES, _rs_back(reds, BIG_KINDS)):
        grads[n] = t[None]

    delta, new_m, new_v = {}, {}, {}
    for n in BIG_NAMES + ("w_mod",):
        for dst, t in zip((delta, new_m, new_v), _adamw(a[n][0], grads[n][0], a["m_" + n][0], a["v_" + n][0], "adamw_" + n)):
            dst[n] = t[None]
    rest = [n for n in WEIGHT_NAMES if n not in BIG_NAMES and n != "w_mod"]
    shapes = [a[n].shape for n in rest]
    pr = lambda pre: _pack_rows([a[pre + n] for n in rest])
    for dst, t in zip((delta, new_m, new_v),
                      _adamw(pr(""), _pack_rows([grads[n] for n in rest]), pr("m_"), pr("v_"), "adamw_small")):
        dst.update(zip(rest, _unpack_rows(t, shapes)))

    return (loss, grad_x[None], *[grads[n] for n in WEIGHT_NAMES], *[delta[n] for n in WEIGHT_NAMES],
            *[new_m[n] for n in WEIGHT_NAMES], *[new_v[n] for n in WEIGHT_NAMES])


def kernel(x, c, ctx, c_ctx, w_mod, b_mod, norm1_w, w_in, s5_lambda_re_f, s5_lambda_im_f, s5_log_step_f, s5_lambda_re_b, s5_lambda_im_b, s5_log_step_b, s5_b_re, s5_b_im, s5_c_re, s5_c_im, s5_d, s5_w_glu, s5_b_glu, ret_log_decay_f, ret_log_decay_b, w_out, norm2_w, w_up, conv_w, conv_b, w_down, final_norm_w, loss_target, m_c_ctx, m_w_mod, m_b_mod, m_norm1_w, m_w_in, m_s5_lambda_re_f, m_s5_lambda_im_f, m_s5_log_step_f, m_s5_lambda_re_b, m_s5_lambda_im_b, m_s5_log_step_b, m_s5_b_re, m_s5_b_im, m_s5_c_re, m_s5_c_im, m_s5_d, m_s5_w_glu, m_s5_b_glu, m_ret_log_decay_f, m_ret_log_decay_b, m_w_out, m_norm2_w, m_w_up, m_conv_w, m_conv_b, m_w_down, m_final_norm_w, v_c_ctx, v_w_mod, v_b_mod, v_norm1_w, v_w_in, v_s5_lambda_re_f, v_s5_lambda_im_f, v_s5_log_step_f, v_s5_lambda_re_b, v_s5_lambda_im_b, v_s5_log_step_b, v_s5_b_re, v_s5_b_im, v_s5_c_re, v_s5_c_im, v_s5_d, v_s5_w_glu, v_s5_b_glu, v_ret_log_decay_f, v_ret_log_decay_b, v_w_out, v_norm2_w, v_w_up, v_conv_w, v_conv_b, v_w_down, v_final_norm_w):
    return _step(dict(locals()))
```

```python
import functools
import math

import jax
import jax.numpy as jnp
from jax import lax
from jax.experimental import pallas as pl
from jax.experimental.pallas import tpu as pltpu

F32 = jnp.float32
BF16 = jnp.bfloat16

D_MODEL = 1024
S5_WIDTH = 512
S5_GROUPS = 32
S5_GROUP = 16
S5_STATE = 64
RET_WIDTH = 512
RET_HEADS = 4
RET_DH = 128
RET_CHUNK = 128
GRID_W = 64
ROPE_THETA = 10000.0
D_FF = 2816
NORM_EPS = 1e-6
IN_COLS = S5_WIDTH + 4 * RET_WIDTH

S5_T = 16
S5_NB = 4
S5_BW = S5_T * 128
S5_SW = 8 * 2 * S5_STATE

ADAM_LR, ADAM_B1, ADAM_B2, ADAM_EPS, ADAM_WD, ADAM_STEP = 0.001, 0.9, 0.999, 1e-08, 0.01, 10

VMEM_LIMIT = 56 * 1024 * 1024
MESH_ID = pl.DeviceIdType.MESH


def _params(sem=None):
    return pltpu.CompilerParams(dimension_semantics=sem, vmem_limit_bytes=VMEM_LIMIT)


def _full(shape):
    n = len(shape)
    return pl.BlockSpec(shape, lambda *_: (0,) * n)


def _dot(a, b):
    return jnp.dot(a, b, preferred_element_type=F32)


def _dot_nt(a, b):
    return lax.dot_general(a, b, (((1,), (1,)), ((), ())), preferred_element_type=F32)


def _dot_tn(a, b):
    return lax.dot_general(a, b, (((0,), (0,)), ((), ())), preferred_element_type=F32)


def _dot_hi(a, b):
    return jnp.dot(a, b, preferred_element_type=F32, precision=lax.Precision.HIGHEST)


def _dot_nt_hi(a, b):
    return lax.dot_general(a, b, (((1,), (1,)), ((), ())), preferred_element_type=F32,
                           precision=lax.Precision.HIGHEST)


def _gelu(x):
    return 0.5 * x * (1.0 + jnp.tanh(0.7978845608028654 * (x + 0.044715 * (x * x * x))))


def _sigmoid(x):
    return 1.0 / (1.0 + jnp.exp(-x))


def _silu(x):
    return x * _sigmoid(x)


def _rms_mod(x, nw, sh, sc):
    r = lax.rsqrt(jnp.mean(x * x, axis=-1, keepdims=True) + NORM_EPS)
    return (x * r * nw) * (1.0 + sc) + sh


def _rms(x, nw):
    r = lax.rsqrt(jnp.mean(x * x, axis=-1, keepdims=True) + NORM_EPS)
    return x * r * nw


def _head_norm_gate(y, g):
    mu = jnp.mean(y, axis=-1, keepdims=True)
    yc = y - mu
    var = jnp.mean(yc * yc, axis=-1, keepdims=True)
    return _silu(g) * (yc * lax.rsqrt(var + NORM_EPS))


def _swap_pairs(t):
    lane = lax.broadcasted_iota(jnp.int32, t.shape, 1)
    return jnp.where(lane % 2 == 0, pltpu.roll(t, RET_DH - 1, 1), pltpu.roll(t, 1, 1))


def _rope(t, cos_t, sin_t):
    return t * cos_t + _swap_pairs(t) * sin_t


def _rope_t(dt, cos_t, sin_t):
    return dt * cos_t + _swap_pairs(dt * sin_t)


def _pick(n, prefs):
    for p in prefs:
        if n % p == 0:
            return p
    return n


def _mm(a, w, *, nt=False, out_dtype=F32, name):
    m, k = a.shape
    n = w.shape[0] if nt else w.shape[1]
    tm = _pick(m, (512, 256, 128))
    tn = _pick(n, (1408, 1024, 1280, 512))

    def body(a_ref, w_ref, o_ref):
        f = _dot_nt if nt else _dot
        o_ref[...] = f(a_ref[...], w_ref[...]).astype(out_dtype)

    w_spec = pl.BlockSpec((tn, k), lambda j, i: (j, 0)) if nt else pl.BlockSpec((k, tn), lambda j, i: (0, j))
    return pl.pallas_call(
        body, name=name, grid=(n // tn, m // tm),
        in_specs=[pl.BlockSpec((tm, k), lambda j, i: (i, 0)), w_spec],
        out_specs=pl.BlockSpec((tm, tn), lambda j, i: (i, j)),
        out_shape=jax.ShapeDtypeStruct((m, n), out_dtype),
        compiler_params=_params(("parallel", "parallel")),
    )(a, w)


def _mm_tn(a, b, *, name):
    m, k = a.shape
    n = b.shape[1]
    tm = _pick(m, (512, 256, 128))
    tn = _pick(n, (1408, 1024, 1280, 512))

    def body(a_ref, b_ref, o_ref):
        @pl.when(pl.program_id(1) == 0)
        def _():
            o_ref[...] = jnp.zeros_like(o_ref)
        o_ref[...] += _dot_tn(a_ref[...], b_ref[...])

    return pl.pallas_call(
        body, name=name, grid=(n // tn, m // tm),
        in_specs=[pl.BlockSpec((tm, k), lambda j, i: (i, 0)), pl.BlockSpec((tm, tn), lambda j, i: (i, j))],
        out_specs=pl.BlockSpec((k, tn), lambda j, i: (0, j)),
        out_shape=jax.ShapeDtypeStruct((k, n), F32),
        compiler_params=_params(("parallel", "arbitrary")),
    )(a, b)


TOK_TILE = 256


def _norm_inproj(x, ctx, n1w, mod4, w_in_b):
    l, lc = x.shape[0], ctx.shape[0]
    tm = TOK_TILE
    nct = lc // tm
    la = l + lc

    def body(x_ref, c_ref, nw_ref, mod_ref, w_ref, p_ref, h_ref):
        is_ctx = pl.program_id(0) < nct
        xt = jnp.where(is_ctx, c_ref[...], x_ref[...])
        sh = jnp.where(is_ctx, mod_ref[0:1, :], mod_ref[2:3, :])
        sc = jnp.where(is_ctx, mod_ref[1:2, :], mod_ref[3:4, :])
        hb = _rms_mod(xt, nw_ref[...], sh, sc).astype(BF16)
        h_ref[...] = hb
        p_ref[...] = _dot(hb, w_ref[...])

    return pl.pallas_call(
        body, name="norm_inproj", grid=(la // tm,),
        in_specs=[pl.BlockSpec((tm, D_MODEL), lambda i: (jnp.maximum(i - nct, 0), 0)),
                  pl.BlockSpec((tm, D_MODEL), lambda i: (jnp.minimum(i, nct - 1), 0)),
                  _full((1, D_MODEL)), _full((4, D_MODEL)), _full((D_MODEL, IN_COLS))],
        out_specs=[pl.BlockSpec((tm, IN_COLS), lambda i: (i, 0)), pl.BlockSpec((tm, D_MODEL), lambda i: (i, 0))],
        out_shape=[jax.ShapeDtypeStruct((la, IN_COLS), F32), jax.ShapeDtypeStruct((la, D_MODEL), BF16)],
        compiler_params=_params(("parallel",)),
    )(x, ctx, n1w, mod4, w_in_b)


def _norm_inproj_bwd(x, ctx, n1w, mod4, dh1, dx1):
    l, lc = x.shape[0], ctx.shape[0]
    tm = TOK_TILE
    nct = lc // tm
    la = l + lc

    def body(x_ref, c_ref, nw_ref, mod_ref, dh_ref, dx1_ref, gx_ref, acc_ref):
        i = pl.program_id(0)
        is_ctx = i < nct

        @pl.when(i == 0)
        def _():
            acc_ref[...] = jnp.zeros_like(acc_ref)

        xt = jnp.where(is_ctx, c_ref[...], x_ref[...])
        sh = jnp.where(is_ctx, mod_ref[0:1, :], mod_ref[2:3, :])
        sc = jnp.where(is_ctx, mod_ref[1:2, :], mod_ref[3:4, :])
        _, vjp = jax.vjp(_rms_mod, xt, nw_ref[...], sh, sc)
        dx, dnw, dsh, dsc = vjp(dh_ref[...])
        gx_ref[...] = dx + dx1_ref[...]
        cf = jnp.where(is_ctx, 1.0, 0.0)
        acc_ref[0:1, :] += dnw
        acc_ref[1:2, :] += cf * dsh
        acc_ref[2:3, :] += cf * dsc
        acc_ref[3:4, :] += (1.0 - cf) * dsh
        acc_ref[4:5, :] += (1.0 - cf) * dsc

    return pl.pallas_call(
        body, name="norm_inproj_bwd", grid=(la // tm,),
        in_specs=[pl.BlockSpec((tm, D_MODEL), lambda i: (jnp.maximum(i - nct, 0), 0)),
                  pl.BlockSpec((tm, D_MODEL), lambda i: (jnp.minimum(i, nct - 1), 0)),
                  _full((1, D_MODEL)), _full((4, D_MODEL)),
                  pl.BlockSpec((tm, D_MODEL), lambda i: (i, 0)),
                  pl.BlockSpec((tm, D_MODEL), lambda i: (jnp.maximum(i - nct, 0), 0))],
        out_specs=[pl.BlockSpec((tm, D_MODEL), lambda i: (jnp.maximum(i - nct, 0), 0)), _full((8, D_MODEL))],
        out_shape=[jax.ShapeDtypeStruct((l, D_MODEL), F32), jax.ShapeDtypeStruct((8, D_MODEL), F32)],
        compiler_params=_params(("arbitrary",)),
    )(x, ctx, n1w, mod4, dh1, dx1)


def _iota2(shape, dim):
    return lax.broadcasted_iota(jnp.int32, shape, dim)


def _group_mask(rows, cols, row_div, col_div):
    return jnp.where(_iota2((rows, cols), 0) // row_div == _iota2((rows, cols), 1) // col_div, 1.0, 0.0).astype(F32)


def _s5_gen_dir(lre, lim, lst, b_re, b_im, c_re, c_im):
    step = jnp.exp(lst)
    mag = jnp.exp(lre * step)
    ar = mag * jnp.cos(lim * step)
    ai = mag * jnp.sin(lim * step)
    den = lre * lre + lim * lim
    xr = ar - 1.0
    cr = (xr * lre + ai * lim) / den
    ci = (ai * lre - xr * lim) / den
    rexp = _group_mask(128, 8, S5_GROUP, 1)
    are, aie = _dot_hi(rexp, ar), _dot_hi(rexp, ai)
    cre, cie = _dot_hi(rexp, cr), _dot_hi(rexp, ci)
    bbr = cre * b_re - cie * b_im
    bbi = cre * b_im + cie * b_re
    gmask = _group_mask(128, 128, S5_GROUP, S5_GROUP)
    pr, pi = jnp.ones_like(are), jnp.zeros_like(are)
    xs, ys = [], []
    for t in range(S5_T + 1):
        if t < S5_T:
            xs.append(jnp.concatenate([bbr * pr - bbi * pi, bbr * pi + bbi * pr], axis=1))
        ys.append(jnp.concatenate([c_re * pr - c_im * pi, -(c_re * pi + c_im * pr)], axis=1))
        pr, pi = pr * are - pi * aie, pr * aie + pi * are
    gs = [_dot_nt_hi(x_t, ys[0]) * gmask for x_t in xs]
    r16, i16 = ar, ai
    for _ in range(4):
        r16, i16 = r16 * r16 - i16 * i16, 2.0 * r16 * i16
    return xs, ys, gs, jnp.concatenate([r16, i16], axis=1)


def _s5_expand(z):
    return jnp.concatenate([z] * 8, axis=1) * _group_mask(128, S5_SW, S5_GROUP, 128)


def _s5_contract(z):
    zm = z * _group_mask(128, S5_SW, S5_GROUP, 128)
    acc = zm[:, 0:128]
    for k in range(1, 8):
        acc = acc + zm[:, 128 * k:128 * (k + 1)]
    return acc


def _s5_param_specs():
    blk3 = lambda r, c: pl.BlockSpec((1, 1, r, c), lambda b, j: (0, b, 0, 0))
    dir3 = lambda r, c: pl.BlockSpec((2, 1, r, c), lambda b, j: (0, b, 0, 0))
    return [dir3(8, S5_STATE), dir3(8, S5_STATE), dir3(8, 1), blk3(128, S5_STATE), blk3(128, S5_STATE),
            blk3(128, S5_STATE), blk3(128, S5_STATE), blk3(1, 128)]


def _s5_gen(lre, lim, lst, b_re, b_im, c_re, c_im, dvec):
    def body(lre_ref, lim_ref, lst_ref, bre_ref, bim_ref, cre_ref, cim_ref, d_ref,
             kb_ref, wst_ref, wout_ref, a16_ref, x_scr, y_scr, g_scr):
        j = pl.program_id(1)

        @pl.when(j == 0)
        def _():
            eye = _group_mask(128, 128, 1, 1)
            g0 = eye * d_ref[0, 0]
            for dr in range(2):
                xs, ys, gs, a16 = _s5_gen_dir(lre_ref[dr, 0], lim_ref[dr, 0], lst_ref[dr, 0], bre_ref[0, 0],
                                              bim_ref[0, 0], cre_ref[0, 0], cim_ref[0, 0])
                a16_ref[0, dr] = a16
                for t in range(S5_T):
                    x_scr[dr, t] = xs[t]
                for t in range(S5_T + 1):
                    y_scr[dr, t] = ys[t]
                g0 = g0 + gs[0]
                for t in range(1, S5_T):
                    g_scr[(S5_T - 1) + t if dr == 0 else (S5_T - 1) - t] = gs[t]
            g_scr[S5_T - 1] = g0

        for i in range(S5_T):
            kb_ref[0, :, 128 * i:128 * (i + 1)] = g_scr[i - j + (S5_T - 1)].astype(BF16)
        wst_ref[0, 0] = _s5_expand(x_scr[0, S5_T - 1 - j]).astype(BF16)
        wst_ref[0, 1] = _s5_expand(x_scr[1, j]).astype(BF16)
        wout_ref[0, 0] = _s5_expand(y_scr[0, j + 1]).astype(BF16)
        wout_ref[0, 1] = _s5_expand(y_scr[1, S5_T - j]).astype(BF16)

    return pl.pallas_call(
        body, name="s5_gen", grid=(S5_NB, S5_T),
        in_specs=_s5_param_specs(),
        out_specs=[pl.BlockSpec((1, 128, S5_BW), lambda b, j: (b, j, 0)),
                   pl.BlockSpec((1, 2, 128, S5_SW), lambda b, j: (b, 0, j, 0)),
                   pl.BlockSpec((1, 2, 128, S5_SW), lambda b, j: (b, 0, j, 0)),
                   pl.BlockSpec((1, 2, 8, 128), lambda b, j: (b, 0, 0, 0))],
        out_shape=[jax.ShapeDtypeStruct((S5_NB, S5_BW, S5_BW), BF16),
                   jax.ShapeDtypeStruct((S5_NB, 2, S5_BW, S5_SW), BF16),
                   jax.ShapeDtypeStruct((S5_NB, 2, S5_BW, S5_SW), BF16),
                   jax.ShapeDtypeStruct((S5_NB, 2, 8, 128), F32)],
        scratch_shapes=[pltpu.VMEM((2, S5_T, 128, 128), F32), pltpu.VMEM((2, S5_T + 1, 128, 128), F32),
                        pltpu.VMEM((2 * S5_T - 1, 128, 128), F32)],
        compiler_params=_params(("parallel", "arbitrary")),
    )(lre, lim, lst, b_re, b_im, c_re, c_im, dvec)


def _s5_gen_bwd(lre, lim, lst, b_re, b_im, c_re, c_im, dvec, dkb, dwst, dwout, da16):
    def body(lre_ref, lim_ref, lst_ref, bre_ref, bim_ref, cre_ref, cim_ref, d_ref,
             dkb_ref, dwst_ref, dwout_ref, da16_ref,
             glre_ref, glim_ref, glst_ref, gbre_ref, gbim_ref, gcre_ref, gcim_ref, gd_ref,
             dx_scr, dy_scr, dg_scr):
        j = pl.program_id(1)

        @pl.when(j == 0)
        def _():
            dg_scr[...] = jnp.zeros_like(dg_scr)
            dy_scr[0, 0] = jnp.zeros((128, 128), F32)
            dy_scr[1, 0] = jnp.zeros((128, 128), F32)

        for i in range(S5_T):
            dg_scr[i - j + (S5_T - 1)] += dkb_ref[0, :, 128 * i:128 * (i + 1)]
        dx_scr[0, S5_T - 1 - j] = _s5_contract(dwst_ref[0, 0])
        dx_scr[1, j] = _s5_contract(dwst_ref[0, 1])
        dy_scr[0, j + 1] = _s5_contract(dwout_ref[0, 0])
        dy_scr[1, S5_T - j] = _s5_contract(dwout_ref[0, 1])

        @pl.when(j == S5_T - 1)
        def _():
            eye = _group_mask(128, 128, 1, 1)
            gd_ref[0, 0] = jnp.sum(dg_scr[S5_T - 1] * eye, axis=0, keepdims=True)
            gb = [None, None, None, None]
            for dr in range(2):
                args = (lre_ref[dr, 0], lim_ref[dr, 0], lst_ref[dr, 0], bre_ref[0, 0], bim_ref[0, 0],
                        cre_ref[0, 0], cim_ref[0, 0])
                _, vjp = jax.vjp(_s5_gen_dir, *args)
                dxs = [dx_scr[dr, t] for t in range(S5_T)]
                dys = [dy_scr[dr, t] for t in range(S5_T + 1)]
                dgs = [dg_scr[(S5_T - 1) + t if dr == 0 else (S5_T - 1) - t] for t in range(S5_T)]
                g = vjp((dxs, dys, dgs, da16_ref[0, dr]))
                glre_ref[dr, 0] = g[0]
                glim_ref[dr, 0] = g[1]
                glst_ref[dr, 0] = g[2]
                for q in range(4):
                    gb[q] = g[3 + q] if gb[q] is None else gb[q] + g[3 + q]
            gbre_ref[0, 0] = gb[0]
            gbim_ref[0, 0] = gb[1]
            gcre_ref[0, 0] = gb[2]
            gcim_ref[0, 0] = gb[3]

    shp = lambda a: jax.ShapeDtypeStruct(a.shape, F32)
    return pl.pallas_call(
        body, name="s5_gen_bwd", grid=(S5_NB, S5_T),
        in_specs=_s5_param_specs() + [
            pl.BlockSpec((1, 128, S5_BW), lambda b, j: (b, j, 0)),
            pl.BlockSpec((1, 2, 128, S5_SW), lambda b, j: (b, 0, j, 0)),
            pl.BlockSpec((1, 2, 128, S5_SW), lambda b, j: (b, 0, j, 0)),
            pl.BlockSpec((1, 2, 8, 128), lambda b, j: (b, 0, 0, 0))],
        out_specs=_s5_param_specs(),
        out_shape=[shp(lre), shp(lim), shp(lst), shp(b_re), shp(b_im), shp(c_re), shp(c_im), shp(dvec)],
        scratch_shapes=[pltpu.VMEM((2, S5_T, 128, 128), F32), pltpu.VMEM((2, S5_T + 1, 128, 128), F32),
                        pltpu.VMEM((2 * S5_T - 1, 128, 128), F32)],
        compiler_params=_params(("parallel", "arbitrary")),
    )(lre, lim, lst, b_re, b_im, c_re, c_im, dvec, dkb, dwst, dwout, da16)


def _s5_ucat(u_ref, lo=0, hi=S5_T):
    return jnp.concatenate([u_ref[:, j, :] for j in range(lo, hi)], axis=1).astype(BF16)


def _s5_put_groups(o_ref, dr, val):
    for gi in range(8):
        o_ref[dr, :, gi, :] = val[:, 128 * gi:128 * (gi + 1)]


def _s5_get_groups(s_ref, dr, n=8):
    return jnp.concatenate([s_ref[dr, :, gi, :] for gi in range(n)], axis=1).astype(BF16)


def _s5_state(p3, wst):
    cn = p3.shape[0]

    def body(u_ref, w_ref, o_ref):
        u = _s5_ucat(u_ref)
        _s5_put_groups(o_ref, 0, _dot(u, w_ref[0, 0]))
        _s5_put_groups(o_ref, 1, _dot(u, w_ref[0, 1]))

    return pl.pallas_call(
        body, name="s5_state", grid=(S5_NB,),
        in_specs=[pl.BlockSpec((cn, S5_T, 128), lambda b: (0, 0, b)),
                  pl.BlockSpec((1, 2, S5_BW, S5_SW), lambda b: (b, 0, 0, 0))],
        out_specs=pl.BlockSpec((2, cn, 8, 128), lambda b: (0, 0, b, 0)),
        out_shape=jax.ShapeDtypeStruct((2, cn, S5_GROUPS, 128), F32),
        compiler_params=_params(("parallel",)),
    )(p3, wst)


def _s5_a_forms(a):
    ra = pltpu.roll(a, S5_STATE, 1)
    low = _iota2(a.shape, 1) < S5_STATE
    return jnp.where(low, a, ra), jnp.where(low, -ra, a)


def _s5_scan(sloc, a16, ncc):
    cn = sloc.shape[1]

    def body(s_ref, a_ref, h_ref):
        forms = [_s5_a_forms(a_ref[dr]) for dr in range(2)]

        def step(s, hs):
            out = []
            for dr in range(2):
                arr, aii = forms[dr]
                h, rh = hs[dr]
                c = s if dr == 0 else jnp.where(s < ncc, ncc - 1 - s, cn - 1 - (s - ncc))
                h_ref[dr, c] = h
                sc = s_ref[dr, c]
                out.append((h * arr + rh * aii + sc, rh * arr - h * aii + pltpu.roll(sc, S5_STATE, 1)))
            return tuple(out)

        zero = jnp.zeros((S5_GROUPS, 128), F32)
        lax.fori_loop(0, cn, step, ((zero, zero), (zero, zero)), unroll=4)

    return pl.pallas_call(
        body, name="s5_scan",
        out_shape=jax.ShapeDtypeStruct(sloc.shape, F32),
        compiler_params=_params(),
    )(sloc, a16)


def _s5_scan_bwd(e, hs, a16, ncc):
    cn = e.shape[1]

    def body(e_ref, h_ref, a_ref, ds_ref, da_ref):
        forms = [_s5_a_forms(a_ref[dr]) for dr in range(2)]
        low = _iota2((S5_GROUPS, 128), 1) < S5_STATE

        def step(s, carry):
            out = []
            r = cn - 1 - s
            for dr in range(2):
                arr, aii = forms[dr]
                g, rg, da = carry[dr]
                c = r if dr == 0 else jnp.where(r < ncc, ncc - 1 - r, cn - 1 - (r - ncc))
                ds_ref[dr, c] = g
                h = h_ref[dr, c]
                rh = pltpu.roll(h, S5_STATE, 1)
                da = da + jnp.where(low, g * h + rg * rh, g * rh - rg * h)
                ec = e_ref[dr, c]
                out.append((ec + g * arr - rg * aii, pltpu.roll(ec, S5_STATE, 1) + rg * arr + g * aii, da))
            return tuple(out)

        zero = jnp.zeros((S5_GROUPS, 128), F32)
        res = lax.fori_loop(0, cn, step, ((zero, zero, zero), (zero, zero, zero)), unroll=4)
        da_ref[0] = res[0][2]
        da_ref[1] = res[1][2]

    return pl.pallas_call(
        body, name="s5_scan_bwd",
        out_shape=[jax.ShapeDtypeStruct(e.shape, F32), jax.ShapeDtypeStruct((2, S5_GROUPS, 128), F32)],
        compiler_params=_params(),
    )(e, hs, a16)


def _s5_out(p3, kb, h2, wout):
    cn = p3.shape[0]
    half = S5_T // 2

    def body(u_ref, k_ref, h_ref, w_ref, y_ref):
        u = _s5_ucat(u_ref)
        y = _dot(u, k_ref[0])
        y = y + _dot_nt(_s5_get_groups(h_ref, 0), w_ref[0, 0])
        y = y + _dot_nt(_s5_get_groups(h_ref, 1), w_ref[0, 1])
        for i in range(half):
            y_ref[:, i, :] = y[:, 128 * i:128 * (i + 1)]

    return pl.pallas_call(
        body, name="s5_out", grid=(S5_NB, 2),
        in_specs=[pl.BlockSpec((cn, S5_T, 128), lambda b, q: (0, 0, b)),
                  pl.BlockSpec((1, S5_BW, S5_BW // 2), lambda b, q: (b, 0, q)),
                  pl.BlockSpec((2, cn, 8, 128), lambda b, q: (0, 0, b, 0)),
                  pl.BlockSpec((1, 2, S5_BW // 2, S5_SW), lambda b, q: (b, 0, q, 0))],
        out_specs=pl.BlockSpec((cn, half, 128), lambda b, q: (0, q, b)),
        out_shape=jax.ShapeDtypeStruct((cn, S5_T, S5_WIDTH), F32),
        compiler_params=_params(("parallel", "parallel")),
    )(p3, kb, h2, wout)


def _s5_bwd_h(dy3, wout):
    cn = dy3.shape[0]

    def body(d_ref, w_ref, e_ref):
        d = _s5_ucat(d_ref)
        _s5_put_groups(e_ref, 0, _dot(d, w_ref[0, 0]))
        _s5_put_groups(e_ref, 1, _dot(d, w_ref[0, 1]))

    return pl.pallas_call(
        body, name="s5_bwd_h", grid=(S5_NB,),
        in_specs=[pl.BlockSpec((cn, S5_T, 128), lambda b: (0, 0, b)),
                  pl.BlockSpec((1, 2, S5_BW, S5_SW), lambda b: (b, 0, 0, 0))],
        out_specs=pl.BlockSpec((2, cn, 8, 128), lambda b: (0, 0, b, 0)),
        out_shape=jax.ShapeDtypeStruct((2, cn, S5_GROUPS, 128), F32),
        compiler_params=_params(("parallel",)),
    )(dy3, wout)


def _s5_bwd_u(dy3, kb, ds2, wst):
    cn = dy3.shape[0]
    half = S5_T // 2

    def body(d_ref, k_ref, s_ref, w_ref, o_ref):
        d = _s5_ucat(d_ref)
        du = _dot_nt(d, k_ref[0])
        du = du + _dot_nt(_s5_get_groups(s_ref, 0), w_ref[0, 0])
        du = du + _dot_nt(_s5_get_groups(s_ref, 1), w_ref[0, 1])
        for j in range(half):
            o_ref[:, j, :] = du[:, 128 * j:128 * (j + 1)]

    return pl.pallas_call(
        body, name="s5_bwd_u", grid=(S5_NB, 2),
        in_specs=[pl.BlockSpec((cn, S5_T, 128), lambda b, q: (0, 0, b)),
                  pl.BlockSpec((1, S5_BW // 2, S5_BW), lambda b, q: (b, q, 0)),
                  pl.BlockSpec((2, cn, 8, 128), lambda b, q: (0, 0, b, 0)),
                  pl.BlockSpec((1, 2, S5_BW // 2, S5_SW), lambda b, q: (b, 0, q, 0))],
        out_specs=pl.BlockSpec((cn, half, 128), lambda b, q: (0, q, b)),
        out_shape=jax.ShapeDtypeStruct((cn, S5_T, S5_WIDTH), F32),
        compiler_params=_params(("parallel", "parallel")),
    )(dy3, kb, ds2, wst)


def _s5_bwd_kb(p3, dy3):
    cn = p3.shape[0]
    half = S5_T // 2

    def body(u_ref, d_ref, o_ref):
        o_ref[0] = _dot_tn(_s5_ucat(u_ref), _s5_ucat(d_ref, 0, half))

    return pl.pallas_call(
        body, name="s5_bwd_kb", grid=(S5_NB, 2),
        in_specs=[pl.BlockSpec((cn, S5_T, 128), lambda b, q: (0, 0, b)),
                  pl.BlockSpec((cn, half, 128), lambda b, q: (0, q, b))],
        out_specs=pl.BlockSpec((1, S5_BW, S5_BW // 2), lambda b, q: (b, 0, q)),
        out_shape=jax.ShapeDtypeStruct((S5_NB, S5_BW, S5_BW), F32),
        compiler_params=_params(("parallel", "parallel")),
    )(p3, dy3)


def _s5_bwd_w(u3, st, name):
    cn = u3.shape[0]

    def body(u_ref, s_ref, w_ref):
        w_ref[0, 0] = _dot_tn(_s5_ucat(u_ref), _s5_get_groups(s_ref, 0))

    return pl.pallas_call(
        body, name=name, grid=(S5_NB, 2),
        in_specs=[pl.BlockSpec((cn, S5_T, 128), lambda b, q: (0, 0, b)),
                  pl.BlockSpec((1, cn, 8, 128), lambda b, q: (q, 0, b, 0))],
        out_specs=pl.BlockSpec((1, 1, S5_BW, S5_SW), lambda b, q: (b, q, 0, 0)),
        out_shape=jax.ShapeDtypeStruct((S5_NB, 2, S5_BW, S5_SW), F32),
        compiler_params=_params(("parallel", "parallel")),
    )(u3, st)


def _s5_glu(y_all, w_glu_b, b_glu, nct):
    la = y_all.shape[0]
    tm = TOK_TILE
    l = la - nct * tm

    def body(y_ref, w_ref, b_ref, o_ref):
        yg = _gelu(y_ref[...])
        z = _dot(yg.astype(BF16), w_ref[...]) + b_ref[...]
        o_ref[...] = (yg * _sigmoid(z)).astype(BF16)

    return pl.pallas_call(
        body, name="s5_glu", grid=(l // tm,),
        in_specs=[pl.BlockSpec((tm, S5_WIDTH), lambda i: (i + nct, 0)),
                  _full((S5_WIDTH, S5_WIDTH)), _full((1, S5_WIDTH))],
        out_specs=pl.BlockSpec((tm, S5_WIDTH), lambda i: (i, 0)),
        out_shape=jax.ShapeDtypeStruct((l, S5_WIDTH), BF16),
        compiler_params=_params(("parallel",)),
    )(y_all, w_glu_b, b_glu)


def _s5_glu_bwd(y_all, dmix, w_glu_b, b_glu, nct):
    la = y_all.shape[0]
    tm = TOK_TILE

    def body(y_ref, d_ref, w_ref, b_ref, dy_ref, gw_ref, gb_ref):
        i = pl.program_id(0)

        @pl.when(i == 0)
        def _():
            gw_ref[...] = jnp.zeros_like(gw_ref)
            gb_ref[...] = jnp.zeros_like(gb_ref)

        @pl.when(i < nct)
        def _():
            dy_ref[...] = jnp.zeros_like(dy_ref)

        @pl.when(i >= nct)
        def _():
            y = y_ref[...]
            yg, gelu_vjp = jax.vjp(_gelu, y)
            ygb = yg.astype(BF16)
            sg = _sigmoid(_dot(ygb, w_ref[...]) + b_ref[...])
            ds = d_ref[...]
            dz = ds * yg * sg * (1.0 - sg)
            dzb = dz.astype(BF16)
            dyg = ds * sg + _dot_nt(dzb, w_ref[...])
            dy_ref[...] = gelu_vjp(dyg)[0]
            gw_ref[...] += _dot_tn(ygb, dzb)
            gb_ref[...] += jnp.sum(dz, axis=0, keepdims=True)

    return pl.pallas_call(
        body, name="s5_glu_bwd", grid=(la // tm,),
        in_specs=[pl.BlockSpec((tm, S5_WIDTH), lambda i: (i, 0)),
                  pl.BlockSpec((tm, S5_WIDTH), lambda i: (jnp.maximum(i - nct, 0), 0)),
                  _full((S5_WIDTH, S5_WIDTH)), _full((1, S5_WIDTH))],
        out_specs=[pl.BlockSpec((tm, S5_WIDTH), lambda i: (i, 0)), _full((S5_WIDTH, S5_WIDTH)),
                   _full((1, S5_WIDTH))],
        out_shape=[jax.ShapeDtypeStruct((la, S5_WIDTH), F32), jax.ShapeDtypeStruct((S5_WIDTH, S5_WIDTH), F32),
                   jax.ShapeDtypeStruct((1, S5_WIDTH), F32)],
        compiler_params=_params(("arbitrary",)),
    )(y_all, dmix, w_glu_b, b_glu)


K_SCALE = RET_DH ** -0.5
Q_COL, K_COL, V_COL, G_COL = 4, 8, 12, 16


def _ret_chunk_of(step, ncc, nch, rev):
    if not rev:
        return step
    return jnp.where(step < ncc, ncc - 1 - step, nch - 1 - (step - ncc))


def _ret_decay(ld, rev):
    c = _iota2((RET_CHUNK, RET_CHUNK), 0).astype(F32)
    m = _iota2((RET_CHUNK, RET_CHUNK), 1).astype(F32)
    diff = (m - c) if rev else (c - m)
    keep = (diff > 0) if rev else (diff >= 0)
    expo = jnp.maximum(diff, 0.0)
    dm = jnp.where(keep, jnp.exp(ld * expo), 0.0)
    xi_e = (RET_CHUNK - c) if rev else (c + 1.0)
    zeta_e = c if rev else (RET_CHUNK - 1.0 - c)
    return dm, expo, jnp.exp(ld * xi_e), xi_e, jnp.exp(ld * zeta_e), zeta_e


RET_TABLES = 7


def _ret_tables(ld2):
    def body(ld_ref, t_ref):
        dr, h = pl.program_id(0), pl.program_id(1)
        ldh = ld_ref[dr, h]
        for rev in (False, True):
            @pl.when(dr == int(rev))
            def _(rev=rev):
                dm, expo, xi, xi_e, zeta, zeta_e = _ret_decay(ldh, rev)
                t_ref[0, 0, 0] = dm
                t_ref[0, 0, 1] = dm * expo
                t_ref[0, 0, 2] = xi
                t_ref[0, 0, 3] = xi * xi_e
                t_ref[0, 0, 4] = zeta
                t_ref[0, 0, 5] = zeta * zeta_e
                t_ref[0, 0, 6] = jnp.zeros_like(dm) + jnp.exp(ldh * RET_CHUNK)

    return pl.pallas_call(
        body, name="ret_tables", grid=(2, RET_HEADS),
        in_specs=[pl.BlockSpec(memory_space=pltpu.SMEM)],
        out_specs=pl.BlockSpec((1, 1, RET_TABLES, RET_CHUNK, RET_CHUNK), lambda d, h: (d, h, 0, 0, 0)),
        out_shape=jax.ShapeDtypeStruct((2, RET_HEADS, RET_TABLES, RET_CHUNK, RET_CHUNK), F32),
        compiler_params=_params(("parallel", "parallel")),
    )(ld2)


def _ret_specs(nch, ncc, rev, step_of):
    chunk = lambda n: _ret_chunk_of(step_of(n), ncc, nch, rev)
    cols = [pl.BlockSpec((RET_CHUNK, RET_WIDTH), functools.partial(lambda n, cb: (chunk(n), cb), cb=cb))
            for cb in (1, 2, 3)]
    tab = pl.BlockSpec((RET_CHUNK, RET_DH), lambda n: (chunk(n), 0))
    return cols + [tab, tab], pl.BlockSpec((RET_CHUNK, RET_WIDTH), lambda n: (chunk(n), 0))


def _ret_scan(p_all, cos_t, sin_t, tabs, ncc):
    la = p_all.shape[0]
    nch = la // RET_CHUNK

    def body(t_ref, qf, kf, vf, cf, sf, qb, kb, vb, cb, sb, of_ref, ob_ref, ssf_ref, ssb_ref, s_scr):
        @pl.when(pl.program_id(0) == 0)
        def _():
            s_scr[...] = jnp.zeros_like(s_scr)

        for dr, (q_ref, k_ref, v_ref, c_ref, n_ref, o_ref, ss_ref) in enumerate(
                ((qf, kf, vf, cf, sf, of_ref, ssf_ref), (qb, kb, vb, cb, sb, ob_ref, ssb_ref))):
            cs, sn = c_ref[...], n_ref[...]
            for h in range(RET_HEADS):
                sl = slice(RET_DH * h, RET_DH * (h + 1))
                dm, xi, zeta = t_ref[dr, h, 0], t_ref[dr, h, 2], t_ref[dr, h, 4]
                q = _rope(q_ref[:, sl], cs, sn)
                k = _rope(k_ref[:, sl] * K_SCALE, cs, sn)
                vh = v_ref[:, sl].astype(BF16)
                s = s_scr[dr, h]
                ss_ref[0, h] = s
                sc = (_dot_nt(q.astype(BF16), k.astype(BF16)) * dm).astype(BF16)
                o_ref[:, sl] = _dot(sc, vh) + _dot((q * xi).astype(BF16), s.astype(BF16))
                s_scr[dr, h] = t_ref[dr, h, 6] * s + _dot_tn((k * zeta).astype(BF16), vh)

    in_f, out_f = _ret_specs(nch, ncc, False, lambda n: n)
    in_b, out_b = _ret_specs(nch, ncc, True, lambda n: n)
    ss_spec = pl.BlockSpec((1, RET_HEADS, RET_DH, RET_DH), lambda n: (n, 0, 0, 0))
    o_shape = jax.ShapeDtypeStruct((la, RET_WIDTH), F32)
    ss_shape = jax.ShapeDtypeStruct((nch, RET_HEADS, RET_DH, RET_DH), F32)
    return pl.pallas_call(
        body, name="ret_scan", grid=(nch,),
        in_specs=[_full(tabs.shape)] + in_f + in_b,
        out_specs=[out_f, out_b, ss_spec, ss_spec],
        out_shape=[o_shape, o_shape, ss_shape, ss_shape],
        scratch_shapes=[pltpu.VMEM((2, RET_HEADS, RET_DH, RET_DH), F32)],
        compiler_params=_params(("arbitrary",)),
    )(tabs, p_all, p_all, p_all, cos_t, sin_t, p_all, p_all, p_all, cos_t, sin_t)


def _ret_scan_bwd(p_all, cos_t, sin_t, tabs, ssf, ssb, dy_all, ncc):
    la = p_all.shape[0]
    nch = la // RET_CHUNK

    def body(t_ref, qf, kf, vf, cf, sf, dof, ssf_ref, qb, kb, vb, cb, sb, dob_, ssb_ref,
             dqf, dkf, dvf, dqb, dkb, dvb, dld_ref, ds_scr):
        @pl.when(pl.program_id(0) == 0)
        def _():
            ds_scr[...] = jnp.zeros_like(ds_scr)
            dld_ref[...] = jnp.zeros_like(dld_ref)

        for dr, (q_ref, k_ref, v_ref, c_ref, n_ref, do_ref, ss_ref, dq_ref, dk_ref, dv_ref) in enumerate(
                ((qf, kf, vf, cf, sf, dof, ssf_ref, dqf, dkf, dvf), (qb, kb, vb, cb, sb, dob_, ssb_ref, dqb, dkb, dvb))):
            cs, sn = c_ref[...], n_ref[...]
            for h in range(RET_HEADS):
                sl = slice(RET_DH * h, RET_DH * (h + 1))
                dm, dm_d, xi, xi_d, zeta, zeta_d, gc = [t_ref[dr, h, t] for t in range(RET_TABLES)]
                q = _rope(q_ref[:, sl], cs, sn)
                k = _rope(k_ref[:, sl] * K_SCALE, cs, sn)
                q16, k16, v16 = q.astype(BF16), k.astype(BF16), v_ref[:, sl].astype(BF16)
                s = ss_ref[0, h]
                s16 = s.astype(BF16)
                ds_in = ds_scr[dr, h]
                ds16 = ds_in.astype(BF16)
                do16 = do_ref[:, sl].astype(BF16)
                qk = _dot_nt(q16, k16)
                dsv = _dot_nt(do16, v16)
                dsc = (dsv * dm).astype(BF16)
                sc16 = (qk * dm).astype(BF16)
                dos = _dot_nt(do16, s16)
                vds = _dot_nt(v16, ds16)
                dq_ref[:, sl] = _dot(dsc, k16) + dos * xi
                dk_ref[:, sl] = _dot_tn(dsc, q16) + vds * zeta
                dv_ref[:, sl] = _dot_tn(sc16, do16) + _dot((k * zeta).astype(BF16), ds16)
                ds_scr[dr, h] = _dot_tn((q * xi).astype(BF16), do16) + gc * ds_in
                dld = jnp.sum(dsv * qk * dm_d + q * dos * xi_d + k * vds * zeta_d + RET_CHUNK * gc * s * ds_in)
                dld_ref[dr, h] += dld

    back = lambda n: nch - 1 - n
    in_f, out_f = _ret_specs(nch, ncc, False, back)
    in_b, out_b = _ret_specs(nch, ncc, True, back)
    ss_spec = pl.BlockSpec((1, RET_HEADS, RET_DH, RET_DH), lambda n: (nch - 1 - n, 0, 0, 0))
    shp = jax.ShapeDtypeStruct((la, RET_WIDTH), F32)
    return pl.pallas_call(
        body, name="ret_scan_bwd", grid=(nch,),
        in_specs=[_full(tabs.shape)] + in_f + [out_f, ss_spec] + in_b + [out_b, ss_spec],
        out_specs=[out_f, out_f, out_f, out_b, out_b, out_b, _full((2, RET_HEADS, 8, 128))],
        out_shape=[shp] * 6 + [jax.ShapeDtypeStruct((2, RET_HEADS, 8, 128), F32)],
        scratch_shapes=[pltpu.VMEM((2, RET_HEADS, RET_DH, RET_DH), F32)],
        compiler_params=_params(("arbitrary",)),
    )(tabs, p_all, p_all, p_all, cos_t, sin_t, dy_all, ssf, p_all, p_all, p_all, cos_t, sin_t, dy_all, ssb)


def _ret_gate(of, ob, p_all, nct):
    la = of.shape[0]
    tm = TOK_TILE
    l = la - nct * tm

    def body(of_ref, ob_ref, g_ref, r_ref, y_ref):
        y = of_ref[...] + ob_ref[...]
        y_ref[...] = y
        for h in range(RET_HEADS):
            sl = slice(RET_DH * h, RET_DH * (h + 1))
            r_ref[:, sl] = _head_norm_gate(y[:, sl], g_ref[:, sl]).astype(BF16)

    row = pl.BlockSpec((tm, RET_WIDTH), lambda i: (i + nct, 0))
    out = pl.BlockSpec((tm, RET_WIDTH), lambda i: (i, 0))
    return pl.pallas_call(
        body, name="ret_gate", grid=(l // tm,),
        in_specs=[row, row, pl.BlockSpec((tm, RET_WIDTH), lambda i: (i + nct, G_COL // 4))],
        out_specs=[out, out],
        out_shape=[jax.ShapeDtypeStruct((l, RET_WIDTH), BF16), jax.ShapeDtypeStruct((l, RET_WIDTH), F32)],
        compiler_params=_params(("parallel",)),
    )(of, ob, p_all)


def _ret_gate_bwd(y_ret, p_all, dmix, nct):
    la = p_all.shape[0]
    tm = TOK_TILE

    def body(y_ref, g_ref, d_ref, dy_ref, dg_ref):
        i = pl.program_id(0)

        @pl.when(i < nct)
        def _():
            dy_ref[...] = jnp.zeros_like(dy_ref)
            dg_ref[...] = jnp.zeros_like(dg_ref)

        @pl.when(i >= nct)
        def _():
            for h in range(RET_HEADS):
                sl = slice(RET_DH * h, RET_DH * (h + 1))
                _, vjp = jax.vjp(_head_norm_gate, y_ref[:, sl], g_ref[:, sl])
                dy, dg = vjp(d_ref[:, sl])
                dy_ref[:, sl] = dy
                dg_ref[:, sl] = dg

    xrow = lambda cb: pl.BlockSpec((tm, RET_WIDTH), lambda i: (jnp.maximum(i - nct, 0), cb))
    out = pl.BlockSpec((tm, RET_WIDTH), lambda i: (i, 0))
    shp = jax.ShapeDtypeStruct((la, RET_WIDTH), F32)
    return pl.pallas_call(
        body, name="ret_gate_bwd", grid=(la // tm,),
        in_specs=[xrow(0), pl.BlockSpec((tm, RET_WIDTH), lambda i: (i, G_COL // 4)), xrow(1)],
        out_specs=[out, out], out_shape=[shp, shp],
        compiler_params=_params(("parallel",)),
    )(y_ret, p_all, dmix)


def _ret_qkv_grad(dqf, dkf, dvf, dqb, dkb, dvb, du, dg, cos_t, sin_t):
    la = dqf.shape[0]
    tm = TOK_TILE

    def body(dqf_ref, dkf_ref, dvf_ref, dqb_ref, dkb_ref, dvb_ref, du_ref, dg_ref, cos_ref, sin_ref, dp_ref):
        cs, sn = cos_ref[...], sin_ref[...]
        dp_ref[:, 0:S5_WIDTH] = du_ref[...].astype(BF16)
        for h in range(RET_HEADS):
            sl = slice(RET_DH * h, RET_DH * (h + 1))
            dq = _rope_t(dqf_ref[:, sl] + dqb_ref[:, sl], cs, sn)
            dk = _rope_t(dkf_ref[:, sl] + dkb_ref[:, sl], cs, sn) * K_SCALE
            dp_ref[:, 128 * (Q_COL + h):128 * (Q_COL + h + 1)] = dq.astype(BF16)
            dp_ref[:, 128 * (K_COL + h):128 * (K_COL + h + 1)] = dk.astype(BF16)
        dp_ref[:, 128 * V_COL:128 * G_COL] = (dvf_ref[...] + dvb_ref[...]).astype(BF16)
        dp_ref[:, 128 * G_COL:IN_COLS] = dg_ref[...].astype(BF16)

    row = pl.BlockSpec((tm, RET_WIDTH), lambda i: (i, 0))
    tab = pl.BlockSpec((tm, RET_DH), lambda i: (i, 0))
    return pl.pallas_call(
        body, name="ret_qkv_grad", grid=(la // tm,),
        in_specs=[row] * 8 + [tab, tab],
        out_specs=pl.BlockSpec((tm, IN_COLS), lambda i: (i, 0)),
        out_shape=jax.ShapeDtypeStruct((la, IN_COLS), BF16),
        compiler_params=_params(("parallel",)),
    )(dqf, dkf, dvf, dqb, dkb, dvb, du, dg, cos_t, sin_t)


def _outproj_up(x, s5x, retx, w_out_b, mod3, n2w, w_up_b):
    l = x.shape[0]
    tm = TOK_TILE

    def body(x_ref, s_ref, r_ref, wo_ref, mod_ref, nw_ref, wu_ref, x1_ref, mix_ref, h2_ref, up_ref):
        mix = _dot(s_ref[...], wo_ref[0:S5_WIDTH, :]) + _dot(r_ref[...], wo_ref[S5_WIDTH:D_MODEL, :])
        mix_ref[...] = mix
        x1 = x_ref[...] + mod_ref[0:1, :] * mix
        x1_ref[...] = x1
        h2 = _rms_mod(x1, nw_ref[...], mod_ref[1:2, :], mod_ref[2:3, :]).astype(BF16)
        h2_ref[...] = h2
        up_ref[...] = _dot(h2, wu_ref[...])

    row = lambda w: pl.BlockSpec((tm, w), lambda i: (i, 0))
    return pl.pallas_call(
        body, name="outproj_up", grid=(l // tm,),
        in_specs=[row(D_MODEL), row(S5_WIDTH), row(RET_WIDTH), _full((D_MODEL, D_MODEL)), _full((3, D_MODEL)),
                  _full((1, D_MODEL)), _full((D_MODEL, 2 * D_FF))],
        out_specs=[row(D_MODEL), row(D_MODEL), row(D_MODEL), row(2 * D_FF)],
        out_shape=[jax.ShapeDtypeStruct((l, D_MODEL), F32), jax.ShapeDtypeStruct((l, D_MODEL), F32),
                   jax.ShapeDtypeStruct((l, D_MODEL), BF16), jax.ShapeDtypeStruct((l, 2 * D_FF), F32)],
        compiler_params=_params(("parallel",)),
    )(x, s5x, retx, w_out_b, mod3, n2w, w_up_b)


HALO = 8


def _conv_taps(g, prev_row, next_row):
    t = g.shape[0]
    r = _iota2(g.shape, 0)
    gprev = jnp.where(r == 0, prev_row, pltpu.roll(g, 1, 0))
    gnext = jnp.where(r == t - 1, next_row, pltpu.roll(g, t - 1, 0))
    return gprev, gnext


def _ffn_loss(up, x1, conv_w, conv_b, w_down_b, gate, fnw, tgt):
    l = x1.shape[0]
    tm = TOK_TILE
    nt = l // tm
    hb = tm // HALO

    def body(up_a, up_g, hp_ref, hn_ref, x1_ref, cw_ref, cb_ref, wd_ref, gate_ref, fn_ref, tgt_ref,
             act_ref, dx2_ref, ddn_ref, acc_ref):
        i = pl.program_id(0)

        @pl.when(i == 0)
        def _():
            acc_ref[...] = jnp.zeros_like(acc_ref)

        g = up_g[...]
        prev_row = jnp.where(i == 0, 0.0, hp_ref[HALO - 1:HALO, :])
        next_row = jnp.where(i == nt - 1, 0.0, hn_ref[0:1, :])
        gprev, gnext = _conv_taps(g, prev_row, next_row)
        gc = cb_ref[...] + gprev * cw_ref[0:1, :] + g * cw_ref[1:2, :] + gnext * cw_ref[2:3, :]
        act = (_gelu(gc) * up_a[...]).astype(BF16)
        act_ref[...] = act
        dn = _dot(act, wd_ref[...])
        x2 = x1_ref[...] + gate_ref[...] * dn
        y, vjp = jax.vjp(_rms, x2, fn_ref[...])
        err = y - tgt_ref[...]
        dx2, dfn = vjp(err * (1.0 / D_MODEL))
        dx2_ref[...] = dx2
        ddn_ref[...] = (dx2 * gate_ref[...]).astype(BF16)
        acc_ref[0:1, :] += dfn
        acc_ref[1:2, :] += jnp.sum(dx2 * dn, axis=0, keepdims=True)
        acc_ref[2:3, :] += (0.5 / D_MODEL) * jnp.sum(err * err)

    row = lambda w: pl.BlockSpec((tm, w), lambda i: (i, 0))
    last = l // HALO - 1
    return pl.pallas_call(
        body, name="ffn_loss", grid=(nt,),
        in_specs=[pl.BlockSpec((tm, D_FF), lambda i: (i, 0)), pl.BlockSpec((tm, D_FF), lambda i: (i, 1)),
                  pl.BlockSpec((HALO, D_FF), lambda i: (jnp.maximum(i * hb - 1, 0), 1)),
                  pl.BlockSpec((HALO, D_FF), lambda i: (jnp.minimum((i + 1) * hb, last), 1)),
                  row(D_MODEL), _full((3, D_FF)), _full((1, D_FF)), _full((D_FF, D_MODEL)),
                  _full((1, D_MODEL)), _full((1, D_MODEL)), row(D_MODEL)],
        out_specs=[row(D_FF), row(D_MODEL), row(D_MODEL), _full((8, D_MODEL))],
        out_shape=[jax.ShapeDtypeStruct((l, D_FF), BF16), jax.ShapeDtypeStruct((l, D_MODEL), F32),
                   jax.ShapeDtypeStruct((l, D_MODEL), BF16), jax.ShapeDtypeStruct((8, D_MODEL), F32)],
        compiler_params=_params(("arbitrary",)),
    )(up, up, up, up, x1, conv_w, conv_b, w_down_b, gate, fnw, tgt)


def _convglu_bwd(up, dact, conv_w, conv_b):
    l = up.shape[0]
    tm = 128
    nt = l // tm
    hb = tm // HALO
    te = tm + 2 * HALO

    def body(a_ref, ap_ref, an_ref, g_ref, gp_ref, gn_ref, d_ref, dp_ref, dn_ref, cw_ref, cb_ref,
             dup_ref, acc_ref):
        i = pl.program_id(0)

        @pl.when(i == 0)
        def _():
            acc_ref[...] = jnp.zeros_like(acc_ref)

        row = _iota2((te, D_FF), 0) + (i * tm - HALO)
        valid = (row >= 0) & (row < l)

        def ext(p, c, n):
            return jnp.where(valid, jnp.concatenate([p[...], c[...], n[...]], axis=0), 0.0)

        ae, ge, de = ext(ap_ref, a_ref, an_ref), ext(gp_ref, g_ref, gn_ref), ext(dp_ref, d_ref, dn_ref)
        gprev = pltpu.roll(ge, 1, 0)
        gnext = pltpu.roll(ge, te - 1, 0)
        w0, w1, w2 = cw_ref[0:1, :], cw_ref[1:2, :], cw_ref[2:3, :]
        gce = cb_ref[...] + gprev * w0 + ge * w1 + gnext * w2
        _, vjp = jax.vjp(lambda a, gc: _gelu(gc) * a, ae, gce)
        dae, dgce = vjp(de)
        dge = dgce * w1 + pltpu.roll(dgce, te - 1, 0) * w0 + pltpu.roll(dgce, 1, 0) * w2
        mid = slice(HALO, HALO + tm)
        dup_ref[:, 0:D_FF] = dae[mid].astype(BF16)
        dup_ref[:, D_FF:2 * D_FF] = dge[mid].astype(BF16)
        dgc = dgce[mid]
        acc_ref[0:1, :] += jnp.sum(dgc * gprev[mid], axis=0, keepdims=True)
        acc_ref[1:2, :] += jnp.sum(dgc * ge[mid], axis=0, keepdims=True)
        acc_ref[2:3, :] += jnp.sum(dgc * gnext[mid], axis=0, keepdims=True)
        acc_ref[3:4, :] += jnp.sum(dgc, axis=0, keepdims=True)

    last = l // HALO - 1

    def trio(cb):
        return [pl.BlockSpec((tm, D_FF), lambda i: (i, cb)),
                pl.BlockSpec((HALO, D_FF), lambda i: (jnp.maximum(i * hb - 1, 0), cb)),
                pl.BlockSpec((HALO, D_FF), lambda i: (jnp.minimum((i + 1) * hb, last), cb))]

    return pl.pallas_call(
        body, name="convglu_bwd", grid=(nt,),
        in_specs=trio(0) + trio(1) + trio(0) + [_full((3, D_FF)), _full((1, D_FF))],
        out_specs=[pl.BlockSpec((tm, 2 * D_FF), lambda i: (i, 0)), _full((8, D_FF))],
        out_shape=[jax.ShapeDtypeStruct((l, 2 * D_FF), BF16), jax.ShapeDtypeStruct((8, D_FF), F32)],
        compiler_params=_params(("arbitrary",)),
    )(up, up, up, up, up, up, dact, dact, dact, conv_w, conv_b)


def _norm2_bwd(x1, dh2, dx2, mix, mod3, n2w):
    l = x1.shape[0]
    tm = TOK_TILE

    def body(x1_ref, dh_ref, dx2_ref, mix_ref, mod_ref, nw_ref, dx1_ref, dmix_ref, acc_ref):
        @pl.when(pl.program_id(0) == 0)
        def _():
            acc_ref[...] = jnp.zeros_like(acc_ref)

        _, vjp = jax.vjp(_rms_mod, x1_ref[...], nw_ref[...], mod_ref[1:2, :], mod_ref[2:3, :])
        dx, dnw, dsh, dsc = vjp(dh_ref[...])
        dx1 = dx + dx2_ref[...]
        dx1_ref[...] = dx1
        dmix_ref[...] = (dx1 * mod_ref[0:1, :]).astype(BF16)
        acc_ref[0:1, :] += dnw
        acc_ref[1:2, :] += jnp.sum(dx1 * mix_ref[...], axis=0, keepdims=True)
        acc_ref[2:3, :] += dsh
        acc_ref[3:4, :] += dsc

    row = pl.BlockSpec((tm, D_MODEL), lambda i: (i, 0))
    return pl.pallas_call(
        body, name="norm2_bwd", grid=(l // tm,),
        in_specs=[row, row, row, row, _full((3, D_MODEL)), _full((1, D_MODEL))],
        out_specs=[row, row, _full((8, D_MODEL))],
        out_shape=[jax.ShapeDtypeStruct((l, D_MODEL), F32), jax.ShapeDtypeStruct((l, D_MODEL), BF16),
                   jax.ShapeDtypeStruct((8, D_MODEL), F32)],
        compiler_params=_params(("arbitrary",)),
    )(x1, dh2, dx2, mix, mod3, n2w)


MOD_ROWS = 16
MOD_COLS = 6 * D_MODEL // 4


def _mod_fwd(c_all, c_ctx, w_mod_b, b_loc):
    def body(c_ref, cc_ref, w_ref, b_ref, m_ref, s_ref):
        cond = jnp.concatenate([c_ref[...], jnp.broadcast_to(cc_ref[...], (8, D_MODEL))], axis=0)
        s = _silu(cond).astype(BF16)
        s_ref[...] = s
        m_ref[...] = _dot(s, w_ref[...]) + b_ref[...]

    return pl.pallas_call(
        body, name="mod_fwd",
        out_shape=[jax.ShapeDtypeStruct((MOD_ROWS, MOD_COLS), F32), jax.ShapeDtypeStruct((MOD_ROWS, D_MODEL), BF16)],
        compiler_params=_params(),
    )(c_all, c_ctx, w_mod_b, b_loc)


def _mod_bwd_sum(dm_all):
    def body(d_ref, dm_ref, gb_ref):
        rows = [d_ref[k, 0:1, :] for k in range(8)]
        ctx_sum = d_ref[0, 1:2, :]
        for k in range(1, 8):
            ctx_sum = ctx_sum + d_ref[k, 1:2, :]
        gb = ctx_sum
        for k in range(8):
            gb = gb + rows[k]
        gb_ref[...] = gb
        dm_ref[...] = jnp.concatenate(rows + [ctx_sum] + [jnp.zeros((7, 6 * D_MODEL), F32)], axis=0)

    return pl.pallas_call(
        body, name="mod_bwd_sum",
        out_shape=[jax.ShapeDtypeStruct((MOD_ROWS, 6 * D_MODEL), F32), jax.ShapeDtypeStruct((1, 6 * D_MODEL), F32)],
        compiler_params=_params(),
    )(dm_all)


def _mod_bwd_w(dm_loc, s_b, c_ctx, w_mod_b):
    def body(d_ref, s_ref, cc_ref, w_ref, gw_ref, gc_ref):
        db = d_ref[...].astype(BF16)
        gw_ref[...] = _dot_tn(s_ref[...], db)
        ds = _dot_nt(db, w_ref[...])
        _, vjp = jax.vjp(_silu, cc_ref[...])
        gc_ref[...] = jnp.broadcast_to(vjp(ds[8:9, :])[0], (8, D_MODEL))

    return pl.pallas_call(
        body, name="mod_bwd_w",
        out_shape=[jax.ShapeDtypeStruct((D_MODEL, MOD_COLS), F32), jax.ShapeDtypeStruct((8, D_MODEL), F32)],
        compiler_params=_params(),
    )(dm_loc, s_b, c_ctx, w_mod_b)


def _adamw(w, g, m, v, name):
    r, c = w.shape
    tr = _pick(r, (256, 128, 64, 32, 16, 8))
    bc1 = 1.0 - ADAM_B1 ** ADAM_STEP
    bc2 = 1.0 - ADAM_B2 ** ADAM_STEP

    def body(w_ref, g_ref, m_ref, v_ref, d_ref, nm_ref, nv_ref):
        gg = g_ref[...]
        nm = ADAM_B1 * m_ref[...] + (1.0 - ADAM_B1) * gg
        nv = ADAM_B2 * v_ref[...] + (1.0 - ADAM_B2) * (gg * gg)
        nm_ref[...] = nm
        nv_ref[...] = nv
        d_ref[...] = -ADAM_LR * ((nm / bc1) / (jnp.sqrt(nv / bc2) + ADAM_EPS) + ADAM_WD * w_ref[...])

    blk = pl.BlockSpec((tr, c), lambda i: (i, 0))
    shp = jax.ShapeDtypeStruct((r, c), F32)
    return pl.pallas_call(
        body, name=name, grid=(r // tr,), in_specs=[blk] * 4, out_specs=[blk] * 3, out_shape=[shp] * 3,
        compiler_params=_params(("parallel",)),
    )(w, g, m, v)


def _sum_slots(a, name):
    n, r, c = a.shape
    tr = _pick(r, (376, 256, 208, 128, 64, 32, 16, 8))

    def body(a_ref, o_ref):
        acc = a_ref[0].astype(F32)
        for k in range(1, n):
            acc = acc + a_ref[k].astype(F32)
        o_ref[...] = acc

    return pl.pallas_call(
        body, name=name, grid=(r // tr,),
        in_specs=[pl.BlockSpec((n, tr, c), lambda i: (0, i, 0))],
        out_specs=pl.BlockSpec((tr, c), lambda i: (i, 0)),
        out_shape=jax.ShapeDtypeStruct((r, c), F32),
        compiler_params=_params(("parallel",)),
    )(a)


def _mesh_pos():
    return lax.axis_index("x"), lax.axis_index("y"), lax.axis_index("c")


def _all_gather8(v, name):
    m_per, n = v.shape

    def body(x_ref, out_ref, send_sems, recv_sems, local_sem):
        x, y, c = _mesh_pos()
        me, sibling = (x, y, c), (x, y, 1 - c)
        chips = [(1 - x, y), (x, 1 - y), (1 - x, 1 - y)]

        def rows(px, py, pc):
            return out_ref.at[pl.ds((4 * px + 2 * py + pc) * m_per, m_per), :]

        def copy(k, block, to, src=None):
            return pltpu.make_async_remote_copy(
                src_ref=rows(*block) if src is None else src, dst_ref=rows(*block),
                send_sem=send_sems.at[k], recv_sem=recv_sems.at[k], device_id=to, device_id_type=MESH_ID)

        mine = pltpu.make_async_copy(x_ref, rows(*me), local_sem)
        mine.start()
        first = [copy(0, me, sibling, src=x_ref)]
        first += [copy(1 + j, me, (*chip, c), src=x_ref) for j, chip in enumerate(chips)]
        for cp in first:
            cp.start()
        passed = [copy(4 + j, (*chip, c), sibling) for j, chip in enumerate(chips)]
        for j, chip in enumerate(chips):
            copy(1 + j, (*chip, c), me).wait_recv()
            passed[j].start()
        copy(0, sibling, me).wait_recv()
        for j, chip in enumerate(chips):
            copy(4 + j, (*chip, 1 - c), me).wait_recv()
        for cp in first + passed:
            cp.wait_send()
        mine.wait()

    return pl.pallas_call(
        body, name=name,
        out_shape=jax.ShapeDtypeStruct((8 * m_per, n), v.dtype),
        in_specs=[pl.BlockSpec(memory_space=pltpu.VMEM)],
        out_specs=pl.BlockSpec(memory_space=pltpu.VMEM),
        scratch_shapes=[pltpu.SemaphoreType.DMA((7,)), pltpu.SemaphoreType.DMA((7,)), pltpu.SemaphoreType.DMA],
        compiler_params=_params(),
    )(v)


ANY = pl.BlockSpec(memory_space=pl.ANY)
PEER_CHIPS = lambda x, y: [(x, 1 - y), (1 - x, y), (1 - x, 1 - y)]


def _shard_region(ref, kind, k, rl, cl, r0, nr, c0, nc):
    if kind == "col":
        return ref.at[pl.ds(r0, nr), pl.ds(k * cl + c0, nc)]
    return ref.at[pl.ds(k * rl + r0, nr), pl.ds(c0, nc)]


def _place_shard(w, kind, chip, name):
    rl, cl = w.shape
    tr = _pick(rl, (256, 128, 64))
    nt = rl // tr

    def body(chip_ref, w_ref, o_ref):
        o_ref[...] = w_ref[...].astype(BF16)

    o_map = (lambda i, chip_ref: (i, chip_ref[0])) if kind == "col" else (lambda i, chip_ref: (chip_ref[0] * nt + i, 0))
    return pl.pallas_call(
        body, name=name,
        grid_spec=pltpu.PrefetchScalarGridSpec(
            num_scalar_prefetch=1, grid=(nt,),
            in_specs=[pl.BlockSpec((tr, cl), lambda i, chip_ref: (i, 0))], out_specs=pl.BlockSpec((tr, cl), o_map)),
        out_shape=jax.ShapeDtypeStruct((rl, 4 * cl) if kind == "col" else (4 * rl, cl), BF16),
        compiler_params=_params(("parallel",)),
    )(chip.reshape(1), w)


def _gather_weights(placed, kinds):
    n = len(placed)
    shard_shapes = [(p.shape[0], p.shape[1] // 4) if k == "col" else (p.shape[0] // 4, p.shape[1])
                    for p, k in zip(placed, kinds)]

    def body(*refs):
        outs = refs[n:2 * n]
        send_sems, recv_sems = refs[2 * n:]
        x, y, c = _mesh_pos()
        me = 2 * x + y
        peers = PEER_CHIPS(x, y)
        sends = []
        for a in range(n):
            rl, cl = shard_shapes[a]
            rh = rl // 2
            reg = functools.partial(_shard_region, outs[a], kinds[a], rl=rl, cl=cl, c0=0, nc=cl)
            for j, (px, py) in enumerate(peers):
                half = reg(k=me, r0=c * rh, nr=rh)
                cp = pltpu.make_async_remote_copy(
                    src_ref=half, dst_ref=half, send_sem=send_sems.at[a, j], recv_sem=recv_sems.at[a, j],
                    device_id=(px, py, c), device_id_type=MESH_ID)
                cp.start()
                sends.append(cp)
        for a in range(n):
            rl, cl = shard_shapes[a]
            rh = rl // 2
            reg = functools.partial(_shard_region, outs[a], kinds[a], rl=rl, cl=cl, c0=0, nc=cl)
            for j, (px, py) in enumerate(peers):
                got = reg(k=2 * px + py, r0=c * rh, nr=rh)
                pltpu.make_async_remote_copy(src_ref=got, dst_ref=got, send_sem=send_sems.at[a, j],
                                             recv_sem=recv_sems.at[a, j], device_id=(px, py, c),
                                             device_id_type=MESH_ID).wait_recv()
                fwd = pltpu.make_async_remote_copy(src_ref=got, dst_ref=got, send_sem=send_sems.at[a, 3 + j],
                                                   recv_sem=recv_sems.at[a, 3 + j], device_id=(x, y, 1 - c),
                                                   device_id_type=MESH_ID)
                fwd.start()
                sends.append(fwd)
        for a in range(n):
            rl, cl = shard_shapes[a]
            rh = rl // 2
            reg = functools.partial(_shard_region, outs[a], kinds[a], rl=rl, cl=cl, c0=0, nc=cl)
            for j, (px, py) in enumerate(peers):
                got = reg(k=2 * px + py, r0=(1 - c) * rh, nr=rh)
                pltpu.make_async_remote_copy(src_ref=got, dst_ref=got, send_sem=send_sems.at[a, 3 + j],
                                             recv_sem=recv_sems.at[a, 3 + j], device_id=(x, y, 1 - c),
                                             device_id_type=MESH_ID).wait_recv()
        for cp in sends:
            cp.wait_send()

    return pl.pallas_call(
        body, name="gather_weights",
        out_shape=[jax.ShapeDtypeStruct(p.shape, p.dtype) for p in placed],
        in_specs=[ANY] * n, out_specs=[ANY] * n, input_output_aliases={a: a for a in range(n)},
        scratch_shapes=[pltpu.SemaphoreType.DMA((n, 6)), pltpu.SemaphoreType.DMA((n, 6))],
        compiler_params=_params(),
    )(*placed)


def _half(kind, r, c):
    return (r // 2, c) if kind == "col" else (r, c // 2)


def _half_of(ref, kind, which):
    r, c = ref.shape
    hr, hc = _half(kind, r, c)
    return ref.at[pl.ds(which * hr, hr), :] if kind == "col" else ref.at[:, pl.ds(which * hc, hc)]


def _rs_sibling(grads, kinds):
    n = len(grads)

    def body(*refs):
        srcs, dsts = refs[:n], refs[n:2 * n]
        send_sems, recv_sems = refs[2 * n:]
        x, y, c = _mesh_pos()
        cps = [pltpu.make_async_remote_copy(src_ref=_half_of(srcs[a], kinds[a], 1 - c), dst_ref=dsts[a],
                                            send_sem=send_sems.at[a], recv_sem=recv_sems.at[a],
                                            device_id=(x, y, 1 - c), device_id_type=MESH_ID) for a in range(n)]
        for cp in cps:
            cp.start()
        for cp in cps:
            cp.wait()

    return pl.pallas_call(
        body, name="rs_sibling",
        out_shape=[jax.ShapeDtypeStruct(_half(k, *g.shape), g.dtype) for g, k in zip(grads, kinds)],
        in_specs=[ANY] * n, out_specs=[ANY] * n,
        scratch_shapes=[pltpu.SemaphoreType.DMA((n,)), pltpu.SemaphoreType.DMA((n,))],
        compiler_params=_params(),
    )(*grads)


def _pair_sum(gf, rv, kind, ci, name):
    r, c = rv.shape
    tr = _pick(r, (128, 64, 32, 16, 8))
    nt = r // tr

    def body(ci_ref, g_ref, r_ref, o_ref):
        o_ref[...] = (g_ref[...] + r_ref[...]).astype(BF16)

    g_map = (lambda i, ci_ref: (ci_ref[0] * nt + i, 0)) if kind == "col" else (lambda i, ci_ref: (i, ci_ref[0]))
    blk = pl.BlockSpec((tr, c), lambda i, ci_ref: (i, 0))
    return pl.pallas_call(
        body, name=name,
        grid_spec=pltpu.PrefetchScalarGridSpec(num_scalar_prefetch=1, grid=(nt,),
                                               in_specs=[pl.BlockSpec((tr, c), g_map), blk], out_specs=blk),
        out_shape=jax.ShapeDtypeStruct((r, c), BF16),
        compiler_params=_params(("parallel",)),
    )(ci.reshape(1), gf, rv)


def _rs_chips(pairs, kinds):
    n = len(pairs)
    shapes = []
    for p, k in zip(pairs, kinds):
        r, c = p.shape
        shapes.append((r, c // 4) if k == "col" else (r // 4, c))

    def body(*refs):
        srcs, dsts = refs[:n], refs[n:2 * n]
        send_sems, recv_sems = refs[2 * n:]
        x, y, c = _mesh_pos()
        me = 2 * x + y
        peers = PEER_CHIPS(x, y)
        cps = []
        for a in range(n):
            rl, cl = shapes[a]
            reg = functools.partial(_shard_region, srcs[a], kinds[a], rl=rl, cl=cl, r0=0, nr=rl, c0=0, nc=cl)
            for j, (px, py) in enumerate(peers):
                cps.append(pltpu.make_async_remote_copy(
                    src_ref=reg(k=2 * px + py), dst_ref=dsts[a].at[me], send_sem=send_sems.at[a, j],
                    recv_sem=recv_sems.at[a, j], device_id=(px, py, c), device_id_type=MESH_ID))
                cps[-1].start()
        for a in range(n):
            for j, (px, py) in enumerate(peers):
                slot = dsts[a].at[2 * px + py]
                pltpu.make_async_remote_copy(src_ref=slot, dst_ref=slot, send_sem=send_sems.at[a, j],
                                             recv_sem=recv_sems.at[a, j], device_id=(px, py, c),
                                             device_id_type=MESH_ID).wait_recv()
        for cp in cps:
            cp.wait_send()

    return pl.pallas_call(
        body, name="rs_chips",
        out_shape=[jax.ShapeDtypeStruct((4,) + s, p.dtype) for s, p in zip(shapes, pairs)],
        in_specs=[ANY] * n, out_specs=[ANY] * n,
        scratch_shapes=[pltpu.SemaphoreType.DMA((n, 3)), pltpu.SemaphoreType.DMA((n, 3))],
        compiler_params=_params(),
    )(*pairs)


def _sum_chips(pair, got, kind, pos, name):
    _, r, c = got.shape
    tr = _pick(r, (256, 128, 64, 32, 16))
    nt = r // tr

    def body(pos_ref, own_ref, g1_ref, g2_ref, g3_ref, o_ref):
        o_ref[...] = ((own_ref[...].astype(F32) + g1_ref[0].astype(F32)) + g2_ref[0].astype(F32)) + g3_ref[0].astype(F32)

    if kind == "col":
        own_map = lambda i, p: (i, p[1])
        out_map = lambda i, p: (p[0] * nt + i, 0)
        out_shape = (2 * r, c)
    else:
        own_map = lambda i, p: (p[1] * nt + i, 0)
        out_map = lambda i, p: (i, p[0])
        out_shape = (r, 2 * c)
    peer = lambda m: pl.BlockSpec((1, tr, c), lambda i, p: (p[1] ^ m, i, 0))
    return pl.pallas_call(
        body, name=name,
        grid_spec=pltpu.PrefetchScalarGridSpec(
            num_scalar_prefetch=1, grid=(nt,),
            in_specs=[pl.BlockSpec((tr, c), own_map), peer(1), peer(2), peer(3)],
            out_specs=pl.BlockSpec((tr, c), out_map)),
        out_shape=jax.ShapeDtypeStruct(out_shape, F32),
        compiler_params=_params(("parallel",)),
    )(pos, pair, got, got, got)


def _rs_back(halves, kinds):
    n = len(halves)

    def body(*refs):
        outs = refs[n:2 * n]
        send_sems, recv_sems = refs[2 * n:]
        x, y, c = _mesh_pos()
        cps = []
        for a in range(n):
            mine = _half_of(outs[a], kinds[a], c)
            cps.append(pltpu.make_async_remote_copy(src_ref=mine, dst_ref=mine, send_sem=send_sems.at[a],
                                                    recv_sem=recv_sems.at[a], device_id=(x, y, 1 - c),
                                                    device_id_type=MESH_ID))
            cps[-1].start()
        for a in range(n):
            other = _half_of(outs[a], kinds[a], 1 - c)
            pltpu.make_async_remote_copy(src_ref=other, dst_ref=other, send_sem=send_sems.at[a],
                                         recv_sem=recv_sems.at[a], device_id=(x, y, 1 - c),
                                         device_id_type=MESH_ID).wait_recv()
        for cp in cps:
            cp.wait_send()

    return pl.pallas_call(
        body, name="rs_back",
        out_shape=[jax.ShapeDtypeStruct(h.shape, h.dtype) for h in halves],
        in_specs=[ANY] * n, out_specs=[ANY] * n, input_output_aliases={a: a for a in range(n)},
        scratch_shapes=[pltpu.SemaphoreType.DMA((n,)), pltpu.SemaphoreType.DMA((n,))],
        compiler_params=_params(),
    )(*halves)


def _rope_tables(l, lc):
    rows = l // GRID_W
    row = jnp.repeat(jnp.arange(rows, dtype=F32), GRID_W)
    col = jnp.tile(jnp.arange(GRID_W, dtype=F32), rows)
    n_freq = RET_DH // 4
    inv_freq = ROPE_THETA ** (-jnp.arange(n_freq, dtype=F32) / n_freq)
    ang = jnp.concatenate([row[:, None] * inv_freq, col[:, None] * inv_freq], axis=-1)
    cos_t = jnp.repeat(jnp.cos(ang), 2, axis=-1)
    sin_t = jnp.repeat(jnp.sin(ang), 2, axis=-1) * jnp.tile(jnp.array([-1.0, 1.0], F32), RET_DH // 2)
    cos_t = jnp.concatenate([jnp.ones((lc, RET_DH), F32), cos_t], axis=0)
    sin_t = jnp.concatenate([jnp.zeros((lc, RET_DH), F32), sin_t], axis=0)
    return cos_t, sin_t


def _s5_pack(a):
    blk = lambda t: t.reshape(1, S5_NB, 128, S5_STATE)
    lre = jnp.stack([a["s5_lambda_re_f"][0], a["s5_lambda_re_b"][0]]).reshape(2, S5_NB, 8, S5_STATE)
    lim = jnp.stack([a["s5_lambda_im_f"][0], a["s5_lambda_im_b"][0]]).reshape(2, S5_NB, 8, S5_STATE)
    lst = jnp.stack([a["s5_log_step_f"][0], a["s5_log_step_b"][0]]).reshape(2, S5_NB, 8, 1)
    b_re = blk(a["s5_b_re"][0].transpose(0, 2, 1))
    b_im = blk(a["s5_b_im"][0].transpose(0, 2, 1))
    return (lre, lim, lst, b_re, b_im, blk(a["s5_c_re"][0]), blk(a["s5_c_im"][0]),
            a["s5_d"].reshape(1, S5_NB, 1, 128))


def _s5_unpack(g):
    glre, glim, glst, gbre, gbim, gcre, gcim, gd = g
    unb = lambda t: t.reshape(S5_GROUPS, S5_GROUP, S5_STATE).transpose(0, 2, 1)[None]
    return {
        "s5_lambda_re_f": glre[0].reshape(1, S5_GROUPS, S5_STATE), "s5_lambda_re_b": glre[1].reshape(1, S5_GROUPS, S5_STATE),
        "s5_lambda_im_f": glim[0].reshape(1, S5_GROUPS, S5_STATE), "s5_lambda_im_b": glim[1].reshape(1, S5_GROUPS, S5_STATE),
        "s5_log_step_f": glst[0].reshape(1, S5_GROUPS), "s5_log_step_b": glst[1].reshape(1, S5_GROUPS),
        "s5_b_re": unb(gbre), "s5_b_im": unb(gbim),
        "s5_c_re": gcre.reshape(1, S5_GROUPS, S5_GROUP, S5_STATE), "s5_c_im": gcim.reshape(1, S5_GROUPS, S5_GROUP, S5_STATE),
        "s5_d": gd.reshape(1, S5_WIDTH),
    }


def _local_step(a, wb, mx, mc, conv_w):
    x, ctx, tgt = a["x"][0], a["ctx"][0], a["loss_target"][0]
    l, lc = x.shape[0], ctx.shape[0]
    la = l + lc
    nct, ncc, nrc, cn = lc // TOK_TILE, lc // S5_T, lc // RET_CHUNK, la // S5_T
    n1w, n2w, fnw = a["norm1_w"], a["norm2_w"], a["final_norm_w"].reshape(1, D_MODEL)
    conv_b, b_glu = a["conv_b"], a["s5_b_glu"]
    ld2 = jnp.concatenate([a["ret_log_decay_f"], a["ret_log_decay_b"]], axis=0)
    mod4 = jnp.concatenate([mc[0:2], mx[0:2]], axis=0)
    mod3 = mx[2:5]
    gate5 = mx[5:6]
    cos_t, sin_t = _rope_tables(l, lc)
    s5p = _s5_pack(a)

    p_all, h1b = _norm_inproj(x, ctx, n1w, mod4, wb["w_in"])
    p3 = p_all.reshape(cn, S5_T, IN_COLS)
    kb, wst, wout, a16 = _s5_gen(*s5p)
    sloc = _s5_state(p3, wst)
    a16s = a16.transpose(1, 0, 2, 3).reshape(2, S5_GROUPS, 128)
    hs = _s5_scan(sloc, a16s, ncc)
    y_all = _s5_out(p3, kb, hs, wout).reshape(la, S5_WIDTH)
    s5x = _s5_glu(y_all, wb["s5_w_glu"], b_glu, nct)
    tabs = _ret_tables(ld2)
    of, ob, ssf, ssb = _ret_scan(p_all, cos_t, sin_t, tabs, nrc)
    retx, y_ret = _ret_gate(of, ob, p_all, nct)
    x1, mix, h2b, up = _outproj_up(x, s5x, retx, wb["w_out"], mod3, n2w, wb["w_up"])
    act, dx2, ddn, acc_f = _ffn_loss(up, x1, conv_w, conv_b, wb["w_down"], gate5, fnw, tgt)

    g = {}
    dact = _mm(ddn, wb["w_down"], nt=True, name="dact")
    g["w_down"] = _mm_tn(act, ddn, name="gw_down")
    dup, acc_c = _convglu_bwd(up, dact, conv_w, conv_b)
    dh2 = _mm(dup, wb["w_up"], nt=True, name="dh2")
    g["w_up"] = _mm_tn(h2b, dup, name="gw_up")
    dx1, dmixb, acc_2 = _norm2_bwd(x1, dh2, dx2, mix, mod3, n2w)
    dmix = _mm(dmixb, wb["w_out"], nt=True, name="dmix")
    g["w_out"] = jnp.concatenate([_mm_tn(s5x, dmixb, name="gw_out_s5"), _mm_tn(retx, dmixb, name="gw_out_ret")], axis=0)

    dy_s5, g["s5_w_glu"], g["s5_b_glu"] = _s5_glu_bwd(y_all, dmix, wb["s5_w_glu"], b_glu, nct)
    dy3 = dy_s5.reshape(cn, S5_T, S5_WIDTH)
    e = _s5_bwd_h(dy3, wout)
    ds, da16 = _s5_scan_bwd(e, hs, a16s, ncc)
    du = _s5_bwd_u(dy3, kb, ds, wst).reshape(la, S5_WIDTH)
    dkb = _s5_bwd_kb(p3, dy3)
    dwst = _s5_bwd_w(p3, ds, "s5_bwd_wst")
    dwout = _s5_bwd_w(dy3, hs, "s5_bwd_wout")
    da16p = da16.reshape(2, S5_NB, 8, 128).transpose(1, 0, 2, 3)
    g.update(_s5_unpack(_s5_gen_bwd(*s5p, dkb, dwst, dwout, da16p)))

    dy_ret, dg = _ret_gate_bwd(y_ret, p_all, dmix, nct)
    dqf, dkf, dvf, dqb, dkb_, dvb, dld = _ret_scan_bwd(p_all, cos_t, sin_t, tabs, ssf, ssb, dy_ret, nrc)
    g["ret_log_decay_f"] = dld[0, :, 0, 0].reshape(1, RET_HEADS)
    g["ret_log_decay_b"] = dld[1, :, 0, 0].reshape(1, RET_HEADS)
    dp = _ret_qkv_grad(dqf, dkf, dvf, dqb, dkb_, dvb, du, dg, cos_t, sin_t)
    dh1 = _mm(dp, wb["w_in"], nt=True, name="dh1")
    g["w_in"] = _mm_tn(h1b, dp, name="gw_in")
    grad_x, acc_1 = _norm_inproj_bwd(x, ctx, n1w, mod4, dh1, dx1)

    g["norm1_w"], g["norm2_w"], g["final_norm_w"] = acc_1[0:1], acc_2[0:1], acc_f[0]
    g["conv_w"], g["conv_b"] = acc_c[0:3], acc_c[3:4]
    zero = jnp.zeros((1, D_MODEL), F32)
    dmx = jnp.concatenate([acc_1[3:5], acc_2[1:2], acc_2[2:4], acc_f[1:2]], axis=0)
    dmc = jnp.concatenate([acc_1[1:3], zero, zero, zero, zero], axis=0)
    return acc_f[2, 0], grad_x, g, dmx, dmc


WEIGHT_NAMES = ("c_ctx", "w_mod", "b_mod", "norm1_w", "w_in", "s5_lambda_re_f", "s5_lambda_im_f", "s5_log_step_f",
                "s5_lambda_re_b", "s5_lambda_im_b", "s5_log_step_b", "s5_b_re", "s5_b_im", "s5_c_re", "s5_c_im",
                "s5_d", "s5_w_glu", "s5_b_glu", "ret_log_decay_f", "ret_log_decay_b", "w_out", "norm2_w", "w_up",
                "conv_w", "conv_b", "w_down", "final_norm_w")
BIG_NAMES = ("w_in", "w_out", "w_up", "w_down", "s5_w_glu")
BIG_KINDS = ("col", "row", "col", "row", "row")
SMALL_NAMES = ("norm1_w", "norm2_w", "final_norm_w", "conv_b", "conv_w", "s5_lambda_re_f", "s5_lambda_im_f",
               "s5_log_step_f", "s5_lambda_re_b", "s5_lambda_im_b", "s5_log_step_b", "s5_b_re", "s5_b_im", "s5_c_re",
               "s5_c_im", "s5_d", "s5_b_glu", "ret_log_decay_f", "ret_log_decay_b")
ROW = 1024
N_CHIPS = 4


def _pack_rows(parts):
    flat = jnp.concatenate([p.reshape(-1) for p in parts])
    n = flat.shape[0]
    rows = -(-n // (8 * ROW)) * 8
    return jnp.pad(flat, (0, rows * ROW - n)).reshape(rows, ROW)


def _unpack_rows(packed, shapes):
    flat = packed.reshape(-1)
    out, off = [], 0
    for s in shapes:
        n = math.prod(s)
        out.append(flat[off:off + n].reshape(s))
        off += n
    return out


def _step(a):
    xi, yi, ci = _mesh_pos()
    chip = 2 * xi + yi
    dev = 2 * chip + ci

    cw_loc = a["conv_w"].reshape(-1)
    small_in = jnp.concatenate([a["c"].reshape(-1), jnp.pad(cw_loc, (0, 24 * 128 - cw_loc.shape[0]))]).reshape(32, 128)
    sg = _all_gather8(small_in, "gather_cond").reshape(8, 32, 128)
    c_all = sg[:, 0:8].reshape(8, D_MODEL)
    conv_w = sg[0::2, 8:32].reshape(N_CHIPS, -1)[:, :cw_loc.shape[0]].reshape(N_CHIPS, 3, -1)
    conv_w = conv_w.transpose(1, 0, 2).reshape(3, D_FF)

    placed = [_place_shard(a[n][0], k, chip, "place_" + n) for n, k in zip(BIG_NAMES, BIG_KINDS)]
    wb = dict(zip(BIG_NAMES, _gather_weights(placed, BIG_KINDS)))

    w_mod_b = a["w_mod"][0].astype(BF16)
    c_ctx = a["c_ctx"].reshape(1, D_MODEL)
    b_loc = lax.dynamic_slice_in_dim(a["b_mod"], chip * MOD_COLS, MOD_COLS, 1)
    m_loc, s_b = _mod_fwd(c_all, c_ctx, w_mod_b, b_loc)
    mg = _all_gather8(m_loc, "gather_mod").reshape(8, MOD_ROWS, MOD_COLS)
    m_full = mg[0::2].transpose(1, 0, 2).reshape(MOD_ROWS, 6 * D_MODEL)
    mx = lax.dynamic_slice_in_dim(m_full, dev, 1, 0).reshape(6, D_MODEL)
    mc = m_full[8].reshape(6, D_MODEL)

    loss_part, grad_x, g, dmx, dmc = _local_step(a, wb, mx, mc, conv_w)
    loss = lax.psum(loss_part, ("x", "y", "c"))

    dm_pair = jnp.concatenate([dmx.reshape(1, -1), dmc.reshape(1, -1), jnp.zeros((6, 6 * D_MODEL), F32)], axis=0)
    dm_all = _all_gather8(dm_pair, "gather_dmod").reshape(8, 8, 6 * D_MODEL)
    dm16, gb_mod = _mod_bwd_sum(dm_all)
    dm_loc = lax.dynamic_slice_in_dim(dm16, chip * MOD_COLS, MOD_COLS, 1)
    gw_mod, gcc = _mod_bwd_w(dm_loc, s_b, c_ctx, w_mod_b)

    small_parts = [g[n] for n in SMALL_NAMES] + [gcc[0]]
    small_shapes = [p.shape for p in small_parts]
    sp = _pack_rows(small_parts)
    tot = _sum_slots(_all_gather8(sp, "gather_small_grads").reshape(8, sp.shape[0], ROW), "sum_small_grads")
    small = dict(zip(SMALL_NAMES + ("c_ctx",), _unpack_rows(tot, small_shapes)))
    grads = {n: small[n].reshape(a[n].shape) for n in SMALL_NAMES if n != "conv_w"}
    grads["c_ctx"] = (0.5 * small["c_ctx"]).reshape(a["c_ctx"].shape)
    grads["conv_w"] = lax.dynamic_slice_in_dim(small["conv_w"], chip * (D_FF // N_CHIPS), D_FF // N_CHIPS, 1)[None]
    grads["b_mod"] = gb_mod
    grads["w_mod"] = gw_mod[None]

    gfull = [g[n] for n in BIG_NAMES]
    sib = _rs_sibling(gfull, BIG_KINDS)
    pairs = [_pair_sum(gf, rv, k, ci, "rs_pair_" + n) for gf, rv, k, n in zip(gfull, sib, BIG_KINDS, BIG_NAMES)]
    pos = jnp.stack([ci, chip])
    halves = [_sum_chips(p, t, k, pos, "rs_sum_" + n)
              for p, t, k, n in zip(pairs, _rs_chips(pairs, BIG_KINDS), BIG_KINDS, BIG_NAMES)]
    for n, t in zip(BIG_NAMES, _rs_back(halves, BIG_KINDS)):
        grads[n] = t[None]

    delta, new_m, new_v = {}, {}, {}
    for n in BIG_NAMES + ("w_mod",):
        for dst, t in zip((delta, new_m, new_v), _adamw(a[n][0], grads[n][0], a["m_" + n][0], a["v_" + n][0], "adamw_" + n)):
            dst[n] = t[None]
    rest = [n for n in WEIGHT_NAMES if n not in BIG_NAMES and n != "w_mod"]
    shapes = [a[n].shape for n in rest]
    pr = lambda pre: _pack_rows([a[pre + n] for n in rest])
    for dst, t in zip((delta, new_m, new_v),
                      _adamw(pr(""), _pack_rows([grads[n] for n in rest]), pr("m_"), pr("v_"), "adamw_small")):
        dst.update(zip(rest, _unpack_rows(t, shapes)))

    return (loss, grad_x[None], *[grads[n] for n in WEIGHT_NAMES], *[delta[n] for n in WEIGHT_NAMES],
            *[new_m[n] for n in WEIGHT_NAMES], *[new_v[n] for n in WEIGHT_NAMES])


def kernel(x, c, ctx, c_ctx, w_mod, b_mod, norm1_w, w_in, s5_lambda_re_f, s5_lambda_im_f, s5_log_step_f, s5_lambda_re_b, s5_lambda_im_b, s5_log_step_b, s5_b_re, s5_b_im, s5_c_re, s5_c_im, s5_d, s5_w_glu, s5_b_glu, ret_log_decay_f, ret_log_decay_b, w_out, norm2_w, w_up, conv_w, conv_b, w_down, final_norm_w, loss_target, m_c_ctx, m_w_mod, m_b_mod, m_norm1_w, m_w_in, m_s5_lambda_re_f, m_s5_lambda_im_f, m_s5_log_step_f, m_s5_lambda_re_b, m_s5_lambda_im_b, m_s5_log_step_b, m_s5_b_re, m_s5_b_im, m_s5_c_re, m_s5_c_im, m_s5_d, m_s5_w_glu, m_s5_b_glu, m_ret_log_decay_f, m_ret_log_decay_b, m_w_out, m_norm2_w, m_w_up, m_conv_w, m_conv_b, m_w_down, m_final_norm_w, v_c_ctx, v_w_mod, v_b_mod, v_norm1_w, v_w_in, v_s5_lambda_re_f, v_s5_lambda_im_f, v_s5_log_step_f, v_s5_lambda_re_b, v_s5_lambda_im_b, v_s5_log_step_b, v_s5_b_re, v_s5_b_im, v_s5_c_re, v_s5_c_im, v_s5_d, v_s5_w_glu, v_s5_b_glu, v_ret_log_decay_f, v_ret_log_decay_b, v_w_out, v_norm2_w, v_w_up, v_conv_w, v_conv_b, v_w_down, v_final_norm_w):
    return _step(dict(locals()))
```

```python
import functools
import math

import jax
import jax.numpy as jnp
from jax import lax
from jax.experimental import pallas as pl
from jax.experimental.pallas import tpu as pltpu

F32 = jnp.float32
BF16 = jnp.bfloat16

D_MODEL = 1024
S5_WIDTH = 512
S5_GROUPS = 32
S5_GROUP = 16
S5_STATE = 64
RET_WIDTH = 512
RET_HEADS = 4
RET_DH = 128
RET_CHUNK = 128
GRID_W = 64
ROPE_THETA = 10000.0
D_FF = 2816
NORM_EPS = 1e-6
IN_COLS = S5_WIDTH + 4 * RET_WIDTH

S5_T = 16
S5_NB = 4
S5_BW = S5_T * 128
S5_SW = 8 * 2 * S5_STATE

ADAM_LR, ADAM_B1, ADAM_B2, ADAM_EPS, ADAM_WD, ADAM_STEP = 0.001, 0.9, 0.999, 1e-08, 0.01, 10

VMEM_LIMIT = 56 * 1024 * 1024
MESH_ID = pl.DeviceIdType.MESH


def _params(sem=None):
    return pltpu.CompilerParams(dimension_semantics=sem, vmem_limit_bytes=VMEM_LIMIT)


def _full(shape):
    n = len(shape)
    return pl.BlockSpec(shape, lambda *_: (0,) * n)


def _dot(a, b):
    return jnp.dot(a, b, preferred_element_type=F32)


def _dot_nt(a, b):
    return lax.dot_general(a, b, (((1,), (1,)), ((), ())), preferred_element_type=F32)


def _dot_tn(a, b):
    return lax.dot_general(a, b, (((0,), (0,)), ((), ())), preferred_element_type=F32)


def _dot_hi(a, b):
    return jnp.dot(a, b, preferred_element_type=F32, precision=lax.Precision.HIGHEST)


def _dot_nt_hi(a, b):
    return lax.dot_general(a, b, (((1,), (1,)), ((), ())), preferred_element_type=F32,
                           precision=lax.Precision.HIGHEST)


def _gelu(x):
    return 0.5 * x * (1.0 + jnp.tanh(0.7978845608028654 * (x + 0.044715 * (x * x * x))))


def _sigmoid(x):
    return 1.0 / (1.0 + jnp.exp(-x))


def _silu(x):
    return x * _sigmoid(x)


def _rms_mod(x, nw, sh, sc):
    r = lax.rsqrt(jnp.mean(x * x, axis=-1, keepdims=True) + NORM_EPS)
    return (x * r * nw) * (1.0 + sc) + sh


def _rms(x, nw):
    r = lax.rsqrt(jnp.mean(x * x, axis=-1, keepdims=True) + NORM_EPS)
    return x * r * nw


def _head_norm_gate(y, g):
    mu = jnp.mean(y, axis=-1, keepdims=True)
    yc = y - mu
    var = jnp.mean(yc * yc, axis=-1, keepdims=True)
    return _silu(g) * (yc * lax.rsqrt(var + NORM_EPS))


def _swap_pairs(t):
    lane = lax.broadcasted_iota(jnp.int32, t.shape, 1)
    return jnp.where(lane % 2 == 0, pltpu.roll(t, RET_DH - 1, 1), pltpu.roll(t, 1, 1))


def _rope(t, cos_t, sin_t):
    return t * cos_t + _swap_pairs(t) * sin_t


def _rope_t(dt, cos_t, sin_t):
    return dt * cos_t + _swap_pairs(dt * sin_t)


def _pick(n, prefs):
    for p in prefs:
        if n % p == 0:
            return p
    return n


def _mm(a, w, *, nt=False, out_dtype=F32, name):
    m, k = a.shape
    n = w.shape[0] if nt else w.shape[1]
    tm = _pick(m, (512, 256, 128))
    tn = _pick(n, (1408, 1024, 1280, 512))

    def body(a_ref, w_ref, o_ref):
        f = _dot_nt if nt else _dot
        o_ref[...] = f(a_ref[...], w_ref[...]).astype(out_dtype)

    w_spec = pl.BlockSpec((tn, k), lambda j, i: (j, 0)) if nt else pl.BlockSpec((k, tn), lambda j, i: (0, j))
    return pl.pallas_call(
        body, name=name, grid=(n // tn, m // tm),
        in_specs=[pl.BlockSpec((tm, k), lambda j, i: (i, 0)), w_spec],
        out_specs=pl.BlockSpec((tm, tn), lambda j, i: (i, j)),
        out_shape=jax.ShapeDtypeStruct((m, n), out_dtype),
        compiler_params=_params(("parallel", "parallel")),
    )(a, w)


def _mm_tn(a, b, *, name):
    m, k = a.shape
    n = b.shape[1]
    tm = _pick(m, (512, 256, 128))
    tn = _pick(n, (1408, 1024, 1280, 512))

    def body(a_ref, b_ref, o_ref):
        @pl.when(pl.program_id(1) == 0)
        def _():
            o_ref[...] = jnp.zeros_like(o_ref)
        o_ref[...] += _dot_tn(a_ref[...], b_ref[...])

    return pl.pallas_call(
        body, name=name, grid=(n // tn, m // tm),
        in_specs=[pl.BlockSpec((tm, k), lambda j, i: (i, 0)), pl.BlockSpec((tm, tn), lambda j, i: (i, j))],
        out_specs=pl.BlockSpec((k, tn), lambda j, i: (0, j)),
        out_shape=jax.ShapeDtypeStruct((k, n), F32),
        compiler_params=_params(("parallel", "arbitrary")),
    )(a, b)


TOK_TILE = 256


def _norm_inproj(x, ctx, n1w, mod4, w_in_b):
    l, lc = x.shape[0], ctx.shape[0]
    tm = TOK_TILE
    nct = lc // tm
    la = l + lc

    def body(x_ref, c_ref, nw_ref, mod_ref, w_ref, p_ref, h_ref):
        is_ctx = pl.program_id(0) < nct
        xt = jnp.where(is_ctx, c_ref[...], x_ref[...])
        sh = jnp.where(is_ctx, mod_ref[0:1, :], mod_ref[2:3, :])
        sc = jnp.where(is_ctx, mod_ref[1:2, :], mod_ref[3:4, :])
        hb = _rms_mod(xt, nw_ref[...], sh, sc).astype(BF16)
        h_ref[...] = hb
        p_ref[...] = _dot(hb, w_ref[...])

    return pl.pallas_call(
        body, name="norm_inproj", grid=(la // tm,),
        in_specs=[pl.BlockSpec((tm, D_MODEL), lambda i: (jnp.maximum(i - nct, 0), 0)),
                  pl.BlockSpec((tm, D_MODEL), lambda i: (jnp.minimum(i, nct - 1), 0)),
                  _full((1, D_MODEL)), _full((4, D_MODEL)), _full((D_MODEL, IN_COLS))],
        out_specs=[pl.BlockSpec((tm, IN_COLS), lambda i: (i, 0)), pl.BlockSpec((tm, D_MODEL), lambda i: (i, 0))],
        out_shape=[jax.ShapeDtypeStruct((la, IN_COLS), F32), jax.ShapeDtypeStruct((la, D_MODEL), BF16)],
        compiler_params=_params(("parallel",)),
    )(x, ctx, n1w, mod4, w_in_b)


def _norm_inproj_bwd(x, ctx, n1w, mod4, dh1, dx1):
    l, lc = x.shape[0], ctx.shape[0]
    tm = TOK_TILE
    nct = lc // tm
    la = l + lc

    def body(x_ref, c_ref, nw_ref, mod_ref, dh_ref, dx1_ref, gx_ref, acc_ref):
        i = pl.program_id(0)
        is_ctx = i < nct

        @pl.when(i == 0)
        def _():
            acc_ref[...] = jnp.zeros_like(acc_ref)

        xt = jnp.where(is_ctx, c_ref[...], x_ref[...])
        sh = jnp.where(is_ctx, mod_ref[0:1, :], mod_ref[2:3, :])
        sc = jnp.where(is_ctx, mod_ref[1:2, :], mod_ref[3:4, :])
        _, vjp = jax.vjp(_rms_mod, xt, nw_ref[...], sh, sc)
        dx, dnw, dsh, dsc = vjp(dh_ref[...])
        gx_ref[...] = dx + dx1_ref[...]
        cf = jnp.where(is_ctx, 1.0, 0.0)
        acc_ref[0:1, :] += dnw
        acc_ref[1:2, :] += cf * dsh
        acc_ref[2:3, :] += cf * dsc
        acc_ref[3:4, :] += (1.0 - cf) * dsh
        acc_ref[4:5, :] += (1.0 - cf) * dsc

    return pl.pallas_call(
        body, name="norm_inproj_bwd", grid=(la // tm,),
        in_specs=[pl.BlockSpec((tm, D_MODEL), lambda i: (jnp.maximum(i - nct, 0), 0)),
                  pl.BlockSpec((tm, D_MODEL), lambda i: (jnp.minimum(i, nct - 1), 0)),
                  _full((1, D_MODEL)), _full((4, D_MODEL)),
                  pl.BlockSpec((tm, D_MODEL), lambda i: (i, 0)),
                  pl.BlockSpec((tm, D_MODEL), lambda i: (jnp.maximum(i - nct, 0), 0))],
        out_specs=[pl.BlockSpec((tm, D_MODEL), lambda i: (jnp.maximum(i - nct, 0), 0)), _full((8, D_MODEL))],
        out_shape=[jax.ShapeDtypeStruct((l, D_MODEL), F32), jax.ShapeDtypeStruct((8, D_MODEL), F32)],
        compiler_params=_params(("arbitrary",)),
    )(x, ctx, n1w, mod4, dh1, dx1)


def _iota2(shape, dim):
    return lax.broadcasted_iota(jnp.int32, shape, dim)


def _group_mask(rows, cols, row_div, col_div):
    return jnp.where(_iota2((rows, cols), 0) // row_div == _iota2((rows, cols), 1) // col_div, 1.0, 0.0).astype(F32)


def _s5_gen_dir(lre, lim, lst, b_re, b_im, c_re, c_im):
    step = jnp.exp(lst)
    mag = jnp.exp(lre * step)
    ar = mag * jnp.cos(lim * step)
    ai = mag * jnp.sin(lim * step)
    den = lre * lre + lim * lim
    xr = ar - 1.0
    cr = (xr * lre + ai * lim) / den
    ci = (ai * lre - xr * lim) / den
    rexp = _group_mask(128, 8, S5_GROUP, 1)
    are, aie = _dot_hi(rexp, ar), _dot_hi(rexp, ai)
    cre, cie = _dot_hi(rexp, cr), _dot_hi(rexp, ci)
    bbr = cre * b_re - cie * b_im
    bbi = cre * b_im + cie * b_re
    gmask = _group_mask(128, 128, S5_GROUP, S5_GROUP)
    pr, pi = jnp.ones_like(are), jnp.zeros_like(are)
    xs, ys = [], []
    for t in range(S5_T + 1):
        if t < S5_T:
            xs.append(jnp.concatenate([bbr * pr - bbi * pi, bbr * pi + bbi * pr], axis=1))
        ys.append(jnp.concatenate([c_re * pr - c_im * pi, -(c_re * pi + c_im * pr)], axis=1))
        pr, pi = pr * are - pi * aie, pr * aie + pi * are
    gs = [_dot_nt_hi(x_t, ys[0]) * gmask for x_t in xs]
    r16, i16 = ar, ai
    for _ in range(4):
        r16, i16 = r16 * r16 - i16 * i16, 2.0 * r16 * i16
    return xs, ys, gs, jnp.concatenate([r16, i16], axis=1)


def _s5_expand(z):
    return jnp.concatenate([z] * 8, axis=1) * _group_mask(128, S5_SW, S5_GROUP, 128)


def _s5_contract(z):
    zm = z * _group_mask(128, S5_SW, S5_GROUP, 128)
    acc = zm[:, 0:128]
    for k in range(1, 8):
        acc = acc + zm[:, 128 * k:128 * (k + 1)]
    return acc


def _s5_param_specs():
    blk3 = lambda r, c: pl.BlockSpec((1, 1, r, c), lambda b, j: (0, b, 0, 0))
    dir3 = lambda r, c: pl.BlockSpec((2, 1, r, c), lambda b, j: (0, b, 0, 0))
    return [dir3(8, S5_STATE), dir3(8, S5_STATE), dir3(8, 1), blk3(128, S5_STATE), blk3(128, S5_STATE),
            blk3(128, S5_STATE), blk3(128, S5_STATE), blk3(1, 128)]


def _s5_gen(lre, lim, lst, b_re, b_im, c_re, c_im, dvec):
    def body(lre_ref, lim_ref, lst_ref, bre_ref, bim_ref, cre_ref, cim_ref, d_ref,
             kb_ref, wst_ref, wout_ref, a16_ref, x_scr, y_scr, g_scr):
        j = pl.program_id(1)

        @pl.when(j == 0)
        def _():
            eye = _group_mask(128, 128, 1, 1)
            g0 = eye * d_ref[0, 0]
            for dr in range(2):
                xs, ys, gs, a16 = _s5_gen_dir(lre_ref[dr, 0], lim_ref[dr, 0], lst_ref[dr, 0], bre_ref[0, 0],
                                              bim_ref[0, 0], cre_ref[0, 0], cim_ref[0, 0])
                a16_ref[0, dr] = a16
                for t in range(S5_T):
                    x_scr[dr, t] = xs[t]
                for t in range(S5_T + 1):
                    y_scr[dr, t] = ys[t]
                g0 = g0 + gs[0]
                for t in range(1, S5_T):
                    g_scr[(S5_T - 1) + t if dr == 0 else (S5_T - 1) - t] = gs[t]
            g_scr[S5_T - 1] = g0

        for i in range(S5_T):
            kb_ref[0, :, 128 * i:128 * (i + 1)] = g_scr[i - j + (S5_T - 1)].astype(BF16)
        wst_ref[0, 0] = _s5_expand(x_scr[0, S5_T - 1 - j]).astype(BF16)
        wst_ref[0, 1] = _s5_expand(x_scr[1, j]).astype(BF16)
        wout_ref[0, 0] = _s5_expand(y_scr[0, j + 1]).astype(BF16)
        wout_ref[0, 1] = _s5_expand(y_scr[1, S5_T - j]).astype(BF16)

    return pl.pallas_call(
        body, name="s5_gen", grid=(S5_NB, S5_T),
        in_specs=_s5_param_specs(),
        out_specs=[pl.BlockSpec((1, 128, S5_BW), lambda b, j: (b, j, 0)),
                   pl.BlockSpec((1, 2, 128, S5_SW), lambda b, j: (b, 0, j, 0)),
                   pl.BlockSpec((1, 2, 128, S5_SW), lambda b, j: (b, 0, j, 0)),
                   pl.BlockSpec((1, 2, 8, 128), lambda b, j: (b, 0, 0, 0))],
        out_shape=[jax.ShapeDtypeStruct((S5_NB, S5_BW, S5_BW), BF16),
                   jax.ShapeDtypeStruct((S5_NB, 2, S5_BW, S5_SW), BF16),
                   jax.ShapeDtypeStruct((S5_NB, 2, S5_BW, S5_SW), BF16),
                   jax.ShapeDtypeStruct((S5_NB, 2, 8, 128), F32)],
        scratch_shapes=[pltpu.VMEM((2, S5_T, 128, 128), F32), pltpu.VMEM((2, S5_T + 1, 128, 128), F32),
                        pltpu.VMEM((2 * S5_T - 1, 128, 128), F32)],
        compiler_params=_params(("parallel", "arbitrary")),
    )(lre, lim, lst, b_re, b_im, c_re, c_im, dvec)


def _s5_gen_bwd(lre, lim, lst, b_re, b_im, c_re, c_im, dvec, dkb, dwst, dwout, da16):
    def body(lre_ref, lim_ref, lst_ref, bre_ref, bim_ref, cre_ref, cim_ref, d_ref,
             dkb_ref, dwst_ref, dwout_ref, da16_ref,
             glre_ref, glim_ref, glst_ref, gbre_ref, gbim_ref, gcre_ref, gcim_ref, gd_ref,
             dx_scr, dy_scr, dg_scr):
        j = pl.program_id(1)

        @pl.when(j == 0)
        def _():
            dg_scr[...] = jnp.zeros_like(dg_scr)
            dy_scr[0, 0] = jnp.zeros((128, 128), F32)
            dy_scr[1, 0] = jnp.zeros((128, 128), F32)

        for i in range(S5_T):
            dg_scr[i - j + (S5_T - 1)] += dkb_ref[0, :, 128 * i:128 * (i + 1)]
        dx_scr[0, S5_T - 1 - j] = _s5_contract(dwst_ref[0, 0])
        dx_scr[1, j] = _s5_contract(dwst_ref[0, 1])
        dy_scr[0, j + 1] = _s5_contract(dwout_ref[0, 0])
        dy_scr[1, S5_T - j] = _s5_contract(dwout_ref[0, 1])

        @pl.when(j == S5_T - 1)
        def _():
            eye = _group_mask(128, 128, 1, 1)
            gd_ref[0, 0] = jnp.sum(dg_scr[S5_T - 1] * eye, axis=0, keepdims=True)
            gb = [None, None, None, None]
            for dr in range(2):
                args = (lre_ref[dr, 0], lim_ref[dr, 0], lst_ref[dr, 0], bre_ref[0, 0], bim_ref[0, 0],
                        cre_ref[0, 0], cim_ref[0, 0])
                _, vjp = jax.vjp(_s5_gen_dir, *args)
                dxs = [dx_scr[dr, t] for t in range(S5_T)]
                dys = [dy_scr[dr, t] for t in range(S5_T + 1)]
                dgs = [dg_scr[(S5_T - 1) + t if dr == 0 else (S5_T - 1) - t] for t in range(S5_T)]
                g = vjp((dxs, dys, dgs, da16_ref[0, dr]))
                glre_ref[dr, 0] = g[0]
                glim_ref[dr, 0] = g[1]
                glst_ref[dr, 0] = g[2]
                for q in range(4):
                    gb[q] = g[3 + q] if gb[q] is None else gb[q] + g[3 + q]
            gbre_ref[0, 0] = gb[0]
            gbim_ref[0, 0] = gb[1]
            gcre_ref[0, 0] = gb[2]
            gcim_ref[0, 0] = gb[3]

    shp = lambda a: jax.ShapeDtypeStruct(a.shape, F32)
    return pl.pallas_call(
        body, name="s5_gen_bwd", grid=(S5_NB, S5_T),
        in_specs=_s5_param_specs() + [
            pl.BlockSpec((1, 128, S5_BW), lambda b, j: (b, j, 0)),
            pl.BlockSpec((1, 2, 128, S5_SW), lambda b, j: (b, 0, j, 0)),
            pl.BlockSpec((1, 2, 128, S5_SW), lambda b, j: (b, 0, j, 0)),
            pl.BlockSpec((1, 2, 8, 128), lambda b, j: (b, 0, 0, 0))],
        out_specs=_s5_param_specs(),
        out_shape=[shp(lre), shp(lim), shp(lst), shp(b_re), shp(b_im), shp(c_re), shp(c_im), shp(dvec)],
        scratch_shapes=[pltpu.VMEM((2, S5_T, 128, 128), F32), pltpu.VMEM((2, S5_T + 1, 128, 128), F32),
                        pltpu.VMEM((2 * S5_T - 1, 128, 128), F32)],
        compiler_params=_params(("parallel", "arbitrary")),
    )(lre, lim, lst, b_re, b_im, c_re, c_im, dvec, dkb, dwst, dwout, da16)


def _s5_ucat(u_ref, lo=0, hi=S5_T):
    return jnp.concatenate([u_ref[:, j, :] for j in range(lo, hi)], axis=1).astype(BF16)


def _s5_put_groups(o_ref, dr, val):
    for gi in range(8):
        o_ref[dr, :, gi, :] = val[:, 128 * gi:128 * (gi + 1)]


def _s5_get_groups(s_ref, dr, n=8):
    return jnp.concatenate([s_ref[dr, :, gi, :] for gi in range(n)], axis=1).astype(BF16)


def _s5_state(p3, wst):
    cn = p3.shape[0]

    def body(u_ref, w_ref, o_ref):
        u = _s5_ucat(u_ref)
        _s5_put_groups(o_ref, 0, _dot(u, w_ref[0, 0]))
        _s5_put_groups(o_ref, 1, _dot(u, w_ref[0, 1]))

    return pl.pallas_call(
        body, name="s5_state", grid=(S5_NB,),
        in_specs=[pl.BlockSpec((cn, S5_T, 128), lambda b: (0, 0, b)),
                  pl.BlockSpec((1, 2, S5_BW, S5_SW), lambda b: (b, 0, 0, 0))],
        out_specs=pl.BlockSpec((2, cn, 8, 128), lambda b: (0, 0, b, 0)),
        out_shape=jax.ShapeDtypeStruct((2, cn, S5_GROUPS, 128), F32),
        compiler_params=_params(("parallel",)),
    )(p3, wst)


def _s5_a_forms(a):
    ra = pltpu.roll(a, S5_STATE, 1)
    low = _iota2(a.shape, 1) < S5_STATE
    return jnp.where(low, a, ra), jnp.where(low, -ra, a)


def _s5_scan(sloc, a16, ncc):
    cn = sloc.shape[1]

    def body(s_ref, a_ref, h_ref):
        forms = [_s5_a_forms(a_ref[dr]) for dr in range(2)]

        def step(s, hs):
            out = []
            for dr in range(2):
                arr, aii = forms[dr]
                h, rh = hs[dr]
                c = s if dr == 0 else jnp.where(s < ncc, ncc - 1 - s, cn - 1 - (s - ncc))
                h_ref[dr, c] = h
                sc = s_ref[dr, c]
                out.append((h * arr + rh * aii + sc, rh * arr - h * aii + pltpu.roll(sc, S5_STATE, 1)))
            return tuple(out)

        zero = jnp.zeros((S5_GROUPS, 128), F32)
        lax.fori_loop(0, cn, step, ((zero, zero), (zero, zero)), unroll=4)

    return pl.pallas_call(
        body, name="s5_scan",
        out_shape=jax.ShapeDtypeStruct(sloc.shape, F32),
        compiler_params=_params(),
    )(sloc, a16)


def _s5_scan_bwd(e, hs, a16, ncc):
    cn = e.shape[1]

    def body(e_ref, h_ref, a_ref, ds_ref, da_ref):
        forms = [_s5_a_forms(a_ref[dr]) for dr in range(2)]
        low = _iota2((S5_GROUPS, 128), 1) < S5_STATE

        def step(s, carry):
            out = []
            r = cn - 1 - s
            for dr in range(2):
                arr, aii = forms[dr]
                g, rg, da = carry[dr]
                c = r if dr == 0 else jnp.where(r < ncc, ncc - 1 - r, cn - 1 - (r - ncc))
                ds_ref[dr, c] = g
                h = h_ref[dr, c]
                rh = pltpu.roll(h, S5_STATE, 1)
                da = da + jnp.where(low, g * h + rg * rh, g * rh - rg * h)
                ec = e_ref[dr, c]
                out.append((ec + g * arr - rg * aii, pltpu.roll(ec, S5_STATE, 1) + rg * arr + g * aii, da))
            return tuple(out)

        zero = jnp.zeros((S5_GROUPS, 128), F32)
        res = lax.fori_loop(0, cn, step, ((zero, zero, zero), (zero, zero, zero)), unroll=4)
        da_ref[0] = res[0][2]
        da_ref[1] = res[1][2]

    return pl.pallas_call(
        body, name="s5_scan_bwd",
        out_shape=[jax.ShapeDtypeStruct(e.shape, F32), jax.ShapeDtypeStruct((2, S5_GROUPS, 128), F32)],
        compiler_params=_params(),
    )(e, hs, a16)


def _s5_out(p3, kb, h2, wout):
    cn = p3.shape[0]
    half = S5_T // 2

    def body(u_ref, k_ref, h_ref, w_ref, y_ref):
        u = _s5_ucat(u_ref)
        y = _dot(u, k_ref[0])
        y = y + _dot_nt(_s5_get_groups(h_ref, 0), w_ref[0, 0])
        y = y + _dot_nt(_s5_get_groups(h_ref, 1), w_ref[0, 1])
        for i in range(half):
            y_ref[:, i, :] = y[:, 128 * i:128 * (i + 1)]

    return pl.pallas_call(
        body, name="s5_out", grid=(S5_NB, 2),
        in_specs=[pl.BlockSpec((cn, S5_T, 128), lambda b, q: (0, 0, b)),
                  pl.BlockSpec((1, S5_BW, S5_BW // 2), lambda b, q: (b, 0, q)),
                  pl.BlockSpec((2, cn, 8, 128), lambda b, q: (0, 0, b, 0)),
                  pl.BlockSpec((1, 2, S5_BW // 2, S5_SW), lambda b, q: (b, 0, q, 0))],
        out_specs=pl.BlockSpec((cn, half, 128), lambda b, q: (0, q, b)),
        out_shape=jax.ShapeDtypeStruct((cn, S5_T, S5_WIDTH), F32),
        compiler_params=_params(("parallel", "parallel")),
    )(p3, kb, h2, wout)


def _s5_bwd_h(dy3, wout):
    cn = dy3.shape[0]

    def body(d_ref, w_ref, e_ref):
        d = _s5_ucat(d_ref)
        _s5_put_groups(e_ref, 0, _dot(d, w_ref[0, 0]))
        _s5_put_groups(e_ref, 1, _dot(d, w_ref[0, 1]))

    return pl.pallas_call(
        body, name="s5_bwd_h", grid=(S5_NB,),
        in_specs=[pl.BlockSpec((cn, S5_T, 128), lambda b: (0, 0, b)),
                  pl.BlockSpec((1, 2, S5_BW, S5_SW), lambda b: (b, 0, 0, 0))],
        out_specs=pl.BlockSpec((2, cn, 8, 128), lambda b: (0, 0, b, 0)),
        out_shape=jax.ShapeDtypeStruct((2, cn, S5_GROUPS, 128), F32),
        compiler_params=_params(("parallel",)),
    )(dy3, wout)


def _s5_bwd_u(dy3, kb, ds2, wst):
    cn = dy3.shape[0]
    half = S5_T // 2

    def body(d_ref, k_ref, s_ref, w_ref, o_ref):
        d = _s5_ucat(d_ref)
        du = _dot_nt(d, k_ref[0])
        du = du + _dot_nt(_s5_get_groups(s_ref, 0), w_ref[0, 0])
        du = du + _dot_nt(_s5_get_groups(s_ref, 1), w_ref[0, 1])
        for j in range(half):
            o_ref[:, j, :] = du[:, 128 * j:128 * (j + 1)]

    return pl.pallas_call(
        body, name="s5_bwd_u", grid=(S5_NB, 2),
        in_specs=[pl.BlockSpec((cn, S5_T, 128), lambda b, q: (0, 0, b)),
                  pl.BlockSpec((1, S5_BW // 2, S5_BW), lambda b, q: (b, q, 0)),
                  pl.BlockSpec((2, cn, 8, 128), lambda b, q: (0, 0, b, 0)),
                  pl.BlockSpec((1, 2, S5_BW // 2, S5_SW), lambda b, q: (b, 0, q, 0))],
        out_specs=pl.BlockSpec((cn, half, 128), lambda b, q: (0, q, b)),
        out_shape=jax.ShapeDtypeStruct((cn, S5_T, S5_WIDTH), F32),
        compiler_params=_params(("parallel", "parallel")),
    )(dy3, kb, ds2, wst)


def _s5_bwd_kb(p3, dy3):
    cn = p3.shape[0]
    half = S5_T // 2

    def body(u_ref, d_ref, o_ref):
        o_ref[0] = _dot_tn(_s5_ucat(u_ref), _s5_ucat(d_ref, 0, half))

    return pl.pallas_call(
        body, name="s5_bwd_kb", grid=(S5_NB, 2),
        in_specs=[pl.BlockSpec((cn, S5_T, 128), lambda b, q: (0, 0, b)),
                  pl.BlockSpec((cn, half, 128), lambda b, q: (0, q, b))],
        out_specs=pl.BlockSpec((1, S5_BW, S5_BW // 2), lambda b, q: (b, 0, q)),
        out_shape=jax.ShapeDtypeStruct((S5_NB, S5_BW, S5_BW), F32),
        compiler_params=_params(("parallel", "parallel")),
    )(p3, dy3)


def _s5_bwd_w(u3, st, name):
    cn = u3.shape[0]

    def body(u_ref, s_ref, w_ref):
        w_ref[0, 0] = _dot_tn(_s5_ucat(u_ref), _s5_get_groups(s_ref, 0))

    return pl.pallas_call(
        body, name=name, grid=(S5_NB, 2),
        in_specs=[pl.BlockSpec((cn, S5_T, 128), lambda b, q: (0, 0, b)),
                  pl.BlockSpec((1, cn, 8, 128), lambda b, q: (q, 0, b, 0))],
        out_specs=pl.BlockSpec((1, 1, S5_BW, S5_SW), lambda b, q: (b, q, 0, 0)),
        out_shape=jax.ShapeDtypeStruct((S5_NB, 2, S5_BW, S5_SW), F32),
        compiler_params=_params(("parallel", "parallel")),
    )(u3, st)


def _s5_glu(y_all, w_glu_b, b_glu, nct):
    la = y_all.shape[0]
    tm = TOK_TILE
    l = la - nct * tm

    def body(y_ref, w_ref, b_ref, o_ref):
        yg = _gelu(y_ref[...])
        z = _dot(yg.astype(BF16), w_ref[...]) + b_ref[...]
        o_ref[...] = (yg * _sigmoid(z)).astype(BF16)

    return pl.pallas_call(
        body, name="s5_glu", grid=(l // tm,),
        in_specs=[pl.BlockSpec((tm, S5_WIDTH), lambda i: (i + nct, 0)),
                  _full((S5_WIDTH, S5_WIDTH)), _full((1, S5_WIDTH))],
        out_specs=pl.BlockSpec((tm, S5_WIDTH), lambda i: (i, 0)),
        out_shape=jax.ShapeDtypeStruct((l, S5_WIDTH), BF16),
        compiler_params=_params(("parallel",)),
    )(y_all, w_glu_b, b_glu)


def _s5_glu_bwd(y_all, dmix, w_glu_b, b_glu, nct):
    la = y_all.shape[0]
    tm = TOK_TILE

    def body(y_ref, d_ref, w_ref, b_ref, dy_ref, gw_ref, gb_ref):
        i = pl.program_id(0)

        @pl.when(i == 0)
        def _():
            gw_ref[...] = jnp.zeros_like(gw_ref)
            gb_ref[...] = jnp.zeros_like(gb_ref)

        @pl.when(i < nct)
        def _():
            dy_ref[...] = jnp.zeros_like(dy_ref)

        @pl.when(i >= nct)
        def _():
            y = y_ref[...]
            yg, gelu_vjp = jax.vjp(_gelu, y)
            ygb = yg.astype(BF16)
            sg = _sigmoid(_dot(ygb, w_ref[...]) + b_ref[...])
            ds = d_ref[...]
            dz = ds * yg * sg * (1.0 - sg)
            dzb = dz.astype(BF16)
            dyg = ds * sg + _dot_nt(dzb, w_ref[...])
            dy_ref[...] = gelu_vjp(dyg)[0]
            gw_ref[...] += _dot_tn(ygb, dzb)
            gb_ref[...] += jnp.sum(dz, axis=0, keepdims=True)

    return pl.pallas_call(
        body, name="s5_glu_bwd", grid=(la // tm,),
        in_specs=[pl.BlockSpec((tm, S5_WIDTH), lambda i: (i, 0)),
                  pl.BlockSpec((tm, S5_WIDTH), lambda i: (jnp.maximum(i - nct, 0), 0)),
                  _full((S5_WIDTH, S5_WIDTH)), _full((1, S5_WIDTH))],
        out_specs=[pl.BlockSpec((tm, S5_WIDTH), lambda i: (i, 0)), _full((S5_WIDTH, S5_WIDTH)),
                   _full((1, S5_WIDTH))],
        out_shape=[jax.ShapeDtypeStruct((la, S5_WIDTH), F32), jax.ShapeDtypeStruct((S5_WIDTH, S5_WIDTH), F32),
                   jax.ShapeDtypeStruct((1, S5_WIDTH), F32)],
        compiler_params=_params(("arbitrary",)),
    )(y_all, dmix, w_glu_b, b_glu)


K_SCALE = RET_DH ** -0.5
Q_COL, K_COL, V_COL, G_COL = 4, 8, 12, 16


def _ret_chunk_of(step, ncc, nch, rev):
    if not rev:
        return step
    return jnp.where(step < ncc, ncc - 1 - step, nch - 1 - (step - ncc))


def _ret_decay(ld, rev):
    c = _iota2((RET_CHUNK, RET_CHUNK), 0).astype(F32)
    m = _iota2((RET_CHUNK, RET_CHUNK), 1).astype(F32)
    diff = (m - c) if rev else (c - m)
    keep = (diff > 0) if rev else (diff >= 0)
    expo = jnp.maximum(diff, 0.0)
    dm = jnp.where(keep, jnp.exp(ld * expo), 0.0)
    xi_e = (RET_CHUNK - c) if rev else (c + 1.0)
    zeta_e = c if rev else (RET_CHUNK - 1.0 - c)
    return dm, expo, jnp.exp(ld * xi_e), xi_e, jnp.exp(ld * zeta_e), zeta_e


RET_TABLES = 7


def _ret_tables(ld2):
    def body(ld_ref, t_ref):
        dr, h = pl.program_id(0), pl.program_id(1)
        ldh = ld_ref[dr, h]
        for rev in (False, True):
            @pl.when(dr == int(rev))
            def _(rev=rev):
                dm, expo, xi, xi_e, zeta, zeta_e = _ret_decay(ldh, rev)
                t_ref[0, 0, 0] = dm
                t_ref[0, 0, 1] = dm * expo
                t_ref[0, 0, 2] = xi
                t_ref[0, 0, 3] = xi * xi_e
                t_ref[0, 0, 4] = zeta
                t_ref[0, 0, 5] = zeta * zeta_e
                t_ref[0, 0, 6] = jnp.zeros_like(dm) + jnp.exp(ldh * RET_CHUNK)

    return pl.pallas_call(
        body, name="ret_tables", grid=(2, RET_HEADS),
        in_specs=[pl.BlockSpec(memory_space=pltpu.SMEM)],
        out_specs=pl.BlockSpec((1, 1, RET_TABLES, RET_CHUNK, RET_CHUNK), lambda d, h: (d, h, 0, 0, 0)),
        out_shape=jax.ShapeDtypeStruct((2, RET_HEADS, RET_TABLES, RET_CHUNK, RET_CHUNK), F32),
        compiler_params=_params(("parallel", "parallel")),
    )(ld2)


def _ret_specs(nch, ncc, rev, step_of):
    chunk = lambda n: _ret_chunk_of(step_of(n), ncc, nch, rev)
    cols = [pl.BlockSpec((RET_CHUNK, RET_WIDTH), functools.partial(lambda n, cb: (chunk(n), cb), cb=cb))
            for cb in (1, 2, 3)]
    tab = pl.BlockSpec((RET_CHUNK, RET_DH), lambda n: (chunk(n), 0))
    return cols + [tab, tab], pl.BlockSpec((RET_CHUNK, RET_WIDTH), lambda n: (chunk(n), 0))


def _ret_scan(p_all, cos_t, sin_t, tabs, ncc):
    la = p_all.shape[0]
    nch = la // RET_CHUNK

    def body(t_ref, qf, kf, vf, cf, sf, qb, kb, vb, cb, sb, of_ref, ob_ref, ssf_ref, ssb_ref, s_scr):
        @pl.when(pl.program_id(0) == 0)
        def _():
            s_scr[...] = jnp.zeros_like(s_scr)

        for dr, (q_ref, k_ref, v_ref, c_ref, n_ref, o_ref, ss_ref) in enumerate(
                ((qf, kf, vf, cf, sf, of_ref, ssf_ref), (qb, kb, vb, cb, sb, ob_ref, ssb_ref))):
            cs, sn = c_ref[...], n_ref[...]
            for h in range(RET_HEADS):
                sl = slice(RET_DH * h, RET_DH * (h + 1))
                dm, xi, zeta = t_ref[dr, h, 0], t_ref[dr, h, 2], t_ref[dr, h, 4]
                q = _rope(q_ref[:, sl], cs, sn)
                k = _rope(k_ref[:, sl] * K_SCALE, cs, sn)
                vh = v_ref[:, sl].astype(BF16)
                s = s_scr[dr, h]
                ss_ref[0, h] = s
                sc = (_dot_nt(q.astype(BF16), k.astype(BF16)) * dm).astype(BF16)
                o_ref[:, sl] = _dot(sc, vh) + _dot((q * xi).astype(BF16), s.astype(BF16))
                s_scr[dr, h] = t_ref[dr, h, 6] * s + _dot_tn((k * zeta).astype(BF16), vh)

    in_f, out_f = _ret_specs(nch, ncc, False, lambda n: n)
    in_b, out_b = _ret_specs(nch, ncc, True, lambda n: n)
    ss_spec = pl.BlockSpec((1, RET_HEADS, RET_DH, RET_DH), lambda n: (n, 0, 0, 0))
    o_shape = jax.ShapeDtypeStruct((la, RET_WIDTH), F32)
    ss_shape = jax.ShapeDtypeStruct((nch, RET_HEADS, RET_DH, RET_DH), F32)
    return pl.pallas_call(
        body, name="ret_scan", grid=(nch,),
        in_specs=[_full(tabs.shape)] + in_f + in_b,
        out_specs=[out_f, out_b, ss_spec, ss_spec],
        out_shape=[o_shape, o_shape, ss_shape, ss_shape],
        scratch_shapes=[pltpu.VMEM((2, RET_HEADS, RET_DH, RET_DH), F32)],
        compiler_params=_params(("arbitrary",)),
    )(tabs, p_all, p_all, p_all, cos_t, sin_t, p_all, p_all, p_all, cos_t, sin_t)


def _ret_scan_bwd(p_all, cos_t, sin_t, tabs, ssf, ssb, dy_all, ncc):
    la = p_all.shape[0]
    nch = la // RET_CHUNK

    def body(t_ref, qf, kf, vf, cf, sf, dof, ssf_ref, qb, kb, vb, cb, sb, dob_, ssb_ref,
             dqf, dkf, dvf, dqb, dkb, dvb, dld_ref, ds_scr):
        @pl.when(pl.program_id(0) == 0)
        def _():
            ds_scr[...] = jnp.zeros_like(ds_scr)
            dld_ref[...] = jnp.zeros_like(dld_ref)

        for dr, (q_ref, k_ref, v_ref, c_ref, n_ref, do_ref, ss_ref, dq_ref, dk_ref, dv_ref) in enumerate(
                ((qf, kf, vf, cf, sf, dof, ssf_ref, dqf, dkf, dvf), (qb, kb, vb, cb, sb, dob_, ssb_ref, dqb, dkb, dvb))):
            cs, sn = c_ref[...], n_ref[...]
            for h in range(RET_HEADS):
                sl = slice(RET_DH * h, RET_DH * (h + 1))
                dm, dm_d, xi, xi_d, zeta, zeta_d, gc = [t_ref[dr, h, t] for t in range(RET_TABLES)]
                q = _rope(q_ref[:, sl], cs, sn)
                k = _rope(k_ref[:, sl] * K_SCALE, cs, sn)
                q16, k16, v16 = q.astype(BF16), k.astype(BF16), v_ref[:, sl].astype(BF16)
                s = ss_ref[0, h]
                s16 = s.astype(BF16)
                ds_in = ds_scr[dr, h]
                ds16 = ds_in.astype(BF16)
                do16 = do_ref[:, sl].astype(BF16)
                qk = _dot_nt(q16, k16)
                dsv = _dot_nt(do16, v16)
                dsc = (dsv * dm).astype(BF16)
                sc16 = (qk * dm).astype(BF16)
                dos = _dot_nt(do16, s16)
                vds = _dot_nt(v16, ds16)
                dq_ref[:, sl] = _dot(dsc, k16) + dos * xi
                dk_ref[:, sl] = _dot_tn(dsc, q16) + vds * zeta
                dv_ref[:, sl] = _dot_tn(sc16, do16) + _dot((k * zeta).astype(BF16), ds16)
                ds_scr[dr, h] = _dot_tn((q * xi).astype(BF16), do16) + gc * ds_in
                dld = jnp.sum(dsv * qk * dm_d + q * dos * xi_d + k * vds * zeta_d + RET_CHUNK * gc * s * ds_in)
                dld_ref[dr, h] += dld

    back = lambda n: nch - 1 - n
    in_f, out_f = _ret_specs(nch, ncc, False, back)
    in_b, out_b = _ret_specs(nch, ncc, True, back)
    ss_spec = pl.BlockSpec((1, RET_HEADS, RET_DH, RET_DH), lambda n: (nch - 1 - n, 0, 0, 0))
    shp = jax.ShapeDtypeStruct((la, RET_WIDTH), F32)
    return pl.pallas_call(
        body, name="ret_scan_bwd", grid=(nch,),
        in_specs=[_full(tabs.shape)] + in_f + [out_f, ss_spec] + in_b + [out_b, ss_spec],
        out_specs=[out_f, out_f, out_f, out_b, out_b, out_b, _full((2, RET_HEADS, 8, 128))],
        out_shape=[shp] * 6 + [jax.ShapeDtypeStruct((2, RET_HEADS, 8, 128), F32)],
        scratch_shapes=[pltpu.VMEM((2, RET_HEADS, RET_DH, RET_DH), F32)],
        compiler_params=_params(("arbitrary",)),
    )(tabs, p_all, p_all, p_all, cos_t, sin_t, dy_all, ssf, p_all, p_all, p_all, cos_t, sin_t, dy_all, ssb)


def _ret_gate(of, ob, p_all, nct):
    la = of.shape[0]
    tm = TOK_TILE
    l = la - nct * tm

    def body(of_ref, ob_ref, g_ref, r_ref, y_ref):
        y = of_ref[...] + ob_ref[...]
        y_ref[...] = y
        for h in range(RET_HEADS):
            sl = slice(RET_DH * h, RET_DH * (h + 1))
            r_ref[:, sl] = _head_norm_gate(y[:, sl], g_ref[:, sl]).astype(BF16)

    row = pl.BlockSpec((tm, RET_WIDTH), lambda i: (i + nct, 0))
    out = pl.BlockSpec((tm, RET_WIDTH), lambda i: (i, 0))
    return pl.pallas_call(
        body, name="ret_gate", grid=(l // tm,),
        in_specs=[row, row, pl.BlockSpec((tm, RET_WIDTH), lambda i: (i + nct, G_COL // 4))],
        out_specs=[out, out],
        out_shape=[jax.ShapeDtypeStruct((l, RET_WIDTH), BF16), jax.ShapeDtypeStruct((l, RET_WIDTH), F32)],
        compiler_params=_params(("parallel",)),
    )(of, ob, p_all)


def _ret_gate_bwd(y_ret, p_all, dmix, nct):
    la = p_all.shape[0]
    tm = TOK_TILE

    def body(y_ref, g_ref, d_ref, dy_ref, dg_ref):
        i = pl.program_id(0)

        @pl.when(i < nct)
        def _():
            dy_ref[...] = jnp.zeros_like(dy_ref)
            dg_ref[...] = jnp.zeros_like(dg_ref)

        @pl.when(i >= nct)
        def _():
            for h in range(RET_HEADS):
                sl = slice(RET_DH * h, RET_DH * (h + 1))
                _, vjp = jax.vjp(_head_norm_gate, y_ref[:, sl], g_ref[:, sl])
                dy, dg = vjp(d_ref[:, sl])
                dy_ref[:, sl] = dy
                dg_ref[:, sl] = dg

    xrow = lambda cb: pl.BlockSpec((tm, RET_WIDTH), lambda i: (jnp.maximum(i - nct, 0), cb))
    out = pl.BlockSpec((tm, RET_WIDTH), lambda i: (i, 0))
    shp = jax.ShapeDtypeStruct((la, RET_WIDTH), F32)
    return pl.pallas_call(
        body, name="ret_gate_bwd", grid=(la // tm,),
        in_specs=[xrow(0), pl.BlockSpec((tm, RET_WIDTH), lambda i: (i, G_COL // 4)), xrow(1)],
        out_specs=[out, out], out_shape=[shp, shp],
        compiler_params=_params(("parallel",)),
    )(y_ret, p_all, dmix)


def _in_bwd(dqf, dkf, dvf, dqb, dkb, dvb, du, dg, cos_t, sin_t, w_in_b, x, ctx, n1w, mod4, dx1):
    l, lc = x.shape[0], ctx.shape[0]
    la = l + lc
    tm = TOK_TILE
    nct = lc // tm

    def body(dqf_ref, dkf_ref, dvf_ref, dqb_ref, dkb_ref, dvb_ref, du_ref, dg_ref, cos_ref, sin_ref,
             w_ref, x_ref, c_ref, nw_ref, mod_ref, dx1_ref, dp_ref, gx_ref, acc_ref):
        i = pl.program_id(0)
        is_ctx = i < nct

        @pl.when(i == 0)
        def _():
            acc_ref[...] = jnp.zeros_like(acc_ref)

        cs, sn = cos_ref[...], sin_ref[...]
        dp_ref[:, 0:S5_WIDTH] = du_ref[...].astype(BF16)
        for h in range(RET_HEADS):
            sl = slice(RET_DH * h, RET_DH * (h + 1))
            dq = _rope_t(dqf_ref[:, sl] + dqb_ref[:, sl], cs, sn)
            dk = _rope_t(dkf_ref[:, sl] + dkb_ref[:, sl], cs, sn) * K_SCALE
            dp_ref[:, 128 * (Q_COL + h):128 * (Q_COL + h + 1)] = dq.astype(BF16)
            dp_ref[:, 128 * (K_COL + h):128 * (K_COL + h + 1)] = dk.astype(BF16)
        dp_ref[:, 128 * V_COL:128 * G_COL] = (dvf_ref[...] + dvb_ref[...]).astype(BF16)
        dp_ref[:, 128 * G_COL:IN_COLS] = dg_ref[...].astype(BF16)

        dh1 = _dot_nt(dp_ref[...], w_ref[...])
        xt = jnp.where(is_ctx, c_ref[...], x_ref[...])
        sh = jnp.where(is_ctx, mod_ref[0:1, :], mod_ref[2:3, :])
        sc = jnp.where(is_ctx, mod_ref[1:2, :], mod_ref[3:4, :])
        _, vjp = jax.vjp(_rms_mod, xt, nw_ref[...], sh, sc)
        dx, dnw, dsh, dsc = vjp(dh1)
        gx_ref[...] = dx + dx1_ref[...]
        cf = jnp.where(is_ctx, 1.0, 0.0)
        acc_ref[0:1, :] += dnw
        acc_ref[1:2, :] += cf * dsh
        acc_ref[2:3, :] += cf * dsc
        acc_ref[3:4, :] += (1.0 - cf) * dsh
        acc_ref[4:5, :] += (1.0 - cf) * dsc

    row = pl.BlockSpec((tm, RET_WIDTH), lambda i: (i, 0))
    tab = pl.BlockSpec((tm, RET_DH), lambda i: (i, 0))
    xrow = pl.BlockSpec((tm, D_MODEL), lambda i: (jnp.maximum(i - nct, 0), 0))
    return pl.pallas_call(
        body, name="in_bwd", grid=(la // tm,),
        in_specs=[row] * 8 + [tab, tab, _full((D_MODEL, IN_COLS)), xrow,
                              pl.BlockSpec((tm, D_MODEL), lambda i: (jnp.minimum(i, nct - 1), 0)),
                              _full((1, D_MODEL)), _full((4, D_MODEL)), xrow],
        out_specs=[pl.BlockSpec((tm, IN_COLS), lambda i: (i, 0)), xrow, _full((8, D_MODEL))],
        out_shape=[jax.ShapeDtypeStruct((la, IN_COLS), BF16), jax.ShapeDtypeStruct((l, D_MODEL), F32),
                   jax.ShapeDtypeStruct((8, D_MODEL), F32)],
        compiler_params=_params(("arbitrary",)),
    )(dqf, dkf, dvf, dqb, dkb, dvb, du, dg, cos_t, sin_t, w_in_b, x, ctx, n1w, mod4, dx1)


def _outproj_up(x, s5x, retx, w_out_b, mod3, n2w, w_up_b):
    l = x.shape[0]
    tm = TOK_TILE

    def body(x_ref, s_ref, r_ref, wo_ref, mod_ref, nw_ref, wu_ref, x1_ref, mix_ref, h2_ref, up_ref):
        mix = _dot(s_ref[...], wo_ref[0:S5_WIDTH, :]) + _dot(r_ref[...], wo_ref[S5_WIDTH:D_MODEL, :])
        mix_ref[...] = mix
        x1 = x_ref[...] + mod_ref[0:1, :] * mix
        x1_ref[...] = x1
        h2 = _rms_mod(x1, nw_ref[...], mod_ref[1:2, :], mod_ref[2:3, :]).astype(BF16)
        h2_ref[...] = h2
        up_ref[...] = _dot(h2, wu_ref[...])

    row = lambda w: pl.BlockSpec((tm, w), lambda i: (i, 0))
    return pl.pallas_call(
        body, name="outproj_up", grid=(l // tm,),
        in_specs=[row(D_MODEL), row(S5_WIDTH), row(RET_WIDTH), _full((D_MODEL, D_MODEL)), _full((3, D_MODEL)),
                  _full((1, D_MODEL)), _full((D_MODEL, 2 * D_FF))],
        out_specs=[row(D_MODEL), row(D_MODEL), row(D_MODEL), row(2 * D_FF)],
        out_shape=[jax.ShapeDtypeStruct((l, D_MODEL), F32), jax.ShapeDtypeStruct((l, D_MODEL), F32),
                   jax.ShapeDtypeStruct((l, D_MODEL), BF16), jax.ShapeDtypeStruct((l, 2 * D_FF), F32)],
        compiler_params=_params(("parallel",)),
    )(x, s5x, retx, w_out_b, mod3, n2w, w_up_b)


HALO = 8


def _conv_taps(g, prev_row, next_row):
    t = g.shape[0]
    r = _iota2(g.shape, 0)
    gprev = jnp.where(r == 0, prev_row, pltpu.roll(g, 1, 0))
    gnext = jnp.where(r == t - 1, next_row, pltpu.roll(g, t - 1, 0))
    return gprev, gnext


def _ffn_loss(up, x1, conv_w, conv_b, w_down_b, gate, fnw, tgt):
    l = x1.shape[0]
    tm = TOK_TILE
    nt = l // tm
    hb = tm // HALO

    def body(up_a, up_g, hp_ref, hn_ref, x1_ref, cw_ref, cb_ref, wd_ref, gate_ref, fn_ref, tgt_ref,
             act_ref, dx2_ref, ddn_ref, dact_ref, acc_ref):
        i = pl.program_id(0)

        @pl.when(i == 0)
        def _():
            acc_ref[...] = jnp.zeros_like(acc_ref)

        g = up_g[...]
        prev_row = jnp.where(i == 0, 0.0, hp_ref[HALO - 1:HALO, :])
        next_row = jnp.where(i == nt - 1, 0.0, hn_ref[0:1, :])
        gprev, gnext = _conv_taps(g, prev_row, next_row)
        gc = cb_ref[...] + gprev * cw_ref[0:1, :] + g * cw_ref[1:2, :] + gnext * cw_ref[2:3, :]
        act = (_gelu(gc) * up_a[...]).astype(BF16)
        act_ref[...] = act
        dn = _dot(act, wd_ref[...])
        x2 = x1_ref[...] + gate_ref[...] * dn
        y, vjp = jax.vjp(_rms, x2, fn_ref[...])
        err = y - tgt_ref[...]
        dx2, dfn = vjp(err * (1.0 / D_MODEL))
        dx2_ref[...] = dx2
        ddn = (dx2 * gate_ref[...]).astype(BF16)
        ddn_ref[...] = ddn
        dact_ref[...] = _dot_nt(ddn, wd_ref[...])
        acc_ref[0:1, :] += dfn
        acc_ref[1:2, :] += jnp.sum(dx2 * dn, axis=0, keepdims=True)
        acc_ref[2:3, :] += (0.5 / D_MODEL) * jnp.sum(err * err)

    row = lambda w: pl.BlockSpec((tm, w), lambda i: (i, 0))
    last = l // HALO - 1
    return pl.pallas_call(
        body, name="ffn_loss", grid=(nt,),
        in_specs=[pl.BlockSpec((tm, D_FF), lambda i: (i, 0)), pl.BlockSpec((tm, D_FF), lambda i: (i, 1)),
                  pl.BlockSpec((HALO, D_FF), lambda i: (jnp.maximum(i * hb - 1, 0), 1)),
                  pl.BlockSpec((HALO, D_FF), lambda i: (jnp.minimum((i + 1) * hb, last), 1)),
                  row(D_MODEL), _full((3, D_FF)), _full((1, D_FF)), _full((D_FF, D_MODEL)),
                  _full((1, D_MODEL)), _full((1, D_MODEL)), row(D_MODEL)],
        out_specs=[row(D_FF), row(D_MODEL), row(D_MODEL), row(D_FF), _full((8, D_MODEL))],
        out_shape=[jax.ShapeDtypeStruct((l, D_FF), BF16), jax.ShapeDtypeStruct((l, D_MODEL), F32),
                   jax.ShapeDtypeStruct((l, D_MODEL), BF16), jax.ShapeDtypeStruct((l, D_FF), F32),
                   jax.ShapeDtypeStruct((8, D_MODEL), F32)],
        compiler_params=_params(("arbitrary",)),
    )(up, up, up, up, x1, conv_w, conv_b, w_down_b, gate, fnw, tgt)


def _convglu_bwd(up, dact, conv_w, conv_b):
    l = up.shape[0]
    tm = 128
    nt = l // tm
    hb = tm // HALO
    te = tm + 2 * HALO

    def body(a_ref, ap_ref, an_ref, g_ref, gp_ref, gn_ref, d_ref, dp_ref, dn_ref, cw_ref, cb_ref,
             dup_ref, acc_ref):
        i = pl.program_id(0)

        @pl.when(i == 0)
        def _():
            acc_ref[...] = jnp.zeros_like(acc_ref)

        row = _iota2((te, D_FF), 0) + (i * tm - HALO)
        valid = (row >= 0) & (row < l)

        def ext(p, c, n):
            return jnp.where(valid, jnp.concatenate([p[...], c[...], n[...]], axis=0), 0.0)

        ae, ge, de = ext(ap_ref, a_ref, an_ref), ext(gp_ref, g_ref, gn_ref), ext(dp_ref, d_ref, dn_ref)
        gprev = pltpu.roll(ge, 1, 0)
        gnext = pltpu.roll(ge, te - 1, 0)
        w0, w1, w2 = cw_ref[0:1, :], cw_ref[1:2, :], cw_ref[2:3, :]
        gce = cb_ref[...] + gprev * w0 + ge * w1 + gnext * w2
        _, vjp = jax.vjp(lambda a, gc: _gelu(gc) * a, ae, gce)
        dae, dgce = vjp(de)
        dge = dgce * w1 + pltpu.roll(dgce, te - 1, 0) * w0 + pltpu.roll(dgce, 1, 0) * w2
        mid = slice(HALO, HALO + tm)
        dup_ref[:, 0:D_FF] = dae[mid].astype(BF16)
        dup_ref[:, D_FF:2 * D_FF] = dge[mid].astype(BF16)
        dgc = dgce[mid]
        acc_ref[0:1, :] += jnp.sum(dgc * gprev[mid], axis=0, keepdims=True)
        acc_ref[1:2, :] += jnp.sum(dgc * ge[mid], axis=0, keepdims=True)
        acc_ref[2:3, :] += jnp.sum(dgc * gnext[mid], axis=0, keepdims=True)
        acc_ref[3:4, :] += jnp.sum(dgc, axis=0, keepdims=True)

    last = l // HALO - 1

    def trio(cb):
        return [pl.BlockSpec((tm, D_FF), lambda i: (i, cb)),
                pl.BlockSpec((HALO, D_FF), lambda i: (jnp.maximum(i * hb - 1, 0), cb)),
                pl.BlockSpec((HALO, D_FF), lambda i: (jnp.minimum((i + 1) * hb, last), cb))]

    return pl.pallas_call(
        body, name="convglu_bwd", grid=(nt,),
        in_specs=trio(0) + trio(1) + trio(0) + [_full((3, D_FF)), _full((1, D_FF))],
        out_specs=[pl.BlockSpec((tm, 2 * D_FF), lambda i: (i, 0)), _full((8, D_FF))],
        out_shape=[jax.ShapeDtypeStruct((l, 2 * D_FF), BF16), jax.ShapeDtypeStruct((8, D_FF), F32)],
        compiler_params=_params(("arbitrary",)),
    )(up, up, up, up, up, up, dact, dact, dact, conv_w, conv_b)


def _up_bwd(dup, w_up_b, w_out_b, x1, dx2, mix, mod3, n2w):
    l = x1.shape[0]
    tm = TOK_TILE

    def body(dup_ref, wu_ref, wo_ref, x1_ref, dx2_ref, mix_ref, mod_ref, nw_ref, dx1_ref, dmixb_ref, dmix_ref, acc_ref):
        @pl.when(pl.program_id(0) == 0)
        def _():
            acc_ref[...] = jnp.zeros_like(acc_ref)

        dh2 = _dot_nt(dup_ref[...], wu_ref[...])
        _, vjp = jax.vjp(_rms_mod, x1_ref[...], nw_ref[...], mod_ref[1:2, :], mod_ref[2:3, :])
        dx, dnw, dsh, dsc = vjp(dh2)
        dx1 = dx + dx2_ref[...]
        dx1_ref[...] = dx1
        dmixb = (dx1 * mod_ref[0:1, :]).astype(BF16)
        dmixb_ref[...] = dmixb
        dmix_ref[...] = _dot_nt(dmixb, wo_ref[...])
        acc_ref[0:1, :] += dnw
        acc_ref[1:2, :] += jnp.sum(dx1 * mix_ref[...], axis=0, keepdims=True)
        acc_ref[2:3, :] += dsh
        acc_ref[3:4, :] += dsc

    row = pl.BlockSpec((tm, D_MODEL), lambda i: (i, 0))
    return pl.pallas_call(
        body, name="up_bwd", grid=(l // tm,),
        in_specs=[pl.BlockSpec((tm, 2 * D_FF), lambda i: (i, 0)), _full((D_MODEL, 2 * D_FF)),
                  _full((D_MODEL, D_MODEL)), row, row, row, _full((3, D_MODEL)), _full((1, D_MODEL))],
        out_specs=[row, row, row, _full((8, D_MODEL))],
        out_shape=[jax.ShapeDtypeStruct((l, D_MODEL), F32), jax.ShapeDtypeStruct((l, D_MODEL), BF16),
                   jax.ShapeDtypeStruct((l, D_MODEL), F32), jax.ShapeDtypeStruct((8, D_MODEL), F32)],
        compiler_params=_params(("arbitrary",)),
    )(dup, w_up_b, w_out_b, x1, dx2, mix, mod3, n2w)


MOD_ROWS = 16
MOD_COLS = 6 * D_MODEL // 4


def _mod_fwd(c_all, c_ctx, w_mod_b, b_loc):
    def body(c_ref, cc_ref, w_ref, b_ref, m_ref, s_ref):
        cond = jnp.concatenate([c_ref[...], jnp.broadcast_to(cc_ref[...], (8, D_MODEL))], axis=0)
        s = _silu(cond).astype(BF16)
        s_ref[...] = s
        m_ref[...] = _dot(s, w_ref[...]) + b_ref[...]

    return pl.pallas_call(
        body, name="mod_fwd",
        out_shape=[jax.ShapeDtypeStruct((MOD_ROWS, MOD_COLS), F32), jax.ShapeDtypeStruct((MOD_ROWS, D_MODEL), BF16)],
        compiler_params=_params(),
    )(c_all, c_ctx, w_mod_b, b_loc)


def _mod_bwd_sum(dm_all):
    def body(d_ref, dm_ref, gb_ref):
        rows = [d_ref[k, 0:1, :] for k in range(8)]
        ctx_sum = d_ref[0, 1:2, :]
        for k in range(1, 8):
            ctx_sum = ctx_sum + d_ref[k, 1:2, :]
        gb = ctx_sum
        for k in range(8):
            gb = gb + rows[k]
        gb_ref[...] = gb
        dm_ref[...] = jnp.concatenate(rows + [ctx_sum] + [jnp.zeros((7, 6 * D_MODEL), F32)], axis=0)

    return pl.pallas_call(
        body, name="mod_bwd_sum",
        out_shape=[jax.ShapeDtypeStruct((MOD_ROWS, 6 * D_MODEL), F32), jax.ShapeDtypeStruct((1, 6 * D_MODEL), F32)],
        compiler_params=_params(),
    )(dm_all)


def _mod_bwd_w(dm_loc, s_b, c_ctx, w_mod_b):
    def body(d_ref, s_ref, cc_ref, w_ref, gw_ref, gc_ref):
        db = d_ref[...].astype(BF16)
        gw_ref[...] = _dot_tn(s_ref[...], db)
        ds = _dot_nt(db, w_ref[...])
        _, vjp = jax.vjp(_silu, cc_ref[...])
        gc_ref[...] = jnp.broadcast_to(vjp(ds[8:9, :])[0], (8, D_MODEL))

    return pl.pallas_call(
        body, name="mod_bwd_w",
        out_shape=[jax.ShapeDtypeStruct((D_MODEL, MOD_COLS), F32), jax.ShapeDtypeStruct((8, D_MODEL), F32)],
        compiler_params=_params(),
    )(dm_loc, s_b, c_ctx, w_mod_b)


def _adamw(w, g, m, v, name):
    r, c = w.shape
    tr = _pick(r, (256, 128, 64, 32, 16, 8))
    bc1 = 1.0 - ADAM_B1 ** ADAM_STEP
    bc2 = 1.0 - ADAM_B2 ** ADAM_STEP

    def body(w_ref, g_ref, m_ref, v_ref, d_ref, nm_ref, nv_ref):
        gg = g_ref[...]
        nm = ADAM_B1 * m_ref[...] + (1.0 - ADAM_B1) * gg
        nv = ADAM_B2 * v_ref[...] + (1.0 - ADAM_B2) * (gg * gg)
        nm_ref[...] = nm
        nv_ref[...] = nv
        d_ref[...] = -ADAM_LR * ((nm / bc1) / (jnp.sqrt(nv / bc2) + ADAM_EPS) + ADAM_WD * w_ref[...])

    blk = pl.BlockSpec((tr, c), lambda i: (i, 0))
    shp = jax.ShapeDtypeStruct((r, c), F32)
    return pl.pallas_call(
        body, name=name, grid=(r // tr,), in_specs=[blk] * 4, out_specs=[blk] * 3, out_shape=[shp] * 3,
        compiler_params=_params(("parallel",)),
    )(w, g, m, v)


def _sum_slots(a, name):
    n, r, c = a.shape
    tr = _pick(r, (376, 256, 208, 128, 64, 32, 16, 8))

    def body(a_ref, o_ref):
        acc = a_ref[0].astype(F32)
        for k in range(1, n):
            acc = acc + a_ref[k].astype(F32)
        o_ref[...] = acc

    return pl.pallas_call(
        body, name=name, grid=(r // tr,),
        in_specs=[pl.BlockSpec((n, tr, c), lambda i: (0, i, 0))],
        out_specs=pl.BlockSpec((tr, c), lambda i: (i, 0)),
        out_shape=jax.ShapeDtypeStruct((r, c), F32),
        compiler_params=_params(("parallel",)),
    )(a)


def _mesh_pos():
    return lax.axis_index("x"), lax.axis_index("y"), lax.axis_index("c")


def _all_gather8(v, name):
    m_per, n = v.shape

    def body(x_ref, out_ref, send_sems, recv_sems, local_sem):
        x, y, c = _mesh_pos()
        me, sibling = (x, y, c), (x, y, 1 - c)
        chips = [(1 - x, y), (x, 1 - y), (1 - x, 1 - y)]

        def rows(px, py, pc):
            return out_ref.at[pl.ds((4 * px + 2 * py + pc) * m_per, m_per), :]

        def copy(k, block, to, src=None):
            return pltpu.make_async_remote_copy(
                src_ref=rows(*block) if src is None else src, dst_ref=rows(*block),
                send_sem=send_sems.at[k], recv_sem=recv_sems.at[k], device_id=to, device_id_type=MESH_ID)

        mine = pltpu.make_async_copy(x_ref, rows(*me), local_sem)
        mine.start()
        first = [copy(0, me, sibling, src=x_ref)]
        first += [copy(1 + j, me, (*chip, c), src=x_ref) for j, chip in enumerate(chips)]
        for cp in first:
            cp.start()
        passed = [copy(4 + j, (*chip, c), sibling) for j, chip in enumerate(chips)]
        for j, chip in enumerate(chips):
            copy(1 + j, (*chip, c), me).wait_recv()
            passed[j].start()
        copy(0, sibling, me).wait_recv()
        for j, chip in enumerate(chips):
            copy(4 + j, (*chip, 1 - c), me).wait_recv()
        for cp in first + passed:
            cp.wait_send()
        mine.wait()

    return pl.pallas_call(
        body, name=name,
        out_shape=jax.ShapeDtypeStruct((8 * m_per, n), v.dtype),
        in_specs=[pl.BlockSpec(memory_space=pltpu.VMEM)],
        out_specs=pl.BlockSpec(memory_space=pltpu.VMEM),
        scratch_shapes=[pltpu.SemaphoreType.DMA((7,)), pltpu.SemaphoreType.DMA((7,)), pltpu.SemaphoreType.DMA],
        compiler_params=_params(),
    )(v)


ANY = pl.BlockSpec(memory_space=pl.ANY)
PEER_CHIPS = lambda x, y: [(x, 1 - y), (1 - x, y), (1 - x, 1 - y)]


def _shard_region(ref, kind, k, rl, cl, r0, nr, c0, nc):
    if kind == "col":
        return ref.at[pl.ds(r0, nr), pl.ds(k * cl + c0, nc)]
    return ref.at[pl.ds(k * rl + r0, nr), pl.ds(c0, nc)]


def _place_shard(w, kind, chip, name):
    rl, cl = w.shape
    tr = _pick(rl, (256, 128, 64))
    nt = rl // tr

    def body(chip_ref, w_ref, o_ref):
        o_ref[...] = w_ref[...].astype(BF16)

    o_map = (lambda i, chip_ref: (i, chip_ref[0])) if kind == "col" else (lambda i, chip_ref: (chip_ref[0] * nt + i, 0))
    return pl.pallas_call(
        body, name=name,
        grid_spec=pltpu.PrefetchScalarGridSpec(
            num_scalar_prefetch=1, grid=(nt,),
            in_specs=[pl.BlockSpec((tr, cl), lambda i, chip_ref: (i, 0))], out_specs=pl.BlockSpec((tr, cl), o_map)),
        out_shape=jax.ShapeDtypeStruct((rl, 4 * cl) if kind == "col" else (4 * rl, cl), BF16),
        compiler_params=_params(("parallel",)),
    )(chip.reshape(1), w)


def _gather_weights(placed, kinds):
    n = len(placed)
    shard_shapes = [(p.shape[0], p.shape[1] // 4) if k == "col" else (p.shape[0] // 4, p.shape[1])
                    for p, k in zip(placed, kinds)]

    def body(*refs):
        outs = refs[n:2 * n]
        send_sems, recv_sems = refs[2 * n:]
        x, y, c = _mesh_pos()
        me = 2 * x + y
        peers = PEER_CHIPS(x, y)
        sends = []
        for a in range(n):
            rl, cl = shard_shapes[a]
            rh = rl // 2
            reg = functools.partial(_shard_region, outs[a], kinds[a], rl=rl, cl=cl, c0=0, nc=cl)
            for j, (px, py) in enumerate(peers):
                half = reg(k=me, r0=c * rh, nr=rh)
                cp = pltpu.make_async_remote_copy(
                    src_ref=half, dst_ref=half, send_sem=send_sems.at[a, j], recv_sem=recv_sems.at[a, j],
                    device_id=(px, py, c), device_id_type=MESH_ID)
                cp.start()
                sends.append(cp)
        for a in range(n):
            rl, cl = shard_shapes[a]
            rh = rl // 2
            reg = functools.partial(_shard_region, outs[a], kinds[a], rl=rl, cl=cl, c0=0, nc=cl)
            for j, (px, py) in enumerate(peers):
                got = reg(k=2 * px + py, r0=c * rh, nr=rh)
                pltpu.make_async_remote_copy(src_ref=got, dst_ref=got, send_sem=send_sems.at[a, j],
                                             recv_sem=recv_sems.at[a, j], device_id=(px, py, c),
                                             device_id_type=MESH_ID).wait_recv()
                fwd = pltpu.make_async_remote_copy(src_ref=got, dst_ref=got, send_sem=send_sems.at[a, 3 + j],
                                                   recv_sem=recv_sems.at[a, 3 + j], device_id=(x, y, 1 - c),
                                                   device_id_type=MESH_ID)
                fwd.start()
                sends.append(fwd)
        for a in range(n):
            rl, cl = shard_shapes[a]
            rh = rl // 2
            reg = functools.partial(_shard_region, outs[a], kinds[a], rl=rl, cl=cl, c0=0, nc=cl)
            for j, (px, py) in enumerate(peers):
                got = reg(k=2 * px + py, r0=(1 - c) * rh, nr=rh)
                pltpu.make_async_remote_copy(src_ref=got, dst_ref=got, send_sem=send_sems.at[a, 3 + j],
                                             recv_sem=recv_sems.at[a, 3 + j], device_id=(x, y, 1 - c),
                                             device_id_type=MESH_ID).wait_recv()
        for cp in sends:
            cp.wait_send()

    return pl.pallas_call(
        body, name="gather_weights",
        out_shape=[jax.ShapeDtypeStruct(p.shape, p.dtype) for p in placed],
        in_specs=[ANY] * n, out_specs=[ANY] * n, input_output_aliases={a: a for a in range(n)},
        scratch_shapes=[pltpu.SemaphoreType.DMA((n, 6)), pltpu.SemaphoreType.DMA((n, 6))],
        compiler_params=_params(),
    )(*placed)


def _half(kind, r, c):
    return (r // 2, c) if kind == "col" else (r, c // 2)


def _half_of(ref, kind, which):
    r, c = ref.shape
    hr, hc = _half(kind, r, c)
    return ref.at[pl.ds(which * hr, hr), :] if kind == "col" else ref.at[:, pl.ds(which * hc, hc)]


def _rs_sibling(grads, kinds):
    n = len(grads)

    def body(*refs):
        srcs, dsts = refs[:n], refs[n:2 * n]
        send_sems, recv_sems = refs[2 * n:]
        x, y, c = _mesh_pos()
        cps = [pltpu.make_async_remote_copy(src_ref=_half_of(srcs[a], kinds[a], 1 - c), dst_ref=dsts[a],
                                            send_sem=send_sems.at[a], recv_sem=recv_sems.at[a],
                                            device_id=(x, y, 1 - c), device_id_type=MESH_ID) for a in range(n)]
        for cp in cps:
            cp.start()
        for cp in cps:
            cp.wait()

    return pl.pallas_call(
        body, name="rs_sibling",
        out_shape=[jax.ShapeDtypeStruct(_half(k, *g.shape), g.dtype) for g, k in zip(grads, kinds)],
        in_specs=[ANY] * n, out_specs=[ANY] * n,
        scratch_shapes=[pltpu.SemaphoreType.DMA((n,)), pltpu.SemaphoreType.DMA((n,))],
        compiler_params=_params(),
    )(*grads)


def _pair_sum(gf, rv, kind, ci, name):
    r, c = rv.shape
    tr = _pick(r, (128, 64, 32, 16, 8))
    nt = r // tr

    def body(ci_ref, g_ref, r_ref, o_ref):
        o_ref[...] = (g_ref[...] + r_ref[...]).astype(BF16)

    g_map = (lambda i, ci_ref: (ci_ref[0] * nt + i, 0)) if kind == "col" else (lambda i, ci_ref: (i, ci_ref[0]))
    blk = pl.BlockSpec((tr, c), lambda i, ci_ref: (i, 0))
    return pl.pallas_call(
        body, name=name,
        grid_spec=pltpu.PrefetchScalarGridSpec(num_scalar_prefetch=1, grid=(nt,),
                                               in_specs=[pl.BlockSpec((tr, c), g_map), blk], out_specs=blk),
        out_shape=jax.ShapeDtypeStruct((r, c), BF16),
        compiler_params=_params(("parallel",)),
    )(ci.reshape(1), gf, rv)


def _rs_chips(pairs, kinds):
    n = len(pairs)
    shapes = []
    for p, k in zip(pairs, kinds):
        r, c = p.shape
        shapes.append((r, c // 4) if k == "col" else (r // 4, c))

    def body(*refs):
        srcs, dsts = refs[:n], refs[n:2 * n]
        send_sems, recv_sems = refs[2 * n:]
        x, y, c = _mesh_pos()
        me = 2 * x + y
        peers = PEER_CHIPS(x, y)
        cps = []
        for a in range(n):
            rl, cl = shapes[a]
            reg = functools.partial(_shard_region, srcs[a], kinds[a], rl=rl, cl=cl, r0=0, nr=rl, c0=0, nc=cl)
            for j, (px, py) in enumerate(peers):
                cps.append(pltpu.make_async_remote_copy(
                    src_ref=reg(k=2 * px + py), dst_ref=dsts[a].at[me], send_sem=send_sems.at[a, j],
                    recv_sem=recv_sems.at[a, j], device_id=(px, py, c), device_id_type=MESH_ID))
                cps[-1].start()
        for a in range(n):
            for j, (px, py) in enumerate(peers):
                slot = dsts[a].at[2 * px + py]
                pltpu.make_async_remote_copy(src_ref=slot, dst_ref=slot, send_sem=send_sems.at[a, j],
                                             recv_sem=recv_sems.at[a, j], device_id=(px, py, c),
                                             device_id_type=MESH_ID).wait_recv()
        for cp in cps:
            cp.wait_send()

    return pl.pallas_call(
        body, name="rs_chips",
        out_shape=[jax.ShapeDtypeStruct((4,) + s, p.dtype) for s, p in zip(shapes, pairs)],
        in_specs=[ANY] * n, out_specs=[ANY] * n,
        scratch_shapes=[pltpu.SemaphoreType.DMA((n, 3)), pltpu.SemaphoreType.DMA((n, 3))],
        compiler_params=_params(),
    )(*pairs)


def _sum_chips(pair, got, kind, pos, name):
    _, r, c = got.shape
    tr = _pick(r, (256, 128, 64, 32, 16))
    nt = r // tr

    def body(pos_ref, own_ref, g1_ref, g2_ref, g3_ref, o_ref):
        o_ref[...] = ((own_ref[...].astype(F32) + g1_ref[0].astype(F32)) + g2_ref[0].astype(F32)) + g3_ref[0].astype(F32)

    if kind == "col":
        own_map = lambda i, p: (i, p[1])
        out_map = lambda i, p: (p[0] * nt + i, 0)
        out_shape = (2 * r, c)
    else:
        own_map = lambda i, p: (p[1] * nt + i, 0)
        out_map = lambda i, p: (i, p[0])
        out_shape = (r, 2 * c)
    peer = lambda m: pl.BlockSpec((1, tr, c), lambda i, p: (p[1] ^ m, i, 0))
    return pl.pallas_call(
        body, name=name,
        grid_spec=pltpu.PrefetchScalarGridSpec(
            num_scalar_prefetch=1, grid=(nt,),
            in_specs=[pl.BlockSpec((tr, c), own_map), peer(1), peer(2), peer(3)],
            out_specs=pl.BlockSpec((tr, c), out_map)),
        out_shape=jax.ShapeDtypeStruct(out_shape, F32),
        compiler_params=_params(("parallel",)),
    )(pos, pair, got, got, got)


def _rs_back(halves, kinds):
    n = len(halves)

    def body(*refs):
        outs = refs[n:2 * n]
        send_sems, recv_sems = refs[2 * n:]
        x, y, c = _mesh_pos()
        cps = []
        for a in range(n):
            mine = _half_of(outs[a], kinds[a], c)
            cps.append(pltpu.make_async_remote_copy(src_ref=mine, dst_ref=mine, send_sem=send_sems.at[a],
                                                    recv_sem=recv_sems.at[a], device_id=(x, y, 1 - c),
                                                    device_id_type=MESH_ID))
            cps[-1].start()
        for a in range(n):
            other = _half_of(outs[a], kinds[a], 1 - c)
            pltpu.make_async_remote_copy(src_ref=other, dst_ref=other, send_sem=send_sems.at[a],
                                         recv_sem=recv_sems.at[a], device_id=(x, y, 1 - c),
                                         device_id_type=MESH_ID).wait_recv()
        for cp in cps:
            cp.wait_send()

    return pl.pallas_call(
        body, name="rs_back",
        out_shape=[jax.ShapeDtypeStruct(h.shape, h.dtype) for h in halves],
        in_specs=[ANY] * n, out_specs=[ANY] * n, input_output_aliases={a: a for a in range(n)},
        scratch_shapes=[pltpu.SemaphoreType.DMA((n,)), pltpu.SemaphoreType.DMA((n,))],
        compiler_params=_params(),
    )(*halves)


def _rope_tables(l, lc):
    rows = l // GRID_W
    row = jnp.repeat(jnp.arange(rows, dtype=F32), GRID_W)
    col = jnp.tile(jnp.arange(GRID_W, dtype=F32), rows)
    n_freq = RET_DH // 4
    inv_freq = ROPE_THETA ** (-jnp.arange(n_freq, dtype=F32) / n_freq)
    ang = jnp.concatenate([row[:, None] * inv_freq, col[:, None] * inv_freq], axis=-1)
    cos_t = jnp.repeat(jnp.cos(ang), 2, axis=-1)
    sin_t = jnp.repeat(jnp.sin(ang), 2, axis=-1) * jnp.tile(jnp.array([-1.0, 1.0], F32), RET_DH // 2)
    cos_t = jnp.concatenate([jnp.ones((lc, RET_DH), F32), cos_t], axis=0)
    sin_t = jnp.concatenate([jnp.zeros((lc, RET_DH), F32), sin_t], axis=0)
    return cos_t, sin_t


def _s5_pack(a):
    blk = lambda t: t.reshape(1, S5_NB, 128, S5_STATE)
    lre = jnp.stack([a["s5_lambda_re_f"][0], a["s5_lambda_re_b"][0]]).reshape(2, S5_NB, 8, S5_STATE)
    lim = jnp.stack([a["s5_lambda_im_f"][0], a["s5_lambda_im_b"][0]]).reshape(2, S5_NB, 8, S5_STATE)
    lst = jnp.stack([a["s5_log_step_f"][0], a["s5_log_step_b"][0]]).reshape(2, S5_NB, 8, 1)
    b_re = blk(a["s5_b_re"][0].transpose(0, 2, 1))
    b_im = blk(a["s5_b_im"][0].transpose(0, 2, 1))
    return (lre, lim, lst, b_re, b_im, blk(a["s5_c_re"][0]), blk(a["s5_c_im"][0]),
            a["s5_d"].reshape(1, S5_NB, 1, 128))


def _s5_unpack(g):
    glre, glim, glst, gbre, gbim, gcre, gcim, gd = g
    unb = lambda t: t.reshape(S5_GROUPS, S5_GROUP, S5_STATE).transpose(0, 2, 1)[None]
    return {
        "s5_lambda_re_f": glre[0].reshape(1, S5_GROUPS, S5_STATE), "s5_lambda_re_b": glre[1].reshape(1, S5_GROUPS, S5_STATE),
        "s5_lambda_im_f": glim[0].reshape(1, S5_GROUPS, S5_STATE), "s5_lambda_im_b": glim[1].reshape(1, S5_GROUPS, S5_STATE),
        "s5_log_step_f": glst[0].reshape(1, S5_GROUPS), "s5_log_step_b": glst[1].reshape(1, S5_GROUPS),
        "s5_b_re": unb(gbre), "s5_b_im": unb(gbim),
        "s5_c_re": gcre.reshape(1, S5_GROUPS, S5_GROUP, S5_STATE), "s5_c_im": gcim.reshape(1, S5_GROUPS, S5_GROUP, S5_STATE),
        "s5_d": gd.reshape(1, S5_WIDTH),
    }


def _local_step(a, wb, mx, mc, conv_w):
    x, ctx, tgt = a["x"][0], a["ctx"][0], a["loss_target"][0]
    l, lc = x.shape[0], ctx.shape[0]
    la = l + lc
    nct, ncc, nrc, cn = lc // TOK_TILE, lc // S5_T, lc // RET_CHUNK, la // S5_T
    n1w, n2w, fnw = a["norm1_w"], a["norm2_w"], a["final_norm_w"].reshape(1, D_MODEL)
    conv_b, b_glu = a["conv_b"], a["s5_b_glu"]
    ld2 = jnp.concatenate([a["ret_log_decay_f"], a["ret_log_decay_b"]], axis=0)
    mod4 = jnp.concatenate([mc[0:2], mx[0:2]], axis=0)
    mod3 = mx[2:5]
    gate5 = mx[5:6]
    cos_t, sin_t = _rope_tables(l, lc)
    s5p = _s5_pack(a)

    p_all, h1b = _norm_inproj(x, ctx, n1w, mod4, wb["w_in"])
    p3 = p_all.reshape(cn, S5_T, IN_COLS)
    kb, wst, wout, a16 = _s5_gen(*s5p)
    sloc = _s5_state(p3, wst)
    a16s = a16.transpose(1, 0, 2, 3).reshape(2, S5_GROUPS, 128)
    hs = _s5_scan(sloc, a16s, ncc)
    y_all = _s5_out(p3, kb, hs, wout).reshape(la, S5_WIDTH)
    s5x = _s5_glu(y_all, wb["s5_w_glu"], b_glu, nct)
    tabs = _ret_tables(ld2)
    of, ob, ssf, ssb = _ret_scan(p_all, cos_t, sin_t, tabs, nrc)
    retx, y_ret = _ret_gate(of, ob, p_all, nct)
    x1, mix, h2b, up = _outproj_up(x, s5x, retx, wb["w_out"], mod3, n2w, wb["w_up"])
    act, dx2, ddn, dact, acc_f = _ffn_loss(up, x1, conv_w, conv_b, wb["w_down"], gate5, fnw, tgt)

    g = {}
    g["w_down"] = _mm_tn(act, ddn, name="gw_down")
    dup, acc_c = _convglu_bwd(up, dact, conv_w, conv_b)
    g["w_up"] = _mm_tn(h2b, dup, name="gw_up")
    dx1, dmixb, dmix, acc_2 = _up_bwd(dup, wb["w_up"], wb["w_out"], x1, dx2, mix, mod3, n2w)
    g["w_out"] = jnp.concatenate([_mm_tn(s5x, dmixb, name="gw_out_s5"), _mm_tn(retx, dmixb, name="gw_out_ret")], axis=0)

    dy_s5, g["s5_w_glu"], g["s5_b_glu"] = _s5_glu_bwd(y_all, dmix, wb["s5_w_glu"], b_glu, nct)
    dy3 = dy_s5.reshape(cn, S5_T, S5_WIDTH)
    e = _s5_bwd_h(dy3, wout)
    ds, da16 = _s5_scan_bwd(e, hs, a16s, ncc)
    du = _s5_bwd_u(dy3, kb, ds, wst).reshape(la, S5_WIDTH)
    dkb = _s5_bwd_kb(p3, dy3)
    dwst = _s5_bwd_w(p3, ds, "s5_bwd_wst")
    dwout = _s5_bwd_w(dy3, hs, "s5_bwd_wout")
    da16p = da16.reshape(2, S5_NB, 8, 128).transpose(1, 0, 2, 3)
    g.update(_s5_unpack(_s5_gen_bwd(*s5p, dkb, dwst, dwout, da16p)))

    dy_ret, dg = _ret_gate_bwd(y_ret, p_all, dmix, nct)
    dqf, dkf, dvf, dqb, dkb_, dvb, dld = _ret_scan_bwd(p_all, cos_t, sin_t, tabs, ssf, ssb, dy_ret, nrc)
    g["ret_log_decay_f"] = dld[0, :, 0, 0].reshape(1, RET_HEADS)
    g["ret_log_decay_b"] = dld[1, :, 0, 0].reshape(1, RET_HEADS)
    dp, grad_x, acc_1 = _in_bwd(dqf, dkf, dvf, dqb, dkb_, dvb, du, dg, cos_t, sin_t, wb["w_in"], x, ctx, n1w, mod4, dx1)
    g["w_in"] = _mm_tn(h1b, dp, name="gw_in")

    g["norm1_w"], g["norm2_w"], g["final_norm_w"] = acc_1[0:1], acc_2[0:1], acc_f[0]
    g["conv_w"], g["conv_b"] = acc_c[0:3], acc_c[3:4]
    zero = jnp.zeros((1, D_MODEL), F32)
    dmx = jnp.concatenate([acc_1[3:5], acc_2[1:2], acc_2[2:4], acc_f[1:2]], axis=0)
    dmc = jnp.concatenate([acc_1[1:3], zero, zero, zero, zero], axis=0)
    return acc_f[2, 0], grad_x, g, dmx, dmc


WEIGHT_NAMES = ("c_ctx", "w_mod", "b_mod", "norm1_w", "w_in", "s5_lambda_re_f", "s5_lambda_im_f", "s5_log_step_f",
                "s5_lambda_re_b", "s5_lambda_im_b", "s5_log_step_b", "s5_b_re", "s5_b_im", "s5_c_re", "s5_c_im",
                "s5_d", "s5_w_glu", "s5_b_glu", "ret_log_decay_f", "ret_log_decay_b", "w_out", "norm2_w", "w_up",
                "conv_w", "conv_b", "w_down", "final_norm_w")
BIG_NAMES = ("w_in", "w_out", "w_up", "w_down", "s5_w_glu")
BIG_KINDS = ("col", "row", "col", "row", "row")
SMALL_NAMES = ("norm1_w", "norm2_w", "final_norm_w", "conv_b", "conv_w", "s5_lambda_re_f", "s5_lambda_im_f",
               "s5_log_step_f", "s5_lambda_re_b", "s5_lambda_im_b", "s5_log_step_b", "s5_b_re", "s5_b_im", "s5_c_re",
               "s5_c_im", "s5_d", "s5_b_glu", "ret_log_decay_f", "ret_log_decay_b")
ROW = 1024
N_CHIPS = 4


def _pack_rows(parts):
    flat = jnp.concatenate([p.reshape(-1) for p in parts])
    n = flat.shape[0]
    rows = -(-n // (8 * ROW)) * 8
    return jnp.pad(flat, (0, rows * ROW - n)).reshape(rows, ROW)


def _unpack_rows(packed, shapes):
    flat = packed.reshape(-1)
    out, off = [], 0
    for s in shapes:
        n = math.prod(s)
        out.append(flat[off:off + n].reshape(s))
        off += n
    return out


def _step(a):
    xi, yi, ci = _mesh_pos()
    chip = 2 * xi + yi
    dev = 2 * chip + ci

    cw_loc = a["conv_w"].reshape(-1)
    small_in = jnp.concatenate([a["c"].reshape(-1), jnp.pad(cw_loc, (0, 24 * 128 - cw_loc.shape[0]))]).reshape(32, 128)
    sg = _all_gather8(small_in, "gather_cond").reshape(8, 32, 128)
    c_all = sg[:, 0:8].reshape(8, D_MODEL)
    conv_w = sg[0::2, 8:32].reshape(N_CHIPS, -1)[:, :cw_loc.shape[0]].reshape(N_CHIPS, 3, -1)
    conv_w = conv_w.transpose(1, 0, 2).reshape(3, D_FF)

    placed = [_place_shard(a[n][0], k, chip, "place_" + n) for n, k in zip(BIG_NAMES, BIG_KINDS)]
    wb = dict(zip(BIG_NAMES, _gather_weights(placed, BIG_KINDS)))

    w_mod_b = a["w_mod"][0].astype(BF16)
    c_ctx = a["c_ctx"].reshape(1, D_MODEL)
    b_loc = lax.dynamic_slice_in_dim(a["b_mod"], chip * MOD_COLS, MOD_COLS, 1)
    m_loc, s_b = _mod_fwd(c_all, c_ctx, w_mod_b, b_loc)
    mg = _all_gather8(m_loc, "gather_mod").reshape(8, MOD_ROWS, MOD_COLS)
    m_full = mg[0::2].transpose(1, 0, 2).reshape(MOD_ROWS, 6 * D_MODEL)
    mx = lax.dynamic_slice_in_dim(m_full, dev, 1, 0).reshape(6, D_MODEL)
    mc = m_full[8].reshape(6, D_MODEL)

    loss_part, grad_x, g, dmx, dmc = _local_step(a, wb, mx, mc, conv_w)
    loss = lax.psum(loss_part, ("x", "y", "c"))

    dm_pair = jnp.concatenate([dmx.reshape(1, -1), dmc.reshape(1, -1), jnp.zeros((6, 6 * D_MODEL), F32)], axis=0)
    dm_all = _all_gather8(dm_pair, "gather_dmod").reshape(8, 8, 6 * D_MODEL)
    dm16, gb_mod = _mod_bwd_sum(dm_all)
    dm_loc = lax.dynamic_slice_in_dim(dm16, chip * MOD_COLS, MOD_COLS, 1)
    gw_mod, gcc = _mod_bwd_w(dm_loc, s_b, c_ctx, w_mod_b)

    small_parts = [g[n] for n in SMALL_NAMES] + [gcc[0]]
    small_shapes = [p.shape for p in small_parts]
    sp = _pack_rows(small_parts)
    tot = _sum_slots(_all_gather8(sp, "gather_small_grads").reshape(8, sp.shape[0], ROW), "sum_small_grads")
    small = dict(zip(SMALL_NAMES + ("c_ctx",), _unpack_rows(tot, small_shapes)))
    grads = {n: small[n].reshape(a[n].shape) for n in SMALL_NAMES if n != "conv_w"}
    grads["c_ctx"] = (0.5 * small["c_ctx"]).reshape(a["c_ctx"].shape)
    grads["conv_w"] = lax.dynamic_slice_in_dim(small["conv_w"], chip * (D_FF // N_CHIPS), D_FF // N_CHIPS, 1)[None]
    grads["b_mod"] = gb_mod
    grads["w_mod"] = gw_mod[None]

    gfull = [g[n] for n in BIG_NAMES]
    sib = _rs_sibling(gfull, BIG_KINDS)
    pairs = [_pair_sum(gf, rv, k, ci, "rs_pair_" + n) for gf, rv, k, n in zip(gfull, sib, BIG_KINDS, BIG_NAMES)]
    pos = jnp.stack([ci, chip])
    halves = [_sum_chips(p, t, k, pos, "rs_sum_" + n)
              for p, t, k, n in zip(pairs, _rs_chips(pairs, BIG_KINDS), BIG_KINDS, BIG_NAMES)]
    for n, t in zip(BIG_NAMES, _rs_back(halves, BIG_KINDS)):
        grads[n] = t[None]

    delta, new_m, new_v = {}, {}, {}
    for n in BIG_NAMES + ("w_mod",):
        for dst, t in zip((delta, new_m, new_v), _adamw(a[n][0], grads[n][0], a["m_" + n][0], a["v_" + n][0], "adamw_" + n)):
            dst[n] = t[None]
    rest = [n for n in WEIGHT_NAMES if n not in BIG_NAMES and n != "w_mod"]
    shapes = [a[n].shape for n in rest]
    pr = lambda pre: _pack_rows([a[pre + n] for n in rest])
    for dst, t in zip((delta, new_m, new_v),
                      _adamw(pr(""), _pack_rows([grads[n] for n in rest]), pr("m_"), pr("v_"), "adamw_small")):
        dst.update(zip(rest, _unpack_rows(t, shapes)))

    return (loss, grad_x[None], *[grads[n] for n in WEIGHT_NAMES], *[delta[n] for n in WEIGHT_NAMES],
            *[new_m[n] for n in WEIGHT_NAMES], *[new_v[n] for n in WEIGHT_NAMES])


def kernel(x, c, ctx, c_ctx, w_mod, b_mod, norm1_w, w_in, s5_lambda_re_f, s5_lambda_im_f, s5_log_step_f, s5_lambda_re_b, s5_lambda_im_b, s5_log_step_b, s5_b_re, s5_b_im, s5_c_re, s5_c_im, s5_d, s5_w_glu, s5_b_glu, ret_log_decay_f, ret_log_decay_b, w_out, norm2_w, w_up, conv_w, conv_b, w_down, final_norm_w, loss_target, m_c_ctx, m_w_mod, m_b_mod, m_norm1_w, m_w_in, m_s5_lambda_re_f, m_s5_lambda_im_f, m_s5_log_step_f, m_s5_lambda_re_b, m_s5_lambda_im_b, m_s5_log_step_b, m_s5_b_re, m_s5_b_im, m_s5_c_re, m_s5_c_im, m_s5_d, m_s5_w_glu, m_s5_b_glu, m_ret_log_decay_f, m_ret_log_decay_b, m_w_out, m_norm2_w, m_w_up, m_conv_w, m_conv_b, m_w_down, m_final_norm_w, v_c_ctx, v_w_mod, v_b_mod, v_norm1_w, v_w_in, v_s5_lambda_re_f, v_s5_lambda_im_f, v_s5_log_step_f, v_s5_lambda_re_b, v_s5_lambda_im_b, v_s5_log_step_b, v_s5_b_re, v_s5_b_im, v_s5_c_re, v_s5_c_im, v_s5_d, v_s5_w_glu, v_s5_b_glu, v_ret_log_decay_f, v_ret_log_decay_b, v_w_out, v_norm2_w, v_w_up, v_conv_w, v_conv_b, v_w_down, v_final_norm_w):
    return _step(dict(locals()))
```

```python
import functools
import math

import jax
import jax.numpy as jnp
from jax import lax
from jax.experimental import pallas as pl
from jax.experimental.pallas import tpu as pltpu

F32 = jnp.float32
BF16 = jnp.bfloat16

D_MODEL = 1024
S5_WIDTH = 512
S5_GROUPS = 32
S5_GROUP = 16
S5_STATE = 64
RET_WIDTH = 512
RET_HEADS = 4
RET_DH = 128
RET_CHUNK = 128
GRID_W = 64
ROPE_THETA = 10000.0
D_FF = 2816
NORM_EPS = 1e-6
IN_COLS = S5_WIDTH + 4 * RET_WIDTH

S5_T = 16
S5_NB = 4
S5_BW = S5_T * 128
S5_SW = 8 * 2 * S5_STATE

ADAM_LR, ADAM_B1, ADAM_B2, ADAM_EPS, ADAM_WD, ADAM_STEP = 0.001, 0.9, 0.999, 1e-08, 0.01, 10

VMEM_LIMIT = 56 * 1024 * 1024
MESH_ID = pl.DeviceIdType.MESH


def _params(sem=None):
    return pltpu.CompilerParams(dimension_semantics=sem, vmem_limit_bytes=VMEM_LIMIT)


def _full(shape):
    n = len(shape)
    return pl.BlockSpec(shape, lambda *_: (0,) * n)


def _dot(a, b):
    return jnp.dot(a, b, preferred_element_type=F32)


def _dot_nt(a, b):
    return lax.dot_general(a, b, (((1,), (1,)), ((), ())), preferred_element_type=F32)


def _dot_tn(a, b):
    return lax.dot_general(a, b, (((0,), (0,)), ((), ())), preferred_element_type=F32)


def _dot_hi(a, b):
    return jnp.dot(a, b, preferred_element_type=F32, precision=lax.Precision.HIGHEST)


def _dot_nt_hi(a, b):
    return lax.dot_general(a, b, (((1,), (1,)), ((), ())), preferred_element_type=F32,
                           precision=lax.Precision.HIGHEST)


def _gelu(x):
    return 0.5 * x * (1.0 + jnp.tanh(0.7978845608028654 * (x + 0.044715 * (x * x * x))))


def _sigmoid(x):
    return 1.0 / (1.0 + jnp.exp(-x))


def _silu(x):
    return x * _sigmoid(x)


def _rms_mod(x, nw, sh, sc):
    r = lax.rsqrt(jnp.mean(x * x, axis=-1, keepdims=True) + NORM_EPS)
    return (x * r * nw) * (1.0 + sc) + sh


def _rms(x, nw):
    r = lax.rsqrt(jnp.mean(x * x, axis=-1, keepdims=True) + NORM_EPS)
    return x * r * nw


def _head_norm_gate(y, g):
    mu = jnp.mean(y, axis=-1, keepdims=True)
    yc = y - mu
    var = jnp.mean(yc * yc, axis=-1, keepdims=True)
    return _silu(g) * (yc * lax.rsqrt(var + NORM_EPS))


def _swap_pairs(t):
    lane = lax.broadcasted_iota(jnp.int32, t.shape, 1)
    return jnp.where(lane % 2 == 0, pltpu.roll(t, RET_DH - 1, 1), pltpu.roll(t, 1, 1))


def _rope(t, cos_t, sin_t):
    return t * cos_t + _swap_pairs(t) * sin_t


def _rope_t(dt, cos_t, sin_t):
    return dt * cos_t + _swap_pairs(dt * sin_t)


def _pick(n, prefs):
    for p in prefs:
        if n % p == 0:
            return p
    return n


def _mm_tn(a, b, *, name):
    m, k = a.shape
    n = b.shape[1]
    tm = _pick(m, (512, 256, 128))
    tn = _pick(n, (1408, 1024, 1280, 512))

    def body(a_ref, b_ref, o_ref):
        @pl.when(pl.program_id(1) == 0)
        def _():
            o_ref[...] = jnp.zeros_like(o_ref)
        o_ref[...] += _dot_tn(a_ref[...], b_ref[...])

    return pl.pallas_call(
        body, name=name, grid=(n // tn, m // tm),
        in_specs=[pl.BlockSpec((tm, k), lambda j, i: (i, 0)), pl.BlockSpec((tm, tn), lambda j, i: (i, j))],
        out_specs=pl.BlockSpec((k, tn), lambda j, i: (0, j)),
        out_shape=jax.ShapeDtypeStruct((k, n), F32),
        compiler_params=_params(("parallel", "arbitrary")),
    )(a, b)


TOK_TILE = 256


def _norm_inproj(x, ctx, n1w, mod4, w_in_b):
    l, lc = x.shape[0], ctx.shape[0]
    tm = TOK_TILE
    nct = lc // tm
    la = l + lc

    def body(x_ref, c_ref, nw_ref, mod_ref, w_ref, p_ref, h_ref):
        is_ctx = pl.program_id(0) < nct
        xt = jnp.where(is_ctx, c_ref[...], x_ref[...])
        sh = jnp.where(is_ctx, mod_ref[0:1, :], mod_ref[2:3, :])
        sc = jnp.where(is_ctx, mod_ref[1:2, :], mod_ref[3:4, :])
        hb = _rms_mod(xt, nw_ref[...], sh, sc).astype(BF16)
        h_ref[...] = hb
        p_ref[...] = _dot(hb, w_ref[...])

    return pl.pallas_call(
        body, name="norm_inproj", grid=(la // tm,),
        in_specs=[pl.BlockSpec((tm, D_MODEL), lambda i: (jnp.maximum(i - nct, 0), 0)),
                  pl.BlockSpec((tm, D_MODEL), lambda i: (jnp.minimum(i, nct - 1), 0)),
                  _full((1, D_MODEL)), _full((4, D_MODEL)), _full((D_MODEL, IN_COLS))],
        out_specs=[pl.BlockSpec((tm, IN_COLS), lambda i: (i, 0)), pl.BlockSpec((tm, D_MODEL), lambda i: (i, 0))],
        out_shape=[jax.ShapeDtypeStruct((la, IN_COLS), F32), jax.ShapeDtypeStruct((la, D_MODEL), BF16)],
        compiler_params=_params(("parallel",)),
    )(x, ctx, n1w, mod4, w_in_b)


def _iota2(shape, dim):
    return lax.broadcasted_iota(jnp.int32, shape, dim)


def _group_mask(rows, cols, row_div, col_div):
    return jnp.where(_iota2((rows, cols), 0) // row_div == _iota2((rows, cols), 1) // col_div, 1.0, 0.0).astype(F32)


def _s5_gen_dir(lre, lim, lst, b_re, b_im, c_re, c_im):
    step = jnp.exp(lst)
    mag = jnp.exp(lre * step)
    ar = mag * jnp.cos(lim * step)
    ai = mag * jnp.sin(lim * step)
    den = lre * lre + lim * lim
    xr = ar - 1.0
    cr = (xr * lre + ai * lim) / den
    ci = (ai * lre - xr * lim) / den
    rexp = _group_mask(128, 8, S5_GROUP, 1)
    are, aie = _dot_hi(rexp, ar), _dot_hi(rexp, ai)
    cre, cie = _dot_hi(rexp, cr), _dot_hi(rexp, ci)
    bbr = cre * b_re - cie * b_im
    bbi = cre * b_im + cie * b_re
    gmask = _group_mask(128, 128, S5_GROUP, S5_GROUP)
    pr, pi = jnp.ones_like(are), jnp.zeros_like(are)
    xs, ys = [], []
    for t in range(S5_T + 1):
        if t < S5_T:
            xs.append(jnp.concatenate([bbr * pr - bbi * pi, bbr * pi + bbi * pr], axis=1))
        ys.append(jnp.concatenate([c_re * pr - c_im * pi, -(c_re * pi + c_im * pr)], axis=1))
        pr, pi = pr * are - pi * aie, pr * aie + pi * are
    gs = [_dot_nt_hi(x_t, ys[0]) * gmask for x_t in xs]
    r16, i16 = ar, ai
    for _ in range(4):
        r16, i16 = r16 * r16 - i16 * i16, 2.0 * r16 * i16
    return xs, ys, gs, jnp.concatenate([r16, i16], axis=1)


def _s5_expand(z):
    return jnp.concatenate([z] * 8, axis=1) * _group_mask(128, S5_SW, S5_GROUP, 128)


def _s5_contract(z):
    zm = z * _group_mask(128, S5_SW, S5_GROUP, 128)
    acc = zm[:, 0:128]
    for k in range(1, 8):
        acc = acc + zm[:, 128 * k:128 * (k + 1)]
    return acc


def _s5_param_specs():
    blk3 = lambda r, c: pl.BlockSpec((1, 1, r, c), lambda b, *_: (0, b, 0, 0))
    dir3 = lambda r, c: pl.BlockSpec((2, 1, r, c), lambda b, *_: (0, b, 0, 0))
    return [dir3(8, S5_STATE), dir3(8, S5_STATE), dir3(8, 1), blk3(128, S5_STATE), blk3(128, S5_STATE),
            blk3(128, S5_STATE), blk3(128, S5_STATE), blk3(1, 128)]


def _s5_gen(lre, lim, lst, b_re, b_im, c_re, c_im, dvec):
    def body(lre_ref, lim_ref, lst_ref, bre_ref, bim_ref, cre_ref, cim_ref, d_ref, gg_ref, xw_ref, yw_ref, a16_ref):
        eye = _group_mask(128, 128, 1, 1)
        g0 = eye * d_ref[0, 0]
        for dr in range(2):
            xs, ys, gs, a16 = _s5_gen_dir(lre_ref[dr, 0], lim_ref[dr, 0], lst_ref[dr, 0], bre_ref[0, 0],
                                          bim_ref[0, 0], cre_ref[0, 0], cim_ref[0, 0])
            a16_ref[0, dr] = a16
            for j in range(S5_T):
                xw_ref[0, dr, j] = xs[S5_T - 1 - j if dr == 0 else j]
                yw_ref[0, dr, j] = ys[j + 1 if dr == 0 else S5_T - j]
            g0 = g0 + gs[0]
            for t in range(1, S5_T):
                gg_ref[0, (S5_T - 1) + t if dr == 0 else (S5_T - 1) - t] = gs[t]
        gg_ref[0, S5_T - 1] = g0

    blk = pl.BlockSpec((1, 2, S5_T, 128, 128), lambda b: (b, 0, 0, 0, 0))
    return pl.pallas_call(
        body, name="s5_gen", grid=(S5_NB,),
        in_specs=_s5_param_specs(),
        out_specs=[pl.BlockSpec((1, 2 * S5_T - 1, 128, 128), lambda b: (b, 0, 0, 0)), blk, blk,
                   pl.BlockSpec((1, 2, 8, 128), lambda b: (b, 0, 0, 0))],
        out_shape=[jax.ShapeDtypeStruct((S5_NB, 2 * S5_T - 1, 128, 128), F32),
                   jax.ShapeDtypeStruct((S5_NB, 2, S5_T, 128, 128), F32),
                   jax.ShapeDtypeStruct((S5_NB, 2, S5_T, 128, 128), F32),
                   jax.ShapeDtypeStruct((S5_NB, 2, 8, 128), F32)],
        compiler_params=_params(("parallel",)),
    )(lre, lim, lst, b_re, b_im, c_re, c_im, dvec)


def _s5_fill_state_mat(w_scr, src_ref, dr):
    for j in range(S5_T):
        w_scr[128 * j:128 * (j + 1), :] = _s5_expand(src_ref[0, dr, j]).astype(BF16)


def _s5_fill_toeplitz(k_scr, gg_ref):
    for j in range(S5_T):
        for i in range(S5_T):
            k_scr[128 * j:128 * (j + 1), 128 * i:128 * (i + 1)] = gg_ref[0, i - j + (S5_T - 1)].astype(BF16)


S5_GEN_SPECS = [pl.BlockSpec((1, 2 * S5_T - 1, 128, 128), lambda b: (b, 0, 0, 0)),
                pl.BlockSpec((1, 2, S5_T, 128, 128), lambda b: (b, 0, 0, 0, 0))]


def _s5_gen_bwd(lre, lim, lst, b_re, b_im, c_re, c_im, dvec, dg, dx, dy, da16):
    def body(lre_ref, lim_ref, lst_ref, bre_ref, bim_ref, cre_ref, cim_ref, d_ref, dg_ref, dx_ref, dy_ref, da16_ref,
             glre_ref, glim_ref, glst_ref, gbre_ref, gbim_ref, gcre_ref, gcim_ref, gd_ref):
        eye = _group_mask(128, 128, 1, 1)
        gd_ref[0, 0] = jnp.sum(dg_ref[0, S5_T - 1] * eye, axis=0, keepdims=True)
        gb = [None, None, None, None]
        for dr in range(2):
            args = (lre_ref[dr, 0], lim_ref[dr, 0], lst_ref[dr, 0], bre_ref[0, 0], bim_ref[0, 0],
                    cre_ref[0, 0], cim_ref[0, 0])
            _, vjp = jax.vjp(_s5_gen_dir, *args)
            dxs = [dx_ref[0, dr, S5_T - 1 - t if dr == 0 else t] for t in range(S5_T)]
            dys = [jnp.zeros((128, 128), F32)] + [dy_ref[0, dr, t - 1 if dr == 0 else S5_T - t]
                                                  for t in range(1, S5_T + 1)]
            dgs = [dg_ref[0, (S5_T - 1) + t if dr == 0 else (S5_T - 1) - t] for t in range(S5_T)]
            g = vjp((dxs, dys, dgs, da16_ref[0, dr]))
            glre_ref[dr, 0] = g[0]
            glim_ref[dr, 0] = g[1]
            glst_ref[dr, 0] = g[2]
            for q in range(4):
                gb[q] = g[3 + q] if gb[q] is None else gb[q] + g[3 + q]
        gbre_ref[0, 0] = gb[0]
        gbim_ref[0, 0] = gb[1]
        gcre_ref[0, 0] = gb[2]
        gcim_ref[0, 0] = gb[3]

    shp = lambda a: jax.ShapeDtypeStruct(a.shape, F32)
    return pl.pallas_call(
        body, name="s5_gen_bwd", grid=(S5_NB,),
        in_specs=_s5_param_specs() + [
            pl.BlockSpec((1, 2 * S5_T - 1, 128, 128), lambda b: (b, 0, 0, 0)),
            pl.BlockSpec((1, 2, S5_T, 128, 128), lambda b: (b, 0, 0, 0, 0)),
            pl.BlockSpec((1, 2, S5_T, 128, 128), lambda b: (b, 0, 0, 0, 0)),
            pl.BlockSpec((1, 2, 8, 128), lambda b: (b, 0, 0, 0))],
        out_specs=_s5_param_specs(),
        out_shape=[shp(lre), shp(lim), shp(lst), shp(b_re), shp(b_im), shp(c_re), shp(c_im), shp(dvec)],
        compiler_params=_params(("parallel",)),
    )(lre, lim, lst, b_re, b_im, c_re, c_im, dvec, dg, dx, dy, da16)


def _s5_ucat(u_ref, lo=0, hi=S5_T):
    return jnp.concatenate([u_ref[:, j, :] for j in range(lo, hi)], axis=1).astype(BF16)


def _s5_put_groups(o_ref, dr, val):
    for gi in range(8):
        o_ref[dr, :, gi, :] = val[:, 128 * gi:128 * (gi + 1)]


def _s5_get_groups(s_ref, dr, n=8):
    return jnp.concatenate([s_ref[dr, :, gi, :] for gi in range(n)], axis=1).astype(BF16)


def _s5_to_states(u3, blocks, name):
    cn = u3.shape[0]

    def body(u_ref, b_ref, o_ref, w_scr):
        u = _s5_ucat(u_ref)
        for dr in range(2):
            _s5_fill_state_mat(w_scr, b_ref, dr)
            _s5_put_groups(o_ref, dr, _dot(u, w_scr[...]))

    return pl.pallas_call(
        body, name=name, grid=(S5_NB,),
        in_specs=[pl.BlockSpec((cn, S5_T, 128), lambda b: (0, 0, b)), S5_GEN_SPECS[1]],
        out_specs=pl.BlockSpec((2, cn, 8, 128), lambda b: (0, 0, b, 0)),
        out_shape=jax.ShapeDtypeStruct((2, cn, S5_GROUPS, 128), F32),
        scratch_shapes=[pltpu.VMEM((S5_BW, S5_SW), BF16)],
        compiler_params=_params(("parallel",)),
    )(u3, blocks)


def _s5_from_states(u3, gg, st, blocks, transposed, name):
    cn = u3.shape[0]

    def body(u_ref, g_ref, s_ref, b_ref, o_ref, k_scr, w_scr):
        u = _s5_ucat(u_ref)
        _s5_fill_toeplitz(k_scr, g_ref)
        y = _dot_nt(u, k_scr[...]) if transposed else _dot(u, k_scr[...])
        for dr in range(2):
            _s5_fill_state_mat(w_scr, b_ref, dr)
            y = y + _dot_nt(_s5_get_groups(s_ref, dr), w_scr[...])
        for i in range(S5_T):
            o_ref[:, i, :] = y[:, 128 * i:128 * (i + 1)]

    return pl.pallas_call(
        body, name=name, grid=(S5_NB,),
        in_specs=[pl.BlockSpec((cn, S5_T, 128), lambda b: (0, 0, b)), S5_GEN_SPECS[0],
                  pl.BlockSpec((2, cn, 8, 128), lambda b: (0, 0, b, 0)), S5_GEN_SPECS[1]],
        out_specs=pl.BlockSpec((cn, S5_T, 128), lambda b: (0, 0, b)),
        out_shape=jax.ShapeDtypeStruct((cn, S5_T, S5_WIDTH), F32),
        scratch_shapes=[pltpu.VMEM((S5_BW, S5_BW), BF16), pltpu.VMEM((S5_BW, S5_SW), BF16)],
        compiler_params=_params(("parallel",)),
    )(u3, gg, st, blocks)


def _s5_a_forms(a):
    ra = pltpu.roll(a, S5_STATE, 1)
    low = _iota2(a.shape, 1) < S5_STATE
    return jnp.where(low, a, ra), jnp.where(low, -ra, a)


def _s5_scan(sloc, a16, ncc):
    cn = sloc.shape[1]

    def body(s_ref, a_ref, h_ref):
        forms = [_s5_a_forms(a_ref[dr]) for dr in range(2)]

        def step(s, hs):
            out = []
            for dr in range(2):
                arr, aii = forms[dr]
                h, rh = hs[dr]
                c = s if dr == 0 else jnp.where(s < ncc, ncc - 1 - s, cn - 1 - (s - ncc))
                h_ref[dr, c] = h
                sc = s_ref[dr, c]
                out.append((h * arr + rh * aii + sc, rh * arr - h * aii + pltpu.roll(sc, S5_STATE, 1)))
            return tuple(out)

        zero = jnp.zeros((S5_GROUPS, 128), F32)
        lax.fori_loop(0, cn, step, ((zero, zero), (zero, zero)), unroll=4)

    return pl.pallas_call(
        body, name="s5_scan",
        out_shape=jax.ShapeDtypeStruct(sloc.shape, F32),
        compiler_params=_params(),
    )(sloc, a16)


def _s5_scan_bwd(e, hs, a16, ncc):
    cn = e.shape[1]

    def body(e_ref, h_ref, a_ref, ds_ref, da_ref):
        forms = [_s5_a_forms(a_ref[dr]) for dr in range(2)]
        low = _iota2((S5_GROUPS, 128), 1) < S5_STATE

        def step(s, carry):
            out = []
            r = cn - 1 - s
            for dr in range(2):
                arr, aii = forms[dr]
                g, rg, da = carry[dr]
                c = r if dr == 0 else jnp.where(r < ncc, ncc - 1 - r, cn - 1 - (r - ncc))
                ds_ref[dr, c] = g
                h = h_ref[dr, c]
                rh = pltpu.roll(h, S5_STATE, 1)
                da = da + jnp.where(low, g * h + rg * rh, g * rh - rg * h)
                ec = e_ref[dr, c]
                out.append((ec + g * arr - rg * aii, pltpu.roll(ec, S5_STATE, 1) + rg * arr + g * aii, da))
            return tuple(out)

        zero = jnp.zeros((S5_GROUPS, 128), F32)
        res = lax.fori_loop(0, cn, step, ((zero, zero, zero), (zero, zero, zero)), unroll=4)
        da_ref[0] = res[0][2]
        da_ref[1] = res[1][2]

    return pl.pallas_call(
        body, name="s5_scan_bwd",
        out_shape=[jax.ShapeDtypeStruct(e.shape, F32), jax.ShapeDtypeStruct((2, S5_GROUPS, 128), F32)],
        compiler_params=_params(),
    )(e, hs, a16)


def _s5_bwd_kb(p3, dy3):
    cn = p3.shape[0]
    half = S5_T // 2

    def body(u_ref, d_ref, o_ref):
        q = pl.program_id(1)

        @pl.when(q == 0)
        def _():
            o_ref[...] = jnp.zeros_like(o_ref)

        dk = _dot_tn(_s5_ucat(u_ref), _s5_ucat(d_ref, 0, half))
        for j in range(S5_T):
            for i in range(half):
                o_ref[0, half * q + i - j + (S5_T - 1)] += dk[128 * j:128 * (j + 1), 128 * i:128 * (i + 1)]

    return pl.pallas_call(
        body, name="s5_bwd_kb", grid=(S5_NB, 2),
        in_specs=[pl.BlockSpec((cn, S5_T, 128), lambda b, q: (0, 0, b)),
                  pl.BlockSpec((cn, half, 128), lambda b, q: (0, q, b))],
        out_specs=pl.BlockSpec((1, 2 * S5_T - 1, 128, 128), lambda b, q: (b, 0, 0, 0)),
        out_shape=jax.ShapeDtypeStruct((S5_NB, 2 * S5_T - 1, 128, 128), F32),
        compiler_params=_params(("parallel", "arbitrary")),
    )(p3, dy3)


def _s5_bwd_w(u3, st, name):
    cn = u3.shape[0]

    def body(u_ref, s_ref, w_ref):
        dw = _dot_tn(_s5_ucat(u_ref), _s5_get_groups(s_ref, 0))
        for j in range(S5_T):
            w_ref[0, 0, j] = _s5_contract(dw[128 * j:128 * (j + 1), :])

    return pl.pallas_call(
        body, name=name, grid=(S5_NB, 2),
        in_specs=[pl.BlockSpec((cn, S5_T, 128), lambda b, q: (0, 0, b)),
                  pl.BlockSpec((1, cn, 8, 128), lambda b, q: (q, 0, b, 0))],
        out_specs=pl.BlockSpec((1, 1, S5_T, 128, 128), lambda b, q: (b, q, 0, 0, 0)),
        out_shape=jax.ShapeDtypeStruct((S5_NB, 2, S5_T, 128, 128), F32),
        compiler_params=_params(("parallel", "parallel")),
    )(u3, st)


def _s5_glu(y_all, w_glu_b, b_glu, nct):
    la = y_all.shape[0]
    tm = TOK_TILE
    l = la - nct * tm

    def body(y_ref, w_ref, b_ref, o_ref):
        yg = _gelu(y_ref[...])
        z = _dot(yg.astype(BF16), w_ref[...]) + b_ref[...]
        o_ref[...] = (yg * _sigmoid(z)).astype(BF16)

    return pl.pallas_call(
        body, name="s5_glu", grid=(l // tm,),
        in_specs=[pl.BlockSpec((tm, S5_WIDTH), lambda i: (i + nct, 0)),
                  _full((S5_WIDTH, S5_WIDTH)), _full((1, S5_WIDTH))],
        out_specs=pl.BlockSpec((tm, S5_WIDTH), lambda i: (i, 0)),
        out_shape=jax.ShapeDtypeStruct((l, S5_WIDTH), BF16),
        compiler_params=_params(("parallel",)),
    )(y_all, w_glu_b, b_glu)


def _s5_glu_bwd(y_all, dmix, w_glu_b, b_glu, nct):
    la = y_all.shape[0]
    tm = TOK_TILE

    def body(y_ref, d_ref, w_ref, b_ref, dy_ref, gw_ref, gb_ref):
        i = pl.program_id(0)

        @pl.when(i == 0)
        def _():
            gw_ref[...] = jnp.zeros_like(gw_ref)
            gb_ref[...] = jnp.zeros_like(gb_ref)

        @pl.when(i < nct)
        def _():
            dy_ref[...] = jnp.zeros_like(dy_ref)

        @pl.when(i >= nct)
        def _():
            y = y_ref[...]
            yg, gelu_vjp = jax.vjp(_gelu, y)
            ygb = yg.astype(BF16)
            sg = _sigmoid(_dot(ygb, w_ref[...]) + b_ref[...])
            ds = d_ref[...]
            dz = ds * yg * sg * (1.0 - sg)
            dzb = dz.astype(BF16)
            dyg = ds * sg + _dot_nt(dzb, w_ref[...])
            dy_ref[...] = gelu_vjp(dyg)[0]
            gw_ref[...] += _dot_tn(ygb, dzb)
            gb_ref[...] += jnp.sum(dz, axis=0, keepdims=True)

    return pl.pallas_call(
        body, name="s5_glu_bwd", grid=(la // tm,),
        in_specs=[pl.BlockSpec((tm, S5_WIDTH), lambda i: (i, 0)),
                  pl.BlockSpec((tm, S5_WIDTH), lambda i: (jnp.maximum(i - nct, 0), 0)),
                  _full((S5_WIDTH, S5_WIDTH)), _full((1, S5_WIDTH))],
        out_specs=[pl.BlockSpec((tm, S5_WIDTH), lambda i: (i, 0)), _full((S5_WIDTH, S5_WIDTH)),
                   _full((1, S5_WIDTH))],
        out_shape=[jax.ShapeDtypeStruct((la, S5_WIDTH), F32), jax.ShapeDtypeStruct((S5_WIDTH, S5_WIDTH), F32),
                   jax.ShapeDtypeStruct((1, S5_WIDTH), F32)],
        compiler_params=_params(("arbitrary",)),
    )(y_all, dmix, w_glu_b, b_glu)


K_SCALE = RET_DH ** -0.5
Q_COL, K_COL, V_COL, G_COL = 4, 8, 12, 16


def _ret_chunk_of(step, ncc, nch, rev):
    if not rev:
        return step
    return jnp.where(step < ncc, ncc - 1 - step, nch - 1 - (step - ncc))


def _ret_decay(ld, rev):
    c = _iota2((RET_CHUNK, RET_CHUNK), 0).astype(F32)
    m = _iota2((RET_CHUNK, RET_CHUNK), 1).astype(F32)
    diff = (m - c) if rev else (c - m)
    keep = (diff > 0) if rev else (diff >= 0)
    expo = jnp.maximum(diff, 0.0)
    dm = jnp.where(keep, jnp.exp(ld * expo), 0.0)
    xi_e = (RET_CHUNK - c) if rev else (c + 1.0)
    zeta_e = c if rev else (RET_CHUNK - 1.0 - c)
    return dm, expo, jnp.exp(ld * xi_e), xi_e, jnp.exp(ld * zeta_e), zeta_e


RET_TABLES = 7


def _ret_tables(ld2):
    def body(ld_ref, t_ref):
        dr, h = pl.program_id(0), pl.program_id(1)
        ldh = ld_ref[dr, h]
        for rev in (False, True):
            @pl.when(dr == int(rev))
            def _(rev=rev):
                dm, expo, xi, xi_e, zeta, zeta_e = _ret_decay(ldh, rev)
                t_ref[0, 0, 0] = dm
                t_ref[0, 0, 1] = dm * expo
                t_ref[0, 0, 2] = xi
                t_ref[0, 0, 3] = xi * xi_e
                t_ref[0, 0, 4] = zeta
                t_ref[0, 0, 5] = zeta * zeta_e
                t_ref[0, 0, 6] = jnp.zeros_like(dm) + jnp.exp(ldh * RET_CHUNK)

    return pl.pallas_call(
        body, name="ret_tables", grid=(2, RET_HEADS),
        in_specs=[pl.BlockSpec(memory_space=pltpu.SMEM)],
        out_specs=pl.BlockSpec((1, 1, RET_TABLES, RET_CHUNK, RET_CHUNK), lambda d, h: (d, h, 0, 0, 0)),
        out_shape=jax.ShapeDtypeStruct((2, RET_HEADS, RET_TABLES, RET_CHUNK, RET_CHUNK), F32),
        compiler_params=_params(("parallel", "parallel")),
    )(ld2)


def _ret_specs(nch, ncc, rev, step_of):
    chunk = lambda n: _ret_chunk_of(step_of(n), ncc, nch, rev)
    cols = [pl.BlockSpec((RET_CHUNK, RET_WIDTH), functools.partial(lambda n, cb: (chunk(n), cb), cb=cb))
            for cb in (1, 2, 3)]
    tab = pl.BlockSpec((RET_CHUNK, RET_DH), lambda n: (chunk(n), 0))
    return cols + [tab, tab], pl.BlockSpec((RET_CHUNK, RET_WIDTH), lambda n: (chunk(n), 0))


def _ret_scan(p_all, cos_t, sin_t, tabs, ncc):
    la = p_all.shape[0]
    nch = la // RET_CHUNK

    def body(t_ref, qf, kf, vf, cf, sf, qb, kb, vb, cb, sb, of_ref, ob_ref, ssf_ref, ssb_ref, s_scr):
        @pl.when(pl.program_id(0) == 0)
        def _():
            s_scr[...] = jnp.zeros_like(s_scr)

        for dr, (q_ref, k_ref, v_ref, c_ref, n_ref, o_ref, ss_ref) in enumerate(
                ((qf, kf, vf, cf, sf, of_ref, ssf_ref), (qb, kb, vb, cb, sb, ob_ref, ssb_ref))):
            cs, sn = c_ref[...], n_ref[...]
            for h in range(RET_HEADS):
                sl = slice(RET_DH * h, RET_DH * (h + 1))
                dm, xi, zeta = t_ref[dr, h, 0], t_ref[dr, h, 2], t_ref[dr, h, 4]
                q = _rope(q_ref[:, sl], cs, sn)
                k = _rope(k_ref[:, sl] * K_SCALE, cs, sn)
                vh = v_ref[:, sl].astype(BF16)
                s = s_scr[dr, h]
                ss_ref[0, h] = s
                sc = (_dot_nt(q.astype(BF16), k.astype(BF16)) * dm).astype(BF16)
                o_ref[:, sl] = _dot(sc, vh) + _dot((q * xi).astype(BF16), s.astype(BF16))
                s_scr[dr, h] = t_ref[dr, h, 6] * s + _dot_tn((k * zeta).astype(BF16), vh)

    in_f, out_f = _ret_specs(nch, ncc, False, lambda n: n)
    in_b, out_b = _ret_specs(nch, ncc, True, lambda n: n)
    ss_spec = pl.BlockSpec((1, RET_HEADS, RET_DH, RET_DH), lambda n: (n, 0, 0, 0))
    o_shape = jax.ShapeDtypeStruct((la, RET_WIDTH), F32)
    ss_shape = jax.ShapeDtypeStruct((nch, RET_HEADS, RET_DH, RET_DH), F32)
    return pl.pallas_call(
        body, name="ret_scan", grid=(nch,),
        in_specs=[_full(tabs.shape)] + in_f + in_b,
        out_specs=[out_f, out_b, ss_spec, ss_spec],
        out_shape=[o_shape, o_shape, ss_shape, ss_shape],
        scratch_shapes=[pltpu.VMEM((2, RET_HEADS, RET_DH, RET_DH), F32)],
        compiler_params=_params(("arbitrary",)),
    )(tabs, p_all, p_all, p_all, cos_t, sin_t, p_all, p_all, p_all, cos_t, sin_t)


def _ret_scan_bwd(p_all, cos_t, sin_t, tabs, ssf, ssb, dy_all, ncc):
    la = p_all.shape[0]
    nch = la // RET_CHUNK

    def body(t_ref, qf, kf, vf, cf, sf, dof, ssf_ref, qb, kb, vb, cb, sb, dob_, ssb_ref,
             dqf, dkf, dvf, dqb, dkb, dvb, dld_ref, ds_scr):
        @pl.when(pl.program_id(0) == 0)
        def _():
            ds_scr[...] = jnp.zeros_like(ds_scr)
            dld_ref[...] = jnp.zeros_like(dld_ref)

        for dr, (q_ref, k_ref, v_ref, c_ref, n_ref, do_ref, ss_ref, dq_ref, dk_ref, dv_ref) in enumerate(
                ((qf, kf, vf, cf, sf, dof, ssf_ref, dqf, dkf, dvf), (qb, kb, vb, cb, sb, dob_, ssb_ref, dqb, dkb, dvb))):
            cs, sn = c_ref[...], n_ref[...]
            for h in range(RET_HEADS):
                sl = slice(RET_DH * h, RET_DH * (h + 1))
                dm, dm_d, xi, xi_d, zeta, zeta_d, gc = [t_ref[dr, h, t] for t in range(RET_TABLES)]
                q = _rope(q_ref[:, sl], cs, sn)
                k = _rope(k_ref[:, sl] * K_SCALE, cs, sn)
                q16, k16, v16 = q.astype(BF16), k.astype(BF16), v_ref[:, sl].astype(BF16)
                s = ss_ref[0, h]
                s16 = s.astype(BF16)
                ds_in = ds_scr[dr, h]
                ds16 = ds_in.astype(BF16)
                do16 = do_ref[:, sl].astype(BF16)
                qk = _dot_nt(q16, k16)
                dsv = _dot_nt(do16, v16)
                dsc = (dsv * dm).astype(BF16)
                sc16 = (qk * dm).astype(BF16)
                dos = _dot_nt(do16, s16)
                vds = _dot_nt(v16, ds16)
                dq_ref[:, sl] = _dot(dsc, k16) + dos * xi
                dk_ref[:, sl] = _dot_tn(dsc, q16) + vds * zeta
                dv_ref[:, sl] = _dot_tn(sc16, do16) + _dot((k * zeta).astype(BF16), ds16)
                ds_scr[dr, h] = _dot_tn((q * xi).astype(BF16), do16) + gc * ds_in
                dld = jnp.sum(dsv * qk * dm_d + q * dos * xi_d + k * vds * zeta_d + RET_CHUNK * gc * s * ds_in)
                dld_ref[dr, h] += dld

    back = lambda n: nch - 1 - n
    in_f, out_f = _ret_specs(nch, ncc, False, back)
    in_b, out_b = _ret_specs(nch, ncc, True, back)
    ss_spec = pl.BlockSpec((1, RET_HEADS, RET_DH, RET_DH), lambda n: (nch - 1 - n, 0, 0, 0))
    shp = jax.ShapeDtypeStruct((la, RET_WIDTH), F32)
    return pl.pallas_call(
        body, name="ret_scan_bwd", grid=(nch,),
        in_specs=[_full(tabs.shape)] + in_f + [out_f, ss_spec] + in_b + [out_b, ss_spec],
        out_specs=[out_f, out_f, out_f, out_b, out_b, out_b, _full((2, RET_HEADS, 8, 128))],
        out_shape=[shp] * 6 + [jax.ShapeDtypeStruct((2, RET_HEADS, 8, 128), F32)],
        scratch_shapes=[pltpu.VMEM((2, RET_HEADS, RET_DH, RET_DH), F32)],
        compiler_params=_params(("arbitrary",)),
    )(tabs, p_all, p_all, p_all, cos_t, sin_t, dy_all, ssf, p_all, p_all, p_all, cos_t, sin_t, dy_all, ssb)


def _ret_gate(of, ob, p_all, nct):
    la = of.shape[0]
    tm = TOK_TILE
    l = la - nct * tm

    def body(of_ref, ob_ref, g_ref, r_ref, y_ref):
        y = of_ref[...] + ob_ref[...]
        y_ref[...] = y
        for h in range(RET_HEADS):
            sl = slice(RET_DH * h, RET_DH * (h + 1))
            r_ref[:, sl] = _head_norm_gate(y[:, sl], g_ref[:, sl]).astype(BF16)

    row = pl.BlockSpec((tm, RET_WIDTH), lambda i: (i + nct, 0))
    out = pl.BlockSpec((tm, RET_WIDTH), lambda i: (i, 0))
    return pl.pallas_call(
        body, name="ret_gate", grid=(l // tm,),
        in_specs=[row, row, pl.BlockSpec((tm, RET_WIDTH), lambda i: (i + nct, G_COL // 4))],
        out_specs=[out, out],
        out_shape=[jax.ShapeDtypeStruct((l, RET_WIDTH), BF16), jax.ShapeDtypeStruct((l, RET_WIDTH), F32)],
        compiler_params=_params(("parallel",)),
    )(of, ob, p_all)


def _ret_gate_bwd(y_ret, p_all, dmix, nct):
    la = p_all.shape[0]
    tm = TOK_TILE

    def body(y_ref, g_ref, d_ref, dy_ref, dg_ref):
        i = pl.program_id(0)

        @pl.when(i < nct)
        def _():
            dy_ref[...] = jnp.zeros_like(dy_ref)
            dg_ref[...] = jnp.zeros_like(dg_ref)

        @pl.when(i >= nct)
        def _():
            for h in range(RET_HEADS):
                sl = slice(RET_DH * h, RET_DH * (h + 1))
                _, vjp = jax.vjp(_head_norm_gate, y_ref[:, sl], g_ref[:, sl])
                dy, dg = vjp(d_ref[:, sl])
                dy_ref[:, sl] = dy
                dg_ref[:, sl] = dg

    xrow = lambda cb: pl.BlockSpec((tm, RET_WIDTH), lambda i: (jnp.maximum(i - nct, 0), cb))
    out = pl.BlockSpec((tm, RET_WIDTH), lambda i: (i, 0))
    shp = jax.ShapeDtypeStruct((la, RET_WIDTH), F32)
    return pl.pallas_call(
        body, name="ret_gate_bwd", grid=(la // tm,),
        in_specs=[xrow(0), pl.BlockSpec((tm, RET_WIDTH), lambda i: (i, G_COL // 4)), xrow(1)],
        out_specs=[out, out], out_shape=[shp, shp],
        compiler_params=_params(("parallel",)),
    )(y_ret, p_all, dmix)


def _in_bwd(dqf, dkf, dvf, dqb, dkb, dvb, du, dg, cos_t, sin_t, w_in_b, x, ctx, n1w, mod4, dx1):
    l, lc = x.shape[0], ctx.shape[0]
    la = l + lc
    tm = TOK_TILE
    nct = lc // tm

    def body(dqf_ref, dkf_ref, dvf_ref, dqb_ref, dkb_ref, dvb_ref, du_ref, dg_ref, cos_ref, sin_ref,
             w_ref, x_ref, c_ref, nw_ref, mod_ref, dx1_ref, dp_ref, gx_ref, acc_ref):
        i = pl.program_id(0)
        is_ctx = i < nct

        @pl.when(i == 0)
        def _():
            acc_ref[...] = jnp.zeros_like(acc_ref)

        cs, sn = cos_ref[...], sin_ref[...]
        dp_ref[:, 0:S5_WIDTH] = du_ref[...].astype(BF16)
        for h in range(RET_HEADS):
            sl = slice(RET_DH * h, RET_DH * (h + 1))
            dq = _rope_t(dqf_ref[:, sl] + dqb_ref[:, sl], cs, sn)
            dk = _rope_t(dkf_ref[:, sl] + dkb_ref[:, sl], cs, sn) * K_SCALE
            dp_ref[:, 128 * (Q_COL + h):128 * (Q_COL + h + 1)] = dq.astype(BF16)
            dp_ref[:, 128 * (K_COL + h):128 * (K_COL + h + 1)] = dk.astype(BF16)
        dp_ref[:, 128 * V_COL:128 * G_COL] = (dvf_ref[...] + dvb_ref[...]).astype(BF16)
        dp_ref[:, 128 * G_COL:IN_COLS] = dg_ref[...].astype(BF16)

        dh1 = _dot_nt(dp_ref[...], w_ref[...])
        xt = jnp.where(is_ctx, c_ref[...], x_ref[...])
        sh = jnp.where(is_ctx, mod_ref[0:1, :], mod_ref[2:3, :])
        sc = jnp.where(is_ctx, mod_ref[1:2, :], mod_ref[3:4, :])
        _, vjp = jax.vjp(_rms_mod, xt, nw_ref[...], sh, sc)
        dx, dnw, dsh, dsc = vjp(dh1)
        gx_ref[...] = dx + dx1_ref[...]
        cf = jnp.where(is_ctx, 1.0, 0.0)
        acc_ref[0:1, :] += dnw
        acc_ref[1:2, :] += cf * dsh
        acc_ref[2:3, :] += cf * dsc
        acc_ref[3:4, :] += (1.0 - cf) * dsh
        acc_ref[4:5, :] += (1.0 - cf) * dsc

    row = pl.BlockSpec((tm, RET_WIDTH), lambda i: (i, 0))
    tab = pl.BlockSpec((tm, RET_DH), lambda i: (i, 0))
    xrow = pl.BlockSpec((tm, D_MODEL), lambda i: (jnp.maximum(i - nct, 0), 0))
    return pl.pallas_call(
        body, name="in_bwd", grid=(la // tm,),
        in_specs=[row] * 8 + [tab, tab, _full((D_MODEL, IN_COLS)), xrow,
                              pl.BlockSpec((tm, D_MODEL), lambda i: (jnp.minimum(i, nct - 1), 0)),
                              _full((1, D_MODEL)), _full((4, D_MODEL)), xrow],
        out_specs=[pl.BlockSpec((tm, IN_COLS), lambda i: (i, 0)), xrow, _full((8, D_MODEL))],
        out_shape=[jax.ShapeDtypeStruct((la, IN_COLS), BF16), jax.ShapeDtypeStruct((l, D_MODEL), F32),
                   jax.ShapeDtypeStruct((8, D_MODEL), F32)],
        compiler_params=_params(("arbitrary",)),
    )(dqf, dkf, dvf, dqb, dkb, dvb, du, dg, cos_t, sin_t, w_in_b, x, ctx, n1w, mod4, dx1)


def _outproj_up(x, s5x, retx, w_out_b, mod3, n2w, w_up_b):
    l = x.shape[0]
    tm = TOK_TILE

    def body(x_ref, s_ref, r_ref, wo_ref, mod_ref, nw_ref, wu_ref, x1_ref, mix_ref, h2_ref, up_ref):
        mix = _dot(s_ref[...], wo_ref[0:S5_WIDTH, :]) + _dot(r_ref[...], wo_ref[S5_WIDTH:D_MODEL, :])
        mix_ref[...] = mix
        x1 = x_ref[...] + mod_ref[0:1, :] * mix
        x1_ref[...] = x1
        h2 = _rms_mod(x1, nw_ref[...], mod_ref[1:2, :], mod_ref[2:3, :]).astype(BF16)
        h2_ref[...] = h2
        up_ref[...] = _dot(h2, wu_ref[...])

    row = lambda w: pl.BlockSpec((tm, w), lambda i: (i, 0))
    return pl.pallas_call(
        body, name="outproj_up", grid=(l // tm,),
        in_specs=[row(D_MODEL), row(S5_WIDTH), row(RET_WIDTH), _full((D_MODEL, D_MODEL)), _full((3, D_MODEL)),
                  _full((1, D_MODEL)), _full((D_MODEL, 2 * D_FF))],
        out_specs=[row(D_MODEL), row(D_MODEL), row(D_MODEL), row(2 * D_FF)],
        out_shape=[jax.ShapeDtypeStruct((l, D_MODEL), F32), jax.ShapeDtypeStruct((l, D_MODEL), F32),
                   jax.ShapeDtypeStruct((l, D_MODEL), BF16), jax.ShapeDtypeStruct((l, 2 * D_FF), F32)],
        compiler_params=_params(("parallel",)),
    )(x, s5x, retx, w_out_b, mod3, n2w, w_up_b)


HALO = 8


def _conv_taps(g, prev_row, next_row):
    t = g.shape[0]
    r = _iota2(g.shape, 0)
    gprev = jnp.where(r == 0, prev_row, pltpu.roll(g, 1, 0))
    gnext = jnp.where(r == t - 1, next_row, pltpu.roll(g, t - 1, 0))
    return gprev, gnext


def _ffn_loss(up, x1, conv_w, conv_b, w_down_b, gate, fnw, tgt):
    l = x1.shape[0]
    tm = TOK_TILE
    nt = l // tm
    hb = tm // HALO

    def body(up_a, up_g, hp_ref, hn_ref, x1_ref, cw_ref, cb_ref, wd_ref, gate_ref, fn_ref, tgt_ref,
             act_ref, dx2_ref, ddn_ref, dact_ref, acc_ref):
        i = pl.program_id(0)

        @pl.when(i == 0)
        def _():
            acc_ref[...] = jnp.zeros_like(acc_ref)

        g = up_g[...]
        prev_row = jnp.where(i == 0, 0.0, hp_ref[HALO - 1:HALO, :])
        next_row = jnp.where(i == nt - 1, 0.0, hn_ref[0:1, :])
        gprev, gnext = _conv_taps(g, prev_row, next_row)
        gc = cb_ref[...] + gprev * cw_ref[0:1, :] + g * cw_ref[1:2, :] + gnext * cw_ref[2:3, :]
        act = (_gelu(gc) * up_a[...]).astype(BF16)
        act_ref[...] = act
        dn = _dot(act, wd_ref[...])
        x2 = x1_ref[...] + gate_ref[...] * dn
        y, vjp = jax.vjp(_rms, x2, fn_ref[...])
        err = y - tgt_ref[...]
        dx2, dfn = vjp(err * (1.0 / D_MODEL))
        dx2_ref[...] = dx2
        ddn = (dx2 * gate_ref[...]).astype(BF16)
        ddn_ref[...] = ddn
        dact_ref[...] = _dot_nt(ddn, wd_ref[...])
        acc_ref[0:1, :] += dfn
        acc_ref[1:2, :] += jnp.sum(dx2 * dn, axis=0, keepdims=True)
        acc_ref[2:3, :] += (0.5 / D_MODEL) * jnp.sum(err * err)

    row = lambda w: pl.BlockSpec((tm, w), lambda i: (i, 0))
    last = l // HALO - 1
    return pl.pallas_call(
        body, name="ffn_loss", grid=(nt,),
        in_specs=[pl.BlockSpec((tm, D_FF), lambda i: (i, 0)), pl.BlockSpec((tm, D_FF), lambda i: (i, 1)),
                  pl.BlockSpec((HALO, D_FF), lambda i: (jnp.maximum(i * hb - 1, 0), 1)),
                  pl.BlockSpec((HALO, D_FF), lambda i: (jnp.minimum((i + 1) * hb, last), 1)),
                  row(D_MODEL), _full((3, D_FF)), _full((1, D_FF)), _full((D_FF, D_MODEL)),
                  _full((1, D_MODEL)), _full((1, D_MODEL)), row(D_MODEL)],
        out_specs=[row(D_FF), row(D_MODEL), row(D_MODEL), row(D_FF), _full((8, D_MODEL))],
        out_shape=[jax.ShapeDtypeStruct((l, D_FF), BF16), jax.ShapeDtypeStruct((l, D_MODEL), F32),
                   jax.ShapeDtypeStruct((l, D_MODEL), BF16), jax.ShapeDtypeStruct((l, D_FF), F32),
                   jax.ShapeDtypeStruct((8, D_MODEL), F32)],
        compiler_params=_params(("arbitrary",)),
    )(up, up, up, up, x1, conv_w, conv_b, w_down_b, gate, fnw, tgt)


def _convglu_bwd(up, dact, conv_w, conv_b):
    l = up.shape[0]
    tm = 128
    nt = l // tm
    hb = tm // HALO
    te = tm + 2 * HALO

    def body(a_ref, ap_ref, an_ref, g_ref, gp_ref, gn_ref, d_ref, dp_ref, dn_ref, cw_ref, cb_ref,
             dup_ref, acc_ref):
        i = pl.program_id(0)

        @pl.when(i == 0)
        def _():
            acc_ref[...] = jnp.zeros_like(acc_ref)

        row = _iota2((te, D_FF), 0) + (i * tm - HALO)
        valid = (row >= 0) & (row < l)

        def ext(p, c, n):
            return jnp.where(valid, jnp.concatenate([p[...], c[...], n[...]], axis=0), 0.0)

        ae, ge, de = ext(ap_ref, a_ref, an_ref), ext(gp_ref, g_ref, gn_ref), ext(dp_ref, d_ref, dn_ref)
        gprev = pltpu.roll(ge, 1, 0)
        gnext = pltpu.roll(ge, te - 1, 0)
        w0, w1, w2 = cw_ref[0:1, :], cw_ref[1:2, :], cw_ref[2:3, :]
        gce = cb_ref[...] + gprev * w0 + ge * w1 + gnext * w2
        _, vjp = jax.vjp(lambda a, gc: _gelu(gc) * a, ae, gce)
        dae, dgce = vjp(de)
        dge = dgce * w1 + pltpu.roll(dgce, te - 1, 0) * w0 + pltpu.roll(dgce, 1, 0) * w2
        mid = slice(HALO, HALO + tm)
        dup_ref[:, 0:D_FF] = dae[mid].astype(BF16)
        dup_ref[:, D_FF:2 * D_FF] = dge[mid].astype(BF16)
        dgc = dgce[mid]
        acc_ref[0:1, :] += jnp.sum(dgc * gprev[mid], axis=0, keepdims=True)
        acc_ref[1:2, :] += jnp.sum(dgc * ge[mid], axis=0, keepdims=True)
        acc_ref[2:3, :] += jnp.sum(dgc * gnext[mid], axis=0, keepdims=True)
        acc_ref[3:4, :] += jnp.sum(dgc, axis=0, keepdims=True)

    last = l // HALO - 1

    def trio(cb):
        return [pl.BlockSpec((tm, D_FF), lambda i: (i, cb)),
                pl.BlockSpec((HALO, D_FF), lambda i: (jnp.maximum(i * hb - 1, 0), cb)),
                pl.BlockSpec((HALO, D_FF), lambda i: (jnp.minimum((i + 1) * hb, last), cb))]

    return pl.pallas_call(
        body, name="convglu_bwd", grid=(nt,),
        in_specs=trio(0) + trio(1) + trio(0) + [_full((3, D_FF)), _full((1, D_FF))],
        out_specs=[pl.BlockSpec((tm, 2 * D_FF), lambda i: (i, 0)), _full((8, D_FF))],
        out_shape=[jax.ShapeDtypeStruct((l, 2 * D_FF), BF16), jax.ShapeDtypeStruct((8, D_FF), F32)],
        compiler_params=_params(("arbitrary",)),
    )(up, up, up, up, up, up, dact, dact, dact, conv_w, conv_b)


def _up_bwd(dup, w_up_b, w_out_b, x1, dx2, mix, mod3, n2w):
    l = x1.shape[0]
    tm = TOK_TILE

    def body(dup_ref, wu_ref, wo_ref, x1_ref, dx2_ref, mix_ref, mod_ref, nw_ref, dx1_ref, dmixb_ref, dmix_ref, acc_ref):
        @pl.when(pl.program_id(0) == 0)
        def _():
            acc_ref[...] = jnp.zeros_like(acc_ref)

        dh2 = _dot_nt(dup_ref[...], wu_ref[...])
        _, vjp = jax.vjp(_rms_mod, x1_ref[...], nw_ref[...], mod_ref[1:2, :], mod_ref[2:3, :])
        dx, dnw, dsh, dsc = vjp(dh2)
        dx1 = dx + dx2_ref[...]
        dx1_ref[...] = dx1
        dmixb = (dx1 * mod_ref[0:1, :]).astype(BF16)
        dmixb_ref[...] = dmixb
        dmix_ref[...] = _dot_nt(dmixb, wo_ref[...])
        acc_ref[0:1, :] += dnw
        acc_ref[1:2, :] += jnp.sum(dx1 * mix_ref[...], axis=0, keepdims=True)
        acc_ref[2:3, :] += dsh
        acc_ref[3:4, :] += dsc

    row = pl.BlockSpec((tm, D_MODEL), lambda i: (i, 0))
    return pl.pallas_call(
        body, name="up_bwd", grid=(l // tm,),
        in_specs=[pl.BlockSpec((tm, 2 * D_FF), lambda i: (i, 0)), _full((D_MODEL, 2 * D_FF)),
                  _full((D_MODEL, D_MODEL)), row, row, row, _full((3, D_MODEL)), _full((1, D_MODEL))],
        out_specs=[row, row, row, _full((8, D_MODEL))],
        out_shape=[jax.ShapeDtypeStruct((l, D_MODEL), F32), jax.ShapeDtypeStruct((l, D_MODEL), BF16),
                   jax.ShapeDtypeStruct((l, D_MODEL), F32), jax.ShapeDtypeStruct((8, D_MODEL), F32)],
        compiler_params=_params(("arbitrary",)),
    )(dup, w_up_b, w_out_b, x1, dx2, mix, mod3, n2w)


MOD_ROWS = 16
MOD_COLS = 6 * D_MODEL // 4


def _mod_fwd(c_all, c_ctx, w_mod_b, b_loc):
    def body(c_ref, cc_ref, w_ref, b_ref, m_ref, s_ref):
        cond = jnp.concatenate([c_ref[...], jnp.broadcast_to(cc_ref[...], (8, D_MODEL))], axis=0)
        s = _silu(cond).astype(BF16)
        s_ref[...] = s
        m_ref[...] = _dot(s, w_ref[...]) + b_ref[...]

    return pl.pallas_call(
        body, name="mod_fwd",
        out_shape=[jax.ShapeDtypeStruct((MOD_ROWS, MOD_COLS), F32), jax.ShapeDtypeStruct((MOD_ROWS, D_MODEL), BF16)],
        compiler_params=_params(),
    )(c_all, c_ctx, w_mod_b, b_loc)


def _mod_bwd_sum(dm_all):
    def body(d_ref, dm_ref, gb_ref):
        rows = [d_ref[k, 0:1, :] for k in range(8)]
        ctx_sum = d_ref[0, 1:2, :]
        for k in range(1, 8):
            ctx_sum = ctx_sum + d_ref[k, 1:2, :]
        gb = ctx_sum
        for k in range(8):
            gb = gb + rows[k]
        gb_ref[...] = gb
        dm_ref[...] = jnp.concatenate(rows + [ctx_sum] + [jnp.zeros((7, 6 * D_MODEL), F32)], axis=0)

    return pl.pallas_call(
        body, name="mod_bwd_sum",
        out_shape=[jax.ShapeDtypeStruct((MOD_ROWS, 6 * D_MODEL), F32), jax.ShapeDtypeStruct((1, 6 * D_MODEL), F32)],
        compiler_params=_params(),
    )(dm_all)


def _mod_bwd_w(dm_loc, s_b, c_ctx, w_mod_b):
    def body(d_ref, s_ref, cc_ref, w_ref, gw_ref, gc_ref):
        db = d_ref[...].astype(BF16)
        gw_ref[...] = _dot_tn(s_ref[...], db)
        ds = _dot_nt(db, w_ref[...])
        _, vjp = jax.vjp(_silu, cc_ref[...])
        gc_ref[...] = jnp.broadcast_to(vjp(ds[8:9, :])[0], (8, D_MODEL))

    return pl.pallas_call(
        body, name="mod_bwd_w",
        out_shape=[jax.ShapeDtypeStruct((D_MODEL, MOD_COLS), F32), jax.ShapeDtypeStruct((8, D_MODEL), F32)],
        compiler_params=_params(),
    )(dm_loc, s_b, c_ctx, w_mod_b)


def _adamw(w, g, m, v, name):
    r, c = w.shape
    tr = _pick(r, (256, 128, 64, 32, 16, 8))
    bc1 = 1.0 - ADAM_B1 ** ADAM_STEP
    bc2 = 1.0 - ADAM_B2 ** ADAM_STEP

    def body(w_ref, g_ref, m_ref, v_ref, d_ref, nm_ref, nv_ref):
        gg = g_ref[...]
        nm = ADAM_B1 * m_ref[...] + (1.0 - ADAM_B1) * gg
        nv = ADAM_B2 * v_ref[...] + (1.0 - ADAM_B2) * (gg * gg)
        nm_ref[...] = nm
        nv_ref[...] = nv
        d_ref[...] = -ADAM_LR * ((nm / bc1) / (jnp.sqrt(nv / bc2) + ADAM_EPS) + ADAM_WD * w_ref[...])

    blk = pl.BlockSpec((tr, c), lambda i: (i, 0))
    shp = jax.ShapeDtypeStruct((r, c), F32)
    return pl.pallas_call(
        body, name=name, grid=(r // tr,), in_specs=[blk] * 4, out_specs=[blk] * 3, out_shape=[shp] * 3,
        compiler_params=_params(("parallel",)),
    )(w, g, m, v)


def _sum_slots(a, name):
    n, r, c = a.shape
    tr = _pick(r, (376, 256, 208, 128, 64, 32, 16, 8))

    def body(a_ref, o_ref):
        acc = a_ref[0].astype(F32)
        for k in range(1, n):
            acc = acc + a_ref[k].astype(F32)
        o_ref[...] = acc

    return pl.pallas_call(
        body, name=name, grid=(r // tr,),
        in_specs=[pl.BlockSpec((n, tr, c), lambda i: (0, i, 0))],
        out_specs=pl.BlockSpec((tr, c), lambda i: (i, 0)),
        out_shape=jax.ShapeDtypeStruct((r, c), F32),
        compiler_params=_params(("parallel",)),
    )(a)


def _mesh_pos():
    return lax.axis_index("x"), lax.axis_index("y"), lax.axis_index("c")


def _all_gather8(v, name):
    m_per, n = v.shape

    def body(x_ref, out_ref, send_sems, recv_sems, local_sem):
        x, y, c = _mesh_pos()
        me, sibling = (x, y, c), (x, y, 1 - c)
        chips = [(1 - x, y), (x, 1 - y), (1 - x, 1 - y)]

        def rows(px, py, pc):
            return out_ref.at[pl.ds((4 * px + 2 * py + pc) * m_per, m_per), :]

        def copy(k, block, to, src=None):
            return pltpu.make_async_remote_copy(
                src_ref=rows(*block) if src is None else src, dst_ref=rows(*block),
                send_sem=send_sems.at[k], recv_sem=recv_sems.at[k], device_id=to, device_id_type=MESH_ID)

        mine = pltpu.make_async_copy(x_ref, rows(*me), local_sem)
        mine.start()
        first = [copy(0, me, sibling, src=x_ref)]
        first += [copy(1 + j, me, (*chip, c), src=x_ref) for j, chip in enumerate(chips)]
        for cp in first:
            cp.start()
        passed = [copy(4 + j, (*chip, c), sibling) for j, chip in enumerate(chips)]
        for j, chip in enumerate(chips):
            copy(1 + j, (*chip, c), me).wait_recv()
            passed[j].start()
        copy(0, sibling, me).wait_recv()
        for j, chip in enumerate(chips):
            copy(4 + j, (*chip, 1 - c), me).wait_recv()
        for cp in first + passed:
            cp.wait_send()
        mine.wait()

    return pl.pallas_call(
        body, name=name,
        out_shape=jax.ShapeDtypeStruct((8 * m_per, n), v.dtype),
        in_specs=[pl.BlockSpec(memory_space=pltpu.VMEM)],
        out_specs=pl.BlockSpec(memory_space=pltpu.VMEM),
        scratch_shapes=[pltpu.SemaphoreType.DMA((7,)), pltpu.SemaphoreType.DMA((7,)), pltpu.SemaphoreType.DMA],
        compiler_params=_params(),
    )(v)


ANY = pl.BlockSpec(memory_space=pl.ANY)
PEER_CHIPS = lambda x, y: [(x, 1 - y), (1 - x, y), (1 - x, 1 - y)]


def _shard_region(ref, kind, k, rl, cl, r0, nr, c0, nc):
    if kind == "col":
        return ref.at[pl.ds(r0, nr), pl.ds(k * cl + c0, nc)]
    return ref.at[pl.ds(k * rl + r0, nr), pl.ds(c0, nc)]


def _place_shard(w, kind, chip, name):
    rl, cl = w.shape
    tr = _pick(rl, (256, 128, 64))
    nt = rl // tr

    def body(chip_ref, w_ref, o_ref):
        o_ref[...] = w_ref[...].astype(BF16)

    o_map = (lambda i, chip_ref: (i, chip_ref[0])) if kind == "col" else (lambda i, chip_ref: (chip_ref[0] * nt + i, 0))
    return pl.pallas_call(
        body, name=name,
        grid_spec=pltpu.PrefetchScalarGridSpec(
            num_scalar_prefetch=1, grid=(nt,),
            in_specs=[pl.BlockSpec((tr, cl), lambda i, chip_ref: (i, 0))], out_specs=pl.BlockSpec((tr, cl), o_map)),
        out_shape=jax.ShapeDtypeStruct((rl, 4 * cl) if kind == "col" else (4 * rl, cl), BF16),
        compiler_params=_params(("parallel",)),
    )(chip.reshape(1), w)


def _gather_weights(placed, kinds):
    n = len(placed)
    shard_shapes = [(p.shape[0], p.shape[1] // 4) if k == "col" else (p.shape[0] // 4, p.shape[1])
                    for p, k in zip(placed, kinds)]

    def body(*refs):
        outs = refs[n:2 * n]
        send_sems, recv_sems = refs[2 * n:]
        x, y, c = _mesh_pos()
        me = 2 * x + y
        peers = PEER_CHIPS(x, y)
        sends = []
        for a in range(n):
            rl, cl = shard_shapes[a]
            rh = rl // 2
            reg = functools.partial(_shard_region, outs[a], kinds[a], rl=rl, cl=cl, c0=0, nc=cl)
            for j, (px, py) in enumerate(peers):
                half = reg(k=me, r0=c * rh, nr=rh)
                cp = pltpu.make_async_remote_copy(
                    src_ref=half, dst_ref=half, send_sem=send_sems.at[a, j], recv_sem=recv_sems.at[a, j],
                    device_id=(px, py, c), device_id_type=MESH_ID)
                cp.start()
                sends.append(cp)
        for a in range(n):
            rl, cl = shard_shapes[a]
            rh = rl // 2
            reg = functools.partial(_shard_region, outs[a], kinds[a], rl=rl, cl=cl, c0=0, nc=cl)
            for j, (px, py) in enumerate(peers):
                got = reg(k=2 * px + py, r0=c * rh, nr=rh)
                pltpu.make_async_remote_copy(src_ref=got, dst_ref=got, send_sem=send_sems.at[a, j],
                                             recv_sem=recv_sems.at[a, j], device_id=(px, py, c),
                                             device_id_type=MESH_ID).wait_recv()
                fwd = pltpu.make_async_remote_copy(src_ref=got, dst_ref=got, send_sem=send_sems.at[a, 3 + j],
                                                   recv_sem=recv_sems.at[a, 3 + j], device_id=(x, y, 1 - c),
                                                   device_id_type=MESH_ID)
                fwd.start()
                sends.append(fwd)
        for a in range(n):
            rl, cl = shard_shapes[a]
            rh = rl // 2
            reg = functools.partial(_shard_region, outs[a], kinds[a], rl=rl, cl=cl, c0=0, nc=cl)
            for j, (px, py) in enumerate(peers):
                got = reg(k=2 * px + py, r0=(1 - c) * rh, nr=rh)
                pltpu.make_async_remote_copy(src_ref=got, dst_ref=got, send_sem=send_sems.at[a, 3 + j],
                                             recv_sem=recv_sems.at[a, 3 + j], device_id=(x, y, 1 - c),
                                             device_id_type=MESH_ID).wait_recv()
        for cp in sends:
            cp.wait_send()

    return pl.pallas_call(
        body, name="gather_weights",
        out_shape=[jax.ShapeDtypeStruct(p.shape, p.dtype) for p in placed],
        in_specs=[ANY] * n, out_specs=[ANY] * n, input_output_aliases={a: a for a in range(n)},
        scratch_shapes=[pltpu.SemaphoreType.DMA((n, 6)), pltpu.SemaphoreType.DMA((n, 6))],
        compiler_params=_params(),
    )(*placed)


def _half(kind, r, c):
    return (r // 2, c) if kind == "col" else (r, c // 2)


def _half_of(ref, kind, which):
    r, c = ref.shape
    hr, hc = _half(kind, r, c)
    return ref.at[pl.ds(which * hr, hr), :] if kind == "col" else ref.at[:, pl.ds(which * hc, hc)]


def _rs_sibling(grads, kinds):
    n = len(grads)

    def body(*refs):
        srcs, dsts = refs[:n], refs[n:2 * n]
        send_sems, recv_sems = refs[2 * n:]
        x, y, c = _mesh_pos()
        cps = [pltpu.make_async_remote_copy(src_ref=_half_of(srcs[a], kinds[a], 1 - c), dst_ref=dsts[a],
                                            send_sem=send_sems.at[a], recv_sem=recv_sems.at[a],
                                            device_id=(x, y, 1 - c), device_id_type=MESH_ID) for a in range(n)]
        for cp in cps:
            cp.start()
        for cp in cps:
            cp.wait()

    return pl.pallas_call(
        body, name="rs_sibling",
        out_shape=[jax.ShapeDtypeStruct(_half(k, *g.shape), g.dtype) for g, k in zip(grads, kinds)],
        in_specs=[ANY] * n, out_specs=[ANY] * n,
        scratch_shapes=[pltpu.SemaphoreType.DMA((n,)), pltpu.SemaphoreType.DMA((n,))],
        compiler_params=_params(),
    )(*grads)


def _pair_sum(gf, rv, kind, ci, name):
    r, c = rv.shape
    tr = _pick(r, (128, 64, 32, 16, 8))
    nt = r // tr

    def body(ci_ref, g_ref, r_ref, o_ref):
        o_ref[...] = (g_ref[...] + r_ref[...]).astype(BF16)

    g_map = (lambda i, ci_ref: (ci_ref[0] * nt + i, 0)) if kind == "col" else (lambda i, ci_ref: (i, ci_ref[0]))
    blk = pl.BlockSpec((tr, c), lambda i, ci_ref: (i, 0))
    return pl.pallas_call(
        body, name=name,
        grid_spec=pltpu.PrefetchScalarGridSpec(num_scalar_prefetch=1, grid=(nt,),
                                               in_specs=[pl.BlockSpec((tr, c), g_map), blk], out_specs=blk),
        out_shape=jax.ShapeDtypeStruct((r, c), BF16),
        compiler_params=_params(("parallel",)),
    )(ci.reshape(1), gf, rv)


def _rs_chips(pairs, kinds):
    n = len(pairs)
    shapes = []
    for p, k in zip(pairs, kinds):
        r, c = p.shape
        shapes.append((r, c // 4) if k == "col" else (r // 4, c))

    def body(*refs):
        srcs, dsts = refs[:n], refs[n:2 * n]
        send_sems, recv_sems = refs[2 * n:]
        x, y, c = _mesh_pos()
        me = 2 * x + y
        peers = PEER_CHIPS(x, y)
        cps = []
        for a in range(n):
            rl, cl = shapes[a]
            reg = functools.partial(_shard_region, srcs[a], kinds[a], rl=rl, cl=cl, r0=0, nr=rl, c0=0, nc=cl)
            for j, (px, py) in enumerate(peers):
                cps.append(pltpu.make_async_remote_copy(
                    src_ref=reg(k=2 * px + py), dst_ref=dsts[a].at[me], send_sem=send_sems.at[a, j],
                    recv_sem=recv_sems.at[a, j], device_id=(px, py, c), device_id_type=MESH_ID))
                cps[-1].start()
        for a in range(n):
            for j, (px, py) in enumerate(peers):
                slot = dsts[a].at[2 * px + py]
                pltpu.make_async_remote_copy(src_ref=slot, dst_ref=slot, send_sem=send_sems.at[a, j],
                                             recv_sem=recv_sems.at[a, j], device_id=(px, py, c),
                                             device_id_type=MESH_ID).wait_recv()
        for cp in cps:
            cp.wait_send()

    return pl.pallas_call(
        body, name="rs_chips",
        out_shape=[jax.ShapeDtypeStruct((4,) + s, p.dtype) for s, p in zip(shapes, pairs)],
        in_specs=[ANY] * n, out_specs=[ANY] * n,
        scratch_shapes=[pltpu.SemaphoreType.DMA((n, 3)), pltpu.SemaphoreType.DMA((n, 3))],
        compiler_params=_params(),
    )(*pairs)


def _sum_chips(pair, got, kind, pos, name):
    _, r, c = got.shape
    tr = _pick(r, (256, 128, 64, 32, 16))
    nt = r // tr

    def body(pos_ref, own_ref, g1_ref, g2_ref, g3_ref, o_ref):
        o_ref[...] = ((own_ref[...].astype(F32) + g1_ref[0].astype(F32)) + g2_ref[0].astype(F32)) + g3_ref[0].astype(F32)

    if kind == "col":
        own_map = lambda i, p: (i, p[1])
        out_map = lambda i, p: (p[0] * nt + i, 0)
        out_shape = (2 * r, c)
    else:
        own_map = lambda i, p: (p[1] * nt + i, 0)
        out_map = lambda i, p: (i, p[0])
        out_shape = (r, 2 * c)
    peer = lambda m: pl.BlockSpec((1, tr, c), lambda i, p: (p[1] ^ m, i, 0))
    return pl.pallas_call(
        body, name=name,
        grid_spec=pltpu.PrefetchScalarGridSpec(
            num_scalar_prefetch=1, grid=(nt,),
            in_specs=[pl.BlockSpec((tr, c), own_map), peer(1), peer(2), peer(3)],
            out_specs=pl.BlockSpec((tr, c), out_map)),
        out_shape=jax.ShapeDtypeStruct(out_shape, F32),
        compiler_params=_params(("parallel",)),
    )(pos, pair, got, got, got)


def _rs_back(halves, kinds):
    n = len(halves)

    def body(*refs):
        outs = refs[n:2 * n]
        send_sems, recv_sems = refs[2 * n:]
        x, y, c = _mesh_pos()
        cps = []
        for a in range(n):
            mine = _half_of(outs[a], kinds[a], c)
            cps.append(pltpu.make_async_remote_copy(src_ref=mine, dst_ref=mine, send_sem=send_sems.at[a],
                                                    recv_sem=recv_sems.at[a], device_id=(x, y, 1 - c),
                                                    device_id_type=MESH_ID))
            cps[-1].start()
        for a in range(n):
            other = _half_of(outs[a], kinds[a], 1 - c)
            pltpu.make_async_remote_copy(src_ref=other, dst_ref=other, send_sem=send_sems.at[a],
                                         recv_sem=recv_sems.at[a], device_id=(x, y, 1 - c),
                                         device_id_type=MESH_ID).wait_recv()
        for cp in cps:
            cp.wait_send()

    return pl.pallas_call(
        body, name="rs_back",
        out_shape=[jax.ShapeDtypeStruct(h.shape, h.dtype) for h in halves],
        in_specs=[ANY] * n, out_specs=[ANY] * n, input_output_aliases={a: a for a in range(n)},
        scratch_shapes=[pltpu.SemaphoreType.DMA((n,)), pltpu.SemaphoreType.DMA((n,))],
        compiler_params=_params(),
    )(*halves)


def _rope_tables(l, lc):
    rows = l // GRID_W
    row = jnp.repeat(jnp.arange(rows, dtype=F32), GRID_W)
    col = jnp.tile(jnp.arange(GRID_W, dtype=F32), rows)
    n_freq = RET_DH // 4
    inv_freq = ROPE_THETA ** (-jnp.arange(n_freq, dtype=F32) / n_freq)
    ang = jnp.concatenate([row[:, None] * inv_freq, col[:, None] * inv_freq], axis=-1)
    cos_t = jnp.repeat(jnp.cos(ang), 2, axis=-1)
    sin_t = jnp.repeat(jnp.sin(ang), 2, axis=-1) * jnp.tile(jnp.array([-1.0, 1.0], F32), RET_DH // 2)
    cos_t = jnp.concatenate([jnp.ones((lc, RET_DH), F32), cos_t], axis=0)
    sin_t = jnp.concatenate([jnp.zeros((lc, RET_DH), F32), sin_t], axis=0)
    return cos_t, sin_t


def _s5_pack(a):
    blk = lambda t: t.reshape(1, S5_NB, 128, S5_STATE)
    lre = jnp.stack([a["s5_lambda_re_f"][0], a["s5_lambda_re_b"][0]]).reshape(2, S5_NB, 8, S5_STATE)
    lim = jnp.stack([a["s5_lambda_im_f"][0], a["s5_lambda_im_b"][0]]).reshape(2, S5_NB, 8, S5_STATE)
    lst = jnp.stack([a["s5_log_step_f"][0], a["s5_log_step_b"][0]]).reshape(2, S5_NB, 8, 1)
    b_re = blk(a["s5_b_re"][0].transpose(0, 2, 1))
    b_im = blk(a["s5_b_im"][0].transpose(0, 2, 1))
    return (lre, lim, lst, b_re, b_im, blk(a["s5_c_re"][0]), blk(a["s5_c_im"][0]),
            a["s5_d"].reshape(1, S5_NB, 1, 128))


def _s5_unpack(g):
    glre, glim, glst, gbre, gbim, gcre, gcim, gd = g
    unb = lambda t: t.reshape(S5_GROUPS, S5_GROUP, S5_STATE).transpose(0, 2, 1)[None]
    return {
        "s5_lambda_re_f": glre[0].reshape(1, S5_GROUPS, S5_STATE), "s5_lambda_re_b": glre[1].reshape(1, S5_GROUPS, S5_STATE),
        "s5_lambda_im_f": glim[0].reshape(1, S5_GROUPS, S5_STATE), "s5_lambda_im_b": glim[1].reshape(1, S5_GROUPS, S5_STATE),
        "s5_log_step_f": glst[0].reshape(1, S5_GROUPS), "s5_log_step_b": glst[1].reshape(1, S5_GROUPS),
        "s5_b_re": unb(gbre), "s5_b_im": unb(gbim),
        "s5_c_re": gcre.reshape(1, S5_GROUPS, S5_GROUP, S5_STATE), "s5_c_im": gcim.reshape(1, S5_GROUPS, S5_GROUP, S5_STATE),
        "s5_d": gd.reshape(1, S5_WIDTH),
    }


def _local_step(a, wb, mx, mc, conv_w):
    x, ctx, tgt = a["x"][0], a["ctx"][0], a["loss_target"][0]
    l, lc = x.shape[0], ctx.shape[0]
    la = l + lc
    nct, ncc, nrc, cn = lc // TOK_TILE, lc // S5_T, lc // RET_CHUNK, la // S5_T
    n1w, n2w, fnw = a["norm1_w"], a["norm2_w"], a["final_norm_w"].reshape(1, D_MODEL)
    conv_b, b_glu = a["conv_b"], a["s5_b_glu"]
    ld2 = jnp.concatenate([a["ret_log_decay_f"], a["ret_log_decay_b"]], axis=0)
    mod4 = jnp.concatenate([mc[0:2], mx[0:2]], axis=0)
    mod3 = mx[2:5]
    gate5 = mx[5:6]
    cos_t, sin_t = _rope_tables(l, lc)
    s5p = _s5_pack(a)

    p_all, h1b = _norm_inproj(x, ctx, n1w, mod4, wb["w_in"])
    p3 = p_all.reshape(cn, S5_T, IN_COLS)
    gg, xw, yw, a16 = _s5_gen(*s5p)
    sloc = _s5_to_states(p3, xw, "s5_state")
    a16s = a16.transpose(1, 0, 2, 3).reshape(2, S5_GROUPS, 128)
    hs = _s5_scan(sloc, a16s, ncc)
    y_all = _s5_from_states(p3, gg, hs, yw, False, "s5_out").reshape(la, S5_WIDTH)
    s5x = _s5_glu(y_all, wb["s5_w_glu"], b_glu, nct)
    tabs = _ret_tables(ld2)
    of, ob, ssf, ssb = _ret_scan(p_all, cos_t, sin_t, tabs, nrc)
    retx, y_ret = _ret_gate(of, ob, p_all, nct)
    x1, mix, h2b, up = _outproj_up(x, s5x, retx, wb["w_out"], mod3, n2w, wb["w_up"])
    act, dx2, ddn, dact, acc_f = _ffn_loss(up, x1, conv_w, conv_b, wb["w_down"], gate5, fnw, tgt)

    g = {}
    g["w_down"] = _mm_tn(act, ddn, name="gw_down")
    dup, acc_c = _convglu_bwd(up, dact, conv_w, conv_b)
    g["w_up"] = _mm_tn(h2b, dup, name="gw_up")
    dx1, dmixb, dmix, acc_2 = _up_bwd(dup, wb["w_up"], wb["w_out"], x1, dx2, mix, mod3, n2w)
    g["w_out"] = jnp.concatenate([_mm_tn(s5x, dmixb, name="gw_out_s5"), _mm_tn(retx, dmixb, name="gw_out_ret")], axis=0)

    dy_s5, g["s5_w_glu"], g["s5_b_glu"] = _s5_glu_bwd(y_all, dmix, wb["s5_w_glu"], b_glu, nct)
    dy3 = dy_s5.reshape(cn, S5_T, S5_WIDTH)
    e = _s5_to_states(dy3, yw, "s5_bwd_h")
    ds, da16 = _s5_scan_bwd(e, hs, a16s, ncc)
    du = _s5_from_states(dy3, gg, ds, xw, True, "s5_bwd_u").reshape(la, S5_WIDTH)
    dkb = _s5_bwd_kb(p3, dy3)
    dwst = _s5_bwd_w(p3, ds, "s5_bwd_wst")
    dwout = _s5_bwd_w(dy3, hs, "s5_bwd_wout")
    da16p = da16.reshape(2, S5_NB, 8, 128).transpose(1, 0, 2, 3)
    g.update(_s5_unpack(_s5_gen_bwd(*s5p, dkb, dwst, dwout, da16p)))

    dy_ret, dg = _ret_gate_bwd(y_ret, p_all, dmix, nct)
    dqf, dkf, dvf, dqb, dkb_, dvb, dld = _ret_scan_bwd(p_all, cos_t, sin_t, tabs, ssf, ssb, dy_ret, nrc)
    g["ret_log_decay_f"] = dld[0, :, 0, 0].reshape(1, RET_HEADS)
    g["ret_log_decay_b"] = dld[1, :, 0, 0].reshape(1, RET_HEADS)
    dp, grad_x, acc_1 = _in_bwd(dqf, dkf, dvf, dqb, dkb_, dvb, du, dg, cos_t, sin_t, wb["w_in"], x, ctx, n1w, mod4, dx1)
    g["w_in"] = _mm_tn(h1b, dp, name="gw_in")

    g["norm1_w"], g["norm2_w"], g["final_norm_w"] = acc_1[0:1], acc_2[0:1], acc_f[0]
    g["conv_w"], g["conv_b"] = acc_c[0:3], acc_c[3:4]
    zero = jnp.zeros((1, D_MODEL), F32)
    dmx = jnp.concatenate([acc_1[3:5], acc_2[1:2], acc_2[2:4], acc_f[1:2]], axis=0)
    dmc = jnp.concatenate([acc_1[1:3], zero, zero, zero, zero], axis=0)
    return acc_f[2, 0], grad_x, g, dmx, dmc


WEIGHT_NAMES = ("c_ctx", "w_mod", "b_mod", "norm1_w", "w_in", "s5_lambda_re_f", "s5_lambda_im_f", "s5_log_step_f",
                "s5_lambda_re_b", "s5_lambda_im_b", "s5_log_step_b", "s5_b_re", "s5_b_im", "s5_c_re", "s5_c_im",
                "s5_d", "s5_w_glu", "s5_b_glu", "ret_log_decay_f", "ret_log_decay_b", "w_out", "norm2_w", "w_up",
                "conv_w", "conv_b", "w_down", "final_norm_w")
BIG_NAMES = ("w_in", "w_out", "w_up", "w_down", "s5_w_glu")
BIG_KINDS = ("col", "row", "col", "row", "row")
SMALL_NAMES = ("norm1_w", "norm2_w", "final_norm_w", "conv_b", "conv_w", "s5_lambda_re_f", "s5_lambda_im_f",
               "s5_log_step_f", "s5_lambda_re_b", "s5_lambda_im_b", "s5_log_step_b", "s5_b_re", "s5_b_im", "s5_c_re",
               "s5_c_im", "s5_d", "s5_b_glu", "ret_log_decay_f", "ret_log_decay_b")
ROW = 1024
N_CHIPS = 4


def _pack_rows(parts):
    flat = jnp.concatenate([p.reshape(-1) for p in parts])
    n = flat.shape[0]
    rows = -(-n // (8 * ROW)) * 8
    return jnp.pad(flat, (0, rows * ROW - n)).reshape(rows, ROW)


def _unpack_rows(packed, shapes):
    flat = packed.reshape(-1)
    out, off = [], 0
    for s in shapes:
        n = math.prod(s)
        out.append(flat[off:off + n].reshape(s))
        off += n
    return out


def _step(a):
    xi, yi, ci = _mesh_pos()
    chip = 2 * xi + yi
    dev = 2 * chip + ci

    cw_loc = a["conv_w"].reshape(-1)
    small_in = jnp.concatenate([a["c"].reshape(-1), jnp.pad(cw_loc, (0, 24 * 128 - cw_loc.shape[0]))]).reshape(32, 128)
    sg = _all_gather8(small_in, "gather_cond").reshape(8, 32, 128)
    c_all = sg[:, 0:8].reshape(8, D_MODEL)
    conv_w = sg[0::2, 8:32].reshape(N_CHIPS, -1)[:, :cw_loc.shape[0]].reshape(N_CHIPS, 3, -1)
    conv_w = conv_w.transpose(1, 0, 2).reshape(3, D_FF)

    placed = [_place_shard(a[n][0], k, chip, "place_" + n) for n, k in zip(BIG_NAMES, BIG_KINDS)]
    wb = dict(zip(BIG_NAMES, _gather_weights(placed, BIG_KINDS)))

    w_mod_b = a["w_mod"][0].astype(BF16)
    c_ctx = a["c_ctx"].reshape(1, D_MODEL)
    b_loc = lax.dynamic_slice_in_dim(a["b_mod"], chip * MOD_COLS, MOD_COLS, 1)
    m_loc, s_b = _mod_fwd(c_all, c_ctx, w_mod_b, b_loc)
    mg = _all_gather8(m_loc, "gather_mod").reshape(8, MOD_ROWS, MOD_COLS)
    m_full = mg[0::2].transpose(1, 0, 2).reshape(MOD_ROWS, 6 * D_MODEL)
    mx = lax.dynamic_slice_in_dim(m_full, dev, 1, 0).reshape(6, D_MODEL)
    mc = m_full[8].reshape(6, D_MODEL)

    loss_part, grad_x, g, dmx, dmc = _local_step(a, wb, mx, mc, conv_w)
    loss = lax.psum(loss_part, ("x", "y", "c"))

    dm_pair = jnp.concatenate([dmx.reshape(1, -1), dmc.reshape(1, -1), jnp.zeros((6, 6 * D_MODEL), F32)], axis=0)
    dm_all = _all_gather8(dm_pair, "gather_dmod").reshape(8, 8, 6 * D_MODEL)
    dm16, gb_mod = _mod_bwd_sum(dm_all)
    dm_loc = lax.dynamic_slice_in_dim(dm16, chip * MOD_COLS, MOD_COLS, 1)
    gw_mod, gcc = _mod_bwd_w(dm_loc, s_b, c_ctx, w_mod_b)

    small_parts = [g[n] for n in SMALL_NAMES] + [gcc[0]]
    small_shapes = [p.shape for p in small_parts]
    sp = _pack_rows(small_parts)
    tot = _sum_slots(_all_gather8(sp, "gather_small_grads").reshape(8, sp.shape[0], ROW), "sum_small_grads")
    small = dict(zip(SMALL_NAMES + ("c_ctx",), _unpack_rows(tot, small_shapes)))
    grads = {n: small[n].reshape(a[n].shape) for n in SMALL_NAMES if n != "conv_w"}
    grads["c_ctx"] = (0.5 * small["c_ctx"]).reshape(a["c_ctx"].shape)
    grads["conv_w"] = lax.dynamic_slice_in_dim(small["conv_w"], chip * (D_FF // N_CHIPS), D_FF // N_CHIPS, 1)[None]
    grads["b_mod"] = gb_mod
    grads["w_mod"] = gw_mod[None]

    gfull = [g[n] for n in BIG_NAMES]
    sib = _rs_sibling(gfull, BIG_KINDS)
    pairs = [_pair_sum(gf, rv, k, ci, "rs_pair_" + n) for gf, rv, k, n in zip(gfull, sib, BIG_KINDS, BIG_NAMES)]
    pos = jnp.stack([ci, chip])
    halves = [_sum_chips(p, t, k, pos, "rs_sum_" + n)
              for p, t, k, n in zip(pairs, _rs_chips(pairs, BIG_KINDS), BIG_KINDS, BIG_NAMES)]
    for n, t in zip(BIG_NAMES, _rs_back(halves, BIG_KINDS)):
        grads[n] = t[None]

    delta, new_m, new_v = {}, {}, {}
    for n in BIG_NAMES + ("w_mod",):
        for dst, t in zip((delta, new_m, new_v), _adamw(a[n][0], grads[n][0], a["m_" + n][0], a["v_" + n][0], "adamw_" + n)):
            dst[n] = t[None]
    rest = [n for n in WEIGHT_NAMES if n not in BIG_NAMES and n != "w_mod"]
    shapes = [a[n].shape for n in rest]
    pr = lambda pre: _pack_rows([a[pre + n] for n in rest])
    for dst, t in zip((delta, new_m, new_v),
                      _adamw(pr(""), _pack_rows([grads[n] for n in rest]), pr("m_"), pr("v_"), "adamw_small")):
        dst.update(zip(rest, _unpack_rows(t, shapes)))

    return (loss, grad_x[None], *[grads[n] for n in WEIGHT_NAMES], *[delta[n] for n in WEIGHT_NAMES],
            *[new_m[n] for n in WEIGHT_NAMES], *[new_v[n] for n in WEIGHT_NAMES])


def kernel(x, c, ctx, c_ctx, w_mod, b_mod, norm1_w, w_in, s5_lambda_re_f, s5_lambda_im_f, s5_log_step_f, s5_lambda_re_b, s5_lambda_im_b, s5_log_step_b, s5_b_re, s5_b_im, s5_c_re, s5_c_im, s5_d, s5_w_glu, s5_b_glu, ret_log_decay_f, ret_log_decay_b, w_out, norm2_w, w_up, conv_w, conv_b, w_down, final_norm_w, loss_target, m_c_ctx, m_w_mod, m_b_mod, m_norm1_w, m_w_in, m_s5_lambda_re_f, m_s5_lambda_im_f, m_s5_log_step_f, m_s5_lambda_re_b, m_s5_lambda_im_b, m_s5_log_step_b, m_s5_b_re, m_s5_b_im, m_s5_c_re, m_s5_c_im, m_s5_d, m_s5_w_glu, m_s5_b_glu, m_ret_log_decay_f, m_ret_log_decay_b, m_w_out, m_norm2_w, m_w_up, m_conv_w, m_conv_b, m_w_down, m_final_norm_w, v_c_ctx, v_w_mod, v_b_mod, v_norm1_w, v_w_in, v_s5_lambda_re_f, v_s5_lambda_im_f, v_s5_log_step_f, v_s5_lambda_re_b, v_s5_lambda_im_b, v_s5_log_step_b, v_s5_b_re, v_s5_b_im, v_s5_c_re, v_s5_c_im, v_s5_d, v_s5_w_glu, v_s5_b_glu, v_ret_log_decay_f, v_ret_log_decay_b, v_w_out, v_norm2_w, v_w_up, v_conv_w, v_conv_b, v_w_down, v_final_norm_w):
    return _step(dict(locals()))
```

```python
import functools
import math

import jax
import jax.numpy as jnp
from jax import lax
from jax.experimental import pallas as pl
from jax.experimental.pallas import tpu as pltpu

F32 = jnp.float32
BF16 = jnp.bfloat16

D_MODEL = 1024
S5_WIDTH = 512
S5_GROUPS = 32
S5_GROUP = 16
S5_STATE = 64
RET_WIDTH = 512
RET_HEADS = 4
RET_DH = 128
RET_CHUNK = 128
GRID_W = 64
ROPE_THETA = 10000.0
D_FF = 2816
NORM_EPS = 1e-6
IN_COLS = S5_WIDTH + 4 * RET_WIDTH

S5_T = 16
S5_NB = 4
S5_BW = S5_T * 128
S5_SW = 8 * 2 * S5_STATE

ADAM_LR, ADAM_B1, ADAM_B2, ADAM_EPS, ADAM_WD, ADAM_STEP = 0.001, 0.9, 0.999, 1e-08, 0.01, 10

VMEM_LIMIT = 56 * 1024 * 1024
MESH_ID = pl.DeviceIdType.MESH


def _params(sem=None):
    return pltpu.CompilerParams(dimension_semantics=sem, vmem_limit_bytes=VMEM_LIMIT)


def _full(shape):
    n = len(shape)
    return pl.BlockSpec(shape, lambda *_: (0,) * n)


def _dot(a, b):
    return jnp.dot(a, b, preferred_element_type=F32)


def _dot_nt(a, b):
    return lax.dot_general(a, b, (((1,), (1,)), ((), ())), preferred_element_type=F32)


def _dot_tn(a, b):
    return lax.dot_general(a, b, (((0,), (0,)), ((), ())), preferred_element_type=F32)


def _dot_hi(a, b):
    return jnp.dot(a, b, preferred_element_type=F32, precision=lax.Precision.HIGHEST)


def _dot_nt_hi(a, b):
    return lax.dot_general(a, b, (((1,), (1,)), ((), ())), preferred_element_type=F32,
                           precision=lax.Precision.HIGHEST)


def _gelu(x):
    return 0.5 * x * (1.0 + jnp.tanh(0.7978845608028654 * (x + 0.044715 * (x * x * x))))


def _sigmoid(x):
    return 1.0 / (1.0 + jnp.exp(-x))


def _silu(x):
    return x * _sigmoid(x)


def _rms_mod(x, nw, sh, sc):
    r = lax.rsqrt(jnp.mean(x * x, axis=-1, keepdims=True) + NORM_EPS)
    return (x * r * nw) * (1.0 + sc) + sh


def _rms(x, nw):
    r = lax.rsqrt(jnp.mean(x * x, axis=-1, keepdims=True) + NORM_EPS)
    return x * r * nw


def _head_norm_gate(y, g):
    mu = jnp.mean(y, axis=-1, keepdims=True)
    yc = y - mu
    var = jnp.mean(yc * yc, axis=-1, keepdims=True)
    return _silu(g) * (yc * lax.rsqrt(var + NORM_EPS))


def _swap_pairs(t):
    lane = lax.broadcasted_iota(jnp.int32, t.shape, 1)
    return jnp.where(lane % 2 == 0, pltpu.roll(t, RET_DH - 1, 1), pltpu.roll(t, 1, 1))


def _rope(t, cos_t, sin_t):
    return t * cos_t + _swap_pairs(t) * sin_t


def _rope_t(dt, cos_t, sin_t):
    return dt * cos_t + _swap_pairs(dt * sin_t)


def _pick(n, prefs):
    for p in prefs:
        if n % p == 0:
            return p
    return n


def _mm_tn(a, b, *, name):
    m, k = a.shape
    n = b.shape[1]
    tm = _pick(m, (512, 256, 128))
    tn = _pick(n, (1408, 1024, 1280, 512))

    def body(a_ref, b_ref, o_ref):
        @pl.when(pl.program_id(1) == 0)
        def _():
            o_ref[...] = jnp.zeros_like(o_ref)
        o_ref[...] += _dot_tn(a_ref[...], b_ref[...])

    return pl.pallas_call(
        body, name=name, grid=(n // tn, m // tm),
        in_specs=[pl.BlockSpec((tm, k), lambda j, i: (i, 0)), pl.BlockSpec((tm, tn), lambda j, i: (i, j))],
        out_specs=pl.BlockSpec((k, tn), lambda j, i: (0, j)),
        out_shape=jax.ShapeDtypeStruct((k, n), F32),
        compiler_params=_params(("parallel", "arbitrary")),
    )(a, b)


TOK_TILE = 256


def _norm_inproj(x, ctx, n1w, mod4, w_in_b):
    l, lc = x.shape[0], ctx.shape[0]
    tm = TOK_TILE
    nct = lc // tm
    la = l + lc

    def body(x_ref, c_ref, nw_ref, mod_ref, w_ref, p_ref, h_ref):
        is_ctx = pl.program_id(0) < nct
        xt = jnp.where(is_ctx, c_ref[...], x_ref[...])
        sh = jnp.where(is_ctx, mod_ref[0:1, :], mod_ref[2:3, :])
        sc = jnp.where(is_ctx, mod_ref[1:2, :], mod_ref[3:4, :])
        hb = _rms_mod(xt, nw_ref[...], sh, sc).astype(BF16)
        h_ref[...] = hb
        p_ref[...] = _dot(hb, w_ref[...])

    return pl.pallas_call(
        body, name="norm_inproj", grid=(la // tm,),
        in_specs=[pl.BlockSpec((tm, D_MODEL), lambda i: (jnp.maximum(i - nct, 0), 0)),
                  pl.BlockSpec((tm, D_MODEL), lambda i: (jnp.minimum(i, nct - 1), 0)),
                  _full((1, D_MODEL)), _full((4, D_MODEL)), _full((D_MODEL, IN_COLS))],
        out_specs=[pl.BlockSpec((tm, IN_COLS), lambda i: (i, 0)), pl.BlockSpec((tm, D_MODEL), lambda i: (i, 0))],
        out_shape=[jax.ShapeDtypeStruct((la, IN_COLS), F32), jax.ShapeDtypeStruct((la, D_MODEL), BF16)],
        compiler_params=_params(("parallel",)),
    )(x, ctx, n1w, mod4, w_in_b)


def _iota2(shape, dim):
    return lax.broadcasted_iota(jnp.int32, shape, dim)


def _group_mask(rows, cols, row_div, col_div):
    return jnp.where(_iota2((rows, cols), 0) // row_div == _iota2((rows, cols), 1) // col_div, 1.0, 0.0).astype(F32)


def _s5_gen_dir(lre, lim, lst, b_re, b_im, c_re, c_im):
    step = jnp.exp(lst)
    mag = jnp.exp(lre * step)
    ar = mag * jnp.cos(lim * step)
    ai = mag * jnp.sin(lim * step)
    den = lre * lre + lim * lim
    xr = ar - 1.0
    cr = (xr * lre + ai * lim) / den
    ci = (ai * lre - xr * lim) / den
    rexp = _group_mask(128, 8, S5_GROUP, 1)
    are, aie = _dot_hi(rexp, ar), _dot_hi(rexp, ai)
    cre, cie = _dot_hi(rexp, cr), _dot_hi(rexp, ci)
    bbr = cre * b_re - cie * b_im
    bbi = cre * b_im + cie * b_re
    gmask = _group_mask(128, 128, S5_GROUP, S5_GROUP)
    pr, pi = jnp.ones_like(are), jnp.zeros_like(are)
    xs, ys = [], []
    for t in range(S5_T + 1):
        if t < S5_T:
            xs.append(jnp.concatenate([bbr * pr - bbi * pi, bbr * pi + bbi * pr], axis=1))
        ys.append(jnp.concatenate([c_re * pr - c_im * pi, -(c_re * pi + c_im * pr)], axis=1))
        pr, pi = pr * are - pi * aie, pr * aie + pi * are
    gs = [_dot_nt_hi(x_t, ys[0]) * gmask for x_t in xs]
    r16, i16 = ar, ai
    for _ in range(4):
        r16, i16 = r16 * r16 - i16 * i16, 2.0 * r16 * i16
    return xs, ys, gs, jnp.concatenate([r16, i16], axis=1)


def _s5_expand(z):
    return jnp.concatenate([z] * 8, axis=1) * _group_mask(128, S5_SW, S5_GROUP, 128)


def _s5_contract(z):
    zm = z * _group_mask(128, S5_SW, S5_GROUP, 128)
    acc = zm[:, 0:128]
    for k in range(1, 8):
        acc = acc + zm[:, 128 * k:128 * (k + 1)]
    return acc


def _s5_param_specs():
    blk3 = lambda r, c: pl.BlockSpec((1, 1, r, c), lambda b, *_: (0, b, 0, 0))
    dir3 = lambda r, c: pl.BlockSpec((2, 1, r, c), lambda b, *_: (0, b, 0, 0))
    return [dir3(8, S5_STATE), dir3(8, S5_STATE), dir3(8, 1), blk3(128, S5_STATE), blk3(128, S5_STATE),
            blk3(128, S5_STATE), blk3(128, S5_STATE), blk3(1, 128)]


def _s5_gen(lre, lim, lst, b_re, b_im, c_re, c_im, dvec):
    def body(lre_ref, lim_ref, lst_ref, bre_ref, bim_ref, cre_ref, cim_ref, d_ref, gg_ref, xw_ref, yw_ref, a16_ref):
        eye = _group_mask(128, 128, 1, 1)
        g0 = eye * d_ref[0, 0]
        for dr in range(2):
            xs, ys, gs, a16 = _s5_gen_dir(lre_ref[dr, 0], lim_ref[dr, 0], lst_ref[dr, 0], bre_ref[0, 0],
                                          bim_ref[0, 0], cre_ref[0, 0], cim_ref[0, 0])
            a16_ref[0, dr] = a16
            for j in range(S5_T):
                xw_ref[0, dr, j] = xs[S5_T - 1 - j if dr == 0 else j]
                yw_ref[0, dr, j] = ys[j + 1 if dr == 0 else S5_T - j]
            g0 = g0 + gs[0]
            for t in range(1, S5_T):
                gg_ref[0, (S5_T - 1) + t if dr == 0 else (S5_T - 1) - t] = gs[t]
        gg_ref[0, S5_T - 1] = g0

    blk = pl.BlockSpec((1, 2, S5_T, 128, 128), lambda b: (b, 0, 0, 0, 0))
    return pl.pallas_call(
        body, name="s5_gen", grid=(S5_NB,),
        in_specs=_s5_param_specs(),
        out_specs=[pl.BlockSpec((1, 2 * S5_T - 1, 128, 128), lambda b: (b, 0, 0, 0)), blk, blk,
                   pl.BlockSpec((1, 2, 8, 128), lambda b: (b, 0, 0, 0))],
        out_shape=[jax.ShapeDtypeStruct((S5_NB, 2 * S5_T - 1, 128, 128), F32),
                   jax.ShapeDtypeStruct((S5_NB, 2, S5_T, 128, 128), F32),
                   jax.ShapeDtypeStruct((S5_NB, 2, S5_T, 128, 128), F32),
                   jax.ShapeDtypeStruct((S5_NB, 2, 8, 128), F32)],
        compiler_params=_params(("parallel",)),
    )(lre, lim, lst, b_re, b_im, c_re, c_im, dvec)


def _s5_fill_state_mat(w_scr, src_ref, dr):
    for j in range(S5_T):
        w_scr[128 * j:128 * (j + 1), :] = _s5_expand(src_ref[0, dr, j]).astype(BF16)


def _s5_fill_toeplitz(k_scr, gg_ref):
    for j in range(S5_T):
        for i in range(S5_T):
            k_scr[128 * j:128 * (j + 1), 128 * i:128 * (i + 1)] = gg_ref[0, i - j + (S5_T - 1)].astype(BF16)


S5_GEN_SPECS = [pl.BlockSpec((1, 2 * S5_T - 1, 128, 128), lambda b: (b, 0, 0, 0)),
                pl.BlockSpec((1, 2, S5_T, 128, 128), lambda b: (b, 0, 0, 0, 0))]


def _s5_gen_bwd(lre, lim, lst, b_re, b_im, c_re, c_im, dvec, dg, dx, dy, da16):
    def body(lre_ref, lim_ref, lst_ref, bre_ref, bim_ref, cre_ref, cim_ref, d_ref, dg_ref, dx_ref, dy_ref, da16_ref,
             glre_ref, glim_ref, glst_ref, gbre_ref, gbim_ref, gcre_ref, gcim_ref, gd_ref):
        eye = _group_mask(128, 128, 1, 1)
        gd_ref[0, 0] = jnp.sum(dg_ref[0, S5_T - 1] * eye, axis=0, keepdims=True)
        gb = [None, None, None, None]
        for dr in range(2):
            args = (lre_ref[dr, 0], lim_ref[dr, 0], lst_ref[dr, 0], bre_ref[0, 0], bim_ref[0, 0],
                    cre_ref[0, 0], cim_ref[0, 0])
            _, vjp = jax.vjp(_s5_gen_dir, *args)
            dxs = [dx_ref[0, dr, S5_T - 1 - t if dr == 0 else t] for t in range(S5_T)]
            dys = [jnp.zeros((128, 128), F32)] + [dy_ref[0, dr, t - 1 if dr == 0 else S5_T - t]
                                                  for t in range(1, S5_T + 1)]
            dgs = [dg_ref[0, (S5_T - 1) + t if dr == 0 else (S5_T - 1) - t] for t in range(S5_T)]
            g = vjp((dxs, dys, dgs, da16_ref[0, dr]))
            glre_ref[dr, 0] = g[0]
            glim_ref[dr, 0] = g[1]
            glst_ref[dr, 0] = g[2]
            for q in range(4):
                gb[q] = g[3 + q] if gb[q] is None else gb[q] + g[3 + q]
        gbre_ref[0, 0] = gb[0]
        gbim_ref[0, 0] = gb[1]
        gcre_ref[0, 0] = gb[2]
        gcim_ref[0, 0] = gb[3]

    shp = lambda a: jax.ShapeDtypeStruct(a.shape, F32)
    return pl.pallas_call(
        body, name="s5_gen_bwd", grid=(S5_NB,),
        in_specs=_s5_param_specs() + [
            pl.BlockSpec((1, 2 * S5_T - 1, 128, 128), lambda b: (b, 0, 0, 0)),
            pl.BlockSpec((1, 2, S5_T, 128, 128), lambda b: (b, 0, 0, 0, 0)),
            pl.BlockSpec((1, 2, S5_T, 128, 128), lambda b: (b, 0, 0, 0, 0)),
            pl.BlockSpec((1, 2, 8, 128), lambda b: (b, 0, 0, 0))],
        out_specs=_s5_param_specs(),
        out_shape=[shp(lre), shp(lim), shp(lst), shp(b_re), shp(b_im), shp(c_re), shp(c_im), shp(dvec)],
        compiler_params=_params(("parallel",)),
    )(lre, lim, lst, b_re, b_im, c_re, c_im, dvec, dg, dx, dy, da16)


def _s5_ucat(u_ref, lo=0, hi=S5_T):
    return jnp.concatenate([u_ref[:, j, :] for j in range(lo, hi)], axis=1).astype(BF16)


def _s5_put_groups(o_ref, dr, val):
    for gi in range(8):
        o_ref[dr, :, gi, :] = val[:, 128 * gi:128 * (gi + 1)]


def _s5_get_groups(s_ref, dr, n=8):
    return jnp.concatenate([s_ref[dr, :, gi, :] for gi in range(n)], axis=1).astype(BF16)


def _s5_to_states(u3, blocks, name):
    cn = u3.shape[0]

    def body(u_ref, b_ref, o_ref, w_scr):
        u = _s5_ucat(u_ref)
        for dr in range(2):
            _s5_fill_state_mat(w_scr, b_ref, dr)
            _s5_put_groups(o_ref, dr, _dot(u, w_scr[...]))

    return pl.pallas_call(
        body, name=name, grid=(S5_NB,),
        in_specs=[pl.BlockSpec((cn, S5_T, 128), lambda b: (0, 0, b)), S5_GEN_SPECS[1]],
        out_specs=pl.BlockSpec((2, cn, 8, 128), lambda b: (0, 0, b, 0)),
        out_shape=jax.ShapeDtypeStruct((2, cn, S5_GROUPS, 128), F32),
        scratch_shapes=[pltpu.VMEM((S5_BW, S5_SW), BF16)],
        compiler_params=_params(("parallel",)),
    )(u3, blocks)


def _s5_from_states(u3, gg, st, blocks, transposed, name):
    cn = u3.shape[0]

    def body(u_ref, g_ref, s_ref, b_ref, o_ref, k_scr, w_scr):
        u = _s5_ucat(u_ref)
        _s5_fill_toeplitz(k_scr, g_ref)
        y = _dot_nt(u, k_scr[...]) if transposed else _dot(u, k_scr[...])
        for dr in range(2):
            _s5_fill_state_mat(w_scr, b_ref, dr)
            y = y + _dot_nt(_s5_get_groups(s_ref, dr), w_scr[...])
        for i in range(S5_T):
            o_ref[:, i, :] = y[:, 128 * i:128 * (i + 1)]

    return pl.pallas_call(
        body, name=name, grid=(S5_NB,),
        in_specs=[pl.BlockSpec((cn, S5_T, 128), lambda b: (0, 0, b)), S5_GEN_SPECS[0],
                  pl.BlockSpec((2, cn, 8, 128), lambda b: (0, 0, b, 0)), S5_GEN_SPECS[1]],
        out_specs=pl.BlockSpec((cn, S5_T, 128), lambda b: (0, 0, b)),
        out_shape=jax.ShapeDtypeStruct((cn, S5_T, S5_WIDTH), F32),
        scratch_shapes=[pltpu.VMEM((S5_BW, S5_BW), BF16), pltpu.VMEM((S5_BW, S5_SW), BF16)],
        compiler_params=_params(("parallel",)),
    )(u3, gg, st, blocks)


def _s5_a_forms(a):
    ra = pltpu.roll(a, S5_STATE, 1)
    low = _iota2(a.shape, 1) < S5_STATE
    return jnp.where(low, a, ra), jnp.where(low, -ra, a)


def _s5_scan(sloc, a16, ncc):
    cn = sloc.shape[1]

    def body(s_ref, a_ref, h_ref):
        forms = [_s5_a_forms(a_ref[dr]) for dr in range(2)]

        def step(s, hs):
            out = []
            for dr in range(2):
                arr, aii = forms[dr]
                h, rh = hs[dr]
                c = s if dr == 0 else jnp.where(s < ncc, ncc - 1 - s, cn - 1 - (s - ncc))
                h_ref[dr, c] = h
                sc = s_ref[dr, c]
                out.append((h * arr + rh * aii + sc, rh * arr - h * aii + pltpu.roll(sc, S5_STATE, 1)))
            return tuple(out)

        zero = jnp.zeros((S5_GROUPS, 128), F32)
        lax.fori_loop(0, cn, step, ((zero, zero), (zero, zero)), unroll=4)

    return pl.pallas_call(
        body, name="s5_scan",
        out_shape=jax.ShapeDtypeStruct(sloc.shape, F32),
        compiler_params=_params(),
    )(sloc, a16)


def _s5_scan_bwd(e, hs, a16, ncc):
    cn = e.shape[1]

    def body(e_ref, h_ref, a_ref, ds_ref, da_ref):
        forms = [_s5_a_forms(a_ref[dr]) for dr in range(2)]
        low = _iota2((S5_GROUPS, 128), 1) < S5_STATE

        def step(s, carry):
            out = []
            r = cn - 1 - s
            for dr in range(2):
                arr, aii = forms[dr]
                g, rg, da = carry[dr]
                c = r if dr == 0 else jnp.where(r < ncc, ncc - 1 - r, cn - 1 - (r - ncc))
                ds_ref[dr, c] = g
                h = h_ref[dr, c]
                rh = pltpu.roll(h, S5_STATE, 1)
                da = da + jnp.where(low, g * h + rg * rh, g * rh - rg * h)
                ec = e_ref[dr, c]
                out.append((ec + g * arr - rg * aii, pltpu.roll(ec, S5_STATE, 1) + rg * arr + g * aii, da))
            return tuple(out)

        zero = jnp.zeros((S5_GROUPS, 128), F32)
        res = lax.fori_loop(0, cn, step, ((zero, zero, zero), (zero, zero, zero)), unroll=4)
        da_ref[0] = res[0][2]
        da_ref[1] = res[1][2]

    return pl.pallas_call(
        body, name="s5_scan_bwd",
        out_shape=[jax.ShapeDtypeStruct(e.shape, F32), jax.ShapeDtypeStruct((2, S5_GROUPS, 128), F32)],
        compiler_params=_params(),
    )(e, hs, a16)


def _s5_bwd_kb(p3, dy3):
    cn = p3.shape[0]
    half = S5_T // 2

    def body(u_ref, d_ref, o_ref):
        q = pl.program_id(1)

        @pl.when(q == 0)
        def _():
            o_ref[...] = jnp.zeros_like(o_ref)

        dk = _dot_tn(_s5_ucat(u_ref), _s5_ucat(d_ref, 0, half))
        for j in range(S5_T):
            for i in range(half):
                o_ref[0, half * q + i - j + (S5_T - 1)] += dk[128 * j:128 * (j + 1), 128 * i:128 * (i + 1)]

    return pl.pallas_call(
        body, name="s5_bwd_kb", grid=(S5_NB, 2),
        in_specs=[pl.BlockSpec((cn, S5_T, 128), lambda b, q: (0, 0, b)),
                  pl.BlockSpec((cn, half, 128), lambda b, q: (0, q, b))],
        out_specs=pl.BlockSpec((1, 2 * S5_T - 1, 128, 128), lambda b, q: (b, 0, 0, 0)),
        out_shape=jax.ShapeDtypeStruct((S5_NB, 2 * S5_T - 1, 128, 128), F32),
        compiler_params=_params(("parallel", "arbitrary")),
    )(p3, dy3)


def _s5_bwd_w(u3, st, name):
    cn = u3.shape[0]

    def body(u_ref, s_ref, w_ref):
        dw = _dot_tn(_s5_ucat(u_ref), _s5_get_groups(s_ref, 0))
        for j in range(S5_T):
            w_ref[0, 0, j] = _s5_contract(dw[128 * j:128 * (j + 1), :])

    return pl.pallas_call(
        body, name=name, grid=(S5_NB, 2),
        in_specs=[pl.BlockSpec((cn, S5_T, 128), lambda b, q: (0, 0, b)),
                  pl.BlockSpec((1, cn, 8, 128), lambda b, q: (q, 0, b, 0))],
        out_specs=pl.BlockSpec((1, 1, S5_T, 128, 128), lambda b, q: (b, q, 0, 0, 0)),
        out_shape=jax.ShapeDtypeStruct((S5_NB, 2, S5_T, 128, 128), F32),
        compiler_params=_params(("parallel", "parallel")),
    )(u3, st)


def _s5_glu(y_all, w_glu_b, b_glu, nct):
    la = y_all.shape[0]
    tm = TOK_TILE
    l = la - nct * tm

    def body(y_ref, w_ref, b_ref, o_ref):
        yg = _gelu(y_ref[...])
        z = _dot(yg.astype(BF16), w_ref[...]) + b_ref[...]
        o_ref[...] = (yg * _sigmoid(z)).astype(BF16)

    return pl.pallas_call(
        body, name="s5_glu", grid=(l // tm,),
        in_specs=[pl.BlockSpec((tm, S5_WIDTH), lambda i: (i + nct, 0)),
                  _full((S5_WIDTH, S5_WIDTH)), _full((1, S5_WIDTH))],
        out_specs=pl.BlockSpec((tm, S5_WIDTH), lambda i: (i, 0)),
        out_shape=jax.ShapeDtypeStruct((l, S5_WIDTH), BF16),
        compiler_params=_params(("parallel",)),
    )(y_all, w_glu_b, b_glu)


def _s5_glu_bwd(y_all, dmix, w_glu_b, b_glu, nct):
    la = y_all.shape[0]
    tm = TOK_TILE

    def body(y_ref, d_ref, w_ref, b_ref, dy_ref, gw_ref, gb_ref):
        i = pl.program_id(0)

        @pl.when(i == 0)
        def _():
            gw_ref[...] = jnp.zeros_like(gw_ref)
            gb_ref[...] = jnp.zeros_like(gb_ref)

        @pl.when(i < nct)
        def _():
            dy_ref[...] = jnp.zeros_like(dy_ref)

        @pl.when(i >= nct)
        def _():
            y = y_ref[...]
            yg, gelu_vjp = jax.vjp(_gelu, y)
            ygb = yg.astype(BF16)
            sg = _sigmoid(_dot(ygb, w_ref[...]) + b_ref[...])
            ds = d_ref[...]
            dz = ds * yg * sg * (1.0 - sg)
            dzb = dz.astype(BF16)
            dyg = ds * sg + _dot_nt(dzb, w_ref[...])
            dy_ref[...] = gelu_vjp(dyg)[0]
            gw_ref[...] += _dot_tn(ygb, dzb)
            gb_ref[...] += jnp.sum(dz, axis=0, keepdims=True)

    return pl.pallas_call(
        body, name="s5_glu_bwd", grid=(la // tm,),
        in_specs=[pl.BlockSpec((tm, S5_WIDTH), lambda i: (i, 0)),
                  pl.BlockSpec((tm, S5_WIDTH), lambda i: (jnp.maximum(i - nct, 0), 0)),
                  _full((S5_WIDTH, S5_WIDTH)), _full((1, S5_WIDTH))],
        out_specs=[pl.BlockSpec((tm, S5_WIDTH), lambda i: (i, 0)), _full((S5_WIDTH, S5_WIDTH)),
                   _full((1, S5_WIDTH))],
        out_shape=[jax.ShapeDtypeStruct((la, S5_WIDTH), F32), jax.ShapeDtypeStruct((S5_WIDTH, S5_WIDTH), F32),
                   jax.ShapeDtypeStruct((1, S5_WIDTH), F32)],
        compiler_params=_params(("arbitrary",)),
    )(y_all, dmix, w_glu_b, b_glu)


K_SCALE = RET_DH ** -0.5
Q_COL, K_COL, V_COL, G_COL = 4, 8, 12, 16


def _ret_chunk_of(step, ncc, nch, rev):
    if not rev:
        return step
    return jnp.where(step < ncc, ncc - 1 - step, nch - 1 - (step - ncc))


def _ret_decay(ld, rev):
    c = _iota2((RET_CHUNK, RET_CHUNK), 0).astype(F32)
    m = _iota2((RET_CHUNK, RET_CHUNK), 1).astype(F32)
    diff = (m - c) if rev else (c - m)
    keep = (diff > 0) if rev else (diff >= 0)
    expo = jnp.maximum(diff, 0.0)
    dm = jnp.where(keep, jnp.exp(ld * expo), 0.0)
    xi_e = (RET_CHUNK - c) if rev else (c + 1.0)
    zeta_e = c if rev else (RET_CHUNK - 1.0 - c)
    return dm, expo, jnp.exp(ld * xi_e), xi_e, jnp.exp(ld * zeta_e), zeta_e


RET_TABLES = 7


def _ret_tables(ld2):
    def body(ld_ref, t_ref):
        dr, h = pl.program_id(0), pl.program_id(1)
        ldh = ld_ref[dr, h]
        for rev in (False, True):
            @pl.when(dr == int(rev))
            def _(rev=rev):
                dm, expo, xi, xi_e, zeta, zeta_e = _ret_decay(ldh, rev)
                t_ref[0, 0, 0] = dm
                t_ref[0, 0, 1] = dm * expo
                t_ref[0, 0, 2] = xi
                t_ref[0, 0, 3] = xi * xi_e
                t_ref[0, 0, 4] = zeta
                t_ref[0, 0, 5] = zeta * zeta_e
                t_ref[0, 0, 6] = jnp.zeros_like(dm) + jnp.exp(ldh * RET_CHUNK)

    return pl.pallas_call(
        body, name="ret_tables", grid=(2, RET_HEADS),
        in_specs=[pl.BlockSpec(memory_space=pltpu.SMEM)],
        out_specs=pl.BlockSpec((1, 1, RET_TABLES, RET_CHUNK, RET_CHUNK), lambda d, h: (d, h, 0, 0, 0)),
        out_shape=jax.ShapeDtypeStruct((2, RET_HEADS, RET_TABLES, RET_CHUNK, RET_CHUNK), F32),
        compiler_params=_params(("parallel", "parallel")),
    )(ld2)


def _ret_specs(nch, ncc, rev, step_of):
    chunk = lambda n: _ret_chunk_of(step_of(n), ncc, nch, rev)
    cols = [pl.BlockSpec((RET_CHUNK, RET_WIDTH), functools.partial(lambda n, cb: (chunk(n), cb), cb=cb))
            for cb in (1, 2, 3)]
    tab = pl.BlockSpec((RET_CHUNK, RET_DH), lambda n: (chunk(n), 0))
    return cols + [tab, tab], pl.BlockSpec((RET_CHUNK, RET_WIDTH), lambda n: (chunk(n), 0))


def _ret_scan(p_all, cos_t, sin_t, tabs, ncc, placed, kinds):
    la = p_all.shape[0]
    nch = la // RET_CHUNK
    n = len(placed)
    shard_shapes = _gather_shard_shapes(placed, kinds)

    def body(t_ref, qf, kf, vf, cf, sf, qb, kb, vb, cb, sb, *rest):
        of_ref, ob_ref, ssf_ref, ssb_ref = rest[n:n + 4]
        s_scr, send_sems, recv_sems = rest[2 * n + 4:]
        step = pl.program_id(0)

        @pl.when(step == 0)
        def _():
            s_scr[...] = jnp.zeros_like(s_scr)
            for cp in _gather_chip_copies(rest[n + 4:2 * n + 4], kinds, shard_shapes, send_sems, recv_sems, False)[0]:
                cp.start()

        @pl.when(step == nch - 1)
        def _():
            sends, arrivals = _gather_chip_copies(rest[n + 4:2 * n + 4], kinds, shard_shapes, send_sems, recv_sems)
            for cp in arrivals:
                cp.wait_recv()
            for cp in sends:
                cp.wait_send()

        for dr, (q_ref, k_ref, v_ref, c_ref, n_ref, o_ref, ss_ref) in enumerate(
                ((qf, kf, vf, cf, sf, of_ref, ssf_ref), (qb, kb, vb, cb, sb, ob_ref, ssb_ref))):
            cs, sn = c_ref[...], n_ref[...]
            for h in range(RET_HEADS):
                sl = slice(RET_DH * h, RET_DH * (h + 1))
                dm, xi, zeta = t_ref[dr, h, 0], t_ref[dr, h, 2], t_ref[dr, h, 4]
                q = _rope(q_ref[:, sl], cs, sn)
                k = _rope(k_ref[:, sl] * K_SCALE, cs, sn)
                vh = v_ref[:, sl].astype(BF16)
                s = s_scr[dr, h]
                ss_ref[0, h] = s
                sc = (_dot_nt(q.astype(BF16), k.astype(BF16)) * dm).astype(BF16)
                o_ref[:, sl] = _dot(sc, vh) + _dot((q * xi).astype(BF16), s.astype(BF16))
                s_scr[dr, h] = t_ref[dr, h, 6] * s + _dot_tn((k * zeta).astype(BF16), vh)

    in_f, out_f = _ret_specs(nch, ncc, False, lambda n: n)
    in_b, out_b = _ret_specs(nch, ncc, True, lambda n: n)
    ss_spec = pl.BlockSpec((1, RET_HEADS, RET_DH, RET_DH), lambda n: (n, 0, 0, 0))
    o_shape = jax.ShapeDtypeStruct((la, RET_WIDTH), F32)
    ss_shape = jax.ShapeDtypeStruct((nch, RET_HEADS, RET_DH, RET_DH), F32)
    return pl.pallas_call(
        body, name="ret_scan", grid=(nch,),
        in_specs=[_full(tabs.shape)] + in_f + in_b + [ANY] * n,
        out_specs=[out_f, out_b, ss_spec, ss_spec] + [ANY] * n,
        out_shape=[o_shape, o_shape, ss_shape, ss_shape] + [jax.ShapeDtypeStruct(p.shape, p.dtype) for p in placed],
        input_output_aliases={11 + a: 4 + a for a in range(n)},
        scratch_shapes=[pltpu.VMEM((2, RET_HEADS, RET_DH, RET_DH), F32),
                        pltpu.SemaphoreType.DMA((n, 3)), pltpu.SemaphoreType.DMA((n, 3))],
        compiler_params=_params(("arbitrary",)),
    )(tabs, p_all, p_all, p_all, cos_t, sin_t, p_all, p_all, p_all, cos_t, sin_t, *placed)


def _ret_scan_bwd(p_all, cos_t, sin_t, tabs, ssf, ssb, dy_all, ncc):
    la = p_all.shape[0]
    nch = la // RET_CHUNK

    def body(t_ref, qf, kf, vf, cf, sf, dof, ssf_ref, qb, kb, vb, cb, sb, dob_, ssb_ref,
             dqf, dkf, dvf, dqb, dkb, dvb, dld_ref, ds_scr):
        @pl.when(pl.program_id(0) == 0)
        def _():
            ds_scr[...] = jnp.zeros_like(ds_scr)
            dld_ref[...] = jnp.zeros_like(dld_ref)

        for dr, (q_ref, k_ref, v_ref, c_ref, n_ref, do_ref, ss_ref, dq_ref, dk_ref, dv_ref) in enumerate(
                ((qf, kf, vf, cf, sf, dof, ssf_ref, dqf, dkf, dvf), (qb, kb, vb, cb, sb, dob_, ssb_ref, dqb, dkb, dvb))):
            cs, sn = c_ref[...], n_ref[...]
            for h in range(RET_HEADS):
                sl = slice(RET_DH * h, RET_DH * (h + 1))
                dm, dm_d, xi, xi_d, zeta, zeta_d, gc = [t_ref[dr, h, t] for t in range(RET_TABLES)]
                q = _rope(q_ref[:, sl], cs, sn)
                k = _rope(k_ref[:, sl] * K_SCALE, cs, sn)
                q16, k16, v16 = q.astype(BF16), k.astype(BF16), v_ref[:, sl].astype(BF16)
                s = ss_ref[0, h]
                s16 = s.astype(BF16)
                ds_in = ds_scr[dr, h]
                ds16 = ds_in.astype(BF16)
                do16 = do_ref[:, sl].astype(BF16)
                qk = _dot_nt(q16, k16)
                dsv = _dot_nt(do16, v16)
                dsc = (dsv * dm).astype(BF16)
                sc16 = (qk * dm).astype(BF16)
                dos = _dot_nt(do16, s16)
                vds = _dot_nt(v16, ds16)
                dq_ref[:, sl] = _dot(dsc, k16) + dos * xi
                dk_ref[:, sl] = _dot_tn(dsc, q16) + vds * zeta
                dv_ref[:, sl] = _dot_tn(sc16, do16) + _dot((k * zeta).astype(BF16), ds16)
                ds_scr[dr, h] = _dot_tn((q * xi).astype(BF16), do16) + gc * ds_in
                dld = jnp.sum(dsv * qk * dm_d + q * dos * xi_d + k * vds * zeta_d + RET_CHUNK * gc * s * ds_in)
                dld_ref[dr, h] += dld

    back = lambda n: nch - 1 - n
    in_f, out_f = _ret_specs(nch, ncc, False, back)
    in_b, out_b = _ret_specs(nch, ncc, True, back)
    ss_spec = pl.BlockSpec((1, RET_HEADS, RET_DH, RET_DH), lambda n: (nch - 1 - n, 0, 0, 0))
    shp = jax.ShapeDtypeStruct((la, RET_WIDTH), F32)
    return pl.pallas_call(
        body, name="ret_scan_bwd", grid=(nch,),
        in_specs=[_full(tabs.shape)] + in_f + [out_f, ss_spec] + in_b + [out_b, ss_spec],
        out_specs=[out_f, out_f, out_f, out_b, out_b, out_b, _full((2, RET_HEADS, 8, 128))],
        out_shape=[shp] * 6 + [jax.ShapeDtypeStruct((2, RET_HEADS, 8, 128), F32)],
        scratch_shapes=[pltpu.VMEM((2, RET_HEADS, RET_DH, RET_DH), F32)],
        compiler_params=_params(("arbitrary",)),
    )(tabs, p_all, p_all, p_all, cos_t, sin_t, dy_all, ssf, p_all, p_all, p_all, cos_t, sin_t, dy_all, ssb)


def _ret_gate(of, ob, p_all, nct):
    la = of.shape[0]
    tm = TOK_TILE
    l = la - nct * tm

    def body(of_ref, ob_ref, g_ref, r_ref, y_ref):
        y = of_ref[...] + ob_ref[...]
        y_ref[...] = y
        for h in range(RET_HEADS):
            sl = slice(RET_DH * h, RET_DH * (h + 1))
            r_ref[:, sl] = _head_norm_gate(y[:, sl], g_ref[:, sl]).astype(BF16)

    row = pl.BlockSpec((tm, RET_WIDTH), lambda i: (i + nct, 0))
    out = pl.BlockSpec((tm, RET_WIDTH), lambda i: (i, 0))
    return pl.pallas_call(
        body, name="ret_gate", grid=(l // tm,),
        in_specs=[row, row, pl.BlockSpec((tm, RET_WIDTH), lambda i: (i + nct, G_COL // 4))],
        out_specs=[out, out],
        out_shape=[jax.ShapeDtypeStruct((l, RET_WIDTH), BF16), jax.ShapeDtypeStruct((l, RET_WIDTH), F32)],
        compiler_params=_params(("parallel",)),
    )(of, ob, p_all)


def _ret_gate_bwd(y_ret, p_all, dmix, nct):
    la = p_all.shape[0]
    tm = TOK_TILE

    def body(y_ref, g_ref, d_ref, dy_ref, dg_ref):
        i = pl.program_id(0)

        @pl.when(i < nct)
        def _():
            dy_ref[...] = jnp.zeros_like(dy_ref)
            dg_ref[...] = jnp.zeros_like(dg_ref)

        @pl.when(i >= nct)
        def _():
            for h in range(RET_HEADS):
                sl = slice(RET_DH * h, RET_DH * (h + 1))
                _, vjp = jax.vjp(_head_norm_gate, y_ref[:, sl], g_ref[:, sl])
                dy, dg = vjp(d_ref[:, sl])
                dy_ref[:, sl] = dy
                dg_ref[:, sl] = dg

    xrow = lambda cb: pl.BlockSpec((tm, RET_WIDTH), lambda i: (jnp.maximum(i - nct, 0), cb))
    out = pl.BlockSpec((tm, RET_WIDTH), lambda i: (i, 0))
    shp = jax.ShapeDtypeStruct((la, RET_WIDTH), F32)
    return pl.pallas_call(
        body, name="ret_gate_bwd", grid=(la // tm,),
        in_specs=[xrow(0), pl.BlockSpec((tm, RET_WIDTH), lambda i: (i, G_COL // 4)), xrow(1)],
        out_specs=[out, out], out_shape=[shp, shp],
        compiler_params=_params(("parallel",)),
    )(y_ret, p_all, dmix)


def _in_bwd(dqf, dkf, dvf, dqb, dkb, dvb, du, dg, cos_t, sin_t, w_in_b, x, ctx, n1w, mod4, dx1):
    l, lc = x.shape[0], ctx.shape[0]
    la = l + lc
    tm = TOK_TILE
    nct = lc // tm

    def body(dqf_ref, dkf_ref, dvf_ref, dqb_ref, dkb_ref, dvb_ref, du_ref, dg_ref, cos_ref, sin_ref,
             w_ref, x_ref, c_ref, nw_ref, mod_ref, dx1_ref, dp_ref, gx_ref, acc_ref):
        i = pl.program_id(0)
        is_ctx = i < nct

        @pl.when(i == 0)
        def _():
            acc_ref[...] = jnp.zeros_like(acc_ref)

        cs, sn = cos_ref[...], sin_ref[...]
        dp_ref[:, 0:S5_WIDTH] = du_ref[...].astype(BF16)
        for h in range(RET_HEADS):
            sl = slice(RET_DH * h, RET_DH * (h + 1))
            dq = _rope_t(dqf_ref[:, sl] + dqb_ref[:, sl], cs, sn)
            dk = _rope_t(dkf_ref[:, sl] + dkb_ref[:, sl], cs, sn) * K_SCALE
            dp_ref[:, 128 * (Q_COL + h):128 * (Q_COL + h + 1)] = dq.astype(BF16)
            dp_ref[:, 128 * (K_COL + h):128 * (K_COL + h + 1)] = dk.astype(BF16)
        dp_ref[:, 128 * V_COL:128 * G_COL] = (dvf_ref[...] + dvb_ref[...]).astype(BF16)
        dp_ref[:, 128 * G_COL:IN_COLS] = dg_ref[...].astype(BF16)

        dh1 = _dot_nt(dp_ref[...], w_ref[...])
        xt = jnp.where(is_ctx, c_ref[...], x_ref[...])
        sh = jnp.where(is_ctx, mod_ref[0:1, :], mod_ref[2:3, :])
        sc = jnp.where(is_ctx, mod_ref[1:2, :], mod_ref[3:4, :])
        _, vjp = jax.vjp(_rms_mod, xt, nw_ref[...], sh, sc)
        dx, dnw, dsh, dsc = vjp(dh1)
        gx_ref[...] = dx + dx1_ref[...]
        cf = jnp.where(is_ctx, 1.0, 0.0)
        acc_ref[0:1, :] += dnw
        acc_ref[1:2, :] += cf * dsh
        acc_ref[2:3, :] += cf * dsc
        acc_ref[3:4, :] += (1.0 - cf) * dsh
        acc_ref[4:5, :] += (1.0 - cf) * dsc

    row = pl.BlockSpec((tm, RET_WIDTH), lambda i: (i, 0))
    tab = pl.BlockSpec((tm, RET_DH), lambda i: (i, 0))
    xrow = pl.BlockSpec((tm, D_MODEL), lambda i: (jnp.maximum(i - nct, 0), 0))
    return pl.pallas_call(
        body, name="in_bwd", grid=(la // tm,),
        in_specs=[row] * 8 + [tab, tab, _full((D_MODEL, IN_COLS)), xrow,
                              pl.BlockSpec((tm, D_MODEL), lambda i: (jnp.minimum(i, nct - 1), 0)),
                              _full((1, D_MODEL)), _full((4, D_MODEL)), xrow],
        out_specs=[pl.BlockSpec((tm, IN_COLS), lambda i: (i, 0)), xrow, _full((8, D_MODEL))],
        out_shape=[jax.ShapeDtypeStruct((la, IN_COLS), BF16), jax.ShapeDtypeStruct((l, D_MODEL), F32),
                   jax.ShapeDtypeStruct((8, D_MODEL), F32)],
        compiler_params=_params(("arbitrary",)),
    )(dqf, dkf, dvf, dqb, dkb, dvb, du, dg, cos_t, sin_t, w_in_b, x, ctx, n1w, mod4, dx1)


def _outproj_up(x, s5x, retx, w_out_b, mod3, n2w, w_up_b):
    l = x.shape[0]
    tm = TOK_TILE

    def body(x_ref, s_ref, r_ref, wo_ref, mod_ref, nw_ref, wu_ref, x1_ref, mix_ref, h2_ref, up_ref):
        mix = _dot(s_ref[...], wo_ref[0:S5_WIDTH, :]) + _dot(r_ref[...], wo_ref[S5_WIDTH:D_MODEL, :])
        mix_ref[...] = mix
        x1 = x_ref[...] + mod_ref[0:1, :] * mix
        x1_ref[...] = x1
        h2 = _rms_mod(x1, nw_ref[...], mod_ref[1:2, :], mod_ref[2:3, :]).astype(BF16)
        h2_ref[...] = h2
        up_ref[...] = _dot(h2, wu_ref[...])

    row = lambda w: pl.BlockSpec((tm, w), lambda i: (i, 0))
    return pl.pallas_call(
        body, name="outproj_up", grid=(l // tm,),
        in_specs=[row(D_MODEL), row(S5_WIDTH), row(RET_WIDTH), _full((D_MODEL, D_MODEL)), _full((3, D_MODEL)),
                  _full((1, D_MODEL)), _full((D_MODEL, 2 * D_FF))],
        out_specs=[row(D_MODEL), row(D_MODEL), row(D_MODEL), row(2 * D_FF)],
        out_shape=[jax.ShapeDtypeStruct((l, D_MODEL), F32), jax.ShapeDtypeStruct((l, D_MODEL), F32),
                   jax.ShapeDtypeStruct((l, D_MODEL), BF16), jax.ShapeDtypeStruct((l, 2 * D_FF), F32)],
        compiler_params=_params(("parallel",)),
    )(x, s5x, retx, w_out_b, mod3, n2w, w_up_b)


HALO = 8


def _conv_taps(g, prev_row, next_row):
    t = g.shape[0]
    r = _iota2(g.shape, 0)
    gprev = jnp.where(r == 0, prev_row, pltpu.roll(g, 1, 0))
    gnext = jnp.where(r == t - 1, next_row, pltpu.roll(g, t - 1, 0))
    return gprev, gnext


def _ffn_loss(up, x1, conv_w, conv_b, w_down_b, gate, fnw, tgt):
    l = x1.shape[0]
    tm = TOK_TILE
    nt = l // tm
    hb = tm // HALO

    def body(up_a, up_g, hp_ref, hn_ref, x1_ref, cw_ref, cb_ref, wd_ref, gate_ref, fn_ref, tgt_ref,
             act_ref, dx2_ref, ddn_ref, dact_ref, acc_ref):
        i = pl.program_id(0)

        @pl.when(i == 0)
        def _():
            acc_ref[...] = jnp.zeros_like(acc_ref)

        g = up_g[...]
        prev_row = jnp.where(i == 0, 0.0, hp_ref[HALO - 1:HALO, :])
        next_row = jnp.where(i == nt - 1, 0.0, hn_ref[0:1, :])
        gprev, gnext = _conv_taps(g, prev_row, next_row)
        gc = cb_ref[...] + gprev * cw_ref[0:1, :] + g * cw_ref[1:2, :] + gnext * cw_ref[2:3, :]
        act = (_gelu(gc) * up_a[...]).astype(BF16)
        act_ref[...] = act
        dn = _dot(act, wd_ref[...])
        x2 = x1_ref[...] + gate_ref[...] * dn
        y, vjp = jax.vjp(_rms, x2, fn_ref[...])
        err = y - tgt_ref[...]
        dx2, dfn = vjp(err * (1.0 / D_MODEL))
        dx2_ref[...] = dx2
        ddn = (dx2 * gate_ref[...]).astype(BF16)
        ddn_ref[...] = ddn
        dact_ref[...] = _dot_nt(ddn, wd_ref[...])
        acc_ref[0:1, :] += dfn
        acc_ref[1:2, :] += jnp.sum(dx2 * dn, axis=0, keepdims=True)
        acc_ref[2:3, :] += (0.5 / D_MODEL) * jnp.sum(err * err)

    row = lambda w: pl.BlockSpec((tm, w), lambda i: (i, 0))
    last = l // HALO - 1
    return pl.pallas_call(
        body, name="ffn_loss", grid=(nt,),
        in_specs=[pl.BlockSpec((tm, D_FF), lambda i: (i, 0)), pl.BlockSpec((tm, D_FF), lambda i: (i, 1)),
                  pl.BlockSpec((HALO, D_FF), lambda i: (jnp.maximum(i * hb - 1, 0), 1)),
                  pl.BlockSpec((HALO, D_FF), lambda i: (jnp.minimum((i + 1) * hb, last), 1)),
                  row(D_MODEL), _full((3, D_FF)), _full((1, D_FF)), _full((D_FF, D_MODEL)),
                  _full((1, D_MODEL)), _full((1, D_MODEL)), row(D_MODEL)],
        out_specs=[row(D_FF), row(D_MODEL), row(D_MODEL), row(D_FF), _full((8, D_MODEL))],
        out_shape=[jax.ShapeDtypeStruct((l, D_FF), BF16), jax.ShapeDtypeStruct((l, D_MODEL), F32),
                   jax.ShapeDtypeStruct((l, D_MODEL), BF16), jax.ShapeDtypeStruct((l, D_FF), F32),
                   jax.ShapeDtypeStruct((8, D_MODEL), F32)],
        compiler_params=_params(("arbitrary",)),
    )(up, up, up, up, x1, conv_w, conv_b, w_down_b, gate, fnw, tgt)


def _convglu_bwd(up, dact, conv_w, conv_b):
    l = up.shape[0]
    tm = 128
    nt = l // tm
    hb = tm // HALO
    te = tm + 2 * HALO

    def body(a_ref, ap_ref, an_ref, g_ref, gp_ref, gn_ref, d_ref, dp_ref, dn_ref, cw_ref, cb_ref,
             dup_ref, acc_ref):
        i = pl.program_id(0)

        @pl.when(i == 0)
        def _():
            acc_ref[...] = jnp.zeros_like(acc_ref)

        row = _iota2((te, D_FF), 0) + (i * tm - HALO)
        valid = (row >= 0) & (row < l)

        def ext(p, c, n):
            return jnp.where(valid, jnp.concatenate([p[...], c[...], n[...]], axis=0), 0.0)

        ae, ge, de = ext(ap_ref, a_ref, an_ref), ext(gp_ref, g_ref, gn_ref), ext(dp_ref, d_ref, dn_ref)
        gprev = pltpu.roll(ge, 1, 0)
        gnext = pltpu.roll(ge, te - 1, 0)
        w0, w1, w2 = cw_ref[0:1, :], cw_ref[1:2, :], cw_ref[2:3, :]
        gce = cb_ref[...] + gprev * w0 + ge * w1 + gnext * w2
        _, vjp = jax.vjp(lambda a, gc: _gelu(gc) * a, ae, gce)
        dae, dgce = vjp(de)
        dge = dgce * w1 + pltpu.roll(dgce, te - 1, 0) * w0 + pltpu.roll(dgce, 1, 0) * w2
        mid = slice(HALO, HALO + tm)
        dup_ref[:, 0:D_FF] = dae[mid].astype(BF16)
        dup_ref[:, D_FF:2 * D_FF] = dge[mid].astype(BF16)
        dgc = dgce[mid]
        acc_ref[0:1, :] += jnp.sum(dgc * gprev[mid], axis=0, keepdims=True)
        acc_ref[1:2, :] += jnp.sum(dgc * ge[mid], axis=0, keepdims=True)
        acc_ref[2:3, :] += jnp.sum(dgc * gnext[mid], axis=0, keepdims=True)
        acc_ref[3:4, :] += jnp.sum(dgc, axis=0, keepdims=True)

    last = l // HALO - 1

    def trio(cb):
        return [pl.BlockSpec((tm, D_FF), lambda i: (i, cb)),
                pl.BlockSpec((HALO, D_FF), lambda i: (jnp.maximum(i * hb - 1, 0), cb)),
                pl.BlockSpec((HALO, D_FF), lambda i: (jnp.minimum((i + 1) * hb, last), cb))]

    return pl.pallas_call(
        body, name="convglu_bwd", grid=(nt,),
        in_specs=trio(0) + trio(1) + trio(0) + [_full((3, D_FF)), _full((1, D_FF))],
        out_specs=[pl.BlockSpec((tm, 2 * D_FF), lambda i: (i, 0)), _full((8, D_FF))],
        out_shape=[jax.ShapeDtypeStruct((l, 2 * D_FF), BF16), jax.ShapeDtypeStruct((8, D_FF), F32)],
        compiler_params=_params(("arbitrary",)),
    )(up, up, up, up, up, up, dact, dact, dact, conv_w, conv_b)


def _up_bwd(dup, w_up_b, w_out_b, x1, dx2, mix, mod3, n2w, pairs, kinds):
    l = x1.shape[0]
    tm = TOK_TILE
    nt = l // tm
    n = len(pairs)
    shapes = _rs_slot_shapes(pairs, kinds)

    def body(dup_ref, wu_ref, wo_ref, x1_ref, dx2_ref, mix_ref, mod_ref, nw_ref, *rest):
        dx1_ref, dmixb_ref, dmix_ref, acc_ref = rest[n:n + 4]
        exchange = functools.partial(_rs_chip_copies, rest[:n], rest[n + 4:2 * n + 4], kinds, shapes, *rest[2 * n + 4:])
        step = pl.program_id(0)

        @pl.when(step == 0)
        def _():
            acc_ref[...] = jnp.zeros_like(acc_ref)
            for cp in exchange(with_arrivals=False)[0]:
                cp.start()

        @pl.when(step == nt - 1)
        def _():
            sends, arrivals = exchange()
            for cp in arrivals:
                cp.wait_recv()
            for cp in sends:
                cp.wait_send()

        dh2 = _dot_nt(dup_ref[...], wu_ref[...])
        _, vjp = jax.vjp(_rms_mod, x1_ref[...], nw_ref[...], mod_ref[1:2, :], mod_ref[2:3, :])
        dx, dnw, dsh, dsc = vjp(dh2)
        dx1 = dx + dx2_ref[...]
        dx1_ref[...] = dx1
        dmixb = (dx1 * mod_ref[0:1, :]).astype(BF16)
        dmixb_ref[...] = dmixb
        dmix_ref[...] = _dot_nt(dmixb, wo_ref[...])
        acc_ref[0:1, :] += dnw
        acc_ref[1:2, :] += jnp.sum(dx1 * mix_ref[...], axis=0, keepdims=True)
        acc_ref[2:3, :] += dsh
        acc_ref[3:4, :] += dsc

    row = pl.BlockSpec((tm, D_MODEL), lambda i: (i, 0))
    return pl.pallas_call(
        body, name="up_bwd", grid=(nt,),
        in_specs=[pl.BlockSpec((tm, 2 * D_FF), lambda i: (i, 0)), _full((D_MODEL, 2 * D_FF)),
                  _full((D_MODEL, D_MODEL)), row, row, row, _full((3, D_MODEL)), _full((1, D_MODEL))] + [ANY] * n,
        out_specs=[row, row, row, _full((8, D_MODEL))] + [ANY] * n,
        out_shape=[jax.ShapeDtypeStruct((l, D_MODEL), F32), jax.ShapeDtypeStruct((l, D_MODEL), BF16),
                   jax.ShapeDtypeStruct((l, D_MODEL), F32), jax.ShapeDtypeStruct((8, D_MODEL), F32)]
        + [jax.ShapeDtypeStruct((4,) + s, p.dtype) for s, p in zip(shapes, pairs)],
        scratch_shapes=[pltpu.SemaphoreType.DMA((n, 3)), pltpu.SemaphoreType.DMA((n, 3))],
        compiler_params=_params(("arbitrary",)),
    )(dup, w_up_b, w_out_b, x1, dx2, mix, mod3, n2w, *pairs)


MOD_ROWS = 16
MOD_COLS = 6 * D_MODEL // 4


def _mod_fwd(c_all, c_ctx, w_mod_b, b_loc):
    def body(c_ref, cc_ref, w_ref, b_ref, m_ref, s_ref):
        cond = jnp.concatenate([c_ref[...], jnp.broadcast_to(cc_ref[...], (8, D_MODEL))], axis=0)
        s = _silu(cond).astype(BF16)
        s_ref[...] = s
        m_ref[...] = _dot(s, w_ref[...]) + b_ref[...]

    return pl.pallas_call(
        body, name="mod_fwd",
        out_shape=[jax.ShapeDtypeStruct((MOD_ROWS, MOD_COLS), F32), jax.ShapeDtypeStruct((MOD_ROWS, D_MODEL), BF16)],
        compiler_params=_params(),
    )(c_all, c_ctx, w_mod_b, b_loc)


def _mod_bwd_sum(dm_all):
    def body(d_ref, dm_ref, gb_ref):
        rows = [d_ref[k, 0:1, :] for k in range(8)]
        ctx_sum = d_ref[0, 1:2, :]
        for k in range(1, 8):
            ctx_sum = ctx_sum + d_ref[k, 1:2, :]
        gb = ctx_sum
        for k in range(8):
            gb = gb + rows[k]
        gb_ref[...] = gb
        dm_ref[...] = jnp.concatenate(rows + [ctx_sum] + [jnp.zeros((7, 6 * D_MODEL), F32)], axis=0)

    return pl.pallas_call(
        body, name="mod_bwd_sum",
        out_shape=[jax.ShapeDtypeStruct((MOD_ROWS, 6 * D_MODEL), F32), jax.ShapeDtypeStruct((1, 6 * D_MODEL), F32)],
        compiler_params=_params(),
    )(dm_all)


def _mod_bwd_w(dm_loc, s_b, c_ctx, w_mod_b):
    def body(d_ref, s_ref, cc_ref, w_ref, gw_ref, gc_ref):
        db = d_ref[...].astype(BF16)
        gw_ref[...] = _dot_tn(s_ref[...], db)
        ds = _dot_nt(db, w_ref[...])
        _, vjp = jax.vjp(_silu, cc_ref[...])
        gc_ref[...] = jnp.broadcast_to(vjp(ds[8:9, :])[0], (8, D_MODEL))

    return pl.pallas_call(
        body, name="mod_bwd_w",
        out_shape=[jax.ShapeDtypeStruct((D_MODEL, MOD_COLS), F32), jax.ShapeDtypeStruct((8, D_MODEL), F32)],
        compiler_params=_params(),
    )(dm_loc, s_b, c_ctx, w_mod_b)


def _adamw(w, g, m, v, name):
    r, c = w.shape
    tr = _pick(r, (256, 128, 64, 32, 16, 8))
    bc1 = 1.0 - ADAM_B1 ** ADAM_STEP
    bc2 = 1.0 - ADAM_B2 ** ADAM_STEP

    def body(w_ref, g_ref, m_ref, v_ref, d_ref, nm_ref, nv_ref):
        gg = g_ref[...]
        nm = ADAM_B1 * m_ref[...] + (1.0 - ADAM_B1) * gg
        nv = ADAM_B2 * v_ref[...] + (1.0 - ADAM_B2) * (gg * gg)
        nm_ref[...] = nm
        nv_ref[...] = nv
        d_ref[...] = -ADAM_LR * ((nm / bc1) / (jnp.sqrt(nv / bc2) + ADAM_EPS) + ADAM_WD * w_ref[...])

    blk = pl.BlockSpec((tr, c), lambda i: (i, 0))
    shp = jax.ShapeDtypeStruct((r, c), F32)
    return pl.pallas_call(
        body, name=name, grid=(r // tr,), in_specs=[blk] * 4, out_specs=[blk] * 3, out_shape=[shp] * 3,
        compiler_params=_params(("parallel",)),
    )(w, g, m, v)


def _sum_slots(a, name):
    n, r, c = a.shape
    tr = _pick(r, (376, 256, 208, 128, 64, 32, 16, 8))

    def body(a_ref, o_ref):
        acc = a_ref[0].astype(F32)
        for k in range(1, n):
            acc = acc + a_ref[k].astype(F32)
        o_ref[...] = acc

    return pl.pallas_call(
        body, name=name, grid=(r // tr,),
        in_specs=[pl.BlockSpec((n, tr, c), lambda i: (0, i, 0))],
        out_specs=pl.BlockSpec((tr, c), lambda i: (i, 0)),
        out_shape=jax.ShapeDtypeStruct((r, c), F32),
        compiler_params=_params(("parallel",)),
    )(a)


def _mesh_pos():
    return lax.axis_index("x"), lax.axis_index("y"), lax.axis_index("c")


def _all_gather8(v, name):
    m_per, n = v.shape

    def body(x_ref, out_ref, send_sems, recv_sems, local_sem):
        x, y, c = _mesh_pos()
        me, sibling = (x, y, c), (x, y, 1 - c)
        chips = [(1 - x, y), (x, 1 - y), (1 - x, 1 - y)]

        def rows(px, py, pc):
            return out_ref.at[pl.ds((4 * px + 2 * py + pc) * m_per, m_per), :]

        def copy(k, block, to, src=None):
            return pltpu.make_async_remote_copy(
                src_ref=rows(*block) if src is None else src, dst_ref=rows(*block),
                send_sem=send_sems.at[k], recv_sem=recv_sems.at[k], device_id=to, device_id_type=MESH_ID)

        mine = pltpu.make_async_copy(x_ref, rows(*me), local_sem)
        mine.start()
        first = [copy(0, me, sibling, src=x_ref)]
        first += [copy(1 + j, me, (*chip, c), src=x_ref) for j, chip in enumerate(chips)]
        for cp in first:
            cp.start()
        passed = [copy(4 + j, (*chip, c), sibling) for j, chip in enumerate(chips)]
        for j, chip in enumerate(chips):
            copy(1 + j, (*chip, c), me).wait_recv()
            passed[j].start()
        copy(0, sibling, me).wait_recv()
        for j, chip in enumerate(chips):
            copy(4 + j, (*chip, 1 - c), me).wait_recv()
        for cp in first + passed:
            cp.wait_send()
        mine.wait()

    return pl.pallas_call(
        body, name=name,
        out_shape=jax.ShapeDtypeStruct((8 * m_per, n), v.dtype),
        in_specs=[pl.BlockSpec(memory_space=pltpu.VMEM)],
        out_specs=pl.BlockSpec(memory_space=pltpu.VMEM),
        scratch_shapes=[pltpu.SemaphoreType.DMA((7,)), pltpu.SemaphoreType.DMA((7,)), pltpu.SemaphoreType.DMA],
        compiler_params=_params(),
    )(v)


ANY = pl.BlockSpec(memory_space=pl.ANY)
PEER_CHIPS = lambda x, y: [(x, 1 - y), (1 - x, y), (1 - x, 1 - y)]


def _shard_region(ref, kind, k, rl, cl, r0, nr, c0, nc):
    if kind == "col":
        return ref.at[pl.ds(r0, nr), pl.ds(k * cl + c0, nc)]
    return ref.at[pl.ds(k * rl + r0, nr), pl.ds(c0, nc)]


def _place_shard(w, kind, chip, name):
    rl, cl = w.shape
    tr = _pick(rl, (256, 128, 64))
    nt = rl // tr

    def body(chip_ref, w_ref, o_ref):
        o_ref[...] = w_ref[...].astype(BF16)

    o_map = (lambda i, chip_ref: (i, chip_ref[0])) if kind == "col" else (lambda i, chip_ref: (chip_ref[0] * nt + i, 0))
    return pl.pallas_call(
        body, name=name,
        grid_spec=pltpu.PrefetchScalarGridSpec(
            num_scalar_prefetch=1, grid=(nt,),
            in_specs=[pl.BlockSpec((tr, cl), lambda i, chip_ref: (i, 0))], out_specs=pl.BlockSpec((tr, cl), o_map)),
        out_shape=jax.ShapeDtypeStruct((rl, 4 * cl) if kind == "col" else (4 * rl, cl), BF16),
        compiler_params=_params(("parallel",)),
    )(chip.reshape(1), w)


def _gather_shard_shapes(placed, kinds):
    return [(p.shape[0], p.shape[1] // 4) if k == "col" else (p.shape[0] // 4, p.shape[1]) for p, k in zip(placed, kinds)]


def _gather_chip_copies(outs, kinds, shard_shapes, send_sems, recv_sems, with_arrivals=True):
    x, y, c = _mesh_pos()
    me = 2 * x + y
    sends, arrivals = [], []
    for a in range(len(outs)):
        rl, cl = shard_shapes[a]
        rh = rl // 2
        reg = functools.partial(_shard_region, outs[a], kinds[a], rl=rl, cl=cl, r0=c * rh, nr=rh, c0=0, nc=cl)
        for j, (px, py) in enumerate(PEER_CHIPS(x, y)):
            to = dict(send_sem=send_sems.at[a, j], recv_sem=recv_sems.at[a, j], device_id=(px, py, c),
                      device_id_type=MESH_ID)
            sends.append(pltpu.make_async_remote_copy(src_ref=reg(k=me), dst_ref=reg(k=me), **to))
            if with_arrivals:
                got = reg(k=2 * px + py)
                arrivals.append(pltpu.make_async_remote_copy(src_ref=got, dst_ref=got, **to))
    return sends, arrivals


def _gather_sibling_copies(outs, kinds, shard_shapes, send_sems, recv_sems):
    x, y, c = _mesh_pos()
    forwards, arrivals = [], []
    for a in range(len(outs)):
        rl, cl = shard_shapes[a]
        rh = rl // 2
        for j, (px, py) in enumerate(PEER_CHIPS(x, y)):
            to = dict(send_sem=send_sems.at[a, j], recv_sem=recv_sems.at[a, j], device_id=(x, y, 1 - c),
                      device_id_type=MESH_ID)
            reg = functools.partial(_shard_region, outs[a], kinds[a], k=2 * px + py, rl=rl, cl=cl, nr=rh, c0=0, nc=cl)
            forwards.append(pltpu.make_async_remote_copy(src_ref=reg(r0=c * rh), dst_ref=reg(r0=c * rh), **to))
            arrivals.append(pltpu.make_async_remote_copy(src_ref=reg(r0=(1 - c) * rh), dst_ref=reg(r0=(1 - c) * rh), **to))
    return forwards, arrivals


def _gather_weights(placed, kinds):
    n = len(placed)
    shard_shapes = _gather_shard_shapes(placed, kinds)

    def body(*refs):
        outs = refs[n:2 * n]
        ici_send, ici_recv, sib_send, sib_recv = refs[2 * n:]
        sends, arrivals = _gather_chip_copies(outs, kinds, shard_shapes, ici_send, ici_recv)
        for cp in sends:
            cp.start()
        forwards, from_sibling = _gather_sibling_copies(outs, kinds, shard_shapes, sib_send, sib_recv)
        for cp, fwd in zip(arrivals, forwards):
            cp.wait_recv()
            fwd.start()
        for cp in from_sibling:
            cp.wait_recv()
        for cp in sends + forwards:
            cp.wait_send()

    return pl.pallas_call(
        body, name="gather_weights",
        out_shape=[jax.ShapeDtypeStruct(p.shape, p.dtype) for p in placed],
        in_specs=[ANY] * n, out_specs=[ANY] * n, input_output_aliases={a: a for a in range(n)},
        scratch_shapes=[pltpu.SemaphoreType.DMA((n, 3))] * 4,
        compiler_params=_params(),
    )(*placed)


def _gather_sibling(placed, kinds):
    n = len(placed)
    shard_shapes = _gather_shard_shapes(placed, kinds)

    def body(*refs):
        forwards, from_sibling = _gather_sibling_copies(refs[n:2 * n], kinds, shard_shapes, *refs[2 * n:])
        for cp in forwards:
            cp.start()
        for cp in from_sibling:
            cp.wait_recv()
        for cp in forwards:
            cp.wait_send()

    return pl.pallas_call(
        body, name="gather_sibling",
        out_shape=[jax.ShapeDtypeStruct(p.shape, p.dtype) for p in placed],
        in_specs=[ANY] * n, out_specs=[ANY] * n, input_output_aliases={a: a for a in range(n)},
        scratch_shapes=[pltpu.SemaphoreType.DMA((n, 3))] * 2,
        compiler_params=_params(),
    )(*placed)


def _half(kind, r, c):
    return (r // 2, c) if kind == "col" else (r, c // 2)


def _half_of(ref, kind, which):
    r, c = ref.shape
    hr, hc = _half(kind, r, c)
    return ref.at[pl.ds(which * hr, hr), :] if kind == "col" else ref.at[:, pl.ds(which * hc, hc)]


def _rs_sibling(grads, kinds, name):
    n = len(grads)

    def body(*refs):
        srcs, dsts = refs[:n], refs[n:2 * n]
        send_sems, recv_sems = refs[2 * n:]
        x, y, c = _mesh_pos()
        cps = [pltpu.make_async_remote_copy(src_ref=_half_of(srcs[a], kinds[a], 1 - c), dst_ref=dsts[a],
                                            send_sem=send_sems.at[a], recv_sem=recv_sems.at[a],
                                            device_id=(x, y, 1 - c), device_id_type=MESH_ID) for a in range(n)]
        for cp in cps:
            cp.start()
        for cp in cps:
            cp.wait()

    return pl.pallas_call(
        body, name=name,
        out_shape=[jax.ShapeDtypeStruct(_half(k, *g.shape), g.dtype) for g, k in zip(grads, kinds)],
        in_specs=[ANY] * n, out_specs=[ANY] * n,
        scratch_shapes=[pltpu.SemaphoreType.DMA((n,)), pltpu.SemaphoreType.DMA((n,))],
        compiler_params=_params(),
    )(*grads)


def _pair_sum(gf, rv, kind, ci, name):
    r, c = rv.shape
    tr = _pick(r, (128, 64, 32, 16, 8))
    nt = r // tr

    def body(ci_ref, g_ref, r_ref, o_ref):
        o_ref[...] = (g_ref[...] + r_ref[...]).astype(BF16)

    g_map = (lambda i, ci_ref: (ci_ref[0] * nt + i, 0)) if kind == "col" else (lambda i, ci_ref: (i, ci_ref[0]))
    blk = pl.BlockSpec((tr, c), lambda i, ci_ref: (i, 0))
    return pl.pallas_call(
        body, name=name,
        grid_spec=pltpu.PrefetchScalarGridSpec(num_scalar_prefetch=1, grid=(nt,),
                                               in_specs=[pl.BlockSpec((tr, c), g_map), blk], out_specs=blk),
        out_shape=jax.ShapeDtypeStruct((r, c), BF16),
        compiler_params=_params(("parallel",)),
    )(ci.reshape(1), gf, rv)


def _rs_slot_shapes(pairs, kinds):
    return [(p.shape[0], p.shape[1] // 4) if k == "col" else (p.shape[0] // 4, p.shape[1]) for p, k in zip(pairs, kinds)]


def _rs_chip_copies(srcs, dsts, kinds, shapes, send_sems, recv_sems, with_arrivals=True):
    x, y, c = _mesh_pos()
    me = 2 * x + y
    sends, arrivals = [], []
    for a in range(len(srcs)):
        rl, cl = shapes[a]
        reg = functools.partial(_shard_region, srcs[a], kinds[a], rl=rl, cl=cl, r0=0, nr=rl, c0=0, nc=cl)
        for j, (px, py) in enumerate(PEER_CHIPS(x, y)):
            to = dict(send_sem=send_sems.at[a, j], recv_sem=recv_sems.at[a, j], device_id=(px, py, c),
                      device_id_type=MESH_ID)
            sends.append(pltpu.make_async_remote_copy(src_ref=reg(k=2 * px + py), dst_ref=dsts[a].at[me], **to))
            if with_arrivals:
                slot = dsts[a].at[2 * px + py]
                arrivals.append(pltpu.make_async_remote_copy(src_ref=slot, dst_ref=slot, **to))
    return sends, arrivals


def _rs_chips(pairs, kinds):
    n = len(pairs)
    shapes = _rs_slot_shapes(pairs, kinds)

    def body(*refs):
        sends, arrivals = _rs_chip_copies(refs[:n], refs[n:2 * n], kinds, shapes, *refs[2 * n:])
        for cp in sends:
            cp.start()
        for cp in arrivals:
            cp.wait_recv()
        for cp in sends:
            cp.wait_send()

    return pl.pallas_call(
        body, name="rs_chips",
        out_shape=[jax.ShapeDtypeStruct((4,) + s, p.dtype) for s, p in zip(shapes, pairs)],
        in_specs=[ANY] * n, out_specs=[ANY] * n,
        scratch_shapes=[pltpu.SemaphoreType.DMA((n, 3)), pltpu.SemaphoreType.DMA((n, 3))],
        compiler_params=_params(),
    )(*pairs)


def _sum_chips(pair, got, kind, pos, name):
    _, r, c = got.shape
    tr = _pick(r, (256, 128, 64, 32, 16))
    nt = r // tr

    def body(pos_ref, own_ref, g1_ref, g2_ref, g3_ref, o_ref):
        o_ref[...] = ((own_ref[...].astype(F32) + g1_ref[0].astype(F32)) + g2_ref[0].astype(F32)) + g3_ref[0].astype(F32)

    if kind == "col":
        own_map = lambda i, p: (i, p[1])
        out_map = lambda i, p: (p[0] * nt + i, 0)
        out_shape = (2 * r, c)
    else:
        own_map = lambda i, p: (p[1] * nt + i, 0)
        out_map = lambda i, p: (i, p[0])
        out_shape = (r, 2 * c)
    peer = lambda m: pl.BlockSpec((1, tr, c), lambda i, p: (p[1] ^ m, i, 0))
    return pl.pallas_call(
        body, name=name,
        grid_spec=pltpu.PrefetchScalarGridSpec(
            num_scalar_prefetch=1, grid=(nt,),
            in_specs=[pl.BlockSpec((tr, c), own_map), peer(1), peer(2), peer(3)],
            out_specs=pl.BlockSpec((tr, c), out_map)),
        out_shape=jax.ShapeDtypeStruct(out_shape, F32),
        compiler_params=_params(("parallel",)),
    )(pos, pair, got, got, got)


def _rs_back(halves, kinds):
    n = len(halves)

    def body(*refs):
        outs = refs[n:2 * n]
        send_sems, recv_sems = refs[2 * n:]
        x, y, c = _mesh_pos()
        cps = []
        for a in range(n):
            mine = _half_of(outs[a], kinds[a], c)
            cps.append(pltpu.make_async_remote_copy(src_ref=mine, dst_ref=mine, send_sem=send_sems.at[a],
                                                    recv_sem=recv_sems.at[a], device_id=(x, y, 1 - c),
                                                    device_id_type=MESH_ID))
            cps[-1].start()
        for a in range(n):
            other = _half_of(outs[a], kinds[a], 1 - c)
            pltpu.make_async_remote_copy(src_ref=other, dst_ref=other, send_sem=send_sems.at[a],
                                         recv_sem=recv_sems.at[a], device_id=(x, y, 1 - c),
                                         device_id_type=MESH_ID).wait_recv()
        for cp in cps:
            cp.wait_send()

    return pl.pallas_call(
        body, name="rs_back",
        out_shape=[jax.ShapeDtypeStruct(h.shape, h.dtype) for h in halves],
        in_specs=[ANY] * n, out_specs=[ANY] * n, input_output_aliases={a: a for a in range(n)},
        scratch_shapes=[pltpu.SemaphoreType.DMA((n,)), pltpu.SemaphoreType.DMA((n,))],
        compiler_params=_params(),
    )(*halves)


def _rope_tables(l, lc):
    rows = l // GRID_W
    row = jnp.repeat(jnp.arange(rows, dtype=F32), GRID_W)
    col = jnp.tile(jnp.arange(GRID_W, dtype=F32), rows)
    n_freq = RET_DH // 4
    inv_freq = ROPE_THETA ** (-jnp.arange(n_freq, dtype=F32) / n_freq)
    ang = jnp.concatenate([row[:, None] * inv_freq, col[:, None] * inv_freq], axis=-1)
    cos_t = jnp.repeat(jnp.cos(ang), 2, axis=-1)
    sin_t = jnp.repeat(jnp.sin(ang), 2, axis=-1) * jnp.tile(jnp.array([-1.0, 1.0], F32), RET_DH // 2)
    cos_t = jnp.concatenate([jnp.ones((lc, RET_DH), F32), cos_t], axis=0)
    sin_t = jnp.concatenate([jnp.zeros((lc, RET_DH), F32), sin_t], axis=0)
    return cos_t, sin_t


def _s5_pack(a):
    blk = lambda t: t.reshape(1, S5_NB, 128, S5_STATE)
    lre = jnp.stack([a["s5_lambda_re_f"][0], a["s5_lambda_re_b"][0]]).reshape(2, S5_NB, 8, S5_STATE)
    lim = jnp.stack([a["s5_lambda_im_f"][0], a["s5_lambda_im_b"][0]]).reshape(2, S5_NB, 8, S5_STATE)
    lst = jnp.stack([a["s5_log_step_f"][0], a["s5_log_step_b"][0]]).reshape(2, S5_NB, 8, 1)
    b_re = blk(a["s5_b_re"][0].transpose(0, 2, 1))
    b_im = blk(a["s5_b_im"][0].transpose(0, 2, 1))
    return (lre, lim, lst, b_re, b_im, blk(a["s5_c_re"][0]), blk(a["s5_c_im"][0]),
            a["s5_d"].reshape(1, S5_NB, 1, 128))


def _s5_unpack(g):
    glre, glim, glst, gbre, gbim, gcre, gcim, gd = g
    unb = lambda t: t.reshape(S5_GROUPS, S5_GROUP, S5_STATE).transpose(0, 2, 1)[None]
    return {
        "s5_lambda_re_f": glre[0].reshape(1, S5_GROUPS, S5_STATE), "s5_lambda_re_b": glre[1].reshape(1, S5_GROUPS, S5_STATE),
        "s5_lambda_im_f": glim[0].reshape(1, S5_GROUPS, S5_STATE), "s5_lambda_im_b": glim[1].reshape(1, S5_GROUPS, S5_STATE),
        "s5_log_step_f": glst[0].reshape(1, S5_GROUPS), "s5_log_step_b": glst[1].reshape(1, S5_GROUPS),
        "s5_b_re": unb(gbre), "s5_b_im": unb(gbim),
        "s5_c_re": gcre.reshape(1, S5_GROUPS, S5_GROUP, S5_STATE), "s5_c_im": gcim.reshape(1, S5_GROUPS, S5_GROUP, S5_STATE),
        "s5_d": gd.reshape(1, S5_WIDTH),
    }


def _local_step(a, wb, late, mx, mc, conv_w, ci):
    x, ctx, tgt = a["x"][0], a["ctx"][0], a["loss_target"][0]
    l, lc = x.shape[0], ctx.shape[0]
    la = l + lc
    nct, ncc, nrc, cn = lc // TOK_TILE, lc // S5_T, lc // RET_CHUNK, la // S5_T
    n1w, n2w, fnw = a["norm1_w"], a["norm2_w"], a["final_norm_w"].reshape(1, D_MODEL)
    conv_b, b_glu = a["conv_b"], a["s5_b_glu"]
    ld2 = jnp.concatenate([a["ret_log_decay_f"], a["ret_log_decay_b"]], axis=0)
    mod4 = jnp.concatenate([mc[0:2], mx[0:2]], axis=0)
    mod3 = mx[2:5]
    gate5 = mx[5:6]
    cos_t, sin_t = _rope_tables(l, lc)
    s5p = _s5_pack(a)

    p_all, h1b = _norm_inproj(x, ctx, n1w, mod4, wb["w_in"])
    p3 = p_all.reshape(cn, S5_T, IN_COLS)
    gg, xw, yw, a16 = _s5_gen(*s5p)
    sloc = _s5_to_states(p3, xw, "s5_state")
    a16s = a16.transpose(1, 0, 2, 3).reshape(2, S5_GROUPS, 128)
    hs = _s5_scan(sloc, a16s, ncc)
    y_all = _s5_from_states(p3, gg, hs, yw, False, "s5_out").reshape(la, S5_WIDTH)
    s5x = _s5_glu(y_all, wb["s5_w_glu"], b_glu, nct)
    tabs = _ret_tables(ld2)
    of, ob, ssf, ssb, *late = _ret_scan(p_all, cos_t, sin_t, tabs, nrc, late, LATE_KINDS)
    wb = {**wb, **dict(zip(LATE_NAMES, _gather_sibling(late, LATE_KINDS)))}
    retx, y_ret = _ret_gate(of, ob, p_all, nct)
    x1, mix, h2b, up = _outproj_up(x, s5x, retx, wb["w_out"], mod3, n2w, wb["w_up"])
    act, dx2, ddn, dact, acc_f = _ffn_loss(up, x1, conv_w, conv_b, wb["w_down"], gate5, fnw, tgt)

    g = {}
    g["w_down"] = _mm_tn(act, ddn, name="gw_down")
    dup, acc_c = _convglu_bwd(up, dact, conv_w, conv_b)
    g["w_up"] = _mm_tn(h2b, dup, name="gw_up")
    first = [g[n] for n in FIRST_GRADS]
    first_pairs = [_pair_sum(gf, rv, k, ci, "rs_pair_" + n)
                   for gf, rv, k, n in zip(first, _rs_sibling(first, FIRST_KINDS, "rs_sibling_first"), FIRST_KINDS, FIRST_GRADS)]
    dx1, dmixb, dmix, acc_2, *first_got = _up_bwd(dup, wb["w_up"], wb["w_out"], x1, dx2, mix, mod3, n2w,
                                                  first_pairs, FIRST_KINDS)
    g["w_out"] = jnp.concatenate([_mm_tn(s5x, dmixb, name="gw_out_s5"), _mm_tn(retx, dmixb, name="gw_out_ret")], axis=0)

    dy_s5, g["s5_w_glu"], g["s5_b_glu"] = _s5_glu_bwd(y_all, dmix, wb["s5_w_glu"], b_glu, nct)
    dy3 = dy_s5.reshape(cn, S5_T, S5_WIDTH)
    e = _s5_to_states(dy3, yw, "s5_bwd_h")
    ds, da16 = _s5_scan_bwd(e, hs, a16s, ncc)
    du = _s5_from_states(dy3, gg, ds, xw, True, "s5_bwd_u").reshape(la, S5_WIDTH)
    dkb = _s5_bwd_kb(p3, dy3)
    dwst = _s5_bwd_w(p3, ds, "s5_bwd_wst")
    dwout = _s5_bwd_w(dy3, hs, "s5_bwd_wout")
    da16p = da16.reshape(2, S5_NB, 8, 128).transpose(1, 0, 2, 3)
    g.update(_s5_unpack(_s5_gen_bwd(*s5p, dkb, dwst, dwout, da16p)))

    dy_ret, dg = _ret_gate_bwd(y_ret, p_all, dmix, nct)
    dqf, dkf, dvf, dqb, dkb_, dvb, dld = _ret_scan_bwd(p_all, cos_t, sin_t, tabs, ssf, ssb, dy_ret, nrc)
    g["ret_log_decay_f"] = dld[0, :, 0, 0].reshape(1, RET_HEADS)
    g["ret_log_decay_b"] = dld[1, :, 0, 0].reshape(1, RET_HEADS)
    dp, grad_x, acc_1 = _in_bwd(dqf, dkf, dvf, dqb, dkb_, dvb, du, dg, cos_t, sin_t, wb["w_in"], x, ctx, n1w, mod4, dx1)
    g["w_in"] = _mm_tn(h1b, dp, name="gw_in")

    g["norm1_w"], g["norm2_w"], g["final_norm_w"] = acc_1[0:1], acc_2[0:1], acc_f[0]
    g["conv_w"], g["conv_b"] = acc_c[0:3], acc_c[3:4]
    zero = jnp.zeros((1, D_MODEL), F32)
    dmx = jnp.concatenate([acc_1[3:5], acc_2[1:2], acc_2[2:4], acc_f[1:2]], axis=0)
    dmc = jnp.concatenate([acc_1[1:3], zero, zero, zero, zero], axis=0)
    return acc_f[2, 0], grad_x, g, dmx, dmc, first_pairs, first_got


WEIGHT_NAMES = ("c_ctx", "w_mod", "b_mod", "norm1_w", "w_in", "s5_lambda_re_f", "s5_lambda_im_f", "s5_log_step_f",
                "s5_lambda_re_b", "s5_lambda_im_b", "s5_log_step_b", "s5_b_re", "s5_b_im", "s5_c_re", "s5_c_im",
                "s5_d", "s5_w_glu", "s5_b_glu", "ret_log_decay_f", "ret_log_decay_b", "w_out", "norm2_w", "w_up",
                "conv_w", "conv_b", "w_down", "final_norm_w")
BIG_NAMES = ("w_in", "w_out", "w_up", "w_down", "s5_w_glu")
BIG_KINDS = ("col", "row", "col", "row", "row")
EARLY_NAMES, EARLY_KINDS = ("w_in", "s5_w_glu"), ("col", "row")
LATE_NAMES, LATE_KINDS = ("w_out", "w_up", "w_down"), ("row", "col", "row")
FIRST_GRADS, FIRST_KINDS = ("w_down", "w_up"), ("row", "col")
LAST_GRADS, LAST_KINDS = ("w_in", "w_out", "s5_w_glu"), ("col", "row", "row")
SMALL_NAMES = ("norm1_w", "norm2_w", "final_norm_w", "conv_b", "conv_w", "s5_lambda_re_f", "s5_lambda_im_f",
               "s5_log_step_f", "s5_lambda_re_b", "s5_lambda_im_b", "s5_log_step_b", "s5_b_re", "s5_b_im", "s5_c_re",
               "s5_c_im", "s5_d", "s5_b_glu", "ret_log_decay_f", "ret_log_decay_b")
ROW = 1024
N_CHIPS = 4


def _pack_rows(parts):
    flat = jnp.concatenate([p.reshape(-1) for p in parts])
    n = flat.shape[0]
    rows = -(-n // (8 * ROW)) * 8
    return jnp.pad(flat, (0, rows * ROW - n)).reshape(rows, ROW)


def _unpack_rows(packed, shapes):
    flat = packed.reshape(-1)
    out, off = [], 0
    for s in shapes:
        n = math.prod(s)
        out.append(flat[off:off + n].reshape(s))
        off += n
    return out


def _step(a):
    xi, yi, ci = _mesh_pos()
    chip = 2 * xi + yi
    dev = 2 * chip + ci

    cw_loc = a["conv_w"].reshape(-1)
    small_in = jnp.concatenate([a["c"].reshape(-1), jnp.pad(cw_loc, (0, 24 * 128 - cw_loc.shape[0]))]).reshape(32, 128)
    sg = _all_gather8(small_in, "gather_cond").reshape(8, 32, 128)
    c_all = sg[:, 0:8].reshape(8, D_MODEL)
    conv_w = sg[0::2, 8:32].reshape(N_CHIPS, -1)[:, :cw_loc.shape[0]].reshape(N_CHIPS, 3, -1)
    conv_w = conv_w.transpose(1, 0, 2).reshape(3, D_FF)

    placed = {n: _place_shard(a[n][0], k, chip, "place_" + n) for n, k in zip(BIG_NAMES, BIG_KINDS)}
    wb = dict(zip(EARLY_NAMES, _gather_weights([placed[n] for n in EARLY_NAMES], EARLY_KINDS)))
    late = [placed[n] for n in LATE_NAMES]

    w_mod_b = a["w_mod"][0].astype(BF16)
    c_ctx = a["c_ctx"].reshape(1, D_MODEL)
    b_loc = lax.dynamic_slice_in_dim(a["b_mod"], chip * MOD_COLS, MOD_COLS, 1)
    m_loc, s_b = _mod_fwd(c_all, c_ctx, w_mod_b, b_loc)
    mg = _all_gather8(m_loc, "gather_mod").reshape(8, MOD_ROWS, MOD_COLS)
    m_full = mg[0::2].transpose(1, 0, 2).reshape(MOD_ROWS, 6 * D_MODEL)
    mx = lax.dynamic_slice_in_dim(m_full, dev, 1, 0).reshape(6, D_MODEL)
    mc = m_full[8].reshape(6, D_MODEL)

    loss_part, grad_x, g, dmx, dmc, first_pairs, first_got = _local_step(a, wb, late, mx, mc, conv_w, ci)
    loss = lax.psum(loss_part, ("x", "y", "c"))

    dm_pair = jnp.concatenate([dmx.reshape(1, -1), dmc.reshape(1, -1), jnp.zeros((6, 6 * D_MODEL), F32)], axis=0)
    dm_all = _all_gather8(dm_pair, "gather_dmod").reshape(8, 8, 6 * D_MODEL)
    dm16, gb_mod = _mod_bwd_sum(dm_all)
    dm_loc = lax.dynamic_slice_in_dim(dm16, chip * MOD_COLS, MOD_COLS, 1)
    gw_mod, gcc = _mod_bwd_w(dm_loc, s_b, c_ctx, w_mod_b)

    small_parts = [g[n] for n in SMALL_NAMES] + [gcc[0]]
    small_shapes = [p.shape for p in small_parts]
    sp = _pack_rows(small_parts)
    tot = _sum_slots(_all_gather8(sp, "gather_small_grads").reshape(8, sp.shape[0], ROW), "sum_small_grads")
    small = dict(zip(SMALL_NAMES + ("c_ctx",), _unpack_rows(tot, small_shapes)))
    grads = {n: small[n].reshape(a[n].shape) for n in SMALL_NAMES if n != "conv_w"}
    grads["c_ctx"] = (0.5 * small["c_ctx"]).reshape(a["c_ctx"].shape)
    grads["conv_w"] = lax.dynamic_slice_in_dim(small["conv_w"], chip * (D_FF // N_CHIPS), D_FF // N_CHIPS, 1)[None]
    grads["b_mod"] = gb_mod
    grads["w_mod"] = gw_mod[None]

    last = [g[n] for n in LAST_GRADS]
    last_pairs = [_pair_sum(gf, rv, k, ci, "rs_pair_" + n)
                  for gf, rv, k, n in zip(last, _rs_sibling(last, LAST_KINDS, "rs_sibling_last"), LAST_KINDS, LAST_GRADS)]
    last_got = _rs_chips(last_pairs, LAST_KINDS)
    pos = jnp.stack([ci, chip])
    order = FIRST_GRADS + LAST_GRADS
    order_kinds = FIRST_KINDS + LAST_KINDS
    halves = [_sum_chips(p, t, k, pos, "rs_sum_" + n)
              for p, t, k, n in zip(first_pairs + last_pairs, list(first_got) + list(last_got), order_kinds, order)]
    for n, t in zip(order, _rs_back(halves, order_kinds)):
        grads[n] = t[None]

    delta, new_m, new_v = {}, {}, {}
    for n in BIG_NAMES + ("w_mod",):
        for dst, t in zip((delta, new_m, new_v), _adamw(a[n][0], grads[n][0], a["m_" + n][0], a["v_" + n][0], "adamw_" + n)):
            dst[n] = t[None]
    rest = [n for n in WEIGHT_NAMES if n not in BIG_NAMES and n != "w_mod"]
    shapes = [a[n].shape for n in rest]
    pr = lambda pre: _pack_rows([a[pre + n] for n in rest])
    for dst, t in zip((delta, new_m, new_v),
                      _adamw(pr(""), _pack_rows([grads[n] for n in rest]), pr("m_"), pr("v_"), "adamw_small")):
        dst.update(zip(rest, _unpack_rows(t, shapes)))

    return (loss, grad_x[None], *[grads[n] for n in WEIGHT_NAMES], *[delta[n] for n in WEIGHT_NAMES],
            *[new_m[n] for n in WEIGHT_NAMES], *[new_v[n] for n in WEIGHT_NAMES])


def kernel(x, c, ctx, c_ctx, w_mod, b_mod, norm1_w, w_in, s5_lambda_re_f, s5_lambda_im_f, s5_log_step_f, s5_lambda_re_b, s5_lambda_im_b, s5_log_step_b, s5_b_re, s5_b_im, s5_c_re, s5_c_im, s5_d, s5_w_glu, s5_b_glu, ret_log_decay_f, ret_log_decay_b, w_out, norm2_w, w_up, conv_w, conv_b, w_down, final_norm_w, loss_target, m_c_ctx, m_w_mod, m_b_mod, m_norm1_w, m_w_in, m_s5_lambda_re_f, m_s5_lambda_im_f, m_s5_log_step_f, m_s5_lambda_re_b, m_s5_lambda_im_b, m_s5_log_step_b, m_s5_b_re, m_s5_b_im, m_s5_c_re, m_s5_c_im, m_s5_d, m_s5_w_glu, m_s5_b_glu, m_ret_log_decay_f, m_ret_log_decay_b, m_w_out, m_norm2_w, m_w_up, m_conv_w, m_conv_b, m_w_down, m_final_norm_w, v_c_ctx, v_w_mod, v_b_mod, v_norm1_w, v_w_in, v_s5_lambda_re_f, v_s5_lambda_im_f, v_s5_log_step_f, v_s5_lambda_re_b, v_s5_lambda_im_b, v_s5_log_step_b, v_s5_b_re, v_s5_b_im, v_s5_c_re, v_s5_c_im, v_s5_d, v_s5_w_glu, v_s5_b_glu, v_ret_log_decay_f, v_ret_log_decay_b, v_w_out, v_norm2_w, v_w_up, v_conv_w, v_conv_b, v_w_down, v_final_norm_w):
    return _step(dict(locals()))
```

```python
import functools
import math

import jax
import jax.numpy as jnp
from jax import lax
from jax.experimental import pallas as pl
from jax.experimental.pallas import tpu as pltpu

F32 = jnp.float32
BF16 = jnp.bfloat16

D_MODEL = 1024
S5_WIDTH = 512
S5_GROUPS = 32
S5_GROUP = 16
S5_STATE = 64
RET_WIDTH = 512
RET_HEADS = 4
RET_DH = 128
RET_CHUNK = 256
GRID_W = 64
ROPE_THETA = 10000.0
D_FF = 2816
NORM_EPS = 1e-6
IN_COLS = S5_WIDTH + 4 * RET_WIDTH

S5_T = 16
S5_NB = 4
S5_BW = S5_T * 128
S5_SW = 8 * 2 * S5_STATE

ADAM_LR, ADAM_B1, ADAM_B2, ADAM_EPS, ADAM_WD, ADAM_STEP = 0.001, 0.9, 0.999, 1e-08, 0.01, 10

VMEM_LIMIT = 56 * 1024 * 1024
MESH_ID = pl.DeviceIdType.MESH


def _params(sem=None):
    return pltpu.CompilerParams(dimension_semantics=sem, vmem_limit_bytes=VMEM_LIMIT)


def _full(shape):
    n = len(shape)
    return pl.BlockSpec(shape, lambda *_: (0,) * n)


def _dot(a, b):
    return jnp.dot(a, b, preferred_element_type=F32)


def _dot_nt(a, b):
    return lax.dot_general(a, b, (((1,), (1,)), ((), ())), preferred_element_type=F32)


def _dot_tn(a, b):
    return lax.dot_general(a, b, (((0,), (0,)), ((), ())), preferred_element_type=F32)


def _dot_hi(a, b):
    return jnp.dot(a, b, preferred_element_type=F32, precision=lax.Precision.HIGHEST)


def _dot_nt_hi(a, b):
    return lax.dot_general(a, b, (((1,), (1,)), ((), ())), preferred_element_type=F32,
                           precision=lax.Precision.HIGHEST)


def _gelu(x):
    return 0.5 * x * (1.0 + jnp.tanh(0.7978845608028654 * (x + 0.044715 * (x * x * x))))


def _sigmoid(x):
    return 1.0 / (1.0 + jnp.exp(-x))


def _silu(x):
    return x * _sigmoid(x)


def _rms_mod(x, nw, sh, sc):
    r = lax.rsqrt(jnp.mean(x * x, axis=-1, keepdims=True) + NORM_EPS)
    return (x * r * nw) * (1.0 + sc) + sh


def _rms(x, nw):
    r = lax.rsqrt(jnp.mean(x * x, axis=-1, keepdims=True) + NORM_EPS)
    return x * r * nw


def _head_norm_gate(y, g):
    mu = jnp.mean(y, axis=-1, keepdims=True)
    yc = y - mu
    var = jnp.mean(yc * yc, axis=-1, keepdims=True)
    return _silu(g) * (yc * lax.rsqrt(var + NORM_EPS))


def _swap_pairs(t):
    lane = lax.broadcasted_iota(jnp.int32, t.shape, 1)
    return jnp.where(lane % 2 == 0, pltpu.roll(t, RET_DH - 1, 1), pltpu.roll(t, 1, 1))


def _rope(t, cos_t, sin_t):
    return t * cos_t + _swap_pairs(t) * sin_t


def _rope_t(dt, cos_t, sin_t):
    return dt * cos_t + _swap_pairs(dt * sin_t)


def _pick(n, prefs):
    for p in prefs:
        if n % p == 0:
            return p
    return n


def _mm_tn(a, b, *, name):
    m, k = a.shape
    n = b.shape[1]
    tm = _pick(m, (512, 256, 128))
    tn = _pick(n, (1408, 1024, 1280, 512))

    def body(a_ref, b_ref, o_ref):
        @pl.when(pl.program_id(1) == 0)
        def _():
            o_ref[...] = jnp.zeros_like(o_ref)
        o_ref[...] += _dot_tn(a_ref[...], b_ref[...])

    return pl.pallas_call(
        body, name=name, grid=(n // tn, m // tm),
        in_specs=[pl.BlockSpec((tm, k), lambda j, i: (i, 0)), pl.BlockSpec((tm, tn), lambda j, i: (i, j))],
        out_specs=pl.BlockSpec((k, tn), lambda j, i: (0, j)),
        out_shape=jax.ShapeDtypeStruct((k, n), F32),
        compiler_params=_params(("parallel", "arbitrary")),
    )(a, b)


TOK_TILE = 256


def _norm_inproj(x, ctx, n1w, mod4, w_in_b):
    l, lc = x.shape[0], ctx.shape[0]
    tm = TOK_TILE
    nct = lc // tm
    la = l + lc

    def body(x_ref, c_ref, nw_ref, mod_ref, w_ref, p_ref, h_ref):
        is_ctx = pl.program_id(0) < nct
        xt = jnp.where(is_ctx, c_ref[...], x_ref[...])
        sh = jnp.where(is_ctx, mod_ref[0:1, :], mod_ref[2:3, :])
        sc = jnp.where(is_ctx, mod_ref[1:2, :], mod_ref[3:4, :])
        hb = _rms_mod(xt, nw_ref[...], sh, sc).astype(BF16)
        h_ref[...] = hb
        p_ref[...] = _dot(hb, w_ref[...])

    return pl.pallas_call(
        body, name="norm_inproj", grid=(la // tm,),
        in_specs=[pl.BlockSpec((tm, D_MODEL), lambda i: (jnp.maximum(i - nct, 0), 0)),
                  pl.BlockSpec((tm, D_MODEL), lambda i: (jnp.minimum(i, nct - 1), 0)),
                  _full((1, D_MODEL)), _full((4, D_MODEL)), _full((D_MODEL, IN_COLS))],
        out_specs=[pl.BlockSpec((tm, IN_COLS), lambda i: (i, 0)), pl.BlockSpec((tm, D_MODEL), lambda i: (i, 0))],
        out_shape=[jax.ShapeDtypeStruct((la, IN_COLS), F32), jax.ShapeDtypeStruct((la, D_MODEL), BF16)],
        compiler_params=_params(("parallel",)),
    )(x, ctx, n1w, mod4, w_in_b)


def _iota2(shape, dim):
    return lax.broadcasted_iota(jnp.int32, shape, dim)


def _group_mask(rows, cols, row_div, col_div):
    return jnp.where(_iota2((rows, cols), 0) // row_div == _iota2((rows, cols), 1) // col_div, 1.0, 0.0).astype(F32)


def _s5_gen_dir(lre, lim, lst, b_re, b_im, c_re, c_im):
    step = jnp.exp(lst)
    mag = jnp.exp(lre * step)
    ar = mag * jnp.cos(lim * step)
    ai = mag * jnp.sin(lim * step)
    den = lre * lre + lim * lim
    xr = ar - 1.0
    cr = (xr * lre + ai * lim) / den
    ci = (ai * lre - xr * lim) / den
    rexp = _group_mask(128, 8, S5_GROUP, 1)
    are, aie = _dot_hi(rexp, ar), _dot_hi(rexp, ai)
    cre, cie = _dot_hi(rexp, cr), _dot_hi(rexp, ci)
    bbr = cre * b_re - cie * b_im
    bbi = cre * b_im + cie * b_re
    gmask = _group_mask(128, 128, S5_GROUP, S5_GROUP)
    pr, pi = jnp.ones_like(are), jnp.zeros_like(are)
    xs, ys = [], []
    for t in range(S5_T + 1):
        if t < S5_T:
            xs.append(jnp.concatenate([bbr * pr - bbi * pi, bbr * pi + bbi * pr], axis=1))
        ys.append(jnp.concatenate([c_re * pr - c_im * pi, -(c_re * pi + c_im * pr)], axis=1))
        pr, pi = pr * are - pi * aie, pr * aie + pi * are
    gs = [_dot_nt_hi(x_t, ys[0]) * gmask for x_t in xs]
    r16, i16 = ar, ai
    for _ in range(4):
        r16, i16 = r16 * r16 - i16 * i16, 2.0 * r16 * i16
    return xs, ys, gs, jnp.concatenate([r16, i16], axis=1)


def _s5_expand(z):
    return jnp.concatenate([z] * 8, axis=1) * _group_mask(128, S5_SW, S5_GROUP, 128)


def _s5_contract(z):
    zm = z * _group_mask(128, S5_SW, S5_GROUP, 128)
    acc = zm[:, 0:128]
    for k in range(1, 8):
        acc = acc + zm[:, 128 * k:128 * (k + 1)]
    return acc


def _s5_param_specs():
    blk3 = lambda r, c: pl.BlockSpec((1, 1, r, c), lambda b, *_: (0, b, 0, 0))
    dir3 = lambda r, c: pl.BlockSpec((2, 1, r, c), lambda b, *_: (0, b, 0, 0))
    return [dir3(8, S5_STATE), dir3(8, S5_STATE), dir3(8, 1), blk3(128, S5_STATE), blk3(128, S5_STATE),
            blk3(128, S5_STATE), blk3(128, S5_STATE), blk3(1, 128)]


def _s5_gen(lre, lim, lst, b_re, b_im, c_re, c_im, dvec):
    def body(lre_ref, lim_ref, lst_ref, bre_ref, bim_ref, cre_ref, cim_ref, d_ref, gg_ref, xw_ref, yw_ref, a16_ref):
        eye = _group_mask(128, 128, 1, 1)
        g0 = eye * d_ref[0, 0]
        for dr in range(2):
            xs, ys, gs, a16 = _s5_gen_dir(lre_ref[dr, 0], lim_ref[dr, 0], lst_ref[dr, 0], bre_ref[0, 0],
                                          bim_ref[0, 0], cre_ref[0, 0], cim_ref[0, 0])
            a16_ref[0, dr] = a16
            for j in range(S5_T):
                xw_ref[0, dr, j] = xs[S5_T - 1 - j if dr == 0 else j]
                yw_ref[0, dr, j] = ys[j + 1 if dr == 0 else S5_T - j]
            g0 = g0 + gs[0]
            for t in range(1, S5_T):
                gg_ref[0, (S5_T - 1) + t if dr == 0 else (S5_T - 1) - t] = gs[t]
        gg_ref[0, S5_T - 1] = g0

    blk = pl.BlockSpec((1, 2, S5_T, 128, 128), lambda b: (b, 0, 0, 0, 0))
    return pl.pallas_call(
        body, name="s5_gen", grid=(S5_NB,),
        in_specs=_s5_param_specs(),
        out_specs=[pl.BlockSpec((1, 2 * S5_T - 1, 128, 128), lambda b: (b, 0, 0, 0)), blk, blk,
                   pl.BlockSpec((1, 2, 8, 128), lambda b: (b, 0, 0, 0))],
        out_shape=[jax.ShapeDtypeStruct((S5_NB, 2 * S5_T - 1, 128, 128), F32),
                   jax.ShapeDtypeStruct((S5_NB, 2, S5_T, 128, 128), F32),
                   jax.ShapeDtypeStruct((S5_NB, 2, S5_T, 128, 128), F32),
                   jax.ShapeDtypeStruct((S5_NB, 2, 8, 128), F32)],
        compiler_params=_params(("parallel",)),
    )(lre, lim, lst, b_re, b_im, c_re, c_im, dvec)


def _s5_fill_state_mat(w_scr, src_ref, dr):
    for j in range(S5_T):
        w_scr[128 * j:128 * (j + 1), :] = _s5_expand(src_ref[0, dr, j]).astype(BF16)


def _s5_fill_toeplitz(k_scr, gg_ref):
    for j in range(S5_T):
        for i in range(S5_T):
            k_scr[128 * j:128 * (j + 1), 128 * i:128 * (i + 1)] = gg_ref[0, i - j + (S5_T - 1)].astype(BF16)


S5_GEN_SPECS = [pl.BlockSpec((1, 2 * S5_T - 1, 128, 128), lambda b: (b, 0, 0, 0)),
                pl.BlockSpec((1, 2, S5_T, 128, 128), lambda b: (b, 0, 0, 0, 0))]


def _s5_gen_bwd(lre, lim, lst, b_re, b_im, c_re, c_im, dvec, dg, dx, dy, da16):
    def body(lre_ref, lim_ref, lst_ref, bre_ref, bim_ref, cre_ref, cim_ref, d_ref, dg_ref, dx_ref, dy_ref, da16_ref,
             glre_ref, glim_ref, glst_ref, gbre_ref, gbim_ref, gcre_ref, gcim_ref, gd_ref):
        eye = _group_mask(128, 128, 1, 1)
        gd_ref[0, 0] = jnp.sum(dg_ref[0, S5_T - 1] * eye, axis=0, keepdims=True)
        gb = [None, None, None, None]
        for dr in range(2):
            args = (lre_ref[dr, 0], lim_ref[dr, 0], lst_ref[dr, 0], bre_ref[0, 0], bim_ref[0, 0],
                    cre_ref[0, 0], cim_ref[0, 0])
            _, vjp = jax.vjp(_s5_gen_dir, *args)
            dxs = [dx_ref[0, dr, S5_T - 1 - t if dr == 0 else t] for t in range(S5_T)]
            dys = [jnp.zeros((128, 128), F32)] + [dy_ref[0, dr, t - 1 if dr == 0 else S5_T - t]
                                                  for t in range(1, S5_T + 1)]
            dgs = [dg_ref[0, (S5_T - 1) + t if dr == 0 else (S5_T - 1) - t] for t in range(S5_T)]
            g = vjp((dxs, dys, dgs, da16_ref[0, dr]))
            glre_ref[dr, 0] = g[0]
            glim_ref[dr, 0] = g[1]
            glst_ref[dr, 0] = g[2]
            for q in range(4):
                gb[q] = g[3 + q] if gb[q] is None else gb[q] + g[3 + q]
        gbre_ref[0, 0] = gb[0]
        gbim_ref[0, 0] = gb[1]
        gcre_ref[0, 0] = gb[2]
        gcim_ref[0, 0] = gb[3]

    shp = lambda a: jax.ShapeDtypeStruct(a.shape, F32)
    return pl.pallas_call(
        body, name="s5_gen_bwd", grid=(S5_NB,),
        in_specs=_s5_param_specs() + [
            pl.BlockSpec((1, 2 * S5_T - 1, 128, 128), lambda b: (b, 0, 0, 0)),
            pl.BlockSpec((1, 2, S5_T, 128, 128), lambda b: (b, 0, 0, 0, 0)),
            pl.BlockSpec((1, 2, S5_T, 128, 128), lambda b: (b, 0, 0, 0, 0)),
            pl.BlockSpec((1, 2, 8, 128), lambda b: (b, 0, 0, 0))],
        out_specs=_s5_param_specs(),
        out_shape=[shp(lre), shp(lim), shp(lst), shp(b_re), shp(b_im), shp(c_re), shp(c_im), shp(dvec)],
        compiler_params=_params(("parallel",)),
    )(lre, lim, lst, b_re, b_im, c_re, c_im, dvec, dg, dx, dy, da16)


def _s5_ucat(u_ref, lo=0, hi=S5_T):
    return jnp.concatenate([u_ref[:, j, :] for j in range(lo, hi)], axis=1).astype(BF16)


def _s5_put_groups(o_ref, dr, val):
    for gi in range(8):
        o_ref[dr, :, gi, :] = val[:, 128 * gi:128 * (gi + 1)]


def _s5_get_groups(s_ref, dr, n=8):
    return jnp.concatenate([s_ref[dr, :, gi, :] for gi in range(n)], axis=1).astype(BF16)


def _s5_to_states(u3, blocks, name):
    cn = u3.shape[0]

    def body(u_ref, b_ref, o_ref, w_scr):
        u = _s5_ucat(u_ref)
        for dr in range(2):
            _s5_fill_state_mat(w_scr, b_ref, dr)
            _s5_put_groups(o_ref, dr, _dot(u, w_scr[...]))

    return pl.pallas_call(
        body, name=name, grid=(S5_NB,),
        in_specs=[pl.BlockSpec((cn, S5_T, 128), lambda b: (0, 0, b)), S5_GEN_SPECS[1]],
        out_specs=pl.BlockSpec((2, cn, 8, 128), lambda b: (0, 0, b, 0)),
        out_shape=jax.ShapeDtypeStruct((2, cn, S5_GROUPS, 128), F32),
        scratch_shapes=[pltpu.VMEM((S5_BW, S5_SW), BF16)],
        compiler_params=_params(("parallel",)),
    )(u3, blocks)


def _s5_from_states(u3, gg, st, blocks, transposed, name):
    cn = u3.shape[0]

    def body(u_ref, g_ref, s_ref, b_ref, o_ref, k_scr, w_scr):
        u = _s5_ucat(u_ref)
        _s5_fill_toeplitz(k_scr, g_ref)
        y = _dot_nt(u, k_scr[...]) if transposed else _dot(u, k_scr[...])
        for dr in range(2):
            _s5_fill_state_mat(w_scr, b_ref, dr)
            y = y + _dot_nt(_s5_get_groups(s_ref, dr), w_scr[...])
        for i in range(S5_T):
            o_ref[:, i, :] = y[:, 128 * i:128 * (i + 1)]

    return pl.pallas_call(
        body, name=name, grid=(S5_NB,),
        in_specs=[pl.BlockSpec((cn, S5_T, 128), lambda b: (0, 0, b)), S5_GEN_SPECS[0],
                  pl.BlockSpec((2, cn, 8, 128), lambda b: (0, 0, b, 0)), S5_GEN_SPECS[1]],
        out_specs=pl.BlockSpec((cn, S5_T, 128), lambda b: (0, 0, b)),
        out_shape=jax.ShapeDtypeStruct((cn, S5_T, S5_WIDTH), F32),
        scratch_shapes=[pltpu.VMEM((S5_BW, S5_BW), BF16), pltpu.VMEM((S5_BW, S5_SW), BF16)],
        compiler_params=_params(("parallel",)),
    )(u3, gg, st, blocks)


def _s5_a_forms(a):
    ra = pltpu.roll(a, S5_STATE, 1)
    low = _iota2(a.shape, 1) < S5_STATE
    return jnp.where(low, a, ra), jnp.where(low, -ra, a)


def _s5_scan(sloc, a16, ncc):
    cn = sloc.shape[1]

    def body(s_ref, a_ref, h_ref):
        forms = [_s5_a_forms(a_ref[dr]) for dr in range(2)]

        def step(s, hs):
            out = []
            for dr in range(2):
                arr, aii = forms[dr]
                h, rh = hs[dr]
                c = s if dr == 0 else jnp.where(s < ncc, ncc - 1 - s, cn - 1 - (s - ncc))
                h_ref[dr, c] = h
                sc = s_ref[dr, c]
                out.append((h * arr + rh * aii + sc, rh * arr - h * aii + pltpu.roll(sc, S5_STATE, 1)))
            return tuple(out)

        zero = jnp.zeros((S5_GROUPS, 128), F32)
        lax.fori_loop(0, cn, step, ((zero, zero), (zero, zero)), unroll=4)

    return pl.pallas_call(
        body, name="s5_scan",
        out_shape=jax.ShapeDtypeStruct(sloc.shape, F32),
        compiler_params=_params(),
    )(sloc, a16)


def _s5_scan_bwd(e, hs, a16, ncc):
    cn = e.shape[1]

    def body(e_ref, h_ref, a_ref, ds_ref, da_ref):
        forms = [_s5_a_forms(a_ref[dr]) for dr in range(2)]
        low = _iota2((S5_GROUPS, 128), 1) < S5_STATE

        def step(s, carry):
            out = []
            r = cn - 1 - s
            for dr in range(2):
                arr, aii = forms[dr]
                g, rg, da = carry[dr]
                c = r if dr == 0 else jnp.where(r < ncc, ncc - 1 - r, cn - 1 - (r - ncc))
                ds_ref[dr, c] = g
                h = h_ref[dr, c]
                rh = pltpu.roll(h, S5_STATE, 1)
                da = da + jnp.where(low, g * h + rg * rh, g * rh - rg * h)
                ec = e_ref[dr, c]
                out.append((ec + g * arr - rg * aii, pltpu.roll(ec, S5_STATE, 1) + rg * arr + g * aii, da))
            return tuple(out)

        zero = jnp.zeros((S5_GROUPS, 128), F32)
        res = lax.fori_loop(0, cn, step, ((zero, zero, zero), (zero, zero, zero)), unroll=4)
        da_ref[0] = res[0][2]
        da_ref[1] = res[1][2]

    return pl.pallas_call(
        body, name="s5_scan_bwd",
        out_shape=[jax.ShapeDtypeStruct(e.shape, F32), jax.ShapeDtypeStruct((2, S5_GROUPS, 128), F32)],
        compiler_params=_params(),
    )(e, hs, a16)


def _s5_bwd_kb(p3, dy3):
    cn = p3.shape[0]
    half = S5_T // 2

    def body(u_ref, d_ref, o_ref):
        q = pl.program_id(1)

        @pl.when(q == 0)
        def _():
            o_ref[...] = jnp.zeros_like(o_ref)

        dk = _dot_tn(_s5_ucat(u_ref), _s5_ucat(d_ref, 0, half))
        for j in range(S5_T):
            for i in range(half):
                o_ref[0, half * q + i - j + (S5_T - 1)] += dk[128 * j:128 * (j + 1), 128 * i:128 * (i + 1)]

    return pl.pallas_call(
        body, name="s5_bwd_kb", grid=(S5_NB, 2),
        in_specs=[pl.BlockSpec((cn, S5_T, 128), lambda b, q: (0, 0, b)),
                  pl.BlockSpec((cn, half, 128), lambda b, q: (0, q, b))],
        out_specs=pl.BlockSpec((1, 2 * S5_T - 1, 128, 128), lambda b, q: (b, 0, 0, 0)),
        out_shape=jax.ShapeDtypeStruct((S5_NB, 2 * S5_T - 1, 128, 128), F32),
        compiler_params=_params(("parallel", "arbitrary")),
    )(p3, dy3)


def _s5_bwd_w(u3, st, name):
    cn = u3.shape[0]

    def body(u_ref, s_ref, w_ref):
        dw = _dot_tn(_s5_ucat(u_ref), _s5_get_groups(s_ref, 0))
        for j in range(S5_T):
            w_ref[0, 0, j] = _s5_contract(dw[128 * j:128 * (j + 1), :])

    return pl.pallas_call(
        body, name=name, grid=(S5_NB, 2),
        in_specs=[pl.BlockSpec((cn, S5_T, 128), lambda b, q: (0, 0, b)),
                  pl.BlockSpec((1, cn, 8, 128), lambda b, q: (q, 0, b, 0))],
        out_specs=pl.BlockSpec((1, 1, S5_T, 128, 128), lambda b, q: (b, q, 0, 0, 0)),
        out_shape=jax.ShapeDtypeStruct((S5_NB, 2, S5_T, 128, 128), F32),
        compiler_params=_params(("parallel", "parallel")),
    )(u3, st)


def _s5_glu(y_all, w_glu_b, b_glu, nct):
    la = y_all.shape[0]
    tm = TOK_TILE
    l = la - nct * tm

    def body(y_ref, w_ref, b_ref, o_ref):
        yg = _gelu(y_ref[...])
        z = _dot(yg.astype(BF16), w_ref[...]) + b_ref[...]
        o_ref[...] = (yg * _sigmoid(z)).astype(BF16)

    return pl.pallas_call(
        body, name="s5_glu", grid=(l // tm,),
        in_specs=[pl.BlockSpec((tm, S5_WIDTH), lambda i: (i + nct, 0)),
                  _full((S5_WIDTH, S5_WIDTH)), _full((1, S5_WIDTH))],
        out_specs=pl.BlockSpec((tm, S5_WIDTH), lambda i: (i, 0)),
        out_shape=jax.ShapeDtypeStruct((l, S5_WIDTH), BF16),
        compiler_params=_params(("parallel",)),
    )(y_all, w_glu_b, b_glu)


def _s5_glu_bwd(y_all, dmix, w_glu_b, b_glu, nct):
    la = y_all.shape[0]
    tm = TOK_TILE

    def body(y_ref, d_ref, w_ref, b_ref, dy_ref, gw_ref, gb_ref):
        i = pl.program_id(0)

        @pl.when(i == 0)
        def _():
            gw_ref[...] = jnp.zeros_like(gw_ref)
            gb_ref[...] = jnp.zeros_like(gb_ref)

        @pl.when(i < nct)
        def _():
            dy_ref[...] = jnp.zeros_like(dy_ref)

        @pl.when(i >= nct)
        def _():
            y = y_ref[...]
            yg, gelu_vjp = jax.vjp(_gelu, y)
            ygb = yg.astype(BF16)
            sg = _sigmoid(_dot(ygb, w_ref[...]) + b_ref[...])
            ds = d_ref[...]
            dz = ds * yg * sg * (1.0 - sg)
            dzb = dz.astype(BF16)
            dyg = ds * sg + _dot_nt(dzb, w_ref[...])
            dy_ref[...] = gelu_vjp(dyg)[0]
            gw_ref[...] += _dot_tn(ygb, dzb)
            gb_ref[...] += jnp.sum(dz, axis=0, keepdims=True)

    return pl.pallas_call(
        body, name="s5_glu_bwd", grid=(la // tm,),
        in_specs=[pl.BlockSpec((tm, S5_WIDTH), lambda i: (i, 0)),
                  pl.BlockSpec((tm, S5_WIDTH), lambda i: (jnp.maximum(i - nct, 0), 0)),
                  _full((S5_WIDTH, S5_WIDTH)), _full((1, S5_WIDTH))],
        out_specs=[pl.BlockSpec((tm, S5_WIDTH), lambda i: (i, 0)), _full((S5_WIDTH, S5_WIDTH)),
                   _full((1, S5_WIDTH))],
        out_shape=[jax.ShapeDtypeStruct((la, S5_WIDTH), F32), jax.ShapeDtypeStruct((S5_WIDTH, S5_WIDTH), F32),
                   jax.ShapeDtypeStruct((1, S5_WIDTH), F32)],
        compiler_params=_params(("arbitrary",)),
    )(y_all, dmix, w_glu_b, b_glu)


K_SCALE = RET_DH ** -0.5
Q_COL, K_COL, V_COL, G_COL = 4, 8, 12, 16


def _ret_chunk_of(step, ncc, nch, rev):
    if not rev:
        return step
    return jnp.where(step < ncc, ncc - 1 - step, nch - 1 - (step - ncc))


def _ret_decay(ld, rev):
    c = _iota2((RET_CHUNK, RET_CHUNK), 0).astype(F32)
    m = _iota2((RET_CHUNK, RET_CHUNK), 1).astype(F32)
    diff = (m - c) if rev else (c - m)
    keep = (diff > 0) if rev else (diff >= 0)
    expo = jnp.maximum(diff, 0.0)
    dm = jnp.where(keep, jnp.exp(ld * expo), 0.0)
    xi_e = (RET_CHUNK - c) if rev else (c + 1.0)
    zeta_e = c if rev else (RET_CHUNK - 1.0 - c)
    return dm, expo, jnp.exp(ld * xi_e), xi_e, jnp.exp(ld * zeta_e), zeta_e


RET_TABLES = 7


def _ret_tables(ld2):
    def body(ld_ref, t_ref):
        dr, h = pl.program_id(0), pl.program_id(1)
        ldh = ld_ref[dr, h]
        for rev in (False, True):
            @pl.when(dr == int(rev))
            def _(rev=rev):
                dm, expo, xi, xi_e, zeta, zeta_e = _ret_decay(ldh, rev)
                t_ref[0, 0, 0] = dm
                t_ref[0, 0, 1] = dm * expo
                t_ref[0, 0, 2] = xi
                t_ref[0, 0, 3] = xi * xi_e
                t_ref[0, 0, 4] = zeta
                t_ref[0, 0, 5] = zeta * zeta_e
                t_ref[0, 0, 6] = jnp.zeros_like(dm) + jnp.exp(ldh * RET_CHUNK)

    return pl.pallas_call(
        body, name="ret_tables", grid=(2, RET_HEADS),
        in_specs=[pl.BlockSpec(memory_space=pltpu.SMEM)],
        out_specs=pl.BlockSpec((1, 1, RET_TABLES, RET_CHUNK, RET_CHUNK), lambda d, h: (d, h, 0, 0, 0)),
        out_shape=jax.ShapeDtypeStruct((2, RET_HEADS, RET_TABLES, RET_CHUNK, RET_CHUNK), F32),
        compiler_params=_params(("parallel", "parallel")),
    )(ld2)


def _ret_specs(nch, ncc, rev, step_of):
    chunk = lambda n: _ret_chunk_of(step_of(n), ncc, nch, rev)
    cols = [pl.BlockSpec((RET_CHUNK, RET_WIDTH), functools.partial(lambda n, cb: (chunk(n), cb), cb=cb))
            for cb in (1, 2, 3)]
    tab = pl.BlockSpec((RET_CHUNK, RET_DH), lambda n: (chunk(n), 0))
    return cols + [tab, tab], pl.BlockSpec((RET_CHUNK, RET_WIDTH), lambda n: (chunk(n), 0))


def _ret_scan(p_all, cos_t, sin_t, tabs, ncc, placed, kinds):
    la = p_all.shape[0]
    nch = la // RET_CHUNK
    n = len(placed)
    shard_shapes = _gather_shard_shapes(placed, kinds)

    def body(t_ref, qf, kf, vf, cf, sf, qb, kb, vb, cb, sb, *rest):
        of_ref, ob_ref, ssf_ref, ssb_ref = rest[n:n + 4]
        s_scr, send_sems, recv_sems = rest[2 * n + 4:]
        step = pl.program_id(0)

        @pl.when(step == 0)
        def _():
            s_scr[...] = jnp.zeros_like(s_scr)
            for cp in _gather_chip_copies(rest[n + 4:2 * n + 4], kinds, shard_shapes, send_sems, recv_sems, False)[0]:
                cp.start()

        @pl.when(step == nch - 1)
        def _():
            sends, arrivals = _gather_chip_copies(rest[n + 4:2 * n + 4], kinds, shard_shapes, send_sems, recv_sems)
            for cp in arrivals:
                cp.wait_recv()
            for cp in sends:
                cp.wait_send()

        for dr, (q_ref, k_ref, v_ref, c_ref, n_ref, o_ref, ss_ref) in enumerate(
                ((qf, kf, vf, cf, sf, of_ref, ssf_ref), (qb, kb, vb, cb, sb, ob_ref, ssb_ref))):
            cs, sn = c_ref[...], n_ref[...]
            for h in range(RET_HEADS):
                sl = slice(RET_DH * h, RET_DH * (h + 1))
                dm, xi, zeta = t_ref[dr, h, 0], t_ref[dr, h, 2, :, 0:RET_DH], t_ref[dr, h, 4, :, 0:RET_DH]
                q = _rope(q_ref[:, sl], cs, sn)
                k = _rope(k_ref[:, sl] * K_SCALE, cs, sn)
                vh = v_ref[:, sl].astype(BF16)
                s = s_scr[dr, h]
                ss_ref[0, h] = s
                sc = (_dot_nt(q.astype(BF16), k.astype(BF16)) * dm).astype(BF16)
                o_ref[:, sl] = _dot(sc, vh) + _dot((q * xi).astype(BF16), s.astype(BF16))
                s_scr[dr, h] = t_ref[dr, h, 6, 0:RET_DH, 0:RET_DH] * s + _dot_tn((k * zeta).astype(BF16), vh)

    in_f, out_f = _ret_specs(nch, ncc, False, lambda n: n)
    in_b, out_b = _ret_specs(nch, ncc, True, lambda n: n)
    ss_spec = pl.BlockSpec((1, RET_HEADS, RET_DH, RET_DH), lambda n: (n, 0, 0, 0))
    o_shape = jax.ShapeDtypeStruct((la, RET_WIDTH), F32)
    ss_shape = jax.ShapeDtypeStruct((nch, RET_HEADS, RET_DH, RET_DH), F32)
    return pl.pallas_call(
        body, name="ret_scan", grid=(nch,),
        in_specs=[_full(tabs.shape)] + in_f + in_b + [ANY] * n,
        out_specs=[out_f, out_b, ss_spec, ss_spec] + [ANY] * n,
        out_shape=[o_shape, o_shape, ss_shape, ss_shape] + [jax.ShapeDtypeStruct(p.shape, p.dtype) for p in placed],
        input_output_aliases={11 + a: 4 + a for a in range(n)},
        scratch_shapes=[pltpu.VMEM((2, RET_HEADS, RET_DH, RET_DH), F32),
                        pltpu.SemaphoreType.DMA((n, 3)), pltpu.SemaphoreType.DMA((n, 3))],
        compiler_params=_params(("arbitrary",)),
    )(tabs, p_all, p_all, p_all, cos_t, sin_t, p_all, p_all, p_all, cos_t, sin_t, *placed)


def _ret_scan_bwd(p_all, cos_t, sin_t, tabs, ssf, ssb, dy_all, ncc):
    la = p_all.shape[0]
    nch = la // RET_CHUNK

    def body(t_ref, qf, kf, vf, cf, sf, dof, ssf_ref, qb, kb, vb, cb, sb, dob_, ssb_ref,
             dqf, dkf, dvf, dqb, dkb, dvb, dld_ref, ds_scr):
        @pl.when(pl.program_id(0) == 0)
        def _():
            ds_scr[...] = jnp.zeros_like(ds_scr)
            dld_ref[...] = jnp.zeros_like(dld_ref)

        for dr, (q_ref, k_ref, v_ref, c_ref, n_ref, do_ref, ss_ref, dq_ref, dk_ref, dv_ref) in enumerate(
                ((qf, kf, vf, cf, sf, dof, ssf_ref, dqf, dkf, dvf), (qb, kb, vb, cb, sb, dob_, ssb_ref, dqb, dkb, dvb))):
            cs, sn = c_ref[...], n_ref[...]
            for h in range(RET_HEADS):
                sl = slice(RET_DH * h, RET_DH * (h + 1))
                dm, dm_d = t_ref[dr, h, 0], t_ref[dr, h, 1]
                xi, xi_d, zeta, zeta_d = [t_ref[dr, h, t, :, 0:RET_DH] for t in (2, 3, 4, 5)]
                gc = t_ref[dr, h, 6, 0:RET_DH, 0:RET_DH]
                q = _rope(q_ref[:, sl], cs, sn)
                k = _rope(k_ref[:, sl] * K_SCALE, cs, sn)
                q16, k16, v16 = q.astype(BF16), k.astype(BF16), v_ref[:, sl].astype(BF16)
                s = ss_ref[0, h]
                s16 = s.astype(BF16)
                ds_in = ds_scr[dr, h]
                ds16 = ds_in.astype(BF16)
                do16 = do_ref[:, sl].astype(BF16)
                qk = _dot_nt(q16, k16)
                dsv = _dot_nt(do16, v16)
                dsc = (dsv * dm).astype(BF16)
                sc16 = (qk * dm).astype(BF16)
                dos = _dot_nt(do16, s16)
                vds = _dot_nt(v16, ds16)
                dq_ref[:, sl] = _dot(dsc, k16) + dos * xi
                dk_ref[:, sl] = _dot_tn(dsc, q16) + vds * zeta
                dv_ref[:, sl] = _dot_tn(sc16, do16) + _dot((k * zeta).astype(BF16), ds16)
                ds_scr[dr, h] = _dot_tn((q * xi).astype(BF16), do16) + gc * ds_in
                dld = (jnp.sum(dsv * qk * dm_d) + jnp.sum(q * dos * xi_d + k * vds * zeta_d)
                       + RET_CHUNK * jnp.sum(gc * s * ds_in))
                dld_ref[dr, h] += dld

    back = lambda n: nch - 1 - n
    in_f, out_f = _ret_specs(nch, ncc, False, back)
    in_b, out_b = _ret_specs(nch, ncc, True, back)
    ss_spec = pl.BlockSpec((1, RET_HEADS, RET_DH, RET_DH), lambda n: (nch - 1 - n, 0, 0, 0))
    shp = jax.ShapeDtypeStruct((la, RET_WIDTH), F32)
    return pl.pallas_call(
        body, name="ret_scan_bwd", grid=(nch,),
        in_specs=[_full(tabs.shape)] + in_f + [out_f, ss_spec] + in_b + [out_b, ss_spec],
        out_specs=[out_f, out_f, out_f, out_b, out_b, out_b, _full((2, RET_HEADS, 8, 128))],
        out_shape=[shp] * 6 + [jax.ShapeDtypeStruct((2, RET_HEADS, 8, 128), F32)],
        scratch_shapes=[pltpu.VMEM((2, RET_HEADS, RET_DH, RET_DH), F32)],
        compiler_params=_params(("arbitrary",)),
    )(tabs, p_all, p_all, p_all, cos_t, sin_t, dy_all, ssf, p_all, p_all, p_all, cos_t, sin_t, dy_all, ssb)


def _ret_gate(of, ob, p_all, nct):
    la = of.shape[0]
    tm = TOK_TILE
    l = la - nct * tm

    def body(of_ref, ob_ref, g_ref, r_ref, y_ref):
        y = of_ref[...] + ob_ref[...]
        y_ref[...] = y
        for h in range(RET_HEADS):
            sl = slice(RET_DH * h, RET_DH * (h + 1))
            r_ref[:, sl] = _head_norm_gate(y[:, sl], g_ref[:, sl]).astype(BF16)

    row = pl.BlockSpec((tm, RET_WIDTH), lambda i: (i + nct, 0))
    out = pl.BlockSpec((tm, RET_WIDTH), lambda i: (i, 0))
    return pl.pallas_call(
        body, name="ret_gate", grid=(l // tm,),
        in_specs=[row, row, pl.BlockSpec((tm, RET_WIDTH), lambda i: (i + nct, G_COL // 4))],
        out_specs=[out, out],
        out_shape=[jax.ShapeDtypeStruct((l, RET_WIDTH), BF16), jax.ShapeDtypeStruct((l, RET_WIDTH), F32)],
        compiler_params=_params(("parallel",)),
    )(of, ob, p_all)


def _ret_gate_bwd(y_ret, p_all, dmix, nct):
    la = p_all.shape[0]
    tm = TOK_TILE

    def body(y_ref, g_ref, d_ref, dy_ref, dg_ref):
        i = pl.program_id(0)

        @pl.when(i < nct)
        def _():
            dy_ref[...] = jnp.zeros_like(dy_ref)
            dg_ref[...] = jnp.zeros_like(dg_ref)

        @pl.when(i >= nct)
        def _():
            for h in range(RET_HEADS):
                sl = slice(RET_DH * h, RET_DH * (h + 1))
                _, vjp = jax.vjp(_head_norm_gate, y_ref[:, sl], g_ref[:, sl])
                dy, dg = vjp(d_ref[:, sl])
                dy_ref[:, sl] = dy
                dg_ref[:, sl] = dg

    xrow = lambda cb: pl.BlockSpec((tm, RET_WIDTH), lambda i: (jnp.maximum(i - nct, 0), cb))
    out = pl.BlockSpec((tm, RET_WIDTH), lambda i: (i, 0))
    shp = jax.ShapeDtypeStruct((la, RET_WIDTH), F32)
    return pl.pallas_call(
        body, name="ret_gate_bwd", grid=(la // tm,),
        in_specs=[xrow(0), pl.BlockSpec((tm, RET_WIDTH), lambda i: (i, G_COL // 4)), xrow(1)],
        out_specs=[out, out], out_shape=[shp, shp],
        compiler_params=_params(("parallel",)),
    )(y_ret, p_all, dmix)


def _in_bwd(dqf, dkf, dvf, dqb, dkb, dvb, du, dg, cos_t, sin_t, w_in_b, x, ctx, n1w, mod4, dx1):
    l, lc = x.shape[0], ctx.shape[0]
    la = l + lc
    tm = TOK_TILE
    nct = lc // tm

    def body(dqf_ref, dkf_ref, dvf_ref, dqb_ref, dkb_ref, dvb_ref, du_ref, dg_ref, cos_ref, sin_ref,
             w_ref, x_ref, c_ref, nw_ref, mod_ref, dx1_ref, dp_ref, gx_ref, acc_ref):
        i = pl.program_id(0)
        is_ctx = i < nct

        @pl.when(i == 0)
        def _():
            acc_ref[...] = jnp.zeros_like(acc_ref)

        cs, sn = cos_ref[...], sin_ref[...]
        dp_ref[:, 0:S5_WIDTH] = du_ref[...].astype(BF16)
        for h in range(RET_HEADS):
            sl = slice(RET_DH * h, RET_DH * (h + 1))
            dq = _rope_t(dqf_ref[:, sl] + dqb_ref[:, sl], cs, sn)
            dk = _rope_t(dkf_ref[:, sl] + dkb_ref[:, sl], cs, sn) * K_SCALE
            dp_ref[:, 128 * (Q_COL + h):128 * (Q_COL + h + 1)] = dq.astype(BF16)
            dp_ref[:, 128 * (K_COL + h):128 * (K_COL + h + 1)] = dk.astype(BF16)
        dp_ref[:, 128 * V_COL:128 * G_COL] = (dvf_ref[...] + dvb_ref[...]).astype(BF16)
        dp_ref[:, 128 * G_COL:IN_COLS] = dg_ref[...].astype(BF16)

        dh1 = _dot_nt(dp_ref[...], w_ref[...])
        xt = jnp.where(is_ctx, c_ref[...], x_ref[...])
        sh = jnp.where(is_ctx, mod_ref[0:1, :], mod_ref[2:3, :])
        sc = jnp.where(is_ctx, mod_ref[1:2, :], mod_ref[3:4, :])
        _, vjp = jax.vjp(_rms_mod, xt, nw_ref[...], sh, sc)
        dx, dnw, dsh, dsc = vjp(dh1)
        gx_ref[...] = dx + dx1_ref[...]
        cf = jnp.where(is_ctx, 1.0, 0.0)
        acc_ref[0:1, :] += dnw
        acc_ref[1:2, :] += cf * dsh
        acc_ref[2:3, :] += cf * dsc
        acc_ref[3:4, :] += (1.0 - cf) * dsh
        acc_ref[4:5, :] += (1.0 - cf) * dsc

    row = pl.BlockSpec((tm, RET_WIDTH), lambda i: (i, 0))
    tab = pl.BlockSpec((tm, RET_DH), lambda i: (i, 0))
    xrow = pl.BlockSpec((tm, D_MODEL), lambda i: (jnp.maximum(i - nct, 0), 0))
    return pl.pallas_call(
        body, name="in_bwd", grid=(la // tm,),
        in_specs=[row] * 8 + [tab, tab, _full((D_MODEL, IN_COLS)), xrow,
                              pl.BlockSpec((tm, D_MODEL), lambda i: (jnp.minimum(i, nct - 1), 0)),
                              _full((1, D_MODEL)), _full((4, D_MODEL)), xrow],
        out_specs=[pl.BlockSpec((tm, IN_COLS), lambda i: (i, 0)), xrow, _full((8, D_MODEL))],
        out_shape=[jax.ShapeDtypeStruct((la, IN_COLS), BF16), jax.ShapeDtypeStruct((l, D_MODEL), F32),
                   jax.ShapeDtypeStruct((8, D_MODEL), F32)],
        compiler_params=_params(("arbitrary",)),
    )(dqf, dkf, dvf, dqb, dkb, dvb, du, dg, cos_t, sin_t, w_in_b, x, ctx, n1w, mod4, dx1)


def _outproj_up(x, s5x, retx, w_out_b, mod3, n2w, w_up_b):
    l = x.shape[0]
    tm = TOK_TILE

    def body(x_ref, s_ref, r_ref, wo_ref, mod_ref, nw_ref, wu_ref, x1_ref, mix_ref, h2_ref, up_ref):
        mix = _dot(s_ref[...], wo_ref[0:S5_WIDTH, :]) + _dot(r_ref[...], wo_ref[S5_WIDTH:D_MODEL, :])
        mix_ref[...] = mix
        x1 = x_ref[...] + mod_ref[0:1, :] * mix
        x1_ref[...] = x1
        h2 = _rms_mod(x1, nw_ref[...], mod_ref[1:2, :], mod_ref[2:3, :]).astype(BF16)
        h2_ref[...] = h2
        up_ref[...] = _dot(h2, wu_ref[...])

    row = lambda w: pl.BlockSpec((tm, w), lambda i: (i, 0))
    return pl.pallas_call(
        body, name="outproj_up", grid=(l // tm,),
        in_specs=[row(D_MODEL), row(S5_WIDTH), row(RET_WIDTH), _full((D_MODEL, D_MODEL)), _full((3, D_MODEL)),
                  _full((1, D_MODEL)), _full((D_MODEL, 2 * D_FF))],
        out_specs=[row(D_MODEL), row(D_MODEL), row(D_MODEL), row(2 * D_FF)],
        out_shape=[jax.ShapeDtypeStruct((l, D_MODEL), F32), jax.ShapeDtypeStruct((l, D_MODEL), F32),
                   jax.ShapeDtypeStruct((l, D_MODEL), BF16), jax.ShapeDtypeStruct((l, 2 * D_FF), F32)],
        compiler_params=_params(("parallel",)),
    )(x, s5x, retx, w_out_b, mod3, n2w, w_up_b)


HALO = 8


def _conv_taps(g, prev_row, next_row):
    t = g.shape[0]
    r = _iota2(g.shape, 0)
    gprev = jnp.where(r == 0, prev_row, pltpu.roll(g, 1, 0))
    gnext = jnp.where(r == t - 1, next_row, pltpu.roll(g, t - 1, 0))
    return gprev, gnext


def _ffn_loss(up, x1, conv_w, conv_b, w_down_b, gate, fnw, tgt):
    l = x1.shape[0]
    tm = TOK_TILE
    nt = l // tm
    hb = tm // HALO

    def body(up_a, up_g, hp_ref, hn_ref, x1_ref, cw_ref, cb_ref, wd_ref, gate_ref, fn_ref, tgt_ref,
             act_ref, dx2_ref, ddn_ref, dact_ref, acc_ref):
        i = pl.program_id(0)

        @pl.when(i == 0)
        def _():
            acc_ref[...] = jnp.zeros_like(acc_ref)

        g = up_g[...]
        prev_row = jnp.where(i == 0, 0.0, hp_ref[HALO - 1:HALO, :])
        next_row = jnp.where(i == nt - 1, 0.0, hn_ref[0:1, :])
        gprev, gnext = _conv_taps(g, prev_row, next_row)
        gc = cb_ref[...] + gprev * cw_ref[0:1, :] + g * cw_ref[1:2, :] + gnext * cw_ref[2:3, :]
        act = (_gelu(gc) * up_a[...]).astype(BF16)
        act_ref[...] = act
        dn = _dot(act, wd_ref[...])
        x2 = x1_ref[...] + gate_ref[...] * dn
        y, vjp = jax.vjp(_rms, x2, fn_ref[...])
        err = y - tgt_ref[...]
        dx2, dfn = vjp(err * (1.0 / D_MODEL))
        dx2_ref[...] = dx2
        ddn = (dx2 * gate_ref[...]).astype(BF16)
        ddn_ref[...] = ddn
        dact_ref[...] = _dot_nt(ddn, wd_ref[...])
        acc_ref[0:1, :] += dfn
        acc_ref[1:2, :] += jnp.sum(dx2 * dn, axis=0, keepdims=True)
        acc_ref[2:3, :] += (0.5 / D_MODEL) * jnp.sum(err * err)

    row = lambda w: pl.BlockSpec((tm, w), lambda i: (i, 0))
    last = l // HALO - 1
    return pl.pallas_call(
        body, name="ffn_loss", grid=(nt,),
        in_specs=[pl.BlockSpec((tm, D_FF), lambda i: (i, 0)), pl.BlockSpec((tm, D_FF), lambda i: (i, 1)),
                  pl.BlockSpec((HALO, D_FF), lambda i: (jnp.maximum(i * hb - 1, 0), 1)),
                  pl.BlockSpec((HALO, D_FF), lambda i: (jnp.minimum((i + 1) * hb, last), 1)),
                  row(D_MODEL), _full((3, D_FF)), _full((1, D_FF)), _full((D_FF, D_MODEL)),
                  _full((1, D_MODEL)), _full((1, D_MODEL)), row(D_MODEL)],
        out_specs=[row(D_FF), row(D_MODEL), row(D_MODEL), row(D_FF), _full((8, D_MODEL))],
        out_shape=[jax.ShapeDtypeStruct((l, D_FF), BF16), jax.ShapeDtypeStruct((l, D_MODEL), F32),
                   jax.ShapeDtypeStruct((l, D_MODEL), BF16), jax.ShapeDtypeStruct((l, D_FF), F32),
                   jax.ShapeDtypeStruct((8, D_MODEL), F32)],
        compiler_params=_params(("arbitrary",)),
    )(up, up, up, up, x1, conv_w, conv_b, w_down_b, gate, fnw, tgt)


def _convglu_bwd(up, dact, conv_w, conv_b):
    l = up.shape[0]
    tm = 128
    nt = l // tm
    hb = tm // HALO
    te = tm + 2 * HALO

    def body(a_ref, ap_ref, an_ref, g_ref, gp_ref, gn_ref, d_ref, dp_ref, dn_ref, cw_ref, cb_ref,
             dup_ref, acc_ref):
        i = pl.program_id(0)

        @pl.when(i == 0)
        def _():
            acc_ref[...] = jnp.zeros_like(acc_ref)

        row = _iota2((te, D_FF), 0) + (i * tm - HALO)
        valid = (row >= 0) & (row < l)

        def ext(p, c, n):
            return jnp.where(valid, jnp.concatenate([p[...], c[...], n[...]], axis=0), 0.0)

        ae, ge, de = ext(ap_ref, a_ref, an_ref), ext(gp_ref, g_ref, gn_ref), ext(dp_ref, d_ref, dn_ref)
        gprev = pltpu.roll(ge, 1, 0)
        gnext = pltpu.roll(ge, te - 1, 0)
        w0, w1, w2 = cw_ref[0:1, :], cw_ref[1:2, :], cw_ref[2:3, :]
        gce = cb_ref[...] + gprev * w0 + ge * w1 + gnext * w2
        _, vjp = jax.vjp(lambda a, gc: _gelu(gc) * a, ae, gce)
        dae, dgce = vjp(de)
        dge = dgce * w1 + pltpu.roll(dgce, te - 1, 0) * w0 + pltpu.roll(dgce, 1, 0) * w2
        mid = slice(HALO, HALO + tm)
        dup_ref[:, 0:D_FF] = dae[mid].astype(BF16)
        dup_ref[:, D_FF:2 * D_FF] = dge[mid].astype(BF16)
        dgc = dgce[mid]
        acc_ref[0:1, :] += jnp.sum(dgc * gprev[mid], axis=0, keepdims=True)
        acc_ref[1:2, :] += jnp.sum(dgc * ge[mid], axis=0, keepdims=True)
        acc_ref[2:3, :] += jnp.sum(dgc * gnext[mid], axis=0, keepdims=True)
        acc_ref[3:4, :] += jnp.sum(dgc, axis=0, keepdims=True)

    last = l // HALO - 1

    def trio(cb):
        return [pl.BlockSpec((tm, D_FF), lambda i: (i, cb)),
                pl.BlockSpec((HALO, D_FF), lambda i: (jnp.maximum(i * hb - 1, 0), cb)),
                pl.BlockSpec((HALO, D_FF), lambda i: (jnp.minimum((i + 1) * hb, last), cb))]

    return pl.pallas_call(
        body, name="convglu_bwd", grid=(nt,),
        in_specs=trio(0) + trio(1) + trio(0) + [_full((3, D_FF)), _full((1, D_FF))],
        out_specs=[pl.BlockSpec((tm, 2 * D_FF), lambda i: (i, 0)), _full((8, D_FF))],
        out_shape=[jax.ShapeDtypeStruct((l, 2 * D_FF), BF16), jax.ShapeDtypeStruct((8, D_FF), F32)],
        compiler_params=_params(("arbitrary",)),
    )(up, up, up, up, up, up, dact, dact, dact, conv_w, conv_b)


def _up_bwd(dup, w_up_b, w_out_b, x1, dx2, mix, mod3, n2w, pairs, kinds):
    l = x1.shape[0]
    tm = TOK_TILE
    nt = l // tm
    n = len(pairs)
    shapes = _rs_slot_shapes(pairs, kinds)

    def body(dup_ref, wu_ref, wo_ref, x1_ref, dx2_ref, mix_ref, mod_ref, nw_ref, *rest):
        dx1_ref, dmixb_ref, dmix_ref, acc_ref = rest[n:n + 4]
        exchange = functools.partial(_rs_chip_copies, rest[:n], rest[n + 4:2 * n + 4], kinds, shapes, *rest[2 * n + 4:])
        step = pl.program_id(0)

        @pl.when(step == 0)
        def _():
            acc_ref[...] = jnp.zeros_like(acc_ref)
            for cp in exchange(with_arrivals=False)[0]:
                cp.start()

        @pl.when(step == nt - 1)
        def _():
            sends, arrivals = exchange()
            for cp in arrivals:
                cp.wait_recv()
            for cp in sends:
                cp.wait_send()

        dh2 = _dot_nt(dup_ref[...], wu_ref[...])
        _, vjp = jax.vjp(_rms_mod, x1_ref[...], nw_ref[...], mod_ref[1:2, :], mod_ref[2:3, :])
        dx, dnw, dsh, dsc = vjp(dh2)
        dx1 = dx + dx2_ref[...]
        dx1_ref[...] = dx1
        dmixb = (dx1 * mod_ref[0:1, :]).astype(BF16)
        dmixb_ref[...] = dmixb
        dmix_ref[...] = _dot_nt(dmixb, wo_ref[...])
        acc_ref[0:1, :] += dnw
        acc_ref[1:2, :] += jnp.sum(dx1 * mix_ref[...], axis=0, keepdims=True)
        acc_ref[2:3, :] += dsh
        acc_ref[3:4, :] += dsc

    row = pl.BlockSpec((tm, D_MODEL), lambda i: (i, 0))
    return pl.pallas_call(
        body, name="up_bwd", grid=(nt,),
        in_specs=[pl.BlockSpec((tm, 2 * D_FF), lambda i: (i, 0)), _full((D_MODEL, 2 * D_FF)),
                  _full((D_MODEL, D_MODEL)), row, row, row, _full((3, D_MODEL)), _full((1, D_MODEL))] + [ANY] * n,
        out_specs=[row, row, row, _full((8, D_MODEL))] + [ANY] * n,
        out_shape=[jax.ShapeDtypeStruct((l, D_MODEL), F32), jax.ShapeDtypeStruct((l, D_MODEL), BF16),
                   jax.ShapeDtypeStruct((l, D_MODEL), F32), jax.ShapeDtypeStruct((8, D_MODEL), F32)]
        + [jax.ShapeDtypeStruct((4,) + s, p.dtype) for s, p in zip(shapes, pairs)],
        scratch_shapes=[pltpu.SemaphoreType.DMA((n, 3)), pltpu.SemaphoreType.DMA((n, 3))],
        compiler_params=_params(("arbitrary",)),
    )(dup, w_up_b, w_out_b, x1, dx2, mix, mod3, n2w, *pairs)


MOD_ROWS = 16
MOD_COLS = 6 * D_MODEL // 4


def _mod_fwd(c_all, c_ctx, w_mod_b, b_loc):
    def body(c_ref, cc_ref, w_ref, b_ref, m_ref, s_ref):
        cond = jnp.concatenate([c_ref[...], jnp.broadcast_to(cc_ref[...], (8, D_MODEL))], axis=0)
        s = _silu(cond).astype(BF16)
        s_ref[...] = s
        m_ref[...] = _dot(s, w_ref[...]) + b_ref[...]

    return pl.pallas_call(
        body, name="mod_fwd",
        out_shape=[jax.ShapeDtypeStruct((MOD_ROWS, MOD_COLS), F32), jax.ShapeDtypeStruct((MOD_ROWS, D_MODEL), BF16)],
        compiler_params=_params(),
    )(c_all, c_ctx, w_mod_b, b_loc)


def _mod_bwd_sum(dm_all):
    def body(d_ref, dm_ref, gb_ref):
        rows = [d_ref[k, 0:1, :] for k in range(8)]
        ctx_sum = d_ref[0, 1:2, :]
        for k in range(1, 8):
            ctx_sum = ctx_sum + d_ref[k, 1:2, :]
        gb = ctx_sum
        for k in range(8):
            gb = gb + rows[k]
        gb_ref[...] = gb
        dm_ref[...] = jnp.concatenate(rows + [ctx_sum] + [jnp.zeros((7, 6 * D_MODEL), F32)], axis=0)

    return pl.pallas_call(
        body, name="mod_bwd_sum",
        out_shape=[jax.ShapeDtypeStruct((MOD_ROWS, 6 * D_MODEL), F32), jax.ShapeDtypeStruct((1, 6 * D_MODEL), F32)],
        compiler_params=_params(),
    )(dm_all)


def _mod_bwd_w(dm_loc, s_b, c_ctx, w_mod_b):
    def body(d_ref, s_ref, cc_ref, w_ref, gw_ref, gc_ref):
        db = d_ref[...].astype(BF16)
        gw_ref[...] = _dot_tn(s_ref[...], db)
        ds = _dot_nt(db, w_ref[...])
        _, vjp = jax.vjp(_silu, cc_ref[...])
        gc_ref[...] = jnp.broadcast_to(vjp(ds[8:9, :])[0], (8, D_MODEL))

    return pl.pallas_call(
        body, name="mod_bwd_w",
        out_shape=[jax.ShapeDtypeStruct((D_MODEL, MOD_COLS), F32), jax.ShapeDtypeStruct((8, D_MODEL), F32)],
        compiler_params=_params(),
    )(dm_loc, s_b, c_ctx, w_mod_b)


def _adamw(w, g, m, v, name):
    r, c = w.shape
    tr = _pick(r, (256, 128, 64, 32, 16, 8))
    bc1 = 1.0 - ADAM_B1 ** ADAM_STEP
    bc2 = 1.0 - ADAM_B2 ** ADAM_STEP

    def body(w_ref, g_ref, m_ref, v_ref, d_ref, nm_ref, nv_ref):
        gg = g_ref[...]
        nm = ADAM_B1 * m_ref[...] + (1.0 - ADAM_B1) * gg
        nv = ADAM_B2 * v_ref[...] + (1.0 - ADAM_B2) * (gg * gg)
        nm_ref[...] = nm
        nv_ref[...] = nv
        d_ref[...] = -ADAM_LR * ((nm / bc1) / (jnp.sqrt(nv / bc2) + ADAM_EPS) + ADAM_WD * w_ref[...])

    blk = pl.BlockSpec((tr, c), lambda i: (i, 0))
    shp = jax.ShapeDtypeStruct((r, c), F32)
    return pl.pallas_call(
        body, name=name, grid=(r // tr,), in_specs=[blk] * 4, out_specs=[blk] * 3, out_shape=[shp] * 3,
        compiler_params=_params(("parallel",)),
    )(w, g, m, v)


def _sum_slots(a, name):
    n, r, c = a.shape
    tr = _pick(r, (376, 256, 208, 128, 64, 32, 16, 8))

    def body(a_ref, o_ref):
        acc = a_ref[0].astype(F32)
        for k in range(1, n):
            acc = acc + a_ref[k].astype(F32)
        o_ref[...] = acc

    return pl.pallas_call(
        body, name=name, grid=(r // tr,),
        in_specs=[pl.BlockSpec((n, tr, c), lambda i: (0, i, 0))],
        out_specs=pl.BlockSpec((tr, c), lambda i: (i, 0)),
        out_shape=jax.ShapeDtypeStruct((r, c), F32),
        compiler_params=_params(("parallel",)),
    )(a)


def _mesh_pos():
    return lax.axis_index("x"), lax.axis_index("y"), lax.axis_index("c")


def _all_gather8(v, name):
    m_per, n = v.shape

    def body(x_ref, out_ref, send_sems, recv_sems, local_sem):
        x, y, c = _mesh_pos()
        me, sibling = (x, y, c), (x, y, 1 - c)
        chips = [(1 - x, y), (x, 1 - y), (1 - x, 1 - y)]

        def rows(px, py, pc):
            return out_ref.at[pl.ds((4 * px + 2 * py + pc) * m_per, m_per), :]

        def copy(k, block, to, src=None):
            return pltpu.make_async_remote_copy(
                src_ref=rows(*block) if src is None else src, dst_ref=rows(*block),
                send_sem=send_sems.at[k], recv_sem=recv_sems.at[k], device_id=to, device_id_type=MESH_ID)

        mine = pltpu.make_async_copy(x_ref, rows(*me), local_sem)
        mine.start()
        first = [copy(0, me, sibling, src=x_ref)]
        first += [copy(1 + j, me, (*chip, c), src=x_ref) for j, chip in enumerate(chips)]
        for cp in first:
            cp.start()
        passed = [copy(4 + j, (*chip, c), sibling) for j, chip in enumerate(chips)]
        for j, chip in enumerate(chips):
            copy(1 + j, (*chip, c), me).wait_recv()
            passed[j].start()
        copy(0, sibling, me).wait_recv()
        for j, chip in enumerate(chips):
            copy(4 + j, (*chip, 1 - c), me).wait_recv()
        for cp in first + passed:
            cp.wait_send()
        mine.wait()

    return pl.pallas_call(
        body, name=name,
        out_shape=jax.ShapeDtypeStruct((8 * m_per, n), v.dtype),
        in_specs=[pl.BlockSpec(memory_space=pltpu.VMEM)],
        out_specs=pl.BlockSpec(memory_space=pltpu.VMEM),
        scratch_shapes=[pltpu.SemaphoreType.DMA((7,)), pltpu.SemaphoreType.DMA((7,)), pltpu.SemaphoreType.DMA],
        compiler_params=_params(),
    )(v)


ANY = pl.BlockSpec(memory_space=pl.ANY)
PEER_CHIPS = lambda x, y: [(x, 1 - y), (1 - x, y), (1 - x, 1 - y)]


def _shard_region(ref, kind, k, rl, cl, r0, nr, c0, nc):
    if kind == "col":
        return ref.at[pl.ds(r0, nr), pl.ds(k * cl + c0, nc)]
    return ref.at[pl.ds(k * rl + r0, nr), pl.ds(c0, nc)]


def _place_shard(w, kind, chip, name):
    rl, cl = w.shape
    tr = _pick(rl, (256, 128, 64))
    nt = rl // tr

    def body(chip_ref, w_ref, o_ref):
        o_ref[...] = w_ref[...].astype(BF16)

    o_map = (lambda i, chip_ref: (i, chip_ref[0])) if kind == "col" else (lambda i, chip_ref: (chip_ref[0] * nt + i, 0))
    return pl.pallas_call(
        body, name=name,
        grid_spec=pltpu.PrefetchScalarGridSpec(
            num_scalar_prefetch=1, grid=(nt,),
            in_specs=[pl.BlockSpec((tr, cl), lambda i, chip_ref: (i, 0))], out_specs=pl.BlockSpec((tr, cl), o_map)),
        out_shape=jax.ShapeDtypeStruct((rl, 4 * cl) if kind == "col" else (4 * rl, cl), BF16),
        compiler_params=_params(("parallel",)),
    )(chip.reshape(1), w)


def _gather_shard_shapes(placed, kinds):
    return [(p.shape[0], p.shape[1] // 4) if k == "col" else (p.shape[0] // 4, p.shape[1]) for p, k in zip(placed, kinds)]


def _gather_chip_copies(outs, kinds, shard_shapes, send_sems, recv_sems, with_arrivals=True):
    x, y, c = _mesh_pos()
    me = 2 * x + y
    sends, arrivals = [], []
    for a in range(len(outs)):
        rl, cl = shard_shapes[a]
        rh = rl // 2
        reg = functools.partial(_shard_region, outs[a], kinds[a], rl=rl, cl=cl, r0=c * rh, nr=rh, c0=0, nc=cl)
        for j, (px, py) in enumerate(PEER_CHIPS(x, y)):
            to = dict(send_sem=send_sems.at[a, j], recv_sem=recv_sems.at[a, j], device_id=(px, py, c),
                      device_id_type=MESH_ID)
            sends.append(pltpu.make_async_remote_copy(src_ref=reg(k=me), dst_ref=reg(k=me), **to))
            if with_arrivals:
                got = reg(k=2 * px + py)
                arrivals.append(pltpu.make_async_remote_copy(src_ref=got, dst_ref=got, **to))
    return sends, arrivals


def _gather_sibling_copies(outs, kinds, shard_shapes, send_sems, recv_sems):
    x, y, c = _mesh_pos()
    forwards, arrivals = [], []
    for a in range(len(outs)):
        rl, cl = shard_shapes[a]
        rh = rl // 2
        for j, (px, py) in enumerate(PEER_CHIPS(x, y)):
            to = dict(send_sem=send_sems.at[a, j], recv_sem=recv_sems.at[a, j], device_id=(x, y, 1 - c),
                      device_id_type=MESH_ID)
            reg = functools.partial(_shard_region, outs[a], kinds[a], k=2 * px + py, rl=rl, cl=cl, nr=rh, c0=0, nc=cl)
            forwards.append(pltpu.make_async_remote_copy(src_ref=reg(r0=c * rh), dst_ref=reg(r0=c * rh), **to))
            arrivals.append(pltpu.make_async_remote_copy(src_ref=reg(r0=(1 - c) * rh), dst_ref=reg(r0=(1 - c) * rh), **to))
    return forwards, arrivals


def _gather_weights(placed, kinds):
    n = len(placed)
    shard_shapes = _gather_shard_shapes(placed, kinds)

    def body(*refs):
        outs = refs[n:2 * n]
        ici_send, ici_recv, sib_send, sib_recv = refs[2 * n:]
        sends, arrivals = _gather_chip_copies(outs, kinds, shard_shapes, ici_send, ici_recv)
        for cp in sends:
            cp.start()
        forwards, from_sibling = _gather_sibling_copies(outs, kinds, shard_shapes, sib_send, sib_recv)
        for cp, fwd in zip(arrivals, forwards):
            cp.wait_recv()
            fwd.start()
        for cp in from_sibling:
            cp.wait_recv()
        for cp in sends + forwards:
            cp.wait_send()

    return pl.pallas_call(
        body, name="gather_weights",
        out_shape=[jax.ShapeDtypeStruct(p.shape, p.dtype) for p in placed],
        in_specs=[ANY] * n, out_specs=[ANY] * n, input_output_aliases={a: a for a in range(n)},
        scratch_shapes=[pltpu.SemaphoreType.DMA((n, 3))] * 4,
        compiler_params=_params(),
    )(*placed)


def _gather_sibling(placed, kinds):
    n = len(placed)
    shard_shapes = _gather_shard_shapes(placed, kinds)

    def body(*refs):
        forwards, from_sibling = _gather_sibling_copies(refs[n:2 * n], kinds, shard_shapes, *refs[2 * n:])
        for cp in forwards:
            cp.start()
        for cp in from_sibling:
            cp.wait_recv()
        for cp in forwards:
            cp.wait_send()

    return pl.pallas_call(
        body, name="gather_sibling",
        out_shape=[jax.ShapeDtypeStruct(p.shape, p.dtype) for p in placed],
        in_specs=[ANY] * n, out_specs=[ANY] * n, input_output_aliases={a: a for a in range(n)},
        scratch_shapes=[pltpu.SemaphoreType.DMA((n, 3))] * 2,
        compiler_params=_params(),
    )(*placed)


def _half(kind, r, c):
    return (r // 2, c) if kind == "col" else (r, c // 2)


def _half_of(ref, kind, which):
    r, c = ref.shape
    hr, hc = _half(kind, r, c)
    return ref.at[pl.ds(which * hr, hr), :] if kind == "col" else ref.at[:, pl.ds(which * hc, hc)]


def _rs_sibling(grads, kinds, name):
    n = len(grads)

    def body(*refs):
        srcs, dsts = refs[:n], refs[n:2 * n]
        send_sems, recv_sems = refs[2 * n:]
        x, y, c = _mesh_pos()
        cps = [pltpu.make_async_remote_copy(src_ref=_half_of(srcs[a], kinds[a], 1 - c), dst_ref=dsts[a],
                                            send_sem=send_sems.at[a], recv_sem=recv_sems.at[a],
                                            device_id=(x, y, 1 - c), device_id_type=MESH_ID) for a in range(n)]
        for cp in cps:
            cp.start()
        for cp in cps:
            cp.wait()

    return pl.pallas_call(
        body, name=name,
        out_shape=[jax.ShapeDtypeStruct(_half(k, *g.shape), g.dtype) for g, k in zip(grads, kinds)],
        in_specs=[ANY] * n, out_specs=[ANY] * n,
        scratch_shapes=[pltpu.SemaphoreType.DMA((n,)), pltpu.SemaphoreType.DMA((n,))],
        compiler_params=_params(),
    )(*grads)


def _pair_sum(gf, rv, kind, ci, name):
    r, c = rv.shape
    tr = _pick(r, (128, 64, 32, 16, 8))
    nt = r // tr

    def body(ci_ref, g_ref, r_ref, o_ref):
        o_ref[...] = (g_ref[...] + r_ref[...]).astype(BF16)

    g_map = (lambda i, ci_ref: (ci_ref[0] * nt + i, 0)) if kind == "col" else (lambda i, ci_ref: (i, ci_ref[0]))
    blk = pl.BlockSpec((tr, c), lambda i, ci_ref: (i, 0))
    return pl.pallas_call(
        body, name=name,
        grid_spec=pltpu.PrefetchScalarGridSpec(num_scalar_prefetch=1, grid=(nt,),
                                               in_specs=[pl.BlockSpec((tr, c), g_map), blk], out_specs=blk),
        out_shape=jax.ShapeDtypeStruct((r, c), BF16),
        compiler_params=_params(("parallel",)),
    )(ci.reshape(1), gf, rv)


def _rs_slot_shapes(pairs, kinds):
    return [(p.shape[0], p.shape[1] // 4) if k == "col" else (p.shape[0] // 4, p.shape[1]) for p, k in zip(pairs, kinds)]


def _rs_chip_copies(srcs, dsts, kinds, shapes, send_sems, recv_sems, with_arrivals=True):
    x, y, c = _mesh_pos()
    me = 2 * x + y
    sends, arrivals = [], []
    for a in range(len(srcs)):
        rl, cl = shapes[a]
        reg = functools.partial(_shard_region, srcs[a], kinds[a], rl=rl, cl=cl, r0=0, nr=rl, c0=0, nc=cl)
        for j, (px, py) in enumerate(PEER_CHIPS(x, y)):
            to = dict(send_sem=send_sems.at[a, j], recv_sem=recv_sems.at[a, j], device_id=(px, py, c),
                      device_id_type=MESH_ID)
            sends.append(pltpu.make_async_remote_copy(src_ref=reg(k=2 * px + py), dst_ref=dsts[a].at[me], **to))
            if with_arrivals:
                slot = dsts[a].at[2 * px + py]
                arrivals.append(pltpu.make_async_remote_copy(src_ref=slot, dst_ref=slot, **to))
    return sends, arrivals


def _rs_chips(pairs, kinds):
    n = len(pairs)
    shapes = _rs_slot_shapes(pairs, kinds)

    def body(*refs):
        sends, arrivals = _rs_chip_copies(refs[:n], refs[n:2 * n], kinds, shapes, *refs[2 * n:])
        for cp in sends:
            cp.start()
        for cp in arrivals:
            cp.wait_recv()
        for cp in sends:
            cp.wait_send()

    return pl.pallas_call(
        body, name="rs_chips",
        out_shape=[jax.ShapeDtypeStruct((4,) + s, p.dtype) for s, p in zip(shapes, pairs)],
        in_specs=[ANY] * n, out_specs=[ANY] * n,
        scratch_shapes=[pltpu.SemaphoreType.DMA((n, 3)), pltpu.SemaphoreType.DMA((n, 3))],
        compiler_params=_params(),
    )(*pairs)


def _sum_chips(pair, got, kind, pos, name):
    _, r, c = got.shape
    tr = _pick(r, (256, 128, 64, 32, 16))
    nt = r // tr

    def body(pos_ref, own_ref, g1_ref, g2_ref, g3_ref, o_ref):
        o_ref[...] = ((own_ref[...].astype(F32) + g1_ref[0].astype(F32)) + g2_ref[0].astype(F32)) + g3_ref[0].astype(F32)

    if kind == "col":
        own_map = lambda i, p: (i, p[1])
        out_map = lambda i, p: (p[0] * nt + i, 0)
        out_shape = (2 * r, c)
    else:
        own_map = lambda i, p: (p[1] * nt + i, 0)
        out_map = lambda i, p: (i, p[0])
        out_shape = (r, 2 * c)
    peer = lambda m: pl.BlockSpec((1, tr, c), lambda i, p: (p[1] ^ m, i, 0))
    return pl.pallas_call(
        body, name=name,
        grid_spec=pltpu.PrefetchScalarGridSpec(
            num_scalar_prefetch=1, grid=(nt,),
            in_specs=[pl.BlockSpec((tr, c), own_map), peer(1), peer(2), peer(3)],
            out_specs=pl.BlockSpec((tr, c), out_map)),
        out_shape=jax.ShapeDtypeStruct(out_shape, F32),
        compiler_params=_params(("parallel",)),
    )(pos, pair, got, got, got)


def _rs_back(halves, kinds):
    n = len(halves)

    def body(*refs):
        outs = refs[n:2 * n]
        send_sems, recv_sems = refs[2 * n:]
        x, y, c = _mesh_pos()
        cps = []
        for a in range(n):
            mine = _half_of(outs[a], kinds[a], c)
            cps.append(pltpu.make_async_remote_copy(src_ref=mine, dst_ref=mine, send_sem=send_sems.at[a],
                                                    recv_sem=recv_sems.at[a], device_id=(x, y, 1 - c),
                                                    device_id_type=MESH_ID))
            cps[-1].start()
        for a in range(n):
            other = _half_of(outs[a], kinds[a], 1 - c)
            pltpu.make_async_remote_copy(src_ref=other, dst_ref=other, send_sem=send_sems.at[a],
                                         recv_sem=recv_sems.at[a], device_id=(x, y, 1 - c),
                                         device_id_type=MESH_ID).wait_recv()
        for cp in cps:
            cp.wait_send()

    return pl.pallas_call(
        body, name="rs_back",
        out_shape=[jax.ShapeDtypeStruct(h.shape, h.dtype) for h in halves],
        in_specs=[ANY] * n, out_specs=[ANY] * n, input_output_aliases={a: a for a in range(n)},
        scratch_shapes=[pltpu.SemaphoreType.DMA((n,)), pltpu.SemaphoreType.DMA((n,))],
        compiler_params=_params(),
    )(*halves)


def _rope_tables(l, lc):
    rows = l // GRID_W
    row = jnp.repeat(jnp.arange(rows, dtype=F32), GRID_W)
    col = jnp.tile(jnp.arange(GRID_W, dtype=F32), rows)
    n_freq = RET_DH // 4
    inv_freq = ROPE_THETA ** (-jnp.arange(n_freq, dtype=F32) / n_freq)
    ang = jnp.concatenate([row[:, None] * inv_freq, col[:, None] * inv_freq], axis=-1)
    cos_t = jnp.repeat(jnp.cos(ang), 2, axis=-1)
    sin_t = jnp.repeat(jnp.sin(ang), 2, axis=-1) * jnp.tile(jnp.array([-1.0, 1.0], F32), RET_DH // 2)
    cos_t = jnp.concatenate([jnp.ones((lc, RET_DH), F32), cos_t], axis=0)
    sin_t = jnp.concatenate([jnp.zeros((lc, RET_DH), F32), sin_t], axis=0)
    return cos_t, sin_t


def _s5_pack(a):
    blk = lambda t: t.reshape(1, S5_NB, 128, S5_STATE)
    lre = jnp.stack([a["s5_lambda_re_f"][0], a["s5_lambda_re_b"][0]]).reshape(2, S5_NB, 8, S5_STATE)
    lim = jnp.stack([a["s5_lambda_im_f"][0], a["s5_lambda_im_b"][0]]).reshape(2, S5_NB, 8, S5_STATE)
    lst = jnp.stack([a["s5_log_step_f"][0], a["s5_log_step_b"][0]]).reshape(2, S5_NB, 8, 1)
    b_re = blk(a["s5_b_re"][0].transpose(0, 2, 1))
    b_im = blk(a["s5_b_im"][0].transpose(0, 2, 1))
    return (lre, lim, lst, b_re, b_im, blk(a["s5_c_re"][0]), blk(a["s5_c_im"][0]),
            a["s5_d"].reshape(1, S5_NB, 1, 128))


def _s5_unpack(g):
    glre, glim, glst, gbre, gbim, gcre, gcim, gd = g
    unb = lambda t: t.reshape(S5_GROUPS, S5_GROUP, S5_STATE).transpose(0, 2, 1)[None]
    return {
        "s5_lambda_re_f": glre[0].reshape(1, S5_GROUPS, S5_STATE), "s5_lambda_re_b": glre[1].reshape(1, S5_GROUPS, S5_STATE),
        "s5_lambda_im_f": glim[0].reshape(1, S5_GROUPS, S5_STATE), "s5_lambda_im_b": glim[1].reshape(1, S5_GROUPS, S5_STATE),
        "s5_log_step_f": glst[0].reshape(1, S5_GROUPS), "s5_log_step_b": glst[1].reshape(1, S5_GROUPS),
        "s5_b_re": unb(gbre), "s5_b_im": unb(gbim),
        "s5_c_re": gcre.reshape(1, S5_GROUPS, S5_GROUP, S5_STATE), "s5_c_im": gcim.reshape(1, S5_GROUPS, S5_GROUP, S5_STATE),
        "s5_d": gd.reshape(1, S5_WIDTH),
    }


def _local_step(a, wb, late, mx, mc, conv_w, ci):
    x, ctx, tgt = a["x"][0], a["ctx"][0], a["loss_target"][0]
    l, lc = x.shape[0], ctx.shape[0]
    la = l + lc
    nct, ncc, nrc, cn = lc // TOK_TILE, lc // S5_T, lc // RET_CHUNK, la // S5_T
    n1w, n2w, fnw = a["norm1_w"], a["norm2_w"], a["final_norm_w"].reshape(1, D_MODEL)
    conv_b, b_glu = a["conv_b"], a["s5_b_glu"]
    ld2 = jnp.concatenate([a["ret_log_decay_f"], a["ret_log_decay_b"]], axis=0)
    mod4 = jnp.concatenate([mc[0:2], mx[0:2]], axis=0)
    mod3 = mx[2:5]
    gate5 = mx[5:6]
    cos_t, sin_t = _rope_tables(l, lc)
    s5p = _s5_pack(a)

    p_all, h1b = _norm_inproj(x, ctx, n1w, mod4, wb["w_in"])
    p3 = p_all.reshape(cn, S5_T, IN_COLS)
    gg, xw, yw, a16 = _s5_gen(*s5p)
    sloc = _s5_to_states(p3, xw, "s5_state")
    a16s = a16.transpose(1, 0, 2, 3).reshape(2, S5_GROUPS, 128)
    hs = _s5_scan(sloc, a16s, ncc)
    y_all = _s5_from_states(p3, gg, hs, yw, False, "s5_out").reshape(la, S5_WIDTH)
    s5x = _s5_glu(y_all, wb["s5_w_glu"], b_glu, nct)
    tabs = _ret_tables(ld2)
    of, ob, ssf, ssb, *late = _ret_scan(p_all, cos_t, sin_t, tabs, nrc, late, LATE_KINDS)
    wb = {**wb, **dict(zip(LATE_NAMES, _gather_sibling(late, LATE_KINDS)))}
    retx, y_ret = _ret_gate(of, ob, p_all, nct)
    x1, mix, h2b, up = _outproj_up(x, s5x, retx, wb["w_out"], mod3, n2w, wb["w_up"])
    act, dx2, ddn, dact, acc_f = _ffn_loss(up, x1, conv_w, conv_b, wb["w_down"], gate5, fnw, tgt)

    g = {}
    g["w_down"] = _mm_tn(act, ddn, name="gw_down")
    dup, acc_c = _convglu_bwd(up, dact, conv_w, conv_b)
    g["w_up"] = _mm_tn(h2b, dup, name="gw_up")
    first = [g[n] for n in FIRST_GRADS]
    first_pairs = [_pair_sum(gf, rv, k, ci, "rs_pair_" + n)
                   for gf, rv, k, n in zip(first, _rs_sibling(first, FIRST_KINDS, "rs_sibling_first"), FIRST_KINDS, FIRST_GRADS)]
    dx1, dmixb, dmix, acc_2, *first_got = _up_bwd(dup, wb["w_up"], wb["w_out"], x1, dx2, mix, mod3, n2w,
                                                  first_pairs, FIRST_KINDS)
    g["w_out"] = jnp.concatenate([_mm_tn(s5x, dmixb, name="gw_out_s5"), _mm_tn(retx, dmixb, name="gw_out_ret")], axis=0)

    dy_s5, g["s5_w_glu"], g["s5_b_glu"] = _s5_glu_bwd(y_all, dmix, wb["s5_w_glu"], b_glu, nct)
    dy3 = dy_s5.reshape(cn, S5_T, S5_WIDTH)
    e = _s5_to_states(dy3, yw, "s5_bwd_h")
    ds, da16 = _s5_scan_bwd(e, hs, a16s, ncc)
    du = _s5_from_states(dy3, gg, ds, xw, True, "s5_bwd_u").reshape(la, S5_WIDTH)
    dkb = _s5_bwd_kb(p3, dy3)
    dwst = _s5_bwd_w(p3, ds, "s5_bwd_wst")
    dwout = _s5_bwd_w(dy3, hs, "s5_bwd_wout")
    da16p = da16.reshape(2, S5_NB, 8, 128).transpose(1, 0, 2, 3)
    g.update(_s5_unpack(_s5_gen_bwd(*s5p, dkb, dwst, dwout, da16p)))

    dy_ret, dg = _ret_gate_bwd(y_ret, p_all, dmix, nct)
    dqf, dkf, dvf, dqb, dkb_, dvb, dld = _ret_scan_bwd(p_all, cos_t, sin_t, tabs, ssf, ssb, dy_ret, nrc)
    g["ret_log_decay_f"] = dld[0, :, 0, 0].reshape(1, RET_HEADS)
    g["ret_log_decay_b"] = dld[1, :, 0, 0].reshape(1, RET_HEADS)
    dp, grad_x, acc_1 = _in_bwd(dqf, dkf, dvf, dqb, dkb_, dvb, du, dg, cos_t, sin_t, wb["w_in"], x, ctx, n1w, mod4, dx1)
    g["w_in"] = _mm_tn(h1b, dp, name="gw_in")

    g["norm1_w"], g["norm2_w"], g["final_norm_w"] = acc_1[0:1], acc_2[0:1], acc_f[0]
    g["conv_w"], g["conv_b"] = acc_c[0:3], acc_c[3:4]
    zero = jnp.zeros((1, D_MODEL), F32)
    dmx = jnp.concatenate([acc_1[3:5], acc_2[1:2], acc_2[2:4], acc_f[1:2]], axis=0)
    dmc = jnp.concatenate([acc_1[1:3], zero, zero, zero, zero], axis=0)
    return acc_f[2, 0], grad_x, g, dmx, dmc, first_pairs, first_got


WEIGHT_NAMES = ("c_ctx", "w_mod", "b_mod", "norm1_w", "w_in", "s5_lambda_re_f", "s5_lambda_im_f", "s5_log_step_f",
                "s5_lambda_re_b", "s5_lambda_im_b", "s5_log_step_b", "s5_b_re", "s5_b_im", "s5_c_re", "s5_c_im",
                "s5_d", "s5_w_glu", "s5_b_glu", "ret_log_decay_f", "ret_log_decay_b", "w_out", "norm2_w", "w_up",
                "conv_w", "conv_b", "w_down", "final_norm_w")
BIG_NAMES = ("w_in", "w_out", "w_up", "w_down", "s5_w_glu")
BIG_KINDS = ("col", "row", "col", "row", "row")
EARLY_NAMES, EARLY_KINDS = ("w_in", "s5_w_glu"), ("col", "row")
LATE_NAMES, LATE_KINDS = ("w_out", "w_up", "w_down"), ("row", "col", "row")
FIRST_GRADS, FIRST_KINDS = ("w_down", "w_up"), ("row", "col")
LAST_GRADS, LAST_KINDS = ("w_in", "w_out", "s5_w_glu"), ("col", "row", "row")
SMALL_NAMES = ("norm1_w", "norm2_w", "final_norm_w", "conv_b", "conv_w", "s5_lambda_re_f", "s5_lambda_im_f",
               "s5_log_step_f", "s5_lambda_re_b", "s5_lambda_im_b", "s5_log_step_b", "s5_b_re", "s5_b_im", "s5_c_re",
               "s5_c_im", "s5_d", "s5_b_glu", "ret_log_decay_f", "ret_log_decay_b")
ROW = 1024
N_CHIPS = 4


def _pack_rows(parts):
    flat = jnp.concatenate([p.reshape(-1) for p in parts])
    n = flat.shape[0]
    rows = -(-n // (8 * ROW)) * 8
    return jnp.pad(flat, (0, rows * ROW - n)).reshape(rows, ROW)


def _unpack_rows(packed, shapes):
    flat = packed.reshape(-1)
    out, off = [], 0
    for s in shapes:
        n = math.prod(s)
        out.append(flat[off:off + n].reshape(s))
        off += n
    return out


def _step(a):
    xi, yi, ci = _mesh_pos()
    chip = 2 * xi + yi
    dev = 2 * chip + ci

    cw_loc = a["conv_w"].reshape(-1)
    small_in = jnp.concatenate([a["c"].reshape(-1), jnp.pad(cw_loc, (0, 24 * 128 - cw_loc.shape[0]))]).reshape(32, 128)
    sg = _all_gather8(small_in, "gather_cond").reshape(8, 32, 128)
    c_all = sg[:, 0:8].reshape(8, D_MODEL)
    conv_w = sg[0::2, 8:32].reshape(N_CHIPS, -1)[:, :cw_loc.shape[0]].reshape(N_CHIPS, 3, -1)
    conv_w = conv_w.transpose(1, 0, 2).reshape(3, D_FF)

    placed = {n: _place_shard(a[n][0], k, chip, "place_" + n) for n, k in zip(BIG_NAMES, BIG_KINDS)}
    wb = dict(zip(EARLY_NAMES, _gather_weights([placed[n] for n in EARLY_NAMES], EARLY_KINDS)))
    late = [placed[n] for n in LATE_NAMES]

    w_mod_b = a["w_mod"][0].astype(BF16)
    c_ctx = a["c_ctx"].reshape(1, D_MODEL)
    b_loc = lax.dynamic_slice_in_dim(a["b_mod"], chip * MOD_COLS, MOD_COLS, 1)
    m_loc, s_b = _mod_fwd(c_all, c_ctx, w_mod_b, b_loc)
    mg = _all_gather8(m_loc, "gather_mod").reshape(8, MOD_ROWS, MOD_COLS)
    m_full = mg[0::2].transpose(1, 0, 2).reshape(MOD_ROWS, 6 * D_MODEL)
    mx = lax.dynamic_slice_in_dim(m_full, dev, 1, 0).reshape(6, D_MODEL)
    mc = m_full[8].reshape(6, D_MODEL)

    loss_part, grad_x, g, dmx, dmc, first_pairs, first_got = _local_step(a, wb, late, mx, mc, conv_w, ci)
    loss = lax.psum(loss_part, ("x", "y", "c"))

    dm_pair = jnp.concatenate([dmx.reshape(1, -1), dmc.reshape(1, -1), jnp.zeros((6, 6 * D_MODEL), F32)], axis=0)
    dm_all = _all_gather8(dm_pair, "gather_dmod").reshape(8, 8, 6 * D_MODEL)
    dm16, gb_mod = _mod_bwd_sum(dm_all)
    dm_loc = lax.dynamic_slice_in_dim(dm16, chip * MOD_COLS, MOD_COLS, 1)
    gw_mod, gcc = _mod_bwd_w(dm_loc, s_b, c_ctx, w_mod_b)

    small_parts = [g[n] for n in SMALL_NAMES] + [gcc[0]]
    small_shapes = [p.shape for p in small_parts]
    sp = _pack_rows(small_parts)
    tot = _sum_slots(_all_gather8(sp, "gather_small_grads").reshape(8, sp.shape[0], ROW), "sum_small_grads")
    small = dict(zip(SMALL_NAMES + ("c_ctx",), _unpack_rows(tot, small_shapes)))
    grads = {n: small[n].reshape(a[n].shape) for n in SMALL_NAMES if n != "conv_w"}
    grads["c_ctx"] = (0.5 * small["c_ctx"]).reshape(a["c_ctx"].shape)
    grads["conv_w"] = lax.dynamic_slice_in_dim(small["conv_w"], chip * (D_FF // N_CHIPS), D_FF // N_CHIPS, 1)[None]
    grads["b_mod"] = gb_mod
    grads["w_mod"] = gw_mod[None]

    last = [g[n] for n in LAST_GRADS]
    last_pairs = [_pair_sum(gf, rv, k, ci, "rs_pair_" + n)
                  for gf, rv, k, n in zip(last, _rs_sibling(last, LAST_KINDS, "rs_sibling_last"), LAST_KINDS, LAST_GRADS)]
    last_got = _rs_chips(last_pairs, LAST_KINDS)
    pos = jnp.stack([ci, chip])
    order = FIRST_GRADS + LAST_GRADS
    order_kinds = FIRST_KINDS + LAST_KINDS
    halves = [_sum_chips(p, t, k, pos, "rs_sum_" + n)
              for p, t, k, n in zip(first_pairs + last_pairs, list(first_got) + list(last_got), order_kinds, order)]
    for n, t in zip(order, _rs_back(halves, order_kinds)):
        grads[n] = t[None]

    delta, new_m, new_v = {}, {}, {}
    for n in BIG_NAMES + ("w_mod",):
        for dst, t in zip((delta, new_m, new_v), _adamw(a[n][0], grads[n][0], a["m_" + n][0], a["v_" + n][0], "adamw_" + n)):
            dst[n] = t[None]
    rest = [n for n in WEIGHT_NAMES if n not in BIG_NAMES and n != "w_mod"]
    shapes = [a[n].shape for n in rest]
    pr = lambda pre: _pack_rows([a[pre + n] for n in rest])
    for dst, t in zip((delta, new_m, new_v),
                      _adamw(pr(""), _pack_rows([grads[n] for n in rest]), pr("m_"), pr("v_"), "adamw_small")):
        dst.update(zip(rest, _unpack_rows(t, shapes)))

    return (loss, grad_x[None], *[grads[n] for n in WEIGHT_NAMES], *[delta[n] for n in WEIGHT_NAMES],
            *[new_m[n] for n in WEIGHT_NAMES], *[new_v[n] for n in WEIGHT_NAMES])


def kernel(x, c, ctx, c_ctx, w_mod, b_mod, norm1_w, w_in, s5_lambda_re_f, s5_lambda_im_f, s5_log_step_f, s5_lambda_re_b, s5_lambda_im_b, s5_log_step_b, s5_b_re, s5_b_im, s5_c_re, s5_c_im, s5_d, s5_w_glu, s5_b_glu, ret_log_decay_f, ret_log_decay_b, w_out, norm2_w, w_up, conv_w, conv_b, w_down, final_norm_w, loss_target, m_c_ctx, m_w_mod, m_b_mod, m_norm1_w, m_w_in, m_s5_lambda_re_f, m_s5_lambda_im_f, m_s5_log_step_f, m_s5_lambda_re_b, m_s5_lambda_im_b, m_s5_log_step_b, m_s5_b_re, m_s5_b_im, m_s5_c_re, m_s5_c_im, m_s5_d, m_s5_w_glu, m_s5_b_glu, m_ret_log_decay_f, m_ret_log_decay_b, m_w_out, m_norm2_w, m_w_up, m_conv_w, m_conv_b, m_w_down, m_final_norm_w, v_c_ctx, v_w_mod, v_b_mod, v_norm1_w, v_w_in, v_s5_lambda_re_f, v_s5_lambda_im_f, v_s5_log_step_f, v_s5_lambda_re_b, v_s5_lambda_im_b, v_s5_log_step_b, v_s5_b_re, v_s5_b_im, v_s5_c_re, v_s5_c_im, v_s5_d, v_s5_w_glu, v_s5_b_glu, v_ret_log_decay_f, v_ret_log_decay_b, v_w_out, v_norm2_w, v_w_up, v_conv_w, v_conv_b, v_w_down, v_final_norm_w):
    return _step(dict(locals()))
```

```python
import functools
import math

import jax
import jax.numpy as jnp
from jax import lax
from jax.experimental import pallas as pl
from jax.experimental.pallas import tpu as pltpu

F32 = jnp.float32
BF16 = jnp.bfloat16

D_MODEL = 1024
S5_WIDTH = 512
S5_GROUPS = 32
S5_GROUP = 16
S5_STATE = 64
RET_WIDTH = 512
RET_HEADS = 4
RET_DH = 128
RET_CHUNK = 256
GRID_W = 64
ROPE_THETA = 10000.0
D_FF = 2816
NORM_EPS = 1e-6
IN_COLS = S5_WIDTH + 4 * RET_WIDTH

S5_T = 16
S5_NB = 4
S5_BW = S5_T * 128
S5_SW = 8 * 2 * S5_STATE

ADAM_LR, ADAM_B1, ADAM_B2, ADAM_EPS, ADAM_WD, ADAM_STEP = 0.001, 0.9, 0.999, 1e-08, 0.01, 10

VMEM_LIMIT = 56 * 1024 * 1024
MM_TN_VMEM = 40 * 1024 * 1024
MESH_ID = pl.DeviceIdType.MESH


def _params(sem=None):
    return pltpu.CompilerParams(dimension_semantics=sem, vmem_limit_bytes=VMEM_LIMIT)


def _full(shape):
    n = len(shape)
    return pl.BlockSpec(shape, lambda *_: (0,) * n)


def _dot(a, b):
    return jnp.dot(a, b, preferred_element_type=F32)


def _dot_nt(a, b):
    return lax.dot_general(a, b, (((1,), (1,)), ((), ())), preferred_element_type=F32)


def _dot_tn(a, b):
    return lax.dot_general(a, b, (((0,), (0,)), ((), ())), preferred_element_type=F32)


def _dot_hi(a, b):
    return jnp.dot(a, b, preferred_element_type=F32, precision=lax.Precision.HIGHEST)


def _dot_nt_hi(a, b):
    return lax.dot_general(a, b, (((1,), (1,)), ((), ())), preferred_element_type=F32,
                           precision=lax.Precision.HIGHEST)


def _gelu(x):
    return 0.5 * x * (1.0 + jnp.tanh(0.7978845608028654 * (x + 0.044715 * (x * x * x))))


def _sigmoid(x):
    return 1.0 / (1.0 + jnp.exp(-x))


def _silu(x):
    return x * _sigmoid(x)


def _rms_mod(x, nw, sh, sc):
    r = lax.rsqrt(jnp.mean(x * x, axis=-1, keepdims=True) + NORM_EPS)
    return (x * r * nw) * (1.0 + sc) + sh


def _rms(x, nw):
    r = lax.rsqrt(jnp.mean(x * x, axis=-1, keepdims=True) + NORM_EPS)
    return x * r * nw


def _head_norm_gate(y, g):
    mu = jnp.mean(y, axis=-1, keepdims=True)
    yc = y - mu
    var = jnp.mean(yc * yc, axis=-1, keepdims=True)
    return _silu(g) * (yc * lax.rsqrt(var + NORM_EPS))


def _swap_pairs(t):
    lane = lax.broadcasted_iota(jnp.int32, t.shape, 1)
    return jnp.where(lane % 2 == 0, pltpu.roll(t, RET_DH - 1, 1), pltpu.roll(t, 1, 1))


def _rope(t, cos_t, sin_t):
    return t * cos_t + _swap_pairs(t) * sin_t


def _rope_t(dt, cos_t, sin_t):
    return dt * cos_t + _swap_pairs(dt * sin_t)


def _pick(n, prefs):
    for p in prefs:
        if n % p == 0:
            return p
    return n


def _mm_tn(a, b, *, name):
    m, k = a.shape
    n = b.shape[1]
    tn = _pick(n, (1408, 1024, 1280, 512))
    fits = lambda t: 2 * (2 * t * k + 2 * t * tn + 4 * k * tn) <= MM_TN_VMEM
    tm = _pick(m, [t for t in (2816, 2048, 1024, 768, 512, 256) if fits(t)] + [128])

    def body(a_ref, b_ref, o_ref):
        @pl.when(pl.program_id(1) == 0)
        def _():
            o_ref[...] = jnp.zeros_like(o_ref)
        o_ref[...] += _dot_tn(a_ref[...], b_ref[...])

    return pl.pallas_call(
        body, name=name, grid=(n // tn, m // tm),
        in_specs=[pl.BlockSpec((tm, k), lambda j, i: (i, 0)), pl.BlockSpec((tm, tn), lambda j, i: (i, j))],
        out_specs=pl.BlockSpec((k, tn), lambda j, i: (0, j)),
        out_shape=jax.ShapeDtypeStruct((k, n), F32),
        compiler_params=_params(("parallel", "arbitrary")),
    )(a, b)


TOK_TILE = 256


def _behind(step, last, copies):
    @pl.when(step == 0)
    def _():
        for cp in copies(with_arrivals=False)[0]:
            cp.start()

    @pl.when(step == last)
    def _():
        sends, arrivals = copies()
        for cp in arrivals:
            cp.wait_recv()
        for cp in sends:
            cp.wait_send()


def _norm_inproj(x, ctx, n1w, mod4, w_in_b, placed, kinds):
    l, lc = x.shape[0], ctx.shape[0]
    tm = TOK_TILE
    nct = lc // tm
    la = l + lc
    n = len(placed)
    shard_shapes = _gather_shard_shapes(placed, kinds)

    def body(x_ref, c_ref, nw_ref, mod_ref, w_ref, *rest):
        p_ref, h_ref = rest[n:n + 2]
        _behind(pl.program_id(0), la // tm - 1,
                functools.partial(_gather_chip_copies, rest[n + 2:2 * n + 2], kinds, shard_shapes, *rest[2 * n + 2:]))
        is_ctx = pl.program_id(0) < nct
        xt = jnp.where(is_ctx, c_ref[...], x_ref[...])
        sh = jnp.where(is_ctx, mod_ref[0:1, :], mod_ref[2:3, :])
        sc = jnp.where(is_ctx, mod_ref[1:2, :], mod_ref[3:4, :])
        hb = _rms_mod(xt, nw_ref[...], sh, sc).astype(BF16)
        h_ref[...] = hb
        p_ref[...] = _dot(hb, w_ref[...])

    return pl.pallas_call(
        body, name="norm_inproj", grid=(la // tm,),
        in_specs=[pl.BlockSpec((tm, D_MODEL), lambda i: (jnp.maximum(i - nct, 0), 0)),
                  pl.BlockSpec((tm, D_MODEL), lambda i: (jnp.minimum(i, nct - 1), 0)),
                  _full((1, D_MODEL)), _full((4, D_MODEL)), _full((D_MODEL, IN_COLS))] + [ANY] * n,
        out_specs=[pl.BlockSpec((tm, IN_COLS), lambda i: (i, 0)), pl.BlockSpec((tm, D_MODEL), lambda i: (i, 0))]
        + [ANY] * n,
        out_shape=[jax.ShapeDtypeStruct((la, IN_COLS), F32), jax.ShapeDtypeStruct((la, D_MODEL), BF16)]
        + [jax.ShapeDtypeStruct(p.shape, p.dtype) for p in placed],
        input_output_aliases={5 + a: 2 + a for a in range(n)},
        scratch_shapes=[pltpu.SemaphoreType.DMA((n, 3)), pltpu.SemaphoreType.DMA((n, 3))],
        compiler_params=_params(("arbitrary",)),
    )(x, ctx, n1w, mod4, w_in_b, *placed)


def _iota2(shape, dim):
    return lax.broadcasted_iota(jnp.int32, shape, dim)


def _group_mask(rows, cols, row_div, col_div):
    return jnp.where(_iota2((rows, cols), 0) // row_div == _iota2((rows, cols), 1) // col_div, 1.0, 0.0).astype(F32)


def _s5_gen_dir(lre, lim, lst, b_re, b_im, c_re, c_im):
    step = jnp.exp(lst)
    mag = jnp.exp(lre * step)
    ar = mag * jnp.cos(lim * step)
    ai = mag * jnp.sin(lim * step)
    den = lre * lre + lim * lim
    xr = ar - 1.0
    cr = (xr * lre + ai * lim) / den
    ci = (ai * lre - xr * lim) / den
    rexp = _group_mask(128, 8, S5_GROUP, 1)
    are, aie = _dot_hi(rexp, ar), _dot_hi(rexp, ai)
    cre, cie = _dot_hi(rexp, cr), _dot_hi(rexp, ci)
    bbr = cre * b_re - cie * b_im
    bbi = cre * b_im + cie * b_re
    gmask = _group_mask(128, 128, S5_GROUP, S5_GROUP)
    pr, pi = jnp.ones_like(are), jnp.zeros_like(are)
    xs, ys = [], []
    for t in range(S5_T + 1):
        if t < S5_T:
            xs.append(jnp.concatenate([bbr * pr - bbi * pi, bbr * pi + bbi * pr], axis=1))
        ys.append(jnp.concatenate([c_re * pr - c_im * pi, -(c_re * pi + c_im * pr)], axis=1))
        pr, pi = pr * are - pi * aie, pr * aie + pi * are
    gs = [_dot_nt_hi(x_t, ys[0]) * gmask for x_t in xs]
    r16, i16 = ar, ai
    for _ in range(4):
        r16, i16 = r16 * r16 - i16 * i16, 2.0 * r16 * i16
    return xs, ys, gs, jnp.concatenate([r16, i16], axis=1)


def _s5_expand(z):
    return jnp.concatenate([z] * 8, axis=1) * _group_mask(128, S5_SW, S5_GROUP, 128)


def _s5_contract(z):
    zm = z * _group_mask(128, S5_SW, S5_GROUP, 128)
    acc = zm[:, 0:128]
    for k in range(1, 8):
        acc = acc + zm[:, 128 * k:128 * (k + 1)]
    return acc


def _s5_param_specs():
    blk3 = lambda r, c: pl.BlockSpec((1, 1, r, c), lambda b, *_: (0, b, 0, 0))
    dir3 = lambda r, c: pl.BlockSpec((2, 1, r, c), lambda b, *_: (0, b, 0, 0))
    return [dir3(8, S5_STATE), dir3(8, S5_STATE), dir3(8, 1), blk3(128, S5_STATE), blk3(128, S5_STATE),
            blk3(128, S5_STATE), blk3(128, S5_STATE), blk3(1, 128)]


def _s5_gen(lre, lim, lst, b_re, b_im, c_re, c_im, dvec):
    def body(lre_ref, lim_ref, lst_ref, bre_ref, bim_ref, cre_ref, cim_ref, d_ref, gg_ref, xw_ref, yw_ref, a16_ref):
        eye = _group_mask(128, 128, 1, 1)
        g0 = eye * d_ref[0, 0]
        for dr in range(2):
            xs, ys, gs, a16 = _s5_gen_dir(lre_ref[dr, 0], lim_ref[dr, 0], lst_ref[dr, 0], bre_ref[0, 0],
                                          bim_ref[0, 0], cre_ref[0, 0], cim_ref[0, 0])
            a16_ref[0, dr] = a16
            for j in range(S5_T):
                xw_ref[0, dr, j] = xs[S5_T - 1 - j if dr == 0 else j]
                yw_ref[0, dr, j] = ys[j + 1 if dr == 0 else S5_T - j]
            g0 = g0 + gs[0]
            for t in range(1, S5_T):
                gg_ref[0, (S5_T - 1) + t if dr == 0 else (S5_T - 1) - t] = gs[t]
        gg_ref[0, S5_T - 1] = g0

    blk = pl.BlockSpec((1, 2, S5_T, 128, 128), lambda b: (b, 0, 0, 0, 0))
    return pl.pallas_call(
        body, name="s5_gen", grid=(S5_NB,),
        in_specs=_s5_param_specs(),
        out_specs=[pl.BlockSpec((1, 2 * S5_T - 1, 128, 128), lambda b: (b, 0, 0, 0)), blk, blk,
                   pl.BlockSpec((1, 2, 8, 128), lambda b: (b, 0, 0, 0))],
        out_shape=[jax.ShapeDtypeStruct((S5_NB, 2 * S5_T - 1, 128, 128), F32),
                   jax.ShapeDtypeStruct((S5_NB, 2, S5_T, 128, 128), F32),
                   jax.ShapeDtypeStruct((S5_NB, 2, S5_T, 128, 128), F32),
                   jax.ShapeDtypeStruct((S5_NB, 2, 8, 128), F32)],
        compiler_params=_params(("parallel",)),
    )(lre, lim, lst, b_re, b_im, c_re, c_im, dvec)


def _s5_fill_state_mat(w_scr, src_ref, dr):
    for j in range(S5_T):
        w_scr[128 * j:128 * (j + 1), :] = _s5_expand(src_ref[0, dr, j]).astype(BF16)


def _s5_fill_toeplitz(k_scr, gg_ref):
    for j in range(S5_T):
        for i in range(S5_T):
            k_scr[128 * j:128 * (j + 1), 128 * i:128 * (i + 1)] = gg_ref[0, i - j + (S5_T - 1)].astype(BF16)


S5_GEN_SPECS = [pl.BlockSpec((1, 2 * S5_T - 1, 128, 128), lambda b: (b, 0, 0, 0)),
                pl.BlockSpec((1, 2, S5_T, 128, 128), lambda b: (b, 0, 0, 0, 0))]


def _s5_gen_bwd(lre, lim, lst, b_re, b_im, c_re, c_im, dvec, dg, dx, dy, da16):
    def body(lre_ref, lim_ref, lst_ref, bre_ref, bim_ref, cre_ref, cim_ref, d_ref, dg_ref, dx_ref, dy_ref, da16_ref,
             glre_ref, glim_ref, glst_ref, gbre_ref, gbim_ref, gcre_ref, gcim_ref, gd_ref):
        eye = _group_mask(128, 128, 1, 1)
        gd_ref[0, 0] = jnp.sum(dg_ref[0, S5_T - 1] * eye, axis=0, keepdims=True)
        gb = [None, None, None, None]
        for dr in range(2):
            args = (lre_ref[dr, 0], lim_ref[dr, 0], lst_ref[dr, 0], bre_ref[0, 0], bim_ref[0, 0],
                    cre_ref[0, 0], cim_ref[0, 0])
            _, vjp = jax.vjp(_s5_gen_dir, *args)
            dxs = [dx_ref[0, dr, S5_T - 1 - t if dr == 0 else t] for t in range(S5_T)]
            dys = [jnp.zeros((128, 128), F32)] + [dy_ref[0, dr, t - 1 if dr == 0 else S5_T - t]
                                                  for t in range(1, S5_T + 1)]
            dgs = [dg_ref[0, (S5_T - 1) + t if dr == 0 else (S5_T - 1) - t] for t in range(S5_T)]
            g = vjp((dxs, dys, dgs, da16_ref[0, dr]))
            glre_ref[dr, 0] = g[0]
            glim_ref[dr, 0] = g[1]
            glst_ref[dr, 0] = g[2]
            for q in range(4):
                gb[q] = g[3 + q] if gb[q] is None else gb[q] + g[3 + q]
        gbre_ref[0, 0] = gb[0]
        gbim_ref[0, 0] = gb[1]
        gcre_ref[0, 0] = gb[2]
        gcim_ref[0, 0] = gb[3]

    shp = lambda a: jax.ShapeDtypeStruct(a.shape, F32)
    return pl.pallas_call(
        body, name="s5_gen_bwd", grid=(S5_NB,),
        in_specs=_s5_param_specs() + [
            pl.BlockSpec((1, 2 * S5_T - 1, 128, 128), lambda b: (b, 0, 0, 0)),
            pl.BlockSpec((1, 2, S5_T, 128, 128), lambda b: (b, 0, 0, 0, 0)),
            pl.BlockSpec((1, 2, S5_T, 128, 128), lambda b: (b, 0, 0, 0, 0)),
            pl.BlockSpec((1, 2, 8, 128), lambda b: (b, 0, 0, 0))],
        out_specs=_s5_param_specs(),
        out_shape=[shp(lre), shp(lim), shp(lst), shp(b_re), shp(b_im), shp(c_re), shp(c_im), shp(dvec)],
        compiler_params=_params(("parallel",)),
    )(lre, lim, lst, b_re, b_im, c_re, c_im, dvec, dg, dx, dy, da16)


def _s5_ucat(u_ref, lo=0, hi=S5_T):
    return jnp.concatenate([u_ref[:, j, :] for j in range(lo, hi)], axis=1).astype(BF16)


def _s5_put_groups(o_ref, dr, val):
    for gi in range(8):
        o_ref[dr, :, gi, :] = val[:, 128 * gi:128 * (gi + 1)]


def _s5_get_groups(s_ref, dr, n=8):
    return jnp.concatenate([s_ref[dr, :, gi, :] for gi in range(n)], axis=1).astype(BF16)


def _s5_to_states(u3, blocks, name):
    cn = u3.shape[0]

    def body(u_ref, b_ref, o_ref, w_scr):
        u = _s5_ucat(u_ref)
        for dr in range(2):
            _s5_fill_state_mat(w_scr, b_ref, dr)
            _s5_put_groups(o_ref, dr, _dot(u, w_scr[...]))

    return pl.pallas_call(
        body, name=name, grid=(S5_NB,),
        in_specs=[pl.BlockSpec((cn, S5_T, 128), lambda b: (0, 0, b)), S5_GEN_SPECS[1]],
        out_specs=pl.BlockSpec((2, cn, 8, 128), lambda b: (0, 0, b, 0)),
        out_shape=jax.ShapeDtypeStruct((2, cn, S5_GROUPS, 128), F32),
        scratch_shapes=[pltpu.VMEM((S5_BW, S5_SW), BF16)],
        compiler_params=_params(("parallel",)),
    )(u3, blocks)


def _s5_from_states(u3, gg, st, blocks, transposed, name):
    cn = u3.shape[0]

    def body(u_ref, g_ref, s_ref, b_ref, o_ref, k_scr, w_scr):
        u = _s5_ucat(u_ref)
        _s5_fill_toeplitz(k_scr, g_ref)
        y = _dot_nt(u, k_scr[...]) if transposed else _dot(u, k_scr[...])
        for dr in range(2):
            _s5_fill_state_mat(w_scr, b_ref, dr)
            y = y + _dot_nt(_s5_get_groups(s_ref, dr), w_scr[...])
        for i in range(S5_T):
            o_ref[:, i, :] = y[:, 128 * i:128 * (i + 1)]

    return pl.pallas_call(
        body, name=name, grid=(S5_NB,),
        in_specs=[pl.BlockSpec((cn, S5_T, 128), lambda b: (0, 0, b)), S5_GEN_SPECS[0],
                  pl.BlockSpec((2, cn, 8, 128), lambda b: (0, 0, b, 0)), S5_GEN_SPECS[1]],
        out_specs=pl.BlockSpec((cn, S5_T, 128), lambda b: (0, 0, b)),
        out_shape=jax.ShapeDtypeStruct((cn, S5_T, S5_WIDTH), F32),
        scratch_shapes=[pltpu.VMEM((S5_BW, S5_BW), BF16), pltpu.VMEM((S5_BW, S5_SW), BF16)],
        compiler_params=_params(("parallel",)),
    )(u3, gg, st, blocks)


def _s5_a_forms(a):
    ra = pltpu.roll(a, S5_STATE, 1)
    low = _iota2(a.shape, 1) < S5_STATE
    return jnp.where(low, a, ra), jnp.where(low, -ra, a)


def _s5_scan(sloc, a16, ncc):
    cn = sloc.shape[1]

    def body(s_ref, a_ref, h_ref):
        forms = [_s5_a_forms(a_ref[dr]) for dr in range(2)]

        def step(s, hs):
            out = []
            for dr in range(2):
                arr, aii = forms[dr]
                h, rh = hs[dr]
                c = s if dr == 0 else jnp.where(s < ncc, ncc - 1 - s, cn - 1 - (s - ncc))
                h_ref[dr, c] = h
                sc = s_ref[dr, c]
                out.append((h * arr + rh * aii + sc, rh * arr - h * aii + pltpu.roll(sc, S5_STATE, 1)))
            return tuple(out)

        zero = jnp.zeros((S5_GROUPS, 128), F32)
        lax.fori_loop(0, cn, step, ((zero, zero), (zero, zero)), unroll=4)

    return pl.pallas_call(
        body, name="s5_scan",
        out_shape=jax.ShapeDtypeStruct(sloc.shape, F32),
        compiler_params=_params(),
    )(sloc, a16)


def _s5_scan_bwd(e, hs, a16, ncc):
    cn = e.shape[1]

    def body(e_ref, h_ref, a_ref, ds_ref, da_ref):
        forms = [_s5_a_forms(a_ref[dr]) for dr in range(2)]
        low = _iota2((S5_GROUPS, 128), 1) < S5_STATE

        def step(s, carry):
            out = []
            r = cn - 1 - s
            for dr in range(2):
                arr, aii = forms[dr]
                g, rg, da = carry[dr]
                c = r if dr == 0 else jnp.where(r < ncc, ncc - 1 - r, cn - 1 - (r - ncc))
                ds_ref[dr, c] = g
                h = h_ref[dr, c]
                rh = pltpu.roll(h, S5_STATE, 1)
                da = da + jnp.where(low, g * h + rg * rh, g * rh - rg * h)
                ec = e_ref[dr, c]
                out.append((ec + g * arr - rg * aii, pltpu.roll(ec, S5_STATE, 1) + rg * arr + g * aii, da))
            return tuple(out)

        zero = jnp.zeros((S5_GROUPS, 128), F32)
        res = lax.fori_loop(0, cn, step, ((zero, zero, zero), (zero, zero, zero)), unroll=4)
        da_ref[0] = res[0][2]
        da_ref[1] = res[1][2]

    return pl.pallas_call(
        body, name="s5_scan_bwd",
        out_shape=[jax.ShapeDtypeStruct(e.shape, F32), jax.ShapeDtypeStruct((2, S5_GROUPS, 128), F32)],
        compiler_params=_params(),
    )(e, hs, a16)


def _s5_bwd_kb(p3, dy3):
    cn = p3.shape[0]
    half = S5_T // 2

    def body(u_ref, d_ref, o_ref):
        q = pl.program_id(1)

        @pl.when(q == 0)
        def _():
            o_ref[...] = jnp.zeros_like(o_ref)

        dk = _dot_tn(_s5_ucat(u_ref), _s5_ucat(d_ref, 0, half))
        for j in range(S5_T):
            for i in range(half):
                o_ref[0, half * q + i - j + (S5_T - 1)] += dk[128 * j:128 * (j + 1), 128 * i:128 * (i + 1)]

    return pl.pallas_call(
        body, name="s5_bwd_kb", grid=(S5_NB, 2),
        in_specs=[pl.BlockSpec((cn, S5_T, 128), lambda b, q: (0, 0, b)),
                  pl.BlockSpec((cn, half, 128), lambda b, q: (0, q, b))],
        out_specs=pl.BlockSpec((1, 2 * S5_T - 1, 128, 128), lambda b, q: (b, 0, 0, 0)),
        out_shape=jax.ShapeDtypeStruct((S5_NB, 2 * S5_T - 1, 128, 128), F32),
        compiler_params=_params(("parallel", "arbitrary")),
    )(p3, dy3)


def _s5_bwd_w(u3, st, name):
    cn = u3.shape[0]

    def body(u_ref, s_ref, w_ref):
        dw = _dot_tn(_s5_ucat(u_ref), _s5_get_groups(s_ref, 0))
        for j in range(S5_T):
            w_ref[0, 0, j] = _s5_contract(dw[128 * j:128 * (j + 1), :])

    return pl.pallas_call(
        body, name=name, grid=(S5_NB, 2),
        in_specs=[pl.BlockSpec((cn, S5_T, 128), lambda b, q: (0, 0, b)),
                  pl.BlockSpec((1, cn, 8, 128), lambda b, q: (q, 0, b, 0))],
        out_specs=pl.BlockSpec((1, 1, S5_T, 128, 128), lambda b, q: (b, q, 0, 0, 0)),
        out_shape=jax.ShapeDtypeStruct((S5_NB, 2, S5_T, 128, 128), F32),
        compiler_params=_params(("parallel", "parallel")),
    )(u3, st)


def _s5_glu(y_all, w_glu_b, b_glu, nct):
    la = y_all.shape[0]
    tm = TOK_TILE
    l = la - nct * tm

    def body(y_ref, w_ref, b_ref, o_ref):
        yg = _gelu(y_ref[...])
        z = _dot(yg.astype(BF16), w_ref[...]) + b_ref[...]
        o_ref[...] = (yg * _sigmoid(z)).astype(BF16)

    return pl.pallas_call(
        body, name="s5_glu", grid=(l // tm,),
        in_specs=[pl.BlockSpec((tm, S5_WIDTH), lambda i: (i + nct, 0)),
                  _full((S5_WIDTH, S5_WIDTH)), _full((1, S5_WIDTH))],
        out_specs=pl.BlockSpec((tm, S5_WIDTH), lambda i: (i, 0)),
        out_shape=jax.ShapeDtypeStruct((l, S5_WIDTH), BF16),
        compiler_params=_params(("parallel",)),
    )(y_all, w_glu_b, b_glu)


def _s5_glu_bwd(y_all, dmix, w_glu_b, b_glu, nct):
    la = y_all.shape[0]
    tm = TOK_TILE

    def body(y_ref, d_ref, w_ref, b_ref, dy_ref, gw_ref, gb_ref):
        i = pl.program_id(0)

        @pl.when(i == 0)
        def _():
            gw_ref[...] = jnp.zeros_like(gw_ref)
            gb_ref[...] = jnp.zeros_like(gb_ref)

        @pl.when(i < nct)
        def _():
            dy_ref[...] = jnp.zeros_like(dy_ref)

        @pl.when(i >= nct)
        def _():
            y = y_ref[...]
            yg, gelu_vjp = jax.vjp(_gelu, y)
            ygb = yg.astype(BF16)
            sg = _sigmoid(_dot(ygb, w_ref[...]) + b_ref[...])
            ds = d_ref[...]
            dz = ds * yg * sg * (1.0 - sg)
            dzb = dz.astype(BF16)
            dyg = ds * sg + _dot_nt(dzb, w_ref[...])
            dy_ref[...] = gelu_vjp(dyg)[0]
            gw_ref[...] += _dot_tn(ygb, dzb)
            gb_ref[...] += jnp.sum(dz, axis=0, keepdims=True)

    return pl.pallas_call(
        body, name="s5_glu_bwd", grid=(la // tm,),
        in_specs=[pl.BlockSpec((tm, S5_WIDTH), lambda i: (i, 0)),
                  pl.BlockSpec((tm, S5_WIDTH), lambda i: (jnp.maximum(i - nct, 0), 0)),
                  _full((S5_WIDTH, S5_WIDTH)), _full((1, S5_WIDTH))],
        out_specs=[pl.BlockSpec((tm, S5_WIDTH), lambda i: (i, 0)), _full((S5_WIDTH, S5_WIDTH)),
                   _full((1, S5_WIDTH))],
        out_shape=[jax.ShapeDtypeStruct((la, S5_WIDTH), F32), jax.ShapeDtypeStruct((S5_WIDTH, S5_WIDTH), F32),
                   jax.ShapeDtypeStruct((1, S5_WIDTH), F32)],
        compiler_params=_params(("arbitrary",)),
    )(y_all, dmix, w_glu_b, b_glu)


K_SCALE = RET_DH ** -0.5
Q_COL, K_COL, V_COL, G_COL = 4, 8, 12, 16


def _ret_chunk_of(step, ncc, nch, rev):
    if not rev:
        return step
    return jnp.where(step < ncc, ncc - 1 - step, nch - 1 - (step - ncc))


def _ret_decay(ld, rev):
    c = _iota2((RET_CHUNK, RET_CHUNK), 0).astype(F32)
    m = _iota2((RET_CHUNK, RET_CHUNK), 1).astype(F32)
    diff = (m - c) if rev else (c - m)
    keep = (diff > 0) if rev else (diff >= 0)
    expo = jnp.maximum(diff, 0.0)
    dm = jnp.where(keep, jnp.exp(ld * expo), 0.0)
    xi_e = (RET_CHUNK - c) if rev else (c + 1.0)
    zeta_e = c if rev else (RET_CHUNK - 1.0 - c)
    return dm, expo, jnp.exp(ld * xi_e), xi_e, jnp.exp(ld * zeta_e), zeta_e


RET_TABLES = 7


def _ret_tables(ld2):
    def body(ld_ref, t_ref):
        dr, h = pl.program_id(0), pl.program_id(1)
        ldh = ld_ref[dr, h]
        for rev in (False, True):
            @pl.when(dr == int(rev))
            def _(rev=rev):
                dm, expo, xi, xi_e, zeta, zeta_e = _ret_decay(ldh, rev)
                t_ref[0, 0, 0] = dm
                t_ref[0, 0, 1] = dm * expo
                t_ref[0, 0, 2] = xi
                t_ref[0, 0, 3] = xi * xi_e
                t_ref[0, 0, 4] = zeta
                t_ref[0, 0, 5] = zeta * zeta_e
                t_ref[0, 0, 6] = jnp.zeros_like(dm) + jnp.exp(ldh * RET_CHUNK)

    return pl.pallas_call(
        body, name="ret_tables", grid=(2, RET_HEADS),
        in_specs=[pl.BlockSpec(memory_space=pltpu.SMEM)],
        out_specs=pl.BlockSpec((1, 1, RET_TABLES, RET_CHUNK, RET_CHUNK), lambda d, h: (d, h, 0, 0, 0)),
        out_shape=jax.ShapeDtypeStruct((2, RET_HEADS, RET_TABLES, RET_CHUNK, RET_CHUNK), F32),
        compiler_params=_params(("parallel", "parallel")),
    )(ld2)


def _ret_specs(nch, ncc, rev, step_of):
    chunk = lambda n: _ret_chunk_of(step_of(n), ncc, nch, rev)
    cols = [pl.BlockSpec((RET_CHUNK, RET_WIDTH), functools.partial(lambda n, cb: (chunk(n), cb), cb=cb))
            for cb in (1, 2, 3)]
    tab = pl.BlockSpec((RET_CHUNK, RET_DH), lambda n: (chunk(n), 0))
    return cols + [tab, tab], pl.BlockSpec((RET_CHUNK, RET_WIDTH), lambda n: (chunk(n), 0))


def _ret_scan(p_all, cos_t, sin_t, tabs, ncc, placed, kinds):
    la = p_all.shape[0]
    nch = la // RET_CHUNK
    n = len(placed)
    shard_shapes = _gather_shard_shapes(placed, kinds)

    def body(t_ref, qf, kf, vf, cf, sf, qb, kb, vb, cb, sb, *rest):
        of_ref, ob_ref, ssf_ref, ssb_ref = rest[n:n + 4]
        s_scr, send_sems, recv_sems = rest[2 * n + 4:]
        step = pl.program_id(0)

        @pl.when(step == 0)
        def _():
            s_scr[...] = jnp.zeros_like(s_scr)
            for cp in _gather_chip_copies(rest[n + 4:2 * n + 4], kinds, shard_shapes, send_sems, recv_sems, False)[0]:
                cp.start()

        @pl.when(step == nch - 1)
        def _():
            sends, arrivals = _gather_chip_copies(rest[n + 4:2 * n + 4], kinds, shard_shapes, send_sems, recv_sems)
            for cp in arrivals:
                cp.wait_recv()
            for cp in sends:
                cp.wait_send()

        for dr, (q_ref, k_ref, v_ref, c_ref, n_ref, o_ref, ss_ref) in enumerate(
                ((qf, kf, vf, cf, sf, of_ref, ssf_ref), (qb, kb, vb, cb, sb, ob_ref, ssb_ref))):
            cs, sn = c_ref[...], n_ref[...]
            for h in range(RET_HEADS):
                sl = slice(RET_DH * h, RET_DH * (h + 1))
                dm, xi, zeta = t_ref[dr, h, 0], t_ref[dr, h, 2, :, 0:RET_DH], t_ref[dr, h, 4, :, 0:RET_DH]
                q = _rope(q_ref[:, sl], cs, sn)
                k = _rope(k_ref[:, sl] * K_SCALE, cs, sn)
                vh = v_ref[:, sl].astype(BF16)
                s = s_scr[dr, h]
                ss_ref[0, h] = s
                sc = (_dot_nt(q.astype(BF16), k.astype(BF16)) * dm).astype(BF16)
                o_ref[:, sl] = _dot(sc, vh) + _dot((q * xi).astype(BF16), s.astype(BF16))
                s_scr[dr, h] = t_ref[dr, h, 6, 0:RET_DH, 0:RET_DH] * s + _dot_tn((k * zeta).astype(BF16), vh)

    in_f, out_f = _ret_specs(nch, ncc, False, lambda n: n)
    in_b, out_b = _ret_specs(nch, ncc, True, lambda n: n)
    ss_spec = pl.BlockSpec((1, RET_HEADS, RET_DH, RET_DH), lambda n: (n, 0, 0, 0))
    o_shape = jax.ShapeDtypeStruct((la, RET_WIDTH), F32)
    ss_shape = jax.ShapeDtypeStruct((nch, RET_HEADS, RET_DH, RET_DH), F32)
    return pl.pallas_call(
        body, name="ret_scan", grid=(nch,),
        in_specs=[_full(tabs.shape)] + in_f + in_b + [ANY] * n,
        out_specs=[out_f, out_b, ss_spec, ss_spec] + [ANY] * n,
        out_shape=[o_shape, o_shape, ss_shape, ss_shape] + [jax.ShapeDtypeStruct(p.shape, p.dtype) for p in placed],
        input_output_aliases={11 + a: 4 + a for a in range(n)},
        scratch_shapes=[pltpu.VMEM((2, RET_HEADS, RET_DH, RET_DH), F32),
                        pltpu.SemaphoreType.DMA((n, 3)), pltpu.SemaphoreType.DMA((n, 3))],
        compiler_params=_params(("arbitrary",)),
    )(tabs, p_all, p_all, p_all, cos_t, sin_t, p_all, p_all, p_all, cos_t, sin_t, *placed)


def _ret_scan_bwd(p_all, cos_t, sin_t, tabs, ssf, ssb, dy_all, ncc):
    la = p_all.shape[0]
    nch = la // RET_CHUNK

    def body(t_ref, qf, kf, vf, cf, sf, dof, ssf_ref, qb, kb, vb, cb, sb, dob_, ssb_ref,
             dqf, dkf, dvf, dqb, dkb, dvb, dld_ref, ds_scr):
        @pl.when(pl.program_id(0) == 0)
        def _():
            ds_scr[...] = jnp.zeros_like(ds_scr)
            dld_ref[...] = jnp.zeros_like(dld_ref)

        for dr, (q_ref, k_ref, v_ref, c_ref, n_ref, do_ref, ss_ref, dq_ref, dk_ref, dv_ref) in enumerate(
                ((qf, kf, vf, cf, sf, dof, ssf_ref, dqf, dkf, dvf), (qb, kb, vb, cb, sb, dob_, ssb_ref, dqb, dkb, dvb))):
            cs, sn = c_ref[...], n_ref[...]
            for h in range(RET_HEADS):
                sl = slice(RET_DH * h, RET_DH * (h + 1))
                dm, dm_d = t_ref[dr, h, 0], t_ref[dr, h, 1]
                xi, xi_d, zeta, zeta_d = [t_ref[dr, h, t, :, 0:RET_DH] for t in (2, 3, 4, 5)]
                gc = t_ref[dr, h, 6, 0:RET_DH, 0:RET_DH]
                q = _rope(q_ref[:, sl], cs, sn)
                k = _rope(k_ref[:, sl] * K_SCALE, cs, sn)
                q16, k16, v16 = q.astype(BF16), k.astype(BF16), v_ref[:, sl].astype(BF16)
                s = ss_ref[0, h]
                s16 = s.astype(BF16)
                ds_in = ds_scr[dr, h]
                ds16 = ds_in.astype(BF16)
                do16 = do_ref[:, sl].astype(BF16)
                qk = _dot_nt(q16, k16)
                dsv = _dot_nt(do16, v16)
                dsc = (dsv * dm).astype(BF16)
                sc16 = (qk * dm).astype(BF16)
                dos = _dot_nt(do16, s16)
                vds = _dot_nt(v16, ds16)
                dq_ref[:, sl] = _dot(dsc, k16) + dos * xi
                dk_ref[:, sl] = _dot_tn(dsc, q16) + vds * zeta
                dv_ref[:, sl] = _dot_tn(sc16, do16) + _dot((k * zeta).astype(BF16), ds16)
                ds_scr[dr, h] = _dot_tn((q * xi).astype(BF16), do16) + gc * ds_in
                dld = (jnp.sum(dsv * qk * dm_d) + jnp.sum(q * dos * xi_d + k * vds * zeta_d)
                       + RET_CHUNK * jnp.sum(gc * s * ds_in))
                dld_ref[dr, h] += dld

    back = lambda n: nch - 1 - n
    in_f, out_f = _ret_specs(nch, ncc, False, back)
    in_b, out_b = _ret_specs(nch, ncc, True, back)
    ss_spec = pl.BlockSpec((1, RET_HEADS, RET_DH, RET_DH), lambda n: (nch - 1 - n, 0, 0, 0))
    shp = jax.ShapeDtypeStruct((la, RET_WIDTH), F32)
    return pl.pallas_call(
        body, name="ret_scan_bwd", grid=(nch,),
        in_specs=[_full(tabs.shape)] + in_f + [out_f, ss_spec] + in_b + [out_b, ss_spec],
        out_specs=[out_f, out_f, out_f, out_b, out_b, out_b, _full((2, RET_HEADS, 8, 128))],
        out_shape=[shp] * 6 + [jax.ShapeDtypeStruct((2, RET_HEADS, 8, 128), F32)],
        scratch_shapes=[pltpu.VMEM((2, RET_HEADS, RET_DH, RET_DH), F32)],
        compiler_params=_params(("arbitrary",)),
    )(tabs, p_all, p_all, p_all, cos_t, sin_t, dy_all, ssf, p_all, p_all, p_all, cos_t, sin_t, dy_all, ssb)


def _ret_gate(of, ob, p_all, nct):
    la = of.shape[0]
    tm = TOK_TILE
    l = la - nct * tm

    def body(of_ref, ob_ref, g_ref, r_ref, y_ref):
        y = of_ref[...] + ob_ref[...]
        y_ref[...] = y
        for h in range(RET_HEADS):
            sl = slice(RET_DH * h, RET_DH * (h + 1))
            r_ref[:, sl] = _head_norm_gate(y[:, sl], g_ref[:, sl]).astype(BF16)

    row = pl.BlockSpec((tm, RET_WIDTH), lambda i: (i + nct, 0))
    out = pl.BlockSpec((tm, RET_WIDTH), lambda i: (i, 0))
    return pl.pallas_call(
        body, name="ret_gate", grid=(l // tm,),
        in_specs=[row, row, pl.BlockSpec((tm, RET_WIDTH), lambda i: (i + nct, G_COL // 4))],
        out_specs=[out, out],
        out_shape=[jax.ShapeDtypeStruct((l, RET_WIDTH), BF16), jax.ShapeDtypeStruct((l, RET_WIDTH), F32)],
        compiler_params=_params(("parallel",)),
    )(of, ob, p_all)


def _ret_gate_bwd(y_ret, p_all, dmix, nct):
    la = p_all.shape[0]
    tm = TOK_TILE

    def body(y_ref, g_ref, d_ref, dy_ref, dg_ref):
        i = pl.program_id(0)

        @pl.when(i < nct)
        def _():
            dy_ref[...] = jnp.zeros_like(dy_ref)
            dg_ref[...] = jnp.zeros_like(dg_ref)

        @pl.when(i >= nct)
        def _():
            for h in range(RET_HEADS):
                sl = slice(RET_DH * h, RET_DH * (h + 1))
                _, vjp = jax.vjp(_head_norm_gate, y_ref[:, sl], g_ref[:, sl])
                dy, dg = vjp(d_ref[:, sl])
                dy_ref[:, sl] = dy
                dg_ref[:, sl] = dg

    xrow = lambda cb: pl.BlockSpec((tm, RET_WIDTH), lambda i: (jnp.maximum(i - nct, 0), cb))
    out = pl.BlockSpec((tm, RET_WIDTH), lambda i: (i, 0))
    shp = jax.ShapeDtypeStruct((la, RET_WIDTH), F32)
    return pl.pallas_call(
        body, name="ret_gate_bwd", grid=(la // tm,),
        in_specs=[xrow(0), pl.BlockSpec((tm, RET_WIDTH), lambda i: (i, G_COL // 4)), xrow(1)],
        out_specs=[out, out], out_shape=[shp, shp],
        compiler_params=_params(("parallel",)),
    )(y_ret, p_all, dmix)


def _in_bwd(dqf, dkf, dvf, dqb, dkb, dvb, du, dg, cos_t, sin_t, w_in_b, x, ctx, n1w, mod4, dx1):
    l, lc = x.shape[0], ctx.shape[0]
    la = l + lc
    tm = TOK_TILE
    nct = lc // tm

    def body(dqf_ref, dkf_ref, dvf_ref, dqb_ref, dkb_ref, dvb_ref, du_ref, dg_ref, cos_ref, sin_ref,
             w_ref, x_ref, c_ref, nw_ref, mod_ref, dx1_ref, dp_ref, gx_ref, acc_ref):
        i = pl.program_id(0)
        is_ctx = i < nct

        @pl.when(i == 0)
        def _():
            acc_ref[...] = jnp.zeros_like(acc_ref)

        cs, sn = cos_ref[...], sin_ref[...]
        dp_ref[:, 0:S5_WIDTH] = du_ref[...].astype(BF16)
        for h in range(RET_HEADS):
            sl = slice(RET_DH * h, RET_DH * (h + 1))
            dq = _rope_t(dqf_ref[:, sl] + dqb_ref[:, sl], cs, sn)
            dk = _rope_t(dkf_ref[:, sl] + dkb_ref[:, sl], cs, sn) * K_SCALE
            dp_ref[:, 128 * (Q_COL + h):128 * (Q_COL + h + 1)] = dq.astype(BF16)
            dp_ref[:, 128 * (K_COL + h):128 * (K_COL + h + 1)] = dk.astype(BF16)
        dp_ref[:, 128 * V_COL:128 * G_COL] = (dvf_ref[...] + dvb_ref[...]).astype(BF16)
        dp_ref[:, 128 * G_COL:IN_COLS] = dg_ref[...].astype(BF16)

        dh1 = _dot_nt(dp_ref[...], w_ref[...])
        xt = jnp.where(is_ctx, c_ref[...], x_ref[...])
        sh = jnp.where(is_ctx, mod_ref[0:1, :], mod_ref[2:3, :])
        sc = jnp.where(is_ctx, mod_ref[1:2, :], mod_ref[3:4, :])
        _, vjp = jax.vjp(_rms_mod, xt, nw_ref[...], sh, sc)
        dx, dnw, dsh, dsc = vjp(dh1)
        gx_ref[...] = dx + dx1_ref[...]
        cf = jnp.where(is_ctx, 1.0, 0.0)
        acc_ref[0:1, :] += dnw
        acc_ref[1:2, :] += cf * dsh
        acc_ref[2:3, :] += cf * dsc
        acc_ref[3:4, :] += (1.0 - cf) * dsh
        acc_ref[4:5, :] += (1.0 - cf) * dsc

    row = pl.BlockSpec((tm, RET_WIDTH), lambda i: (i, 0))
    tab = pl.BlockSpec((tm, RET_DH), lambda i: (i, 0))
    xrow = pl.BlockSpec((tm, D_MODEL), lambda i: (jnp.maximum(i - nct, 0), 0))
    return pl.pallas_call(
        body, name="in_bwd", grid=(la // tm,),
        in_specs=[row] * 8 + [tab, tab, _full((D_MODEL, IN_COLS)), xrow,
                              pl.BlockSpec((tm, D_MODEL), lambda i: (jnp.minimum(i, nct - 1), 0)),
                              _full((1, D_MODEL)), _full((4, D_MODEL)), xrow],
        out_specs=[pl.BlockSpec((tm, IN_COLS), lambda i: (i, 0)), xrow, _full((8, D_MODEL))],
        out_shape=[jax.ShapeDtypeStruct((la, IN_COLS), BF16), jax.ShapeDtypeStruct((l, D_MODEL), F32),
                   jax.ShapeDtypeStruct((8, D_MODEL), F32)],
        compiler_params=_params(("arbitrary",)),
    )(dqf, dkf, dvf, dqb, dkb, dvb, du, dg, cos_t, sin_t, w_in_b, x, ctx, n1w, mod4, dx1)


def _outproj_up(x, s5x, retx, w_out_b, mod3, n2w, w_up_b):
    l = x.shape[0]
    tm = TOK_TILE

    def body(x_ref, s_ref, r_ref, wo_ref, mod_ref, nw_ref, wu_ref, x1_ref, mix_ref, h2_ref, up_ref):
        mix = _dot(s_ref[...], wo_ref[0:S5_WIDTH, :]) + _dot(r_ref[...], wo_ref[S5_WIDTH:D_MODEL, :])
        mix_ref[...] = mix
        x1 = x_ref[...] + mod_ref[0:1, :] * mix
        x1_ref[...] = x1
        h2 = _rms_mod(x1, nw_ref[...], mod_ref[1:2, :], mod_ref[2:3, :]).astype(BF16)
        h2_ref[...] = h2
        up_ref[...] = _dot(h2, wu_ref[...])

    row = lambda w: pl.BlockSpec((tm, w), lambda i: (i, 0))
    return pl.pallas_call(
        body, name="outproj_up", grid=(l // tm,),
        in_specs=[row(D_MODEL), row(S5_WIDTH), row(RET_WIDTH), _full((D_MODEL, D_MODEL)), _full((3, D_MODEL)),
                  _full((1, D_MODEL)), _full((D_MODEL, 2 * D_FF))],
        out_specs=[row(D_MODEL), row(D_MODEL), row(D_MODEL), row(2 * D_FF)],
        out_shape=[jax.ShapeDtypeStruct((l, D_MODEL), F32), jax.ShapeDtypeStruct((l, D_MODEL), F32),
                   jax.ShapeDtypeStruct((l, D_MODEL), BF16), jax.ShapeDtypeStruct((l, 2 * D_FF), F32)],
        compiler_params=_params(("parallel",)),
    )(x, s5x, retx, w_out_b, mod3, n2w, w_up_b)


HALO = 8


def _conv_taps(g, prev_row, next_row):
    t = g.shape[0]
    r = _iota2(g.shape, 0)
    gprev = jnp.where(r == 0, prev_row, pltpu.roll(g, 1, 0))
    gnext = jnp.where(r == t - 1, next_row, pltpu.roll(g, t - 1, 0))
    return gprev, gnext


def _ffn_loss(up, x1, conv_w, conv_b, w_down_b, gate, fnw, tgt):
    l = x1.shape[0]
    tm = TOK_TILE
    nt = l // tm
    hb = tm // HALO

    def body(up_a, up_g, hp_ref, hn_ref, x1_ref, cw_ref, cb_ref, wd_ref, gate_ref, fn_ref, tgt_ref,
             act_ref, dx2_ref, ddn_ref, dact_ref, acc_ref, act_scr):
        i = pl.program_id(0)
        front = jnp.minimum(i, nt - 1)

        @pl.when(i == 0)
        def _():
            acc_ref[...] = jnp.zeros_like(acc_ref)
            act_scr[...] = jnp.zeros_like(act_scr)

        g = up_g[...]
        prev_row = jnp.where(front == 0, 0.0, hp_ref[HALO - 1:HALO, :])
        next_row = jnp.where(front == nt - 1, 0.0, hn_ref[0:1, :])
        gprev, gnext = _conv_taps(g, prev_row, next_row)
        gc = cb_ref[...] + gprev * cw_ref[0:1, :] + g * cw_ref[1:2, :] + gnext * cw_ref[2:3, :]
        act_prev = act_scr[(i + 1) % 2]
        act = (_gelu(gc) * up_a[...]).astype(BF16)
        act_ref[...] = act
        act_scr[i % 2] = act

        dn = _dot(act_prev, wd_ref[...])
        x2 = x1_ref[...] + gate_ref[...] * dn
        y, vjp = jax.vjp(_rms, x2, fn_ref[...])
        err = y - tgt_ref[...]
        dx2, dfn = vjp(err * (1.0 / D_MODEL))
        dx2_ref[...] = dx2
        ddn = (dx2 * gate_ref[...]).astype(BF16)
        ddn_ref[...] = ddn
        dact_ref[...] = _dot_nt(ddn, wd_ref[...])
        live = i > 0
        acc_ref[0:1, :] += jnp.where(live, dfn, 0.0)
        acc_ref[1:2, :] += jnp.where(live, jnp.sum(dx2 * dn, axis=0, keepdims=True), 0.0)
        acc_ref[2:3, :] += jnp.where(live, (0.5 / D_MODEL) * jnp.sum(err * err), 0.0)

    ahead = lambda w, cb=0: pl.BlockSpec((tm, w), lambda i: (jnp.minimum(i, nt - 1), cb))
    behind = lambda w: pl.BlockSpec((tm, w), lambda i: (jnp.maximum(i - 1, 0), 0))
    last = l // HALO - 1
    return pl.pallas_call(
        body, name="ffn_loss", grid=(nt + 1,),
        in_specs=[ahead(D_FF, 0), ahead(D_FF, 1),
                  pl.BlockSpec((HALO, D_FF), lambda i: (jnp.maximum(jnp.minimum(i, nt - 1) * hb - 1, 0), 1)),
                  pl.BlockSpec((HALO, D_FF), lambda i: (jnp.minimum((jnp.minimum(i, nt - 1) + 1) * hb, last), 1)),
                  behind(D_MODEL), _full((3, D_FF)), _full((1, D_FF)), _full((D_FF, D_MODEL)),
                  _full((1, D_MODEL)), _full((1, D_MODEL)), behind(D_MODEL)],
        out_specs=[ahead(D_FF), behind(D_MODEL), behind(D_MODEL), behind(D_FF), _full((8, D_MODEL))],
        out_shape=[jax.ShapeDtypeStruct((l, D_FF), BF16), jax.ShapeDtypeStruct((l, D_MODEL), F32),
                   jax.ShapeDtypeStruct((l, D_MODEL), BF16), jax.ShapeDtypeStruct((l, D_FF), F32),
                   jax.ShapeDtypeStruct((8, D_MODEL), F32)],
        scratch_shapes=[pltpu.VMEM((2, tm, D_FF), BF16)],
        compiler_params=_params(("arbitrary",)),
    )(up, up, up, up, x1, conv_w, conv_b, w_down_b, gate, fnw, tgt)


def _convglu_bwd(up, dact, conv_w, conv_b):
    l = up.shape[0]
    tm = 128
    nt = l // tm
    hb = tm // HALO
    te = tm + 2 * HALO

    def body(a_ref, ap_ref, an_ref, g_ref, gp_ref, gn_ref, d_ref, dp_ref, dn_ref, cw_ref, cb_ref,
             dup_ref, acc_ref):
        i = pl.program_id(0)

        @pl.when(i == 0)
        def _():
            acc_ref[...] = jnp.zeros_like(acc_ref)

        row = _iota2((te, D_FF), 0) + (i * tm - HALO)
        valid = (row >= 0) & (row < l)

        def ext(p, c, n):
            return jnp.where(valid, jnp.concatenate([p[...], c[...], n[...]], axis=0), 0.0)

        ae, ge, de = ext(ap_ref, a_ref, an_ref), ext(gp_ref, g_ref, gn_ref), ext(dp_ref, d_ref, dn_ref)
        gprev = pltpu.roll(ge, 1, 0)
        gnext = pltpu.roll(ge, te - 1, 0)
        w0, w1, w2 = cw_ref[0:1, :], cw_ref[1:2, :], cw_ref[2:3, :]
        gce = cb_ref[...] + gprev * w0 + ge * w1 + gnext * w2
        _, vjp = jax.vjp(lambda a, gc: _gelu(gc) * a, ae, gce)
        dae, dgce = vjp(de)
        dge = dgce * w1 + pltpu.roll(dgce, te - 1, 0) * w0 + pltpu.roll(dgce, 1, 0) * w2
        mid = slice(HALO, HALO + tm)
        dup_ref[:, 0:D_FF] = dae[mid].astype(BF16)
        dup_ref[:, D_FF:2 * D_FF] = dge[mid].astype(BF16)
        dgc = dgce[mid]
        acc_ref[0:1, :] += jnp.sum(dgc * gprev[mid], axis=0, keepdims=True)
        acc_ref[1:2, :] += jnp.sum(dgc * ge[mid], axis=0, keepdims=True)
        acc_ref[2:3, :] += jnp.sum(dgc * gnext[mid], axis=0, keepdims=True)
        acc_ref[3:4, :] += jnp.sum(dgc, axis=0, keepdims=True)

    last = l // HALO - 1

    def trio(cb):
        return [pl.BlockSpec((tm, D_FF), lambda i: (i, cb)),
                pl.BlockSpec((HALO, D_FF), lambda i: (jnp.maximum(i * hb - 1, 0), cb)),
                pl.BlockSpec((HALO, D_FF), lambda i: (jnp.minimum((i + 1) * hb, last), cb))]

    return pl.pallas_call(
        body, name="convglu_bwd", grid=(nt,),
        in_specs=trio(0) + trio(1) + trio(0) + [_full((3, D_FF)), _full((1, D_FF))],
        out_specs=[pl.BlockSpec((tm, 2 * D_FF), lambda i: (i, 0)), _full((8, D_FF))],
        out_shape=[jax.ShapeDtypeStruct((l, 2 * D_FF), BF16), jax.ShapeDtypeStruct((8, D_FF), F32)],
        compiler_params=_params(("arbitrary",)),
    )(up, up, up, up, up, up, dact, dact, dact, conv_w, conv_b)


def _up_bwd(dup, w_up_b, w_out_b, x1, dx2, mix, mod3, n2w, pairs, kinds):
    l = x1.shape[0]
    tm = TOK_TILE
    nt = l // tm
    n = len(pairs)
    shapes = _rs_slot_shapes(pairs, kinds)

    def body(dup_ref, wu_ref, wo_ref, x1_ref, dx2_ref, mix_ref, mod_ref, nw_ref, *rest):
        dx1_ref, dmixb_ref, dmix_ref, acc_ref = rest[n:n + 4]
        exchange = functools.partial(_rs_chip_copies, rest[:n], rest[n + 4:2 * n + 4], kinds, shapes, *rest[2 * n + 4:])
        step = pl.program_id(0)

        @pl.when(step == 0)
        def _():
            acc_ref[...] = jnp.zeros_like(acc_ref)
            for cp in exchange(with_arrivals=False)[0]:
                cp.start()

        @pl.when(step == nt - 1)
        def _():
            sends, arrivals = exchange()
            for cp in arrivals:
                cp.wait_recv()
            for cp in sends:
                cp.wait_send()

        dh2 = _dot_nt(dup_ref[...], wu_ref[...])
        _, vjp = jax.vjp(_rms_mod, x1_ref[...], nw_ref[...], mod_ref[1:2, :], mod_ref[2:3, :])
        dx, dnw, dsh, dsc = vjp(dh2)
        dx1 = dx + dx2_ref[...]
        dx1_ref[...] = dx1
        dmixb = (dx1 * mod_ref[0:1, :]).astype(BF16)
        dmixb_ref[...] = dmixb
        dmix_ref[...] = _dot_nt(dmixb, wo_ref[...])
        acc_ref[0:1, :] += dnw
        acc_ref[1:2, :] += jnp.sum(dx1 * mix_ref[...], axis=0, keepdims=True)
        acc_ref[2:3, :] += dsh
        acc_ref[3:4, :] += dsc

    row = pl.BlockSpec((tm, D_MODEL), lambda i: (i, 0))
    return pl.pallas_call(
        body, name="up_bwd", grid=(nt,),
        in_specs=[pl.BlockSpec((tm, 2 * D_FF), lambda i: (i, 0)), _full((D_MODEL, 2 * D_FF)),
                  _full((D_MODEL, D_MODEL)), row, row, row, _full((3, D_MODEL)), _full((1, D_MODEL))] + [ANY] * n,
        out_specs=[row, row, row, _full((8, D_MODEL))] + [ANY] * n,
        out_shape=[jax.ShapeDtypeStruct((l, D_MODEL), F32), jax.ShapeDtypeStruct((l, D_MODEL), BF16),
                   jax.ShapeDtypeStruct((l, D_MODEL), F32), jax.ShapeDtypeStruct((8, D_MODEL), F32)]
        + [jax.ShapeDtypeStruct((4,) + s, p.dtype) for s, p in zip(shapes, pairs)],
        scratch_shapes=[pltpu.SemaphoreType.DMA((n, 3)), pltpu.SemaphoreType.DMA((n, 3))],
        compiler_params=_params(("arbitrary",)),
    )(dup, w_up_b, w_out_b, x1, dx2, mix, mod3, n2w, *pairs)


MOD_ROWS = 16
MOD_COLS = 6 * D_MODEL // 4


def _mod_fwd(c_all, c_ctx, w_mod_b, b_loc):
    def body(c_ref, cc_ref, w_ref, b_ref, m_ref, s_ref):
        cond = jnp.concatenate([c_ref[...], jnp.broadcast_to(cc_ref[...], (8, D_MODEL))], axis=0)
        s = _silu(cond).astype(BF16)
        s_ref[...] = s
        m_ref[...] = _dot(s, w_ref[...]) + b_ref[...]

    return pl.pallas_call(
        body, name="mod_fwd",
        out_shape=[jax.ShapeDtypeStruct((MOD_ROWS, MOD_COLS), F32), jax.ShapeDtypeStruct((MOD_ROWS, D_MODEL), BF16)],
        compiler_params=_params(),
    )(c_all, c_ctx, w_mod_b, b_loc)


def _mod_bwd_sum(dm_all):
    def body(d_ref, dm_ref, gb_ref):
        rows = [d_ref[k, 0:1, :] for k in range(8)]
        ctx_sum = d_ref[0, 1:2, :]
        for k in range(1, 8):
            ctx_sum = ctx_sum + d_ref[k, 1:2, :]
        gb = ctx_sum
        for k in range(8):
            gb = gb + rows[k]
        gb_ref[...] = gb
        dm_ref[...] = jnp.concatenate(rows + [ctx_sum] + [jnp.zeros((7, 6 * D_MODEL), F32)], axis=0)

    return pl.pallas_call(
        body, name="mod_bwd_sum",
        out_shape=[jax.ShapeDtypeStruct((MOD_ROWS, 6 * D_MODEL), F32), jax.ShapeDtypeStruct((1, 6 * D_MODEL), F32)],
        compiler_params=_params(),
    )(dm_all)


def _mod_bwd_w(dm_loc, s_b, c_ctx, w_mod_b):
    def body(d_ref, s_ref, cc_ref, w_ref, gw_ref, gc_ref):
        db = d_ref[...].astype(BF16)
        gw_ref[...] = _dot_tn(s_ref[...], db)
        ds = _dot_nt(db, w_ref[...])
        _, vjp = jax.vjp(_silu, cc_ref[...])
        gc_ref[...] = jnp.broadcast_to(vjp(ds[8:9, :])[0], (8, D_MODEL))

    return pl.pallas_call(
        body, name="mod_bwd_w",
        out_shape=[jax.ShapeDtypeStruct((D_MODEL, MOD_COLS), F32), jax.ShapeDtypeStruct((8, D_MODEL), F32)],
        compiler_params=_params(),
    )(dm_loc, s_b, c_ctx, w_mod_b)


def _adamw(w, g, m, v, name):
    r, c = w.shape
    tr = _pick(r, (256, 128, 64, 32, 16, 8))
    bc1 = 1.0 - ADAM_B1 ** ADAM_STEP
    bc2 = 1.0 - ADAM_B2 ** ADAM_STEP

    def body(w_ref, g_ref, m_ref, v_ref, d_ref, nm_ref, nv_ref):
        gg = g_ref[...]
        nm = ADAM_B1 * m_ref[...] + (1.0 - ADAM_B1) * gg
        nv = ADAM_B2 * v_ref[...] + (1.0 - ADAM_B2) * (gg * gg)
        nm_ref[...] = nm
        nv_ref[...] = nv
        d_ref[...] = -ADAM_LR * ((nm / bc1) / (jnp.sqrt(nv / bc2) + ADAM_EPS) + ADAM_WD * w_ref[...])

    blk = pl.BlockSpec((tr, c), lambda i: (i, 0))
    shp = jax.ShapeDtypeStruct((r, c), F32)
    return pl.pallas_call(
        body, name=name, grid=(r // tr,), in_specs=[blk] * 4, out_specs=[blk] * 3, out_shape=[shp] * 3,
        compiler_params=_params(("parallel",)),
    )(w, g, m, v)


def _sum_slots(a, name):
    n, r, c = a.shape
    tr = _pick(r, (376, 256, 208, 128, 64, 32, 16, 8))

    def body(a_ref, o_ref):
        acc = a_ref[0].astype(F32)
        for k in range(1, n):
            acc = acc + a_ref[k].astype(F32)
        o_ref[...] = acc

    return pl.pallas_call(
        body, name=name, grid=(r // tr,),
        in_specs=[pl.BlockSpec((n, tr, c), lambda i: (0, i, 0))],
        out_specs=pl.BlockSpec((tr, c), lambda i: (i, 0)),
        out_shape=jax.ShapeDtypeStruct((r, c), F32),
        compiler_params=_params(("parallel",)),
    )(a)


def _mesh_pos():
    return lax.axis_index("x"), lax.axis_index("y"), lax.axis_index("c")


def _all_gather8(v, name):
    m_per, n = v.shape

    def body(x_ref, out_ref, send_sems, recv_sems, local_sem):
        x, y, c = _mesh_pos()
        me, sibling = (x, y, c), (x, y, 1 - c)
        chips = [(1 - x, y), (x, 1 - y), (1 - x, 1 - y)]

        def rows(px, py, pc):
            return out_ref.at[pl.ds((4 * px + 2 * py + pc) * m_per, m_per), :]

        def copy(k, block, to, src=None):
            return pltpu.make_async_remote_copy(
                src_ref=rows(*block) if src is None else src, dst_ref=rows(*block),
                send_sem=send_sems.at[k], recv_sem=recv_sems.at[k], device_id=to, device_id_type=MESH_ID)

        mine = pltpu.make_async_copy(x_ref, rows(*me), local_sem)
        mine.start()
        first = [copy(0, me, sibling, src=x_ref)]
        first += [copy(1 + j, me, (*chip, c), src=x_ref) for j, chip in enumerate(chips)]
        for cp in first:
            cp.start()
        passed = [copy(4 + j, (*chip, c), sibling) for j, chip in enumerate(chips)]
        for j, chip in enumerate(chips):
            copy(1 + j, (*chip, c), me).wait_recv()
            passed[j].start()
        copy(0, sibling, me).wait_recv()
        for j, chip in enumerate(chips):
            copy(4 + j, (*chip, 1 - c), me).wait_recv()
        for cp in first + passed:
            cp.wait_send()
        mine.wait()

    return pl.pallas_call(
        body, name=name,
        out_shape=jax.ShapeDtypeStruct((8 * m_per, n), v.dtype),
        in_specs=[pl.BlockSpec(memory_space=pltpu.VMEM)],
        out_specs=pl.BlockSpec(memory_space=pltpu.VMEM),
        scratch_shapes=[pltpu.SemaphoreType.DMA((7,)), pltpu.SemaphoreType.DMA((7,)), pltpu.SemaphoreType.DMA],
        compiler_params=_params(),
    )(v)


ANY = pl.BlockSpec(memory_space=pl.ANY)
PEER_CHIPS = lambda x, y: [(x, 1 - y), (1 - x, y), (1 - x, 1 - y)]


def _shard_region(ref, kind, k, rl, cl, r0, nr, c0, nc):
    if kind == "col":
        return ref.at[pl.ds(r0, nr), pl.ds(k * cl + c0, nc)]
    return ref.at[pl.ds(k * rl + r0, nr), pl.ds(c0, nc)]


def _place_shard(w, kind, chip, name):
    rl, cl = w.shape
    tr = _pick(rl, (256, 128, 64))
    nt = rl // tr

    def body(chip_ref, w_ref, o_ref):
        o_ref[...] = w_ref[...].astype(BF16)

    o_map = (lambda i, chip_ref: (i, chip_ref[0])) if kind == "col" else (lambda i, chip_ref: (chip_ref[0] * nt + i, 0))
    return pl.pallas_call(
        body, name=name,
        grid_spec=pltpu.PrefetchScalarGridSpec(
            num_scalar_prefetch=1, grid=(nt,),
            in_specs=[pl.BlockSpec((tr, cl), lambda i, chip_ref: (i, 0))], out_specs=pl.BlockSpec((tr, cl), o_map)),
        out_shape=jax.ShapeDtypeStruct((rl, 4 * cl) if kind == "col" else (4 * rl, cl), BF16),
        compiler_params=_params(("parallel",)),
    )(chip.reshape(1), w)


def _gather_shard_shapes(placed, kinds):
    return [(p.shape[0], p.shape[1] // 4) if k == "col" else (p.shape[0] // 4, p.shape[1]) for p, k in zip(placed, kinds)]


def _gather_chip_copies(outs, kinds, shard_shapes, send_sems, recv_sems, with_arrivals=True):
    x, y, c = _mesh_pos()
    me = 2 * x + y
    sends, arrivals = [], []
    for a in range(len(outs)):
        rl, cl = shard_shapes[a]
        rh = rl // 2
        reg = functools.partial(_shard_region, outs[a], kinds[a], rl=rl, cl=cl, r0=c * rh, nr=rh, c0=0, nc=cl)
        for j, (px, py) in enumerate(PEER_CHIPS(x, y)):
            to = dict(send_sem=send_sems.at[a, j], recv_sem=recv_sems.at[a, j], device_id=(px, py, c),
                      device_id_type=MESH_ID)
            sends.append(pltpu.make_async_remote_copy(src_ref=reg(k=me), dst_ref=reg(k=me), **to))
            if with_arrivals:
                got = reg(k=2 * px + py)
                arrivals.append(pltpu.make_async_remote_copy(src_ref=got, dst_ref=got, **to))
    return sends, arrivals


def _gather_sibling_copies(outs, kinds, shard_shapes, send_sems, recv_sems):
    x, y, c = _mesh_pos()
    forwards, arrivals = [], []
    for a in range(len(outs)):
        rl, cl = shard_shapes[a]
        rh = rl // 2
        for j, (px, py) in enumerate(PEER_CHIPS(x, y)):
            to = dict(send_sem=send_sems.at[a, j], recv_sem=recv_sems.at[a, j], device_id=(x, y, 1 - c),
                      device_id_type=MESH_ID)
            reg = functools.partial(_shard_region, outs[a], kinds[a], k=2 * px + py, rl=rl, cl=cl, nr=rh, c0=0, nc=cl)
            forwards.append(pltpu.make_async_remote_copy(src_ref=reg(r0=c * rh), dst_ref=reg(r0=c * rh), **to))
            arrivals.append(pltpu.make_async_remote_copy(src_ref=reg(r0=(1 - c) * rh), dst_ref=reg(r0=(1 - c) * rh), **to))
    return forwards, arrivals


def _gather_weights(placed, kinds):
    n = len(placed)
    shard_shapes = _gather_shard_shapes(placed, kinds)

    def body(*refs):
        outs = refs[n:2 * n]
        ici_send, ici_recv, sib_send, sib_recv = refs[2 * n:]
        sends, arrivals = _gather_chip_copies(outs, kinds, shard_shapes, ici_send, ici_recv)
        for cp in sends:
            cp.start()
        forwards, from_sibling = _gather_sibling_copies(outs, kinds, shard_shapes, sib_send, sib_recv)
        for cp, fwd in zip(arrivals, forwards):
            cp.wait_recv()
            fwd.start()
        for cp in from_sibling:
            cp.wait_recv()
        for cp in sends + forwards:
            cp.wait_send()

    return pl.pallas_call(
        body, name="gather_weights",
        out_shape=[jax.ShapeDtypeStruct(p.shape, p.dtype) for p in placed],
        in_specs=[ANY] * n, out_specs=[ANY] * n, input_output_aliases={a: a for a in range(n)},
        scratch_shapes=[pltpu.SemaphoreType.DMA((n, 3))] * 4,
        compiler_params=_params(),
    )(*placed)


def _gather_sibling(placed, kinds):
    n = len(placed)
    shard_shapes = _gather_shard_shapes(placed, kinds)

    def body(*refs):
        forwards, from_sibling = _gather_sibling_copies(refs[n:2 * n], kinds, shard_shapes, *refs[2 * n:])
        for cp in forwards:
            cp.start()
        for cp in from_sibling:
            cp.wait_recv()
        for cp in forwards:
            cp.wait_send()

    return pl.pallas_call(
        body, name="gather_sibling",
        out_shape=[jax.ShapeDtypeStruct(p.shape, p.dtype) for p in placed],
        in_specs=[ANY] * n, out_specs=[ANY] * n, input_output_aliases={a: a for a in range(n)},
        scratch_shapes=[pltpu.SemaphoreType.DMA((n, 3))] * 2,
        compiler_params=_params(),
    )(*placed)


def _half(kind, r, c):
    return (r // 2, c) if kind == "col" else (r, c // 2)


def _half_of(ref, kind, which):
    r, c = ref.shape
    hr, hc = _half(kind, r, c)
    return ref.at[pl.ds(which * hr, hr), :] if kind == "col" else ref.at[:, pl.ds(which * hc, hc)]


def _rs_sibling(grads, kinds, name):
    n = len(grads)

    def body(*refs):
        srcs, dsts = refs[:n], refs[n:2 * n]
        send_sems, recv_sems = refs[2 * n:]
        x, y, c = _mesh_pos()
        cps = [pltpu.make_async_remote_copy(src_ref=_half_of(srcs[a], kinds[a], 1 - c), dst_ref=dsts[a],
                                            send_sem=send_sems.at[a], recv_sem=recv_sems.at[a],
                                            device_id=(x, y, 1 - c), device_id_type=MESH_ID) for a in range(n)]
        for cp in cps:
            cp.start()
        for cp in cps:
            cp.wait()

    return pl.pallas_call(
        body, name=name,
        out_shape=[jax.ShapeDtypeStruct(_half(k, *g.shape), g.dtype) for g, k in zip(grads, kinds)],
        in_specs=[ANY] * n, out_specs=[ANY] * n,
        scratch_shapes=[pltpu.SemaphoreType.DMA((n,)), pltpu.SemaphoreType.DMA((n,))],
        compiler_params=_params(),
    )(*grads)


def _pair_sum(gf, rv, kind, ci, name):
    r, c = rv.shape
    tr = _pick(r, (128, 64, 32, 16, 8))
    nt = r // tr

    def body(ci_ref, g_ref, r_ref, o_ref):
        o_ref[...] = (g_ref[...] + r_ref[...]).astype(BF16)

    g_map = (lambda i, ci_ref: (ci_ref[0] * nt + i, 0)) if kind == "col" else (lambda i, ci_ref: (i, ci_ref[0]))
    blk = pl.BlockSpec((tr, c), lambda i, ci_ref: (i, 0))
    return pl.pallas_call(
        body, name=name,
        grid_spec=pltpu.PrefetchScalarGridSpec(num_scalar_prefetch=1, grid=(nt,),
                                               in_specs=[pl.BlockSpec((tr, c), g_map), blk], out_specs=blk),
        out_shape=jax.ShapeDtypeStruct((r, c), BF16),
        compiler_params=_params(("parallel",)),
    )(ci.reshape(1), gf, rv)


def _rs_slot_shapes(pairs, kinds):
    return [(p.shape[0], p.shape[1] // 4) if k == "col" else (p.shape[0] // 4, p.shape[1]) for p, k in zip(pairs, kinds)]


def _rs_chip_copies(srcs, dsts, kinds, shapes, send_sems, recv_sems, with_arrivals=True):
    x, y, c = _mesh_pos()
    me = 2 * x + y
    sends, arrivals = [], []
    for a in range(len(srcs)):
        rl, cl = shapes[a]
        reg = functools.partial(_shard_region, srcs[a], kinds[a], rl=rl, cl=cl, r0=0, nr=rl, c0=0, nc=cl)
        for j, (px, py) in enumerate(PEER_CHIPS(x, y)):
            to = dict(send_sem=send_sems.at[a, j], recv_sem=recv_sems.at[a, j], device_id=(px, py, c),
                      device_id_type=MESH_ID)
            sends.append(pltpu.make_async_remote_copy(src_ref=reg(k=2 * px + py), dst_ref=dsts[a].at[me], **to))
            if with_arrivals:
                slot = dsts[a].at[2 * px + py]
                arrivals.append(pltpu.make_async_remote_copy(src_ref=slot, dst_ref=slot, **to))
    return sends, arrivals


def _rs_chips(pairs, kinds):
    n = len(pairs)
    shapes = _rs_slot_shapes(pairs, kinds)

    def body(*refs):
        sends, arrivals = _rs_chip_copies(refs[:n], refs[n:2 * n], kinds, shapes, *refs[2 * n:])
        for cp in sends:
            cp.start()
        for cp in arrivals:
            cp.wait_recv()
        for cp in sends:
            cp.wait_send()

    return pl.pallas_call(
        body, name="rs_chips",
        out_shape=[jax.ShapeDtypeStruct((4,) + s, p.dtype) for s, p in zip(shapes, pairs)],
        in_specs=[ANY] * n, out_specs=[ANY] * n,
        scratch_shapes=[pltpu.SemaphoreType.DMA((n, 3)), pltpu.SemaphoreType.DMA((n, 3))],
        compiler_params=_params(),
    )(*pairs)


def _sum_chips(pair, got, kind, pos, name):
    _, r, c = got.shape
    tr = _pick(r, (256, 128, 64, 32, 16))
    nt = r // tr

    def body(pos_ref, own_ref, g1_ref, g2_ref, g3_ref, o_ref):
        o_ref[...] = ((own_ref[...].astype(F32) + g1_ref[0].astype(F32)) + g2_ref[0].astype(F32)) + g3_ref[0].astype(F32)

    if kind == "col":
        own_map = lambda i, p: (i, p[1])
        out_map = lambda i, p: (p[0] * nt + i, 0)
        out_shape = (2 * r, c)
    else:
        own_map = lambda i, p: (p[1] * nt + i, 0)
        out_map = lambda i, p: (i, p[0])
        out_shape = (r, 2 * c)
    peer = lambda m: pl.BlockSpec((1, tr, c), lambda i, p: (p[1] ^ m, i, 0))
    return pl.pallas_call(
        body, name=name,
        grid_spec=pltpu.PrefetchScalarGridSpec(
            num_scalar_prefetch=1, grid=(nt,),
            in_specs=[pl.BlockSpec((tr, c), own_map), peer(1), peer(2), peer(3)],
            out_specs=pl.BlockSpec((tr, c), out_map)),
        out_shape=jax.ShapeDtypeStruct(out_shape, F32),
        compiler_params=_params(("parallel",)),
    )(pos, pair, got, got, got)


def _rs_back(halves, kinds):
    n = len(halves)

    def body(*refs):
        outs = refs[n:2 * n]
        send_sems, recv_sems = refs[2 * n:]
        x, y, c = _mesh_pos()
        cps = []
        for a in range(n):
            mine = _half_of(outs[a], kinds[a], c)
            cps.append(pltpu.make_async_remote_copy(src_ref=mine, dst_ref=mine, send_sem=send_sems.at[a],
                                                    recv_sem=recv_sems.at[a], device_id=(x, y, 1 - c),
                                                    device_id_type=MESH_ID))
            cps[-1].start()
        for a in range(n):
            other = _half_of(outs[a], kinds[a], 1 - c)
            pltpu.make_async_remote_copy(src_ref=other, dst_ref=other, send_sem=send_sems.at[a],
                                         recv_sem=recv_sems.at[a], device_id=(x, y, 1 - c),
                                         device_id_type=MESH_ID).wait_recv()
        for cp in cps:
            cp.wait_send()

    return pl.pallas_call(
        body, name="rs_back",
        out_shape=[jax.ShapeDtypeStruct(h.shape, h.dtype) for h in halves],
        in_specs=[ANY] * n, out_specs=[ANY] * n, input_output_aliases={a: a for a in range(n)},
        scratch_shapes=[pltpu.SemaphoreType.DMA((n,)), pltpu.SemaphoreType.DMA((n,))],
        compiler_params=_params(),
    )(*halves)


def _rope_tables(l, lc):
    rows = l // GRID_W
    row = jnp.repeat(jnp.arange(rows, dtype=F32), GRID_W)
    col = jnp.tile(jnp.arange(GRID_W, dtype=F32), rows)
    n_freq = RET_DH // 4
    inv_freq = ROPE_THETA ** (-jnp.arange(n_freq, dtype=F32) / n_freq)
    ang = jnp.concatenate([row[:, None] * inv_freq, col[:, None] * inv_freq], axis=-1)
    cos_t = jnp.repeat(jnp.cos(ang), 2, axis=-1)
    sin_t = jnp.repeat(jnp.sin(ang), 2, axis=-1) * jnp.tile(jnp.array([-1.0, 1.0], F32), RET_DH // 2)
    cos_t = jnp.concatenate([jnp.ones((lc, RET_DH), F32), cos_t], axis=0)
    sin_t = jnp.concatenate([jnp.zeros((lc, RET_DH), F32), sin_t], axis=0)
    return cos_t, sin_t


def _s5_pack(a):
    blk = lambda t: t.reshape(1, S5_NB, 128, S5_STATE)
    lre = jnp.stack([a["s5_lambda_re_f"][0], a["s5_lambda_re_b"][0]]).reshape(2, S5_NB, 8, S5_STATE)
    lim = jnp.stack([a["s5_lambda_im_f"][0], a["s5_lambda_im_b"][0]]).reshape(2, S5_NB, 8, S5_STATE)
    lst = jnp.stack([a["s5_log_step_f"][0], a["s5_log_step_b"][0]]).reshape(2, S5_NB, 8, 1)
    b_re = blk(a["s5_b_re"][0].transpose(0, 2, 1))
    b_im = blk(a["s5_b_im"][0].transpose(0, 2, 1))
    return (lre, lim, lst, b_re, b_im, blk(a["s5_c_re"][0]), blk(a["s5_c_im"][0]),
            a["s5_d"].reshape(1, S5_NB, 1, 128))


def _s5_unpack(g):
    glre, glim, glst, gbre, gbim, gcre, gcim, gd = g
    unb = lambda t: t.reshape(S5_GROUPS, S5_GROUP, S5_STATE).transpose(0, 2, 1)[None]
    return {
        "s5_lambda_re_f": glre[0].reshape(1, S5_GROUPS, S5_STATE), "s5_lambda_re_b": glre[1].reshape(1, S5_GROUPS, S5_STATE),
        "s5_lambda_im_f": glim[0].reshape(1, S5_GROUPS, S5_STATE), "s5_lambda_im_b": glim[1].reshape(1, S5_GROUPS, S5_STATE),
        "s5_log_step_f": glst[0].reshape(1, S5_GROUPS), "s5_log_step_b": glst[1].reshape(1, S5_GROUPS),
        "s5_b_re": unb(gbre), "s5_b_im": unb(gbim),
        "s5_c_re": gcre.reshape(1, S5_GROUPS, S5_GROUP, S5_STATE), "s5_c_im": gcim.reshape(1, S5_GROUPS, S5_GROUP, S5_STATE),
        "s5_d": gd.reshape(1, S5_WIDTH),
    }


def _local_step(a, wb, late, mx, mc, conv_w, ci):
    x, ctx, tgt = a["x"][0], a["ctx"][0], a["loss_target"][0]
    l, lc = x.shape[0], ctx.shape[0]
    la = l + lc
    nct, ncc, nrc, cn = lc // TOK_TILE, lc // S5_T, lc // RET_CHUNK, la // S5_T
    n1w, n2w, fnw = a["norm1_w"], a["norm2_w"], a["final_norm_w"].reshape(1, D_MODEL)
    conv_b, b_glu = a["conv_b"], a["s5_b_glu"]
    ld2 = jnp.concatenate([a["ret_log_decay_f"], a["ret_log_decay_b"]], axis=0)
    mod4 = jnp.concatenate([mc[0:2], mx[0:2]], axis=0)
    mod3 = mx[2:5]
    gate5 = mx[5:6]
    cos_t, sin_t = _rope_tables(l, lc)
    s5p = _s5_pack(a)

    p_all, h1b, w_out_p, w_down_p = _norm_inproj(x, ctx, n1w, mod4, wb["w_in"], [late[0], late[2]],
                                                 (LATE_KINDS[0], LATE_KINDS[2]))
    p3 = p_all.reshape(cn, S5_T, IN_COLS)
    gg, xw, yw, a16 = _s5_gen(*s5p)
    sloc = _s5_to_states(p3, xw, "s5_state")
    a16s = a16.transpose(1, 0, 2, 3).reshape(2, S5_GROUPS, 128)
    hs = _s5_scan(sloc, a16s, ncc)
    y_all = _s5_from_states(p3, gg, hs, yw, False, "s5_out").reshape(la, S5_WIDTH)
    s5x = _s5_glu(y_all, wb["s5_w_glu"], b_glu, nct)
    tabs = _ret_tables(ld2)
    of, ob, ssf, ssb, w_up_p = _ret_scan(p_all, cos_t, sin_t, tabs, nrc, [late[1]], (LATE_KINDS[1],))
    wb = {**wb, **dict(zip(LATE_NAMES, _gather_sibling([w_out_p, w_up_p, w_down_p], LATE_KINDS)))}
    retx, y_ret = _ret_gate(of, ob, p_all, nct)
    x1, mix, h2b, up = _outproj_up(x, s5x, retx, wb["w_out"], mod3, n2w, wb["w_up"])
    act, dx2, ddn, dact, acc_f = _ffn_loss(up, x1, conv_w, conv_b, wb["w_down"], gate5, fnw, tgt)

    g = {}
    g["w_down"] = _mm_tn(act, ddn, name="gw_down")
    dup, acc_c = _convglu_bwd(up, dact, conv_w, conv_b)
    g["w_up"] = _mm_tn(h2b, dup, name="gw_up")
    first = [g[n] for n in FIRST_GRADS]
    first_pairs = [_pair_sum(gf, rv, k, ci, "rs_pair_" + n)
                   for gf, rv, k, n in zip(first, _rs_sibling(first, FIRST_KINDS, "rs_sibling_first"), FIRST_KINDS, FIRST_GRADS)]
    dx1, dmixb, dmix, acc_2, *first_got = _up_bwd(dup, wb["w_up"], wb["w_out"], x1, dx2, mix, mod3, n2w,
                                                  first_pairs, FIRST_KINDS)
    g["w_out"] = jnp.concatenate([_mm_tn(s5x, dmixb, name="gw_out_s5"), _mm_tn(retx, dmixb, name="gw_out_ret")], axis=0)

    dy_s5, g["s5_w_glu"], g["s5_b_glu"] = _s5_glu_bwd(y_all, dmix, wb["s5_w_glu"], b_glu, nct)
    dy3 = dy_s5.reshape(cn, S5_T, S5_WIDTH)
    e = _s5_to_states(dy3, yw, "s5_bwd_h")
    ds, da16 = _s5_scan_bwd(e, hs, a16s, ncc)
    du = _s5_from_states(dy3, gg, ds, xw, True, "s5_bwd_u").reshape(la, S5_WIDTH)
    dkb = _s5_bwd_kb(p3, dy3)
    dwst = _s5_bwd_w(p3, ds, "s5_bwd_wst")
    dwout = _s5_bwd_w(dy3, hs, "s5_bwd_wout")
    da16p = da16.reshape(2, S5_NB, 8, 128).transpose(1, 0, 2, 3)
    g.update(_s5_unpack(_s5_gen_bwd(*s5p, dkb, dwst, dwout, da16p)))

    dy_ret, dg = _ret_gate_bwd(y_ret, p_all, dmix, nct)
    dqf, dkf, dvf, dqb, dkb_, dvb, dld = _ret_scan_bwd(p_all, cos_t, sin_t, tabs, ssf, ssb, dy_ret, nrc)
    g["ret_log_decay_f"] = dld[0, :, 0, 0].reshape(1, RET_HEADS)
    g["ret_log_decay_b"] = dld[1, :, 0, 0].reshape(1, RET_HEADS)
    dp, grad_x, acc_1 = _in_bwd(dqf, dkf, dvf, dqb, dkb_, dvb, du, dg, cos_t, sin_t, wb["w_in"], x, ctx, n1w, mod4, dx1)
    g["w_in"] = _mm_tn(h1b, dp, name="gw_in")

    g["norm1_w"], g["norm2_w"], g["final_norm_w"] = acc_1[0:1], acc_2[0:1], acc_f[0]
    g["conv_w"], g["conv_b"] = acc_c[0:3], acc_c[3:4]
    zero = jnp.zeros((1, D_MODEL), F32)
    dmx = jnp.concatenate([acc_1[3:5], acc_2[1:2], acc_2[2:4], acc_f[1:2]], axis=0)
    dmc = jnp.concatenate([acc_1[1:3], zero, zero, zero, zero], axis=0)
    return acc_f[2, 0], grad_x, g, dmx, dmc, first_pairs, first_got


WEIGHT_NAMES = ("c_ctx", "w_mod", "b_mod", "norm1_w", "w_in", "s5_lambda_re_f", "s5_lambda_im_f", "s5_log_step_f",
                "s5_lambda_re_b", "s5_lambda_im_b", "s5_log_step_b", "s5_b_re", "s5_b_im", "s5_c_re", "s5_c_im",
                "s5_d", "s5_w_glu", "s5_b_glu", "ret_log_decay_f", "ret_log_decay_b", "w_out", "norm2_w", "w_up",
                "conv_w", "conv_b", "w_down", "final_norm_w")
BIG_NAMES = ("w_in", "w_out", "w_up", "w_down", "s5_w_glu")
BIG_KINDS = ("col", "row", "col", "row", "row")
EARLY_NAMES, EARLY_KINDS = ("w_in", "s5_w_glu"), ("col", "row")
LATE_NAMES, LATE_KINDS = ("w_out", "w_up", "w_down"), ("row", "col", "row")
FIRST_GRADS, FIRST_KINDS = ("w_down", "w_up"), ("row", "col")
LAST_GRADS, LAST_KINDS = ("w_in", "w_out", "s5_w_glu"), ("col", "row", "row")
SMALL_NAMES = ("norm1_w", "norm2_w", "final_norm_w", "conv_b", "conv_w", "s5_lambda_re_f", "s5_lambda_im_f",
               "s5_log_step_f", "s5_lambda_re_b", "s5_lambda_im_b", "s5_log_step_b", "s5_b_re", "s5_b_im", "s5_c_re",
               "s5_c_im", "s5_d", "s5_b_glu", "ret_log_decay_f", "ret_log_decay_b")
ROW = 1024
N_CHIPS = 4


def _pack_rows(parts):
    flat = jnp.concatenate([p.reshape(-1) for p in parts])
    n = flat.shape[0]
    rows = -(-n // (8 * ROW)) * 8
    return jnp.pad(flat, (0, rows * ROW - n)).reshape(rows, ROW)


def _unpack_rows(packed, shapes):
    flat = packed.reshape(-1)
    out, off = [], 0
    for s in shapes:
        n = math.prod(s)
        out.append(flat[off:off + n].reshape(s))
        off += n
    return out


def _step(a):
    xi, yi, ci = _mesh_pos()
    chip = 2 * xi + yi
    dev = 2 * chip + ci

    cw_loc = a["conv_w"].reshape(-1)
    small_in = jnp.concatenate([a["c"].reshape(-1), jnp.pad(cw_loc, (0, 24 * 128 - cw_loc.shape[0]))]).reshape(32, 128)
    sg = _all_gather8(small_in, "gather_cond").reshape(8, 32, 128)
    c_all = sg[:, 0:8].reshape(8, D_MODEL)
    conv_w = sg[0::2, 8:32].reshape(N_CHIPS, -1)[:, :cw_loc.shape[0]].reshape(N_CHIPS, 3, -1)
    conv_w = conv_w.transpose(1, 0, 2).reshape(3, D_FF)

    placed = {n: _place_shard(a[n][0], k, chip, "place_" + n) for n, k in zip(BIG_NAMES, BIG_KINDS)}
    wb = dict(zip(EARLY_NAMES, _gather_weights([placed[n] for n in EARLY_NAMES], EARLY_KINDS)))
    late = [placed[n] for n in LATE_NAMES]

    w_mod_b = a["w_mod"][0].astype(BF16)
    c_ctx = a["c_ctx"].reshape(1, D_MODEL)
    b_loc = lax.dynamic_slice_in_dim(a["b_mod"], chip * MOD_COLS, MOD_COLS, 1)
    m_loc, s_b = _mod_fwd(c_all, c_ctx, w_mod_b, b_loc)
    mg = _all_gather8(m_loc, "gather_mod").reshape(8, MOD_ROWS, MOD_COLS)
    m_full = mg[0::2].transpose(1, 0, 2).reshape(MOD_ROWS, 6 * D_MODEL)
    mx = lax.dynamic_slice_in_dim(m_full, dev, 1, 0).reshape(6, D_MODEL)
    mc = m_full[8].reshape(6, D_MODEL)

    loss_part, grad_x, g, dmx, dmc, first_pairs, first_got = _local_step(a, wb, late, mx, mc, conv_w, ci)
    loss = lax.psum(loss_part, ("x", "y", "c"))

    dm_pair = jnp.concatenate([dmx.reshape(1, -1), dmc.reshape(1, -1), jnp.zeros((6, 6 * D_MODEL), F32)], axis=0)
    dm_all = _all_gather8(dm_pair, "gather_dmod").reshape(8, 8, 6 * D_MODEL)
    dm16, gb_mod = _mod_bwd_sum(dm_all)
    dm_loc = lax.dynamic_slice_in_dim(dm16, chip * MOD_COLS, MOD_COLS, 1)
    gw_mod, gcc = _mod_bwd_w(dm_loc, s_b, c_ctx, w_mod_b)

    small_parts = [g[n] for n in SMALL_NAMES] + [gcc[0]]
    small_shapes = [p.shape for p in small_parts]
    sp = _pack_rows(small_parts)
    tot = _sum_slots(_all_gather8(sp, "gather_small_grads").reshape(8, sp.shape[0], ROW), "sum_small_grads")
    small = dict(zip(SMALL_NAMES + ("c_ctx",), _unpack_rows(tot, small_shapes)))
    grads = {n: small[n].reshape(a[n].shape) for n in SMALL_NAMES if n != "conv_w"}
    grads["c_ctx"] = (0.5 * small["c_ctx"]).reshape(a["c_ctx"].shape)
    grads["conv_w"] = lax.dynamic_slice_in_dim(small["conv_w"], chip * (D_FF // N_CHIPS), D_FF // N_CHIPS, 1)[None]
    grads["b_mod"] = gb_mod
    grads["w_mod"] = gw_mod[None]

    last = [g[n] for n in LAST_GRADS]
    last_pairs = [_pair_sum(gf, rv, k, ci, "rs_pair_" + n)
                  for gf, rv, k, n in zip(last, _rs_sibling(last, LAST_KINDS, "rs_sibling_last"), LAST_KINDS, LAST_GRADS)]
    last_got = _rs_chips(last_pairs, LAST_KINDS)
    pos = jnp.stack([ci, chip])
    order = FIRST_GRADS + LAST_GRADS
    order_kinds = FIRST_KINDS + LAST_KINDS
    halves = [_sum_chips(p, t, k, pos, "rs_sum_" + n)
              for p, t, k, n in zip(first_pairs + last_pairs, list(first_got) + list(last_got), order_kinds, order)]
    for n, t in zip(order, _rs_back(halves, order_kinds)):
        grads[n] = t[None]

    delta, new_m, new_v = {}, {}, {}
    for n in BIG_NAMES + ("w_mod",):
        for dst, t in zip((delta, new_m, new_v), _adamw(a[n][0], grads[n][0], a["m_" + n][0], a["v_" + n][0], "adamw_" + n)):
            dst[n] = t[None]
    rest = [n for n in WEIGHT_NAMES if n not in BIG_NAMES and n != "w_mod"]
    shapes = [a[n].shape for n in rest]
    pr = lambda pre: _pack_rows([a[pre + n] for n in rest])
    for dst, t in zip((delta, new_m, new_v),
                      _adamw(pr(""), _pack_rows([grads[n] for n in rest]), pr("m_"), pr("v_"), "adamw_small")):
        dst.update(zip(rest, _unpack_rows(t, shapes)))

    return (loss, grad_x[None], *[grads[n] for n in WEIGHT_NAMES], *[delta[n] for n in WEIGHT_NAMES],
            *[new_m[n] for n in WEIGHT_NAMES], *[new_v[n] for n in WEIGHT_NAMES])


def kernel(x, c, ctx, c_ctx, w_mod, b_mod, norm1_w, w_in, s5_lambda_re_f, s5_lambda_im_f, s5_log_step_f, s5_lambda_re_b, s5_lambda_im_b, s5_log_step_b, s5_b_re, s5_b_im, s5_c_re, s5_c_im, s5_d, s5_w_glu, s5_b_glu, ret_log_decay_f, ret_log_decay_b, w_out, norm2_w, w_up, conv_w, conv_b, w_down, final_norm_w, loss_target, m_c_ctx, m_w_mod, m_b_mod, m_norm1_w, m_w_in, m_s5_lambda_re_f, m_s5_lambda_im_f, m_s5_log_step_f, m_s5_lambda_re_b, m_s5_lambda_im_b, m_s5_log_step_b, m_s5_b_re, m_s5_b_im, m_s5_c_re, m_s5_c_im, m_s5_d, m_s5_w_glu, m_s5_b_glu, m_ret_log_decay_f, m_ret_log_decay_b, m_w_out, m_norm2_w, m_w_up, m_conv_w, m_conv_b, m_w_down, m_final_norm_w, v_c_ctx, v_w_mod, v_b_mod, v_norm1_w, v_w_in, v_s5_lambda_re_f, v_s5_lambda_im_f, v_s5_log_step_f, v_s5_lambda_re_b, v_s5_lambda_im_b, v_s5_log_step_b, v_s5_b_re, v_s5_b_im, v_s5_c_re, v_s5_c_im, v_s5_d, v_s5_w_glu, v_s5_b_glu, v_ret_log_decay_f, v_ret_log_decay_b, v_w_out, v_norm2_w, v_w_up, v_conv_w, v_conv_b, v_w_down, v_final_norm_w):
    return _step(dict(locals()))
```

```python
import functools
import math

import jax
import jax.numpy as jnp
from jax import lax
from jax.experimental import pallas as pl
from jax.experimental.pallas import tpu as pltpu

F32 = jnp.float32
BF16 = jnp.bfloat16

D_MODEL = 1024
S5_WIDTH = 512
S5_GROUPS = 32
S5_GROUP = 16
S5_STATE = 64
RET_WIDTH = 512
RET_HEADS = 4
RET_DH = 128
RET_CHUNK = 256
GRID_W = 64
ROPE_THETA = 10000.0
D_FF = 2816
NORM_EPS = 1e-6
IN_COLS = S5_WIDTH + 4 * RET_WIDTH

S5_T = 16
S5_NB = 4
S5_BW = S5_T * 128
S5_SW = 8 * 2 * S5_STATE

ADAM_LR, ADAM_B1, ADAM_B2, ADAM_EPS, ADAM_WD, ADAM_STEP = 0.001, 0.9, 0.999, 1e-08, 0.01, 10

VMEM_LIMIT = 56 * 1024 * 1024
MM_TN_VMEM = 40 * 1024 * 1024
MESH_ID = pl.DeviceIdType.MESH


def _params(sem=None):
    return pltpu.CompilerParams(dimension_semantics=sem, vmem_limit_bytes=VMEM_LIMIT)


def _full(shape):
    n = len(shape)
    return pl.BlockSpec(shape, lambda *_: (0,) * n)


def _dot(a, b):
    return jnp.dot(a, b, preferred_element_type=F32)


def _dot_nt(a, b):
    return lax.dot_general(a, b, (((1,), (1,)), ((), ())), preferred_element_type=F32)


def _dot_tn(a, b):
    return lax.dot_general(a, b, (((0,), (0,)), ((), ())), preferred_element_type=F32)


def _dot_hi(a, b):
    return jnp.dot(a, b, preferred_element_type=F32, precision=lax.Precision.HIGHEST)


def _dot_nt_hi(a, b):
    return lax.dot_general(a, b, (((1,), (1,)), ((), ())), preferred_element_type=F32,
                           precision=lax.Precision.HIGHEST)


def _gelu(x):
    return 0.5 * x * (1.0 + jnp.tanh(0.7978845608028654 * (x + 0.044715 * (x * x * x))))


def _sigmoid(x):
    return 1.0 / (1.0 + jnp.exp(-x))


def _silu(x):
    return x * _sigmoid(x)


def _rms_mod(x, nw, sh, sc):
    r = lax.rsqrt(jnp.mean(x * x, axis=-1, keepdims=True) + NORM_EPS)
    return (x * r * nw) * (1.0 + sc) + sh


def _rms(x, nw):
    r = lax.rsqrt(jnp.mean(x * x, axis=-1, keepdims=True) + NORM_EPS)
    return x * r * nw


def _head_norm_gate(y, g):
    mu = jnp.mean(y, axis=-1, keepdims=True)
    yc = y - mu
    var = jnp.mean(yc * yc, axis=-1, keepdims=True)
    return _silu(g) * (yc * lax.rsqrt(var + NORM_EPS))


def _swap_pairs(t):
    lane = lax.broadcasted_iota(jnp.int32, t.shape, 1)
    return jnp.where(lane % 2 == 0, pltpu.roll(t, RET_DH - 1, 1), pltpu.roll(t, 1, 1))


def _rope(t, cos_t, sin_t):
    return t * cos_t + _swap_pairs(t) * sin_t


def _rope_t(dt, cos_t, sin_t):
    return dt * cos_t + _swap_pairs(dt * sin_t)


def _pick(n, prefs):
    for p in prefs:
        if n % p == 0:
            return p
    return n


def _mm_tn(a, b, *, name):
    m, k = a.shape
    n = b.shape[1]
    tn = _pick(n, (1408, 1024, 1280, 512))
    fits = lambda t: 2 * (2 * t * k + 2 * t * tn + 4 * k * tn) <= MM_TN_VMEM
    tm = _pick(m, [t for t in (2816, 2048, 1024, 768, 512, 256) if fits(t)] + [128])

    def body(a_ref, b_ref, o_ref):
        @pl.when(pl.program_id(1) == 0)
        def _():
            o_ref[...] = jnp.zeros_like(o_ref)
        o_ref[...] += _dot_tn(a_ref[...], b_ref[...])

    return pl.pallas_call(
        body, name=name, grid=(n // tn, m // tm),
        in_specs=[pl.BlockSpec((tm, k), lambda j, i: (i, 0)), pl.BlockSpec((tm, tn), lambda j, i: (i, j))],
        out_specs=pl.BlockSpec((k, tn), lambda j, i: (0, j)),
        out_shape=jax.ShapeDtypeStruct((k, n), F32),
        compiler_params=_params(("parallel", "arbitrary")),
    )(a, b)


TOK_TILE = 256


def _behind(step, last, copies):
    @pl.when(step == 0)
    def _():
        for cp in copies(with_arrivals=False)[0]:
            cp.start()

    @pl.when(step == last)
    def _():
        sends, arrivals = copies()
        for cp in arrivals:
            cp.wait_recv()
        for cp in sends:
            cp.wait_send()


def _norm_inproj(x, ctx, n1w, mod4, w_in_b, placed, kinds):
    l, lc = x.shape[0], ctx.shape[0]
    tm = TOK_TILE
    nct = lc // tm
    la = l + lc
    n = len(placed)
    shard_shapes = _gather_shard_shapes(placed, kinds)

    def body(x_ref, c_ref, nw_ref, mod_ref, w_ref, *rest):
        p_ref, h_ref = rest[n:n + 2]
        _behind(pl.program_id(0), la // tm - 1,
                functools.partial(_gather_chip_copies, rest[n + 2:2 * n + 2], kinds, shard_shapes, *rest[2 * n + 2:]))
        is_ctx = pl.program_id(0) < nct
        xt = jnp.where(is_ctx, c_ref[...], x_ref[...])
        sh = jnp.where(is_ctx, mod_ref[0:1, :], mod_ref[2:3, :])
        sc = jnp.where(is_ctx, mod_ref[1:2, :], mod_ref[3:4, :])
        hb = _rms_mod(xt, nw_ref[...], sh, sc).astype(BF16)
        h_ref[...] = hb
        p_ref[...] = _dot(hb, w_ref[...])

    return pl.pallas_call(
        body, name="norm_inproj", grid=(la // tm,),
        in_specs=[pl.BlockSpec((tm, D_MODEL), lambda i: (jnp.maximum(i - nct, 0), 0)),
                  pl.BlockSpec((tm, D_MODEL), lambda i: (jnp.minimum(i, nct - 1), 0)),
                  _full((1, D_MODEL)), _full((4, D_MODEL)), _full((D_MODEL, IN_COLS))] + [ANY] * n,
        out_specs=[pl.BlockSpec((tm, IN_COLS), lambda i: (i, 0)), pl.BlockSpec((tm, D_MODEL), lambda i: (i, 0))]
        + [ANY] * n,
        out_shape=[jax.ShapeDtypeStruct((la, IN_COLS), F32), jax.ShapeDtypeStruct((la, D_MODEL), BF16)]
        + [jax.ShapeDtypeStruct(p.shape, p.dtype) for p in placed],
        input_output_aliases={5 + a: 2 + a for a in range(n)},
        scratch_shapes=[pltpu.SemaphoreType.DMA((n, 3)), pltpu.SemaphoreType.DMA((n, 3))],
        compiler_params=_params(("arbitrary",)),
    )(x, ctx, n1w, mod4, w_in_b, *placed)


def _iota2(shape, dim):
    return lax.broadcasted_iota(jnp.int32, shape, dim)


def _group_mask(rows, cols, row_div, col_div):
    return jnp.where(_iota2((rows, cols), 0) // row_div == _iota2((rows, cols), 1) // col_div, 1.0, 0.0).astype(F32)


def _s5_gen_dir(lre, lim, lst, b_re, b_im, c_re, c_im):
    step = jnp.exp(lst)
    mag = jnp.exp(lre * step)
    ar = mag * jnp.cos(lim * step)
    ai = mag * jnp.sin(lim * step)
    den = lre * lre + lim * lim
    xr = ar - 1.0
    cr = (xr * lre + ai * lim) / den
    ci = (ai * lre - xr * lim) / den
    rexp = _group_mask(128, 8, S5_GROUP, 1)
    are, aie = _dot_hi(rexp, ar), _dot_hi(rexp, ai)
    cre, cie = _dot_hi(rexp, cr), _dot_hi(rexp, ci)
    bbr = cre * b_re - cie * b_im
    bbi = cre * b_im + cie * b_re
    gmask = _group_mask(128, 128, S5_GROUP, S5_GROUP)
    pr, pi = jnp.ones_like(are), jnp.zeros_like(are)
    xs, ys = [], []
    for t in range(S5_T + 1):
        if t < S5_T:
            xs.append(jnp.concatenate([bbr * pr - bbi * pi, bbr * pi + bbi * pr], axis=1))
        ys.append(jnp.concatenate([c_re * pr - c_im * pi, -(c_re * pi + c_im * pr)], axis=1))
        pr, pi = pr * are - pi * aie, pr * aie + pi * are
    gs = [_dot_nt_hi(x_t, ys[0]) * gmask for x_t in xs]
    r16, i16 = ar, ai
    for _ in range(4):
        r16, i16 = r16 * r16 - i16 * i16, 2.0 * r16 * i16
    return xs, ys, gs, jnp.concatenate([r16, i16], axis=1)


def _s5_expand(z):
    return jnp.concatenate([z] * 8, axis=1) * _group_mask(128, S5_SW, S5_GROUP, 128)


def _s5_contract(z):
    zm = z * _group_mask(128, S5_SW, S5_GROUP, 128)
    acc = zm[:, 0:128]
    for k in range(1, 8):
        acc = acc + zm[:, 128 * k:128 * (k + 1)]
    return acc


def _s5_param_specs():
    blk3 = lambda r, c: pl.BlockSpec((1, 1, r, c), lambda b, *_: (0, b, 0, 0))
    dir3 = lambda r, c: pl.BlockSpec((2, 1, r, c), lambda b, *_: (0, b, 0, 0))
    return [dir3(8, S5_STATE), dir3(8, S5_STATE), dir3(8, 1), blk3(128, S5_STATE), blk3(128, S5_STATE),
            blk3(128, S5_STATE), blk3(128, S5_STATE), blk3(1, 128)]


def _s5_gen(lre, lim, lst, b_re, b_im, c_re, c_im, dvec):
    def body(lre_ref, lim_ref, lst_ref, bre_ref, bim_ref, cre_ref, cim_ref, d_ref, gg_ref, xw_ref, yw_ref, a16_ref):
        eye = _group_mask(128, 128, 1, 1)
        g0 = eye * d_ref[0, 0]
        for dr in range(2):
            xs, ys, gs, a16 = _s5_gen_dir(lre_ref[dr, 0], lim_ref[dr, 0], lst_ref[dr, 0], bre_ref[0, 0],
                                          bim_ref[0, 0], cre_ref[0, 0], cim_ref[0, 0])
            a16_ref[0, dr] = a16
            for j in range(S5_T):
                xw_ref[0, dr, j] = xs[S5_T - 1 - j if dr == 0 else j]
                yw_ref[0, dr, j] = ys[j + 1 if dr == 0 else S5_T - j]
            g0 = g0 + gs[0]
            for t in range(1, S5_T):
                gg_ref[0, (S5_T - 1) + t if dr == 0 else (S5_T - 1) - t] = gs[t]
        gg_ref[0, S5_T - 1] = g0

    blk = pl.BlockSpec((1, 2, S5_T, 128, 128), lambda b: (b, 0, 0, 0, 0))
    return pl.pallas_call(
        body, name="s5_gen", grid=(S5_NB,),
        in_specs=_s5_param_specs(),
        out_specs=[pl.BlockSpec((1, 2 * S5_T - 1, 128, 128), lambda b: (b, 0, 0, 0)), blk, blk,
                   pl.BlockSpec((1, 2, 8, 128), lambda b: (b, 0, 0, 0))],
        out_shape=[jax.ShapeDtypeStruct((S5_NB, 2 * S5_T - 1, 128, 128), F32),
                   jax.ShapeDtypeStruct((S5_NB, 2, S5_T, 128, 128), F32),
                   jax.ShapeDtypeStruct((S5_NB, 2, S5_T, 128, 128), F32),
                   jax.ShapeDtypeStruct((S5_NB, 2, 8, 128), F32)],
        compiler_params=_params(("parallel",)),
    )(lre, lim, lst, b_re, b_im, c_re, c_im, dvec)


def _s5_fill_state_mat(w_scr, src_ref, dr):
    for j in range(S5_T):
        w_scr[128 * j:128 * (j + 1), :] = _s5_expand(src_ref[0, dr, j]).astype(BF16)


def _s5_fill_toeplitz(k_scr, gg_ref):
    for j in range(S5_T):
        for i in range(S5_T):
            k_scr[128 * j:128 * (j + 1), 128 * i:128 * (i + 1)] = gg_ref[0, i - j + (S5_T - 1)].astype(BF16)


S5_GEN_SPECS = [pl.BlockSpec((1, 2 * S5_T - 1, 128, 128), lambda b: (b, 0, 0, 0)),
                pl.BlockSpec((1, 2, S5_T, 128, 128), lambda b: (b, 0, 0, 0, 0))]


def _s5_gen_bwd(lre, lim, lst, b_re, b_im, c_re, c_im, dvec, dg, dx, dy, da16):
    def body(lre_ref, lim_ref, lst_ref, bre_ref, bim_ref, cre_ref, cim_ref, d_ref, dg_ref, dx_ref, dy_ref, da16_ref,
             glre_ref, glim_ref, glst_ref, gbre_ref, gbim_ref, gcre_ref, gcim_ref, gd_ref):
        eye = _group_mask(128, 128, 1, 1)
        gd_ref[0, 0] = jnp.sum(dg_ref[0, S5_T - 1] * eye, axis=0, keepdims=True)
        gb = [None, None, None, None]
        for dr in range(2):
            args = (lre_ref[dr, 0], lim_ref[dr, 0], lst_ref[dr, 0], bre_ref[0, 0], bim_ref[0, 0],
                    cre_ref[0, 0], cim_ref[0, 0])
            _, vjp = jax.vjp(_s5_gen_dir, *args)
            dxs = [dx_ref[0, dr, S5_T - 1 - t if dr == 0 else t] for t in range(S5_T)]
            dys = [jnp.zeros((128, 128), F32)] + [dy_ref[0, dr, t - 1 if dr == 0 else S5_T - t]
                                                  for t in range(1, S5_T + 1)]
            dgs = [dg_ref[0, (S5_T - 1) + t if dr == 0 else (S5_T - 1) - t] for t in range(S5_T)]
            g = vjp((dxs, dys, dgs, da16_ref[0, dr]))
            glre_ref[dr, 0] = g[0]
            glim_ref[dr, 0] = g[1]
            glst_ref[dr, 0] = g[2]
            for q in range(4):
                gb[q] = g[3 + q] if gb[q] is None else gb[q] + g[3 + q]
        gbre_ref[0, 0] = gb[0]
        gbim_ref[0, 0] = gb[1]
        gcre_ref[0, 0] = gb[2]
        gcim_ref[0, 0] = gb[3]

    shp = lambda a: jax.ShapeDtypeStruct(a.shape, F32)
    return pl.pallas_call(
        body, name="s5_gen_bwd", grid=(S5_NB,),
        in_specs=_s5_param_specs() + [
            pl.BlockSpec((1, 2 * S5_T - 1, 128, 128), lambda b: (b, 0, 0, 0)),
            pl.BlockSpec((1, 2, S5_T, 128, 128), lambda b: (b, 0, 0, 0, 0)),
            pl.BlockSpec((1, 2, S5_T, 128, 128), lambda b: (b, 0, 0, 0, 0)),
            pl.BlockSpec((1, 2, 8, 128), lambda b: (b, 0, 0, 0))],
        out_specs=_s5_param_specs(),
        out_shape=[shp(lre), shp(lim), shp(lst), shp(b_re), shp(b_im), shp(c_re), shp(c_im), shp(dvec)],
        compiler_params=_params(("parallel",)),
    )(lre, lim, lst, b_re, b_im, c_re, c_im, dvec, dg, dx, dy, da16)


def _s5_rows(t):
    cn = t.shape[0] // S5_T
    return t.reshape(cn, S5_T, S5_NB, 128).transpose(2, 0, 1, 3).reshape(S5_NB, cn, S5_BW).astype(BF16)


def _s5_put_groups(o_ref, dr, val):
    for gi in range(8):
        o_ref[dr, :, gi, :] = val[:, 128 * gi:128 * (gi + 1)]


def _s5_get_groups(s_ref, dr, n=8):
    return jnp.concatenate([s_ref[dr, :, gi, :] for gi in range(n)], axis=1).astype(BF16)


def _s5_to_states(u3, blocks, name):
    cn = u3.shape[1]

    def body(u_ref, b_ref, o_ref, w_scr):
        u = u_ref[0]
        for dr in range(2):
            _s5_fill_state_mat(w_scr, b_ref, dr)
            _s5_put_groups(o_ref, dr, _dot(u, w_scr[...]))

    return pl.pallas_call(
        body, name=name, grid=(S5_NB,),
        in_specs=[pl.BlockSpec((1, cn, S5_BW), lambda b: (b, 0, 0)), S5_GEN_SPECS[1]],
        out_specs=pl.BlockSpec((2, cn, 8, 128), lambda b: (0, 0, b, 0)),
        out_shape=jax.ShapeDtypeStruct((2, cn, S5_GROUPS, 128), F32),
        scratch_shapes=[pltpu.VMEM((S5_BW, S5_SW), BF16)],
        compiler_params=_params(("parallel",)),
    )(u3, blocks)


def _s5_from_states(u3, gg, st, blocks, transposed, name):
    cn = u3.shape[1]

    def body(u_ref, g_ref, s_ref, b_ref, o_ref, k_scr, w_scr):
        u = u_ref[0]
        _s5_fill_toeplitz(k_scr, g_ref)
        y = _dot_nt(u, k_scr[...]) if transposed else _dot(u, k_scr[...])
        for dr in range(2):
            _s5_fill_state_mat(w_scr, b_ref, dr)
            y = y + _dot_nt(_s5_get_groups(s_ref, dr), w_scr[...])
        for i in range(S5_T):
            o_ref[:, i, :] = y[:, 128 * i:128 * (i + 1)]

    return pl.pallas_call(
        body, name=name, grid=(S5_NB,),
        in_specs=[pl.BlockSpec((1, cn, S5_BW), lambda b: (b, 0, 0)), S5_GEN_SPECS[0],
                  pl.BlockSpec((2, cn, 8, 128), lambda b: (0, 0, b, 0)), S5_GEN_SPECS[1]],
        out_specs=pl.BlockSpec((cn, S5_T, 128), lambda b: (0, 0, b)),
        out_shape=jax.ShapeDtypeStruct((cn, S5_T, S5_WIDTH), F32),
        scratch_shapes=[pltpu.VMEM((S5_BW, S5_BW), BF16), pltpu.VMEM((S5_BW, S5_SW), BF16)],
        compiler_params=_params(("parallel",)),
    )(u3, gg, st, blocks)


def _s5_a_forms(a):
    ra = pltpu.roll(a, S5_STATE, 1)
    low = _iota2(a.shape, 1) < S5_STATE
    return jnp.where(low, a, ra), jnp.where(low, -ra, a)


def _s5_scan(sloc, a16, ncc):
    cn = sloc.shape[1]

    def body(s_ref, a_ref, h_ref):
        forms = [_s5_a_forms(a_ref[dr]) for dr in range(2)]

        def step(s, hs):
            out = []
            for dr in range(2):
                arr, aii = forms[dr]
                h, rh = hs[dr]
                c = s if dr == 0 else jnp.where(s < ncc, ncc - 1 - s, cn - 1 - (s - ncc))
                h_ref[dr, c] = h
                sc = s_ref[dr, c]
                out.append((h * arr + rh * aii + sc, rh * arr - h * aii + pltpu.roll(sc, S5_STATE, 1)))
            return tuple(out)

        zero = jnp.zeros((S5_GROUPS, 128), F32)
        lax.fori_loop(0, cn, step, ((zero, zero), (zero, zero)), unroll=4)

    return pl.pallas_call(
        body, name="s5_scan",
        out_shape=jax.ShapeDtypeStruct(sloc.shape, F32),
        compiler_params=_params(),
    )(sloc, a16)


def _s5_scan_bwd(e, hs, a16, ncc):
    cn = e.shape[1]

    def body(e_ref, h_ref, a_ref, ds_ref, da_ref):
        forms = [_s5_a_forms(a_ref[dr]) for dr in range(2)]
        low = _iota2((S5_GROUPS, 128), 1) < S5_STATE

        def step(s, carry):
            out = []
            r = cn - 1 - s
            for dr in range(2):
                arr, aii = forms[dr]
                g, rg, da = carry[dr]
                c = r if dr == 0 else jnp.where(r < ncc, ncc - 1 - r, cn - 1 - (r - ncc))
                ds_ref[dr, c] = g
                h = h_ref[dr, c]
                rh = pltpu.roll(h, S5_STATE, 1)
                da = da + jnp.where(low, g * h + rg * rh, g * rh - rg * h)
                ec = e_ref[dr, c]
                out.append((ec + g * arr - rg * aii, pltpu.roll(ec, S5_STATE, 1) + rg * arr + g * aii, da))
            return tuple(out)

        zero = jnp.zeros((S5_GROUPS, 128), F32)
        res = lax.fori_loop(0, cn, step, ((zero, zero, zero), (zero, zero, zero)), unroll=4)
        da_ref[0] = res[0][2]
        da_ref[1] = res[1][2]

    return pl.pallas_call(
        body, name="s5_scan_bwd",
        out_shape=[jax.ShapeDtypeStruct(e.shape, F32), jax.ShapeDtypeStruct((2, S5_GROUPS, 128), F32)],
        compiler_params=_params(),
    )(e, hs, a16)


def _s5_bwd_kb(p3, dy3):
    cn = p3.shape[1]
    half = S5_T // 2

    def body(u_ref, d_ref, o_ref):
        q = pl.program_id(1)

        @pl.when(q == 0)
        def _():
            o_ref[...] = jnp.zeros_like(o_ref)

        dk = _dot_tn(u_ref[0], d_ref[0])
        for j in range(S5_T):
            for i in range(half):
                o_ref[0, half * q + i - j + (S5_T - 1)] += dk[128 * j:128 * (j + 1), 128 * i:128 * (i + 1)]

    return pl.pallas_call(
        body, name="s5_bwd_kb", grid=(S5_NB, 2),
        in_specs=[pl.BlockSpec((1, cn, S5_BW), lambda b, q: (b, 0, 0)),
                  pl.BlockSpec((1, cn, S5_BW // 2), lambda b, q: (b, 0, q))],
        out_specs=pl.BlockSpec((1, 2 * S5_T - 1, 128, 128), lambda b, q: (b, 0, 0, 0)),
        out_shape=jax.ShapeDtypeStruct((S5_NB, 2 * S5_T - 1, 128, 128), F32),
        compiler_params=_params(("parallel", "arbitrary")),
    )(p3, dy3)


def _s5_bwd_w(u3, st, name):
    cn = u3.shape[1]

    def body(u_ref, s_ref, w_ref):
        dw = _dot_tn(u_ref[0], _s5_get_groups(s_ref, 0))
        for j in range(S5_T):
            w_ref[0, 0, j] = _s5_contract(dw[128 * j:128 * (j + 1), :])

    return pl.pallas_call(
        body, name=name, grid=(S5_NB, 2),
        in_specs=[pl.BlockSpec((1, cn, S5_BW), lambda b, q: (b, 0, 0)),
                  pl.BlockSpec((1, cn, 8, 128), lambda b, q: (q, 0, b, 0))],
        out_specs=pl.BlockSpec((1, 1, S5_T, 128, 128), lambda b, q: (b, q, 0, 0, 0)),
        out_shape=jax.ShapeDtypeStruct((S5_NB, 2, S5_T, 128, 128), F32),
        compiler_params=_params(("parallel", "parallel")),
    )(u3, st)


def _s5_glu(y_all, w_glu_b, b_glu, nct):
    la = y_all.shape[0]
    tm = TOK_TILE
    l = la - nct * tm

    def body(y_ref, w_ref, b_ref, o_ref):
        yg = _gelu(y_ref[...])
        z = _dot(yg.astype(BF16), w_ref[...]) + b_ref[...]
        o_ref[...] = (yg * _sigmoid(z)).astype(BF16)

    return pl.pallas_call(
        body, name="s5_glu", grid=(l // tm,),
        in_specs=[pl.BlockSpec((tm, S5_WIDTH), lambda i: (i + nct, 0)),
                  _full((S5_WIDTH, S5_WIDTH)), _full((1, S5_WIDTH))],
        out_specs=pl.BlockSpec((tm, S5_WIDTH), lambda i: (i, 0)),
        out_shape=jax.ShapeDtypeStruct((l, S5_WIDTH), BF16),
        compiler_params=_params(("parallel",)),
    )(y_all, w_glu_b, b_glu)


def _s5_glu_bwd(y_all, dmix, w_glu_b, b_glu, nct):
    la = y_all.shape[0]
    tm = TOK_TILE

    def body(y_ref, d_ref, w_ref, b_ref, dy_ref, gw_ref, gb_ref):
        i = pl.program_id(0)

        @pl.when(i == 0)
        def _():
            gw_ref[...] = jnp.zeros_like(gw_ref)
            gb_ref[...] = jnp.zeros_like(gb_ref)

        @pl.when(i < nct)
        def _():
            dy_ref[...] = jnp.zeros_like(dy_ref)

        @pl.when(i >= nct)
        def _():
            y = y_ref[...]
            yg, gelu_vjp = jax.vjp(_gelu, y)
            ygb = yg.astype(BF16)
            sg = _sigmoid(_dot(ygb, w_ref[...]) + b_ref[...])
            ds = d_ref[...]
            dz = ds * yg * sg * (1.0 - sg)
            dzb = dz.astype(BF16)
            dyg = ds * sg + _dot_nt(dzb, w_ref[...])
            dy_ref[...] = gelu_vjp(dyg)[0]
            gw_ref[...] += _dot_tn(ygb, dzb)
            gb_ref[...] += jnp.sum(dz, axis=0, keepdims=True)

    return pl.pallas_call(
        body, name="s5_glu_bwd", grid=(la // tm,),
        in_specs=[pl.BlockSpec((tm, S5_WIDTH), lambda i: (i, 0)),
                  pl.BlockSpec((tm, S5_WIDTH), lambda i: (jnp.maximum(i - nct, 0), 0)),
                  _full((S5_WIDTH, S5_WIDTH)), _full((1, S5_WIDTH))],
        out_specs=[pl.BlockSpec((tm, S5_WIDTH), lambda i: (i, 0)), _full((S5_WIDTH, S5_WIDTH)),
                   _full((1, S5_WIDTH))],
        out_shape=[jax.ShapeDtypeStruct((la, S5_WIDTH), F32), jax.ShapeDtypeStruct((S5_WIDTH, S5_WIDTH), F32),
                   jax.ShapeDtypeStruct((1, S5_WIDTH), F32)],
        compiler_params=_params(("arbitrary",)),
    )(y_all, dmix, w_glu_b, b_glu)


K_SCALE = RET_DH ** -0.5
Q_COL, K_COL, V_COL, G_COL = 4, 8, 12, 16


def _ret_chunk_of(step, ncc, nch, rev):
    if not rev:
        return step
    return jnp.where(step < ncc, ncc - 1 - step, nch - 1 - (step - ncc))


def _ret_decay(ld, rev):
    c = _iota2((RET_CHUNK, RET_CHUNK), 0).astype(F32)
    m = _iota2((RET_CHUNK, RET_CHUNK), 1).astype(F32)
    diff = (m - c) if rev else (c - m)
    keep = (diff > 0) if rev else (diff >= 0)
    expo = jnp.maximum(diff, 0.0)
    dm = jnp.where(keep, jnp.exp(ld * expo), 0.0)
    xi_e = (RET_CHUNK - c) if rev else (c + 1.0)
    zeta_e = c if rev else (RET_CHUNK - 1.0 - c)
    return dm, expo, jnp.exp(ld * xi_e), xi_e, jnp.exp(ld * zeta_e), zeta_e


RET_TABLES = 7


def _ret_tables(ld2):
    def body(ld_ref, t_ref):
        dr, h = pl.program_id(0), pl.program_id(1)
        ldh = ld_ref[dr, h]
        for rev in (False, True):
            @pl.when(dr == int(rev))
            def _(rev=rev):
                dm, expo, xi, xi_e, zeta, zeta_e = _ret_decay(ldh, rev)
                t_ref[0, 0, 0] = dm
                t_ref[0, 0, 1] = dm * expo
                t_ref[0, 0, 2] = xi
                t_ref[0, 0, 3] = xi * xi_e
                t_ref[0, 0, 4] = zeta
                t_ref[0, 0, 5] = zeta * zeta_e
                t_ref[0, 0, 6] = jnp.zeros_like(dm) + jnp.exp(ldh * RET_CHUNK)

    return pl.pallas_call(
        body, name="ret_tables", grid=(2, RET_HEADS),
        in_specs=[pl.BlockSpec(memory_space=pltpu.SMEM)],
        out_specs=pl.BlockSpec((1, 1, RET_TABLES, RET_CHUNK, RET_CHUNK), lambda d, h: (d, h, 0, 0, 0)),
        out_shape=jax.ShapeDtypeStruct((2, RET_HEADS, RET_TABLES, RET_CHUNK, RET_CHUNK), F32),
        compiler_params=_params(("parallel", "parallel")),
    )(ld2)


def _ret_specs(nch, ncc, rev, step_of):
    chunk = lambda n: _ret_chunk_of(step_of(n), ncc, nch, rev)
    cols = [pl.BlockSpec((RET_CHUNK, RET_WIDTH), functools.partial(lambda n, cb: (chunk(n), cb), cb=cb))
            for cb in (1, 2, 3)]
    tab = pl.BlockSpec((RET_CHUNK, RET_DH), lambda n: (chunk(n), 0))
    return cols + [tab, tab], pl.BlockSpec((RET_CHUNK, RET_WIDTH), lambda n: (chunk(n), 0))


def _ret_scan(p_all, cos_t, sin_t, tabs, ncc, placed, kinds):
    la = p_all.shape[0]
    nch = la // RET_CHUNK
    n = len(placed)
    shard_shapes = _gather_shard_shapes(placed, kinds)

    def body(t_ref, qf, kf, vf, cf, sf, qb, kb, vb, cb, sb, *rest):
        of_ref, ob_ref, ssf_ref, ssb_ref = rest[n:n + 4]
        s_scr, send_sems, recv_sems = rest[2 * n + 4:]
        step = pl.program_id(0)

        @pl.when(step == 0)
        def _():
            s_scr[...] = jnp.zeros_like(s_scr)
            for cp in _gather_chip_copies(rest[n + 4:2 * n + 4], kinds, shard_shapes, send_sems, recv_sems, False)[0]:
                cp.start()

        @pl.when(step == nch - 1)
        def _():
            sends, arrivals = _gather_chip_copies(rest[n + 4:2 * n + 4], kinds, shard_shapes, send_sems, recv_sems)
            for cp in arrivals:
                cp.wait_recv()
            for cp in sends:
                cp.wait_send()

        for dr, (q_ref, k_ref, v_ref, c_ref, n_ref, o_ref, ss_ref) in enumerate(
                ((qf, kf, vf, cf, sf, of_ref, ssf_ref), (qb, kb, vb, cb, sb, ob_ref, ssb_ref))):
            cs, sn = c_ref[...], n_ref[...]
            for h in range(RET_HEADS):
                sl = slice(RET_DH * h, RET_DH * (h + 1))
                dm, xi, zeta = t_ref[dr, h, 0], t_ref[dr, h, 2, :, 0:RET_DH], t_ref[dr, h, 4, :, 0:RET_DH]
                q = _rope(q_ref[:, sl], cs, sn)
                k = _rope(k_ref[:, sl] * K_SCALE, cs, sn)
                vh = v_ref[:, sl].astype(BF16)
                s = s_scr[dr, h]
                ss_ref[0, h] = s
                sc = (_dot_nt(q.astype(BF16), k.astype(BF16)) * dm).astype(BF16)
                o_ref[:, sl] = _dot(sc, vh) + _dot((q * xi).astype(BF16), s.astype(BF16))
                s_scr[dr, h] = t_ref[dr, h, 6, 0:RET_DH, 0:RET_DH] * s + _dot_tn((k * zeta).astype(BF16), vh)

    in_f, out_f = _ret_specs(nch, ncc, False, lambda n: n)
    in_b, out_b = _ret_specs(nch, ncc, True, lambda n: n)
    ss_spec = pl.BlockSpec((1, RET_HEADS, RET_DH, RET_DH), lambda n: (n, 0, 0, 0))
    o_shape = jax.ShapeDtypeStruct((la, RET_WIDTH), F32)
    ss_shape = jax.ShapeDtypeStruct((nch, RET_HEADS, RET_DH, RET_DH), F32)
    return pl.pallas_call(
        body, name="ret_scan", grid=(nch,),
        in_specs=[_full(tabs.shape)] + in_f + in_b + [ANY] * n,
        out_specs=[out_f, out_b, ss_spec, ss_spec] + [ANY] * n,
        out_shape=[o_shape, o_shape, ss_shape, ss_shape] + [jax.ShapeDtypeStruct(p.shape, p.dtype) for p in placed],
        input_output_aliases={11 + a: 4 + a for a in range(n)},
        scratch_shapes=[pltpu.VMEM((2, RET_HEADS, RET_DH, RET_DH), F32),
                        pltpu.SemaphoreType.DMA((n, 3)), pltpu.SemaphoreType.DMA((n, 3))],
        compiler_params=_params(("arbitrary",)),
    )(tabs, p_all, p_all, p_all, cos_t, sin_t, p_all, p_all, p_all, cos_t, sin_t, *placed)


def _ret_scan_bwd(p_all, cos_t, sin_t, tabs, ssf, ssb, dy_all, ncc):
    la = p_all.shape[0]
    nch = la // RET_CHUNK

    def body(t_ref, qf, kf, vf, cf, sf, dof, ssf_ref, qb, kb, vb, cb, sb, dob_, ssb_ref,
             dqf, dkf, dvf, dqb, dkb, dvb, dld_ref, ds_scr):
        @pl.when(pl.program_id(0) == 0)
        def _():
            ds_scr[...] = jnp.zeros_like(ds_scr)
            dld_ref[...] = jnp.zeros_like(dld_ref)

        for dr, (q_ref, k_ref, v_ref, c_ref, n_ref, do_ref, ss_ref, dq_ref, dk_ref, dv_ref) in enumerate(
                ((qf, kf, vf, cf, sf, dof, ssf_ref, dqf, dkf, dvf), (qb, kb, vb, cb, sb, dob_, ssb_ref, dqb, dkb, dvb))):
            cs, sn = c_ref[...], n_ref[...]
            for h in range(RET_HEADS):
                sl = slice(RET_DH * h, RET_DH * (h + 1))
                dm, dm_d = t_ref[dr, h, 0], t_ref[dr, h, 1]
                xi, xi_d, zeta, zeta_d = [t_ref[dr, h, t, :, 0:RET_DH] for t in (2, 3, 4, 5)]
                gc = t_ref[dr, h, 6, 0:RET_DH, 0:RET_DH]
                q = _rope(q_ref[:, sl], cs, sn)
                k = _rope(k_ref[:, sl] * K_SCALE, cs, sn)
                q16, k16, v16 = q.astype(BF16), k.astype(BF16), v_ref[:, sl].astype(BF16)
                s = ss_ref[0, h]
                s16 = s.astype(BF16)
                ds_in = ds_scr[dr, h]
                ds16 = ds_in.astype(BF16)
                do16 = do_ref[:, sl].astype(BF16)
                qk = _dot_nt(q16, k16)
                dsv = _dot_nt(do16, v16)
                dsc = (dsv * dm).astype(BF16)
                sc16 = (qk * dm).astype(BF16)
                dos = _dot_nt(do16, s16)
                vds = _dot_nt(v16, ds16)
                dq_ref[:, sl] = _dot(dsc, k16) + dos * xi
                dk_ref[:, sl] = _dot_tn(dsc, q16) + vds * zeta
                dv_ref[:, sl] = _dot_tn(sc16, do16) + _dot((k * zeta).astype(BF16), ds16)
                ds_scr[dr, h] = _dot_tn((q * xi).astype(BF16), do16) + gc * ds_in
                dld = (jnp.sum(dsv * qk * dm_d) + jnp.sum(q * dos * xi_d + k * vds * zeta_d)
                       + RET_CHUNK * jnp.sum(gc * s * ds_in))
                dld_ref[dr, h] += dld

    back = lambda n: nch - 1 - n
    in_f, out_f = _ret_specs(nch, ncc, False, back)
    in_b, out_b = _ret_specs(nch, ncc, True, back)
    ss_spec = pl.BlockSpec((1, RET_HEADS, RET_DH, RET_DH), lambda n: (nch - 1 - n, 0, 0, 0))
    shp = jax.ShapeDtypeStruct((la, RET_WIDTH), F32)
    return pl.pallas_call(
        body, name="ret_scan_bwd", grid=(nch,),
        in_specs=[_full(tabs.shape)] + in_f + [out_f, ss_spec] + in_b + [out_b, ss_spec],
        out_specs=[out_f, out_f, out_f, out_b, out_b, out_b, _full((2, RET_HEADS, 8, 128))],
        out_shape=[shp] * 6 + [jax.ShapeDtypeStruct((2, RET_HEADS, 8, 128), F32)],
        scratch_shapes=[pltpu.VMEM((2, RET_HEADS, RET_DH, RET_DH), F32)],
        compiler_params=_params(("arbitrary",)),
    )(tabs, p_all, p_all, p_all, cos_t, sin_t, dy_all, ssf, p_all, p_all, p_all, cos_t, sin_t, dy_all, ssb)


def _ret_gate(of, ob, p_all, nct):
    la = of.shape[0]
    tm = TOK_TILE
    l = la - nct * tm

    def body(of_ref, ob_ref, g_ref, r_ref, y_ref):
        y = of_ref[...] + ob_ref[...]
        y_ref[...] = y
        for h in range(RET_HEADS):
            sl = slice(RET_DH * h, RET_DH * (h + 1))
            r_ref[:, sl] = _head_norm_gate(y[:, sl], g_ref[:, sl]).astype(BF16)

    row = pl.BlockSpec((tm, RET_WIDTH), lambda i: (i + nct, 0))
    out = pl.BlockSpec((tm, RET_WIDTH), lambda i: (i, 0))
    return pl.pallas_call(
        body, name="ret_gate", grid=(l // tm,),
        in_specs=[row, row, pl.BlockSpec((tm, RET_WIDTH), lambda i: (i + nct, G_COL // 4))],
        out_specs=[out, out],
        out_shape=[jax.ShapeDtypeStruct((l, RET_WIDTH), BF16), jax.ShapeDtypeStruct((l, RET_WIDTH), F32)],
        compiler_params=_params(("parallel",)),
    )(of, ob, p_all)


def _ret_gate_bwd(y_ret, p_all, dmix, nct):
    la = p_all.shape[0]
    tm = TOK_TILE

    def body(y_ref, g_ref, d_ref, dy_ref, dg_ref):
        i = pl.program_id(0)

        @pl.when(i < nct)
        def _():
            dy_ref[...] = jnp.zeros_like(dy_ref)
            dg_ref[...] = jnp.zeros_like(dg_ref)

        @pl.when(i >= nct)
        def _():
            for h in range(RET_HEADS):
                sl = slice(RET_DH * h, RET_DH * (h + 1))
                _, vjp = jax.vjp(_head_norm_gate, y_ref[:, sl], g_ref[:, sl])
                dy, dg = vjp(d_ref[:, sl])
                dy_ref[:, sl] = dy
                dg_ref[:, sl] = dg

    xrow = lambda cb: pl.BlockSpec((tm, RET_WIDTH), lambda i: (jnp.maximum(i - nct, 0), cb))
    out = pl.BlockSpec((tm, RET_WIDTH), lambda i: (i, 0))
    shp = jax.ShapeDtypeStruct((la, RET_WIDTH), F32)
    return pl.pallas_call(
        body, name="ret_gate_bwd", grid=(la // tm,),
        in_specs=[xrow(0), pl.BlockSpec((tm, RET_WIDTH), lambda i: (i, G_COL // 4)), xrow(1)],
        out_specs=[out, out], out_shape=[shp, shp],
        compiler_params=_params(("parallel",)),
    )(y_ret, p_all, dmix)


def _in_bwd(dqf, dkf, dvf, dqb, dkb, dvb, du, dg, cos_t, sin_t, w_in_b, x, ctx, n1w, mod4, dx1):
    l, lc = x.shape[0], ctx.shape[0]
    la = l + lc
    tm = TOK_TILE
    nct = lc // tm

    def body(dqf_ref, dkf_ref, dvf_ref, dqb_ref, dkb_ref, dvb_ref, du_ref, dg_ref, cos_ref, sin_ref,
             w_ref, x_ref, c_ref, nw_ref, mod_ref, dx1_ref, dp_ref, gx_ref, acc_ref):
        i = pl.program_id(0)
        is_ctx = i < nct

        @pl.when(i == 0)
        def _():
            acc_ref[...] = jnp.zeros_like(acc_ref)

        cs, sn = cos_ref[...], sin_ref[...]
        def piece(k, val):
            cols = slice(S5_WIDTH * k, S5_WIDTH * (k + 1))
            dp_ref[:, cols] = val.astype(BF16)
            return _dot_nt(dp_ref[:, cols], w_ref[:, cols])

        dh1 = piece(0, du_ref[...])
        dh1 = dh1 + piece(3, dvf_ref[...] + dvb_ref[...])
        dh1 = dh1 + piece(4, dg_ref[...])
        for k, (f_ref, b_ref, scale) in ((1, (dqf_ref, dqb_ref, 1.0)), (2, (dkf_ref, dkb_ref, K_SCALE))):
            heads = [_rope_t(f_ref[:, RET_DH * h:RET_DH * (h + 1)] + b_ref[:, RET_DH * h:RET_DH * (h + 1)], cs, sn) * scale
                     for h in range(RET_HEADS)]
            dh1 = dh1 + piece(k, jnp.concatenate(heads, axis=1))
        xt = jnp.where(is_ctx, c_ref[...], x_ref[...])
        sh = jnp.where(is_ctx, mod_ref[0:1, :], mod_ref[2:3, :])
        sc = jnp.where(is_ctx, mod_ref[1:2, :], mod_ref[3:4, :])
        _, vjp = jax.vjp(_rms_mod, xt, nw_ref[...], sh, sc)
        dx, dnw, dsh, dsc = vjp(dh1)
        gx_ref[...] = dx + dx1_ref[...]
        cf = jnp.where(is_ctx, 1.0, 0.0)
        acc_ref[0:1, :] += dnw
        acc_ref[1:2, :] += cf * dsh
        acc_ref[2:3, :] += cf * dsc
        acc_ref[3:4, :] += (1.0 - cf) * dsh
        acc_ref[4:5, :] += (1.0 - cf) * dsc

    row = pl.BlockSpec((tm, RET_WIDTH), lambda i: (i, 0))
    tab = pl.BlockSpec((tm, RET_DH), lambda i: (i, 0))
    xrow = pl.BlockSpec((tm, D_MODEL), lambda i: (jnp.maximum(i - nct, 0), 0))
    return pl.pallas_call(
        body, name="in_bwd", grid=(la // tm,),
        in_specs=[row] * 8 + [tab, tab, _full((D_MODEL, IN_COLS)), xrow,
                              pl.BlockSpec((tm, D_MODEL), lambda i: (jnp.minimum(i, nct - 1), 0)),
                              _full((1, D_MODEL)), _full((4, D_MODEL)), xrow],
        out_specs=[pl.BlockSpec((tm, IN_COLS), lambda i: (i, 0)), xrow, _full((8, D_MODEL))],
        out_shape=[jax.ShapeDtypeStruct((la, IN_COLS), BF16), jax.ShapeDtypeStruct((l, D_MODEL), F32),
                   jax.ShapeDtypeStruct((8, D_MODEL), F32)],
        compiler_params=_params(("arbitrary",)),
    )(dqf, dkf, dvf, dqb, dkb, dvb, du, dg, cos_t, sin_t, w_in_b, x, ctx, n1w, mod4, dx1)


def _outproj_up(x, s5x, retx, w_out_b, mod3, n2w, w_up_b):
    l = x.shape[0]
    tm = TOK_TILE

    def body(x_ref, s_ref, r_ref, wo_ref, mod_ref, nw_ref, wu_ref, x1_ref, mix_ref, h2_ref, up_ref):
        mix = _dot(s_ref[...], wo_ref[0:S5_WIDTH, :]) + _dot(r_ref[...], wo_ref[S5_WIDTH:D_MODEL, :])
        mix_ref[...] = mix
        x1 = x_ref[...] + mod_ref[0:1, :] * mix
        x1_ref[...] = x1
        h2 = _rms_mod(x1, nw_ref[...], mod_ref[1:2, :], mod_ref[2:3, :]).astype(BF16)
        h2_ref[...] = h2
        up_ref[...] = _dot(h2, wu_ref[...])

    row = lambda w: pl.BlockSpec((tm, w), lambda i: (i, 0))
    return pl.pallas_call(
        body, name="outproj_up", grid=(l // tm,),
        in_specs=[row(D_MODEL), row(S5_WIDTH), row(RET_WIDTH), _full((D_MODEL, D_MODEL)), _full((3, D_MODEL)),
                  _full((1, D_MODEL)), _full((D_MODEL, 2 * D_FF))],
        out_specs=[row(D_MODEL), row(D_MODEL), row(D_MODEL), row(2 * D_FF)],
        out_shape=[jax.ShapeDtypeStruct((l, D_MODEL), F32), jax.ShapeDtypeStruct((l, D_MODEL), F32),
                   jax.ShapeDtypeStruct((l, D_MODEL), BF16), jax.ShapeDtypeStruct((l, 2 * D_FF), F32)],
        compiler_params=_params(("parallel",)),
    )(x, s5x, retx, w_out_b, mod3, n2w, w_up_b)


HALO = 8


def _conv_taps(g, prev_row, next_row):
    t = g.shape[0]
    r = _iota2(g.shape, 0)
    gprev = jnp.where(r == 0, prev_row, pltpu.roll(g, 1, 0))
    gnext = jnp.where(r == t - 1, next_row, pltpu.roll(g, t - 1, 0))
    return gprev, gnext


def _ffn_loss(up, x1, conv_w, conv_b, w_down_b, gate, fnw, tgt):
    l = x1.shape[0]
    tm = TOK_TILE
    nt = l // tm
    hb = tm // HALO

    cw = 256

    def body(up_a, up_g, hp_ref, hn_ref, x1_ref, cw_ref, cb_ref, wd_ref, gate_ref, fn_ref, tgt_ref,
             act_ref, dx2_ref, ddn_ref, dact_ref, acc_ref):
        i = pl.program_id(0)

        @pl.when(i == 0)
        def _():
            acc_ref[...] = jnp.zeros_like(acc_ref)

        dn = jnp.zeros((tm, D_MODEL), F32)
        for c in range(D_FF // cw):
            cols = slice(cw * c, cw * (c + 1))
            g = up_g[:, cols]
            prev_row = jnp.where(i == 0, 0.0, hp_ref[HALO - 1:HALO, cols])
            next_row = jnp.where(i == nt - 1, 0.0, hn_ref[0:1, cols])
            gprev, gnext = _conv_taps(g, prev_row, next_row)
            gc = cb_ref[:, cols] + gprev * cw_ref[0:1, cols] + g * cw_ref[1:2, cols] + gnext * cw_ref[2:3, cols]
            act = (_gelu(gc) * up_a[:, cols]).astype(BF16)
            act_ref[:, cols] = act
            dn = dn + _dot(act, wd_ref[cols, :])
        x2 = x1_ref[...] + gate_ref[...] * dn
        y, vjp = jax.vjp(_rms, x2, fn_ref[...])
        err = y - tgt_ref[...]
        dx2, dfn = vjp(err * (1.0 / D_MODEL))
        dx2_ref[...] = dx2
        ddn = (dx2 * gate_ref[...]).astype(BF16)
        ddn_ref[...] = ddn
        for c in range(D_FF // cw):
            cols = slice(cw * c, cw * (c + 1))
            dact_ref[:, cols] = _dot_nt(ddn, wd_ref[cols, :])
        acc_ref[0:1, :] += dfn
        acc_ref[1:2, :] += jnp.sum(dx2 * dn, axis=0, keepdims=True)
        acc_ref[2:3, :] += (0.5 / D_MODEL) * jnp.sum(err * err)

    row = lambda w: pl.BlockSpec((tm, w), lambda i: (i, 0))
    last = l // HALO - 1
    return pl.pallas_call(
        body, name="ffn_loss", grid=(nt,),
        in_specs=[pl.BlockSpec((tm, D_FF), lambda i: (i, 0)), pl.BlockSpec((tm, D_FF), lambda i: (i, 1)),
                  pl.BlockSpec((HALO, D_FF), lambda i: (jnp.maximum(i * hb - 1, 0), 1)),
                  pl.BlockSpec((HALO, D_FF), lambda i: (jnp.minimum((i + 1) * hb, last), 1)),
                  row(D_MODEL), _full((3, D_FF)), _full((1, D_FF)), _full((D_FF, D_MODEL)),
                  _full((1, D_MODEL)), _full((1, D_MODEL)), row(D_MODEL)],
        out_specs=[row(D_FF), row(D_MODEL), row(D_MODEL), row(D_FF), _full((8, D_MODEL))],
        out_shape=[jax.ShapeDtypeStruct((l, D_FF), BF16), jax.ShapeDtypeStruct((l, D_MODEL), F32),
                   jax.ShapeDtypeStruct((l, D_MODEL), BF16), jax.ShapeDtypeStruct((l, D_FF), F32),
                   jax.ShapeDtypeStruct((8, D_MODEL), F32)],
        compiler_params=_params(("arbitrary",)),
    )(up, up, up, up, x1, conv_w, conv_b, w_down_b, gate, fnw, tgt)


def _convglu_bwd(up, dact, conv_w, conv_b):
    l = up.shape[0]
    tm = 128
    nt = l // tm
    hb = tm // HALO
    te = tm + 2 * HALO

    def body(a_ref, ap_ref, an_ref, g_ref, gp_ref, gn_ref, d_ref, dp_ref, dn_ref, cw_ref, cb_ref,
             dup_ref, acc_ref):
        i = pl.program_id(0)

        @pl.when(i == 0)
        def _():
            acc_ref[...] = jnp.zeros_like(acc_ref)

        row = _iota2((te, D_FF), 0) + (i * tm - HALO)
        valid = (row >= 0) & (row < l)

        def ext(p, c, n):
            return jnp.where(valid, jnp.concatenate([p[...], c[...], n[...]], axis=0), 0.0)

        ae, ge, de = ext(ap_ref, a_ref, an_ref), ext(gp_ref, g_ref, gn_ref), ext(dp_ref, d_ref, dn_ref)
        gprev = pltpu.roll(ge, 1, 0)
        gnext = pltpu.roll(ge, te - 1, 0)
        w0, w1, w2 = cw_ref[0:1, :], cw_ref[1:2, :], cw_ref[2:3, :]
        gce = cb_ref[...] + gprev * w0 + ge * w1 + gnext * w2
        _, vjp = jax.vjp(lambda a, gc: _gelu(gc) * a, ae, gce)
        dae, dgce = vjp(de)
        dge = dgce * w1 + pltpu.roll(dgce, te - 1, 0) * w0 + pltpu.roll(dgce, 1, 0) * w2
        mid = slice(HALO, HALO + tm)
        dup_ref[:, 0:D_FF] = dae[mid].astype(BF16)
        dup_ref[:, D_FF:2 * D_FF] = dge[mid].astype(BF16)
        dgc = dgce[mid]
        acc_ref[0:1, :] += jnp.sum(dgc * gprev[mid], axis=0, keepdims=True)
        acc_ref[1:2, :] += jnp.sum(dgc * ge[mid], axis=0, keepdims=True)
        acc_ref[2:3, :] += jnp.sum(dgc * gnext[mid], axis=0, keepdims=True)
        acc_ref[3:4, :] += jnp.sum(dgc, axis=0, keepdims=True)

    last = l // HALO - 1

    def trio(cb):
        return [pl.BlockSpec((tm, D_FF), lambda i: (i, cb)),
                pl.BlockSpec((HALO, D_FF), lambda i: (jnp.maximum(i * hb - 1, 0), cb)),
                pl.BlockSpec((HALO, D_FF), lambda i: (jnp.minimum((i + 1) * hb, last), cb))]

    return pl.pallas_call(
        body, name="convglu_bwd", grid=(nt,),
        in_specs=trio(0) + trio(1) + trio(0) + [_full((3, D_FF)), _full((1, D_FF))],
        out_specs=[pl.BlockSpec((tm, 2 * D_FF), lambda i: (i, 0)), _full((8, D_FF))],
        out_shape=[jax.ShapeDtypeStruct((l, 2 * D_FF), BF16), jax.ShapeDtypeStruct((8, D_FF), F32)],
        compiler_params=_params(("arbitrary",)),
    )(up, up, up, up, up, up, dact, dact, dact, conv_w, conv_b)


def _up_bwd(dup, w_up_b, w_out_b, x1, dx2, mix, mod3, n2w, pairs, kinds):
    l = x1.shape[0]
    tm = TOK_TILE
    nt = l // tm
    n = len(pairs)
    shapes = _rs_slot_shapes(pairs, kinds)

    def body(dup_ref, wu_ref, wo_ref, x1_ref, dx2_ref, mix_ref, mod_ref, nw_ref, *rest):
        dx1_ref, dmixb_ref, dmix_ref, acc_ref = rest[n:n + 4]
        exchange = functools.partial(_rs_chip_copies, rest[:n], rest[n + 4:2 * n + 4], kinds, shapes, *rest[2 * n + 4:])
        step = pl.program_id(0)

        @pl.when(step == 0)
        def _():
            acc_ref[...] = jnp.zeros_like(acc_ref)
            for cp in exchange(with_arrivals=False)[0]:
                cp.start()

        @pl.when(step == nt - 1)
        def _():
            sends, arrivals = exchange()
            for cp in arrivals:
                cp.wait_recv()
            for cp in sends:
                cp.wait_send()

        dh2 = _dot_nt(dup_ref[...], wu_ref[...])
        _, vjp = jax.vjp(_rms_mod, x1_ref[...], nw_ref[...], mod_ref[1:2, :], mod_ref[2:3, :])
        dx, dnw, dsh, dsc = vjp(dh2)
        dx1 = dx + dx2_ref[...]
        dx1_ref[...] = dx1
        dmixb = (dx1 * mod_ref[0:1, :]).astype(BF16)
        dmixb_ref[...] = dmixb
        dmix_ref[...] = _dot_nt(dmixb, wo_ref[...])
        acc_ref[0:1, :] += dnw
        acc_ref[1:2, :] += jnp.sum(dx1 * mix_ref[...], axis=0, keepdims=True)
        acc_ref[2:3, :] += dsh
        acc_ref[3:4, :] += dsc

    row = pl.BlockSpec((tm, D_MODEL), lambda i: (i, 0))
    return pl.pallas_call(
        body, name="up_bwd", grid=(nt,),
        in_specs=[pl.BlockSpec((tm, 2 * D_FF), lambda i: (i, 0)), _full((D_MODEL, 2 * D_FF)),
                  _full((D_MODEL, D_MODEL)), row, row, row, _full((3, D_MODEL)), _full((1, D_MODEL))] + [ANY] * n,
        out_specs=[row, row, row, _full((8, D_MODEL))] + [ANY] * n,
        out_shape=[jax.ShapeDtypeStruct((l, D_MODEL), F32), jax.ShapeDtypeStruct((l, D_MODEL), BF16),
                   jax.ShapeDtypeStruct((l, D_MODEL), F32), jax.ShapeDtypeStruct((8, D_MODEL), F32)]
        + [jax.ShapeDtypeStruct((4,) + s, p.dtype) for s, p in zip(shapes, pairs)],
        scratch_shapes=[pltpu.SemaphoreType.DMA((n, 3)), pltpu.SemaphoreType.DMA((n, 3))],
        compiler_params=_params(("arbitrary",)),
    )(dup, w_up_b, w_out_b, x1, dx2, mix, mod3, n2w, *pairs)


MOD_ROWS = 16
MOD_COLS = 6 * D_MODEL // 4


def _mod_fwd(c_all, c_ctx, w_mod_b, b_loc):
    def body(c_ref, cc_ref, w_ref, b_ref, m_ref, s_ref):
        cond = jnp.concatenate([c_ref[...], jnp.broadcast_to(cc_ref[...], (8, D_MODEL))], axis=0)
        s = _silu(cond).astype(BF16)
        s_ref[...] = s
        m_ref[...] = _dot(s, w_ref[...]) + b_ref[...]

    return pl.pallas_call(
        body, name="mod_fwd",
        out_shape=[jax.ShapeDtypeStruct((MOD_ROWS, MOD_COLS), F32), jax.ShapeDtypeStruct((MOD_ROWS, D_MODEL), BF16)],
        compiler_params=_params(),
    )(c_all, c_ctx, w_mod_b, b_loc)


def _mod_bwd_sum(dm_all):
    def body(d_ref, dm_ref, gb_ref):
        rows = [d_ref[k, 0:1, :] for k in range(8)]
        ctx_sum = d_ref[0, 1:2, :]
        for k in range(1, 8):
            ctx_sum = ctx_sum + d_ref[k, 1:2, :]
        gb = ctx_sum
        for k in range(8):
            gb = gb + rows[k]
        gb_ref[...] = gb
        dm_ref[...] = jnp.concatenate(rows + [ctx_sum] + [jnp.zeros((7, 6 * D_MODEL), F32)], axis=0)

    return pl.pallas_call(
        body, name="mod_bwd_sum",
        out_shape=[jax.ShapeDtypeStruct((MOD_ROWS, 6 * D_MODEL), F32), jax.ShapeDtypeStruct((1, 6 * D_MODEL), F32)],
        compiler_params=_params(),
    )(dm_all)


def _mod_bwd_w(dm_loc, s_b, c_ctx, w_mod_b):
    def body(d_ref, s_ref, cc_ref, w_ref, gw_ref, gc_ref):
        db = d_ref[...].astype(BF16)
        gw_ref[...] = _dot_tn(s_ref[...], db)
        ds = _dot_nt(db, w_ref[...])
        _, vjp = jax.vjp(_silu, cc_ref[...])
        gc_ref[...] = jnp.broadcast_to(vjp(ds[8:9, :])[0], (8, D_MODEL))

    return pl.pallas_call(
        body, name="mod_bwd_w",
        out_shape=[jax.ShapeDtypeStruct((D_MODEL, MOD_COLS), F32), jax.ShapeDtypeStruct((8, D_MODEL), F32)],
        compiler_params=_params(),
    )(dm_loc, s_b, c_ctx, w_mod_b)


def _adamw(w, g, m, v, name):
    r, c = w.shape
    tr = _pick(r, (256, 128, 64, 32, 16, 8))
    bc1 = 1.0 - ADAM_B1 ** ADAM_STEP
    bc2 = 1.0 - ADAM_B2 ** ADAM_STEP

    def body(w_ref, g_ref, m_ref, v_ref, d_ref, nm_ref, nv_ref):
        gg = g_ref[...]
        nm = ADAM_B1 * m_ref[...] + (1.0 - ADAM_B1) * gg
        nv = ADAM_B2 * v_ref[...] + (1.0 - ADAM_B2) * (gg * gg)
        nm_ref[...] = nm
        nv_ref[...] = nv
        d_ref[...] = -ADAM_LR * ((nm / bc1) / (jnp.sqrt(nv / bc2) + ADAM_EPS) + ADAM_WD * w_ref[...])

    blk = pl.BlockSpec((tr, c), lambda i: (i, 0))
    shp = jax.ShapeDtypeStruct((r, c), F32)
    return pl.pallas_call(
        body, name=name, grid=(r // tr,), in_specs=[blk] * 4, out_specs=[blk] * 3, out_shape=[shp] * 3,
        compiler_params=_params(("parallel",)),
    )(w, g, m, v)


def _sum_slots(a, name):
    n, r, c = a.shape
    tr = _pick(r, (376, 256, 208, 128, 64, 32, 16, 8))

    def body(a_ref, o_ref):
        acc = a_ref[0].astype(F32)
        for k in range(1, n):
            acc = acc + a_ref[k].astype(F32)
        o_ref[...] = acc

    return pl.pallas_call(
        body, name=name, grid=(r // tr,),
        in_specs=[pl.BlockSpec((n, tr, c), lambda i: (0, i, 0))],
        out_specs=pl.BlockSpec((tr, c), lambda i: (i, 0)),
        out_shape=jax.ShapeDtypeStruct((r, c), F32),
        compiler_params=_params(("parallel",)),
    )(a)


def _mesh_pos():
    return lax.axis_index("x"), lax.axis_index("y"), lax.axis_index("c")


def _all_gather8(v, name):
    m_per, n = v.shape

    def body(x_ref, out_ref, send_sems, recv_sems, local_sem):
        x, y, c = _mesh_pos()
        me, sibling = (x, y, c), (x, y, 1 - c)
        chips = [(1 - x, y), (x, 1 - y), (1 - x, 1 - y)]

        def rows(px, py, pc):
            return out_ref.at[pl.ds((4 * px + 2 * py + pc) * m_per, m_per), :]

        def copy(k, block, to, src=None):
            return pltpu.make_async_remote_copy(
                src_ref=rows(*block) if src is None else src, dst_ref=rows(*block),
                send_sem=send_sems.at[k], recv_sem=recv_sems.at[k], device_id=to, device_id_type=MESH_ID)

        mine = pltpu.make_async_copy(x_ref, rows(*me), local_sem)
        mine.start()
        first = [copy(0, me, sibling, src=x_ref)]
        first += [copy(1 + j, me, (*chip, c), src=x_ref) for j, chip in enumerate(chips)]
        for cp in first:
            cp.start()
        passed = [copy(4 + j, (*chip, c), sibling) for j, chip in enumerate(chips)]
        for j, chip in enumerate(chips):
            copy(1 + j, (*chip, c), me).wait_recv()
            passed[j].start()
        copy(0, sibling, me).wait_recv()
        for j, chip in enumerate(chips):
            copy(4 + j, (*chip, 1 - c), me).wait_recv()
        for cp in first + passed:
            cp.wait_send()
        mine.wait()

    return pl.pallas_call(
        body, name=name,
        out_shape=jax.ShapeDtypeStruct((8 * m_per, n), v.dtype),
        in_specs=[pl.BlockSpec(memory_space=pltpu.VMEM)],
        out_specs=pl.BlockSpec(memory_space=pltpu.VMEM),
        scratch_shapes=[pltpu.SemaphoreType.DMA((7,)), pltpu.SemaphoreType.DMA((7,)), pltpu.SemaphoreType.DMA],
        compiler_params=_params(),
    )(v)


ANY = pl.BlockSpec(memory_space=pl.ANY)
PEER_CHIPS = lambda x, y: [(x, 1 - y), (1 - x, y), (1 - x, 1 - y)]


def _shard_region(ref, kind, k, rl, cl, r0, nr, c0, nc):
    if kind == "col":
        return ref.at[pl.ds(r0, nr), pl.ds(k * cl + c0, nc)]
    return ref.at[pl.ds(k * rl + r0, nr), pl.ds(c0, nc)]


def _place_shard(w, kind, chip, name):
    rl, cl = w.shape
    tr = _pick(rl, (256, 128, 64))
    nt = rl // tr

    def body(chip_ref, w_ref, o_ref):
        o_ref[...] = w_ref[...].astype(BF16)

    o_map = (lambda i, chip_ref: (i, chip_ref[0])) if kind == "col" else (lambda i, chip_ref: (chip_ref[0] * nt + i, 0))
    return pl.pallas_call(
        body, name=name,
        grid_spec=pltpu.PrefetchScalarGridSpec(
            num_scalar_prefetch=1, grid=(nt,),
            in_specs=[pl.BlockSpec((tr, cl), lambda i, chip_ref: (i, 0))], out_specs=pl.BlockSpec((tr, cl), o_map)),
        out_shape=jax.ShapeDtypeStruct((rl, 4 * cl) if kind == "col" else (4 * rl, cl), BF16),
        compiler_params=_params(("parallel",)),
    )(chip.reshape(1), w)


def _gather_shard_shapes(placed, kinds):
    return [(p.shape[0], p.shape[1] // 4) if k == "col" else (p.shape[0] // 4, p.shape[1]) for p, k in zip(placed, kinds)]


def _gather_chip_copies(outs, kinds, shard_shapes, send_sems, recv_sems, with_arrivals=True):
    x, y, c = _mesh_pos()
    me = 2 * x + y
    sends, arrivals = [], []
    for a in range(len(outs)):
        rl, cl = shard_shapes[a]
        rh = rl // 2
        reg = functools.partial(_shard_region, outs[a], kinds[a], rl=rl, cl=cl, r0=c * rh, nr=rh, c0=0, nc=cl)
        for j, (px, py) in enumerate(PEER_CHIPS(x, y)):
            to = dict(send_sem=send_sems.at[a, j], recv_sem=recv_sems.at[a, j], device_id=(px, py, c),
                      device_id_type=MESH_ID)
            sends.append(pltpu.make_async_remote_copy(src_ref=reg(k=me), dst_ref=reg(k=me), **to))
            if with_arrivals:
                got = reg(k=2 * px + py)
                arrivals.append(pltpu.make_async_remote_copy(src_ref=got, dst_ref=got, **to))
    return sends, arrivals


def _gather_sibling_copies(outs, kinds, shard_shapes, send_sems, recv_sems):
    x, y, c = _mesh_pos()
    forwards, arrivals = [], []
    for a in range(len(outs)):
        rl, cl = shard_shapes[a]
        rh = rl // 2
        for j, (px, py) in enumerate(PEER_CHIPS(x, y)):
            to = dict(send_sem=send_sems.at[a, j], recv_sem=recv_sems.at[a, j], device_id=(x, y, 1 - c),
                      device_id_type=MESH_ID)
            reg = functools.partial(_shard_region, outs[a], kinds[a], k=2 * px + py, rl=rl, cl=cl, nr=rh, c0=0, nc=cl)
            forwards.append(pltpu.make_async_remote_copy(src_ref=reg(r0=c * rh), dst_ref=reg(r0=c * rh), **to))
            arrivals.append(pltpu.make_async_remote_copy(src_ref=reg(r0=(1 - c) * rh), dst_ref=reg(r0=(1 - c) * rh), **to))
    return forwards, arrivals


def _gather_weights(placed, kinds):
    n = len(placed)
    shard_shapes = _gather_shard_shapes(placed, kinds)

    def body(*refs):
        outs = refs[n:2 * n]
        ici_send, ici_recv, sib_send, sib_recv = refs[2 * n:]
        sends, arrivals = _gather_chip_copies(outs, kinds, shard_shapes, ici_send, ici_recv)
        for cp in sends:
            cp.start()
        forwards, from_sibling = _gather_sibling_copies(outs, kinds, shard_shapes, sib_send, sib_recv)
        for cp, fwd in zip(arrivals, forwards):
            cp.wait_recv()
            fwd.start()
        for cp in from_sibling:
            cp.wait_recv()
        for cp in sends + forwards:
            cp.wait_send()

    return pl.pallas_call(
        body, name="gather_weights",
        out_shape=[jax.ShapeDtypeStruct(p.shape, p.dtype) for p in placed],
        in_specs=[ANY] * n, out_specs=[ANY] * n, input_output_aliases={a: a for a in range(n)},
        scratch_shapes=[pltpu.SemaphoreType.DMA((n, 3))] * 4,
        compiler_params=_params(),
    )(*placed)


def _gather_sibling(placed, kinds):
    n = len(placed)
    shard_shapes = _gather_shard_shapes(placed, kinds)

    def body(*refs):
        forwards, from_sibling = _gather_sibling_copies(refs[n:2 * n], kinds, shard_shapes, *refs[2 * n:])
        for cp in forwards:
            cp.start()
        for cp in from_sibling:
            cp.wait_recv()
        for cp in forwards:
            cp.wait_send()

    return pl.pallas_call(
        body, name="gather_sibling",
        out_shape=[jax.ShapeDtypeStruct(p.shape, p.dtype) for p in placed],
        in_specs=[ANY] * n, out_specs=[ANY] * n, input_output_aliases={a: a for a in range(n)},
        scratch_shapes=[pltpu.SemaphoreType.DMA((n, 3))] * 2,
        compiler_params=_params(),
    )(*placed)


def _half(kind, r, c):
    return (r // 2, c) if kind == "col" else (r, c // 2)


def _half_of(ref, kind, which):
    r, c = ref.shape
    hr, hc = _half(kind, r, c)
    return ref.at[pl.ds(which * hr, hr), :] if kind == "col" else ref.at[:, pl.ds(which * hc, hc)]


def _rs_sibling(grads, kinds, name):
    n = len(grads)

    def body(*refs):
        srcs, dsts = refs[:n], refs[n:2 * n]
        send_sems, recv_sems = refs[2 * n:]
        x, y, c = _mesh_pos()
        cps = [pltpu.make_async_remote_copy(src_ref=_half_of(srcs[a], kinds[a], 1 - c), dst_ref=dsts[a],
                                            send_sem=send_sems.at[a], recv_sem=recv_sems.at[a],
                                            device_id=(x, y, 1 - c), device_id_type=MESH_ID) for a in range(n)]
        for cp in cps:
            cp.start()
        for cp in cps:
            cp.wait()

    return pl.pallas_call(
        body, name=name,
        out_shape=[jax.ShapeDtypeStruct(_half(k, *g.shape), g.dtype) for g, k in zip(grads, kinds)],
        in_specs=[ANY] * n, out_specs=[ANY] * n,
        scratch_shapes=[pltpu.SemaphoreType.DMA((n,)), pltpu.SemaphoreType.DMA((n,))],
        compiler_params=_params(),
    )(*grads)


def _pair_sum(gf, rv, kind, ci, name):
    r, c = rv.shape
    tr = _pick(r, (128, 64, 32, 16, 8))
    nt = r // tr

    def body(ci_ref, g_ref, r_ref, o_ref):
        o_ref[...] = (g_ref[...] + r_ref[...]).astype(BF16)

    g_map = (lambda i, ci_ref: (ci_ref[0] * nt + i, 0)) if kind == "col" else (lambda i, ci_ref: (i, ci_ref[0]))
    blk = pl.BlockSpec((tr, c), lambda i, ci_ref: (i, 0))
    return pl.pallas_call(
        body, name=name,
        grid_spec=pltpu.PrefetchScalarGridSpec(num_scalar_prefetch=1, grid=(nt,),
                                               in_specs=[pl.BlockSpec((tr, c), g_map), blk], out_specs=blk),
        out_shape=jax.ShapeDtypeStruct((r, c), BF16),
        compiler_params=_params(("parallel",)),
    )(ci.reshape(1), gf, rv)


def _rs_slot_shapes(pairs, kinds):
    return [(p.shape[0], p.shape[1] // 4) if k == "col" else (p.shape[0] // 4, p.shape[1]) for p, k in zip(pairs, kinds)]


def _rs_chip_copies(srcs, dsts, kinds, shapes, send_sems, recv_sems, with_arrivals=True):
    x, y, c = _mesh_pos()
    me = 2 * x + y
    sends, arrivals = [], []
    for a in range(len(srcs)):
        rl, cl = shapes[a]
        reg = functools.partial(_shard_region, srcs[a], kinds[a], rl=rl, cl=cl, r0=0, nr=rl, c0=0, nc=cl)
        for j, (px, py) in enumerate(PEER_CHIPS(x, y)):
            to = dict(send_sem=send_sems.at[a, j], recv_sem=recv_sems.at[a, j], device_id=(px, py, c),
                      device_id_type=MESH_ID)
            sends.append(pltpu.make_async_remote_copy(src_ref=reg(k=2 * px + py), dst_ref=dsts[a].at[me], **to))
            if with_arrivals:
                slot = dsts[a].at[2 * px + py]
                arrivals.append(pltpu.make_async_remote_copy(src_ref=slot, dst_ref=slot, **to))
    return sends, arrivals


def _rs_chips(pairs, kinds):
    n = len(pairs)
    shapes = _rs_slot_shapes(pairs, kinds)

    def body(*refs):
        sends, arrivals = _rs_chip_copies(refs[:n], refs[n:2 * n], kinds, shapes, *refs[2 * n:])
        for cp in sends:
            cp.start()
        for cp in arrivals:
            cp.wait_recv()
        for cp in sends:
            cp.wait_send()

    return pl.pallas_call(
        body, name="rs_chips",
        out_shape=[jax.ShapeDtypeStruct((4,) + s, p.dtype) for s, p in zip(shapes, pairs)],
        in_specs=[ANY] * n, out_specs=[ANY] * n,
        scratch_shapes=[pltpu.SemaphoreType.DMA((n, 3)), pltpu.SemaphoreType.DMA((n, 3))],
        compiler_params=_params(),
    )(*pairs)


def _sum_chips(pair, got, kind, pos, name):
    _, r, c = got.shape
    tr = _pick(r, (256, 128, 64, 32, 16))
    nt = r // tr

    def body(pos_ref, own_ref, g1_ref, g2_ref, g3_ref, o_ref):
        o_ref[...] = ((own_ref[...].astype(F32) + g1_ref[0].astype(F32)) + g2_ref[0].astype(F32)) + g3_ref[0].astype(F32)

    if kind == "col":
        own_map = lambda i, p: (i, p[1])
        out_map = lambda i, p: (p[0] * nt + i, 0)
        out_shape = (2 * r, c)
    else:
        own_map = lambda i, p: (p[1] * nt + i, 0)
        out_map = lambda i, p: (i, p[0])
        out_shape = (r, 2 * c)
    peer = lambda m: pl.BlockSpec((1, tr, c), lambda i, p: (p[1] ^ m, i, 0))
    return pl.pallas_call(
        body, name=name,
        grid_spec=pltpu.PrefetchScalarGridSpec(
            num_scalar_prefetch=1, grid=(nt,),
            in_specs=[pl.BlockSpec((tr, c), own_map), peer(1), peer(2), peer(3)],
            out_specs=pl.BlockSpec((tr, c), out_map)),
        out_shape=jax.ShapeDtypeStruct(out_shape, F32),
        compiler_params=_params(("parallel",)),
    )(pos, pair, got, got, got)


def _rs_back(halves, kinds):
    n = len(halves)

    def body(*refs):
        outs = refs[n:2 * n]
        send_sems, recv_sems = refs[2 * n:]
        x, y, c = _mesh_pos()
        cps = []
        for a in range(n):
            mine = _half_of(outs[a], kinds[a], c)
            cps.append(pltpu.make_async_remote_copy(src_ref=mine, dst_ref=mine, send_sem=send_sems.at[a],
                                                    recv_sem=recv_sems.at[a], device_id=(x, y, 1 - c),
                                                    device_id_type=MESH_ID))
            cps[-1].start()
        for a in range(n):
            other = _half_of(outs[a], kinds[a], 1 - c)
            pltpu.make_async_remote_copy(src_ref=other, dst_ref=other, send_sem=send_sems.at[a],
                                         recv_sem=recv_sems.at[a], device_id=(x, y, 1 - c),
                                         device_id_type=MESH_ID).wait_recv()
        for cp in cps:
            cp.wait_send()

    return pl.pallas_call(
        body, name="rs_back",
        out_shape=[jax.ShapeDtypeStruct(h.shape, h.dtype) for h in halves],
        in_specs=[ANY] * n, out_specs=[ANY] * n, input_output_aliases={a: a for a in range(n)},
        scratch_shapes=[pltpu.SemaphoreType.DMA((n,)), pltpu.SemaphoreType.DMA((n,))],
        compiler_params=_params(),
    )(*halves)


def _rope_tables(l, lc):
    rows = l // GRID_W
    row = jnp.repeat(jnp.arange(rows, dtype=F32), GRID_W)
    col = jnp.tile(jnp.arange(GRID_W, dtype=F32), rows)
    n_freq = RET_DH // 4
    inv_freq = ROPE_THETA ** (-jnp.arange(n_freq, dtype=F32) / n_freq)
    ang = jnp.concatenate([row[:, None] * inv_freq, col[:, None] * inv_freq], axis=-1)
    cos_t = jnp.repeat(jnp.cos(ang), 2, axis=-1)
    sin_t = jnp.repeat(jnp.sin(ang), 2, axis=-1) * jnp.tile(jnp.array([-1.0, 1.0], F32), RET_DH // 2)
    cos_t = jnp.concatenate([jnp.ones((lc, RET_DH), F32), cos_t], axis=0)
    sin_t = jnp.concatenate([jnp.zeros((lc, RET_DH), F32), sin_t], axis=0)
    return cos_t, sin_t


def _s5_pack(a):
    blk = lambda t: t.reshape(1, S5_NB, 128, S5_STATE)
    lre = jnp.stack([a["s5_lambda_re_f"][0], a["s5_lambda_re_b"][0]]).reshape(2, S5_NB, 8, S5_STATE)
    lim = jnp.stack([a["s5_lambda_im_f"][0], a["s5_lambda_im_b"][0]]).reshape(2, S5_NB, 8, S5_STATE)
    lst = jnp.stack([a["s5_log_step_f"][0], a["s5_log_step_b"][0]]).reshape(2, S5_NB, 8, 1)
    b_re = blk(a["s5_b_re"][0].transpose(0, 2, 1))
    b_im = blk(a["s5_b_im"][0].transpose(0, 2, 1))
    return (lre, lim, lst, b_re, b_im, blk(a["s5_c_re"][0]), blk(a["s5_c_im"][0]),
            a["s5_d"].reshape(1, S5_NB, 1, 128))


def _s5_unpack(g):
    glre, glim, glst, gbre, gbim, gcre, gcim, gd = g
    unb = lambda t: t.reshape(S5_GROUPS, S5_GROUP, S5_STATE).transpose(0, 2, 1)[None]
    return {
        "s5_lambda_re_f": glre[0].reshape(1, S5_GROUPS, S5_STATE), "s5_lambda_re_b": glre[1].reshape(1, S5_GROUPS, S5_STATE),
        "s5_lambda_im_f": glim[0].reshape(1, S5_GROUPS, S5_STATE), "s5_lambda_im_b": glim[1].reshape(1, S5_GROUPS, S5_STATE),
        "s5_log_step_f": glst[0].reshape(1, S5_GROUPS), "s5_log_step_b": glst[1].reshape(1, S5_GROUPS),
        "s5_b_re": unb(gbre), "s5_b_im": unb(gbim),
        "s5_c_re": gcre.reshape(1, S5_GROUPS, S5_GROUP, S5_STATE), "s5_c_im": gcim.reshape(1, S5_GROUPS, S5_GROUP, S5_STATE),
        "s5_d": gd.reshape(1, S5_WIDTH),
    }


def _local_step(a, wb, late, mx, mc, conv_w, ci):
    x, ctx, tgt = a["x"][0], a["ctx"][0], a["loss_target"][0]
    l, lc = x.shape[0], ctx.shape[0]
    la = l + lc
    nct, ncc, nrc, cn = lc // TOK_TILE, lc // S5_T, lc // RET_CHUNK, la // S5_T
    n1w, n2w, fnw = a["norm1_w"], a["norm2_w"], a["final_norm_w"].reshape(1, D_MODEL)
    conv_b, b_glu = a["conv_b"], a["s5_b_glu"]
    ld2 = jnp.concatenate([a["ret_log_decay_f"], a["ret_log_decay_b"]], axis=0)
    mod4 = jnp.concatenate([mc[0:2], mx[0:2]], axis=0)
    mod3 = mx[2:5]
    gate5 = mx[5:6]
    cos_t, sin_t = _rope_tables(l, lc)
    s5p = _s5_pack(a)

    p_all, h1b, w_out_p, w_down_p = _norm_inproj(x, ctx, n1w, mod4, wb["w_in"], [late[0], late[2]],
                                                 (LATE_KINDS[0], LATE_KINDS[2]))
    p3 = _s5_rows(p_all[:, 0:S5_WIDTH])
    gg, xw, yw, a16 = _s5_gen(*s5p)
    sloc = _s5_to_states(p3, xw, "s5_state")
    a16s = a16.transpose(1, 0, 2, 3).reshape(2, S5_GROUPS, 128)
    hs = _s5_scan(sloc, a16s, ncc)
    y_all = _s5_from_states(p3, gg, hs, yw, False, "s5_out").reshape(la, S5_WIDTH)
    s5x = _s5_glu(y_all, wb["s5_w_glu"], b_glu, nct)
    tabs = _ret_tables(ld2)
    of, ob, ssf, ssb, w_up_p = _ret_scan(p_all, cos_t, sin_t, tabs, nrc, [late[1]], (LATE_KINDS[1],))
    wb = {**wb, **dict(zip(LATE_NAMES, _gather_sibling([w_out_p, w_up_p, w_down_p], LATE_KINDS)))}
    retx, y_ret = _ret_gate(of, ob, p_all, nct)
    x1, mix, h2b, up = _outproj_up(x, s5x, retx, wb["w_out"], mod3, n2w, wb["w_up"])
    act, dx2, ddn, dact, acc_f = _ffn_loss(up, x1, conv_w, conv_b, wb["w_down"], gate5, fnw, tgt)

    g = {}
    g["w_down"] = _mm_tn(act, ddn, name="gw_down")
    dup, acc_c = _convglu_bwd(up, dact, conv_w, conv_b)
    g["w_up"] = _mm_tn(h2b, dup, name="gw_up")
    first = [g[n] for n in FIRST_GRADS]
    first_pairs = [_pair_sum(gf, rv, k, ci, "rs_pair_" + n)
                   for gf, rv, k, n in zip(first, _rs_sibling(first, FIRST_KINDS, "rs_sibling_first"), FIRST_KINDS, FIRST_GRADS)]
    dx1, dmixb, dmix, acc_2, *first_got = _up_bwd(dup, wb["w_up"], wb["w_out"], x1, dx2, mix, mod3, n2w,
                                                  first_pairs, FIRST_KINDS)
    g["w_out"] = jnp.concatenate([_mm_tn(s5x, dmixb, name="gw_out_s5"), _mm_tn(retx, dmixb, name="gw_out_ret")], axis=0)

    dy_s5, g["s5_w_glu"], g["s5_b_glu"] = _s5_glu_bwd(y_all, dmix, wb["s5_w_glu"], b_glu, nct)
    dy3 = _s5_rows(dy_s5)
    e = _s5_to_states(dy3, yw, "s5_bwd_h")
    ds, da16 = _s5_scan_bwd(e, hs, a16s, ncc)
    du = _s5_from_states(dy3, gg, ds, xw, True, "s5_bwd_u").reshape(la, S5_WIDTH)
    dkb = _s5_bwd_kb(p3, dy3)
    dwst = _s5_bwd_w(p3, ds, "s5_bwd_wst")
    dwout = _s5_bwd_w(dy3, hs, "s5_bwd_wout")
    da16p = da16.reshape(2, S5_NB, 8, 128).transpose(1, 0, 2, 3)
    g.update(_s5_unpack(_s5_gen_bwd(*s5p, dkb, dwst, dwout, da16p)))

    dy_ret, dg = _ret_gate_bwd(y_ret, p_all, dmix, nct)
    dqf, dkf, dvf, dqb, dkb_, dvb, dld = _ret_scan_bwd(p_all, cos_t, sin_t, tabs, ssf, ssb, dy_ret, nrc)
    g["ret_log_decay_f"] = dld[0, :, 0, 0].reshape(1, RET_HEADS)
    g["ret_log_decay_b"] = dld[1, :, 0, 0].reshape(1, RET_HEADS)
    dp, grad_x, acc_1 = _in_bwd(dqf, dkf, dvf, dqb, dkb_, dvb, du, dg, cos_t, sin_t, wb["w_in"], x, ctx, n1w, mod4, dx1)
    g["w_in"] = _mm_tn(h1b, dp, name="gw_in")

    g["norm1_w"], g["norm2_w"], g["final_norm_w"] = acc_1[0:1], acc_2[0:1], acc_f[0]
    g["conv_w"], g["conv_b"] = acc_c[0:3], acc_c[3:4]
    zero = jnp.zeros((1, D_MODEL), F32)
    dmx = jnp.concatenate([acc_1[3:5], acc_2[1:2], acc_2[2:4], acc_f[1:2]], axis=0)
    dmc = jnp.concatenate([acc_1[1:3], zero, zero, zero, zero], axis=0)
    return acc_f[2, 0], grad_x, g, dmx, dmc, first_pairs, first_got


WEIGHT_NAMES = ("c_ctx", "w_mod", "b_mod", "norm1_w", "w_in", "s5_lambda_re_f", "s5_lambda_im_f", "s5_log_step_f",
                "s5_lambda_re_b", "s5_lambda_im_b", "s5_log_step_b", "s5_b_re", "s5_b_im", "s5_c_re", "s5_c_im",
                "s5_d", "s5_w_glu", "s5_b_glu", "ret_log_decay_f", "ret_log_decay_b", "w_out", "norm2_w", "w_up",
                "conv_w", "conv_b", "w_down", "final_norm_w")
BIG_NAMES = ("w_in", "w_out", "w_up", "w_down", "s5_w_glu")
BIG_KINDS = ("col", "row", "col", "row", "row")
EARLY_NAMES, EARLY_KINDS = ("w_in", "s5_w_glu"), ("col", "row")
LATE_NAMES, LATE_KINDS = ("w_out", "w_up", "w_down"), ("row", "col", "row")
FIRST_GRADS, FIRST_KINDS = ("w_down", "w_up"), ("row", "col")
LAST_GRADS, LAST_KINDS = ("w_in", "w_out", "s5_w_glu"), ("col", "row", "row")
SMALL_NAMES = ("norm1_w", "norm2_w", "final_norm_w", "conv_b", "conv_w", "s5_lambda_re_f", "s5_lambda_im_f",
               "s5_log_step_f", "s5_lambda_re_b", "s5_lambda_im_b", "s5_log_step_b", "s5_b_re", "s5_b_im", "s5_c_re",
               "s5_c_im", "s5_d", "s5_b_glu", "ret_log_decay_f", "ret_log_decay_b")
ROW = 1024
N_CHIPS = 4


def _pack_rows(parts):
    flat = jnp.concatenate([p.reshape(-1) for p in parts])
    n = flat.shape[0]
    rows = -(-n // (8 * ROW)) * 8
    return jnp.pad(flat, (0, rows * ROW - n)).reshape(rows, ROW)


def _unpack_rows(packed, shapes):
    flat = packed.reshape(-1)
    out, off = [], 0
    for s in shapes:
        n = math.prod(s)
        out.append(flat[off:off + n].reshape(s))
        off += n
    return out


def _step(a):
    xi, yi, ci = _mesh_pos()
    chip = 2 * xi + yi
    dev = 2 * chip + ci

    cw_loc = a["conv_w"].reshape(-1)
    small_in = jnp.concatenate([a["c"].reshape(-1), jnp.pad(cw_loc, (0, 24 * 128 - cw_loc.shape[0]))]).reshape(32, 128)
    sg = _all_gather8(small_in, "gather_cond").reshape(8, 32, 128)
    c_all = sg[:, 0:8].reshape(8, D_MODEL)
    conv_w = sg[0::2, 8:32].reshape(N_CHIPS, -1)[:, :cw_loc.shape[0]].reshape(N_CHIPS, 3, -1)
    conv_w = conv_w.transpose(1, 0, 2).reshape(3, D_FF)

    placed = {n: _place_shard(a[n][0], k, chip, "place_" + n) for n, k in zip(BIG_NAMES, BIG_KINDS)}
    wb = dict(zip(EARLY_NAMES, _gather_weights([placed[n] for n in EARLY_NAMES], EARLY_KINDS)))
    late = [placed[n] for n in LATE_NAMES]

    w_mod_b = a["w_mod"][0].astype(BF16)
    c_ctx = a["c_ctx"].reshape(1, D_MODEL)
    b_loc = lax.dynamic_slice_in_dim(a["b_mod"], chip * MOD_COLS, MOD_COLS, 1)
    m_loc, s_b = _mod_fwd(c_all, c_ctx, w_mod_b, b_loc)
    mg = _all_gather8(m_loc, "gather_mod").reshape(8, MOD_ROWS, MOD_COLS)
    m_full = mg[0::2].transpose(1, 0, 2).reshape(MOD_ROWS, 6 * D_MODEL)
    mx = lax.dynamic_slice_in_dim(m_full, dev, 1, 0).reshape(6, D_MODEL)
    mc = m_full[8].reshape(6, D_MODEL)

    loss_part, grad_x, g, dmx, dmc, first_pairs, first_got = _local_step(a, wb, late, mx, mc, conv_w, ci)
    loss = lax.psum(loss_part, ("x", "y", "c"))

    dm_pair = jnp.concatenate([dmx.reshape(1, -1), dmc.reshape(1, -1), jnp.zeros((6, 6 * D_MODEL), F32)], axis=0)
    dm_all = _all_gather8(dm_pair, "gather_dmod").reshape(8, 8, 6 * D_MODEL)
    dm16, gb_mod = _mod_bwd_sum(dm_all)
    dm_loc = lax.dynamic_slice_in_dim(dm16, chip * MOD_COLS, MOD_COLS, 1)
    gw_mod, gcc = _mod_bwd_w(dm_loc, s_b, c_ctx, w_mod_b)

    small_parts = [g[n] for n in SMALL_NAMES] + [gcc[0]]
    small_shapes = [p.shape for p in small_parts]
    sp = _pack_rows(small_parts)
    tot = _sum_slots(_all_gather8(sp, "gather_small_grads").reshape(8, sp.shape[0], ROW), "sum_small_grads")
    small = dict(zip(SMALL_NAMES + ("c_ctx",), _unpack_rows(tot, small_shapes)))
    grads = {n: small[n].reshape(a[n].shape) for n in SMALL_NAMES if n != "conv_w"}
    grads["c_ctx"] = (0.5 * small["c_ctx"]).reshape(a["c_ctx"].shape)
    grads["conv_w"] = lax.dynamic_slice_in_dim(small["conv_w"], chip * (D_FF // N_CHIPS), D_FF // N_CHIPS, 1)[None]
    grads["b_mod"] = gb_mod
    grads["w_mod"] = gw_mod[None]

    last = [g[n] for n in LAST_GRADS]
    last_pairs = [_pair_sum(gf, rv, k, ci, "rs_pair_" + n)
                  for gf, rv, k, n in zip(last, _rs_sibling(last, LAST_KINDS, "rs_sibling_last"), LAST_KINDS, LAST_GRADS)]
    last_got = _rs_chips(last_pairs, LAST_KINDS)
    pos = jnp.stack([ci, chip])
    order = FIRST_GRADS + LAST_GRADS
    order_kinds = FIRST_KINDS + LAST_KINDS
    halves = [_sum_chips(p, t, k, pos, "rs_sum_" + n)
              for p, t, k, n in zip(first_pairs + last_pairs, list(first_got) + list(last_got), order_kinds, order)]
    for n, t in zip(order, _rs_back(halves, order_kinds)):
        grads[n] = t[None]

    delta, new_m, new_v = {}, {}, {}
    for n in BIG_NAMES + ("w_mod",):
        for dst, t in zip((delta, new_m, new_v), _adamw(a[n][0], grads[n][0], a["m_" + n][0], a["v_" + n][0], "adamw_" + n)):
            dst[n] = t[None]
    rest = [n for n in WEIGHT_NAMES if n not in BIG_NAMES and n != "w_mod"]
    shapes = [a[n].shape for n in rest]
    pr = lambda pre: _pack_rows([a[pre + n] for n in rest])
    for dst, t in zip((delta, new_m, new_v),
                      _adamw(pr(""), _pack_rows([grads[n] for n in rest]), pr("m_"), pr("v_"), "adamw_small")):
        dst.update(zip(rest, _unpack_rows(t, shapes)))

    return (loss, grad_x[None], *[grads[n] for n in WEIGHT_NAMES], *[delta[n] for n in WEIGHT_NAMES],
            *[new_m[n] for n in WEIGHT_NAMES], *[new_v[n] for n in WEIGHT_NAMES])


def kernel(x, c, ctx, c_ctx, w_mod, b_mod, norm1_w, w_in, s5_lambda_re_f, s5_lambda_im_f, s5_log_step_f, s5_lambda_re_b, s5_lambda_im_b, s5_log_step_b, s5_b_re, s5_b_im, s5_c_re, s5_c_im, s5_d, s5_w_glu, s5_b_glu, ret_log_decay_f, ret_log_decay_b, w_out, norm2_w, w_up, conv_w, conv_b, w_down, final_norm_w, loss_target, m_c_ctx, m_w_mod, m_b_mod, m_norm1_w, m_w_in, m_s5_lambda_re_f, m_s5_lambda_im_f, m_s5_log_step_f, m_s5_lambda_re_b, m_s5_lambda_im_b, m_s5_log_step_b, m_s5_b_re, m_s5_b_im, m_s5_c_re, m_s5_c_im, m_s5_d, m_s5_w_glu, m_s5_b_glu, m_ret_log_decay_f, m_ret_log_decay_b, m_w_out, m_norm2_w, m_w_up, m_conv_w, m_conv_b, m_w_down, m_final_norm_w, v_c_ctx, v_w_mod, v_b_mod, v_norm1_w, v_w_in, v_s5_lambda_re_f, v_s5_lambda_im_f, v_s5_log_step_f, v_s5_lambda_re_b, v_s5_lambda_im_b, v_s5_log_step_b, v_s5_b_re, v_s5_b_im, v_s5_c_re, v_s5_c_im, v_s5_d, v_s5_w_glu, v_s5_b_glu, v_ret_log_decay_f, v_ret_log_decay_b, v_w_out, v_norm2_w, v_w_up, v_conv_w, v_conv_b, v_w_down, v_final_norm_w):
    return _step(dict(locals()))
```

```python
import functools
import math

import jax
import jax.numpy as jnp
from jax import lax
from jax.experimental import pallas as pl
from jax.experimental.pallas import tpu as pltpu

F32 = jnp.float32
BF16 = jnp.bfloat16

D_MODEL = 1024
S5_WIDTH = 512
S5_GROUPS = 32
S5_GROUP = 16
S5_STATE = 64
RET_WIDTH = 512
RET_HEADS = 4
RET_DH = 128
RET_CHUNK = 256
GRID_W = 64
ROPE_THETA = 10000.0
D_FF = 2816
NORM_EPS = 1e-6
IN_COLS = S5_WIDTH + 4 * RET_WIDTH

S5_T = 16
S5_NB = 4
S5_BW = S5_T * 128
S5_SW = 8 * 2 * S5_STATE

ADAM_LR, ADAM_B1, ADAM_B2, ADAM_EPS, ADAM_WD, ADAM_STEP = 0.001, 0.9, 0.999, 1e-08, 0.01, 10

VMEM_LIMIT = 56 * 1024 * 1024
MM_TN_VMEM = 40 * 1024 * 1024
MESH_ID = pl.DeviceIdType.MESH


def _params(sem=None):
    return pltpu.CompilerParams(dimension_semantics=sem, vmem_limit_bytes=VMEM_LIMIT)


def _full(shape):
    n = len(shape)
    return pl.BlockSpec(shape, lambda *_: (0,) * n)


def _dot(a, b):
    return jnp.dot(a, b, preferred_element_type=F32)


def _dot_nt(a, b):
    return lax.dot_general(a, b, (((1,), (1,)), ((), ())), preferred_element_type=F32)


def _dot_tn(a, b):
    return lax.dot_general(a, b, (((0,), (0,)), ((), ())), preferred_element_type=F32)


def _dot_hi(a, b):
    return jnp.dot(a, b, preferred_element_type=F32, precision=lax.Precision.HIGHEST)


def _dot_nt_hi(a, b):
    return lax.dot_general(a, b, (((1,), (1,)), ((), ())), preferred_element_type=F32,
                           precision=lax.Precision.HIGHEST)


def _gelu(x):
    return 0.5 * x * (1.0 + jnp.tanh(0.7978845608028654 * (x + 0.044715 * (x * x * x))))


def _sigmoid(x):
    return 1.0 / (1.0 + jnp.exp(-x))


def _silu(x):
    return x * _sigmoid(x)


def _rms_mod(x, nw, sh, sc):
    r = lax.rsqrt(jnp.mean(x * x, axis=-1, keepdims=True) + NORM_EPS)
    return (x * r * nw) * (1.0 + sc) + sh


def _rms(x, nw):
    r = lax.rsqrt(jnp.mean(x * x, axis=-1, keepdims=True) + NORM_EPS)
    return x * r * nw


def _head_norm_gate(y, g):
    mu = jnp.mean(y, axis=-1, keepdims=True)
    yc = y - mu
    var = jnp.mean(yc * yc, axis=-1, keepdims=True)
    return _silu(g) * (yc * lax.rsqrt(var + NORM_EPS))


def _swap_pairs(t):
    lane = lax.broadcasted_iota(jnp.int32, t.shape, 1)
    return jnp.where(lane % 2 == 0, pltpu.roll(t, RET_DH - 1, 1), pltpu.roll(t, 1, 1))


def _rope(t, cos_t, sin_t):
    return t * cos_t + _swap_pairs(t) * sin_t


def _rope_t(dt, cos_t, sin_t):
    return dt * cos_t + _swap_pairs(dt * sin_t)


def _pick(n, prefs):
    for p in prefs:
        if n % p == 0:
            return p
    return n


def _mm_tn(a, b, *, name):
    m, k = a.shape
    n = b.shape[1]
    tn = _pick(n, (1408, 1024, 1280, 512))
    fits = lambda t: 2 * (2 * t * k + 2 * t * tn + 4 * k * tn) <= MM_TN_VMEM
    tm = _pick(m, [t for t in (2816, 2048, 1024, 768, 512, 256) if fits(t)] + [128])

    def body(a_ref, b_ref, o_ref):
        @pl.when(pl.program_id(1) == 0)
        def _():
            o_ref[...] = jnp.zeros_like(o_ref)
        o_ref[...] += _dot_tn(a_ref[...], b_ref[...])

    return pl.pallas_call(
        body, name=name, grid=(n // tn, m // tm),
        in_specs=[pl.BlockSpec((tm, k), lambda j, i: (i, 0)), pl.BlockSpec((tm, tn), lambda j, i: (i, j))],
        out_specs=pl.BlockSpec((k, tn), lambda j, i: (0, j)),
        out_shape=jax.ShapeDtypeStruct((k, n), F32),
        compiler_params=_params(("parallel", "arbitrary")),
    )(a, b)


TOK_TILE = 256


def _behind(step, last, copies):
    @pl.when(step == 0)
    def _():
        for cp in copies(with_arrivals=False)[0]:
            cp.start()

    @pl.when(step == last)
    def _():
        sends, arrivals = copies()
        for cp in arrivals:
            cp.wait_recv()
        for cp in sends:
            cp.wait_send()


def _norm_inproj(x, ctx, n1w, mod4, w_in_b, placed, kinds):
    l, lc = x.shape[0], ctx.shape[0]
    tm = TOK_TILE
    nct = lc // tm
    la = l + lc
    n = len(placed)
    shard_shapes = _gather_shard_shapes(placed, kinds)

    def body(x_ref, c_ref, nw_ref, mod_ref, w_ref, *rest):
        p_ref, h_ref, u_ref = rest[n:n + 3]
        _behind(pl.program_id(0), la // tm - 1,
                functools.partial(_gather_chip_copies, rest[n + 3:2 * n + 3], kinds, shard_shapes, *rest[2 * n + 3:]))
        is_ctx = pl.program_id(0) < nct
        xt = jnp.where(is_ctx, c_ref[...], x_ref[...])
        sh = jnp.where(is_ctx, mod_ref[0:1, :], mod_ref[2:3, :])
        sc = jnp.where(is_ctx, mod_ref[1:2, :], mod_ref[3:4, :])
        hb = _rms_mod(xt, nw_ref[...], sh, sc).astype(BF16)
        h_ref[...] = hb
        p = _dot(hb, w_ref[...])
        p_ref[...] = p
        u_ref[...] = p[:, 0:S5_WIDTH].astype(BF16)

    return pl.pallas_call(
        body, name="norm_inproj", grid=(la // tm,),
        in_specs=[pl.BlockSpec((tm, D_MODEL), lambda i: (jnp.maximum(i - nct, 0), 0)),
                  pl.BlockSpec((tm, D_MODEL), lambda i: (jnp.minimum(i, nct - 1), 0)),
                  _full((1, D_MODEL)), _full((4, D_MODEL)), _full((D_MODEL, IN_COLS))] + [ANY] * n,
        out_specs=[pl.BlockSpec((tm, IN_COLS), lambda i: (i, 0)), pl.BlockSpec((tm, D_MODEL), lambda i: (i, 0)),
                   pl.BlockSpec((tm, S5_WIDTH), lambda i: (i, 0))] + [ANY] * n,
        out_shape=[jax.ShapeDtypeStruct((la, IN_COLS), F32), jax.ShapeDtypeStruct((la, D_MODEL), BF16),
                   jax.ShapeDtypeStruct((la, S5_WIDTH), BF16)]
        + [jax.ShapeDtypeStruct(p.shape, p.dtype) for p in placed],
        input_output_aliases={5 + a: 3 + a for a in range(n)},
        scratch_shapes=[pltpu.SemaphoreType.DMA((n, 3)), pltpu.SemaphoreType.DMA((n, 3))],
        compiler_params=_params(("arbitrary",)),
    )(x, ctx, n1w, mod4, w_in_b, *placed)


def _iota2(shape, dim):
    return lax.broadcasted_iota(jnp.int32, shape, dim)


def _group_mask(rows, cols, row_div, col_div):
    return jnp.where(_iota2((rows, cols), 0) // row_div == _iota2((rows, cols), 1) // col_div, 1.0, 0.0).astype(F32)


def _s5_gen_dir(lre, lim, lst, b_re, b_im, c_re, c_im):
    step = jnp.exp(lst)
    mag = jnp.exp(lre * step)
    ar = mag * jnp.cos(lim * step)
    ai = mag * jnp.sin(lim * step)
    den = lre * lre + lim * lim
    xr = ar - 1.0
    cr = (xr * lre + ai * lim) / den
    ci = (ai * lre - xr * lim) / den
    rexp = _group_mask(128, 8, S5_GROUP, 1)
    are, aie = _dot_hi(rexp, ar), _dot_hi(rexp, ai)
    cre, cie = _dot_hi(rexp, cr), _dot_hi(rexp, ci)
    bbr = cre * b_re - cie * b_im
    bbi = cre * b_im + cie * b_re
    gmask = _group_mask(128, 128, S5_GROUP, S5_GROUP)
    pr, pi = jnp.ones_like(are), jnp.zeros_like(are)
    xs, ys = [], []
    for t in range(S5_T + 1):
        if t < S5_T:
            xs.append(jnp.concatenate([bbr * pr - bbi * pi, bbr * pi + bbi * pr], axis=1))
        ys.append(jnp.concatenate([c_re * pr - c_im * pi, -(c_re * pi + c_im * pr)], axis=1))
        pr, pi = pr * are - pi * aie, pr * aie + pi * are
    gs = [_dot_nt_hi(x_t, ys[0]) * gmask for x_t in xs]
    r16, i16 = ar, ai
    for _ in range(4):
        r16, i16 = r16 * r16 - i16 * i16, 2.0 * r16 * i16
    return xs, ys, gs, jnp.concatenate([r16, i16], axis=1)


def _s5_expand(z):
    return jnp.concatenate([z] * 8, axis=1) * _group_mask(128, S5_SW, S5_GROUP, 128)


def _s5_contract(z):
    zm = z * _group_mask(128, S5_SW, S5_GROUP, 128)
    acc = zm[:, 0:128]
    for k in range(1, 8):
        acc = acc + zm[:, 128 * k:128 * (k + 1)]
    return acc


def _s5_param_specs():
    blk3 = lambda r, c: pl.BlockSpec((1, 1, r, c), lambda b, *_: (0, b, 0, 0))
    dir3 = lambda r, c: pl.BlockSpec((2, 1, r, c), lambda b, *_: (0, b, 0, 0))
    return [dir3(8, S5_STATE), dir3(8, S5_STATE), dir3(8, 1), blk3(128, S5_STATE), blk3(128, S5_STATE),
            blk3(128, S5_STATE), blk3(128, S5_STATE), blk3(1, 128)]


def _s5_gen(lre, lim, lst, b_re, b_im, c_re, c_im, dvec):
    def body(lre_ref, lim_ref, lst_ref, bre_ref, bim_ref, cre_ref, cim_ref, d_ref, gg_ref, xw_ref, yw_ref, a16_ref):
        eye = _group_mask(128, 128, 1, 1)
        g0 = eye * d_ref[0, 0]
        for dr in range(2):
            xs, ys, gs, a16 = _s5_gen_dir(lre_ref[dr, 0], lim_ref[dr, 0], lst_ref[dr, 0], bre_ref[0, 0],
                                          bim_ref[0, 0], cre_ref[0, 0], cim_ref[0, 0])
            a16_ref[0, dr] = a16
            for j in range(S5_T):
                xw_ref[0, dr, j] = xs[S5_T - 1 - j if dr == 0 else j]
                yw_ref[0, dr, j] = ys[j + 1 if dr == 0 else S5_T - j]
            g0 = g0 + gs[0]
            for t in range(1, S5_T):
                gg_ref[0, (S5_T - 1) + t if dr == 0 else (S5_T - 1) - t] = gs[t]
        gg_ref[0, S5_T - 1] = g0

    blk = pl.BlockSpec((1, 2, S5_T, 128, 128), lambda b: (b, 0, 0, 0, 0))
    return pl.pallas_call(
        body, name="s5_gen", grid=(S5_NB,),
        in_specs=_s5_param_specs(),
        out_specs=[pl.BlockSpec((1, 2 * S5_T - 1, 128, 128), lambda b: (b, 0, 0, 0)), blk, blk,
                   pl.BlockSpec((1, 2, 8, 128), lambda b: (b, 0, 0, 0))],
        out_shape=[jax.ShapeDtypeStruct((S5_NB, 2 * S5_T - 1, 128, 128), F32),
                   jax.ShapeDtypeStruct((S5_NB, 2, S5_T, 128, 128), F32),
                   jax.ShapeDtypeStruct((S5_NB, 2, S5_T, 128, 128), F32),
                   jax.ShapeDtypeStruct((S5_NB, 2, 8, 128), F32)],
        compiler_params=_params(("parallel",)),
    )(lre, lim, lst, b_re, b_im, c_re, c_im, dvec)


def _s5_fill_state_mat(w_scr, src_ref, dr):
    for j in range(S5_T):
        w_scr[128 * j:128 * (j + 1), :] = _s5_expand(src_ref[0, dr, j]).astype(BF16)


def _s5_fill_toeplitz(k_scr, gg_ref):
    for j in range(S5_T):
        for i in range(S5_T):
            k_scr[128 * j:128 * (j + 1), 128 * i:128 * (i + 1)] = gg_ref[0, i - j + (S5_T - 1)].astype(BF16)


S5_GEN_SPECS = [pl.BlockSpec((1, 2 * S5_T - 1, 128, 128), lambda b: (b, 0, 0, 0)),
                pl.BlockSpec((1, 2, S5_T, 128, 128), lambda b: (b, 0, 0, 0, 0))]


def _s5_gen_bwd(lre, lim, lst, b_re, b_im, c_re, c_im, dvec, dg, dx, dy, da16):
    def body(lre_ref, lim_ref, lst_ref, bre_ref, bim_ref, cre_ref, cim_ref, d_ref, dg_ref, dx_ref, dy_ref, da16_ref,
             glre_ref, glim_ref, glst_ref, gbre_ref, gbim_ref, gcre_ref, gcim_ref, gd_ref):
        eye = _group_mask(128, 128, 1, 1)
        gd_ref[0, 0] = jnp.sum(dg_ref[0, S5_T - 1] * eye, axis=0, keepdims=True)
        gb = [None, None, None, None]
        for dr in range(2):
            args = (lre_ref[dr, 0], lim_ref[dr, 0], lst_ref[dr, 0], bre_ref[0, 0], bim_ref[0, 0],
                    cre_ref[0, 0], cim_ref[0, 0])
            _, vjp = jax.vjp(_s5_gen_dir, *args)
            dxs = [dx_ref[0, dr, S5_T - 1 - t if dr == 0 else t] for t in range(S5_T)]
            dys = [jnp.zeros((128, 128), F32)] + [dy_ref[0, dr, t - 1 if dr == 0 else S5_T - t]
                                                  for t in range(1, S5_T + 1)]
            dgs = [dg_ref[0, (S5_T - 1) + t if dr == 0 else (S5_T - 1) - t] for t in range(S5_T)]
            g = vjp((dxs, dys, dgs, da16_ref[0, dr]))
            glre_ref[dr, 0] = g[0]
            glim_ref[dr, 0] = g[1]
            glst_ref[dr, 0] = g[2]
            for q in range(4):
                gb[q] = g[3 + q] if gb[q] is None else gb[q] + g[3 + q]
        gbre_ref[0, 0] = gb[0]
        gbim_ref[0, 0] = gb[1]
        gcre_ref[0, 0] = gb[2]
        gcim_ref[0, 0] = gb[3]

    shp = lambda a: jax.ShapeDtypeStruct(a.shape, F32)
    return pl.pallas_call(
        body, name="s5_gen_bwd", grid=(S5_NB,),
        in_specs=_s5_param_specs() + [
            pl.BlockSpec((1, 2 * S5_T - 1, 128, 128), lambda b: (b, 0, 0, 0)),
            pl.BlockSpec((1, 2, S5_T, 128, 128), lambda b: (b, 0, 0, 0, 0)),
            pl.BlockSpec((1, 2, S5_T, 128, 128), lambda b: (b, 0, 0, 0, 0)),
            pl.BlockSpec((1, 2, 8, 128), lambda b: (b, 0, 0, 0))],
        out_specs=_s5_param_specs(),
        out_shape=[shp(lre), shp(lim), shp(lst), shp(b_re), shp(b_im), shp(c_re), shp(c_im), shp(dvec)],
        compiler_params=_params(("parallel",)),
    )(lre, lim, lst, b_re, b_im, c_re, c_im, dvec, dg, dx, dy, da16)


def _s5_rows(t):
    cn = t.shape[0] // S5_T
    return t.reshape(cn, S5_T, S5_NB, 128).transpose(2, 0, 1, 3).reshape(S5_NB, cn, S5_BW)


def _s5_put_groups(o_ref, dr, val):
    for gi in range(8):
        o_ref[dr, :, gi, :] = val[:, 128 * gi:128 * (gi + 1)]


def _s5_get_groups(s_ref, dr, n=8):
    return jnp.concatenate([s_ref[dr, :, gi, :] for gi in range(n)], axis=1).astype(BF16)


def _s5_to_states(u3, blocks, name):
    cn = u3.shape[1]

    def body(u_ref, b_ref, o_ref, w_scr):
        u = u_ref[0]
        for dr in range(2):
            _s5_fill_state_mat(w_scr, b_ref, dr)
            _s5_put_groups(o_ref, dr, _dot(u, w_scr[...]))

    return pl.pallas_call(
        body, name=name, grid=(S5_NB,),
        in_specs=[pl.BlockSpec((1, cn, S5_BW), lambda b: (b, 0, 0)), S5_GEN_SPECS[1]],
        out_specs=pl.BlockSpec((2, cn, 8, 128), lambda b: (0, 0, b, 0)),
        out_shape=jax.ShapeDtypeStruct((2, cn, S5_GROUPS, 128), F32),
        scratch_shapes=[pltpu.VMEM((S5_BW, S5_SW), BF16)],
        compiler_params=_params(("parallel",)),
    )(u3, blocks)


def _s5_from_states(u3, gg, st, blocks, transposed, name):
    cn = u3.shape[1]

    def body(u_ref, g_ref, s_ref, b_ref, o_ref, k_scr, w_scr):
        u = u_ref[0]
        _s5_fill_toeplitz(k_scr, g_ref)
        y = _dot_nt(u, k_scr[...]) if transposed else _dot(u, k_scr[...])
        for dr in range(2):
            _s5_fill_state_mat(w_scr, b_ref, dr)
            y = y + _dot_nt(_s5_get_groups(s_ref, dr), w_scr[...])
        for i in range(S5_T):
            o_ref[:, i, :] = y[:, 128 * i:128 * (i + 1)]

    return pl.pallas_call(
        body, name=name, grid=(S5_NB,),
        in_specs=[pl.BlockSpec((1, cn, S5_BW), lambda b: (b, 0, 0)), S5_GEN_SPECS[0],
                  pl.BlockSpec((2, cn, 8, 128), lambda b: (0, 0, b, 0)), S5_GEN_SPECS[1]],
        out_specs=pl.BlockSpec((cn, S5_T, 128), lambda b: (0, 0, b)),
        out_shape=jax.ShapeDtypeStruct((cn, S5_T, S5_WIDTH), F32),
        scratch_shapes=[pltpu.VMEM((S5_BW, S5_BW), BF16), pltpu.VMEM((S5_BW, S5_SW), BF16)],
        compiler_params=_params(("parallel",)),
    )(u3, gg, st, blocks)


def _s5_a_forms(a):
    ra = pltpu.roll(a, S5_STATE, 1)
    low = _iota2(a.shape, 1) < S5_STATE
    return jnp.where(low, a, ra), jnp.where(low, -ra, a)


def _s5_scan(sloc, a16, ncc):
    cn = sloc.shape[1]

    def body(s_ref, a_ref, h_ref):
        forms = [_s5_a_forms(a_ref[dr]) for dr in range(2)]

        def step(s, hs):
            out = []
            for dr in range(2):
                arr, aii = forms[dr]
                h, rh = hs[dr]
                c = s if dr == 0 else jnp.where(s < ncc, ncc - 1 - s, cn - 1 - (s - ncc))
                h_ref[dr, c] = h
                sc = s_ref[dr, c]
                out.append((h * arr + rh * aii + sc, rh * arr - h * aii + pltpu.roll(sc, S5_STATE, 1)))
            return tuple(out)

        zero = jnp.zeros((S5_GROUPS, 128), F32)
        lax.fori_loop(0, cn, step, ((zero, zero), (zero, zero)), unroll=4)

    return pl.pallas_call(
        body, name="s5_scan",
        out_shape=jax.ShapeDtypeStruct(sloc.shape, F32),
        compiler_params=_params(),
    )(sloc, a16)


def _s5_scan_bwd(e, hs, a16, ncc):
    cn = e.shape[1]

    def body(e_ref, h_ref, a_ref, ds_ref, da_ref):
        forms = [_s5_a_forms(a_ref[dr]) for dr in range(2)]
        low = _iota2((S5_GROUPS, 128), 1) < S5_STATE

        def step(s, carry):
            out = []
            r = cn - 1 - s
            for dr in range(2):
                arr, aii = forms[dr]
                g, rg, da = carry[dr]
                c = r if dr == 0 else jnp.where(r < ncc, ncc - 1 - r, cn - 1 - (r - ncc))
                ds_ref[dr, c] = g
                h = h_ref[dr, c]
                rh = pltpu.roll(h, S5_STATE, 1)
                da = da + jnp.where(low, g * h + rg * rh, g * rh - rg * h)
                ec = e_ref[dr, c]
                out.append((ec + g * arr - rg * aii, pltpu.roll(ec, S5_STATE, 1) + rg * arr + g * aii, da))
            return tuple(out)

        zero = jnp.zeros((S5_GROUPS, 128), F32)
        res = lax.fori_loop(0, cn, step, ((zero, zero, zero), (zero, zero, zero)), unroll=4)
        da_ref[0] = res[0][2]
        da_ref[1] = res[1][2]

    return pl.pallas_call(
        body, name="s5_scan_bwd",
        out_shape=[jax.ShapeDtypeStruct(e.shape, F32), jax.ShapeDtypeStruct((2, S5_GROUPS, 128), F32)],
        compiler_params=_params(),
    )(e, hs, a16)


def _s5_bwd_kb(p3, dy3):
    cn = p3.shape[1]
    half = S5_T // 2

    def body(u_ref, d_ref, o_ref):
        q = pl.program_id(1)

        @pl.when(q == 0)
        def _():
            o_ref[...] = jnp.zeros_like(o_ref)

        dk = _dot_tn(u_ref[0], d_ref[0])
        for j in range(S5_T):
            for i in range(half):
                o_ref[0, half * q + i - j + (S5_T - 1)] += dk[128 * j:128 * (j + 1), 128 * i:128 * (i + 1)]

    return pl.pallas_call(
        body, name="s5_bwd_kb", grid=(S5_NB, 2),
        in_specs=[pl.BlockSpec((1, cn, S5_BW), lambda b, q: (b, 0, 0)),
                  pl.BlockSpec((1, cn, S5_BW // 2), lambda b, q: (b, 0, q))],
        out_specs=pl.BlockSpec((1, 2 * S5_T - 1, 128, 128), lambda b, q: (b, 0, 0, 0)),
        out_shape=jax.ShapeDtypeStruct((S5_NB, 2 * S5_T - 1, 128, 128), F32),
        compiler_params=_params(("parallel", "arbitrary")),
    )(p3, dy3)


def _s5_bwd_w(u3, st, name):
    cn = u3.shape[1]

    def body(u_ref, s_ref, w_ref):
        dw = _dot_tn(u_ref[0], _s5_get_groups(s_ref, 0))
        for j in range(S5_T):
            w_ref[0, 0, j] = _s5_contract(dw[128 * j:128 * (j + 1), :])

    return pl.pallas_call(
        body, name=name, grid=(S5_NB, 2),
        in_specs=[pl.BlockSpec((1, cn, S5_BW), lambda b, q: (b, 0, 0)),
                  pl.BlockSpec((1, cn, 8, 128), lambda b, q: (q, 0, b, 0))],
        out_specs=pl.BlockSpec((1, 1, S5_T, 128, 128), lambda b, q: (b, q, 0, 0, 0)),
        out_shape=jax.ShapeDtypeStruct((S5_NB, 2, S5_T, 128, 128), F32),
        compiler_params=_params(("parallel", "parallel")),
    )(u3, st)


def _s5_glu(y_all, w_glu_b, b_glu, nct):
    la = y_all.shape[0]
    tm = TOK_TILE
    l = la - nct * tm

    def body(y_ref, w_ref, b_ref, o_ref):
        yg = _gelu(y_ref[...])
        z = _dot(yg.astype(BF16), w_ref[...]) + b_ref[...]
        o_ref[...] = (yg * _sigmoid(z)).astype(BF16)

    return pl.pallas_call(
        body, name="s5_glu", grid=(l // tm,),
        in_specs=[pl.BlockSpec((tm, S5_WIDTH), lambda i: (i + nct, 0)),
                  _full((S5_WIDTH, S5_WIDTH)), _full((1, S5_WIDTH))],
        out_specs=pl.BlockSpec((tm, S5_WIDTH), lambda i: (i, 0)),
        out_shape=jax.ShapeDtypeStruct((l, S5_WIDTH), BF16),
        compiler_params=_params(("parallel",)),
    )(y_all, w_glu_b, b_glu)


def _s5_glu_bwd(y_all, dmix, w_glu_b, b_glu, nct):
    la = y_all.shape[0]
    tm = TOK_TILE

    def body(y_ref, d_ref, w_ref, b_ref, dy_ref, gw_ref, gb_ref):
        i = pl.program_id(0)

        @pl.when(i == 0)
        def _():
            gw_ref[...] = jnp.zeros_like(gw_ref)
            gb_ref[...] = jnp.zeros_like(gb_ref)

        @pl.when(i < nct)
        def _():
            dy_ref[...] = jnp.zeros_like(dy_ref)

        @pl.when(i >= nct)
        def _():
            y = y_ref[...]
            yg, gelu_vjp = jax.vjp(_gelu, y)
            ygb = yg.astype(BF16)
            sg = _sigmoid(_dot(ygb, w_ref[...]) + b_ref[...])
            ds = d_ref[...]
            dz = ds * yg * sg * (1.0 - sg)
            dzb = dz.astype(BF16)
            dyg = ds * sg + _dot_nt(dzb, w_ref[...])
            dy_ref[...] = gelu_vjp(dyg)[0].astype(BF16)
            gw_ref[...] += _dot_tn(ygb, dzb)
            gb_ref[...] += jnp.sum(dz, axis=0, keepdims=True)

    return pl.pallas_call(
        body, name="s5_glu_bwd", grid=(la // tm,),
        in_specs=[pl.BlockSpec((tm, S5_WIDTH), lambda i: (i, 0)),
                  pl.BlockSpec((tm, S5_WIDTH), lambda i: (jnp.maximum(i - nct, 0), 0)),
                  _full((S5_WIDTH, S5_WIDTH)), _full((1, S5_WIDTH))],
        out_specs=[pl.BlockSpec((tm, S5_WIDTH), lambda i: (i, 0)), _full((S5_WIDTH, S5_WIDTH)),
                   _full((1, S5_WIDTH))],
        out_shape=[jax.ShapeDtypeStruct((la, S5_WIDTH), BF16), jax.ShapeDtypeStruct((S5_WIDTH, S5_WIDTH), F32),
                   jax.ShapeDtypeStruct((1, S5_WIDTH), F32)],
        compiler_params=_params(("arbitrary",)),
    )(y_all, dmix, w_glu_b, b_glu)


K_SCALE = RET_DH ** -0.5
Q_COL, K_COL, V_COL, G_COL = 4, 8, 12, 16


def _ret_chunk_of(step, ncc, nch, rev):
    if not rev:
        return step
    return jnp.where(step < ncc, ncc - 1 - step, nch - 1 - (step - ncc))


def _ret_decay(ld, rev):
    c = _iota2((RET_CHUNK, RET_CHUNK), 0).astype(F32)
    m = _iota2((RET_CHUNK, RET_CHUNK), 1).astype(F32)
    diff = (m - c) if rev else (c - m)
    keep = (diff > 0) if rev else (diff >= 0)
    expo = jnp.maximum(diff, 0.0)
    dm = jnp.where(keep, jnp.exp(ld * expo), 0.0)
    xi_e = (RET_CHUNK - c) if rev else (c + 1.0)
    zeta_e = c if rev else (RET_CHUNK - 1.0 - c)
    return dm, expo, jnp.exp(ld * xi_e), xi_e, jnp.exp(ld * zeta_e), zeta_e


RET_TABLES = 7


def _ret_tables(ld2):
    def body(ld_ref, t_ref):
        dr, h = pl.program_id(0), pl.program_id(1)
        ldh = ld_ref[dr, h]
        for rev in (False, True):
            @pl.when(dr == int(rev))
            def _(rev=rev):
                dm, expo, xi, xi_e, zeta, zeta_e = _ret_decay(ldh, rev)
                t_ref[0, 0, 0] = dm
                t_ref[0, 0, 1] = dm * expo
                t_ref[0, 0, 2] = xi
                t_ref[0, 0, 3] = xi * xi_e
                t_ref[0, 0, 4] = zeta
                t_ref[0, 0, 5] = zeta * zeta_e
                t_ref[0, 0, 6] = jnp.zeros_like(dm) + jnp.exp(ldh * RET_CHUNK)

    return pl.pallas_call(
        body, name="ret_tables", grid=(2, RET_HEADS),
        in_specs=[pl.BlockSpec(memory_space=pltpu.SMEM)],
        out_specs=pl.BlockSpec((1, 1, RET_TABLES, RET_CHUNK, RET_CHUNK), lambda d, h: (d, h, 0, 0, 0)),
        out_shape=jax.ShapeDtypeStruct((2, RET_HEADS, RET_TABLES, RET_CHUNK, RET_CHUNK), F32),
        compiler_params=_params(("parallel", "parallel")),
    )(ld2)


def _ret_specs(nch, ncc, rev, step_of):
    chunk = lambda n: _ret_chunk_of(step_of(n), ncc, nch, rev)
    cols = [pl.BlockSpec((RET_CHUNK, RET_WIDTH), functools.partial(lambda n, cb: (chunk(n), cb), cb=cb))
            for cb in (1, 2, 3)]
    tab = pl.BlockSpec((RET_CHUNK, RET_DH), lambda n: (chunk(n), 0))
    return cols + [tab, tab], pl.BlockSpec((RET_CHUNK, RET_WIDTH), lambda n: (chunk(n), 0))


def _ret_scan(p_all, cos_t, sin_t, tabs, ncc, placed, kinds):
    la = p_all.shape[0]
    nch = la // RET_CHUNK
    n = len(placed)
    shard_shapes = _gather_shard_shapes(placed, kinds)

    def body(t_ref, qf, kf, vf, cf, sf, qb, kb, vb, cb, sb, *rest):
        of_ref, ob_ref, ssf_ref, ssb_ref = rest[n:n + 4]
        s_scr, send_sems, recv_sems = rest[2 * n + 4:]
        step = pl.program_id(0)

        @pl.when(step == 0)
        def _():
            s_scr[...] = jnp.zeros_like(s_scr)
            for cp in _gather_chip_copies(rest[n + 4:2 * n + 4], kinds, shard_shapes, send_sems, recv_sems, False)[0]:
                cp.start()

        @pl.when(step == nch - 1)
        def _():
            sends, arrivals = _gather_chip_copies(rest[n + 4:2 * n + 4], kinds, shard_shapes, send_sems, recv_sems)
            for cp in arrivals:
                cp.wait_recv()
            for cp in sends:
                cp.wait_send()

        for dr, (q_ref, k_ref, v_ref, c_ref, n_ref, o_ref, ss_ref) in enumerate(
                ((qf, kf, vf, cf, sf, of_ref, ssf_ref), (qb, kb, vb, cb, sb, ob_ref, ssb_ref))):
            cs, sn = c_ref[...], n_ref[...]
            for h in range(RET_HEADS):
                sl = slice(RET_DH * h, RET_DH * (h + 1))
                dm, xi, zeta = t_ref[dr, h, 0], t_ref[dr, h, 2, :, 0:RET_DH], t_ref[dr, h, 4, :, 0:RET_DH]
                q = _rope(q_ref[:, sl], cs, sn)
                k = _rope(k_ref[:, sl] * K_SCALE, cs, sn)
                vh = v_ref[:, sl].astype(BF16)
                s = s_scr[dr, h]
                ss_ref[0, h] = s
                sc = (_dot_nt(q.astype(BF16), k.astype(BF16)) * dm).astype(BF16)
                o_ref[:, sl] = _dot(sc, vh) + _dot((q * xi).astype(BF16), s.astype(BF16))
                s_scr[dr, h] = t_ref[dr, h, 6, 0:RET_DH, 0:RET_DH] * s + _dot_tn((k * zeta).astype(BF16), vh)

    in_f, out_f = _ret_specs(nch, ncc, False, lambda n: n)
    in_b, out_b = _ret_specs(nch, ncc, True, lambda n: n)
    ss_spec = pl.BlockSpec((1, RET_HEADS, RET_DH, RET_DH), lambda n: (n, 0, 0, 0))
    o_shape = jax.ShapeDtypeStruct((la, RET_WIDTH), F32)
    ss_shape = jax.ShapeDtypeStruct((nch, RET_HEADS, RET_DH, RET_DH), F32)
    return pl.pallas_call(
        body, name="ret_scan", grid=(nch,),
        in_specs=[_full(tabs.shape)] + in_f + in_b + [ANY] * n,
        out_specs=[out_f, out_b, ss_spec, ss_spec] + [ANY] * n,
        out_shape=[o_shape, o_shape, ss_shape, ss_shape] + [jax.ShapeDtypeStruct(p.shape, p.dtype) for p in placed],
        input_output_aliases={11 + a: 4 + a for a in range(n)},
        scratch_shapes=[pltpu.VMEM((2, RET_HEADS, RET_DH, RET_DH), F32),
                        pltpu.SemaphoreType.DMA((n, 3)), pltpu.SemaphoreType.DMA((n, 3))],
        compiler_params=_params(("arbitrary",)),
    )(tabs, p_all, p_all, p_all, cos_t, sin_t, p_all, p_all, p_all, cos_t, sin_t, *placed)


def _ret_scan_bwd(p_all, cos_t, sin_t, tabs, ssf, ssb, dy_all, ncc):
    la = p_all.shape[0]
    nch = la // RET_CHUNK

    def body(t_ref, qf, kf, vf, cf, sf, dof, ssf_ref, qb, kb, vb, cb, sb, dob_, ssb_ref,
             dqf, dkf, dvf, dqb, dkb, dvb, dld_ref, ds_scr):
        @pl.when(pl.program_id(0) == 0)
        def _():
            ds_scr[...] = jnp.zeros_like(ds_scr)
            dld_ref[...] = jnp.zeros_like(dld_ref)

        for dr, (q_ref, k_ref, v_ref, c_ref, n_ref, do_ref, ss_ref, dq_ref, dk_ref, dv_ref) in enumerate(
                ((qf, kf, vf, cf, sf, dof, ssf_ref, dqf, dkf, dvf), (qb, kb, vb, cb, sb, dob_, ssb_ref, dqb, dkb, dvb))):
            cs, sn = c_ref[...], n_ref[...]
            for h in range(RET_HEADS):
                sl = slice(RET_DH * h, RET_DH * (h + 1))
                dm, dm_d = t_ref[dr, h, 0], t_ref[dr, h, 1]
                xi, xi_d, zeta, zeta_d = [t_ref[dr, h, t, :, 0:RET_DH] for t in (2, 3, 4, 5)]
                gc = t_ref[dr, h, 6, 0:RET_DH, 0:RET_DH]
                q = _rope(q_ref[:, sl], cs, sn)
                k = _rope(k_ref[:, sl] * K_SCALE, cs, sn)
                q16, k16, v16 = q.astype(BF16), k.astype(BF16), v_ref[:, sl].astype(BF16)
                s = ss_ref[0, h]
                s16 = s.astype(BF16)
                ds_in = ds_scr[dr, h]
                ds16 = ds_in.astype(BF16)
                do16 = do_ref[:, sl].astype(BF16)
                qk = _dot_nt(q16, k16)
                dsv = _dot_nt(do16, v16)
                dsc = (dsv * dm).astype(BF16)
                sc16 = (qk * dm).astype(BF16)
                dos = _dot_nt(do16, s16)
                vds = _dot_nt(v16, ds16)
                dq_ref[:, sl] = _dot(dsc, k16) + dos * xi
                dk_ref[:, sl] = _dot_tn(dsc, q16) + vds * zeta
                dv_ref[:, sl] = _dot_tn(sc16, do16) + _dot((k * zeta).astype(BF16), ds16)
                ds_scr[dr, h] = _dot_tn((q * xi).astype(BF16), do16) + gc * ds_in
                dld = (jnp.sum(dsv * qk * dm_d) + jnp.sum(q * dos * xi_d + k * vds * zeta_d)
                       + RET_CHUNK * jnp.sum(gc * s * ds_in))
                dld_ref[dr, h] += dld

    back = lambda n: nch - 1 - n
    in_f, out_f = _ret_specs(nch, ncc, False, back)
    in_b, out_b = _ret_specs(nch, ncc, True, back)
    ss_spec = pl.BlockSpec((1, RET_HEADS, RET_DH, RET_DH), lambda n: (nch - 1 - n, 0, 0, 0))
    shp = jax.ShapeDtypeStruct((la, RET_WIDTH), F32)
    return pl.pallas_call(
        body, name="ret_scan_bwd", grid=(nch,),
        in_specs=[_full(tabs.shape)] + in_f + [out_f, ss_spec] + in_b + [out_b, ss_spec],
        out_specs=[out_f, out_f, out_f, out_b, out_b, out_b, _full((2, RET_HEADS, 8, 128))],
        out_shape=[shp] * 6 + [jax.ShapeDtypeStruct((2, RET_HEADS, 8, 128), F32)],
        scratch_shapes=[pltpu.VMEM((2, RET_HEADS, RET_DH, RET_DH), F32)],
        compiler_params=_params(("arbitrary",)),
    )(tabs, p_all, p_all, p_all, cos_t, sin_t, dy_all, ssf, p_all, p_all, p_all, cos_t, sin_t, dy_all, ssb)


def _ret_gate(of, ob, p_all, nct):
    la = of.shape[0]
    tm = TOK_TILE
    l = la - nct * tm

    def body(of_ref, ob_ref, g_ref, r_ref, y_ref):
        y = of_ref[...] + ob_ref[...]
        y_ref[...] = y
        for h in range(RET_HEADS):
            sl = slice(RET_DH * h, RET_DH * (h + 1))
            r_ref[:, sl] = _head_norm_gate(y[:, sl], g_ref[:, sl]).astype(BF16)

    row = pl.BlockSpec((tm, RET_WIDTH), lambda i: (i + nct, 0))
    out = pl.BlockSpec((tm, RET_WIDTH), lambda i: (i, 0))
    return pl.pallas_call(
        body, name="ret_gate", grid=(l // tm,),
        in_specs=[row, row, pl.BlockSpec((tm, RET_WIDTH), lambda i: (i + nct, G_COL // 4))],
        out_specs=[out, out],
        out_shape=[jax.ShapeDtypeStruct((l, RET_WIDTH), BF16), jax.ShapeDtypeStruct((l, RET_WIDTH), F32)],
        compiler_params=_params(("parallel",)),
    )(of, ob, p_all)


def _ret_gate_bwd(y_ret, p_all, dmix, nct):
    la = p_all.shape[0]
    tm = TOK_TILE

    def body(y_ref, g_ref, d_ref, dy_ref, dg_ref):
        i = pl.program_id(0)

        @pl.when(i < nct)
        def _():
            dy_ref[...] = jnp.zeros_like(dy_ref)
            dg_ref[...] = jnp.zeros_like(dg_ref)

        @pl.when(i >= nct)
        def _():
            for h in range(RET_HEADS):
                sl = slice(RET_DH * h, RET_DH * (h + 1))
                _, vjp = jax.vjp(_head_norm_gate, y_ref[:, sl], g_ref[:, sl])
                dy, dg = vjp(d_ref[:, sl])
                dy_ref[:, sl] = dy
                dg_ref[:, sl] = dg

    xrow = lambda cb: pl.BlockSpec((tm, RET_WIDTH), lambda i: (jnp.maximum(i - nct, 0), cb))
    out = pl.BlockSpec((tm, RET_WIDTH), lambda i: (i, 0))
    shp = jax.ShapeDtypeStruct((la, RET_WIDTH), F32)
    return pl.pallas_call(
        body, name="ret_gate_bwd", grid=(la // tm,),
        in_specs=[xrow(0), pl.BlockSpec((tm, RET_WIDTH), lambda i: (i, G_COL // 4)), xrow(1)],
        out_specs=[out, out], out_shape=[shp, shp],
        compiler_params=_params(("parallel",)),
    )(y_ret, p_all, dmix)


def _in_bwd(dqf, dkf, dvf, dqb, dkb, dvb, du, dg, cos_t, sin_t, w_in_b, x, ctx, n1w, mod4, dx1):
    l, lc = x.shape[0], ctx.shape[0]
    la = l + lc
    tm = TOK_TILE
    nct = lc // tm

    def body(dqf_ref, dkf_ref, dvf_ref, dqb_ref, dkb_ref, dvb_ref, du_ref, dg_ref, cos_ref, sin_ref,
             w_ref, x_ref, c_ref, nw_ref, mod_ref, dx1_ref, dp_ref, gx_ref, acc_ref):
        i = pl.program_id(0)
        is_ctx = i < nct

        @pl.when(i == 0)
        def _():
            acc_ref[...] = jnp.zeros_like(acc_ref)

        cs, sn = cos_ref[...], sin_ref[...]
        def piece(k, val):
            cols = slice(S5_WIDTH * k, S5_WIDTH * (k + 1))
            dp_ref[:, cols] = val.astype(BF16)
            return _dot_nt(dp_ref[:, cols], w_ref[:, cols])

        dh1 = piece(0, du_ref[...])
        dh1 = dh1 + piece(3, dvf_ref[...] + dvb_ref[...])
        dh1 = dh1 + piece(4, dg_ref[...])
        for k, (f_ref, b_ref, scale) in ((1, (dqf_ref, dqb_ref, 1.0)), (2, (dkf_ref, dkb_ref, K_SCALE))):
            heads = [_rope_t(f_ref[:, RET_DH * h:RET_DH * (h + 1)] + b_ref[:, RET_DH * h:RET_DH * (h + 1)], cs, sn) * scale
                     for h in range(RET_HEADS)]
            dh1 = dh1 + piece(k, jnp.concatenate(heads, axis=1))
        xt = jnp.where(is_ctx, c_ref[...], x_ref[...])
        sh = jnp.where(is_ctx, mod_ref[0:1, :], mod_ref[2:3, :])
        sc = jnp.where(is_ctx, mod_ref[1:2, :], mod_ref[3:4, :])
        _, vjp = jax.vjp(_rms_mod, xt, nw_ref[...], sh, sc)
        dx, dnw, dsh, dsc = vjp(dh1)
        gx_ref[...] = dx + dx1_ref[...]
        cf = jnp.where(is_ctx, 1.0, 0.0)
        acc_ref[0:1, :] += dnw
        acc_ref[1:2, :] += cf * dsh
        acc_ref[2:3, :] += cf * dsc
        acc_ref[3:4, :] += (1.0 - cf) * dsh
        acc_ref[4:5, :] += (1.0 - cf) * dsc

    row = pl.BlockSpec((tm, RET_WIDTH), lambda i: (i, 0))
    tab = pl.BlockSpec((tm, RET_DH), lambda i: (i, 0))
    xrow = pl.BlockSpec((tm, D_MODEL), lambda i: (jnp.maximum(i - nct, 0), 0))
    return pl.pallas_call(
        body, name="in_bwd", grid=(la // tm,),
        in_specs=[row] * 8 + [tab, tab, _full((D_MODEL, IN_COLS)), xrow,
                              pl.BlockSpec((tm, D_MODEL), lambda i: (jnp.minimum(i, nct - 1), 0)),
                              _full((1, D_MODEL)), _full((4, D_MODEL)), xrow],
        out_specs=[pl.BlockSpec((tm, IN_COLS), lambda i: (i, 0)), xrow, _full((8, D_MODEL))],
        out_shape=[jax.ShapeDtypeStruct((la, IN_COLS), BF16), jax.ShapeDtypeStruct((l, D_MODEL), F32),
                   jax.ShapeDtypeStruct((8, D_MODEL), F32)],
        compiler_params=_params(("arbitrary",)),
    )(dqf, dkf, dvf, dqb, dkb, dvb, du, dg, cos_t, sin_t, w_in_b, x, ctx, n1w, mod4, dx1)


def _outproj_up(x, s5x, retx, w_out_b, mod3, n2w, w_up_b):
    l = x.shape[0]
    tm = TOK_TILE

    def body(x_ref, s_ref, r_ref, wo_ref, mod_ref, nw_ref, wu_ref, x1_ref, mix_ref, h2_ref, up_ref):
        mix = _dot(s_ref[...], wo_ref[0:S5_WIDTH, :]) + _dot(r_ref[...], wo_ref[S5_WIDTH:D_MODEL, :])
        mix_ref[...] = mix
        x1 = x_ref[...] + mod_ref[0:1, :] * mix
        x1_ref[...] = x1
        h2 = _rms_mod(x1, nw_ref[...], mod_ref[1:2, :], mod_ref[2:3, :]).astype(BF16)
        h2_ref[...] = h2
        up_ref[...] = _dot(h2, wu_ref[...])

    row = lambda w: pl.BlockSpec((tm, w), lambda i: (i, 0))
    return pl.pallas_call(
        body, name="outproj_up", grid=(l // tm,),
        in_specs=[row(D_MODEL), row(S5_WIDTH), row(RET_WIDTH), _full((D_MODEL, D_MODEL)), _full((3, D_MODEL)),
                  _full((1, D_MODEL)), _full((D_MODEL, 2 * D_FF))],
        out_specs=[row(D_MODEL), row(D_MODEL), row(D_MODEL), row(2 * D_FF)],
        out_shape=[jax.ShapeDtypeStruct((l, D_MODEL), F32), jax.ShapeDtypeStruct((l, D_MODEL), F32),
                   jax.ShapeDtypeStruct((l, D_MODEL), BF16), jax.ShapeDtypeStruct((l, 2 * D_FF), F32)],
        compiler_params=_params(("parallel",)),
    )(x, s5x, retx, w_out_b, mod3, n2w, w_up_b)


HALO = 8


def _conv_taps(g, prev_row, next_row):
    t = g.shape[0]
    r = _iota2(g.shape, 0)
    gprev = jnp.where(r == 0, prev_row, pltpu.roll(g, 1, 0))
    gnext = jnp.where(r == t - 1, next_row, pltpu.roll(g, t - 1, 0))
    return gprev, gnext


def _ffn_loss(up, x1, conv_w, conv_b, w_down_b, gate, fnw, tgt):
    l = x1.shape[0]
    tm = TOK_TILE
    nt = l // tm
    hb = tm // HALO

    cw = 256

    def body(up_a, up_g, hp_ref, hn_ref, x1_ref, cw_ref, cb_ref, wd_ref, gate_ref, fn_ref, tgt_ref,
             act_ref, dx2_ref, ddn_ref, dact_ref, acc_ref):
        i = pl.program_id(0)

        @pl.when(i == 0)
        def _():
            acc_ref[...] = jnp.zeros_like(acc_ref)

        dn = jnp.zeros((tm, D_MODEL), F32)
        for c in range(D_FF // cw):
            cols = slice(cw * c, cw * (c + 1))
            g = up_g[:, cols]
            prev_row = jnp.where(i == 0, 0.0, hp_ref[HALO - 1:HALO, cols])
            next_row = jnp.where(i == nt - 1, 0.0, hn_ref[0:1, cols])
            gprev, gnext = _conv_taps(g, prev_row, next_row)
            gc = cb_ref[:, cols] + gprev * cw_ref[0:1, cols] + g * cw_ref[1:2, cols] + gnext * cw_ref[2:3, cols]
            act = (_gelu(gc) * up_a[:, cols]).astype(BF16)
            act_ref[:, cols] = act
            dn = dn + _dot(act, wd_ref[cols, :])
        x2 = x1_ref[...] + gate_ref[...] * dn
        y, vjp = jax.vjp(_rms, x2, fn_ref[...])
        err = y - tgt_ref[...]
        dx2, dfn = vjp(err * (1.0 / D_MODEL))
        dx2_ref[...] = dx2
        ddn = (dx2 * gate_ref[...]).astype(BF16)
        ddn_ref[...] = ddn
        for c in range(D_FF // cw):
            cols = slice(cw * c, cw * (c + 1))
            dact_ref[:, cols] = _dot_nt(ddn, wd_ref[cols, :])
        acc_ref[0:1, :] += dfn
        acc_ref[1:2, :] += jnp.sum(dx2 * dn, axis=0, keepdims=True)
        acc_ref[2:3, :] += (0.5 / D_MODEL) * jnp.sum(err * err)

    row = lambda w: pl.BlockSpec((tm, w), lambda i: (i, 0))
    last = l // HALO - 1
    return pl.pallas_call(
        body, name="ffn_loss", grid=(nt,),
        in_specs=[pl.BlockSpec((tm, D_FF), lambda i: (i, 0)), pl.BlockSpec((tm, D_FF), lambda i: (i, 1)),
                  pl.BlockSpec((HALO, D_FF), lambda i: (jnp.maximum(i * hb - 1, 0), 1)),
                  pl.BlockSpec((HALO, D_FF), lambda i: (jnp.minimum((i + 1) * hb, last), 1)),
                  row(D_MODEL), _full((3, D_FF)), _full((1, D_FF)), _full((D_FF, D_MODEL)),
                  _full((1, D_MODEL)), _full((1, D_MODEL)), row(D_MODEL)],
        out_specs=[row(D_FF), row(D_MODEL), row(D_MODEL), row(D_FF), _full((8, D_MODEL))],
        out_shape=[jax.ShapeDtypeStruct((l, D_FF), BF16), jax.ShapeDtypeStruct((l, D_MODEL), F32),
                   jax.ShapeDtypeStruct((l, D_MODEL), BF16), jax.ShapeDtypeStruct((l, D_FF), F32),
                   jax.ShapeDtypeStruct((8, D_MODEL), F32)],
        compiler_params=_params(("arbitrary",)),
    )(up, up, up, up, x1, conv_w, conv_b, w_down_b, gate, fnw, tgt)


def _convglu_bwd(up, dact, conv_w, conv_b):
    l = up.shape[0]
    tm = 128
    nt = l // tm
    hb = tm // HALO
    te = tm + 2 * HALO

    def body(a_ref, ap_ref, an_ref, g_ref, gp_ref, gn_ref, d_ref, dp_ref, dn_ref, cw_ref, cb_ref,
             dup_ref, acc_ref):
        i = pl.program_id(0)

        @pl.when(i == 0)
        def _():
            acc_ref[...] = jnp.zeros_like(acc_ref)

        row = _iota2((te, D_FF), 0) + (i * tm - HALO)
        valid = (row >= 0) & (row < l)

        def ext(p, c, n):
            return jnp.where(valid, jnp.concatenate([p[...], c[...], n[...]], axis=0), 0.0)

        ae, ge, de = ext(ap_ref, a_ref, an_ref), ext(gp_ref, g_ref, gn_ref), ext(dp_ref, d_ref, dn_ref)
        gprev = pltpu.roll(ge, 1, 0)
        gnext = pltpu.roll(ge, te - 1, 0)
        w0, w1, w2 = cw_ref[0:1, :], cw_ref[1:2, :], cw_ref[2:3, :]
        gce = cb_ref[...] + gprev * w0 + ge * w1 + gnext * w2
        _, vjp = jax.vjp(lambda a, gc: _gelu(gc) * a, ae, gce)
        dae, dgce = vjp(de)
        dge = dgce * w1 + pltpu.roll(dgce, te - 1, 0) * w0 + pltpu.roll(dgce, 1, 0) * w2
        mid = slice(HALO, HALO + tm)
        dup_ref[:, 0:D_FF] = dae[mid].astype(BF16)
        dup_ref[:, D_FF:2 * D_FF] = dge[mid].astype(BF16)
        dgc = dgce[mid]
        acc_ref[0:1, :] += jnp.sum(dgc * gprev[mid], axis=0, keepdims=True)
        acc_ref[1:2, :] += jnp.sum(dgc * ge[mid], axis=0, keepdims=True)
        acc_ref[2:3, :] += jnp.sum(dgc * gnext[mid], axis=0, keepdims=True)
        acc_ref[3:4, :] += jnp.sum(dgc, axis=0, keepdims=True)

    last = l // HALO - 1

    def trio(cb):
        return [pl.BlockSpec((tm, D_FF), lambda i: (i, cb)),
                pl.BlockSpec((HALO, D_FF), lambda i: (jnp.maximum(i * hb - 1, 0), cb)),
                pl.BlockSpec((HALO, D_FF), lambda i: (jnp.minimum((i + 1) * hb, last), cb))]

    return pl.pallas_call(
        body, name="convglu_bwd", grid=(nt,),
        in_specs=trio(0) + trio(1) + trio(0) + [_full((3, D_FF)), _full((1, D_FF))],
        out_specs=[pl.BlockSpec((tm, 2 * D_FF), lambda i: (i, 0)), _full((8, D_FF))],
        out_shape=[jax.ShapeDtypeStruct((l, 2 * D_FF), BF16), jax.ShapeDtypeStruct((8, D_FF), F32)],
        compiler_params=_params(("arbitrary",)),
    )(up, up, up, up, up, up, dact, dact, dact, conv_w, conv_b)


def _up_bwd(dup, w_up_b, w_out_b, x1, dx2, mix, mod3, n2w, pairs, kinds):
    l = x1.shape[0]
    tm = TOK_TILE
    nt = l // tm
    n = len(pairs)
    shapes = _rs_slot_shapes(pairs, kinds)

    def body(dup_ref, wu_ref, wo_ref, x1_ref, dx2_ref, mix_ref, mod_ref, nw_ref, *rest):
        dx1_ref, dmixb_ref, dmix_ref, acc_ref = rest[n:n + 4]
        exchange = functools.partial(_rs_chip_copies, rest[:n], rest[n + 4:2 * n + 4], kinds, shapes, *rest[2 * n + 4:])
        step = pl.program_id(0)

        @pl.when(step == 0)
        def _():
            acc_ref[...] = jnp.zeros_like(acc_ref)
            for cp in exchange(with_arrivals=False)[0]:
                cp.start()

        @pl.when(step == nt - 1)
        def _():
            sends, arrivals = exchange()
            for cp in arrivals:
                cp.wait_recv()
            for cp in sends:
                cp.wait_send()

        dh2 = _dot_nt(dup_ref[...], wu_ref[...])
        _, vjp = jax.vjp(_rms_mod, x1_ref[...], nw_ref[...], mod_ref[1:2, :], mod_ref[2:3, :])
        dx, dnw, dsh, dsc = vjp(dh2)
        dx1 = dx + dx2_ref[...]
        dx1_ref[...] = dx1
        dmixb = (dx1 * mod_ref[0:1, :]).astype(BF16)
        dmixb_ref[...] = dmixb
        dmix_ref[...] = _dot_nt(dmixb, wo_ref[...])
        acc_ref[0:1, :] += dnw
        acc_ref[1:2, :] += jnp.sum(dx1 * mix_ref[...], axis=0, keepdims=True)
        acc_ref[2:3, :] += dsh
        acc_ref[3:4, :] += dsc

    row = pl.BlockSpec((tm, D_MODEL), lambda i: (i, 0))
    return pl.pallas_call(
        body, name="up_bwd", grid=(nt,),
        in_specs=[pl.BlockSpec((tm, 2 * D_FF), lambda i: (i, 0)), _full((D_MODEL, 2 * D_FF)),
                  _full((D_MODEL, D_MODEL)), row, row, row, _full((3, D_MODEL)), _full((1, D_MODEL))] + [ANY] * n,
        out_specs=[row, row, row, _full((8, D_MODEL))] + [ANY] * n,
        out_shape=[jax.ShapeDtypeStruct((l, D_MODEL), F32), jax.ShapeDtypeStruct((l, D_MODEL), BF16),
                   jax.ShapeDtypeStruct((l, D_MODEL), F32), jax.ShapeDtypeStruct((8, D_MODEL), F32)]
        + [jax.ShapeDtypeStruct((4,) + s, p.dtype) for s, p in zip(shapes, pairs)],
        scratch_shapes=[pltpu.SemaphoreType.DMA((n, 3)), pltpu.SemaphoreType.DMA((n, 3))],
        compiler_params=_params(("arbitrary",)),
    )(dup, w_up_b, w_out_b, x1, dx2, mix, mod3, n2w, *pairs)


MOD_ROWS = 16
MOD_COLS = 6 * D_MODEL // 4


def _mod_fwd(c_all, c_ctx, w_mod_b, b_loc):
    def body(c_ref, cc_ref, w_ref, b_ref, m_ref, s_ref):
        cond = jnp.concatenate([c_ref[...], jnp.broadcast_to(cc_ref[...], (8, D_MODEL))], axis=0)
        s = _silu(cond).astype(BF16)
        s_ref[...] = s
        m_ref[...] = _dot(s, w_ref[...]) + b_ref[...]

    return pl.pallas_call(
        body, name="mod_fwd",
        out_shape=[jax.ShapeDtypeStruct((MOD_ROWS, MOD_COLS), F32), jax.ShapeDtypeStruct((MOD_ROWS, D_MODEL), BF16)],
        compiler_params=_params(),
    )(c_all, c_ctx, w_mod_b, b_loc)


def _mod_bwd_sum(dm_all):
    def body(d_ref, dm_ref, gb_ref):
        rows = [d_ref[k, 0:1, :] for k in range(8)]
        ctx_sum = d_ref[0, 1:2, :]
        for k in range(1, 8):
            ctx_sum = ctx_sum + d_ref[k, 1:2, :]
        gb = ctx_sum
        for k in range(8):
            gb = gb + rows[k]
        gb_ref[...] = gb
        dm_ref[...] = jnp.concatenate(rows + [ctx_sum] + [jnp.zeros((7, 6 * D_MODEL), F32)], axis=0)

    return pl.pallas_call(
        body, name="mod_bwd_sum",
        out_shape=[jax.ShapeDtypeStruct((MOD_ROWS, 6 * D_MODEL), F32), jax.ShapeDtypeStruct((1, 6 * D_MODEL), F32)],
        compiler_params=_params(),
    )(dm_all)


def _mod_bwd_w(dm_loc, s_b, c_ctx, w_mod_b):
    def body(d_ref, s_ref, cc_ref, w_ref, gw_ref, gc_ref):
        db = d_ref[...].astype(BF16)
        gw_ref[...] = _dot_tn(s_ref[...], db)
        ds = _dot_nt(db, w_ref[...])
        _, vjp = jax.vjp(_silu, cc_ref[...])
        gc_ref[...] = jnp.broadcast_to(vjp(ds[8:9, :])[0], (8, D_MODEL))

    return pl.pallas_call(
        body, name="mod_bwd_w",
        out_shape=[jax.ShapeDtypeStruct((D_MODEL, MOD_COLS), F32), jax.ShapeDtypeStruct((8, D_MODEL), F32)],
        compiler_params=_params(),
    )(dm_loc, s_b, c_ctx, w_mod_b)


def _adamw(w, g, m, v, name):
    r, c = w.shape
    tr = _pick(r, (256, 128, 64, 32, 16, 8))
    bc1 = 1.0 - ADAM_B1 ** ADAM_STEP
    bc2 = 1.0 - ADAM_B2 ** ADAM_STEP

    def body(w_ref, g_ref, m_ref, v_ref, d_ref, nm_ref, nv_ref):
        gg = g_ref[...]
        nm = ADAM_B1 * m_ref[...] + (1.0 - ADAM_B1) * gg
        nv = ADAM_B2 * v_ref[...] + (1.0 - ADAM_B2) * (gg * gg)
        nm_ref[...] = nm
        nv_ref[...] = nv
        d_ref[...] = -ADAM_LR * ((nm / bc1) / (jnp.sqrt(nv / bc2) + ADAM_EPS) + ADAM_WD * w_ref[...])

    blk = pl.BlockSpec((tr, c), lambda i: (i, 0))
    shp = jax.ShapeDtypeStruct((r, c), F32)
    return pl.pallas_call(
        body, name=name, grid=(r // tr,), in_specs=[blk] * 4, out_specs=[blk] * 3, out_shape=[shp] * 3,
        compiler_params=_params(("parallel",)),
    )(w, g, m, v)


def _sum_slots(a, name):
    n, r, c = a.shape
    tr = _pick(r, (376, 256, 208, 128, 64, 32, 16, 8))

    def body(a_ref, o_ref):
        acc = a_ref[0].astype(F32)
        for k in range(1, n):
            acc = acc + a_ref[k].astype(F32)
        o_ref[...] = acc

    return pl.pallas_call(
        body, name=name, grid=(r // tr,),
        in_specs=[pl.BlockSpec((n, tr, c), lambda i: (0, i, 0))],
        out_specs=pl.BlockSpec((tr, c), lambda i: (i, 0)),
        out_shape=jax.ShapeDtypeStruct((r, c), F32),
        compiler_params=_params(("parallel",)),
    )(a)


def _mesh_pos():
    return lax.axis_index("x"), lax.axis_index("y"), lax.axis_index("c")


def _all_gather8(v, name):
    m_per, n = v.shape

    def body(x_ref, out_ref, send_sems, recv_sems, local_sem):
        x, y, c = _mesh_pos()
        me, sibling = (x, y, c), (x, y, 1 - c)
        chips = [(1 - x, y), (x, 1 - y), (1 - x, 1 - y)]

        def rows(px, py, pc):
            return out_ref.at[pl.ds((4 * px + 2 * py + pc) * m_per, m_per), :]

        def copy(k, block, to, src=None):
            return pltpu.make_async_remote_copy(
                src_ref=rows(*block) if src is None else src, dst_ref=rows(*block),
                send_sem=send_sems.at[k], recv_sem=recv_sems.at[k], device_id=to, device_id_type=MESH_ID)

        mine = pltpu.make_async_copy(x_ref, rows(*me), local_sem)
        mine.start()
        first = [copy(0, me, sibling, src=x_ref)]
        first += [copy(1 + j, me, (*chip, c), src=x_ref) for j, chip in enumerate(chips)]
        for cp in first:
            cp.start()
        passed = [copy(4 + j, (*chip, c), sibling) for j, chip in enumerate(chips)]
        for j, chip in enumerate(chips):
            copy(1 + j, (*chip, c), me).wait_recv()
            passed[j].start()
        copy(0, sibling, me).wait_recv()
        for j, chip in enumerate(chips):
            copy(4 + j, (*chip, 1 - c), me).wait_recv()
        for cp in first + passed:
            cp.wait_send()
        mine.wait()

    return pl.pallas_call(
        body, name=name,
        out_shape=jax.ShapeDtypeStruct((8 * m_per, n), v.dtype),
        in_specs=[pl.BlockSpec(memory_space=pltpu.VMEM)],
        out_specs=pl.BlockSpec(memory_space=pltpu.VMEM),
        scratch_shapes=[pltpu.SemaphoreType.DMA((7,)), pltpu.SemaphoreType.DMA((7,)), pltpu.SemaphoreType.DMA],
        compiler_params=_params(),
    )(v)


ANY = pl.BlockSpec(memory_space=pl.ANY)
PEER_CHIPS = lambda x, y: [(x, 1 - y), (1 - x, y), (1 - x, 1 - y)]


def _shard_region(ref, kind, k, rl, cl, r0, nr, c0, nc):
    if kind == "col":
        return ref.at[pl.ds(r0, nr), pl.ds(k * cl + c0, nc)]
    return ref.at[pl.ds(k * rl + r0, nr), pl.ds(c0, nc)]


def _place_shard(w, kind, chip, name):
    rl, cl = w.shape
    tr = _pick(rl, (256, 128, 64))
    nt = rl // tr

    def body(chip_ref, w_ref, o_ref):
        o_ref[...] = w_ref[...].astype(BF16)

    o_map = (lambda i, chip_ref: (i, chip_ref[0])) if kind == "col" else (lambda i, chip_ref: (chip_ref[0] * nt + i, 0))
    return pl.pallas_call(
        body, name=name,
        grid_spec=pltpu.PrefetchScalarGridSpec(
            num_scalar_prefetch=1, grid=(nt,),
            in_specs=[pl.BlockSpec((tr, cl), lambda i, chip_ref: (i, 0))], out_specs=pl.BlockSpec((tr, cl), o_map)),
        out_shape=jax.ShapeDtypeStruct((rl, 4 * cl) if kind == "col" else (4 * rl, cl), BF16),
        compiler_params=_params(("parallel",)),
    )(chip.reshape(1), w)


def _gather_shard_shapes(placed, kinds):
    return [(p.shape[0], p.shape[1] // 4) if k == "col" else (p.shape[0] // 4, p.shape[1]) for p, k in zip(placed, kinds)]


def _gather_chip_copies(outs, kinds, shard_shapes, send_sems, recv_sems, with_arrivals=True):
    x, y, c = _mesh_pos()
    me = 2 * x + y
    sends, arrivals = [], []
    for a in range(len(outs)):
        rl, cl = shard_shapes[a]
        rh = rl // 2
        reg = functools.partial(_shard_region, outs[a], kinds[a], rl=rl, cl=cl, r0=c * rh, nr=rh, c0=0, nc=cl)
        for j, (px, py) in enumerate(PEER_CHIPS(x, y)):
            to = dict(send_sem=send_sems.at[a, j], recv_sem=recv_sems.at[a, j], device_id=(px, py, c),
                      device_id_type=MESH_ID)
            sends.append(pltpu.make_async_remote_copy(src_ref=reg(k=me), dst_ref=reg(k=me), **to))
            if with_arrivals:
                got = reg(k=2 * px + py)
                arrivals.append(pltpu.make_async_remote_copy(src_ref=got, dst_ref=got, **to))
    return sends, arrivals


def _gather_sibling_copies(outs, kinds, shard_shapes, send_sems, recv_sems):
    x, y, c = _mesh_pos()
    forwards, arrivals = [], []
    for a in range(len(outs)):
        rl, cl = shard_shapes[a]
        rh = rl // 2
        for j, (px, py) in enumerate(PEER_CHIPS(x, y)):
            to = dict(send_sem=send_sems.at[a, j], recv_sem=recv_sems.at[a, j], device_id=(x, y, 1 - c),
                      device_id_type=MESH_ID)
            reg = functools.partial(_shard_region, outs[a], kinds[a], k=2 * px + py, rl=rl, cl=cl, nr=rh, c0=0, nc=cl)
            forwards.append(pltpu.make_async_remote_copy(src_ref=reg(r0=c * rh), dst_ref=reg(r0=c * rh), **to))
            arrivals.append(pltpu.make_async_remote_copy(src_ref=reg(r0=(1 - c) * rh), dst_ref=reg(r0=(1 - c) * rh), **to))
    return forwards, arrivals


def _gather_weights(placed, kinds):
    n = len(placed)
    shard_shapes = _gather_shard_shapes(placed, kinds)

    def body(*refs):
        outs = refs[n:2 * n]
        ici_send, ici_recv, sib_send, sib_recv = refs[2 * n:]
        sends, arrivals = _gather_chip_copies(outs, kinds, shard_shapes, ici_send, ici_recv)
        for cp in sends:
            cp.start()
        forwards, from_sibling = _gather_sibling_copies(outs, kinds, shard_shapes, sib_send, sib_recv)
        for cp, fwd in zip(arrivals, forwards):
            cp.wait_recv()
            fwd.start()
        for cp in from_sibling:
            cp.wait_recv()
        for cp in sends + forwards:
            cp.wait_send()

    return pl.pallas_call(
        body, name="gather_weights",
        out_shape=[jax.ShapeDtypeStruct(p.shape, p.dtype) for p in placed],
        in_specs=[ANY] * n, out_specs=[ANY] * n, input_output_aliases={a: a for a in range(n)},
        scratch_shapes=[pltpu.SemaphoreType.DMA((n, 3))] * 4,
        compiler_params=_params(),
    )(*placed)


def _gather_sibling(placed, kinds):
    n = len(placed)
    shard_shapes = _gather_shard_shapes(placed, kinds)

    def body(*refs):
        forwards, from_sibling = _gather_sibling_copies(refs[n:2 * n], kinds, shard_shapes, *refs[2 * n:])
        for cp in forwards:
            cp.start()
        for cp in from_sibling:
            cp.wait_recv()
        for cp in forwards:
            cp.wait_send()

    return pl.pallas_call(
        body, name="gather_sibling",
        out_shape=[jax.ShapeDtypeStruct(p.shape, p.dtype) for p in placed],
        in_specs=[ANY] * n, out_specs=[ANY] * n, input_output_aliases={a: a for a in range(n)},
        scratch_shapes=[pltpu.SemaphoreType.DMA((n, 3))] * 2,
        compiler_params=_params(),
    )(*placed)


def _half(kind, r, c):
    return (r // 2, c) if kind == "col" else (r, c // 2)


def _half_of(ref, kind, which):
    r, c = ref.shape
    hr, hc = _half(kind, r, c)
    return ref.at[pl.ds(which * hr, hr), :] if kind == "col" else ref.at[:, pl.ds(which * hc, hc)]


def _rs_sibling(grads, kinds, name):
    n = len(grads)

    def body(*refs):
        srcs, dsts = refs[:n], refs[n:2 * n]
        send_sems, recv_sems = refs[2 * n:]
        x, y, c = _mesh_pos()
        cps = [pltpu.make_async_remote_copy(src_ref=_half_of(srcs[a], kinds[a], 1 - c), dst_ref=dsts[a],
                                            send_sem=send_sems.at[a], recv_sem=recv_sems.at[a],
                                            device_id=(x, y, 1 - c), device_id_type=MESH_ID) for a in range(n)]
        for cp in cps:
            cp.start()
        for cp in cps:
            cp.wait()

    return pl.pallas_call(
        body, name=name,
        out_shape=[jax.ShapeDtypeStruct(_half(k, *g.shape), g.dtype) for g, k in zip(grads, kinds)],
        in_specs=[ANY] * n, out_specs=[ANY] * n,
        scratch_shapes=[pltpu.SemaphoreType.DMA((n,)), pltpu.SemaphoreType.DMA((n,))],
        compiler_params=_params(),
    )(*grads)


def _pair_sum(gf, rv, kind, ci, name):
    r, c = rv.shape
    tr = _pick(r, (128, 64, 32, 16, 8))
    nt = r // tr

    def body(ci_ref, g_ref, r_ref, o_ref):
        o_ref[...] = (g_ref[...] + r_ref[...]).astype(BF16)

    g_map = (lambda i, ci_ref: (ci_ref[0] * nt + i, 0)) if kind == "col" else (lambda i, ci_ref: (i, ci_ref[0]))
    blk = pl.BlockSpec((tr, c), lambda i, ci_ref: (i, 0))
    return pl.pallas_call(
        body, name=name,
        grid_spec=pltpu.PrefetchScalarGridSpec(num_scalar_prefetch=1, grid=(nt,),
                                               in_specs=[pl.BlockSpec((tr, c), g_map), blk], out_specs=blk),
        out_shape=jax.ShapeDtypeStruct((r, c), BF16),
        compiler_params=_params(("parallel",)),
    )(ci.reshape(1), gf, rv)


def _rs_slot_shapes(pairs, kinds):
    return [(p.shape[0], p.shape[1] // 4) if k == "col" else (p.shape[0] // 4, p.shape[1]) for p, k in zip(pairs, kinds)]


def _rs_chip_copies(srcs, dsts, kinds, shapes, send_sems, recv_sems, with_arrivals=True):
    x, y, c = _mesh_pos()
    me = 2 * x + y
    sends, arrivals = [], []
    for a in range(len(srcs)):
        rl, cl = shapes[a]
        reg = functools.partial(_shard_region, srcs[a], kinds[a], rl=rl, cl=cl, r0=0, nr=rl, c0=0, nc=cl)
        for j, (px, py) in enumerate(PEER_CHIPS(x, y)):
            to = dict(send_sem=send_sems.at[a, j], recv_sem=recv_sems.at[a, j], device_id=(px, py, c),
                      device_id_type=MESH_ID)
            sends.append(pltpu.make_async_remote_copy(src_ref=reg(k=2 * px + py), dst_ref=dsts[a].at[me], **to))
            if with_arrivals:
                slot = dsts[a].at[2 * px + py]
                arrivals.append(pltpu.make_async_remote_copy(src_ref=slot, dst_ref=slot, **to))
    return sends, arrivals


def _rs_chips(pairs, kinds):
    n = len(pairs)
    shapes = _rs_slot_shapes(pairs, kinds)

    def body(*refs):
        sends, arrivals = _rs_chip_copies(refs[:n], refs[n:2 * n], kinds, shapes, *refs[2 * n:])
        for cp in sends:
            cp.start()
        for cp in arrivals:
            cp.wait_recv()
        for cp in sends:
            cp.wait_send()

    return pl.pallas_call(
        body, name="rs_chips",
        out_shape=[jax.ShapeDtypeStruct((4,) + s, p.dtype) for s, p in zip(shapes, pairs)],
        in_specs=[ANY] * n, out_specs=[ANY] * n,
        scratch_shapes=[pltpu.SemaphoreType.DMA((n, 3)), pltpu.SemaphoreType.DMA((n, 3))],
        compiler_params=_params(),
    )(*pairs)


def _sum_chips(pair, got, kind, pos, name):
    _, r, c = got.shape
    tr = _pick(r, (256, 128, 64, 32, 16))
    nt = r // tr

    def body(pos_ref, own_ref, g1_ref, g2_ref, g3_ref, o_ref):
        o_ref[...] = ((own_ref[...].astype(F32) + g1_ref[0].astype(F32)) + g2_ref[0].astype(F32)) + g3_ref[0].astype(F32)

    if kind == "col":
        own_map = lambda i, p: (i, p[1])
        out_map = lambda i, p: (p[0] * nt + i, 0)
        out_shape = (2 * r, c)
    else:
        own_map = lambda i, p: (p[1] * nt + i, 0)
        out_map = lambda i, p: (i, p[0])
        out_shape = (r, 2 * c)
    peer = lambda m: pl.BlockSpec((1, tr, c), lambda i, p: (p[1] ^ m, i, 0))
    return pl.pallas_call(
        body, name=name,
        grid_spec=pltpu.PrefetchScalarGridSpec(
            num_scalar_prefetch=1, grid=(nt,),
            in_specs=[pl.BlockSpec((tr, c), own_map), peer(1), peer(2), peer(3)],
            out_specs=pl.BlockSpec((tr, c), out_map)),
        out_shape=jax.ShapeDtypeStruct(out_shape, F32),
        compiler_params=_params(("parallel",)),
    )(pos, pair, got, got, got)


def _rs_back(halves, kinds):
    n = len(halves)

    def body(*refs):
        outs = refs[n:2 * n]
        send_sems, recv_sems = refs[2 * n:]
        x, y, c = _mesh_pos()
        cps = []
        for a in range(n):
            mine = _half_of(outs[a], kinds[a], c)
            cps.append(pltpu.make_async_remote_copy(src_ref=mine, dst_ref=mine, send_sem=send_sems.at[a],
                                                    recv_sem=recv_sems.at[a], device_id=(x, y, 1 - c),
                                                    device_id_type=MESH_ID))
            cps[-1].start()
        for a in range(n):
            other = _half_of(outs[a], kinds[a], 1 - c)
            pltpu.make_async_remote_copy(src_ref=other, dst_ref=other, send_sem=send_sems.at[a],
                                         recv_sem=recv_sems.at[a], device_id=(x, y, 1 - c),
                                         device_id_type=MESH_ID).wait_recv()
        for cp in cps:
            cp.wait_send()

    return pl.pallas_call(
        body, name="rs_back",
        out_shape=[jax.ShapeDtypeStruct(h.shape, h.dtype) for h in halves],
        in_specs=[ANY] * n, out_specs=[ANY] * n, input_output_aliases={a: a for a in range(n)},
        scratch_shapes=[pltpu.SemaphoreType.DMA((n,)), pltpu.SemaphoreType.DMA((n,))],
        compiler_params=_params(),
    )(*halves)


def _rope_tables(l, lc):
    rows = l // GRID_W
    row = jnp.repeat(jnp.arange(rows, dtype=F32), GRID_W)
    col = jnp.tile(jnp.arange(GRID_W, dtype=F32), rows)
    n_freq = RET_DH // 4
    inv_freq = ROPE_THETA ** (-jnp.arange(n_freq, dtype=F32) / n_freq)
    ang = jnp.concatenate([row[:, None] * inv_freq, col[:, None] * inv_freq], axis=-1)
    cos_t = jnp.repeat(jnp.cos(ang), 2, axis=-1)
    sin_t = jnp.repeat(jnp.sin(ang), 2, axis=-1) * jnp.tile(jnp.array([-1.0, 1.0], F32), RET_DH // 2)
    cos_t = jnp.concatenate([jnp.ones((lc, RET_DH), F32), cos_t], axis=0)
    sin_t = jnp.concatenate([jnp.zeros((lc, RET_DH), F32), sin_t], axis=0)
    return cos_t, sin_t


def _s5_pack(a):
    blk = lambda t: t.reshape(1, S5_NB, 128, S5_STATE)
    lre = jnp.stack([a["s5_lambda_re_f"][0], a["s5_lambda_re_b"][0]]).reshape(2, S5_NB, 8, S5_STATE)
    lim = jnp.stack([a["s5_lambda_im_f"][0], a["s5_lambda_im_b"][0]]).reshape(2, S5_NB, 8, S5_STATE)
    lst = jnp.stack([a["s5_log_step_f"][0], a["s5_log_step_b"][0]]).reshape(2, S5_NB, 8, 1)
    b_re = blk(a["s5_b_re"][0].transpose(0, 2, 1))
    b_im = blk(a["s5_b_im"][0].transpose(0, 2, 1))
    return (lre, lim, lst, b_re, b_im, blk(a["s5_c_re"][0]), blk(a["s5_c_im"][0]),
            a["s5_d"].reshape(1, S5_NB, 1, 128))


def _s5_unpack(g):
    glre, glim, glst, gbre, gbim, gcre, gcim, gd = g
    unb = lambda t: t.reshape(S5_GROUPS, S5_GROUP, S5_STATE).transpose(0, 2, 1)[None]
    return {
        "s5_lambda_re_f": glre[0].reshape(1, S5_GROUPS, S5_STATE), "s5_lambda_re_b": glre[1].reshape(1, S5_GROUPS, S5_STATE),
        "s5_lambda_im_f": glim[0].reshape(1, S5_GROUPS, S5_STATE), "s5_lambda_im_b": glim[1].reshape(1, S5_GROUPS, S5_STATE),
        "s5_log_step_f": glst[0].reshape(1, S5_GROUPS), "s5_log_step_b": glst[1].reshape(1, S5_GROUPS),
        "s5_b_re": unb(gbre), "s5_b_im": unb(gbim),
        "s5_c_re": gcre.reshape(1, S5_GROUPS, S5_GROUP, S5_STATE), "s5_c_im": gcim.reshape(1, S5_GROUPS, S5_GROUP, S5_STATE),
        "s5_d": gd.reshape(1, S5_WIDTH),
    }


def _local_step(a, wb, late, mx, mc, conv_w, ci):
    x, ctx, tgt = a["x"][0], a["ctx"][0], a["loss_target"][0]
    l, lc = x.shape[0], ctx.shape[0]
    la = l + lc
    nct, ncc, nrc, cn = lc // TOK_TILE, lc // S5_T, lc // RET_CHUNK, la // S5_T
    n1w, n2w, fnw = a["norm1_w"], a["norm2_w"], a["final_norm_w"].reshape(1, D_MODEL)
    conv_b, b_glu = a["conv_b"], a["s5_b_glu"]
    ld2 = jnp.concatenate([a["ret_log_decay_f"], a["ret_log_decay_b"]], axis=0)
    mod4 = jnp.concatenate([mc[0:2], mx[0:2]], axis=0)
    mod3 = mx[2:5]
    gate5 = mx[5:6]
    cos_t, sin_t = _rope_tables(l, lc)
    s5p = _s5_pack(a)

    p_all, h1b, u_b, w_out_p, w_down_p = _norm_inproj(x, ctx, n1w, mod4, wb["w_in"], [late[0], late[2]],
                                                 (LATE_KINDS[0], LATE_KINDS[2]))
    p3 = _s5_rows(u_b)
    gg, xw, yw, a16 = _s5_gen(*s5p)
    sloc = _s5_to_states(p3, xw, "s5_state")
    a16s = a16.transpose(1, 0, 2, 3).reshape(2, S5_GROUPS, 128)
    hs = _s5_scan(sloc, a16s, ncc)
    y_all = _s5_from_states(p3, gg, hs, yw, False, "s5_out").reshape(la, S5_WIDTH)
    s5x = _s5_glu(y_all, wb["s5_w_glu"], b_glu, nct)
    tabs = _ret_tables(ld2)
    of, ob, ssf, ssb, w_up_p = _ret_scan(p_all, cos_t, sin_t, tabs, nrc, [late[1]], (LATE_KINDS[1],))
    wb = {**wb, **dict(zip(LATE_NAMES, _gather_sibling([w_out_p, w_up_p, w_down_p], LATE_KINDS)))}
    retx, y_ret = _ret_gate(of, ob, p_all, nct)
    x1, mix, h2b, up = _outproj_up(x, s5x, retx, wb["w_out"], mod3, n2w, wb["w_up"])
    act, dx2, ddn, dact, acc_f = _ffn_loss(up, x1, conv_w, conv_b, wb["w_down"], gate5, fnw, tgt)

    g = {}
    g["w_down"] = _mm_tn(act, ddn, name="gw_down")
    dup, acc_c = _convglu_bwd(up, dact, conv_w, conv_b)
    g["w_up"] = _mm_tn(h2b, dup, name="gw_up")
    first = [g[n] for n in FIRST_GRADS]
    first_pairs = [_pair_sum(gf, rv, k, ci, "rs_pair_" + n)
                   for gf, rv, k, n in zip(first, _rs_sibling(first, FIRST_KINDS, "rs_sibling_first"), FIRST_KINDS, FIRST_GRADS)]
    dx1, dmixb, dmix, acc_2, *first_got = _up_bwd(dup, wb["w_up"], wb["w_out"], x1, dx2, mix, mod3, n2w,
                                                  first_pairs, FIRST_KINDS)
    g["w_out"] = jnp.concatenate([_mm_tn(s5x, dmixb, name="gw_out_s5"), _mm_tn(retx, dmixb, name="gw_out_ret")], axis=0)

    dy_s5, g["s5_w_glu"], g["s5_b_glu"] = _s5_glu_bwd(y_all, dmix, wb["s5_w_glu"], b_glu, nct)
    dy3 = _s5_rows(dy_s5)
    e = _s5_to_states(dy3, yw, "s5_bwd_h")
    ds, da16 = _s5_scan_bwd(e, hs, a16s, ncc)
    du = _s5_from_states(dy3, gg, ds, xw, True, "s5_bwd_u").reshape(la, S5_WIDTH)
    dkb = _s5_bwd_kb(p3, dy3)
    dwst = _s5_bwd_w(p3, ds, "s5_bwd_wst")
    dwout = _s5_bwd_w(dy3, hs, "s5_bwd_wout")
    da16p = da16.reshape(2, S5_NB, 8, 128).transpose(1, 0, 2, 3)
    g.update(_s5_unpack(_s5_gen_bwd(*s5p, dkb, dwst, dwout, da16p)))

    dy_ret, dg = _ret_gate_bwd(y_ret, p_all, dmix, nct)
    dqf, dkf, dvf, dqb, dkb_, dvb, dld = _ret_scan_bwd(p_all, cos_t, sin_t, tabs, ssf, ssb, dy_ret, nrc)
    g["ret_log_decay_f"] = dld[0, :, 0, 0].reshape(1, RET_HEADS)
    g["ret_log_decay_b"] = dld[1, :, 0, 0].reshape(1, RET_HEADS)
    dp, grad_x, acc_1 = _in_bwd(dqf, dkf, dvf, dqb, dkb_, dvb, du, dg, cos_t, sin_t, wb["w_in"], x, ctx, n1w, mod4, dx1)
    g["w_in"] = _mm_tn(h1b, dp, name="gw_in")

    g["norm1_w"], g["norm2_w"], g["final_norm_w"] = acc_1[0:1], acc_2[0:1], acc_f[0]
    g["conv_w"], g["conv_b"] = acc_c[0:3], acc_c[3:4]
    zero = jnp.zeros((1, D_MODEL), F32)
    dmx = jnp.concatenate([acc_1[3:5], acc_2[1:2], acc_2[2:4], acc_f[1:2]], axis=0)
    dmc = jnp.concatenate([acc_1[1:3], zero, zero, zero, zero], axis=0)
    return acc_f[2, 0], grad_x, g, dmx, dmc, first_pairs, first_got


WEIGHT_NAMES = ("c_ctx", "w_mod", "b_mod", "norm1_w", "w_in", "s5_lambda_re_f", "s5_lambda_im_f", "s5_log_step_f",
                "s5_lambda_re_b", "s5_lambda_im_b", "s5_log_step_b", "s5_b_re", "s5_b_im", "s5_c_re", "s5_c_im",
                "s5_d", "s5_w_glu", "s5_b_glu", "ret_log_decay_f", "ret_log_decay_b", "w_out", "norm2_w", "w_up",
                "conv_w", "conv_b", "w_down", "final_norm_w")
BIG_NAMES = ("w_in", "w_out", "w_up", "w_down", "s5_w_glu")
BIG_KINDS = ("col", "row", "col", "row", "row")
EARLY_NAMES, EARLY_KINDS = ("w_in", "s5_w_glu"), ("col", "row")
LATE_NAMES, LATE_KINDS = ("w_out", "w_up", "w_down"), ("row", "col", "row")
FIRST_GRADS, FIRST_KINDS = ("w_down", "w_up"), ("row", "col")
LAST_GRADS, LAST_KINDS = ("w_in", "w_out", "s5_w_glu"), ("col", "row", "row")
SMALL_NAMES = ("norm1_w", "norm2_w", "final_norm_w", "conv_b", "conv_w", "s5_lambda_re_f", "s5_lambda_im_f",
               "s5_log_step_f", "s5_lambda_re_b", "s5_lambda_im_b", "s5_log_step_b", "s5_b_re", "s5_b_im", "s5_c_re",
               "s5_c_im", "s5_d", "s5_b_glu", "ret_log_decay_f", "ret_log_decay_b")
ROW = 1024
N_CHIPS = 4


def _pack_rows(parts):
    flat = jnp.concatenate([p.reshape(-1) for p in parts])
    n = flat.shape[0]
    rows = -(-n // (8 * ROW)) * 8
    return jnp.pad(flat, (0, rows * ROW - n)).reshape(rows, ROW)


def _unpack_rows(packed, shapes):
    flat = packed.reshape(-1)
    out, off = [], 0
    for s in shapes:
        n = math.prod(s)
        out.append(flat[off:off + n].reshape(s))
        off += n
    return out


def _step(a):
    xi, yi, ci = _mesh_pos()
    chip = 2 * xi + yi
    dev = 2 * chip + ci

    cw_loc = a["conv_w"].reshape(-1)
    small_in = jnp.concatenate([a["c"].reshape(-1), jnp.pad(cw_loc, (0, 24 * 128 - cw_loc.shape[0]))]).reshape(32, 128)
    sg = _all_gather8(small_in, "gather_cond").reshape(8, 32, 128)
    c_all = sg[:, 0:8].reshape(8, D_MODEL)
    conv_w = sg[0::2, 8:32].reshape(N_CHIPS, -1)[:, :cw_loc.shape[0]].reshape(N_CHIPS, 3, -1)
    conv_w = conv_w.transpose(1, 0, 2).reshape(3, D_FF)

    placed = {n: _place_shard(a[n][0], k, chip, "place_" + n) for n, k in zip(BIG_NAMES, BIG_KINDS)}
    wb = dict(zip(EARLY_NAMES, _gather_weights([placed[n] for n in EARLY_NAMES], EARLY_KINDS)))
    late = [placed[n] for n in LATE_NAMES]

    w_mod_b = a["w_mod"][0].astype(BF16)
    c_ctx = a["c_ctx"].reshape(1, D_MODEL)
    b_loc = lax.dynamic_slice_in_dim(a["b_mod"], chip * MOD_COLS, MOD_COLS, 1)
    m_loc, s_b = _mod_fwd(c_all, c_ctx, w_mod_b, b_loc)
    mg = _all_gather8(m_loc, "gather_mod").reshape(8, MOD_ROWS, MOD_COLS)
    m_full = mg[0::2].transpose(1, 0, 2).reshape(MOD_ROWS, 6 * D_MODEL)
    mx = lax.dynamic_slice_in_dim(m_full, dev, 1, 0).reshape(6, D_MODEL)
    mc = m_full[8].reshape(6, D_MODEL)

    loss_part, grad_x, g, dmx, dmc, first_pairs, first_got = _local_step(a, wb, late, mx, mc, conv_w, ci)
    loss = lax.psum(loss_part, ("x", "y", "c"))

    dm_pair = jnp.concatenate([dmx.reshape(1, -1), dmc.reshape(1, -1), jnp.zeros((6, 6 * D_MODEL), F32)], axis=0)
    dm_all = _all_gather8(dm_pair, "gather_dmod").reshape(8, 8, 6 * D_MODEL)
    dm16, gb_mod = _mod_bwd_sum(dm_all)
    dm_loc = lax.dynamic_slice_in_dim(dm16, chip * MOD_COLS, MOD_COLS, 1)
    gw_mod, gcc = _mod_bwd_w(dm_loc, s_b, c_ctx, w_mod_b)

    small_parts = [g[n] for n in SMALL_NAMES] + [gcc[0]]
    small_shapes = [p.shape for p in small_parts]
    sp = _pack_rows(small_parts)
    tot = _sum_slots(_all_gather8(sp, "gather_small_grads").reshape(8, sp.shape[0], ROW), "sum_small_grads")
    small = dict(zip(SMALL_NAMES + ("c_ctx",), _unpack_rows(tot, small_shapes)))
    grads = {n: small[n].reshape(a[n].shape) for n in SMALL_NAMES if n != "conv_w"}
    grads["c_ctx"] = (0.5 * small["c_ctx"]).reshape(a["c_ctx"].shape)
    grads["conv_w"] = lax.dynamic_slice_in_dim(small["conv_w"], chip * (D_FF // N_CHIPS), D_FF // N_CHIPS, 1)[None]
    grads["b_mod"] = gb_mod
    grads["w_mod"] = gw_mod[None]

    last = [g[n] for n in LAST_GRADS]
    last_pairs = [_pair_sum(gf, rv, k, ci, "rs_pair_" + n)
                  for gf, rv, k, n in zip(last, _rs_sibling(last, LAST_KINDS, "rs_sibling_last"), LAST_KINDS, LAST_GRADS)]
    last_got = _rs_chips(last_pairs, LAST_KINDS)
    pos = jnp.stack([ci, chip])
    order = FIRST_GRADS + LAST_GRADS
    order_kinds = FIRST_KINDS + LAST_KINDS
    halves = [_sum_chips(p, t, k, pos, "rs_sum_" + n)
              for p, t, k, n in zip(first_pairs + last_pairs, list(first_got) + list(last_got), order_kinds, order)]
    for n, t in zip(order, _rs_back(halves, order_kinds)):
        grads[n] = t[None]

    delta, new_m, new_v = {}, {}, {}
    for n in BIG_NAMES + ("w_mod",):
        for dst, t in zip((delta, new_m, new_v), _adamw(a[n][0], grads[n][0], a["m_" + n][0], a["v_" + n][0], "adamw_" + n)):
            dst[n] = t[None]
    rest = [n for n in WEIGHT_NAMES if n not in BIG_NAMES and n != "w_mod"]
    shapes = [a[n].shape for n in rest]
    pr = lambda pre: _pack_rows([a[pre + n] for n in rest])
    for dst, t in zip((delta, new_m, new_v),
                      _adamw(pr(""), _pack_rows([grads[n] for n in rest]), pr("m_"), pr("v_"), "adamw_small")):
        dst.update(zip(rest, _unpack_rows(t, shapes)))

    return (loss, grad_x[None], *[grads[n] for n in WEIGHT_NAMES], *[delta[n] for n in WEIGHT_NAMES],
            *[new_m[n] for n in WEIGHT_NAMES], *[new_v[n] for n in WEIGHT_NAMES])


def kernel(x, c, ctx, c_ctx, w_mod, b_mod, norm1_w, w_in, s5_lambda_re_f, s5_lambda_im_f, s5_log_step_f, s5_lambda_re_b, s5_lambda_im_b, s5_log_step_b, s5_b_re, s5_b_im, s5_c_re, s5_c_im, s5_d, s5_w_glu, s5_b_glu, ret_log_decay_f, ret_log_decay_b, w_out, norm2_w, w_up, conv_w, conv_b, w_down, final_norm_w, loss_target, m_c_ctx, m_w_mod, m_b_mod, m_norm1_w, m_w_in, m_s5_lambda_re_f, m_s5_lambda_im_f, m_s5_log_step_f, m_s5_lambda_re_b, m_s5_lambda_im_b, m_s5_log_step_b, m_s5_b_re, m_s5_b_im, m_s5_c_re, m_s5_c_im, m_s5_d, m_s5_w_glu, m_s5_b_glu, m_ret_log_decay_f, m_ret_log_decay_b, m_w_out, m_norm2_w, m_w_up, m_conv_w, m_conv_b, m_w_down, m_final_norm_w, v_c_ctx, v_w_mod, v_b_mod, v_norm1_w, v_w_in, v_s5_lambda_re_f, v_s5_lambda_im_f, v_s5_log_step_f, v_s5_lambda_re_b, v_s5_lambda_im_b, v_s5_log_step_b, v_s5_b_re, v_s5_b_im, v_s5_c_re, v_s5_c_im, v_s5_d, v_s5_w_glu, v_s5_b_glu, v_ret_log_decay_f, v_ret_log_decay_b, v_w_out, v_norm2_w, v_w_up, v_conv_w, v_conv_b, v_w_down, v_final_norm_w):
    return _step(dict(locals()))
```

```python
import functools
import math

import jax
import jax.numpy as jnp
from jax import lax
from jax.experimental import pallas as pl
from jax.experimental.pallas import tpu as pltpu

F32 = jnp.float32
BF16 = jnp.bfloat16

D_MODEL = 1024
S5_WIDTH = 512
S5_GROUPS = 32
S5_GROUP = 16
S5_STATE = 64
RET_WIDTH = 512
RET_HEADS = 4
RET_DH = 128
RET_CHUNK = 256
GRID_W = 64
ROPE_THETA = 10000.0
D_FF = 2816
NORM_EPS = 1e-6
IN_COLS = S5_WIDTH + 4 * RET_WIDTH

S5_T = 16
S5_NB = 4
S5_BW = S5_T * 128
S5_SW = 8 * 2 * S5_STATE

ADAM_LR, ADAM_B1, ADAM_B2, ADAM_EPS, ADAM_WD, ADAM_STEP = 0.001, 0.9, 0.999, 1e-08, 0.01, 10

VMEM_LIMIT = 56 * 1024 * 1024
MM_TN_VMEM = 40 * 1024 * 1024
MESH_ID = pl.DeviceIdType.MESH


def _params(sem=None):
    return pltpu.CompilerParams(dimension_semantics=sem, vmem_limit_bytes=VMEM_LIMIT)


def _full(shape):
    n = len(shape)
    return pl.BlockSpec(shape, lambda *_: (0,) * n)


def _dot(a, b):
    return jnp.dot(a, b, preferred_element_type=F32)


def _dot_nt(a, b):
    return lax.dot_general(a, b, (((1,), (1,)), ((), ())), preferred_element_type=F32)


def _dot_tn(a, b):
    return lax.dot_general(a, b, (((0,), (0,)), ((), ())), preferred_element_type=F32)


def _dot_hi(a, b):
    return jnp.dot(a, b, preferred_element_type=F32, precision=lax.Precision.HIGHEST)


def _dot_nt_hi(a, b):
    return lax.dot_general(a, b, (((1,), (1,)), ((), ())), preferred_element_type=F32,
                           precision=lax.Precision.HIGHEST)


def _gelu(x):
    return 0.5 * x * (1.0 + jnp.tanh(0.7978845608028654 * (x + 0.044715 * (x * x * x))))


def _gelu_and_grad(x):
    c, ca = 0.7978845608028654, 0.7978845608028654 * 0.044715
    x2 = x * x
    t = jnp.tanh(x * (c + ca * x2))
    h = 0.5 * x
    return h + h * t, 0.5 + 0.5 * t + h * (1.0 - t * t) * (c + 3.0 * ca * x2)


def _sigmoid(x):
    return 1.0 / (1.0 + jnp.exp(-x))


def _silu(x):
    return x * _sigmoid(x)


def _rms_mod(x, nw, sh, sc):
    r = lax.rsqrt(jnp.mean(x * x, axis=-1, keepdims=True) + NORM_EPS)
    return (x * r * nw) * (1.0 + sc) + sh


def _rms(x, nw):
    r = lax.rsqrt(jnp.mean(x * x, axis=-1, keepdims=True) + NORM_EPS)
    return x * r * nw


def _head_norm_gate(y, g):
    mu = jnp.mean(y, axis=-1, keepdims=True)
    yc = y - mu
    var = jnp.mean(yc * yc, axis=-1, keepdims=True)
    return _silu(g) * (yc * lax.rsqrt(var + NORM_EPS))


def _swap_pairs(t):
    lane = lax.broadcasted_iota(jnp.int32, t.shape, 1)
    return jnp.where(lane % 2 == 0, pltpu.roll(t, RET_DH - 1, 1), pltpu.roll(t, 1, 1))


def _rope(t, cos_t, sin_t):
    return t * cos_t + _swap_pairs(t) * sin_t


def _rope_t(dt, cos_t, sin_t):
    return dt * cos_t + _swap_pairs(dt * sin_t)


def _pick(n, prefs):
    for p in prefs:
        if n % p == 0:
            return p
    return n


def _mm_tn(a, b, *, name):
    m, k = a.shape
    n = b.shape[1]
    tn = _pick(n, (1408, 1024, 1280, 512))
    fits = lambda t: 2 * (2 * t * k + 2 * t * tn + 4 * k * tn) <= MM_TN_VMEM
    tm = _pick(m, [t for t in (2816, 2048, 1024, 768, 512, 256) if fits(t)] + [128])

    def body(a_ref, b_ref, o_ref):
        @pl.when(pl.program_id(1) == 0)
        def _():
            o_ref[...] = jnp.zeros_like(o_ref)
        o_ref[...] += _dot_tn(a_ref[...], b_ref[...])

    return pl.pallas_call(
        body, name=name, grid=(n // tn, m // tm),
        in_specs=[pl.BlockSpec((tm, k), lambda j, i: (i, 0)), pl.BlockSpec((tm, tn), lambda j, i: (i, j))],
        out_specs=pl.BlockSpec((k, tn), lambda j, i: (0, j)),
        out_shape=jax.ShapeDtypeStruct((k, n), F32),
        compiler_params=_params(("parallel", "arbitrary")),
    )(a, b)


TOK_TILE = 256


def _behind(step, last, copies):
    @pl.when(step == 0)
    def _():
        for cp in copies(with_arrivals=False)[0]:
            cp.start()

    @pl.when(step == last)
    def _():
        sends, arrivals = copies()
        for cp in arrivals:
            cp.wait_recv()
        for cp in sends:
            cp.wait_send()


def _norm_inproj(x, ctx, n1w, mod4, w_in_b, placed, kinds):
    l, lc = x.shape[0], ctx.shape[0]
    tm = TOK_TILE
    nct = lc // tm
    la = l + lc
    n = len(placed)
    shard_shapes = _gather_shard_shapes(placed, kinds)

    def body(x_ref, c_ref, nw_ref, mod_ref, w_ref, *rest):
        p_ref, h_ref, u_ref = rest[n:n + 3]
        _behind(pl.program_id(0), la // tm - 1,
                functools.partial(_gather_chip_copies, rest[n + 3:2 * n + 3], kinds, shard_shapes, *rest[2 * n + 3:]))
        is_ctx = pl.program_id(0) < nct
        xt = jnp.where(is_ctx, c_ref[...], x_ref[...])
        sh = jnp.where(is_ctx, mod_ref[0:1, :], mod_ref[2:3, :])
        sc = jnp.where(is_ctx, mod_ref[1:2, :], mod_ref[3:4, :])
        hb = _rms_mod(xt, nw_ref[...], sh, sc).astype(BF16)
        h_ref[...] = hb
        p = _dot(hb, w_ref[...])
        p_ref[...] = p
        u_ref[...] = p[:, 0:S5_WIDTH].astype(BF16)

    return pl.pallas_call(
        body, name="norm_inproj", grid=(la // tm,),
        in_specs=[pl.BlockSpec((tm, D_MODEL), lambda i: (jnp.maximum(i - nct, 0), 0)),
                  pl.BlockSpec((tm, D_MODEL), lambda i: (jnp.minimum(i, nct - 1), 0)),
                  _full((1, D_MODEL)), _full((4, D_MODEL)), _full((D_MODEL, IN_COLS))] + [ANY] * n,
        out_specs=[pl.BlockSpec((tm, IN_COLS), lambda i: (i, 0)), pl.BlockSpec((tm, D_MODEL), lambda i: (i, 0)),
                   pl.BlockSpec((tm, S5_WIDTH), lambda i: (i, 0))] + [ANY] * n,
        out_shape=[jax.ShapeDtypeStruct((la, IN_COLS), F32), jax.ShapeDtypeStruct((la, D_MODEL), BF16),
                   jax.ShapeDtypeStruct((la, S5_WIDTH), BF16)]
        + [jax.ShapeDtypeStruct(p.shape, p.dtype) for p in placed],
        input_output_aliases={5 + a: 3 + a for a in range(n)},
        scratch_shapes=[pltpu.SemaphoreType.DMA((n, 3)), pltpu.SemaphoreType.DMA((n, 3))],
        compiler_params=_params(("arbitrary",)),
    )(x, ctx, n1w, mod4, w_in_b, *placed)


def _iota2(shape, dim):
    return lax.broadcasted_iota(jnp.int32, shape, dim)


def _group_mask(rows, cols, row_div, col_div):
    return jnp.where(_iota2((rows, cols), 0) // row_div == _iota2((rows, cols), 1) // col_div, 1.0, 0.0).astype(F32)


def _s5_gen_dir(lre, lim, lst, b_re, b_im, c_re, c_im):
    step = jnp.exp(lst)
    mag = jnp.exp(lre * step)
    ar = mag * jnp.cos(lim * step)
    ai = mag * jnp.sin(lim * step)
    den = lre * lre + lim * lim
    xr = ar - 1.0
    cr = (xr * lre + ai * lim) / den
    ci = (ai * lre - xr * lim) / den
    rexp = _group_mask(128, 8, S5_GROUP, 1)
    are, aie = _dot_hi(rexp, ar), _dot_hi(rexp, ai)
    cre, cie = _dot_hi(rexp, cr), _dot_hi(rexp, ci)
    bbr = cre * b_re - cie * b_im
    bbi = cre * b_im + cie * b_re
    gmask = _group_mask(128, 128, S5_GROUP, S5_GROUP)
    pr, pi = jnp.ones_like(are), jnp.zeros_like(are)
    xs, ys = [], []
    for t in range(S5_T + 1):
        if t < S5_T:
            xs.append(jnp.concatenate([bbr * pr - bbi * pi, bbr * pi + bbi * pr], axis=1))
        ys.append(jnp.concatenate([c_re * pr - c_im * pi, -(c_re * pi + c_im * pr)], axis=1))
        pr, pi = pr * are - pi * aie, pr * aie + pi * are
    gs = [_dot_nt_hi(x_t, ys[0]) * gmask for x_t in xs]
    r16, i16 = ar, ai
    for _ in range(4):
        r16, i16 = r16 * r16 - i16 * i16, 2.0 * r16 * i16
    return xs, ys, gs, jnp.concatenate([r16, i16], axis=1)


def _s5_expand(z):
    return jnp.concatenate([z] * 8, axis=1) * _group_mask(128, S5_SW, S5_GROUP, 128)


def _s5_contract(z):
    zm = z * _group_mask(128, S5_SW, S5_GROUP, 128)
    acc = zm[:, 0:128]
    for k in range(1, 8):
        acc = acc + zm[:, 128 * k:128 * (k + 1)]
    return acc


def _s5_param_specs():
    blk3 = lambda r, c: pl.BlockSpec((1, 1, r, c), lambda b, *_: (0, b, 0, 0))
    dir3 = lambda r, c: pl.BlockSpec((2, 1, r, c), lambda b, *_: (0, b, 0, 0))
    return [dir3(8, S5_STATE), dir3(8, S5_STATE), dir3(8, 1), blk3(128, S5_STATE), blk3(128, S5_STATE),
            blk3(128, S5_STATE), blk3(128, S5_STATE), blk3(1, 128)]


def _s5_gen(lre, lim, lst, b_re, b_im, c_re, c_im, dvec):
    def body(lre_ref, lim_ref, lst_ref, bre_ref, bim_ref, cre_ref, cim_ref, d_ref, gg_ref, xw_ref, yw_ref, a16_ref):
        eye = _group_mask(128, 128, 1, 1)
        g0 = eye * d_ref[0, 0]
        for dr in range(2):
            xs, ys, gs, a16 = _s5_gen_dir(lre_ref[dr, 0], lim_ref[dr, 0], lst_ref[dr, 0], bre_ref[0, 0],
                                          bim_ref[0, 0], cre_ref[0, 0], cim_ref[0, 0])
            a16_ref[0, dr] = a16
            for j in range(S5_T):
                xw_ref[0, dr, j] = xs[S5_T - 1 - j if dr == 0 else j]
                yw_ref[0, dr, j] = ys[j + 1 if dr == 0 else S5_T - j]
            g0 = g0 + gs[0]
            for t in range(1, S5_T):
                gg_ref[0, (S5_T - 1) + t if dr == 0 else (S5_T - 1) - t] = gs[t]
        gg_ref[0, S5_T - 1] = g0

    blk = pl.BlockSpec((1, 2, S5_T, 128, 128), lambda b: (b, 0, 0, 0, 0))
    return pl.pallas_call(
        body, name="s5_gen", grid=(S5_NB,),
        in_specs=_s5_param_specs(),
        out_specs=[pl.BlockSpec((1, 2 * S5_T - 1, 128, 128), lambda b: (b, 0, 0, 0)), blk, blk,
                   pl.BlockSpec((1, 2, 8, 128), lambda b: (b, 0, 0, 0))],
        out_shape=[jax.ShapeDtypeStruct((S5_NB, 2 * S5_T - 1, 128, 128), F32),
                   jax.ShapeDtypeStruct((S5_NB, 2, S5_T, 128, 128), F32),
                   jax.ShapeDtypeStruct((S5_NB, 2, S5_T, 128, 128), F32),
                   jax.ShapeDtypeStruct((S5_NB, 2, 8, 128), F32)],
        compiler_params=_params(("parallel",)),
    )(lre, lim, lst, b_re, b_im, c_re, c_im, dvec)


def _s5_fill_state_mat(w_scr, src_ref, dr):
    for j in range(S5_T):
        w_scr[128 * j:128 * (j + 1), :] = _s5_expand(src_ref[0, dr, j]).astype(BF16)


def _s5_fill_toeplitz(k_scr, gg_ref):
    for j in range(S5_T):
        for i in range(S5_T):
            k_scr[128 * j:128 * (j + 1), 128 * i:128 * (i + 1)] = gg_ref[0, i - j + (S5_T - 1)].astype(BF16)


S5_GEN_SPECS = [pl.BlockSpec((1, 2 * S5_T - 1, 128, 128), lambda b: (b, 0, 0, 0)),
                pl.BlockSpec((1, 2, S5_T, 128, 128), lambda b: (b, 0, 0, 0, 0))]


def _s5_gen_bwd(lre, lim, lst, b_re, b_im, c_re, c_im, dvec, dg, dx, dy, da16):
    def body(lre_ref, lim_ref, lst_ref, bre_ref, bim_ref, cre_ref, cim_ref, d_ref, dg_ref, dx_ref, dy_ref, da16_ref,
             glre_ref, glim_ref, glst_ref, gbre_ref, gbim_ref, gcre_ref, gcim_ref, gd_ref):
        eye = _group_mask(128, 128, 1, 1)
        gd_ref[0, 0] = jnp.sum(dg_ref[0, S5_T - 1] * eye, axis=0, keepdims=True)
        gb = [None, None, None, None]
        for dr in range(2):
            args = (lre_ref[dr, 0], lim_ref[dr, 0], lst_ref[dr, 0], bre_ref[0, 0], bim_ref[0, 0],
                    cre_ref[0, 0], cim_ref[0, 0])
            _, vjp = jax.vjp(_s5_gen_dir, *args)
            dxs = [dx_ref[0, dr, S5_T - 1 - t if dr == 0 else t] for t in range(S5_T)]
            dys = [jnp.zeros((128, 128), F32)] + [dy_ref[0, dr, t - 1 if dr == 0 else S5_T - t]
                                                  for t in range(1, S5_T + 1)]
            dgs = [dg_ref[0, (S5_T - 1) + t if dr == 0 else (S5_T - 1) - t] for t in range(S5_T)]
            g = vjp((dxs, dys, dgs, da16_ref[0, dr]))
            glre_ref[dr, 0] = g[0]
            glim_ref[dr, 0] = g[1]
            glst_ref[dr, 0] = g[2]
            for q in range(4):
                gb[q] = g[3 + q] if gb[q] is None else gb[q] + g[3 + q]
        gbre_ref[0, 0] = gb[0]
        gbim_ref[0, 0] = gb[1]
        gcre_ref[0, 0] = gb[2]
        gcim_ref[0, 0] = gb[3]

    shp = lambda a: jax.ShapeDtypeStruct(a.shape, F32)
    return pl.pallas_call(
        body, name="s5_gen_bwd", grid=(S5_NB,),
        in_specs=_s5_param_specs() + [
            pl.BlockSpec((1, 2 * S5_T - 1, 128, 128), lambda b: (b, 0, 0, 0)),
            pl.BlockSpec((1, 2, S5_T, 128, 128), lambda b: (b, 0, 0, 0, 0)),
            pl.BlockSpec((1, 2, S5_T, 128, 128), lambda b: (b, 0, 0, 0, 0)),
            pl.BlockSpec((1, 2, 8, 128), lambda b: (b, 0, 0, 0))],
        out_specs=_s5_param_specs(),
        out_shape=[shp(lre), shp(lim), shp(lst), shp(b_re), shp(b_im), shp(c_re), shp(c_im), shp(dvec)],
        compiler_params=_params(("parallel",)),
    )(lre, lim, lst, b_re, b_im, c_re, c_im, dvec, dg, dx, dy, da16)


def _s5_rows(t):
    cn = t.shape[0] // S5_T
    return t.reshape(cn, S5_T, S5_NB, 128).transpose(2, 0, 1, 3).reshape(S5_NB, cn, S5_BW)


def _s5_put_groups(o_ref, dr, val):
    for gi in range(8):
        o_ref[dr, :, gi, :] = val[:, 128 * gi:128 * (gi + 1)]


def _s5_get_groups(s_ref, dr, n=8):
    return jnp.concatenate([s_ref[dr, :, gi, :] for gi in range(n)], axis=1).astype(BF16)


def _s5_to_states(u3, blocks, name):
    cn = u3.shape[1]

    def body(u_ref, b_ref, o_ref, w_scr):
        u = u_ref[0]
        for dr in range(2):
            _s5_fill_state_mat(w_scr, b_ref, dr)
            _s5_put_groups(o_ref, dr, _dot(u, w_scr[...]))

    return pl.pallas_call(
        body, name=name, grid=(S5_NB,),
        in_specs=[pl.BlockSpec((1, cn, S5_BW), lambda b: (b, 0, 0)), S5_GEN_SPECS[1]],
        out_specs=pl.BlockSpec((2, cn, 8, 128), lambda b: (0, 0, b, 0)),
        out_shape=jax.ShapeDtypeStruct((2, cn, S5_GROUPS, 128), F32),
        scratch_shapes=[pltpu.VMEM((S5_BW, S5_SW), BF16)],
        compiler_params=_params(("parallel",)),
    )(u3, blocks)


def _s5_from_states(u3, gg, st, blocks, transposed, name):
    cn = u3.shape[1]

    def body(u_ref, g_ref, s_ref, b_ref, o_ref, k_scr, w_scr):
        u = u_ref[0]
        _s5_fill_toeplitz(k_scr, g_ref)
        y = _dot_nt(u, k_scr[...]) if transposed else _dot(u, k_scr[...])
        for dr in range(2):
            _s5_fill_state_mat(w_scr, b_ref, dr)
            y = y + _dot_nt(_s5_get_groups(s_ref, dr), w_scr[...])
        for i in range(S5_T):
            o_ref[:, i, :] = y[:, 128 * i:128 * (i + 1)]

    return pl.pallas_call(
        body, name=name, grid=(S5_NB,),
        in_specs=[pl.BlockSpec((1, cn, S5_BW), lambda b: (b, 0, 0)), S5_GEN_SPECS[0],
                  pl.BlockSpec((2, cn, 8, 128), lambda b: (0, 0, b, 0)), S5_GEN_SPECS[1]],
        out_specs=pl.BlockSpec((cn, S5_T, 128), lambda b: (0, 0, b)),
        out_shape=jax.ShapeDtypeStruct((cn, S5_T, S5_WIDTH), F32),
        scratch_shapes=[pltpu.VMEM((S5_BW, S5_BW), BF16), pltpu.VMEM((S5_BW, S5_SW), BF16)],
        compiler_params=_params(("parallel",)),
    )(u3, gg, st, blocks)


def _s5_a_forms(a):
    ra = pltpu.roll(a, S5_STATE, 1)
    low = _iota2(a.shape, 1) < S5_STATE
    return jnp.where(low, a, ra), jnp.where(low, -ra, a)


def _s5_scan(sloc, a16, ncc):
    cn = sloc.shape[1]

    def body(s_ref, a_ref, h_ref):
        forms = [_s5_a_forms(a_ref[dr]) for dr in range(2)]

        def step(s, hs):
            out = []
            for dr in range(2):
                arr, aii = forms[dr]
                h, rh = hs[dr]
                c = s if dr == 0 else jnp.where(s < ncc, ncc - 1 - s, cn - 1 - (s - ncc))
                h_ref[dr, c] = h
                sc = s_ref[dr, c]
                out.append((h * arr + rh * aii + sc, rh * arr - h * aii + pltpu.roll(sc, S5_STATE, 1)))
            return tuple(out)

        zero = jnp.zeros((S5_GROUPS, 128), F32)
        lax.fori_loop(0, cn, step, ((zero, zero), (zero, zero)), unroll=4)

    return pl.pallas_call(
        body, name="s5_scan",
        out_shape=jax.ShapeDtypeStruct(sloc.shape, F32),
        compiler_params=_params(),
    )(sloc, a16)


def _s5_scan_bwd(e, hs, a16, ncc):
    cn = e.shape[1]

    def body(e_ref, h_ref, a_ref, ds_ref, da_ref):
        forms = [_s5_a_forms(a_ref[dr]) for dr in range(2)]
        low = _iota2((S5_GROUPS, 128), 1) < S5_STATE

        def step(s, carry):
            out = []
            r = cn - 1 - s
            for dr in range(2):
                arr, aii = forms[dr]
                g, rg, da = carry[dr]
                c = r if dr == 0 else jnp.where(r < ncc, ncc - 1 - r, cn - 1 - (r - ncc))
                ds_ref[dr, c] = g
                h = h_ref[dr, c]
                rh = pltpu.roll(h, S5_STATE, 1)
                da = da + jnp.where(low, g * h + rg * rh, g * rh - rg * h)
                ec = e_ref[dr, c]
                out.append((ec + g * arr - rg * aii, pltpu.roll(ec, S5_STATE, 1) + rg * arr + g * aii, da))
            return tuple(out)

        zero = jnp.zeros((S5_GROUPS, 128), F32)
        res = lax.fori_loop(0, cn, step, ((zero, zero, zero), (zero, zero, zero)), unroll=4)
        da_ref[0] = res[0][2]
        da_ref[1] = res[1][2]

    return pl.pallas_call(
        body, name="s5_scan_bwd",
        out_shape=[jax.ShapeDtypeStruct(e.shape, F32), jax.ShapeDtypeStruct((2, S5_GROUPS, 128), F32)],
        compiler_params=_params(),
    )(e, hs, a16)


def _s5_bwd_kb(p3, dy3):
    cn = p3.shape[1]
    half = S5_T // 2

    def body(u_ref, d_ref, o_ref):
        q = pl.program_id(1)

        @pl.when(q == 0)
        def _():
            o_ref[...] = jnp.zeros_like(o_ref)

        dk = _dot_tn(u_ref[0], d_ref[0])
        for j in range(S5_T):
            for i in range(half):
                o_ref[0, half * q + i - j + (S5_T - 1)] += dk[128 * j:128 * (j + 1), 128 * i:128 * (i + 1)]

    return pl.pallas_call(
        body, name="s5_bwd_kb", grid=(S5_NB, 2),
        in_specs=[pl.BlockSpec((1, cn, S5_BW), lambda b, q: (b, 0, 0)),
                  pl.BlockSpec((1, cn, S5_BW // 2), lambda b, q: (b, 0, q))],
        out_specs=pl.BlockSpec((1, 2 * S5_T - 1, 128, 128), lambda b, q: (b, 0, 0, 0)),
        out_shape=jax.ShapeDtypeStruct((S5_NB, 2 * S5_T - 1, 128, 128), F32),
        compiler_params=_params(("parallel", "arbitrary")),
    )(p3, dy3)


def _s5_bwd_w(u3, st, name):
    cn = u3.shape[1]

    def body(u_ref, s_ref, w_ref):
        dw = _dot_tn(u_ref[0], _s5_get_groups(s_ref, 0))
        for j in range(S5_T):
            w_ref[0, 0, j] = _s5_contract(dw[128 * j:128 * (j + 1), :])

    return pl.pallas_call(
        body, name=name, grid=(S5_NB, 2),
        in_specs=[pl.BlockSpec((1, cn, S5_BW), lambda b, q: (b, 0, 0)),
                  pl.BlockSpec((1, cn, 8, 128), lambda b, q: (q, 0, b, 0))],
        out_specs=pl.BlockSpec((1, 1, S5_T, 128, 128), lambda b, q: (b, q, 0, 0, 0)),
        out_shape=jax.ShapeDtypeStruct((S5_NB, 2, S5_T, 128, 128), F32),
        compiler_params=_params(("parallel", "parallel")),
    )(u3, st)


def _s5_glu(y_all, w_glu_b, b_glu, nct):
    la = y_all.shape[0]
    tm = TOK_TILE
    l = la - nct * tm

    def body(y_ref, w_ref, b_ref, o_ref):
        yg = _gelu(y_ref[...])
        z = _dot(yg.astype(BF16), w_ref[...]) + b_ref[...]
        o_ref[...] = (yg * _sigmoid(z)).astype(BF16)

    return pl.pallas_call(
        body, name="s5_glu", grid=(l // tm,),
        in_specs=[pl.BlockSpec((tm, S5_WIDTH), lambda i: (i + nct, 0)),
                  _full((S5_WIDTH, S5_WIDTH)), _full((1, S5_WIDTH))],
        out_specs=pl.BlockSpec((tm, S5_WIDTH), lambda i: (i, 0)),
        out_shape=jax.ShapeDtypeStruct((l, S5_WIDTH), BF16),
        compiler_params=_params(("parallel",)),
    )(y_all, w_glu_b, b_glu)


def _s5_glu_bwd(y_all, dmix, w_glu_b, b_glu, nct):
    la = y_all.shape[0]
    tm = TOK_TILE

    def body(y_ref, d_ref, w_ref, b_ref, dy_ref, gw_ref, gb_ref):
        i = pl.program_id(0)

        @pl.when(i == 0)
        def _():
            gw_ref[...] = jnp.zeros_like(gw_ref)
            gb_ref[...] = jnp.zeros_like(gb_ref)

        @pl.when(i < nct)
        def _():
            dy_ref[...] = jnp.zeros_like(dy_ref)

        @pl.when(i >= nct)
        def _():
            y = y_ref[...]
            yg, gelu_vjp = jax.vjp(_gelu, y)
            ygb = yg.astype(BF16)
            sg = _sigmoid(_dot(ygb, w_ref[...]) + b_ref[...])
            ds = d_ref[...]
            dz = ds * yg * sg * (1.0 - sg)
            dzb = dz.astype(BF16)
            dyg = ds * sg + _dot_nt(dzb, w_ref[...])
            dy_ref[...] = gelu_vjp(dyg)[0].astype(BF16)
            gw_ref[...] += _dot_tn(ygb, dzb)
            gb_ref[...] += jnp.sum(dz, axis=0, keepdims=True)

    return pl.pallas_call(
        body, name="s5_glu_bwd", grid=(la // tm,),
        in_specs=[pl.BlockSpec((tm, S5_WIDTH), lambda i: (i, 0)),
                  pl.BlockSpec((tm, S5_WIDTH), lambda i: (jnp.maximum(i - nct, 0), 0)),
                  _full((S5_WIDTH, S5_WIDTH)), _full((1, S5_WIDTH))],
        out_specs=[pl.BlockSpec((tm, S5_WIDTH), lambda i: (i, 0)), _full((S5_WIDTH, S5_WIDTH)),
                   _full((1, S5_WIDTH))],
        out_shape=[jax.ShapeDtypeStruct((la, S5_WIDTH), BF16), jax.ShapeDtypeStruct((S5_WIDTH, S5_WIDTH), F32),
                   jax.ShapeDtypeStruct((1, S5_WIDTH), F32)],
        compiler_params=_params(("arbitrary",)),
    )(y_all, dmix, w_glu_b, b_glu)


K_SCALE = RET_DH ** -0.5
Q_COL, K_COL, V_COL, G_COL = 4, 8, 12, 16


def _ret_chunk_of(step, ncc, nch, rev):
    if not rev:
        return step
    return jnp.where(step < ncc, ncc - 1 - step, nch - 1 - (step - ncc))


def _ret_decay(ld, rev):
    c = _iota2((RET_CHUNK, RET_CHUNK), 0).astype(F32)
    m = _iota2((RET_CHUNK, RET_CHUNK), 1).astype(F32)
    diff = (m - c) if rev else (c - m)
    keep = (diff > 0) if rev else (diff >= 0)
    expo = jnp.maximum(diff, 0.0)
    dm = jnp.where(keep, jnp.exp(ld * expo), 0.0)
    xi_e = (RET_CHUNK - c) if rev else (c + 1.0)
    zeta_e = c if rev else (RET_CHUNK - 1.0 - c)
    return dm, expo, jnp.exp(ld * xi_e), xi_e, jnp.exp(ld * zeta_e), zeta_e


RET_TABLES = 7


def _ret_tables(ld2):
    def body(ld_ref, t_ref):
        dr, h = pl.program_id(0), pl.program_id(1)
        ldh = ld_ref[dr, h]
        for rev in (False, True):
            @pl.when(dr == int(rev))
            def _(rev=rev):
                dm, expo, xi, xi_e, zeta, zeta_e = _ret_decay(ldh, rev)
                t_ref[0, 0, 0] = dm
                t_ref[0, 0, 1] = dm * expo
                t_ref[0, 0, 2] = xi
                t_ref[0, 0, 3] = xi * xi_e
                t_ref[0, 0, 4] = zeta
                t_ref[0, 0, 5] = zeta * zeta_e
                t_ref[0, 0, 6] = jnp.zeros_like(dm) + jnp.exp(ldh * RET_CHUNK)

    return pl.pallas_call(
        body, name="ret_tables", grid=(2, RET_HEADS),
        in_specs=[pl.BlockSpec(memory_space=pltpu.SMEM)],
        out_specs=pl.BlockSpec((1, 1, RET_TABLES, RET_CHUNK, RET_CHUNK), lambda d, h: (d, h, 0, 0, 0)),
        out_shape=jax.ShapeDtypeStruct((2, RET_HEADS, RET_TABLES, RET_CHUNK, RET_CHUNK), F32),
        compiler_params=_params(("parallel", "parallel")),
    )(ld2)


def _ret_specs(nch, ncc, rev, step_of):
    chunk = lambda n: _ret_chunk_of(step_of(n), ncc, nch, rev)
    cols = [pl.BlockSpec((RET_CHUNK, RET_WIDTH), functools.partial(lambda n, cb: (chunk(n), cb), cb=cb))
            for cb in (1, 2, 3)]
    tab = pl.BlockSpec((RET_CHUNK, RET_DH), lambda n: (chunk(n), 0))
    return cols + [tab, tab], pl.BlockSpec((RET_CHUNK, RET_WIDTH), lambda n: (chunk(n), 0))


def _ret_scan(p_all, cos_t, sin_t, tabs, ncc, placed, kinds):
    la = p_all.shape[0]
    nch = la // RET_CHUNK
    n = len(placed)
    shard_shapes = _gather_shard_shapes(placed, kinds)

    def body(t_ref, qf, kf, vf, cf, sf, qb, kb, vb, cb, sb, *rest):
        of_ref, ob_ref, ssf_ref, ssb_ref = rest[n:n + 4]
        s_scr, send_sems, recv_sems = rest[2 * n + 4:]
        step = pl.program_id(0)

        @pl.when(step == 0)
        def _():
            s_scr[...] = jnp.zeros_like(s_scr)
            for cp in _gather_chip_copies(rest[n + 4:2 * n + 4], kinds, shard_shapes, send_sems, recv_sems, False)[0]:
                cp.start()

        @pl.when(step == nch - 1)
        def _():
            sends, arrivals = _gather_chip_copies(rest[n + 4:2 * n + 4], kinds, shard_shapes, send_sems, recv_sems)
            for cp in arrivals:
                cp.wait_recv()
            for cp in sends:
                cp.wait_send()

        for dr, (q_ref, k_ref, v_ref, c_ref, n_ref, o_ref, ss_ref) in enumerate(
                ((qf, kf, vf, cf, sf, of_ref, ssf_ref), (qb, kb, vb, cb, sb, ob_ref, ssb_ref))):
            cs, sn = c_ref[...], n_ref[...]
            for h in range(RET_HEADS):
                sl = slice(RET_DH * h, RET_DH * (h + 1))
                dm, xi, zeta = t_ref[dr, h, 0], t_ref[dr, h, 2, :, 0:RET_DH], t_ref[dr, h, 4, :, 0:RET_DH]
                q = _rope(q_ref[:, sl], cs, sn)
                k = _rope(k_ref[:, sl] * K_SCALE, cs, sn)
                vh = v_ref[:, sl].astype(BF16)
                s = s_scr[dr, h]
                ss_ref[0, h] = s
                sc = (_dot_nt(q.astype(BF16), k.astype(BF16)) * dm).astype(BF16)
                o_ref[:, sl] = _dot(sc, vh) + _dot((q * xi).astype(BF16), s.astype(BF16))
                s_scr[dr, h] = t_ref[dr, h, 6, 0:RET_DH, 0:RET_DH] * s + _dot_tn((k * zeta).astype(BF16), vh)

    in_f, out_f = _ret_specs(nch, ncc, False, lambda n: n)
    in_b, out_b = _ret_specs(nch, ncc, True, lambda n: n)
    ss_spec = pl.BlockSpec((1, RET_HEADS, RET_DH, RET_DH), lambda n: (n, 0, 0, 0))
    o_shape = jax.ShapeDtypeStruct((la, RET_WIDTH), F32)
    ss_shape = jax.ShapeDtypeStruct((nch, RET_HEADS, RET_DH, RET_DH), F32)
    return pl.pallas_call(
        body, name="ret_scan", grid=(nch,),
        in_specs=[_full(tabs.shape)] + in_f + in_b + [ANY] * n,
        out_specs=[out_f, out_b, ss_spec, ss_spec] + [ANY] * n,
        out_shape=[o_shape, o_shape, ss_shape, ss_shape] + [jax.ShapeDtypeStruct(p.shape, p.dtype) for p in placed],
        input_output_aliases={11 + a: 4 + a for a in range(n)},
        scratch_shapes=[pltpu.VMEM((2, RET_HEADS, RET_DH, RET_DH), F32),
                        pltpu.SemaphoreType.DMA((n, 3)), pltpu.SemaphoreType.DMA((n, 3))],
        compiler_params=_params(("arbitrary",)),
    )(tabs, p_all, p_all, p_all, cos_t, sin_t, p_all, p_all, p_all, cos_t, sin_t, *placed)


def _ret_scan_bwd(p_all, cos_t, sin_t, tabs, ssf, ssb, dy_all, ncc):
    la = p_all.shape[0]
    nch = la // RET_CHUNK

    def body(t_ref, qf, kf, vf, cf, sf, dof, ssf_ref, qb, kb, vb, cb, sb, dob_, ssb_ref,
             dqf, dkf, dvf, dqb, dkb, dvb, dld_ref, ds_scr):
        @pl.when(pl.program_id(0) == 0)
        def _():
            ds_scr[...] = jnp.zeros_like(ds_scr)
            dld_ref[...] = jnp.zeros_like(dld_ref)

        for dr, (q_ref, k_ref, v_ref, c_ref, n_ref, do_ref, ss_ref, dq_ref, dk_ref, dv_ref) in enumerate(
                ((qf, kf, vf, cf, sf, dof, ssf_ref, dqf, dkf, dvf), (qb, kb, vb, cb, sb, dob_, ssb_ref, dqb, dkb, dvb))):
            cs, sn = c_ref[...], n_ref[...]
            for h in range(RET_HEADS):
                sl = slice(RET_DH * h, RET_DH * (h + 1))
                dm, dm_d = t_ref[dr, h, 0], t_ref[dr, h, 1]
                xi, xi_d, zeta, zeta_d = [t_ref[dr, h, t, :, 0:RET_DH] for t in (2, 3, 4, 5)]
                gc = t_ref[dr, h, 6, 0:RET_DH, 0:RET_DH]
                q = _rope(q_ref[:, sl], cs, sn)
                k = _rope(k_ref[:, sl] * K_SCALE, cs, sn)
                q16, k16, v16 = q.astype(BF16), k.astype(BF16), v_ref[:, sl].astype(BF16)
                s = ss_ref[0, h]
                s16 = s.astype(BF16)
                ds_in = ds_scr[dr, h]
                ds16 = ds_in.astype(BF16)
                do16 = do_ref[:, sl].astype(BF16)
                qk = _dot_nt(q16, k16)
                dsv = _dot_nt(do16, v16)
                dsc = (dsv * dm).astype(BF16)
                sc16 = (qk * dm).astype(BF16)
                dos = _dot_nt(do16, s16)
                vds = _dot_nt(v16, ds16)
                dq_ref[:, sl] = _dot(dsc, k16) + dos * xi
                dk_ref[:, sl] = _dot_tn(dsc, q16) + vds * zeta
                dv_ref[:, sl] = _dot_tn(sc16, do16) + _dot((k * zeta).astype(BF16), ds16)
                ds_scr[dr, h] = _dot_tn((q * xi).astype(BF16), do16) + gc * ds_in
                dld = (jnp.sum(dsv * qk * dm_d) + jnp.sum(q * dos * xi_d + k * vds * zeta_d)
                       + RET_CHUNK * jnp.sum(gc * s * ds_in))
                dld_ref[dr, h] += dld

    back = lambda n: nch - 1 - n
    in_f, out_f = _ret_specs(nch, ncc, False, back)
    in_b, out_b = _ret_specs(nch, ncc, True, back)
    ss_spec = pl.BlockSpec((1, RET_HEADS, RET_DH, RET_DH), lambda n: (nch - 1 - n, 0, 0, 0))
    shp = jax.ShapeDtypeStruct((la, RET_WIDTH), F32)
    return pl.pallas_call(
        body, name="ret_scan_bwd", grid=(nch,),
        in_specs=[_full(tabs.shape)] + in_f + [out_f, ss_spec] + in_b + [out_b, ss_spec],
        out_specs=[out_f, out_f, out_f, out_b, out_b, out_b, _full((2, RET_HEADS, 8, 128))],
        out_shape=[shp] * 6 + [jax.ShapeDtypeStruct((2, RET_HEADS, 8, 128), F32)],
        scratch_shapes=[pltpu.VMEM((2, RET_HEADS, RET_DH, RET_DH), F32)],
        compiler_params=_params(("arbitrary",)),
    )(tabs, p_all, p_all, p_all, cos_t, sin_t, dy_all, ssf, p_all, p_all, p_all, cos_t, sin_t, dy_all, ssb)


def _ret_gate(of, ob, p_all, nct):
    la = of.shape[0]
    tm = TOK_TILE
    l = la - nct * tm

    def body(of_ref, ob_ref, g_ref, r_ref, y_ref):
        y = of_ref[...] + ob_ref[...]
        y_ref[...] = y
        for h in range(RET_HEADS):
            sl = slice(RET_DH * h, RET_DH * (h + 1))
            r_ref[:, sl] = _head_norm_gate(y[:, sl], g_ref[:, sl]).astype(BF16)

    row = pl.BlockSpec((tm, RET_WIDTH), lambda i: (i + nct, 0))
    out = pl.BlockSpec((tm, RET_WIDTH), lambda i: (i, 0))
    return pl.pallas_call(
        body, name="ret_gate", grid=(l // tm,),
        in_specs=[row, row, pl.BlockSpec((tm, RET_WIDTH), lambda i: (i + nct, G_COL // 4))],
        out_specs=[out, out],
        out_shape=[jax.ShapeDtypeStruct((l, RET_WIDTH), BF16), jax.ShapeDtypeStruct((l, RET_WIDTH), F32)],
        compiler_params=_params(("parallel",)),
    )(of, ob, p_all)


def _ret_gate_bwd(y_ret, p_all, dmix, nct):
    la = p_all.shape[0]
    tm = TOK_TILE

    def body(y_ref, g_ref, d_ref, dy_ref, dg_ref):
        i = pl.program_id(0)

        @pl.when(i < nct)
        def _():
            dy_ref[...] = jnp.zeros_like(dy_ref)
            dg_ref[...] = jnp.zeros_like(dg_ref)

        @pl.when(i >= nct)
        def _():
            for h in range(RET_HEADS):
                sl = slice(RET_DH * h, RET_DH * (h + 1))
                _, vjp = jax.vjp(_head_norm_gate, y_ref[:, sl], g_ref[:, sl])
                dy, dg = vjp(d_ref[:, sl])
                dy_ref[:, sl] = dy
                dg_ref[:, sl] = dg

    xrow = lambda cb: pl.BlockSpec((tm, RET_WIDTH), lambda i: (jnp.maximum(i - nct, 0), cb))
    out = pl.BlockSpec((tm, RET_WIDTH), lambda i: (i, 0))
    shp = jax.ShapeDtypeStruct((la, RET_WIDTH), F32)
    return pl.pallas_call(
        body, name="ret_gate_bwd", grid=(la // tm,),
        in_specs=[xrow(0), pl.BlockSpec((tm, RET_WIDTH), lambda i: (i, G_COL // 4)), xrow(1)],
        out_specs=[out, out], out_shape=[shp, shp],
        compiler_params=_params(("parallel",)),
    )(y_ret, p_all, dmix)


def _in_bwd(dqf, dkf, dvf, dqb, dkb, dvb, du, dg, cos_t, sin_t, w_in_b, x, ctx, n1w, mod4, dx1):
    l, lc = x.shape[0], ctx.shape[0]
    la = l + lc
    tm = TOK_TILE
    nct = lc // tm

    def body(dqf_ref, dkf_ref, dvf_ref, dqb_ref, dkb_ref, dvb_ref, du_ref, dg_ref, cos_ref, sin_ref,
             w_ref, x_ref, c_ref, nw_ref, mod_ref, dx1_ref, dp_ref, gx_ref, acc_ref):
        i = pl.program_id(0)
        is_ctx = i < nct

        @pl.when(i == 0)
        def _():
            acc_ref[...] = jnp.zeros_like(acc_ref)

        cs, sn = cos_ref[...], sin_ref[...]
        def piece(k, val):
            cols = slice(S5_WIDTH * k, S5_WIDTH * (k + 1))
            dp_ref[:, cols] = val.astype(BF16)
            return _dot_nt(dp_ref[:, cols], w_ref[:, cols])

        dh1 = piece(0, du_ref[...])
        dh1 = dh1 + piece(3, dvf_ref[...] + dvb_ref[...])
        dh1 = dh1 + piece(4, dg_ref[...])
        for k, (f_ref, b_ref, scale) in ((1, (dqf_ref, dqb_ref, 1.0)), (2, (dkf_ref, dkb_ref, K_SCALE))):
            heads = [_rope_t(f_ref[:, RET_DH * h:RET_DH * (h + 1)] + b_ref[:, RET_DH * h:RET_DH * (h + 1)], cs, sn) * scale
                     for h in range(RET_HEADS)]
            dh1 = dh1 + piece(k, jnp.concatenate(heads, axis=1))
        xt = jnp.where(is_ctx, c_ref[...], x_ref[...])
        sh = jnp.where(is_ctx, mod_ref[0:1, :], mod_ref[2:3, :])
        sc = jnp.where(is_ctx, mod_ref[1:2, :], mod_ref[3:4, :])
        _, vjp = jax.vjp(_rms_mod, xt, nw_ref[...], sh, sc)
        dx, dnw, dsh, dsc = vjp(dh1)
        gx_ref[...] = dx + dx1_ref[...]
        cf = jnp.where(is_ctx, 1.0, 0.0)
        acc_ref[0:1, :] += dnw
        acc_ref[1:2, :] += cf * dsh
        acc_ref[2:3, :] += cf * dsc
        acc_ref[3:4, :] += (1.0 - cf) * dsh
        acc_ref[4:5, :] += (1.0 - cf) * dsc

    row = pl.BlockSpec((tm, RET_WIDTH), lambda i: (i, 0))
    tab = pl.BlockSpec((tm, RET_DH), lambda i: (i, 0))
    xrow = pl.BlockSpec((tm, D_MODEL), lambda i: (jnp.maximum(i - nct, 0), 0))
    return pl.pallas_call(
        body, name="in_bwd", grid=(la // tm,),
        in_specs=[row] * 8 + [tab, tab, _full((D_MODEL, IN_COLS)), xrow,
                              pl.BlockSpec((tm, D_MODEL), lambda i: (jnp.minimum(i, nct - 1), 0)),
                              _full((1, D_MODEL)), _full((4, D_MODEL)), xrow],
        out_specs=[pl.BlockSpec((tm, IN_COLS), lambda i: (i, 0)), xrow, _full((8, D_MODEL))],
        out_shape=[jax.ShapeDtypeStruct((la, IN_COLS), BF16), jax.ShapeDtypeStruct((l, D_MODEL), F32),
                   jax.ShapeDtypeStruct((8, D_MODEL), F32)],
        compiler_params=_params(("arbitrary",)),
    )(dqf, dkf, dvf, dqb, dkb, dvb, du, dg, cos_t, sin_t, w_in_b, x, ctx, n1w, mod4, dx1)


def _outproj_up(x, s5x, retx, w_out_b, mod3, n2w, w_up_b):
    l = x.shape[0]
    tm = TOK_TILE

    def body(x_ref, s_ref, r_ref, wo_ref, mod_ref, nw_ref, wu_ref, x1_ref, mix_ref, h2_ref, up_ref):
        mix = _dot(s_ref[...], wo_ref[0:S5_WIDTH, :]) + _dot(r_ref[...], wo_ref[S5_WIDTH:D_MODEL, :])
        mix_ref[...] = mix
        x1 = x_ref[...] + mod_ref[0:1, :] * mix
        x1_ref[...] = x1
        h2 = _rms_mod(x1, nw_ref[...], mod_ref[1:2, :], mod_ref[2:3, :]).astype(BF16)
        h2_ref[...] = h2
        up_ref[...] = _dot(h2, wu_ref[...])

    row = lambda w: pl.BlockSpec((tm, w), lambda i: (i, 0))
    return pl.pallas_call(
        body, name="outproj_up", grid=(l // tm,),
        in_specs=[row(D_MODEL), row(S5_WIDTH), row(RET_WIDTH), _full((D_MODEL, D_MODEL)), _full((3, D_MODEL)),
                  _full((1, D_MODEL)), _full((D_MODEL, 2 * D_FF))],
        out_specs=[row(D_MODEL), row(D_MODEL), row(D_MODEL), row(2 * D_FF)],
        out_shape=[jax.ShapeDtypeStruct((l, D_MODEL), F32), jax.ShapeDtypeStruct((l, D_MODEL), F32),
                   jax.ShapeDtypeStruct((l, D_MODEL), BF16), jax.ShapeDtypeStruct((l, 2 * D_FF), F32)],
        compiler_params=_params(("parallel",)),
    )(x, s5x, retx, w_out_b, mod3, n2w, w_up_b)


HALO = 8


def _conv_taps(g, prev_row, next_row):
    t = g.shape[0]
    r = _iota2(g.shape, 0)
    gprev = jnp.where(r == 0, prev_row, pltpu.roll(g, 1, 0))
    gnext = jnp.where(r == t - 1, next_row, pltpu.roll(g, t - 1, 0))
    return gprev, gnext


def _ffn_loss(up, x1, conv_w, conv_b, w_down_b, gate, fnw, tgt):
    l = x1.shape[0]
    tm = TOK_TILE
    nt = l // tm
    hb = tm // HALO

    cw = 256

    def body(up_a, up_g, hp_ref, hn_ref, x1_ref, cw_ref, cb_ref, wd_ref, gate_ref, fn_ref, tgt_ref,
             act_ref, dx2_ref, ddn_ref, dact_ref, acc_ref):
        i = pl.program_id(0)

        @pl.when(i == 0)
        def _():
            acc_ref[...] = jnp.zeros_like(acc_ref)

        dn = jnp.zeros((tm, D_MODEL), F32)
        for c in range(D_FF // cw):
            cols = slice(cw * c, cw * (c + 1))
            g = up_g[:, cols]
            prev_row = jnp.where(i == 0, 0.0, hp_ref[HALO - 1:HALO, cols])
            next_row = jnp.where(i == nt - 1, 0.0, hn_ref[0:1, cols])
            gprev, gnext = _conv_taps(g, prev_row, next_row)
            gc = cb_ref[:, cols] + gprev * cw_ref[0:1, cols] + g * cw_ref[1:2, cols] + gnext * cw_ref[2:3, cols]
            act = (_gelu(gc) * up_a[:, cols]).astype(BF16)
            act_ref[:, cols] = act
            dn = dn + _dot(act, wd_ref[cols, :])
        x2 = x1_ref[...] + gate_ref[...] * dn
        y, vjp = jax.vjp(_rms, x2, fn_ref[...])
        err = y - tgt_ref[...]
        dx2, dfn = vjp(err * (1.0 / D_MODEL))
        dx2_ref[...] = dx2
        ddn = (dx2 * gate_ref[...]).astype(BF16)
        ddn_ref[...] = ddn
        for c in range(D_FF // cw):
            cols = slice(cw * c, cw * (c + 1))
            dact_ref[:, cols] = _dot_nt(ddn, wd_ref[cols, :])
        acc_ref[0:1, :] += dfn
        acc_ref[1:2, :] += jnp.sum(dx2 * dn, axis=0, keepdims=True)
        acc_ref[2:3, :] += (0.5 / D_MODEL) * jnp.sum(err * err)

    row = lambda w: pl.BlockSpec((tm, w), lambda i: (i, 0))
    last = l // HALO - 1
    return pl.pallas_call(
        body, name="ffn_loss", grid=(nt,),
        in_specs=[pl.BlockSpec((tm, D_FF), lambda i: (i, 0)), pl.BlockSpec((tm, D_FF), lambda i: (i, 1)),
                  pl.BlockSpec((HALO, D_FF), lambda i: (jnp.maximum(i * hb - 1, 0), 1)),
                  pl.BlockSpec((HALO, D_FF), lambda i: (jnp.minimum((i + 1) * hb, last), 1)),
                  row(D_MODEL), _full((3, D_FF)), _full((1, D_FF)), _full((D_FF, D_MODEL)),
                  _full((1, D_MODEL)), _full((1, D_MODEL)), row(D_MODEL)],
        out_specs=[row(D_FF), row(D_MODEL), row(D_MODEL), row(D_FF), _full((8, D_MODEL))],
        out_shape=[jax.ShapeDtypeStruct((l, D_FF), BF16), jax.ShapeDtypeStruct((l, D_MODEL), F32),
                   jax.ShapeDtypeStruct((l, D_MODEL), BF16), jax.ShapeDtypeStruct((l, D_FF), F32),
                   jax.ShapeDtypeStruct((8, D_MODEL), F32)],
        compiler_params=_params(("arbitrary",)),
    )(up, up, up, up, x1, conv_w, conv_b, w_down_b, gate, fnw, tgt)


def _convglu_bwd(up, dact, conv_w, conv_b):
    l = up.shape[0]
    tm = 128
    nt = l // tm
    hb = tm // HALO
    te = tm + 2 * HALO

    def body(a_ref, ap_ref, an_ref, g_ref, gp_ref, gn_ref, d_ref, dp_ref, dn_ref, cw_ref, cb_ref,
             dup_ref, acc_ref):
        i = pl.program_id(0)

        @pl.when(i == 0)
        def _():
            acc_ref[...] = jnp.zeros_like(acc_ref)

        def ext(p, c, n):
            return jnp.concatenate([jnp.where(i == 0, 0.0, p[...]), c[...], jnp.where(i == nt - 1, 0.0, n[...])], axis=0)

        ae, ge, de = ext(ap_ref, a_ref, an_ref), ext(gp_ref, g_ref, gn_ref), ext(dp_ref, d_ref, dn_ref)
        gprev = pltpu.roll(ge, 1, 0)
        gnext = pltpu.roll(ge, te - 1, 0)
        w0, w1, w2 = cw_ref[0:1, :], cw_ref[1:2, :], cw_ref[2:3, :]
        gce = cb_ref[...] + gprev * w0 + ge * w1 + gnext * w2
        gel, dgel = _gelu_and_grad(gce)
        dae = de * gel
        dgce = de * ae * dgel
        dge = dgce * w1 + pltpu.roll(dgce, te - 1, 0) * w0 + pltpu.roll(dgce, 1, 0) * w2
        mid = slice(HALO, HALO + tm)
        dup_ref[:, 0:D_FF] = dae[mid].astype(BF16)
        dup_ref[:, D_FF:2 * D_FF] = dge[mid].astype(BF16)
        dgc = dgce[mid]
        acc_ref[0:1, :] += jnp.sum(dgc * gprev[mid], axis=0, keepdims=True)
        acc_ref[1:2, :] += jnp.sum(dgc * ge[mid], axis=0, keepdims=True)
        acc_ref[2:3, :] += jnp.sum(dgc * gnext[mid], axis=0, keepdims=True)
        acc_ref[3:4, :] += jnp.sum(dgc, axis=0, keepdims=True)

    last = l // HALO - 1

    def trio(cb):
        return [pl.BlockSpec((tm, D_FF), lambda i: (i, cb)),
                pl.BlockSpec((HALO, D_FF), lambda i: (jnp.maximum(i * hb - 1, 0), cb)),
                pl.BlockSpec((HALO, D_FF), lambda i: (jnp.minimum((i + 1) * hb, last), cb))]

    return pl.pallas_call(
        body, name="convglu_bwd", grid=(nt,),
        in_specs=trio(0) + trio(1) + trio(0) + [_full((3, D_FF)), _full((1, D_FF))],
        out_specs=[pl.BlockSpec((tm, 2 * D_FF), lambda i: (i, 0)), _full((8, D_FF))],
        out_shape=[jax.ShapeDtypeStruct((l, 2 * D_FF), BF16), jax.ShapeDtypeStruct((8, D_FF), F32)],
        compiler_params=_params(("arbitrary",)),
    )(up, up, up, up, up, up, dact, dact, dact, conv_w, conv_b)


def _up_bwd(dup, w_up_b, w_out_b, x1, dx2, mix, mod3, n2w, pairs, kinds):
    l = x1.shape[0]
    tm = TOK_TILE
    nt = l // tm
    n = len(pairs)
    shapes = _rs_slot_shapes(pairs, kinds)

    def body(dup_ref, wu_ref, wo_ref, x1_ref, dx2_ref, mix_ref, mod_ref, nw_ref, *rest):
        dx1_ref, dmixb_ref, dmix_ref, acc_ref = rest[n:n + 4]
        exchange = functools.partial(_rs_chip_copies, rest[:n], rest[n + 4:2 * n + 4], kinds, shapes, *rest[2 * n + 4:])
        step = pl.program_id(0)

        @pl.when(step == 0)
        def _():
            acc_ref[...] = jnp.zeros_like(acc_ref)
            for cp in exchange(with_arrivals=False)[0]:
                cp.start()

        @pl.when(step == nt - 1)
        def _():
            sends, arrivals = exchange()
            for cp in arrivals:
                cp.wait_recv()
            for cp in sends:
                cp.wait_send()

        dh2 = _dot_nt(dup_ref[...], wu_ref[...])
        _, vjp = jax.vjp(_rms_mod, x1_ref[...], nw_ref[...], mod_ref[1:2, :], mod_ref[2:3, :])
        dx, dnw, dsh, dsc = vjp(dh2)
        dx1 = dx + dx2_ref[...]
        dx1_ref[...] = dx1
        dmixb = (dx1 * mod_ref[0:1, :]).astype(BF16)
        dmixb_ref[...] = dmixb
        dmix_ref[...] = _dot_nt(dmixb, wo_ref[...])
        acc_ref[0:1, :] += dnw
        acc_ref[1:2, :] += jnp.sum(dx1 * mix_ref[...], axis=0, keepdims=True)
        acc_ref[2:3, :] += dsh
        acc_ref[3:4, :] += dsc

    row = pl.BlockSpec((tm, D_MODEL), lambda i: (i, 0))
    return pl.pallas_call(
        body, name="up_bwd", grid=(nt,),
        in_specs=[pl.BlockSpec((tm, 2 * D_FF), lambda i: (i, 0)), _full((D_MODEL, 2 * D_FF)),
                  _full((D_MODEL, D_MODEL)), row, row, row, _full((3, D_MODEL)), _full((1, D_MODEL))] + [ANY] * n,
        out_specs=[row, row, row, _full((8, D_MODEL))] + [ANY] * n,
        out_shape=[jax.ShapeDtypeStruct((l, D_MODEL), F32), jax.ShapeDtypeStruct((l, D_MODEL), BF16),
                   jax.ShapeDtypeStruct((l, D_MODEL), F32), jax.ShapeDtypeStruct((8, D_MODEL), F32)]
        + [jax.ShapeDtypeStruct((4,) + s, p.dtype) for s, p in zip(shapes, pairs)],
        scratch_shapes=[pltpu.SemaphoreType.DMA((n, 3)), pltpu.SemaphoreType.DMA((n, 3))],
        compiler_params=_params(("arbitrary",)),
    )(dup, w_up_b, w_out_b, x1, dx2, mix, mod3, n2w, *pairs)


MOD_ROWS = 16
MOD_COLS = 6 * D_MODEL // 4


def _mod_fwd(c_all, c_ctx, w_mod_b, b_loc):
    def body(c_ref, cc_ref, w_ref, b_ref, m_ref, s_ref):
        cond = jnp.concatenate([c_ref[...], jnp.broadcast_to(cc_ref[...], (8, D_MODEL))], axis=0)
        s = _silu(cond).astype(BF16)
        s_ref[...] = s
        m_ref[...] = _dot(s, w_ref[...]) + b_ref[...]

    return pl.pallas_call(
        body, name="mod_fwd",
        out_shape=[jax.ShapeDtypeStruct((MOD_ROWS, MOD_COLS), F32), jax.ShapeDtypeStruct((MOD_ROWS, D_MODEL), BF16)],
        compiler_params=_params(),
    )(c_all, c_ctx, w_mod_b, b_loc)


def _mod_bwd_sum(dm_all):
    def body(d_ref, dm_ref, gb_ref):
        rows = [d_ref[k, 0:1, :] for k in range(8)]
        ctx_sum = d_ref[0, 1:2, :]
        for k in range(1, 8):
            ctx_sum = ctx_sum + d_ref[k, 1:2, :]
        gb = ctx_sum
        for k in range(8):
            gb = gb + rows[k]
        gb_ref[...] = gb
        dm_ref[...] = jnp.concatenate(rows + [ctx_sum] + [jnp.zeros((7, 6 * D_MODEL), F32)], axis=0)

    return pl.pallas_call(
        body, name="mod_bwd_sum",
        out_shape=[jax.ShapeDtypeStruct((MOD_ROWS, 6 * D_MODEL), F32), jax.ShapeDtypeStruct((1, 6 * D_MODEL), F32)],
        compiler_params=_params(),
    )(dm_all)


def _mod_bwd_w(dm_loc, s_b, c_ctx, w_mod_b):
    def body(d_ref, s_ref, cc_ref, w_ref, gw_ref, gc_ref):
        db = d_ref[...].astype(BF16)
        gw_ref[...] = _dot_tn(s_ref[...], db)
        ds = _dot_nt(db, w_ref[...])
        _, vjp = jax.vjp(_silu, cc_ref[...])
        gc_ref[...] = jnp.broadcast_to(vjp(ds[8:9, :])[0], (8, D_MODEL))

    return pl.pallas_call(
        body, name="mod_bwd_w",
        out_shape=[jax.ShapeDtypeStruct((D_MODEL, MOD_COLS), F32), jax.ShapeDtypeStruct((8, D_MODEL), F32)],
        compiler_params=_params(),
    )(dm_loc, s_b, c_ctx, w_mod_b)


def _adamw(w, g, m, v, name):
    r, c = w.shape
    tr = _pick(r, (256, 128, 64, 32, 16, 8))
    bc1 = 1.0 - ADAM_B1 ** ADAM_STEP
    bc2 = 1.0 - ADAM_B2 ** ADAM_STEP

    def body(w_ref, g_ref, m_ref, v_ref, d_ref, nm_ref, nv_ref):
        gg = g_ref[...]
        nm = ADAM_B1 * m_ref[...] + (1.0 - ADAM_B1) * gg
        nv = ADAM_B2 * v_ref[...] + (1.0 - ADAM_B2) * (gg * gg)
        nm_ref[...] = nm
        nv_ref[...] = nv
        d_ref[...] = -ADAM_LR * ((nm / bc1) / (jnp.sqrt(nv / bc2) + ADAM_EPS) + ADAM_WD * w_ref[...])

    blk = pl.BlockSpec((tr, c), lambda i: (i, 0))
    shp = jax.ShapeDtypeStruct((r, c), F32)
    return pl.pallas_call(
        body, name=name, grid=(r // tr,), in_specs=[blk] * 4, out_specs=[blk] * 3, out_shape=[shp] * 3,
        compiler_params=_params(("parallel",)),
    )(w, g, m, v)


def _sum_slots(a, name):
    n, r, c = a.shape
    tr = _pick(r, (376, 256, 208, 128, 64, 32, 16, 8))

    def body(a_ref, o_ref):
        acc = a_ref[0].astype(F32)
        for k in range(1, n):
            acc = acc + a_ref[k].astype(F32)
        o_ref[...] = acc

    return pl.pallas_call(
        body, name=name, grid=(r // tr,),
        in_specs=[pl.BlockSpec((n, tr, c), lambda i: (0, i, 0))],
        out_specs=pl.BlockSpec((tr, c), lambda i: (i, 0)),
        out_shape=jax.ShapeDtypeStruct((r, c), F32),
        compiler_params=_params(("parallel",)),
    )(a)


def _mesh_pos():
    return lax.axis_index("x"), lax.axis_index("y"), lax.axis_index("c")


def _all_gather8(v, name):
    m_per, n = v.shape

    def body(x_ref, out_ref, send_sems, recv_sems, local_sem):
        x, y, c = _mesh_pos()
        me, sibling = (x, y, c), (x, y, 1 - c)
        chips = [(1 - x, y), (x, 1 - y), (1 - x, 1 - y)]

        def rows(px, py, pc):
            return out_ref.at[pl.ds((4 * px + 2 * py + pc) * m_per, m_per), :]

        def copy(k, block, to, src=None):
            return pltpu.make_async_remote_copy(
                src_ref=rows(*block) if src is None else src, dst_ref=rows(*block),
                send_sem=send_sems.at[k], recv_sem=recv_sems.at[k], device_id=to, device_id_type=MESH_ID)

        mine = pltpu.make_async_copy(x_ref, rows(*me), local_sem)
        mine.start()
        first = [copy(0, me, sibling, src=x_ref)]
        first += [copy(1 + j, me, (*chip, c), src=x_ref) for j, chip in enumerate(chips)]
        for cp in first:
            cp.start()
        passed = [copy(4 + j, (*chip, c), sibling) for j, chip in enumerate(chips)]
        for j, chip in enumerate(chips):
            copy(1 + j, (*chip, c), me).wait_recv()
            passed[j].start()
        copy(0, sibling, me).wait_recv()
        for j, chip in enumerate(chips):
            copy(4 + j, (*chip, 1 - c), me).wait_recv()
        for cp in first + passed:
            cp.wait_send()
        mine.wait()

    return pl.pallas_call(
        body, name=name,
        out_shape=jax.ShapeDtypeStruct((8 * m_per, n), v.dtype),
        in_specs=[pl.BlockSpec(memory_space=pltpu.VMEM)],
        out_specs=pl.BlockSpec(memory_space=pltpu.VMEM),
        scratch_shapes=[pltpu.SemaphoreType.DMA((7,)), pltpu.SemaphoreType.DMA((7,)), pltpu.SemaphoreType.DMA],
        compiler_params=_params(),
    )(v)


ANY = pl.BlockSpec(memory_space=pl.ANY)
PEER_CHIPS = lambda x, y: [(x, 1 - y), (1 - x, y), (1 - x, 1 - y)]


def _shard_region(ref, kind, k, rl, cl, r0, nr, c0, nc):
    if kind == "col":
        return ref.at[pl.ds(r0, nr), pl.ds(k * cl + c0, nc)]
    return ref.at[pl.ds(k * rl + r0, nr), pl.ds(c0, nc)]


def _place_shard(w, kind, chip, name):
    rl, cl = w.shape
    tr = _pick(rl, (256, 128, 64))
    nt = rl // tr

    def body(chip_ref, w_ref, o_ref):
        o_ref[...] = w_ref[...].astype(BF16)

    o_map = (lambda i, chip_ref: (i, chip_ref[0])) if kind == "col" else (lambda i, chip_ref: (chip_ref[0] * nt + i, 0))
    return pl.pallas_call(
        body, name=name,
        grid_spec=pltpu.PrefetchScalarGridSpec(
            num_scalar_prefetch=1, grid=(nt,),
            in_specs=[pl.BlockSpec((tr, cl), lambda i, chip_ref: (i, 0))], out_specs=pl.BlockSpec((tr, cl), o_map)),
        out_shape=jax.ShapeDtypeStruct((rl, 4 * cl) if kind == "col" else (4 * rl, cl), BF16),
        compiler_params=_params(("parallel",)),
    )(chip.reshape(1), w)


def _gather_shard_shapes(placed, kinds):
    return [(p.shape[0], p.shape[1] // 4) if k == "col" else (p.shape[0] // 4, p.shape[1]) for p, k in zip(placed, kinds)]


def _gather_chip_copies(outs, kinds, shard_shapes, send_sems, recv_sems, with_arrivals=True):
    x, y, c = _mesh_pos()
    me = 2 * x + y
    sends, arrivals = [], []
    for a in range(len(outs)):
        rl, cl = shard_shapes[a]
        rh = rl // 2
        reg = functools.partial(_shard_region, outs[a], kinds[a], rl=rl, cl=cl, r0=c * rh, nr=rh, c0=0, nc=cl)
        for j, (px, py) in enumerate(PEER_CHIPS(x, y)):
            to = dict(send_sem=send_sems.at[a, j], recv_sem=recv_sems.at[a, j], device_id=(px, py, c),
                      device_id_type=MESH_ID)
            sends.append(pltpu.make_async_remote_copy(src_ref=reg(k=me), dst_ref=reg(k=me), **to))
            if with_arrivals:
                got = reg(k=2 * px + py)
                arrivals.append(pltpu.make_async_remote_copy(src_ref=got, dst_ref=got, **to))
    return sends, arrivals


def _gather_sibling_copies(outs, kinds, shard_shapes, send_sems, recv_sems):
    x, y, c = _mesh_pos()
    forwards, arrivals = [], []
    for a in range(len(outs)):
        rl, cl = shard_shapes[a]
        rh = rl // 2
        for j, (px, py) in enumerate(PEER_CHIPS(x, y)):
            to = dict(send_sem=send_sems.at[a, j], recv_sem=recv_sems.at[a, j], device_id=(x, y, 1 - c),
                      device_id_type=MESH_ID)
            reg = functools.partial(_shard_region, outs[a], kinds[a], k=2 * px + py, rl=rl, cl=cl, nr=rh, c0=0, nc=cl)
            forwards.append(pltpu.make_async_remote_copy(src_ref=reg(r0=c * rh), dst_ref=reg(r0=c * rh), **to))
            arrivals.append(pltpu.make_async_remote_copy(src_ref=reg(r0=(1 - c) * rh), dst_ref=reg(r0=(1 - c) * rh), **to))
    return forwards, arrivals


def _gather_weights(placed, kinds):
    n = len(placed)
    shard_shapes = _gather_shard_shapes(placed, kinds)

    def body(*refs):
        outs = refs[n:2 * n]
        ici_send, ici_recv, sib_send, sib_recv = refs[2 * n:]
        sends, arrivals = _gather_chip_copies(outs, kinds, shard_shapes, ici_send, ici_recv)
        for cp in sends:
            cp.start()
        forwards, from_sibling = _gather_sibling_copies(outs, kinds, shard_shapes, sib_send, sib_recv)
        for cp, fwd in zip(arrivals, forwards):
            cp.wait_recv()
            fwd.start()
        for cp in from_sibling:
            cp.wait_recv()
        for cp in sends + forwards:
            cp.wait_send()

    return pl.pallas_call(
        body, name="gather_weights",
        out_shape=[jax.ShapeDtypeStruct(p.shape, p.dtype) for p in placed],
        in_specs=[ANY] * n, out_specs=[ANY] * n, input_output_aliases={a: a for a in range(n)},
        scratch_shapes=[pltpu.SemaphoreType.DMA((n, 3))] * 4,
        compiler_params=_params(),
    )(*placed)


def _gather_sibling(placed, kinds):
    n = len(placed)
    shard_shapes = _gather_shard_shapes(placed, kinds)

    def body(*refs):
        forwards, from_sibling = _gather_sibling_copies(refs[n:2 * n], kinds, shard_shapes, *refs[2 * n:])
        for cp in forwards:
            cp.start()
        for cp in from_sibling:
            cp.wait_recv()
        for cp in forwards:
            cp.wait_send()

    return pl.pallas_call(
        body, name="gather_sibling",
        out_shape=[jax.ShapeDtypeStruct(p.shape, p.dtype) for p in placed],
        in_specs=[ANY] * n, out_specs=[ANY] * n, input_output_aliases={a: a for a in range(n)},
        scratch_shapes=[pltpu.SemaphoreType.DMA((n, 3))] * 2,
        compiler_params=_params(),
    )(*placed)


def _half(kind, r, c):
    return (r // 2, c) if kind == "col" else (r, c // 2)


def _half_of(ref, kind, which):
    r, c = ref.shape
    hr, hc = _half(kind, r, c)
    return ref.at[pl.ds(which * hr, hr), :] if kind == "col" else ref.at[:, pl.ds(which * hc, hc)]


def _rs_sibling(grads, kinds, name):
    n = len(grads)

    def body(*refs):
        srcs, dsts = refs[:n], refs[n:2 * n]
        send_sems, recv_sems = refs[2 * n:]
        x, y, c = _mesh_pos()
        cps = [pltpu.make_async_remote_copy(src_ref=_half_of(srcs[a], kinds[a], 1 - c), dst_ref=dsts[a],
                                            send_sem=send_sems.at[a], recv_sem=recv_sems.at[a],
                                            device_id=(x, y, 1 - c), device_id_type=MESH_ID) for a in range(n)]
        for cp in cps:
            cp.start()
        for cp in cps:
            cp.wait()

    return pl.pallas_call(
        body, name=name,
        out_shape=[jax.ShapeDtypeStruct(_half(k, *g.shape), g.dtype) for g, k in zip(grads, kinds)],
        in_specs=[ANY] * n, out_specs=[ANY] * n,
        scratch_shapes=[pltpu.SemaphoreType.DMA((n,)), pltpu.SemaphoreType.DMA((n,))],
        compiler_params=_params(),
    )(*grads)


def _pair_sum(gf, rv, kind, ci, name):
    r, c = rv.shape
    tr = _pick(r, (128, 64, 32, 16, 8))
    nt = r // tr

    def body(ci_ref, g_ref, r_ref, o_ref):
        o_ref[...] = (g_ref[...] + r_ref[...]).astype(BF16)

    g_map = (lambda i, ci_ref: (ci_ref[0] * nt + i, 0)) if kind == "col" else (lambda i, ci_ref: (i, ci_ref[0]))
    blk = pl.BlockSpec((tr, c), lambda i, ci_ref: (i, 0))
    return pl.pallas_call(
        body, name=name,
        grid_spec=pltpu.PrefetchScalarGridSpec(num_scalar_prefetch=1, grid=(nt,),
                                               in_specs=[pl.BlockSpec((tr, c), g_map), blk], out_specs=blk),
        out_shape=jax.ShapeDtypeStruct((r, c), BF16),
        compiler_params=_params(("parallel",)),
    )(ci.reshape(1), gf, rv)


def _rs_slot_shapes(pairs, kinds):
    return [(p.shape[0], p.shape[1] // 4) if k == "col" else (p.shape[0] // 4, p.shape[1]) for p, k in zip(pairs, kinds)]


def _rs_chip_copies(srcs, dsts, kinds, shapes, send_sems, recv_sems, with_arrivals=True):
    x, y, c = _mesh_pos()
    me = 2 * x + y
    sends, arrivals = [], []
    for a in range(len(srcs)):
        rl, cl = shapes[a]
        reg = functools.partial(_shard_region, srcs[a], kinds[a], rl=rl, cl=cl, r0=0, nr=rl, c0=0, nc=cl)
        for j, (px, py) in enumerate(PEER_CHIPS(x, y)):
            to = dict(send_sem=send_sems.at[a, j], recv_sem=recv_sems.at[a, j], device_id=(px, py, c),
                      device_id_type=MESH_ID)
            sends.append(pltpu.make_async_remote_copy(src_ref=reg(k=2 * px + py), dst_ref=dsts[a].at[me], **to))
            if with_arrivals:
                slot = dsts[a].at[2 * px + py]
                arrivals.append(pltpu.make_async_remote_copy(src_ref=slot, dst_ref=slot, **to))
    return sends, arrivals


def _rs_chips(pairs, kinds):
    n = len(pairs)
    shapes = _rs_slot_shapes(pairs, kinds)

    def body(*refs):
        sends, arrivals = _rs_chip_copies(refs[:n], refs[n:2 * n], kinds, shapes, *refs[2 * n:])
        for cp in sends:
            cp.start()
        for cp in arrivals:
            cp.wait_recv()
        for cp in sends:
            cp.wait_send()

    return pl.pallas_call(
        body, name="rs_chips",
        out_shape=[jax.ShapeDtypeStruct((4,) + s, p.dtype) for s, p in zip(shapes, pairs)],
        in_specs=[ANY] * n, out_specs=[ANY] * n,
        scratch_shapes=[pltpu.SemaphoreType.DMA((n, 3)), pltpu.SemaphoreType.DMA((n, 3))],
        compiler_params=_params(),
    )(*pairs)


def _sum_chips(pair, got, kind, pos, name):
    _, r, c = got.shape
    tr = _pick(r, (256, 128, 64, 32, 16))
    nt = r // tr

    def body(pos_ref, own_ref, g1_ref, g2_ref, g3_ref, o_ref):
        o_ref[...] = ((own_ref[...].astype(F32) + g1_ref[0].astype(F32)) + g2_ref[0].astype(F32)) + g3_ref[0].astype(F32)

    if kind == "col":
        own_map = lambda i, p: (i, p[1])
        out_map = lambda i, p: (p[0] * nt + i, 0)
        out_shape = (2 * r, c)
    else:
        own_map = lambda i, p: (p[1] * nt + i, 0)
        out_map = lambda i, p: (i, p[0])
        out_shape = (r, 2 * c)
    peer = lambda m: pl.BlockSpec((1, tr, c), lambda i, p: (p[1] ^ m, i, 0))
    return pl.pallas_call(
        body, name=name,
        grid_spec=pltpu.PrefetchScalarGridSpec(
            num_scalar_prefetch=1, grid=(nt,),
            in_specs=[pl.BlockSpec((tr, c), own_map), peer(1), peer(2), peer(3)],
            out_specs=pl.BlockSpec((tr, c), out_map)),
        out_shape=jax.ShapeDtypeStruct(out_shape, F32),
        compiler_params=_params(("parallel",)),
    )(pos, pair, got, got, got)


def _rs_back(halves, kinds):
    n = len(halves)

    def body(*refs):
        outs = refs[n:2 * n]
        send_sems, recv_sems = refs[2 * n:]
        x, y, c = _mesh_pos()
        cps = []
        for a in range(n):
            mine = _half_of(outs[a], kinds[a], c)
            cps.append(pltpu.make_async_remote_copy(src_ref=mine, dst_ref=mine, send_sem=send_sems.at[a],
                                                    recv_sem=recv_sems.at[a], device_id=(x, y, 1 - c),
                                                    device_id_type=MESH_ID))
            cps[-1].start()
        for a in range(n):
            other = _half_of(outs[a], kinds[a], 1 - c)
            pltpu.make_async_remote_copy(src_ref=other, dst_ref=other, send_sem=send_sems.at[a],
                                         recv_sem=recv_sems.at[a], device_id=(x, y, 1 - c),
                                         device_id_type=MESH_ID).wait_recv()
        for cp in cps:
            cp.wait_send()

    return pl.pallas_call(
        body, name="rs_back",
        out_shape=[jax.ShapeDtypeStruct(h.shape, h.dtype) for h in halves],
        in_specs=[ANY] * n, out_specs=[ANY] * n, input_output_aliases={a: a for a in range(n)},
        scratch_shapes=[pltpu.SemaphoreType.DMA((n,)), pltpu.SemaphoreType.DMA((n,))],
        compiler_params=_params(),
    )(*halves)


def _rope_tables(l, lc):
    rows = l // GRID_W
    row = jnp.repeat(jnp.arange(rows, dtype=F32), GRID_W)
    col = jnp.tile(jnp.arange(GRID_W, dtype=F32), rows)
    n_freq = RET_DH // 4
    inv_freq = ROPE_THETA ** (-jnp.arange(n_freq, dtype=F32) / n_freq)
    ang = jnp.concatenate([row[:, None] * inv_freq, col[:, None] * inv_freq], axis=-1)
    cos_t = jnp.repeat(jnp.cos(ang), 2, axis=-1)
    sin_t = jnp.repeat(jnp.sin(ang), 2, axis=-1) * jnp.tile(jnp.array([-1.0, 1.0], F32), RET_DH // 2)
    cos_t = jnp.concatenate([jnp.ones((lc, RET_DH), F32), cos_t], axis=0)
    sin_t = jnp.concatenate([jnp.zeros((lc, RET_DH), F32), sin_t], axis=0)
    return cos_t, sin_t


def _s5_pack(a):
    blk = lambda t: t.reshape(1, S5_NB, 128, S5_STATE)
    lre = jnp.stack([a["s5_lambda_re_f"][0], a["s5_lambda_re_b"][0]]).reshape(2, S5_NB, 8, S5_STATE)
    lim = jnp.stack([a["s5_lambda_im_f"][0], a["s5_lambda_im_b"][0]]).reshape(2, S5_NB, 8, S5_STATE)
    lst = jnp.stack([a["s5_log_step_f"][0], a["s5_log_step_b"][0]]).reshape(2, S5_NB, 8, 1)
    b_re = blk(a["s5_b_re"][0].transpose(0, 2, 1))
    b_im = blk(a["s5_b_im"][0].transpose(0, 2, 1))
    return (lre, lim, lst, b_re, b_im, blk(a["s5_c_re"][0]), blk(a["s5_c_im"][0]),
            a["s5_d"].reshape(1, S5_NB, 1, 128))


def _s5_unpack(g):
    glre, glim, glst, gbre, gbim, gcre, gcim, gd = g
    unb = lambda t: t.reshape(S5_GROUPS, S5_GROUP, S5_STATE).transpose(0, 2, 1)[None]
    return {
        "s5_lambda_re_f": glre[0].reshape(1, S5_GROUPS, S5_STATE), "s5_lambda_re_b": glre[1].reshape(1, S5_GROUPS, S5_STATE),
        "s5_lambda_im_f": glim[0].reshape(1, S5_GROUPS, S5_STATE), "s5_lambda_im_b": glim[1].reshape(1, S5_GROUPS, S5_STATE),
        "s5_log_step_f": glst[0].reshape(1, S5_GROUPS), "s5_log_step_b": glst[1].reshape(1, S5_GROUPS),
        "s5_b_re": unb(gbre), "s5_b_im": unb(gbim),
        "s5_c_re": gcre.reshape(1, S5_GROUPS, S5_GROUP, S5_STATE), "s5_c_im": gcim.reshape(1, S5_GROUPS, S5_GROUP, S5_STATE),
        "s5_d": gd.reshape(1, S5_WIDTH),
    }


def _local_step(a, wb, late, mx, mc, conv_w, ci):
    x, ctx, tgt = a["x"][0], a["ctx"][0], a["loss_target"][0]
    l, lc = x.shape[0], ctx.shape[0]
    la = l + lc
    nct, ncc, nrc, cn = lc // TOK_TILE, lc // S5_T, lc // RET_CHUNK, la // S5_T
    n1w, n2w, fnw = a["norm1_w"], a["norm2_w"], a["final_norm_w"].reshape(1, D_MODEL)
    conv_b, b_glu = a["conv_b"], a["s5_b_glu"]
    ld2 = jnp.concatenate([a["ret_log_decay_f"], a["ret_log_decay_b"]], axis=0)
    mod4 = jnp.concatenate([mc[0:2], mx[0:2]], axis=0)
    mod3 = mx[2:5]
    gate5 = mx[5:6]
    cos_t, sin_t = _rope_tables(l, lc)
    s5p = _s5_pack(a)

    p_all, h1b, u_b, w_out_p, w_down_p = _norm_inproj(x, ctx, n1w, mod4, wb["w_in"], [late[0], late[2]],
                                                 (LATE_KINDS[0], LATE_KINDS[2]))
    p3 = _s5_rows(u_b)
    gg, xw, yw, a16 = _s5_gen(*s5p)
    sloc = _s5_to_states(p3, xw, "s5_state")
    a16s = a16.transpose(1, 0, 2, 3).reshape(2, S5_GROUPS, 128)
    hs = _s5_scan(sloc, a16s, ncc)
    y_all = _s5_from_states(p3, gg, hs, yw, False, "s5_out").reshape(la, S5_WIDTH)
    s5x = _s5_glu(y_all, wb["s5_w_glu"], b_glu, nct)
    tabs = _ret_tables(ld2)
    of, ob, ssf, ssb, w_up_p = _ret_scan(p_all, cos_t, sin_t, tabs, nrc, [late[1]], (LATE_KINDS[1],))
    wb = {**wb, **dict(zip(LATE_NAMES, _gather_sibling([w_out_p, w_up_p, w_down_p], LATE_KINDS)))}
    retx, y_ret = _ret_gate(of, ob, p_all, nct)
    x1, mix, h2b, up = _outproj_up(x, s5x, retx, wb["w_out"], mod3, n2w, wb["w_up"])
    act, dx2, ddn, dact, acc_f = _ffn_loss(up, x1, conv_w, conv_b, wb["w_down"], gate5, fnw, tgt)

    g = {}
    g["w_down"] = _mm_tn(act, ddn, name="gw_down")
    dup, acc_c = _convglu_bwd(up, dact, conv_w, conv_b)
    g["w_up"] = _mm_tn(h2b, dup, name="gw_up")
    first = [g[n] for n in FIRST_GRADS]
    first_pairs = [_pair_sum(gf, rv, k, ci, "rs_pair_" + n)
                   for gf, rv, k, n in zip(first, _rs_sibling(first, FIRST_KINDS, "rs_sibling_first"), FIRST_KINDS, FIRST_GRADS)]
    dx1, dmixb, dmix, acc_2, *first_got = _up_bwd(dup, wb["w_up"], wb["w_out"], x1, dx2, mix, mod3, n2w,
                                                  first_pairs, FIRST_KINDS)
    g["w_out"] = jnp.concatenate([_mm_tn(s5x, dmixb, name="gw_out_s5"), _mm_tn(retx, dmixb, name="gw_out_ret")], axis=0)

    dy_s5, g["s5_w_glu"], g["s5_b_glu"] = _s5_glu_bwd(y_all, dmix, wb["s5_w_glu"], b_glu, nct)
    dy3 = _s5_rows(dy_s5)
    e = _s5_to_states(dy3, yw, "s5_bwd_h")
    ds, da16 = _s5_scan_bwd(e, hs, a16s, ncc)
    du = _s5_from_states(dy3, gg, ds, xw, True, "s5_bwd_u").reshape(la, S5_WIDTH)
    dkb = _s5_bwd_kb(p3, dy3)
    dwst = _s5_bwd_w(p3, ds, "s5_bwd_wst")
    dwout = _s5_bwd_w(dy3, hs, "s5_bwd_wout")
    da16p = da16.reshape(2, S5_NB, 8, 128).transpose(1, 0, 2, 3)
    g.update(_s5_unpack(_s5_gen_bwd(*s5p, dkb, dwst, dwout, da16p)))

    dy_ret, dg = _ret_gate_bwd(y_ret, p_all, dmix, nct)
    dqf, dkf, dvf, dqb, dkb_, dvb, dld = _ret_scan_bwd(p_all, cos_t, sin_t, tabs, ssf, ssb, dy_ret, nrc)
    g["ret_log_decay_f"] = dld[0, :, 0, 0].reshape(1, RET_HEADS)
    g["ret_log_decay_b"] = dld[1, :, 0, 0].reshape(1, RET_HEADS)
    dp, grad_x, acc_1 = _in_bwd(dqf, dkf, dvf, dqb, dkb_, dvb, du, dg, cos_t, sin_t, wb["w_in"], x, ctx, n1w, mod4, dx1)
    g["w_in"] = _mm_tn(h1b, dp, name="gw_in")

    g["norm1_w"], g["norm2_w"], g["final_norm_w"] = acc_1[0:1], acc_2[0:1], acc_f[0]
    g["conv_w"], g["conv_b"] = acc_c[0:3], acc_c[3:4]
    zero = jnp.zeros((1, D_MODEL), F32)
    dmx = jnp.concatenate([acc_1[3:5], acc_2[1:2], acc_2[2:4], acc_f[1:2]], axis=0)
    dmc = jnp.concatenate([acc_1[1:3], zero, zero, zero, zero], axis=0)
    return acc_f[2, 0], grad_x, g, dmx, dmc, first_pairs, first_got


WEIGHT_NAMES = ("c_ctx", "w_mod", "b_mod", "norm1_w", "w_in", "s5_lambda_re_f", "s5_lambda_im_f", "s5_log_step_f",
                "s5_lambda_re_b", "s5_lambda_im_b", "s5_log_step_b", "s5_b_re", "s5_b_im", "s5_c_re", "s5_c_im",
                "s5_d", "s5_w_glu", "s5_b_glu", "ret_log_decay_f", "ret_log_decay_b", "w_out", "norm2_w", "w_up",
                "conv_w", "conv_b", "w_down", "final_norm_w")
BIG_NAMES = ("w_in", "w_out", "w_up", "w_down", "s5_w_glu")
BIG_KINDS = ("col", "row", "col", "row", "row")
EARLY_NAMES, EARLY_KINDS = ("w_in", "s5_w_glu"), ("col", "row")
LATE_NAMES, LATE_KINDS = ("w_out", "w_up", "w_down"), ("row", "col", "row")
FIRST_GRADS, FIRST_KINDS = ("w_down", "w_up"), ("row", "col")
LAST_GRADS, LAST_KINDS = ("w_in", "w_out", "s5_w_glu"), ("col", "row", "row")
SMALL_NAMES = ("norm1_w", "norm2_w", "final_norm_w", "conv_b", "conv_w", "s5_lambda_re_f", "s5_lambda_im_f",
               "s5_log_step_f", "s5_lambda_re_b", "s5_lambda_im_b", "s5_log_step_b", "s5_b_re", "s5_b_im", "s5_c_re",
               "s5_c_im", "s5_d", "s5_b_glu", "ret_log_decay_f", "ret_log_decay_b")
ROW = 1024
N_CHIPS = 4


def _pack_rows(parts):
    flat = jnp.concatenate([p.reshape(-1) for p in parts])
    n = flat.shape[0]
    rows = -(-n // (8 * ROW)) * 8
    return jnp.pad(flat, (0, rows * ROW - n)).reshape(rows, ROW)


def _unpack_rows(packed, shapes):
    flat = packed.reshape(-1)
    out, off = [], 0
    for s in shapes:
        n = math.prod(s)
        out.append(flat[off:off + n].reshape(s))
        off += n
    return out


def _step(a):
    xi, yi, ci = _mesh_pos()
    chip = 2 * xi + yi
    dev = 2 * chip + ci

    cw_loc = a["conv_w"].reshape(-1)
    small_in = jnp.concatenate([a["c"].reshape(-1), jnp.pad(cw_loc, (0, 24 * 128 - cw_loc.shape[0]))]).reshape(32, 128)
    sg = _all_gather8(small_in, "gather_cond").reshape(8, 32, 128)
    c_all = sg[:, 0:8].reshape(8, D_MODEL)
    conv_w = sg[0::2, 8:32].reshape(N_CHIPS, -1)[:, :cw_loc.shape[0]].reshape(N_CHIPS, 3, -1)
    conv_w = conv_w.transpose(1, 0, 2).reshape(3, D_FF)

    placed = {n: _place_shard(a[n][0], k, chip, "place_" + n) for n, k in zip(BIG_NAMES, BIG_KINDS)}
    wb = dict(zip(EARLY_NAMES, _gather_weights([placed[n] for n in EARLY_NAMES], EARLY_KINDS)))
    late = [placed[n] for n in LATE_NAMES]

    w_mod_b = a["w_mod"][0].astype(BF16)
    c_ctx = a["c_ctx"].reshape(1, D_MODEL)
    b_loc = lax.dynamic_slice_in_dim(a["b_mod"], chip * MOD_COLS, MOD_COLS, 1)
    m_loc, s_b = _mod_fwd(c_all, c_ctx, w_mod_b, b_loc)
    mg = _all_gather8(m_loc, "gather_mod").reshape(8, MOD_ROWS, MOD_COLS)
    m_full = mg[0::2].transpose(1, 0, 2).reshape(MOD_ROWS, 6 * D_MODEL)
    mx = lax.dynamic_slice_in_dim(m_full, dev, 1, 0).reshape(6, D_MODEL)
    mc = m_full[8].reshape(6, D_MODEL)

    loss_part, grad_x, g, dmx, dmc, first_pairs, first_got = _local_step(a, wb, late, mx, mc, conv_w, ci)
    loss = lax.psum(loss_part, ("x", "y", "c"))

    dm_pair = jnp.concatenate([dmx.reshape(1, -1), dmc.reshape(1, -1), jnp.zeros((6, 6 * D_MODEL), F32)], axis=0)
    dm_all = _all_gather8(dm_pair, "gather_dmod").reshape(8, 8, 6 * D_MODEL)
    dm16, gb_mod = _mod_bwd_sum(dm_all)
    dm_loc = lax.dynamic_slice_in_dim(dm16, chip * MOD_COLS, MOD_COLS, 1)
    gw_mod, gcc = _mod_bwd_w(dm_loc, s_b, c_ctx, w_mod_b)

    small_parts = [g[n] for n in SMALL_NAMES] + [gcc[0]]
    small_shapes = [p.shape for p in small_parts]
    sp = _pack_rows(small_parts)
    tot = _sum_slots(_all_gather8(sp, "gather_small_grads").reshape(8, sp.shape[0], ROW), "sum_small_grads")
    small = dict(zip(SMALL_NAMES + ("c_ctx",), _unpack_rows(tot, small_shapes)))
    grads = {n: small[n].reshape(a[n].shape) for n in SMALL_NAMES if n != "conv_w"}
    grads["c_ctx"] = (0.5 * small["c_ctx"]).reshape(a["c_ctx"].shape)
    grads["conv_w"] = lax.dynamic_slice_in_dim(small["conv_w"], chip * (D_FF // N_CHIPS), D_FF // N_CHIPS, 1)[None]
    grads["b_mod"] = gb_mod
    grads["w_mod"] = gw_mod[None]

    last = [g[n] for n in LAST_GRADS]
    last_pairs = [_pair_sum(gf, rv, k, ci, "rs_pair_" + n)
                  for gf, rv, k, n in zip(last, _rs_sibling(last, LAST_KINDS, "rs_sibling_last"), LAST_KINDS, LAST_GRADS)]
    last_got = _rs_chips(last_pairs, LAST_KINDS)
    pos = jnp.stack([ci, chip])
    order = FIRST_GRADS + LAST_GRADS
    order_kinds = FIRST_KINDS + LAST_KINDS
    halves = [_sum_chips(p, t, k, pos, "rs_sum_" + n)
              for p, t, k, n in zip(first_pairs + last_pairs, list(first_got) + list(last_got), order_kinds, order)]
    for n, t in zip(order, _rs_back(halves, order_kinds)):
        grads[n] = t[None]

    delta, new_m, new_v = {}, {}, {}
    for n in BIG_NAMES + ("w_mod",):
        for dst, t in zip((delta, new_m, new_v), _adamw(a[n][0], grads[n][0], a["m_" + n][0], a["v_" + n][0], "adamw_" + n)):
            dst[n] = t[None]
    rest = [n for n in WEIGHT_NAMES if n not in BIG_NAMES and n != "w_mod"]
    shapes = [a[n].shape for n in rest]
    pr = lambda pre: _pack_rows([a[pre + n] for n in rest])
    for dst, t in zip((delta, new_m, new_v),
                      _adamw(pr(""), _pack_rows([grads[n] for n in rest]), pr("m_"), pr("v_"), "adamw_small")):
        dst.update(zip(rest, _unpack_rows(t, shapes)))

    return (loss, grad_x[None], *[grads[n] for n in WEIGHT_NAMES], *[delta[n] for n in WEIGHT_NAMES],
            *[new_m[n] for n in WEIGHT_NAMES], *[new_v[n] for n in WEIGHT_NAMES])


def kernel(x, c, ctx, c_ctx, w_mod, b_mod, norm1_w, w_in, s5_lambda_re_f, s5_lambda_im_f, s5_log_step_f, s5_lambda_re_b, s5_lambda_im_b, s5_log_step_b, s5_b_re, s5_b_im, s5_c_re, s5_c_im, s5_d, s5_w_glu, s5_b_glu, ret_log_decay_f, ret_log_decay_b, w_out, norm2_w, w_up, conv_w, conv_b, w_down, final_norm_w, loss_target, m_c_ctx, m_w_mod, m_b_mod, m_norm1_w, m_w_in, m_s5_lambda_re_f, m_s5_lambda_im_f, m_s5_log_step_f, m_s5_lambda_re_b, m_s5_lambda_im_b, m_s5_log_step_b, m_s5_b_re, m_s5_b_im, m_s5_c_re, m_s5_c_im, m_s5_d, m_s5_w_glu, m_s5_b_glu, m_ret_log_decay_f, m_ret_log_decay_b, m_w_out, m_norm2_w, m_w_up, m_conv_w, m_conv_b, m_w_down, m_final_norm_w, v_c_ctx, v_w_mod, v_b_mod, v_norm1_w, v_w_in, v_s5_lambda_re_f, v_s5_lambda_im_f, v_s5_log_step_f, v_s5_lambda_re_b, v_s5_lambda_im_b, v_s5_log_step_b, v_s5_b_re, v_s5_b_im, v_s5_c_re, v_s5_c_im, v_s5_d, v_s5_w_glu, v_s5_b_glu, v_ret_log_decay_f, v_ret_log_decay_b, v_w_out, v_norm2_w, v_w_up, v_conv_w, v_conv_b, v_w_down, v_final_norm_w):
    return _step(dict(locals()))
```

```python
import functools
import math

import jax
import jax.numpy as jnp
from jax import lax
from jax.experimental import pallas as pl
from jax.experimental.pallas import tpu as pltpu

F32 = jnp.float32
BF16 = jnp.bfloat16

D_MODEL = 1024
S5_WIDTH = 512
S5_GROUPS = 32
S5_GROUP = 16
S5_STATE = 64
RET_WIDTH = 512
RET_HEADS = 4
RET_DH = 128
RET_CHUNK = 256
GRID_W = 64
ROPE_THETA = 10000.0
D_FF = 2816
NORM_EPS = 1e-6
IN_COLS = S5_WIDTH + 4 * RET_WIDTH

S5_T = 16
S5_NB = 4
S5_BW = S5_T * 128
S5_SW = 8 * 2 * S5_STATE

ADAM_LR, ADAM_B1, ADAM_B2, ADAM_EPS, ADAM_WD, ADAM_STEP = 0.001, 0.9, 0.999, 1e-08, 0.01, 10

VMEM_LIMIT = 56 * 1024 * 1024
MM_TN_VMEM = 40 * 1024 * 1024
MESH_ID = pl.DeviceIdType.MESH


def _params(sem=None):
    return pltpu.CompilerParams(dimension_semantics=sem, vmem_limit_bytes=VMEM_LIMIT)


def _full(shape):
    n = len(shape)
    return pl.BlockSpec(shape, lambda *_: (0,) * n)


def _dot(a, b):
    return jnp.dot(a, b, preferred_element_type=F32)


def _dot_nt(a, b):
    return lax.dot_general(a, b, (((1,), (1,)), ((), ())), preferred_element_type=F32)


def _dot_tn(a, b):
    return lax.dot_general(a, b, (((0,), (0,)), ((), ())), preferred_element_type=F32)


def _dot_hi(a, b):
    return jnp.dot(a, b, preferred_element_type=F32, precision=lax.Precision.HIGHEST)


def _dot_nt_hi(a, b):
    return lax.dot_general(a, b, (((1,), (1,)), ((), ())), preferred_element_type=F32,
                           precision=lax.Precision.HIGHEST)


def _gelu(x):
    return 0.5 * x * (1.0 + jnp.tanh(0.7978845608028654 * (x + 0.044715 * (x * x * x))))


def _gelu_and_grad(x):
    c, ca = 0.7978845608028654, 0.7978845608028654 * 0.044715
    x2 = x * x
    t = jnp.tanh(x * (c + ca * x2))
    h = 0.5 * x
    return h + h * t, 0.5 + 0.5 * t + h * (1.0 - t * t) * (c + 3.0 * ca * x2)


def _sigmoid(x):
    return 1.0 / (1.0 + jnp.exp(-x))


def _silu(x):
    return x * _sigmoid(x)


def _rms_mod(x, nw, sh, sc):
    r = lax.rsqrt(jnp.mean(x * x, axis=-1, keepdims=True) + NORM_EPS)
    return (x * r * nw) * (1.0 + sc) + sh


def _rms(x, nw):
    r = lax.rsqrt(jnp.mean(x * x, axis=-1, keepdims=True) + NORM_EPS)
    return x * r * nw


def _head_norm_gate(y, g):
    mu = jnp.mean(y, axis=-1, keepdims=True)
    yc = y - mu
    var = jnp.mean(yc * yc, axis=-1, keepdims=True)
    return _silu(g) * (yc * lax.rsqrt(var + NORM_EPS))


def _swap_pairs(t):
    lane = lax.broadcasted_iota(jnp.int32, t.shape, 1)
    return jnp.where(lane % 2 == 0, pltpu.roll(t, RET_DH - 1, 1), pltpu.roll(t, 1, 1))


def _rope(t, cos_t, sin_t):
    return t * cos_t + _swap_pairs(t) * sin_t


def _rope_t(dt, cos_t, sin_t):
    return dt * cos_t + _swap_pairs(dt * sin_t)


def _pick(n, prefs):
    for p in prefs:
        if n % p == 0:
            return p
    return n


def _mm_tn(a, b, *, name):
    m, k = a.shape
    n = b.shape[1]
    tn = _pick(n, (1408, 1024, 1280, 512))
    fits = lambda t: 2 * (2 * t * k + 2 * t * tn + 4 * k * tn) <= MM_TN_VMEM
    tm = _pick(m, [t for t in (2816, 2048, 1024, 768, 512, 256) if fits(t)] + [128])

    def body(a_ref, b_ref, o_ref):
        @pl.when(pl.program_id(1) == 0)
        def _():
            o_ref[...] = jnp.zeros_like(o_ref)
        o_ref[...] += _dot_tn(a_ref[...], b_ref[...])

    return pl.pallas_call(
        body, name=name, grid=(n // tn, m // tm),
        in_specs=[pl.BlockSpec((tm, k), lambda j, i: (i, 0)), pl.BlockSpec((tm, tn), lambda j, i: (i, j))],
        out_specs=pl.BlockSpec((k, tn), lambda j, i: (0, j)),
        out_shape=jax.ShapeDtypeStruct((k, n), F32),
        compiler_params=_params(("parallel", "arbitrary")),
    )(a, b)


TOK_TILE = 256


def _behind(step, last, copies):
    @pl.when(step == 0)
    def _():
        for cp in copies(with_arrivals=False)[0]:
            cp.start()

    @pl.when(step == last)
    def _():
        sends, arrivals = copies()
        for cp in arrivals:
            cp.wait_recv()
        for cp in sends:
            cp.wait_send()


def _norm_inproj(x, ctx, n1w, mod4, w_in_b, placed, kinds):
    l, lc = x.shape[0], ctx.shape[0]
    tm = TOK_TILE
    nct = lc // tm
    la = l + lc
    n = len(placed)
    shard_shapes = _gather_shard_shapes(placed, kinds)

    def body(x_ref, c_ref, nw_ref, mod_ref, w_ref, *rest):
        p_ref, h_ref, u_ref = rest[n:n + 3]
        _behind(pl.program_id(0), la // tm - 1,
                functools.partial(_gather_chip_copies, rest[n + 3:2 * n + 3], kinds, shard_shapes, *rest[2 * n + 3:]))
        is_ctx = pl.program_id(0) < nct
        xt = jnp.where(is_ctx, c_ref[...], x_ref[...])
        sh = jnp.where(is_ctx, mod_ref[0:1, :], mod_ref[2:3, :])
        sc = jnp.where(is_ctx, mod_ref[1:2, :], mod_ref[3:4, :])
        hb = _rms_mod(xt, nw_ref[...], sh, sc).astype(BF16)
        h_ref[...] = hb
        p = _dot(hb, w_ref[...])
        p_ref[...] = p
        u_ref[...] = p[:, 0:S5_WIDTH].astype(BF16)

    return pl.pallas_call(
        body, name="norm_inproj", grid=(la // tm,),
        in_specs=[pl.BlockSpec((tm, D_MODEL), lambda i: (jnp.maximum(i - nct, 0), 0)),
                  pl.BlockSpec((tm, D_MODEL), lambda i: (jnp.minimum(i, nct - 1), 0)),
                  _full((1, D_MODEL)), _full((4, D_MODEL)), _full((D_MODEL, IN_COLS))] + [ANY] * n,
        out_specs=[pl.BlockSpec((tm, IN_COLS), lambda i: (i, 0)), pl.BlockSpec((tm, D_MODEL), lambda i: (i, 0)),
                   pl.BlockSpec((tm, S5_WIDTH), lambda i: (i, 0))] + [ANY] * n,
        out_shape=[jax.ShapeDtypeStruct((la, IN_COLS), F32), jax.ShapeDtypeStruct((la, D_MODEL), BF16),
                   jax.ShapeDtypeStruct((la, S5_WIDTH), BF16)]
        + [jax.ShapeDtypeStruct(p.shape, p.dtype) for p in placed],
        input_output_aliases={5 + a: 3 + a for a in range(n)},
        scratch_shapes=[pltpu.SemaphoreType.DMA((n, 3)), pltpu.SemaphoreType.DMA((n, 3))],
        compiler_params=_params(("arbitrary",)),
    )(x, ctx, n1w, mod4, w_in_b, *placed)


def _iota2(shape, dim):
    return lax.broadcasted_iota(jnp.int32, shape, dim)


def _group_mask(rows, cols, row_div, col_div):
    return jnp.where(_iota2((rows, cols), 0) // row_div == _iota2((rows, cols), 1) // col_div, 1.0, 0.0).astype(F32)


def _s5_gen_dir(lre, lim, lst, b_re, b_im, c_re, c_im):
    step = jnp.exp(lst)
    mag = jnp.exp(lre * step)
    ar = mag * jnp.cos(lim * step)
    ai = mag * jnp.sin(lim * step)
    den = lre * lre + lim * lim
    xr = ar - 1.0
    cr = (xr * lre + ai * lim) / den
    ci = (ai * lre - xr * lim) / den
    rexp = _group_mask(128, 8, S5_GROUP, 1)
    are, aie = _dot_hi(rexp, ar), _dot_hi(rexp, ai)
    cre, cie = _dot_hi(rexp, cr), _dot_hi(rexp, ci)
    bbr = cre * b_re - cie * b_im
    bbi = cre * b_im + cie * b_re
    gmask = _group_mask(128, 128, S5_GROUP, S5_GROUP)
    pr, pi = jnp.ones_like(are), jnp.zeros_like(are)
    xs, ys = [], []
    for t in range(S5_T + 1):
        if t < S5_T:
            xs.append(jnp.concatenate([bbr * pr - bbi * pi, bbr * pi + bbi * pr], axis=1))
        ys.append(jnp.concatenate([c_re * pr - c_im * pi, -(c_re * pi + c_im * pr)], axis=1))
        pr, pi = pr * are - pi * aie, pr * aie + pi * are
    gs = [_dot_nt_hi(x_t, ys[0]) * gmask for x_t in xs]
    r16, i16 = ar, ai
    for _ in range(4):
        r16, i16 = r16 * r16 - i16 * i16, 2.0 * r16 * i16
    return xs, ys, gs, jnp.concatenate([r16, i16], axis=1)


def _s5_expand(z):
    return jnp.concatenate([z] * 8, axis=1) * _group_mask(128, S5_SW, S5_GROUP, 128)


def _s5_contract(z):
    zm = z * _group_mask(128, S5_SW, S5_GROUP, 128)
    acc = zm[:, 0:128]
    for k in range(1, 8):
        acc = acc + zm[:, 128 * k:128 * (k + 1)]
    return acc


def _s5_param_specs():
    blk3 = lambda r, c: pl.BlockSpec((1, 1, r, c), lambda b, *_: (0, b, 0, 0))
    dir3 = lambda r, c: pl.BlockSpec((2, 1, r, c), lambda b, *_: (0, b, 0, 0))
    return [dir3(8, S5_STATE), dir3(8, S5_STATE), dir3(8, 1), blk3(128, S5_STATE), blk3(128, S5_STATE),
            blk3(128, S5_STATE), blk3(128, S5_STATE), blk3(1, 128)]


def _s5_gen(lre, lim, lst, b_re, b_im, c_re, c_im, dvec):
    def body(lre_ref, lim_ref, lst_ref, bre_ref, bim_ref, cre_ref, cim_ref, d_ref, gg_ref, xw_ref, yw_ref, a16_ref):
        eye = _group_mask(128, 128, 1, 1)
        g0 = eye * d_ref[0, 0]
        for dr in range(2):
            xs, ys, gs, a16 = _s5_gen_dir(lre_ref[dr, 0], lim_ref[dr, 0], lst_ref[dr, 0], bre_ref[0, 0],
                                          bim_ref[0, 0], cre_ref[0, 0], cim_ref[0, 0])
            a16_ref[0, dr] = a16
            for j in range(S5_T):
                xw_ref[0, dr, j] = xs[S5_T - 1 - j if dr == 0 else j]
                yw_ref[0, dr, j] = ys[j + 1 if dr == 0 else S5_T - j]
            g0 = g0 + gs[0]
            for t in range(1, S5_T):
                gg_ref[0, (S5_T - 1) + t if dr == 0 else (S5_T - 1) - t] = gs[t]
        gg_ref[0, S5_T - 1] = g0

    blk = pl.BlockSpec((1, 2, S5_T, 128, 128), lambda b: (b, 0, 0, 0, 0))
    return pl.pallas_call(
        body, name="s5_gen", grid=(S5_NB,),
        in_specs=_s5_param_specs(),
        out_specs=[pl.BlockSpec((1, 2 * S5_T - 1, 128, 128), lambda b: (b, 0, 0, 0)), blk, blk,
                   pl.BlockSpec((1, 2, 8, 128), lambda b: (b, 0, 0, 0))],
        out_shape=[jax.ShapeDtypeStruct((S5_NB, 2 * S5_T - 1, 128, 128), F32),
                   jax.ShapeDtypeStruct((S5_NB, 2, S5_T, 128, 128), F32),
                   jax.ShapeDtypeStruct((S5_NB, 2, S5_T, 128, 128), F32),
                   jax.ShapeDtypeStruct((S5_NB, 2, 8, 128), F32)],
        compiler_params=_params(("parallel",)),
    )(lre, lim, lst, b_re, b_im, c_re, c_im, dvec)


def _s5_fill_state_mat(w_scr, src_ref, dr):
    for j in range(S5_T):
        w_scr[128 * j:128 * (j + 1), :] = _s5_expand(src_ref[0, dr, j]).astype(BF16)


def _s5_fill_toeplitz(k_scr, gg_ref):
    for j in range(S5_T):
        for i in range(S5_T):
            k_scr[128 * j:128 * (j + 1), 128 * i:128 * (i + 1)] = gg_ref[0, i - j + (S5_T - 1)].astype(BF16)


S5_GEN_SPECS = [pl.BlockSpec((1, 2 * S5_T - 1, 128, 128), lambda b: (b, 0, 0, 0)),
                pl.BlockSpec((1, 2, S5_T, 128, 128), lambda b: (b, 0, 0, 0, 0))]


def _s5_gen_bwd(lre, lim, lst, b_re, b_im, c_re, c_im, dvec, dg, dx, dy, da16):
    def body(lre_ref, lim_ref, lst_ref, bre_ref, bim_ref, cre_ref, cim_ref, d_ref, dg_ref, dx_ref, dy_ref, da16_ref,
             glre_ref, glim_ref, glst_ref, gbre_ref, gbim_ref, gcre_ref, gcim_ref, gd_ref):
        eye = _group_mask(128, 128, 1, 1)
        gd_ref[0, 0] = jnp.sum(dg_ref[0, S5_T - 1] * eye, axis=0, keepdims=True)
        gb = [None, None, None, None]
        for dr in range(2):
            args = (lre_ref[dr, 0], lim_ref[dr, 0], lst_ref[dr, 0], bre_ref[0, 0], bim_ref[0, 0],
                    cre_ref[0, 0], cim_ref[0, 0])
            _, vjp = jax.vjp(_s5_gen_dir, *args)
            dxs = [dx_ref[0, dr, S5_T - 1 - t if dr == 0 else t] for t in range(S5_T)]
            dys = [jnp.zeros((128, 128), F32)] + [dy_ref[0, dr, t - 1 if dr == 0 else S5_T - t]
                                                  for t in range(1, S5_T + 1)]
            dgs = [dg_ref[0, (S5_T - 1) + t if dr == 0 else (S5_T - 1) - t] for t in range(S5_T)]
            g = vjp((dxs, dys, dgs, da16_ref[0, dr]))
            glre_ref[dr, 0] = g[0]
            glim_ref[dr, 0] = g[1]
            glst_ref[dr, 0] = g[2]
            for q in range(4):
                gb[q] = g[3 + q] if gb[q] is None else gb[q] + g[3 + q]
        gbre_ref[0, 0] = gb[0]
        gbim_ref[0, 0] = gb[1]
        gcre_ref[0, 0] = gb[2]
        gcim_ref[0, 0] = gb[3]

    shp = lambda a: jax.ShapeDtypeStruct(a.shape, F32)
    return pl.pallas_call(
        body, name="s5_gen_bwd", grid=(S5_NB,),
        in_specs=_s5_param_specs() + [
            pl.BlockSpec((1, 2 * S5_T - 1, 128, 128), lambda b: (b, 0, 0, 0)),
            pl.BlockSpec((1, 2, S5_T, 128, 128), lambda b: (b, 0, 0, 0, 0)),
            pl.BlockSpec((1, 2, S5_T, 128, 128), lambda b: (b, 0, 0, 0, 0)),
            pl.BlockSpec((1, 2, 8, 128), lambda b: (b, 0, 0, 0))],
        out_specs=_s5_param_specs(),
        out_shape=[shp(lre), shp(lim), shp(lst), shp(b_re), shp(b_im), shp(c_re), shp(c_im), shp(dvec)],
        compiler_params=_params(("parallel",)),
    )(lre, lim, lst, b_re, b_im, c_re, c_im, dvec, dg, dx, dy, da16)


def _s5_rows(t):
    cn = t.shape[0] // S5_T
    return t.reshape(cn, S5_T, S5_NB, 128).transpose(2, 0, 1, 3).reshape(S5_NB, cn, S5_BW)


def _s5_put_groups(o_ref, dr, val):
    for gi in range(8):
        o_ref[dr, :, gi, :] = val[:, 128 * gi:128 * (gi + 1)]


def _s5_get_groups(s_ref, dr, n=8):
    return jnp.concatenate([s_ref[dr, :, gi, :] for gi in range(n)], axis=1).astype(BF16)


def _s5_to_states(u3, blocks, name):
    cn = u3.shape[1]

    def body(u_ref, b_ref, o_ref, w_scr):
        u = u_ref[0]
        for dr in range(2):
            _s5_fill_state_mat(w_scr, b_ref, dr)
            _s5_put_groups(o_ref, dr, _dot(u, w_scr[...]))

    return pl.pallas_call(
        body, name=name, grid=(S5_NB,),
        in_specs=[pl.BlockSpec((1, cn, S5_BW), lambda b: (b, 0, 0)), S5_GEN_SPECS[1]],
        out_specs=pl.BlockSpec((2, cn, 8, 128), lambda b: (0, 0, b, 0)),
        out_shape=jax.ShapeDtypeStruct((2, cn, S5_GROUPS, 128), F32),
        scratch_shapes=[pltpu.VMEM((S5_BW, S5_SW), BF16)],
        compiler_params=_params(("parallel",)),
    )(u3, blocks)


def _s5_from_states(u3, gg, st, blocks, transposed, name):
    cn = u3.shape[1]

    def body(u_ref, g_ref, s_ref, b_ref, o_ref, k_scr, w_scr):
        u = u_ref[0]
        _s5_fill_toeplitz(k_scr, g_ref)
        y = _dot_nt(u, k_scr[...]) if transposed else _dot(u, k_scr[...])
        for dr in range(2):
            _s5_fill_state_mat(w_scr, b_ref, dr)
            y = y + _dot_nt(_s5_get_groups(s_ref, dr), w_scr[...])
        for i in range(S5_T):
            o_ref[:, i, :] = y[:, 128 * i:128 * (i + 1)]

    return pl.pallas_call(
        body, name=name, grid=(S5_NB,),
        in_specs=[pl.BlockSpec((1, cn, S5_BW), lambda b: (b, 0, 0)), S5_GEN_SPECS[0],
                  pl.BlockSpec((2, cn, 8, 128), lambda b: (0, 0, b, 0)), S5_GEN_SPECS[1]],
        out_specs=pl.BlockSpec((cn, S5_T, 128), lambda b: (0, 0, b)),
        out_shape=jax.ShapeDtypeStruct((cn, S5_T, S5_WIDTH), F32),
        scratch_shapes=[pltpu.VMEM((S5_BW, S5_BW), BF16), pltpu.VMEM((S5_BW, S5_SW), BF16)],
        compiler_params=_params(("parallel",)),
    )(u3, gg, st, blocks)


def _s5_a_forms(a):
    ra = pltpu.roll(a, S5_STATE, 1)
    low = _iota2(a.shape, 1) < S5_STATE
    return jnp.where(low, a, ra), jnp.where(low, -ra, a)


def _s5_scan(sloc, a16, ncc):
    cn = sloc.shape[1]

    def body(s_ref, a_ref, h_ref):
        forms = [_s5_a_forms(a_ref[dr]) for dr in range(2)]

        def step(s, hs):
            out = []
            for dr in range(2):
                arr, aii = forms[dr]
                h, rh = hs[dr]
                c = s if dr == 0 else jnp.where(s < ncc, ncc - 1 - s, cn - 1 - (s - ncc))
                h_ref[dr, c] = h
                sc = s_ref[dr, c]
                out.append((h * arr + rh * aii + sc, rh * arr - h * aii + pltpu.roll(sc, S5_STATE, 1)))
            return tuple(out)

        zero = jnp.zeros((S5_GROUPS, 128), F32)
        lax.fori_loop(0, cn, step, ((zero, zero), (zero, zero)), unroll=4)

    return pl.pallas_call(
        body, name="s5_scan",
        out_shape=jax.ShapeDtypeStruct(sloc.shape, F32),
        compiler_params=_params(),
    )(sloc, a16)


def _s5_scan_bwd(e, hs, a16, ncc):
    cn = e.shape[1]

    def body(e_ref, h_ref, a_ref, ds_ref, da_ref):
        forms = [_s5_a_forms(a_ref[dr]) for dr in range(2)]
        low = _iota2((S5_GROUPS, 128), 1) < S5_STATE

        def step(s, carry):
            out = []
            r = cn - 1 - s
            for dr in range(2):
                arr, aii = forms[dr]
                g, rg, da = carry[dr]
                c = r if dr == 0 else jnp.where(r < ncc, ncc - 1 - r, cn - 1 - (r - ncc))
                ds_ref[dr, c] = g
                h = h_ref[dr, c]
                rh = pltpu.roll(h, S5_STATE, 1)
                da = da + jnp.where(low, g * h + rg * rh, g * rh - rg * h)
                ec = e_ref[dr, c]
                out.append((ec + g * arr - rg * aii, pltpu.roll(ec, S5_STATE, 1) + rg * arr + g * aii, da))
            return tuple(out)

        zero = jnp.zeros((S5_GROUPS, 128), F32)
        res = lax.fori_loop(0, cn, step, ((zero, zero, zero), (zero, zero, zero)), unroll=4)
        da_ref[0] = res[0][2]
        da_ref[1] = res[1][2]

    return pl.pallas_call(
        body, name="s5_scan_bwd",
        out_shape=[jax.ShapeDtypeStruct(e.shape, F32), jax.ShapeDtypeStruct((2, S5_GROUPS, 128), F32)],
        compiler_params=_params(),
    )(e, hs, a16)


def _s5_bwd_kb(p3, dy3):
    cn = p3.shape[1]
    half = S5_T // 2

    def body(u_ref, d_ref, o_ref):
        q = pl.program_id(1)

        @pl.when(q == 0)
        def _():
            o_ref[...] = jnp.zeros_like(o_ref)

        dk = _dot_tn(u_ref[0], d_ref[0])
        for j in range(S5_T):
            for i in range(half):
                o_ref[0, half * q + i - j + (S5_T - 1)] += dk[128 * j:128 * (j + 1), 128 * i:128 * (i + 1)]

    return pl.pallas_call(
        body, name="s5_bwd_kb", grid=(S5_NB, 2),
        in_specs=[pl.BlockSpec((1, cn, S5_BW), lambda b, q: (b, 0, 0)),
                  pl.BlockSpec((1, cn, S5_BW // 2), lambda b, q: (b, 0, q))],
        out_specs=pl.BlockSpec((1, 2 * S5_T - 1, 128, 128), lambda b, q: (b, 0, 0, 0)),
        out_shape=jax.ShapeDtypeStruct((S5_NB, 2 * S5_T - 1, 128, 128), F32),
        compiler_params=_params(("parallel", "arbitrary")),
    )(p3, dy3)


def _s5_bwd_w(u3, st, name):
    cn = u3.shape[1]

    def body(u_ref, s_ref, w_ref):
        dw = _dot_tn(u_ref[0], _s5_get_groups(s_ref, 0))
        for j in range(S5_T):
            w_ref[0, 0, j] = _s5_contract(dw[128 * j:128 * (j + 1), :])

    return pl.pallas_call(
        body, name=name, grid=(S5_NB, 2),
        in_specs=[pl.BlockSpec((1, cn, S5_BW), lambda b, q: (b, 0, 0)),
                  pl.BlockSpec((1, cn, 8, 128), lambda b, q: (q, 0, b, 0))],
        out_specs=pl.BlockSpec((1, 1, S5_T, 128, 128), lambda b, q: (b, q, 0, 0, 0)),
        out_shape=jax.ShapeDtypeStruct((S5_NB, 2, S5_T, 128, 128), F32),
        compiler_params=_params(("parallel", "parallel")),
    )(u3, st)


def _s5_glu_bwd(y_all, dmix, w_glu_b, b_glu, nct):
    la = y_all.shape[0]
    tm = TOK_TILE

    def body(y_ref, d_ref, w_ref, b_ref, dy_ref, gw_ref, gb_ref):
        i = pl.program_id(0)

        @pl.when(i == 0)
        def _():
            gw_ref[...] = jnp.zeros_like(gw_ref)
            gb_ref[...] = jnp.zeros_like(gb_ref)

        @pl.when(i < nct)
        def _():
            dy_ref[...] = jnp.zeros_like(dy_ref)

        @pl.when(i >= nct)
        def _():
            y = y_ref[...]
            yg, gelu_vjp = jax.vjp(_gelu, y)
            ygb = yg.astype(BF16)
            sg = _sigmoid(_dot(ygb, w_ref[...]) + b_ref[...])
            ds = d_ref[...]
            dz = ds * yg * sg * (1.0 - sg)
            dzb = dz.astype(BF16)
            dyg = ds * sg + _dot_nt(dzb, w_ref[...])
            dy_ref[...] = gelu_vjp(dyg)[0].astype(BF16)
            gw_ref[...] += _dot_tn(ygb, dzb)
            gb_ref[...] += jnp.sum(dz, axis=0, keepdims=True)

    return pl.pallas_call(
        body, name="s5_glu_bwd", grid=(la // tm,),
        in_specs=[pl.BlockSpec((tm, S5_WIDTH), lambda i: (i, 0)),
                  pl.BlockSpec((tm, S5_WIDTH), lambda i: (jnp.maximum(i - nct, 0), 0)),
                  _full((S5_WIDTH, S5_WIDTH)), _full((1, S5_WIDTH))],
        out_specs=[pl.BlockSpec((tm, S5_WIDTH), lambda i: (i, 0)), _full((S5_WIDTH, S5_WIDTH)),
                   _full((1, S5_WIDTH))],
        out_shape=[jax.ShapeDtypeStruct((la, S5_WIDTH), BF16), jax.ShapeDtypeStruct((S5_WIDTH, S5_WIDTH), F32),
                   jax.ShapeDtypeStruct((1, S5_WIDTH), F32)],
        compiler_params=_params(("arbitrary",)),
    )(y_all, dmix, w_glu_b, b_glu)


K_SCALE = RET_DH ** -0.5
Q_COL, K_COL, V_COL, G_COL = 4, 8, 12, 16


def _ret_chunk_of(step, ncc, nch, rev):
    if not rev:
        return step
    return jnp.where(step < ncc, ncc - 1 - step, nch - 1 - (step - ncc))


def _ret_decay(ld, rev):
    c = _iota2((RET_CHUNK, RET_CHUNK), 0).astype(F32)
    m = _iota2((RET_CHUNK, RET_CHUNK), 1).astype(F32)
    diff = (m - c) if rev else (c - m)
    keep = (diff > 0) if rev else (diff >= 0)
    expo = jnp.maximum(diff, 0.0)
    dm = jnp.where(keep, jnp.exp(ld * expo), 0.0)
    xi_e = (RET_CHUNK - c) if rev else (c + 1.0)
    zeta_e = c if rev else (RET_CHUNK - 1.0 - c)
    return dm, expo, jnp.exp(ld * xi_e), xi_e, jnp.exp(ld * zeta_e), zeta_e


RET_TABLES = 7


def _ret_tables(ld2):
    def body(ld_ref, t_ref):
        dr, h = pl.program_id(0), pl.program_id(1)
        ldh = ld_ref[dr, h]
        for rev in (False, True):
            @pl.when(dr == int(rev))
            def _(rev=rev):
                dm, expo, xi, xi_e, zeta, zeta_e = _ret_decay(ldh, rev)
                t_ref[0, 0, 0] = dm
                t_ref[0, 0, 1] = dm * expo
                t_ref[0, 0, 2] = xi
                t_ref[0, 0, 3] = xi * xi_e
                t_ref[0, 0, 4] = zeta
                t_ref[0, 0, 5] = zeta * zeta_e
                t_ref[0, 0, 6] = jnp.zeros_like(dm) + jnp.exp(ldh * RET_CHUNK)

    return pl.pallas_call(
        body, name="ret_tables", grid=(2, RET_HEADS),
        in_specs=[pl.BlockSpec(memory_space=pltpu.SMEM)],
        out_specs=pl.BlockSpec((1, 1, RET_TABLES, RET_CHUNK, RET_CHUNK), lambda d, h: (d, h, 0, 0, 0)),
        out_shape=jax.ShapeDtypeStruct((2, RET_HEADS, RET_TABLES, RET_CHUNK, RET_CHUNK), F32),
        compiler_params=_params(("parallel", "parallel")),
    )(ld2)


def _ret_specs(nch, ncc, rev, step_of):
    chunk = lambda n: _ret_chunk_of(step_of(n), ncc, nch, rev)
    cols = [pl.BlockSpec((RET_CHUNK, RET_WIDTH), functools.partial(lambda n, cb: (chunk(n), cb), cb=cb))
            for cb in (1, 2, 3)]
    tab = pl.BlockSpec((RET_CHUNK, RET_DH), lambda n: (chunk(n), 0))
    return cols + [tab, tab], pl.BlockSpec((RET_CHUNK, RET_WIDTH), lambda n: (chunk(n), 0))


def _ret_scan(p_all, cos_t, sin_t, tabs, ncc, placed, kinds):
    la = p_all.shape[0]
    nch = la // RET_CHUNK
    n = len(placed)
    shard_shapes = _gather_shard_shapes(placed, kinds)

    def body(t_ref, qf, kf, vf, cf, sf, qb, kb, vb, cb, sb, *rest):
        of_ref, ob_ref, ssf_ref, ssb_ref = rest[n:n + 4]
        s_scr, send_sems, recv_sems = rest[2 * n + 4:]
        step = pl.program_id(0)

        @pl.when(step == 0)
        def _():
            s_scr[...] = jnp.zeros_like(s_scr)
            for cp in _gather_chip_copies(rest[n + 4:2 * n + 4], kinds, shard_shapes, send_sems, recv_sems, False)[0]:
                cp.start()

        @pl.when(step == nch - 1)
        def _():
            sends, arrivals = _gather_chip_copies(rest[n + 4:2 * n + 4], kinds, shard_shapes, send_sems, recv_sems)
            for cp in arrivals:
                cp.wait_recv()
            for cp in sends:
                cp.wait_send()

        for dr, (q_ref, k_ref, v_ref, c_ref, n_ref, o_ref, ss_ref) in enumerate(
                ((qf, kf, vf, cf, sf, of_ref, ssf_ref), (qb, kb, vb, cb, sb, ob_ref, ssb_ref))):
            cs, sn = c_ref[...], n_ref[...]
            for h in range(RET_HEADS):
                sl = slice(RET_DH * h, RET_DH * (h + 1))
                dm, xi, zeta = t_ref[dr, h, 0], t_ref[dr, h, 2, :, 0:RET_DH], t_ref[dr, h, 4, :, 0:RET_DH]
                q = _rope(q_ref[:, sl], cs, sn)
                k = _rope(k_ref[:, sl] * K_SCALE, cs, sn)
                vh = v_ref[:, sl].astype(BF16)
                s = s_scr[dr, h]
                ss_ref[0, h] = s
                sc = (_dot_nt(q.astype(BF16), k.astype(BF16)) * dm).astype(BF16)
                o_ref[:, sl] = _dot(sc, vh) + _dot((q * xi).astype(BF16), s.astype(BF16))
                s_scr[dr, h] = t_ref[dr, h, 6, 0:RET_DH, 0:RET_DH] * s + _dot_tn((k * zeta).astype(BF16), vh)

    in_f, out_f = _ret_specs(nch, ncc, False, lambda n: n)
    in_b, out_b = _ret_specs(nch, ncc, True, lambda n: n)
    ss_spec = pl.BlockSpec((1, RET_HEADS, RET_DH, RET_DH), lambda n: (n, 0, 0, 0))
    o_shape = jax.ShapeDtypeStruct((la, RET_WIDTH), F32)
    ss_shape = jax.ShapeDtypeStruct((nch, RET_HEADS, RET_DH, RET_DH), F32)
    return pl.pallas_call(
        body, name="ret_scan", grid=(nch,),
        in_specs=[_full(tabs.shape)] + in_f + in_b + [ANY] * n,
        out_specs=[out_f, out_b, ss_spec, ss_spec] + [ANY] * n,
        out_shape=[o_shape, o_shape, ss_shape, ss_shape] + [jax.ShapeDtypeStruct(p.shape, p.dtype) for p in placed],
        input_output_aliases={11 + a: 4 + a for a in range(n)},
        scratch_shapes=[pltpu.VMEM((2, RET_HEADS, RET_DH, RET_DH), F32),
                        pltpu.SemaphoreType.DMA((n, 3)), pltpu.SemaphoreType.DMA((n, 3))],
        compiler_params=_params(("arbitrary",)),
    )(tabs, p_all, p_all, p_all, cos_t, sin_t, p_all, p_all, p_all, cos_t, sin_t, *placed)


def _ret_scan_bwd(p_all, cos_t, sin_t, tabs, ssf, ssb, dy_all, ncc):
    la = p_all.shape[0]
    nch = la // RET_CHUNK

    def body(t_ref, qf, kf, vf, cf, sf, dof, ssf_ref, qb, kb, vb, cb, sb, dob_, ssb_ref,
             dqf, dkf, dvf, dqb, dkb, dvb, dld_ref, ds_scr):
        @pl.when(pl.program_id(0) == 0)
        def _():
            ds_scr[...] = jnp.zeros_like(ds_scr)
            dld_ref[...] = jnp.zeros_like(dld_ref)

        for dr, (q_ref, k_ref, v_ref, c_ref, n_ref, do_ref, ss_ref, dq_ref, dk_ref, dv_ref) in enumerate(
                ((qf, kf, vf, cf, sf, dof, ssf_ref, dqf, dkf, dvf), (qb, kb, vb, cb, sb, dob_, ssb_ref, dqb, dkb, dvb))):
            cs, sn = c_ref[...], n_ref[...]
            for h in range(RET_HEADS):
                sl = slice(RET_DH * h, RET_DH * (h + 1))
                dm, dm_d = t_ref[dr, h, 0], t_ref[dr, h, 1]
                xi, xi_d, zeta, zeta_d = [t_ref[dr, h, t, :, 0:RET_DH] for t in (2, 3, 4, 5)]
                gc = t_ref[dr, h, 6, 0:RET_DH, 0:RET_DH]
                q = _rope(q_ref[:, sl], cs, sn)
                k = _rope(k_ref[:, sl] * K_SCALE, cs, sn)
                q16, k16, v16 = q.astype(BF16), k.astype(BF16), v_ref[:, sl].astype(BF16)
                s = ss_ref[0, h]
                s16 = s.astype(BF16)
                ds_in = ds_scr[dr, h]
                ds16 = ds_in.astype(BF16)
                do16 = do_ref[:, sl].astype(BF16)
                qk = _dot_nt(q16, k16)
                dsv = _dot_nt(do16, v16)
                dsc = (dsv * dm).astype(BF16)
                sc16 = (qk * dm).astype(BF16)
                dos = _dot_nt(do16, s16)
                vds = _dot_nt(v16, ds16)
                dq_ref[:, sl] = _dot(dsc, k16) + dos * xi
                dk_ref[:, sl] = _dot_tn(dsc, q16) + vds * zeta
                dv_ref[:, sl] = _dot_tn(sc16, do16) + _dot((k * zeta).astype(BF16), ds16)
                ds_scr[dr, h] = _dot_tn((q * xi).astype(BF16), do16) + gc * ds_in
                dld = (jnp.sum(dsv * qk * dm_d) + jnp.sum(q * dos * xi_d + k * vds * zeta_d)
                       + RET_CHUNK * jnp.sum(gc * s * ds_in))
                dld_ref[dr, h] += dld

    back = lambda n: nch - 1 - n
    in_f, out_f = _ret_specs(nch, ncc, False, back)
    in_b, out_b = _ret_specs(nch, ncc, True, back)
    ss_spec = pl.BlockSpec((1, RET_HEADS, RET_DH, RET_DH), lambda n: (nch - 1 - n, 0, 0, 0))
    shp = jax.ShapeDtypeStruct((la, RET_WIDTH), F32)
    return pl.pallas_call(
        body, name="ret_scan_bwd", grid=(nch,),
        in_specs=[_full(tabs.shape)] + in_f + [out_f, ss_spec] + in_b + [out_b, ss_spec],
        out_specs=[out_f, out_f, out_f, out_b, out_b, out_b, _full((2, RET_HEADS, 8, 128))],
        out_shape=[shp] * 6 + [jax.ShapeDtypeStruct((2, RET_HEADS, 8, 128), F32)],
        scratch_shapes=[pltpu.VMEM((2, RET_HEADS, RET_DH, RET_DH), F32)],
        compiler_params=_params(("arbitrary",)),
    )(tabs, p_all, p_all, p_all, cos_t, sin_t, dy_all, ssf, p_all, p_all, p_all, cos_t, sin_t, dy_all, ssb)


def _ret_gate_bwd(y_ret, p_all, dmix, nct):
    la = p_all.shape[0]
    tm = TOK_TILE

    def body(y_ref, g_ref, d_ref, dy_ref, dg_ref):
        i = pl.program_id(0)

        @pl.when(i < nct)
        def _():
            dy_ref[...] = jnp.zeros_like(dy_ref)
            dg_ref[...] = jnp.zeros_like(dg_ref)

        @pl.when(i >= nct)
        def _():
            for h in range(RET_HEADS):
                sl = slice(RET_DH * h, RET_DH * (h + 1))
                _, vjp = jax.vjp(_head_norm_gate, y_ref[:, sl], g_ref[:, sl])
                dy, dg = vjp(d_ref[:, sl])
                dy_ref[:, sl] = dy
                dg_ref[:, sl] = dg

    xrow = lambda cb: pl.BlockSpec((tm, RET_WIDTH), lambda i: (jnp.maximum(i - nct, 0), cb))
    out = pl.BlockSpec((tm, RET_WIDTH), lambda i: (i, 0))
    shp = jax.ShapeDtypeStruct((la, RET_WIDTH), F32)
    return pl.pallas_call(
        body, name="ret_gate_bwd", grid=(la // tm,),
        in_specs=[xrow(0), pl.BlockSpec((tm, RET_WIDTH), lambda i: (i, G_COL // 4)), xrow(1)],
        out_specs=[out, out], out_shape=[shp, shp],
        compiler_params=_params(("parallel",)),
    )(y_ret, p_all, dmix)


def _in_bwd(dqf, dkf, dvf, dqb, dkb, dvb, du, dg, cos_t, sin_t, w_in_b, x, ctx, n1w, mod4, dx1):
    l, lc = x.shape[0], ctx.shape[0]
    la = l + lc
    tm = TOK_TILE
    nct = lc // tm

    def body(dqf_ref, dkf_ref, dvf_ref, dqb_ref, dkb_ref, dvb_ref, du_ref, dg_ref, cos_ref, sin_ref,
             w_ref, x_ref, c_ref, nw_ref, mod_ref, dx1_ref, dp_ref, gx_ref, acc_ref):
        i = pl.program_id(0)
        is_ctx = i < nct

        @pl.when(i == 0)
        def _():
            acc_ref[...] = jnp.zeros_like(acc_ref)

        cs, sn = cos_ref[...], sin_ref[...]
        def piece(k, val):
            cols = slice(S5_WIDTH * k, S5_WIDTH * (k + 1))
            dp_ref[:, cols] = val.astype(BF16)
            return _dot_nt(dp_ref[:, cols], w_ref[:, cols])

        dh1 = piece(0, du_ref[...])
        dh1 = dh1 + piece(3, dvf_ref[...] + dvb_ref[...])
        dh1 = dh1 + piece(4, jnp.where(is_ctx, 0.0, dg_ref[...]))
        for k, (f_ref, b_ref, scale) in ((1, (dqf_ref, dqb_ref, 1.0)), (2, (dkf_ref, dkb_ref, K_SCALE))):
            heads = [_rope_t(f_ref[:, RET_DH * h:RET_DH * (h + 1)] + b_ref[:, RET_DH * h:RET_DH * (h + 1)], cs, sn) * scale
                     for h in range(RET_HEADS)]
            dh1 = dh1 + piece(k, jnp.concatenate(heads, axis=1))
        xt = jnp.where(is_ctx, c_ref[...], x_ref[...])
        sh = jnp.where(is_ctx, mod_ref[0:1, :], mod_ref[2:3, :])
        sc = jnp.where(is_ctx, mod_ref[1:2, :], mod_ref[3:4, :])
        _, vjp = jax.vjp(_rms_mod, xt, nw_ref[...], sh, sc)
        dx, dnw, dsh, dsc = vjp(dh1)
        gx_ref[...] = dx + dx1_ref[...]
        cf = jnp.where(is_ctx, 1.0, 0.0)
        acc_ref[0:1, :] += dnw
        acc_ref[1:2, :] += cf * dsh
        acc_ref[2:3, :] += cf * dsc
        acc_ref[3:4, :] += (1.0 - cf) * dsh
        acc_ref[4:5, :] += (1.0 - cf) * dsc

    row = pl.BlockSpec((tm, RET_WIDTH), lambda i: (i, 0))
    tab = pl.BlockSpec((tm, RET_DH), lambda i: (i, 0))
    xrow = pl.BlockSpec((tm, D_MODEL), lambda i: (jnp.maximum(i - nct, 0), 0))
    return pl.pallas_call(
        body, name="in_bwd", grid=(la // tm,),
        in_specs=[row] * 7 + [pl.BlockSpec((tm, RET_WIDTH), lambda i: (jnp.maximum(i - nct, 0), 0)),
                              tab, tab, _full((D_MODEL, IN_COLS)), xrow,
                              pl.BlockSpec((tm, D_MODEL), lambda i: (jnp.minimum(i, nct - 1), 0)),
                              _full((1, D_MODEL)), _full((4, D_MODEL)), xrow],
        out_specs=[pl.BlockSpec((tm, IN_COLS), lambda i: (i, 0)), xrow, _full((8, D_MODEL))],
        out_shape=[jax.ShapeDtypeStruct((la, IN_COLS), BF16), jax.ShapeDtypeStruct((l, D_MODEL), F32),
                   jax.ShapeDtypeStruct((8, D_MODEL), F32)],
        compiler_params=_params(("arbitrary",)),
    )(dqf, dkf, dvf, dqb, dkb, dvb, du, dg, cos_t, sin_t, w_in_b, x, ctx, n1w, mod4, dx1)


def _outproj_up(x, y_all, of, ob, p_all, w_glu_b, b_glu, w_out_b, mod3, n2w, w_up_b, nct):
    l = x.shape[0]
    tm = TOK_TILE

    def body(x_ref, y_ref, of_ref, ob_ref, g_ref, wg_ref, bg_ref, wo_ref, mod_ref, nw_ref, wu_ref,
             x1_ref, mix_ref, h2_ref, up_ref, s_ref, r_ref, yr_ref):
        yg = _gelu(y_ref[...])
        s5 = (yg * _sigmoid(_dot(yg.astype(BF16), wg_ref[...]) + bg_ref[...])).astype(BF16)
        s_ref[...] = s5
        yr = of_ref[...] + ob_ref[...]
        yr_ref[...] = yr
        for h in range(RET_HEADS):
            sl = slice(RET_DH * h, RET_DH * (h + 1))
            r_ref[:, sl] = _head_norm_gate(yr[:, sl], g_ref[:, sl]).astype(BF16)
        mix = _dot(s5, wo_ref[0:S5_WIDTH, :]) + _dot(r_ref[...], wo_ref[S5_WIDTH:D_MODEL, :])
        mix_ref[...] = mix
        x1 = x_ref[...] + mod_ref[0:1, :] * mix
        x1_ref[...] = x1
        h2 = _rms_mod(x1, nw_ref[...], mod_ref[1:2, :], mod_ref[2:3, :]).astype(BF16)
        h2_ref[...] = h2
        up_ref[...] = _dot(h2, wu_ref[...])

    row = lambda w: pl.BlockSpec((tm, w), lambda i: (i, 0))
    arow = pl.BlockSpec((tm, RET_WIDTH), lambda i: (i + nct, 0))
    return pl.pallas_call(
        body, name="outproj_up", grid=(l // tm,),
        in_specs=[row(D_MODEL), arow, arow, arow, pl.BlockSpec((tm, RET_WIDTH), lambda i: (i + nct, G_COL // 4)),
                  _full((S5_WIDTH, S5_WIDTH)), _full((1, S5_WIDTH)), _full((D_MODEL, D_MODEL)), _full((3, D_MODEL)),
                  _full((1, D_MODEL)), _full((D_MODEL, 2 * D_FF))],
        out_specs=[row(D_MODEL), row(D_MODEL), row(D_MODEL), row(2 * D_FF), row(S5_WIDTH), row(RET_WIDTH),
                   row(RET_WIDTH)],
        out_shape=[jax.ShapeDtypeStruct((l, D_MODEL), F32), jax.ShapeDtypeStruct((l, D_MODEL), F32),
                   jax.ShapeDtypeStruct((l, D_MODEL), BF16), jax.ShapeDtypeStruct((l, 2 * D_FF), F32),
                   jax.ShapeDtypeStruct((l, S5_WIDTH), BF16), jax.ShapeDtypeStruct((l, RET_WIDTH), BF16),
                   jax.ShapeDtypeStruct((l, RET_WIDTH), F32)],
        compiler_params=_params(("parallel",)),
    )(x, y_all, of, ob, p_all, w_glu_b, b_glu, w_out_b, mod3, n2w, w_up_b)


HALO = 8


def _conv_taps(g, prev_row, next_row):
    t = g.shape[0]
    r = _iota2(g.shape, 0)
    gprev = jnp.where(r == 0, prev_row, pltpu.roll(g, 1, 0))
    gnext = jnp.where(r == t - 1, next_row, pltpu.roll(g, t - 1, 0))
    return gprev, gnext


def _ffn_loss(up, x1, conv_w, conv_b, w_down_b, gate, fnw, tgt):
    l = x1.shape[0]
    tm = TOK_TILE
    nt = l // tm
    hb = tm // HALO

    cw = 256

    def body(up_a, up_g, hp_ref, hn_ref, x1_ref, cw_ref, cb_ref, wd_ref, gate_ref, fn_ref, tgt_ref,
             act_ref, dx2_ref, ddn_ref, dact_ref, acc_ref):
        i = pl.program_id(0)

        @pl.when(i == 0)
        def _():
            acc_ref[...] = jnp.zeros_like(acc_ref)

        dn = jnp.zeros((tm, D_MODEL), F32)
        for c in range(D_FF // cw):
            cols = slice(cw * c, cw * (c + 1))
            g = up_g[:, cols]
            prev_row = jnp.where(i == 0, 0.0, hp_ref[HALO - 1:HALO, cols])
            next_row = jnp.where(i == nt - 1, 0.0, hn_ref[0:1, cols])
            gprev, gnext = _conv_taps(g, prev_row, next_row)
            gc = cb_ref[:, cols] + gprev * cw_ref[0:1, cols] + g * cw_ref[1:2, cols] + gnext * cw_ref[2:3, cols]
            act = (_gelu(gc) * up_a[:, cols]).astype(BF16)
            act_ref[:, cols] = act
            dn = dn + _dot(act, wd_ref[cols, :])
        x2 = x1_ref[...] + gate_ref[...] * dn
        y, vjp = jax.vjp(_rms, x2, fn_ref[...])
        err = y - tgt_ref[...]
        dx2, dfn = vjp(err * (1.0 / D_MODEL))
        dx2_ref[...] = dx2
        ddn = (dx2 * gate_ref[...]).astype(BF16)
        ddn_ref[...] = ddn
        for c in range(D_FF // cw):
            cols = slice(cw * c, cw * (c + 1))
            dact_ref[:, cols] = _dot_nt(ddn, wd_ref[cols, :])
        acc_ref[0:1, :] += dfn
        acc_ref[1:2, :] += jnp.sum(dx2 * dn, axis=0, keepdims=True)
        acc_ref[2:3, :] += (0.5 / D_MODEL) * jnp.sum(err * err)

    row = lambda w: pl.BlockSpec((tm, w), lambda i: (i, 0))
    last = l // HALO - 1
    return pl.pallas_call(
        body, name="ffn_loss", grid=(nt,),
        in_specs=[pl.BlockSpec((tm, D_FF), lambda i: (i, 0)), pl.BlockSpec((tm, D_FF), lambda i: (i, 1)),
                  pl.BlockSpec((HALO, D_FF), lambda i: (jnp.maximum(i * hb - 1, 0), 1)),
                  pl.BlockSpec((HALO, D_FF), lambda i: (jnp.minimum((i + 1) * hb, last), 1)),
                  row(D_MODEL), _full((3, D_FF)), _full((1, D_FF)), _full((D_FF, D_MODEL)),
                  _full((1, D_MODEL)), _full((1, D_MODEL)), row(D_MODEL)],
        out_specs=[row(D_FF), row(D_MODEL), row(D_MODEL), row(D_FF), _full((8, D_MODEL))],
        out_shape=[jax.ShapeDtypeStruct((l, D_FF), BF16), jax.ShapeDtypeStruct((l, D_MODEL), F32),
                   jax.ShapeDtypeStruct((l, D_MODEL), BF16), jax.ShapeDtypeStruct((l, D_FF), F32),
                   jax.ShapeDtypeStruct((8, D_MODEL), F32)],
        compiler_params=_params(("arbitrary",)),
    )(up, up, up, up, x1, conv_w, conv_b, w_down_b, gate, fnw, tgt)


def _convglu_bwd(up, dact, conv_w, conv_b):
    l = up.shape[0]
    tm = 128
    nt = l // tm
    hb = tm // HALO
    te = tm + 2 * HALO

    def body(a_ref, ap_ref, an_ref, g_ref, gp_ref, gn_ref, d_ref, dp_ref, dn_ref, cw_ref, cb_ref,
             dup_ref, acc_ref):
        i = pl.program_id(0)

        @pl.when(i == 0)
        def _():
            acc_ref[...] = jnp.zeros_like(acc_ref)

        def ext(p, c, n):
            return jnp.concatenate([jnp.where(i == 0, 0.0, p[...]), c[...], jnp.where(i == nt - 1, 0.0, n[...])], axis=0)

        ae, ge, de = ext(ap_ref, a_ref, an_ref), ext(gp_ref, g_ref, gn_ref), ext(dp_ref, d_ref, dn_ref)
        gprev = pltpu.roll(ge, 1, 0)
        gnext = pltpu.roll(ge, te - 1, 0)
        w0, w1, w2 = cw_ref[0:1, :], cw_ref[1:2, :], cw_ref[2:3, :]
        gce = cb_ref[...] + gprev * w0 + ge * w1 + gnext * w2
        gel, dgel = _gelu_and_grad(gce)
        dae = de * gel
        dgce = de * ae * dgel
        dge = dgce * w1 + pltpu.roll(dgce, te - 1, 0) * w0 + pltpu.roll(dgce, 1, 0) * w2
        mid = slice(HALO, HALO + tm)
        dup_ref[:, 0:D_FF] = dae[mid].astype(BF16)
        dup_ref[:, D_FF:2 * D_FF] = dge[mid].astype(BF16)
        dgc = dgce[mid]
        acc_ref[0:1, :] += jnp.sum(dgc * gprev[mid], axis=0, keepdims=True)
        acc_ref[1:2, :] += jnp.sum(dgc * ge[mid], axis=0, keepdims=True)
        acc_ref[2:3, :] += jnp.sum(dgc * gnext[mid], axis=0, keepdims=True)
        acc_ref[3:4, :] += jnp.sum(dgc, axis=0, keepdims=True)

    last = l // HALO - 1

    def trio(cb):
        return [pl.BlockSpec((tm, D_FF), lambda i: (i, cb)),
                pl.BlockSpec((HALO, D_FF), lambda i: (jnp.maximum(i * hb - 1, 0), cb)),
                pl.BlockSpec((HALO, D_FF), lambda i: (jnp.minimum((i + 1) * hb, last), cb))]

    return pl.pallas_call(
        body, name="convglu_bwd", grid=(nt,),
        in_specs=trio(0) + trio(1) + trio(0) + [_full((3, D_FF)), _full((1, D_FF))],
        out_specs=[pl.BlockSpec((tm, 2 * D_FF), lambda i: (i, 0)), _full((8, D_FF))],
        out_shape=[jax.ShapeDtypeStruct((l, 2 * D_FF), BF16), jax.ShapeDtypeStruct((8, D_FF), F32)],
        compiler_params=_params(("arbitrary",)),
    )(up, up, up, up, up, up, dact, dact, dact, conv_w, conv_b)


def _up_bwd(dup, w_up_b, w_out_b, x1, dx2, mix, mod3, n2w, y_all, y_ret, p_all, w_glu_b, b_glu, nct, pairs, kinds):
    l = x1.shape[0]
    tm = TOK_TILE
    nt = l // tm
    n = len(pairs)
    shapes = _rs_slot_shapes(pairs, kinds)
    n_out = 8

    def body(dup_ref, wu_ref, wo_ref, x1_ref, dx2_ref, mix_ref, mod_ref, nw_ref, y_ref, yr_ref, g_ref, wg_ref, bg_ref,
             *rest):
        dx1_ref, dmixb_ref, acc_ref, dys_ref, dyr_ref, dg_ref, gw_ref, gb_ref = rest[n:n + n_out]
        step = pl.program_id(0)

        @pl.when(step == 0)
        def _():
            acc_ref[...] = jnp.zeros_like(acc_ref)
            gw_ref[...] = jnp.zeros_like(gw_ref)
            gb_ref[...] = jnp.zeros_like(gb_ref)

        _behind(step, nt - 1, functools.partial(_rs_chip_copies, rest[:n], rest[n + n_out:2 * n + n_out], kinds,
                                                shapes, *rest[2 * n + n_out:]))

        dh2 = _dot_nt(dup_ref[...], wu_ref[...])
        _, vjp = jax.vjp(_rms_mod, x1_ref[...], nw_ref[...], mod_ref[1:2, :], mod_ref[2:3, :])
        dx, dnw, dsh, dsc = vjp(dh2)
        dx1 = dx + dx2_ref[...]
        dx1_ref[...] = dx1
        dmixb = (dx1 * mod_ref[0:1, :]).astype(BF16)
        dmixb_ref[...] = dmixb
        dmix = _dot_nt(dmixb, wo_ref[...])
        acc_ref[0:1, :] += dnw
        acc_ref[1:2, :] += jnp.sum(dx1 * mix_ref[...], axis=0, keepdims=True)
        acc_ref[2:3, :] += dsh
        acc_ref[3:4, :] += dsc

        yg, dgel = _gelu_and_grad(y_ref[...])
        ygb = yg.astype(BF16)
        sg = _sigmoid(_dot(ygb, wg_ref[...]) + bg_ref[...])
        ds = dmix[:, 0:S5_WIDTH]
        dz = ds * yg * sg * (1.0 - sg)
        dzb = dz.astype(BF16)
        dys_ref[...] = ((ds * sg + _dot_nt(dzb, wg_ref[...])) * dgel).astype(BF16)
        gw_ref[...] += _dot_tn(ygb, dzb)
        gb_ref[...] += jnp.sum(dz, axis=0, keepdims=True)

        for h in range(RET_HEADS):
            sl = slice(RET_DH * h, RET_DH * (h + 1))
            _, hvjp = jax.vjp(_head_norm_gate, yr_ref[:, sl], g_ref[:, sl])
            dyr, dg = hvjp(dmix[:, S5_WIDTH + RET_DH * h:S5_WIDTH + RET_DH * (h + 1)])
            dyr_ref[:, sl] = dyr
            dg_ref[:, sl] = dg

    row = pl.BlockSpec((tm, D_MODEL), lambda i: (i, 0))
    half = pl.BlockSpec((tm, S5_WIDTH), lambda i: (i, 0))
    f32h = jax.ShapeDtypeStruct((l, RET_WIDTH), F32)
    return pl.pallas_call(
        body, name="up_bwd", grid=(nt,),
        in_specs=[pl.BlockSpec((tm, 2 * D_FF), lambda i: (i, 0)), _full((D_MODEL, 2 * D_FF)),
                  _full((D_MODEL, D_MODEL)), row, row, row, _full((3, D_MODEL)), _full((1, D_MODEL)),
                  pl.BlockSpec((tm, S5_WIDTH), lambda i: (i + nct, 0)), half,
                  pl.BlockSpec((tm, RET_WIDTH), lambda i: (i + nct, G_COL // 4)),
                  _full((S5_WIDTH, S5_WIDTH)), _full((1, S5_WIDTH))] + [ANY] * n,
        out_specs=[row, row, _full((8, D_MODEL)), half, half, half, _full((S5_WIDTH, S5_WIDTH)),
                   _full((1, S5_WIDTH))] + [ANY] * n,
        out_shape=[jax.ShapeDtypeStruct((l, D_MODEL), F32), jax.ShapeDtypeStruct((l, D_MODEL), BF16),
                   jax.ShapeDtypeStruct((8, D_MODEL), F32), jax.ShapeDtypeStruct((l, S5_WIDTH), BF16), f32h, f32h,
                   jax.ShapeDtypeStruct((S5_WIDTH, S5_WIDTH), F32), jax.ShapeDtypeStruct((1, S5_WIDTH), F32)]
        + [jax.ShapeDtypeStruct((4,) + s, p.dtype) for s, p in zip(shapes, pairs)],
        scratch_shapes=[pltpu.SemaphoreType.DMA((n, 3)), pltpu.SemaphoreType.DMA((n, 3))],
        compiler_params=_params(("arbitrary",)),
    )(dup, w_up_b, w_out_b, x1, dx2, mix, mod3, n2w, y_all, y_ret, p_all, w_glu_b, b_glu, *pairs)


MOD_ROWS = 16
MOD_COLS = 6 * D_MODEL // 4


def _mod_fwd(c_all, c_ctx, w_mod_b, b_loc):
    def body(c_ref, cc_ref, w_ref, b_ref, m_ref, s_ref):
        cond = jnp.concatenate([c_ref[...], jnp.broadcast_to(cc_ref[...], (8, D_MODEL))], axis=0)
        s = _silu(cond).astype(BF16)
        s_ref[...] = s
        m_ref[...] = _dot(s, w_ref[...]) + b_ref[...]

    return pl.pallas_call(
        body, name="mod_fwd",
        out_shape=[jax.ShapeDtypeStruct((MOD_ROWS, MOD_COLS), F32), jax.ShapeDtypeStruct((MOD_ROWS, D_MODEL), BF16)],
        compiler_params=_params(),
    )(c_all, c_ctx, w_mod_b, b_loc)


def _mod_bwd_sum(dm_all):
    def body(d_ref, dm_ref, gb_ref):
        rows = [d_ref[k, 0:1, :] for k in range(8)]
        ctx_sum = d_ref[0, 1:2, :]
        for k in range(1, 8):
            ctx_sum = ctx_sum + d_ref[k, 1:2, :]
        gb = ctx_sum
        for k in range(8):
            gb = gb + rows[k]
        gb_ref[...] = gb
        dm_ref[...] = jnp.concatenate(rows + [ctx_sum] + [jnp.zeros((7, 6 * D_MODEL), F32)], axis=0)

    return pl.pallas_call(
        body, name="mod_bwd_sum",
        out_shape=[jax.ShapeDtypeStruct((MOD_ROWS, 6 * D_MODEL), F32), jax.ShapeDtypeStruct((1, 6 * D_MODEL), F32)],
        compiler_params=_params(),
    )(dm_all)


def _mod_bwd_w(dm_loc, s_b, c_ctx, w_mod_b):
    def body(d_ref, s_ref, cc_ref, w_ref, gw_ref, gc_ref):
        db = d_ref[...].astype(BF16)
        gw_ref[...] = _dot_tn(s_ref[...], db)
        ds = _dot_nt(db, w_ref[...])
        _, vjp = jax.vjp(_silu, cc_ref[...])
        gc_ref[...] = jnp.broadcast_to(vjp(ds[8:9, :])[0], (8, D_MODEL))

    return pl.pallas_call(
        body, name="mod_bwd_w",
        out_shape=[jax.ShapeDtypeStruct((D_MODEL, MOD_COLS), F32), jax.ShapeDtypeStruct((8, D_MODEL), F32)],
        compiler_params=_params(),
    )(dm_loc, s_b, c_ctx, w_mod_b)


def _adamw(w, g, m, v, name):
    r, c = w.shape
    tr = _pick(r, (256, 128, 64, 32, 16, 8))
    bc1 = 1.0 - ADAM_B1 ** ADAM_STEP
    bc2 = 1.0 - ADAM_B2 ** ADAM_STEP

    def body(w_ref, g_ref, m_ref, v_ref, d_ref, nm_ref, nv_ref):
        gg = g_ref[...]
        nm = ADAM_B1 * m_ref[...] + (1.0 - ADAM_B1) * gg
        nv = ADAM_B2 * v_ref[...] + (1.0 - ADAM_B2) * (gg * gg)
        nm_ref[...] = nm
        nv_ref[...] = nv
        d_ref[...] = -ADAM_LR * ((nm / bc1) / (jnp.sqrt(nv / bc2) + ADAM_EPS) + ADAM_WD * w_ref[...])

    blk = pl.BlockSpec((tr, c), lambda i: (i, 0))
    shp = jax.ShapeDtypeStruct((r, c), F32)
    return pl.pallas_call(
        body, name=name, grid=(r // tr,), in_specs=[blk] * 4, out_specs=[blk] * 3, out_shape=[shp] * 3,
        compiler_params=_params(("parallel",)),
    )(w, g, m, v)


def _sum_slots(a, name):
    n, r, c = a.shape
    tr = _pick(r, (376, 256, 208, 128, 64, 32, 16, 8))

    def body(a_ref, o_ref):
        acc = a_ref[0].astype(F32)
        for k in range(1, n):
            acc = acc + a_ref[k].astype(F32)
        o_ref[...] = acc

    return pl.pallas_call(
        body, name=name, grid=(r // tr,),
        in_specs=[pl.BlockSpec((n, tr, c), lambda i: (0, i, 0))],
        out_specs=pl.BlockSpec((tr, c), lambda i: (i, 0)),
        out_shape=jax.ShapeDtypeStruct((r, c), F32),
        compiler_params=_params(("parallel",)),
    )(a)


def _mesh_pos():
    return lax.axis_index("x"), lax.axis_index("y"), lax.axis_index("c")


def _all_gather8(v, name):
    m_per, n = v.shape

    def body(x_ref, out_ref, send_sems, recv_sems, local_sem):
        x, y, c = _mesh_pos()
        me, sibling = (x, y, c), (x, y, 1 - c)
        chips = [(1 - x, y), (x, 1 - y), (1 - x, 1 - y)]

        def rows(px, py, pc):
            return out_ref.at[pl.ds((4 * px + 2 * py + pc) * m_per, m_per), :]

        def copy(k, block, to, src=None):
            return pltpu.make_async_remote_copy(
                src_ref=rows(*block) if src is None else src, dst_ref=rows(*block),
                send_sem=send_sems.at[k], recv_sem=recv_sems.at[k], device_id=to, device_id_type=MESH_ID)

        mine = pltpu.make_async_copy(x_ref, rows(*me), local_sem)
        mine.start()
        first = [copy(0, me, sibling, src=x_ref)]
        first += [copy(1 + j, me, (*chip, c), src=x_ref) for j, chip in enumerate(chips)]
        for cp in first:
            cp.start()
        passed = [copy(4 + j, (*chip, c), sibling) for j, chip in enumerate(chips)]
        for j, chip in enumerate(chips):
            copy(1 + j, (*chip, c), me).wait_recv()
            passed[j].start()
        copy(0, sibling, me).wait_recv()
        for j, chip in enumerate(chips):
            copy(4 + j, (*chip, 1 - c), me).wait_recv()
        for cp in first + passed:
            cp.wait_send()
        mine.wait()

    return pl.pallas_call(
        body, name=name,
        out_shape=jax.ShapeDtypeStruct((8 * m_per, n), v.dtype),
        in_specs=[pl.BlockSpec(memory_space=pltpu.VMEM)],
        out_specs=pl.BlockSpec(memory_space=pltpu.VMEM),
        scratch_shapes=[pltpu.SemaphoreType.DMA((7,)), pltpu.SemaphoreType.DMA((7,)), pltpu.SemaphoreType.DMA],
        compiler_params=_params(),
    )(v)


ANY = pl.BlockSpec(memory_space=pl.ANY)
PEER_CHIPS = lambda x, y: [(x, 1 - y), (1 - x, y), (1 - x, 1 - y)]


def _shard_region(ref, kind, k, rl, cl, r0, nr, c0, nc):
    if kind == "col":
        return ref.at[pl.ds(r0, nr), pl.ds(k * cl + c0, nc)]
    return ref.at[pl.ds(k * rl + r0, nr), pl.ds(c0, nc)]


def _place_shard(w, kind, chip, name):
    rl, cl = w.shape
    tr = _pick(rl, (256, 128, 64))
    nt = rl // tr

    def body(chip_ref, w_ref, o_ref):
        o_ref[...] = w_ref[...].astype(BF16)

    o_map = (lambda i, chip_ref: (i, chip_ref[0])) if kind == "col" else (lambda i, chip_ref: (chip_ref[0] * nt + i, 0))
    return pl.pallas_call(
        body, name=name,
        grid_spec=pltpu.PrefetchScalarGridSpec(
            num_scalar_prefetch=1, grid=(nt,),
            in_specs=[pl.BlockSpec((tr, cl), lambda i, chip_ref: (i, 0))], out_specs=pl.BlockSpec((tr, cl), o_map)),
        out_shape=jax.ShapeDtypeStruct((rl, 4 * cl) if kind == "col" else (4 * rl, cl), BF16),
        compiler_params=_params(("parallel",)),
    )(chip.reshape(1), w)


def _gather_shard_shapes(placed, kinds):
    return [(p.shape[0], p.shape[1] // 4) if k == "col" else (p.shape[0] // 4, p.shape[1]) for p, k in zip(placed, kinds)]


def _gather_chip_copies(outs, kinds, shard_shapes, send_sems, recv_sems, with_arrivals=True):
    x, y, c = _mesh_pos()
    me = 2 * x + y
    sends, arrivals = [], []
    for a in range(len(outs)):
        rl, cl = shard_shapes[a]
        rh = rl // 2
        reg = functools.partial(_shard_region, outs[a], kinds[a], rl=rl, cl=cl, r0=c * rh, nr=rh, c0=0, nc=cl)
        for j, (px, py) in enumerate(PEER_CHIPS(x, y)):
            to = dict(send_sem=send_sems.at[a, j], recv_sem=recv_sems.at[a, j], device_id=(px, py, c),
                      device_id_type=MESH_ID)
            sends.append(pltpu.make_async_remote_copy(src_ref=reg(k=me), dst_ref=reg(k=me), **to))
            if with_arrivals:
                got = reg(k=2 * px + py)
                arrivals.append(pltpu.make_async_remote_copy(src_ref=got, dst_ref=got, **to))
    return sends, arrivals


def _gather_sibling_copies(outs, kinds, shard_shapes, send_sems, recv_sems):
    x, y, c = _mesh_pos()
    forwards, arrivals = [], []
    for a in range(len(outs)):
        rl, cl = shard_shapes[a]
        rh = rl // 2
        for j, (px, py) in enumerate(PEER_CHIPS(x, y)):
            to = dict(send_sem=send_sems.at[a, j], recv_sem=recv_sems.at[a, j], device_id=(x, y, 1 - c),
                      device_id_type=MESH_ID)
            reg = functools.partial(_shard_region, outs[a], kinds[a], k=2 * px + py, rl=rl, cl=cl, nr=rh, c0=0, nc=cl)
            forwards.append(pltpu.make_async_remote_copy(src_ref=reg(r0=c * rh), dst_ref=reg(r0=c * rh), **to))
            arrivals.append(pltpu.make_async_remote_copy(src_ref=reg(r0=(1 - c) * rh), dst_ref=reg(r0=(1 - c) * rh), **to))
    return forwards, arrivals


def _gather_weights(placed, kinds):
    n = len(placed)
    shard_shapes = _gather_shard_shapes(placed, kinds)

    def body(*refs):
        outs = refs[n:2 * n]
        ici_send, ici_recv, sib_send, sib_recv = refs[2 * n:]
        sends, arrivals = _gather_chip_copies(outs, kinds, shard_shapes, ici_send, ici_recv)
        for cp in sends:
            cp.start()
        forwards, from_sibling = _gather_sibling_copies(outs, kinds, shard_shapes, sib_send, sib_recv)
        for cp, fwd in zip(arrivals, forwards):
            cp.wait_recv()
            fwd.start()
        for cp in from_sibling:
            cp.wait_recv()
        for cp in sends + forwards:
            cp.wait_send()

    return pl.pallas_call(
        body, name="gather_weights",
        out_shape=[jax.ShapeDtypeStruct(p.shape, p.dtype) for p in placed],
        in_specs=[ANY] * n, out_specs=[ANY] * n, input_output_aliases={a: a for a in range(n)},
        scratch_shapes=[pltpu.SemaphoreType.DMA((n, 3))] * 4,
        compiler_params=_params(),
    )(*placed)


def _gather_sibling(placed, kinds):
    n = len(placed)
    shard_shapes = _gather_shard_shapes(placed, kinds)

    def body(*refs):
        forwards, from_sibling = _gather_sibling_copies(refs[n:2 * n], kinds, shard_shapes, *refs[2 * n:])
        for cp in forwards:
            cp.start()
        for cp in from_sibling:
            cp.wait_recv()
        for cp in forwards:
            cp.wait_send()

    return pl.pallas_call(
        body, name="gather_sibling",
        out_shape=[jax.ShapeDtypeStruct(p.shape, p.dtype) for p in placed],
        in_specs=[ANY] * n, out_specs=[ANY] * n, input_output_aliases={a: a for a in range(n)},
        scratch_shapes=[pltpu.SemaphoreType.DMA((n, 3))] * 2,
        compiler_params=_params(),
    )(*placed)


def _half(kind, r, c):
    return (r // 2, c) if kind == "col" else (r, c // 2)


def _half_of(ref, kind, which):
    r, c = ref.shape
    hr, hc = _half(kind, r, c)
    return ref.at[pl.ds(which * hr, hr), :] if kind == "col" else ref.at[:, pl.ds(which * hc, hc)]


def _rs_sibling(grads, kinds, name):
    n = len(grads)

    def body(*refs):
        srcs, dsts = refs[:n], refs[n:2 * n]
        send_sems, recv_sems = refs[2 * n:]
        x, y, c = _mesh_pos()
        cps = [pltpu.make_async_remote_copy(src_ref=_half_of(srcs[a], kinds[a], 1 - c), dst_ref=dsts[a],
                                            send_sem=send_sems.at[a], recv_sem=recv_sems.at[a],
                                            device_id=(x, y, 1 - c), device_id_type=MESH_ID) for a in range(n)]
        for cp in cps:
            cp.start()
        for cp in cps:
            cp.wait()

    return pl.pallas_call(
        body, name=name,
        out_shape=[jax.ShapeDtypeStruct(_half(k, *g.shape), g.dtype) for g, k in zip(grads, kinds)],
        in_specs=[ANY] * n, out_specs=[ANY] * n,
        scratch_shapes=[pltpu.SemaphoreType.DMA((n,)), pltpu.SemaphoreType.DMA((n,))],
        compiler_params=_params(),
    )(*grads)


def _pair_sum(gf, rv, kind, ci, name):
    r, c = rv.shape
    tr = _pick(r, (128, 64, 32, 16, 8))
    nt = r // tr

    def body(ci_ref, g_ref, r_ref, o_ref):
        o_ref[...] = (g_ref[...] + r_ref[...]).astype(BF16)

    g_map = (lambda i, ci_ref: (ci_ref[0] * nt + i, 0)) if kind == "col" else (lambda i, ci_ref: (i, ci_ref[0]))
    blk = pl.BlockSpec((tr, c), lambda i, ci_ref: (i, 0))
    return pl.pallas_call(
        body, name=name,
        grid_spec=pltpu.PrefetchScalarGridSpec(num_scalar_prefetch=1, grid=(nt,),
                                               in_specs=[pl.BlockSpec((tr, c), g_map), blk], out_specs=blk),
        out_shape=jax.ShapeDtypeStruct((r, c), BF16),
        compiler_params=_params(("parallel",)),
    )(ci.reshape(1), gf, rv)


def _rs_slot_shapes(pairs, kinds):
    return [(p.shape[0], p.shape[1] // 4) if k == "col" else (p.shape[0] // 4, p.shape[1]) for p, k in zip(pairs, kinds)]


def _rs_chip_copies(srcs, dsts, kinds, shapes, send_sems, recv_sems, with_arrivals=True):
    x, y, c = _mesh_pos()
    me = 2 * x + y
    sends, arrivals = [], []
    for a in range(len(srcs)):
        rl, cl = shapes[a]
        reg = functools.partial(_shard_region, srcs[a], kinds[a], rl=rl, cl=cl, r0=0, nr=rl, c0=0, nc=cl)
        for j, (px, py) in enumerate(PEER_CHIPS(x, y)):
            to = dict(send_sem=send_sems.at[a, j], recv_sem=recv_sems.at[a, j], device_id=(px, py, c),
                      device_id_type=MESH_ID)
            sends.append(pltpu.make_async_remote_copy(src_ref=reg(k=2 * px + py), dst_ref=dsts[a].at[me], **to))
            if with_arrivals:
                slot = dsts[a].at[2 * px + py]
                arrivals.append(pltpu.make_async_remote_copy(src_ref=slot, dst_ref=slot, **to))
    return sends, arrivals


def _rs_chips(pairs, kinds):
    n = len(pairs)
    shapes = _rs_slot_shapes(pairs, kinds)

    def body(*refs):
        sends, arrivals = _rs_chip_copies(refs[:n], refs[n:2 * n], kinds, shapes, *refs[2 * n:])
        for cp in sends:
            cp.start()
        for cp in arrivals:
            cp.wait_recv()
        for cp in sends:
            cp.wait_send()

    return pl.pallas_call(
        body, name="rs_chips",
        out_shape=[jax.ShapeDtypeStruct((4,) + s, p.dtype) for s, p in zip(shapes, pairs)],
        in_specs=[ANY] * n, out_specs=[ANY] * n,
        scratch_shapes=[pltpu.SemaphoreType.DMA((n, 3)), pltpu.SemaphoreType.DMA((n, 3))],
        compiler_params=_params(),
    )(*pairs)


def _sum_chips(pair, got, kind, pos, name):
    _, r, c = got.shape
    tr = _pick(r, (256, 128, 64, 32, 16))
    nt = r // tr

    def body(pos_ref, own_ref, g1_ref, g2_ref, g3_ref, o_ref):
        o_ref[...] = ((own_ref[...].astype(F32) + g1_ref[0].astype(F32)) + g2_ref[0].astype(F32)) + g3_ref[0].astype(F32)

    if kind == "col":
        own_map = lambda i, p: (i, p[1])
        out_map = lambda i, p: (p[0] * nt + i, 0)
        out_shape = (2 * r, c)
    else:
        own_map = lambda i, p: (p[1] * nt + i, 0)
        out_map = lambda i, p: (i, p[0])
        out_shape = (r, 2 * c)
    peer = lambda m: pl.BlockSpec((1, tr, c), lambda i, p: (p[1] ^ m, i, 0))
    return pl.pallas_call(
        body, name=name,
        grid_spec=pltpu.PrefetchScalarGridSpec(
            num_scalar_prefetch=1, grid=(nt,),
            in_specs=[pl.BlockSpec((tr, c), own_map), peer(1), peer(2), peer(3)],
            out_specs=pl.BlockSpec((tr, c), out_map)),
        out_shape=jax.ShapeDtypeStruct(out_shape, F32),
        compiler_params=_params(("parallel",)),
    )(pos, pair, got, got, got)


def _rs_back(halves, kinds):
    n = len(halves)

    def body(*refs):
        outs = refs[n:2 * n]
        send_sems, recv_sems = refs[2 * n:]
        x, y, c = _mesh_pos()
        cps = []
        for a in range(n):
            mine = _half_of(outs[a], kinds[a], c)
            cps.append(pltpu.make_async_remote_copy(src_ref=mine, dst_ref=mine, send_sem=send_sems.at[a],
                                                    recv_sem=recv_sems.at[a], device_id=(x, y, 1 - c),
                                                    device_id_type=MESH_ID))
            cps[-1].start()
        for a in range(n):
            other = _half_of(outs[a], kinds[a], 1 - c)
            pltpu.make_async_remote_copy(src_ref=other, dst_ref=other, send_sem=send_sems.at[a],
                                         recv_sem=recv_sems.at[a], device_id=(x, y, 1 - c),
                                         device_id_type=MESH_ID).wait_recv()
        for cp in cps:
            cp.wait_send()

    return pl.pallas_call(
        body, name="rs_back",
        out_shape=[jax.ShapeDtypeStruct(h.shape, h.dtype) for h in halves],
        in_specs=[ANY] * n, out_specs=[ANY] * n, input_output_aliases={a: a for a in range(n)},
        scratch_shapes=[pltpu.SemaphoreType.DMA((n,)), pltpu.SemaphoreType.DMA((n,))],
        compiler_params=_params(),
    )(*halves)


def _rope_tables(l, lc):
    rows = l // GRID_W
    row = jnp.repeat(jnp.arange(rows, dtype=F32), GRID_W)
    col = jnp.tile(jnp.arange(GRID_W, dtype=F32), rows)
    n_freq = RET_DH // 4
    inv_freq = ROPE_THETA ** (-jnp.arange(n_freq, dtype=F32) / n_freq)
    ang = jnp.concatenate([row[:, None] * inv_freq, col[:, None] * inv_freq], axis=-1)
    cos_t = jnp.repeat(jnp.cos(ang), 2, axis=-1)
    sin_t = jnp.repeat(jnp.sin(ang), 2, axis=-1) * jnp.tile(jnp.array([-1.0, 1.0], F32), RET_DH // 2)
    cos_t = jnp.concatenate([jnp.ones((lc, RET_DH), F32), cos_t], axis=0)
    sin_t = jnp.concatenate([jnp.zeros((lc, RET_DH), F32), sin_t], axis=0)
    return cos_t, sin_t


def _s5_pack(a):
    blk = lambda t: t.reshape(1, S5_NB, 128, S5_STATE)
    lre = jnp.stack([a["s5_lambda_re_f"][0], a["s5_lambda_re_b"][0]]).reshape(2, S5_NB, 8, S5_STATE)
    lim = jnp.stack([a["s5_lambda_im_f"][0], a["s5_lambda_im_b"][0]]).reshape(2, S5_NB, 8, S5_STATE)
    lst = jnp.stack([a["s5_log_step_f"][0], a["s5_log_step_b"][0]]).reshape(2, S5_NB, 8, 1)
    b_re = blk(a["s5_b_re"][0].transpose(0, 2, 1))
    b_im = blk(a["s5_b_im"][0].transpose(0, 2, 1))
    return (lre, lim, lst, b_re, b_im, blk(a["s5_c_re"][0]), blk(a["s5_c_im"][0]),
            a["s5_d"].reshape(1, S5_NB, 1, 128))


def _s5_unpack(g):
    glre, glim, glst, gbre, gbim, gcre, gcim, gd = g
    unb = lambda t: t.reshape(S5_GROUPS, S5_GROUP, S5_STATE).transpose(0, 2, 1)[None]
    return {
        "s5_lambda_re_f": glre[0].reshape(1, S5_GROUPS, S5_STATE), "s5_lambda_re_b": glre[1].reshape(1, S5_GROUPS, S5_STATE),
        "s5_lambda_im_f": glim[0].reshape(1, S5_GROUPS, S5_STATE), "s5_lambda_im_b": glim[1].reshape(1, S5_GROUPS, S5_STATE),
        "s5_log_step_f": glst[0].reshape(1, S5_GROUPS), "s5_log_step_b": glst[1].reshape(1, S5_GROUPS),
        "s5_b_re": unb(gbre), "s5_b_im": unb(gbim),
        "s5_c_re": gcre.reshape(1, S5_GROUPS, S5_GROUP, S5_STATE), "s5_c_im": gcim.reshape(1, S5_GROUPS, S5_GROUP, S5_STATE),
        "s5_d": gd.reshape(1, S5_WIDTH),
    }


def _local_step(a, wb, late, mx, mc, conv_w, ci):
    x, ctx, tgt = a["x"][0], a["ctx"][0], a["loss_target"][0]
    l, lc = x.shape[0], ctx.shape[0]
    la = l + lc
    nct, ncc, nrc, cn = lc // TOK_TILE, lc // S5_T, lc // RET_CHUNK, la // S5_T
    n1w, n2w, fnw = a["norm1_w"], a["norm2_w"], a["final_norm_w"].reshape(1, D_MODEL)
    conv_b, b_glu = a["conv_b"], a["s5_b_glu"]
    ld2 = jnp.concatenate([a["ret_log_decay_f"], a["ret_log_decay_b"]], axis=0)
    mod4 = jnp.concatenate([mc[0:2], mx[0:2]], axis=0)
    mod3 = mx[2:5]
    gate5 = mx[5:6]
    cos_t, sin_t = _rope_tables(l, lc)
    s5p = _s5_pack(a)

    p_all, h1b, u_b, w_out_p, w_down_p = _norm_inproj(x, ctx, n1w, mod4, wb["w_in"], [late[0], late[2]],
                                                 (LATE_KINDS[0], LATE_KINDS[2]))
    p3 = _s5_rows(u_b)
    gg, xw, yw, a16 = _s5_gen(*s5p)
    sloc = _s5_to_states(p3, xw, "s5_state")
    a16s = a16.transpose(1, 0, 2, 3).reshape(2, S5_GROUPS, 128)
    hs = _s5_scan(sloc, a16s, ncc)
    y_all = _s5_from_states(p3, gg, hs, yw, False, "s5_out").reshape(la, S5_WIDTH)
    tabs = _ret_tables(ld2)
    of, ob, ssf, ssb, w_up_p = _ret_scan(p_all, cos_t, sin_t, tabs, nrc, [late[1]], (LATE_KINDS[1],))
    wb = {**wb, **dict(zip(LATE_NAMES, _gather_sibling([w_out_p, w_up_p, w_down_p], LATE_KINDS)))}
    x1, mix, h2b, up, s5x, retx, y_ret = _outproj_up(x, y_all, of, ob, p_all, wb["s5_w_glu"], b_glu, wb["w_out"],
                                                     mod3, n2w, wb["w_up"], nct)
    act, dx2, ddn, dact, acc_f = _ffn_loss(up, x1, conv_w, conv_b, wb["w_down"], gate5, fnw, tgt)

    g = {}
    g["w_down"] = _mm_tn(act, ddn, name="gw_down")
    dup, acc_c = _convglu_bwd(up, dact, conv_w, conv_b)
    g["w_up"] = _mm_tn(h2b, dup, name="gw_up")
    first = [g[n] for n in FIRST_GRADS]
    first_pairs = [_pair_sum(gf, rv, k, ci, "rs_pair_" + n)
                   for gf, rv, k, n in zip(first, _rs_sibling(first, FIRST_KINDS, "rs_sibling_first"), FIRST_KINDS, FIRST_GRADS)]
    dx1, dmixb, acc_2, dy_s5, dy_ret, dg, g["s5_w_glu"], g["s5_b_glu"], *first_got = _up_bwd(
        dup, wb["w_up"], wb["w_out"], x1, dx2, mix, mod3, n2w, y_all, y_ret, p_all, wb["s5_w_glu"], b_glu, nct,
        first_pairs, FIRST_KINDS)
    dy_s5 = jnp.concatenate([jnp.zeros((lc, S5_WIDTH), BF16), dy_s5], axis=0)
    dy_ret = jnp.concatenate([jnp.zeros((lc, RET_WIDTH), F32), dy_ret], axis=0)
    g["w_out"] = jnp.concatenate([_mm_tn(s5x, dmixb, name="gw_out_s5"), _mm_tn(retx, dmixb, name="gw_out_ret")], axis=0)

    dy3 = _s5_rows(dy_s5)
    e = _s5_to_states(dy3, yw, "s5_bwd_h")
    ds, da16 = _s5_scan_bwd(e, hs, a16s, ncc)
    du = _s5_from_states(dy3, gg, ds, xw, True, "s5_bwd_u").reshape(la, S5_WIDTH)
    dkb = _s5_bwd_kb(p3, dy3)
    dwst = _s5_bwd_w(p3, ds, "s5_bwd_wst")
    dwout = _s5_bwd_w(dy3, hs, "s5_bwd_wout")
    da16p = da16.reshape(2, S5_NB, 8, 128).transpose(1, 0, 2, 3)
    g.update(_s5_unpack(_s5_gen_bwd(*s5p, dkb, dwst, dwout, da16p)))

    dqf, dkf, dvf, dqb, dkb_, dvb, dld = _ret_scan_bwd(p_all, cos_t, sin_t, tabs, ssf, ssb, dy_ret, nrc)
    g["ret_log_decay_f"] = dld[0, :, 0, 0].reshape(1, RET_HEADS)
    g["ret_log_decay_b"] = dld[1, :, 0, 0].reshape(1, RET_HEADS)
    dp, grad_x, acc_1 = _in_bwd(dqf, dkf, dvf, dqb, dkb_, dvb, du, dg, cos_t, sin_t, wb["w_in"], x, ctx, n1w, mod4, dx1)
    g["w_in"] = _mm_tn(h1b, dp, name="gw_in")

    g["norm1_w"], g["norm2_w"], g["final_norm_w"] = acc_1[0:1], acc_2[0:1], acc_f[0]
    g["conv_w"], g["conv_b"] = acc_c[0:3], acc_c[3:4]
    zero = jnp.zeros((1, D_MODEL), F32)
    dmx = jnp.concatenate([acc_1[3:5], acc_2[1:2], acc_2[2:4], acc_f[1:2]], axis=0)
    dmc = jnp.concatenate([acc_1[1:3], zero, zero, zero, zero], axis=0)
    return acc_f[2, 0], grad_x, g, dmx, dmc, first_pairs, first_got


WEIGHT_NAMES = ("c_ctx", "w_mod", "b_mod", "norm1_w", "w_in", "s5_lambda_re_f", "s5_lambda_im_f", "s5_log_step_f",
                "s5_lambda_re_b", "s5_lambda_im_b", "s5_log_step_b", "s5_b_re", "s5_b_im", "s5_c_re", "s5_c_im",
                "s5_d", "s5_w_glu", "s5_b_glu", "ret_log_decay_f", "ret_log_decay_b", "w_out", "norm2_w", "w_up",
                "conv_w", "conv_b", "w_down", "final_norm_w")
BIG_NAMES = ("w_in", "w_out", "w_up", "w_down", "s5_w_glu")
BIG_KINDS = ("col", "row", "col", "row", "row")
EARLY_NAMES, EARLY_KINDS = ("w_in", "s5_w_glu"), ("col", "row")
LATE_NAMES, LATE_KINDS = ("w_out", "w_up", "w_down"), ("row", "col", "row")
FIRST_GRADS, FIRST_KINDS = ("w_down", "w_up"), ("row", "col")
LAST_GRADS, LAST_KINDS = ("w_in", "w_out", "s5_w_glu"), ("col", "row", "row")
SMALL_NAMES = ("norm1_w", "norm2_w", "final_norm_w", "conv_b", "conv_w", "s5_lambda_re_f", "s5_lambda_im_f",
               "s5_log_step_f", "s5_lambda_re_b", "s5_lambda_im_b", "s5_log_step_b", "s5_b_re", "s5_b_im", "s5_c_re",
               "s5_c_im", "s5_d", "s5_b_glu", "ret_log_decay_f", "ret_log_decay_b")
ROW = 1024
N_CHIPS = 4


def _pack_rows(parts):
    flat = jnp.concatenate([p.reshape(-1) for p in parts])
    n = flat.shape[0]
    rows = -(-n // (8 * ROW)) * 8
    return jnp.pad(flat, (0, rows * ROW - n)).reshape(rows, ROW)


def _unpack_rows(packed, shapes):
    flat = packed.reshape(-1)
    out, off = [], 0
    for s in shapes:
        n = math.prod(s)
        out.append(flat[off:off + n].reshape(s))
        off += n
    return out


def _step(a):
    xi, yi, ci = _mesh_pos()
    chip = 2 * xi + yi
    dev = 2 * chip + ci

    cw_loc = a["conv_w"].reshape(-1)
    small_in = jnp.concatenate([a["c"].reshape(-1), jnp.pad(cw_loc, (0, 24 * 128 - cw_loc.shape[0]))]).reshape(32, 128)
    sg = _all_gather8(small_in, "gather_cond").reshape(8, 32, 128)
    c_all = sg[:, 0:8].reshape(8, D_MODEL)
    conv_w = sg[0::2, 8:32].reshape(N_CHIPS, -1)[:, :cw_loc.shape[0]].reshape(N_CHIPS, 3, -1)
    conv_w = conv_w.transpose(1, 0, 2).reshape(3, D_FF)

    placed = {n: _place_shard(a[n][0], k, chip, "place_" + n) for n, k in zip(BIG_NAMES, BIG_KINDS)}
    wb = dict(zip(EARLY_NAMES, _gather_weights([placed[n] for n in EARLY_NAMES], EARLY_KINDS)))
    late = [placed[n] for n in LATE_NAMES]

    w_mod_b = a["w_mod"][0].astype(BF16)
    c_ctx = a["c_ctx"].reshape(1, D_MODEL)
    b_loc = lax.dynamic_slice_in_dim(a["b_mod"], chip * MOD_COLS, MOD_COLS, 1)
    m_loc, s_b = _mod_fwd(c_all, c_ctx, w_mod_b, b_loc)
    mg = _all_gather8(m_loc, "gather_mod").reshape(8, MOD_ROWS, MOD_COLS)
    m_full = mg[0::2].transpose(1, 0, 2).reshape(MOD_ROWS, 6 * D_MODEL)
    mx = lax.dynamic_slice_in_dim(m_full, dev, 1, 0).reshape(6, D_MODEL)
    mc = m_full[8].reshape(6, D_MODEL)

    loss_part, grad_x, g, dmx, dmc, first_pairs, first_got = _local_step(a, wb, late, mx, mc, conv_w, ci)
    loss = lax.psum(loss_part, ("x", "y", "c"))

    dm_pair = jnp.concatenate([dmx.reshape(1, -1), dmc.reshape(1, -1), jnp.zeros((6, 6 * D_MODEL), F32)], axis=0)
    dm_all = _all_gather8(dm_pair, "gather_dmod").reshape(8, 8, 6 * D_MODEL)
    dm16, gb_mod = _mod_bwd_sum(dm_all)
    dm_loc = lax.dynamic_slice_in_dim(dm16, chip * MOD_COLS, MOD_COLS, 1)
    gw_mod, gcc = _mod_bwd_w(dm_loc, s_b, c_ctx, w_mod_b)

    small_parts = [g[n] for n in SMALL_NAMES] + [gcc[0]]
    small_shapes = [p.shape for p in small_parts]
    sp = _pack_rows(small_parts)
    tot = _sum_slots(_all_gather8(sp, "gather_small_grads").reshape(8, sp.shape[0], ROW), "sum_small_grads")
    small = dict(zip(SMALL_NAMES + ("c_ctx",), _unpack_rows(tot, small_shapes)))
    grads = {n: small[n].reshape(a[n].shape) for n in SMALL_NAMES if n != "conv_w"}
    grads["c_ctx"] = (0.5 * small["c_ctx"]).reshape(a["c_ctx"].shape)
    grads["conv_w"] = lax.dynamic_slice_in_dim(small["conv_w"], chip * (D_FF // N_CHIPS), D_FF // N_CHIPS, 1)[None]
    grads["b_mod"] = gb_mod
    grads["w_mod"] = gw_mod[None]

    last = [g[n] for n in LAST_GRADS]
    last_pairs = [_pair_sum(gf, rv, k, ci, "rs_pair_" + n)
                  for gf, rv, k, n in zip(last, _rs_sibling(last, LAST_KINDS, "rs_sibling_last"), LAST_KINDS, LAST_GRADS)]
    last_got = _rs_chips(last_pairs, LAST_KINDS)
    pos = jnp.stack([ci, chip])
    order = FIRST_GRADS + LAST_GRADS
    order_kinds = FIRST_KINDS + LAST_KINDS
    halves = [_sum_chips(p, t, k, pos, "rs_sum_" + n)
              for p, t, k, n in zip(first_pairs + last_pairs, list(first_got) + list(last_got), order_kinds, order)]
    for n, t in zip(order, _rs_back(halves, order_kinds)):
        grads[n] = t[None]

    delta, new_m, new_v = {}, {}, {}
    for n in BIG_NAMES + ("w_mod",):
        for dst, t in zip((delta, new_m, new_v), _adamw(a[n][0], grads[n][0], a["m_" + n][0], a["v_" + n][0], "adamw_" + n)):
            dst[n] = t[None]
    rest = [n for n in WEIGHT_NAMES if n not in BIG_NAMES and n != "w_mod"]
    shapes = [a[n].shape for n in rest]
    pr = lambda pre: _pack_rows([a[pre + n] for n in rest])
    for dst, t in zip((delta, new_m, new_v),
                      _adamw(pr(""), _pack_rows([grads[n] for n in rest]), pr("m_"), pr("v_"), "adamw_small")):
        dst.update(zip(rest, _unpack_rows(t, shapes)))

    return (loss, grad_x[None], *[grads[n] for n in WEIGHT_NAMES], *[delta[n] for n in WEIGHT_NAMES],
            *[new_m[n] for n in WEIGHT_NAMES], *[new_v[n] for n in WEIGHT_NAMES])


def kernel(x, c, ctx, c_ctx, w_mod, b_mod, norm1_w, w_in, s5_lambda_re_f, s5_lambda_im_f, s5_log_step_f, s5_lambda_re_b, s5_lambda_im_b, s5_log_step_b, s5_b_re, s5_b_im, s5_c_re, s5_c_im, s5_d, s5_w_glu, s5_b_glu, ret_log_decay_f, ret_log_decay_b, w_out, norm2_w, w_up, conv_w, conv_b, w_down, final_norm_w, loss_target, m_c_ctx, m_w_mod, m_b_mod, m_norm1_w, m_w_in, m_s5_lambda_re_f, m_s5_lambda_im_f, m_s5_log_step_f, m_s5_lambda_re_b, m_s5_lambda_im_b, m_s5_log_step_b, m_s5_b_re, m_s5_b_im, m_s5_c_re, m_s5_c_im, m_s5_d, m_s5_w_glu, m_s5_b_glu, m_ret_log_decay_f, m_ret_log_decay_b, m_w_out, m_norm2_w, m_w_up, m_conv_w, m_conv_b, m_w_down, m_final_norm_w, v_c_ctx, v_w_mod, v_b_mod, v_norm1_w, v_w_in, v_s5_lambda_re_f, v_s5_lambda_im_f, v_s5_log_step_f, v_s5_lambda_re_b, v_s5_lambda_im_b, v_s5_log_step_b, v_s5_b_re, v_s5_b_im, v_s5_c_re, v_s5_c_im, v_s5_d, v_s5_w_glu, v_s5_b_glu, v_ret_log_decay_f, v_ret_log_decay_b, v_w_out, v_norm2_w, v_w_up, v_conv_w, v_conv_b, v_w_down, v_final_norm_w):
    return _step(dict(locals()))
```

```python
import functools
import math

import jax
import jax.numpy as jnp
from jax import lax
from jax.experimental import pallas as pl
from jax.experimental.pallas import tpu as pltpu

F32 = jnp.float32
BF16 = jnp.bfloat16

D_MODEL = 1024
S5_WIDTH = 512
S5_GROUPS = 32
S5_GROUP = 16
S5_STATE = 64
RET_WIDTH = 512
RET_HEADS = 4
RET_DH = 128
RET_CHUNK = 256
GRID_W = 64
ROPE_THETA = 10000.0
D_FF = 2816
NORM_EPS = 1e-6
IN_COLS = S5_WIDTH + 4 * RET_WIDTH

S5_T = 16
S5_NB = 4
S5_BW = S5_T * 128
S5_SW = 8 * 2 * S5_STATE

ADAM_LR, ADAM_B1, ADAM_B2, ADAM_EPS, ADAM_WD, ADAM_STEP = 0.001, 0.9, 0.999, 1e-08, 0.01, 10

VMEM_LIMIT = 56 * 1024 * 1024
MM_TN_VMEM = 40 * 1024 * 1024
MESH_ID = pl.DeviceIdType.MESH


def _params(sem=None):
    return pltpu.CompilerParams(dimension_semantics=sem, vmem_limit_bytes=VMEM_LIMIT)


def _full(shape):
    n = len(shape)
    return pl.BlockSpec(shape, lambda *_: (0,) * n)


def _dot(a, b):
    return jnp.dot(a, b, preferred_element_type=F32)


def _dot_nt(a, b):
    return lax.dot_general(a, b, (((1,), (1,)), ((), ())), preferred_element_type=F32)


def _dot_tn(a, b):
    return lax.dot_general(a, b, (((0,), (0,)), ((), ())), preferred_element_type=F32)


def _dot_hi(a, b):
    return jnp.dot(a, b, preferred_element_type=F32, precision=lax.Precision.HIGHEST)


def _dot_nt_hi(a, b):
    return lax.dot_general(a, b, (((1,), (1,)), ((), ())), preferred_element_type=F32,
                           precision=lax.Precision.HIGHEST)


def _gelu(x):
    return 0.5 * x * (1.0 + jnp.tanh(0.7978845608028654 * (x + 0.044715 * (x * x * x))))


def _gelu_and_grad(x):
    c, ca = 0.7978845608028654, 0.7978845608028654 * 0.044715
    x2 = x * x
    t = jnp.tanh(x * (c + ca * x2))
    h = 0.5 * x
    return h + h * t, 0.5 + 0.5 * t + h * (1.0 - t * t) * (c + 3.0 * ca * x2)


def _sigmoid(x):
    return 1.0 / (1.0 + jnp.exp(-x))


def _silu(x):
    return x * _sigmoid(x)


def _rms_mod(x, nw, sh, sc):
    r = lax.rsqrt(jnp.mean(x * x, axis=-1, keepdims=True) + NORM_EPS)
    return (x * r * nw) * (1.0 + sc) + sh


def _rms(x, nw):
    r = lax.rsqrt(jnp.mean(x * x, axis=-1, keepdims=True) + NORM_EPS)
    return x * r * nw


def _head_norm_gate(y, g):
    mu = jnp.mean(y, axis=-1, keepdims=True)
    yc = y - mu
    var = jnp.mean(yc * yc, axis=-1, keepdims=True)
    return _silu(g) * (yc * lax.rsqrt(var + NORM_EPS))


def _swap_pairs(t):
    lane = lax.broadcasted_iota(jnp.int32, t.shape, 1)
    return jnp.where(lane % 2 == 0, pltpu.roll(t, RET_DH - 1, 1), pltpu.roll(t, 1, 1))


def _rope(t, cos_t, sin_t):
    return t * cos_t + _swap_pairs(t) * sin_t


def _rope_t(dt, cos_t, sin_t):
    return dt * cos_t + _swap_pairs(dt * sin_t)


def _pick(n, prefs):
    for p in prefs:
        if n % p == 0:
            return p
    return n


def _mm_tn(a, b, *, name):
    m, k = a.shape
    n = b.shape[1]
    tn = _pick(n, (1408, 1024, 1280, 512))
    fits = lambda t: 2 * (2 * t * k + 2 * t * tn + 4 * k * tn) <= MM_TN_VMEM
    tm = _pick(m, [t for t in (2816, 2048, 1024, 768, 512, 256) if fits(t)] + [128])

    def body(a_ref, b_ref, o_ref):
        @pl.when(pl.program_id(1) == 0)
        def _():
            o_ref[...] = jnp.zeros_like(o_ref)
        o_ref[...] += _dot_tn(a_ref[...], b_ref[...])

    return pl.pallas_call(
        body, name=name, grid=(n // tn, m // tm),
        in_specs=[pl.BlockSpec((tm, k), lambda j, i: (i, 0)), pl.BlockSpec((tm, tn), lambda j, i: (i, j))],
        out_specs=pl.BlockSpec((k, tn), lambda j, i: (0, j)),
        out_shape=jax.ShapeDtypeStruct((k, n), F32),
        compiler_params=_params(("parallel", "arbitrary")),
    )(a, b)


TOK_TILE = 256


def _behind(step, last, copies):
    @pl.when(step == 0)
    def _():
        for cp in copies(with_arrivals=False)[0]:
            cp.start()

    @pl.when(step == last)
    def _():
        sends, arrivals = copies()
        for cp in arrivals:
            cp.wait_recv()
        for cp in sends:
            cp.wait_send()


def _norm_inproj(x, ctx, n1w, mod4, w_in_b, placed, kinds):
    l, lc = x.shape[0], ctx.shape[0]
    tm = TOK_TILE
    nct = lc // tm
    la = l + lc
    n = len(placed)
    shard_shapes = _gather_shard_shapes(placed, kinds)

    def body(x_ref, c_ref, nw_ref, mod_ref, w_ref, *rest):
        p_ref, h_ref, u_ref = rest[n:n + 3]
        _behind(pl.program_id(0), la // tm - 1,
                functools.partial(_gather_chip_copies, rest[n + 3:2 * n + 3], kinds, shard_shapes, *rest[2 * n + 3:]))
        is_ctx = pl.program_id(0) < nct
        xt = jnp.where(is_ctx, c_ref[...], x_ref[...])
        sh = jnp.where(is_ctx, mod_ref[0:1, :], mod_ref[2:3, :])
        sc = jnp.where(is_ctx, mod_ref[1:2, :], mod_ref[3:4, :])
        hb = _rms_mod(xt, nw_ref[...], sh, sc).astype(BF16)
        h_ref[...] = hb
        p = _dot(hb, w_ref[...])
        p_ref[...] = p
        u_ref[...] = p[:, 0:S5_WIDTH].astype(BF16)

    return pl.pallas_call(
        body, name="norm_inproj", grid=(la // tm,),
        in_specs=[pl.BlockSpec((tm, D_MODEL), lambda i: (jnp.maximum(i - nct, 0), 0)),
                  pl.BlockSpec((tm, D_MODEL), lambda i: (jnp.minimum(i, nct - 1), 0)),
                  _full((1, D_MODEL)), _full((4, D_MODEL)), _full((D_MODEL, IN_COLS))] + [ANY] * n,
        out_specs=[pl.BlockSpec((tm, IN_COLS), lambda i: (i, 0)), pl.BlockSpec((tm, D_MODEL), lambda i: (i, 0)),
                   pl.BlockSpec((tm, S5_WIDTH), lambda i: (i, 0))] + [ANY] * n,
        out_shape=[jax.ShapeDtypeStruct((la, IN_COLS), F32), jax.ShapeDtypeStruct((la, D_MODEL), BF16),
                   jax.ShapeDtypeStruct((la, S5_WIDTH), BF16)]
        + [jax.ShapeDtypeStruct(p.shape, p.dtype) for p in placed],
        input_output_aliases={5 + a: 3 + a for a in range(n)},
        scratch_shapes=[pltpu.SemaphoreType.DMA((n, 3)), pltpu.SemaphoreType.DMA((n, 3))],
        compiler_params=_params(("arbitrary",)),
    )(x, ctx, n1w, mod4, w_in_b, *placed)


def _iota2(shape, dim):
    return lax.broadcasted_iota(jnp.int32, shape, dim)


def _group_mask(rows, cols, row_div, col_div):
    return jnp.where(_iota2((rows, cols), 0) // row_div == _iota2((rows, cols), 1) // col_div, 1.0, 0.0).astype(F32)


def _s5_gen_dir(lre, lim, lst, b_re, b_im, c_re, c_im):
    step = jnp.exp(lst)
    mag = jnp.exp(lre * step)
    ar = mag * jnp.cos(lim * step)
    ai = mag * jnp.sin(lim * step)
    den = lre * lre + lim * lim
    xr = ar - 1.0
    cr = (xr * lre + ai * lim) / den
    ci = (ai * lre - xr * lim) / den
    rexp = _group_mask(128, 8, S5_GROUP, 1)
    are, aie = _dot_hi(rexp, ar), _dot_hi(rexp, ai)
    cre, cie = _dot_hi(rexp, cr), _dot_hi(rexp, ci)
    bbr = cre * b_re - cie * b_im
    bbi = cre * b_im + cie * b_re
    gmask = _group_mask(128, 128, S5_GROUP, S5_GROUP)
    pr, pi = jnp.ones_like(are), jnp.zeros_like(are)
    xs, ys = [], []
    for t in range(S5_T + 1):
        if t < S5_T:
            xs.append(jnp.concatenate([bbr * pr - bbi * pi, bbr * pi + bbi * pr], axis=1))
        ys.append(jnp.concatenate([c_re * pr - c_im * pi, -(c_re * pi + c_im * pr)], axis=1))
        pr, pi = pr * are - pi * aie, pr * aie + pi * are
    gs = [_dot_nt_hi(x_t, ys[0]) * gmask for x_t in xs]
    r16, i16 = ar, ai
    for _ in range(4):
        r16, i16 = r16 * r16 - i16 * i16, 2.0 * r16 * i16
    return xs, ys, gs, jnp.concatenate([r16, i16], axis=1)


def _s5_expand(z):
    return jnp.concatenate([z] * 8, axis=1) * _group_mask(128, S5_SW, S5_GROUP, 128)


def _s5_contract(z):
    zm = z * _group_mask(128, S5_SW, S5_GROUP, 128)
    acc = zm[:, 0:128]
    for k in range(1, 8):
        acc = acc + zm[:, 128 * k:128 * (k + 1)]
    return acc


def _s5_param_specs():
    blk3 = lambda r, c: pl.BlockSpec((1, 1, r, c), lambda b, *_: (0, b, 0, 0))
    dir3 = lambda r, c: pl.BlockSpec((2, 1, r, c), lambda b, *_: (0, b, 0, 0))
    return [dir3(8, S5_STATE), dir3(8, S5_STATE), dir3(8, 1), blk3(128, S5_STATE), blk3(128, S5_STATE),
            blk3(128, S5_STATE), blk3(128, S5_STATE), blk3(1, 128)]


def _s5_gen(lre, lim, lst, b_re, b_im, c_re, c_im, dvec):
    def body(lre_ref, lim_ref, lst_ref, bre_ref, bim_ref, cre_ref, cim_ref, d_ref, gg_ref, xw_ref, yw_ref, a16_ref):
        eye = _group_mask(128, 128, 1, 1)
        g0 = eye * d_ref[0, 0]
        for dr in range(2):
            xs, ys, gs, a16 = _s5_gen_dir(lre_ref[dr, 0], lim_ref[dr, 0], lst_ref[dr, 0], bre_ref[0, 0],
                                          bim_ref[0, 0], cre_ref[0, 0], cim_ref[0, 0])
            a16_ref[0, dr] = a16
            for j in range(S5_T):
                xw_ref[0, dr, j] = xs[S5_T - 1 - j if dr == 0 else j]
                yw_ref[0, dr, j] = ys[j + 1 if dr == 0 else S5_T - j]
            g0 = g0 + gs[0]
            for t in range(1, S5_T):
                gg_ref[0, (S5_T - 1) + t if dr == 0 else (S5_T - 1) - t] = gs[t]
        gg_ref[0, S5_T - 1] = g0

    blk = pl.BlockSpec((1, 2, S5_T, 128, 128), lambda b: (b, 0, 0, 0, 0))
    return pl.pallas_call(
        body, name="s5_gen", grid=(S5_NB,),
        in_specs=_s5_param_specs(),
        out_specs=[pl.BlockSpec((1, 2 * S5_T - 1, 128, 128), lambda b: (b, 0, 0, 0)), blk, blk,
                   pl.BlockSpec((1, 2, 8, 128), lambda b: (b, 0, 0, 0))],
        out_shape=[jax.ShapeDtypeStruct((S5_NB, 2 * S5_T - 1, 128, 128), F32),
                   jax.ShapeDtypeStruct((S5_NB, 2, S5_T, 128, 128), F32),
                   jax.ShapeDtypeStruct((S5_NB, 2, S5_T, 128, 128), F32),
                   jax.ShapeDtypeStruct((S5_NB, 2, 8, 128), F32)],
        compiler_params=_params(("parallel",)),
    )(lre, lim, lst, b_re, b_im, c_re, c_im, dvec)


def _s5_fill_state_mat(w_scr, src_ref, dr):
    for j in range(S5_T):
        w_scr[128 * j:128 * (j + 1), :] = _s5_expand(src_ref[0, dr, j]).astype(BF16)


def _s5_fill_toeplitz(k_scr, gg_ref):
    for j in range(S5_T):
        for i in range(S5_T):
            k_scr[128 * j:128 * (j + 1), 128 * i:128 * (i + 1)] = gg_ref[0, i - j + (S5_T - 1)].astype(BF16)


S5_GEN_SPECS = [pl.BlockSpec((1, 2 * S5_T - 1, 128, 128), lambda b: (b, 0, 0, 0)),
                pl.BlockSpec((1, 2, S5_T, 128, 128), lambda b: (b, 0, 0, 0, 0))]


def _s5_gen_bwd(lre, lim, lst, b_re, b_im, c_re, c_im, dvec, dg, dx, dy, da16):
    def body(lre_ref, lim_ref, lst_ref, bre_ref, bim_ref, cre_ref, cim_ref, d_ref, dg_ref, dx_ref, dy_ref, da16_ref,
             glre_ref, glim_ref, glst_ref, gbre_ref, gbim_ref, gcre_ref, gcim_ref, gd_ref):
        eye = _group_mask(128, 128, 1, 1)
        gd_ref[0, 0] = jnp.sum(dg_ref[0, S5_T - 1] * eye, axis=0, keepdims=True)
        gb = [None, None, None, None]
        for dr in range(2):
            args = (lre_ref[dr, 0], lim_ref[dr, 0], lst_ref[dr, 0], bre_ref[0, 0], bim_ref[0, 0],
                    cre_ref[0, 0], cim_ref[0, 0])
            _, vjp = jax.vjp(_s5_gen_dir, *args)
            dxs = [dx_ref[0, dr, S5_T - 1 - t if dr == 0 else t] for t in range(S5_T)]
            dys = [jnp.zeros((128, 128), F32)] + [dy_ref[0, dr, t - 1 if dr == 0 else S5_T - t]
                                                  for t in range(1, S5_T + 1)]
            dgs = [dg_ref[0, (S5_T - 1) + t if dr == 0 else (S5_T - 1) - t] for t in range(S5_T)]
            g = vjp((dxs, dys, dgs, da16_ref[0, dr]))
            glre_ref[dr, 0] = g[0]
            glim_ref[dr, 0] = g[1]
            glst_ref[dr, 0] = g[2]
            for q in range(4):
                gb[q] = g[3 + q] if gb[q] is None else gb[q] + g[3 + q]
        gbre_ref[0, 0] = gb[0]
        gbim_ref[0, 0] = gb[1]
        gcre_ref[0, 0] = gb[2]
        gcim_ref[0, 0] = gb[3]

    shp = lambda a: jax.ShapeDtypeStruct(a.shape, F32)
    return pl.pallas_call(
        body, name="s5_gen_bwd", grid=(S5_NB,),
        in_specs=_s5_param_specs() + [
            pl.BlockSpec((1, 2 * S5_T - 1, 128, 128), lambda b: (b, 0, 0, 0)),
            pl.BlockSpec((1, 2, S5_T, 128, 128), lambda b: (b, 0, 0, 0, 0)),
            pl.BlockSpec((1, 2, S5_T, 128, 128), lambda b: (b, 0, 0, 0, 0)),
            pl.BlockSpec((1, 2, 8, 128), lambda b: (b, 0, 0, 0))],
        out_specs=_s5_param_specs(),
        out_shape=[shp(lre), shp(lim), shp(lst), shp(b_re), shp(b_im), shp(c_re), shp(c_im), shp(dvec)],
        compiler_params=_params(("parallel",)),
    )(lre, lim, lst, b_re, b_im, c_re, c_im, dvec, dg, dx, dy, da16)


def _s5_rows(t):
    cn = t.shape[0] // S5_T
    return t.reshape(cn, S5_T, S5_NB, 128).transpose(2, 0, 1, 3).reshape(S5_NB, cn, S5_BW)


def _s5_put_groups(o_ref, dr, val):
    for gi in range(8):
        o_ref[dr, :, gi, :] = val[:, 128 * gi:128 * (gi + 1)]


def _s5_get_groups(s_ref, dr, n=8):
    return jnp.concatenate([s_ref[dr, :, gi, :] for gi in range(n)], axis=1).astype(BF16)


def _s5_to_states(u3, blocks, name):
    cn = u3.shape[1]

    def body(u_ref, b_ref, o_ref, w_scr):
        u = u_ref[0]
        for dr in range(2):
            _s5_fill_state_mat(w_scr, b_ref, dr)
            _s5_put_groups(o_ref, dr, _dot(u, w_scr[...]))

    return pl.pallas_call(
        body, name=name, grid=(S5_NB,),
        in_specs=[pl.BlockSpec((1, cn, S5_BW), lambda b: (b, 0, 0)), S5_GEN_SPECS[1]],
        out_specs=pl.BlockSpec((2, cn, 8, 128), lambda b: (0, 0, b, 0)),
        out_shape=jax.ShapeDtypeStruct((2, cn, S5_GROUPS, 128), F32),
        scratch_shapes=[pltpu.VMEM((S5_BW, S5_SW), BF16)],
        compiler_params=_params(("parallel",)),
    )(u3, blocks)


def _s5_from_states(u3, gg, st, blocks, transposed, name):
    cn = u3.shape[1]

    def body(u_ref, g_ref, s_ref, b_ref, o_ref, k_scr, w_scr):
        u = u_ref[0]
        _s5_fill_toeplitz(k_scr, g_ref)
        y = _dot_nt(u, k_scr[...]) if transposed else _dot(u, k_scr[...])
        for dr in range(2):
            _s5_fill_state_mat(w_scr, b_ref, dr)
            y = y + _dot_nt(_s5_get_groups(s_ref, dr), w_scr[...])
        for i in range(S5_T):
            o_ref[:, i, :] = y[:, 128 * i:128 * (i + 1)]

    return pl.pallas_call(
        body, name=name, grid=(S5_NB,),
        in_specs=[pl.BlockSpec((1, cn, S5_BW), lambda b: (b, 0, 0)), S5_GEN_SPECS[0],
                  pl.BlockSpec((2, cn, 8, 128), lambda b: (0, 0, b, 0)), S5_GEN_SPECS[1]],
        out_specs=pl.BlockSpec((cn, S5_T, 128), lambda b: (0, 0, b)),
        out_shape=jax.ShapeDtypeStruct((cn, S5_T, S5_WIDTH), F32),
        scratch_shapes=[pltpu.VMEM((S5_BW, S5_BW), BF16), pltpu.VMEM((S5_BW, S5_SW), BF16)],
        compiler_params=_params(("parallel",)),
    )(u3, gg, st, blocks)


def _s5_a_forms(a):
    ra = pltpu.roll(a, S5_STATE, 1)
    low = _iota2(a.shape, 1) < S5_STATE
    return jnp.where(low, a, ra), jnp.where(low, -ra, a)


def _s5_scan(sloc, a16, ncc):
    cn = sloc.shape[1]

    def body(s_ref, a_ref, h_ref):
        forms = [_s5_a_forms(a_ref[dr]) for dr in range(2)]

        def step(s, hs):
            out = []
            for dr in range(2):
                arr, aii = forms[dr]
                h, rh = hs[dr]
                c = s if dr == 0 else jnp.where(s < ncc, ncc - 1 - s, cn - 1 - (s - ncc))
                h_ref[dr, c] = h
                sc = s_ref[dr, c]
                out.append((h * arr + rh * aii + sc, rh * arr - h * aii + pltpu.roll(sc, S5_STATE, 1)))
            return tuple(out)

        zero = jnp.zeros((S5_GROUPS, 128), F32)
        lax.fori_loop(0, cn, step, ((zero, zero), (zero, zero)), unroll=4)

    return pl.pallas_call(
        body, name="s5_scan",
        out_shape=jax.ShapeDtypeStruct(sloc.shape, F32),
        compiler_params=_params(),
    )(sloc, a16)


def _s5_scan_bwd(e, hs, a16, ncc):
    cn = e.shape[1]

    def body(e_ref, h_ref, a_ref, ds_ref, da_ref):
        forms = [_s5_a_forms(a_ref[dr]) for dr in range(2)]
        low = _iota2((S5_GROUPS, 128), 1) < S5_STATE

        def step(s, carry):
            out = []
            r = cn - 1 - s
            for dr in range(2):
                arr, aii = forms[dr]
                g, rg, da = carry[dr]
                c = r if dr == 0 else jnp.where(r < ncc, ncc - 1 - r, cn - 1 - (r - ncc))
                ds_ref[dr, c] = g
                h = h_ref[dr, c]
                rh = pltpu.roll(h, S5_STATE, 1)
                da = da + jnp.where(low, g * h + rg * rh, g * rh - rg * h)
                ec = e_ref[dr, c]
                out.append((ec + g * arr - rg * aii, pltpu.roll(ec, S5_STATE, 1) + rg * arr + g * aii, da))
            return tuple(out)

        zero = jnp.zeros((S5_GROUPS, 128), F32)
        res = lax.fori_loop(0, cn, step, ((zero, zero, zero), (zero, zero, zero)), unroll=4)
        da_ref[0] = res[0][2]
        da_ref[1] = res[1][2]

    return pl.pallas_call(
        body, name="s5_scan_bwd",
        out_shape=[jax.ShapeDtypeStruct(e.shape, F32), jax.ShapeDtypeStruct((2, S5_GROUPS, 128), F32)],
        compiler_params=_params(),
    )(e, hs, a16)


def _s5_bwd_kb(p3, dy3):
    cn = p3.shape[1]
    half = S5_T // 2

    def body(u_ref, d_ref, o_ref):
        q = pl.program_id(1)

        @pl.when(q == 0)
        def _():
            o_ref[...] = jnp.zeros_like(o_ref)

        dk = _dot_tn(u_ref[0], d_ref[0])
        for j in range(S5_T):
            for i in range(half):
                o_ref[0, half * q + i - j + (S5_T - 1)] += dk[128 * j:128 * (j + 1), 128 * i:128 * (i + 1)]

    return pl.pallas_call(
        body, name="s5_bwd_kb", grid=(S5_NB, 2),
        in_specs=[pl.BlockSpec((1, cn, S5_BW), lambda b, q: (b, 0, 0)),
                  pl.BlockSpec((1, cn, S5_BW // 2), lambda b, q: (b, 0, q))],
        out_specs=pl.BlockSpec((1, 2 * S5_T - 1, 128, 128), lambda b, q: (b, 0, 0, 0)),
        out_shape=jax.ShapeDtypeStruct((S5_NB, 2 * S5_T - 1, 128, 128), F32),
        compiler_params=_params(("parallel", "arbitrary")),
    )(p3, dy3)


def _s5_bwd_w(u3, st, name):
    cn = u3.shape[1]

    def body(u_ref, s_ref, w_ref):
        dw = _dot_tn(u_ref[0], _s5_get_groups(s_ref, 0))
        for j in range(S5_T):
            w_ref[0, 0, j] = _s5_contract(dw[128 * j:128 * (j + 1), :])

    return pl.pallas_call(
        body, name=name, grid=(S5_NB, 2),
        in_specs=[pl.BlockSpec((1, cn, S5_BW), lambda b, q: (b, 0, 0)),
                  pl.BlockSpec((1, cn, 8, 128), lambda b, q: (q, 0, b, 0))],
        out_specs=pl.BlockSpec((1, 1, S5_T, 128, 128), lambda b, q: (b, q, 0, 0, 0)),
        out_shape=jax.ShapeDtypeStruct((S5_NB, 2, S5_T, 128, 128), F32),
        compiler_params=_params(("parallel", "parallel")),
    )(u3, st)


def _s5_glu_bwd(y_all, dmix, w_glu_b, b_glu, nct):
    la = y_all.shape[0]
    tm = TOK_TILE

    def body(y_ref, d_ref, w_ref, b_ref, dy_ref, gw_ref, gb_ref):
        i = pl.program_id(0)

        @pl.when(i == 0)
        def _():
            gw_ref[...] = jnp.zeros_like(gw_ref)
            gb_ref[...] = jnp.zeros_like(gb_ref)

        @pl.when(i < nct)
        def _():
            dy_ref[...] = jnp.zeros_like(dy_ref)

        @pl.when(i >= nct)
        def _():
            y = y_ref[...]
            yg, gelu_vjp = jax.vjp(_gelu, y)
            ygb = yg.astype(BF16)
            sg = _sigmoid(_dot(ygb, w_ref[...]) + b_ref[...])
            ds = d_ref[...]
            dz = ds * yg * sg * (1.0 - sg)
            dzb = dz.astype(BF16)
            dyg = ds * sg + _dot_nt(dzb, w_ref[...])
            dy_ref[...] = gelu_vjp(dyg)[0].astype(BF16)
            gw_ref[...] += _dot_tn(ygb, dzb)
            gb_ref[...] += jnp.sum(dz, axis=0, keepdims=True)

    return pl.pallas_call(
        body, name="s5_glu_bwd", grid=(la // tm,),
        in_specs=[pl.BlockSpec((tm, S5_WIDTH), lambda i: (i, 0)),
                  pl.BlockSpec((tm, S5_WIDTH), lambda i: (jnp.maximum(i - nct, 0), 0)),
                  _full((S5_WIDTH, S5_WIDTH)), _full((1, S5_WIDTH))],
        out_specs=[pl.BlockSpec((tm, S5_WIDTH), lambda i: (i, 0)), _full((S5_WIDTH, S5_WIDTH)),
                   _full((1, S5_WIDTH))],
        out_shape=[jax.ShapeDtypeStruct((la, S5_WIDTH), BF16), jax.ShapeDtypeStruct((S5_WIDTH, S5_WIDTH), F32),
                   jax.ShapeDtypeStruct((1, S5_WIDTH), F32)],
        compiler_params=_params(("arbitrary",)),
    )(y_all, dmix, w_glu_b, b_glu)


K_SCALE = RET_DH ** -0.5
Q_COL, K_COL, V_COL, G_COL = 4, 8, 12, 16


def _ret_chunk_of(step, ncc, nch, rev):
    if not rev:
        return step
    return jnp.where(step < ncc, ncc - 1 - step, nch - 1 - (step - ncc))


def _ret_decay(ld, rev):
    c = _iota2((RET_CHUNK, RET_CHUNK), 0).astype(F32)
    m = _iota2((RET_CHUNK, RET_CHUNK), 1).astype(F32)
    diff = (m - c) if rev else (c - m)
    keep = (diff > 0) if rev else (diff >= 0)
    expo = jnp.maximum(diff, 0.0)
    dm = jnp.where(keep, jnp.exp(ld * expo), 0.0)
    xi_e = (RET_CHUNK - c) if rev else (c + 1.0)
    zeta_e = c if rev else (RET_CHUNK - 1.0 - c)
    return dm, expo, jnp.exp(ld * xi_e), xi_e, jnp.exp(ld * zeta_e), zeta_e


RET_TABLES = 7


def _ret_tables(ld2):
    def body(ld_ref, t_ref):
        dr, h = pl.program_id(0), pl.program_id(1)
        ldh = ld_ref[dr, h]
        for rev in (False, True):
            @pl.when(dr == int(rev))
            def _(rev=rev):
                dm, expo, xi, xi_e, zeta, zeta_e = _ret_decay(ldh, rev)
                t_ref[0, 0, 0] = dm
                t_ref[0, 0, 1] = dm * expo
                t_ref[0, 0, 2] = xi
                t_ref[0, 0, 3] = xi * xi_e
                t_ref[0, 0, 4] = zeta
                t_ref[0, 0, 5] = zeta * zeta_e
                t_ref[0, 0, 6] = jnp.zeros_like(dm) + jnp.exp(ldh * RET_CHUNK)

    return pl.pallas_call(
        body, name="ret_tables", grid=(2, RET_HEADS),
        in_specs=[pl.BlockSpec(memory_space=pltpu.SMEM)],
        out_specs=pl.BlockSpec((1, 1, RET_TABLES, RET_CHUNK, RET_CHUNK), lambda d, h: (d, h, 0, 0, 0)),
        out_shape=jax.ShapeDtypeStruct((2, RET_HEADS, RET_TABLES, RET_CHUNK, RET_CHUNK), F32),
        compiler_params=_params(("parallel", "parallel")),
    )(ld2)


def _ret_specs(nch, ncc, rev, step_of):
    chunk = lambda n: _ret_chunk_of(step_of(n), ncc, nch, rev)
    cols = [pl.BlockSpec((RET_CHUNK, RET_WIDTH), functools.partial(lambda n, cb: (chunk(n), cb), cb=cb))
            for cb in (1, 2, 3)]
    tab = pl.BlockSpec((RET_CHUNK, RET_DH), lambda n: (chunk(n), 0))
    return cols + [tab, tab], pl.BlockSpec((RET_CHUNK, RET_WIDTH), lambda n: (chunk(n), 0))


def _ret_scan(p_all, cos_t, sin_t, tabs, ncc, placed, kinds):
    la = p_all.shape[0]
    nch = la // RET_CHUNK
    n = len(placed)
    shard_shapes = _gather_shard_shapes(placed, kinds)

    def body(t_ref, qf, kf, vf, cf, sf, qb, kb, vb, cb, sb, *rest):
        of_ref, ob_ref, ssf_ref, ssb_ref = rest[n:n + 4]
        s_scr, send_sems, recv_sems = rest[2 * n + 4:]
        step = pl.program_id(0)

        @pl.when(step == 0)
        def _():
            s_scr[...] = jnp.zeros_like(s_scr)
            for cp in _gather_chip_copies(rest[n + 4:2 * n + 4], kinds, shard_shapes, send_sems, recv_sems, False)[0]:
                cp.start()

        @pl.when(step == nch - 1)
        def _():
            sends, arrivals = _gather_chip_copies(rest[n + 4:2 * n + 4], kinds, shard_shapes, send_sems, recv_sems)
            for cp in arrivals:
                cp.wait_recv()
            for cp in sends:
                cp.wait_send()

        for dr, (q_ref, k_ref, v_ref, c_ref, n_ref, o_ref, ss_ref) in enumerate(
                ((qf, kf, vf, cf, sf, of_ref, ssf_ref), (qb, kb, vb, cb, sb, ob_ref, ssb_ref))):
            cs, sn = c_ref[...], n_ref[...]
            for h in range(RET_HEADS):
                sl = slice(RET_DH * h, RET_DH * (h + 1))
                dm, xi, zeta = t_ref[dr, h, 0], t_ref[dr, h, 2, :, 0:RET_DH], t_ref[dr, h, 4, :, 0:RET_DH]
                q = _rope(q_ref[:, sl], cs, sn)
                k = _rope(k_ref[:, sl] * K_SCALE, cs, sn)
                vh = v_ref[:, sl].astype(BF16)
                s = s_scr[dr, h]
                ss_ref[0, h] = s
                sc = (_dot_nt(q.astype(BF16), k.astype(BF16)) * dm).astype(BF16)
                o_ref[:, sl] = _dot(sc, vh) + _dot((q * xi).astype(BF16), s.astype(BF16))
                s_scr[dr, h] = t_ref[dr, h, 6, 0:RET_DH, 0:RET_DH] * s + _dot_tn((k * zeta).astype(BF16), vh)

    in_f, out_f = _ret_specs(nch, ncc, False, lambda n: n)
    in_b, out_b = _ret_specs(nch, ncc, True, lambda n: n)
    ss_spec = pl.BlockSpec((1, RET_HEADS, RET_DH, RET_DH), lambda n: (n, 0, 0, 0))
    o_shape = jax.ShapeDtypeStruct((la, RET_WIDTH), F32)
    ss_shape = jax.ShapeDtypeStruct((nch, RET_HEADS, RET_DH, RET_DH), F32)
    return pl.pallas_call(
        body, name="ret_scan", grid=(nch,),
        in_specs=[_full(tabs.shape)] + in_f + in_b + [ANY] * n,
        out_specs=[out_f, out_b, ss_spec, ss_spec] + [ANY] * n,
        out_shape=[o_shape, o_shape, ss_shape, ss_shape] + [jax.ShapeDtypeStruct(p.shape, p.dtype) for p in placed],
        input_output_aliases={11 + a: 4 + a for a in range(n)},
        scratch_shapes=[pltpu.VMEM((2, RET_HEADS, RET_DH, RET_DH), F32),
                        pltpu.SemaphoreType.DMA((n, 3)), pltpu.SemaphoreType.DMA((n, 3))],
        compiler_params=_params(("arbitrary",)),
    )(tabs, p_all, p_all, p_all, cos_t, sin_t, p_all, p_all, p_all, cos_t, sin_t, *placed)


def _ret_scan_bwd(p_all, cos_t, sin_t, tabs, ssf, ssb, dy_all, ncc):
    la = p_all.shape[0]
    nch = la // RET_CHUNK

    def body(t_ref, qf, kf, vf, cf, sf, dof, ssf_ref, qb, kb, vb, cb, sb, dob_, ssb_ref,
             dqf, dkf, dvf, dqb, dkb, dvb, dld_ref, ds_scr):
        @pl.when(pl.program_id(0) == 0)
        def _():
            ds_scr[...] = jnp.zeros_like(ds_scr)
            dld_ref[...] = jnp.zeros_like(dld_ref)

        for dr, (q_ref, k_ref, v_ref, c_ref, n_ref, do_ref, ss_ref, dq_ref, dk_ref, dv_ref) in enumerate(
                ((qf, kf, vf, cf, sf, dof, ssf_ref, dqf, dkf, dvf), (qb, kb, vb, cb, sb, dob_, ssb_ref, dqb, dkb, dvb))):
            cs, sn = c_ref[...], n_ref[...]
            on_ctx = _ret_chunk_of(nch - 1 - pl.program_id(0), ncc, nch, dr == 1) < ncc
            for h in range(RET_HEADS):
                sl = slice(RET_DH * h, RET_DH * (h + 1))
                dm, dm_d = t_ref[dr, h, 0], t_ref[dr, h, 1]
                xi, xi_d, zeta, zeta_d = [t_ref[dr, h, t, :, 0:RET_DH] for t in (2, 3, 4, 5)]
                gc = t_ref[dr, h, 6, 0:RET_DH, 0:RET_DH]
                q = _rope(q_ref[:, sl], cs, sn)
                k = _rope(k_ref[:, sl] * K_SCALE, cs, sn)
                q16, k16, v16 = q.astype(BF16), k.astype(BF16), v_ref[:, sl].astype(BF16)
                s = ss_ref[0, h]
                s16 = s.astype(BF16)
                ds_in = ds_scr[dr, h]
                ds16 = ds_in.astype(BF16)
                do16 = jnp.where(on_ctx, 0.0, do_ref[:, sl]).astype(BF16)
                qk = _dot_nt(q16, k16)
                dsv = _dot_nt(do16, v16)
                dsc = (dsv * dm).astype(BF16)
                sc16 = (qk * dm).astype(BF16)
                dos = _dot_nt(do16, s16)
                vds = _dot_nt(v16, ds16)
                dq_ref[:, sl] = _dot(dsc, k16) + dos * xi
                dk_ref[:, sl] = _dot_tn(dsc, q16) + vds * zeta
                dv_ref[:, sl] = _dot_tn(sc16, do16) + _dot((k * zeta).astype(BF16), ds16)
                ds_scr[dr, h] = _dot_tn((q * xi).astype(BF16), do16) + gc * ds_in
                dld = (jnp.sum(dsv * qk * dm_d) + jnp.sum(q * dos * xi_d + k * vds * zeta_d)
                       + RET_CHUNK * jnp.sum(gc * s * ds_in))
                dld_ref[dr, h] += dld

    back = lambda n: nch - 1 - n
    in_f, out_f = _ret_specs(nch, ncc, False, back)
    in_b, out_b = _ret_specs(nch, ncc, True, back)
    ss_spec = pl.BlockSpec((1, RET_HEADS, RET_DH, RET_DH), lambda n: (nch - 1 - n, 0, 0, 0))
    shp = jax.ShapeDtypeStruct((la, RET_WIDTH), F32)
    dy_spec = lambda rev: pl.BlockSpec(
        (RET_CHUNK, RET_WIDTH), lambda n: (jnp.maximum(_ret_chunk_of(nch - 1 - n, ncc, nch, rev) - ncc, 0), 0))
    return pl.pallas_call(
        body, name="ret_scan_bwd", grid=(nch,),
        in_specs=[_full(tabs.shape)] + in_f + [dy_spec(False), ss_spec] + in_b + [dy_spec(True), ss_spec],
        out_specs=[out_f, out_f, out_f, out_b, out_b, out_b, _full((2, RET_HEADS, 8, 128))],
        out_shape=[shp] * 6 + [jax.ShapeDtypeStruct((2, RET_HEADS, 8, 128), F32)],
        scratch_shapes=[pltpu.VMEM((2, RET_HEADS, RET_DH, RET_DH), F32)],
        compiler_params=_params(("arbitrary",)),
    )(tabs, p_all, p_all, p_all, cos_t, sin_t, dy_all, ssf, p_all, p_all, p_all, cos_t, sin_t, dy_all, ssb)


def _ret_gate_bwd(y_ret, p_all, dmix, nct):
    la = p_all.shape[0]
    tm = TOK_TILE

    def body(y_ref, g_ref, d_ref, dy_ref, dg_ref):
        i = pl.program_id(0)

        @pl.when(i < nct)
        def _():
            dy_ref[...] = jnp.zeros_like(dy_ref)
            dg_ref[...] = jnp.zeros_like(dg_ref)

        @pl.when(i >= nct)
        def _():
            for h in range(RET_HEADS):
                sl = slice(RET_DH * h, RET_DH * (h + 1))
                _, vjp = jax.vjp(_head_norm_gate, y_ref[:, sl], g_ref[:, sl])
                dy, dg = vjp(d_ref[:, sl])
                dy_ref[:, sl] = dy
                dg_ref[:, sl] = dg

    xrow = lambda cb: pl.BlockSpec((tm, RET_WIDTH), lambda i: (jnp.maximum(i - nct, 0), cb))
    out = pl.BlockSpec((tm, RET_WIDTH), lambda i: (i, 0))
    shp = jax.ShapeDtypeStruct((la, RET_WIDTH), F32)
    return pl.pallas_call(
        body, name="ret_gate_bwd", grid=(la // tm,),
        in_specs=[xrow(0), pl.BlockSpec((tm, RET_WIDTH), lambda i: (i, G_COL // 4)), xrow(1)],
        out_specs=[out, out], out_shape=[shp, shp],
        compiler_params=_params(("parallel",)),
    )(y_ret, p_all, dmix)


def _in_bwd(dqf, dkf, dvf, dqb, dkb, dvb, du, dg, cos_t, sin_t, w_in_b, x, ctx, n1w, mod4, dx1):
    l, lc = x.shape[0], ctx.shape[0]
    la = l + lc
    tm = TOK_TILE
    nct = lc // tm

    def body(dqf_ref, dkf_ref, dvf_ref, dqb_ref, dkb_ref, dvb_ref, du_ref, dg_ref, cos_ref, sin_ref,
             w_ref, x_ref, c_ref, nw_ref, mod_ref, dx1_ref, dp_ref, gx_ref, acc_ref):
        i = pl.program_id(0)
        is_ctx = i < nct

        @pl.when(i == 0)
        def _():
            acc_ref[...] = jnp.zeros_like(acc_ref)

        cs, sn = cos_ref[...], sin_ref[...]
        def piece(k, val):
            cols = slice(S5_WIDTH * k, S5_WIDTH * (k + 1))
            dp_ref[:, cols] = val.astype(BF16)
            return _dot_nt(dp_ref[:, cols], w_ref[:, cols])

        dh1 = piece(0, du_ref[...])
        dh1 = dh1 + piece(3, dvf_ref[...] + dvb_ref[...])
        dh1 = dh1 + piece(4, jnp.where(is_ctx, 0.0, dg_ref[...]))
        for k, (f_ref, b_ref, scale) in ((1, (dqf_ref, dqb_ref, 1.0)), (2, (dkf_ref, dkb_ref, K_SCALE))):
            heads = [_rope_t(f_ref[:, RET_DH * h:RET_DH * (h + 1)] + b_ref[:, RET_DH * h:RET_DH * (h + 1)], cs, sn) * scale
                     for h in range(RET_HEADS)]
            dh1 = dh1 + piece(k, jnp.concatenate(heads, axis=1))
        xt = jnp.where(is_ctx, c_ref[...], x_ref[...])
        sh = jnp.where(is_ctx, mod_ref[0:1, :], mod_ref[2:3, :])
        sc = jnp.where(is_ctx, mod_ref[1:2, :], mod_ref[3:4, :])
        _, vjp = jax.vjp(_rms_mod, xt, nw_ref[...], sh, sc)
        dx, dnw, dsh, dsc = vjp(dh1)
        gx_ref[...] = dx + dx1_ref[...]
        cf = jnp.where(is_ctx, 1.0, 0.0)
        acc_ref[0:1, :] += dnw
        acc_ref[1:2, :] += cf * dsh
        acc_ref[2:3, :] += cf * dsc
        acc_ref[3:4, :] += (1.0 - cf) * dsh
        acc_ref[4:5, :] += (1.0 - cf) * dsc

    row = pl.BlockSpec((tm, RET_WIDTH), lambda i: (i, 0))
    tab = pl.BlockSpec((tm, RET_DH), lambda i: (i, 0))
    xrow = pl.BlockSpec((tm, D_MODEL), lambda i: (jnp.maximum(i - nct, 0), 0))
    return pl.pallas_call(
        body, name="in_bwd", grid=(la // tm,),
        in_specs=[row] * 7 + [pl.BlockSpec((tm, RET_WIDTH), lambda i: (jnp.maximum(i - nct, 0), 0)),
                              tab, tab, _full((D_MODEL, IN_COLS)), xrow,
                              pl.BlockSpec((tm, D_MODEL), lambda i: (jnp.minimum(i, nct - 1), 0)),
                              _full((1, D_MODEL)), _full((4, D_MODEL)), xrow],
        out_specs=[pl.BlockSpec((tm, IN_COLS), lambda i: (i, 0)), xrow, _full((8, D_MODEL))],
        out_shape=[jax.ShapeDtypeStruct((la, IN_COLS), BF16), jax.ShapeDtypeStruct((l, D_MODEL), F32),
                   jax.ShapeDtypeStruct((8, D_MODEL), F32)],
        compiler_params=_params(("arbitrary",)),
    )(dqf, dkf, dvf, dqb, dkb, dvb, du, dg, cos_t, sin_t, w_in_b, x, ctx, n1w, mod4, dx1)


def _outproj_up(x, y_all, of, ob, p_all, w_glu_b, b_glu, w_out_b, mod3, n2w, w_up_b, nct):
    l = x.shape[0]
    tm = TOK_TILE

    def body(x_ref, y_ref, of_ref, ob_ref, g_ref, wg_ref, bg_ref, wo_ref, mod_ref, nw_ref, wu_ref,
             x1_ref, mix_ref, h2_ref, up_ref, mb_ref, yr_ref):
        yg = _gelu(y_ref[...])
        mb_ref[:, 0:S5_WIDTH] = (yg * _sigmoid(_dot(yg.astype(BF16), wg_ref[...]) + bg_ref[...])).astype(BF16)
        yr = of_ref[...] + ob_ref[...]
        yr_ref[...] = yr
        for h in range(RET_HEADS):
            sl = slice(RET_DH * h, RET_DH * (h + 1))
            mb_ref[:, S5_WIDTH + RET_DH * h:S5_WIDTH + RET_DH * (h + 1)] = (
                _head_norm_gate(yr[:, sl], g_ref[:, sl]).astype(BF16))
        mix = _dot(mb_ref[...], wo_ref[...])
        mix_ref[...] = mix
        x1 = x_ref[...] + mod_ref[0:1, :] * mix
        x1_ref[...] = x1
        h2 = _rms_mod(x1, nw_ref[...], mod_ref[1:2, :], mod_ref[2:3, :]).astype(BF16)
        h2_ref[...] = h2
        up_ref[...] = _dot(h2, wu_ref[...])

    row = lambda w: pl.BlockSpec((tm, w), lambda i: (i, 0))
    arow = pl.BlockSpec((tm, RET_WIDTH), lambda i: (i + nct, 0))
    return pl.pallas_call(
        body, name="outproj_up", grid=(l // tm,),
        in_specs=[row(D_MODEL), arow, arow, arow, pl.BlockSpec((tm, RET_WIDTH), lambda i: (i + nct, G_COL // 4)),
                  _full((S5_WIDTH, S5_WIDTH)), _full((1, S5_WIDTH)), _full((D_MODEL, D_MODEL)), _full((3, D_MODEL)),
                  _full((1, D_MODEL)), _full((D_MODEL, 2 * D_FF))],
        out_specs=[row(D_MODEL), row(D_MODEL), row(D_MODEL), row(2 * D_FF), row(D_MODEL), row(RET_WIDTH)],
        out_shape=[jax.ShapeDtypeStruct((l, D_MODEL), F32), jax.ShapeDtypeStruct((l, D_MODEL), F32),
                   jax.ShapeDtypeStruct((l, D_MODEL), BF16), jax.ShapeDtypeStruct((l, 2 * D_FF), F32),
                   jax.ShapeDtypeStruct((l, D_MODEL), BF16), jax.ShapeDtypeStruct((l, RET_WIDTH), F32)],
        compiler_params=_params(("parallel",)),
    )(x, y_all, of, ob, p_all, w_glu_b, b_glu, w_out_b, mod3, n2w, w_up_b)


HALO = 8


def _conv_taps(g, prev_row, next_row):
    t = g.shape[0]
    r = _iota2(g.shape, 0)
    gprev = jnp.where(r == 0, prev_row, pltpu.roll(g, 1, 0))
    gnext = jnp.where(r == t - 1, next_row, pltpu.roll(g, t - 1, 0))
    return gprev, gnext


def _ffn_loss(up, x1, conv_w, conv_b, w_down_b, gate, fnw, tgt):
    l = x1.shape[0]
    tm = TOK_TILE
    nt = l // tm
    hb = tm // HALO

    cw = 256

    def body(up_a, up_g, hp_ref, hn_ref, x1_ref, cw_ref, cb_ref, wd_ref, gate_ref, fn_ref, tgt_ref,
             act_ref, dx2_ref, ddn_ref, dact_ref, acc_ref):
        i = pl.program_id(0)

        @pl.when(i == 0)
        def _():
            acc_ref[...] = jnp.zeros_like(acc_ref)

        dn = jnp.zeros((tm, D_MODEL), F32)
        for c in range(D_FF // cw):
            cols = slice(cw * c, cw * (c + 1))
            g = up_g[:, cols]
            prev_row = jnp.where(i == 0, 0.0, hp_ref[HALO - 1:HALO, cols])
            next_row = jnp.where(i == nt - 1, 0.0, hn_ref[0:1, cols])
            gprev, gnext = _conv_taps(g, prev_row, next_row)
            gc = cb_ref[:, cols] + gprev * cw_ref[0:1, cols] + g * cw_ref[1:2, cols] + gnext * cw_ref[2:3, cols]
            act = (_gelu(gc) * up_a[:, cols]).astype(BF16)
            act_ref[:, cols] = act
            dn = dn + _dot(act, wd_ref[cols, :])
        x2 = x1_ref[...] + gate_ref[...] * dn
        y, vjp = jax.vjp(_rms, x2, fn_ref[...])
        err = y - tgt_ref[...]
        dx2, dfn = vjp(err * (1.0 / D_MODEL))
        dx2_ref[...] = dx2
        ddn = (dx2 * gate_ref[...]).astype(BF16)
        ddn_ref[...] = ddn
        for c in range(D_FF // cw):
            cols = slice(cw * c, cw * (c + 1))
            dact_ref[:, cols] = _dot_nt(ddn, wd_ref[cols, :])
        acc_ref[0:1, :] += dfn
        acc_ref[1:2, :] += jnp.sum(dx2 * dn, axis=0, keepdims=True)
        acc_ref[2:3, :] += (0.5 / D_MODEL) * jnp.sum(err * err)

    row = lambda w: pl.BlockSpec((tm, w), lambda i: (i, 0))
    last = l // HALO - 1
    return pl.pallas_call(
        body, name="ffn_loss", grid=(nt,),
        in_specs=[pl.BlockSpec((tm, D_FF), lambda i: (i, 0)), pl.BlockSpec((tm, D_FF), lambda i: (i, 1)),
                  pl.BlockSpec((HALO, D_FF), lambda i: (jnp.maximum(i * hb - 1, 0), 1)),
                  pl.BlockSpec((HALO, D_FF), lambda i: (jnp.minimum((i + 1) * hb, last), 1)),
                  row(D_MODEL), _full((3, D_FF)), _full((1, D_FF)), _full((D_FF, D_MODEL)),
                  _full((1, D_MODEL)), _full((1, D_MODEL)), row(D_MODEL)],
        out_specs=[row(D_FF), row(D_MODEL), row(D_MODEL), row(D_FF), _full((8, D_MODEL))],
        out_shape=[jax.ShapeDtypeStruct((l, D_FF), BF16), jax.ShapeDtypeStruct((l, D_MODEL), F32),
                   jax.ShapeDtypeStruct((l, D_MODEL), BF16), jax.ShapeDtypeStruct((l, D_FF), F32),
                   jax.ShapeDtypeStruct((8, D_MODEL), F32)],
        compiler_params=_params(("arbitrary",)),
    )(up, up, up, up, x1, conv_w, conv_b, w_down_b, gate, fnw, tgt)


def _convglu_bwd(up, dact, conv_w, conv_b):
    l = up.shape[0]
    tm = 128
    nt = l // tm
    hb = tm // HALO
    te = tm + 2 * HALO

    def body(a_ref, ap_ref, an_ref, g_ref, gp_ref, gn_ref, d_ref, dp_ref, dn_ref, cw_ref, cb_ref,
             dup_ref, acc_ref):
        i = pl.program_id(0)

        @pl.when(i == 0)
        def _():
            acc_ref[...] = jnp.zeros_like(acc_ref)

        def ext(p, c, n):
            return jnp.concatenate([jnp.where(i == 0, 0.0, p[...]), c[...], jnp.where(i == nt - 1, 0.0, n[...])], axis=0)

        ae, ge, de = ext(ap_ref, a_ref, an_ref), ext(gp_ref, g_ref, gn_ref), ext(dp_ref, d_ref, dn_ref)
        gprev = pltpu.roll(ge, 1, 0)
        gnext = pltpu.roll(ge, te - 1, 0)
        w0, w1, w2 = cw_ref[0:1, :], cw_ref[1:2, :], cw_ref[2:3, :]
        gce = cb_ref[...] + gprev * w0 + ge * w1 + gnext * w2
        gel, dgel = _gelu_and_grad(gce)
        dae = de * gel
        dgce = de * ae * dgel
        dge = dgce * w1 + pltpu.roll(dgce, te - 1, 0) * w0 + pltpu.roll(dgce, 1, 0) * w2
        mid = slice(HALO, HALO + tm)
        dup_ref[:, 0:D_FF] = dae[mid].astype(BF16)
        dup_ref[:, D_FF:2 * D_FF] = dge[mid].astype(BF16)
        dgc = dgce[mid]
        acc_ref[0:1, :] += jnp.sum(dgc * gprev[mid], axis=0, keepdims=True)
        acc_ref[1:2, :] += jnp.sum(dgc * ge[mid], axis=0, keepdims=True)
        acc_ref[2:3, :] += jnp.sum(dgc * gnext[mid], axis=0, keepdims=True)
        acc_ref[3:4, :] += jnp.sum(dgc, axis=0, keepdims=True)

    last = l // HALO - 1

    def trio(cb):
        return [pl.BlockSpec((tm, D_FF), lambda i: (i, cb)),
                pl.BlockSpec((HALO, D_FF), lambda i: (jnp.maximum(i * hb - 1, 0), cb)),
                pl.BlockSpec((HALO, D_FF), lambda i: (jnp.minimum((i + 1) * hb, last), cb))]

    return pl.pallas_call(
        body, name="convglu_bwd", grid=(nt,),
        in_specs=trio(0) + trio(1) + trio(0) + [_full((3, D_FF)), _full((1, D_FF))],
        out_specs=[pl.BlockSpec((tm, 2 * D_FF), lambda i: (i, 0)), _full((8, D_FF))],
        out_shape=[jax.ShapeDtypeStruct((l, 2 * D_FF), BF16), jax.ShapeDtypeStruct((8, D_FF), F32)],
        compiler_params=_params(("arbitrary",)),
    )(up, up, up, up, up, up, dact, dact, dact, conv_w, conv_b)


def _up_bwd(dup, w_up_b, w_out_b, x1, dx2, mix, mod3, n2w, y_all, y_ret, p_all, w_glu_b, b_glu, nct, pairs, kinds):
    l = x1.shape[0]
    tm = TOK_TILE
    nt = l // tm
    n = len(pairs)
    shapes = _rs_slot_shapes(pairs, kinds)
    n_out = 8

    def body(dup_ref, wu_ref, wo_ref, x1_ref, dx2_ref, mix_ref, mod_ref, nw_ref, y_ref, yr_ref, g_ref, wg_ref, bg_ref,
             *rest):
        dx1_ref, dmixb_ref, acc_ref, dys_ref, dyr_ref, dg_ref, gw_ref, gb_ref = rest[n:n + n_out]
        step = pl.program_id(0)

        @pl.when(step == 0)
        def _():
            acc_ref[...] = jnp.zeros_like(acc_ref)
            gw_ref[...] = jnp.zeros_like(gw_ref)
            gb_ref[...] = jnp.zeros_like(gb_ref)

        _behind(step, nt - 1, functools.partial(_rs_chip_copies, rest[:n], rest[n + n_out:2 * n + n_out], kinds,
                                                shapes, *rest[2 * n + n_out:]))

        dh2 = _dot_nt(dup_ref[...], wu_ref[...])
        _, vjp = jax.vjp(_rms_mod, x1_ref[...], nw_ref[...], mod_ref[1:2, :], mod_ref[2:3, :])
        dx, dnw, dsh, dsc = vjp(dh2)
        dx1 = dx + dx2_ref[...]
        dx1_ref[...] = dx1
        dmixb = (dx1 * mod_ref[0:1, :]).astype(BF16)
        dmixb_ref[...] = dmixb
        dmix = _dot_nt(dmixb, wo_ref[...])
        acc_ref[0:1, :] += dnw
        acc_ref[1:2, :] += jnp.sum(dx1 * mix_ref[...], axis=0, keepdims=True)
        acc_ref[2:3, :] += dsh
        acc_ref[3:4, :] += dsc

        yg, dgel = _gelu_and_grad(y_ref[...])
        ygb = yg.astype(BF16)
        sg = _sigmoid(_dot(ygb, wg_ref[...]) + bg_ref[...])
        ds = dmix[:, 0:S5_WIDTH]
        dz = ds * yg * sg * (1.0 - sg)
        dzb = dz.astype(BF16)
        dys_ref[...] = ((ds * sg + _dot_nt(dzb, wg_ref[...])) * dgel).astype(BF16)
        gw_ref[...] += _dot_tn(ygb, dzb)
        gb_ref[...] += jnp.sum(dz, axis=0, keepdims=True)

        for h in range(RET_HEADS):
            sl = slice(RET_DH * h, RET_DH * (h + 1))
            _, hvjp = jax.vjp(_head_norm_gate, yr_ref[:, sl], g_ref[:, sl])
            dyr, dg = hvjp(dmix[:, S5_WIDTH + RET_DH * h:S5_WIDTH + RET_DH * (h + 1)])
            dyr_ref[:, sl] = dyr
            dg_ref[:, sl] = dg

    row = pl.BlockSpec((tm, D_MODEL), lambda i: (i, 0))
    half = pl.BlockSpec((tm, S5_WIDTH), lambda i: (i, 0))
    f32h = jax.ShapeDtypeStruct((l, RET_WIDTH), F32)
    return pl.pallas_call(
        body, name="up_bwd", grid=(nt,),
        in_specs=[pl.BlockSpec((tm, 2 * D_FF), lambda i: (i, 0)), _full((D_MODEL, 2 * D_FF)),
                  _full((D_MODEL, D_MODEL)), row, row, row, _full((3, D_MODEL)), _full((1, D_MODEL)),
                  pl.BlockSpec((tm, S5_WIDTH), lambda i: (i + nct, 0)), half,
                  pl.BlockSpec((tm, RET_WIDTH), lambda i: (i + nct, G_COL // 4)),
                  _full((S5_WIDTH, S5_WIDTH)), _full((1, S5_WIDTH))] + [ANY] * n,
        out_specs=[row, row, _full((8, D_MODEL)), half, half, half, _full((S5_WIDTH, S5_WIDTH)),
                   _full((1, S5_WIDTH))] + [ANY] * n,
        out_shape=[jax.ShapeDtypeStruct((l, D_MODEL), F32), jax.ShapeDtypeStruct((l, D_MODEL), BF16),
                   jax.ShapeDtypeStruct((8, D_MODEL), F32), jax.ShapeDtypeStruct((l, S5_WIDTH), BF16), f32h, f32h,
                   jax.ShapeDtypeStruct((S5_WIDTH, S5_WIDTH), F32), jax.ShapeDtypeStruct((1, S5_WIDTH), F32)]
        + [jax.ShapeDtypeStruct((4,) + s, p.dtype) for s, p in zip(shapes, pairs)],
        scratch_shapes=[pltpu.SemaphoreType.DMA((n, 3)), pltpu.SemaphoreType.DMA((n, 3))],
        compiler_params=_params(("arbitrary",)),
    )(dup, w_up_b, w_out_b, x1, dx2, mix, mod3, n2w, y_all, y_ret, p_all, w_glu_b, b_glu, *pairs)


MOD_ROWS = 16
MOD_COLS = 6 * D_MODEL // 4


def _mod_fwd(c_all, c_ctx, w_mod_b, b_loc):
    def body(c_ref, cc_ref, w_ref, b_ref, m_ref, s_ref):
        cond = jnp.concatenate([c_ref[...], jnp.broadcast_to(cc_ref[...], (8, D_MODEL))], axis=0)
        s = _silu(cond).astype(BF16)
        s_ref[...] = s
        m_ref[...] = _dot(s, w_ref[...]) + b_ref[...]

    return pl.pallas_call(
        body, name="mod_fwd",
        out_shape=[jax.ShapeDtypeStruct((MOD_ROWS, MOD_COLS), F32), jax.ShapeDtypeStruct((MOD_ROWS, D_MODEL), BF16)],
        compiler_params=_params(),
    )(c_all, c_ctx, w_mod_b, b_loc)


def _mod_bwd_sum(dm_all):
    def body(d_ref, dm_ref, gb_ref):
        rows = [d_ref[k, 0:1, :] for k in range(8)]
        ctx_sum = d_ref[0, 1:2, :]
        for k in range(1, 8):
            ctx_sum = ctx_sum + d_ref[k, 1:2, :]
        gb = ctx_sum
        for k in range(8):
            gb = gb + rows[k]
        gb_ref[...] = gb
        dm_ref[...] = jnp.concatenate(rows + [ctx_sum] + [jnp.zeros((7, 6 * D_MODEL), F32)], axis=0)

    return pl.pallas_call(
        body, name="mod_bwd_sum",
        out_shape=[jax.ShapeDtypeStruct((MOD_ROWS, 6 * D_MODEL), F32), jax.ShapeDtypeStruct((1, 6 * D_MODEL), F32)],
        compiler_params=_params(),
    )(dm_all)


def _mod_bwd_w(dm_loc, s_b, c_ctx, w_mod_b):
    def body(d_ref, s_ref, cc_ref, w_ref, gw_ref, gc_ref):
        db = d_ref[...].astype(BF16)
        gw_ref[...] = _dot_tn(s_ref[...], db)
        ds = _dot_nt(db, w_ref[...])
        _, vjp = jax.vjp(_silu, cc_ref[...])
        gc_ref[...] = jnp.broadcast_to(vjp(ds[8:9, :])[0], (8, D_MODEL))

    return pl.pallas_call(
        body, name="mod_bwd_w",
        out_shape=[jax.ShapeDtypeStruct((D_MODEL, MOD_COLS), F32), jax.ShapeDtypeStruct((8, D_MODEL), F32)],
        compiler_params=_params(),
    )(dm_loc, s_b, c_ctx, w_mod_b)


def _adamw(w, g, m, v, name):
    r, c = w.shape
    tr = _pick(r, (256, 128, 64, 32, 16, 8))
    bc1 = 1.0 - ADAM_B1 ** ADAM_STEP
    bc2 = 1.0 - ADAM_B2 ** ADAM_STEP

    def body(w_ref, g_ref, m_ref, v_ref, d_ref, nm_ref, nv_ref):
        gg = g_ref[...]
        nm = ADAM_B1 * m_ref[...] + (1.0 - ADAM_B1) * gg
        nv = ADAM_B2 * v_ref[...] + (1.0 - ADAM_B2) * (gg * gg)
        nm_ref[...] = nm
        nv_ref[...] = nv
        d_ref[...] = -ADAM_LR * ((nm / bc1) / (jnp.sqrt(nv / bc2) + ADAM_EPS) + ADAM_WD * w_ref[...])

    blk = pl.BlockSpec((tr, c), lambda i: (i, 0))
    shp = jax.ShapeDtypeStruct((r, c), F32)
    return pl.pallas_call(
        body, name=name, grid=(r // tr,), in_specs=[blk] * 4, out_specs=[blk] * 3, out_shape=[shp] * 3,
        compiler_params=_params(("parallel",)),
    )(w, g, m, v)


def _sum_slots(a, name):
    n, r, c = a.shape
    tr = _pick(r, (376, 256, 208, 128, 64, 32, 16, 8))

    def body(a_ref, o_ref):
        acc = a_ref[0].astype(F32)
        for k in range(1, n):
            acc = acc + a_ref[k].astype(F32)
        o_ref[...] = acc

    return pl.pallas_call(
        body, name=name, grid=(r // tr,),
        in_specs=[pl.BlockSpec((n, tr, c), lambda i: (0, i, 0))],
        out_specs=pl.BlockSpec((tr, c), lambda i: (i, 0)),
        out_shape=jax.ShapeDtypeStruct((r, c), F32),
        compiler_params=_params(("parallel",)),
    )(a)


def _mesh_pos():
    return lax.axis_index("x"), lax.axis_index("y"), lax.axis_index("c")


def _all_gather8(v, name):
    m_per, n = v.shape

    def body(x_ref, out_ref, send_sems, recv_sems, local_sem):
        x, y, c = _mesh_pos()
        me, sibling = (x, y, c), (x, y, 1 - c)
        chips = [(1 - x, y), (x, 1 - y), (1 - x, 1 - y)]

        def rows(px, py, pc):
            return out_ref.at[pl.ds((4 * px + 2 * py + pc) * m_per, m_per), :]

        def copy(k, block, to, src=None):
            return pltpu.make_async_remote_copy(
                src_ref=rows(*block) if src is None else src, dst_ref=rows(*block),
                send_sem=send_sems.at[k], recv_sem=recv_sems.at[k], device_id=to, device_id_type=MESH_ID)

        mine = pltpu.make_async_copy(x_ref, rows(*me), local_sem)
        mine.start()
        first = [copy(0, me, sibling, src=x_ref)]
        first += [copy(1 + j, me, (*chip, c), src=x_ref) for j, chip in enumerate(chips)]
        for cp in first:
            cp.start()
        passed = [copy(4 + j, (*chip, c), sibling) for j, chip in enumerate(chips)]
        for j, chip in enumerate(chips):
            copy(1 + j, (*chip, c), me).wait_recv()
            passed[j].start()
        copy(0, sibling, me).wait_recv()
        for j, chip in enumerate(chips):
            copy(4 + j, (*chip, 1 - c), me).wait_recv()
        for cp in first + passed:
            cp.wait_send()
        mine.wait()

    return pl.pallas_call(
        body, name=name,
        out_shape=jax.ShapeDtypeStruct((8 * m_per, n), v.dtype),
        in_specs=[pl.BlockSpec(memory_space=pltpu.VMEM)],
        out_specs=pl.BlockSpec(memory_space=pltpu.VMEM),
        scratch_shapes=[pltpu.SemaphoreType.DMA((7,)), pltpu.SemaphoreType.DMA((7,)), pltpu.SemaphoreType.DMA],
        compiler_params=_params(),
    )(v)


ANY = pl.BlockSpec(memory_space=pl.ANY)
PEER_CHIPS = lambda x, y: [(x, 1 - y), (1 - x, y), (1 - x, 1 - y)]


def _shard_region(ref, kind, k, rl, cl, r0, nr, c0, nc):
    if kind == "col":
        return ref.at[pl.ds(r0, nr), pl.ds(k * cl + c0, nc)]
    return ref.at[pl.ds(k * rl + r0, nr), pl.ds(c0, nc)]


def _place_shard(w, kind, chip, name):
    rl, cl = w.shape
    tr = _pick(rl, (256, 128, 64))
    nt = rl // tr

    def body(chip_ref, w_ref, o_ref):
        o_ref[...] = w_ref[...].astype(BF16)

    o_map = (lambda i, chip_ref: (i, chip_ref[0])) if kind == "col" else (lambda i, chip_ref: (chip_ref[0] * nt + i, 0))
    return pl.pallas_call(
        body, name=name,
        grid_spec=pltpu.PrefetchScalarGridSpec(
            num_scalar_prefetch=1, grid=(nt,),
            in_specs=[pl.BlockSpec((tr, cl), lambda i, chip_ref: (i, 0))], out_specs=pl.BlockSpec((tr, cl), o_map)),
        out_shape=jax.ShapeDtypeStruct((rl, 4 * cl) if kind == "col" else (4 * rl, cl), BF16),
        compiler_params=_params(("parallel",)),
    )(chip.reshape(1), w)


def _gather_shard_shapes(placed, kinds):
    return [(p.shape[0], p.shape[1] // 4) if k == "col" else (p.shape[0] // 4, p.shape[1]) for p, k in zip(placed, kinds)]


def _gather_chip_copies(outs, kinds, shard_shapes, send_sems, recv_sems, with_arrivals=True):
    x, y, c = _mesh_pos()
    me = 2 * x + y
    sends, arrivals = [], []
    for a in range(len(outs)):
        rl, cl = shard_shapes[a]
        rh = rl // 2
        reg = functools.partial(_shard_region, outs[a], kinds[a], rl=rl, cl=cl, r0=c * rh, nr=rh, c0=0, nc=cl)
        for j, (px, py) in enumerate(PEER_CHIPS(x, y)):
            to = dict(send_sem=send_sems.at[a, j], recv_sem=recv_sems.at[a, j], device_id=(px, py, c),
                      device_id_type=MESH_ID)
            sends.append(pltpu.make_async_remote_copy(src_ref=reg(k=me), dst_ref=reg(k=me), **to))
            if with_arrivals:
                got = reg(k=2 * px + py)
                arrivals.append(pltpu.make_async_remote_copy(src_ref=got, dst_ref=got, **to))
    return sends, arrivals


def _gather_sibling_copies(outs, kinds, shard_shapes, send_sems, recv_sems):
    x, y, c = _mesh_pos()
    forwards, arrivals = [], []
    for a in range(len(outs)):
        rl, cl = shard_shapes[a]
        rh = rl // 2
        for j, (px, py) in enumerate(PEER_CHIPS(x, y)):
            to = dict(send_sem=send_sems.at[a, j], recv_sem=recv_sems.at[a, j], device_id=(x, y, 1 - c),
                      device_id_type=MESH_ID)
            reg = functools.partial(_shard_region, outs[a], kinds[a], k=2 * px + py, rl=rl, cl=cl, nr=rh, c0=0, nc=cl)
            forwards.append(pltpu.make_async_remote_copy(src_ref=reg(r0=c * rh), dst_ref=reg(r0=c * rh), **to))
            arrivals.append(pltpu.make_async_remote_copy(src_ref=reg(r0=(1 - c) * rh), dst_ref=reg(r0=(1 - c) * rh), **to))
    return forwards, arrivals


def _gather_weights(placed, kinds):
    n = len(placed)
    shard_shapes = _gather_shard_shapes(placed, kinds)

    def body(*refs):
        outs = refs[n:2 * n]
        ici_send, ici_recv, sib_send, sib_recv = refs[2 * n:]
        sends, arrivals = _gather_chip_copies(outs, kinds, shard_shapes, ici_send, ici_recv)
        for cp in sends:
            cp.start()
        forwards, from_sibling = _gather_sibling_copies(outs, kinds, shard_shapes, sib_send, sib_recv)
        for cp, fwd in zip(arrivals, forwards):
            cp.wait_recv()
            fwd.start()
        for cp in from_sibling:
            cp.wait_recv()
        for cp in sends + forwards:
            cp.wait_send()

    return pl.pallas_call(
        body, name="gather_weights",
        out_shape=[jax.ShapeDtypeStruct(p.shape, p.dtype) for p in placed],
        in_specs=[ANY] * n, out_specs=[ANY] * n, input_output_aliases={a: a for a in range(n)},
        scratch_shapes=[pltpu.SemaphoreType.DMA((n, 3))] * 4,
        compiler_params=_params(),
    )(*placed)


def _gather_sibling(placed, kinds):
    n = len(placed)
    shard_shapes = _gather_shard_shapes(placed, kinds)

    def body(*refs):
        forwards, from_sibling = _gather_sibling_copies(refs[n:2 * n], kinds, shard_shapes, *refs[2 * n:])
        for cp in forwards:
            cp.start()
        for cp in from_sibling:
            cp.wait_recv()
        for cp in forwards:
            cp.wait_send()

    return pl.pallas_call(
        body, name="gather_sibling",
        out_shape=[jax.ShapeDtypeStruct(p.shape, p.dtype) for p in placed],
        in_specs=[ANY] * n, out_specs=[ANY] * n, input_output_aliases={a: a for a in range(n)},
        scratch_shapes=[pltpu.SemaphoreType.DMA((n, 3))] * 2,
        compiler_params=_params(),
    )(*placed)


def _half(kind, r, c):
    return (r // 2, c) if kind == "col" else (r, c // 2)


def _half_of(ref, kind, which):
    r, c = ref.shape
    hr, hc = _half(kind, r, c)
    return ref.at[pl.ds(which * hr, hr), :] if kind == "col" else ref.at[:, pl.ds(which * hc, hc)]


def _rs_sibling(grads, kinds, name):
    n = len(grads)

    def body(*refs):
        srcs, dsts = refs[:n], refs[n:2 * n]
        send_sems, recv_sems = refs[2 * n:]
        x, y, c = _mesh_pos()
        cps = [pltpu.make_async_remote_copy(src_ref=_half_of(srcs[a], kinds[a], 1 - c), dst_ref=dsts[a],
                                            send_sem=send_sems.at[a], recv_sem=recv_sems.at[a],
                                            device_id=(x, y, 1 - c), device_id_type=MESH_ID) for a in range(n)]
        for cp in cps:
            cp.start()
        for cp in cps:
            cp.wait()

    return pl.pallas_call(
        body, name=name,
        out_shape=[jax.ShapeDtypeStruct(_half(k, *g.shape), g.dtype) for g, k in zip(grads, kinds)],
        in_specs=[ANY] * n, out_specs=[ANY] * n,
        scratch_shapes=[pltpu.SemaphoreType.DMA((n,)), pltpu.SemaphoreType.DMA((n,))],
        compiler_params=_params(),
    )(*grads)


def _pair_sum(gf, rv, kind, ci, name):
    r, c = rv.shape
    tr = _pick(r, (128, 64, 32, 16, 8))
    nt = r // tr

    def body(ci_ref, g_ref, r_ref, o_ref):
        o_ref[...] = (g_ref[...] + r_ref[...]).astype(BF16)

    g_map = (lambda i, ci_ref: (ci_ref[0] * nt + i, 0)) if kind == "col" else (lambda i, ci_ref: (i, ci_ref[0]))
    blk = pl.BlockSpec((tr, c), lambda i, ci_ref: (i, 0))
    return pl.pallas_call(
        body, name=name,
        grid_spec=pltpu.PrefetchScalarGridSpec(num_scalar_prefetch=1, grid=(nt,),
                                               in_specs=[pl.BlockSpec((tr, c), g_map), blk], out_specs=blk),
        out_shape=jax.ShapeDtypeStruct((r, c), BF16),
        compiler_params=_params(("parallel",)),
    )(ci.reshape(1), gf, rv)


def _rs_slot_shapes(pairs, kinds):
    return [(p.shape[0], p.shape[1] // 4) if k == "col" else (p.shape[0] // 4, p.shape[1]) for p, k in zip(pairs, kinds)]


def _rs_chip_copies(srcs, dsts, kinds, shapes, send_sems, recv_sems, with_arrivals=True):
    x, y, c = _mesh_pos()
    me = 2 * x + y
    sends, arrivals = [], []
    for a in range(len(srcs)):
        rl, cl = shapes[a]
        reg = functools.partial(_shard_region, srcs[a], kinds[a], rl=rl, cl=cl, r0=0, nr=rl, c0=0, nc=cl)
        for j, (px, py) in enumerate(PEER_CHIPS(x, y)):
            to = dict(send_sem=send_sems.at[a, j], recv_sem=recv_sems.at[a, j], device_id=(px, py, c),
                      device_id_type=MESH_ID)
            sends.append(pltpu.make_async_remote_copy(src_ref=reg(k=2 * px + py), dst_ref=dsts[a].at[me], **to))
            if with_arrivals:
                slot = dsts[a].at[2 * px + py]
                arrivals.append(pltpu.make_async_remote_copy(src_ref=slot, dst_ref=slot, **to))
    return sends, arrivals


def _rs_chips(pairs, kinds):
    n = len(pairs)
    shapes = _rs_slot_shapes(pairs, kinds)

    def body(*refs):
        sends, arrivals = _rs_chip_copies(refs[:n], refs[n:2 * n], kinds, shapes, *refs[2 * n:])
        for cp in sends:
            cp.start()
        for cp in arrivals:
            cp.wait_recv()
        for cp in sends:
            cp.wait_send()

    return pl.pallas_call(
        body, name="rs_chips",
        out_shape=[jax.ShapeDtypeStruct((4,) + s, p.dtype) for s, p in zip(shapes, pairs)],
        in_specs=[ANY] * n, out_specs=[ANY] * n,
        scratch_shapes=[pltpu.SemaphoreType.DMA((n, 3)), pltpu.SemaphoreType.DMA((n, 3))],
        compiler_params=_params(),
    )(*pairs)


def _sum_chips(pair, got, kind, pos, name):
    _, r, c = got.shape
    tr = _pick(r, (256, 128, 64, 32, 16))
    nt = r // tr

    def body(pos_ref, own_ref, g1_ref, g2_ref, g3_ref, o_ref):
        o_ref[...] = ((own_ref[...].astype(F32) + g1_ref[0].astype(F32)) + g2_ref[0].astype(F32)) + g3_ref[0].astype(F32)

    if kind == "col":
        own_map = lambda i, p: (i, p[1])
        out_map = lambda i, p: (p[0] * nt + i, 0)
        out_shape = (2 * r, c)
    else:
        own_map = lambda i, p: (p[1] * nt + i, 0)
        out_map = lambda i, p: (i, p[0])
        out_shape = (r, 2 * c)
    peer = lambda m: pl.BlockSpec((1, tr, c), lambda i, p: (p[1] ^ m, i, 0))
    return pl.pallas_call(
        body, name=name,
        grid_spec=pltpu.PrefetchScalarGridSpec(
            num_scalar_prefetch=1, grid=(nt,),
            in_specs=[pl.BlockSpec((tr, c), own_map), peer(1), peer(2), peer(3)],
            out_specs=pl.BlockSpec((tr, c), out_map)),
        out_shape=jax.ShapeDtypeStruct(out_shape, F32),
        compiler_params=_params(("parallel",)),
    )(pos, pair, got, got, got)


def _rs_back(halves, kinds):
    n = len(halves)

    def body(*refs):
        outs = refs[n:2 * n]
        send_sems, recv_sems = refs[2 * n:]
        x, y, c = _mesh_pos()
        cps = []
        for a in range(n):
            mine = _half_of(outs[a], kinds[a], c)
            cps.append(pltpu.make_async_remote_copy(src_ref=mine, dst_ref=mine, send_sem=send_sems.at[a],
                                                    recv_sem=recv_sems.at[a], device_id=(x, y, 1 - c),
                                                    device_id_type=MESH_ID))
            cps[-1].start()
        for a in range(n):
            other = _half_of(outs[a], kinds[a], 1 - c)
            pltpu.make_async_remote_copy(src_ref=other, dst_ref=other, send_sem=send_sems.at[a],
                                         recv_sem=recv_sems.at[a], device_id=(x, y, 1 - c),
                                         device_id_type=MESH_ID).wait_recv()
        for cp in cps:
            cp.wait_send()

    return pl.pallas_call(
        body, name="rs_back",
        out_shape=[jax.ShapeDtypeStruct(h.shape, h.dtype) for h in halves],
        in_specs=[ANY] * n, out_specs=[ANY] * n, input_output_aliases={a: a for a in range(n)},
        scratch_shapes=[pltpu.SemaphoreType.DMA((n,)), pltpu.SemaphoreType.DMA((n,))],
        compiler_params=_params(),
    )(*halves)


def _rope_tables(l, lc):
    rows = l // GRID_W
    n_freq = RET_DH // 4
    inv_freq = ROPE_THETA ** (-jnp.arange(n_freq, dtype=F32) / n_freq)
    sign = jnp.tile(jnp.array([-1.0, 1.0], F32), n_freq)

    def half(n):
        ang = jnp.repeat(jnp.arange(n, dtype=F32)[:, None] * inv_freq, 2, axis=-1)
        return jnp.cos(ang), jnp.sin(ang) * sign

    (cr, sr), (cc, sc) = half(rows), half(GRID_W)
    grid = lambda r, c: jnp.concatenate([jnp.repeat(r, GRID_W, axis=0), jnp.tile(c, (rows, 1))], axis=-1)
    cos_t = jnp.concatenate([jnp.ones((lc, RET_DH), F32), grid(cr, cc)], axis=0)
    sin_t = jnp.concatenate([jnp.zeros((lc, RET_DH), F32), grid(sr, sc)], axis=0)
    return cos_t, sin_t


def _s5_pack(a):
    blk = lambda t: t.reshape(1, S5_NB, 128, S5_STATE)
    lre = jnp.stack([a["s5_lambda_re_f"][0], a["s5_lambda_re_b"][0]]).reshape(2, S5_NB, 8, S5_STATE)
    lim = jnp.stack([a["s5_lambda_im_f"][0], a["s5_lambda_im_b"][0]]).reshape(2, S5_NB, 8, S5_STATE)
    lst = jnp.stack([a["s5_log_step_f"][0], a["s5_log_step_b"][0]]).reshape(2, S5_NB, 8, 1)
    b_re = blk(a["s5_b_re"][0].transpose(0, 2, 1))
    b_im = blk(a["s5_b_im"][0].transpose(0, 2, 1))
    return (lre, lim, lst, b_re, b_im, blk(a["s5_c_re"][0]), blk(a["s5_c_im"][0]),
            a["s5_d"].reshape(1, S5_NB, 1, 128))


def _s5_unpack(g):
    glre, glim, glst, gbre, gbim, gcre, gcim, gd = g
    unb = lambda t: t.reshape(S5_GROUPS, S5_GROUP, S5_STATE).transpose(0, 2, 1)[None]
    return {
        "s5_lambda_re_f": glre[0].reshape(1, S5_GROUPS, S5_STATE), "s5_lambda_re_b": glre[1].reshape(1, S5_GROUPS, S5_STATE),
        "s5_lambda_im_f": glim[0].reshape(1, S5_GROUPS, S5_STATE), "s5_lambda_im_b": glim[1].reshape(1, S5_GROUPS, S5_STATE),
        "s5_log_step_f": glst[0].reshape(1, S5_GROUPS), "s5_log_step_b": glst[1].reshape(1, S5_GROUPS),
        "s5_b_re": unb(gbre), "s5_b_im": unb(gbim),
        "s5_c_re": gcre.reshape(1, S5_GROUPS, S5_GROUP, S5_STATE), "s5_c_im": gcim.reshape(1, S5_GROUPS, S5_GROUP, S5_STATE),
        "s5_d": gd.reshape(1, S5_WIDTH),
    }


def _local_step(a, wb, late, mx, mc, conv_w, ci):
    x, ctx, tgt = a["x"][0], a["ctx"][0], a["loss_target"][0]
    l, lc = x.shape[0], ctx.shape[0]
    la = l + lc
    nct, ncc, nrc, cn = lc // TOK_TILE, lc // S5_T, lc // RET_CHUNK, la // S5_T
    n1w, n2w, fnw = a["norm1_w"], a["norm2_w"], a["final_norm_w"].reshape(1, D_MODEL)
    conv_b, b_glu = a["conv_b"], a["s5_b_glu"]
    ld2 = jnp.concatenate([a["ret_log_decay_f"], a["ret_log_decay_b"]], axis=0)
    mod4 = jnp.concatenate([mc[0:2], mx[0:2]], axis=0)
    mod3 = mx[2:5]
    gate5 = mx[5:6]
    cos_t, sin_t = _rope_tables(l, lc)
    s5p = _s5_pack(a)

    p_all, h1b, u_b, w_out_p, w_down_p = _norm_inproj(x, ctx, n1w, mod4, wb["w_in"], [late[0], late[2]],
                                                 (LATE_KINDS[0], LATE_KINDS[2]))
    p3 = _s5_rows(u_b)
    gg, xw, yw, a16 = _s5_gen(*s5p)
    sloc = _s5_to_states(p3, xw, "s5_state")
    a16s = a16.transpose(1, 0, 2, 3).reshape(2, S5_GROUPS, 128)
    hs = _s5_scan(sloc, a16s, ncc)
    y_all = _s5_from_states(p3, gg, hs, yw, False, "s5_out").reshape(la, S5_WIDTH)
    tabs = _ret_tables(ld2)
    of, ob, ssf, ssb, w_up_p = _ret_scan(p_all, cos_t, sin_t, tabs, nrc, [late[1]], (LATE_KINDS[1],))
    wb = {**wb, **dict(zip(LATE_NAMES, _gather_sibling([w_out_p, w_up_p, w_down_p], LATE_KINDS)))}
    x1, mix, h2b, up, mixb, y_ret = _outproj_up(x, y_all, of, ob, p_all, wb["s5_w_glu"], b_glu, wb["w_out"],
                                                     mod3, n2w, wb["w_up"], nct)
    act, dx2, ddn, dact, acc_f = _ffn_loss(up, x1, conv_w, conv_b, wb["w_down"], gate5, fnw, tgt)

    g = {}
    g["w_down"] = _mm_tn(act, ddn, name="gw_down")
    dup, acc_c = _convglu_bwd(up, dact, conv_w, conv_b)
    g["w_up"] = _mm_tn(h2b, dup, name="gw_up")
    first = [g[n] for n in FIRST_GRADS]
    first_pairs = [_pair_sum(gf, rv, k, ci, "rs_pair_" + n)
                   for gf, rv, k, n in zip(first, _rs_sibling(first, FIRST_KINDS, "rs_sibling_first"), FIRST_KINDS, FIRST_GRADS)]
    dx1, dmixb, acc_2, dy_s5, dy_ret, dg, g["s5_w_glu"], g["s5_b_glu"], *first_got = _up_bwd(
        dup, wb["w_up"], wb["w_out"], x1, dx2, mix, mod3, n2w, y_all, y_ret, p_all, wb["s5_w_glu"], b_glu, nct,
        first_pairs, FIRST_KINDS)
    dy_s5 = jnp.concatenate([jnp.zeros((lc, S5_WIDTH), BF16), dy_s5], axis=0)
    g["w_out"] = _mm_tn(mixb, dmixb, name="gw_out")

    dy3 = _s5_rows(dy_s5)
    e = _s5_to_states(dy3, yw, "s5_bwd_h")
    ds, da16 = _s5_scan_bwd(e, hs, a16s, ncc)
    du = _s5_from_states(dy3, gg, ds, xw, True, "s5_bwd_u").reshape(la, S5_WIDTH)
    dkb = _s5_bwd_kb(p3, dy3)
    dwst = _s5_bwd_w(p3, ds, "s5_bwd_wst")
    dwout = _s5_bwd_w(dy3, hs, "s5_bwd_wout")
    da16p = da16.reshape(2, S5_NB, 8, 128).transpose(1, 0, 2, 3)
    g.update(_s5_unpack(_s5_gen_bwd(*s5p, dkb, dwst, dwout, da16p)))

    dqf, dkf, dvf, dqb, dkb_, dvb, dld = _ret_scan_bwd(p_all, cos_t, sin_t, tabs, ssf, ssb, dy_ret, nrc)
    g["ret_log_decay_f"] = dld[0, :, 0, 0].reshape(1, RET_HEADS)
    g["ret_log_decay_b"] = dld[1, :, 0, 0].reshape(1, RET_HEADS)
    dp, grad_x, acc_1 = _in_bwd(dqf, dkf, dvf, dqb, dkb_, dvb, du, dg, cos_t, sin_t, wb["w_in"], x, ctx, n1w, mod4, dx1)
    g["w_in"] = _mm_tn(h1b, dp, name="gw_in")

    g["norm1_w"], g["norm2_w"], g["final_norm_w"] = acc_1[0:1], acc_2[0:1], acc_f[0]
    g["conv_w"], g["conv_b"] = acc_c[0:3], acc_c[3:4]
    zero = jnp.zeros((1, D_MODEL), F32)
    dmx = jnp.concatenate([acc_1[3:5], acc_2[1:2], acc_2[2:4], acc_f[1:2]], axis=0)
    dmc = jnp.concatenate([acc_1[1:3], zero, zero, zero, zero], axis=0)
    return acc_f[2, 0], grad_x, g, dmx, dmc, first_pairs, first_got


WEIGHT_NAMES = ("c_ctx", "w_mod", "b_mod", "norm1_w", "w_in", "s5_lambda_re_f", "s5_lambda_im_f", "s5_log_step_f",
                "s5_lambda_re_b", "s5_lambda_im_b", "s5_log_step_b", "s5_b_re", "s5_b_im", "s5_c_re", "s5_c_im",
                "s5_d", "s5_w_glu", "s5_b_glu", "ret_log_decay_f", "ret_log_decay_b", "w_out", "norm2_w", "w_up",
                "conv_w", "conv_b", "w_down", "final_norm_w")
BIG_NAMES = ("w_in", "w_out", "w_up", "w_down", "s5_w_glu")
BIG_KINDS = ("col", "row", "col", "row", "row")
EARLY_NAMES, EARLY_KINDS = ("w_in", "s5_w_glu"), ("col", "row")
LATE_NAMES, LATE_KINDS = ("w_out", "w_up", "w_down"), ("row", "col", "row")
FIRST_GRADS, FIRST_KINDS = ("w_down", "w_up"), ("row", "col")
LAST_GRADS, LAST_KINDS = ("w_in", "w_out", "s5_w_glu"), ("col", "row", "row")
SMALL_NAMES = ("norm1_w", "norm2_w", "final_norm_w", "conv_b", "conv_w", "s5_lambda_re_f", "s5_lambda_im_f",
               "s5_log_step_f", "s5_lambda_re_b", "s5_lambda_im_b", "s5_log_step_b", "s5_b_re", "s5_b_im", "s5_c_re",
               "s5_c_im", "s5_d", "s5_b_glu", "ret_log_decay_f", "ret_log_decay_b")
ROW = 1024
N_CHIPS = 4


def _pack_rows(parts):
    flat = jnp.concatenate([p.reshape(-1) for p in parts])
    n = flat.shape[0]
    rows = -(-n // (8 * ROW)) * 8
    return jnp.pad(flat, (0, rows * ROW - n)).reshape(rows, ROW)


def _unpack_rows(packed, shapes):
    flat = packed.reshape(-1)
    out, off = [], 0
    for s in shapes:
        n = math.prod(s)
        out.append(flat[off:off + n].reshape(s))
        off += n
    return out


def _step(a):
    xi, yi, ci = _mesh_pos()
    chip = 2 * xi + yi
    dev = 2 * chip + ci

    cw_loc = a["conv_w"].reshape(-1)
    small_in = jnp.concatenate([a["c"].reshape(-1), jnp.pad(cw_loc, (0, 24 * 128 - cw_loc.shape[0]))]).reshape(32, 128)
    sg = _all_gather8(small_in, "gather_cond").reshape(8, 32, 128)
    c_all = sg[:, 0:8].reshape(8, D_MODEL)
    conv_w = sg[0::2, 8:32].reshape(N_CHIPS, -1)[:, :cw_loc.shape[0]].reshape(N_CHIPS, 3, -1)
    conv_w = conv_w.transpose(1, 0, 2).reshape(3, D_FF)

    placed = {n: _place_shard(a[n][0], k, chip, "place_" + n) for n, k in zip(BIG_NAMES, BIG_KINDS)}
    wb = dict(zip(EARLY_NAMES, _gather_weights([placed[n] for n in EARLY_NAMES], EARLY_KINDS)))
    late = [placed[n] for n in LATE_NAMES]

    w_mod_b = a["w_mod"][0].astype(BF16)
    c_ctx = a["c_ctx"].reshape(1, D_MODEL)
    b_loc = lax.dynamic_slice_in_dim(a["b_mod"], chip * MOD_COLS, MOD_COLS, 1)
    m_loc, s_b = _mod_fwd(c_all, c_ctx, w_mod_b, b_loc)
    mg = _all_gather8(m_loc, "gather_mod").reshape(8, MOD_ROWS, MOD_COLS)
    m_full = mg[0::2].transpose(1, 0, 2).reshape(MOD_ROWS, 6 * D_MODEL)
    mx = lax.dynamic_slice_in_dim(m_full, dev, 1, 0).reshape(6, D_MODEL)
    mc = m_full[8].reshape(6, D_MODEL)

    loss_part, grad_x, g, dmx, dmc, first_pairs, first_got = _local_step(a, wb, late, mx, mc, conv_w, ci)
    loss = lax.psum(loss_part, ("x", "y", "c"))

    dm_pair = jnp.concatenate([dmx.reshape(1, -1), dmc.reshape(1, -1), jnp.zeros((6, 6 * D_MODEL), F32)], axis=0)
    dm_all = _all_gather8(dm_pair, "gather_dmod").reshape(8, 8, 6 * D_MODEL)
    dm16, gb_mod = _mod_bwd_sum(dm_all)
    dm_loc = lax.dynamic_slice_in_dim(dm16, chip * MOD_COLS, MOD_COLS, 1)
    gw_mod, gcc = _mod_bwd_w(dm_loc, s_b, c_ctx, w_mod_b)

    small_parts = [g[n] for n in SMALL_NAMES] + [gcc[0]]
    small_shapes = [p.shape for p in small_parts]
    sp = _pack_rows(small_parts)
    tot = _sum_slots(_all_gather8(sp, "gather_small_grads").reshape(8, sp.shape[0], ROW), "sum_small_grads")
    small = dict(zip(SMALL_NAMES + ("c_ctx",), _unpack_rows(tot, small_shapes)))
    grads = {n: small[n].reshape(a[n].shape) for n in SMALL_NAMES if n != "conv_w"}
    grads["c_ctx"] = (0.5 * small["c_ctx"]).reshape(a["c_ctx"].shape)
    grads["conv_w"] = lax.dynamic_slice_in_dim(small["conv_w"], chip * (D_FF // N_CHIPS), D_FF // N_CHIPS, 1)[None]
    grads["b_mod"] = gb_mod
    grads["w_mod"] = gw_mod[None]

    last = [g[n] for n in LAST_GRADS]
    last_pairs = [_pair_sum(gf, rv, k, ci, "rs_pair_" + n)
                  for gf, rv, k, n in zip(last, _rs_sibling(last, LAST_KINDS, "rs_sibling_last"), LAST_KINDS, LAST_GRADS)]
    last_got = _rs_chips(last_pairs, LAST_KINDS)
    pos = jnp.stack([ci, chip])
    order = FIRST_GRADS + LAST_GRADS
    order_kinds = FIRST_KINDS + LAST_KINDS
    halves = [_sum_chips(p, t, k, pos, "rs_sum_" + n)
              for p, t, k, n in zip(first_pairs + last_pairs, list(first_got) + list(last_got), order_kinds, order)]
    for n, t in zip(order, _rs_back(halves, order_kinds)):
        grads[n] = t[None]

    delta, new_m, new_v = {}, {}, {}
    for n in BIG_NAMES + ("w_mod",):
        for dst, t in zip((delta, new_m, new_v), _adamw(a[n][0], grads[n][0], a["m_" + n][0], a["v_" + n][0], "adamw_" + n)):
            dst[n] = t[None]
    rest = [n for n in WEIGHT_NAMES if n not in BIG_NAMES and n != "w_mod"]
    shapes = [a[n].shape for n in rest]
    pr = lambda pre: _pack_rows([a[pre + n] for n in rest])
    for dst, t in zip((delta, new_m, new_v),
                      _adamw(pr(""), _pack_rows([grads[n] for n in rest]), pr("m_"), pr("v_"), "adamw_small")):
        dst.update(zip(rest, _unpack_rows(t, shapes)))

    return (loss, grad_x[None], *[grads[n] for n in WEIGHT_NAMES], *[delta[n] for n in WEIGHT_NAMES],
            *[new_m[n] for n in WEIGHT_NAMES], *[new_v[n] for n in WEIGHT_NAMES])


def kernel(x, c, ctx, c_ctx, w_mod, b_mod, norm1_w, w_in, s5_lambda_re_f, s5_lambda_im_f, s5_log_step_f, s5_lambda_re_b, s5_lambda_im_b, s5_log_step_b, s5_b_re, s5_b_im, s5_c_re, s5_c_im, s5_d, s5_w_glu, s5_b_glu, ret_log_decay_f, ret_log_decay_b, w_out, norm2_w, w_up, conv_w, conv_b, w_down, final_norm_w, loss_target, m_c_ctx, m_w_mod, m_b_mod, m_norm1_w, m_w_in, m_s5_lambda_re_f, m_s5_lambda_im_f, m_s5_log_step_f, m_s5_lambda_re_b, m_s5_lambda_im_b, m_s5_log_step_b, m_s5_b_re, m_s5_b_im, m_s5_c_re, m_s5_c_im, m_s5_d, m_s5_w_glu, m_s5_b_glu, m_ret_log_decay_f, m_ret_log_decay_b, m_w_out, m_norm2_w, m_w_up, m_conv_w, m_conv_b, m_w_down, m_final_norm_w, v_c_ctx, v_w_mod, v_b_mod, v_norm1_w, v_w_in, v_s5_lambda_re_f, v_s5_lambda_im_f, v_s5_log_step_f, v_s5_lambda_re_b, v_s5_lambda_im_b, v_s5_log_step_b, v_s5_b_re, v_s5_b_im, v_s5_c_re, v_s5_c_im, v_s5_d, v_s5_w_glu, v_s5_b_glu, v_ret_log_decay_f, v_ret_log_decay_b, v_w_out, v_norm2_w, v_w_up, v_conv_w, v_conv_b, v_w_down, v_final_norm_w):
    return _step(dict(locals()))
```

```python
import functools
import math

import jax
import jax.numpy as jnp
from jax import lax
from jax.experimental import pallas as pl
from jax.experimental.pallas import tpu as pltpu

F32 = jnp.float32
BF16 = jnp.bfloat16

D_MODEL = 1024
S5_WIDTH = 512
S5_GROUPS = 32
S5_GROUP = 16
S5_STATE = 64
RET_WIDTH = 512
RET_HEADS = 4
RET_DH = 128
RET_CHUNK = 256
GRID_W = 64
ROPE_THETA = 10000.0
D_FF = 2816
NORM_EPS = 1e-6
IN_COLS = S5_WIDTH + 4 * RET_WIDTH

S5_T = 16
S5_NB = 4
S5_BW = S5_T * 128
S5_SW = 8 * 2 * S5_STATE

ADAM_LR, ADAM_B1, ADAM_B2, ADAM_EPS, ADAM_WD, ADAM_STEP = 0.001, 0.9, 0.999, 1e-08, 0.01, 10

VMEM_LIMIT = 56 * 1024 * 1024
MM_TN_VMEM = 40 * 1024 * 1024
MESH_ID = pl.DeviceIdType.MESH


def _params(sem=None):
    return pltpu.CompilerParams(dimension_semantics=sem, vmem_limit_bytes=VMEM_LIMIT)


def _full(shape):
    n = len(shape)
    return pl.BlockSpec(shape, lambda *_: (0,) * n)


def _dot(a, b):
    return jnp.dot(a, b, preferred_element_type=F32)


def _dot_nt(a, b):
    return lax.dot_general(a, b, (((1,), (1,)), ((), ())), preferred_element_type=F32)


def _dot_tn(a, b):
    return lax.dot_general(a, b, (((0,), (0,)), ((), ())), preferred_element_type=F32)


def _dot_hi(a, b):
    return jnp.dot(a, b, preferred_element_type=F32, precision=lax.Precision.HIGHEST)


def _dot_nt_hi(a, b):
    return lax.dot_general(a, b, (((1,), (1,)), ((), ())), preferred_element_type=F32,
                           precision=lax.Precision.HIGHEST)


def _gelu(x):
    return 0.5 * x * (1.0 + jnp.tanh(0.7978845608028654 * (x + 0.044715 * (x * x * x))))


def _gelu_and_grad(x):
    c, ca = 0.7978845608028654, 0.7978845608028654 * 0.044715
    x2 = x * x
    t = jnp.tanh(x * (c + ca * x2))
    h = 0.5 * x
    return h + h * t, 0.5 + 0.5 * t + h * (1.0 - t * t) * (c + 3.0 * ca * x2)


def _sigmoid(x):
    return 1.0 / (1.0 + jnp.exp(-x))


def _silu(x):
    return x * _sigmoid(x)


def _rms_mod(x, nw, sh, sc):
    r = lax.rsqrt(jnp.mean(x * x, axis=-1, keepdims=True) + NORM_EPS)
    return (x * r * nw) * (1.0 + sc) + sh


def _rms(x, nw):
    r = lax.rsqrt(jnp.mean(x * x, axis=-1, keepdims=True) + NORM_EPS)
    return x * r * nw


def _head_norm_gate(y, g):
    mu = jnp.mean(y, axis=-1, keepdims=True)
    yc = y - mu
    var = jnp.mean(yc * yc, axis=-1, keepdims=True)
    return _silu(g) * (yc * lax.rsqrt(var + NORM_EPS))


def _swap_pairs(t):
    lane = lax.broadcasted_iota(jnp.int32, t.shape, 1)
    return jnp.where(lane % 2 == 0, pltpu.roll(t, RET_DH - 1, 1), pltpu.roll(t, 1, 1))


def _rope(t, cos_t, sin_t):
    return t * cos_t + _swap_pairs(t) * sin_t


def _rope_t(dt, cos_t, sin_t):
    return dt * cos_t + _swap_pairs(dt * sin_t)


def _pick(n, prefs):
    for p in prefs:
        if n % p == 0:
            return p
    return n


def _mm_tn(a, b, *, name):
    m, k = a.shape
    n = b.shape[1]
    tn = _pick(n, (1408, 1024, 1280, 512))
    fits = lambda t: 2 * (2 * t * k + 2 * t * tn + 4 * k * tn) <= MM_TN_VMEM
    tm = _pick(m, [t for t in (2816, 2048, 1024, 768, 512, 256) if fits(t)] + [128])

    def body(a_ref, b_ref, o_ref):
        @pl.when(pl.program_id(1) == 0)
        def _():
            o_ref[...] = jnp.zeros_like(o_ref)
        o_ref[...] += _dot_tn(a_ref[...], b_ref[...])

    return pl.pallas_call(
        body, name=name, grid=(n // tn, m // tm),
        in_specs=[pl.BlockSpec((tm, k), lambda j, i: (i, 0)), pl.BlockSpec((tm, tn), lambda j, i: (i, j))],
        out_specs=pl.BlockSpec((k, tn), lambda j, i: (0, j)),
        out_shape=jax.ShapeDtypeStruct((k, n), F32),
        compiler_params=_params(("parallel", "arbitrary")),
    )(a, b)


TOK_TILE = 256


def _behind(step, last, copies):
    @pl.when(step == 0)
    def _():
        for cp in copies(with_arrivals=False)[0]:
            cp.start()

    @pl.when(step == last)
    def _():
        sends, arrivals = copies()
        for cp in arrivals:
            cp.wait_recv()
        for cp in sends:
            cp.wait_send()


def _s5_put_rows(rows_ref, scr, val):
    nchunk = scr.shape[0]
    for c in range(nchunk):
        scr[c] = val[S5_T * c:S5_T * (c + 1), :]
    for b in range(S5_NB):
        for j in range(S5_T):
            rows_ref[b, :, 128 * j:128 * (j + 1)] = scr[:, j, 128 * b:128 * (b + 1)].astype(BF16)


def _norm_inproj(x, ctx, n1w, mod4, w_in_b, placed, kinds):
    l, lc = x.shape[0], ctx.shape[0]
    tm = TOK_TILE
    nct = lc // tm
    la = l + lc
    n = len(placed)
    shard_shapes = _gather_shard_shapes(placed, kinds)

    def body(x_ref, c_ref, nw_ref, mod_ref, w_ref, *rest):
        p_ref, h_ref, u_ref = rest[n:n + 3]
        send_sems, recv_sems, u_scr = rest[2 * n + 3:]
        _behind(pl.program_id(0), la // tm - 1,
                functools.partial(_gather_chip_copies, rest[n + 3:2 * n + 3], kinds, shard_shapes, send_sems, recv_sems))
        is_ctx = pl.program_id(0) < nct
        xt = jnp.where(is_ctx, c_ref[...], x_ref[...])
        sh = jnp.where(is_ctx, mod_ref[0:1, :], mod_ref[2:3, :])
        sc = jnp.where(is_ctx, mod_ref[1:2, :], mod_ref[3:4, :])
        hb = _rms_mod(xt, nw_ref[...], sh, sc).astype(BF16)
        h_ref[...] = hb
        p = _dot(hb, w_ref[...])
        p_ref[...] = p
        _s5_put_rows(u_ref, u_scr, p[:, 0:S5_WIDTH])

    return pl.pallas_call(
        body, name="norm_inproj", grid=(la // tm,),
        in_specs=[pl.BlockSpec((tm, D_MODEL), lambda i: (jnp.maximum(i - nct, 0), 0)),
                  pl.BlockSpec((tm, D_MODEL), lambda i: (jnp.minimum(i, nct - 1), 0)),
                  _full((1, D_MODEL)), _full((4, D_MODEL)), _full((D_MODEL, IN_COLS))] + [ANY] * n,
        out_specs=[pl.BlockSpec((tm, IN_COLS), lambda i: (i, 0)), pl.BlockSpec((tm, D_MODEL), lambda i: (i, 0)),
                   pl.BlockSpec((S5_NB, tm // S5_T, S5_BW), lambda i: (0, i, 0))] + [ANY] * n,
        out_shape=[jax.ShapeDtypeStruct((la, IN_COLS), F32), jax.ShapeDtypeStruct((la, D_MODEL), BF16),
                   jax.ShapeDtypeStruct((S5_NB, la // S5_T, S5_BW), BF16)]
        + [jax.ShapeDtypeStruct(p.shape, p.dtype) for p in placed],
        input_output_aliases={5 + a: 3 + a for a in range(n)},
        scratch_shapes=[pltpu.SemaphoreType.DMA((n, 3)), pltpu.SemaphoreType.DMA((n, 3)),
                        pltpu.VMEM((tm // S5_T, S5_T, S5_WIDTH), F32)],
        compiler_params=_params(("arbitrary",)),
    )(x, ctx, n1w, mod4, w_in_b, *placed)


def _iota2(shape, dim):
    return lax.broadcasted_iota(jnp.int32, shape, dim)


def _group_mask(rows, cols, row_div, col_div):
    return jnp.where(_iota2((rows, cols), 0) // row_div == _iota2((rows, cols), 1) // col_div, 1.0, 0.0).astype(F32)


def _s5_gen_dir(lre, lim, lst, b_re, b_im, c_re, c_im):
    step = jnp.exp(lst)
    mag = jnp.exp(lre * step)
    ar = mag * jnp.cos(lim * step)
    ai = mag * jnp.sin(lim * step)
    den = lre * lre + lim * lim
    xr = ar - 1.0
    cr = (xr * lre + ai * lim) / den
    ci = (ai * lre - xr * lim) / den
    rexp = _group_mask(128, 8, S5_GROUP, 1)
    are, aie = _dot_hi(rexp, ar), _dot_hi(rexp, ai)
    cre, cie = _dot_hi(rexp, cr), _dot_hi(rexp, ci)
    bbr = cre * b_re - cie * b_im
    bbi = cre * b_im + cie * b_re
    gmask = _group_mask(128, 128, S5_GROUP, S5_GROUP)
    pr, pi = jnp.ones_like(are), jnp.zeros_like(are)
    xs, ys = [], []
    for t in range(S5_T + 1):
        if t < S5_T:
            xs.append(jnp.concatenate([bbr * pr - bbi * pi, bbr * pi + bbi * pr], axis=1))
        ys.append(jnp.concatenate([c_re * pr - c_im * pi, -(c_re * pi + c_im * pr)], axis=1))
        pr, pi = pr * are - pi * aie, pr * aie + pi * are
    gs = [_dot_nt_hi(x_t, ys[0]) * gmask for x_t in xs]
    r16, i16 = ar, ai
    for _ in range(4):
        r16, i16 = r16 * r16 - i16 * i16, 2.0 * r16 * i16
    return xs, ys, gs, jnp.concatenate([r16, i16], axis=1)


def _s5_expand(z):
    return jnp.concatenate([z] * 8, axis=1) * _group_mask(128, S5_SW, S5_GROUP, 128)


def _s5_contract(z):
    zm = z * _group_mask(128, S5_SW, S5_GROUP, 128)
    acc = zm[:, 0:128]
    for k in range(1, 8):
        acc = acc + zm[:, 128 * k:128 * (k + 1)]
    return acc


def _s5_param_specs():
    blk3 = lambda r, c: pl.BlockSpec((1, 1, r, c), lambda b, *_: (0, b, 0, 0))
    dir3 = lambda r, c: pl.BlockSpec((2, 1, r, c), lambda b, *_: (0, b, 0, 0))
    return [dir3(8, S5_STATE), dir3(8, S5_STATE), dir3(8, 1), blk3(128, S5_STATE), blk3(128, S5_STATE),
            blk3(128, S5_STATE), blk3(128, S5_STATE), blk3(1, 128)]


def _s5_gen(lre, lim, lst, b_re, b_im, c_re, c_im, dvec):
    def body(lre_ref, lim_ref, lst_ref, bre_ref, bim_ref, cre_ref, cim_ref, d_ref, gg_ref, xw_ref, yw_ref, a16_ref):
        eye = _group_mask(128, 128, 1, 1)
        g0 = eye * d_ref[0, 0]
        for dr in range(2):
            xs, ys, gs, a16 = _s5_gen_dir(lre_ref[dr, 0], lim_ref[dr, 0], lst_ref[dr, 0], bre_ref[0, 0],
                                          bim_ref[0, 0], cre_ref[0, 0], cim_ref[0, 0])
            a16_ref[0, dr] = a16
            for j in range(S5_T):
                xw_ref[0, dr, j] = xs[S5_T - 1 - j if dr == 0 else j]
                yw_ref[0, dr, j] = ys[j + 1 if dr == 0 else S5_T - j]
            g0 = g0 + gs[0]
            for t in range(1, S5_T):
                gg_ref[0, (S5_T - 1) + t if dr == 0 else (S5_T - 1) - t] = gs[t]
        gg_ref[0, S5_T - 1] = g0

    blk = pl.BlockSpec((1, 2, S5_T, 128, 128), lambda b: (b, 0, 0, 0, 0))
    return pl.pallas_call(
        body, name="s5_gen", grid=(S5_NB,),
        in_specs=_s5_param_specs(),
        out_specs=[pl.BlockSpec((1, 2 * S5_T - 1, 128, 128), lambda b: (b, 0, 0, 0)), blk, blk,
                   pl.BlockSpec((1, 2, 8, 128), lambda b: (b, 0, 0, 0))],
        out_shape=[jax.ShapeDtypeStruct((S5_NB, 2 * S5_T - 1, 128, 128), F32),
                   jax.ShapeDtypeStruct((S5_NB, 2, S5_T, 128, 128), F32),
                   jax.ShapeDtypeStruct((S5_NB, 2, S5_T, 128, 128), F32),
                   jax.ShapeDtypeStruct((S5_NB, 2, 8, 128), F32)],
        compiler_params=_params(("parallel",)),
    )(lre, lim, lst, b_re, b_im, c_re, c_im, dvec)


def _s5_fill_state_mat(w_scr, src_ref, dr):
    for j in range(S5_T):
        w_scr[128 * j:128 * (j + 1), :] = _s5_expand(src_ref[0, dr, j]).astype(BF16)


def _s5_fill_toeplitz(k_scr, gg_ref):
    for j in range(S5_T):
        for i in range(S5_T):
            k_scr[128 * j:128 * (j + 1), 128 * i:128 * (i + 1)] = gg_ref[0, i - j + (S5_T - 1)].astype(BF16)


S5_GEN_SPECS = [pl.BlockSpec((1, 2 * S5_T - 1, 128, 128), lambda b: (b, 0, 0, 0)),
                pl.BlockSpec((1, 2, S5_T, 128, 128), lambda b: (b, 0, 0, 0, 0))]


def _s5_gen_bwd(lre, lim, lst, b_re, b_im, c_re, c_im, dvec, dg, dx, dy, da16):
    def body(lre_ref, lim_ref, lst_ref, bre_ref, bim_ref, cre_ref, cim_ref, d_ref, dg_ref, dx_ref, dy_ref, da16_ref,
             glre_ref, glim_ref, glst_ref, gbre_ref, gbim_ref, gcre_ref, gcim_ref, gd_ref):
        eye = _group_mask(128, 128, 1, 1)
        gd_ref[0, 0] = jnp.sum(dg_ref[0, S5_T - 1] * eye, axis=0, keepdims=True)
        gb = [None, None, None, None]
        for dr in range(2):
            args = (lre_ref[dr, 0], lim_ref[dr, 0], lst_ref[dr, 0], bre_ref[0, 0], bim_ref[0, 0],
                    cre_ref[0, 0], cim_ref[0, 0])
            _, vjp = jax.vjp(_s5_gen_dir, *args)
            dxs = [dx_ref[0, dr, S5_T - 1 - t if dr == 0 else t] for t in range(S5_T)]
            dys = [jnp.zeros((128, 128), F32)] + [dy_ref[0, dr, t - 1 if dr == 0 else S5_T - t]
                                                  for t in range(1, S5_T + 1)]
            dgs = [dg_ref[0, (S5_T - 1) + t if dr == 0 else (S5_T - 1) - t] for t in range(S5_T)]
            g = vjp((dxs, dys, dgs, da16_ref[0, dr]))
            glre_ref[dr, 0] = g[0]
            glim_ref[dr, 0] = g[1]
            glst_ref[dr, 0] = g[2]
            for q in range(4):
                gb[q] = g[3 + q] if gb[q] is None else gb[q] + g[3 + q]
        gbre_ref[0, 0] = gb[0]
        gbim_ref[0, 0] = gb[1]
        gcre_ref[0, 0] = gb[2]
        gcim_ref[0, 0] = gb[3]

    shp = lambda a: jax.ShapeDtypeStruct(a.shape, F32)
    return pl.pallas_call(
        body, name="s5_gen_bwd", grid=(S5_NB,),
        in_specs=_s5_param_specs() + [
            pl.BlockSpec((1, 2 * S5_T - 1, 128, 128), lambda b: (b, 0, 0, 0)),
            pl.BlockSpec((1, 2, S5_T, 128, 128), lambda b: (b, 0, 0, 0, 0)),
            pl.BlockSpec((1, 2, S5_T, 128, 128), lambda b: (b, 0, 0, 0, 0)),
            pl.BlockSpec((1, 2, 8, 128), lambda b: (b, 0, 0, 0))],
        out_specs=_s5_param_specs(),
        out_shape=[shp(lre), shp(lim), shp(lst), shp(b_re), shp(b_im), shp(c_re), shp(c_im), shp(dvec)],
        compiler_params=_params(("parallel",)),
    )(lre, lim, lst, b_re, b_im, c_re, c_im, dvec, dg, dx, dy, da16)


def _s5_rows(t):
    cn = t.shape[0] // S5_T
    return t.reshape(cn, S5_T, S5_NB, 128).transpose(2, 0, 1, 3).reshape(S5_NB, cn, S5_BW)


def _s5_put_groups(o_ref, dr, val):
    for gi in range(8):
        o_ref[dr, :, gi, :] = val[:, 128 * gi:128 * (gi + 1)]


def _s5_get_groups(s_ref, dr, n=8):
    return jnp.concatenate([s_ref[dr, :, gi, :] for gi in range(n)], axis=1).astype(BF16)


def _s5_to_states(u3, blocks, name):
    cn = u3.shape[1]

    def body(u_ref, b_ref, o_ref, w_scr):
        u = u_ref[0]
        for dr in range(2):
            _s5_fill_state_mat(w_scr, b_ref, dr)
            _s5_put_groups(o_ref, dr, _dot(u, w_scr[...]))

    return pl.pallas_call(
        body, name=name, grid=(S5_NB,),
        in_specs=[pl.BlockSpec((1, cn, S5_BW), lambda b: (b, 0, 0)), S5_GEN_SPECS[1]],
        out_specs=pl.BlockSpec((2, cn, 8, 128), lambda b: (0, 0, b, 0)),
        out_shape=jax.ShapeDtypeStruct((2, cn, S5_GROUPS, 128), F32),
        scratch_shapes=[pltpu.VMEM((S5_BW, S5_SW), BF16)],
        compiler_params=_params(("parallel",)),
    )(u3, blocks)


def _s5_from_states(u3, gg, st, blocks, transposed, name):
    cn = u3.shape[1]

    def body(u_ref, g_ref, s_ref, b_ref, o_ref, k_scr, w_scr):
        u = u_ref[0]
        _s5_fill_toeplitz(k_scr, g_ref)
        y = _dot_nt(u, k_scr[...]) if transposed else _dot(u, k_scr[...])
        for dr in range(2):
            _s5_fill_state_mat(w_scr, b_ref, dr)
            y = y + _dot_nt(_s5_get_groups(s_ref, dr), w_scr[...])
        for i in range(S5_T):
            o_ref[:, i, :] = y[:, 128 * i:128 * (i + 1)]

    return pl.pallas_call(
        body, name=name, grid=(S5_NB,),
        in_specs=[pl.BlockSpec((1, cn, S5_BW), lambda b: (b, 0, 0)), S5_GEN_SPECS[0],
                  pl.BlockSpec((2, cn, 8, 128), lambda b: (0, 0, b, 0)), S5_GEN_SPECS[1]],
        out_specs=pl.BlockSpec((cn, S5_T, 128), lambda b: (0, 0, b)),
        out_shape=jax.ShapeDtypeStruct((cn, S5_T, S5_WIDTH), F32),
        scratch_shapes=[pltpu.VMEM((S5_BW, S5_BW), BF16), pltpu.VMEM((S5_BW, S5_SW), BF16)],
        compiler_params=_params(("parallel",)),
    )(u3, gg, st, blocks)


def _s5_a_forms(a):
    ra = pltpu.roll(a, S5_STATE, 1)
    low = _iota2(a.shape, 1) < S5_STATE
    return jnp.where(low, a, ra), jnp.where(low, -ra, a)


def _s5_scan(sloc, a16, ncc, placed, kinds):
    cn = sloc.shape[1]
    n = len(placed)
    shard_shapes = _gather_shard_shapes(placed, kinds)

    def body(s_ref, a_ref, *rest):
        h_ref = rest[n]
        sends, arrivals = _gather_chip_copies(rest[n + 1:2 * n + 1], kinds, shard_shapes, *rest[2 * n + 1:])
        for cp in sends:
            cp.start()
        forms = [_s5_a_forms(a_ref[dr]) for dr in range(2)]

        def step(s, hs):
            out = []
            for dr in range(2):
                arr, aii = forms[dr]
                h, rh = hs[dr]
                c = s if dr == 0 else jnp.where(s < ncc, ncc - 1 - s, cn - 1 - (s - ncc))
                h_ref[dr, c] = h
                sc = s_ref[dr, c]
                out.append((h * arr + rh * aii + sc, rh * arr - h * aii + pltpu.roll(sc, S5_STATE, 1)))
            return tuple(out)

        zero = jnp.zeros((S5_GROUPS, 128), F32)
        lax.fori_loop(0, cn, step, ((zero, zero), (zero, zero)), unroll=4)
        for cp in arrivals:
            cp.wait_recv()
        for cp in sends:
            cp.wait_send()

    vmem = pl.BlockSpec(memory_space=pltpu.VMEM)
    return pl.pallas_call(
        body, name="s5_scan",
        in_specs=[vmem, vmem] + [ANY] * n, out_specs=[vmem] + [ANY] * n,
        out_shape=[jax.ShapeDtypeStruct(sloc.shape, F32)] + [jax.ShapeDtypeStruct(p.shape, p.dtype) for p in placed],
        input_output_aliases={2 + a: 1 + a for a in range(n)},
        scratch_shapes=[pltpu.SemaphoreType.DMA((n, 3)), pltpu.SemaphoreType.DMA((n, 3))],
        compiler_params=_params(),
    )(sloc, a16, *placed)


def _s5_scan_bwd(e, hs, a16, ncc):
    cn = e.shape[1]

    def body(e_ref, h_ref, a_ref, ds_ref, da_ref):
        forms = [_s5_a_forms(a_ref[dr]) for dr in range(2)]
        low = _iota2((S5_GROUPS, 128), 1) < S5_STATE

        def step(s, carry):
            out = []
            r = cn - 1 - s
            for dr in range(2):
                arr, aii = forms[dr]
                g, rg, da = carry[dr]
                c = r if dr == 0 else jnp.where(r < ncc, ncc - 1 - r, cn - 1 - (r - ncc))
                ds_ref[dr, c] = g
                h = h_ref[dr, c]
                rh = pltpu.roll(h, S5_STATE, 1)
                da = da + jnp.where(low, g * h + rg * rh, g * rh - rg * h)
                ec = e_ref[dr, c]
                out.append((ec + g * arr - rg * aii, pltpu.roll(ec, S5_STATE, 1) + rg * arr + g * aii, da))
            return tuple(out)

        zero = jnp.zeros((S5_GROUPS, 128), F32)
        res = lax.fori_loop(0, cn, step, ((zero, zero, zero), (zero, zero, zero)), unroll=4)
        da_ref[0] = res[0][2]
        da_ref[1] = res[1][2]

    return pl.pallas_call(
        body, name="s5_scan_bwd",
        out_shape=[jax.ShapeDtypeStruct(e.shape, F32), jax.ShapeDtypeStruct((2, S5_GROUPS, 128), F32)],
        compiler_params=_params(),
    )(e, hs, a16)


def _s5_bwd_kb(p3, dy3):
    cn = p3.shape[1]
    half = S5_T // 2

    def body(u_ref, d_ref, o_ref):
        q = pl.program_id(1)

        @pl.when(q == 0)
        def _():
            o_ref[...] = jnp.zeros_like(o_ref)

        dk = _dot_tn(u_ref[0], d_ref[0])
        for j in range(S5_T):
            for i in range(half):
                o_ref[0, half * q + i - j + (S5_T - 1)] += dk[128 * j:128 * (j + 1), 128 * i:128 * (i + 1)]

    return pl.pallas_call(
        body, name="s5_bwd_kb", grid=(S5_NB, 2),
        in_specs=[pl.BlockSpec((1, cn, S5_BW), lambda b, q: (b, 0, 0)),
                  pl.BlockSpec((1, cn, S5_BW // 2), lambda b, q: (b, 0, q))],
        out_specs=pl.BlockSpec((1, 2 * S5_T - 1, 128, 128), lambda b, q: (b, 0, 0, 0)),
        out_shape=jax.ShapeDtypeStruct((S5_NB, 2 * S5_T - 1, 128, 128), F32),
        compiler_params=_params(("parallel", "arbitrary")),
    )(p3, dy3)


def _s5_bwd_w(u3, st, name):
    cn = u3.shape[1]

    def body(u_ref, s_ref, w_ref):
        dw = _dot_tn(u_ref[0], _s5_get_groups(s_ref, 0))
        for j in range(S5_T):
            w_ref[0, 0, j] = _s5_contract(dw[128 * j:128 * (j + 1), :])

    return pl.pallas_call(
        body, name=name, grid=(S5_NB, 2),
        in_specs=[pl.BlockSpec((1, cn, S5_BW), lambda b, q: (b, 0, 0)),
                  pl.BlockSpec((1, cn, 8, 128), lambda b, q: (q, 0, b, 0))],
        out_specs=pl.BlockSpec((1, 1, S5_T, 128, 128), lambda b, q: (b, q, 0, 0, 0)),
        out_shape=jax.ShapeDtypeStruct((S5_NB, 2, S5_T, 128, 128), F32),
        compiler_params=_params(("parallel", "parallel")),
    )(u3, st)


def _s5_glu_bwd(y_all, dmix, w_glu_b, b_glu, nct):
    la = y_all.shape[0]
    tm = TOK_TILE

    def body(y_ref, d_ref, w_ref, b_ref, dy_ref, gw_ref, gb_ref):
        i = pl.program_id(0)

        @pl.when(i == 0)
        def _():
            gw_ref[...] = jnp.zeros_like(gw_ref)
            gb_ref[...] = jnp.zeros_like(gb_ref)

        @pl.when(i < nct)
        def _():
            dy_ref[...] = jnp.zeros_like(dy_ref)

        @pl.when(i >= nct)
        def _():
            y = y_ref[...]
            yg, gelu_vjp = jax.vjp(_gelu, y)
            ygb = yg.astype(BF16)
            sg = _sigmoid(_dot(ygb, w_ref[...]) + b_ref[...])
            ds = d_ref[...]
            dz = ds * yg * sg * (1.0 - sg)
            dzb = dz.astype(BF16)
            dyg = ds * sg + _dot_nt(dzb, w_ref[...])
            dy_ref[...] = gelu_vjp(dyg)[0].astype(BF16)
            gw_ref[...] += _dot_tn(ygb, dzb)
            gb_ref[...] += jnp.sum(dz, axis=0, keepdims=True)

    return pl.pallas_call(
        body, name="s5_glu_bwd", grid=(la // tm,),
        in_specs=[pl.BlockSpec((tm, S5_WIDTH), lambda i: (i, 0)),
                  pl.BlockSpec((tm, S5_WIDTH), lambda i: (jnp.maximum(i - nct, 0), 0)),
                  _full((S5_WIDTH, S5_WIDTH)), _full((1, S5_WIDTH))],
        out_specs=[pl.BlockSpec((tm, S5_WIDTH), lambda i: (i, 0)), _full((S5_WIDTH, S5_WIDTH)),
                   _full((1, S5_WIDTH))],
        out_shape=[jax.ShapeDtypeStruct((la, S5_WIDTH), BF16), jax.ShapeDtypeStruct((S5_WIDTH, S5_WIDTH), F32),
                   jax.ShapeDtypeStruct((1, S5_WIDTH), F32)],
        compiler_params=_params(("arbitrary",)),
    )(y_all, dmix, w_glu_b, b_glu)


K_SCALE = RET_DH ** -0.5
Q_COL, K_COL, V_COL, G_COL = 4, 8, 12, 16


def _ret_chunk_of(step, ncc, nch, rev):
    if not rev:
        return step
    return jnp.where(step < ncc, ncc - 1 - step, nch - 1 - (step - ncc))


def _ret_decay(ld, rev):
    c = _iota2((RET_CHUNK, RET_CHUNK), 0).astype(F32)
    m = _iota2((RET_CHUNK, RET_CHUNK), 1).astype(F32)
    diff = (m - c) if rev else (c - m)
    keep = (diff > 0) if rev else (diff >= 0)
    expo = jnp.maximum(diff, 0.0)
    dm = jnp.where(keep, jnp.exp(ld * expo), 0.0)
    xi_e = (RET_CHUNK - c) if rev else (c + 1.0)
    zeta_e = c if rev else (RET_CHUNK - 1.0 - c)
    return dm, expo, jnp.exp(ld * xi_e), xi_e, jnp.exp(ld * zeta_e), zeta_e


RET_TABLES = 7


def _ret_tables(ld2):
    def body(ld_ref, t_ref):
        dr, h = pl.program_id(0), pl.program_id(1)
        ldh = ld_ref[dr, h]
        for rev in (False, True):
            @pl.when(dr == int(rev))
            def _(rev=rev):
                dm, expo, xi, xi_e, zeta, zeta_e = _ret_decay(ldh, rev)
                t_ref[0, 0, 0] = dm
                t_ref[0, 0, 1] = dm * expo
                t_ref[0, 0, 2] = xi
                t_ref[0, 0, 3] = xi * xi_e
                t_ref[0, 0, 4] = zeta
                t_ref[0, 0, 5] = zeta * zeta_e
                t_ref[0, 0, 6] = jnp.zeros_like(dm) + jnp.exp(ldh * RET_CHUNK)

    return pl.pallas_call(
        body, name="ret_tables", grid=(2, RET_HEADS),
        in_specs=[pl.BlockSpec(memory_space=pltpu.SMEM)],
        out_specs=pl.BlockSpec((1, 1, RET_TABLES, RET_CHUNK, RET_CHUNK), lambda d, h: (d, h, 0, 0, 0)),
        out_shape=jax.ShapeDtypeStruct((2, RET_HEADS, RET_TABLES, RET_CHUNK, RET_CHUNK), F32),
        compiler_params=_params(("parallel", "parallel")),
    )(ld2)


def _ret_specs(nch, ncc, rev, step_of):
    chunk = lambda n: _ret_chunk_of(step_of(n), ncc, nch, rev)
    cols = [pl.BlockSpec((RET_CHUNK, RET_WIDTH), functools.partial(lambda n, cb: (chunk(n), cb), cb=cb))
            for cb in (1, 2, 3)]
    tab = pl.BlockSpec((RET_CHUNK, RET_DH), lambda n: (chunk(n), 0))
    return cols + [tab, tab], pl.BlockSpec((RET_CHUNK, RET_WIDTH), lambda n: (chunk(n), 0))


def _ret_scan(p_all, cos_t, sin_t, tabs, ncc):
    la = p_all.shape[0]
    nch = la // RET_CHUNK

    def body(t_ref, qf, kf, vf, cf, sf, qb, kb, vb, cb, sb, of_ref, ob_ref, ssf_ref, ssb_ref, s_scr):
        @pl.when(pl.program_id(0) == 0)
        def _():
            s_scr[...] = jnp.zeros_like(s_scr)

        for dr, (q_ref, k_ref, v_ref, c_ref, n_ref, o_ref, ss_ref) in enumerate(
                ((qf, kf, vf, cf, sf, of_ref, ssf_ref), (qb, kb, vb, cb, sb, ob_ref, ssb_ref))):
            cs, sn = c_ref[...], n_ref[...]
            for h in range(RET_HEADS):
                sl = slice(RET_DH * h, RET_DH * (h + 1))
                dm, xi, zeta = t_ref[dr, h, 0], t_ref[dr, h, 2, :, 0:RET_DH], t_ref[dr, h, 4, :, 0:RET_DH]
                q = _rope(q_ref[:, sl], cs, sn)
                k = _rope(k_ref[:, sl] * K_SCALE, cs, sn)
                vh = v_ref[:, sl].astype(BF16)
                s = s_scr[dr, h]
                ss_ref[0, h] = s
                sc = (_dot_nt(q.astype(BF16), k.astype(BF16)) * dm).astype(BF16)
                o_ref[:, sl] = _dot(sc, vh) + _dot((q * xi).astype(BF16), s.astype(BF16))
                s_scr[dr, h] = t_ref[dr, h, 6, 0:RET_DH, 0:RET_DH] * s + _dot_tn((k * zeta).astype(BF16), vh)

    in_f, out_f = _ret_specs(nch, ncc, False, lambda n: n)
    in_b, out_b = _ret_specs(nch, ncc, True, lambda n: n)
    ss_spec = pl.BlockSpec((1, RET_HEADS, RET_DH, RET_DH), lambda n: (n, 0, 0, 0))
    o_shape = jax.ShapeDtypeStruct((la, RET_WIDTH), F32)
    ss_shape = jax.ShapeDtypeStruct((nch, RET_HEADS, RET_DH, RET_DH), F32)
    return pl.pallas_call(
        body, name="ret_scan", grid=(nch,),
        in_specs=[_full(tabs.shape)] + in_f + in_b,
        out_specs=[out_f, out_b, ss_spec, ss_spec],
        out_shape=[o_shape, o_shape, ss_shape, ss_shape],
        scratch_shapes=[pltpu.VMEM((2, RET_HEADS, RET_DH, RET_DH), F32)],
        compiler_params=_params(("arbitrary",)),
    )(tabs, p_all, p_all, p_all, cos_t, sin_t, p_all, p_all, p_all, cos_t, sin_t)


def _ret_scan_bwd(p_all, cos_t, sin_t, tabs, ssf, ssb, dy_all, ncc):
    la = p_all.shape[0]
    nch = la // RET_CHUNK

    def body(t_ref, qf, kf, vf, cf, sf, dof, ssf_ref, qb, kb, vb, cb, sb, dob_, ssb_ref,
             dqf, dkf, dvf, dqb, dkb, dvb, dld_ref, ds_scr):
        @pl.when(pl.program_id(0) == 0)
        def _():
            ds_scr[...] = jnp.zeros_like(ds_scr)
            dld_ref[...] = jnp.zeros_like(dld_ref)

        for dr, (q_ref, k_ref, v_ref, c_ref, n_ref, do_ref, ss_ref, dq_ref, dk_ref, dv_ref) in enumerate(
                ((qf, kf, vf, cf, sf, dof, ssf_ref, dqf, dkf, dvf), (qb, kb, vb, cb, sb, dob_, ssb_ref, dqb, dkb, dvb))):
            cs, sn = c_ref[...], n_ref[...]
            on_ctx = _ret_chunk_of(nch - 1 - pl.program_id(0), ncc, nch, dr == 1) < ncc
            for h in range(RET_HEADS):
                sl = slice(RET_DH * h, RET_DH * (h + 1))
                dm, dm_d = t_ref[dr, h, 0], t_ref[dr, h, 1]
                xi, xi_d, zeta, zeta_d = [t_ref[dr, h, t, :, 0:RET_DH] for t in (2, 3, 4, 5)]
                gc = t_ref[dr, h, 6, 0:RET_DH, 0:RET_DH]
                q = _rope(q_ref[:, sl], cs, sn)
                k = _rope(k_ref[:, sl] * K_SCALE, cs, sn)
                q16, k16, v16 = q.astype(BF16), k.astype(BF16), v_ref[:, sl].astype(BF16)
                s = ss_ref[0, h]
                s16 = s.astype(BF16)
                ds_in = ds_scr[dr, h]
                ds16 = ds_in.astype(BF16)
                do16 = jnp.where(on_ctx, 0.0, do_ref[:, sl]).astype(BF16)
                qk = _dot_nt(q16, k16)
                dsv = _dot_nt(do16, v16)
                dsc = (dsv * dm).astype(BF16)
                sc16 = (qk * dm).astype(BF16)
                dos = _dot_nt(do16, s16)
                vds = _dot_nt(v16, ds16)
                dq_ref[:, sl] = _dot(dsc, k16) + dos * xi
                dk_ref[:, sl] = _dot_tn(dsc, q16) + vds * zeta
                dv_ref[:, sl] = _dot_tn(sc16, do16) + _dot((k * zeta).astype(BF16), ds16)
                ds_scr[dr, h] = _dot_tn((q * xi).astype(BF16), do16) + gc * ds_in
                dld = (jnp.sum(dsv * qk * dm_d) + jnp.sum(q * dos * xi_d + k * vds * zeta_d)
                       + RET_CHUNK * jnp.sum(gc * s * ds_in))
                dld_ref[dr, h] += dld

    back = lambda n: nch - 1 - n
    in_f, out_f = _ret_specs(nch, ncc, False, back)
    in_b, out_b = _ret_specs(nch, ncc, True, back)
    ss_spec = pl.BlockSpec((1, RET_HEADS, RET_DH, RET_DH), lambda n: (nch - 1 - n, 0, 0, 0))
    shp = jax.ShapeDtypeStruct((la, RET_WIDTH), F32)
    dy_spec = lambda rev: pl.BlockSpec(
        (RET_CHUNK, RET_WIDTH), lambda n: (jnp.maximum(_ret_chunk_of(nch - 1 - n, ncc, nch, rev) - ncc, 0), 0))
    return pl.pallas_call(
        body, name="ret_scan_bwd", grid=(nch,),
        in_specs=[_full(tabs.shape)] + in_f + [dy_spec(False), ss_spec] + in_b + [dy_spec(True), ss_spec],
        out_specs=[out_f, out_f, out_f, out_b, out_b, out_b, _full((2, RET_HEADS, 8, 128))],
        out_shape=[shp] * 6 + [jax.ShapeDtypeStruct((2, RET_HEADS, 8, 128), F32)],
        scratch_shapes=[pltpu.VMEM((2, RET_HEADS, RET_DH, RET_DH), F32)],
        compiler_params=_params(("arbitrary",)),
    )(tabs, p_all, p_all, p_all, cos_t, sin_t, dy_all, ssf, p_all, p_all, p_all, cos_t, sin_t, dy_all, ssb)


def _ret_gate_bwd(y_ret, p_all, dmix, nct):
    la = p_all.shape[0]
    tm = TOK_TILE

    def body(y_ref, g_ref, d_ref, dy_ref, dg_ref):
        i = pl.program_id(0)

        @pl.when(i < nct)
        def _():
            dy_ref[...] = jnp.zeros_like(dy_ref)
            dg_ref[...] = jnp.zeros_like(dg_ref)

        @pl.when(i >= nct)
        def _():
            for h in range(RET_HEADS):
                sl = slice(RET_DH * h, RET_DH * (h + 1))
                _, vjp = jax.vjp(_head_norm_gate, y_ref[:, sl], g_ref[:, sl])
                dy, dg = vjp(d_ref[:, sl])
                dy_ref[:, sl] = dy
                dg_ref[:, sl] = dg

    xrow = lambda cb: pl.BlockSpec((tm, RET_WIDTH), lambda i: (jnp.maximum(i - nct, 0), cb))
    out = pl.BlockSpec((tm, RET_WIDTH), lambda i: (i, 0))
    shp = jax.ShapeDtypeStruct((la, RET_WIDTH), F32)
    return pl.pallas_call(
        body, name="ret_gate_bwd", grid=(la // tm,),
        in_specs=[xrow(0), pl.BlockSpec((tm, RET_WIDTH), lambda i: (i, G_COL // 4)), xrow(1)],
        out_specs=[out, out], out_shape=[shp, shp],
        compiler_params=_params(("parallel",)),
    )(y_ret, p_all, dmix)


def _in_bwd(dqf, dkf, dvf, dqb, dkb, dvb, du, dg, cos_t, sin_t, w_in_b, x, ctx, n1w, mod4, dx1):
    l, lc = x.shape[0], ctx.shape[0]
    la = l + lc
    tm = TOK_TILE
    nct = lc // tm

    def body(dqf_ref, dkf_ref, dvf_ref, dqb_ref, dkb_ref, dvb_ref, du_ref, dg_ref, cos_ref, sin_ref,
             w_ref, x_ref, c_ref, nw_ref, mod_ref, dx1_ref, dp_ref, gx_ref, acc_ref):
        i = pl.program_id(0)
        is_ctx = i < nct

        @pl.when(i == 0)
        def _():
            acc_ref[...] = jnp.zeros_like(acc_ref)

        cs, sn = cos_ref[...], sin_ref[...]
        def piece(k, val):
            cols = slice(S5_WIDTH * k, S5_WIDTH * (k + 1))
            dp_ref[:, cols] = val.astype(BF16)
            return _dot_nt(dp_ref[:, cols], w_ref[:, cols])

        dh1 = piece(0, du_ref[...])
        dh1 = dh1 + piece(3, dvf_ref[...] + dvb_ref[...])
        dh1 = dh1 + piece(4, jnp.where(is_ctx, 0.0, dg_ref[...]))
        for k, (f_ref, b_ref, scale) in ((1, (dqf_ref, dqb_ref, 1.0)), (2, (dkf_ref, dkb_ref, K_SCALE))):
            heads = [_rope_t(f_ref[:, RET_DH * h:RET_DH * (h + 1)] + b_ref[:, RET_DH * h:RET_DH * (h + 1)], cs, sn) * scale
                     for h in range(RET_HEADS)]
            dh1 = dh1 + piece(k, jnp.concatenate(heads, axis=1))
        xt = jnp.where(is_ctx, c_ref[...], x_ref[...])
        sh = jnp.where(is_ctx, mod_ref[0:1, :], mod_ref[2:3, :])
        sc = jnp.where(is_ctx, mod_ref[1:2, :], mod_ref[3:4, :])
        _, vjp = jax.vjp(_rms_mod, xt, nw_ref[...], sh, sc)
        dx, dnw, dsh, dsc = vjp(dh1)
        gx_ref[...] = dx + dx1_ref[...]
        cf = jnp.where(is_ctx, 1.0, 0.0)
        acc_ref[0:1, :] += dnw
        acc_ref[1:2, :] += cf * dsh
        acc_ref[2:3, :] += cf * dsc
        acc_ref[3:4, :] += (1.0 - cf) * dsh
        acc_ref[4:5, :] += (1.0 - cf) * dsc

    row = pl.BlockSpec((tm, RET_WIDTH), lambda i: (i, 0))
    tab = pl.BlockSpec((tm, RET_DH), lambda i: (i, 0))
    xrow = pl.BlockSpec((tm, D_MODEL), lambda i: (jnp.maximum(i - nct, 0), 0))
    return pl.pallas_call(
        body, name="in_bwd", grid=(la // tm,),
        in_specs=[row] * 7 + [pl.BlockSpec((tm, RET_WIDTH), lambda i: (jnp.maximum(i - nct, 0), 0)),
                              tab, tab, _full((D_MODEL, IN_COLS)), xrow,
                              pl.BlockSpec((tm, D_MODEL), lambda i: (jnp.minimum(i, nct - 1), 0)),
                              _full((1, D_MODEL)), _full((4, D_MODEL)), xrow],
        out_specs=[pl.BlockSpec((tm, IN_COLS), lambda i: (i, 0)), xrow, _full((8, D_MODEL))],
        out_shape=[jax.ShapeDtypeStruct((la, IN_COLS), BF16), jax.ShapeDtypeStruct((l, D_MODEL), F32),
                   jax.ShapeDtypeStruct((8, D_MODEL), F32)],
        compiler_params=_params(("arbitrary",)),
    )(dqf, dkf, dvf, dqb, dkb, dvb, du, dg, cos_t, sin_t, w_in_b, x, ctx, n1w, mod4, dx1)


def _outproj_up(x, y_all, of, ob, p_all, w_glu_b, b_glu, w_out_b, mod3, n2w, w_up_b, nct):
    l = x.shape[0]
    tm = TOK_TILE

    def body(x_ref, y_ref, of_ref, ob_ref, g_ref, wg_ref, bg_ref, wo_ref, mod_ref, nw_ref, wu_ref,
             x1_ref, mix_ref, h2_ref, up_ref, mb_ref, yr_ref):
        yg = _gelu(y_ref[...])
        mb_ref[:, 0:S5_WIDTH] = (yg * _sigmoid(_dot(yg.astype(BF16), wg_ref[...]) + bg_ref[...])).astype(BF16)
        yr = of_ref[...] + ob_ref[...]
        yr_ref[...] = yr
        for h in range(RET_HEADS):
            sl = slice(RET_DH * h, RET_DH * (h + 1))
            mb_ref[:, S5_WIDTH + RET_DH * h:S5_WIDTH + RET_DH * (h + 1)] = (
                _head_norm_gate(yr[:, sl], g_ref[:, sl]).astype(BF16))
        mix = _dot(mb_ref[...], wo_ref[...])
        mix_ref[...] = mix
        x1 = x_ref[...] + mod_ref[0:1, :] * mix
        x1_ref[...] = x1
        h2 = _rms_mod(x1, nw_ref[...], mod_ref[1:2, :], mod_ref[2:3, :]).astype(BF16)
        h2_ref[...] = h2
        up_ref[...] = _dot(h2, wu_ref[...])

    row = lambda w: pl.BlockSpec((tm, w), lambda i: (i, 0))
    arow = pl.BlockSpec((tm, RET_WIDTH), lambda i: (i + nct, 0))
    return pl.pallas_call(
        body, name="outproj_up", grid=(l // tm,),
        in_specs=[row(D_MODEL), arow, arow, arow, pl.BlockSpec((tm, RET_WIDTH), lambda i: (i + nct, G_COL // 4)),
                  _full((S5_WIDTH, S5_WIDTH)), _full((1, S5_WIDTH)), _full((D_MODEL, D_MODEL)), _full((3, D_MODEL)),
                  _full((1, D_MODEL)), _full((D_MODEL, 2 * D_FF))],
        out_specs=[row(D_MODEL), row(D_MODEL), row(D_MODEL), row(2 * D_FF), row(D_MODEL), row(RET_WIDTH)],
        out_shape=[jax.ShapeDtypeStruct((l, D_MODEL), F32), jax.ShapeDtypeStruct((l, D_MODEL), F32),
                   jax.ShapeDtypeStruct((l, D_MODEL), BF16), jax.ShapeDtypeStruct((l, 2 * D_FF), F32),
                   jax.ShapeDtypeStruct((l, D_MODEL), BF16), jax.ShapeDtypeStruct((l, RET_WIDTH), F32)],
        compiler_params=_params(("parallel",)),
    )(x, y_all, of, ob, p_all, w_glu_b, b_glu, w_out_b, mod3, n2w, w_up_b)


HALO = 8


def _conv_taps(g, prev_row, next_row):
    t = g.shape[0]
    r = _iota2(g.shape, 0)
    gprev = jnp.where(r == 0, prev_row, pltpu.roll(g, 1, 0))
    gnext = jnp.where(r == t - 1, next_row, pltpu.roll(g, t - 1, 0))
    return gprev, gnext


def _ffn_loss(up, x1, conv_w, conv_b, w_down_b, gate, fnw, tgt):
    l = x1.shape[0]
    tm = TOK_TILE
    nt = l // tm
    hb = tm // HALO

    cw = 256

    def body(up_a, up_g, hp_ref, hn_ref, x1_ref, cw_ref, cb_ref, wd_ref, gate_ref, fn_ref, tgt_ref,
             act_ref, dx2_ref, ddn_ref, dact_ref, acc_ref):
        i = pl.program_id(0)

        @pl.when(i == 0)
        def _():
            acc_ref[...] = jnp.zeros_like(acc_ref)

        dn = jnp.zeros((tm, D_MODEL), F32)
        for c in range(D_FF // cw):
            cols = slice(cw * c, cw * (c + 1))
            g = up_g[:, cols]
            prev_row = jnp.where(i == 0, 0.0, hp_ref[HALO - 1:HALO, cols])
            next_row = jnp.where(i == nt - 1, 0.0, hn_ref[0:1, cols])
            gprev, gnext = _conv_taps(g, prev_row, next_row)
            gc = cb_ref[:, cols] + gprev * cw_ref[0:1, cols] + g * cw_ref[1:2, cols] + gnext * cw_ref[2:3, cols]
            act = (_gelu(gc) * up_a[:, cols]).astype(BF16)
            act_ref[:, cols] = act
            dn = dn + _dot(act, wd_ref[cols, :])
        x2 = x1_ref[...] + gate_ref[...] * dn
        y, vjp = jax.vjp(_rms, x2, fn_ref[...])
        err = y - tgt_ref[...]
        dx2, dfn = vjp(err * (1.0 / D_MODEL))
        dx2_ref[...] = dx2
        ddn = (dx2 * gate_ref[...]).astype(BF16)
        ddn_ref[...] = ddn
        for c in range(D_FF // cw):
            cols = slice(cw * c, cw * (c + 1))
            dact_ref[:, cols] = _dot_nt(ddn, wd_ref[cols, :])
        acc_ref[0:1, :] += dfn
        acc_ref[1:2, :] += jnp.sum(dx2 * dn, axis=0, keepdims=True)
        acc_ref[2:3, :] += (0.5 / D_MODEL) * jnp.sum(err * err)

    row = lambda w: pl.BlockSpec((tm, w), lambda i: (i, 0))
    last = l // HALO - 1
    return pl.pallas_call(
        body, name="ffn_loss", grid=(nt,),
        in_specs=[pl.BlockSpec((tm, D_FF), lambda i: (i, 0)), pl.BlockSpec((tm, D_FF), lambda i: (i, 1)),
                  pl.BlockSpec((HALO, D_FF), lambda i: (jnp.maximum(i * hb - 1, 0), 1)),
                  pl.BlockSpec((HALO, D_FF), lambda i: (jnp.minimum((i + 1) * hb, last), 1)),
                  row(D_MODEL), _full((3, D_FF)), _full((1, D_FF)), _full((D_FF, D_MODEL)),
                  _full((1, D_MODEL)), _full((1, D_MODEL)), row(D_MODEL)],
        out_specs=[row(D_FF), row(D_MODEL), row(D_MODEL), row(D_FF), _full((8, D_MODEL))],
        out_shape=[jax.ShapeDtypeStruct((l, D_FF), BF16), jax.ShapeDtypeStruct((l, D_MODEL), F32),
                   jax.ShapeDtypeStruct((l, D_MODEL), BF16), jax.ShapeDtypeStruct((l, D_FF), F32),
                   jax.ShapeDtypeStruct((8, D_MODEL), F32)],
        compiler_params=_params(("arbitrary",)),
    )(up, up, up, up, x1, conv_w, conv_b, w_down_b, gate, fnw, tgt)


def _convglu_bwd(up, dact, conv_w, conv_b):
    l = up.shape[0]
    tm = 128
    nt = l // tm
    hb = tm // HALO
    te = tm + 2 * HALO

    def body(a_ref, ap_ref, an_ref, g_ref, gp_ref, gn_ref, d_ref, dp_ref, dn_ref, cw_ref, cb_ref,
             dup_ref, acc_ref):
        i = pl.program_id(0)

        @pl.when(i == 0)
        def _():
            acc_ref[...] = jnp.zeros_like(acc_ref)

        def ext(p, c, n):
            return jnp.concatenate([jnp.where(i == 0, 0.0, p[...]), c[...], jnp.where(i == nt - 1, 0.0, n[...])], axis=0)

        ae, ge, de = ext(ap_ref, a_ref, an_ref), ext(gp_ref, g_ref, gn_ref), ext(dp_ref, d_ref, dn_ref)
        gprev = pltpu.roll(ge, 1, 0)
        gnext = pltpu.roll(ge, te - 1, 0)
        w0, w1, w2 = cw_ref[0:1, :], cw_ref[1:2, :], cw_ref[2:3, :]
        gce = cb_ref[...] + gprev * w0 + ge * w1 + gnext * w2
        gel, dgel = _gelu_and_grad(gce)
        dae = de * gel
        dgce = de * ae * dgel
        dge = dgce * w1 + pltpu.roll(dgce, te - 1, 0) * w0 + pltpu.roll(dgce, 1, 0) * w2
        mid = slice(HALO, HALO + tm)
        dup_ref[:, 0:D_FF] = dae[mid].astype(BF16)
        dup_ref[:, D_FF:2 * D_FF] = dge[mid].astype(BF16)
        dgc = dgce[mid]
        acc_ref[0:1, :] += jnp.sum(dgc * gprev[mid], axis=0, keepdims=True)
        acc_ref[1:2, :] += jnp.sum(dgc * ge[mid], axis=0, keepdims=True)
        acc_ref[2:3, :] += jnp.sum(dgc * gnext[mid], axis=0, keepdims=True)
        acc_ref[3:4, :] += jnp.sum(dgc, axis=0, keepdims=True)

    last = l // HALO - 1

    def trio(cb):
        return [pl.BlockSpec((tm, D_FF), lambda i: (i, cb)),
                pl.BlockSpec((HALO, D_FF), lambda i: (jnp.maximum(i * hb - 1, 0), cb)),
                pl.BlockSpec((HALO, D_FF), lambda i: (jnp.minimum((i + 1) * hb, last), cb))]

    return pl.pallas_call(
        body, name="convglu_bwd", grid=(nt,),
        in_specs=trio(0) + trio(1) + trio(0) + [_full((3, D_FF)), _full((1, D_FF))],
        out_specs=[pl.BlockSpec((tm, 2 * D_FF), lambda i: (i, 0)), _full((8, D_FF))],
        out_shape=[jax.ShapeDtypeStruct((l, 2 * D_FF), BF16), jax.ShapeDtypeStruct((8, D_FF), F32)],
        compiler_params=_params(("arbitrary",)),
    )(up, up, up, up, up, up, dact, dact, dact, conv_w, conv_b)


def _up_bwd(dup, w_up_b, w_out_b, x1, dx2, mix, mod3, n2w, y_all, y_ret, p_all, w_glu_b, b_glu, zero_rows, nct, pairs,
            kinds):
    l = x1.shape[0]
    tm = TOK_TILE
    nt = l // tm
    n = len(pairs)
    shapes = _rs_slot_shapes(pairs, kinds)
    n_out = 8

    def body(dup_ref, wu_ref, wo_ref, x1_ref, dx2_ref, mix_ref, mod_ref, nw_ref, y_ref, yr_ref, g_ref, wg_ref, bg_ref,
             zero_rows_ref, *rest):
        dx1_ref, dmixb_ref, acc_ref, dys_ref, dyr_ref, dg_ref, gw_ref, gb_ref = rest[n:n + n_out]
        send_sems, recv_sems, dy_scr = rest[2 * n + n_out:]
        step = pl.program_id(0)

        @pl.when(step == 0)
        def _():
            acc_ref[...] = jnp.zeros_like(acc_ref)
            gw_ref[...] = jnp.zeros_like(gw_ref)
            gb_ref[...] = jnp.zeros_like(gb_ref)

        _behind(step, nt - 1, functools.partial(_rs_chip_copies, rest[:n], rest[n + n_out:2 * n + n_out], kinds,
                                                shapes, send_sems, recv_sems))

        dh2 = _dot_nt(dup_ref[...], wu_ref[...])
        _, vjp = jax.vjp(_rms_mod, x1_ref[...], nw_ref[...], mod_ref[1:2, :], mod_ref[2:3, :])
        dx, dnw, dsh, dsc = vjp(dh2)
        dx1 = dx + dx2_ref[...]
        dx1_ref[...] = dx1
        dmixb = (dx1 * mod_ref[0:1, :]).astype(BF16)
        dmixb_ref[...] = dmixb
        dmix = _dot_nt(dmixb, wo_ref[...])
        acc_ref[0:1, :] += dnw
        acc_ref[1:2, :] += jnp.sum(dx1 * mix_ref[...], axis=0, keepdims=True)
        acc_ref[2:3, :] += dsh
        acc_ref[3:4, :] += dsc

        yg, dgel = _gelu_and_grad(y_ref[...])
        ygb = yg.astype(BF16)
        sg = _sigmoid(_dot(ygb, wg_ref[...]) + bg_ref[...])
        ds = dmix[:, 0:S5_WIDTH]
        dz = ds * yg * sg * (1.0 - sg)
        dzb = dz.astype(BF16)
        _s5_put_rows(dys_ref, dy_scr, (ds * sg + _dot_nt(dzb, wg_ref[...])) * dgel)
        gw_ref[...] += _dot_tn(ygb, dzb)
        gb_ref[...] += jnp.sum(dz, axis=0, keepdims=True)

        for h in range(RET_HEADS):
            sl = slice(RET_DH * h, RET_DH * (h + 1))
            _, hvjp = jax.vjp(_head_norm_gate, yr_ref[:, sl], g_ref[:, sl])
            dyr, dg = hvjp(dmix[:, S5_WIDTH + RET_DH * h:S5_WIDTH + RET_DH * (h + 1)])
            dyr_ref[:, sl] = dyr
            dg_ref[:, sl] = dg

    row = pl.BlockSpec((tm, D_MODEL), lambda i: (i, 0))
    half = pl.BlockSpec((tm, S5_WIDTH), lambda i: (i, 0))
    f32h = jax.ShapeDtypeStruct((l, RET_WIDTH), F32)
    return pl.pallas_call(
        body, name="up_bwd", grid=(nt,),
        in_specs=[pl.BlockSpec((tm, 2 * D_FF), lambda i: (i, 0)), _full((D_MODEL, 2 * D_FF)),
                  _full((D_MODEL, D_MODEL)), row, row, row, _full((3, D_MODEL)), _full((1, D_MODEL)),
                  pl.BlockSpec((tm, S5_WIDTH), lambda i: (i + nct, 0)), half,
                  pl.BlockSpec((tm, RET_WIDTH), lambda i: (i + nct, G_COL // 4)),
                  _full((S5_WIDTH, S5_WIDTH)), _full((1, S5_WIDTH)), ANY] + [ANY] * n,
        out_specs=[row, row, _full((8, D_MODEL)),
                   pl.BlockSpec((S5_NB, tm // S5_T, S5_BW), lambda i: (0, i + nct, 0)), half, half,
                   _full((S5_WIDTH, S5_WIDTH)),
                   _full((1, S5_WIDTH))] + [ANY] * n,
        out_shape=[jax.ShapeDtypeStruct((l, D_MODEL), F32), jax.ShapeDtypeStruct((l, D_MODEL), BF16),
                   jax.ShapeDtypeStruct((8, D_MODEL), F32), jax.ShapeDtypeStruct(zero_rows.shape, BF16), f32h, f32h,
                   jax.ShapeDtypeStruct((S5_WIDTH, S5_WIDTH), F32), jax.ShapeDtypeStruct((1, S5_WIDTH), F32)]
        + [jax.ShapeDtypeStruct((4,) + s, p.dtype) for s, p in zip(shapes, pairs)],
        input_output_aliases={13: 3},
        scratch_shapes=[pltpu.SemaphoreType.DMA((n, 3)), pltpu.SemaphoreType.DMA((n, 3)),
                        pltpu.VMEM((tm // S5_T, S5_T, S5_WIDTH), F32)],
        compiler_params=_params(("arbitrary",)),
    )(dup, w_up_b, w_out_b, x1, dx2, mix, mod3, n2w, y_all, y_ret, p_all, w_glu_b, b_glu, zero_rows, *pairs)


MOD_ROWS = 16
MOD_COLS = 6 * D_MODEL // 4


def _mod_fwd(c_all, c_ctx, w_mod_b, b_loc):
    def body(c_ref, cc_ref, w_ref, b_ref, m_ref, s_ref):
        cond = jnp.concatenate([c_ref[...], jnp.broadcast_to(cc_ref[...], (8, D_MODEL))], axis=0)
        s = _silu(cond).astype(BF16)
        s_ref[...] = s
        m_ref[...] = _dot(s, w_ref[...]) + b_ref[...]

    return pl.pallas_call(
        body, name="mod_fwd",
        out_shape=[jax.ShapeDtypeStruct((MOD_ROWS, MOD_COLS), F32), jax.ShapeDtypeStruct((MOD_ROWS, D_MODEL), BF16)],
        compiler_params=_params(),
    )(c_all, c_ctx, w_mod_b, b_loc)


def _mod_bwd_sum(dm_all):
    def body(d_ref, dm_ref, gb_ref):
        rows = [d_ref[k, 0:1, :] for k in range(8)]
        ctx_sum = d_ref[0, 1:2, :]
        for k in range(1, 8):
            ctx_sum = ctx_sum + d_ref[k, 1:2, :]
        gb = ctx_sum
        for k in range(8):
            gb = gb + rows[k]
        gb_ref[...] = gb
        dm_ref[...] = jnp.concatenate(rows + [ctx_sum] + [jnp.zeros((7, 6 * D_MODEL), F32)], axis=0)

    return pl.pallas_call(
        body, name="mod_bwd_sum",
        out_shape=[jax.ShapeDtypeStruct((MOD_ROWS, 6 * D_MODEL), F32), jax.ShapeDtypeStruct((1, 6 * D_MODEL), F32)],
        compiler_params=_params(),
    )(dm_all)


def _mod_bwd_w(dm_loc, s_b, c_ctx, w_mod_b):
    def body(d_ref, s_ref, cc_ref, w_ref, gw_ref, gc_ref):
        db = d_ref[...].astype(BF16)
        gw_ref[...] = _dot_tn(s_ref[...], db)
        ds = _dot_nt(db, w_ref[...])
        _, vjp = jax.vjp(_silu, cc_ref[...])
        gc_ref[...] = jnp.broadcast_to(vjp(ds[8:9, :])[0], (8, D_MODEL))

    return pl.pallas_call(
        body, name="mod_bwd_w",
        out_shape=[jax.ShapeDtypeStruct((D_MODEL, MOD_COLS), F32), jax.ShapeDtypeStruct((8, D_MODEL), F32)],
        compiler_params=_params(),
    )(dm_loc, s_b, c_ctx, w_mod_b)


def _adamw(w, g, m, v, name):
    r, c = w.shape
    tr = _pick(r, (256, 128, 64, 32, 16, 8))
    bc1 = 1.0 - ADAM_B1 ** ADAM_STEP
    bc2 = 1.0 - ADAM_B2 ** ADAM_STEP

    def body(w_ref, g_ref, m_ref, v_ref, d_ref, nm_ref, nv_ref):
        gg = g_ref[...]
        nm = ADAM_B1 * m_ref[...] + (1.0 - ADAM_B1) * gg
        nv = ADAM_B2 * v_ref[...] + (1.0 - ADAM_B2) * (gg * gg)
        nm_ref[...] = nm
        nv_ref[...] = nv
        d_ref[...] = -ADAM_LR * ((nm / bc1) / (jnp.sqrt(nv / bc2) + ADAM_EPS) + ADAM_WD * w_ref[...])

    blk = pl.BlockSpec((tr, c), lambda i: (i, 0))
    shp = jax.ShapeDtypeStruct((r, c), F32)
    return pl.pallas_call(
        body, name=name, grid=(r // tr,), in_specs=[blk] * 4, out_specs=[blk] * 3, out_shape=[shp] * 3,
        compiler_params=_params(("parallel",)),
    )(w, g, m, v)


def _sum_slots(a, name):
    n, r, c = a.shape
    tr = _pick(r, (376, 256, 208, 128, 64, 32, 16, 8))

    def body(a_ref, o_ref):
        acc = a_ref[0].astype(F32)
        for k in range(1, n):
            acc = acc + a_ref[k].astype(F32)
        o_ref[...] = acc

    return pl.pallas_call(
        body, name=name, grid=(r // tr,),
        in_specs=[pl.BlockSpec((n, tr, c), lambda i: (0, i, 0))],
        out_specs=pl.BlockSpec((tr, c), lambda i: (i, 0)),
        out_shape=jax.ShapeDtypeStruct((r, c), F32),
        compiler_params=_params(("parallel",)),
    )(a)


def _mesh_pos():
    return lax.axis_index("x"), lax.axis_index("y"), lax.axis_index("c")


def _all_gather8(v, name):
    m_per, n = v.shape

    def body(x_ref, out_ref, send_sems, recv_sems, local_sem):
        x, y, c = _mesh_pos()
        me, sibling = (x, y, c), (x, y, 1 - c)
        chips = [(1 - x, y), (x, 1 - y), (1 - x, 1 - y)]

        def rows(px, py, pc):
            return out_ref.at[pl.ds((4 * px + 2 * py + pc) * m_per, m_per), :]

        def copy(k, block, to, src=None):
            return pltpu.make_async_remote_copy(
                src_ref=rows(*block) if src is None else src, dst_ref=rows(*block),
                send_sem=send_sems.at[k], recv_sem=recv_sems.at[k], device_id=to, device_id_type=MESH_ID)

        mine = pltpu.make_async_copy(x_ref, rows(*me), local_sem)
        mine.start()
        first = [copy(0, me, sibling, src=x_ref)]
        first += [copy(1 + j, me, (*chip, c), src=x_ref) for j, chip in enumerate(chips)]
        for cp in first:
            cp.start()
        passed = [copy(4 + j, (*chip, c), sibling) for j, chip in enumerate(chips)]
        for j, chip in enumerate(chips):
            copy(1 + j, (*chip, c), me).wait_recv()
            passed[j].start()
        copy(0, sibling, me).wait_recv()
        for j, chip in enumerate(chips):
            copy(4 + j, (*chip, 1 - c), me).wait_recv()
        for cp in first + passed:
            cp.wait_send()
        mine.wait()

    return pl.pallas_call(
        body, name=name,
        out_shape=jax.ShapeDtypeStruct((8 * m_per, n), v.dtype),
        in_specs=[pl.BlockSpec(memory_space=pltpu.VMEM)],
        out_specs=pl.BlockSpec(memory_space=pltpu.VMEM),
        scratch_shapes=[pltpu.SemaphoreType.DMA((7,)), pltpu.SemaphoreType.DMA((7,)), pltpu.SemaphoreType.DMA],
        compiler_params=_params(),
    )(v)


ANY = pl.BlockSpec(memory_space=pl.ANY)
PEER_CHIPS = lambda x, y: [(x, 1 - y), (1 - x, y), (1 - x, 1 - y)]


def _shard_region(ref, kind, k, rl, cl, r0, nr, c0, nc):
    if kind == "col":
        return ref.at[pl.ds(r0, nr), pl.ds(k * cl + c0, nc)]
    return ref.at[pl.ds(k * rl + r0, nr), pl.ds(c0, nc)]


def _place_shard(w, kind, chip, name):
    rl, cl = w.shape
    tr = _pick(rl, (256, 128, 64))
    nt = rl // tr

    def body(chip_ref, w_ref, o_ref):
        o_ref[...] = w_ref[...].astype(BF16)

    o_map = (lambda i, chip_ref: (i, chip_ref[0])) if kind == "col" else (lambda i, chip_ref: (chip_ref[0] * nt + i, 0))
    return pl.pallas_call(
        body, name=name,
        grid_spec=pltpu.PrefetchScalarGridSpec(
            num_scalar_prefetch=1, grid=(nt,),
            in_specs=[pl.BlockSpec((tr, cl), lambda i, chip_ref: (i, 0))], out_specs=pl.BlockSpec((tr, cl), o_map)),
        out_shape=jax.ShapeDtypeStruct((rl, 4 * cl) if kind == "col" else (4 * rl, cl), BF16),
        compiler_params=_params(("parallel",)),
    )(chip.reshape(1), w)


def _gather_shard_shapes(placed, kinds):
    return [(p.shape[0], p.shape[1] // 4) if k == "col" else (p.shape[0] // 4, p.shape[1]) for p, k in zip(placed, kinds)]


def _gather_chip_copies(outs, kinds, shard_shapes, send_sems, recv_sems, with_arrivals=True):
    x, y, c = _mesh_pos()
    me = 2 * x + y
    sends, arrivals = [], []
    for a in range(len(outs)):
        rl, cl = shard_shapes[a]
        rh = rl // 2
        reg = functools.partial(_shard_region, outs[a], kinds[a], rl=rl, cl=cl, r0=c * rh, nr=rh, c0=0, nc=cl)
        for j, (px, py) in enumerate(PEER_CHIPS(x, y)):
            to = dict(send_sem=send_sems.at[a, j], recv_sem=recv_sems.at[a, j], device_id=(px, py, c),
                      device_id_type=MESH_ID)
            sends.append(pltpu.make_async_remote_copy(src_ref=reg(k=me), dst_ref=reg(k=me), **to))
            if with_arrivals:
                got = reg(k=2 * px + py)
                arrivals.append(pltpu.make_async_remote_copy(src_ref=got, dst_ref=got, **to))
    return sends, arrivals


def _gather_sibling_copies(outs, kinds, shard_shapes, send_sems, recv_sems):
    x, y, c = _mesh_pos()
    forwards, arrivals = [], []
    for a in range(len(outs)):
        rl, cl = shard_shapes[a]
        rh = rl // 2
        for j, (px, py) in enumerate(PEER_CHIPS(x, y)):
            to = dict(send_sem=send_sems.at[a, j], recv_sem=recv_sems.at[a, j], device_id=(x, y, 1 - c),
                      device_id_type=MESH_ID)
            reg = functools.partial(_shard_region, outs[a], kinds[a], k=2 * px + py, rl=rl, cl=cl, nr=rh, c0=0, nc=cl)
            forwards.append(pltpu.make_async_remote_copy(src_ref=reg(r0=c * rh), dst_ref=reg(r0=c * rh), **to))
            arrivals.append(pltpu.make_async_remote_copy(src_ref=reg(r0=(1 - c) * rh), dst_ref=reg(r0=(1 - c) * rh), **to))
    return forwards, arrivals


def _gather_weights(placed, kinds):
    n = len(placed)
    shard_shapes = _gather_shard_shapes(placed, kinds)

    def body(*refs):
        outs = refs[n:2 * n]
        ici_send, ici_recv, sib_send, sib_recv = refs[2 * n:]
        sends, arrivals = _gather_chip_copies(outs, kinds, shard_shapes, ici_send, ici_recv)
        for cp in sends:
            cp.start()
        forwards, from_sibling = _gather_sibling_copies(outs, kinds, shard_shapes, sib_send, sib_recv)
        for cp, fwd in zip(arrivals, forwards):
            cp.wait_recv()
            fwd.start()
        for cp in from_sibling:
            cp.wait_recv()
        for cp in sends + forwards:
            cp.wait_send()

    return pl.pallas_call(
        body, name="gather_weights",
        out_shape=[jax.ShapeDtypeStruct(p.shape, p.dtype) for p in placed],
        in_specs=[ANY] * n, out_specs=[ANY] * n, input_output_aliases={a: a for a in range(n)},
        scratch_shapes=[pltpu.SemaphoreType.DMA((n, 3))] * 4,
        compiler_params=_params(),
    )(*placed)


def _gather_sibling(placed, kinds):
    n = len(placed)
    shard_shapes = _gather_shard_shapes(placed, kinds)

    def body(*refs):
        forwards, from_sibling = _gather_sibling_copies(refs[n:2 * n], kinds, shard_shapes, *refs[2 * n:])
        for cp in forwards:
            cp.start()
        for cp in from_sibling:
            cp.wait_recv()
        for cp in forwards:
            cp.wait_send()

    return pl.pallas_call(
        body, name="gather_sibling",
        out_shape=[jax.ShapeDtypeStruct(p.shape, p.dtype) for p in placed],
        in_specs=[ANY] * n, out_specs=[ANY] * n, input_output_aliases={a: a for a in range(n)},
        scratch_shapes=[pltpu.SemaphoreType.DMA((n, 3))] * 2,
        compiler_params=_params(),
    )(*placed)


def _half(kind, r, c):
    return (r // 2, c) if kind == "col" else (r, c // 2)


def _half_of(ref, kind, which):
    r, c = ref.shape
    hr, hc = _half(kind, r, c)
    return ref.at[pl.ds(which * hr, hr), :] if kind == "col" else ref.at[:, pl.ds(which * hc, hc)]


def _rs_sibling(grads, kinds, name):
    n = len(grads)

    def body(*refs):
        srcs, dsts = refs[:n], refs[n:2 * n]
        send_sems, recv_sems = refs[2 * n:]
        x, y, c = _mesh_pos()
        cps = [pltpu.make_async_remote_copy(src_ref=_half_of(srcs[a], kinds[a], 1 - c), dst_ref=dsts[a],
                                            send_sem=send_sems.at[a], recv_sem=recv_sems.at[a],
                                            device_id=(x, y, 1 - c), device_id_type=MESH_ID) for a in range(n)]
        for cp in cps:
            cp.start()
        for cp in cps:
            cp.wait()

    return pl.pallas_call(
        body, name=name,
        out_shape=[jax.ShapeDtypeStruct(_half(k, *g.shape), g.dtype) for g, k in zip(grads, kinds)],
        in_specs=[ANY] * n, out_specs=[ANY] * n,
        scratch_shapes=[pltpu.SemaphoreType.DMA((n,)), pltpu.SemaphoreType.DMA((n,))],
        compiler_params=_params(),
    )(*grads)


def _pair_sum(gf, rv, kind, ci, name):
    r, c = rv.shape
    tr = _pick(r, (128, 64, 32, 16, 8))
    nt = r // tr

    def body(ci_ref, g_ref, r_ref, o_ref):
        o_ref[...] = (g_ref[...] + r_ref[...]).astype(BF16)

    g_map = (lambda i, ci_ref: (ci_ref[0] * nt + i, 0)) if kind == "col" else (lambda i, ci_ref: (i, ci_ref[0]))
    blk = pl.BlockSpec((tr, c), lambda i, ci_ref: (i, 0))
    return pl.pallas_call(
        body, name=name,
        grid_spec=pltpu.PrefetchScalarGridSpec(num_scalar_prefetch=1, grid=(nt,),
                                               in_specs=[pl.BlockSpec((tr, c), g_map), blk], out_specs=blk),
        out_shape=jax.ShapeDtypeStruct((r, c), BF16),
        compiler_params=_params(("parallel",)),
    )(ci.reshape(1), gf, rv)


def _rs_slot_shapes(pairs, kinds):
    return [(p.shape[0], p.shape[1] // 4) if k == "col" else (p.shape[0] // 4, p.shape[1]) for p, k in zip(pairs, kinds)]


def _rs_chip_copies(srcs, dsts, kinds, shapes, send_sems, recv_sems, with_arrivals=True):
    x, y, c = _mesh_pos()
    me = 2 * x + y
    sends, arrivals = [], []
    for a in range(len(srcs)):
        rl, cl = shapes[a]
        reg = functools.partial(_shard_region, srcs[a], kinds[a], rl=rl, cl=cl, r0=0, nr=rl, c0=0, nc=cl)
        for j, (px, py) in enumerate(PEER_CHIPS(x, y)):
            to = dict(send_sem=send_sems.at[a, j], recv_sem=recv_sems.at[a, j], device_id=(px, py, c),
                      device_id_type=MESH_ID)
            sends.append(pltpu.make_async_remote_copy(src_ref=reg(k=2 * px + py), dst_ref=dsts[a].at[me], **to))
            if with_arrivals:
                slot = dsts[a].at[2 * px + py]
                arrivals.append(pltpu.make_async_remote_copy(src_ref=slot, dst_ref=slot, **to))
    return sends, arrivals


def _rs_chips(pairs, kinds):
    n = len(pairs)
    shapes = _rs_slot_shapes(pairs, kinds)

    def body(*refs):
        sends, arrivals = _rs_chip_copies(refs[:n], refs[n:2 * n], kinds, shapes, *refs[2 * n:])
        for cp in sends:
            cp.start()
        for cp in arrivals:
            cp.wait_recv()
        for cp in sends:
            cp.wait_send()

    return pl.pallas_call(
        body, name="rs_chips",
        out_shape=[jax.ShapeDtypeStruct((4,) + s, p.dtype) for s, p in zip(shapes, pairs)],
        in_specs=[ANY] * n, out_specs=[ANY] * n,
        scratch_shapes=[pltpu.SemaphoreType.DMA((n, 3)), pltpu.SemaphoreType.DMA((n, 3))],
        compiler_params=_params(),
    )(*pairs)


def _sum_chips(pair, got, kind, pos, name):
    _, r, c = got.shape
    tr = _pick(r, (256, 128, 64, 32, 16))
    nt = r // tr

    def body(pos_ref, own_ref, g1_ref, g2_ref, g3_ref, o_ref):
        o_ref[...] = ((own_ref[...].astype(F32) + g1_ref[0].astype(F32)) + g2_ref[0].astype(F32)) + g3_ref[0].astype(F32)

    if kind == "col":
        own_map = lambda i, p: (i, p[1])
        out_map = lambda i, p: (p[0] * nt + i, 0)
        out_shape = (2 * r, c)
    else:
        own_map = lambda i, p: (p[1] * nt + i, 0)
        out_map = lambda i, p: (i, p[0])
        out_shape = (r, 2 * c)
    peer = lambda m: pl.BlockSpec((1, tr, c), lambda i, p: (p[1] ^ m, i, 0))
    return pl.pallas_call(
        body, name=name,
        grid_spec=pltpu.PrefetchScalarGridSpec(
            num_scalar_prefetch=1, grid=(nt,),
            in_specs=[pl.BlockSpec((tr, c), own_map), peer(1), peer(2), peer(3)],
            out_specs=pl.BlockSpec((tr, c), out_map)),
        out_shape=jax.ShapeDtypeStruct(out_shape, F32),
        compiler_params=_params(("parallel",)),
    )(pos, pair, got, got, got)


def _rs_back(halves, kinds):
    n = len(halves)

    def body(*refs):
        outs = refs[n:2 * n]
        send_sems, recv_sems = refs[2 * n:]
        x, y, c = _mesh_pos()
        cps = []
        for a in range(n):
            mine = _half_of(outs[a], kinds[a], c)
            cps.append(pltpu.make_async_remote_copy(src_ref=mine, dst_ref=mine, send_sem=send_sems.at[a],
                                                    recv_sem=recv_sems.at[a], device_id=(x, y, 1 - c),
                                                    device_id_type=MESH_ID))
            cps[-1].start()
        for a in range(n):
            other = _half_of(outs[a], kinds[a], 1 - c)
            pltpu.make_async_remote_copy(src_ref=other, dst_ref=other, send_sem=send_sems.at[a],
                                         recv_sem=recv_sems.at[a], device_id=(x, y, 1 - c),
                                         device_id_type=MESH_ID).wait_recv()
        for cp in cps:
            cp.wait_send()

    return pl.pallas_call(
        body, name="rs_back",
        out_shape=[jax.ShapeDtypeStruct(h.shape, h.dtype) for h in halves],
        in_specs=[ANY] * n, out_specs=[ANY] * n, input_output_aliases={a: a for a in range(n)},
        scratch_shapes=[pltpu.SemaphoreType.DMA((n,)), pltpu.SemaphoreType.DMA((n,))],
        compiler_params=_params(),
    )(*halves)


def _rope_tables(l, lc):
    rows = l // GRID_W
    n_freq = RET_DH // 4
    inv_freq = ROPE_THETA ** (-jnp.arange(n_freq, dtype=F32) / n_freq)
    sign = jnp.tile(jnp.array([-1.0, 1.0], F32), n_freq)

    def half(n):
        ang = jnp.repeat(jnp.arange(n, dtype=F32)[:, None] * inv_freq, 2, axis=-1)
        return jnp.cos(ang), jnp.sin(ang) * sign

    (cr, sr), (cc, sc) = half(rows), half(GRID_W)
    grid = lambda r, c: jnp.concatenate([jnp.repeat(r, GRID_W, axis=0), jnp.tile(c, (rows, 1))], axis=-1)
    cos_t = jnp.concatenate([jnp.ones((lc, RET_DH), F32), grid(cr, cc)], axis=0)
    sin_t = jnp.concatenate([jnp.zeros((lc, RET_DH), F32), grid(sr, sc)], axis=0)
    return cos_t, sin_t


def _s5_pack(a):
    blk = lambda t: t.reshape(1, S5_NB, 128, S5_STATE)
    lre = jnp.stack([a["s5_lambda_re_f"][0], a["s5_lambda_re_b"][0]]).reshape(2, S5_NB, 8, S5_STATE)
    lim = jnp.stack([a["s5_lambda_im_f"][0], a["s5_lambda_im_b"][0]]).reshape(2, S5_NB, 8, S5_STATE)
    lst = jnp.stack([a["s5_log_step_f"][0], a["s5_log_step_b"][0]]).reshape(2, S5_NB, 8, 1)
    b_re = blk(a["s5_b_re"][0].transpose(0, 2, 1))
    b_im = blk(a["s5_b_im"][0].transpose(0, 2, 1))
    return (lre, lim, lst, b_re, b_im, blk(a["s5_c_re"][0]), blk(a["s5_c_im"][0]),
            a["s5_d"].reshape(1, S5_NB, 1, 128))


def _s5_unpack(g):
    glre, glim, glst, gbre, gbim, gcre, gcim, gd = g
    unb = lambda t: t.reshape(S5_GROUPS, S5_GROUP, S5_STATE).transpose(0, 2, 1)[None]
    return {
        "s5_lambda_re_f": glre[0].reshape(1, S5_GROUPS, S5_STATE), "s5_lambda_re_b": glre[1].reshape(1, S5_GROUPS, S5_STATE),
        "s5_lambda_im_f": glim[0].reshape(1, S5_GROUPS, S5_STATE), "s5_lambda_im_b": glim[1].reshape(1, S5_GROUPS, S5_STATE),
        "s5_log_step_f": glst[0].reshape(1, S5_GROUPS), "s5_log_step_b": glst[1].reshape(1, S5_GROUPS),
        "s5_b_re": unb(gbre), "s5_b_im": unb(gbim),
        "s5_c_re": gcre.reshape(1, S5_GROUPS, S5_GROUP, S5_STATE), "s5_c_im": gcim.reshape(1, S5_GROUPS, S5_GROUP, S5_STATE),
        "s5_d": gd.reshape(1, S5_WIDTH),
    }


def _local_step(a, wb, late, mx, mc, conv_w, ci):
    x, ctx, tgt = a["x"][0], a["ctx"][0], a["loss_target"][0]
    l, lc = x.shape[0], ctx.shape[0]
    la = l + lc
    nct, ncc, nrc, cn = lc // TOK_TILE, lc // S5_T, lc // RET_CHUNK, la // S5_T
    n1w, n2w, fnw = a["norm1_w"], a["norm2_w"], a["final_norm_w"].reshape(1, D_MODEL)
    conv_b, b_glu = a["conv_b"], a["s5_b_glu"]
    ld2 = jnp.concatenate([a["ret_log_decay_f"], a["ret_log_decay_b"]], axis=0)
    mod4 = jnp.concatenate([mc[0:2], mx[0:2]], axis=0)
    mod3 = mx[2:5]
    gate5 = mx[5:6]
    cos_t, sin_t = _rope_tables(l, lc)
    s5p = _s5_pack(a)

    p_all, h1b, p3, w_up_p = _norm_inproj(x, ctx, n1w, mod4, wb["w_in"], [late[1]], (LATE_KINDS[1],))
    gg, xw, yw, a16 = _s5_gen(*s5p)
    sloc = _s5_to_states(p3, xw, "s5_state")
    a16s = a16.transpose(1, 0, 2, 3).reshape(2, S5_GROUPS, 128)
    hs, w_out_p, w_down_p = _s5_scan(sloc, a16s, ncc, [late[0], late[2]], (LATE_KINDS[0], LATE_KINDS[2]))
    y_all = _s5_from_states(p3, gg, hs, yw, False, "s5_out").reshape(la, S5_WIDTH)
    tabs = _ret_tables(ld2)
    of, ob, ssf, ssb = _ret_scan(p_all, cos_t, sin_t, tabs, nrc)
    wb = {**wb, **dict(zip(LATE_NAMES, _gather_sibling([w_out_p, w_up_p, w_down_p], LATE_KINDS)))}
    x1, mix, h2b, up, mixb, y_ret = _outproj_up(x, y_all, of, ob, p_all, wb["s5_w_glu"], b_glu, wb["w_out"],
                                                     mod3, n2w, wb["w_up"], nct)
    act, dx2, ddn, dact, acc_f = _ffn_loss(up, x1, conv_w, conv_b, wb["w_down"], gate5, fnw, tgt)

    g = {}
    g["w_down"] = _mm_tn(act, ddn, name="gw_down")
    dup, acc_c = _convglu_bwd(up, dact, conv_w, conv_b)
    g["w_up"] = _mm_tn(h2b, dup, name="gw_up")
    first = [g[n] for n in FIRST_GRADS]
    first_pairs = [_pair_sum(gf, rv, k, ci, "rs_pair_" + n)
                   for gf, rv, k, n in zip(first, _rs_sibling(first, FIRST_KINDS, "rs_sibling_first"), FIRST_KINDS, FIRST_GRADS)]
    dx1, dmixb, acc_2, dy3, dy_ret, dg, g["s5_w_glu"], g["s5_b_glu"], *first_got = _up_bwd(
        dup, wb["w_up"], wb["w_out"], x1, dx2, mix, mod3, n2w, y_all, y_ret, p_all, wb["s5_w_glu"], b_glu,
        jnp.zeros(p3.shape, BF16), nct, first_pairs, FIRST_KINDS)
    g["w_out"] = _mm_tn(mixb, dmixb, name="gw_out")

    e = _s5_to_states(dy3, yw, "s5_bwd_h")
    ds, da16 = _s5_scan_bwd(e, hs, a16s, ncc)
    du = _s5_from_states(dy3, gg, ds, xw, True, "s5_bwd_u").reshape(la, S5_WIDTH)
    dkb = _s5_bwd_kb(p3, dy3)
    dwst = _s5_bwd_w(p3, ds, "s5_bwd_wst")
    dwout = _s5_bwd_w(dy3, hs, "s5_bwd_wout")
    da16p = da16.reshape(2, S5_NB, 8, 128).transpose(1, 0, 2, 3)
    g.update(_s5_unpack(_s5_gen_bwd(*s5p, dkb, dwst, dwout, da16p)))

    dqf, dkf, dvf, dqb, dkb_, dvb, dld = _ret_scan_bwd(p_all, cos_t, sin_t, tabs, ssf, ssb, dy_ret, nrc)
    g["ret_log_decay_f"] = dld[0, :, 0, 0].reshape(1, RET_HEADS)
    g["ret_log_decay_b"] = dld[1, :, 0, 0].reshape(1, RET_HEADS)
    dp, grad_x, acc_1 = _in_bwd(dqf, dkf, dvf, dqb, dkb_, dvb, du, dg, cos_t, sin_t, wb["w_in"], x, ctx, n1w, mod4, dx1)
    g["w_in"] = _mm_tn(h1b, dp, name="gw_in")

    g["norm1_w"], g["norm2_w"], g["final_norm_w"] = acc_1[0:1], acc_2[0:1], acc_f[0]
    g["conv_w"], g["conv_b"] = acc_c[0:3], acc_c[3:4]
    zero = jnp.zeros((1, D_MODEL), F32)
    dmx = jnp.concatenate([acc_1[3:5], acc_2[1:2], acc_2[2:4], acc_f[1:2]], axis=0)
    dmc = jnp.concatenate([acc_1[1:3], zero, zero, zero, zero], axis=0)
    return acc_f[2, 0], grad_x, g, dmx, dmc, first_pairs, first_got


WEIGHT_NAMES = ("c_ctx", "w_mod", "b_mod", "norm1_w", "w_in", "s5_lambda_re_f", "s5_lambda_im_f", "s5_log_step_f",
                "s5_lambda_re_b", "s5_lambda_im_b", "s5_log_step_b", "s5_b_re", "s5_b_im", "s5_c_re", "s5_c_im",
                "s5_d", "s5_w_glu", "s5_b_glu", "ret_log_decay_f", "ret_log_decay_b", "w_out", "norm2_w", "w_up",
                "conv_w", "conv_b", "w_down", "final_norm_w")
BIG_NAMES = ("w_in", "w_out", "w_up", "w_down", "s5_w_glu")
BIG_KINDS = ("col", "row", "col", "row", "row")
EARLY_NAMES, EARLY_KINDS = ("w_in", "s5_w_glu"), ("col", "row")
LATE_NAMES, LATE_KINDS = ("w_out", "w_up", "w_down"), ("row", "col", "row")
FIRST_GRADS, FIRST_KINDS = ("w_down", "w_up"), ("row", "col")
LAST_GRADS, LAST_KINDS = ("w_in", "w_out", "s5_w_glu"), ("col", "row", "row")
SMALL_NAMES = ("norm1_w", "norm2_w", "final_norm_w", "conv_b", "conv_w", "s5_lambda_re_f", "s5_lambda_im_f",
               "s5_log_step_f", "s5_lambda_re_b", "s5_lambda_im_b", "s5_log_step_b", "s5_b_re", "s5_b_im", "s5_c_re",
               "s5_c_im", "s5_d", "s5_b_glu", "ret_log_decay_f", "ret_log_decay_b")
ROW = 1024
N_CHIPS = 4


def _pack_rows(parts):
    flat = jnp.concatenate([p.reshape(-1) for p in parts])
    n = flat.shape[0]
    rows = -(-n // (8 * ROW)) * 8
    return jnp.pad(flat, (0, rows * ROW - n)).reshape(rows, ROW)


def _unpack_rows(packed, shapes):
    flat = packed.reshape(-1)
    out, off = [], 0
    for s in shapes:
        n = math.prod(s)
        out.append(flat[off:off + n].reshape(s))
        off += n
    return out


def _step(a):
    xi, yi, ci = _mesh_pos()
    chip = 2 * xi + yi
    dev = 2 * chip + ci

    cw_loc = a["conv_w"].reshape(-1)
    small_in = jnp.concatenate([a["c"].reshape(-1), jnp.pad(cw_loc, (0, 24 * 128 - cw_loc.shape[0]))]).reshape(32, 128)
    sg = _all_gather8(small_in, "gather_cond").reshape(8, 32, 128)
    c_all = sg[:, 0:8].reshape(8, D_MODEL)
    conv_w = sg[0::2, 8:32].reshape(N_CHIPS, -1)[:, :cw_loc.shape[0]].reshape(N_CHIPS, 3, -1)
    conv_w = conv_w.transpose(1, 0, 2).reshape(3, D_FF)

    placed = {n: _place_shard(a[n][0], k, chip, "place_" + n) for n, k in zip(BIG_NAMES, BIG_KINDS)}
    wb = dict(zip(EARLY_NAMES, _gather_weights([placed[n] for n in EARLY_NAMES], EARLY_KINDS)))
    late = [placed[n] for n in LATE_NAMES]

    w_mod_b = a["w_mod"][0].astype(BF16)
    c_ctx = a["c_ctx"].reshape(1, D_MODEL)
    b_loc = lax.dynamic_slice_in_dim(a["b_mod"], chip * MOD_COLS, MOD_COLS, 1)
    m_loc, s_b = _mod_fwd(c_all, c_ctx, w_mod_b, b_loc)
    mg = _all_gather8(m_loc, "gather_mod").reshape(8, MOD_ROWS, MOD_COLS)
    m_full = mg[0::2].transpose(1, 0, 2).reshape(MOD_ROWS, 6 * D_MODEL)
    mx = lax.dynamic_slice_in_dim(m_full, dev, 1, 0).reshape(6, D_MODEL)
    mc = m_full[8].reshape(6, D_MODEL)

    loss_part, grad_x, g, dmx, dmc, first_pairs, first_got = _local_step(a, wb, late, mx, mc, conv_w, ci)
    loss = lax.psum(loss_part, ("x", "y", "c"))

    dm_pair = jnp.concatenate([dmx.reshape(1, -1), dmc.reshape(1, -1), jnp.zeros((6, 6 * D_MODEL), F32)], axis=0)
    dm_all = _all_gather8(dm_pair, "gather_dmod").reshape(8, 8, 6 * D_MODEL)
    dm16, gb_mod = _mod_bwd_sum(dm_all)
    dm_loc = lax.dynamic_slice_in_dim(dm16, chip * MOD_COLS, MOD_COLS, 1)
    gw_mod, gcc = _mod_bwd_w(dm_loc, s_b, c_ctx, w_mod_b)

    small_parts = [g[n] for n in SMALL_NAMES] + [gcc[0]]
    small_shapes = [p.shape for p in small_parts]
    sp = _pack_rows(small_parts)
    tot = _sum_slots(_all_gather8(sp, "gather_small_grads").reshape(8, sp.shape[0], ROW), "sum_small_grads")
    small = dict(zip(SMALL_NAMES + ("c_ctx",), _unpack_rows(tot, small_shapes)))
    grads = {n: small[n].reshape(a[n].shape) for n in SMALL_NAMES if n != "conv_w"}
    grads["c_ctx"] = (0.5 * small["c_ctx"]).reshape(a["c_ctx"].shape)
    grads["conv_w"] = lax.dynamic_slice_in_dim(small["conv_w"], chip * (D_FF // N_CHIPS), D_FF // N_CHIPS, 1)[None]
    grads["b_mod"] = gb_mod
    grads["w_mod"] = gw_mod[None]

    last = [g[n] for n in LAST_GRADS]
    last_pairs = [_pair_sum(gf, rv, k, ci, "rs_pair_" + n)
                  for gf, rv, k, n in zip(last, _rs_sibling(last, LAST_KINDS, "rs_sibling_last"), LAST_KINDS, LAST_GRADS)]
    last_got = _rs_chips(last_pairs, LAST_KINDS)
    pos = jnp.stack([ci, chip])
    order = FIRST_GRADS + LAST_GRADS
    order_kinds = FIRST_KINDS + LAST_KINDS
    halves = [_sum_chips(p, t, k, pos, "rs_sum_" + n)
              for p, t, k, n in zip(first_pairs + last_pairs, list(first_got) + list(last_got), order_kinds, order)]
    for n, t in zip(order, _rs_back(halves, order_kinds)):
        grads[n] = t[None]

    delta, new_m, new_v = {}, {}, {}
    for n in BIG_NAMES + ("w_mod",):
        for dst, t in zip((delta, new_m, new_v), _adamw(a[n][0], grads[n][0], a["m_" + n][0], a["v_" + n][0], "adamw_" + n)):
            dst[n] = t[None]
    rest = [n for n in WEIGHT_NAMES if n not in BIG_NAMES and n != "w_mod"]
    shapes = [a[n].shape for n in rest]
    pr = lambda pre: _pack_rows([a[pre + n] for n in rest])
    for dst, t in zip((delta, new_m, new_v),
                      _adamw(pr(""), _pack_rows([grads[n] for n in rest]), pr("m_"), pr("v_"), "adamw_small")):
        dst.update(zip(rest, _unpack_rows(t, shapes)))

    return (loss, grad_x[None], *[grads[n] for n in WEIGHT_NAMES], *[delta[n] for n in WEIGHT_NAMES],
            *[new_m[n] for n in WEIGHT_NAMES], *[new_v[n] for n in WEIGHT_NAMES])


def kernel(x, c, ctx, c_ctx, w_mod, b_mod, norm1_w, w_in, s5_lambda_re_f, s5_lambda_im_f, s5_log_step_f, s5_lambda_re_b, s5_lambda_im_b, s5_log_step_b, s5_b_re, s5_b_im, s5_c_re, s5_c_im, s5_d, s5_w_glu, s5_b_glu, ret_log_decay_f, ret_log_decay_b, w_out, norm2_w, w_up, conv_w, conv_b, w_down, final_norm_w, loss_target, m_c_ctx, m_w_mod, m_b_mod, m_norm1_w, m_w_in, m_s5_lambda_re_f, m_s5_lambda_im_f, m_s5_log_step_f, m_s5_lambda_re_b, m_s5_lambda_im_b, m_s5_log_step_b, m_s5_b_re, m_s5_b_im, m_s5_c_re, m_s5_c_im, m_s5_d, m_s5_w_glu, m_s5_b_glu, m_ret_log_decay_f, m_ret_log_decay_b, m_w_out, m_norm2_w, m_w_up, m_conv_w, m_conv_b, m_w_down, m_final_norm_w, v_c_ctx, v_w_mod, v_b_mod, v_norm1_w, v_w_in, v_s5_lambda_re_f, v_s5_lambda_im_f, v_s5_log_step_f, v_s5_lambda_re_b, v_s5_lambda_im_b, v_s5_log_step_b, v_s5_b_re, v_s5_b_im, v_s5_c_re, v_s5_c_im, v_s5_d, v_s5_w_glu, v_s5_b_glu, v_ret_log_decay_f, v_ret_log_decay_b, v_w_out, v_norm2_w, v_w_up, v_conv_w, v_conv_b, v_w_down, v_final_norm_w):
    return _step(dict(locals()))
```

```python
import functools
import math

import jax
import jax.numpy as jnp
from jax import lax
from jax.experimental import pallas as pl
from jax.experimental.pallas import tpu as pltpu

F32 = jnp.float32
BF16 = jnp.bfloat16

D_MODEL = 1024
S5_WIDTH = 512
S5_GROUPS = 32
S5_GROUP = 16
S5_STATE = 64
RET_WIDTH = 512
RET_HEADS = 4
RET_DH = 128
RET_CHUNK = 256
GRID_W = 64
ROPE_THETA = 10000.0
D_FF = 2816
NORM_EPS = 1e-6
IN_COLS = S5_WIDTH + 4 * RET_WIDTH

S5_T = 16
S5_NB = 4
S5_BW = S5_T * 128
S5_SW = 8 * 2 * S5_STATE

ADAM_LR, ADAM_B1, ADAM_B2, ADAM_EPS, ADAM_WD, ADAM_STEP = 0.001, 0.9, 0.999, 1e-08, 0.01, 10

VMEM_LIMIT = 56 * 1024 * 1024
MM_TN_VMEM = 40 * 1024 * 1024
MESH_ID = pl.DeviceIdType.MESH


def _params(sem=None):
    return pltpu.CompilerParams(dimension_semantics=sem, vmem_limit_bytes=VMEM_LIMIT)


def _full(shape):
    n = len(shape)
    return pl.BlockSpec(shape, lambda *_: (0,) * n)


def _dot(a, b):
    return jnp.dot(a, b, preferred_element_type=F32)


def _dot_nt(a, b):
    return lax.dot_general(a, b, (((1,), (1,)), ((), ())), preferred_element_type=F32)


def _dot_tn(a, b):
    return lax.dot_general(a, b, (((0,), (0,)), ((), ())), preferred_element_type=F32)


def _dot_hi(a, b):
    return jnp.dot(a, b, preferred_element_type=F32, precision=lax.Precision.HIGHEST)


def _dot_nt_hi(a, b):
    return lax.dot_general(a, b, (((1,), (1,)), ((), ())), preferred_element_type=F32,
                           precision=lax.Precision.HIGHEST)


def _gelu(x):
    return 0.5 * x * (1.0 + jnp.tanh(0.7978845608028654 * (x + 0.044715 * (x * x * x))))


def _gelu_and_grad(x):
    c, ca = 0.7978845608028654, 0.7978845608028654 * 0.044715
    x2 = x * x
    t = jnp.tanh(x * (c + ca * x2))
    h = 0.5 * x
    return h + h * t, 0.5 + 0.5 * t + h * (1.0 - t * t) * (c + 3.0 * ca * x2)


def _sigmoid(x):
    return 1.0 / (1.0 + jnp.exp(-x))


def _silu(x):
    return x * _sigmoid(x)


def _rms_mod(x, nw, sh, sc):
    r = lax.rsqrt(jnp.mean(x * x, axis=-1, keepdims=True) + NORM_EPS)
    return (x * r * nw) * (1.0 + sc) + sh


def _rms(x, nw):
    r = lax.rsqrt(jnp.mean(x * x, axis=-1, keepdims=True) + NORM_EPS)
    return x * r * nw


def _head_norm_gate(y, g):
    mu = jnp.mean(y, axis=-1, keepdims=True)
    yc = y - mu
    var = jnp.mean(yc * yc, axis=-1, keepdims=True)
    return _silu(g) * (yc * lax.rsqrt(var + NORM_EPS))


def _swap_pairs(t):
    lane = lax.broadcasted_iota(jnp.int32, t.shape, 1)
    return jnp.where(lane % 2 == 0, pltpu.roll(t, RET_DH - 1, 1), pltpu.roll(t, 1, 1))


def _rope(t, cos_t, sin_t):
    return t * cos_t + _swap_pairs(t) * sin_t


def _rope_t(dt, cos_t, sin_t):
    return dt * cos_t + _swap_pairs(dt * sin_t)


def _pick(n, prefs):
    for p in prefs:
        if n % p == 0:
            return p
    return n


def _mm_tn(a, b, *, name):
    m, k = a.shape
    n = b.shape[1]
    tn = _pick(n, (1408, 1024, 1280, 512))
    fits = lambda t: 2 * (2 * t * k + 2 * t * tn + 4 * k * tn) <= MM_TN_VMEM
    tm = _pick(m, [t for t in (2816, 2048, 1024, 768, 512, 256) if fits(t)] + [128])

    def body(a_ref, b_ref, o_ref):
        @pl.when(pl.program_id(1) == 0)
        def _():
            o_ref[...] = jnp.zeros_like(o_ref)
        o_ref[...] += _dot_tn(a_ref[...], b_ref[...])

    return pl.pallas_call(
        body, name=name, grid=(n // tn, m // tm),
        in_specs=[pl.BlockSpec((tm, k), lambda j, i: (i, 0)), pl.BlockSpec((tm, tn), lambda j, i: (i, j))],
        out_specs=pl.BlockSpec((k, tn), lambda j, i: (0, j)),
        out_shape=jax.ShapeDtypeStruct((k, n), F32),
        compiler_params=_params(("parallel", "arbitrary")),
    )(a, b)


TOK_TILE = 256


def _behind(step, last, copies):
    @pl.when(step == 0)
    def _():
        for cp in copies(with_arrivals=False)[0]:
            cp.start()

    @pl.when(step == last)
    def _():
        sends, arrivals = copies()
        for cp in arrivals:
            cp.wait_recv()
        for cp in sends:
            cp.wait_send()


def _s5_put_rows(rows_ref, scr, val):
    nchunk = scr.shape[0]
    for c in range(nchunk):
        scr[c] = val[S5_T * c:S5_T * (c + 1), :]
    for b in range(S5_NB):
        for j in range(S5_T):
            rows_ref[b, :, 128 * j:128 * (j + 1)] = scr[:, j, 128 * b:128 * (b + 1)].astype(BF16)


def _norm_inproj(x, ctx, n1w, mod4, w_in_b, placed, kinds):
    l, lc = x.shape[0], ctx.shape[0]
    tm = TOK_TILE
    nct = lc // tm
    la = l + lc
    n = len(placed)
    shard_shapes = _gather_shard_shapes(placed, kinds)

    def body(x_ref, c_ref, nw_ref, mod_ref, w_ref, *rest):
        p_ref, h_ref, u_ref = rest[n:n + 3]
        send_sems, recv_sems, u_scr = rest[2 * n + 3:]
        _behind(pl.program_id(0), la // tm - 1,
                functools.partial(_gather_chip_copies, rest[n + 3:2 * n + 3], kinds, shard_shapes, send_sems, recv_sems))
        is_ctx = pl.program_id(0) < nct
        xt = jnp.where(is_ctx, c_ref[...], x_ref[...])
        sh = jnp.where(is_ctx, mod_ref[0:1, :], mod_ref[2:3, :])
        sc = jnp.where(is_ctx, mod_ref[1:2, :], mod_ref[3:4, :])
        hb = _rms_mod(xt, nw_ref[...], sh, sc).astype(BF16)
        h_ref[...] = hb
        p = _dot(hb, w_ref[...])
        p_ref[...] = p
        _s5_put_rows(u_ref, u_scr, p[:, 0:S5_WIDTH])

    return pl.pallas_call(
        body, name="norm_inproj", grid=(la // tm,),
        in_specs=[pl.BlockSpec((tm, D_MODEL), lambda i: (jnp.maximum(i - nct, 0), 0)),
                  pl.BlockSpec((tm, D_MODEL), lambda i: (jnp.minimum(i, nct - 1), 0)),
                  _full((1, D_MODEL)), _full((4, D_MODEL)), _full((D_MODEL, IN_COLS))] + [ANY] * n,
        out_specs=[pl.BlockSpec((tm, IN_COLS), lambda i: (i, 0)), pl.BlockSpec((tm, D_MODEL), lambda i: (i, 0)),
                   pl.BlockSpec((S5_NB, tm // S5_T, S5_BW), lambda i: (0, i, 0))] + [ANY] * n,
        out_shape=[jax.ShapeDtypeStruct((la, IN_COLS), F32), jax.ShapeDtypeStruct((la, D_MODEL), BF16),
                   jax.ShapeDtypeStruct((S5_NB, la // S5_T, S5_BW), BF16)]
        + [jax.ShapeDtypeStruct(p.shape, p.dtype) for p in placed],
        input_output_aliases={5 + a: 3 + a for a in range(n)},
        scratch_shapes=[pltpu.SemaphoreType.DMA((n, 3)), pltpu.SemaphoreType.DMA((n, 3)),
                        pltpu.VMEM((tm // S5_T, S5_T, S5_WIDTH), F32)],
        compiler_params=_params(("arbitrary",)),
    )(x, ctx, n1w, mod4, w_in_b, *placed)


def _iota2(shape, dim):
    return lax.broadcasted_iota(jnp.int32, shape, dim)


def _group_mask(rows, cols, row_div, col_div):
    return jnp.where(_iota2((rows, cols), 0) // row_div == _iota2((rows, cols), 1) // col_div, 1.0, 0.0).astype(F32)


def _s5_gen_dir(lre, lim, lst, b_re, b_im, c_re, c_im):
    step = jnp.exp(lst)
    mag = jnp.exp(lre * step)
    ar = mag * jnp.cos(lim * step)
    ai = mag * jnp.sin(lim * step)
    den = lre * lre + lim * lim
    xr = ar - 1.0
    cr = (xr * lre + ai * lim) / den
    ci = (ai * lre - xr * lim) / den
    rexp = _group_mask(128, 8, S5_GROUP, 1)
    are, aie = _dot_hi(rexp, ar), _dot_hi(rexp, ai)
    cre, cie = _dot_hi(rexp, cr), _dot_hi(rexp, ci)
    bbr = cre * b_re - cie * b_im
    bbi = cre * b_im + cie * b_re
    gmask = _group_mask(128, 128, S5_GROUP, S5_GROUP)
    pr, pi = jnp.ones_like(are), jnp.zeros_like(are)
    xs, ys = [], []
    for t in range(S5_T + 1):
        if t < S5_T:
            xs.append(jnp.concatenate([bbr * pr - bbi * pi, bbr * pi + bbi * pr], axis=1))
        ys.append(jnp.concatenate([c_re * pr - c_im * pi, -(c_re * pi + c_im * pr)], axis=1))
        pr, pi = pr * are - pi * aie, pr * aie + pi * are
    gs = [_dot_nt_hi(x_t, ys[0]) * gmask for x_t in xs]
    r16, i16 = ar, ai
    for _ in range(4):
        r16, i16 = r16 * r16 - i16 * i16, 2.0 * r16 * i16
    return xs, ys, gs, jnp.concatenate([r16, i16], axis=1)


def _s5_expand(z):
    return jnp.concatenate([z] * 8, axis=1) * _group_mask(128, S5_SW, S5_GROUP, 128)


def _s5_contract(z):
    zm = z * _group_mask(128, S5_SW, S5_GROUP, 128)
    acc = zm[:, 0:128]
    for k in range(1, 8):
        acc = acc + zm[:, 128 * k:128 * (k + 1)]
    return acc


def _s5_param_specs():
    blk3 = lambda r, c: pl.BlockSpec((1, 1, r, c), lambda b, *_: (0, b, 0, 0))
    dir3 = lambda r, c: pl.BlockSpec((2, 1, r, c), lambda b, *_: (0, b, 0, 0))
    return [dir3(8, S5_STATE), dir3(8, S5_STATE), dir3(8, 1), blk3(128, S5_STATE), blk3(128, S5_STATE),
            blk3(128, S5_STATE), blk3(128, S5_STATE), blk3(1, 128)]


def _s5_gen(lre, lim, lst, b_re, b_im, c_re, c_im, dvec):
    def body(lre_ref, lim_ref, lst_ref, bre_ref, bim_ref, cre_ref, cim_ref, d_ref, gg_ref, xw_ref, yw_ref, a16_ref):
        eye = _group_mask(128, 128, 1, 1)
        g0 = eye * d_ref[0, 0]
        for dr in range(2):
            xs, ys, gs, a16 = _s5_gen_dir(lre_ref[dr, 0], lim_ref[dr, 0], lst_ref[dr, 0], bre_ref[0, 0],
                                          bim_ref[0, 0], cre_ref[0, 0], cim_ref[0, 0])
            a16_ref[0, dr] = a16
            for j in range(S5_T):
                xw_ref[0, dr, j] = xs[S5_T - 1 - j if dr == 0 else j]
                yw_ref[0, dr, j] = ys[j + 1 if dr == 0 else S5_T - j]
            g0 = g0 + gs[0]
            for t in range(1, S5_T):
                gg_ref[0, (S5_T - 1) + t if dr == 0 else (S5_T - 1) - t] = gs[t]
        gg_ref[0, S5_T - 1] = g0

    blk = pl.BlockSpec((1, 2, S5_T, 128, 128), lambda b: (b, 0, 0, 0, 0))
    return pl.pallas_call(
        body, name="s5_gen", grid=(S5_NB,),
        in_specs=_s5_param_specs(),
        out_specs=[pl.BlockSpec((1, 2 * S5_T - 1, 128, 128), lambda b: (b, 0, 0, 0)), blk, blk,
                   pl.BlockSpec((1, 2, 8, 128), lambda b: (b, 0, 0, 0))],
        out_shape=[jax.ShapeDtypeStruct((S5_NB, 2 * S5_T - 1, 128, 128), F32),
                   jax.ShapeDtypeStruct((S5_NB, 2, S5_T, 128, 128), F32),
                   jax.ShapeDtypeStruct((S5_NB, 2, S5_T, 128, 128), F32),
                   jax.ShapeDtypeStruct((S5_NB, 2, 8, 128), F32)],
        compiler_params=_params(("parallel",)),
    )(lre, lim, lst, b_re, b_im, c_re, c_im, dvec)


def _s5_fill_state_mat(w_scr, src_ref, dr):
    for j in range(S5_T):
        w_scr[128 * j:128 * (j + 1), :] = _s5_expand(src_ref[0, dr, j]).astype(BF16)


def _s5_fill_toeplitz(k_scr, gg_ref):
    for j in range(S5_T):
        for i in range(S5_T):
            k_scr[128 * j:128 * (j + 1), 128 * i:128 * (i + 1)] = gg_ref[0, i - j + (S5_T - 1)].astype(BF16)


S5_GEN_SPECS = [pl.BlockSpec((1, 2 * S5_T - 1, 128, 128), lambda b: (b, 0, 0, 0)),
                pl.BlockSpec((1, 2, S5_T, 128, 128), lambda b: (b, 0, 0, 0, 0))]


def _s5_gen_bwd(lre, lim, lst, b_re, b_im, c_re, c_im, dvec, dg, dx, dy, da16):
    def body(lre_ref, lim_ref, lst_ref, bre_ref, bim_ref, cre_ref, cim_ref, d_ref, dg_ref, dx_ref, dy_ref, da16_ref,
             glre_ref, glim_ref, glst_ref, gbre_ref, gbim_ref, gcre_ref, gcim_ref, gd_ref):
        eye = _group_mask(128, 128, 1, 1)
        gd_ref[0, 0] = jnp.sum(dg_ref[0, S5_T - 1] * eye, axis=0, keepdims=True)
        gb = [None, None, None, None]
        for dr in range(2):
            args = (lre_ref[dr, 0], lim_ref[dr, 0], lst_ref[dr, 0], bre_ref[0, 0], bim_ref[0, 0],
                    cre_ref[0, 0], cim_ref[0, 0])
            _, vjp = jax.vjp(_s5_gen_dir, *args)
            dxs = [dx_ref[0, dr, S5_T - 1 - t if dr == 0 else t] for t in range(S5_T)]
            dys = [jnp.zeros((128, 128), F32)] + [dy_ref[0, dr, t - 1 if dr == 0 else S5_T - t]
                                                  for t in range(1, S5_T + 1)]
            dgs = [dg_ref[0, (S5_T - 1) + t if dr == 0 else (S5_T - 1) - t] for t in range(S5_T)]
            g = vjp((dxs, dys, dgs, da16_ref[0, dr]))
            glre_ref[dr, 0] = g[0]
            glim_ref[dr, 0] = g[1]
            glst_ref[dr, 0] = g[2]
            for q in range(4):
                gb[q] = g[3 + q] if gb[q] is None else gb[q] + g[3 + q]
        gbre_ref[0, 0] = gb[0]
        gbim_ref[0, 0] = gb[1]
        gcre_ref[0, 0] = gb[2]
        gcim_ref[0, 0] = gb[3]

    shp = lambda a: jax.ShapeDtypeStruct(a.shape, F32)
    return pl.pallas_call(
        body, name="s5_gen_bwd", grid=(S5_NB,),
        in_specs=_s5_param_specs() + [
            pl.BlockSpec((1, 2 * S5_T - 1, 128, 128), lambda b: (b, 0, 0, 0)),
            pl.BlockSpec((1, 2, S5_T, 128, 128), lambda b: (b, 0, 0, 0, 0)),
            pl.BlockSpec((1, 2, S5_T, 128, 128), lambda b: (b, 0, 0, 0, 0)),
            pl.BlockSpec((1, 2, 8, 128), lambda b: (b, 0, 0, 0))],
        out_specs=_s5_param_specs(),
        out_shape=[shp(lre), shp(lim), shp(lst), shp(b_re), shp(b_im), shp(c_re), shp(c_im), shp(dvec)],
        compiler_params=_params(("parallel",)),
    )(lre, lim, lst, b_re, b_im, c_re, c_im, dvec, dg, dx, dy, da16)


def _s5_put_groups(o_ref, dr, val):
    for gi in range(8):
        o_ref[dr, :, gi, :] = val[:, 128 * gi:128 * (gi + 1)]


def _s5_get_groups(s_ref, dr, n=8):
    return jnp.concatenate([s_ref[dr, :, gi, :] for gi in range(n)], axis=1).astype(BF16)


def _s5_to_states(u3, blocks, name):
    cn = u3.shape[1]

    def body(u_ref, b_ref, o_ref, w_scr):
        u = u_ref[0]
        for dr in range(2):
            _s5_fill_state_mat(w_scr, b_ref, dr)
            _s5_put_groups(o_ref, dr, _dot(u, w_scr[...]))

    return pl.pallas_call(
        body, name=name, grid=(S5_NB,),
        in_specs=[pl.BlockSpec((1, cn, S5_BW), lambda b: (b, 0, 0)), S5_GEN_SPECS[1]],
        out_specs=pl.BlockSpec((2, cn, 8, 128), lambda b: (0, 0, b, 0)),
        out_shape=jax.ShapeDtypeStruct((2, cn, S5_GROUPS, 128), F32),
        scratch_shapes=[pltpu.VMEM((S5_BW, S5_SW), BF16)],
        compiler_params=_params(("parallel",)),
    )(u3, blocks)


def _s5_from_states(u3, gg, st, blocks, transposed, name):
    cn = u3.shape[1]

    def body(u_ref, g_ref, s_ref, b_ref, o_ref, k_scr, w_scr):
        u = u_ref[0]
        _s5_fill_toeplitz(k_scr, g_ref)
        y = _dot_nt(u, k_scr[...]) if transposed else _dot(u, k_scr[...])
        for dr in range(2):
            _s5_fill_state_mat(w_scr, b_ref, dr)
            y = y + _dot_nt(_s5_get_groups(s_ref, dr), w_scr[...])
        for i in range(S5_T):
            o_ref[:, i, :] = y[:, 128 * i:128 * (i + 1)]

    return pl.pallas_call(
        body, name=name, grid=(S5_NB,),
        in_specs=[pl.BlockSpec((1, cn, S5_BW), lambda b: (b, 0, 0)), S5_GEN_SPECS[0],
                  pl.BlockSpec((2, cn, 8, 128), lambda b: (0, 0, b, 0)), S5_GEN_SPECS[1]],
        out_specs=pl.BlockSpec((cn, S5_T, 128), lambda b: (0, 0, b)),
        out_shape=jax.ShapeDtypeStruct((cn, S5_T, S5_WIDTH), F32),
        scratch_shapes=[pltpu.VMEM((S5_BW, S5_BW), BF16), pltpu.VMEM((S5_BW, S5_SW), BF16)],
        compiler_params=_params(("parallel",)),
    )(u3, gg, st, blocks)


def _s5_a_forms(a):
    ra = pltpu.roll(a, S5_STATE, 1)
    low = _iota2(a.shape, 1) < S5_STATE
    return jnp.where(low, a, ra), jnp.where(low, -ra, a)


def _s5_scan(sloc, a16, ncc, placed, kinds):
    cn = sloc.shape[1]
    n = len(placed)
    shard_shapes = _gather_shard_shapes(placed, kinds)

    def body(s_ref, a_ref, *rest):
        h_ref = rest[n]
        sends, arrivals = _gather_chip_copies(rest[n + 1:2 * n + 1], kinds, shard_shapes, *rest[2 * n + 1:])
        for cp in sends:
            cp.start()
        forms = [_s5_a_forms(a_ref[dr]) for dr in range(2)]

        def step(s, hs):
            out = []
            for dr in range(2):
                arr, aii = forms[dr]
                h, rh = hs[dr]
                c = s if dr == 0 else jnp.where(s < ncc, ncc - 1 - s, cn - 1 - (s - ncc))
                h_ref[dr, c] = h
                sc = s_ref[dr, c]
                out.append((h * arr + rh * aii + sc, rh * arr - h * aii + pltpu.roll(sc, S5_STATE, 1)))
            return tuple(out)

        zero = jnp.zeros((S5_GROUPS, 128), F32)
        lax.fori_loop(0, cn, step, ((zero, zero), (zero, zero)), unroll=4)
        for cp in arrivals:
            cp.wait_recv()
        for cp in sends:
            cp.wait_send()

    vmem = pl.BlockSpec(memory_space=pltpu.VMEM)
    return pl.pallas_call(
        body, name="s5_scan",
        in_specs=[vmem, vmem] + [ANY] * n, out_specs=[vmem] + [ANY] * n,
        out_shape=[jax.ShapeDtypeStruct(sloc.shape, F32)] + [jax.ShapeDtypeStruct(p.shape, p.dtype) for p in placed],
        input_output_aliases={2 + a: 1 + a for a in range(n)},
        scratch_shapes=[pltpu.SemaphoreType.DMA((n, 3)), pltpu.SemaphoreType.DMA((n, 3))],
        compiler_params=_params(),
    )(sloc, a16, *placed)


def _s5_scan_bwd(e, hs, a16, ncc):
    cn = e.shape[1]

    def body(e_ref, h_ref, a_ref, ds_ref, da_ref):
        forms = [_s5_a_forms(a_ref[dr]) for dr in range(2)]
        low = _iota2((S5_GROUPS, 128), 1) < S5_STATE

        def step(s, carry):
            out = []
            r = cn - 1 - s
            for dr in range(2):
                arr, aii = forms[dr]
                g, rg, da = carry[dr]
                c = r if dr == 0 else jnp.where(r < ncc, ncc - 1 - r, cn - 1 - (r - ncc))
                ds_ref[dr, c] = g
                h = h_ref[dr, c]
                rh = pltpu.roll(h, S5_STATE, 1)
                da = da + jnp.where(low, g * h + rg * rh, g * rh - rg * h)
                ec = e_ref[dr, c]
                out.append((ec + g * arr - rg * aii, pltpu.roll(ec, S5_STATE, 1) + rg * arr + g * aii, da))
            return tuple(out)

        zero = jnp.zeros((S5_GROUPS, 128), F32)
        res = lax.fori_loop(0, cn, step, ((zero, zero, zero), (zero, zero, zero)), unroll=4)
        da_ref[0] = res[0][2]
        da_ref[1] = res[1][2]

    return pl.pallas_call(
        body, name="s5_scan_bwd",
        out_shape=[jax.ShapeDtypeStruct(e.shape, F32), jax.ShapeDtypeStruct((2, S5_GROUPS, 128), F32)],
        compiler_params=_params(),
    )(e, hs, a16)


def _s5_bwd_kb(p3, dy3):
    cn = p3.shape[1]
    half = S5_T // 2

    def body(u_ref, d_ref, o_ref):
        q = pl.program_id(1)

        @pl.when(q == 0)
        def _():
            o_ref[...] = jnp.zeros_like(o_ref)

        dk = _dot_tn(u_ref[0], d_ref[0])
        for j in range(S5_T):
            for i in range(half):
                o_ref[0, half * q + i - j + (S5_T - 1)] += dk[128 * j:128 * (j + 1), 128 * i:128 * (i + 1)]

    return pl.pallas_call(
        body, name="s5_bwd_kb", grid=(S5_NB, 2),
        in_specs=[pl.BlockSpec((1, cn, S5_BW), lambda b, q: (b, 0, 0)),
                  pl.BlockSpec((1, cn, S5_BW // 2), lambda b, q: (b, 0, q))],
        out_specs=pl.BlockSpec((1, 2 * S5_T - 1, 128, 128), lambda b, q: (b, 0, 0, 0)),
        out_shape=jax.ShapeDtypeStruct((S5_NB, 2 * S5_T - 1, 128, 128), F32),
        compiler_params=_params(("parallel", "arbitrary")),
    )(p3, dy3)


def _s5_bwd_w(u3, st, name):
    cn = u3.shape[1]

    def body(u_ref, s_ref, w_ref):
        dw = _dot_tn(u_ref[0], _s5_get_groups(s_ref, 0))
        for j in range(S5_T):
            w_ref[0, 0, j] = _s5_contract(dw[128 * j:128 * (j + 1), :])

    return pl.pallas_call(
        body, name=name, grid=(S5_NB, 2),
        in_specs=[pl.BlockSpec((1, cn, S5_BW), lambda b, q: (b, 0, 0)),
                  pl.BlockSpec((1, cn, 8, 128), lambda b, q: (q, 0, b, 0))],
        out_specs=pl.BlockSpec((1, 1, S5_T, 128, 128), lambda b, q: (b, q, 0, 0, 0)),
        out_shape=jax.ShapeDtypeStruct((S5_NB, 2, S5_T, 128, 128), F32),
        compiler_params=_params(("parallel", "parallel")),
    )(u3, st)


K_SCALE = RET_DH ** -0.5
G_COL = 16


def _ret_chunk_of(step, ncc, nch, rev):
    if not rev:
        return step
    return jnp.where(step < ncc, ncc - 1 - step, nch - 1 - (step - ncc))


def _ret_decay(ld, rev):
    c = _iota2((RET_CHUNK, RET_CHUNK), 0).astype(F32)
    m = _iota2((RET_CHUNK, RET_CHUNK), 1).astype(F32)
    diff = (m - c) if rev else (c - m)
    keep = (diff > 0) if rev else (diff >= 0)
    expo = jnp.maximum(diff, 0.0)
    dm = jnp.where(keep, jnp.exp(ld * expo), 0.0)
    xi_e = (RET_CHUNK - c) if rev else (c + 1.0)
    zeta_e = c if rev else (RET_CHUNK - 1.0 - c)
    return dm, expo, jnp.exp(ld * xi_e), xi_e, jnp.exp(ld * zeta_e), zeta_e


RET_TABLES = 7


def _ret_tables(ld2):
    def body(ld_ref, t_ref):
        dr, h = pl.program_id(0), pl.program_id(1)
        ldh = ld_ref[dr, h]
        for rev in (False, True):
            @pl.when(dr == int(rev))
            def _(rev=rev):
                dm, expo, xi, xi_e, zeta, zeta_e = _ret_decay(ldh, rev)
                t_ref[0, 0, 0] = dm
                t_ref[0, 0, 1] = dm * expo
                t_ref[0, 0, 2] = xi
                t_ref[0, 0, 3] = xi * xi_e
                t_ref[0, 0, 4] = zeta
                t_ref[0, 0, 5] = zeta * zeta_e
                t_ref[0, 0, 6] = jnp.zeros_like(dm) + jnp.exp(ldh * RET_CHUNK)

    return pl.pallas_call(
        body, name="ret_tables", grid=(2, RET_HEADS),
        in_specs=[pl.BlockSpec(memory_space=pltpu.SMEM)],
        out_specs=pl.BlockSpec((1, 1, RET_TABLES, RET_CHUNK, RET_CHUNK), lambda d, h: (d, h, 0, 0, 0)),
        out_shape=jax.ShapeDtypeStruct((2, RET_HEADS, RET_TABLES, RET_CHUNK, RET_CHUNK), F32),
        compiler_params=_params(("parallel", "parallel")),
    )(ld2)


def _ret_specs(nch, ncc, rev, step_of):
    chunk = lambda n: _ret_chunk_of(step_of(n), ncc, nch, rev)
    cols = [pl.BlockSpec((RET_CHUNK, RET_WIDTH), functools.partial(lambda n, cb: (chunk(n), cb), cb=cb))
            for cb in (1, 2, 3)]
    tab = pl.BlockSpec((RET_CHUNK, RET_DH), lambda n: (chunk(n), 0))
    return cols + [tab, tab], pl.BlockSpec((RET_CHUNK, RET_WIDTH), lambda n: (chunk(n), 0))


def _ret_scan(p_all, cos_t, sin_t, tabs, ncc):
    la = p_all.shape[0]
    nch = la // RET_CHUNK

    def body(t_ref, qf, kf, vf, cf, sf, qb, kb, vb, cb, sb, of_ref, ob_ref, ssf_ref, ssb_ref, s_scr):
        @pl.when(pl.program_id(0) == 0)
        def _():
            s_scr[...] = jnp.zeros_like(s_scr)

        for dr, (q_ref, k_ref, v_ref, c_ref, n_ref, o_ref, ss_ref) in enumerate(
                ((qf, kf, vf, cf, sf, of_ref, ssf_ref), (qb, kb, vb, cb, sb, ob_ref, ssb_ref))):
            cs, sn = c_ref[...], n_ref[...]
            for h in range(RET_HEADS):
                sl = slice(RET_DH * h, RET_DH * (h + 1))
                dm, xi, zeta = t_ref[dr, h, 0], t_ref[dr, h, 2, :, 0:RET_DH], t_ref[dr, h, 4, :, 0:RET_DH]
                q = _rope(q_ref[:, sl], cs, sn)
                k = _rope(k_ref[:, sl] * K_SCALE, cs, sn)
                vh = v_ref[:, sl].astype(BF16)
                s = s_scr[dr, h]
                ss_ref[0, h] = s
                sc = (_dot_nt(q.astype(BF16), k.astype(BF16)) * dm).astype(BF16)
                o_ref[:, sl] = _dot(sc, vh) + _dot((q * xi).astype(BF16), s.astype(BF16))
                s_scr[dr, h] = t_ref[dr, h, 6, 0:RET_DH, 0:RET_DH] * s + _dot_tn((k * zeta).astype(BF16), vh)

    in_f, out_f = _ret_specs(nch, ncc, False, lambda n: n)
    in_b, out_b = _ret_specs(nch, ncc, True, lambda n: n)
    ss_spec = pl.BlockSpec((1, RET_HEADS, RET_DH, RET_DH), lambda n: (n, 0, 0, 0))
    o_shape = jax.ShapeDtypeStruct((la, RET_WIDTH), F32)
    ss_shape = jax.ShapeDtypeStruct((nch, RET_HEADS, RET_DH, RET_DH), F32)
    return pl.pallas_call(
        body, name="ret_scan", grid=(nch,),
        in_specs=[_full(tabs.shape)] + in_f + in_b,
        out_specs=[out_f, out_b, ss_spec, ss_spec],
        out_shape=[o_shape, o_shape, ss_shape, ss_shape],
        scratch_shapes=[pltpu.VMEM((2, RET_HEADS, RET_DH, RET_DH), F32)],
        compiler_params=_params(("arbitrary",)),
    )(tabs, p_all, p_all, p_all, cos_t, sin_t, p_all, p_all, p_all, cos_t, sin_t)


def _ret_scan_bwd(p_all, cos_t, sin_t, tabs, ssf, ssb, dy_all, ncc):
    la = p_all.shape[0]
    nch = la // RET_CHUNK

    def body(t_ref, qf, kf, vf, cf, sf, dof, ssf_ref, qb, kb, vb, cb, sb, dob_, ssb_ref,
             dqf, dkf, dvf, dqb, dkb, dvb, dld_ref, ds_scr):
        @pl.when(pl.program_id(0) == 0)
        def _():
            ds_scr[...] = jnp.zeros_like(ds_scr)
            dld_ref[...] = jnp.zeros_like(dld_ref)

        for dr, (q_ref, k_ref, v_ref, c_ref, n_ref, do_ref, ss_ref, dq_ref, dk_ref, dv_ref) in enumerate(
                ((qf, kf, vf, cf, sf, dof, ssf_ref, dqf, dkf, dvf), (qb, kb, vb, cb, sb, dob_, ssb_ref, dqb, dkb, dvb))):
            cs, sn = c_ref[...], n_ref[...]
            on_ctx = _ret_chunk_of(nch - 1 - pl.program_id(0), ncc, nch, dr == 1) < ncc
            for h in range(RET_HEADS):
                sl = slice(RET_DH * h, RET_DH * (h + 1))
                dm, dm_d = t_ref[dr, h, 0], t_ref[dr, h, 1]
                xi, xi_d, zeta, zeta_d = [t_ref[dr, h, t, :, 0:RET_DH] for t in (2, 3, 4, 5)]
                gc = t_ref[dr, h, 6, 0:RET_DH, 0:RET_DH]
                q = _rope(q_ref[:, sl], cs, sn)
                k = _rope(k_ref[:, sl] * K_SCALE, cs, sn)
                q16, k16, v16 = q.astype(BF16), k.astype(BF16), v_ref[:, sl].astype(BF16)
                s = ss_ref[0, h]
                s16 = s.astype(BF16)
                ds_in = ds_scr[dr, h]
                ds16 = ds_in.astype(BF16)
                do16 = jnp.where(on_ctx, 0.0, do_ref[:, sl]).astype(BF16)
                qk = _dot_nt(q16, k16)
                dsv = _dot_nt(do16, v16)
                dsc = (dsv * dm).astype(BF16)
                sc16 = (qk * dm).astype(BF16)
                dos = _dot_nt(do16, s16)
                vds = _dot_nt(v16, ds16)
                dq_ref[:, sl] = _dot(dsc, k16) + dos * xi
                dk_ref[:, sl] = _dot_tn(dsc, q16) + vds * zeta
                dv_ref[:, sl] = _dot_tn(sc16, do16) + _dot((k * zeta).astype(BF16), ds16)
                ds_scr[dr, h] = _dot_tn((q * xi).astype(BF16), do16) + gc * ds_in
                dld = (jnp.sum(dsv * qk * dm_d) + jnp.sum(q * dos * xi_d + k * vds * zeta_d)
                       + RET_CHUNK * jnp.sum(gc * s * ds_in))
                dld_ref[dr, h] += dld

    back = lambda n: nch - 1 - n
    in_f, out_f = _ret_specs(nch, ncc, False, back)
    in_b, out_b = _ret_specs(nch, ncc, True, back)
    ss_spec = pl.BlockSpec((1, RET_HEADS, RET_DH, RET_DH), lambda n: (nch - 1 - n, 0, 0, 0))
    shp = jax.ShapeDtypeStruct((la, RET_WIDTH), F32)
    dy_spec = lambda rev: pl.BlockSpec(
        (RET_CHUNK, RET_WIDTH), lambda n: (jnp.maximum(_ret_chunk_of(nch - 1 - n, ncc, nch, rev) - ncc, 0), 0))
    return pl.pallas_call(
        body, name="ret_scan_bwd", grid=(nch,),
        in_specs=[_full(tabs.shape)] + in_f + [dy_spec(False), ss_spec] + in_b + [dy_spec(True), ss_spec],
        out_specs=[out_f, out_f, out_f, out_b, out_b, out_b, _full((2, RET_HEADS, 8, 128))],
        out_shape=[shp] * 6 + [jax.ShapeDtypeStruct((2, RET_HEADS, 8, 128), F32)],
        scratch_shapes=[pltpu.VMEM((2, RET_HEADS, RET_DH, RET_DH), F32)],
        compiler_params=_params(("arbitrary",)),
    )(tabs, p_all, p_all, p_all, cos_t, sin_t, dy_all, ssf, p_all, p_all, p_all, cos_t, sin_t, dy_all, ssb)


def _in_bwd(dqf, dkf, dvf, dqb, dkb, dvb, du, dg, cos_t, sin_t, w_in_b, x, ctx, n1w, mod4, dx1):
    l, lc = x.shape[0], ctx.shape[0]
    la = l + lc
    tm = TOK_TILE
    nct = lc // tm

    def body(dqf_ref, dkf_ref, dvf_ref, dqb_ref, dkb_ref, dvb_ref, du_ref, dg_ref, cos_ref, sin_ref,
             w_ref, x_ref, c_ref, nw_ref, mod_ref, dx1_ref, dp_ref, gx_ref, acc_ref):
        i = pl.program_id(0)
        is_ctx = i < nct

        @pl.when(i == 0)
        def _():
            acc_ref[...] = jnp.zeros_like(acc_ref)

        cs, sn = cos_ref[...], sin_ref[...]
        def piece(k, val):
            cols = slice(S5_WIDTH * k, S5_WIDTH * (k + 1))
            dp_ref[:, cols] = val.astype(BF16)
            return _dot_nt(dp_ref[:, cols], w_ref[:, cols])

        dh1 = piece(0, du_ref[...])
        dh1 = dh1 + piece(3, dvf_ref[...] + dvb_ref[...])
        dh1 = dh1 + piece(4, jnp.where(is_ctx, 0.0, dg_ref[...]))
        for k, (f_ref, b_ref, scale) in ((1, (dqf_ref, dqb_ref, 1.0)), (2, (dkf_ref, dkb_ref, K_SCALE))):
            heads = [_rope_t(f_ref[:, RET_DH * h:RET_DH * (h + 1)] + b_ref[:, RET_DH * h:RET_DH * (h + 1)], cs, sn) * scale
                     for h in range(RET_HEADS)]
            dh1 = dh1 + piece(k, jnp.concatenate(heads, axis=1))
        xt = jnp.where(is_ctx, c_ref[...], x_ref[...])
        sh = jnp.where(is_ctx, mod_ref[0:1, :], mod_ref[2:3, :])
        sc = jnp.where(is_ctx, mod_ref[1:2, :], mod_ref[3:4, :])
        _, vjp = jax.vjp(_rms_mod, xt, nw_ref[...], sh, sc)
        dx, dnw, dsh, dsc = vjp(dh1)
        gx_ref[...] = dx + dx1_ref[...]
        cf = jnp.where(is_ctx, 1.0, 0.0)
        acc_ref[0:1, :] += dnw
        acc_ref[1:2, :] += cf * dsh
        acc_ref[2:3, :] += cf * dsc
        acc_ref[3:4, :] += (1.0 - cf) * dsh
        acc_ref[4:5, :] += (1.0 - cf) * dsc

    row = pl.BlockSpec((tm, RET_WIDTH), lambda i: (i, 0))
    tab = pl.BlockSpec((tm, RET_DH), lambda i: (i, 0))
    xrow = pl.BlockSpec((tm, D_MODEL), lambda i: (jnp.maximum(i - nct, 0), 0))
    return pl.pallas_call(
        body, name="in_bwd", grid=(la // tm,),
        in_specs=[row] * 7 + [pl.BlockSpec((tm, RET_WIDTH), lambda i: (jnp.maximum(i - nct, 0), 0)),
                              tab, tab, _full((D_MODEL, IN_COLS)), xrow,
                              pl.BlockSpec((tm, D_MODEL), lambda i: (jnp.minimum(i, nct - 1), 0)),
                              _full((1, D_MODEL)), _full((4, D_MODEL)), xrow],
        out_specs=[pl.BlockSpec((tm, IN_COLS), lambda i: (i, 0)), xrow, _full((8, D_MODEL))],
        out_shape=[jax.ShapeDtypeStruct((la, IN_COLS), BF16), jax.ShapeDtypeStruct((l, D_MODEL), F32),
                   jax.ShapeDtypeStruct((8, D_MODEL), F32)],
        compiler_params=_params(("arbitrary",)),
    )(dqf, dkf, dvf, dqb, dkb, dvb, du, dg, cos_t, sin_t, w_in_b, x, ctx, n1w, mod4, dx1)


def _outproj_up(x, y_all, of, ob, p_all, w_glu_b, b_glu, w_out_b, mod3, n2w, w_up_b, nct):
    l = x.shape[0]
    tm = TOK_TILE

    def body(x_ref, y_ref, of_ref, ob_ref, g_ref, wg_ref, bg_ref, wo_ref, mod_ref, nw_ref, wu_ref,
             x1_ref, mix_ref, h2_ref, up_ref, mb_ref, yr_ref):
        yg = _gelu(y_ref[...])
        mb_ref[:, 0:S5_WIDTH] = (yg * _sigmoid(_dot(yg.astype(BF16), wg_ref[...]) + bg_ref[...])).astype(BF16)
        yr = of_ref[...] + ob_ref[...]
        yr_ref[...] = yr
        for h in range(RET_HEADS):
            sl = slice(RET_DH * h, RET_DH * (h + 1))
            mb_ref[:, S5_WIDTH + RET_DH * h:S5_WIDTH + RET_DH * (h + 1)] = (
                _head_norm_gate(yr[:, sl], g_ref[:, sl]).astype(BF16))
        mix = _dot(mb_ref[...], wo_ref[...])
        mix_ref[...] = mix
        x1 = x_ref[...] + mod_ref[0:1, :] * mix
        x1_ref[...] = x1
        h2 = _rms_mod(x1, nw_ref[...], mod_ref[1:2, :], mod_ref[2:3, :]).astype(BF16)
        h2_ref[...] = h2
        up_ref[...] = _dot(h2, wu_ref[...])

    row = lambda w: pl.BlockSpec((tm, w), lambda i: (i, 0))
    arow = pl.BlockSpec((tm, RET_WIDTH), lambda i: (i + nct, 0))
    return pl.pallas_call(
        body, name="outproj_up", grid=(l // tm,),
        in_specs=[row(D_MODEL), arow, arow, arow, pl.BlockSpec((tm, RET_WIDTH), lambda i: (i + nct, G_COL // 4)),
                  _full((S5_WIDTH, S5_WIDTH)), _full((1, S5_WIDTH)), _full((D_MODEL, D_MODEL)), _full((3, D_MODEL)),
                  _full((1, D_MODEL)), _full((D_MODEL, 2 * D_FF))],
        out_specs=[row(D_MODEL), row(D_MODEL), row(D_MODEL), row(2 * D_FF), row(D_MODEL), row(RET_WIDTH)],
        out_shape=[jax.ShapeDtypeStruct((l, D_MODEL), F32), jax.ShapeDtypeStruct((l, D_MODEL), F32),
                   jax.ShapeDtypeStruct((l, D_MODEL), BF16), jax.ShapeDtypeStruct((l, 2 * D_FF), F32),
                   jax.ShapeDtypeStruct((l, D_MODEL), BF16), jax.ShapeDtypeStruct((l, RET_WIDTH), F32)],
        compiler_params=_params(("parallel",)),
    )(x, y_all, of, ob, p_all, w_glu_b, b_glu, w_out_b, mod3, n2w, w_up_b)


HALO = 8


def _conv_taps(g, prev_row, next_row):
    t = g.shape[0]
    r = _iota2(g.shape, 0)
    gprev = jnp.where(r == 0, prev_row, pltpu.roll(g, 1, 0))
    gnext = jnp.where(r == t - 1, next_row, pltpu.roll(g, t - 1, 0))
    return gprev, gnext


def _ffn_loss(up, x1, conv_w, conv_b, w_down_b, gate, fnw, tgt):
    l = x1.shape[0]
    tm = TOK_TILE
    nt = l // tm
    hb = tm // HALO

    cw = 256

    def body(up_a, up_g, hp_ref, hn_ref, x1_ref, cw_ref, cb_ref, wd_ref, gate_ref, fn_ref, tgt_ref,
             act_ref, dx2_ref, ddn_ref, dact_ref, acc_ref, ddn_scr):
        step = pl.program_id(0)
        i = jnp.minimum(step, nt - 1)

        @pl.when(step == 0)
        def _():
            acc_ref[...] = jnp.zeros_like(acc_ref)
            ddn_scr[...] = jnp.zeros_like(ddn_scr)

        ddn_prev = ddn_scr[...]
        dn = jnp.zeros((tm, D_MODEL), F32)
        for c in range(D_FF // cw):
            cols = slice(cw * c, cw * (c + 1))
            g = up_g[:, cols]
            prev_row = jnp.where(i == 0, 0.0, hp_ref[HALO - 1:HALO, cols])
            next_row = jnp.where(i == nt - 1, 0.0, hn_ref[0:1, cols])
            gprev, gnext = _conv_taps(g, prev_row, next_row)
            gc = cb_ref[:, cols] + gprev * cw_ref[0:1, cols] + g * cw_ref[1:2, cols] + gnext * cw_ref[2:3, cols]
            act = (_gelu(gc) * up_a[:, cols]).astype(BF16)
            act_ref[:, cols] = act
            dn = dn + _dot(act, wd_ref[cols, :])
            dact_ref[:, cols] = _dot_nt(ddn_prev, wd_ref[cols, :])
        x2 = x1_ref[...] + gate_ref[...] * dn
        y, vjp = jax.vjp(_rms, x2, fn_ref[...])
        err = y - tgt_ref[...]
        dx2, dfn = vjp(err * (1.0 / D_MODEL))
        dx2_ref[...] = dx2
        ddn = (dx2 * gate_ref[...]).astype(BF16)
        ddn_ref[...] = ddn
        ddn_scr[...] = ddn
        live = step < nt
        acc_ref[0:1, :] += jnp.where(live, dfn, 0.0)
        acc_ref[1:2, :] += jnp.where(live, jnp.sum(dx2 * dn, axis=0, keepdims=True), 0.0)
        acc_ref[2:3, :] += jnp.where(live, (0.5 / D_MODEL) * jnp.sum(err * err), 0.0)

    tile = lambda s: jnp.minimum(s, nt - 1)
    row = lambda w, cb=0: pl.BlockSpec((tm, w), lambda s: (tile(s), cb))
    last = l // HALO - 1
    return pl.pallas_call(
        body, name="ffn_loss", grid=(nt + 1,),
        in_specs=[row(D_FF, 0), row(D_FF, 1),
                  pl.BlockSpec((HALO, D_FF), lambda s: (jnp.maximum(tile(s) * hb - 1, 0), 1)),
                  pl.BlockSpec((HALO, D_FF), lambda s: (jnp.minimum((tile(s) + 1) * hb, last), 1)),
                  row(D_MODEL), _full((3, D_FF)), _full((1, D_FF)), _full((D_FF, D_MODEL)),
                  _full((1, D_MODEL)), _full((1, D_MODEL)), row(D_MODEL)],
        out_specs=[row(D_FF), row(D_MODEL), row(D_MODEL),
                   pl.BlockSpec((tm, D_FF), lambda s: (jnp.maximum(s - 1, 0), 0)), _full((8, D_MODEL))],
        out_shape=[jax.ShapeDtypeStruct((l, D_FF), BF16), jax.ShapeDtypeStruct((l, D_MODEL), F32),
                   jax.ShapeDtypeStruct((l, D_MODEL), BF16), jax.ShapeDtypeStruct((l, D_FF), F32),
                   jax.ShapeDtypeStruct((8, D_MODEL), F32)],
        scratch_shapes=[pltpu.VMEM((tm, D_MODEL), BF16)],
        compiler_params=_params(("arbitrary",)),
    )(up, up, up, up, x1, conv_w, conv_b, w_down_b, gate, fnw, tgt)


def _convglu_bwd(up, dact, conv_w, conv_b):
    l = up.shape[0]
    tm = 128
    nt = l // tm
    hb = tm // HALO
    te = tm + 2 * HALO

    def body(a_ref, ap_ref, an_ref, g_ref, gp_ref, gn_ref, d_ref, dp_ref, dn_ref, cw_ref, cb_ref,
             dup_ref, acc_ref):
        i = pl.program_id(0)

        @pl.when(i == 0)
        def _():
            acc_ref[...] = jnp.zeros_like(acc_ref)

        def ext(p, c, n):
            return jnp.concatenate([jnp.where(i == 0, 0.0, p[...]), c[...], jnp.where(i == nt - 1, 0.0, n[...])], axis=0)

        ae, ge, de = ext(ap_ref, a_ref, an_ref), ext(gp_ref, g_ref, gn_ref), ext(dp_ref, d_ref, dn_ref)
        gprev = pltpu.roll(ge, 1, 0)
        gnext = pltpu.roll(ge, te - 1, 0)
        w0, w1, w2 = cw_ref[0:1, :], cw_ref[1:2, :], cw_ref[2:3, :]
        gce = cb_ref[...] + gprev * w0 + ge * w1 + gnext * w2
        gel, dgel = _gelu_and_grad(gce)
        dae = de * gel
        dgce = de * ae * dgel
        dge = dgce * w1 + pltpu.roll(dgce, te - 1, 0) * w0 + pltpu.roll(dgce, 1, 0) * w2
        mid = slice(HALO, HALO + tm)
        dup_ref[:, 0:D_FF] = dae[mid].astype(BF16)
        dup_ref[:, D_FF:2 * D_FF] = dge[mid].astype(BF16)
        dgc = dgce[mid]
        acc_ref[0:1, :] += jnp.sum(dgc * gprev[mid], axis=0, keepdims=True)
        acc_ref[1:2, :] += jnp.sum(dgc * ge[mid], axis=0, keepdims=True)
        acc_ref[2:3, :] += jnp.sum(dgc * gnext[mid], axis=0, keepdims=True)
        acc_ref[3:4, :] += jnp.sum(dgc, axis=0, keepdims=True)

    last = l // HALO - 1

    def trio(cb):
        return [pl.BlockSpec((tm, D_FF), lambda i: (i, cb)),
                pl.BlockSpec((HALO, D_FF), lambda i: (jnp.maximum(i * hb - 1, 0), cb)),
                pl.BlockSpec((HALO, D_FF), lambda i: (jnp.minimum((i + 1) * hb, last), cb))]

    return pl.pallas_call(
        body, name="convglu_bwd", grid=(nt,),
        in_specs=trio(0) + trio(1) + trio(0) + [_full((3, D_FF)), _full((1, D_FF))],
        out_specs=[pl.BlockSpec((tm, 2 * D_FF), lambda i: (i, 0)), _full((8, D_FF))],
        out_shape=[jax.ShapeDtypeStruct((l, 2 * D_FF), BF16), jax.ShapeDtypeStruct((8, D_FF), F32)],
        compiler_params=_params(("arbitrary",)),
    )(up, up, up, up, up, up, dact, dact, dact, conv_w, conv_b)


def _up_bwd(dup, w_up_b, w_out_b, x1, dx2, mix, mod3, n2w, y_all, y_ret, p_all, w_glu_b, b_glu, zero_rows, nct, pairs,
            kinds):
    l = x1.shape[0]
    tm = TOK_TILE
    nt = l // tm
    n = len(pairs)
    shapes = _rs_slot_shapes(pairs, kinds)
    n_out = 8

    def body(dup_ref, wu_ref, wo_ref, x1_ref, dx2_ref, mix_ref, mod_ref, nw_ref, y_ref, yr_ref, g_ref, wg_ref, bg_ref,
             zero_rows_ref, *rest):
        dx1_ref, dmixb_ref, acc_ref, dys_ref, dyr_ref, dg_ref, gw_ref, gb_ref = rest[n:n + n_out]
        send_sems, recv_sems, dy_scr = rest[2 * n + n_out:]
        step = pl.program_id(0)

        @pl.when(step == 0)
        def _():
            acc_ref[...] = jnp.zeros_like(acc_ref)
            gw_ref[...] = jnp.zeros_like(gw_ref)
            gb_ref[...] = jnp.zeros_like(gb_ref)

        _behind(step, nt - 1, functools.partial(_rs_chip_copies, rest[:n], rest[n + n_out:2 * n + n_out], kinds,
                                                shapes, send_sems, recv_sems))

        dh2 = _dot_nt(dup_ref[...], wu_ref[...])
        _, vjp = jax.vjp(_rms_mod, x1_ref[...], nw_ref[...], mod_ref[1:2, :], mod_ref[2:3, :])
        dx, dnw, dsh, dsc = vjp(dh2)
        dx1 = dx + dx2_ref[...]
        dx1_ref[...] = dx1
        dmixb = (dx1 * mod_ref[0:1, :]).astype(BF16)
        dmixb_ref[...] = dmixb
        dmix = _dot_nt(dmixb, wo_ref[...])
        acc_ref[0:1, :] += dnw
        acc_ref[1:2, :] += jnp.sum(dx1 * mix_ref[...], axis=0, keepdims=True)
        acc_ref[2:3, :] += dsh
        acc_ref[3:4, :] += dsc

        yg, dgel = _gelu_and_grad(y_ref[...])
        ygb = yg.astype(BF16)
        sg = _sigmoid(_dot(ygb, wg_ref[...]) + bg_ref[...])
        ds = dmix[:, 0:S5_WIDTH]
        dz = ds * yg * sg * (1.0 - sg)
        dzb = dz.astype(BF16)
        _s5_put_rows(dys_ref, dy_scr, (ds * sg + _dot_nt(dzb, wg_ref[...])) * dgel)
        gw_ref[...] += _dot_tn(ygb, dzb)
        gb_ref[...] += jnp.sum(dz, axis=0, keepdims=True)

        for h in range(RET_HEADS):
            sl = slice(RET_DH * h, RET_DH * (h + 1))
            _, hvjp = jax.vjp(_head_norm_gate, yr_ref[:, sl], g_ref[:, sl])
            dyr, dg = hvjp(dmix[:, S5_WIDTH + RET_DH * h:S5_WIDTH + RET_DH * (h + 1)])
            dyr_ref[:, sl] = dyr
            dg_ref[:, sl] = dg

    row = pl.BlockSpec((tm, D_MODEL), lambda i: (i, 0))
    half = pl.BlockSpec((tm, S5_WIDTH), lambda i: (i, 0))
    f32h = jax.ShapeDtypeStruct((l, RET_WIDTH), F32)
    return pl.pallas_call(
        body, name="up_bwd", grid=(nt,),
        in_specs=[pl.BlockSpec((tm, 2 * D_FF), lambda i: (i, 0)), _full((D_MODEL, 2 * D_FF)),
                  _full((D_MODEL, D_MODEL)), row, row, row, _full((3, D_MODEL)), _full((1, D_MODEL)),
                  pl.BlockSpec((tm, S5_WIDTH), lambda i: (i + nct, 0)), half,
                  pl.BlockSpec((tm, RET_WIDTH), lambda i: (i + nct, G_COL // 4)),
                  _full((S5_WIDTH, S5_WIDTH)), _full((1, S5_WIDTH)), ANY] + [ANY] * n,
        out_specs=[row, row, _full((8, D_MODEL)),
                   pl.BlockSpec((S5_NB, tm // S5_T, S5_BW), lambda i: (0, i + nct, 0)), half, half,
                   _full((S5_WIDTH, S5_WIDTH)),
                   _full((1, S5_WIDTH))] + [ANY] * n,
        out_shape=[jax.ShapeDtypeStruct((l, D_MODEL), F32), jax.ShapeDtypeStruct((l, D_MODEL), BF16),
                   jax.ShapeDtypeStruct((8, D_MODEL), F32), jax.ShapeDtypeStruct(zero_rows.shape, BF16), f32h, f32h,
                   jax.ShapeDtypeStruct((S5_WIDTH, S5_WIDTH), F32), jax.ShapeDtypeStruct((1, S5_WIDTH), F32)]
        + [jax.ShapeDtypeStruct((4,) + s, p.dtype) for s, p in zip(shapes, pairs)],
        input_output_aliases={13: 3},
        scratch_shapes=[pltpu.SemaphoreType.DMA((n, 3)), pltpu.SemaphoreType.DMA((n, 3)),
                        pltpu.VMEM((tm // S5_T, S5_T, S5_WIDTH), F32)],
        compiler_params=_params(("arbitrary",)),
    )(dup, w_up_b, w_out_b, x1, dx2, mix, mod3, n2w, y_all, y_ret, p_all, w_glu_b, b_glu, zero_rows, *pairs)


MOD_ROWS = 16
MOD_COLS = 6 * D_MODEL // 4


def _mod_fwd(c_all, c_ctx, w_mod_b, b_loc):
    def body(c_ref, cc_ref, w_ref, b_ref, m_ref, s_ref):
        cond = jnp.concatenate([c_ref[...], jnp.broadcast_to(cc_ref[...], (8, D_MODEL))], axis=0)
        s = _silu(cond).astype(BF16)
        s_ref[...] = s
        m_ref[...] = _dot(s, w_ref[...]) + b_ref[...]

    return pl.pallas_call(
        body, name="mod_fwd",
        out_shape=[jax.ShapeDtypeStruct((MOD_ROWS, MOD_COLS), F32), jax.ShapeDtypeStruct((MOD_ROWS, D_MODEL), BF16)],
        compiler_params=_params(),
    )(c_all, c_ctx, w_mod_b, b_loc)


def _mod_bwd_sum(dm_all):
    def body(d_ref, dm_ref, gb_ref):
        rows = [d_ref[k, 0:1, :] for k in range(8)]
        ctx_sum = d_ref[0, 1:2, :]
        for k in range(1, 8):
            ctx_sum = ctx_sum + d_ref[k, 1:2, :]
        gb = ctx_sum
        for k in range(8):
            gb = gb + rows[k]
        gb_ref[...] = gb
        dm_ref[...] = jnp.concatenate(rows + [ctx_sum] + [jnp.zeros((7, 6 * D_MODEL), F32)], axis=0)

    return pl.pallas_call(
        body, name="mod_bwd_sum",
        out_shape=[jax.ShapeDtypeStruct((MOD_ROWS, 6 * D_MODEL), F32), jax.ShapeDtypeStruct((1, 6 * D_MODEL), F32)],
        compiler_params=_params(),
    )(dm_all)


def _mod_bwd_w(dm_loc, s_b, c_ctx, w_mod_b):
    def body(d_ref, s_ref, cc_ref, w_ref, gw_ref, gc_ref):
        db = d_ref[...].astype(BF16)
        gw_ref[...] = _dot_tn(s_ref[...], db)
        ds = _dot_nt(db, w_ref[...])
        _, vjp = jax.vjp(_silu, cc_ref[...])
        gc_ref[...] = jnp.broadcast_to(vjp(ds[8:9, :])[0], (8, D_MODEL))

    return pl.pallas_call(
        body, name="mod_bwd_w",
        out_shape=[jax.ShapeDtypeStruct((D_MODEL, MOD_COLS), F32), jax.ShapeDtypeStruct((8, D_MODEL), F32)],
        compiler_params=_params(),
    )(dm_loc, s_b, c_ctx, w_mod_b)


def _adamw(w, g, m, v, name):
    r, c = w.shape
    tr = _pick(r, (256, 128, 64, 32, 16, 8))
    bc1 = 1.0 - ADAM_B1 ** ADAM_STEP
    bc2 = 1.0 - ADAM_B2 ** ADAM_STEP

    def body(w_ref, g_ref, m_ref, v_ref, d_ref, nm_ref, nv_ref):
        gg = g_ref[...]
        nm = ADAM_B1 * m_ref[...] + (1.0 - ADAM_B1) * gg
        nv = ADAM_B2 * v_ref[...] + (1.0 - ADAM_B2) * (gg * gg)
        nm_ref[...] = nm
        nv_ref[...] = nv
        d_ref[...] = -ADAM_LR * ((nm / bc1) / (jnp.sqrt(nv / bc2) + ADAM_EPS) + ADAM_WD * w_ref[...])

    blk = pl.BlockSpec((tr, c), lambda i: (i, 0))
    shp = jax.ShapeDtypeStruct((r, c), F32)
    return pl.pallas_call(
        body, name=name, grid=(r // tr,), in_specs=[blk] * 4, out_specs=[blk] * 3, out_shape=[shp] * 3,
        compiler_params=_params(("parallel",)),
    )(w, g, m, v)


def _sum_slots(a, name):
    n, r, c = a.shape
    tr = _pick(r, (376, 256, 208, 128, 64, 32, 16, 8))

    def body(a_ref, o_ref):
        acc = a_ref[0].astype(F32)
        for k in range(1, n):
            acc = acc + a_ref[k].astype(F32)
        o_ref[...] = acc

    return pl.pallas_call(
        body, name=name, grid=(r // tr,),
        in_specs=[pl.BlockSpec((n, tr, c), lambda i: (0, i, 0))],
        out_specs=pl.BlockSpec((tr, c), lambda i: (i, 0)),
        out_shape=jax.ShapeDtypeStruct((r, c), F32),
        compiler_params=_params(("parallel",)),
    )(a)


def _mesh_pos():
    return lax.axis_index("x"), lax.axis_index("y"), lax.axis_index("c")


def _all_gather8(v, name):
    m_per, n = v.shape

    def body(x_ref, out_ref, send_sems, recv_sems, local_sem):
        x, y, c = _mesh_pos()
        me, sibling = (x, y, c), (x, y, 1 - c)
        chips = [(1 - x, y), (x, 1 - y), (1 - x, 1 - y)]

        def rows(px, py, pc):
            return out_ref.at[pl.ds((4 * px + 2 * py + pc) * m_per, m_per), :]

        def copy(k, block, to, src=None):
            return pltpu.make_async_remote_copy(
                src_ref=rows(*block) if src is None else src, dst_ref=rows(*block),
                send_sem=send_sems.at[k], recv_sem=recv_sems.at[k], device_id=to, device_id_type=MESH_ID)

        mine = pltpu.make_async_copy(x_ref, rows(*me), local_sem)
        mine.start()
        first = [copy(0, me, sibling, src=x_ref)]
        first += [copy(1 + j, me, (*chip, c), src=x_ref) for j, chip in enumerate(chips)]
        for cp in first:
            cp.start()
        passed = [copy(4 + j, (*chip, c), sibling) for j, chip in enumerate(chips)]
        for j, chip in enumerate(chips):
            copy(1 + j, (*chip, c), me).wait_recv()
            passed[j].start()
        copy(0, sibling, me).wait_recv()
        for j, chip in enumerate(chips):
            copy(4 + j, (*chip, 1 - c), me).wait_recv()
        for cp in first + passed:
            cp.wait_send()
        mine.wait()

    return pl.pallas_call(
        body, name=name,
        out_shape=jax.ShapeDtypeStruct((8 * m_per, n), v.dtype),
        in_specs=[pl.BlockSpec(memory_space=pltpu.VMEM)],
        out_specs=pl.BlockSpec(memory_space=pltpu.VMEM),
        scratch_shapes=[pltpu.SemaphoreType.DMA((7,)), pltpu.SemaphoreType.DMA((7,)), pltpu.SemaphoreType.DMA],
        compiler_params=_params(),
    )(v)


ANY = pl.BlockSpec(memory_space=pl.ANY)
def PEER_CHIPS(x, y):
    return [(x, 1 - y), (1 - x, y), (1 - x, 1 - y)]


def _shard_region(ref, kind, k, rl, cl, r0, nr, c0, nc):
    if kind == "col":
        return ref.at[pl.ds(r0, nr), pl.ds(k * cl + c0, nc)]
    return ref.at[pl.ds(k * rl + r0, nr), pl.ds(c0, nc)]


def _place_shard(w, kind, chip, name):
    rl, cl = w.shape
    tr = _pick(rl, (256, 128, 64))
    nt = rl // tr

    def body(chip_ref, w_ref, o_ref):
        o_ref[...] = w_ref[...].astype(BF16)

    o_map = (lambda i, chip_ref: (i, chip_ref[0])) if kind == "col" else (lambda i, chip_ref: (chip_ref[0] * nt + i, 0))
    return pl.pallas_call(
        body, name=name,
        grid_spec=pltpu.PrefetchScalarGridSpec(
            num_scalar_prefetch=1, grid=(nt,),
            in_specs=[pl.BlockSpec((tr, cl), lambda i, chip_ref: (i, 0))], out_specs=pl.BlockSpec((tr, cl), o_map)),
        out_shape=jax.ShapeDtypeStruct((rl, 4 * cl) if kind == "col" else (4 * rl, cl), BF16),
        compiler_params=_params(("parallel",)),
    )(chip.reshape(1), w)


def _gather_shard_shapes(placed, kinds):
    return [(p.shape[0], p.shape[1] // 4) if k == "col" else (p.shape[0] // 4, p.shape[1]) for p, k in zip(placed, kinds)]


def _gather_chip_copies(outs, kinds, shard_shapes, send_sems, recv_sems, with_arrivals=True):
    x, y, c = _mesh_pos()
    me = 2 * x + y
    sends, arrivals = [], []
    for a in range(len(outs)):
        rl, cl = shard_shapes[a]
        rh = rl // 2
        reg = functools.partial(_shard_region, outs[a], kinds[a], rl=rl, cl=cl, r0=c * rh, nr=rh, c0=0, nc=cl)
        for j, (px, py) in enumerate(PEER_CHIPS(x, y)):
            to = dict(send_sem=send_sems.at[a, j], recv_sem=recv_sems.at[a, j], device_id=(px, py, c),
                      device_id_type=MESH_ID)
            sends.append(pltpu.make_async_remote_copy(src_ref=reg(k=me), dst_ref=reg(k=me), **to))
            if with_arrivals:
                got = reg(k=2 * px + py)
                arrivals.append(pltpu.make_async_remote_copy(src_ref=got, dst_ref=got, **to))
    return sends, arrivals


def _gather_sibling_copies(outs, kinds, shard_shapes, send_sems, recv_sems):
    x, y, c = _mesh_pos()
    forwards, arrivals = [], []
    for a in range(len(outs)):
        rl, cl = shard_shapes[a]
        rh = rl // 2
        for j, (px, py) in enumerate(PEER_CHIPS(x, y)):
            to = dict(send_sem=send_sems.at[a, j], recv_sem=recv_sems.at[a, j], device_id=(x, y, 1 - c),
                      device_id_type=MESH_ID)
            reg = functools.partial(_shard_region, outs[a], kinds[a], k=2 * px + py, rl=rl, cl=cl, nr=rh, c0=0, nc=cl)
            forwards.append(pltpu.make_async_remote_copy(src_ref=reg(r0=c * rh), dst_ref=reg(r0=c * rh), **to))
            arrivals.append(pltpu.make_async_remote_copy(src_ref=reg(r0=(1 - c) * rh), dst_ref=reg(r0=(1 - c) * rh), **to))
    return forwards, arrivals


def _gather_weights(placed, kinds):
    n = len(placed)
    shard_shapes = _gather_shard_shapes(placed, kinds)

    def body(*refs):
        outs = refs[n:2 * n]
        ici_send, ici_recv, sib_send, sib_recv = refs[2 * n:]
        sends, arrivals = _gather_chip_copies(outs, kinds, shard_shapes, ici_send, ici_recv)
        for cp in sends:
            cp.start()
        forwards, from_sibling = _gather_sibling_copies(outs, kinds, shard_shapes, sib_send, sib_recv)
        for cp, fwd in zip(arrivals, forwards):
            cp.wait_recv()
            fwd.start()
        for cp in from_sibling:
            cp.wait_recv()
        for cp in sends + forwards:
            cp.wait_send()

    return pl.pallas_call(
        body, name="gather_weights",
        out_shape=[jax.ShapeDtypeStruct(p.shape, p.dtype) for p in placed],
        in_specs=[ANY] * n, out_specs=[ANY] * n, input_output_aliases={a: a for a in range(n)},
        scratch_shapes=[pltpu.SemaphoreType.DMA((n, 3))] * 4,
        compiler_params=_params(),
    )(*placed)


def _gather_sibling(placed, kinds):
    n = len(placed)
    shard_shapes = _gather_shard_shapes(placed, kinds)

    def body(*refs):
        forwards, from_sibling = _gather_sibling_copies(refs[n:2 * n], kinds, shard_shapes, *refs[2 * n:])
        for cp in forwards:
            cp.start()
        for cp in from_sibling:
            cp.wait_recv()
        for cp in forwards:
            cp.wait_send()

    return pl.pallas_call(
        body, name="gather_sibling",
        out_shape=[jax.ShapeDtypeStruct(p.shape, p.dtype) for p in placed],
        in_specs=[ANY] * n, out_specs=[ANY] * n, input_output_aliases={a: a for a in range(n)},
        scratch_shapes=[pltpu.SemaphoreType.DMA((n, 3))] * 2,
        compiler_params=_params(),
    )(*placed)


def _half(kind, r, c):
    return (r // 2, c) if kind == "col" else (r, c // 2)


def _half_of(ref, kind, which):
    r, c = ref.shape
    hr, hc = _half(kind, r, c)
    return ref.at[pl.ds(which * hr, hr), :] if kind == "col" else ref.at[:, pl.ds(which * hc, hc)]


def _rs_sibling(grads, kinds, name):
    n = len(grads)

    def body(*refs):
        srcs, dsts = refs[:n], refs[n:2 * n]
        send_sems, recv_sems = refs[2 * n:]
        x, y, c = _mesh_pos()
        cps = [pltpu.make_async_remote_copy(src_ref=_half_of(srcs[a], kinds[a], 1 - c), dst_ref=dsts[a],
                                            send_sem=send_sems.at[a], recv_sem=recv_sems.at[a],
                                            device_id=(x, y, 1 - c), device_id_type=MESH_ID) for a in range(n)]
        for cp in cps:
            cp.start()
        for cp in cps:
            cp.wait()

    return pl.pallas_call(
        body, name=name,
        out_shape=[jax.ShapeDtypeStruct(_half(k, *g.shape), g.dtype) for g, k in zip(grads, kinds)],
        in_specs=[ANY] * n, out_specs=[ANY] * n,
        scratch_shapes=[pltpu.SemaphoreType.DMA((n,)), pltpu.SemaphoreType.DMA((n,))],
        compiler_params=_params(),
    )(*grads)


def _pair_sum(gf, rv, kind, ci, name):
    r, c = rv.shape
    tr = _pick(r, (128, 64, 32, 16, 8))
    nt = r // tr

    def body(ci_ref, g_ref, r_ref, o_ref):
        o_ref[...] = (g_ref[...] + r_ref[...]).astype(BF16)

    g_map = (lambda i, ci_ref: (ci_ref[0] * nt + i, 0)) if kind == "col" else (lambda i, ci_ref: (i, ci_ref[0]))
    blk = pl.BlockSpec((tr, c), lambda i, ci_ref: (i, 0))
    return pl.pallas_call(
        body, name=name,
        grid_spec=pltpu.PrefetchScalarGridSpec(num_scalar_prefetch=1, grid=(nt,),
                                               in_specs=[pl.BlockSpec((tr, c), g_map), blk], out_specs=blk),
        out_shape=jax.ShapeDtypeStruct((r, c), BF16),
        compiler_params=_params(("parallel",)),
    )(ci.reshape(1), gf, rv)


def _rs_slot_shapes(pairs, kinds):
    return [(p.shape[0], p.shape[1] // 4) if k == "col" else (p.shape[0] // 4, p.shape[1]) for p, k in zip(pairs, kinds)]


def _rs_chip_copies(srcs, dsts, kinds, shapes, send_sems, recv_sems, with_arrivals=True):
    x, y, c = _mesh_pos()
    me = 2 * x + y
    sends, arrivals = [], []
    for a in range(len(srcs)):
        rl, cl = shapes[a]
        reg = functools.partial(_shard_region, srcs[a], kinds[a], rl=rl, cl=cl, r0=0, nr=rl, c0=0, nc=cl)
        for j, (px, py) in enumerate(PEER_CHIPS(x, y)):
            to = dict(send_sem=send_sems.at[a, j], recv_sem=recv_sems.at[a, j], device_id=(px, py, c),
                      device_id_type=MESH_ID)
            sends.append(pltpu.make_async_remote_copy(src_ref=reg(k=2 * px + py), dst_ref=dsts[a].at[me], **to))
            if with_arrivals:
                slot = dsts[a].at[2 * px + py]
                arrivals.append(pltpu.make_async_remote_copy(src_ref=slot, dst_ref=slot, **to))
    return sends, arrivals


def _rs_chips(pairs, kinds):
    n = len(pairs)
    shapes = _rs_slot_shapes(pairs, kinds)

    def body(*refs):
        sends, arrivals = _rs_chip_copies(refs[:n], refs[n:2 * n], kinds, shapes, *refs[2 * n:])
        for cp in sends:
            cp.start()
        for cp in arrivals:
            cp.wait_recv()
        for cp in sends:
            cp.wait_send()

    return pl.pallas_call(
        body, name="rs_chips",
        out_shape=[jax.ShapeDtypeStruct((4,) + s, p.dtype) for s, p in zip(shapes, pairs)],
        in_specs=[ANY] * n, out_specs=[ANY] * n,
        scratch_shapes=[pltpu.SemaphoreType.DMA((n, 3)), pltpu.SemaphoreType.DMA((n, 3))],
        compiler_params=_params(),
    )(*pairs)


def _sum_chips(pair, got, kind, pos, name):
    _, r, c = got.shape
    tr = _pick(r, (256, 128, 64, 32, 16))
    nt = r // tr

    def body(pos_ref, own_ref, g1_ref, g2_ref, g3_ref, o_ref):
        o_ref[...] = ((own_ref[...].astype(F32) + g1_ref[0].astype(F32)) + g2_ref[0].astype(F32)) + g3_ref[0].astype(F32)

    if kind == "col":
        own_map = lambda i, p: (i, p[1])
        out_map = lambda i, p: (p[0] * nt + i, 0)
        out_shape = (2 * r, c)
    else:
        own_map = lambda i, p: (p[1] * nt + i, 0)
        out_map = lambda i, p: (i, p[0])
        out_shape = (r, 2 * c)
    peer = lambda m: pl.BlockSpec((1, tr, c), lambda i, p: (p[1] ^ m, i, 0))
    return pl.pallas_call(
        body, name=name,
        grid_spec=pltpu.PrefetchScalarGridSpec(
            num_scalar_prefetch=1, grid=(nt,),
            in_specs=[pl.BlockSpec((tr, c), own_map), peer(1), peer(2), peer(3)],
            out_specs=pl.BlockSpec((tr, c), out_map)),
        out_shape=jax.ShapeDtypeStruct(out_shape, F32),
        compiler_params=_params(("parallel",)),
    )(pos, pair, got, got, got)


def _rs_back(halves, kinds):
    n = len(halves)

    def body(*refs):
        outs = refs[n:2 * n]
        send_sems, recv_sems = refs[2 * n:]
        x, y, c = _mesh_pos()
        cps = []
        for a in range(n):
            mine = _half_of(outs[a], kinds[a], c)
            cps.append(pltpu.make_async_remote_copy(src_ref=mine, dst_ref=mine, send_sem=send_sems.at[a],
                                                    recv_sem=recv_sems.at[a], device_id=(x, y, 1 - c),
                                                    device_id_type=MESH_ID))
            cps[-1].start()
        for a in range(n):
            other = _half_of(outs[a], kinds[a], 1 - c)
            pltpu.make_async_remote_copy(src_ref=other, dst_ref=other, send_sem=send_sems.at[a],
                                         recv_sem=recv_sems.at[a], device_id=(x, y, 1 - c),
                                         device_id_type=MESH_ID).wait_recv()
        for cp in cps:
            cp.wait_send()

    return pl.pallas_call(
        body, name="rs_back",
        out_shape=[jax.ShapeDtypeStruct(h.shape, h.dtype) for h in halves],
        in_specs=[ANY] * n, out_specs=[ANY] * n, input_output_aliases={a: a for a in range(n)},
        scratch_shapes=[pltpu.SemaphoreType.DMA((n,)), pltpu.SemaphoreType.DMA((n,))],
        compiler_params=_params(),
    )(*halves)


def _rope_tables(l, lc):
    rows = l // GRID_W
    n_freq = RET_DH // 4
    inv_freq = ROPE_THETA ** (-jnp.arange(n_freq, dtype=F32) / n_freq)
    sign = jnp.tile(jnp.array([-1.0, 1.0], F32), n_freq)

    def half(n):
        ang = jnp.repeat(jnp.arange(n, dtype=F32)[:, None] * inv_freq, 2, axis=-1)
        return jnp.cos(ang), jnp.sin(ang) * sign

    (cr, sr), (cc, sc) = half(rows), half(GRID_W)
    grid = lambda r, c: jnp.concatenate([jnp.repeat(r, GRID_W, axis=0), jnp.tile(c, (rows, 1))], axis=-1)
    cos_t = jnp.concatenate([jnp.ones((lc, RET_DH), F32), grid(cr, cc)], axis=0)
    sin_t = jnp.concatenate([jnp.zeros((lc, RET_DH), F32), grid(sr, sc)], axis=0)
    return cos_t, sin_t


def _s5_pack(a):
    blk = lambda t: t.reshape(1, S5_NB, 128, S5_STATE)
    lre = jnp.stack([a["s5_lambda_re_f"][0], a["s5_lambda_re_b"][0]]).reshape(2, S5_NB, 8, S5_STATE)
    lim = jnp.stack([a["s5_lambda_im_f"][0], a["s5_lambda_im_b"][0]]).reshape(2, S5_NB, 8, S5_STATE)
    lst = jnp.stack([a["s5_log_step_f"][0], a["s5_log_step_b"][0]]).reshape(2, S5_NB, 8, 1)
    b_re = blk(a["s5_b_re"][0].transpose(0, 2, 1))
    b_im = blk(a["s5_b_im"][0].transpose(0, 2, 1))
    return (lre, lim, lst, b_re, b_im, blk(a["s5_c_re"][0]), blk(a["s5_c_im"][0]),
            a["s5_d"].reshape(1, S5_NB, 1, 128))


def _s5_unpack(g):
    glre, glim, glst, gbre, gbim, gcre, gcim, gd = g
    unb = lambda t: t.reshape(S5_GROUPS, S5_GROUP, S5_STATE).transpose(0, 2, 1)[None]
    return {
        "s5_lambda_re_f": glre[0].reshape(1, S5_GROUPS, S5_STATE), "s5_lambda_re_b": glre[1].reshape(1, S5_GROUPS, S5_STATE),
        "s5_lambda_im_f": glim[0].reshape(1, S5_GROUPS, S5_STATE), "s5_lambda_im_b": glim[1].reshape(1, S5_GROUPS, S5_STATE),
        "s5_log_step_f": glst[0].reshape(1, S5_GROUPS), "s5_log_step_b": glst[1].reshape(1, S5_GROUPS),
        "s5_b_re": unb(gbre), "s5_b_im": unb(gbim),
        "s5_c_re": gcre.reshape(1, S5_GROUPS, S5_GROUP, S5_STATE), "s5_c_im": gcim.reshape(1, S5_GROUPS, S5_GROUP, S5_STATE),
        "s5_d": gd.reshape(1, S5_WIDTH),
    }


def _local_step(a, wb, late, mx, mc, conv_w, ci):
    x, ctx, tgt = a["x"][0], a["ctx"][0], a["loss_target"][0]
    l, lc = x.shape[0], ctx.shape[0]
    la = l + lc
    nct, ncc, nrc, cn = lc // TOK_TILE, lc // S5_T, lc // RET_CHUNK, la // S5_T
    n1w, n2w, fnw = a["norm1_w"], a["norm2_w"], a["final_norm_w"].reshape(1, D_MODEL)
    conv_b, b_glu = a["conv_b"], a["s5_b_glu"]
    ld2 = jnp.concatenate([a["ret_log_decay_f"], a["ret_log_decay_b"]], axis=0)
    mod4 = jnp.concatenate([mc[0:2], mx[0:2]], axis=0)
    mod3 = mx[2:5]
    gate5 = mx[5:6]
    cos_t, sin_t = _rope_tables(l, lc)
    s5p = _s5_pack(a)

    p_all, h1b, p3, w_up_p = _norm_inproj(x, ctx, n1w, mod4, wb["w_in"], [late[1]], (LATE_KINDS[1],))
    gg, xw, yw, a16 = _s5_gen(*s5p)
    sloc = _s5_to_states(p3, xw, "s5_state")
    a16s = a16.transpose(1, 0, 2, 3).reshape(2, S5_GROUPS, 128)
    hs, w_out_p, w_down_p = _s5_scan(sloc, a16s, ncc, [late[0], late[2]], (LATE_KINDS[0], LATE_KINDS[2]))
    y_all = _s5_from_states(p3, gg, hs, yw, False, "s5_out").reshape(la, S5_WIDTH)
    tabs = _ret_tables(ld2)
    of, ob, ssf, ssb = _ret_scan(p_all, cos_t, sin_t, tabs, nrc)
    wb = {**wb, **dict(zip(LATE_NAMES, _gather_sibling([w_out_p, w_up_p, w_down_p], LATE_KINDS)))}
    x1, mix, h2b, up, mixb, y_ret = _outproj_up(x, y_all, of, ob, p_all, wb["s5_w_glu"], b_glu, wb["w_out"],
                                                     mod3, n2w, wb["w_up"], nct)
    act, dx2, ddn, dact, acc_f = _ffn_loss(up, x1, conv_w, conv_b, wb["w_down"], gate5, fnw, tgt)

    g = {}
    g["w_down"] = _mm_tn(act, ddn, name="gw_down")
    dup, acc_c = _convglu_bwd(up, dact, conv_w, conv_b)
    g["w_up"] = _mm_tn(h2b, dup, name="gw_up")
    first = [g[n] for n in FIRST_GRADS]
    first_pairs = [_pair_sum(gf, rv, k, ci, "rs_pair_" + n)
                   for gf, rv, k, n in zip(first, _rs_sibling(first, FIRST_KINDS, "rs_sibling_first"), FIRST_KINDS, FIRST_GRADS)]
    dx1, dmixb, acc_2, dy3, dy_ret, dg, g["s5_w_glu"], g["s5_b_glu"], *first_got = _up_bwd(
        dup, wb["w_up"], wb["w_out"], x1, dx2, mix, mod3, n2w, y_all, y_ret, p_all, wb["s5_w_glu"], b_glu,
        jnp.zeros(p3.shape, BF16), nct, first_pairs, FIRST_KINDS)
    g["w_out"] = _mm_tn(mixb, dmixb, name="gw_out")

    e = _s5_to_states(dy3, yw, "s5_bwd_h")
    ds, da16 = _s5_scan_bwd(e, hs, a16s, ncc)
    du = _s5_from_states(dy3, gg, ds, xw, True, "s5_bwd_u").reshape(la, S5_WIDTH)
    dkb = _s5_bwd_kb(p3, dy3)
    dwst = _s5_bwd_w(p3, ds, "s5_bwd_wst")
    dwout = _s5_bwd_w(dy3, hs, "s5_bwd_wout")
    da16p = da16.reshape(2, S5_NB, 8, 128).transpose(1, 0, 2, 3)
    g.update(_s5_unpack(_s5_gen_bwd(*s5p, dkb, dwst, dwout, da16p)))

    dqf, dkf, dvf, dqb, dkb_, dvb, dld = _ret_scan_bwd(p_all, cos_t, sin_t, tabs, ssf, ssb, dy_ret, nrc)
    g["ret_log_decay_f"] = dld[0, :, 0, 0].reshape(1, RET_HEADS)
    g["ret_log_decay_b"] = dld[1, :, 0, 0].reshape(1, RET_HEADS)
    dp, grad_x, acc_1 = _in_bwd(dqf, dkf, dvf, dqb, dkb_, dvb, du, dg, cos_t, sin_t, wb["w_in"], x, ctx, n1w, mod4, dx1)
    g["w_in"] = _mm_tn(h1b, dp, name="gw_in")

    g["norm1_w"], g["norm2_w"], g["final_norm_w"] = acc_1[0:1], acc_2[0:1], acc_f[0]
    g["conv_w"], g["conv_b"] = acc_c[0:3], acc_c[3:4]
    zero = jnp.zeros((1, D_MODEL), F32)
    dmx = jnp.concatenate([acc_1[3:5], acc_2[1:2], acc_2[2:4], acc_f[1:2]], axis=0)
    dmc = jnp.concatenate([acc_1[1:3], zero, zero, zero, zero], axis=0)
    return acc_f[2, 0], grad_x, g, dmx, dmc, first_pairs, first_got


WEIGHT_NAMES = ("c_ctx", "w_mod", "b_mod", "norm1_w", "w_in", "s5_lambda_re_f", "s5_lambda_im_f", "s5_log_step_f",
                "s5_lambda_re_b", "s5_lambda_im_b", "s5_log_step_b", "s5_b_re", "s5_b_im", "s5_c_re", "s5_c_im",
                "s5_d", "s5_w_glu", "s5_b_glu", "ret_log_decay_f", "ret_log_decay_b", "w_out", "norm2_w", "w_up",
                "conv_w", "conv_b", "w_down", "final_norm_w")
BIG_NAMES = ("w_in", "w_out", "w_up", "w_down", "s5_w_glu")
BIG_KINDS = ("col", "row", "col", "row", "row")
EARLY_NAMES, EARLY_KINDS = ("w_in", "s5_w_glu"), ("col", "row")
LATE_NAMES, LATE_KINDS = ("w_out", "w_up", "w_down"), ("row", "col", "row")
FIRST_GRADS, FIRST_KINDS = ("w_down", "w_up"), ("row", "col")
LAST_GRADS, LAST_KINDS = ("w_in", "w_out", "s5_w_glu"), ("col", "row", "row")
SMALL_NAMES = ("norm1_w", "norm2_w", "final_norm_w", "conv_b", "conv_w", "s5_lambda_re_f", "s5_lambda_im_f",
               "s5_log_step_f", "s5_lambda_re_b", "s5_lambda_im_b", "s5_log_step_b", "s5_b_re", "s5_b_im", "s5_c_re",
               "s5_c_im", "s5_d", "s5_b_glu", "ret_log_decay_f", "ret_log_decay_b")
ROW = 1024
N_CHIPS = 4


def _pack_rows(parts):
    flat = jnp.concatenate([p.reshape(-1) for p in parts])
    n = flat.shape[0]
    rows = -(-n // (8 * ROW)) * 8
    return jnp.pad(flat, (0, rows * ROW - n)).reshape(rows, ROW)


def _unpack_rows(packed, shapes):
    flat = packed.reshape(-1)
    out, off = [], 0
    for s in shapes:
        n = math.prod(s)
        out.append(flat[off:off + n].reshape(s))
        off += n
    return out


def _step(a):
    xi, yi, ci = _mesh_pos()
    chip = 2 * xi + yi
    dev = 2 * chip + ci

    cw_loc = a["conv_w"].reshape(-1)
    small_in = jnp.concatenate([a["c"].reshape(-1), jnp.pad(cw_loc, (0, 24 * 128 - cw_loc.shape[0]))]).reshape(32, 128)
    sg = _all_gather8(small_in, "gather_cond").reshape(8, 32, 128)
    c_all = sg[:, 0:8].reshape(8, D_MODEL)
    conv_w = sg[0::2, 8:32].reshape(N_CHIPS, -1)[:, :cw_loc.shape[0]].reshape(N_CHIPS, 3, -1)
    conv_w = conv_w.transpose(1, 0, 2).reshape(3, D_FF)

    placed = {n: _place_shard(a[n][0], k, chip, "place_" + n) for n, k in zip(BIG_NAMES, BIG_KINDS)}
    wb = dict(zip(EARLY_NAMES, _gather_weights([placed[n] for n in EARLY_NAMES], EARLY_KINDS)))
    late = [placed[n] for n in LATE_NAMES]

    w_mod_b = a["w_mod"][0].astype(BF16)
    c_ctx = a["c_ctx"].reshape(1, D_MODEL)
    b_loc = lax.dynamic_slice_in_dim(a["b_mod"], chip * MOD_COLS, MOD_COLS, 1)
    m_loc, s_b = _mod_fwd(c_all, c_ctx, w_mod_b, b_loc)
    mg = _all_gather8(m_loc, "gather_mod").reshape(8, MOD_ROWS, MOD_COLS)
    m_full = mg[0::2].transpose(1, 0, 2).reshape(MOD_ROWS, 6 * D_MODEL)
    mx = lax.dynamic_slice_in_dim(m_full, dev, 1, 0).reshape(6, D_MODEL)
    mc = m_full[8].reshape(6, D_MODEL)

    loss_part, grad_x, g, dmx, dmc, first_pairs, first_got = _local_step(a, wb, late, mx, mc, conv_w, ci)
    loss = lax.psum(loss_part, ("x", "y", "c"))

    dm_pair = jnp.concatenate([dmx.reshape(1, -1), dmc.reshape(1, -1), jnp.zeros((6, 6 * D_MODEL), F32)], axis=0)
    dm_all = _all_gather8(dm_pair, "gather_dmod").reshape(8, 8, 6 * D_MODEL)
    dm16, gb_mod = _mod_bwd_sum(dm_all)
    dm_loc = lax.dynamic_slice_in_dim(dm16, chip * MOD_COLS, MOD_COLS, 1)
    gw_mod, gcc = _mod_bwd_w(dm_loc, s_b, c_ctx, w_mod_b)

    small_parts = [g[n] for n in SMALL_NAMES] + [gcc[0]]
    small_shapes = [p.shape for p in small_parts]
    sp = _pack_rows(small_parts)
    tot = _sum_slots(_all_gather8(sp, "gather_small_grads").reshape(8, sp.shape[0], ROW), "sum_small_grads")
    small = dict(zip(SMALL_NAMES + ("c_ctx",), _unpack_rows(tot, small_shapes)))
    grads = {n: small[n].reshape(a[n].shape) for n in SMALL_NAMES if n != "conv_w"}
    grads["c_ctx"] = (0.5 * small["c_ctx"]).reshape(a["c_ctx"].shape)
    grads["conv_w"] = lax.dynamic_slice_in_dim(small["conv_w"], chip * (D_FF // N_CHIPS), D_FF // N_CHIPS, 1)[None]
    grads["b_mod"] = gb_mod
    grads["w_mod"] = gw_mod[None]

    last = [g[n] for n in LAST_GRADS]
    last_pairs = [_pair_sum(gf, rv, k, ci, "rs_pair_" + n)
                  for gf, rv, k, n in zip(last, _rs_sibling(last, LAST_KINDS, "rs_sibling_last"), LAST_KINDS, LAST_GRADS)]
    last_got = _rs_chips(last_pairs, LAST_KINDS)
    pos = jnp.stack([ci, chip])
    order = FIRST_GRADS + LAST_GRADS
    order_kinds = FIRST_KINDS + LAST_KINDS
    halves = [_sum_chips(p, t, k, pos, "rs_sum_" + n)
              for p, t, k, n in zip(first_pairs + last_pairs, list(first_got) + list(last_got), order_kinds, order)]
    for n, t in zip(order, _rs_back(halves, order_kinds)):
        grads[n] = t[None]

    delta, new_m, new_v = {}, {}, {}
    for n in BIG_NAMES + ("w_mod",):
        for dst, t in zip((delta, new_m, new_v), _adamw(a[n][0], grads[n][0], a["m_" + n][0], a["v_" + n][0], "adamw_" + n)):
            dst[n] = t[None]
    rest = [n for n in WEIGHT_NAMES if n not in BIG_NAMES and n != "w_mod"]
    shapes = [a[n].shape for n in rest]
    pr = lambda pre: _pack_rows([a[pre + n] for n in rest])
    for dst, t in zip((delta, new_m, new_v),
                      _adamw(pr(""), _pack_rows([grads[n] for n in rest]), pr("m_"), pr("v_"), "adamw_small")):
        dst.update(zip(rest, _unpack_rows(t, shapes)))

    return (loss, grad_x[None], *[grads[n] for n in WEIGHT_NAMES], *[delta[n] for n in WEIGHT_NAMES],
            *[new_m[n] for n in WEIGHT_NAMES], *[new_v[n] for n in WEIGHT_NAMES])


def kernel(x, c, ctx, c_ctx, w_mod, b_mod, norm1_w, w_in, s5_lambda_re_f, s5_lambda_im_f, s5_log_step_f, s5_lambda_re_b, s5_lambda_im_b, s5_log_step_b, s5_b_re, s5_b_im, s5_c_re, s5_c_im, s5_d, s5_w_glu, s5_b_glu, ret_log_decay_f, ret_log_decay_b, w_out, norm2_w, w_up, conv_w, conv_b, w_down, final_norm_w, loss_target, m_c_ctx, m_w_mod, m_b_mod, m_norm1_w, m_w_in, m_s5_lambda_re_f, m_s5_lambda_im_f, m_s5_log_step_f, m_s5_lambda_re_b, m_s5_lambda_im_b, m_s5_log_step_b, m_s5_b_re, m_s5_b_im, m_s5_c_re, m_s5_c_im, m_s5_d, m_s5_w_glu, m_s5_b_glu, m_ret_log_decay_f, m_ret_log_decay_b, m_w_out, m_norm2_w, m_w_up, m_conv_w, m_conv_b, m_w_down, m_final_norm_w, v_c_ctx, v_w_mod, v_b_mod, v_norm1_w, v_w_in, v_s5_lambda_re_f, v_s5_lambda_im_f, v_s5_log_step_f, v_s5_lambda_re_b, v_s5_lambda_im_b, v_s5_log_step_b, v_s5_b_re, v_s5_b_im, v_s5_c_re, v_s5_c_im, v_s5_d, v_s5_w_glu, v_s5_b_glu, v_ret_log_decay_f, v_ret_log_decay_b, v_w_out, v_norm2_w, v_w_up, v_conv_w, v_conv_b, v_w_down, v_final_norm_w):
    return _step(dict(locals()))
```

```python
import functools
import math

import jax
import jax.numpy as jnp
from jax import lax
from jax.experimental import pallas as pl
from jax.experimental.pallas import tpu as pltpu

F32 = jnp.float32
BF16 = jnp.bfloat16

D_MODEL = 1024
S5_WIDTH = 512
S5_GROUPS = 32
S5_GROUP = 16
S5_STATE = 64
RET_WIDTH = 512
RET_HEADS = 4
RET_DH = 128
RET_CHUNK = 256
GRID_W = 64
ROPE_THETA = 10000.0
D_FF = 2816
NORM_EPS = 1e-6
IN_COLS = S5_WIDTH + 4 * RET_WIDTH

S5_T = 16
S5_NB = 4
S5_BW = S5_T * 128
S5_SW = 8 * 2 * S5_STATE

ADAM_LR, ADAM_B1, ADAM_B2, ADAM_EPS, ADAM_WD, ADAM_STEP = 0.001, 0.9, 0.999, 1e-08, 0.01, 10

VMEM_LIMIT = 56 * 1024 * 1024
MM_TN_VMEM = 40 * 1024 * 1024
MESH_ID = pl.DeviceIdType.MESH


def _params(sem=None):
    return pltpu.CompilerParams(dimension_semantics=sem, vmem_limit_bytes=VMEM_LIMIT)


def _full(shape):
    n = len(shape)
    return pl.BlockSpec(shape, lambda *_: (0,) * n)


def _dot(a, b):
    return jnp.dot(a, b, preferred_element_type=F32)


def _dot_nt(a, b):
    return lax.dot_general(a, b, (((1,), (1,)), ((), ())), preferred_element_type=F32)


def _dot_tn(a, b):
    return lax.dot_general(a, b, (((0,), (0,)), ((), ())), preferred_element_type=F32)


def _dot_hi(a, b):
    return jnp.dot(a, b, preferred_element_type=F32, precision=lax.Precision.HIGHEST)


def _dot_nt_hi(a, b):
    return lax.dot_general(a, b, (((1,), (1,)), ((), ())), preferred_element_type=F32,
                           precision=lax.Precision.HIGHEST)


def _gelu(x):
    return 0.5 * x * (1.0 + jnp.tanh(0.7978845608028654 * (x + 0.044715 * (x * x * x))))


def _gelu_and_grad(x):
    c, ca = 0.7978845608028654, 0.7978845608028654 * 0.044715
    x2 = x * x
    t = jnp.tanh(x * (c + ca * x2))
    h = 0.5 * x
    return h + h * t, 0.5 + 0.5 * t + h * (1.0 - t * t) * (c + 3.0 * ca * x2)


def _sigmoid(x):
    return 1.0 / (1.0 + jnp.exp(-x))


def _silu(x):
    return x * _sigmoid(x)


def _rms_mod(x, nw, sh, sc):
    r = lax.rsqrt(jnp.mean(x * x, axis=-1, keepdims=True) + NORM_EPS)
    return (x * r * nw) * (1.0 + sc) + sh


def _rms(x, nw):
    r = lax.rsqrt(jnp.mean(x * x, axis=-1, keepdims=True) + NORM_EPS)
    return x * r * nw


def _head_norm_gate(y, g):
    mu = jnp.mean(y, axis=-1, keepdims=True)
    yc = y - mu
    var = jnp.mean(yc * yc, axis=-1, keepdims=True)
    return _silu(g) * (yc * lax.rsqrt(var + NORM_EPS))


def _swap_pairs(t):
    lane = lax.broadcasted_iota(jnp.int32, t.shape, 1)
    return jnp.where(lane % 2 == 0, pltpu.roll(t, RET_DH - 1, 1), pltpu.roll(t, 1, 1))


def _rope(t, cos_t, sin_t):
    return t * cos_t + _swap_pairs(t) * sin_t


def _rope_t(dt, cos_t, sin_t):
    return dt * cos_t + _swap_pairs(dt * sin_t)


def _pick(n, prefs):
    for p in prefs:
        if n % p == 0:
            return p
    return n


def _mm_tn(a, b, *, name):
    m, k = a.shape
    n = b.shape[1]
    tn = _pick(n, (1408, 1024, 1280, 512))
    fits = lambda t: 2 * (2 * t * k + 2 * t * tn + 4 * k * tn) <= MM_TN_VMEM
    tm = _pick(m, [t for t in (2816, 2048, 1024, 768, 512, 256) if fits(t)] + [128])

    def body(a_ref, b_ref, o_ref):
        @pl.when(pl.program_id(1) == 0)
        def _():
            o_ref[...] = jnp.zeros_like(o_ref)
        o_ref[...] += _dot_tn(a_ref[...], b_ref[...])

    return pl.pallas_call(
        body, name=name, grid=(n // tn, m // tm),
        in_specs=[pl.BlockSpec((tm, k), lambda j, i: (i, 0)), pl.BlockSpec((tm, tn), lambda j, i: (i, j))],
        out_specs=pl.BlockSpec((k, tn), lambda j, i: (0, j)),
        out_shape=jax.ShapeDtypeStruct((k, n), F32),
        compiler_params=_params(("parallel", "arbitrary")),
    )(a, b)


TOK_TILE = 256


def _behind(step, last, copies):
    @pl.when(step == 0)
    def _():
        for cp in copies(with_arrivals=False)[0]:
            cp.start()

    @pl.when(step == last)
    def _():
        sends, arrivals = copies()
        for cp in arrivals:
            cp.wait_recv()
        for cp in sends:
            cp.wait_send()


def _s5_put_rows(rows_ref, scr, val):
    nchunk = scr.shape[0]
    for c in range(nchunk):
        scr[c] = val[S5_T * c:S5_T * (c + 1), :]
    for b in range(S5_NB):
        for j in range(S5_T):
            rows_ref[b, :, 128 * j:128 * (j + 1)] = scr[:, j, 128 * b:128 * (b + 1)].astype(BF16)


def _norm_inproj(x, ctx, n1w, mod4, w_in_b, placed, kinds):
    l, lc = x.shape[0], ctx.shape[0]
    tm = TOK_TILE
    nct = lc // tm
    la = l + lc
    n = len(placed)
    shard_shapes = _gather_shard_shapes(placed, kinds)

    def body(x_ref, c_ref, nw_ref, mod_ref, w_ref, *rest):
        p_ref, h_ref, u_ref = rest[n:n + 3]
        send_sems, recv_sems, u_scr = rest[2 * n + 3:]
        _behind(pl.program_id(0), la // tm - 1,
                functools.partial(_gather_chip_copies, rest[n + 3:2 * n + 3], kinds, shard_shapes, send_sems, recv_sems))
        is_ctx = pl.program_id(0) < nct
        xt = jnp.where(is_ctx, c_ref[...], x_ref[...])
        sh = jnp.where(is_ctx, mod_ref[0:1, :], mod_ref[2:3, :])
        sc = jnp.where(is_ctx, mod_ref[1:2, :], mod_ref[3:4, :])
        hb = _rms_mod(xt, nw_ref[...], sh, sc).astype(BF16)
        h_ref[...] = hb
        p = _dot(hb, w_ref[...])
        p_ref[...] = p
        _s5_put_rows(u_ref, u_scr, p[:, 0:S5_WIDTH])

    return pl.pallas_call(
        body, name="norm_inproj", grid=(la // tm,),
        in_specs=[pl.BlockSpec((tm, D_MODEL), lambda i: (jnp.maximum(i - nct, 0), 0)),
                  pl.BlockSpec((tm, D_MODEL), lambda i: (jnp.minimum(i, nct - 1), 0)),
                  _full((1, D_MODEL)), _full((4, D_MODEL)), _full((D_MODEL, IN_COLS))] + [ANY] * n,
        out_specs=[pl.BlockSpec((tm, IN_COLS), lambda i: (i, 0)), pl.BlockSpec((tm, D_MODEL), lambda i: (i, 0)),
                   pl.BlockSpec((S5_NB, tm // S5_T, S5_BW), lambda i: (0, i, 0))] + [ANY] * n,
        out_shape=[jax.ShapeDtypeStruct((la, IN_COLS), F32), jax.ShapeDtypeStruct((la, D_MODEL), BF16),
                   jax.ShapeDtypeStruct((S5_NB, la // S5_T, S5_BW), BF16)]
        + [jax.ShapeDtypeStruct(p.shape, p.dtype) for p in placed],
        input_output_aliases={5 + a: 3 + a for a in range(n)},
        scratch_shapes=[pltpu.SemaphoreType.DMA((n, 3)), pltpu.SemaphoreType.DMA((n, 3)),
                        pltpu.VMEM((tm // S5_T, S5_T, S5_WIDTH), F32)],
        compiler_params=_params(("arbitrary",)),
    )(x, ctx, n1w, mod4, w_in_b, *placed)


def _iota2(shape, dim):
    return lax.broadcasted_iota(jnp.int32, shape, dim)


def _group_mask(rows, cols, row_div, col_div):
    return jnp.where(_iota2((rows, cols), 0) // row_div == _iota2((rows, cols), 1) // col_div, 1.0, 0.0).astype(F32)


def _s5_gen_dir(lre, lim, lst, b_re, b_im, c_re, c_im):
    step = jnp.exp(lst)
    mag = jnp.exp(lre * step)
    ar = mag * jnp.cos(lim * step)
    ai = mag * jnp.sin(lim * step)
    den = lre * lre + lim * lim
    xr = ar - 1.0
    cr = (xr * lre + ai * lim) / den
    ci = (ai * lre - xr * lim) / den
    rexp = _group_mask(128, 8, S5_GROUP, 1)
    are, aie = _dot_hi(rexp, ar), _dot_hi(rexp, ai)
    cre, cie = _dot_hi(rexp, cr), _dot_hi(rexp, ci)
    bbr = cre * b_re - cie * b_im
    bbi = cre * b_im + cie * b_re
    gmask = _group_mask(128, 128, S5_GROUP, S5_GROUP)
    pr, pi = jnp.ones_like(are), jnp.zeros_like(are)
    xs, ys = [], []
    for t in range(S5_T + 1):
        if t < S5_T:
            xs.append(jnp.concatenate([bbr * pr - bbi * pi, bbr * pi + bbi * pr], axis=1))
        ys.append(jnp.concatenate([c_re * pr - c_im * pi, -(c_re * pi + c_im * pr)], axis=1))
        pr, pi = pr * are - pi * aie, pr * aie + pi * are
    gs = [_dot_nt_hi(x_t, ys[0]) * gmask for x_t in xs]
    r16, i16 = ar, ai
    for _ in range(4):
        r16, i16 = r16 * r16 - i16 * i16, 2.0 * r16 * i16
    return xs, ys, gs, jnp.concatenate([r16, i16], axis=1)


def _s5_expand(z):
    return jnp.concatenate([z] * 8, axis=1) * _group_mask(128, S5_SW, S5_GROUP, 128)


def _s5_contract(z):
    zm = z * _group_mask(128, S5_SW, S5_GROUP, 128)
    acc = zm[:, 0:128]
    for k in range(1, 8):
        acc = acc + zm[:, 128 * k:128 * (k + 1)]
    return acc


def _s5_param_specs():
    blk3 = lambda r, c: pl.BlockSpec((1, 1, r, c), lambda b, *_: (0, b, 0, 0))
    dir3 = lambda r, c: pl.BlockSpec((2, 1, r, c), lambda b, *_: (0, b, 0, 0))
    return [dir3(8, S5_STATE), dir3(8, S5_STATE), dir3(8, 1), blk3(128, S5_STATE), blk3(128, S5_STATE),
            blk3(128, S5_STATE), blk3(128, S5_STATE), blk3(1, 128)]


def _s5_gen(lre, lim, lst, b_re, b_im, c_re, c_im, dvec, placed, kinds):
    n = len(placed)
    shard_shapes = _gather_shard_shapes(placed, kinds)

    def body(lre_ref, lim_ref, lst_ref, bre_ref, bim_ref, cre_ref, cim_ref, d_ref, *rest):
        gg_ref, xw_ref, yw_ref, a16_ref = rest[n:n + 4]
        _behind(pl.program_id(0), S5_NB - 1,
                functools.partial(_gather_chip_copies, rest[n + 4:2 * n + 4], kinds, shard_shapes, *rest[2 * n + 4:]))
        eye = _group_mask(128, 128, 1, 1)
        g0 = eye * d_ref[0, 0]
        for dr in range(2):
            xs, ys, gs, a16 = _s5_gen_dir(lre_ref[dr, 0], lim_ref[dr, 0], lst_ref[dr, 0], bre_ref[0, 0],
                                          bim_ref[0, 0], cre_ref[0, 0], cim_ref[0, 0])
            a16_ref[0, dr] = a16
            for j in range(S5_T):
                xw_ref[0, dr, j] = xs[S5_T - 1 - j if dr == 0 else j]
                yw_ref[0, dr, j] = ys[j + 1 if dr == 0 else S5_T - j]
            g0 = g0 + gs[0]
            for t in range(1, S5_T):
                gg_ref[0, (S5_T - 1) + t if dr == 0 else (S5_T - 1) - t] = gs[t]
        gg_ref[0, S5_T - 1] = g0

    blk = pl.BlockSpec((1, 2, S5_T, 128, 128), lambda b: (b, 0, 0, 0, 0))
    return pl.pallas_call(
        body, name="s5_gen", grid=(S5_NB,),
        in_specs=_s5_param_specs() + [ANY] * n,
        out_specs=[pl.BlockSpec((1, 2 * S5_T - 1, 128, 128), lambda b: (b, 0, 0, 0)), blk, blk,
                   pl.BlockSpec((1, 2, 8, 128), lambda b: (b, 0, 0, 0))] + [ANY] * n,
        out_shape=[jax.ShapeDtypeStruct((S5_NB, 2 * S5_T - 1, 128, 128), F32),
                   jax.ShapeDtypeStruct((S5_NB, 2, S5_T, 128, 128), F32),
                   jax.ShapeDtypeStruct((S5_NB, 2, S5_T, 128, 128), F32),
                   jax.ShapeDtypeStruct((S5_NB, 2, 8, 128), F32)]
        + [jax.ShapeDtypeStruct(p.shape, p.dtype) for p in placed],
        input_output_aliases={8 + a: 4 + a for a in range(n)},
        scratch_shapes=[pltpu.SemaphoreType.DMA((n, 3)), pltpu.SemaphoreType.DMA((n, 3))],
        compiler_params=_params(("arbitrary",)),
    )(lre, lim, lst, b_re, b_im, c_re, c_im, dvec, *placed)


def _s5_fill_state_mat(w_scr, src_ref, dr):
    for j in range(S5_T):
        w_scr[128 * j:128 * (j + 1), :] = _s5_expand(src_ref[0, dr, j]).astype(BF16)


def _s5_fill_toeplitz(k_scr, gg_ref):
    for j in range(S5_T):
        for i in range(S5_T):
            k_scr[128 * j:128 * (j + 1), 128 * i:128 * (i + 1)] = gg_ref[0, i - j + (S5_T - 1)].astype(BF16)


S5_GEN_SPECS = [pl.BlockSpec((1, 2 * S5_T - 1, 128, 128), lambda b: (b, 0, 0, 0)),
                pl.BlockSpec((1, 2, S5_T, 128, 128), lambda b: (b, 0, 0, 0, 0))]


def _s5_gen_bwd(lre, lim, lst, b_re, b_im, c_re, c_im, dvec, dg, dx, dy, da16):
    def body(lre_ref, lim_ref, lst_ref, bre_ref, bim_ref, cre_ref, cim_ref, d_ref, dg_ref, dx_ref, dy_ref, da16_ref,
             glre_ref, glim_ref, glst_ref, gbre_ref, gbim_ref, gcre_ref, gcim_ref, gd_ref):
        eye = _group_mask(128, 128, 1, 1)
        gd_ref[0, 0] = jnp.sum(dg_ref[0, S5_T - 1] * eye, axis=0, keepdims=True)
        gb = [None, None, None, None]
        for dr in range(2):
            args = (lre_ref[dr, 0], lim_ref[dr, 0], lst_ref[dr, 0], bre_ref[0, 0], bim_ref[0, 0],
                    cre_ref[0, 0], cim_ref[0, 0])
            _, vjp = jax.vjp(_s5_gen_dir, *args)
            dxs = [dx_ref[0, dr, S5_T - 1 - t if dr == 0 else t] for t in range(S5_T)]
            dys = [jnp.zeros((128, 128), F32)] + [dy_ref[0, dr, t - 1 if dr == 0 else S5_T - t]
                                                  for t in range(1, S5_T + 1)]
            dgs = [dg_ref[0, (S5_T - 1) + t if dr == 0 else (S5_T - 1) - t] for t in range(S5_T)]
            g = vjp((dxs, dys, dgs, da16_ref[0, dr]))
            glre_ref[dr, 0] = g[0]
            glim_ref[dr, 0] = g[1]
            glst_ref[dr, 0] = g[2]
            for q in range(4):
                gb[q] = g[3 + q] if gb[q] is None else gb[q] + g[3 + q]
        gbre_ref[0, 0] = gb[0]
        gbim_ref[0, 0] = gb[1]
        gcre_ref[0, 0] = gb[2]
        gcim_ref[0, 0] = gb[3]

    shp = lambda a: jax.ShapeDtypeStruct(a.shape, F32)
    return pl.pallas_call(
        body, name="s5_gen_bwd", grid=(S5_NB,),
        in_specs=_s5_param_specs() + [
            pl.BlockSpec((1, 2 * S5_T - 1, 128, 128), lambda b: (b, 0, 0, 0)),
            pl.BlockSpec((1, 2, S5_T, 128, 128), lambda b: (b, 0, 0, 0, 0)),
            pl.BlockSpec((1, 2, S5_T, 128, 128), lambda b: (b, 0, 0, 0, 0)),
            pl.BlockSpec((1, 2, 8, 128), lambda b: (b, 0, 0, 0))],
        out_specs=_s5_param_specs(),
        out_shape=[shp(lre), shp(lim), shp(lst), shp(b_re), shp(b_im), shp(c_re), shp(c_im), shp(dvec)],
        compiler_params=_params(("parallel",)),
    )(lre, lim, lst, b_re, b_im, c_re, c_im, dvec, dg, dx, dy, da16)


def _s5_put_groups(o_ref, dr, val):
    for gi in range(8):
        o_ref[dr, :, gi, :] = val[:, 128 * gi:128 * (gi + 1)]


def _s5_get_groups(s_ref, dr, n=8):
    return jnp.concatenate([s_ref[dr, :, gi, :] for gi in range(n)], axis=1).astype(BF16)


def _s5_to_states(u3, blocks, name):
    cn = u3.shape[1]

    def body(u_ref, b_ref, o_ref, w_scr):
        u = u_ref[0]
        for dr in range(2):
            _s5_fill_state_mat(w_scr, b_ref, dr)
            _s5_put_groups(o_ref, dr, _dot(u, w_scr[...]))

    return pl.pallas_call(
        body, name=name, grid=(S5_NB,),
        in_specs=[pl.BlockSpec((1, cn, S5_BW), lambda b: (b, 0, 0)), S5_GEN_SPECS[1]],
        out_specs=pl.BlockSpec((2, cn, 8, 128), lambda b: (0, 0, b, 0)),
        out_shape=jax.ShapeDtypeStruct((2, cn, S5_GROUPS, 128), F32),
        scratch_shapes=[pltpu.VMEM((S5_BW, S5_SW), BF16)],
        compiler_params=_params(("parallel",)),
    )(u3, blocks)


def _s5_from_states(u3, gg, st, blocks, transposed, name):
    cn = u3.shape[1]

    def body(u_ref, g_ref, s_ref, b_ref, o_ref, k_scr, w_scr):
        u = u_ref[0]
        _s5_fill_toeplitz(k_scr, g_ref)
        y = _dot_nt(u, k_scr[...]) if transposed else _dot(u, k_scr[...])
        for dr in range(2):
            _s5_fill_state_mat(w_scr, b_ref, dr)
            y = y + _dot_nt(_s5_get_groups(s_ref, dr), w_scr[...])
        for i in range(S5_T):
            o_ref[:, i, :] = y[:, 128 * i:128 * (i + 1)]

    return pl.pallas_call(
        body, name=name, grid=(S5_NB,),
        in_specs=[pl.BlockSpec((1, cn, S5_BW), lambda b: (b, 0, 0)), S5_GEN_SPECS[0],
                  pl.BlockSpec((2, cn, 8, 128), lambda b: (0, 0, b, 0)), S5_GEN_SPECS[1]],
        out_specs=pl.BlockSpec((cn, S5_T, 128), lambda b: (0, 0, b)),
        out_shape=jax.ShapeDtypeStruct((cn, S5_T, S5_WIDTH), F32),
        scratch_shapes=[pltpu.VMEM((S5_BW, S5_BW), BF16), pltpu.VMEM((S5_BW, S5_SW), BF16)],
        compiler_params=_params(("parallel",)),
    )(u3, gg, st, blocks)


def _s5_a_forms(a):
    ra = pltpu.roll(a, S5_STATE, 1)
    low = _iota2(a.shape, 1) < S5_STATE
    return jnp.where(low, a, ra), jnp.where(low, -ra, a)


def _s5_scan(sloc, a16, ncc, placed, kinds):
    cn = sloc.shape[1]
    n = len(placed)
    shard_shapes = _gather_shard_shapes(placed, kinds)

    def body(s_ref, a_ref, *rest):
        h_ref = rest[n]
        sends, arrivals = _gather_chip_copies(rest[n + 1:2 * n + 1], kinds, shard_shapes, *rest[2 * n + 1:])
        for cp in sends:
            cp.start()
        forms = [_s5_a_forms(a_ref[dr]) for dr in range(2)]

        def step(s, hs):
            out = []
            for dr in range(2):
                arr, aii = forms[dr]
                h, rh = hs[dr]
                c = s if dr == 0 else jnp.where(s < ncc, ncc - 1 - s, cn - 1 - (s - ncc))
                h_ref[dr, c] = h
                sc = s_ref[dr, c]
                out.append((h * arr + rh * aii + sc, rh * arr - h * aii + pltpu.roll(sc, S5_STATE, 1)))
            return tuple(out)

        zero = jnp.zeros((S5_GROUPS, 128), F32)
        lax.fori_loop(0, cn, step, ((zero, zero), (zero, zero)), unroll=4)
        for cp in arrivals:
            cp.wait_recv()
        for cp in sends:
            cp.wait_send()

    vmem = pl.BlockSpec(memory_space=pltpu.VMEM)
    return pl.pallas_call(
        body, name="s5_scan",
        in_specs=[vmem, vmem] + [ANY] * n, out_specs=[vmem] + [ANY] * n,
        out_shape=[jax.ShapeDtypeStruct(sloc.shape, F32)] + [jax.ShapeDtypeStruct(p.shape, p.dtype) for p in placed],
        input_output_aliases={2 + a: 1 + a for a in range(n)},
        scratch_shapes=[pltpu.SemaphoreType.DMA((n, 3)), pltpu.SemaphoreType.DMA((n, 3))],
        compiler_params=_params(),
    )(sloc, a16, *placed)


def _s5_scan_bwd(e, hs, a16, ncc):
    cn = e.shape[1]

    def body(e_ref, h_ref, a_ref, ds_ref, da_ref):
        forms = [_s5_a_forms(a_ref[dr]) for dr in range(2)]
        low = _iota2((S5_GROUPS, 128), 1) < S5_STATE

        def step(s, carry):
            out = []
            r = cn - 1 - s
            for dr in range(2):
                arr, aii = forms[dr]
                g, rg, da = carry[dr]
                c = r if dr == 0 else jnp.where(r < ncc, ncc - 1 - r, cn - 1 - (r - ncc))
                ds_ref[dr, c] = g
                h = h_ref[dr, c]
                rh = pltpu.roll(h, S5_STATE, 1)
                da = da + jnp.where(low, g * h + rg * rh, g * rh - rg * h)
                ec = e_ref[dr, c]
                out.append((ec + g * arr - rg * aii, pltpu.roll(ec, S5_STATE, 1) + rg * arr + g * aii, da))
            return tuple(out)

        zero = jnp.zeros((S5_GROUPS, 128), F32)
        res = lax.fori_loop(0, cn, step, ((zero, zero, zero), (zero, zero, zero)), unroll=4)
        da_ref[0] = res[0][2]
        da_ref[1] = res[1][2]

    return pl.pallas_call(
        body, name="s5_scan_bwd",
        out_shape=[jax.ShapeDtypeStruct(e.shape, F32), jax.ShapeDtypeStruct((2, S5_GROUPS, 128), F32)],
        compiler_params=_params(),
    )(e, hs, a16)


def _s5_bwd_kb(p3, dy3):
    cn = p3.shape[1]
    half = S5_T // 2

    def body(u_ref, d_ref, o_ref):
        q = pl.program_id(1)

        @pl.when(q == 0)
        def _():
            o_ref[...] = jnp.zeros_like(o_ref)

        dk = _dot_tn(u_ref[0], d_ref[0])
        for j in range(S5_T):
            for i in range(half):
                o_ref[0, half * q + i - j + (S5_T - 1)] += dk[128 * j:128 * (j + 1), 128 * i:128 * (i + 1)]

    return pl.pallas_call(
        body, name="s5_bwd_kb", grid=(S5_NB, 2),
        in_specs=[pl.BlockSpec((1, cn, S5_BW), lambda b, q: (b, 0, 0)),
                  pl.BlockSpec((1, cn, S5_BW // 2), lambda b, q: (b, 0, q))],
        out_specs=pl.BlockSpec((1, 2 * S5_T - 1, 128, 128), lambda b, q: (b, 0, 0, 0)),
        out_shape=jax.ShapeDtypeStruct((S5_NB, 2 * S5_T - 1, 128, 128), F32),
        compiler_params=_params(("parallel", "arbitrary")),
    )(p3, dy3)


def _s5_bwd_w(u3, st, name):
    cn = u3.shape[1]

    def body(u_ref, s_ref, w_ref):
        dw = _dot_tn(u_ref[0], _s5_get_groups(s_ref, 0))
        for j in range(S5_T):
            w_ref[0, 0, j] = _s5_contract(dw[128 * j:128 * (j + 1), :])

    return pl.pallas_call(
        body, name=name, grid=(S5_NB, 2),
        in_specs=[pl.BlockSpec((1, cn, S5_BW), lambda b, q: (b, 0, 0)),
                  pl.BlockSpec((1, cn, 8, 128), lambda b, q: (q, 0, b, 0))],
        out_specs=pl.BlockSpec((1, 1, S5_T, 128, 128), lambda b, q: (b, q, 0, 0, 0)),
        out_shape=jax.ShapeDtypeStruct((S5_NB, 2, S5_T, 128, 128), F32),
        compiler_params=_params(("parallel", "parallel")),
    )(u3, st)


K_SCALE = RET_DH ** -0.5
G_COL = 16


def _ret_chunk_of(step, ncc, nch, rev):
    if not rev:
        return step
    return jnp.where(step < ncc, ncc - 1 - step, nch - 1 - (step - ncc))


def _ret_decay(ld, rev):
    c = _iota2((RET_CHUNK, RET_CHUNK), 0).astype(F32)
    m = _iota2((RET_CHUNK, RET_CHUNK), 1).astype(F32)
    diff = (m - c) if rev else (c - m)
    keep = (diff > 0) if rev else (diff >= 0)
    expo = jnp.maximum(diff, 0.0)
    dm = jnp.where(keep, jnp.exp(ld * expo), 0.0)
    xi_e = (RET_CHUNK - c) if rev else (c + 1.0)
    zeta_e = c if rev else (RET_CHUNK - 1.0 - c)
    return dm, expo, jnp.exp(ld * xi_e), xi_e, jnp.exp(ld * zeta_e), zeta_e


RET_TABLES = 7


def _ret_tables(ld2):
    def body(ld_ref, t_ref):
        dr, h = pl.program_id(0), pl.program_id(1)
        ldh = ld_ref[dr, h]
        for rev in (False, True):
            @pl.when(dr == int(rev))
            def _(rev=rev):
                dm, expo, xi, xi_e, zeta, zeta_e = _ret_decay(ldh, rev)
                t_ref[0, 0, 0] = dm
                t_ref[0, 0, 1] = dm * expo
                t_ref[0, 0, 2] = xi
                t_ref[0, 0, 3] = xi * xi_e
                t_ref[0, 0, 4] = zeta
                t_ref[0, 0, 5] = zeta * zeta_e
                t_ref[0, 0, 6] = jnp.zeros_like(dm) + jnp.exp(ldh * RET_CHUNK)

    return pl.pallas_call(
        body, name="ret_tables", grid=(2, RET_HEADS),
        in_specs=[pl.BlockSpec(memory_space=pltpu.SMEM)],
        out_specs=pl.BlockSpec((1, 1, RET_TABLES, RET_CHUNK, RET_CHUNK), lambda d, h: (d, h, 0, 0, 0)),
        out_shape=jax.ShapeDtypeStruct((2, RET_HEADS, RET_TABLES, RET_CHUNK, RET_CHUNK), F32),
        compiler_params=_params(("parallel", "parallel")),
    )(ld2)


def _ret_specs(nch, ncc, rev, step_of):
    chunk = lambda n: _ret_chunk_of(step_of(n), ncc, nch, rev)
    cols = [pl.BlockSpec((RET_CHUNK, RET_WIDTH), functools.partial(lambda n, cb: (chunk(n), cb), cb=cb))
            for cb in (1, 2, 3)]
    tab = pl.BlockSpec((RET_CHUNK, RET_DH), lambda n: (chunk(n), 0))
    return cols + [tab, tab], pl.BlockSpec((RET_CHUNK, RET_WIDTH), lambda n: (chunk(n), 0))


def _ret_scan(p_all, cos_t, sin_t, tabs, ncc):
    la = p_all.shape[0]
    nch = la // RET_CHUNK

    def body(t_ref, qf, kf, vf, cf, sf, qb, kb, vb, cb, sb, of_ref, ob_ref, ssf_ref, ssb_ref, s_scr):
        @pl.when(pl.program_id(0) == 0)
        def _():
            s_scr[...] = jnp.zeros_like(s_scr)

        for dr, (q_ref, k_ref, v_ref, c_ref, n_ref, o_ref, ss_ref) in enumerate(
                ((qf, kf, vf, cf, sf, of_ref, ssf_ref), (qb, kb, vb, cb, sb, ob_ref, ssb_ref))):
            cs, sn = c_ref[...], n_ref[...]
            for h in range(RET_HEADS):
                sl = slice(RET_DH * h, RET_DH * (h + 1))
                dm, xi, zeta = t_ref[dr, h, 0], t_ref[dr, h, 2, :, 0:RET_DH], t_ref[dr, h, 4, :, 0:RET_DH]
                q = _rope(q_ref[:, sl], cs, sn)
                k = _rope(k_ref[:, sl] * K_SCALE, cs, sn)
                vh = v_ref[:, sl].astype(BF16)
                s = s_scr[dr, h]
                ss_ref[0, h] = s
                sc = (_dot_nt(q.astype(BF16), k.astype(BF16)) * dm).astype(BF16)
                o_ref[:, sl] = _dot(sc, vh) + _dot((q * xi).astype(BF16), s.astype(BF16))
                s_scr[dr, h] = t_ref[dr, h, 6, 0:RET_DH, 0:RET_DH] * s + _dot_tn((k * zeta).astype(BF16), vh)

    in_f, out_f = _ret_specs(nch, ncc, False, lambda n: n)
    in_b, out_b = _ret_specs(nch, ncc, True, lambda n: n)
    ss_spec = pl.BlockSpec((1, RET_HEADS, RET_DH, RET_DH), lambda n: (n, 0, 0, 0))
    o_shape = jax.ShapeDtypeStruct((la, RET_WIDTH), F32)
    ss_shape = jax.ShapeDtypeStruct((nch, RET_HEADS, RET_DH, RET_DH), F32)
    return pl.pallas_call(
        body, name="ret_scan", grid=(nch,),
        in_specs=[_full(tabs.shape)] + in_f + in_b,
        out_specs=[out_f, out_b, ss_spec, ss_spec],
        out_shape=[o_shape, o_shape, ss_shape, ss_shape],
        scratch_shapes=[pltpu.VMEM((2, RET_HEADS, RET_DH, RET_DH), F32)],
        compiler_params=_params(("arbitrary",)),
    )(tabs, p_all, p_all, p_all, cos_t, sin_t, p_all, p_all, p_all, cos_t, sin_t)


def _ret_scan_bwd(p_all, cos_t, sin_t, tabs, ssf, ssb, dy_all, ncc):
    la = p_all.shape[0]
    nch = la // RET_CHUNK

    def body(t_ref, qf, kf, vf, cf, sf, dof, ssf_ref, qb, kb, vb, cb, sb, dob_, ssb_ref,
             dqf, dkf, dvf, dqb, dkb, dvb, dld_ref, ds_scr):
        @pl.when(pl.program_id(0) == 0)
        def _():
            ds_scr[...] = jnp.zeros_like(ds_scr)
            dld_ref[...] = jnp.zeros_like(dld_ref)

        for dr, (q_ref, k_ref, v_ref, c_ref, n_ref, do_ref, ss_ref, dq_ref, dk_ref, dv_ref) in enumerate(
                ((qf, kf, vf, cf, sf, dof, ssf_ref, dqf, dkf, dvf), (qb, kb, vb, cb, sb, dob_, ssb_ref, dqb, dkb, dvb))):
            cs, sn = c_ref[...], n_ref[...]
            on_ctx = _ret_chunk_of(nch - 1 - pl.program_id(0), ncc, nch, dr == 1) < ncc
            for h in range(RET_HEADS):
                sl = slice(RET_DH * h, RET_DH * (h + 1))
                dm, dm_d = t_ref[dr, h, 0], t_ref[dr, h, 1]
                xi, xi_d, zeta, zeta_d = [t_ref[dr, h, t, :, 0:RET_DH] for t in (2, 3, 4, 5)]
                gc = t_ref[dr, h, 6, 0:RET_DH, 0:RET_DH]
                q = _rope(q_ref[:, sl], cs, sn)
                k = _rope(k_ref[:, sl] * K_SCALE, cs, sn)
                q16, k16, v16 = q.astype(BF16), k.astype(BF16), v_ref[:, sl].astype(BF16)
                s = ss_ref[0, h]
                s16 = s.astype(BF16)
                ds_in = ds_scr[dr, h]
                ds16 = ds_in.astype(BF16)
                do16 = jnp.where(on_ctx, 0.0, do_ref[:, sl]).astype(BF16)
                qk = _dot_nt(q16, k16)
                dsv = _dot_nt(do16, v16)
                dsc = (dsv * dm).astype(BF16)
                sc16 = (qk * dm).astype(BF16)
                dos = _dot_nt(do16, s16)
                vds = _dot_nt(v16, ds16)
                dq_ref[:, sl] = _dot(dsc, k16) + dos * xi
                dk_ref[:, sl] = _dot_tn(dsc, q16) + vds * zeta
                dv_ref[:, sl] = _dot_tn(sc16, do16) + _dot((k * zeta).astype(BF16), ds16)
                ds_scr[dr, h] = _dot_tn((q * xi).astype(BF16), do16) + gc * ds_in
                dld = (jnp.sum(dsv * qk * dm_d) + jnp.sum(q * dos * xi_d + k * vds * zeta_d)
                       + RET_CHUNK * jnp.sum(gc * s * ds_in))
                dld_ref[dr, h] += dld

    back = lambda n: nch - 1 - n
    in_f, out_f = _ret_specs(nch, ncc, False, back)
    in_b, out_b = _ret_specs(nch, ncc, True, back)
    ss_spec = pl.BlockSpec((1, RET_HEADS, RET_DH, RET_DH), lambda n: (nch - 1 - n, 0, 0, 0))
    shp = jax.ShapeDtypeStruct((la, RET_WIDTH), F32)
    dy_spec = lambda rev: pl.BlockSpec(
        (RET_CHUNK, RET_WIDTH), lambda n: (jnp.maximum(_ret_chunk_of(nch - 1 - n, ncc, nch, rev) - ncc, 0), 0))
    return pl.pallas_call(
        body, name="ret_scan_bwd", grid=(nch,),
        in_specs=[_full(tabs.shape)] + in_f + [dy_spec(False), ss_spec] + in_b + [dy_spec(True), ss_spec],
        out_specs=[out_f, out_f, out_f, out_b, out_b, out_b, _full((2, RET_HEADS, 8, 128))],
        out_shape=[shp] * 6 + [jax.ShapeDtypeStruct((2, RET_HEADS, 8, 128), F32)],
        scratch_shapes=[pltpu.VMEM((2, RET_HEADS, RET_DH, RET_DH), F32)],
        compiler_params=_params(("arbitrary",)),
    )(tabs, p_all, p_all, p_all, cos_t, sin_t, dy_all, ssf, p_all, p_all, p_all, cos_t, sin_t, dy_all, ssb)


def _in_bwd(dqf, dkf, dvf, dqb, dkb, dvb, du, dg, cos_t, sin_t, w_in_b, x, ctx, n1w, mod4, dx1):
    l, lc = x.shape[0], ctx.shape[0]
    la = l + lc
    tm = TOK_TILE
    nct = lc // tm

    def body(dqf_ref, dkf_ref, dvf_ref, dqb_ref, dkb_ref, dvb_ref, du_ref, dg_ref, cos_ref, sin_ref,
             w_ref, x_ref, c_ref, nw_ref, mod_ref, dx1_ref, dp_ref, gx_ref, acc_ref):
        i = pl.program_id(0)
        is_ctx = i < nct

        @pl.when(i == 0)
        def _():
            acc_ref[...] = jnp.zeros_like(acc_ref)

        cs, sn = cos_ref[...], sin_ref[...]
        def piece(k, val):
            cols = slice(S5_WIDTH * k, S5_WIDTH * (k + 1))
            dp_ref[:, cols] = val.astype(BF16)
            return _dot_nt(dp_ref[:, cols], w_ref[:, cols])

        dh1 = piece(0, du_ref[...])
        dh1 = dh1 + piece(3, dvf_ref[...] + dvb_ref[...])
        dh1 = dh1 + piece(4, jnp.where(is_ctx, 0.0, dg_ref[...]))
        for k, (f_ref, b_ref, scale) in ((1, (dqf_ref, dqb_ref, 1.0)), (2, (dkf_ref, dkb_ref, K_SCALE))):
            heads = [_rope_t(f_ref[:, RET_DH * h:RET_DH * (h + 1)] + b_ref[:, RET_DH * h:RET_DH * (h + 1)], cs, sn) * scale
                     for h in range(RET_HEADS)]
            dh1 = dh1 + piece(k, jnp.concatenate(heads, axis=1))
        xt = jnp.where(is_ctx, c_ref[...], x_ref[...])
        sh = jnp.where(is_ctx, mod_ref[0:1, :], mod_ref[2:3, :])
        sc = jnp.where(is_ctx, mod_ref[1:2, :], mod_ref[3:4, :])
        _, vjp = jax.vjp(_rms_mod, xt, nw_ref[...], sh, sc)
        dx, dnw, dsh, dsc = vjp(dh1)
        gx_ref[...] = dx + dx1_ref[...]
        cf = jnp.where(is_ctx, 1.0, 0.0)
        acc_ref[0:1, :] += dnw
        acc_ref[1:2, :] += cf * dsh
        acc_ref[2:3, :] += cf * dsc
        acc_ref[3:4, :] += (1.0 - cf) * dsh
        acc_ref[4:5, :] += (1.0 - cf) * dsc

    row = pl.BlockSpec((tm, RET_WIDTH), lambda i: (i, 0))
    tab = pl.BlockSpec((tm, RET_DH), lambda i: (i, 0))
    xrow = pl.BlockSpec((tm, D_MODEL), lambda i: (jnp.maximum(i - nct, 0), 0))
    return pl.pallas_call(
        body, name="in_bwd", grid=(la // tm,),
        in_specs=[row] * 7 + [pl.BlockSpec((tm, RET_WIDTH), lambda i: (jnp.maximum(i - nct, 0), 0)),
                              tab, tab, _full((D_MODEL, IN_COLS)), xrow,
                              pl.BlockSpec((tm, D_MODEL), lambda i: (jnp.minimum(i, nct - 1), 0)),
                              _full((1, D_MODEL)), _full((4, D_MODEL)), xrow],
        out_specs=[pl.BlockSpec((tm, IN_COLS), lambda i: (i, 0)), xrow, _full((8, D_MODEL))],
        out_shape=[jax.ShapeDtypeStruct((la, IN_COLS), BF16), jax.ShapeDtypeStruct((l, D_MODEL), F32),
                   jax.ShapeDtypeStruct((8, D_MODEL), F32)],
        compiler_params=_params(("arbitrary",)),
    )(dqf, dkf, dvf, dqb, dkb, dvb, du, dg, cos_t, sin_t, w_in_b, x, ctx, n1w, mod4, dx1)


def _outproj_up(x, y_all, of, ob, p_all, w_glu_b, b_glu, w_out_b, mod3, n2w, w_up_b, nct):
    l = x.shape[0]
    tm = TOK_TILE

    def body(x_ref, y_ref, of_ref, ob_ref, g_ref, wg_ref, bg_ref, wo_ref, mod_ref, nw_ref, wu_ref,
             x1_ref, mix_ref, h2_ref, up_ref, mb_ref, yr_ref):
        yg = _gelu(y_ref[...])
        mb_ref[:, 0:S5_WIDTH] = (yg * _sigmoid(_dot(yg.astype(BF16), wg_ref[...]) + bg_ref[...])).astype(BF16)
        yr = of_ref[...] + ob_ref[...]
        yr_ref[...] = yr
        for h in range(RET_HEADS):
            sl = slice(RET_DH * h, RET_DH * (h + 1))
            mb_ref[:, S5_WIDTH + RET_DH * h:S5_WIDTH + RET_DH * (h + 1)] = (
                _head_norm_gate(yr[:, sl], g_ref[:, sl]).astype(BF16))
        mix = _dot(mb_ref[...], wo_ref[...])
        mix_ref[...] = mix
        x1 = x_ref[...] + mod_ref[0:1, :] * mix
        x1_ref[...] = x1
        h2 = _rms_mod(x1, nw_ref[...], mod_ref[1:2, :], mod_ref[2:3, :]).astype(BF16)
        h2_ref[...] = h2
        up_ref[...] = _dot(h2, wu_ref[...])

    row = lambda w: pl.BlockSpec((tm, w), lambda i: (i, 0))
    arow = pl.BlockSpec((tm, RET_WIDTH), lambda i: (i + nct, 0))
    return pl.pallas_call(
        body, name="outproj_up", grid=(l // tm,),
        in_specs=[row(D_MODEL), arow, arow, arow, pl.BlockSpec((tm, RET_WIDTH), lambda i: (i + nct, G_COL // 4)),
                  _full((S5_WIDTH, S5_WIDTH)), _full((1, S5_WIDTH)), _full((D_MODEL, D_MODEL)), _full((3, D_MODEL)),
                  _full((1, D_MODEL)), _full((D_MODEL, 2 * D_FF))],
        out_specs=[row(D_MODEL), row(D_MODEL), row(D_MODEL), row(2 * D_FF), row(D_MODEL), row(RET_WIDTH)],
        out_shape=[jax.ShapeDtypeStruct((l, D_MODEL), F32), jax.ShapeDtypeStruct((l, D_MODEL), F32),
                   jax.ShapeDtypeStruct((l, D_MODEL), BF16), jax.ShapeDtypeStruct((l, 2 * D_FF), F32),
                   jax.ShapeDtypeStruct((l, D_MODEL), BF16), jax.ShapeDtypeStruct((l, RET_WIDTH), F32)],
        compiler_params=_params(("parallel",)),
    )(x, y_all, of, ob, p_all, w_glu_b, b_glu, w_out_b, mod3, n2w, w_up_b)


HALO = 8


def _conv_taps(g, prev_row, next_row):
    t = g.shape[0]
    r = _iota2(g.shape, 0)
    gprev = jnp.where(r == 0, prev_row, pltpu.roll(g, 1, 0))
    gnext = jnp.where(r == t - 1, next_row, pltpu.roll(g, t - 1, 0))
    return gprev, gnext


def _ffn_loss(up, x1, conv_w, conv_b, w_down_b, gate, fnw, tgt):
    l = x1.shape[0]
    tm = TOK_TILE
    nt = l // tm
    hb = tm // HALO

    cw = 256

    def body(up_a, up_g, hp_ref, hn_ref, x1_ref, cw_ref, cb_ref, wd_ref, gate_ref, fn_ref, tgt_ref,
             act_ref, dx2_ref, ddn_ref, dact_ref, acc_ref, ddn_scr):
        step = pl.program_id(0)
        i = jnp.minimum(step, nt - 1)

        @pl.when(step == 0)
        def _():
            acc_ref[...] = jnp.zeros_like(acc_ref)
            ddn_scr[...] = jnp.zeros_like(ddn_scr)

        ddn_prev = ddn_scr[...]
        dn = jnp.zeros((tm, D_MODEL), F32)
        for c in range(D_FF // cw):
            cols = slice(cw * c, cw * (c + 1))
            g = up_g[:, cols]
            prev_row = jnp.where(i == 0, 0.0, hp_ref[HALO - 1:HALO, cols])
            next_row = jnp.where(i == nt - 1, 0.0, hn_ref[0:1, cols])
            gprev, gnext = _conv_taps(g, prev_row, next_row)
            gc = cb_ref[:, cols] + gprev * cw_ref[0:1, cols] + g * cw_ref[1:2, cols] + gnext * cw_ref[2:3, cols]
            act = (_gelu(gc) * up_a[:, cols]).astype(BF16)
            act_ref[:, cols] = act
            dn = dn + _dot(act, wd_ref[cols, :])
            dact_ref[:, cols] = _dot_nt(ddn_prev, wd_ref[cols, :])
        x2 = x1_ref[...] + gate_ref[...] * dn
        y, vjp = jax.vjp(_rms, x2, fn_ref[...])
        err = y - tgt_ref[...]
        dx2, dfn = vjp(err * (1.0 / D_MODEL))
        dx2_ref[...] = dx2
        ddn = (dx2 * gate_ref[...]).astype(BF16)
        ddn_ref[...] = ddn
        ddn_scr[...] = ddn
        live = step < nt
        acc_ref[0:1, :] += jnp.where(live, dfn, 0.0)
        acc_ref[1:2, :] += jnp.where(live, jnp.sum(dx2 * dn, axis=0, keepdims=True), 0.0)
        acc_ref[2:3, :] += jnp.where(live, (0.5 / D_MODEL) * jnp.sum(err * err), 0.0)

    tile = lambda s: jnp.minimum(s, nt - 1)
    row = lambda w, cb=0: pl.BlockSpec((tm, w), lambda s: (tile(s), cb))
    last = l // HALO - 1
    return pl.pallas_call(
        body, name="ffn_loss", grid=(nt + 1,),
        in_specs=[row(D_FF, 0), row(D_FF, 1),
                  pl.BlockSpec((HALO, D_FF), lambda s: (jnp.maximum(tile(s) * hb - 1, 0), 1)),
                  pl.BlockSpec((HALO, D_FF), lambda s: (jnp.minimum((tile(s) + 1) * hb, last), 1)),
                  row(D_MODEL), _full((3, D_FF)), _full((1, D_FF)), _full((D_FF, D_MODEL)),
                  _full((1, D_MODEL)), _full((1, D_MODEL)), row(D_MODEL)],
        out_specs=[row(D_FF), row(D_MODEL), row(D_MODEL),
                   pl.BlockSpec((tm, D_FF), lambda s: (jnp.maximum(s - 1, 0), 0)), _full((8, D_MODEL))],
        out_shape=[jax.ShapeDtypeStruct((l, D_FF), BF16), jax.ShapeDtypeStruct((l, D_MODEL), F32),
                   jax.ShapeDtypeStruct((l, D_MODEL), BF16), jax.ShapeDtypeStruct((l, D_FF), F32),
                   jax.ShapeDtypeStruct((8, D_MODEL), F32)],
        scratch_shapes=[pltpu.VMEM((tm, D_MODEL), BF16)],
        compiler_params=_params(("arbitrary",)),
    )(up, up, up, up, x1, conv_w, conv_b, w_down_b, gate, fnw, tgt)


def _convglu_bwd(up, dact, conv_w, conv_b):
    l = up.shape[0]
    tm = 128
    nt = l // tm
    hb = tm // HALO
    te = tm + 2 * HALO

    def body(a_ref, ap_ref, an_ref, g_ref, gp_ref, gn_ref, d_ref, dp_ref, dn_ref, cw_ref, cb_ref,
             dup_ref, acc_ref):
        i = pl.program_id(0)

        @pl.when(i == 0)
        def _():
            acc_ref[...] = jnp.zeros_like(acc_ref)

        def ext(p, c, n):
            return jnp.concatenate([jnp.where(i == 0, 0.0, p[...]), c[...], jnp.where(i == nt - 1, 0.0, n[...])], axis=0)

        ae, ge, de = ext(ap_ref, a_ref, an_ref), ext(gp_ref, g_ref, gn_ref), ext(dp_ref, d_ref, dn_ref)
        gprev = pltpu.roll(ge, 1, 0)
        gnext = pltpu.roll(ge, te - 1, 0)
        w0, w1, w2 = cw_ref[0:1, :], cw_ref[1:2, :], cw_ref[2:3, :]
        gce = cb_ref[...] + gprev * w0 + ge * w1 + gnext * w2
        gel, dgel = _gelu_and_grad(gce)
        dae = de * gel
        dgce = de * ae * dgel
        dge = dgce * w1 + pltpu.roll(dgce, te - 1, 0) * w0 + pltpu.roll(dgce, 1, 0) * w2
        mid = slice(HALO, HALO + tm)
        dup_ref[:, 0:D_FF] = dae[mid].astype(BF16)
        dup_ref[:, D_FF:2 * D_FF] = dge[mid].astype(BF16)
        dgc = dgce[mid]
        acc_ref[0:1, :] += jnp.sum(dgc * gprev[mid], axis=0, keepdims=True)
        acc_ref[1:2, :] += jnp.sum(dgc * ge[mid], axis=0, keepdims=True)
        acc_ref[2:3, :] += jnp.sum(dgc * gnext[mid], axis=0, keepdims=True)
        acc_ref[3:4, :] += jnp.sum(dgc, axis=0, keepdims=True)

    last = l // HALO - 1

    def trio(cb):
        return [pl.BlockSpec((tm, D_FF), lambda i: (i, cb)),
                pl.BlockSpec((HALO, D_FF), lambda i: (jnp.maximum(i * hb - 1, 0), cb)),
                pl.BlockSpec((HALO, D_FF), lambda i: (jnp.minimum((i + 1) * hb, last), cb))]

    return pl.pallas_call(
        body, name="convglu_bwd", grid=(nt,),
        in_specs=trio(0) + trio(1) + trio(0) + [_full((3, D_FF)), _full((1, D_FF))],
        out_specs=[pl.BlockSpec((tm, 2 * D_FF), lambda i: (i, 0)), _full((8, D_FF))],
        out_shape=[jax.ShapeDtypeStruct((l, 2 * D_FF), BF16), jax.ShapeDtypeStruct((8, D_FF), F32)],
        compiler_params=_params(("arbitrary",)),
    )(up, up, up, up, up, up, dact, dact, dact, conv_w, conv_b)


def _up_bwd(dup, w_up_b, w_out_b, x1, dx2, mix, mod3, n2w, y_all, y_ret, p_all, w_glu_b, b_glu, zero_rows, nct, pairs,
            kinds):
    l = x1.shape[0]
    tm = TOK_TILE
    nt = l // tm
    n = len(pairs)
    shapes = _rs_slot_shapes(pairs, kinds)
    n_out = 8

    def body(dup_ref, wu_ref, wo_ref, x1_ref, dx2_ref, mix_ref, mod_ref, nw_ref, y_ref, yr_ref, g_ref, wg_ref, bg_ref,
             zero_rows_ref, *rest):
        dx1_ref, dmixb_ref, acc_ref, dys_ref, dyr_ref, dg_ref, gw_ref, gb_ref = rest[n:n + n_out]
        send_sems, recv_sems, dy_scr = rest[2 * n + n_out:]
        step = pl.program_id(0)

        @pl.when(step == 0)
        def _():
            acc_ref[...] = jnp.zeros_like(acc_ref)
            gw_ref[...] = jnp.zeros_like(gw_ref)
            gb_ref[...] = jnp.zeros_like(gb_ref)

        _behind(step, nt - 1, functools.partial(_rs_chip_copies, rest[:n], rest[n + n_out:2 * n + n_out], kinds,
                                                shapes, send_sems, recv_sems))

        dh2 = _dot_nt(dup_ref[...], wu_ref[...])
        _, vjp = jax.vjp(_rms_mod, x1_ref[...], nw_ref[...], mod_ref[1:2, :], mod_ref[2:3, :])
        dx, dnw, dsh, dsc = vjp(dh2)
        dx1 = dx + dx2_ref[...]
        dx1_ref[...] = dx1
        dmixb = (dx1 * mod_ref[0:1, :]).astype(BF16)
        dmixb_ref[...] = dmixb
        dmix = _dot_nt(dmixb, wo_ref[...])
        acc_ref[0:1, :] += dnw
        acc_ref[1:2, :] += jnp.sum(dx1 * mix_ref[...], axis=0, keepdims=True)
        acc_ref[2:3, :] += dsh
        acc_ref[3:4, :] += dsc

        yg, dgel = _gelu_and_grad(y_ref[...])
        ygb = yg.astype(BF16)
        sg = _sigmoid(_dot(ygb, wg_ref[...]) + bg_ref[...])
        ds = dmix[:, 0:S5_WIDTH]
        dz = ds * yg * sg * (1.0 - sg)
        dzb = dz.astype(BF16)
        _s5_put_rows(dys_ref, dy_scr, (ds * sg + _dot_nt(dzb, wg_ref[...])) * dgel)
        gw_ref[...] += _dot_tn(ygb, dzb)
        gb_ref[...] += jnp.sum(dz, axis=0, keepdims=True)

        for h in range(RET_HEADS):
            sl = slice(RET_DH * h, RET_DH * (h + 1))
            _, hvjp = jax.vjp(_head_norm_gate, yr_ref[:, sl], g_ref[:, sl])
            dyr, dg = hvjp(dmix[:, S5_WIDTH + RET_DH * h:S5_WIDTH + RET_DH * (h + 1)])
            dyr_ref[:, sl] = dyr
            dg_ref[:, sl] = dg

    row = pl.BlockSpec((tm, D_MODEL), lambda i: (i, 0))
    half = pl.BlockSpec((tm, S5_WIDTH), lambda i: (i, 0))
    f32h = jax.ShapeDtypeStruct((l, RET_WIDTH), F32)
    return pl.pallas_call(
        body, name="up_bwd", grid=(nt,),
        in_specs=[pl.BlockSpec((tm, 2 * D_FF), lambda i: (i, 0)), _full((D_MODEL, 2 * D_FF)),
                  _full((D_MODEL, D_MODEL)), row, row, row, _full((3, D_MODEL)), _full((1, D_MODEL)),
                  pl.BlockSpec((tm, S5_WIDTH), lambda i: (i + nct, 0)), half,
                  pl.BlockSpec((tm, RET_WIDTH), lambda i: (i + nct, G_COL // 4)),
                  _full((S5_WIDTH, S5_WIDTH)), _full((1, S5_WIDTH)), ANY] + [ANY] * n,
        out_specs=[row, row, _full((8, D_MODEL)),
                   pl.BlockSpec((S5_NB, tm // S5_T, S5_BW), lambda i: (0, i + nct, 0)), half, half,
                   _full((S5_WIDTH, S5_WIDTH)),
                   _full((1, S5_WIDTH))] + [ANY] * n,
        out_shape=[jax.ShapeDtypeStruct((l, D_MODEL), F32), jax.ShapeDtypeStruct((l, D_MODEL), BF16),
                   jax.ShapeDtypeStruct((8, D_MODEL), F32), jax.ShapeDtypeStruct(zero_rows.shape, BF16), f32h, f32h,
                   jax.ShapeDtypeStruct((S5_WIDTH, S5_WIDTH), F32), jax.ShapeDtypeStruct((1, S5_WIDTH), F32)]
        + [jax.ShapeDtypeStruct((4,) + s, p.dtype) for s, p in zip(shapes, pairs)],
        input_output_aliases={13: 3},
        scratch_shapes=[pltpu.SemaphoreType.DMA((n, 3)), pltpu.SemaphoreType.DMA((n, 3)),
                        pltpu.VMEM((tm // S5_T, S5_T, S5_WIDTH), F32)],
        compiler_params=_params(("arbitrary",)),
    )(dup, w_up_b, w_out_b, x1, dx2, mix, mod3, n2w, y_all, y_ret, p_all, w_glu_b, b_glu, zero_rows, *pairs)


MOD_ROWS = 16
MOD_COLS = 6 * D_MODEL // 4


def _mod_fwd(c_all, c_ctx, w_mod_b, b_loc):
    def body(c_ref, cc_ref, w_ref, b_ref, m_ref, s_ref):
        cond = jnp.concatenate([c_ref[...], jnp.broadcast_to(cc_ref[...], (8, D_MODEL))], axis=0)
        s = _silu(cond).astype(BF16)
        s_ref[...] = s
        m_ref[...] = _dot(s, w_ref[...]) + b_ref[...]

    return pl.pallas_call(
        body, name="mod_fwd",
        out_shape=[jax.ShapeDtypeStruct((MOD_ROWS, MOD_COLS), F32), jax.ShapeDtypeStruct((MOD_ROWS, D_MODEL), BF16)],
        compiler_params=_params(),
    )(c_all, c_ctx, w_mod_b, b_loc)


def _mod_bwd_sum(dm_all):
    def body(d_ref, dm_ref, gb_ref):
        rows = [d_ref[k, 0:1, :] for k in range(8)]
        ctx_sum = d_ref[0, 1:2, :]
        for k in range(1, 8):
            ctx_sum = ctx_sum + d_ref[k, 1:2, :]
        gb = ctx_sum
        for k in range(8):
            gb = gb + rows[k]
        gb_ref[...] = gb
        dm_ref[...] = jnp.concatenate(rows + [ctx_sum] + [jnp.zeros((7, 6 * D_MODEL), F32)], axis=0)

    return pl.pallas_call(
        body, name="mod_bwd_sum",
        out_shape=[jax.ShapeDtypeStruct((MOD_ROWS, 6 * D_MODEL), F32), jax.ShapeDtypeStruct((1, 6 * D_MODEL), F32)],
        compiler_params=_params(),
    )(dm_all)


def _mod_bwd_w(dm_loc, s_b, c_ctx, w_mod_b):
    def body(d_ref, s_ref, cc_ref, w_ref, gw_ref, gc_ref):
        db = d_ref[...].astype(BF16)
        gw_ref[...] = _dot_tn(s_ref[...], db)
        ds = _dot_nt(db, w_ref[...])
        _, vjp = jax.vjp(_silu, cc_ref[...])
        gc_ref[...] = jnp.broadcast_to(vjp(ds[8:9, :])[0], (8, D_MODEL))

    return pl.pallas_call(
        body, name="mod_bwd_w",
        out_shape=[jax.ShapeDtypeStruct((D_MODEL, MOD_COLS), F32), jax.ShapeDtypeStruct((8, D_MODEL), F32)],
        compiler_params=_params(),
    )(dm_loc, s_b, c_ctx, w_mod_b)


def _adamw(w, g, m, v, name):
    r, c = w.shape
    tr = _pick(r, (256, 128, 64, 32, 16, 8))
    bc1 = 1.0 - ADAM_B1 ** ADAM_STEP
    bc2 = 1.0 - ADAM_B2 ** ADAM_STEP

    def body(w_ref, g_ref, m_ref, v_ref, d_ref, nm_ref, nv_ref):
        gg = g_ref[...]
        nm = ADAM_B1 * m_ref[...] + (1.0 - ADAM_B1) * gg
        nv = ADAM_B2 * v_ref[...] + (1.0 - ADAM_B2) * (gg * gg)
        nm_ref[...] = nm
        nv_ref[...] = nv
        d_ref[...] = -ADAM_LR * ((nm / bc1) / (jnp.sqrt(nv / bc2) + ADAM_EPS) + ADAM_WD * w_ref[...])

    blk = pl.BlockSpec((tr, c), lambda i: (i, 0))
    shp = jax.ShapeDtypeStruct((r, c), F32)
    return pl.pallas_call(
        body, name=name, grid=(r // tr,), in_specs=[blk] * 4, out_specs=[blk] * 3, out_shape=[shp] * 3,
        compiler_params=_params(("parallel",)),
    )(w, g, m, v)


def _sum_slots(a, name):
    n, r, c = a.shape
    tr = _pick(r, (376, 256, 208, 128, 64, 32, 16, 8))

    def body(a_ref, o_ref):
        acc = a_ref[0].astype(F32)
        for k in range(1, n):
            acc = acc + a_ref[k].astype(F32)
        o_ref[...] = acc

    return pl.pallas_call(
        body, name=name, grid=(r // tr,),
        in_specs=[pl.BlockSpec((n, tr, c), lambda i: (0, i, 0))],
        out_specs=pl.BlockSpec((tr, c), lambda i: (i, 0)),
        out_shape=jax.ShapeDtypeStruct((r, c), F32),
        compiler_params=_params(("parallel",)),
    )(a)


def _mesh_pos():
    return lax.axis_index("x"), lax.axis_index("y"), lax.axis_index("c")


def _all_gather8(v, name):
    m_per, n = v.shape

    def body(x_ref, out_ref, send_sems, recv_sems, local_sem):
        x, y, c = _mesh_pos()
        me, sibling = (x, y, c), (x, y, 1 - c)
        chips = [(1 - x, y), (x, 1 - y), (1 - x, 1 - y)]

        def rows(px, py, pc):
            return out_ref.at[pl.ds((4 * px + 2 * py + pc) * m_per, m_per), :]

        def copy(k, block, to, src=None):
            return pltpu.make_async_remote_copy(
                src_ref=rows(*block) if src is None else src, dst_ref=rows(*block),
                send_sem=send_sems.at[k], recv_sem=recv_sems.at[k], device_id=to, device_id_type=MESH_ID)

        mine = pltpu.make_async_copy(x_ref, rows(*me), local_sem)
        mine.start()
        first = [copy(0, me, sibling, src=x_ref)]
        first += [copy(1 + j, me, (*chip, c), src=x_ref) for j, chip in enumerate(chips)]
        for cp in first:
            cp.start()
        passed = [copy(4 + j, (*chip, c), sibling) for j, chip in enumerate(chips)]
        for j, chip in enumerate(chips):
            copy(1 + j, (*chip, c), me).wait_recv()
            passed[j].start()
        copy(0, sibling, me).wait_recv()
        for j, chip in enumerate(chips):
            copy(4 + j, (*chip, 1 - c), me).wait_recv()
        for cp in first + passed:
            cp.wait_send()
        mine.wait()

    return pl.pallas_call(
        body, name=name,
        out_shape=jax.ShapeDtypeStruct((8 * m_per, n), v.dtype),
        in_specs=[pl.BlockSpec(memory_space=pltpu.VMEM)],
        out_specs=pl.BlockSpec(memory_space=pltpu.VMEM),
        scratch_shapes=[pltpu.SemaphoreType.DMA((7,)), pltpu.SemaphoreType.DMA((7,)), pltpu.SemaphoreType.DMA],
        compiler_params=_params(),
    )(v)


ANY = pl.BlockSpec(memory_space=pl.ANY)
def PEER_CHIPS(x, y):
    return [(x, 1 - y), (1 - x, y), (1 - x, 1 - y)]


def _shard_region(ref, kind, k, rl, cl, r0, nr, c0, nc):
    if kind == "col":
        return ref.at[pl.ds(r0, nr), pl.ds(k * cl + c0, nc)]
    return ref.at[pl.ds(k * rl + r0, nr), pl.ds(c0, nc)]


def _place_shard(w, kind, chip, name):
    rl, cl = w.shape
    tr = _pick(rl, (256, 128, 64))
    nt = rl // tr

    def body(chip_ref, w_ref, o_ref):
        o_ref[...] = w_ref[...].astype(BF16)

    o_map = (lambda i, chip_ref: (i, chip_ref[0])) if kind == "col" else (lambda i, chip_ref: (chip_ref[0] * nt + i, 0))
    return pl.pallas_call(
        body, name=name,
        grid_spec=pltpu.PrefetchScalarGridSpec(
            num_scalar_prefetch=1, grid=(nt,),
            in_specs=[pl.BlockSpec((tr, cl), lambda i, chip_ref: (i, 0))], out_specs=pl.BlockSpec((tr, cl), o_map)),
        out_shape=jax.ShapeDtypeStruct((rl, 4 * cl) if kind == "col" else (4 * rl, cl), BF16),
        compiler_params=_params(("parallel",)),
    )(chip.reshape(1), w)


def _gather_shard_shapes(placed, kinds):
    return [(p.shape[0], p.shape[1] // 4) if k == "col" else (p.shape[0] // 4, p.shape[1]) for p, k in zip(placed, kinds)]


def _gather_chip_copies(outs, kinds, shard_shapes, send_sems, recv_sems, with_arrivals=True):
    x, y, c = _mesh_pos()
    me = 2 * x + y
    sends, arrivals = [], []
    for a in range(len(outs)):
        rl, cl = shard_shapes[a]
        rh = rl // 2
        reg = functools.partial(_shard_region, outs[a], kinds[a], rl=rl, cl=cl, r0=c * rh, nr=rh, c0=0, nc=cl)
        for j, (px, py) in enumerate(PEER_CHIPS(x, y)):
            to = dict(send_sem=send_sems.at[a, j], recv_sem=recv_sems.at[a, j], device_id=(px, py, c),
                      device_id_type=MESH_ID)
            sends.append(pltpu.make_async_remote_copy(src_ref=reg(k=me), dst_ref=reg(k=me), **to))
            if with_arrivals:
                got = reg(k=2 * px + py)
                arrivals.append(pltpu.make_async_remote_copy(src_ref=got, dst_ref=got, **to))
    return sends, arrivals


def _gather_sibling_copies(outs, kinds, shard_shapes, send_sems, recv_sems):
    x, y, c = _mesh_pos()
    forwards, arrivals = [], []
    for a in range(len(outs)):
        rl, cl = shard_shapes[a]
        rh = rl // 2
        for j, (px, py) in enumerate(PEER_CHIPS(x, y)):
            to = dict(send_sem=send_sems.at[a, j], recv_sem=recv_sems.at[a, j], device_id=(x, y, 1 - c),
                      device_id_type=MESH_ID)
            reg = functools.partial(_shard_region, outs[a], kinds[a], k=2 * px + py, rl=rl, cl=cl, nr=rh, c0=0, nc=cl)
            forwards.append(pltpu.make_async_remote_copy(src_ref=reg(r0=c * rh), dst_ref=reg(r0=c * rh), **to))
            arrivals.append(pltpu.make_async_remote_copy(src_ref=reg(r0=(1 - c) * rh), dst_ref=reg(r0=(1 - c) * rh), **to))
    return forwards, arrivals


def _gather_sibling(placed, kinds, name):
    n = len(placed)
    shard_shapes = _gather_shard_shapes(placed, kinds)

    def body(*refs):
        forwards, from_sibling = _gather_sibling_copies(refs[n:2 * n], kinds, shard_shapes, *refs[2 * n:])
        for cp in forwards:
            cp.start()
        for cp in from_sibling:
            cp.wait_recv()
        for cp in forwards:
            cp.wait_send()

    return pl.pallas_call(
        body, name=name,
        out_shape=[jax.ShapeDtypeStruct(p.shape, p.dtype) for p in placed],
        in_specs=[ANY] * n, out_specs=[ANY] * n, input_output_aliases={a: a for a in range(n)},
        scratch_shapes=[pltpu.SemaphoreType.DMA((n, 3))] * 2,
        compiler_params=_params(),
    )(*placed)


def _half(kind, r, c):
    return (r // 2, c) if kind == "col" else (r, c // 2)


def _half_of(ref, kind, which):
    r, c = ref.shape
    hr, hc = _half(kind, r, c)
    return ref.at[pl.ds(which * hr, hr), :] if kind == "col" else ref.at[:, pl.ds(which * hc, hc)]


def _rs_sibling(grads, kinds, name):
    n = len(grads)

    def body(*refs):
        srcs, dsts = refs[:n], refs[n:2 * n]
        send_sems, recv_sems = refs[2 * n:]
        x, y, c = _mesh_pos()
        cps = [pltpu.make_async_remote_copy(src_ref=_half_of(srcs[a], kinds[a], 1 - c), dst_ref=dsts[a],
                                            send_sem=send_sems.at[a], recv_sem=recv_sems.at[a],
                                            device_id=(x, y, 1 - c), device_id_type=MESH_ID) for a in range(n)]
        for cp in cps:
            cp.start()
        for cp in cps:
            cp.wait()

    return pl.pallas_call(
        body, name=name,
        out_shape=[jax.ShapeDtypeStruct(_half(k, *g.shape), g.dtype) for g, k in zip(grads, kinds)],
        in_specs=[ANY] * n, out_specs=[ANY] * n,
        scratch_shapes=[pltpu.SemaphoreType.DMA((n,)), pltpu.SemaphoreType.DMA((n,))],
        compiler_params=_params(),
    )(*grads)


def _pair_sum(gf, rv, kind, ci, name):
    r, c = rv.shape
    tr = _pick(r, (128, 64, 32, 16, 8))
    nt = r // tr

    def body(ci_ref, g_ref, r_ref, o_ref):
        o_ref[...] = (g_ref[...] + r_ref[...]).astype(BF16)

    g_map = (lambda i, ci_ref: (ci_ref[0] * nt + i, 0)) if kind == "col" else (lambda i, ci_ref: (i, ci_ref[0]))
    blk = pl.BlockSpec((tr, c), lambda i, ci_ref: (i, 0))
    return pl.pallas_call(
        body, name=name,
        grid_spec=pltpu.PrefetchScalarGridSpec(num_scalar_prefetch=1, grid=(nt,),
                                               in_specs=[pl.BlockSpec((tr, c), g_map), blk], out_specs=blk),
        out_shape=jax.ShapeDtypeStruct((r, c), BF16),
        compiler_params=_params(("parallel",)),
    )(ci.reshape(1), gf, rv)


def _rs_slot_shapes(pairs, kinds):
    return [(p.shape[0], p.shape[1] // 4) if k == "col" else (p.shape[0] // 4, p.shape[1]) for p, k in zip(pairs, kinds)]


def _rs_chip_copies(srcs, dsts, kinds, shapes, send_sems, recv_sems, with_arrivals=True):
    x, y, c = _mesh_pos()
    me = 2 * x + y
    sends, arrivals = [], []
    for a in range(len(srcs)):
        rl, cl = shapes[a]
        reg = functools.partial(_shard_region, srcs[a], kinds[a], rl=rl, cl=cl, r0=0, nr=rl, c0=0, nc=cl)
        for j, (px, py) in enumerate(PEER_CHIPS(x, y)):
            to = dict(send_sem=send_sems.at[a, j], recv_sem=recv_sems.at[a, j], device_id=(px, py, c),
                      device_id_type=MESH_ID)
            sends.append(pltpu.make_async_remote_copy(src_ref=reg(k=2 * px + py), dst_ref=dsts[a].at[me], **to))
            if with_arrivals:
                slot = dsts[a].at[2 * px + py]
                arrivals.append(pltpu.make_async_remote_copy(src_ref=slot, dst_ref=slot, **to))
    return sends, arrivals


def _rs_chips(pairs, kinds):
    n = len(pairs)
    shapes = _rs_slot_shapes(pairs, kinds)

    def body(*refs):
        sends, arrivals = _rs_chip_copies(refs[:n], refs[n:2 * n], kinds, shapes, *refs[2 * n:])
        for cp in sends:
            cp.start()
        for cp in arrivals:
            cp.wait_recv()
        for cp in sends:
            cp.wait_send()

    return pl.pallas_call(
        body, name="rs_chips",
        out_shape=[jax.ShapeDtypeStruct((4,) + s, p.dtype) for s, p in zip(shapes, pairs)],
        in_specs=[ANY] * n, out_specs=[ANY] * n,
        scratch_shapes=[pltpu.SemaphoreType.DMA((n, 3)), pltpu.SemaphoreType.DMA((n, 3))],
        compiler_params=_params(),
    )(*pairs)


def _sum_chips(pair, got, kind, pos, name):
    _, r, c = got.shape
    tr = _pick(r, (256, 128, 64, 32, 16))
    nt = r // tr

    def body(pos_ref, own_ref, g1_ref, g2_ref, g3_ref, o_ref):
        o_ref[...] = ((own_ref[...].astype(F32) + g1_ref[0].astype(F32)) + g2_ref[0].astype(F32)) + g3_ref[0].astype(F32)

    if kind == "col":
        own_map = lambda i, p: (i, p[1])
        out_map = lambda i, p: (p[0] * nt + i, 0)
        out_shape = (2 * r, c)
    else:
        own_map = lambda i, p: (p[1] * nt + i, 0)
        out_map = lambda i, p: (i, p[0])
        out_shape = (r, 2 * c)
    peer = lambda m: pl.BlockSpec((1, tr, c), lambda i, p: (p[1] ^ m, i, 0))
    return pl.pallas_call(
        body, name=name,
        grid_spec=pltpu.PrefetchScalarGridSpec(
            num_scalar_prefetch=1, grid=(nt,),
            in_specs=[pl.BlockSpec((tr, c), own_map), peer(1), peer(2), peer(3)],
            out_specs=pl.BlockSpec((tr, c), out_map)),
        out_shape=jax.ShapeDtypeStruct(out_shape, F32),
        compiler_params=_params(("parallel",)),
    )(pos, pair, got, got, got)


def _rs_back(halves, kinds):
    n = len(halves)

    def body(*refs):
        outs = refs[n:2 * n]
        send_sems, recv_sems = refs[2 * n:]
        x, y, c = _mesh_pos()
        cps = []
        for a in range(n):
            mine = _half_of(outs[a], kinds[a], c)
            cps.append(pltpu.make_async_remote_copy(src_ref=mine, dst_ref=mine, send_sem=send_sems.at[a],
                                                    recv_sem=recv_sems.at[a], device_id=(x, y, 1 - c),
                                                    device_id_type=MESH_ID))
            cps[-1].start()
        for a in range(n):
            other = _half_of(outs[a], kinds[a], 1 - c)
            pltpu.make_async_remote_copy(src_ref=other, dst_ref=other, send_sem=send_sems.at[a],
                                         recv_sem=recv_sems.at[a], device_id=(x, y, 1 - c),
                                         device_id_type=MESH_ID).wait_recv()
        for cp in cps:
            cp.wait_send()

    return pl.pallas_call(
        body, name="rs_back",
        out_shape=[jax.ShapeDtypeStruct(h.shape, h.dtype) for h in halves],
        in_specs=[ANY] * n, out_specs=[ANY] * n, input_output_aliases={a: a for a in range(n)},
        scratch_shapes=[pltpu.SemaphoreType.DMA((n,)), pltpu.SemaphoreType.DMA((n,))],
        compiler_params=_params(),
    )(*halves)


def _rope_tables(l, lc):
    rows = l // GRID_W
    n_freq = RET_DH // 4
    inv_freq = ROPE_THETA ** (-jnp.arange(n_freq, dtype=F32) / n_freq)
    sign = jnp.tile(jnp.array([-1.0, 1.0], F32), n_freq)

    def half(n):
        ang = jnp.repeat(jnp.arange(n, dtype=F32)[:, None] * inv_freq, 2, axis=-1)
        return jnp.cos(ang), jnp.sin(ang) * sign

    (cr, sr), (cc, sc) = half(rows), half(GRID_W)
    grid = lambda r, c: jnp.concatenate([jnp.repeat(r, GRID_W, axis=0), jnp.tile(c, (rows, 1))], axis=-1)
    cos_t = jnp.concatenate([jnp.ones((lc, RET_DH), F32), grid(cr, cc)], axis=0)
    sin_t = jnp.concatenate([jnp.zeros((lc, RET_DH), F32), grid(sr, sc)], axis=0)
    return cos_t, sin_t


def _s5_pack(a):
    blk = lambda t: t.reshape(1, S5_NB, 128, S5_STATE)
    lre = jnp.stack([a["s5_lambda_re_f"][0], a["s5_lambda_re_b"][0]]).reshape(2, S5_NB, 8, S5_STATE)
    lim = jnp.stack([a["s5_lambda_im_f"][0], a["s5_lambda_im_b"][0]]).reshape(2, S5_NB, 8, S5_STATE)
    lst = jnp.stack([a["s5_log_step_f"][0], a["s5_log_step_b"][0]]).reshape(2, S5_NB, 8, 1)
    b_re = blk(a["s5_b_re"][0].transpose(0, 2, 1))
    b_im = blk(a["s5_b_im"][0].transpose(0, 2, 1))
    return (lre, lim, lst, b_re, b_im, blk(a["s5_c_re"][0]), blk(a["s5_c_im"][0]),
            a["s5_d"].reshape(1, S5_NB, 1, 128))


def _s5_unpack(g):
    glre, glim, glst, gbre, gbim, gcre, gcim, gd = g
    unb = lambda t: t.reshape(S5_GROUPS, S5_GROUP, S5_STATE).transpose(0, 2, 1)[None]
    return {
        "s5_lambda_re_f": glre[0].reshape(1, S5_GROUPS, S5_STATE), "s5_lambda_re_b": glre[1].reshape(1, S5_GROUPS, S5_STATE),
        "s5_lambda_im_f": glim[0].reshape(1, S5_GROUPS, S5_STATE), "s5_lambda_im_b": glim[1].reshape(1, S5_GROUPS, S5_STATE),
        "s5_log_step_f": glst[0].reshape(1, S5_GROUPS), "s5_log_step_b": glst[1].reshape(1, S5_GROUPS),
        "s5_b_re": unb(gbre), "s5_b_im": unb(gbim),
        "s5_c_re": gcre.reshape(1, S5_GROUPS, S5_GROUP, S5_STATE), "s5_c_im": gcim.reshape(1, S5_GROUPS, S5_GROUP, S5_STATE),
        "s5_d": gd.reshape(1, S5_WIDTH),
    }


def _local_step(a, early, late, mx, mc, conv_w, ci):
    x, ctx, tgt = a["x"][0], a["ctx"][0], a["loss_target"][0]
    l, lc = x.shape[0], ctx.shape[0]
    la = l + lc
    nct, ncc, nrc, cn = lc // TOK_TILE, lc // S5_T, lc // RET_CHUNK, la // S5_T
    n1w, n2w, fnw = a["norm1_w"], a["norm2_w"], a["final_norm_w"].reshape(1, D_MODEL)
    conv_b, b_glu = a["conv_b"], a["s5_b_glu"]
    ld2 = jnp.concatenate([a["ret_log_decay_f"], a["ret_log_decay_b"]], axis=0)
    mod4 = jnp.concatenate([mc[0:2], mx[0:2]], axis=0)
    mod3 = mx[2:5]
    gate5 = mx[5:6]
    cos_t, sin_t = _rope_tables(l, lc)
    s5p = _s5_pack(a)

    gg, xw, yw, a16, *early = _s5_gen(*s5p, early, EARLY_KINDS)
    wb = dict(zip(EARLY_NAMES, _gather_sibling(early, EARLY_KINDS, "gather_sibling_early")))
    p_all, h1b, p3, w_up_p = _norm_inproj(x, ctx, n1w, mod4, wb["w_in"], [late[1]], (LATE_KINDS[1],))
    sloc = _s5_to_states(p3, xw, "s5_state")
    a16s = a16.transpose(1, 0, 2, 3).reshape(2, S5_GROUPS, 128)
    hs, w_out_p, w_down_p = _s5_scan(sloc, a16s, ncc, [late[0], late[2]], (LATE_KINDS[0], LATE_KINDS[2]))
    y_all = _s5_from_states(p3, gg, hs, yw, False, "s5_out").reshape(la, S5_WIDTH)
    tabs = _ret_tables(ld2)
    of, ob, ssf, ssb = _ret_scan(p_all, cos_t, sin_t, tabs, nrc)
    wb = {**wb, **dict(zip(LATE_NAMES, _gather_sibling([w_out_p, w_up_p, w_down_p], LATE_KINDS, "gather_sibling_late")))}
    x1, mix, h2b, up, mixb, y_ret = _outproj_up(x, y_all, of, ob, p_all, wb["s5_w_glu"], b_glu, wb["w_out"],
                                                     mod3, n2w, wb["w_up"], nct)
    act, dx2, ddn, dact, acc_f = _ffn_loss(up, x1, conv_w, conv_b, wb["w_down"], gate5, fnw, tgt)

    g = {}
    g["w_down"] = _mm_tn(act, ddn, name="gw_down")
    dup, acc_c = _convglu_bwd(up, dact, conv_w, conv_b)
    g["w_up"] = _mm_tn(h2b, dup, name="gw_up")
    first = [g[n] for n in FIRST_GRADS]
    first_pairs = [_pair_sum(gf, rv, k, ci, "rs_pair_" + n)
                   for gf, rv, k, n in zip(first, _rs_sibling(first, FIRST_KINDS, "rs_sibling_first"), FIRST_KINDS, FIRST_GRADS)]
    dx1, dmixb, acc_2, dy3, dy_ret, dg, g["s5_w_glu"], g["s5_b_glu"], *first_got = _up_bwd(
        dup, wb["w_up"], wb["w_out"], x1, dx2, mix, mod3, n2w, y_all, y_ret, p_all, wb["s5_w_glu"], b_glu,
        jnp.zeros(p3.shape, BF16), nct, first_pairs, FIRST_KINDS)
    g["w_out"] = _mm_tn(mixb, dmixb, name="gw_out")

    e = _s5_to_states(dy3, yw, "s5_bwd_h")
    ds, da16 = _s5_scan_bwd(e, hs, a16s, ncc)
    du = _s5_from_states(dy3, gg, ds, xw, True, "s5_bwd_u").reshape(la, S5_WIDTH)
    dkb = _s5_bwd_kb(p3, dy3)
    dwst = _s5_bwd_w(p3, ds, "s5_bwd_wst")
    dwout = _s5_bwd_w(dy3, hs, "s5_bwd_wout")
    da16p = da16.reshape(2, S5_NB, 8, 128).transpose(1, 0, 2, 3)
    g.update(_s5_unpack(_s5_gen_bwd(*s5p, dkb, dwst, dwout, da16p)))

    dqf, dkf, dvf, dqb, dkb_, dvb, dld = _ret_scan_bwd(p_all, cos_t, sin_t, tabs, ssf, ssb, dy_ret, nrc)
    g["ret_log_decay_f"] = dld[0, :, 0, 0].reshape(1, RET_HEADS)
    g["ret_log_decay_b"] = dld[1, :, 0, 0].reshape(1, RET_HEADS)
    dp, grad_x, acc_1 = _in_bwd(dqf, dkf, dvf, dqb, dkb_, dvb, du, dg, cos_t, sin_t, wb["w_in"], x, ctx, n1w, mod4, dx1)
    g["w_in"] = _mm_tn(h1b, dp, name="gw_in")

    g["norm1_w"], g["norm2_w"], g["final_norm_w"] = acc_1[0:1], acc_2[0:1], acc_f[0]
    g["conv_w"], g["conv_b"] = acc_c[0:3], acc_c[3:4]
    zero = jnp.zeros((1, D_MODEL), F32)
    dmx = jnp.concatenate([acc_1[3:5], acc_2[1:2], acc_2[2:4], acc_f[1:2]], axis=0)
    dmc = jnp.concatenate([acc_1[1:3], zero, zero, zero, zero], axis=0)
    return acc_f[2, 0], grad_x, g, dmx, dmc, first_pairs, first_got


WEIGHT_NAMES = ("c_ctx", "w_mod", "b_mod", "norm1_w", "w_in", "s5_lambda_re_f", "s5_lambda_im_f", "s5_log_step_f",
                "s5_lambda_re_b", "s5_lambda_im_b", "s5_log_step_b", "s5_b_re", "s5_b_im", "s5_c_re", "s5_c_im",
                "s5_d", "s5_w_glu", "s5_b_glu", "ret_log_decay_f", "ret_log_decay_b", "w_out", "norm2_w", "w_up",
                "conv_w", "conv_b", "w_down", "final_norm_w")
BIG_NAMES = ("w_in", "w_out", "w_up", "w_down", "s5_w_glu")
BIG_KINDS = ("col", "row", "col", "row", "row")
EARLY_NAMES, EARLY_KINDS = ("w_in", "s5_w_glu"), ("col", "row")
LATE_NAMES, LATE_KINDS = ("w_out", "w_up", "w_down"), ("row", "col", "row")
FIRST_GRADS, FIRST_KINDS = ("w_down", "w_up"), ("row", "col")
LAST_GRADS, LAST_KINDS = ("w_in", "w_out", "s5_w_glu"), ("col", "row", "row")
SMALL_NAMES = ("norm1_w", "norm2_w", "final_norm_w", "conv_b", "conv_w", "s5_lambda_re_f", "s5_lambda_im_f",
               "s5_log_step_f", "s5_lambda_re_b", "s5_lambda_im_b", "s5_log_step_b", "s5_b_re", "s5_b_im", "s5_c_re",
               "s5_c_im", "s5_d", "s5_b_glu", "ret_log_decay_f", "ret_log_decay_b")
ROW = 1024
N_CHIPS = 4


def _pack_rows(parts):
    flat = jnp.concatenate([p.reshape(-1) for p in parts])
    n = flat.shape[0]
    rows = -(-n // (8 * ROW)) * 8
    return jnp.pad(flat, (0, rows * ROW - n)).reshape(rows, ROW)


def _unpack_rows(packed, shapes):
    flat = packed.reshape(-1)
    out, off = [], 0
    for s in shapes:
        n = math.prod(s)
        out.append(flat[off:off + n].reshape(s))
        off += n
    return out


def _step(a):
    xi, yi, ci = _mesh_pos()
    chip = 2 * xi + yi
    dev = 2 * chip + ci

    cw_loc = a["conv_w"].reshape(-1)
    small_in = jnp.concatenate([a["c"].reshape(-1), jnp.pad(cw_loc, (0, 24 * 128 - cw_loc.shape[0]))]).reshape(32, 128)
    sg = _all_gather8(small_in, "gather_cond").reshape(8, 32, 128)
    c_all = sg[:, 0:8].reshape(8, D_MODEL)
    conv_w = sg[0::2, 8:32].reshape(N_CHIPS, -1)[:, :cw_loc.shape[0]].reshape(N_CHIPS, 3, -1)
    conv_w = conv_w.transpose(1, 0, 2).reshape(3, D_FF)

    placed = {n: _place_shard(a[n][0], k, chip, "place_" + n) for n, k in zip(BIG_NAMES, BIG_KINDS)}
    early = [placed[n] for n in EARLY_NAMES]
    late = [placed[n] for n in LATE_NAMES]

    w_mod_b = a["w_mod"][0].astype(BF16)
    c_ctx = a["c_ctx"].reshape(1, D_MODEL)
    b_loc = lax.dynamic_slice_in_dim(a["b_mod"], chip * MOD_COLS, MOD_COLS, 1)
    m_loc, s_b = _mod_fwd(c_all, c_ctx, w_mod_b, b_loc)
    mg = _all_gather8(m_loc, "gather_mod").reshape(8, MOD_ROWS, MOD_COLS)
    m_full = mg[0::2].transpose(1, 0, 2).reshape(MOD_ROWS, 6 * D_MODEL)
    mx = lax.dynamic_slice_in_dim(m_full, dev, 1, 0).reshape(6, D_MODEL)
    mc = m_full[8].reshape(6, D_MODEL)

    loss_part, grad_x, g, dmx, dmc, first_pairs, first_got = _local_step(a, early, late, mx, mc, conv_w, ci)
    loss = lax.psum(loss_part, ("x", "y", "c"))

    dm_pair = jnp.concatenate([dmx.reshape(1, -1), dmc.reshape(1, -1), jnp.zeros((6, 6 * D_MODEL), F32)], axis=0)
    dm_all = _all_gather8(dm_pair, "gather_dmod").reshape(8, 8, 6 * D_MODEL)
    dm16, gb_mod = _mod_bwd_sum(dm_all)
    dm_loc = lax.dynamic_slice_in_dim(dm16, chip * MOD_COLS, MOD_COLS, 1)
    gw_mod, gcc = _mod_bwd_w(dm_loc, s_b, c_ctx, w_mod_b)

    small_parts = [g[n] for n in SMALL_NAMES] + [gcc[0]]
    small_shapes = [p.shape for p in small_parts]
    sp = _pack_rows(small_parts)
    tot = _sum_slots(_all_gather8(sp, "gather_small_grads").reshape(8, sp.shape[0], ROW), "sum_small_grads")
    small = dict(zip(SMALL_NAMES + ("c_ctx",), _unpack_rows(tot, small_shapes)))
    grads = {n: small[n].reshape(a[n].shape) for n in SMALL_NAMES if n != "conv_w"}
    grads["c_ctx"] = (0.5 * small["c_ctx"]).reshape(a["c_ctx"].shape)
    grads["conv_w"] = lax.dynamic_slice_in_dim(small["conv_w"], chip * (D_FF // N_CHIPS), D_FF // N_CHIPS, 1)[None]
    grads["b_mod"] = gb_mod
    grads["w_mod"] = gw_mod[None]

    last = [g[n] for n in LAST_GRADS]
    last_pairs = [_pair_sum(gf, rv, k, ci, "rs_pair_" + n)
                  for gf, rv, k, n in zip(last, _rs_sibling(last, LAST_KINDS, "rs_sibling_last"), LAST_KINDS, LAST_GRADS)]
    last_got = _rs_chips(last_pairs, LAST_KINDS)
    pos = jnp.stack([ci, chip])
    order = FIRST_GRADS + LAST_GRADS
    order_kinds = FIRST_KINDS + LAST_KINDS
    halves = [_sum_chips(p, t, k, pos, "rs_sum_" + n)
              for p, t, k, n in zip(first_pairs + last_pairs, list(first_got) + list(last_got), order_kinds, order)]
    for n, t in zip(order, _rs_back(halves, order_kinds)):
        grads[n] = t[None]

    delta, new_m, new_v = {}, {}, {}
    for n in BIG_NAMES + ("w_mod",):
        for dst, t in zip((delta, new_m, new_v), _adamw(a[n][0], grads[n][0], a["m_" + n][0], a["v_" + n][0], "adamw_" + n)):
            dst[n] = t[None]
    rest = [n for n in WEIGHT_NAMES if n not in BIG_NAMES and n != "w_mod"]
    shapes = [a[n].shape for n in rest]
    pr = lambda pre: _pack_rows([a[pre + n] for n in rest])
    for dst, t in zip((delta, new_m, new_v),
                      _adamw(pr(""), _pack_rows([grads[n] for n in rest]), pr("m_"), pr("v_"), "adamw_small")):
        dst.update(zip(rest, _unpack_rows(t, shapes)))

    return (loss, grad_x[None], *[grads[n] for n in WEIGHT_NAMES], *[delta[n] for n in WEIGHT_NAMES],
            *[new_m[n] for n in WEIGHT_NAMES], *[new_v[n] for n in WEIGHT_NAMES])


def kernel(x, c, ctx, c_ctx, w_mod, b_mod, norm1_w, w_in, s5_lambda_re_f, s5_lambda_im_f, s5_log_step_f, s5_lambda_re_b, s5_lambda_im_b, s5_log_step_b, s5_b_re, s5_b_im, s5_c_re, s5_c_im, s5_d, s5_w_glu, s5_b_glu, ret_log_decay_f, ret_log_decay_b, w_out, norm2_w, w_up, conv_w, conv_b, w_down, final_norm_w, loss_target, m_c_ctx, m_w_mod, m_b_mod, m_norm1_w, m_w_in, m_s5_lambda_re_f, m_s5_lambda_im_f, m_s5_log_step_f, m_s5_lambda_re_b, m_s5_lambda_im_b, m_s5_log_step_b, m_s5_b_re, m_s5_b_im, m_s5_c_re, m_s5_c_im, m_s5_d, m_s5_w_glu, m_s5_b_glu, m_ret_log_decay_f, m_ret_log_decay_b, m_w_out, m_norm2_w, m_w_up, m_conv_w, m_conv_b, m_w_down, m_final_norm_w, v_c_ctx, v_w_mod, v_b_mod, v_norm1_w, v_w_in, v_s5_lambda_re_f, v_s5_lambda_im_f, v_s5_log_step_f, v_s5_lambda_re_b, v_s5_lambda_im_b, v_s5_log_step_b, v_s5_b_re, v_s5_b_im, v_s5_c_re, v_s5_c_im, v_s5_d, v_s5_w_glu, v_s5_b_glu, v_ret_log_decay_f, v_ret_log_decay_b, v_w_out, v_norm2_w, v_w_up, v_conv_w, v_conv_b, v_w_down, v_final_norm_w):
    return _step(dict(locals()))
```

```python
import functools
import math

import jax
import jax.numpy as jnp
from jax import lax
from jax.experimental import pallas as pl
from jax.experimental.pallas import tpu as pltpu

F32 = jnp.float32
BF16 = jnp.bfloat16

D_MODEL = 1024
S5_WIDTH = 512
S5_GROUPS = 32
S5_GROUP = 16
S5_STATE = 64
RET_WIDTH = 512
RET_HEADS = 4
RET_DH = 128
RET_CHUNK = 256
GRID_W = 64
ROPE_THETA = 10000.0
D_FF = 2816
NORM_EPS = 1e-6
IN_COLS = S5_WIDTH + 4 * RET_WIDTH

S5_T = 16
S5_NB = 4
S5_BW = S5_T * 128
S5_SW = 8 * 2 * S5_STATE

ADAM_LR, ADAM_B1, ADAM_B2, ADAM_EPS, ADAM_WD, ADAM_STEP = 0.001, 0.9, 0.999, 1e-08, 0.01, 10

VMEM_LIMIT = 56 * 1024 * 1024
MM_TN_VMEM = 40 * 1024 * 1024
MESH_ID = pl.DeviceIdType.MESH


def _params(sem=None):
    return pltpu.CompilerParams(dimension_semantics=sem, vmem_limit_bytes=VMEM_LIMIT)


def _full(shape):
    n = len(shape)
    return pl.BlockSpec(shape, lambda *_: (0,) * n)


def _dot(a, b):
    return jnp.dot(a, b, preferred_element_type=F32)


def _dot_nt(a, b):
    return lax.dot_general(a, b, (((1,), (1,)), ((), ())), preferred_element_type=F32)


def _dot_tn(a, b):
    return lax.dot_general(a, b, (((0,), (0,)), ((), ())), preferred_element_type=F32)


def _dot_hi(a, b):
    return jnp.dot(a, b, preferred_element_type=F32, precision=lax.Precision.HIGHEST)


def _dot_nt_hi(a, b):
    return lax.dot_general(a, b, (((1,), (1,)), ((), ())), preferred_element_type=F32,
                           precision=lax.Precision.HIGHEST)


def _gelu(x):
    return 0.5 * x * (1.0 + jnp.tanh(0.7978845608028654 * (x + 0.044715 * (x * x * x))))


def _gelu_and_grad(x):
    c, ca = 0.7978845608028654, 0.7978845608028654 * 0.044715
    x2 = x * x
    t = jnp.tanh(x * (c + ca * x2))
    h = 0.5 * x
    return h + h * t, 0.5 + 0.5 * t + h * (1.0 - t * t) * (c + 3.0 * ca * x2)


def _sigmoid(x):
    return 1.0 / (1.0 + jnp.exp(-x))


def _silu(x):
    return x * _sigmoid(x)


def _rms_mod(x, nw, sh, sc):
    r = lax.rsqrt(jnp.mean(x * x, axis=-1, keepdims=True) + NORM_EPS)
    return (x * r * nw) * (1.0 + sc) + sh


def _rms(x, nw):
    r = lax.rsqrt(jnp.mean(x * x, axis=-1, keepdims=True) + NORM_EPS)
    return x * r * nw


def _head_norm_gate(y, g):
    mu = jnp.mean(y, axis=-1, keepdims=True)
    yc = y - mu
    var = jnp.mean(yc * yc, axis=-1, keepdims=True)
    return _silu(g) * (yc * lax.rsqrt(var + NORM_EPS))


def _swap_pairs(t):
    lane = lax.broadcasted_iota(jnp.int32, t.shape, 1)
    return jnp.where(lane % 2 == 0, pltpu.roll(t, RET_DH - 1, 1), pltpu.roll(t, 1, 1))


def _rope(t, cos_t, sin_t):
    return t * cos_t + _swap_pairs(t) * sin_t


def _rope_t(dt, cos_t, sin_t):
    return dt * cos_t + _swap_pairs(dt * sin_t)


def _pick(n, prefs):
    for p in prefs:
        if n % p == 0:
            return p
    return n


def _mm_tn(a, b, *, name):
    m, k = a.shape
    n = b.shape[1]
    tn = _pick(n, (1408, 1024, 1280, 512))
    fits = lambda t: 2 * (2 * t * k + 2 * t * tn + 4 * k * tn) <= MM_TN_VMEM
    tm = _pick(m, [t for t in (2816, 2048, 1024, 768, 512, 256) if fits(t)] + [128])

    def body(a_ref, b_ref, o_ref):
        @pl.when(pl.program_id(1) == 0)
        def _():
            o_ref[...] = jnp.zeros_like(o_ref)
        o_ref[...] += _dot_tn(a_ref[...], b_ref[...])

    return pl.pallas_call(
        body, name=name, grid=(n // tn, m // tm),
        in_specs=[pl.BlockSpec((tm, k), lambda j, i: (i, 0)), pl.BlockSpec((tm, tn), lambda j, i: (i, j))],
        out_specs=pl.BlockSpec((k, tn), lambda j, i: (0, j)),
        out_shape=jax.ShapeDtypeStruct((k, n), F32),
        compiler_params=_params(("parallel", "arbitrary")),
    )(a, b)


TOK_TILE = 256


def _behind(step, last, copies):
    @pl.when(step == 0)
    def _():
        for cp in copies(with_arrivals=False)[0]:
            cp.start()

    @pl.when(step == last)
    def _():
        sends, arrivals = copies()
        for cp in arrivals:
            cp.wait_recv()
        for cp in sends:
            cp.wait_send()


def _s5_put_rows(rows_ref, scr, val):
    nchunk = scr.shape[0]
    for c in range(nchunk):
        scr[c] = val[S5_T * c:S5_T * (c + 1), :]
    for b in range(S5_NB):
        for j in range(S5_T):
            rows_ref[b, :, 128 * j:128 * (j + 1)] = scr[:, j, 128 * b:128 * (b + 1)].astype(BF16)


def _norm_inproj(x, ctx, n1w, mod4, w_in_b, placed, kinds):
    l, lc = x.shape[0], ctx.shape[0]
    tm = TOK_TILE
    nct = lc // tm
    la = l + lc
    n = len(placed)
    shard_shapes = _gather_shard_shapes(placed, kinds)

    def body(x_ref, c_ref, nw_ref, mod_ref, w_ref, *rest):
        p_ref, h_ref, u_ref = rest[n:n + 3]
        send_sems, recv_sems, u_scr = rest[2 * n + 3:]
        _behind(pl.program_id(0), la // tm - 1,
                functools.partial(_gather_chip_copies, rest[n + 3:2 * n + 3], kinds, shard_shapes, send_sems, recv_sems))
        is_ctx = pl.program_id(0) < nct
        xt = jnp.where(is_ctx, c_ref[...], x_ref[...])
        sh = jnp.where(is_ctx, mod_ref[0:1, :], mod_ref[2:3, :])
        sc = jnp.where(is_ctx, mod_ref[1:2, :], mod_ref[3:4, :])
        hb = _rms_mod(xt, nw_ref[...], sh, sc).astype(BF16)
        h_ref[...] = hb
        p = _dot(hb, w_ref[...])
        p_ref[...] = p
        _s5_put_rows(u_ref, u_scr, p[:, 0:S5_WIDTH])

    return pl.pallas_call(
        body, name="norm_inproj", grid=(la // tm,),
        in_specs=[pl.BlockSpec((tm, D_MODEL), lambda i: (jnp.maximum(i - nct, 0), 0)),
                  pl.BlockSpec((tm, D_MODEL), lambda i: (jnp.minimum(i, nct - 1), 0)),
                  _full((1, D_MODEL)), _full((4, D_MODEL)), _full((D_MODEL, IN_COLS))] + [ANY] * n,
        out_specs=[pl.BlockSpec((tm, IN_COLS), lambda i: (i, 0)), pl.BlockSpec((tm, D_MODEL), lambda i: (i, 0)),
                   pl.BlockSpec((S5_NB, tm // S5_T, S5_BW), lambda i: (0, i, 0))] + [ANY] * n,
        out_shape=[jax.ShapeDtypeStruct((la, IN_COLS), F32), jax.ShapeDtypeStruct((la, D_MODEL), BF16),
                   jax.ShapeDtypeStruct((S5_NB, la // S5_T, S5_BW), BF16)]
        + [jax.ShapeDtypeStruct(p.shape, p.dtype) for p in placed],
        input_output_aliases={5 + a: 3 + a for a in range(n)},
        scratch_shapes=[pltpu.SemaphoreType.DMA((n, 3)), pltpu.SemaphoreType.DMA((n, 3)),
                        pltpu.VMEM((tm // S5_T, S5_T, S5_WIDTH), F32)],
        compiler_params=_params(("arbitrary",)),
    )(x, ctx, n1w, mod4, w_in_b, *placed)


def _iota2(shape, dim):
    return lax.broadcasted_iota(jnp.int32, shape, dim)


def _group_mask(rows, cols, row_div, col_div):
    return jnp.where(_iota2((rows, cols), 0) // row_div == _iota2((rows, cols), 1) // col_div, 1.0, 0.0).astype(F32)


def _s5_gen_dir(lre, lim, lst, b_re, b_im, c_re, c_im):
    step = jnp.exp(lst)
    mag = jnp.exp(lre * step)
    ar = mag * jnp.cos(lim * step)
    ai = mag * jnp.sin(lim * step)
    den = lre * lre + lim * lim
    xr = ar - 1.0
    cr = (xr * lre + ai * lim) / den
    ci = (ai * lre - xr * lim) / den
    rexp = _group_mask(128, 8, S5_GROUP, 1)
    are, aie = _dot_hi(rexp, ar), _dot_hi(rexp, ai)
    cre, cie = _dot_hi(rexp, cr), _dot_hi(rexp, ci)
    bbr = cre * b_re - cie * b_im
    bbi = cre * b_im + cie * b_re
    gmask = _group_mask(128, 128, S5_GROUP, S5_GROUP)
    pr, pi = jnp.ones_like(are), jnp.zeros_like(are)
    xs, ys = [], []
    for t in range(S5_T + 1):
        if t < S5_T:
            xs.append(jnp.concatenate([bbr * pr - bbi * pi, bbr * pi + bbi * pr], axis=1))
        ys.append(jnp.concatenate([c_re * pr - c_im * pi, -(c_re * pi + c_im * pr)], axis=1))
        pr, pi = pr * are - pi * aie, pr * aie + pi * are
    gs = [_dot_nt_hi(x_t, ys[0]) * gmask for x_t in xs]
    r16, i16 = ar, ai
    for _ in range(4):
        r16, i16 = r16 * r16 - i16 * i16, 2.0 * r16 * i16
    return xs, ys, gs, jnp.concatenate([r16, i16], axis=1)


def _s5_expand(z):
    return jnp.concatenate([z] * 8, axis=1) * _group_mask(128, S5_SW, S5_GROUP, 128)


def _s5_contract(z):
    zm = z * _group_mask(128, S5_SW, S5_GROUP, 128)
    acc = zm[:, 0:128]
    for k in range(1, 8):
        acc = acc + zm[:, 128 * k:128 * (k + 1)]
    return acc


def _s5_param_specs():
    blk3 = lambda r, c: pl.BlockSpec((1, 1, r, c), lambda b, *_: (0, b, 0, 0))
    dir3 = lambda r, c: pl.BlockSpec((2, 1, r, c), lambda b, *_: (0, b, 0, 0))
    return [dir3(8, S5_STATE), dir3(8, S5_STATE), dir3(8, 1), blk3(128, S5_STATE), blk3(128, S5_STATE),
            blk3(128, S5_STATE), blk3(128, S5_STATE), blk3(1, 128)]


def _s5_gen(lre, lim, lst, b_re, b_im, c_re, c_im, dvec, placed, kinds):
    n = len(placed)
    shard_shapes = _gather_shard_shapes(placed, kinds)

    def body(lre_ref, lim_ref, lst_ref, bre_ref, bim_ref, cre_ref, cim_ref, d_ref, *rest):
        gg_ref, xw_ref, yw_ref, a16_ref = rest[n:n + 4]
        _behind(pl.program_id(0), S5_NB - 1,
                functools.partial(_gather_chip_copies, rest[n + 4:2 * n + 4], kinds, shard_shapes, *rest[2 * n + 4:]))
        eye = _group_mask(128, 128, 1, 1)
        g0 = eye * d_ref[0, 0]
        for dr in range(2):
            xs, ys, gs, a16 = _s5_gen_dir(lre_ref[dr, 0], lim_ref[dr, 0], lst_ref[dr, 0], bre_ref[0, 0],
                                          bim_ref[0, 0], cre_ref[0, 0], cim_ref[0, 0])
            a16_ref[0, dr] = a16
            for j in range(S5_T):
                xw_ref[0, dr, j] = xs[S5_T - 1 - j if dr == 0 else j]
                yw_ref[0, dr, j] = ys[j + 1 if dr == 0 else S5_T - j]
            g0 = g0 + gs[0]
            for t in range(1, S5_T):
                gg_ref[0, (S5_T - 1) + t if dr == 0 else (S5_T - 1) - t] = gs[t]
        gg_ref[0, S5_T - 1] = g0

    blk = pl.BlockSpec((1, 2, S5_T, 128, 128), lambda b: (b, 0, 0, 0, 0))
    return pl.pallas_call(
        body, name="s5_gen", grid=(S5_NB,),
        in_specs=_s5_param_specs() + [ANY] * n,
        out_specs=[pl.BlockSpec((1, 2 * S5_T - 1, 128, 128), lambda b: (b, 0, 0, 0)), blk, blk,
                   pl.BlockSpec((1, 2, 8, 128), lambda b: (b, 0, 0, 0))] + [ANY] * n,
        out_shape=[jax.ShapeDtypeStruct((S5_NB, 2 * S5_T - 1, 128, 128), F32),
                   jax.ShapeDtypeStruct((S5_NB, 2, S5_T, 128, 128), F32),
                   jax.ShapeDtypeStruct((S5_NB, 2, S5_T, 128, 128), F32),
                   jax.ShapeDtypeStruct((S5_NB, 2, 8, 128), F32)]
        + [jax.ShapeDtypeStruct(p.shape, p.dtype) for p in placed],
        input_output_aliases={8 + a: 4 + a for a in range(n)},
        scratch_shapes=[pltpu.SemaphoreType.DMA((n, 3)), pltpu.SemaphoreType.DMA((n, 3))],
        compiler_params=_params(("arbitrary",)),
    )(lre, lim, lst, b_re, b_im, c_re, c_im, dvec, *placed)


def _s5_fill_state_mat(w_scr, src_ref, dr):
    for j in range(S5_T):
        w_scr[128 * j:128 * (j + 1), :] = _s5_expand(src_ref[0, dr, j]).astype(BF16)


def _s5_fill_toeplitz(k_scr, gg_ref):
    for j in range(S5_T):
        for i in range(S5_T):
            k_scr[128 * j:128 * (j + 1), 128 * i:128 * (i + 1)] = gg_ref[0, i - j + (S5_T - 1)].astype(BF16)


S5_GEN_SPECS = [pl.BlockSpec((1, 2 * S5_T - 1, 128, 128), lambda b: (b, 0, 0, 0)),
                pl.BlockSpec((1, 2, S5_T, 128, 128), lambda b: (b, 0, 0, 0, 0))]


def _s5_gen_bwd(lre, lim, lst, b_re, b_im, c_re, c_im, dvec, dg, dx, dy, da16):
    def body(lre_ref, lim_ref, lst_ref, bre_ref, bim_ref, cre_ref, cim_ref, d_ref, dg_ref, dx_ref, dy_ref, da16_ref,
             glre_ref, glim_ref, glst_ref, gbre_ref, gbim_ref, gcre_ref, gcim_ref, gd_ref):
        eye = _group_mask(128, 128, 1, 1)
        gd_ref[0, 0] = jnp.sum(dg_ref[0, S5_T - 1] * eye, axis=0, keepdims=True)
        gb = [None, None, None, None]
        for dr in range(2):
            args = (lre_ref[dr, 0], lim_ref[dr, 0], lst_ref[dr, 0], bre_ref[0, 0], bim_ref[0, 0],
                    cre_ref[0, 0], cim_ref[0, 0])
            _, vjp = jax.vjp(_s5_gen_dir, *args)
            dxs = [dx_ref[0, dr, S5_T - 1 - t if dr == 0 else t] for t in range(S5_T)]
            dys = [jnp.zeros((128, 128), F32)] + [dy_ref[0, dr, t - 1 if dr == 0 else S5_T - t]
                                                  for t in range(1, S5_T + 1)]
            dgs = [dg_ref[0, (S5_T - 1) + t if dr == 0 else (S5_T - 1) - t] for t in range(S5_T)]
            g = vjp((dxs, dys, dgs, da16_ref[0, dr]))
            glre_ref[dr, 0] = g[0]
            glim_ref[dr, 0] = g[1]
            glst_ref[dr, 0] = g[2]
            for q in range(4):
                gb[q] = g[3 + q] if gb[q] is None else gb[q] + g[3 + q]
        gbre_ref[0, 0] = gb[0]
        gbim_ref[0, 0] = gb[1]
        gcre_ref[0, 0] = gb[2]
        gcim_ref[0, 0] = gb[3]

    shp = lambda a: jax.ShapeDtypeStruct(a.shape, F32)
    return pl.pallas_call(
        body, name="s5_gen_bwd", grid=(S5_NB,),
        in_specs=_s5_param_specs() + [
            pl.BlockSpec((1, 2 * S5_T - 1, 128, 128), lambda b: (b, 0, 0, 0)),
            pl.BlockSpec((1, 2, S5_T, 128, 128), lambda b: (b, 0, 0, 0, 0)),
            pl.BlockSpec((1, 2, S5_T, 128, 128), lambda b: (b, 0, 0, 0, 0)),
            pl.BlockSpec((1, 2, 8, 128), lambda b: (b, 0, 0, 0))],
        out_specs=_s5_param_specs(),
        out_shape=[shp(lre), shp(lim), shp(lst), shp(b_re), shp(b_im), shp(c_re), shp(c_im), shp(dvec)],
        compiler_params=_params(("parallel",)),
    )(lre, lim, lst, b_re, b_im, c_re, c_im, dvec, dg, dx, dy, da16)


def _s5_put_groups(o_ref, dr, val):
    for gi in range(8):
        o_ref[dr, :, gi, :] = val[:, 128 * gi:128 * (gi + 1)]


def _s5_get_groups(s_ref, dr, n=8):
    return jnp.concatenate([s_ref[dr, :, gi, :] for gi in range(n)], axis=1).astype(BF16)


def _s5_to_states(u3, blocks, name):
    cn = u3.shape[1]

    def body(u_ref, b_ref, o_ref, w_scr):
        u = u_ref[0]
        for dr in range(2):
            _s5_fill_state_mat(w_scr, b_ref, dr)
            _s5_put_groups(o_ref, dr, _dot(u, w_scr[...]))

    return pl.pallas_call(
        body, name=name, grid=(S5_NB,),
        in_specs=[pl.BlockSpec((1, cn, S5_BW), lambda b: (b, 0, 0)), S5_GEN_SPECS[1]],
        out_specs=pl.BlockSpec((2, cn, 8, 128), lambda b: (0, 0, b, 0)),
        out_shape=jax.ShapeDtypeStruct((2, cn, S5_GROUPS, 128), F32),
        scratch_shapes=[pltpu.VMEM((S5_BW, S5_SW), BF16)],
        compiler_params=_params(("parallel",)),
    )(u3, blocks)


def _s5_from_states(u3, gg, st, blocks, transposed, name):
    cn = u3.shape[1]

    def body(u_ref, g_ref, s_ref, b_ref, o_ref, k_scr, w_scr):
        u = u_ref[0]
        _s5_fill_toeplitz(k_scr, g_ref)
        y = _dot_nt(u, k_scr[...]) if transposed else _dot(u, k_scr[...])
        for dr in range(2):
            _s5_fill_state_mat(w_scr, b_ref, dr)
            y = y + _dot_nt(_s5_get_groups(s_ref, dr), w_scr[...])
        for i in range(S5_T):
            o_ref[:, i, :] = y[:, 128 * i:128 * (i + 1)]

    return pl.pallas_call(
        body, name=name, grid=(S5_NB,),
        in_specs=[pl.BlockSpec((1, cn, S5_BW), lambda b: (b, 0, 0)), S5_GEN_SPECS[0],
                  pl.BlockSpec((2, cn, 8, 128), lambda b: (0, 0, b, 0)), S5_GEN_SPECS[1]],
        out_specs=pl.BlockSpec((cn, S5_T, 128), lambda b: (0, 0, b)),
        out_shape=jax.ShapeDtypeStruct((cn, S5_T, S5_WIDTH), F32),
        scratch_shapes=[pltpu.VMEM((S5_BW, S5_BW), BF16), pltpu.VMEM((S5_BW, S5_SW), BF16)],
        compiler_params=_params(("parallel",)),
    )(u3, gg, st, blocks)


def _s5_a_forms(a):
    ra = pltpu.roll(a, S5_STATE, 1)
    low = _iota2(a.shape, 1) < S5_STATE
    return jnp.where(low, a, ra), jnp.where(low, -ra, a)


def _s5_scan(sloc, a16, ncc, placed, kinds):
    cn = sloc.shape[1]
    n = len(placed)
    shard_shapes = _gather_shard_shapes(placed, kinds)

    def body(s_ref, a_ref, *rest):
        h_ref = rest[n]
        sends, arrivals = _gather_chip_copies(rest[n + 1:2 * n + 1], kinds, shard_shapes, *rest[2 * n + 1:])
        for cp in sends:
            cp.start()
        forms = [_s5_a_forms(a_ref[dr]) for dr in range(2)]

        def step(s, hs):
            out = []
            for dr in range(2):
                arr, aii = forms[dr]
                h, rh = hs[dr]
                c = s if dr == 0 else jnp.where(s < ncc, ncc - 1 - s, cn - 1 - (s - ncc))
                h_ref[dr, c] = h
                sc = s_ref[dr, c]
                out.append((h * arr + rh * aii + sc, rh * arr - h * aii + pltpu.roll(sc, S5_STATE, 1)))
            return tuple(out)

        zero = jnp.zeros((S5_GROUPS, 128), F32)
        lax.fori_loop(0, cn, step, ((zero, zero), (zero, zero)), unroll=4)
        for cp in arrivals:
            cp.wait_recv()
        for cp in sends:
            cp.wait_send()

    vmem = pl.BlockSpec(memory_space=pltpu.VMEM)
    return pl.pallas_call(
        body, name="s5_scan",
        in_specs=[vmem, vmem] + [ANY] * n, out_specs=[vmem] + [ANY] * n,
        out_shape=[jax.ShapeDtypeStruct(sloc.shape, F32)] + [jax.ShapeDtypeStruct(p.shape, p.dtype) for p in placed],
        input_output_aliases={2 + a: 1 + a for a in range(n)},
        scratch_shapes=[pltpu.SemaphoreType.DMA((n, 3)), pltpu.SemaphoreType.DMA((n, 3))],
        compiler_params=_params(),
    )(sloc, a16, *placed)


def _s5_scan_bwd(e, hs, a16, ncc):
    cn = e.shape[1]

    def body(e_ref, h_ref, a_ref, ds_ref, da_ref):
        forms = [_s5_a_forms(a_ref[dr]) for dr in range(2)]
        low = _iota2((S5_GROUPS, 128), 1) < S5_STATE

        def step(s, carry):
            out = []
            r = cn - 1 - s
            for dr in range(2):
                arr, aii = forms[dr]
                g, rg, da = carry[dr]
                c = r if dr == 0 else jnp.where(r < ncc, ncc - 1 - r, cn - 1 - (r - ncc))
                ds_ref[dr, c] = g
                h = h_ref[dr, c]
                rh = pltpu.roll(h, S5_STATE, 1)
                da = da + jnp.where(low, g * h + rg * rh, g * rh - rg * h)
                ec = e_ref[dr, c]
                out.append((ec + g * arr - rg * aii, pltpu.roll(ec, S5_STATE, 1) + rg * arr + g * aii, da))
            return tuple(out)

        zero = jnp.zeros((S5_GROUPS, 128), F32)
        res = lax.fori_loop(0, cn, step, ((zero, zero, zero), (zero, zero, zero)), unroll=4)
        da_ref[0] = res[0][2]
        da_ref[1] = res[1][2]

    return pl.pallas_call(
        body, name="s5_scan_bwd",
        out_shape=[jax.ShapeDtypeStruct(e.shape, F32), jax.ShapeDtypeStruct((2, S5_GROUPS, 128), F32)],
        compiler_params=_params(),
    )(e, hs, a16)


def _s5_bwd_kb(p3, dy3):
    cn = p3.shape[1]
    half = S5_T // 2

    def body(u_ref, d_ref, o_ref):
        q = pl.program_id(1)

        @pl.when(q == 0)
        def _():
            o_ref[...] = jnp.zeros_like(o_ref)

        dk = _dot_tn(u_ref[0], d_ref[0])
        for j in range(S5_T):
            for i in range(half):
                o_ref[0, half * q + i - j + (S5_T - 1)] += dk[128 * j:128 * (j + 1), 128 * i:128 * (i + 1)]

    return pl.pallas_call(
        body, name="s5_bwd_kb", grid=(S5_NB, 2),
        in_specs=[pl.BlockSpec((1, cn, S5_BW), lambda b, q: (b, 0, 0)),
                  pl.BlockSpec((1, cn, S5_BW // 2), lambda b, q: (b, 0, q))],
        out_specs=pl.BlockSpec((1, 2 * S5_T - 1, 128, 128), lambda b, q: (b, 0, 0, 0)),
        out_shape=jax.ShapeDtypeStruct((S5_NB, 2 * S5_T - 1, 128, 128), F32),
        compiler_params=_params(("parallel", "arbitrary")),
    )(p3, dy3)


def _s5_bwd_w(u3, st, name):
    cn = u3.shape[1]

    def body(u_ref, s_ref, w_ref):
        dw = _dot_tn(u_ref[0], _s5_get_groups(s_ref, 0))
        for j in range(S5_T):
            w_ref[0, 0, j] = _s5_contract(dw[128 * j:128 * (j + 1), :])

    return pl.pallas_call(
        body, name=name, grid=(S5_NB, 2),
        in_specs=[pl.BlockSpec((1, cn, S5_BW), lambda b, q: (b, 0, 0)),
                  pl.BlockSpec((1, cn, 8, 128), lambda b, q: (q, 0, b, 0))],
        out_specs=pl.BlockSpec((1, 1, S5_T, 128, 128), lambda b, q: (b, q, 0, 0, 0)),
        out_shape=jax.ShapeDtypeStruct((S5_NB, 2, S5_T, 128, 128), F32),
        compiler_params=_params(("parallel", "parallel")),
    )(u3, st)


K_SCALE = RET_DH ** -0.5
G_COL = 16


def _ret_chunk_of(step, ncc, nch, rev):
    if not rev:
        return step
    return jnp.where(step < ncc, ncc - 1 - step, nch - 1 - (step - ncc))


def _ret_decay(ld, rev):
    c = _iota2((RET_CHUNK, RET_CHUNK), 0).astype(F32)
    m = _iota2((RET_CHUNK, RET_CHUNK), 1).astype(F32)
    diff = (m - c) if rev else (c - m)
    keep = (diff > 0) if rev else (diff >= 0)
    expo = jnp.maximum(diff, 0.0)
    dm = jnp.where(keep, jnp.exp(ld * expo), 0.0)
    xi_e = (RET_CHUNK - c) if rev else (c + 1.0)
    zeta_e = c if rev else (RET_CHUNK - 1.0 - c)
    return dm, expo, jnp.exp(ld * xi_e), xi_e, jnp.exp(ld * zeta_e), zeta_e


RET_TABLES = 7


def _ret_tables(ld2):
    def body(ld_ref, t_ref):
        dr, h = pl.program_id(0), pl.program_id(1)
        ldh = ld_ref[dr, h]
        for rev in (False, True):
            @pl.when(dr == int(rev))
            def _(rev=rev):
                dm, expo, xi, xi_e, zeta, zeta_e = _ret_decay(ldh, rev)
                t_ref[0, 0, 0] = dm
                t_ref[0, 0, 1] = dm * expo
                t_ref[0, 0, 2] = xi
                t_ref[0, 0, 3] = xi * xi_e
                t_ref[0, 0, 4] = zeta
                t_ref[0, 0, 5] = zeta * zeta_e
                t_ref[0, 0, 6] = jnp.zeros_like(dm) + jnp.exp(ldh * RET_CHUNK)

    return pl.pallas_call(
        body, name="ret_tables", grid=(2, RET_HEADS),
        in_specs=[pl.BlockSpec(memory_space=pltpu.SMEM)],
        out_specs=pl.BlockSpec((1, 1, RET_TABLES, RET_CHUNK, RET_CHUNK), lambda d, h: (d, h, 0, 0, 0)),
        out_shape=jax.ShapeDtypeStruct((2, RET_HEADS, RET_TABLES, RET_CHUNK, RET_CHUNK), F32),
        compiler_params=_params(("parallel", "parallel")),
    )(ld2)


def _ret_specs(nch, ncc, rev, step_of):
    chunk = lambda n: _ret_chunk_of(step_of(n), ncc, nch, rev)
    cols = [pl.BlockSpec((RET_CHUNK, RET_WIDTH), functools.partial(lambda n, cb: (chunk(n), cb), cb=cb))
            for cb in (1, 2, 3)]
    tab = pl.BlockSpec((RET_CHUNK, RET_DH), lambda n: (chunk(n), 0))
    return cols + [tab, tab], pl.BlockSpec((RET_CHUNK, RET_WIDTH), lambda n: (chunk(n), 0))


def _ret_scan(p_all, cos_t, sin_t, tabs, ncc):
    la = p_all.shape[0]
    nch = la // RET_CHUNK

    def body(t_ref, qf, kf, vf, cf, sf, qb, kb, vb, cb, sb, of_ref, ob_ref, ssf_ref, ssb_ref, s_scr):
        @pl.when(pl.program_id(0) == 0)
        def _():
            s_scr[...] = jnp.zeros_like(s_scr)

        for dr, (q_ref, k_ref, v_ref, c_ref, n_ref, o_ref, ss_ref) in enumerate(
                ((qf, kf, vf, cf, sf, of_ref, ssf_ref), (qb, kb, vb, cb, sb, ob_ref, ssb_ref))):
            cs, sn = c_ref[...], n_ref[...]
            for h in range(RET_HEADS):
                sl = slice(RET_DH * h, RET_DH * (h + 1))
                dm, xi, zeta = t_ref[dr, h, 0], t_ref[dr, h, 2, :, 0:RET_DH], t_ref[dr, h, 4, :, 0:RET_DH]
                q = _rope(q_ref[:, sl], cs, sn)
                k = _rope(k_ref[:, sl] * K_SCALE, cs, sn)
                vh = v_ref[:, sl].astype(BF16)
                s = s_scr[dr, h]
                ss_ref[0, h] = s
                sc = (_dot_nt(q.astype(BF16), k.astype(BF16)) * dm).astype(BF16)
                o_ref[:, sl] = _dot(sc, vh) + _dot((q * xi).astype(BF16), s.astype(BF16))
                s_scr[dr, h] = t_ref[dr, h, 6, 0:RET_DH, 0:RET_DH] * s + _dot_tn((k * zeta).astype(BF16), vh)

    in_f, out_f = _ret_specs(nch, ncc, False, lambda n: n)
    in_b, out_b = _ret_specs(nch, ncc, True, lambda n: n)
    ss_spec = pl.BlockSpec((1, RET_HEADS, RET_DH, RET_DH), lambda n: (n, 0, 0, 0))
    o_shape = jax.ShapeDtypeStruct((la, RET_WIDTH), F32)
    ss_shape = jax.ShapeDtypeStruct((nch, RET_HEADS, RET_DH, RET_DH), F32)
    return pl.pallas_call(
        body, name="ret_scan", grid=(nch,),
        in_specs=[_full(tabs.shape)] + in_f + in_b,
        out_specs=[out_f, out_b, ss_spec, ss_spec],
        out_shape=[o_shape, o_shape, ss_shape, ss_shape],
        scratch_shapes=[pltpu.VMEM((2, RET_HEADS, RET_DH, RET_DH), F32)],
        compiler_params=_params(("arbitrary",)),
    )(tabs, p_all, p_all, p_all, cos_t, sin_t, p_all, p_all, p_all, cos_t, sin_t)


def _ret_scan_bwd(p_all, cos_t, sin_t, tabs, ssf, ssb, dy_all, ncc):
    la = p_all.shape[0]
    nch = la // RET_CHUNK

    def body(t_ref, qf, kf, vf, cf, sf, dof, ssf_ref, qb, kb, vb, cb, sb, dob_, ssb_ref,
             dqf, dkf, dvf, dqb, dkb, dvb, dld_ref, ds_scr):
        @pl.when(pl.program_id(0) == 0)
        def _():
            ds_scr[...] = jnp.zeros_like(ds_scr)
            dld_ref[...] = jnp.zeros_like(dld_ref)

        for dr, (q_ref, k_ref, v_ref, c_ref, n_ref, do_ref, ss_ref, dq_ref, dk_ref, dv_ref) in enumerate(
                ((qf, kf, vf, cf, sf, dof, ssf_ref, dqf, dkf, dvf), (qb, kb, vb, cb, sb, dob_, ssb_ref, dqb, dkb, dvb))):
            cs, sn = c_ref[...], n_ref[...]
            on_ctx = _ret_chunk_of(nch - 1 - pl.program_id(0), ncc, nch, dr == 1) < ncc
            for h in range(RET_HEADS):
                sl = slice(RET_DH * h, RET_DH * (h + 1))
                dm, dm_d = t_ref[dr, h, 0], t_ref[dr, h, 1]
                xi, xi_d, zeta, zeta_d = [t_ref[dr, h, t, :, 0:RET_DH] for t in (2, 3, 4, 5)]
                gc = t_ref[dr, h, 6, 0:RET_DH, 0:RET_DH]
                q = _rope(q_ref[:, sl], cs, sn)
                k = _rope(k_ref[:, sl] * K_SCALE, cs, sn)
                q16, k16, v16 = q.astype(BF16), k.astype(BF16), v_ref[:, sl].astype(BF16)
                s = ss_ref[0, h]
                s16 = s.astype(BF16)
                ds_in = ds_scr[dr, h]
                ds16 = ds_in.astype(BF16)
                do16 = jnp.where(on_ctx, 0.0, do_ref[:, sl]).astype(BF16)
                qk = _dot_nt(q16, k16)
                dsv = _dot_nt(do16, v16)
                dsc = (dsv * dm).astype(BF16)
                sc16 = (qk * dm).astype(BF16)
                dos = _dot_nt(do16, s16)
                vds = _dot_nt(v16, ds16)
                dq_ref[:, sl] = _dot(dsc, k16) + dos * xi
                dk_ref[:, sl] = _dot_tn(dsc, q16) + vds * zeta
                dv_ref[:, sl] = _dot_tn(sc16, do16) + _dot((k * zeta).astype(BF16), ds16)
                ds_scr[dr, h] = _dot_tn((q * xi).astype(BF16), do16) + gc * ds_in
                dld = (jnp.sum(dsv * qk * dm_d) + jnp.sum(q * dos * xi_d + k * vds * zeta_d)
                       + RET_CHUNK * jnp.sum(gc * s * ds_in))
                dld_ref[dr, h] += dld

    back = lambda n: nch - 1 - n
    in_f, out_f = _ret_specs(nch, ncc, False, back)
    in_b, out_b = _ret_specs(nch, ncc, True, back)
    ss_spec = pl.BlockSpec((1, RET_HEADS, RET_DH, RET_DH), lambda n: (nch - 1 - n, 0, 0, 0))
    shp = jax.ShapeDtypeStruct((la, RET_WIDTH), F32)
    dy_spec = lambda rev: pl.BlockSpec(
        (RET_CHUNK, RET_WIDTH), lambda n: (jnp.maximum(_ret_chunk_of(nch - 1 - n, ncc, nch, rev) - ncc, 0), 0))
    return pl.pallas_call(
        body, name="ret_scan_bwd", grid=(nch,),
        in_specs=[_full(tabs.shape)] + in_f + [dy_spec(False), ss_spec] + in_b + [dy_spec(True), ss_spec],
        out_specs=[out_f, out_f, out_f, out_b, out_b, out_b, _full((2, RET_HEADS, 8, 128))],
        out_shape=[shp] * 6 + [jax.ShapeDtypeStruct((2, RET_HEADS, 8, 128), F32)],
        scratch_shapes=[pltpu.VMEM((2, RET_HEADS, RET_DH, RET_DH), F32)],
        compiler_params=_params(("arbitrary",)),
    )(tabs, p_all, p_all, p_all, cos_t, sin_t, dy_all, ssf, p_all, p_all, p_all, cos_t, sin_t, dy_all, ssb)


def _in_bwd(dqf, dkf, dvf, dqb, dkb, dvb, du, dg, cos_t, sin_t, w_in_b, x, ctx, n1w, mod4, dx1, pairs, kinds):
    l, lc = x.shape[0], ctx.shape[0]
    la = l + lc
    tm = TOK_TILE
    nct = lc // tm
    n = len(pairs)
    shapes = _rs_slot_shapes(pairs, kinds)

    def body(dqf_ref, dkf_ref, dvf_ref, dqb_ref, dkb_ref, dvb_ref, du_ref, dg_ref, cos_ref, sin_ref,
             w_ref, x_ref, c_ref, nw_ref, mod_ref, dx1_ref, *rest):
        dp_ref, gx_ref, acc_ref = rest[n:n + 3]
        i = pl.program_id(0)
        is_ctx = i < nct
        _behind(i, la // tm - 1, functools.partial(_rs_chip_copies, rest[:n], rest[n + 3:2 * n + 3], kinds, shapes,
                                                  *rest[2 * n + 3:]))

        @pl.when(i == 0)
        def _():
            acc_ref[...] = jnp.zeros_like(acc_ref)

        cs, sn = cos_ref[...], sin_ref[...]
        def piece(k, val):
            cols = slice(S5_WIDTH * k, S5_WIDTH * (k + 1))
            dp_ref[:, cols] = val.astype(BF16)
            return _dot_nt(dp_ref[:, cols], w_ref[:, cols])

        dh1 = piece(0, du_ref[...])
        dh1 = dh1 + piece(3, dvf_ref[...] + dvb_ref[...])
        dh1 = dh1 + piece(4, jnp.where(is_ctx, 0.0, dg_ref[...]))
        for k, (f_ref, b_ref, scale) in ((1, (dqf_ref, dqb_ref, 1.0)), (2, (dkf_ref, dkb_ref, K_SCALE))):
            heads = [_rope_t(f_ref[:, RET_DH * h:RET_DH * (h + 1)] + b_ref[:, RET_DH * h:RET_DH * (h + 1)], cs, sn) * scale
                     for h in range(RET_HEADS)]
            dh1 = dh1 + piece(k, jnp.concatenate(heads, axis=1))
        xt = jnp.where(is_ctx, c_ref[...], x_ref[...])
        sh = jnp.where(is_ctx, mod_ref[0:1, :], mod_ref[2:3, :])
        sc = jnp.where(is_ctx, mod_ref[1:2, :], mod_ref[3:4, :])
        _, vjp = jax.vjp(_rms_mod, xt, nw_ref[...], sh, sc)
        dx, dnw, dsh, dsc = vjp(dh1)
        gx_ref[...] = dx + dx1_ref[...]
        cf = jnp.where(is_ctx, 1.0, 0.0)
        acc_ref[0:1, :] += dnw
        acc_ref[1:2, :] += cf * dsh
        acc_ref[2:3, :] += cf * dsc
        acc_ref[3:4, :] += (1.0 - cf) * dsh
        acc_ref[4:5, :] += (1.0 - cf) * dsc

    row = pl.BlockSpec((tm, RET_WIDTH), lambda i: (i, 0))
    tab = pl.BlockSpec((tm, RET_DH), lambda i: (i, 0))
    xrow = pl.BlockSpec((tm, D_MODEL), lambda i: (jnp.maximum(i - nct, 0), 0))
    return pl.pallas_call(
        body, name="in_bwd", grid=(la // tm,),
        in_specs=[row] * 7 + [pl.BlockSpec((tm, RET_WIDTH), lambda i: (jnp.maximum(i - nct, 0), 0)),
                              tab, tab, _full((D_MODEL, IN_COLS)), xrow,
                              pl.BlockSpec((tm, D_MODEL), lambda i: (jnp.minimum(i, nct - 1), 0)),
                              _full((1, D_MODEL)), _full((4, D_MODEL)), xrow] + [ANY] * n,
        out_specs=[pl.BlockSpec((tm, IN_COLS), lambda i: (i, 0)), xrow, _full((8, D_MODEL))] + [ANY] * n,
        out_shape=[jax.ShapeDtypeStruct((la, IN_COLS), BF16), jax.ShapeDtypeStruct((l, D_MODEL), F32),
                   jax.ShapeDtypeStruct((8, D_MODEL), F32)]
        + [jax.ShapeDtypeStruct((4,) + s, p.dtype) for s, p in zip(shapes, pairs)],
        scratch_shapes=[pltpu.SemaphoreType.DMA((n, 3)), pltpu.SemaphoreType.DMA((n, 3))],
        compiler_params=_params(("arbitrary",)),
    )(dqf, dkf, dvf, dqb, dkb, dvb, du, dg, cos_t, sin_t, w_in_b, x, ctx, n1w, mod4, dx1, *pairs)


def _outproj_up(x, y_all, of, ob, p_all, w_glu_b, b_glu, w_out_b, mod3, n2w, w_up_b, nct):
    l = x.shape[0]
    tm = TOK_TILE

    def body(x_ref, y_ref, of_ref, ob_ref, g_ref, wg_ref, bg_ref, wo_ref, mod_ref, nw_ref, wu_ref,
             x1_ref, mix_ref, h2_ref, up_ref, mb_ref, yr_ref):
        yg = _gelu(y_ref[...])
        mb_ref[:, 0:S5_WIDTH] = (yg * _sigmoid(_dot(yg.astype(BF16), wg_ref[...]) + bg_ref[...])).astype(BF16)
        yr = of_ref[...] + ob_ref[...]
        yr_ref[...] = yr
        for h in range(RET_HEADS):
            sl = slice(RET_DH * h, RET_DH * (h + 1))
            mb_ref[:, S5_WIDTH + RET_DH * h:S5_WIDTH + RET_DH * (h + 1)] = (
                _head_norm_gate(yr[:, sl], g_ref[:, sl]).astype(BF16))
        mix = _dot(mb_ref[...], wo_ref[...])
        mix_ref[...] = mix
        x1 = x_ref[...] + mod_ref[0:1, :] * mix
        x1_ref[...] = x1
        h2 = _rms_mod(x1, nw_ref[...], mod_ref[1:2, :], mod_ref[2:3, :]).astype(BF16)
        h2_ref[...] = h2
        up_ref[...] = _dot(h2, wu_ref[...])

    row = lambda w: pl.BlockSpec((tm, w), lambda i: (i, 0))
    arow = pl.BlockSpec((tm, RET_WIDTH), lambda i: (i + nct, 0))
    return pl.pallas_call(
        body, name="outproj_up", grid=(l // tm,),
        in_specs=[row(D_MODEL), arow, arow, arow, pl.BlockSpec((tm, RET_WIDTH), lambda i: (i + nct, G_COL // 4)),
                  _full((S5_WIDTH, S5_WIDTH)), _full((1, S5_WIDTH)), _full((D_MODEL, D_MODEL)), _full((3, D_MODEL)),
                  _full((1, D_MODEL)), _full((D_MODEL, 2 * D_FF))],
        out_specs=[row(D_MODEL), row(D_MODEL), row(D_MODEL), row(2 * D_FF), row(D_MODEL), row(RET_WIDTH)],
        out_shape=[jax.ShapeDtypeStruct((l, D_MODEL), F32), jax.ShapeDtypeStruct((l, D_MODEL), F32),
                   jax.ShapeDtypeStruct((l, D_MODEL), BF16), jax.ShapeDtypeStruct((l, 2 * D_FF), F32),
                   jax.ShapeDtypeStruct((l, D_MODEL), BF16), jax.ShapeDtypeStruct((l, RET_WIDTH), F32)],
        compiler_params=_params(("parallel",)),
    )(x, y_all, of, ob, p_all, w_glu_b, b_glu, w_out_b, mod3, n2w, w_up_b)


HALO = 8


def _conv_taps(g, prev_row, next_row):
    t = g.shape[0]
    r = _iota2(g.shape, 0)
    gprev = jnp.where(r == 0, prev_row, pltpu.roll(g, 1, 0))
    gnext = jnp.where(r == t - 1, next_row, pltpu.roll(g, t - 1, 0))
    return gprev, gnext


def _ffn_loss(up, x1, conv_w, conv_b, w_down_b, gate, fnw, tgt):
    l = x1.shape[0]
    tm = TOK_TILE
    nt = l // tm
    hb = tm // HALO

    cw = 256

    def body(up_a, up_g, hp_ref, hn_ref, x1_ref, cw_ref, cb_ref, wd_ref, gate_ref, fn_ref, tgt_ref,
             act_ref, dx2_ref, ddn_ref, dact_ref, acc_ref, ddn_scr):
        step = pl.program_id(0)
        i = jnp.minimum(step, nt - 1)

        @pl.when(step == 0)
        def _():
            acc_ref[...] = jnp.zeros_like(acc_ref)
            ddn_scr[...] = jnp.zeros_like(ddn_scr)

        ddn_prev = ddn_scr[...]
        dn = jnp.zeros((tm, D_MODEL), F32)
        for c in range(D_FF // cw):
            cols = slice(cw * c, cw * (c + 1))
            g = up_g[:, cols]
            prev_row = jnp.where(i == 0, 0.0, hp_ref[HALO - 1:HALO, cols])
            next_row = jnp.where(i == nt - 1, 0.0, hn_ref[0:1, cols])
            gprev, gnext = _conv_taps(g, prev_row, next_row)
            gc = cb_ref[:, cols] + gprev * cw_ref[0:1, cols] + g * cw_ref[1:2, cols] + gnext * cw_ref[2:3, cols]
            act = (_gelu(gc) * up_a[:, cols]).astype(BF16)
            act_ref[:, cols] = act
            dn = dn + _dot(act, wd_ref[cols, :])
            dact_ref[:, cols] = _dot_nt(ddn_prev, wd_ref[cols, :])
        x2 = x1_ref[...] + gate_ref[...] * dn
        y, vjp = jax.vjp(_rms, x2, fn_ref[...])
        err = y - tgt_ref[...]
        dx2, dfn = vjp(err * (1.0 / D_MODEL))
        dx2_ref[...] = dx2
        ddn = (dx2 * gate_ref[...]).astype(BF16)
        ddn_ref[...] = ddn
        ddn_scr[...] = ddn
        live = step < nt
        acc_ref[0:1, :] += jnp.where(live, dfn, 0.0)
        acc_ref[1:2, :] += jnp.where(live, jnp.sum(dx2 * dn, axis=0, keepdims=True), 0.0)
        acc_ref[2:3, :] += jnp.where(live, (0.5 / D_MODEL) * jnp.sum(err * err), 0.0)

    tile = lambda s: jnp.minimum(s, nt - 1)
    row = lambda w, cb=0: pl.BlockSpec((tm, w), lambda s: (tile(s), cb))
    last = l // HALO - 1
    return pl.pallas_call(
        body, name="ffn_loss", grid=(nt + 1,),
        in_specs=[row(D_FF, 0), row(D_FF, 1),
                  pl.BlockSpec((HALO, D_FF), lambda s: (jnp.maximum(tile(s) * hb - 1, 0), 1)),
                  pl.BlockSpec((HALO, D_FF), lambda s: (jnp.minimum((tile(s) + 1) * hb, last), 1)),
                  row(D_MODEL), _full((3, D_FF)), _full((1, D_FF)), _full((D_FF, D_MODEL)),
                  _full((1, D_MODEL)), _full((1, D_MODEL)), row(D_MODEL)],
        out_specs=[row(D_FF), row(D_MODEL), row(D_MODEL),
                   pl.BlockSpec((tm, D_FF), lambda s: (jnp.maximum(s - 1, 0), 0)), _full((8, D_MODEL))],
        out_shape=[jax.ShapeDtypeStruct((l, D_FF), BF16), jax.ShapeDtypeStruct((l, D_MODEL), F32),
                   jax.ShapeDtypeStruct((l, D_MODEL), BF16), jax.ShapeDtypeStruct((l, D_FF), F32),
                   jax.ShapeDtypeStruct((8, D_MODEL), F32)],
        scratch_shapes=[pltpu.VMEM((tm, D_MODEL), BF16)],
        compiler_params=_params(("arbitrary",)),
    )(up, up, up, up, x1, conv_w, conv_b, w_down_b, gate, fnw, tgt)


def _convglu_bwd(up, dact, conv_w, conv_b):
    l = up.shape[0]
    tm = 128
    nt = l // tm
    hb = tm // HALO
    te = tm + 2 * HALO

    def body(a_ref, ap_ref, an_ref, g_ref, gp_ref, gn_ref, d_ref, dp_ref, dn_ref, cw_ref, cb_ref,
             dup_ref, acc_ref):
        i = pl.program_id(0)

        @pl.when(i == 0)
        def _():
            acc_ref[...] = jnp.zeros_like(acc_ref)

        def ext(p, c, n):
            return jnp.concatenate([jnp.where(i == 0, 0.0, p[...]), c[...], jnp.where(i == nt - 1, 0.0, n[...])], axis=0)

        ae, ge, de = ext(ap_ref, a_ref, an_ref), ext(gp_ref, g_ref, gn_ref), ext(dp_ref, d_ref, dn_ref)
        gprev = pltpu.roll(ge, 1, 0)
        gnext = pltpu.roll(ge, te - 1, 0)
        w0, w1, w2 = cw_ref[0:1, :], cw_ref[1:2, :], cw_ref[2:3, :]
        gce = cb_ref[...] + gprev * w0 + ge * w1 + gnext * w2
        gel, dgel = _gelu_and_grad(gce)
        dae = de * gel
        dgce = de * ae * dgel
        dge = dgce * w1 + pltpu.roll(dgce, te - 1, 0) * w0 + pltpu.roll(dgce, 1, 0) * w2
        mid = slice(HALO, HALO + tm)
        dup_ref[:, 0:D_FF] = dae[mid].astype(BF16)
        dup_ref[:, D_FF:2 * D_FF] = dge[mid].astype(BF16)
        dgc = dgce[mid]
        acc_ref[0:1, :] += jnp.sum(dgc * gprev[mid], axis=0, keepdims=True)
        acc_ref[1:2, :] += jnp.sum(dgc * ge[mid], axis=0, keepdims=True)
        acc_ref[2:3, :] += jnp.sum(dgc * gnext[mid], axis=0, keepdims=True)
        acc_ref[3:4, :] += jnp.sum(dgc, axis=0, keepdims=True)

    last = l // HALO - 1

    def trio(cb):
        return [pl.BlockSpec((tm, D_FF), lambda i: (i, cb)),
                pl.BlockSpec((HALO, D_FF), lambda i: (jnp.maximum(i * hb - 1, 0), cb)),
                pl.BlockSpec((HALO, D_FF), lambda i: (jnp.minimum((i + 1) * hb, last), cb))]

    return pl.pallas_call(
        body, name="convglu_bwd", grid=(nt,),
        in_specs=trio(0) + trio(1) + trio(0) + [_full((3, D_FF)), _full((1, D_FF))],
        out_specs=[pl.BlockSpec((tm, 2 * D_FF), lambda i: (i, 0)), _full((8, D_FF))],
        out_shape=[jax.ShapeDtypeStruct((l, 2 * D_FF), BF16), jax.ShapeDtypeStruct((8, D_FF), F32)],
        compiler_params=_params(("arbitrary",)),
    )(up, up, up, up, up, up, dact, dact, dact, conv_w, conv_b)


def _up_bwd(dup, w_up_b, w_out_b, x1, dx2, mix, mod3, n2w, y_all, y_ret, p_all, w_glu_b, b_glu, zero_rows, nct, pairs,
            kinds):
    l = x1.shape[0]
    tm = TOK_TILE
    nt = l // tm
    n = len(pairs)
    shapes = _rs_slot_shapes(pairs, kinds)
    n_out = 8

    def body(dup_ref, wu_ref, wo_ref, x1_ref, dx2_ref, mix_ref, mod_ref, nw_ref, y_ref, yr_ref, g_ref, wg_ref, bg_ref,
             zero_rows_ref, *rest):
        dx1_ref, dmixb_ref, acc_ref, dys_ref, dyr_ref, dg_ref, gw_ref, gb_ref = rest[n:n + n_out]
        send_sems, recv_sems, dy_scr = rest[2 * n + n_out:]
        step = pl.program_id(0)

        @pl.when(step == 0)
        def _():
            acc_ref[...] = jnp.zeros_like(acc_ref)
            gw_ref[...] = jnp.zeros_like(gw_ref)
            gb_ref[...] = jnp.zeros_like(gb_ref)

        _behind(step, nt - 1, functools.partial(_rs_chip_copies, rest[:n], rest[n + n_out:2 * n + n_out], kinds,
                                                shapes, send_sems, recv_sems))

        dh2 = _dot_nt(dup_ref[...], wu_ref[...])
        _, vjp = jax.vjp(_rms_mod, x1_ref[...], nw_ref[...], mod_ref[1:2, :], mod_ref[2:3, :])
        dx, dnw, dsh, dsc = vjp(dh2)
        dx1 = dx + dx2_ref[...]
        dx1_ref[...] = dx1
        dmixb = (dx1 * mod_ref[0:1, :]).astype(BF16)
        dmixb_ref[...] = dmixb
        dmix = _dot_nt(dmixb, wo_ref[...])
        acc_ref[0:1, :] += dnw
        acc_ref[1:2, :] += jnp.sum(dx1 * mix_ref[...], axis=0, keepdims=True)
        acc_ref[2:3, :] += dsh
        acc_ref[3:4, :] += dsc

        yg, dgel = _gelu_and_grad(y_ref[...])
        ygb = yg.astype(BF16)
        sg = _sigmoid(_dot(ygb, wg_ref[...]) + bg_ref[...])
        ds = dmix[:, 0:S5_WIDTH]
        dz = ds * yg * sg * (1.0 - sg)
        dzb = dz.astype(BF16)
        _s5_put_rows(dys_ref, dy_scr, (ds * sg + _dot_nt(dzb, wg_ref[...])) * dgel)
        gw_ref[...] += _dot_tn(ygb, dzb)
        gb_ref[...] += jnp.sum(dz, axis=0, keepdims=True)

        for h in range(RET_HEADS):
            sl = slice(RET_DH * h, RET_DH * (h + 1))
            _, hvjp = jax.vjp(_head_norm_gate, yr_ref[:, sl], g_ref[:, sl])
            dyr, dg = hvjp(dmix[:, S5_WIDTH + RET_DH * h:S5_WIDTH + RET_DH * (h + 1)])
            dyr_ref[:, sl] = dyr
            dg_ref[:, sl] = dg

    row = pl.BlockSpec((tm, D_MODEL), lambda i: (i, 0))
    half = pl.BlockSpec((tm, S5_WIDTH), lambda i: (i, 0))
    f32h = jax.ShapeDtypeStruct((l, RET_WIDTH), F32)
    return pl.pallas_call(
        body, name="up_bwd", grid=(nt,),
        in_specs=[pl.BlockSpec((tm, 2 * D_FF), lambda i: (i, 0)), _full((D_MODEL, 2 * D_FF)),
                  _full((D_MODEL, D_MODEL)), row, row, row, _full((3, D_MODEL)), _full((1, D_MODEL)),
                  pl.BlockSpec((tm, S5_WIDTH), lambda i: (i + nct, 0)), half,
                  pl.BlockSpec((tm, RET_WIDTH), lambda i: (i + nct, G_COL // 4)),
                  _full((S5_WIDTH, S5_WIDTH)), _full((1, S5_WIDTH)), ANY] + [ANY] * n,
        out_specs=[row, row, _full((8, D_MODEL)),
                   pl.BlockSpec((S5_NB, tm // S5_T, S5_BW), lambda i: (0, i + nct, 0)), half, half,
                   _full((S5_WIDTH, S5_WIDTH)),
                   _full((1, S5_WIDTH))] + [ANY] * n,
        out_shape=[jax.ShapeDtypeStruct((l, D_MODEL), F32), jax.ShapeDtypeStruct((l, D_MODEL), BF16),
                   jax.ShapeDtypeStruct((8, D_MODEL), F32), jax.ShapeDtypeStruct(zero_rows.shape, BF16), f32h, f32h,
                   jax.ShapeDtypeStruct((S5_WIDTH, S5_WIDTH), F32), jax.ShapeDtypeStruct((1, S5_WIDTH), F32)]
        + [jax.ShapeDtypeStruct((4,) + s, p.dtype) for s, p in zip(shapes, pairs)],
        input_output_aliases={13: 3},
        scratch_shapes=[pltpu.SemaphoreType.DMA((n, 3)), pltpu.SemaphoreType.DMA((n, 3)),
                        pltpu.VMEM((tm // S5_T, S5_T, S5_WIDTH), F32)],
        compiler_params=_params(("arbitrary",)),
    )(dup, w_up_b, w_out_b, x1, dx2, mix, mod3, n2w, y_all, y_ret, p_all, w_glu_b, b_glu, zero_rows, *pairs)


MOD_ROWS = 16
MOD_COLS = 6 * D_MODEL // 4


def _mod_fwd(c_all, c_ctx, w_mod_b, b_loc):
    def body(c_ref, cc_ref, w_ref, b_ref, m_ref, s_ref):
        cond = jnp.concatenate([c_ref[...], jnp.broadcast_to(cc_ref[...], (8, D_MODEL))], axis=0)
        s = _silu(cond).astype(BF16)
        s_ref[...] = s
        m_ref[...] = _dot(s, w_ref[...]) + b_ref[...]

    return pl.pallas_call(
        body, name="mod_fwd",
        out_shape=[jax.ShapeDtypeStruct((MOD_ROWS, MOD_COLS), F32), jax.ShapeDtypeStruct((MOD_ROWS, D_MODEL), BF16)],
        compiler_params=_params(),
    )(c_all, c_ctx, w_mod_b, b_loc)


def _mod_bwd_sum(dm_all):
    def body(d_ref, dm_ref, gb_ref):
        rows = [d_ref[k, 0:1, :] for k in range(8)]
        ctx_sum = d_ref[0, 1:2, :]
        for k in range(1, 8):
            ctx_sum = ctx_sum + d_ref[k, 1:2, :]
        gb = ctx_sum
        for k in range(8):
            gb = gb + rows[k]
        gb_ref[...] = gb
        dm_ref[...] = jnp.concatenate(rows + [ctx_sum] + [jnp.zeros((7, 6 * D_MODEL), F32)], axis=0)

    return pl.pallas_call(
        body, name="mod_bwd_sum",
        out_shape=[jax.ShapeDtypeStruct((MOD_ROWS, 6 * D_MODEL), F32), jax.ShapeDtypeStruct((1, 6 * D_MODEL), F32)],
        compiler_params=_params(),
    )(dm_all)


def _mod_bwd_w(dm_loc, s_b, c_ctx, w_mod_b):
    def body(d_ref, s_ref, cc_ref, w_ref, gw_ref, gc_ref):
        db = d_ref[...].astype(BF16)
        gw_ref[...] = _dot_tn(s_ref[...], db)
        ds = _dot_nt(db, w_ref[...])
        _, vjp = jax.vjp(_silu, cc_ref[...])
        gc_ref[...] = jnp.broadcast_to(vjp(ds[8:9, :])[0], (8, D_MODEL))

    return pl.pallas_call(
        body, name="mod_bwd_w",
        out_shape=[jax.ShapeDtypeStruct((D_MODEL, MOD_COLS), F32), jax.ShapeDtypeStruct((8, D_MODEL), F32)],
        compiler_params=_params(),
    )(dm_loc, s_b, c_ctx, w_mod_b)


def _adamw(w, g, m, v, name):
    r, c = w.shape
    tr = _pick(r, (256, 128, 64, 32, 16, 8))
    bc1 = 1.0 - ADAM_B1 ** ADAM_STEP
    bc2 = 1.0 - ADAM_B2 ** ADAM_STEP

    def body(w_ref, g_ref, m_ref, v_ref, d_ref, nm_ref, nv_ref):
        gg = g_ref[...]
        nm = ADAM_B1 * m_ref[...] + (1.0 - ADAM_B1) * gg
        nv = ADAM_B2 * v_ref[...] + (1.0 - ADAM_B2) * (gg * gg)
        nm_ref[...] = nm
        nv_ref[...] = nv
        d_ref[...] = -ADAM_LR * ((nm / bc1) / (jnp.sqrt(nv / bc2) + ADAM_EPS) + ADAM_WD * w_ref[...])

    blk = pl.BlockSpec((tr, c), lambda i: (i, 0))
    shp = jax.ShapeDtypeStruct((r, c), F32)
    return pl.pallas_call(
        body, name=name, grid=(r // tr,), in_specs=[blk] * 4, out_specs=[blk] * 3, out_shape=[shp] * 3,
        compiler_params=_params(("parallel",)),
    )(w, g, m, v)


def _sum_slots(a, name):
    n, r, c = a.shape
    tr = _pick(r, (376, 256, 208, 128, 64, 32, 16, 8))

    def body(a_ref, o_ref):
        acc = a_ref[0].astype(F32)
        for k in range(1, n):
            acc = acc + a_ref[k].astype(F32)
        o_ref[...] = acc

    return pl.pallas_call(
        body, name=name, grid=(r // tr,),
        in_specs=[pl.BlockSpec((n, tr, c), lambda i: (0, i, 0))],
        out_specs=pl.BlockSpec((tr, c), lambda i: (i, 0)),
        out_shape=jax.ShapeDtypeStruct((r, c), F32),
        compiler_params=_params(("parallel",)),
    )(a)


def _mesh_pos():
    return lax.axis_index("x"), lax.axis_index("y"), lax.axis_index("c")


def _all_gather8(v, name):
    m_per, n = v.shape

    def body(x_ref, out_ref, send_sems, recv_sems, local_sem):
        x, y, c = _mesh_pos()
        me, sibling = (x, y, c), (x, y, 1 - c)
        chips = [(1 - x, y), (x, 1 - y), (1 - x, 1 - y)]

        def rows(px, py, pc):
            return out_ref.at[pl.ds((4 * px + 2 * py + pc) * m_per, m_per), :]

        def copy(k, block, to, src=None):
            return pltpu.make_async_remote_copy(
                src_ref=rows(*block) if src is None else src, dst_ref=rows(*block),
                send_sem=send_sems.at[k], recv_sem=recv_sems.at[k], device_id=to, device_id_type=MESH_ID)

        mine = pltpu.make_async_copy(x_ref, rows(*me), local_sem)
        mine.start()
        first = [copy(0, me, sibling, src=x_ref)]
        first += [copy(1 + j, me, (*chip, c), src=x_ref) for j, chip in enumerate(chips)]
        for cp in first:
            cp.start()
        passed = [copy(4 + j, (*chip, c), sibling) for j, chip in enumerate(chips)]
        for j, chip in enumerate(chips):
            copy(1 + j, (*chip, c), me).wait_recv()
            passed[j].start()
        copy(0, sibling, me).wait_recv()
        for j, chip in enumerate(chips):
            copy(4 + j, (*chip, 1 - c), me).wait_recv()
        for cp in first + passed:
            cp.wait_send()
        mine.wait()

    return pl.pallas_call(
        body, name=name,
        out_shape=jax.ShapeDtypeStruct((8 * m_per, n), v.dtype),
        in_specs=[pl.BlockSpec(memory_space=pltpu.VMEM)],
        out_specs=pl.BlockSpec(memory_space=pltpu.VMEM),
        scratch_shapes=[pltpu.SemaphoreType.DMA((7,)), pltpu.SemaphoreType.DMA((7,)), pltpu.SemaphoreType.DMA],
        compiler_params=_params(),
    )(v)


ANY = pl.BlockSpec(memory_space=pl.ANY)
def PEER_CHIPS(x, y):
    return [(x, 1 - y), (1 - x, y), (1 - x, 1 - y)]


def _shard_region(ref, kind, k, rl, cl, r0, nr, c0, nc):
    if kind == "col":
        return ref.at[pl.ds(r0, nr), pl.ds(k * cl + c0, nc)]
    return ref.at[pl.ds(k * rl + r0, nr), pl.ds(c0, nc)]


def _place_shard(w, kind, chip, name):
    rl, cl = w.shape
    tr = _pick(rl, (256, 128, 64))
    nt = rl // tr

    def body(chip_ref, w_ref, o_ref):
        o_ref[...] = w_ref[...].astype(BF16)

    o_map = (lambda i, chip_ref: (i, chip_ref[0])) if kind == "col" else (lambda i, chip_ref: (chip_ref[0] * nt + i, 0))
    return pl.pallas_call(
        body, name=name,
        grid_spec=pltpu.PrefetchScalarGridSpec(
            num_scalar_prefetch=1, grid=(nt,),
            in_specs=[pl.BlockSpec((tr, cl), lambda i, chip_ref: (i, 0))], out_specs=pl.BlockSpec((tr, cl), o_map)),
        out_shape=jax.ShapeDtypeStruct((rl, 4 * cl) if kind == "col" else (4 * rl, cl), BF16),
        compiler_params=_params(("parallel",)),
    )(chip.reshape(1), w)


def _gather_shard_shapes(placed, kinds):
    return [(p.shape[0], p.shape[1] // 4) if k == "col" else (p.shape[0] // 4, p.shape[1]) for p, k in zip(placed, kinds)]


def _gather_chip_copies(outs, kinds, shard_shapes, send_sems, recv_sems, with_arrivals=True):
    x, y, c = _mesh_pos()
    me = 2 * x + y
    sends, arrivals = [], []
    for a in range(len(outs)):
        rl, cl = shard_shapes[a]
        rh = rl // 2
        reg = functools.partial(_shard_region, outs[a], kinds[a], rl=rl, cl=cl, r0=c * rh, nr=rh, c0=0, nc=cl)
        for j, (px, py) in enumerate(PEER_CHIPS(x, y)):
            to = dict(send_sem=send_sems.at[a, j], recv_sem=recv_sems.at[a, j], device_id=(px, py, c),
                      device_id_type=MESH_ID)
            sends.append(pltpu.make_async_remote_copy(src_ref=reg(k=me), dst_ref=reg(k=me), **to))
            if with_arrivals:
                got = reg(k=2 * px + py)
                arrivals.append(pltpu.make_async_remote_copy(src_ref=got, dst_ref=got, **to))
    return sends, arrivals


def _gather_sibling_copies(outs, kinds, shard_shapes, send_sems, recv_sems):
    x, y, c = _mesh_pos()
    forwards, arrivals = [], []
    for a in range(len(outs)):
        rl, cl = shard_shapes[a]
        rh = rl // 2
        for j, (px, py) in enumerate(PEER_CHIPS(x, y)):
            to = dict(send_sem=send_sems.at[a, j], recv_sem=recv_sems.at[a, j], device_id=(x, y, 1 - c),
                      device_id_type=MESH_ID)
            reg = functools.partial(_shard_region, outs[a], kinds[a], k=2 * px + py, rl=rl, cl=cl, nr=rh, c0=0, nc=cl)
            forwards.append(pltpu.make_async_remote_copy(src_ref=reg(r0=c * rh), dst_ref=reg(r0=c * rh), **to))
            arrivals.append(pltpu.make_async_remote_copy(src_ref=reg(r0=(1 - c) * rh), dst_ref=reg(r0=(1 - c) * rh), **to))
    return forwards, arrivals


def _gather_sibling(placed, kinds, name):
    n = len(placed)
    shard_shapes = _gather_shard_shapes(placed, kinds)

    def body(*refs):
        forwards, from_sibling = _gather_sibling_copies(refs[n:2 * n], kinds, shard_shapes, *refs[2 * n:])
        for cp in forwards:
            cp.start()
        for cp in from_sibling:
            cp.wait_recv()
        for cp in forwards:
            cp.wait_send()

    return pl.pallas_call(
        body, name=name,
        out_shape=[jax.ShapeDtypeStruct(p.shape, p.dtype) for p in placed],
        in_specs=[ANY] * n, out_specs=[ANY] * n, input_output_aliases={a: a for a in range(n)},
        scratch_shapes=[pltpu.SemaphoreType.DMA((n, 3))] * 2,
        compiler_params=_params(),
    )(*placed)


def _half(kind, r, c):
    return (r // 2, c) if kind == "col" else (r, c // 2)


def _half_of(ref, kind, which):
    r, c = ref.shape
    hr, hc = _half(kind, r, c)
    return ref.at[pl.ds(which * hr, hr), :] if kind == "col" else ref.at[:, pl.ds(which * hc, hc)]


def _rs_sibling(grads, kinds, name):
    n = len(grads)

    def body(*refs):
        srcs, dsts = refs[:n], refs[n:2 * n]
        send_sems, recv_sems = refs[2 * n:]
        x, y, c = _mesh_pos()
        cps = [pltpu.make_async_remote_copy(src_ref=_half_of(srcs[a], kinds[a], 1 - c), dst_ref=dsts[a],
                                            send_sem=send_sems.at[a], recv_sem=recv_sems.at[a],
                                            device_id=(x, y, 1 - c), device_id_type=MESH_ID) for a in range(n)]
        for cp in cps:
            cp.start()
        for cp in cps:
            cp.wait()

    return pl.pallas_call(
        body, name=name,
        out_shape=[jax.ShapeDtypeStruct(_half(k, *g.shape), g.dtype) for g, k in zip(grads, kinds)],
        in_specs=[ANY] * n, out_specs=[ANY] * n,
        scratch_shapes=[pltpu.SemaphoreType.DMA((n,)), pltpu.SemaphoreType.DMA((n,))],
        compiler_params=_params(),
    )(*grads)


def _pair_sum(gf, rv, kind, ci, name):
    r, c = rv.shape
    tr = _pick(r, (128, 64, 32, 16, 8))
    nt = r // tr

    def body(ci_ref, g_ref, r_ref, o_ref):
        o_ref[...] = (g_ref[...] + r_ref[...]).astype(BF16)

    g_map = (lambda i, ci_ref: (ci_ref[0] * nt + i, 0)) if kind == "col" else (lambda i, ci_ref: (i, ci_ref[0]))
    blk = pl.BlockSpec((tr, c), lambda i, ci_ref: (i, 0))
    return pl.pallas_call(
        body, name=name,
        grid_spec=pltpu.PrefetchScalarGridSpec(num_scalar_prefetch=1, grid=(nt,),
                                               in_specs=[pl.BlockSpec((tr, c), g_map), blk], out_specs=blk),
        out_shape=jax.ShapeDtypeStruct((r, c), BF16),
        compiler_params=_params(("parallel",)),
    )(ci.reshape(1), gf, rv)


def _rs_slot_shapes(pairs, kinds):
    return [(p.shape[0], p.shape[1] // 4) if k == "col" else (p.shape[0] // 4, p.shape[1]) for p, k in zip(pairs, kinds)]


def _rs_chip_copies(srcs, dsts, kinds, shapes, send_sems, recv_sems, with_arrivals=True):
    x, y, c = _mesh_pos()
    me = 2 * x + y
    sends, arrivals = [], []
    for a in range(len(srcs)):
        rl, cl = shapes[a]
        reg = functools.partial(_shard_region, srcs[a], kinds[a], rl=rl, cl=cl, r0=0, nr=rl, c0=0, nc=cl)
        for j, (px, py) in enumerate(PEER_CHIPS(x, y)):
            to = dict(send_sem=send_sems.at[a, j], recv_sem=recv_sems.at[a, j], device_id=(px, py, c),
                      device_id_type=MESH_ID)
            sends.append(pltpu.make_async_remote_copy(src_ref=reg(k=2 * px + py), dst_ref=dsts[a].at[me], **to))
            if with_arrivals:
                slot = dsts[a].at[2 * px + py]
                arrivals.append(pltpu.make_async_remote_copy(src_ref=slot, dst_ref=slot, **to))
    return sends, arrivals


def _rs_chips(pairs, kinds):
    n = len(pairs)
    shapes = _rs_slot_shapes(pairs, kinds)

    def body(*refs):
        sends, arrivals = _rs_chip_copies(refs[:n], refs[n:2 * n], kinds, shapes, *refs[2 * n:])
        for cp in sends:
            cp.start()
        for cp in arrivals:
            cp.wait_recv()
        for cp in sends:
            cp.wait_send()

    return pl.pallas_call(
        body, name="rs_chips",
        out_shape=[jax.ShapeDtypeStruct((4,) + s, p.dtype) for s, p in zip(shapes, pairs)],
        in_specs=[ANY] * n, out_specs=[ANY] * n,
        scratch_shapes=[pltpu.SemaphoreType.DMA((n, 3)), pltpu.SemaphoreType.DMA((n, 3))],
        compiler_params=_params(),
    )(*pairs)


def _sum_chips(pair, got, kind, pos, name):
    _, r, c = got.shape
    tr = _pick(r, (256, 128, 64, 32, 16))
    nt = r // tr

    def body(pos_ref, own_ref, g1_ref, g2_ref, g3_ref, o_ref):
        o_ref[...] = ((own_ref[...].astype(F32) + g1_ref[0].astype(F32)) + g2_ref[0].astype(F32)) + g3_ref[0].astype(F32)

    if kind == "col":
        own_map = lambda i, p: (i, p[1])
        out_map = lambda i, p: (p[0] * nt + i, 0)
        out_shape = (2 * r, c)
    else:
        own_map = lambda i, p: (p[1] * nt + i, 0)
        out_map = lambda i, p: (i, p[0])
        out_shape = (r, 2 * c)
    peer = lambda m: pl.BlockSpec((1, tr, c), lambda i, p: (p[1] ^ m, i, 0))
    return pl.pallas_call(
        body, name=name,
        grid_spec=pltpu.PrefetchScalarGridSpec(
            num_scalar_prefetch=1, grid=(nt,),
            in_specs=[pl.BlockSpec((tr, c), own_map), peer(1), peer(2), peer(3)],
            out_specs=pl.BlockSpec((tr, c), out_map)),
        out_shape=jax.ShapeDtypeStruct(out_shape, F32),
        compiler_params=_params(("parallel",)),
    )(pos, pair, got, got, got)


def _rs_back(halves, kinds):
    n = len(halves)

    def body(*refs):
        outs = refs[n:2 * n]
        send_sems, recv_sems = refs[2 * n:]
        x, y, c = _mesh_pos()
        cps = []
        for a in range(n):
            mine = _half_of(outs[a], kinds[a], c)
            cps.append(pltpu.make_async_remote_copy(src_ref=mine, dst_ref=mine, send_sem=send_sems.at[a],
                                                    recv_sem=recv_sems.at[a], device_id=(x, y, 1 - c),
                                                    device_id_type=MESH_ID))
            cps[-1].start()
        for a in range(n):
            other = _half_of(outs[a], kinds[a], 1 - c)
            pltpu.make_async_remote_copy(src_ref=other, dst_ref=other, send_sem=send_sems.at[a],
                                         recv_sem=recv_sems.at[a], device_id=(x, y, 1 - c),
                                         device_id_type=MESH_ID).wait_recv()
        for cp in cps:
            cp.wait_send()

    return pl.pallas_call(
        body, name="rs_back",
        out_shape=[jax.ShapeDtypeStruct(h.shape, h.dtype) for h in halves],
        in_specs=[ANY] * n, out_specs=[ANY] * n, input_output_aliases={a: a for a in range(n)},
        scratch_shapes=[pltpu.SemaphoreType.DMA((n,)), pltpu.SemaphoreType.DMA((n,))],
        compiler_params=_params(),
    )(*halves)


def _rope_tables(l, lc):
    rows = l // GRID_W
    n_freq = RET_DH // 4
    inv_freq = ROPE_THETA ** (-jnp.arange(n_freq, dtype=F32) / n_freq)
    sign = jnp.tile(jnp.array([-1.0, 1.0], F32), n_freq)

    def half(n):
        ang = jnp.repeat(jnp.arange(n, dtype=F32)[:, None] * inv_freq, 2, axis=-1)
        return jnp.cos(ang), jnp.sin(ang) * sign

    (cr, sr), (cc, sc) = half(rows), half(GRID_W)
    grid = lambda r, c: jnp.concatenate([jnp.repeat(r, GRID_W, axis=0), jnp.tile(c, (rows, 1))], axis=-1)
    cos_t = jnp.concatenate([jnp.ones((lc, RET_DH), F32), grid(cr, cc)], axis=0)
    sin_t = jnp.concatenate([jnp.zeros((lc, RET_DH), F32), grid(sr, sc)], axis=0)
    return cos_t, sin_t


def _s5_pack(a):
    blk = lambda t: t.reshape(1, S5_NB, 128, S5_STATE)
    lre = jnp.stack([a["s5_lambda_re_f"][0], a["s5_lambda_re_b"][0]]).reshape(2, S5_NB, 8, S5_STATE)
    lim = jnp.stack([a["s5_lambda_im_f"][0], a["s5_lambda_im_b"][0]]).reshape(2, S5_NB, 8, S5_STATE)
    lst = jnp.stack([a["s5_log_step_f"][0], a["s5_log_step_b"][0]]).reshape(2, S5_NB, 8, 1)
    b_re = blk(a["s5_b_re"][0].transpose(0, 2, 1))
    b_im = blk(a["s5_b_im"][0].transpose(0, 2, 1))
    return (lre, lim, lst, b_re, b_im, blk(a["s5_c_re"][0]), blk(a["s5_c_im"][0]),
            a["s5_d"].reshape(1, S5_NB, 1, 128))


def _s5_unpack(g):
    glre, glim, glst, gbre, gbim, gcre, gcim, gd = g
    unb = lambda t: t.reshape(S5_GROUPS, S5_GROUP, S5_STATE).transpose(0, 2, 1)[None]
    return {
        "s5_lambda_re_f": glre[0].reshape(1, S5_GROUPS, S5_STATE), "s5_lambda_re_b": glre[1].reshape(1, S5_GROUPS, S5_STATE),
        "s5_lambda_im_f": glim[0].reshape(1, S5_GROUPS, S5_STATE), "s5_lambda_im_b": glim[1].reshape(1, S5_GROUPS, S5_STATE),
        "s5_log_step_f": glst[0].reshape(1, S5_GROUPS), "s5_log_step_b": glst[1].reshape(1, S5_GROUPS),
        "s5_b_re": unb(gbre), "s5_b_im": unb(gbim),
        "s5_c_re": gcre.reshape(1, S5_GROUPS, S5_GROUP, S5_STATE), "s5_c_im": gcim.reshape(1, S5_GROUPS, S5_GROUP, S5_STATE),
        "s5_d": gd.reshape(1, S5_WIDTH),
    }


def _local_step(a, early, late, mx, mc, conv_w, ci):
    x, ctx, tgt = a["x"][0], a["ctx"][0], a["loss_target"][0]
    l, lc = x.shape[0], ctx.shape[0]
    la = l + lc
    nct, ncc, nrc, cn = lc // TOK_TILE, lc // S5_T, lc // RET_CHUNK, la // S5_T
    n1w, n2w, fnw = a["norm1_w"], a["norm2_w"], a["final_norm_w"].reshape(1, D_MODEL)
    conv_b, b_glu = a["conv_b"], a["s5_b_glu"]
    ld2 = jnp.concatenate([a["ret_log_decay_f"], a["ret_log_decay_b"]], axis=0)
    mod4 = jnp.concatenate([mc[0:2], mx[0:2]], axis=0)
    mod3 = mx[2:5]
    gate5 = mx[5:6]
    cos_t, sin_t = _rope_tables(l, lc)
    s5p = _s5_pack(a)

    gg, xw, yw, a16, *early = _s5_gen(*s5p, early, EARLY_KINDS)
    wb = dict(zip(EARLY_NAMES, _gather_sibling(early, EARLY_KINDS, "gather_sibling_early")))
    p_all, h1b, p3, w_up_p = _norm_inproj(x, ctx, n1w, mod4, wb["w_in"], [late[1]], (LATE_KINDS[1],))
    sloc = _s5_to_states(p3, xw, "s5_state")
    a16s = a16.transpose(1, 0, 2, 3).reshape(2, S5_GROUPS, 128)
    hs, w_out_p, w_down_p = _s5_scan(sloc, a16s, ncc, [late[0], late[2]], (LATE_KINDS[0], LATE_KINDS[2]))
    y_all = _s5_from_states(p3, gg, hs, yw, False, "s5_out").reshape(la, S5_WIDTH)
    tabs = _ret_tables(ld2)
    of, ob, ssf, ssb = _ret_scan(p_all, cos_t, sin_t, tabs, nrc)
    wb = {**wb, **dict(zip(LATE_NAMES, _gather_sibling([w_out_p, w_up_p, w_down_p], LATE_KINDS, "gather_sibling_late")))}
    x1, mix, h2b, up, mixb, y_ret = _outproj_up(x, y_all, of, ob, p_all, wb["s5_w_glu"], b_glu, wb["w_out"],
                                                     mod3, n2w, wb["w_up"], nct)
    act, dx2, ddn, dact, acc_f = _ffn_loss(up, x1, conv_w, conv_b, wb["w_down"], gate5, fnw, tgt)

    g = {}
    g["w_down"] = _mm_tn(act, ddn, name="gw_down")
    dup, acc_c = _convglu_bwd(up, dact, conv_w, conv_b)
    g["w_up"] = _mm_tn(h2b, dup, name="gw_up")
    first = [g[n] for n in FIRST_GRADS]
    first_pairs = [_pair_sum(gf, rv, k, ci, "rs_pair_" + n)
                   for gf, rv, k, n in zip(first, _rs_sibling(first, FIRST_KINDS, "rs_sibling_first"), FIRST_KINDS, FIRST_GRADS)]
    dx1, dmixb, acc_2, dy3, dy_ret, dg, g["s5_w_glu"], g["s5_b_glu"], *first_got = _up_bwd(
        dup, wb["w_up"], wb["w_out"], x1, dx2, mix, mod3, n2w, y_all, y_ret, p_all, wb["s5_w_glu"], b_glu,
        jnp.zeros(p3.shape, BF16), nct, first_pairs, FIRST_KINDS)
    g["w_out"] = _mm_tn(mixb, dmixb, name="gw_out")
    second = [g[n] for n in SECOND_GRADS]
    second_pairs = [_pair_sum(gf, rv, k, ci, "rs_pair_" + n) for gf, rv, k, n in zip(
        second, _rs_sibling(second, SECOND_KINDS, "rs_sibling_second"), SECOND_KINDS, SECOND_GRADS)]

    e = _s5_to_states(dy3, yw, "s5_bwd_h")
    ds, da16 = _s5_scan_bwd(e, hs, a16s, ncc)
    du = _s5_from_states(dy3, gg, ds, xw, True, "s5_bwd_u").reshape(la, S5_WIDTH)
    dkb = _s5_bwd_kb(p3, dy3)
    dwst = _s5_bwd_w(p3, ds, "s5_bwd_wst")
    dwout = _s5_bwd_w(dy3, hs, "s5_bwd_wout")
    da16p = da16.reshape(2, S5_NB, 8, 128).transpose(1, 0, 2, 3)
    g.update(_s5_unpack(_s5_gen_bwd(*s5p, dkb, dwst, dwout, da16p)))

    dqf, dkf, dvf, dqb, dkb_, dvb, dld = _ret_scan_bwd(p_all, cos_t, sin_t, tabs, ssf, ssb, dy_ret, nrc)
    g["ret_log_decay_f"] = dld[0, :, 0, 0].reshape(1, RET_HEADS)
    g["ret_log_decay_b"] = dld[1, :, 0, 0].reshape(1, RET_HEADS)
    dp, grad_x, acc_1, *second_got = _in_bwd(dqf, dkf, dvf, dqb, dkb_, dvb, du, dg, cos_t, sin_t, wb["w_in"], x, ctx,
                                             n1w, mod4, dx1, second_pairs, SECOND_KINDS)
    g["w_in"] = _mm_tn(h1b, dp, name="gw_in")

    g["norm1_w"], g["norm2_w"], g["final_norm_w"] = acc_1[0:1], acc_2[0:1], acc_f[0]
    g["conv_w"], g["conv_b"] = acc_c[0:3], acc_c[3:4]
    zero = jnp.zeros((1, D_MODEL), F32)
    dmx = jnp.concatenate([acc_1[3:5], acc_2[1:2], acc_2[2:4], acc_f[1:2]], axis=0)
    dmc = jnp.concatenate([acc_1[1:3], zero, zero, zero, zero], axis=0)
    return acc_f[2, 0], grad_x, g, dmx, dmc, first_pairs + second_pairs, list(first_got) + list(second_got)


WEIGHT_NAMES = ("c_ctx", "w_mod", "b_mod", "norm1_w", "w_in", "s5_lambda_re_f", "s5_lambda_im_f", "s5_log_step_f",
                "s5_lambda_re_b", "s5_lambda_im_b", "s5_log_step_b", "s5_b_re", "s5_b_im", "s5_c_re", "s5_c_im",
                "s5_d", "s5_w_glu", "s5_b_glu", "ret_log_decay_f", "ret_log_decay_b", "w_out", "norm2_w", "w_up",
                "conv_w", "conv_b", "w_down", "final_norm_w")
BIG_NAMES = ("w_in", "w_out", "w_up", "w_down", "s5_w_glu")
BIG_KINDS = ("col", "row", "col", "row", "row")
EARLY_NAMES, EARLY_KINDS = ("w_in", "s5_w_glu"), ("col", "row")
LATE_NAMES, LATE_KINDS = ("w_out", "w_up", "w_down"), ("row", "col", "row")
FIRST_GRADS, FIRST_KINDS = ("w_down", "w_up"), ("row", "col")
SECOND_GRADS, SECOND_KINDS = ("w_out", "s5_w_glu"), ("row", "row")
LAST_GRADS, LAST_KINDS = ("w_in",), ("col",)
SMALL_NAMES = ("norm1_w", "norm2_w", "final_norm_w", "conv_b", "conv_w", "s5_lambda_re_f", "s5_lambda_im_f",
               "s5_log_step_f", "s5_lambda_re_b", "s5_lambda_im_b", "s5_log_step_b", "s5_b_re", "s5_b_im", "s5_c_re",
               "s5_c_im", "s5_d", "s5_b_glu", "ret_log_decay_f", "ret_log_decay_b")
ROW = 1024
N_CHIPS = 4


def _pack_rows(parts):
    flat = jnp.concatenate([p.reshape(-1) for p in parts])
    n = flat.shape[0]
    rows = -(-n // (8 * ROW)) * 8
    return jnp.pad(flat, (0, rows * ROW - n)).reshape(rows, ROW)


def _unpack_rows(packed, shapes):
    flat = packed.reshape(-1)
    out, off = [], 0
    for s in shapes:
        n = math.prod(s)
        out.append(flat[off:off + n].reshape(s))
        off += n
    return out


def _step(a):
    xi, yi, ci = _mesh_pos()
    chip = 2 * xi + yi
    dev = 2 * chip + ci

    cw_loc = a["conv_w"].reshape(-1)
    small_in = jnp.concatenate([a["c"].reshape(-1), jnp.pad(cw_loc, (0, 24 * 128 - cw_loc.shape[0]))]).reshape(32, 128)
    sg = _all_gather8(small_in, "gather_cond").reshape(8, 32, 128)
    c_all = sg[:, 0:8].reshape(8, D_MODEL)
    conv_w = sg[0::2, 8:32].reshape(N_CHIPS, -1)[:, :cw_loc.shape[0]].reshape(N_CHIPS, 3, -1)
    conv_w = conv_w.transpose(1, 0, 2).reshape(3, D_FF)

    placed = {n: _place_shard(a[n][0], k, chip, "place_" + n) for n, k in zip(BIG_NAMES, BIG_KINDS)}
    early = [placed[n] for n in EARLY_NAMES]
    late = [placed[n] for n in LATE_NAMES]

    w_mod_b = a["w_mod"][0].astype(BF16)
    c_ctx = a["c_ctx"].reshape(1, D_MODEL)
    b_loc = lax.dynamic_slice_in_dim(a["b_mod"], chip * MOD_COLS, MOD_COLS, 1)
    m_loc, s_b = _mod_fwd(c_all, c_ctx, w_mod_b, b_loc)
    mg = _all_gather8(m_loc, "gather_mod").reshape(8, MOD_ROWS, MOD_COLS)
    m_full = mg[0::2].transpose(1, 0, 2).reshape(MOD_ROWS, 6 * D_MODEL)
    mx = lax.dynamic_slice_in_dim(m_full, dev, 1, 0).reshape(6, D_MODEL)
    mc = m_full[8].reshape(6, D_MODEL)

    loss_part, grad_x, g, dmx, dmc, first_pairs, first_got = _local_step(a, early, late, mx, mc, conv_w, ci)
    loss = lax.psum(loss_part, ("x", "y", "c"))

    dm_pair = jnp.concatenate([dmx.reshape(1, -1), dmc.reshape(1, -1), jnp.zeros((6, 6 * D_MODEL), F32)], axis=0)
    dm_all = _all_gather8(dm_pair, "gather_dmod").reshape(8, 8, 6 * D_MODEL)
    dm16, gb_mod = _mod_bwd_sum(dm_all)
    dm_loc = lax.dynamic_slice_in_dim(dm16, chip * MOD_COLS, MOD_COLS, 1)
    gw_mod, gcc = _mod_bwd_w(dm_loc, s_b, c_ctx, w_mod_b)

    small_parts = [g[n] for n in SMALL_NAMES] + [gcc[0]]
    small_shapes = [p.shape for p in small_parts]
    sp = _pack_rows(small_parts)
    tot = _sum_slots(_all_gather8(sp, "gather_small_grads").reshape(8, sp.shape[0], ROW), "sum_small_grads")
    small = dict(zip(SMALL_NAMES + ("c_ctx",), _unpack_rows(tot, small_shapes)))
    grads = {n: small[n].reshape(a[n].shape) for n in SMALL_NAMES if n != "conv_w"}
    grads["c_ctx"] = (0.5 * small["c_ctx"]).reshape(a["c_ctx"].shape)
    grads["conv_w"] = lax.dynamic_slice_in_dim(small["conv_w"], chip * (D_FF // N_CHIPS), D_FF // N_CHIPS, 1)[None]
    grads["b_mod"] = gb_mod
    grads["w_mod"] = gw_mod[None]

    last = [g[n] for n in LAST_GRADS]
    last_pairs = [_pair_sum(gf, rv, k, ci, "rs_pair_" + n)
                  for gf, rv, k, n in zip(last, _rs_sibling(last, LAST_KINDS, "rs_sibling_last"), LAST_KINDS, LAST_GRADS)]
    last_got = _rs_chips(last_pairs, LAST_KINDS)
    pos = jnp.stack([ci, chip])
    order = FIRST_GRADS + SECOND_GRADS + LAST_GRADS
    order_kinds = FIRST_KINDS + SECOND_KINDS + LAST_KINDS
    halves = [_sum_chips(p, t, k, pos, "rs_sum_" + n)
              for p, t, k, n in zip(first_pairs + last_pairs, list(first_got) + list(last_got), order_kinds, order)]
    for n, t in zip(order, _rs_back(halves, order_kinds)):
        grads[n] = t[None]

    delta, new_m, new_v = {}, {}, {}
    for n in BIG_NAMES + ("w_mod",):
        for dst, t in zip((delta, new_m, new_v), _adamw(a[n][0], grads[n][0], a["m_" + n][0], a["v_" + n][0], "adamw_" + n)):
            dst[n] = t[None]
    rest = [n for n in WEIGHT_NAMES if n not in BIG_NAMES and n != "w_mod"]
    shapes = [a[n].shape for n in rest]
    pr = lambda pre: _pack_rows([a[pre + n] for n in rest])
    for dst, t in zip((delta, new_m, new_v),
                      _adamw(pr(""), _pack_rows([grads[n] for n in rest]), pr("m_"), pr("v_"), "adamw_small")):
        dst.update(zip(rest, _unpack_rows(t, shapes)))

    return (loss, grad_x[None], *[grads[n] for n in WEIGHT_NAMES], *[delta[n] for n in WEIGHT_NAMES],
            *[new_m[n] for n in WEIGHT_NAMES], *[new_v[n] for n in WEIGHT_NAMES])


def kernel(x, c, ctx, c_ctx, w_mod, b_mod, norm1_w, w_in, s5_lambda_re_f, s5_lambda_im_f, s5_log_step_f, s5_lambda_re_b, s5_lambda_im_b, s5_log_step_b, s5_b_re, s5_b_im, s5_c_re, s5_c_im, s5_d, s5_w_glu, s5_b_glu, ret_log_decay_f, ret_log_decay_b, w_out, norm2_w, w_up, conv_w, conv_b, w_down, final_norm_w, loss_target, m_c_ctx, m_w_mod, m_b_mod, m_norm1_w, m_w_in, m_s5_lambda_re_f, m_s5_lambda_im_f, m_s5_log_step_f, m_s5_lambda_re_b, m_s5_lambda_im_b, m_s5_log_step_b, m_s5_b_re, m_s5_b_im, m_s5_c_re, m_s5_c_im, m_s5_d, m_s5_w_glu, m_s5_b_glu, m_ret_log_decay_f, m_ret_log_decay_b, m_w_out, m_norm2_w, m_w_up, m_conv_w, m_conv_b, m_w_down, m_final_norm_w, v_c_ctx, v_w_mod, v_b_mod, v_norm1_w, v_w_in, v_s5_lambda_re_f, v_s5_lambda_im_f, v_s5_log_step_f, v_s5_lambda_re_b, v_s5_lambda_im_b, v_s5_log_step_b, v_s5_b_re, v_s5_b_im, v_s5_c_re, v_s5_c_im, v_s5_d, v_s5_w_glu, v_s5_b_glu, v_ret_log_decay_f, v_ret_log_decay_b, v_w_out, v_norm2_w, v_w_up, v_conv_w, v_conv_b, v_w_down, v_final_norm_w):
    return _step(dict(locals()))
```

```python
import functools
import math

import jax
import jax.numpy as jnp
from jax import lax
from jax.experimental import pallas as pl
from jax.experimental.pallas import tpu as pltpu

F32 = jnp.float32
BF16 = jnp.bfloat16

D_MODEL = 1024
S5_WIDTH = 512
S5_GROUPS = 32
S5_GROUP = 16
S5_STATE = 64
RET_WIDTH = 512
RET_HEADS = 4
RET_DH = 128
RET_CHUNK = 256
GRID_W = 64
ROPE_THETA = 10000.0
D_FF = 2816
NORM_EPS = 1e-6
IN_COLS = S5_WIDTH + 4 * RET_WIDTH

S5_T = 16
S5_NB = 4
S5_BW = S5_T * 128
S5_SW = 8 * 2 * S5_STATE

ADAM_LR, ADAM_B1, ADAM_B2, ADAM_EPS, ADAM_WD, ADAM_STEP = 0.001, 0.9, 0.999, 1e-08, 0.01, 10

VMEM_LIMIT = 56 * 1024 * 1024
MM_TN_VMEM = 40 * 1024 * 1024
MESH_ID = pl.DeviceIdType.MESH


def _params(sem=None):
    return pltpu.CompilerParams(dimension_semantics=sem, vmem_limit_bytes=VMEM_LIMIT)


def _full(shape):
    n = len(shape)
    return pl.BlockSpec(shape, lambda *_: (0,) * n)


def _dot(a, b):
    return jnp.dot(a, b, preferred_element_type=F32)


def _dot_nt(a, b):
    return lax.dot_general(a, b, (((1,), (1,)), ((), ())), preferred_element_type=F32)


def _dot_tn(a, b):
    return lax.dot_general(a, b, (((0,), (0,)), ((), ())), preferred_element_type=F32)


def _dot_hi(a, b):
    return jnp.dot(a, b, preferred_element_type=F32, precision=lax.Precision.HIGHEST)


def _dot_nt_hi(a, b):
    return lax.dot_general(a, b, (((1,), (1,)), ((), ())), preferred_element_type=F32,
                           precision=lax.Precision.HIGHEST)


def _gelu(x):
    return 0.5 * x * (1.0 + jnp.tanh(0.7978845608028654 * (x + 0.044715 * (x * x * x))))


def _gelu_and_grad(x):
    c, ca = 0.7978845608028654, 0.7978845608028654 * 0.044715
    x2 = x * x
    t = jnp.tanh(x * (c + ca * x2))
    h = 0.5 * x
    return h + h * t, 0.5 + 0.5 * t + h * (1.0 - t * t) * (c + 3.0 * ca * x2)


def _sigmoid(x):
    return 1.0 / (1.0 + jnp.exp(-x))


def _silu(x):
    return x * _sigmoid(x)


def _rms_mod(x, nw, sh, sc):
    r = lax.rsqrt(jnp.mean(x * x, axis=-1, keepdims=True) + NORM_EPS)
    return (x * r * nw) * (1.0 + sc) + sh


def _rms(x, nw):
    r = lax.rsqrt(jnp.mean(x * x, axis=-1, keepdims=True) + NORM_EPS)
    return x * r * nw


def _head_norm_gate(y, g):
    mu = jnp.mean(y, axis=-1, keepdims=True)
    yc = y - mu
    var = jnp.mean(yc * yc, axis=-1, keepdims=True)
    return _silu(g) * (yc * lax.rsqrt(var + NORM_EPS))


def _swap_pairs(t):
    lane = lax.broadcasted_iota(jnp.int32, t.shape, 1)
    return jnp.where(lane % 2 == 0, pltpu.roll(t, RET_DH - 1, 1), pltpu.roll(t, 1, 1))


def _rope(t, cos_t, sin_t):
    return t * cos_t + _swap_pairs(t) * sin_t


def _rope_t(dt, cos_t, sin_t):
    return dt * cos_t + _swap_pairs(dt * sin_t)


def _pick(n, prefs):
    for p in prefs:
        if n % p == 0:
            return p
    return n


def _mm_tn(a, b, *, name):
    m, k = a.shape
    n = b.shape[1]
    tn = _pick(n, (1408, 1024, 1280, 512))
    fits = lambda t: 2 * (2 * t * k + 2 * t * tn + 4 * k * tn) <= MM_TN_VMEM
    tm = _pick(m, [t for t in (2816, 2048, 1024, 768, 512, 256) if fits(t)] + [128])

    def body(a_ref, b_ref, o_ref):
        @pl.when(pl.program_id(1) == 0)
        def _():
            o_ref[...] = jnp.zeros_like(o_ref)
        o_ref[...] += _dot_tn(a_ref[...], b_ref[...])

    return pl.pallas_call(
        body, name=name, grid=(n // tn, m // tm),
        in_specs=[pl.BlockSpec((tm, k), lambda j, i: (i, 0)), pl.BlockSpec((tm, tn), lambda j, i: (i, j))],
        out_specs=pl.BlockSpec((k, tn), lambda j, i: (0, j)),
        out_shape=jax.ShapeDtypeStruct((k, n), F32),
        compiler_params=_params(("parallel", "arbitrary")),
    )(a, b)


TOK_TILE = 256


def _behind(step, last, copies):
    @pl.when(step == 0)
    def _():
        for cp in copies(with_arrivals=False)[0]:
            cp.start()

    @pl.when(step == last)
    def _():
        sends, arrivals = copies()
        for cp in arrivals:
            cp.wait_recv()
        for cp in sends:
            cp.wait_send()


def _s5_put_rows(rows_ref, scr, val):
    nchunk = scr.shape[0]
    for c in range(nchunk):
        scr[c] = val[S5_T * c:S5_T * (c + 1), :]
    for b in range(S5_NB):
        for j in range(S5_T):
            rows_ref[b, :, 128 * j:128 * (j + 1)] = scr[:, j, 128 * b:128 * (b + 1)].astype(BF16)


def _norm_inproj(x, ctx, n1w, mod4, w_in_b, cos_t, sin_t, placed, kinds):
    l, lc = x.shape[0], ctx.shape[0]
    tm = TOK_TILE
    nct = lc // tm
    la = l + lc
    n = len(placed)
    shard_shapes = _gather_shard_shapes(placed, kinds)

    def body(x_ref, c_ref, nw_ref, mod_ref, w_ref, cos_ref, sin_ref, *rest):
        p_ref, h_ref, u_ref = rest[n:n + 3]
        send_sems, recv_sems, u_scr = rest[2 * n + 3:]
        _behind(pl.program_id(0), la // tm - 1,
                functools.partial(_gather_chip_copies, rest[n + 3:2 * n + 3], kinds, shard_shapes, send_sems, recv_sems))
        is_ctx = pl.program_id(0) < nct
        xt = jnp.where(is_ctx, c_ref[...], x_ref[...])
        sh = jnp.where(is_ctx, mod_ref[0:1, :], mod_ref[2:3, :])
        sc = jnp.where(is_ctx, mod_ref[1:2, :], mod_ref[3:4, :])
        hb = _rms_mod(xt, nw_ref[...], sh, sc).astype(BF16)
        h_ref[...] = hb
        p = _dot(hb, w_ref[...])
        p_ref[...] = p
        cs, sn = cos_ref[...], sin_ref[...]
        for h in range(RET_HEADS):
            q_cols = slice(RET_WIDTH + RET_DH * h, RET_WIDTH + RET_DH * (h + 1))
            k_cols = slice(2 * RET_WIDTH + RET_DH * h, 2 * RET_WIDTH + RET_DH * (h + 1))
            p_ref[:, q_cols] = _rope(p[:, q_cols], cs, sn)
            p_ref[:, k_cols] = _rope(p[:, k_cols] * K_SCALE, cs, sn)
        _s5_put_rows(u_ref, u_scr, p[:, 0:S5_WIDTH])

    return pl.pallas_call(
        body, name="norm_inproj", grid=(la // tm,),
        in_specs=[pl.BlockSpec((tm, D_MODEL), lambda i: (jnp.maximum(i - nct, 0), 0)),
                  pl.BlockSpec((tm, D_MODEL), lambda i: (jnp.minimum(i, nct - 1), 0)),
                  _full((1, D_MODEL)), _full((4, D_MODEL)), _full((D_MODEL, IN_COLS)),
                  pl.BlockSpec((tm, RET_DH), lambda i: (i, 0)), pl.BlockSpec((tm, RET_DH), lambda i: (i, 0))] + [ANY] * n,
        out_specs=[pl.BlockSpec((tm, IN_COLS), lambda i: (i, 0)), pl.BlockSpec((tm, D_MODEL), lambda i: (i, 0)),
                   pl.BlockSpec((S5_NB, tm // S5_T, S5_BW), lambda i: (0, i, 0))] + [ANY] * n,
        out_shape=[jax.ShapeDtypeStruct((la, IN_COLS), F32), jax.ShapeDtypeStruct((la, D_MODEL), BF16),
                   jax.ShapeDtypeStruct((S5_NB, la // S5_T, S5_BW), BF16)]
        + [jax.ShapeDtypeStruct(p.shape, p.dtype) for p in placed],
        input_output_aliases={7 + a: 3 + a for a in range(n)},
        scratch_shapes=[pltpu.SemaphoreType.DMA((n, 3)), pltpu.SemaphoreType.DMA((n, 3)),
                        pltpu.VMEM((tm // S5_T, S5_T, S5_WIDTH), F32)],
        compiler_params=_params(("arbitrary",)),
    )(x, ctx, n1w, mod4, w_in_b, cos_t, sin_t, *placed)


def _iota2(shape, dim):
    return lax.broadcasted_iota(jnp.int32, shape, dim)


def _group_mask(rows, cols, row_div, col_div):
    return jnp.where(_iota2((rows, cols), 0) // row_div == _iota2((rows, cols), 1) // col_div, 1.0, 0.0).astype(F32)


def _s5_gen_dir(lre, lim, lst, b_re, b_im, c_re, c_im):
    step = jnp.exp(lst)
    mag = jnp.exp(lre * step)
    ar = mag * jnp.cos(lim * step)
    ai = mag * jnp.sin(lim * step)
    den = lre * lre + lim * lim
    xr = ar - 1.0
    cr = (xr * lre + ai * lim) / den
    ci = (ai * lre - xr * lim) / den
    rexp = _group_mask(128, 8, S5_GROUP, 1)
    are, aie = _dot_hi(rexp, ar), _dot_hi(rexp, ai)
    cre, cie = _dot_hi(rexp, cr), _dot_hi(rexp, ci)
    bbr = cre * b_re - cie * b_im
    bbi = cre * b_im + cie * b_re
    gmask = _group_mask(128, 128, S5_GROUP, S5_GROUP)
    pr, pi = jnp.ones_like(are), jnp.zeros_like(are)
    xs, ys = [], []
    for t in range(S5_T + 1):
        if t < S5_T:
            xs.append(jnp.concatenate([bbr * pr - bbi * pi, bbr * pi + bbi * pr], axis=1))
        ys.append(jnp.concatenate([c_re * pr - c_im * pi, -(c_re * pi + c_im * pr)], axis=1))
        pr, pi = pr * are - pi * aie, pr * aie + pi * are
    gs = [_dot_nt_hi(x_t, ys[0]) * gmask for x_t in xs]
    r16, i16 = ar, ai
    for _ in range(4):
        r16, i16 = r16 * r16 - i16 * i16, 2.0 * r16 * i16
    return xs, ys, gs, jnp.concatenate([r16, i16], axis=1)


def _s5_expand(z):
    return jnp.concatenate([z] * 8, axis=1) * _group_mask(128, S5_SW, S5_GROUP, 128)


def _s5_contract(z):
    zm = z * _group_mask(128, S5_SW, S5_GROUP, 128)
    acc = zm[:, 0:128]
    for k in range(1, 8):
        acc = acc + zm[:, 128 * k:128 * (k + 1)]
    return acc


def _s5_param_specs():
    blk3 = lambda r, c: pl.BlockSpec((1, 1, r, c), lambda b, *_: (0, b, 0, 0))
    dir3 = lambda r, c: pl.BlockSpec((2, 1, r, c), lambda b, *_: (0, b, 0, 0))
    return [dir3(8, S5_STATE), dir3(8, S5_STATE), dir3(8, 1), blk3(128, S5_STATE), blk3(128, S5_STATE),
            blk3(128, S5_STATE), blk3(128, S5_STATE), blk3(1, 128)]


def _s5_gen(lre, lim, lst, b_re, b_im, c_re, c_im, dvec, placed, kinds):
    n = len(placed)
    shard_shapes = _gather_shard_shapes(placed, kinds)

    def body(lre_ref, lim_ref, lst_ref, bre_ref, bim_ref, cre_ref, cim_ref, d_ref, *rest):
        gg_ref, xw_ref, yw_ref, a16_ref = rest[n:n + 4]
        _behind(pl.program_id(0), S5_NB - 1,
                functools.partial(_gather_chip_copies, rest[n + 4:2 * n + 4], kinds, shard_shapes, *rest[2 * n + 4:]))
        eye = _group_mask(128, 128, 1, 1)
        g0 = eye * d_ref[0, 0]
        for dr in range(2):
            xs, ys, gs, a16 = _s5_gen_dir(lre_ref[dr, 0], lim_ref[dr, 0], lst_ref[dr, 0], bre_ref[0, 0],
                                          bim_ref[0, 0], cre_ref[0, 0], cim_ref[0, 0])
            a16_ref[0, dr] = a16
            for j in range(S5_T):
                xw_ref[0, dr, j] = xs[S5_T - 1 - j if dr == 0 else j]
                yw_ref[0, dr, j] = ys[j + 1 if dr == 0 else S5_T - j]
            g0 = g0 + gs[0]
            for t in range(1, S5_T):
                gg_ref[0, (S5_T - 1) + t if dr == 0 else (S5_T - 1) - t] = gs[t]
        gg_ref[0, S5_T - 1] = g0

    blk = pl.BlockSpec((1, 2, S5_T, 128, 128), lambda b: (b, 0, 0, 0, 0))
    return pl.pallas_call(
        body, name="s5_gen", grid=(S5_NB,),
        in_specs=_s5_param_specs() + [ANY] * n,
        out_specs=[pl.BlockSpec((1, 2 * S5_T - 1, 128, 128), lambda b: (b, 0, 0, 0)), blk, blk,
                   pl.BlockSpec((1, 2, 8, 128), lambda b: (b, 0, 0, 0))] + [ANY] * n,
        out_shape=[jax.ShapeDtypeStruct((S5_NB, 2 * S5_T - 1, 128, 128), F32),
                   jax.ShapeDtypeStruct((S5_NB, 2, S5_T, 128, 128), F32),
                   jax.ShapeDtypeStruct((S5_NB, 2, S5_T, 128, 128), F32),
                   jax.ShapeDtypeStruct((S5_NB, 2, 8, 128), F32)]
        + [jax.ShapeDtypeStruct(p.shape, p.dtype) for p in placed],
        input_output_aliases={8 + a: 4 + a for a in range(n)},
        scratch_shapes=[pltpu.SemaphoreType.DMA((n, 3)), pltpu.SemaphoreType.DMA((n, 3))],
        compiler_params=_params(("arbitrary",)),
    )(lre, lim, lst, b_re, b_im, c_re, c_im, dvec, *placed)


def _s5_fill_state_mat(w_scr, src_ref, dr):
    for j in range(S5_T):
        w_scr[128 * j:128 * (j + 1), :] = _s5_expand(src_ref[0, dr, j]).astype(BF16)


def _s5_fill_toeplitz(k_scr, gg_ref):
    for j in range(S5_T):
        for i in range(S5_T):
            k_scr[128 * j:128 * (j + 1), 128 * i:128 * (i + 1)] = gg_ref[0, i - j + (S5_T - 1)].astype(BF16)


S5_GEN_SPECS = [pl.BlockSpec((1, 2 * S5_T - 1, 128, 128), lambda b: (b, 0, 0, 0)),
                pl.BlockSpec((1, 2, S5_T, 128, 128), lambda b: (b, 0, 0, 0, 0))]


def _s5_gen_bwd(lre, lim, lst, b_re, b_im, c_re, c_im, dvec, dg, dx, dy, da16):
    def body(lre_ref, lim_ref, lst_ref, bre_ref, bim_ref, cre_ref, cim_ref, d_ref, dg_ref, dx_ref, dy_ref, da16_ref,
             glre_ref, glim_ref, glst_ref, gbre_ref, gbim_ref, gcre_ref, gcim_ref, gd_ref):
        eye = _group_mask(128, 128, 1, 1)
        gd_ref[0, 0] = jnp.sum(dg_ref[0, S5_T - 1] * eye, axis=0, keepdims=True)
        gb = [None, None, None, None]
        for dr in range(2):
            args = (lre_ref[dr, 0], lim_ref[dr, 0], lst_ref[dr, 0], bre_ref[0, 0], bim_ref[0, 0],
                    cre_ref[0, 0], cim_ref[0, 0])
            _, vjp = jax.vjp(_s5_gen_dir, *args)
            dxs = [dx_ref[0, dr, S5_T - 1 - t if dr == 0 else t] for t in range(S5_T)]
            dys = [jnp.zeros((128, 128), F32)] + [dy_ref[0, dr, t - 1 if dr == 0 else S5_T - t]
                                                  for t in range(1, S5_T + 1)]
            dgs = [dg_ref[0, (S5_T - 1) + t if dr == 0 else (S5_T - 1) - t] for t in range(S5_T)]
            g = vjp((dxs, dys, dgs, da16_ref[0, dr]))
            glre_ref[dr, 0] = g[0]
            glim_ref[dr, 0] = g[1]
            glst_ref[dr, 0] = g[2]
            for q in range(4):
                gb[q] = g[3 + q] if gb[q] is None else gb[q] + g[3 + q]
        gbre_ref[0, 0] = gb[0]
        gbim_ref[0, 0] = gb[1]
        gcre_ref[0, 0] = gb[2]
        gcim_ref[0, 0] = gb[3]

    shp = lambda a: jax.ShapeDtypeStruct(a.shape, F32)
    return pl.pallas_call(
        body, name="s5_gen_bwd", grid=(S5_NB,),
        in_specs=_s5_param_specs() + [
            pl.BlockSpec((1, 2 * S5_T - 1, 128, 128), lambda b: (b, 0, 0, 0)),
            pl.BlockSpec((1, 2, S5_T, 128, 128), lambda b: (b, 0, 0, 0, 0)),
            pl.BlockSpec((1, 2, S5_T, 128, 128), lambda b: (b, 0, 0, 0, 0)),
            pl.BlockSpec((1, 2, 8, 128), lambda b: (b, 0, 0, 0))],
        out_specs=_s5_param_specs(),
        out_shape=[shp(lre), shp(lim), shp(lst), shp(b_re), shp(b_im), shp(c_re), shp(c_im), shp(dvec)],
        compiler_params=_params(("parallel",)),
    )(lre, lim, lst, b_re, b_im, c_re, c_im, dvec, dg, dx, dy, da16)


def _s5_put_groups(o_ref, dr, val):
    for gi in range(8):
        o_ref[dr, :, gi, :] = val[:, 128 * gi:128 * (gi + 1)]


def _s5_get_groups(s_ref, dr, n=8):
    return jnp.concatenate([s_ref[dr, :, gi, :] for gi in range(n)], axis=1).astype(BF16)


def _s5_to_states(u3, blocks, name):
    cn = u3.shape[1]

    def body(u_ref, b_ref, o_ref, w_scr):
        u = u_ref[0]
        for dr in range(2):
            _s5_fill_state_mat(w_scr, b_ref, dr)
            _s5_put_groups(o_ref, dr, _dot(u, w_scr[...]))

    return pl.pallas_call(
        body, name=name, grid=(S5_NB,),
        in_specs=[pl.BlockSpec((1, cn, S5_BW), lambda b: (b, 0, 0)), S5_GEN_SPECS[1]],
        out_specs=pl.BlockSpec((2, cn, 8, 128), lambda b: (0, 0, b, 0)),
        out_shape=jax.ShapeDtypeStruct((2, cn, S5_GROUPS, 128), F32),
        scratch_shapes=[pltpu.VMEM((S5_BW, S5_SW), BF16)],
        compiler_params=_params(("parallel",)),
    )(u3, blocks)


def _s5_from_states(u3, gg, st, blocks, transposed, name):
    cn = u3.shape[1]

    def body(u_ref, g_ref, s_ref, b_ref, o_ref, k_scr, w_scr):
        u = u_ref[0]
        _s5_fill_toeplitz(k_scr, g_ref)
        y = _dot_nt(u, k_scr[...]) if transposed else _dot(u, k_scr[...])
        for dr in range(2):
            _s5_fill_state_mat(w_scr, b_ref, dr)
            y = y + _dot_nt(_s5_get_groups(s_ref, dr), w_scr[...])
        for i in range(S5_T):
            o_ref[:, i, :] = y[:, 128 * i:128 * (i + 1)]

    return pl.pallas_call(
        body, name=name, grid=(S5_NB,),
        in_specs=[pl.BlockSpec((1, cn, S5_BW), lambda b: (b, 0, 0)), S5_GEN_SPECS[0],
                  pl.BlockSpec((2, cn, 8, 128), lambda b: (0, 0, b, 0)), S5_GEN_SPECS[1]],
        out_specs=pl.BlockSpec((cn, S5_T, 128), lambda b: (0, 0, b)),
        out_shape=jax.ShapeDtypeStruct((cn, S5_T, S5_WIDTH), F32),
        scratch_shapes=[pltpu.VMEM((S5_BW, S5_BW), BF16), pltpu.VMEM((S5_BW, S5_SW), BF16)],
        compiler_params=_params(("parallel",)),
    )(u3, gg, st, blocks)


def _s5_a_forms(a):
    ra = pltpu.roll(a, S5_STATE, 1)
    low = _iota2(a.shape, 1) < S5_STATE
    return jnp.where(low, a, ra), jnp.where(low, -ra, a)


def _s5_scan(sloc, a16, ncc, placed, kinds):
    cn = sloc.shape[1]
    n = len(placed)
    shard_shapes = _gather_shard_shapes(placed, kinds)

    def body(s_ref, a_ref, *rest):
        h_ref = rest[n]
        sends, arrivals = _gather_chip_copies(rest[n + 1:2 * n + 1], kinds, shard_shapes, *rest[2 * n + 1:])
        for cp in sends:
            cp.start()
        forms = [_s5_a_forms(a_ref[dr]) for dr in range(2)]

        def step(s, hs):
            out = []
            for dr in range(2):
                arr, aii = forms[dr]
                h, rh = hs[dr]
                c = s if dr == 0 else jnp.where(s < ncc, ncc - 1 - s, cn - 1 - (s - ncc))
                h_ref[dr, c] = h
                sc = s_ref[dr, c]
                out.append((h * arr + rh * aii + sc, rh * arr - h * aii + pltpu.roll(sc, S5_STATE, 1)))
            return tuple(out)

        zero = jnp.zeros((S5_GROUPS, 128), F32)
        lax.fori_loop(0, cn, step, ((zero, zero), (zero, zero)), unroll=4)
        for cp in arrivals:
            cp.wait_recv()
        for cp in sends:
            cp.wait_send()

    vmem = pl.BlockSpec(memory_space=pltpu.VMEM)
    return pl.pallas_call(
        body, name="s5_scan",
        in_specs=[vmem, vmem] + [ANY] * n, out_specs=[vmem] + [ANY] * n,
        out_shape=[jax.ShapeDtypeStruct(sloc.shape, F32)] + [jax.ShapeDtypeStruct(p.shape, p.dtype) for p in placed],
        input_output_aliases={2 + a: 1 + a for a in range(n)},
        scratch_shapes=[pltpu.SemaphoreType.DMA((n, 3)), pltpu.SemaphoreType.DMA((n, 3))],
        compiler_params=_params(),
    )(sloc, a16, *placed)


def _s5_scan_bwd(e, hs, a16, ncc):
    cn = e.shape[1]

    def body(e_ref, h_ref, a_ref, ds_ref, da_ref):
        forms = [_s5_a_forms(a_ref[dr]) for dr in range(2)]
        low = _iota2((S5_GROUPS, 128), 1) < S5_STATE

        def step(s, carry):
            out = []
            r = cn - 1 - s
            for dr in range(2):
                arr, aii = forms[dr]
                g, rg, da = carry[dr]
                c = r if dr == 0 else jnp.where(r < ncc, ncc - 1 - r, cn - 1 - (r - ncc))
                ds_ref[dr, c] = g
                h = h_ref[dr, c]
                rh = pltpu.roll(h, S5_STATE, 1)
                da = da + jnp.where(low, g * h + rg * rh, g * rh - rg * h)
                ec = e_ref[dr, c]
                out.append((ec + g * arr - rg * aii, pltpu.roll(ec, S5_STATE, 1) + rg * arr + g * aii, da))
            return tuple(out)

        zero = jnp.zeros((S5_GROUPS, 128), F32)
        res = lax.fori_loop(0, cn, step, ((zero, zero, zero), (zero, zero, zero)), unroll=4)
        da_ref[0] = res[0][2]
        da_ref[1] = res[1][2]

    return pl.pallas_call(
        body, name="s5_scan_bwd",
        out_shape=[jax.ShapeDtypeStruct(e.shape, F32), jax.ShapeDtypeStruct((2, S5_GROUPS, 128), F32)],
        compiler_params=_params(),
    )(e, hs, a16)


def _s5_bwd_kb(p3, dy3):
    cn = p3.shape[1]
    half = S5_T // 2

    def body(u_ref, d_ref, o_ref):
        q = pl.program_id(1)

        @pl.when(q == 0)
        def _():
            o_ref[...] = jnp.zeros_like(o_ref)

        dk = _dot_tn(u_ref[0], d_ref[0])
        for j in range(S5_T):
            for i in range(half):
                o_ref[0, half * q + i - j + (S5_T - 1)] += dk[128 * j:128 * (j + 1), 128 * i:128 * (i + 1)]

    return pl.pallas_call(
        body, name="s5_bwd_kb", grid=(S5_NB, 2),
        in_specs=[pl.BlockSpec((1, cn, S5_BW), lambda b, q: (b, 0, 0)),
                  pl.BlockSpec((1, cn, S5_BW // 2), lambda b, q: (b, 0, q))],
        out_specs=pl.BlockSpec((1, 2 * S5_T - 1, 128, 128), lambda b, q: (b, 0, 0, 0)),
        out_shape=jax.ShapeDtypeStruct((S5_NB, 2 * S5_T - 1, 128, 128), F32),
        compiler_params=_params(("parallel", "arbitrary")),
    )(p3, dy3)


def _s5_bwd_w(u3, st, name):
    cn = u3.shape[1]

    def body(u_ref, s_ref, w_ref):
        dw = _dot_tn(u_ref[0], _s5_get_groups(s_ref, 0))
        for j in range(S5_T):
            w_ref[0, 0, j] = _s5_contract(dw[128 * j:128 * (j + 1), :])

    return pl.pallas_call(
        body, name=name, grid=(S5_NB, 2),
        in_specs=[pl.BlockSpec((1, cn, S5_BW), lambda b, q: (b, 0, 0)),
                  pl.BlockSpec((1, cn, 8, 128), lambda b, q: (q, 0, b, 0))],
        out_specs=pl.BlockSpec((1, 1, S5_T, 128, 128), lambda b, q: (b, q, 0, 0, 0)),
        out_shape=jax.ShapeDtypeStruct((S5_NB, 2, S5_T, 128, 128), F32),
        compiler_params=_params(("parallel", "parallel")),
    )(u3, st)


K_SCALE = RET_DH ** -0.5
G_COL = 16


def _ret_chunk_of(step, ncc, nch, rev):
    if not rev:
        return step
    return jnp.where(step < ncc, ncc - 1 - step, nch - 1 - (step - ncc))


def _ret_decay(ld, rev):
    c = _iota2((RET_CHUNK, RET_CHUNK), 0).astype(F32)
    m = _iota2((RET_CHUNK, RET_CHUNK), 1).astype(F32)
    diff = (m - c) if rev else (c - m)
    keep = (diff > 0) if rev else (diff >= 0)
    expo = jnp.maximum(diff, 0.0)
    dm = jnp.where(keep, jnp.exp(ld * expo), 0.0)
    xi_e = (RET_CHUNK - c) if rev else (c + 1.0)
    zeta_e = c if rev else (RET_CHUNK - 1.0 - c)
    return dm, expo, jnp.exp(ld * xi_e), xi_e, jnp.exp(ld * zeta_e), zeta_e


RET_TABLES = 7


def _ret_tables(ld2):
    def body(ld_ref, t_ref):
        dr, h = pl.program_id(0), pl.program_id(1)
        ldh = ld_ref[dr, h]
        for rev in (False, True):
            @pl.when(dr == int(rev))
            def _(rev=rev):
                dm, expo, xi, xi_e, zeta, zeta_e = _ret_decay(ldh, rev)
                t_ref[0, 0, 0] = dm
                t_ref[0, 0, 1] = dm * expo
                t_ref[0, 0, 2] = xi
                t_ref[0, 0, 3] = xi * xi_e
                t_ref[0, 0, 4] = zeta
                t_ref[0, 0, 5] = zeta * zeta_e
                t_ref[0, 0, 6] = jnp.zeros_like(dm) + jnp.exp(ldh * RET_CHUNK)

    return pl.pallas_call(
        body, name="ret_tables", grid=(2, RET_HEADS),
        in_specs=[pl.BlockSpec(memory_space=pltpu.SMEM)],
        out_specs=pl.BlockSpec((1, 1, RET_TABLES, RET_CHUNK, RET_CHUNK), lambda d, h: (d, h, 0, 0, 0)),
        out_shape=jax.ShapeDtypeStruct((2, RET_HEADS, RET_TABLES, RET_CHUNK, RET_CHUNK), F32),
        compiler_params=_params(("parallel", "parallel")),
    )(ld2)


def _ret_specs(nch, ncc, rev, step_of):
    chunk = lambda n: _ret_chunk_of(step_of(n), ncc, nch, rev)
    cols = [pl.BlockSpec((RET_CHUNK, RET_WIDTH), functools.partial(lambda n, cb: (chunk(n), cb), cb=cb))
            for cb in (1, 2, 3)]
    tab = pl.BlockSpec((RET_CHUNK, RET_DH), lambda n: (chunk(n), 0))
    return cols + [tab, tab], pl.BlockSpec((RET_CHUNK, RET_WIDTH), lambda n: (chunk(n), 0))


def _ret_scan(p_all, cos_t, sin_t, tabs, ncc):
    la = p_all.shape[0]
    nch = la // RET_CHUNK

    def body(t_ref, qf, kf, vf, cf, sf, qb, kb, vb, cb, sb, of_ref, ob_ref, ssf_ref, ssb_ref, s_scr):
        @pl.when(pl.program_id(0) == 0)
        def _():
            s_scr[...] = jnp.zeros_like(s_scr)

        for dr, (q_ref, k_ref, v_ref, c_ref, n_ref, o_ref, ss_ref) in enumerate(
                ((qf, kf, vf, cf, sf, of_ref, ssf_ref), (qb, kb, vb, cb, sb, ob_ref, ssb_ref))):
            cs, sn = c_ref[...], n_ref[...]
            for h in range(RET_HEADS):
                sl = slice(RET_DH * h, RET_DH * (h + 1))
                dm, xi, zeta = t_ref[dr, h, 0], t_ref[dr, h, 2, :, 0:RET_DH], t_ref[dr, h, 4, :, 0:RET_DH]
                q, k = q_ref[:, sl], k_ref[:, sl]
                vh = v_ref[:, sl].astype(BF16)
                s = s_scr[dr, h]
                ss_ref[0, h] = s
                sc = (_dot_nt(q.astype(BF16), k.astype(BF16)) * dm).astype(BF16)
                o_ref[:, sl] = _dot(sc, vh) + _dot((q * xi).astype(BF16), s.astype(BF16))
                s_scr[dr, h] = t_ref[dr, h, 6, 0:RET_DH, 0:RET_DH] * s + _dot_tn((k * zeta).astype(BF16), vh)

    in_f, out_f = _ret_specs(nch, ncc, False, lambda n: n)
    in_b, out_b = _ret_specs(nch, ncc, True, lambda n: n)
    ss_spec = pl.BlockSpec((1, RET_HEADS, RET_DH, RET_DH), lambda n: (n, 0, 0, 0))
    o_shape = jax.ShapeDtypeStruct((la, RET_WIDTH), F32)
    ss_shape = jax.ShapeDtypeStruct((nch, RET_HEADS, RET_DH, RET_DH), F32)
    return pl.pallas_call(
        body, name="ret_scan", grid=(nch,),
        in_specs=[_full(tabs.shape)] + in_f + in_b,
        out_specs=[out_f, out_b, ss_spec, ss_spec],
        out_shape=[o_shape, o_shape, ss_shape, ss_shape],
        scratch_shapes=[pltpu.VMEM((2, RET_HEADS, RET_DH, RET_DH), F32)],
        compiler_params=_params(("arbitrary",)),
    )(tabs, p_all, p_all, p_all, cos_t, sin_t, p_all, p_all, p_all, cos_t, sin_t)


def _ret_scan_bwd(p_all, cos_t, sin_t, tabs, ssf, ssb, dy_all, ncc):
    la = p_all.shape[0]
    nch = la // RET_CHUNK

    def body(t_ref, qf, kf, vf, cf, sf, dof, ssf_ref, qb, kb, vb, cb, sb, dob_, ssb_ref,
             dqf, dkf, dvf, dqb, dkb, dvb, dld_ref, ds_scr):
        @pl.when(pl.program_id(0) == 0)
        def _():
            ds_scr[...] = jnp.zeros_like(ds_scr)
            dld_ref[...] = jnp.zeros_like(dld_ref)

        for dr, (q_ref, k_ref, v_ref, c_ref, n_ref, do_ref, ss_ref, dq_ref, dk_ref, dv_ref) in enumerate(
                ((qf, kf, vf, cf, sf, dof, ssf_ref, dqf, dkf, dvf), (qb, kb, vb, cb, sb, dob_, ssb_ref, dqb, dkb, dvb))):
            cs, sn = c_ref[...], n_ref[...]
            on_ctx = _ret_chunk_of(nch - 1 - pl.program_id(0), ncc, nch, dr == 1) < ncc
            for h in range(RET_HEADS):
                sl = slice(RET_DH * h, RET_DH * (h + 1))
                dm, dm_d = t_ref[dr, h, 0], t_ref[dr, h, 1]
                xi, xi_d, zeta, zeta_d = [t_ref[dr, h, t, :, 0:RET_DH] for t in (2, 3, 4, 5)]
                gc = t_ref[dr, h, 6, 0:RET_DH, 0:RET_DH]
                q, k = q_ref[:, sl], k_ref[:, sl]
                q16, k16, v16 = q.astype(BF16), k.astype(BF16), v_ref[:, sl].astype(BF16)
                s = ss_ref[0, h]
                s16 = s.astype(BF16)
                ds_in = ds_scr[dr, h]
                ds16 = ds_in.astype(BF16)
                do16 = jnp.where(on_ctx, 0.0, do_ref[:, sl]).astype(BF16)
                qk = _dot_nt(q16, k16)
                dsv = _dot_nt(do16, v16)
                dsc = (dsv * dm).astype(BF16)
                sc16 = (qk * dm).astype(BF16)
                dos = _dot_nt(do16, s16)
                vds = _dot_nt(v16, ds16)
                dq_ref[:, sl] = _dot(dsc, k16) + dos * xi
                dk_ref[:, sl] = _dot_tn(dsc, q16) + vds * zeta
                dv_ref[:, sl] = _dot_tn(sc16, do16) + _dot((k * zeta).astype(BF16), ds16)
                ds_scr[dr, h] = _dot_tn((q * xi).astype(BF16), do16) + gc * ds_in
                dld = (jnp.sum(dsv * qk * dm_d) + jnp.sum(q * dos * xi_d + k * vds * zeta_d)
                       + RET_CHUNK * jnp.sum(gc * s * ds_in))
                dld_ref[dr, h] += dld

    back = lambda n: nch - 1 - n
    in_f, out_f = _ret_specs(nch, ncc, False, back)
    in_b, out_b = _ret_specs(nch, ncc, True, back)
    ss_spec = pl.BlockSpec((1, RET_HEADS, RET_DH, RET_DH), lambda n: (nch - 1 - n, 0, 0, 0))
    shp = jax.ShapeDtypeStruct((la, RET_WIDTH), F32)
    dy_spec = lambda rev: pl.BlockSpec(
        (RET_CHUNK, RET_WIDTH), lambda n: (jnp.maximum(_ret_chunk_of(nch - 1 - n, ncc, nch, rev) - ncc, 0), 0))
    return pl.pallas_call(
        body, name="ret_scan_bwd", grid=(nch,),
        in_specs=[_full(tabs.shape)] + in_f + [dy_spec(False), ss_spec] + in_b + [dy_spec(True), ss_spec],
        out_specs=[out_f, out_f, out_f, out_b, out_b, out_b, _full((2, RET_HEADS, 8, 128))],
        out_shape=[shp] * 6 + [jax.ShapeDtypeStruct((2, RET_HEADS, 8, 128), F32)],
        scratch_shapes=[pltpu.VMEM((2, RET_HEADS, RET_DH, RET_DH), F32)],
        compiler_params=_params(("arbitrary",)),
    )(tabs, p_all, p_all, p_all, cos_t, sin_t, dy_all, ssf, p_all, p_all, p_all, cos_t, sin_t, dy_all, ssb)


def _in_bwd(dqf, dkf, dvf, dqb, dkb, dvb, du, dg, cos_t, sin_t, w_in_b, x, ctx, n1w, mod4, dx1):
    l, lc = x.shape[0], ctx.shape[0]
    la = l + lc
    tm = TOK_TILE
    nct = lc // tm

    def body(dqf_ref, dkf_ref, dvf_ref, dqb_ref, dkb_ref, dvb_ref, du_ref, dg_ref, cos_ref, sin_ref,
             w_ref, x_ref, c_ref, nw_ref, mod_ref, dx1_ref, dp_ref, gx_ref, acc_ref):
        i = pl.program_id(0)
        is_ctx = i < nct

        @pl.when(i == 0)
        def _():
            acc_ref[...] = jnp.zeros_like(acc_ref)

        cs, sn = cos_ref[...], sin_ref[...]
        def piece(k, val):
            cols = slice(S5_WIDTH * k, S5_WIDTH * (k + 1))
            dp_ref[:, cols] = val.astype(BF16)
            return _dot_nt(dp_ref[:, cols], w_ref[:, cols])

        dh1 = piece(0, du_ref[...])
        dh1 = dh1 + piece(3, dvf_ref[...] + dvb_ref[...])
        dh1 = dh1 + piece(4, jnp.where(is_ctx, 0.0, dg_ref[...]))
        for k, (f_ref, b_ref, scale) in ((1, (dqf_ref, dqb_ref, 1.0)), (2, (dkf_ref, dkb_ref, K_SCALE))):
            heads = [_rope_t(f_ref[:, RET_DH * h:RET_DH * (h + 1)] + b_ref[:, RET_DH * h:RET_DH * (h + 1)], cs, sn) * scale
                     for h in range(RET_HEADS)]
            dh1 = dh1 + piece(k, jnp.concatenate(heads, axis=1))
        xt = jnp.where(is_ctx, c_ref[...], x_ref[...])
        sh = jnp.where(is_ctx, mod_ref[0:1, :], mod_ref[2:3, :])
        sc = jnp.where(is_ctx, mod_ref[1:2, :], mod_ref[3:4, :])
        _, vjp = jax.vjp(_rms_mod, xt, nw_ref[...], sh, sc)
        dx, dnw, dsh, dsc = vjp(dh1)
        gx_ref[...] = dx + dx1_ref[...]
        cf = jnp.where(is_ctx, 1.0, 0.0)
        acc_ref[0:1, :] += dnw
        acc_ref[1:2, :] += cf * dsh
        acc_ref[2:3, :] += cf * dsc
        acc_ref[3:4, :] += (1.0 - cf) * dsh
        acc_ref[4:5, :] += (1.0 - cf) * dsc

    row = pl.BlockSpec((tm, RET_WIDTH), lambda i: (i, 0))
    tab = pl.BlockSpec((tm, RET_DH), lambda i: (i, 0))
    xrow = pl.BlockSpec((tm, D_MODEL), lambda i: (jnp.maximum(i - nct, 0), 0))
    return pl.pallas_call(
        body, name="in_bwd", grid=(la // tm,),
        in_specs=[row] * 7 + [pl.BlockSpec((tm, RET_WIDTH), lambda i: (jnp.maximum(i - nct, 0), 0)),
                              tab, tab, _full((D_MODEL, IN_COLS)), xrow,
                              pl.BlockSpec((tm, D_MODEL), lambda i: (jnp.minimum(i, nct - 1), 0)),
                              _full((1, D_MODEL)), _full((4, D_MODEL)), xrow],
        out_specs=[pl.BlockSpec((tm, IN_COLS), lambda i: (i, 0)), xrow, _full((8, D_MODEL))],
        out_shape=[jax.ShapeDtypeStruct((la, IN_COLS), BF16), jax.ShapeDtypeStruct((l, D_MODEL), F32),
                   jax.ShapeDtypeStruct((8, D_MODEL), F32)],
        compiler_params=_params(("arbitrary",)),
    )(dqf, dkf, dvf, dqb, dkb, dvb, du, dg, cos_t, sin_t, w_in_b, x, ctx, n1w, mod4, dx1)


def _outproj_up(x, y_all, of, ob, p_all, w_glu_b, b_glu, w_out_b, mod3, n2w, w_up_b, nct):
    l = x.shape[0]
    tm = TOK_TILE

    def body(x_ref, y_ref, of_ref, ob_ref, g_ref, wg_ref, bg_ref, wo_ref, mod_ref, nw_ref, wu_ref,
             x1_ref, mix_ref, h2_ref, up_ref, mb_ref, yr_ref):
        yg = _gelu(y_ref[...])
        mb_ref[:, 0:S5_WIDTH] = (yg * _sigmoid(_dot(yg.astype(BF16), wg_ref[...]) + bg_ref[...])).astype(BF16)
        yr = of_ref[...] + ob_ref[...]
        yr_ref[...] = yr
        for h in range(RET_HEADS):
            sl = slice(RET_DH * h, RET_DH * (h + 1))
            mb_ref[:, S5_WIDTH + RET_DH * h:S5_WIDTH + RET_DH * (h + 1)] = (
                _head_norm_gate(yr[:, sl], g_ref[:, sl]).astype(BF16))
        mix = _dot(mb_ref[...], wo_ref[...])
        mix_ref[...] = mix
        x1 = x_ref[...] + mod_ref[0:1, :] * mix
        x1_ref[...] = x1
        h2 = _rms_mod(x1, nw_ref[...], mod_ref[1:2, :], mod_ref[2:3, :]).astype(BF16)
        h2_ref[...] = h2
        up_ref[...] = _dot(h2, wu_ref[...])

    row = lambda w: pl.BlockSpec((tm, w), lambda i: (i, 0))
    arow = pl.BlockSpec((tm, RET_WIDTH), lambda i: (i + nct, 0))
    return pl.pallas_call(
        body, name="outproj_up", grid=(l // tm,),
        in_specs=[row(D_MODEL), arow, arow, arow, pl.BlockSpec((tm, RET_WIDTH), lambda i: (i + nct, G_COL // 4)),
                  _full((S5_WIDTH, S5_WIDTH)), _full((1, S5_WIDTH)), _full((D_MODEL, D_MODEL)), _full((3, D_MODEL)),
                  _full((1, D_MODEL)), _full((D_MODEL, 2 * D_FF))],
        out_specs=[row(D_MODEL), row(D_MODEL), row(D_MODEL), row(2 * D_FF), row(D_MODEL), row(RET_WIDTH)],
        out_shape=[jax.ShapeDtypeStruct((l, D_MODEL), F32), jax.ShapeDtypeStruct((l, D_MODEL), F32),
                   jax.ShapeDtypeStruct((l, D_MODEL), BF16), jax.ShapeDtypeStruct((l, 2 * D_FF), F32),
                   jax.ShapeDtypeStruct((l, D_MODEL), BF16), jax.ShapeDtypeStruct((l, RET_WIDTH), F32)],
        compiler_params=_params(("parallel",)),
    )(x, y_all, of, ob, p_all, w_glu_b, b_glu, w_out_b, mod3, n2w, w_up_b)


HALO = 8


def _conv_taps(g, prev_row, next_row):
    t = g.shape[0]
    r = _iota2(g.shape, 0)
    gprev = jnp.where(r == 0, prev_row, pltpu.roll(g, 1, 0))
    gnext = jnp.where(r == t - 1, next_row, pltpu.roll(g, t - 1, 0))
    return gprev, gnext


def _ffn_loss(up, x1, conv_w, conv_b, w_down_b, gate, fnw, tgt):
    l = x1.shape[0]
    tm = TOK_TILE
    nt = l // tm
    hb = tm // HALO

    cw = 256

    def body(up_a, up_g, hp_ref, hn_ref, x1_ref, cw_ref, cb_ref, wd_ref, gate_ref, fn_ref, tgt_ref,
             act_ref, dx2_ref, ddn_ref, dact_ref, acc_ref, ddn_scr):
        step = pl.program_id(0)
        i = jnp.minimum(step, nt - 1)

        @pl.when(step == 0)
        def _():
            acc_ref[...] = jnp.zeros_like(acc_ref)
            ddn_scr[...] = jnp.zeros_like(ddn_scr)

        ddn_prev = ddn_scr[...]
        dn = jnp.zeros((tm, D_MODEL), F32)
        for c in range(D_FF // cw):
            cols = slice(cw * c, cw * (c + 1))
            g = up_g[:, cols]
            prev_row = jnp.where(i == 0, 0.0, hp_ref[HALO - 1:HALO, cols])
            next_row = jnp.where(i == nt - 1, 0.0, hn_ref[0:1, cols])
            gprev, gnext = _conv_taps(g, prev_row, next_row)
            gc = cb_ref[:, cols] + gprev * cw_ref[0:1, cols] + g * cw_ref[1:2, cols] + gnext * cw_ref[2:3, cols]
            act = (_gelu(gc) * up_a[:, cols]).astype(BF16)
            act_ref[:, cols] = act
            dn = dn + _dot(act, wd_ref[cols, :])
            dact_ref[:, cols] = _dot_nt(ddn_prev, wd_ref[cols, :])
        x2 = x1_ref[...] + gate_ref[...] * dn
        y, vjp = jax.vjp(_rms, x2, fn_ref[...])
        err = y - tgt_ref[...]
        dx2, dfn = vjp(err * (1.0 / D_MODEL))
        dx2_ref[...] = dx2
        ddn = (dx2 * gate_ref[...]).astype(BF16)
        ddn_ref[...] = ddn
        ddn_scr[...] = ddn
        live = step < nt
        acc_ref[0:1, :] += jnp.where(live, dfn, 0.0)
        acc_ref[1:2, :] += jnp.where(live, jnp.sum(dx2 * dn, axis=0, keepdims=True), 0.0)
        acc_ref[2:3, :] += jnp.where(live, (0.5 / D_MODEL) * jnp.sum(err * err), 0.0)

    tile = lambda s: jnp.minimum(s, nt - 1)
    row = lambda w, cb=0: pl.BlockSpec((tm, w), lambda s: (tile(s), cb))
    last = l // HALO - 1
    return pl.pallas_call(
        body, name="ffn_loss", grid=(nt + 1,),
        in_specs=[row(D_FF, 0), row(D_FF, 1),
                  pl.BlockSpec((HALO, D_FF), lambda s: (jnp.maximum(tile(s) * hb - 1, 0), 1)),
                  pl.BlockSpec((HALO, D_FF), lambda s: (jnp.minimum((tile(s) + 1) * hb, last), 1)),
                  row(D_MODEL), _full((3, D_FF)), _full((1, D_FF)), _full((D_FF, D_MODEL)),
                  _full((1, D_MODEL)), _full((1, D_MODEL)), row(D_MODEL)],
        out_specs=[row(D_FF), row(D_MODEL), row(D_MODEL),
                   pl.BlockSpec((tm, D_FF), lambda s: (jnp.maximum(s - 1, 0), 0)), _full((8, D_MODEL))],
        out_shape=[jax.ShapeDtypeStruct((l, D_FF), BF16), jax.ShapeDtypeStruct((l, D_MODEL), F32),
                   jax.ShapeDtypeStruct((l, D_MODEL), BF16), jax.ShapeDtypeStruct((l, D_FF), F32),
                   jax.ShapeDtypeStruct((8, D_MODEL), F32)],
        scratch_shapes=[pltpu.VMEM((tm, D_MODEL), BF16)],
        compiler_params=_params(("arbitrary",)),
    )(up, up, up, up, x1, conv_w, conv_b, w_down_b, gate, fnw, tgt)


def _convglu_bwd(up, dact, conv_w, conv_b):
    l = up.shape[0]
    tm = 128
    nt = l // tm
    hb = tm // HALO
    te = tm + 2 * HALO

    def body(a_ref, ap_ref, an_ref, g_ref, gp_ref, gn_ref, d_ref, dp_ref, dn_ref, cw_ref, cb_ref,
             dup_ref, acc_ref):
        i = pl.program_id(0)

        @pl.when(i == 0)
        def _():
            acc_ref[...] = jnp.zeros_like(acc_ref)

        def ext(p, c, n):
            return jnp.concatenate([jnp.where(i == 0, 0.0, p[...]), c[...], jnp.where(i == nt - 1, 0.0, n[...])], axis=0)

        ae, ge, de = ext(ap_ref, a_ref, an_ref), ext(gp_ref, g_ref, gn_ref), ext(dp_ref, d_ref, dn_ref)
        gprev = pltpu.roll(ge, 1, 0)
        gnext = pltpu.roll(ge, te - 1, 0)
        w0, w1, w2 = cw_ref[0:1, :], cw_ref[1:2, :], cw_ref[2:3, :]
        gce = cb_ref[...] + gprev * w0 + ge * w1 + gnext * w2
        gel, dgel = _gelu_and_grad(gce)
        dae = de * gel
        dgce = de * ae * dgel
        dge = dgce * w1 + pltpu.roll(dgce, te - 1, 0) * w0 + pltpu.roll(dgce, 1, 0) * w2
        mid = slice(HALO, HALO + tm)
        dup_ref[:, 0:D_FF] = dae[mid].astype(BF16)
        dup_ref[:, D_FF:2 * D_FF] = dge[mid].astype(BF16)
        dgc = dgce[mid]
        acc_ref[0:1, :] += jnp.sum(dgc * gprev[mid], axis=0, keepdims=True)
        acc_ref[1:2, :] += jnp.sum(dgc * ge[mid], axis=0, keepdims=True)
        acc_ref[2:3, :] += jnp.sum(dgc * gnext[mid], axis=0, keepdims=True)
        acc_ref[3:4, :] += jnp.sum(dgc, axis=0, keepdims=True)

    last = l // HALO - 1

    def trio(cb):
        return [pl.BlockSpec((tm, D_FF), lambda i: (i, cb)),
                pl.BlockSpec((HALO, D_FF), lambda i: (jnp.maximum(i * hb - 1, 0), cb)),
                pl.BlockSpec((HALO, D_FF), lambda i: (jnp.minimum((i + 1) * hb, last), cb))]

    return pl.pallas_call(
        body, name="convglu_bwd", grid=(nt,),
        in_specs=trio(0) + trio(1) + trio(0) + [_full((3, D_FF)), _full((1, D_FF))],
        out_specs=[pl.BlockSpec((tm, 2 * D_FF), lambda i: (i, 0)), _full((8, D_FF))],
        out_shape=[jax.ShapeDtypeStruct((l, 2 * D_FF), BF16), jax.ShapeDtypeStruct((8, D_FF), F32)],
        compiler_params=_params(("arbitrary",)),
    )(up, up, up, up, up, up, dact, dact, dact, conv_w, conv_b)


def _up_bwd(dup, w_up_b, w_out_b, x1, dx2, mix, mod3, n2w, y_all, y_ret, p_all, w_glu_b, b_glu, zero_rows, nct, pairs,
            kinds):
    l = x1.shape[0]
    tm = TOK_TILE
    nt = l // tm
    n = len(pairs)
    shapes = _rs_slot_shapes(pairs, kinds)
    n_out = 8

    def body(dup_ref, wu_ref, wo_ref, x1_ref, dx2_ref, mix_ref, mod_ref, nw_ref, y_ref, yr_ref, g_ref, wg_ref, bg_ref,
             zero_rows_ref, *rest):
        dx1_ref, dmixb_ref, acc_ref, dys_ref, dyr_ref, dg_ref, gw_ref, gb_ref = rest[n:n + n_out]
        send_sems, recv_sems, dy_scr = rest[2 * n + n_out:]
        step = pl.program_id(0)

        @pl.when(step == 0)
        def _():
            acc_ref[...] = jnp.zeros_like(acc_ref)
            gw_ref[...] = jnp.zeros_like(gw_ref)
            gb_ref[...] = jnp.zeros_like(gb_ref)

        _behind(step, nt - 1, functools.partial(_rs_chip_copies, rest[:n], rest[n + n_out:2 * n + n_out], kinds,
                                                shapes, send_sems, recv_sems))

        dh2 = _dot_nt(dup_ref[...], wu_ref[...])
        _, vjp = jax.vjp(_rms_mod, x1_ref[...], nw_ref[...], mod_ref[1:2, :], mod_ref[2:3, :])
        dx, dnw, dsh, dsc = vjp(dh2)
        dx1 = dx + dx2_ref[...]
        dx1_ref[...] = dx1
        dmixb = (dx1 * mod_ref[0:1, :]).astype(BF16)
        dmixb_ref[...] = dmixb
        dmix = _dot_nt(dmixb, wo_ref[...])
        acc_ref[0:1, :] += dnw
        acc_ref[1:2, :] += jnp.sum(dx1 * mix_ref[...], axis=0, keepdims=True)
        acc_ref[2:3, :] += dsh
        acc_ref[3:4, :] += dsc

        yg, dgel = _gelu_and_grad(y_ref[...])
        ygb = yg.astype(BF16)
        sg = _sigmoid(_dot(ygb, wg_ref[...]) + bg_ref[...])
        ds = dmix[:, 0:S5_WIDTH]
        dz = ds * yg * sg * (1.0 - sg)
        dzb = dz.astype(BF16)
        _s5_put_rows(dys_ref, dy_scr, (ds * sg + _dot_nt(dzb, wg_ref[...])) * dgel)
        gw_ref[...] += _dot_tn(ygb, dzb)
        gb_ref[...] += jnp.sum(dz, axis=0, keepdims=True)

        for h in range(RET_HEADS):
            sl = slice(RET_DH * h, RET_DH * (h + 1))
            _, hvjp = jax.vjp(_head_norm_gate, yr_ref[:, sl], g_ref[:, sl])
            dyr, dg = hvjp(dmix[:, S5_WIDTH + RET_DH * h:S5_WIDTH + RET_DH * (h + 1)])
            dyr_ref[:, sl] = dyr
            dg_ref[:, sl] = dg

    row = pl.BlockSpec((tm, D_MODEL), lambda i: (i, 0))
    half = pl.BlockSpec((tm, S5_WIDTH), lambda i: (i, 0))
    f32h = jax.ShapeDtypeStruct((l, RET_WIDTH), F32)
    return pl.pallas_call(
        body, name="up_bwd", grid=(nt,),
        in_specs=[pl.BlockSpec((tm, 2 * D_FF), lambda i: (i, 0)), _full((D_MODEL, 2 * D_FF)),
                  _full((D_MODEL, D_MODEL)), row, row, row, _full((3, D_MODEL)), _full((1, D_MODEL)),
                  pl.BlockSpec((tm, S5_WIDTH), lambda i: (i + nct, 0)), half,
                  pl.BlockSpec((tm, RET_WIDTH), lambda i: (i + nct, G_COL // 4)),
                  _full((S5_WIDTH, S5_WIDTH)), _full((1, S5_WIDTH)), ANY] + [ANY] * n,
        out_specs=[row, row, _full((8, D_MODEL)),
                   pl.BlockSpec((S5_NB, tm // S5_T, S5_BW), lambda i: (0, i + nct, 0)), half, half,
                   _full((S5_WIDTH, S5_WIDTH)),
                   _full((1, S5_WIDTH))] + [ANY] * n,
        out_shape=[jax.ShapeDtypeStruct((l, D_MODEL), F32), jax.ShapeDtypeStruct((l, D_MODEL), BF16),
                   jax.ShapeDtypeStruct((8, D_MODEL), F32), jax.ShapeDtypeStruct(zero_rows.shape, BF16), f32h, f32h,
                   jax.ShapeDtypeStruct((S5_WIDTH, S5_WIDTH), F32), jax.ShapeDtypeStruct((1, S5_WIDTH), F32)]
        + [jax.ShapeDtypeStruct((4,) + s, p.dtype) for s, p in zip(shapes, pairs)],
        input_output_aliases={13: 3},
        scratch_shapes=[pltpu.SemaphoreType.DMA((n, 3)), pltpu.SemaphoreType.DMA((n, 3)),
                        pltpu.VMEM((tm // S5_T, S5_T, S5_WIDTH), F32)],
        compiler_params=_params(("arbitrary",)),
    )(dup, w_up_b, w_out_b, x1, dx2, mix, mod3, n2w, y_all, y_ret, p_all, w_glu_b, b_glu, zero_rows, *pairs)


MOD_ROWS = 16
MOD_COLS = 6 * D_MODEL // 4


def _mod_fwd(c_all, c_ctx, w_mod_b, b_loc):
    def body(c_ref, cc_ref, w_ref, b_ref, m_ref, s_ref):
        cond = jnp.concatenate([c_ref[...], jnp.broadcast_to(cc_ref[...], (8, D_MODEL))], axis=0)
        s = _silu(cond).astype(BF16)
        s_ref[...] = s
        m_ref[...] = _dot(s, w_ref[...]) + b_ref[...]

    return pl.pallas_call(
        body, name="mod_fwd",
        out_shape=[jax.ShapeDtypeStruct((MOD_ROWS, MOD_COLS), F32), jax.ShapeDtypeStruct((MOD_ROWS, D_MODEL), BF16)],
        compiler_params=_params(),
    )(c_all, c_ctx, w_mod_b, b_loc)


def _mod_bwd_sum(dm_all):
    def body(d_ref, dm_ref, gb_ref):
        rows = [d_ref[k, 0:1, :] for k in range(8)]
        ctx_sum = d_ref[0, 1:2, :]
        for k in range(1, 8):
            ctx_sum = ctx_sum + d_ref[k, 1:2, :]
        gb = ctx_sum
        for k in range(8):
            gb = gb + rows[k]
        gb_ref[...] = gb
        dm_ref[...] = jnp.concatenate(rows + [ctx_sum] + [jnp.zeros((7, 6 * D_MODEL), F32)], axis=0)

    return pl.pallas_call(
        body, name="mod_bwd_sum",
        out_shape=[jax.ShapeDtypeStruct((MOD_ROWS, 6 * D_MODEL), F32), jax.ShapeDtypeStruct((1, 6 * D_MODEL), F32)],
        compiler_params=_params(),
    )(dm_all)


def _mod_bwd_w(dm_loc, s_b, c_ctx, w_mod_b):
    def body(d_ref, s_ref, cc_ref, w_ref, gw_ref, gc_ref):
        db = d_ref[...].astype(BF16)
        gw_ref[...] = _dot_tn(s_ref[...], db)
        ds = _dot_nt(db, w_ref[...])
        _, vjp = jax.vjp(_silu, cc_ref[...])
        gc_ref[...] = jnp.broadcast_to(vjp(ds[8:9, :])[0], (8, D_MODEL))

    return pl.pallas_call(
        body, name="mod_bwd_w",
        out_shape=[jax.ShapeDtypeStruct((D_MODEL, MOD_COLS), F32), jax.ShapeDtypeStruct((8, D_MODEL), F32)],
        compiler_params=_params(),
    )(dm_loc, s_b, c_ctx, w_mod_b)


def _adamw(w, g, m, v, name):
    r, c = w.shape
    tr = _pick(r, (256, 128, 64, 32, 16, 8))
    bc1 = 1.0 - ADAM_B1 ** ADAM_STEP
    bc2 = 1.0 - ADAM_B2 ** ADAM_STEP

    def body(w_ref, g_ref, m_ref, v_ref, d_ref, nm_ref, nv_ref):
        gg = g_ref[...]
        nm = ADAM_B1 * m_ref[...] + (1.0 - ADAM_B1) * gg
        nv = ADAM_B2 * v_ref[...] + (1.0 - ADAM_B2) * (gg * gg)
        nm_ref[...] = nm
        nv_ref[...] = nv
        d_ref[...] = -ADAM_LR * ((nm / bc1) / (jnp.sqrt(nv / bc2) + ADAM_EPS) + ADAM_WD * w_ref[...])

    blk = pl.BlockSpec((tr, c), lambda i: (i, 0))
    shp = jax.ShapeDtypeStruct((r, c), F32)
    return pl.pallas_call(
        body, name=name, grid=(r // tr,), in_specs=[blk] * 4, out_specs=[blk] * 3, out_shape=[shp] * 3,
        compiler_params=_params(("parallel",)),
    )(w, g, m, v)


def _sum_slots(a, name):
    n, r, c = a.shape
    tr = _pick(r, (376, 256, 208, 128, 64, 32, 16, 8))

    def body(a_ref, o_ref):
        acc = a_ref[0].astype(F32)
        for k in range(1, n):
            acc = acc + a_ref[k].astype(F32)
        o_ref[...] = acc

    return pl.pallas_call(
        body, name=name, grid=(r // tr,),
        in_specs=[pl.BlockSpec((n, tr, c), lambda i: (0, i, 0))],
        out_specs=pl.BlockSpec((tr, c), lambda i: (i, 0)),
        out_shape=jax.ShapeDtypeStruct((r, c), F32),
        compiler_params=_params(("parallel",)),
    )(a)


def _mesh_pos():
    return lax.axis_index("x"), lax.axis_index("y"), lax.axis_index("c")


def _all_gather8(v, name):
    m_per, n = v.shape

    def body(x_ref, out_ref, send_sems, recv_sems, local_sem):
        x, y, c = _mesh_pos()
        me, sibling = (x, y, c), (x, y, 1 - c)
        chips = [(1 - x, y), (x, 1 - y), (1 - x, 1 - y)]

        def rows(px, py, pc):
            return out_ref.at[pl.ds((4 * px + 2 * py + pc) * m_per, m_per), :]

        def copy(k, block, to, src=None):
            return pltpu.make_async_remote_copy(
                src_ref=rows(*block) if src is None else src, dst_ref=rows(*block),
                send_sem=send_sems.at[k], recv_sem=recv_sems.at[k], device_id=to, device_id_type=MESH_ID)

        mine = pltpu.make_async_copy(x_ref, rows(*me), local_sem)
        mine.start()
        first = [copy(0, me, sibling, src=x_ref)]
        first += [copy(1 + j, me, (*chip, c), src=x_ref) for j, chip in enumerate(chips)]
        for cp in first:
            cp.start()
        passed = [copy(4 + j, (*chip, c), sibling) for j, chip in enumerate(chips)]
        for j, chip in enumerate(chips):
            copy(1 + j, (*chip, c), me).wait_recv()
            passed[j].start()
        copy(0, sibling, me).wait_recv()
        for j, chip in enumerate(chips):
            copy(4 + j, (*chip, 1 - c), me).wait_recv()
        for cp in first + passed:
            cp.wait_send()
        mine.wait()

    return pl.pallas_call(
        body, name=name,
        out_shape=jax.ShapeDtypeStruct((8 * m_per, n), v.dtype),
        in_specs=[pl.BlockSpec(memory_space=pltpu.VMEM)],
        out_specs=pl.BlockSpec(memory_space=pltpu.VMEM),
        scratch_shapes=[pltpu.SemaphoreType.DMA((7,)), pltpu.SemaphoreType.DMA((7,)), pltpu.SemaphoreType.DMA],
        compiler_params=_params(),
    )(v)


ANY = pl.BlockSpec(memory_space=pl.ANY)
def PEER_CHIPS(x, y):
    return [(x, 1 - y), (1 - x, y), (1 - x, 1 - y)]


def _shard_region(ref, kind, k, rl, cl, r0, nr, c0, nc):
    if kind == "col":
        return ref.at[pl.ds(r0, nr), pl.ds(k * cl + c0, nc)]
    return ref.at[pl.ds(k * rl + r0, nr), pl.ds(c0, nc)]


def _place_shard(w, kind, chip, name):
    rl, cl = w.shape
    tr = _pick(rl, (256, 128, 64))
    nt = rl // tr

    def body(chip_ref, w_ref, o_ref):
        o_ref[...] = w_ref[...].astype(BF16)

    o_map = (lambda i, chip_ref: (i, chip_ref[0])) if kind == "col" else (lambda i, chip_ref: (chip_ref[0] * nt + i, 0))
    return pl.pallas_call(
        body, name=name,
        grid_spec=pltpu.PrefetchScalarGridSpec(
            num_scalar_prefetch=1, grid=(nt,),
            in_specs=[pl.BlockSpec((tr, cl), lambda i, chip_ref: (i, 0))], out_specs=pl.BlockSpec((tr, cl), o_map)),
        out_shape=jax.ShapeDtypeStruct((rl, 4 * cl) if kind == "col" else (4 * rl, cl), BF16),
        compiler_params=_params(("parallel",)),
    )(chip.reshape(1), w)


def _gather_shard_shapes(placed, kinds):
    return [(p.shape[0], p.shape[1] // 4) if k == "col" else (p.shape[0] // 4, p.shape[1]) for p, k in zip(placed, kinds)]


def _gather_chip_copies(outs, kinds, shard_shapes, send_sems, recv_sems, with_arrivals=True):
    x, y, c = _mesh_pos()
    me = 2 * x + y
    sends, arrivals = [], []
    for a in range(len(outs)):
        rl, cl = shard_shapes[a]
        rh = rl // 2
        reg = functools.partial(_shard_region, outs[a], kinds[a], rl=rl, cl=cl, r0=c * rh, nr=rh, c0=0, nc=cl)
        for j, (px, py) in enumerate(PEER_CHIPS(x, y)):
            to = dict(send_sem=send_sems.at[a, j], recv_sem=recv_sems.at[a, j], device_id=(px, py, c),
                      device_id_type=MESH_ID)
            sends.append(pltpu.make_async_remote_copy(src_ref=reg(k=me), dst_ref=reg(k=me), **to))
            if with_arrivals:
                got = reg(k=2 * px + py)
                arrivals.append(pltpu.make_async_remote_copy(src_ref=got, dst_ref=got, **to))
    return sends, arrivals


def _gather_sibling_copies(outs, kinds, shard_shapes, send_sems, recv_sems):
    x, y, c = _mesh_pos()
    forwards, arrivals = [], []
    for a in range(len(outs)):
        rl, cl = shard_shapes[a]
        rh = rl // 2
        for j, (px, py) in enumerate(PEER_CHIPS(x, y)):
            to = dict(send_sem=send_sems.at[a, j], recv_sem=recv_sems.at[a, j], device_id=(x, y, 1 - c),
                      device_id_type=MESH_ID)
            reg = functools.partial(_shard_region, outs[a], kinds[a], k=2 * px + py, rl=rl, cl=cl, nr=rh, c0=0, nc=cl)
            forwards.append(pltpu.make_async_remote_copy(src_ref=reg(r0=c * rh), dst_ref=reg(r0=c * rh), **to))
            arrivals.append(pltpu.make_async_remote_copy(src_ref=reg(r0=(1 - c) * rh), dst_ref=reg(r0=(1 - c) * rh), **to))
    return forwards, arrivals


def _gather_sibling(placed, kinds, name):
    n = len(placed)
    shard_shapes = _gather_shard_shapes(placed, kinds)

    def body(*refs):
        forwards, from_sibling = _gather_sibling_copies(refs[n:2 * n], kinds, shard_shapes, *refs[2 * n:])
        for cp in forwards:
            cp.start()
        for cp in from_sibling:
            cp.wait_recv()
        for cp in forwards:
            cp.wait_send()

    return pl.pallas_call(
        body, name=name,
        out_shape=[jax.ShapeDtypeStruct(p.shape, p.dtype) for p in placed],
        in_specs=[ANY] * n, out_specs=[ANY] * n, input_output_aliases={a: a for a in range(n)},
        scratch_shapes=[pltpu.SemaphoreType.DMA((n, 3))] * 2,
        compiler_params=_params(),
    )(*placed)


def _half(kind, r, c):
    return (r // 2, c) if kind == "col" else (r, c // 2)


def _half_of(ref, kind, which):
    r, c = ref.shape
    hr, hc = _half(kind, r, c)
    return ref.at[pl.ds(which * hr, hr), :] if kind == "col" else ref.at[:, pl.ds(which * hc, hc)]


def _rs_sibling(grads, kinds, name):
    n = len(grads)

    def body(*refs):
        srcs, dsts = refs[:n], refs[n:2 * n]
        send_sems, recv_sems = refs[2 * n:]
        x, y, c = _mesh_pos()
        cps = [pltpu.make_async_remote_copy(src_ref=_half_of(srcs[a], kinds[a], 1 - c), dst_ref=dsts[a],
                                            send_sem=send_sems.at[a], recv_sem=recv_sems.at[a],
                                            device_id=(x, y, 1 - c), device_id_type=MESH_ID) for a in range(n)]
        for cp in cps:
            cp.start()
        for cp in cps:
            cp.wait()

    return pl.pallas_call(
        body, name=name,
        out_shape=[jax.ShapeDtypeStruct(_half(k, *g.shape), g.dtype) for g, k in zip(grads, kinds)],
        in_specs=[ANY] * n, out_specs=[ANY] * n,
        scratch_shapes=[pltpu.SemaphoreType.DMA((n,)), pltpu.SemaphoreType.DMA((n,))],
        compiler_params=_params(),
    )(*grads)


def _pair_sum(gf, rv, kind, ci, name):
    r, c = rv.shape
    tr = _pick(r, (128, 64, 32, 16, 8))
    nt = r // tr

    def body(ci_ref, g_ref, r_ref, o_ref):
        o_ref[...] = (g_ref[...] + r_ref[...]).astype(BF16)

    g_map = (lambda i, ci_ref: (ci_ref[0] * nt + i, 0)) if kind == "col" else (lambda i, ci_ref: (i, ci_ref[0]))
    blk = pl.BlockSpec((tr, c), lambda i, ci_ref: (i, 0))
    return pl.pallas_call(
        body, name=name,
        grid_spec=pltpu.PrefetchScalarGridSpec(num_scalar_prefetch=1, grid=(nt,),
                                               in_specs=[pl.BlockSpec((tr, c), g_map), blk], out_specs=blk),
        out_shape=jax.ShapeDtypeStruct((r, c), BF16),
        compiler_params=_params(("parallel",)),
    )(ci.reshape(1), gf, rv)


def _rs_slot_shapes(pairs, kinds):
    return [(p.shape[0], p.shape[1] // 4) if k == "col" else (p.shape[0] // 4, p.shape[1]) for p, k in zip(pairs, kinds)]


def _rs_chip_copies(srcs, dsts, kinds, shapes, send_sems, recv_sems, with_arrivals=True):
    x, y, c = _mesh_pos()
    me = 2 * x + y
    sends, arrivals = [], []
    for a in range(len(srcs)):
        rl, cl = shapes[a]
        reg = functools.partial(_shard_region, srcs[a], kinds[a], rl=rl, cl=cl, r0=0, nr=rl, c0=0, nc=cl)
        for j, (px, py) in enumerate(PEER_CHIPS(x, y)):
            to = dict(send_sem=send_sems.at[a, j], recv_sem=recv_sems.at[a, j], device_id=(px, py, c),
                      device_id_type=MESH_ID)
            sends.append(pltpu.make_async_remote_copy(src_ref=reg(k=2 * px + py), dst_ref=dsts[a].at[me], **to))
            if with_arrivals:
                slot = dsts[a].at[2 * px + py]
                arrivals.append(pltpu.make_async_remote_copy(src_ref=slot, dst_ref=slot, **to))
    return sends, arrivals


def _rs_chips(pairs, kinds):
    n = len(pairs)
    shapes = _rs_slot_shapes(pairs, kinds)

    def body(*refs):
        sends, arrivals = _rs_chip_copies(refs[:n], refs[n:2 * n], kinds, shapes, *refs[2 * n:])
        for cp in sends:
            cp.start()
        for cp in arrivals:
            cp.wait_recv()
        for cp in sends:
            cp.wait_send()

    return pl.pallas_call(
        body, name="rs_chips",
        out_shape=[jax.ShapeDtypeStruct((4,) + s, p.dtype) for s, p in zip(shapes, pairs)],
        in_specs=[ANY] * n, out_specs=[ANY] * n,
        scratch_shapes=[pltpu.SemaphoreType.DMA((n, 3)), pltpu.SemaphoreType.DMA((n, 3))],
        compiler_params=_params(),
    )(*pairs)


def _sum_chips(pair, got, kind, pos, name):
    _, r, c = got.shape
    tr = _pick(r, (256, 128, 64, 32, 16))
    nt = r // tr

    def body(pos_ref, own_ref, g1_ref, g2_ref, g3_ref, o_ref):
        o_ref[...] = ((own_ref[...].astype(F32) + g1_ref[0].astype(F32)) + g2_ref[0].astype(F32)) + g3_ref[0].astype(F32)

    if kind == "col":
        own_map = lambda i, p: (i, p[1])
        out_map = lambda i, p: (p[0] * nt + i, 0)
        out_shape = (2 * r, c)
    else:
        own_map = lambda i, p: (p[1] * nt + i, 0)
        out_map = lambda i, p: (i, p[0])
        out_shape = (r, 2 * c)
    peer = lambda m: pl.BlockSpec((1, tr, c), lambda i, p: (p[1] ^ m, i, 0))
    return pl.pallas_call(
        body, name=name,
        grid_spec=pltpu.PrefetchScalarGridSpec(
            num_scalar_prefetch=1, grid=(nt,),
            in_specs=[pl.BlockSpec((tr, c), own_map), peer(1), peer(2), peer(3)],
            out_specs=pl.BlockSpec((tr, c), out_map)),
        out_shape=jax.ShapeDtypeStruct(out_shape, F32),
        compiler_params=_params(("parallel",)),
    )(pos, pair, got, got, got)


def _rs_back(halves, kinds):
    n = len(halves)

    def body(*refs):
        outs = refs[n:2 * n]
        send_sems, recv_sems = refs[2 * n:]
        x, y, c = _mesh_pos()
        cps = []
        for a in range(n):
            mine = _half_of(outs[a], kinds[a], c)
            cps.append(pltpu.make_async_remote_copy(src_ref=mine, dst_ref=mine, send_sem=send_sems.at[a],
                                                    recv_sem=recv_sems.at[a], device_id=(x, y, 1 - c),
                                                    device_id_type=MESH_ID))
            cps[-1].start()
        for a in range(n):
            other = _half_of(outs[a], kinds[a], 1 - c)
            pltpu.make_async_remote_copy(src_ref=other, dst_ref=other, send_sem=send_sems.at[a],
                                         recv_sem=recv_sems.at[a], device_id=(x, y, 1 - c),
                                         device_id_type=MESH_ID).wait_recv()
        for cp in cps:
            cp.wait_send()

    return pl.pallas_call(
        body, name="rs_back",
        out_shape=[jax.ShapeDtypeStruct(h.shape, h.dtype) for h in halves],
        in_specs=[ANY] * n, out_specs=[ANY] * n, input_output_aliases={a: a for a in range(n)},
        scratch_shapes=[pltpu.SemaphoreType.DMA((n,)), pltpu.SemaphoreType.DMA((n,))],
        compiler_params=_params(),
    )(*halves)


def _rope_tables(l, lc):
    rows = l // GRID_W
    n_freq = RET_DH // 4
    inv_freq = ROPE_THETA ** (-jnp.arange(n_freq, dtype=F32) / n_freq)
    sign = jnp.tile(jnp.array([-1.0, 1.0], F32), n_freq)

    def half(n):
        ang = jnp.repeat(jnp.arange(n, dtype=F32)[:, None] * inv_freq, 2, axis=-1)
        return jnp.cos(ang), jnp.sin(ang) * sign

    (cr, sr), (cc, sc) = half(rows), half(GRID_W)
    grid = lambda r, c: jnp.concatenate([jnp.repeat(r, GRID_W, axis=0), jnp.tile(c, (rows, 1))], axis=-1)
    cos_t = jnp.concatenate([jnp.ones((lc, RET_DH), F32), grid(cr, cc)], axis=0)
    sin_t = jnp.concatenate([jnp.zeros((lc, RET_DH), F32), grid(sr, sc)], axis=0)
    return cos_t, sin_t


def _s5_pack(a):
    blk = lambda t: t.reshape(1, S5_NB, 128, S5_STATE)
    lre = jnp.stack([a["s5_lambda_re_f"][0], a["s5_lambda_re_b"][0]]).reshape(2, S5_NB, 8, S5_STATE)
    lim = jnp.stack([a["s5_lambda_im_f"][0], a["s5_lambda_im_b"][0]]).reshape(2, S5_NB, 8, S5_STATE)
    lst = jnp.stack([a["s5_log_step_f"][0], a["s5_log_step_b"][0]]).reshape(2, S5_NB, 8, 1)
    b_re = blk(a["s5_b_re"][0].transpose(0, 2, 1))
    b_im = blk(a["s5_b_im"][0].transpose(0, 2, 1))
    return (lre, lim, lst, b_re, b_im, blk(a["s5_c_re"][0]), blk(a["s5_c_im"][0]),
            a["s5_d"].reshape(1, S5_NB, 1, 128))


def _s5_unpack(g):
    glre, glim, glst, gbre, gbim, gcre, gcim, gd = g
    unb = lambda t: t.reshape(S5_GROUPS, S5_GROUP, S5_STATE).transpose(0, 2, 1)[None]
    return {
        "s5_lambda_re_f": glre[0].reshape(1, S5_GROUPS, S5_STATE), "s5_lambda_re_b": glre[1].reshape(1, S5_GROUPS, S5_STATE),
        "s5_lambda_im_f": glim[0].reshape(1, S5_GROUPS, S5_STATE), "s5_lambda_im_b": glim[1].reshape(1, S5_GROUPS, S5_STATE),
        "s5_log_step_f": glst[0].reshape(1, S5_GROUPS), "s5_log_step_b": glst[1].reshape(1, S5_GROUPS),
        "s5_b_re": unb(gbre), "s5_b_im": unb(gbim),
        "s5_c_re": gcre.reshape(1, S5_GROUPS, S5_GROUP, S5_STATE), "s5_c_im": gcim.reshape(1, S5_GROUPS, S5_GROUP, S5_STATE),
        "s5_d": gd.reshape(1, S5_WIDTH),
    }


def _local_step(a, early, late, mx, mc, conv_w, ci):
    x, ctx, tgt = a["x"][0], a["ctx"][0], a["loss_target"][0]
    l, lc = x.shape[0], ctx.shape[0]
    la = l + lc
    nct, ncc, nrc, cn = lc // TOK_TILE, lc // S5_T, lc // RET_CHUNK, la // S5_T
    n1w, n2w, fnw = a["norm1_w"], a["norm2_w"], a["final_norm_w"].reshape(1, D_MODEL)
    conv_b, b_glu = a["conv_b"], a["s5_b_glu"]
    ld2 = jnp.concatenate([a["ret_log_decay_f"], a["ret_log_decay_b"]], axis=0)
    mod4 = jnp.concatenate([mc[0:2], mx[0:2]], axis=0)
    mod3 = mx[2:5]
    gate5 = mx[5:6]
    cos_t, sin_t = _rope_tables(l, lc)
    s5p = _s5_pack(a)

    gg, xw, yw, a16, *early = _s5_gen(*s5p, early, EARLY_KINDS)
    wb = dict(zip(EARLY_NAMES, _gather_sibling(early, EARLY_KINDS, "gather_sibling_early")))
    p_all, h1b, p3, w_up_p = _norm_inproj(x, ctx, n1w, mod4, wb["w_in"], cos_t, sin_t, [late[1]], (LATE_KINDS[1],))
    sloc = _s5_to_states(p3, xw, "s5_state")
    a16s = a16.transpose(1, 0, 2, 3).reshape(2, S5_GROUPS, 128)
    hs, w_out_p, w_down_p = _s5_scan(sloc, a16s, ncc, [late[0], late[2]], (LATE_KINDS[0], LATE_KINDS[2]))
    y_all = _s5_from_states(p3, gg, hs, yw, False, "s5_out").reshape(la, S5_WIDTH)
    tabs = _ret_tables(ld2)
    of, ob, ssf, ssb = _ret_scan(p_all, cos_t, sin_t, tabs, nrc)
    wb = {**wb, **dict(zip(LATE_NAMES, _gather_sibling([w_out_p, w_up_p, w_down_p], LATE_KINDS, "gather_sibling_late")))}
    x1, mix, h2b, up, mixb, y_ret = _outproj_up(x, y_all, of, ob, p_all, wb["s5_w_glu"], b_glu, wb["w_out"],
                                                     mod3, n2w, wb["w_up"], nct)
    act, dx2, ddn, dact, acc_f = _ffn_loss(up, x1, conv_w, conv_b, wb["w_down"], gate5, fnw, tgt)

    g = {}
    g["w_down"] = _mm_tn(act, ddn, name="gw_down")
    dup, acc_c = _convglu_bwd(up, dact, conv_w, conv_b)
    g["w_up"] = _mm_tn(h2b, dup, name="gw_up")
    first = [g[n] for n in FIRST_GRADS]
    first_pairs = [_pair_sum(gf, rv, k, ci, "rs_pair_" + n)
                   for gf, rv, k, n in zip(first, _rs_sibling(first, FIRST_KINDS, "rs_sibling_first"), FIRST_KINDS, FIRST_GRADS)]
    dx1, dmixb, acc_2, dy3, dy_ret, dg, g["s5_w_glu"], g["s5_b_glu"], *first_got = _up_bwd(
        dup, wb["w_up"], wb["w_out"], x1, dx2, mix, mod3, n2w, y_all, y_ret, p_all, wb["s5_w_glu"], b_glu,
        jnp.zeros(p3.shape, BF16), nct, first_pairs, FIRST_KINDS)
    g["w_out"] = _mm_tn(mixb, dmixb, name="gw_out")

    e = _s5_to_states(dy3, yw, "s5_bwd_h")
    ds, da16 = _s5_scan_bwd(e, hs, a16s, ncc)
    du = _s5_from_states(dy3, gg, ds, xw, True, "s5_bwd_u").reshape(la, S5_WIDTH)
    dkb = _s5_bwd_kb(p3, dy3)
    dwst = _s5_bwd_w(p3, ds, "s5_bwd_wst")
    dwout = _s5_bwd_w(dy3, hs, "s5_bwd_wout")
    da16p = da16.reshape(2, S5_NB, 8, 128).transpose(1, 0, 2, 3)
    g.update(_s5_unpack(_s5_gen_bwd(*s5p, dkb, dwst, dwout, da16p)))

    dqf, dkf, dvf, dqb, dkb_, dvb, dld = _ret_scan_bwd(p_all, cos_t, sin_t, tabs, ssf, ssb, dy_ret, nrc)
    g["ret_log_decay_f"] = dld[0, :, 0, 0].reshape(1, RET_HEADS)
    g["ret_log_decay_b"] = dld[1, :, 0, 0].reshape(1, RET_HEADS)
    dp, grad_x, acc_1 = _in_bwd(dqf, dkf, dvf, dqb, dkb_, dvb, du, dg, cos_t, sin_t, wb["w_in"], x, ctx, n1w, mod4, dx1)
    g["w_in"] = _mm_tn(h1b, dp, name="gw_in")

    g["norm1_w"], g["norm2_w"], g["final_norm_w"] = acc_1[0:1], acc_2[0:1], acc_f[0]
    g["conv_w"], g["conv_b"] = acc_c[0:3], acc_c[3:4]
    zero = jnp.zeros((1, D_MODEL), F32)
    dmx = jnp.concatenate([acc_1[3:5], acc_2[1:2], acc_2[2:4], acc_f[1:2]], axis=0)
    dmc = jnp.concatenate([acc_1[1:3], zero, zero, zero, zero], axis=0)
    return acc_f[2, 0], grad_x, g, dmx, dmc, first_pairs, first_got


WEIGHT_NAMES = ("c_ctx", "w_mod", "b_mod", "norm1_w", "w_in", "s5_lambda_re_f", "s5_lambda_im_f", "s5_log_step_f",
                "s5_lambda_re_b", "s5_lambda_im_b", "s5_log_step_b", "s5_b_re", "s5_b_im", "s5_c_re", "s5_c_im",
                "s5_d", "s5_w_glu", "s5_b_glu", "ret_log_decay_f", "ret_log_decay_b", "w_out", "norm2_w", "w_up",
                "conv_w", "conv_b", "w_down", "final_norm_w")
BIG_NAMES = ("w_in", "w_out", "w_up", "w_down", "s5_w_glu")
BIG_KINDS = ("col", "row", "col", "row", "row")
EARLY_NAMES, EARLY_KINDS = ("w_in", "s5_w_glu"), ("col", "row")
LATE_NAMES, LATE_KINDS = ("w_out", "w_up", "w_down"), ("row", "col", "row")
FIRST_GRADS, FIRST_KINDS = ("w_down", "w_up"), ("row", "col")
LAST_GRADS, LAST_KINDS = ("w_in", "w_out", "s5_w_glu"), ("col", "row", "row")
SMALL_NAMES = ("norm1_w", "norm2_w", "final_norm_w", "conv_b", "conv_w", "s5_lambda_re_f", "s5_lambda_im_f",
               "s5_log_step_f", "s5_lambda_re_b", "s5_lambda_im_b", "s5_log_step_b", "s5_b_re", "s5_b_im", "s5_c_re",
               "s5_c_im", "s5_d", "s5_b_glu", "ret_log_decay_f", "ret_log_decay_b")
ROW = 1024
N_CHIPS = 4


def _pack_rows(parts):
    flat = jnp.concatenate([p.reshape(-1) for p in parts])
    n = flat.shape[0]
    rows = -(-n // (8 * ROW)) * 8
    return jnp.pad(flat, (0, rows * ROW - n)).reshape(rows, ROW)


def _unpack_rows(packed, shapes):
    flat = packed.reshape(-1)
    out, off = [], 0
    for s in shapes:
        n = math.prod(s)
        out.append(flat[off:off + n].reshape(s))
        off += n
    return out


def _step(a):
    xi, yi, ci = _mesh_pos()
    chip = 2 * xi + yi
    dev = 2 * chip + ci

    cw_loc = a["conv_w"].reshape(-1)
    small_in = jnp.concatenate([a["c"].reshape(-1), jnp.pad(cw_loc, (0, 24 * 128 - cw_loc.shape[0]))]).reshape(32, 128)
    sg = _all_gather8(small_in, "gather_cond").reshape(8, 32, 128)
    c_all = sg[:, 0:8].reshape(8, D_MODEL)
    conv_w = sg[0::2, 8:32].reshape(N_CHIPS, -1)[:, :cw_loc.shape[0]].reshape(N_CHIPS, 3, -1)
    conv_w = conv_w.transpose(1, 0, 2).reshape(3, D_FF)

    placed = {n: _place_shard(a[n][0], k, chip, "place_" + n) for n, k in zip(BIG_NAMES, BIG_KINDS)}
    early = [placed[n] for n in EARLY_NAMES]
    late = [placed[n] for n in LATE_NAMES]

    w_mod_b = a["w_mod"][0].astype(BF16)
    c_ctx = a["c_ctx"].reshape(1, D_MODEL)
    b_loc = lax.dynamic_slice_in_dim(a["b_mod"], chip * MOD_COLS, MOD_COLS, 1)
    m_loc, s_b = _mod_fwd(c_all, c_ctx, w_mod_b, b_loc)
    mg = _all_gather8(m_loc, "gather_mod").reshape(8, MOD_ROWS, MOD_COLS)
    m_full = mg[0::2].transpose(1, 0, 2).reshape(MOD_ROWS, 6 * D_MODEL)
    mx = lax.dynamic_slice_in_dim(m_full, dev, 1, 0).reshape(6, D_MODEL)
    mc = m_full[8].reshape(6, D_MODEL)

    loss_part, grad_x, g, dmx, dmc, first_pairs, first_got = _local_step(a, early, late, mx, mc, conv_w, ci)
    loss = lax.psum(loss_part, ("x", "y", "c"))

    dm_pair = jnp.concatenate([dmx.reshape(1, -1), dmc.reshape(1, -1), jnp.zeros((6, 6 * D_MODEL), F32)], axis=0)
    dm_all = _all_gather8(dm_pair, "gather_dmod").reshape(8, 8, 6 * D_MODEL)
    dm16, gb_mod = _mod_bwd_sum(dm_all)
    dm_loc = lax.dynamic_slice_in_dim(dm16, chip * MOD_COLS, MOD_COLS, 1)
    gw_mod, gcc = _mod_bwd_w(dm_loc, s_b, c_ctx, w_mod_b)

    small_parts = [g[n] for n in SMALL_NAMES] + [gcc[0]]
    small_shapes = [p.shape for p in small_parts]
    sp = _pack_rows(small_parts)
    tot = _sum_slots(_all_gather8(sp, "gather_small_grads").reshape(8, sp.shape[0], ROW), "sum_small_grads")
    small = dict(zip(SMALL_NAMES + ("c_ctx",), _unpack_rows(tot, small_shapes)))
    grads = {n: small[n].reshape(a[n].shape) for n in SMALL_NAMES if n != "conv_w"}
    grads["c_ctx"] = (0.5 * small["c_ctx"]).reshape(a["c_ctx"].shape)
    grads["conv_w"] = lax.dynamic_slice_in_dim(small["conv_w"], chip * (D_FF // N_CHIPS), D_FF // N_CHIPS, 1)[None]
    grads["b_mod"] = gb_mod
    grads["w_mod"] = gw_mod[None]

    last = [g[n] for n in LAST_GRADS]
    last_pairs = [_pair_sum(gf, rv, k, ci, "rs_pair_" + n)
                  for gf, rv, k, n in zip(last, _rs_sibling(last, LAST_KINDS, "rs_sibling_last"), LAST_KINDS, LAST_GRADS)]
    last_got = _rs_chips(last_pairs, LAST_KINDS)
    pos = jnp.stack([ci, chip])
    order = FIRST_GRADS + LAST_GRADS
    order_kinds = FIRST_KINDS + LAST_KINDS
    halves = [_sum_chips(p, t, k, pos, "rs_sum_" + n)
              for p, t, k, n in zip(first_pairs + last_pairs, list(first_got) + list(last_got), order_kinds, order)]
    for n, t in zip(order, _rs_back(halves, order_kinds)):
        grads[n] = t[None]

    delta, new_m, new_v = {}, {}, {}
    for n in BIG_NAMES + ("w_mod",):
        for dst, t in zip((delta, new_m, new_v), _adamw(a[n][0], grads[n][0], a["m_" + n][0], a["v_" + n][0], "adamw_" + n)):
            dst[n] = t[None]
    rest = [n for n in WEIGHT_NAMES if n not in BIG_NAMES and n != "w_mod"]
    shapes = [a[n].shape for n in rest]
    pr = lambda pre: _pack_rows([a[pre + n] for n in rest])
    for dst, t in zip((delta, new_m, new_v),
                      _adamw(pr(""), _pack_rows([grads[n] for n in rest]), pr("m_"), pr("v_"), "adamw_small")):
        dst.update(zip(rest, _unpack_rows(t, shapes)))

    return (loss, grad_x[None], *[grads[n] for n in WEIGHT_NAMES], *[delta[n] for n in WEIGHT_NAMES],
            *[new_m[n] for n in WEIGHT_NAMES], *[new_v[n] for n in WEIGHT_NAMES])


def kernel(x, c, ctx, c_ctx, w_mod, b_mod, norm1_w, w_in, s5_lambda_re_f, s5_lambda_im_f, s5_log_step_f, s5_lambda_re_b, s5_lambda_im_b, s5_log_step_b, s5_b_re, s5_b_im, s5_c_re, s5_c_im, s5_d, s5_w_glu, s5_b_glu, ret_log_decay_f, ret_log_decay_b, w_out, norm2_w, w_up, conv_w, conv_b, w_down, final_norm_w, loss_target, m_c_ctx, m_w_mod, m_b_mod, m_norm1_w, m_w_in, m_s5_lambda_re_f, m_s5_lambda_im_f, m_s5_log_step_f, m_s5_lambda_re_b, m_s5_lambda_im_b, m_s5_log_step_b, m_s5_b_re, m_s5_b_im, m_s5_c_re, m_s5_c_im, m_s5_d, m_s5_w_glu, m_s5_b_glu, m_ret_log_decay_f, m_ret_log_decay_b, m_w_out, m_norm2_w, m_w_up, m_conv_w, m_conv_b, m_w_down, m_final_norm_w, v_c_ctx, v_w_mod, v_b_mod, v_norm1_w, v_w_in, v_s5_lambda_re_f, v_s5_lambda_im_f, v_s5_log_step_f, v_s5_lambda_re_b, v_s5_lambda_im_b, v_s5_log_step_b, v_s5_b_re, v_s5_b_im, v_s5_c_re, v_s5_c_im, v_s5_d, v_s5_w_glu, v_s5_b_glu, v_ret_log_decay_f, v_ret_log_decay_b, v_w_out, v_norm2_w, v_w_up, v_conv_w, v_conv_b, v_w_down, v_final_norm_w):
    return _step(dict(locals()))
```

```python
import functools
import math

import jax
import jax.numpy as jnp
from jax import lax
from jax.experimental import pallas as pl
from jax.experimental.pallas import tpu as pltpu

F32 = jnp.float32
BF16 = jnp.bfloat16

D_MODEL = 1024
S5_WIDTH = 512
S5_GROUPS = 32
S5_GROUP = 16
S5_STATE = 64
RET_WIDTH = 512
RET_HEADS = 4
RET_DH = 128
RET_CHUNK = 256
GRID_W = 64
ROPE_THETA = 10000.0
D_FF = 2816
NORM_EPS = 1e-6
IN_COLS = S5_WIDTH + 4 * RET_WIDTH

S5_T = 16
S5_NB = 4
S5_BW = S5_T * 128
S5_SW = 8 * 2 * S5_STATE

ADAM_LR, ADAM_B1, ADAM_B2, ADAM_EPS, ADAM_WD, ADAM_STEP = 0.001, 0.9, 0.999, 1e-08, 0.01, 10

VMEM_LIMIT = 56 * 1024 * 1024
MM_TN_VMEM = 40 * 1024 * 1024
MESH_ID = pl.DeviceIdType.MESH


def _params(sem=None):
    return pltpu.CompilerParams(dimension_semantics=sem, vmem_limit_bytes=VMEM_LIMIT)


def _full(shape):
    n = len(shape)
    return pl.BlockSpec(shape, lambda *_: (0,) * n)


def _dot(a, b):
    return jnp.dot(a, b, preferred_element_type=F32)


def _dot_nt(a, b):
    return lax.dot_general(a, b, (((1,), (1,)), ((), ())), preferred_element_type=F32)


def _dot_tn(a, b):
    return lax.dot_general(a, b, (((0,), (0,)), ((), ())), preferred_element_type=F32)


def _dot_hi(a, b):
    return jnp.dot(a, b, preferred_element_type=F32, precision=lax.Precision.HIGHEST)


def _dot_nt_hi(a, b):
    return lax.dot_general(a, b, (((1,), (1,)), ((), ())), preferred_element_type=F32,
                           precision=lax.Precision.HIGHEST)


def _gelu(x):
    return 0.5 * x * (1.0 + jnp.tanh(0.7978845608028654 * (x + 0.044715 * (x * x * x))))


def _gelu_and_grad(x):
    c, ca = 0.7978845608028654, 0.7978845608028654 * 0.044715
    x2 = x * x
    t = jnp.tanh(x * (c + ca * x2))
    h = 0.5 * x
    return h + h * t, 0.5 + 0.5 * t + h * (1.0 - t * t) * (c + 3.0 * ca * x2)


def _sigmoid(x):
    return 1.0 / (1.0 + jnp.exp(-x))


def _silu(x):
    return x * _sigmoid(x)


def _rms_mod(x, nw, sh, sc):
    r = lax.rsqrt(jnp.mean(x * x, axis=-1, keepdims=True) + NORM_EPS)
    return (x * r * nw) * (1.0 + sc) + sh


def _rms(x, nw):
    r = lax.rsqrt(jnp.mean(x * x, axis=-1, keepdims=True) + NORM_EPS)
    return x * r * nw


def _head_norm_gate(y, g):
    mu = jnp.mean(y, axis=-1, keepdims=True)
    yc = y - mu
    var = jnp.mean(yc * yc, axis=-1, keepdims=True)
    return _silu(g) * (yc * lax.rsqrt(var + NORM_EPS))


def _swap_pairs(t):
    lane = lax.broadcasted_iota(jnp.int32, t.shape, 1)
    return jnp.where(lane % 2 == 0, pltpu.roll(t, RET_DH - 1, 1), pltpu.roll(t, 1, 1))


def _rope(t, cos_t, sin_t):
    return t * cos_t + _swap_pairs(t) * sin_t


def _rope_t(dt, cos_t, sin_t):
    return dt * cos_t + _swap_pairs(dt * sin_t)


def _pick(n, prefs):
    for p in prefs:
        if n % p == 0:
            return p
    return n


def _mm_tn(a, b, *, name):
    m, k = a.shape
    n = b.shape[1]
    tn = _pick(n, (1408, 1024, 1280, 512))
    fits = lambda t: 2 * (2 * t * k + 2 * t * tn + 4 * k * tn) <= MM_TN_VMEM
    tm = _pick(m, [t for t in (2816, 2048, 1024, 768, 512, 256) if fits(t)] + [128])

    def body(a_ref, b_ref, o_ref):
        @pl.when(pl.program_id(1) == 0)
        def _():
            o_ref[...] = jnp.zeros_like(o_ref)
        o_ref[...] += _dot_tn(a_ref[...], b_ref[...])

    return pl.pallas_call(
        body, name=name, grid=(n // tn, m // tm),
        in_specs=[pl.BlockSpec((tm, k), lambda j, i: (i, 0)), pl.BlockSpec((tm, tn), lambda j, i: (i, j))],
        out_specs=pl.BlockSpec((k, tn), lambda j, i: (0, j)),
        out_shape=jax.ShapeDtypeStruct((k, n), F32),
        compiler_params=_params(("parallel", "arbitrary")),
    )(a, b)


TOK_TILE = 256


def _behind(step, last, copies):
    @pl.when(step == 0)
    def _():
        for cp in copies(with_arrivals=False)[0]:
            cp.start()

    @pl.when(step == last)
    def _():
        sends, arrivals = copies()
        for cp in arrivals:
            cp.wait_recv()
        for cp in sends:
            cp.wait_send()


def _s5_put_rows(rows_ref, scr, val):
    nchunk = scr.shape[0]
    for c in range(nchunk):
        scr[c] = val[S5_T * c:S5_T * (c + 1), :]
    for b in range(S5_NB):
        for j in range(S5_T):
            rows_ref[b, :, 128 * j:128 * (j + 1)] = scr[:, j, 128 * b:128 * (b + 1)].astype(BF16)


def _norm_inproj(x, ctx, n1w, mod4, w_in_b, cos_t, sin_t, placed, kinds):
    l, lc = x.shape[0], ctx.shape[0]
    tm = TOK_TILE
    nct = lc // tm
    la = l + lc
    n = len(placed)
    shard_shapes = _gather_shard_shapes(placed, kinds)

    def body(x_ref, c_ref, nw_ref, mod_ref, w_ref, cos_ref, sin_ref, *rest):
        p_ref, h_ref, u_ref = rest[n:n + 3]
        send_sems, recv_sems, u_scr = rest[2 * n + 3:]
        _behind(pl.program_id(0), la // tm - 1,
                functools.partial(_gather_chip_copies, rest[n + 3:2 * n + 3], kinds, shard_shapes, send_sems, recv_sems))
        is_ctx = pl.program_id(0) < nct
        xt = jnp.where(is_ctx, c_ref[...], x_ref[...])
        sh = jnp.where(is_ctx, mod_ref[0:1, :], mod_ref[2:3, :])
        sc = jnp.where(is_ctx, mod_ref[1:2, :], mod_ref[3:4, :])
        hb = _rms_mod(xt, nw_ref[...], sh, sc).astype(BF16)
        h_ref[...] = hb
        p = _dot(hb, w_ref[...])
        p_ref[...] = p
        cs, sn = cos_ref[...], sin_ref[...]
        for h in range(RET_HEADS):
            q_cols = slice(RET_WIDTH + RET_DH * h, RET_WIDTH + RET_DH * (h + 1))
            k_cols = slice(2 * RET_WIDTH + RET_DH * h, 2 * RET_WIDTH + RET_DH * (h + 1))
            p_ref[:, q_cols] = _rope(p[:, q_cols], cs, sn)
            p_ref[:, k_cols] = _rope(p[:, k_cols] * K_SCALE, cs, sn)
        _s5_put_rows(u_ref, u_scr, p[:, 0:S5_WIDTH])

    return pl.pallas_call(
        body, name="norm_inproj", grid=(la // tm,),
        in_specs=[pl.BlockSpec((tm, D_MODEL), lambda i: (jnp.maximum(i - nct, 0), 0)),
                  pl.BlockSpec((tm, D_MODEL), lambda i: (jnp.minimum(i, nct - 1), 0)),
                  _full((1, D_MODEL)), _full((4, D_MODEL)), _full((D_MODEL, IN_COLS)),
                  pl.BlockSpec((tm, RET_DH), lambda i: (i, 0)), pl.BlockSpec((tm, RET_DH), lambda i: (i, 0))] + [ANY] * n,
        out_specs=[pl.BlockSpec((tm, IN_COLS), lambda i: (i, 0)), pl.BlockSpec((tm, D_MODEL), lambda i: (i, 0)),
                   pl.BlockSpec((S5_NB, tm // S5_T, S5_BW), lambda i: (0, i, 0))] + [ANY] * n,
        out_shape=[jax.ShapeDtypeStruct((la, IN_COLS), F32), jax.ShapeDtypeStruct((la, D_MODEL), BF16),
                   jax.ShapeDtypeStruct((S5_NB, la // S5_T, S5_BW), BF16)]
        + [jax.ShapeDtypeStruct(p.shape, p.dtype) for p in placed],
        input_output_aliases={7 + a: 3 + a for a in range(n)},
        scratch_shapes=[pltpu.SemaphoreType.DMA((n, 3)), pltpu.SemaphoreType.DMA((n, 3)),
                        pltpu.VMEM((tm // S5_T, S5_T, S5_WIDTH), F32)],
        compiler_params=_params(("arbitrary",)),
    )(x, ctx, n1w, mod4, w_in_b, cos_t, sin_t, *placed)


def _iota2(shape, dim):
    return lax.broadcasted_iota(jnp.int32, shape, dim)


def _group_mask(rows, cols, row_div, col_div):
    return jnp.where(_iota2((rows, cols), 0) // row_div == _iota2((rows, cols), 1) // col_div, 1.0, 0.0).astype(F32)


def _s5_gen_dir(lre, lim, lst, b_re, b_im, c_re, c_im):
    step = jnp.exp(lst)
    mag = jnp.exp(lre * step)
    ar = mag * jnp.cos(lim * step)
    ai = mag * jnp.sin(lim * step)
    den = lre * lre + lim * lim
    xr = ar - 1.0
    cr = (xr * lre + ai * lim) / den
    ci = (ai * lre - xr * lim) / den
    rexp = _group_mask(128, 8, S5_GROUP, 1)
    are, aie = _dot_hi(rexp, ar), _dot_hi(rexp, ai)
    cre, cie = _dot_hi(rexp, cr), _dot_hi(rexp, ci)
    bbr = cre * b_re - cie * b_im
    bbi = cre * b_im + cie * b_re
    gmask = _group_mask(128, 128, S5_GROUP, S5_GROUP)
    pr, pi = jnp.ones_like(are), jnp.zeros_like(are)
    xs, ys = [], []
    for t in range(S5_T + 1):
        if t < S5_T:
            xs.append(jnp.concatenate([bbr * pr - bbi * pi, bbr * pi + bbi * pr], axis=1))
        ys.append(jnp.concatenate([c_re * pr - c_im * pi, -(c_re * pi + c_im * pr)], axis=1))
        pr, pi = pr * are - pi * aie, pr * aie + pi * are
    gs = [_dot_nt_hi(x_t, ys[0]) * gmask for x_t in xs]
    r16, i16 = ar, ai
    for _ in range(4):
        r16, i16 = r16 * r16 - i16 * i16, 2.0 * r16 * i16
    return xs, ys, gs, jnp.concatenate([r16, i16], axis=1)


def _s5_expand(z):
    return jnp.concatenate([z] * 8, axis=1) * _group_mask(128, S5_SW, S5_GROUP, 128)


def _s5_contract(z):
    zm = z * _group_mask(128, S5_SW, S5_GROUP, 128)
    acc = zm[:, 0:128]
    for k in range(1, 8):
        acc = acc + zm[:, 128 * k:128 * (k + 1)]
    return acc


def _s5_param_specs():
    blk3 = lambda r, c: pl.BlockSpec((1, 1, r, c), lambda b, *_: (0, b, 0, 0))
    dir3 = lambda r, c: pl.BlockSpec((2, 1, r, c), lambda b, *_: (0, b, 0, 0))
    return [dir3(8, S5_STATE), dir3(8, S5_STATE), dir3(8, 1), blk3(128, S5_STATE), blk3(128, S5_STATE),
            blk3(128, S5_STATE), blk3(128, S5_STATE), blk3(1, 128)]


def _s5_gen(lre, lim, lst, b_re, b_im, c_re, c_im, dvec, placed, kinds):
    n = len(placed)
    shard_shapes = _gather_shard_shapes(placed, kinds)

    def body(lre_ref, lim_ref, lst_ref, bre_ref, bim_ref, cre_ref, cim_ref, d_ref, *rest):
        gg_ref, xw_ref, yw_ref, a16_ref = rest[n:n + 4]
        _behind(pl.program_id(0), S5_NB - 1,
                functools.partial(_gather_chip_copies, rest[n + 4:2 * n + 4], kinds, shard_shapes, *rest[2 * n + 4:]))
        eye = _group_mask(128, 128, 1, 1)
        g0 = eye * d_ref[0, 0]
        for dr in range(2):
            xs, ys, gs, a16 = _s5_gen_dir(lre_ref[dr, 0], lim_ref[dr, 0], lst_ref[dr, 0], bre_ref[0, 0],
                                          bim_ref[0, 0], cre_ref[0, 0], cim_ref[0, 0])
            a16_ref[0, dr] = a16
            for j in range(S5_T):
                xw_ref[0, dr, j] = xs[S5_T - 1 - j if dr == 0 else j]
                yw_ref[0, dr, j] = ys[j + 1 if dr == 0 else S5_T - j]
            g0 = g0 + gs[0]
            for t in range(1, S5_T):
                gg_ref[0, (S5_T - 1) + t if dr == 0 else (S5_T - 1) - t] = gs[t]
        gg_ref[0, S5_T - 1] = g0

    blk = pl.BlockSpec((1, 2, S5_T, 128, 128), lambda b: (b, 0, 0, 0, 0))
    return pl.pallas_call(
        body, name="s5_gen", grid=(S5_NB,),
        in_specs=_s5_param_specs() + [ANY] * n,
        out_specs=[pl.BlockSpec((1, 2 * S5_T - 1, 128, 128), lambda b: (b, 0, 0, 0)), blk, blk,
                   pl.BlockSpec((1, 2, 8, 128), lambda b: (b, 0, 0, 0))] + [ANY] * n,
        out_shape=[jax.ShapeDtypeStruct((S5_NB, 2 * S5_T - 1, 128, 128), F32),
                   jax.ShapeDtypeStruct((S5_NB, 2, S5_T, 128, 128), F32),
                   jax.ShapeDtypeStruct((S5_NB, 2, S5_T, 128, 128), F32),
                   jax.ShapeDtypeStruct((S5_NB, 2, 8, 128), F32)]
        + [jax.ShapeDtypeStruct(p.shape, p.dtype) for p in placed],
        input_output_aliases={8 + a: 4 + a for a in range(n)},
        scratch_shapes=[pltpu.SemaphoreType.DMA((n, 3)), pltpu.SemaphoreType.DMA((n, 3))],
        compiler_params=_params(("arbitrary",)),
    )(lre, lim, lst, b_re, b_im, c_re, c_im, dvec, *placed)


def _s5_fill_state_mat(w_scr, src_ref, dr):
    for j in range(S5_T):
        w_scr[128 * j:128 * (j + 1), :] = _s5_expand(src_ref[0, dr, j]).astype(BF16)


def _s5_fill_toeplitz(k_scr, gg_ref):
    for j in range(S5_T):
        for i in range(S5_T):
            k_scr[128 * j:128 * (j + 1), 128 * i:128 * (i + 1)] = gg_ref[0, i - j + (S5_T - 1)].astype(BF16)


S5_GEN_SPECS = [pl.BlockSpec((1, 2 * S5_T - 1, 128, 128), lambda b: (b, 0, 0, 0)),
                pl.BlockSpec((1, 2, S5_T, 128, 128), lambda b: (b, 0, 0, 0, 0))]


def _s5_gen_bwd(lre, lim, lst, b_re, b_im, c_re, c_im, dvec, dg, dx, dy, da16):
    def body(lre_ref, lim_ref, lst_ref, bre_ref, bim_ref, cre_ref, cim_ref, d_ref, dg_ref, dx_ref, dy_ref, da16_ref,
             glre_ref, glim_ref, glst_ref, gbre_ref, gbim_ref, gcre_ref, gcim_ref, gd_ref):
        eye = _group_mask(128, 128, 1, 1)
        gd_ref[0, 0] = jnp.sum(dg_ref[0, S5_T - 1] * eye, axis=0, keepdims=True)
        gb = [None, None, None, None]
        for dr in range(2):
            args = (lre_ref[dr, 0], lim_ref[dr, 0], lst_ref[dr, 0], bre_ref[0, 0], bim_ref[0, 0],
                    cre_ref[0, 0], cim_ref[0, 0])
            _, vjp = jax.vjp(_s5_gen_dir, *args)
            dxs = [dx_ref[0, dr, S5_T - 1 - t if dr == 0 else t] for t in range(S5_T)]
            dys = [jnp.zeros((128, 128), F32)] + [dy_ref[0, dr, t - 1 if dr == 0 else S5_T - t]
                                                  for t in range(1, S5_T + 1)]
            dgs = [dg_ref[0, (S5_T - 1) + t if dr == 0 else (S5_T - 1) - t] for t in range(S5_T)]
            g = vjp((dxs, dys, dgs, da16_ref[0, dr]))
            glre_ref[dr, 0] = g[0]
            glim_ref[dr, 0] = g[1]
            glst_ref[dr, 0] = g[2]
            for q in range(4):
                gb[q] = g[3 + q] if gb[q] is None else gb[q] + g[3 + q]
        gbre_ref[0, 0] = gb[0]
        gbim_ref[0, 0] = gb[1]
        gcre_ref[0, 0] = gb[2]
        gcim_ref[0, 0] = gb[3]

    shp = lambda a: jax.ShapeDtypeStruct(a.shape, F32)
    return pl.pallas_call(
        body, name="s5_gen_bwd", grid=(S5_NB,),
        in_specs=_s5_param_specs() + [
            pl.BlockSpec((1, 2 * S5_T - 1, 128, 128), lambda b: (b, 0, 0, 0)),
            pl.BlockSpec((1, 2, S5_T, 128, 128), lambda b: (b, 0, 0, 0, 0)),
            pl.BlockSpec((1, 2, S5_T, 128, 128), lambda b: (b, 0, 0, 0, 0)),
            pl.BlockSpec((1, 2, 8, 128), lambda b: (b, 0, 0, 0))],
        out_specs=_s5_param_specs(),
        out_shape=[shp(lre), shp(lim), shp(lst), shp(b_re), shp(b_im), shp(c_re), shp(c_im), shp(dvec)],
        compiler_params=_params(("parallel",)),
    )(lre, lim, lst, b_re, b_im, c_re, c_im, dvec, dg, dx, dy, da16)


def _s5_put_groups(o_ref, dr, val):
    for gi in range(8):
        o_ref[dr, :, gi, :] = val[:, 128 * gi:128 * (gi + 1)]


def _s5_get_groups(s_ref, dr, n=8):
    return jnp.concatenate([s_ref[dr, :, gi, :] for gi in range(n)], axis=1).astype(BF16)


def _s5_to_states(u3, blocks, name):
    cn = u3.shape[1]

    def body(u_ref, b_ref, o_ref, w_scr):
        u = u_ref[0]
        for dr in range(2):
            _s5_fill_state_mat(w_scr, b_ref, dr)
            _s5_put_groups(o_ref, dr, _dot(u, w_scr[...]))

    return pl.pallas_call(
        body, name=name, grid=(S5_NB,),
        in_specs=[pl.BlockSpec((1, cn, S5_BW), lambda b: (b, 0, 0)), S5_GEN_SPECS[1]],
        out_specs=pl.BlockSpec((2, cn, 8, 128), lambda b: (0, 0, b, 0)),
        out_shape=jax.ShapeDtypeStruct((2, cn, S5_GROUPS, 128), F32),
        scratch_shapes=[pltpu.VMEM((S5_BW, S5_SW), BF16)],
        compiler_params=_params(("parallel",)),
    )(u3, blocks)


def _s5_from_states(u3, gg, st, blocks, transposed, name):
    cn = u3.shape[1]

    def body(u_ref, g_ref, s_ref, b_ref, o_ref, k_scr, w_scr):
        u = u_ref[0]
        _s5_fill_toeplitz(k_scr, g_ref)
        y = _dot_nt(u, k_scr[...]) if transposed else _dot(u, k_scr[...])
        for dr in range(2):
            _s5_fill_state_mat(w_scr, b_ref, dr)
            y = y + _dot_nt(_s5_get_groups(s_ref, dr), w_scr[...])
        for i in range(S5_T):
            o_ref[:, i, :] = y[:, 128 * i:128 * (i + 1)]

    return pl.pallas_call(
        body, name=name, grid=(S5_NB,),
        in_specs=[pl.BlockSpec((1, cn, S5_BW), lambda b: (b, 0, 0)), S5_GEN_SPECS[0],
                  pl.BlockSpec((2, cn, 8, 128), lambda b: (0, 0, b, 0)), S5_GEN_SPECS[1]],
        out_specs=pl.BlockSpec((cn, S5_T, 128), lambda b: (0, 0, b)),
        out_shape=jax.ShapeDtypeStruct((cn, S5_T, S5_WIDTH), F32),
        scratch_shapes=[pltpu.VMEM((S5_BW, S5_BW), BF16), pltpu.VMEM((S5_BW, S5_SW), BF16)],
        compiler_params=_params(("parallel",)),
    )(u3, gg, st, blocks)


def _s5_a_forms(a):
    ra = pltpu.roll(a, S5_STATE, 1)
    low = _iota2(a.shape, 1) < S5_STATE
    return jnp.where(low, a, ra), jnp.where(low, -ra, a)


def _s5_scan(sloc, a16, ncc, placed, kinds):
    cn = sloc.shape[1]
    n = len(placed)
    shard_shapes = _gather_shard_shapes(placed, kinds)

    def body(s_ref, a_ref, *rest):
        h_ref = rest[n]
        sends, arrivals = _gather_chip_copies(rest[n + 1:2 * n + 1], kinds, shard_shapes, *rest[2 * n + 1:])
        for cp in sends:
            cp.start()
        forms = [_s5_a_forms(a_ref[dr]) for dr in range(2)]

        def step(s, hs):
            out = []
            for dr in range(2):
                arr, aii = forms[dr]
                h, rh = hs[dr]
                c = s if dr == 0 else jnp.where(s < ncc, ncc - 1 - s, cn - 1 - (s - ncc))
                h_ref[dr, c] = h
                sc = s_ref[dr, c]
                out.append((h * arr + rh * aii + sc, rh * arr - h * aii + pltpu.roll(sc, S5_STATE, 1)))
            return tuple(out)

        zero = jnp.zeros((S5_GROUPS, 128), F32)
        lax.fori_loop(0, cn, step, ((zero, zero), (zero, zero)), unroll=4)
        for cp in arrivals:
            cp.wait_recv()
        for cp in sends:
            cp.wait_send()

    vmem = pl.BlockSpec(memory_space=pltpu.VMEM)
    return pl.pallas_call(
        body, name="s5_scan",
        in_specs=[vmem, vmem] + [ANY] * n, out_specs=[vmem] + [ANY] * n,
        out_shape=[jax.ShapeDtypeStruct(sloc.shape, F32)] + [jax.ShapeDtypeStruct(p.shape, p.dtype) for p in placed],
        input_output_aliases={2 + a: 1 + a for a in range(n)},
        scratch_shapes=[pltpu.SemaphoreType.DMA((n, 3)), pltpu.SemaphoreType.DMA((n, 3))],
        compiler_params=_params(),
    )(sloc, a16, *placed)


def _s5_scan_bwd(e, hs, a16, ncc):
    cn = e.shape[1]

    def body(e_ref, h_ref, a_ref, ds_ref, da_ref):
        forms = [_s5_a_forms(a_ref[dr]) for dr in range(2)]
        low = _iota2((S5_GROUPS, 128), 1) < S5_STATE

        def step(s, carry):
            out = []
            r = cn - 1 - s
            for dr in range(2):
                arr, aii = forms[dr]
                g, rg, da = carry[dr]
                c = r if dr == 0 else jnp.where(r < ncc, ncc - 1 - r, cn - 1 - (r - ncc))
                ds_ref[dr, c] = g
                h = h_ref[dr, c]
                rh = pltpu.roll(h, S5_STATE, 1)
                da = da + jnp.where(low, g * h + rg * rh, g * rh - rg * h)
                ec = e_ref[dr, c]
                out.append((ec + g * arr - rg * aii, pltpu.roll(ec, S5_STATE, 1) + rg * arr + g * aii, da))
            return tuple(out)

        zero = jnp.zeros((S5_GROUPS, 128), F32)
        res = lax.fori_loop(0, cn, step, ((zero, zero, zero), (zero, zero, zero)), unroll=4)
        da_ref[0] = res[0][2]
        da_ref[1] = res[1][2]

    return pl.pallas_call(
        body, name="s5_scan_bwd",
        out_shape=[jax.ShapeDtypeStruct(e.shape, F32), jax.ShapeDtypeStruct((2, S5_GROUPS, 128), F32)],
        compiler_params=_params(),
    )(e, hs, a16)


def _s5_bwd_kb(p3, dy3):
    cn = p3.shape[1]
    half = S5_T // 2

    def body(u_ref, d_ref, o_ref):
        q = pl.program_id(1)

        @pl.when(q == 0)
        def _():
            o_ref[...] = jnp.zeros_like(o_ref)

        dk = _dot_tn(u_ref[0], d_ref[0])
        for j in range(S5_T):
            for i in range(half):
                o_ref[0, half * q + i - j + (S5_T - 1)] += dk[128 * j:128 * (j + 1), 128 * i:128 * (i + 1)]

    return pl.pallas_call(
        body, name="s5_bwd_kb", grid=(S5_NB, 2),
        in_specs=[pl.BlockSpec((1, cn, S5_BW), lambda b, q: (b, 0, 0)),
                  pl.BlockSpec((1, cn, S5_BW // 2), lambda b, q: (b, 0, q))],
        out_specs=pl.BlockSpec((1, 2 * S5_T - 1, 128, 128), lambda b, q: (b, 0, 0, 0)),
        out_shape=jax.ShapeDtypeStruct((S5_NB, 2 * S5_T - 1, 128, 128), F32),
        compiler_params=_params(("parallel", "arbitrary")),
    )(p3, dy3)


def _s5_bwd_w(u3, st, name):
    cn = u3.shape[1]

    def body(u_ref, s_ref, w_ref):
        dw = _dot_tn(u_ref[0], _s5_get_groups(s_ref, 0))
        for j in range(S5_T):
            w_ref[0, 0, j] = _s5_contract(dw[128 * j:128 * (j + 1), :])

    return pl.pallas_call(
        body, name=name, grid=(S5_NB, 2),
        in_specs=[pl.BlockSpec((1, cn, S5_BW), lambda b, q: (b, 0, 0)),
                  pl.BlockSpec((1, cn, 8, 128), lambda b, q: (q, 0, b, 0))],
        out_specs=pl.BlockSpec((1, 1, S5_T, 128, 128), lambda b, q: (b, q, 0, 0, 0)),
        out_shape=jax.ShapeDtypeStruct((S5_NB, 2, S5_T, 128, 128), F32),
        compiler_params=_params(("parallel", "parallel")),
    )(u3, st)


K_SCALE = RET_DH ** -0.5
G_COL = 16


def _ret_chunk_of(step, ncc, nch, rev):
    if not rev:
        return step
    return jnp.where(step < ncc, ncc - 1 - step, nch - 1 - (step - ncc))


def _ret_decay(ld, rev):
    c = _iota2((RET_CHUNK, RET_CHUNK), 0).astype(F32)
    m = _iota2((RET_CHUNK, RET_CHUNK), 1).astype(F32)
    diff = (m - c) if rev else (c - m)
    keep = (diff > 0) if rev else (diff >= 0)
    expo = jnp.maximum(diff, 0.0)
    dm = jnp.where(keep, jnp.exp(ld * expo), 0.0)
    xi_e = (RET_CHUNK - c) if rev else (c + 1.0)
    zeta_e = c if rev else (RET_CHUNK - 1.0 - c)
    return dm, expo, jnp.exp(ld * xi_e), xi_e, jnp.exp(ld * zeta_e), zeta_e


RET_TABLES = 7


def _ret_tables(ld2):
    def body(ld_ref, t_ref):
        dr, h = pl.program_id(0), pl.program_id(1)
        ldh = ld_ref[dr, h]
        for rev in (False, True):
            @pl.when(dr == int(rev))
            def _(rev=rev):
                dm, expo, xi, xi_e, zeta, zeta_e = _ret_decay(ldh, rev)
                t_ref[0, 0, 0] = dm
                t_ref[0, 0, 1] = dm * expo
                t_ref[0, 0, 2] = xi
                t_ref[0, 0, 3] = xi * xi_e
                t_ref[0, 0, 4] = zeta
                t_ref[0, 0, 5] = zeta * zeta_e
                t_ref[0, 0, 6] = jnp.zeros_like(dm) + jnp.exp(ldh * RET_CHUNK)

    return pl.pallas_call(
        body, name="ret_tables", grid=(2, RET_HEADS),
        in_specs=[pl.BlockSpec(memory_space=pltpu.SMEM)],
        out_specs=pl.BlockSpec((1, 1, RET_TABLES, RET_CHUNK, RET_CHUNK), lambda d, h: (d, h, 0, 0, 0)),
        out_shape=jax.ShapeDtypeStruct((2, RET_HEADS, RET_TABLES, RET_CHUNK, RET_CHUNK), F32),
        compiler_params=_params(("parallel", "parallel")),
    )(ld2)


def _ret_specs(nch, ncc, rev, step_of):
    chunk = lambda n: _ret_chunk_of(step_of(n), ncc, nch, rev)
    cols = [pl.BlockSpec((RET_CHUNK, RET_WIDTH), functools.partial(lambda n, cb: (chunk(n), cb), cb=cb))
            for cb in (1, 2, 3)]
    return cols, pl.BlockSpec((RET_CHUNK, RET_WIDTH), lambda n: (chunk(n), 0))


def _ret_scan(p_all, tabs, ncc):
    la = p_all.shape[0]
    nch = la // RET_CHUNK

    def body(t_ref, qf, kf, vf, qb, kb, vb, of_ref, ob_ref, ssf_ref, ssb_ref, s_scr):
        @pl.when(pl.program_id(0) == 0)
        def _():
            s_scr[...] = jnp.zeros_like(s_scr)

        for dr, (q_ref, k_ref, v_ref, o_ref, ss_ref) in enumerate(
                ((qf, kf, vf, of_ref, ssf_ref), (qb, kb, vb, ob_ref, ssb_ref))):
            for h in range(RET_HEADS):
                sl = slice(RET_DH * h, RET_DH * (h + 1))
                dm, xi, zeta = t_ref[dr, h, 0], t_ref[dr, h, 2, :, 0:RET_DH], t_ref[dr, h, 4, :, 0:RET_DH]
                q, k = q_ref[:, sl], k_ref[:, sl]
                vh = v_ref[:, sl].astype(BF16)
                s = s_scr[dr, h]
                ss_ref[0, h] = s
                sc = (_dot_nt(q.astype(BF16), k.astype(BF16)) * dm).astype(BF16)
                o_ref[:, sl] = _dot(sc, vh) + _dot((q * xi).astype(BF16), s.astype(BF16))
                s_scr[dr, h] = t_ref[dr, h, 6, 0:RET_DH, 0:RET_DH] * s + _dot_tn((k * zeta).astype(BF16), vh)

    in_f, out_f = _ret_specs(nch, ncc, False, lambda n: n)
    in_b, out_b = _ret_specs(nch, ncc, True, lambda n: n)
    ss_spec = pl.BlockSpec((1, RET_HEADS, RET_DH, RET_DH), lambda n: (n, 0, 0, 0))
    o_shape = jax.ShapeDtypeStruct((la, RET_WIDTH), F32)
    ss_shape = jax.ShapeDtypeStruct((nch, RET_HEADS, RET_DH, RET_DH), F32)
    return pl.pallas_call(
        body, name="ret_scan", grid=(nch,),
        in_specs=[_full(tabs.shape)] + in_f + in_b,
        out_specs=[out_f, out_b, ss_spec, ss_spec],
        out_shape=[o_shape, o_shape, ss_shape, ss_shape],
        scratch_shapes=[pltpu.VMEM((2, RET_HEADS, RET_DH, RET_DH), F32)],
        compiler_params=_params(("arbitrary",)),
    )(tabs, p_all, p_all, p_all, p_all, p_all, p_all)


def _ret_scan_bwd(p_all, tabs, ssf, ssb, dy_all, ncc):
    la = p_all.shape[0]
    nch = la // RET_CHUNK

    def body(t_ref, qf, kf, vf, dof, ssf_ref, qb, kb, vb, dob_, ssb_ref,
             dqf, dkf, dvf, dqb, dkb, dvb, dld_ref, ds_scr):
        @pl.when(pl.program_id(0) == 0)
        def _():
            ds_scr[...] = jnp.zeros_like(ds_scr)
            dld_ref[...] = jnp.zeros_like(dld_ref)

        for dr, (q_ref, k_ref, v_ref, do_ref, ss_ref, dq_ref, dk_ref, dv_ref) in enumerate(
                ((qf, kf, vf, dof, ssf_ref, dqf, dkf, dvf), (qb, kb, vb, dob_, ssb_ref, dqb, dkb, dvb))):
            on_ctx = _ret_chunk_of(nch - 1 - pl.program_id(0), ncc, nch, dr == 1) < ncc
            for h in range(RET_HEADS):
                sl = slice(RET_DH * h, RET_DH * (h + 1))
                dm, dm_d = t_ref[dr, h, 0], t_ref[dr, h, 1]
                xi, xi_d, zeta, zeta_d = [t_ref[dr, h, t, :, 0:RET_DH] for t in (2, 3, 4, 5)]
                gc = t_ref[dr, h, 6, 0:RET_DH, 0:RET_DH]
                q, k = q_ref[:, sl], k_ref[:, sl]
                q16, k16, v16 = q.astype(BF16), k.astype(BF16), v_ref[:, sl].astype(BF16)
                s = ss_ref[0, h]
                s16 = s.astype(BF16)
                ds_in = ds_scr[dr, h]
                ds16 = ds_in.astype(BF16)
                do16 = jnp.where(on_ctx, 0.0, do_ref[:, sl]).astype(BF16)
                qk = _dot_nt(q16, k16)
                dsv = _dot_nt(do16, v16)
                dsc = (dsv * dm).astype(BF16)
                sc16 = (qk * dm).astype(BF16)
                dos = _dot_nt(do16, s16)
                vds = _dot_nt(v16, ds16)
                dq_ref[:, sl] = _dot(dsc, k16) + dos * xi
                dk_ref[:, sl] = _dot_tn(dsc, q16) + vds * zeta
                dv_ref[:, sl] = _dot_tn(sc16, do16) + _dot((k * zeta).astype(BF16), ds16)
                ds_scr[dr, h] = _dot_tn((q * xi).astype(BF16), do16) + gc * ds_in
                dld = (jnp.sum(dsv * qk * dm_d) + jnp.sum(q * dos * xi_d + k * vds * zeta_d)
                       + RET_CHUNK * jnp.sum(gc * s * ds_in))
                dld_ref[dr, h] += dld

    back = lambda n: nch - 1 - n
    in_f, out_f = _ret_specs(nch, ncc, False, back)
    in_b, out_b = _ret_specs(nch, ncc, True, back)
    ss_spec = pl.BlockSpec((1, RET_HEADS, RET_DH, RET_DH), lambda n: (nch - 1 - n, 0, 0, 0))
    shp = jax.ShapeDtypeStruct((la, RET_WIDTH), F32)
    dy_spec = lambda rev: pl.BlockSpec(
        (RET_CHUNK, RET_WIDTH), lambda n: (jnp.maximum(_ret_chunk_of(nch - 1 - n, ncc, nch, rev) - ncc, 0), 0))
    return pl.pallas_call(
        body, name="ret_scan_bwd", grid=(nch,),
        in_specs=[_full(tabs.shape)] + in_f + [dy_spec(False), ss_spec] + in_b + [dy_spec(True), ss_spec],
        out_specs=[out_f, out_f, out_f, out_b, out_b, out_b, _full((2, RET_HEADS, 8, 128))],
        out_shape=[shp] * 6 + [jax.ShapeDtypeStruct((2, RET_HEADS, 8, 128), F32)],
        scratch_shapes=[pltpu.VMEM((2, RET_HEADS, RET_DH, RET_DH), F32)],
        compiler_params=_params(("arbitrary",)),
    )(tabs, p_all, p_all, p_all, dy_all, ssf, p_all, p_all, p_all, dy_all, ssb)


def _in_bwd(dqf, dkf, dvf, dqb, dkb, dvb, du, dg, cos_t, sin_t, w_in_b, x, ctx, n1w, mod4, dx1):
    l, lc = x.shape[0], ctx.shape[0]
    la = l + lc
    tm = TOK_TILE
    nct = lc // tm

    def body(dqf_ref, dkf_ref, dvf_ref, dqb_ref, dkb_ref, dvb_ref, du_ref, dg_ref, cos_ref, sin_ref,
             w_ref, x_ref, c_ref, nw_ref, mod_ref, dx1_ref, dp_ref, gx_ref, acc_ref):
        i = pl.program_id(0)
        is_ctx = i < nct

        @pl.when(i == 0)
        def _():
            acc_ref[...] = jnp.zeros_like(acc_ref)

        cs, sn = cos_ref[...], sin_ref[...]
        def piece(k, val):
            cols = slice(S5_WIDTH * k, S5_WIDTH * (k + 1))
            dp_ref[:, cols] = val.astype(BF16)
            return _dot_nt(dp_ref[:, cols], w_ref[:, cols])

        dh1 = piece(0, du_ref[...])
        dh1 = dh1 + piece(3, dvf_ref[...] + dvb_ref[...])
        dh1 = dh1 + piece(4, jnp.where(is_ctx, 0.0, dg_ref[...]))
        for k, (f_ref, b_ref, scale) in ((1, (dqf_ref, dqb_ref, 1.0)), (2, (dkf_ref, dkb_ref, K_SCALE))):
            heads = [_rope_t(f_ref[:, RET_DH * h:RET_DH * (h + 1)] + b_ref[:, RET_DH * h:RET_DH * (h + 1)], cs, sn) * scale
                     for h in range(RET_HEADS)]
            dh1 = dh1 + piece(k, jnp.concatenate(heads, axis=1))
        xt = jnp.where(is_ctx, c_ref[...], x_ref[...])
        sh = jnp.where(is_ctx, mod_ref[0:1, :], mod_ref[2:3, :])
        sc = jnp.where(is_ctx, mod_ref[1:2, :], mod_ref[3:4, :])
        _, vjp = jax.vjp(_rms_mod, xt, nw_ref[...], sh, sc)
        dx, dnw, dsh, dsc = vjp(dh1)
        gx_ref[...] = dx + dx1_ref[...]
        cf = jnp.where(is_ctx, 1.0, 0.0)
        acc_ref[0:1, :] += dnw
        acc_ref[1:2, :] += cf * dsh
        acc_ref[2:3, :] += cf * dsc
        acc_ref[3:4, :] += (1.0 - cf) * dsh
        acc_ref[4:5, :] += (1.0 - cf) * dsc

    row = pl.BlockSpec((tm, RET_WIDTH), lambda i: (i, 0))
    tab = pl.BlockSpec((tm, RET_DH), lambda i: (i, 0))
    xrow = pl.BlockSpec((tm, D_MODEL), lambda i: (jnp.maximum(i - nct, 0), 0))
    return pl.pallas_call(
        body, name="in_bwd", grid=(la // tm,),
        in_specs=[row] * 7 + [pl.BlockSpec((tm, RET_WIDTH), lambda i: (jnp.maximum(i - nct, 0), 0)),
                              tab, tab, _full((D_MODEL, IN_COLS)), xrow,
                              pl.BlockSpec((tm, D_MODEL), lambda i: (jnp.minimum(i, nct - 1), 0)),
                              _full((1, D_MODEL)), _full((4, D_MODEL)), xrow],
        out_specs=[pl.BlockSpec((tm, IN_COLS), lambda i: (i, 0)), xrow, _full((8, D_MODEL))],
        out_shape=[jax.ShapeDtypeStruct((la, IN_COLS), BF16), jax.ShapeDtypeStruct((l, D_MODEL), F32),
                   jax.ShapeDtypeStruct((8, D_MODEL), F32)],
        compiler_params=_params(("arbitrary",)),
    )(dqf, dkf, dvf, dqb, dkb, dvb, du, dg, cos_t, sin_t, w_in_b, x, ctx, n1w, mod4, dx1)


def _outproj_up(x, y_all, of, ob, p_all, w_glu_b, b_glu, w_out_b, mod3, n2w, w_up_b, nct):
    l = x.shape[0]
    tm = TOK_TILE

    def body(x_ref, y_ref, of_ref, ob_ref, g_ref, wg_ref, bg_ref, wo_ref, mod_ref, nw_ref, wu_ref,
             x1_ref, mix_ref, h2_ref, up_ref, mb_ref, yr_ref):
        yg = _gelu(y_ref[...])
        mb_ref[:, 0:S5_WIDTH] = (yg * _sigmoid(_dot(yg.astype(BF16), wg_ref[...]) + bg_ref[...])).astype(BF16)
        yr = of_ref[...] + ob_ref[...]
        yr_ref[...] = yr
        for h in range(RET_HEADS):
            sl = slice(RET_DH * h, RET_DH * (h + 1))
            mb_ref[:, S5_WIDTH + RET_DH * h:S5_WIDTH + RET_DH * (h + 1)] = (
                _head_norm_gate(yr[:, sl], g_ref[:, sl]).astype(BF16))
        mix = _dot(mb_ref[...], wo_ref[...])
        mix_ref[...] = mix
        x1 = x_ref[...] + mod_ref[0:1, :] * mix
        x1_ref[...] = x1
        h2 = _rms_mod(x1, nw_ref[...], mod_ref[1:2, :], mod_ref[2:3, :]).astype(BF16)
        h2_ref[...] = h2
        up_ref[...] = _dot(h2, wu_ref[...])

    row = lambda w: pl.BlockSpec((tm, w), lambda i: (i, 0))
    arow = pl.BlockSpec((tm, RET_WIDTH), lambda i: (i + nct, 0))
    return pl.pallas_call(
        body, name="outproj_up", grid=(l // tm,),
        in_specs=[row(D_MODEL), arow, arow, arow, pl.BlockSpec((tm, RET_WIDTH), lambda i: (i + nct, G_COL // 4)),
                  _full((S5_WIDTH, S5_WIDTH)), _full((1, S5_WIDTH)), _full((D_MODEL, D_MODEL)), _full((3, D_MODEL)),
                  _full((1, D_MODEL)), _full((D_MODEL, 2 * D_FF))],
        out_specs=[row(D_MODEL), row(D_MODEL), row(D_MODEL), row(2 * D_FF), row(D_MODEL), row(RET_WIDTH)],
        out_shape=[jax.ShapeDtypeStruct((l, D_MODEL), F32), jax.ShapeDtypeStruct((l, D_MODEL), F32),
                   jax.ShapeDtypeStruct((l, D_MODEL), BF16), jax.ShapeDtypeStruct((l, 2 * D_FF), F32),
                   jax.ShapeDtypeStruct((l, D_MODEL), BF16), jax.ShapeDtypeStruct((l, RET_WIDTH), F32)],
        compiler_params=_params(("parallel",)),
    )(x, y_all, of, ob, p_all, w_glu_b, b_glu, w_out_b, mod3, n2w, w_up_b)


HALO = 8


def _conv_taps(g, prev_row, next_row):
    t = g.shape[0]
    r = _iota2(g.shape, 0)
    gprev = jnp.where(r == 0, prev_row, pltpu.roll(g, 1, 0))
    gnext = jnp.where(r == t - 1, next_row, pltpu.roll(g, t - 1, 0))
    return gprev, gnext


def _ffn_loss(up, x1, conv_w, conv_b, w_down_b, gate, fnw, tgt):
    l = x1.shape[0]
    tm = TOK_TILE
    nt = l // tm
    hb = tm // HALO

    cw = 256

    def body(up_a, up_g, hp_ref, hn_ref, x1_ref, cw_ref, cb_ref, wd_ref, gate_ref, fn_ref, tgt_ref,
             act_ref, dx2_ref, ddn_ref, dact_ref, acc_ref, ddn_scr):
        step = pl.program_id(0)
        i = jnp.minimum(step, nt - 1)

        @pl.when(step == 0)
        def _():
            acc_ref[...] = jnp.zeros_like(acc_ref)
            ddn_scr[...] = jnp.zeros_like(ddn_scr)

        ddn_prev = ddn_scr[...]
        dn = jnp.zeros((tm, D_MODEL), F32)
        for c in range(D_FF // cw):
            cols = slice(cw * c, cw * (c + 1))
            g = up_g[:, cols]
            prev_row = jnp.where(i == 0, 0.0, hp_ref[HALO - 1:HALO, cols])
            next_row = jnp.where(i == nt - 1, 0.0, hn_ref[0:1, cols])
            gprev, gnext = _conv_taps(g, prev_row, next_row)
            gc = cb_ref[:, cols] + gprev * cw_ref[0:1, cols] + g * cw_ref[1:2, cols] + gnext * cw_ref[2:3, cols]
            act = (_gelu(gc) * up_a[:, cols]).astype(BF16)
            act_ref[:, cols] = act
            dn = dn + _dot(act, wd_ref[cols, :])
            dact_ref[:, cols] = _dot_nt(ddn_prev, wd_ref[cols, :])
        x2 = x1_ref[...] + gate_ref[...] * dn
        y, vjp = jax.vjp(_rms, x2, fn_ref[...])
        err = y - tgt_ref[...]
        dx2, dfn = vjp(err * (1.0 / D_MODEL))
        dx2_ref[...] = dx2
        ddn = (dx2 * gate_ref[...]).astype(BF16)
        ddn_ref[...] = ddn
        ddn_scr[...] = ddn
        live = step < nt
        acc_ref[0:1, :] += jnp.where(live, dfn, 0.0)
        acc_ref[1:2, :] += jnp.where(live, jnp.sum(dx2 * dn, axis=0, keepdims=True), 0.0)
        acc_ref[2:3, :] += jnp.where(live, (0.5 / D_MODEL) * jnp.sum(err * err), 0.0)

    tile = lambda s: jnp.minimum(s, nt - 1)
    row = lambda w, cb=0: pl.BlockSpec((tm, w), lambda s: (tile(s), cb))
    last = l // HALO - 1
    return pl.pallas_call(
        body, name="ffn_loss", grid=(nt + 1,),
        in_specs=[row(D_FF, 0), row(D_FF, 1),
                  pl.BlockSpec((HALO, D_FF), lambda s: (jnp.maximum(tile(s) * hb - 1, 0), 1)),
                  pl.BlockSpec((HALO, D_FF), lambda s: (jnp.minimum((tile(s) + 1) * hb, last), 1)),
                  row(D_MODEL), _full((3, D_FF)), _full((1, D_FF)), _full((D_FF, D_MODEL)),
                  _full((1, D_MODEL)), _full((1, D_MODEL)), row(D_MODEL)],
        out_specs=[row(D_FF), row(D_MODEL), row(D_MODEL),
                   pl.BlockSpec((tm, D_FF), lambda s: (jnp.maximum(s - 1, 0), 0)), _full((8, D_MODEL))],
        out_shape=[jax.ShapeDtypeStruct((l, D_FF), BF16), jax.ShapeDtypeStruct((l, D_MODEL), F32),
                   jax.ShapeDtypeStruct((l, D_MODEL), BF16), jax.ShapeDtypeStruct((l, D_FF), F32),
                   jax.ShapeDtypeStruct((8, D_MODEL), F32)],
        scratch_shapes=[pltpu.VMEM((tm, D_MODEL), BF16)],
        compiler_params=_params(("arbitrary",)),
    )(up, up, up, up, x1, conv_w, conv_b, w_down_b, gate, fnw, tgt)


def _convglu_bwd(up, dact, conv_w, conv_b):
    l = up.shape[0]
    tm = 128
    nt = l // tm
    hb = tm // HALO
    te = tm + 2 * HALO

    def body(a_ref, ap_ref, an_ref, g_ref, gp_ref, gn_ref, d_ref, dp_ref, dn_ref, cw_ref, cb_ref,
             dup_ref, acc_ref):
        i = pl.program_id(0)

        @pl.when(i == 0)
        def _():
            acc_ref[...] = jnp.zeros_like(acc_ref)

        def ext(p, c, n):
            return jnp.concatenate([jnp.where(i == 0, 0.0, p[...]), c[...], jnp.where(i == nt - 1, 0.0, n[...])], axis=0)

        ae, ge, de = ext(ap_ref, a_ref, an_ref), ext(gp_ref, g_ref, gn_ref), ext(dp_ref, d_ref, dn_ref)
        gprev = pltpu.roll(ge, 1, 0)
        gnext = pltpu.roll(ge, te - 1, 0)
        w0, w1, w2 = cw_ref[0:1, :], cw_ref[1:2, :], cw_ref[2:3, :]
        gce = cb_ref[...] + gprev * w0 + ge * w1 + gnext * w2
        gel, dgel = _gelu_and_grad(gce)
        dae = de * gel
        dgce = de * ae * dgel
        dge = dgce * w1 + pltpu.roll(dgce, te - 1, 0) * w0 + pltpu.roll(dgce, 1, 0) * w2
        mid = slice(HALO, HALO + tm)
        dup_ref[:, 0:D_FF] = dae[mid].astype(BF16)
        dup_ref[:, D_FF:2 * D_FF] = dge[mid].astype(BF16)
        dgc = dgce[mid]
        acc_ref[0:1, :] += jnp.sum(dgc * gprev[mid], axis=0, keepdims=True)
        acc_ref[1:2, :] += jnp.sum(dgc * ge[mid], axis=0, keepdims=True)
        acc_ref[2:3, :] += jnp.sum(dgc * gnext[mid], axis=0, keepdims=True)
        acc_ref[3:4, :] += jnp.sum(dgc, axis=0, keepdims=True)

    last = l // HALO - 1

    def trio(cb):
        return [pl.BlockSpec((tm, D_FF), lambda i: (i, cb)),
                pl.BlockSpec((HALO, D_FF), lambda i: (jnp.maximum(i * hb - 1, 0), cb)),
                pl.BlockSpec((HALO, D_FF), lambda i: (jnp.minimum((i + 1) * hb, last), cb))]

    return pl.pallas_call(
        body, name="convglu_bwd", grid=(nt,),
        in_specs=trio(0) + trio(1) + trio(0) + [_full((3, D_FF)), _full((1, D_FF))],
        out_specs=[pl.BlockSpec((tm, 2 * D_FF), lambda i: (i, 0)), _full((8, D_FF))],
        out_shape=[jax.ShapeDtypeStruct((l, 2 * D_FF), BF16), jax.ShapeDtypeStruct((8, D_FF), F32)],
        compiler_params=_params(("arbitrary",)),
    )(up, up, up, up, up, up, dact, dact, dact, conv_w, conv_b)


def _up_bwd(dup, w_up_b, w_out_b, x1, dx2, mix, mod3, n2w, y_all, y_ret, p_all, w_glu_b, b_glu, zero_rows, nct, pairs,
            kinds):
    l = x1.shape[0]
    tm = TOK_TILE
    nt = l // tm
    n = len(pairs)
    shapes = _rs_slot_shapes(pairs, kinds)
    n_out = 8

    def body(dup_ref, wu_ref, wo_ref, x1_ref, dx2_ref, mix_ref, mod_ref, nw_ref, y_ref, yr_ref, g_ref, wg_ref, bg_ref,
             zero_rows_ref, *rest):
        dx1_ref, dmixb_ref, acc_ref, dys_ref, dyr_ref, dg_ref, gw_ref, gb_ref = rest[n:n + n_out]
        send_sems, recv_sems, dy_scr = rest[2 * n + n_out:]
        step = pl.program_id(0)

        @pl.when(step == 0)
        def _():
            acc_ref[...] = jnp.zeros_like(acc_ref)
            gw_ref[...] = jnp.zeros_like(gw_ref)
            gb_ref[...] = jnp.zeros_like(gb_ref)

        _behind(step, nt - 1, functools.partial(_rs_chip_copies, rest[:n], rest[n + n_out:2 * n + n_out], kinds,
                                                shapes, send_sems, recv_sems))

        dh2 = _dot_nt(dup_ref[...], wu_ref[...])
        _, vjp = jax.vjp(_rms_mod, x1_ref[...], nw_ref[...], mod_ref[1:2, :], mod_ref[2:3, :])
        dx, dnw, dsh, dsc = vjp(dh2)
        dx1 = dx + dx2_ref[...]
        dx1_ref[...] = dx1
        dmixb = (dx1 * mod_ref[0:1, :]).astype(BF16)
        dmixb_ref[...] = dmixb
        dmix = _dot_nt(dmixb, wo_ref[...])
        acc_ref[0:1, :] += dnw
        acc_ref[1:2, :] += jnp.sum(dx1 * mix_ref[...], axis=0, keepdims=True)
        acc_ref[2:3, :] += dsh
        acc_ref[3:4, :] += dsc

        yg, dgel = _gelu_and_grad(y_ref[...])
        ygb = yg.astype(BF16)
        sg = _sigmoid(_dot(ygb, wg_ref[...]) + bg_ref[...])
        ds = dmix[:, 0:S5_WIDTH]
        dz = ds * yg * sg * (1.0 - sg)
        dzb = dz.astype(BF16)
        _s5_put_rows(dys_ref, dy_scr, (ds * sg + _dot_nt(dzb, wg_ref[...])) * dgel)
        gw_ref[...] += _dot_tn(ygb, dzb)
        gb_ref[...] += jnp.sum(dz, axis=0, keepdims=True)

        for h in range(RET_HEADS):
            sl = slice(RET_DH * h, RET_DH * (h + 1))
            _, hvjp = jax.vjp(_head_norm_gate, yr_ref[:, sl], g_ref[:, sl])
            dyr, dg = hvjp(dmix[:, S5_WIDTH + RET_DH * h:S5_WIDTH + RET_DH * (h + 1)])
            dyr_ref[:, sl] = dyr
            dg_ref[:, sl] = dg

    row = pl.BlockSpec((tm, D_MODEL), lambda i: (i, 0))
    half = pl.BlockSpec((tm, S5_WIDTH), lambda i: (i, 0))
    f32h = jax.ShapeDtypeStruct((l, RET_WIDTH), F32)
    return pl.pallas_call(
        body, name="up_bwd", grid=(nt,),
        in_specs=[pl.BlockSpec((tm, 2 * D_FF), lambda i: (i, 0)), _full((D_MODEL, 2 * D_FF)),
                  _full((D_MODEL, D_MODEL)), row, row, row, _full((3, D_MODEL)), _full((1, D_MODEL)),
                  pl.BlockSpec((tm, S5_WIDTH), lambda i: (i + nct, 0)), half,
                  pl.BlockSpec((tm, RET_WIDTH), lambda i: (i + nct, G_COL // 4)),
                  _full((S5_WIDTH, S5_WIDTH)), _full((1, S5_WIDTH)), ANY] + [ANY] * n,
        out_specs=[row, row, _full((8, D_MODEL)),
                   pl.BlockSpec((S5_NB, tm // S5_T, S5_BW), lambda i: (0, i + nct, 0)), half, half,
                   _full((S5_WIDTH, S5_WIDTH)),
                   _full((1, S5_WIDTH))] + [ANY] * n,
        out_shape=[jax.ShapeDtypeStruct((l, D_MODEL), F32), jax.ShapeDtypeStruct((l, D_MODEL), BF16),
                   jax.ShapeDtypeStruct((8, D_MODEL), F32), jax.ShapeDtypeStruct(zero_rows.shape, BF16), f32h, f32h,
                   jax.ShapeDtypeStruct((S5_WIDTH, S5_WIDTH), F32), jax.ShapeDtypeStruct((1, S5_WIDTH), F32)]
        + [jax.ShapeDtypeStruct((4,) + s, p.dtype) for s, p in zip(shapes, pairs)],
        input_output_aliases={13: 3},
        scratch_shapes=[pltpu.SemaphoreType.DMA((n, 3)), pltpu.SemaphoreType.DMA((n, 3)),
                        pltpu.VMEM((tm // S5_T, S5_T, S5_WIDTH), F32)],
        compiler_params=_params(("arbitrary",)),
    )(dup, w_up_b, w_out_b, x1, dx2, mix, mod3, n2w, y_all, y_ret, p_all, w_glu_b, b_glu, zero_rows, *pairs)


MOD_ROWS = 16
MOD_COLS = 6 * D_MODEL // 4


def _mod_fwd(c_all, c_ctx, w_mod_b, b_loc):
    def body(c_ref, cc_ref, w_ref, b_ref, m_ref, s_ref):
        cond = jnp.concatenate([c_ref[...], jnp.broadcast_to(cc_ref[...], (8, D_MODEL))], axis=0)
        s = _silu(cond).astype(BF16)
        s_ref[...] = s
        m_ref[...] = _dot(s, w_ref[...]) + b_ref[...]

    return pl.pallas_call(
        body, name="mod_fwd",
        out_shape=[jax.ShapeDtypeStruct((MOD_ROWS, MOD_COLS), F32), jax.ShapeDtypeStruct((MOD_ROWS, D_MODEL), BF16)],
        compiler_params=_params(),
    )(c_all, c_ctx, w_mod_b, b_loc)


def _mod_bwd_sum(dm_all):
    def body(d_ref, dm_ref, gb_ref):
        rows = [d_ref[k, 0:1, :] for k in range(8)]
        ctx_sum = d_ref[0, 1:2, :]
        for k in range(1, 8):
            ctx_sum = ctx_sum + d_ref[k, 1:2, :]
        gb = ctx_sum
        for k in range(8):
            gb = gb + rows[k]
        gb_ref[...] = gb
        dm_ref[...] = jnp.concatenate(rows + [ctx_sum] + [jnp.zeros((7, 6 * D_MODEL), F32)], axis=0)

    return pl.pallas_call(
        body, name="mod_bwd_sum",
        out_shape=[jax.ShapeDtypeStruct((MOD_ROWS, 6 * D_MODEL), F32), jax.ShapeDtypeStruct((1, 6 * D_MODEL), F32)],
        compiler_params=_params(),
    )(dm_all)


def _mod_bwd_w(dm_loc, s_b, c_ctx, w_mod_b):
    def body(d_ref, s_ref, cc_ref, w_ref, gw_ref, gc_ref):
        db = d_ref[...].astype(BF16)
        gw_ref[...] = _dot_tn(s_ref[...], db)
        ds = _dot_nt(db, w_ref[...])
        _, vjp = jax.vjp(_silu, cc_ref[...])
        gc_ref[...] = jnp.broadcast_to(vjp(ds[8:9, :])[0], (8, D_MODEL))

    return pl.pallas_call(
        body, name="mod_bwd_w",
        out_shape=[jax.ShapeDtypeStruct((D_MODEL, MOD_COLS), F32), jax.ShapeDtypeStruct((8, D_MODEL), F32)],
        compiler_params=_params(),
    )(dm_loc, s_b, c_ctx, w_mod_b)


def _adamw(w, g, m, v, name):
    r, c = w.shape
    tr = _pick(r, (256, 128, 64, 32, 16, 8))
    bc1 = 1.0 - ADAM_B1 ** ADAM_STEP
    bc2 = 1.0 - ADAM_B2 ** ADAM_STEP

    def body(w_ref, g_ref, m_ref, v_ref, d_ref, nm_ref, nv_ref):
        gg = g_ref[...]
        nm = ADAM_B1 * m_ref[...] + (1.0 - ADAM_B1) * gg
        nv = ADAM_B2 * v_ref[...] + (1.0 - ADAM_B2) * (gg * gg)
        nm_ref[...] = nm
        nv_ref[...] = nv
        d_ref[...] = -ADAM_LR * ((nm / bc1) / (jnp.sqrt(nv / bc2) + ADAM_EPS) + ADAM_WD * w_ref[...])

    blk = pl.BlockSpec((tr, c), lambda i: (i, 0))
    shp = jax.ShapeDtypeStruct((r, c), F32)
    return pl.pallas_call(
        body, name=name, grid=(r // tr,), in_specs=[blk] * 4, out_specs=[blk] * 3, out_shape=[shp] * 3,
        compiler_params=_params(("parallel",)),
    )(w, g, m, v)


def _sum_slots(a, name):
    n, r, c = a.shape
    tr = _pick(r, (376, 256, 208, 128, 64, 32, 16, 8))

    def body(a_ref, o_ref):
        acc = a_ref[0].astype(F32)
        for k in range(1, n):
            acc = acc + a_ref[k].astype(F32)
        o_ref[...] = acc

    return pl.pallas_call(
        body, name=name, grid=(r // tr,),
        in_specs=[pl.BlockSpec((n, tr, c), lambda i: (0, i, 0))],
        out_specs=pl.BlockSpec((tr, c), lambda i: (i, 0)),
        out_shape=jax.ShapeDtypeStruct((r, c), F32),
        compiler_params=_params(("parallel",)),
    )(a)


def _mesh_pos():
    return lax.axis_index("x"), lax.axis_index("y"), lax.axis_index("c")


def _all_gather8(v, name):
    m_per, n = v.shape

    def body(x_ref, out_ref, send_sems, recv_sems, local_sem):
        x, y, c = _mesh_pos()
        me, sibling = (x, y, c), (x, y, 1 - c)
        chips = [(1 - x, y), (x, 1 - y), (1 - x, 1 - y)]

        def rows(px, py, pc):
            return out_ref.at[pl.ds((4 * px + 2 * py + pc) * m_per, m_per), :]

        def copy(k, block, to, src=None):
            return pltpu.make_async_remote_copy(
                src_ref=rows(*block) if src is None else src, dst_ref=rows(*block),
                send_sem=send_sems.at[k], recv_sem=recv_sems.at[k], device_id=to, device_id_type=MESH_ID)

        mine = pltpu.make_async_copy(x_ref, rows(*me), local_sem)
        mine.start()
        first = [copy(0, me, sibling, src=x_ref)]
        first += [copy(1 + j, me, (*chip, c), src=x_ref) for j, chip in enumerate(chips)]
        for cp in first:
            cp.start()
        passed = [copy(4 + j, (*chip, c), sibling) for j, chip in enumerate(chips)]
        for j, chip in enumerate(chips):
            copy(1 + j, (*chip, c), me).wait_recv()
            passed[j].start()
        copy(0, sibling, me).wait_recv()
        for j, chip in enumerate(chips):
            copy(4 + j, (*chip, 1 - c), me).wait_recv()
        for cp in first + passed:
            cp.wait_send()
        mine.wait()

    return pl.pallas_call(
        body, name=name,
        out_shape=jax.ShapeDtypeStruct((8 * m_per, n), v.dtype),
        in_specs=[pl.BlockSpec(memory_space=pltpu.VMEM)],
        out_specs=pl.BlockSpec(memory_space=pltpu.VMEM),
        scratch_shapes=[pltpu.SemaphoreType.DMA((7,)), pltpu.SemaphoreType.DMA((7,)), pltpu.SemaphoreType.DMA],
        compiler_params=_params(),
    )(v)


ANY = pl.BlockSpec(memory_space=pl.ANY)
def PEER_CHIPS(x, y):
    return [(x, 1 - y), (1 - x, y), (1 - x, 1 - y)]


def _shard_region(ref, kind, k, rl, cl, r0, nr, c0, nc):
    if kind == "col":
        return ref.at[pl.ds(r0, nr), pl.ds(k * cl + c0, nc)]
    return ref.at[pl.ds(k * rl + r0, nr), pl.ds(c0, nc)]


def _place_shard(w, kind, chip, name):
    rl, cl = w.shape
    tr = _pick(rl, (256, 128, 64))
    nt = rl // tr

    def body(chip_ref, w_ref, o_ref):
        o_ref[...] = w_ref[...].astype(BF16)

    o_map = (lambda i, chip_ref: (i, chip_ref[0])) if kind == "col" else (lambda i, chip_ref: (chip_ref[0] * nt + i, 0))
    return pl.pallas_call(
        body, name=name,
        grid_spec=pltpu.PrefetchScalarGridSpec(
            num_scalar_prefetch=1, grid=(nt,),
            in_specs=[pl.BlockSpec((tr, cl), lambda i, chip_ref: (i, 0))], out_specs=pl.BlockSpec((tr, cl), o_map)),
        out_shape=jax.ShapeDtypeStruct((rl, 4 * cl) if kind == "col" else (4 * rl, cl), BF16),
        compiler_params=_params(("parallel",)),
    )(chip.reshape(1), w)


def _gather_shard_shapes(placed, kinds):
    return [(p.shape[0], p.shape[1] // 4) if k == "col" else (p.shape[0] // 4, p.shape[1]) for p, k in zip(placed, kinds)]


def _gather_chip_copies(outs, kinds, shard_shapes, send_sems, recv_sems, with_arrivals=True):
    x, y, c = _mesh_pos()
    me = 2 * x + y
    sends, arrivals = [], []
    for a in range(len(outs)):
        rl, cl = shard_shapes[a]
        rh = rl // 2
        reg = functools.partial(_shard_region, outs[a], kinds[a], rl=rl, cl=cl, r0=c * rh, nr=rh, c0=0, nc=cl)
        for j, (px, py) in enumerate(PEER_CHIPS(x, y)):
            to = dict(send_sem=send_sems.at[a, j], recv_sem=recv_sems.at[a, j], device_id=(px, py, c),
                      device_id_type=MESH_ID)
            sends.append(pltpu.make_async_remote_copy(src_ref=reg(k=me), dst_ref=reg(k=me), **to))
            if with_arrivals:
                got = reg(k=2 * px + py)
                arrivals.append(pltpu.make_async_remote_copy(src_ref=got, dst_ref=got, **to))
    return sends, arrivals


def _gather_sibling_copies(outs, kinds, shard_shapes, send_sems, recv_sems):
    x, y, c = _mesh_pos()
    forwards, arrivals = [], []
    for a in range(len(outs)):
        rl, cl = shard_shapes[a]
        rh = rl // 2
        for j, (px, py) in enumerate(PEER_CHIPS(x, y)):
            to = dict(send_sem=send_sems.at[a, j], recv_sem=recv_sems.at[a, j], device_id=(x, y, 1 - c),
                      device_id_type=MESH_ID)
            reg = functools.partial(_shard_region, outs[a], kinds[a], k=2 * px + py, rl=rl, cl=cl, nr=rh, c0=0, nc=cl)
            forwards.append(pltpu.make_async_remote_copy(src_ref=reg(r0=c * rh), dst_ref=reg(r0=c * rh), **to))
            arrivals.append(pltpu.make_async_remote_copy(src_ref=reg(r0=(1 - c) * rh), dst_ref=reg(r0=(1 - c) * rh), **to))
    return forwards, arrivals


def _gather_sibling(placed, kinds, name):
    n = len(placed)
    shard_shapes = _gather_shard_shapes(placed, kinds)

    def body(*refs):
        forwards, from_sibling = _gather_sibling_copies(refs[n:2 * n], kinds, shard_shapes, *refs[2 * n:])
        for cp in forwards:
            cp.start()
        for cp in from_sibling:
            cp.wait_recv()
        for cp in forwards:
            cp.wait_send()

    return pl.pallas_call(
        body, name=name,
        out_shape=[jax.ShapeDtypeStruct(p.shape, p.dtype) for p in placed],
        in_specs=[ANY] * n, out_specs=[ANY] * n, input_output_aliases={a: a for a in range(n)},
        scratch_shapes=[pltpu.SemaphoreType.DMA((n, 3))] * 2,
        compiler_params=_params(),
    )(*placed)


def _half(kind, r, c):
    return (r // 2, c) if kind == "col" else (r, c // 2)


def _half_of(ref, kind, which):
    r, c = ref.shape
    hr, hc = _half(kind, r, c)
    return ref.at[pl.ds(which * hr, hr), :] if kind == "col" else ref.at[:, pl.ds(which * hc, hc)]


def _rs_sibling(grads, kinds, name):
    n = len(grads)

    def body(*refs):
        srcs, dsts = refs[:n], refs[n:2 * n]
        send_sems, recv_sems = refs[2 * n:]
        x, y, c = _mesh_pos()
        cps = [pltpu.make_async_remote_copy(src_ref=_half_of(srcs[a], kinds[a], 1 - c), dst_ref=dsts[a],
                                            send_sem=send_sems.at[a], recv_sem=recv_sems.at[a],
                                            device_id=(x, y, 1 - c), device_id_type=MESH_ID) for a in range(n)]
        for cp in cps:
            cp.start()
        for cp in cps:
            cp.wait()

    return pl.pallas_call(
        body, name=name,
        out_shape=[jax.ShapeDtypeStruct(_half(k, *g.shape), g.dtype) for g, k in zip(grads, kinds)],
        in_specs=[ANY] * n, out_specs=[ANY] * n,
        scratch_shapes=[pltpu.SemaphoreType.DMA((n,)), pltpu.SemaphoreType.DMA((n,))],
        compiler_params=_params(),
    )(*grads)


def _pair_sum(gf, rv, kind, ci, name):
    r, c = rv.shape
    tr = _pick(r, (128, 64, 32, 16, 8))
    nt = r // tr

    def body(ci_ref, g_ref, r_ref, o_ref):
        o_ref[...] = (g_ref[...] + r_ref[...]).astype(BF16)

    g_map = (lambda i, ci_ref: (ci_ref[0] * nt + i, 0)) if kind == "col" else (lambda i, ci_ref: (i, ci_ref[0]))
    blk = pl.BlockSpec((tr, c), lambda i, ci_ref: (i, 0))
    return pl.pallas_call(
        body, name=name,
        grid_spec=pltpu.PrefetchScalarGridSpec(num_scalar_prefetch=1, grid=(nt,),
                                               in_specs=[pl.BlockSpec((tr, c), g_map), blk], out_specs=blk),
        out_shape=jax.ShapeDtypeStruct((r, c), BF16),
        compiler_params=_params(("parallel",)),
    )(ci.reshape(1), gf, rv)


def _rs_slot_shapes(pairs, kinds):
    return [(p.shape[0], p.shape[1] // 4) if k == "col" else (p.shape[0] // 4, p.shape[1]) for p, k in zip(pairs, kinds)]


def _rs_chip_copies(srcs, dsts, kinds, shapes, send_sems, recv_sems, with_arrivals=True):
    x, y, c = _mesh_pos()
    me = 2 * x + y
    sends, arrivals = [], []
    for a in range(len(srcs)):
        rl, cl = shapes[a]
        reg = functools.partial(_shard_region, srcs[a], kinds[a], rl=rl, cl=cl, r0=0, nr=rl, c0=0, nc=cl)
        for j, (px, py) in enumerate(PEER_CHIPS(x, y)):
            to = dict(send_sem=send_sems.at[a, j], recv_sem=recv_sems.at[a, j], device_id=(px, py, c),
                      device_id_type=MESH_ID)
            sends.append(pltpu.make_async_remote_copy(src_ref=reg(k=2 * px + py), dst_ref=dsts[a].at[me], **to))
            if with_arrivals:
                slot = dsts[a].at[2 * px + py]
                arrivals.append(pltpu.make_async_remote_copy(src_ref=slot, dst_ref=slot, **to))
    return sends, arrivals


def _rs_chips(pairs, kinds):
    n = len(pairs)
    shapes = _rs_slot_shapes(pairs, kinds)

    def body(*refs):
        sends, arrivals = _rs_chip_copies(refs[:n], refs[n:2 * n], kinds, shapes, *refs[2 * n:])
        for cp in sends:
            cp.start()
        for cp in arrivals:
            cp.wait_recv()
        for cp in sends:
            cp.wait_send()

    return pl.pallas_call(
        body, name="rs_chips",
        out_shape=[jax.ShapeDtypeStruct((4,) + s, p.dtype) for s, p in zip(shapes, pairs)],
        in_specs=[ANY] * n, out_specs=[ANY] * n,
        scratch_shapes=[pltpu.SemaphoreType.DMA((n, 3)), pltpu.SemaphoreType.DMA((n, 3))],
        compiler_params=_params(),
    )(*pairs)


def _sum_chips(pair, got, kind, pos, name):
    _, r, c = got.shape
    tr = _pick(r, (256, 128, 64, 32, 16))
    nt = r // tr

    def body(pos_ref, own_ref, g1_ref, g2_ref, g3_ref, o_ref):
        o_ref[...] = ((own_ref[...].astype(F32) + g1_ref[0].astype(F32)) + g2_ref[0].astype(F32)) + g3_ref[0].astype(F32)

    if kind == "col":
        own_map = lambda i, p: (i, p[1])
        out_map = lambda i, p: (p[0] * nt + i, 0)
        out_shape = (2 * r, c)
    else:
        own_map = lambda i, p: (p[1] * nt + i, 0)
        out_map = lambda i, p: (i, p[0])
        out_shape = (r, 2 * c)
    peer = lambda m: pl.BlockSpec((1, tr, c), lambda i, p: (p[1] ^ m, i, 0))
    return pl.pallas_call(
        body, name=name,
        grid_spec=pltpu.PrefetchScalarGridSpec(
            num_scalar_prefetch=1, grid=(nt,),
            in_specs=[pl.BlockSpec((tr, c), own_map), peer(1), peer(2), peer(3)],
            out_specs=pl.BlockSpec((tr, c), out_map)),
        out_shape=jax.ShapeDtypeStruct(out_shape, F32),
        compiler_params=_params(("parallel",)),
    )(pos, pair, got, got, got)


def _rs_back(halves, kinds):
    n = len(halves)

    def body(*refs):
        outs = refs[n:2 * n]
        send_sems, recv_sems = refs[2 * n:]
        x, y, c = _mesh_pos()
        cps = []
        for a in range(n):
            mine = _half_of(outs[a], kinds[a], c)
            cps.append(pltpu.make_async_remote_copy(src_ref=mine, dst_ref=mine, send_sem=send_sems.at[a],
                                                    recv_sem=recv_sems.at[a], device_id=(x, y, 1 - c),
                                                    device_id_type=MESH_ID))
            cps[-1].start()
        for a in range(n):
            other = _half_of(outs[a], kinds[a], 1 - c)
            pltpu.make_async_remote_copy(src_ref=other, dst_ref=other, send_sem=send_sems.at[a],
                                         recv_sem=recv_sems.at[a], device_id=(x, y, 1 - c),
                                         device_id_type=MESH_ID).wait_recv()
        for cp in cps:
            cp.wait_send()

    return pl.pallas_call(
        body, name="rs_back",
        out_shape=[jax.ShapeDtypeStruct(h.shape, h.dtype) for h in halves],
        in_specs=[ANY] * n, out_specs=[ANY] * n, input_output_aliases={a: a for a in range(n)},
        scratch_shapes=[pltpu.SemaphoreType.DMA((n,)), pltpu.SemaphoreType.DMA((n,))],
        compiler_params=_params(),
    )(*halves)


def _rope_tables(l, lc):
    rows = l // GRID_W
    n_freq = RET_DH // 4
    inv_freq = ROPE_THETA ** (-jnp.arange(n_freq, dtype=F32) / n_freq)
    sign = jnp.tile(jnp.array([-1.0, 1.0], F32), n_freq)

    def half(n):
        ang = jnp.repeat(jnp.arange(n, dtype=F32)[:, None] * inv_freq, 2, axis=-1)
        return jnp.cos(ang), jnp.sin(ang) * sign

    (cr, sr), (cc, sc) = half(rows), half(GRID_W)
    grid = lambda r, c: jnp.concatenate([jnp.repeat(r, GRID_W, axis=0), jnp.tile(c, (rows, 1))], axis=-1)
    cos_t = jnp.concatenate([jnp.ones((lc, RET_DH), F32), grid(cr, cc)], axis=0)
    sin_t = jnp.concatenate([jnp.zeros((lc, RET_DH), F32), grid(sr, sc)], axis=0)
    return cos_t, sin_t


def _s5_pack(a):
    blk = lambda t: t.reshape(1, S5_NB, 128, S5_STATE)
    lre = jnp.stack([a["s5_lambda_re_f"][0], a["s5_lambda_re_b"][0]]).reshape(2, S5_NB, 8, S5_STATE)
    lim = jnp.stack([a["s5_lambda_im_f"][0], a["s5_lambda_im_b"][0]]).reshape(2, S5_NB, 8, S5_STATE)
    lst = jnp.stack([a["s5_log_step_f"][0], a["s5_log_step_b"][0]]).reshape(2, S5_NB, 8, 1)
    b_re = blk(a["s5_b_re"][0].transpose(0, 2, 1))
    b_im = blk(a["s5_b_im"][0].transpose(0, 2, 1))
    return (lre, lim, lst, b_re, b_im, blk(a["s5_c_re"][0]), blk(a["s5_c_im"][0]),
            a["s5_d"].reshape(1, S5_NB, 1, 128))


def _s5_unpack(g):
    glre, glim, glst, gbre, gbim, gcre, gcim, gd = g
    unb = lambda t: t.reshape(S5_GROUPS, S5_GROUP, S5_STATE).transpose(0, 2, 1)[None]
    return {
        "s5_lambda_re_f": glre[0].reshape(1, S5_GROUPS, S5_STATE), "s5_lambda_re_b": glre[1].reshape(1, S5_GROUPS, S5_STATE),
        "s5_lambda_im_f": glim[0].reshape(1, S5_GROUPS, S5_STATE), "s5_lambda_im_b": glim[1].reshape(1, S5_GROUPS, S5_STATE),
        "s5_log_step_f": glst[0].reshape(1, S5_GROUPS), "s5_log_step_b": glst[1].reshape(1, S5_GROUPS),
        "s5_b_re": unb(gbre), "s5_b_im": unb(gbim),
        "s5_c_re": gcre.reshape(1, S5_GROUPS, S5_GROUP, S5_STATE), "s5_c_im": gcim.reshape(1, S5_GROUPS, S5_GROUP, S5_STATE),
        "s5_d": gd.reshape(1, S5_WIDTH),
    }


def _local_step(a, early, late, mx, mc, conv_w, ci):
    x, ctx, tgt = a["x"][0], a["ctx"][0], a["loss_target"][0]
    l, lc = x.shape[0], ctx.shape[0]
    la = l + lc
    nct, ncc, nrc, cn = lc // TOK_TILE, lc // S5_T, lc // RET_CHUNK, la // S5_T
    n1w, n2w, fnw = a["norm1_w"], a["norm2_w"], a["final_norm_w"].reshape(1, D_MODEL)
    conv_b, b_glu = a["conv_b"], a["s5_b_glu"]
    ld2 = jnp.concatenate([a["ret_log_decay_f"], a["ret_log_decay_b"]], axis=0)
    mod4 = jnp.concatenate([mc[0:2], mx[0:2]], axis=0)
    mod3 = mx[2:5]
    gate5 = mx[5:6]
    cos_t, sin_t = _rope_tables(l, lc)
    s5p = _s5_pack(a)

    gg, xw, yw, a16, *early = _s5_gen(*s5p, early, EARLY_KINDS)
    wb = dict(zip(EARLY_NAMES, _gather_sibling(early, EARLY_KINDS, "gather_sibling_early")))
    p_all, h1b, p3, w_up_p = _norm_inproj(x, ctx, n1w, mod4, wb["w_in"], cos_t, sin_t, [late[1]], (LATE_KINDS[1],))
    sloc = _s5_to_states(p3, xw, "s5_state")
    a16s = a16.transpose(1, 0, 2, 3).reshape(2, S5_GROUPS, 128)
    hs, w_out_p, w_down_p = _s5_scan(sloc, a16s, ncc, [late[0], late[2]], (LATE_KINDS[0], LATE_KINDS[2]))
    y_all = _s5_from_states(p3, gg, hs, yw, False, "s5_out").reshape(la, S5_WIDTH)
    tabs = _ret_tables(ld2)
    of, ob, ssf, ssb = _ret_scan(p_all, tabs, nrc)
    wb = {**wb, **dict(zip(LATE_NAMES, _gather_sibling([w_out_p, w_up_p, w_down_p], LATE_KINDS, "gather_sibling_late")))}
    x1, mix, h2b, up, mixb, y_ret = _outproj_up(x, y_all, of, ob, p_all, wb["s5_w_glu"], b_glu, wb["w_out"],
                                                     mod3, n2w, wb["w_up"], nct)
    act, dx2, ddn, dact, acc_f = _ffn_loss(up, x1, conv_w, conv_b, wb["w_down"], gate5, fnw, tgt)

    g = {}
    g["w_down"] = _mm_tn(act, ddn, name="gw_down")
    dup, acc_c = _convglu_bwd(up, dact, conv_w, conv_b)
    g["w_up"] = _mm_tn(h2b, dup, name="gw_up")
    first = [g[n] for n in FIRST_GRADS]
    first_pairs = [_pair_sum(gf, rv, k, ci, "rs_pair_" + n)
                   for gf, rv, k, n in zip(first, _rs_sibling(first, FIRST_KINDS, "rs_sibling_first"), FIRST_KINDS, FIRST_GRADS)]
    dx1, dmixb, acc_2, dy3, dy_ret, dg, g["s5_w_glu"], g["s5_b_glu"], *first_got = _up_bwd(
        dup, wb["w_up"], wb["w_out"], x1, dx2, mix, mod3, n2w, y_all, y_ret, p_all, wb["s5_w_glu"], b_glu,
        jnp.zeros(p3.shape, BF16), nct, first_pairs, FIRST_KINDS)
    g["w_out"] = _mm_tn(mixb, dmixb, name="gw_out")

    e = _s5_to_states(dy3, yw, "s5_bwd_h")
    ds, da16 = _s5_scan_bwd(e, hs, a16s, ncc)
    du = _s5_from_states(dy3, gg, ds, xw, True, "s5_bwd_u").reshape(la, S5_WIDTH)
    dkb = _s5_bwd_kb(p3, dy3)
    dwst = _s5_bwd_w(p3, ds, "s5_bwd_wst")
    dwout = _s5_bwd_w(dy3, hs, "s5_bwd_wout")
    da16p = da16.reshape(2, S5_NB, 8, 128).transpose(1, 0, 2, 3)
    g.update(_s5_unpack(_s5_gen_bwd(*s5p, dkb, dwst, dwout, da16p)))

    dqf, dkf, dvf, dqb, dkb_, dvb, dld = _ret_scan_bwd(p_all, tabs, ssf, ssb, dy_ret, nrc)
    g["ret_log_decay_f"] = dld[0, :, 0, 0].reshape(1, RET_HEADS)
    g["ret_log_decay_b"] = dld[1, :, 0, 0].reshape(1, RET_HEADS)
    dp, grad_x, acc_1 = _in_bwd(dqf, dkf, dvf, dqb, dkb_, dvb, du, dg, cos_t, sin_t, wb["w_in"], x, ctx, n1w, mod4, dx1)
    g["w_in"] = _mm_tn(h1b, dp, name="gw_in")

    g["norm1_w"], g["norm2_w"], g["final_norm_w"] = acc_1[0:1], acc_2[0:1], acc_f[0]
    g["conv_w"], g["conv_b"] = acc_c[0:3], acc_c[3:4]
    zero = jnp.zeros((1, D_MODEL), F32)
    dmx = jnp.concatenate([acc_1[3:5], acc_2[1:2], acc_2[2:4], acc_f[1:2]], axis=0)
    dmc = jnp.concatenate([acc_1[1:3], zero, zero, zero, zero], axis=0)
    return acc_f[2, 0], grad_x, g, dmx, dmc, first_pairs, first_got


WEIGHT_NAMES = ("c_ctx", "w_mod", "b_mod", "norm1_w", "w_in", "s5_lambda_re_f", "s5_lambda_im_f", "s5_log_step_f",
                "s5_lambda_re_b", "s5_lambda_im_b", "s5_log_step_b", "s5_b_re", "s5_b_im", "s5_c_re", "s5_c_im",
                "s5_d", "s5_w_glu", "s5_b_glu", "ret_log_decay_f", "ret_log_decay_b", "w_out", "norm2_w", "w_up",
                "conv_w", "conv_b", "w_down", "final_norm_w")
BIG_NAMES = ("w_in", "w_out", "w_up", "w_down", "s5_w_glu")
BIG_KINDS = ("col", "row", "col", "row", "row")
EARLY_NAMES, EARLY_KINDS = ("w_in", "s5_w_glu"), ("col", "row")
LATE_NAMES, LATE_KINDS = ("w_out", "w_up", "w_down"), ("row", "col", "row")
FIRST_GRADS, FIRST_KINDS = ("w_down", "w_up"), ("row", "col")
LAST_GRADS, LAST_KINDS = ("w_in", "w_out", "s5_w_glu"), ("col", "row", "row")
SMALL_NAMES = ("norm1_w", "norm2_w", "final_norm_w", "conv_b", "conv_w", "s5_lambda_re_f", "s5_lambda_im_f",
               "s5_log_step_f", "s5_lambda_re_b", "s5_lambda_im_b", "s5_log_step_b", "s5_b_re", "s5_b_im", "s5_c_re",
               "s5_c_im", "s5_d", "s5_b_glu", "ret_log_decay_f", "ret_log_decay_b")
ROW = 1024
N_CHIPS = 4


def _pack_rows(parts):
    flat = jnp.concatenate([p.reshape(-1) for p in parts])
    n = flat.shape[0]
    rows = -(-n // (8 * ROW)) * 8
    return jnp.pad(flat, (0, rows * ROW - n)).reshape(rows, ROW)


def _unpack_rows(packed, shapes):
    flat = packed.reshape(-1)
    out, off = [], 0
    for s in shapes:
        n = math.prod(s)
        out.append(flat[off:off + n].reshape(s))
        off += n
    return out


def _step(a):
    xi, yi, ci = _mesh_pos()
    chip = 2 * xi + yi
    dev = 2 * chip + ci

    cw_loc = a["conv_w"].reshape(-1)
    small_in = jnp.concatenate([a["c"].reshape(-1), jnp.pad(cw_loc, (0, 24 * 128 - cw_loc.shape[0]))]).reshape(32, 128)
    sg = _all_gather8(small_in, "gather_cond").reshape(8, 32, 128)
    c_all = sg[:, 0:8].reshape(8, D_MODEL)
    conv_w = sg[0::2, 8:32].reshape(N_CHIPS, -1)[:, :cw_loc.shape[0]].reshape(N_CHIPS, 3, -1)
    conv_w = conv_w.transpose(1, 0, 2).reshape(3, D_FF)

    placed = {n: _place_shard(a[n][0], k, chip, "place_" + n) for n, k in zip(BIG_NAMES, BIG_KINDS)}
    early = [placed[n] for n in EARLY_NAMES]
    late = [placed[n] for n in LATE_NAMES]

    w_mod_b = a["w_mod"][0].astype(BF16)
    c_ctx = a["c_ctx"].reshape(1, D_MODEL)
    b_loc = lax.dynamic_slice_in_dim(a["b_mod"], chip * MOD_COLS, MOD_COLS, 1)
    m_loc, s_b = _mod_fwd(c_all, c_ctx, w_mod_b, b_loc)
    mg = _all_gather8(m_loc, "gather_mod").reshape(8, MOD_ROWS, MOD_COLS)
    m_full = mg[0::2].transpose(1, 0, 2).reshape(MOD_ROWS, 6 * D_MODEL)
    mx = lax.dynamic_slice_in_dim(m_full, dev, 1, 0).reshape(6, D_MODEL)
    mc = m_full[8].reshape(6, D_MODEL)

    loss_part, grad_x, g, dmx, dmc, first_pairs, first_got = _local_step(a, early, late, mx, mc, conv_w, ci)
    loss = lax.psum(loss_part, ("x", "y", "c"))

    dm_pair = jnp.concatenate([dmx.reshape(1, -1), dmc.reshape(1, -1), jnp.zeros((6, 6 * D_MODEL), F32)], axis=0)
    dm_all = _all_gather8(dm_pair, "gather_dmod").reshape(8, 8, 6 * D_MODEL)
    dm16, gb_mod = _mod_bwd_sum(dm_all)
    dm_loc = lax.dynamic_slice_in_dim(dm16, chip * MOD_COLS, MOD_COLS, 1)
    gw_mod, gcc = _mod_bwd_w(dm_loc, s_b, c_ctx, w_mod_b)

    small_parts = [g[n] for n in SMALL_NAMES] + [gcc[0]]
    small_shapes = [p.shape for p in small_parts]
    sp = _pack_rows(small_parts)
    tot = _sum_slots(_all_gather8(sp, "gather_small_grads").reshape(8, sp.shape[0], ROW), "sum_small_grads")
    small = dict(zip(SMALL_NAMES + ("c_ctx",), _unpack_rows(tot, small_shapes)))
    grads = {n: small[n].reshape(a[n].shape) for n in SMALL_NAMES if n != "conv_w"}
    grads["c_ctx"] = (0.5 * small["c_ctx"]).reshape(a["c_ctx"].shape)
    grads["conv_w"] = lax.dynamic_slice_in_dim(small["conv_w"], chip * (D_FF // N_CHIPS), D_FF // N_CHIPS, 1)[None]
    grads["b_mod"] = gb_mod
    grads["w_mod"] = gw_mod[None]

    last = [g[n] for n in LAST_GRADS]
    last_pairs = [_pair_sum(gf, rv, k, ci, "rs_pair_" + n)
                  for gf, rv, k, n in zip(last, _rs_sibling(last, LAST_KINDS, "rs_sibling_last"), LAST_KINDS, LAST_GRADS)]
    last_got = _rs_chips(last_pairs, LAST_KINDS)
    pos = jnp.stack([ci, chip])
    order = FIRST_GRADS + LAST_GRADS
    order_kinds = FIRST_KINDS + LAST_KINDS
    halves = [_sum_chips(p, t, k, pos, "rs_sum_" + n)
              for p, t, k, n in zip(first_pairs + last_pairs, list(first_got) + list(last_got), order_kinds, order)]
    for n, t in zip(order, _rs_back(halves, order_kinds)):
        grads[n] = t[None]

    delta, new_m, new_v = {}, {}, {}
    for n in BIG_NAMES + ("w_mod",):
        for dst, t in zip((delta, new_m, new_v), _adamw(a[n][0], grads[n][0], a["m_" + n][0], a["v_" + n][0], "adamw_" + n)):
            dst[n] = t[None]
    rest = [n for n in WEIGHT_NAMES if n not in BIG_NAMES and n != "w_mod"]
    shapes = [a[n].shape for n in rest]
    pr = lambda pre: _pack_rows([a[pre + n] for n in rest])
    for dst, t in zip((delta, new_m, new_v),
                      _adamw(pr(""), _pack_rows([grads[n] for n in rest]), pr("m_"), pr("v_"), "adamw_small")):
        dst.update(zip(rest, _unpack_rows(t, shapes)))

    return (loss, grad_x[None], *[grads[n] for n in WEIGHT_NAMES], *[delta[n] for n in WEIGHT_NAMES],
            *[new_m[n] for n in WEIGHT_NAMES], *[new_v[n] for n in WEIGHT_NAMES])


def kernel(x, c, ctx, c_ctx, w_mod, b_mod, norm1_w, w_in, s5_lambda_re_f, s5_lambda_im_f, s5_log_step_f, s5_lambda_re_b, s5_lambda_im_b, s5_log_step_b, s5_b_re, s5_b_im, s5_c_re, s5_c_im, s5_d, s5_w_glu, s5_b_glu, ret_log_decay_f, ret_log_decay_b, w_out, norm2_w, w_up, conv_w, conv_b, w_down, final_norm_w, loss_target, m_c_ctx, m_w_mod, m_b_mod, m_norm1_w, m_w_in, m_s5_lambda_re_f, m_s5_lambda_im_f, m_s5_log_step_f, m_s5_lambda_re_b, m_s5_lambda_im_b, m_s5_log_step_b, m_s5_b_re, m_s5_b_im, m_s5_c_re, m_s5_c_im, m_s5_d, m_s5_w_glu, m_s5_b_glu, m_ret_log_decay_f, m_ret_log_decay_b, m_w_out, m_norm2_w, m_w_up, m_conv_w, m_conv_b, m_w_down, m_final_norm_w, v_c_ctx, v_w_mod, v_b_mod, v_norm1_w, v_w_in, v_s5_lambda_re_f, v_s5_lambda_im_f, v_s5_log_step_f, v_s5_lambda_re_b, v_s5_lambda_im_b, v_s5_log_step_b, v_s5_b_re, v_s5_b_im, v_s5_c_re, v_s5_c_im, v_s5_d, v_s5_w_glu, v_s5_b_glu, v_ret_log_decay_f, v_ret_log_decay_b, v_w_out, v_norm2_w, v_w_up, v_conv_w, v_conv_b, v_w_down, v_final_norm_w):
    return _step(dict(locals()))
```

```python
import functools
import math

import jax
import jax.numpy as jnp
from jax import lax
from jax.experimental import pallas as pl
from jax.experimental.pallas import tpu as pltpu

F32 = jnp.float32
BF16 = jnp.bfloat16

D_MODEL = 1024
S5_WIDTH = 512
S5_GROUPS = 32
S5_GROUP = 16
S5_STATE = 64
RET_WIDTH = 512
RET_HEADS = 4
RET_DH = 128
RET_CHUNK = 256
GRID_W = 64
ROPE_THETA = 10000.0
D_FF = 2816
NORM_EPS = 1e-6
IN_COLS = S5_WIDTH + 4 * RET_WIDTH

S5_T = 16
S5_NB = 4
S5_BW = S5_T * 128
S5_SW = 8 * 2 * S5_STATE

ADAM_LR, ADAM_B1, ADAM_B2, ADAM_EPS, ADAM_WD, ADAM_STEP = 0.001, 0.9, 0.999, 1e-08, 0.01, 10

VMEM_LIMIT = 56 * 1024 * 1024
MM_TN_VMEM = 40 * 1024 * 1024
MESH_ID = pl.DeviceIdType.MESH


def _params(sem=None):
    return pltpu.CompilerParams(dimension_semantics=sem, vmem_limit_bytes=VMEM_LIMIT)


def _full(shape):
    n = len(shape)
    return pl.BlockSpec(shape, lambda *_: (0,) * n)


def _dot(a, b):
    return jnp.dot(a, b, preferred_element_type=F32)


def _dot_nt(a, b):
    return lax.dot_general(a, b, (((1,), (1,)), ((), ())), preferred_element_type=F32)


def _dot_tn(a, b):
    return lax.dot_general(a, b, (((0,), (0,)), ((), ())), preferred_element_type=F32)


def _dot_hi(a, b):
    return jnp.dot(a, b, preferred_element_type=F32, precision=lax.Precision.HIGHEST)


def _dot_nt_hi(a, b):
    return lax.dot_general(a, b, (((1,), (1,)), ((), ())), preferred_element_type=F32,
                           precision=lax.Precision.HIGHEST)


def _gelu(x):
    return 0.5 * x * (1.0 + jnp.tanh(0.7978845608028654 * (x + 0.044715 * (x * x * x))))


def _gelu_and_grad(x):
    c, ca = 0.7978845608028654, 0.7978845608028654 * 0.044715
    x2 = x * x
    t = jnp.tanh(x * (c + ca * x2))
    h = 0.5 * x
    return h + h * t, 0.5 + 0.5 * t + h * (1.0 - t * t) * (c + 3.0 * ca * x2)


def _sigmoid(x):
    return 1.0 / (1.0 + jnp.exp(-x))


def _silu(x):
    return x * _sigmoid(x)


def _rms_mod(x, nw, sh, sc):
    r = lax.rsqrt(jnp.mean(x * x, axis=-1, keepdims=True) + NORM_EPS)
    return (x * r * nw) * (1.0 + sc) + sh


def _rms(x, nw):
    r = lax.rsqrt(jnp.mean(x * x, axis=-1, keepdims=True) + NORM_EPS)
    return x * r * nw


def _head_norm_gate(y, g):
    mu = jnp.mean(y, axis=-1, keepdims=True)
    yc = y - mu
    var = jnp.mean(yc * yc, axis=-1, keepdims=True)
    return _silu(g) * (yc * lax.rsqrt(var + NORM_EPS))


def _swap_pairs(t):
    lane = lax.broadcasted_iota(jnp.int32, t.shape, 1)
    return jnp.where(lane % 2 == 0, pltpu.roll(t, RET_DH - 1, 1), pltpu.roll(t, 1, 1))


def _rope(t, cos_t, sin_t):
    return t * cos_t + _swap_pairs(t) * sin_t


def _rope_t(dt, cos_t, sin_t):
    return dt * cos_t + _swap_pairs(dt * sin_t)


def _pick(n, prefs):
    for p in prefs:
        if n % p == 0:
            return p
    return n


def _mm_tn(a, b, *, name, gathered=None):
    m, k = a.shape
    n = b.shape[1]
    tn = _pick(n, (1408, 1024, 1280, 512))
    fits = lambda t: 2 * (2 * t * k + 2 * t * tn + 4 * k * tn) <= MM_TN_VMEM
    tm = _pick(m, [t for t in (2816, 2048, 1024, 768, 512, 256) if fits(t)] + [128])
    nj, ni = n // tn, m // tm

    def product(a_ref, b_ref, o_ref):
        @pl.when(pl.program_id(1) == 0)
        def _():
            o_ref[...] = jnp.zeros_like(o_ref)
        o_ref[...] += _dot_tn(a_ref[...], b_ref[...])

    specs = dict(
        grid=(nj, ni),
        in_specs=[pl.BlockSpec((tm, k), lambda j, i: (i, 0)), pl.BlockSpec((tm, tn), lambda j, i: (i, j))],
        out_specs=pl.BlockSpec((k, tn), lambda j, i: (0, j)),
        out_shape=jax.ShapeDtypeStruct((k, n), F32))
    if gathered is None:
        def body(a_ref, b_ref, o_ref):
            product(a_ref, b_ref, o_ref)

        return pl.pallas_call(body, name=name, compiler_params=_params(("parallel", "arbitrary")), **specs)(a, b)

    def body_gather(a_ref, b_ref, v_ref, o_ref, all_ref, send_sems, recv_sems, local_sem):
        begin, finish = _gather8_phases(v_ref, all_ref, send_sems, recv_sems, local_sem, gathered.shape[0])
        step = pl.program_id(0) * ni + pl.program_id(1)
        pl.when(step == 0)(begin)
        product(a_ref, b_ref, o_ref)
        pl.when(step == nj * ni - 1)(finish)

    specs["in_specs"] = specs["in_specs"] + [ANY]
    specs["out_specs"] = [specs["out_specs"], ANY]
    specs["out_shape"] = [specs["out_shape"],
                          jax.ShapeDtypeStruct((8 * gathered.shape[0], gathered.shape[1]), gathered.dtype)]
    return pl.pallas_call(body_gather, name=name, scratch_shapes=list(GATHER8_SCRATCH),
                          compiler_params=_params(("arbitrary", "arbitrary")), **specs)(a, b, gathered)


TOK_TILE = 256


def _behind(step, last, copies):
    @pl.when(step == 0)
    def _():
        for cp in copies(with_arrivals=False)[0]:
            cp.start()

    @pl.when(step == last)
    def _():
        sends, arrivals = copies()
        for cp in arrivals:
            cp.wait_recv()
        for cp in sends:
            cp.wait_send()


def _s5_put_rows(rows_ref, scr, val):
    nchunk = scr.shape[0]
    for c in range(nchunk):
        scr[c] = val[S5_T * c:S5_T * (c + 1), :]
    for b in range(S5_NB):
        for j in range(S5_T):
            rows_ref[b, :, 128 * j:128 * (j + 1)] = scr[:, j, 128 * b:128 * (b + 1)].astype(BF16)


def _norm_inproj(x, ctx, n1w, mod4, w_in_b, cos_t, sin_t, placed, kinds):
    l, lc = x.shape[0], ctx.shape[0]
    tm = TOK_TILE
    nct = lc // tm
    la = l + lc
    n = len(placed)
    shard_shapes = _gather_shard_shapes(placed, kinds)

    def body(x_ref, c_ref, nw_ref, mod_ref, w_ref, cos_ref, sin_ref, *rest):
        p_ref, h_ref, u_ref = rest[n:n + 3]
        send_sems, recv_sems, u_scr = rest[2 * n + 3:]
        _behind(pl.program_id(0), la // tm - 1,
                functools.partial(_gather_chip_copies, rest[n + 3:2 * n + 3], kinds, shard_shapes, send_sems, recv_sems))
        is_ctx = pl.program_id(0) < nct
        xt = jnp.where(is_ctx, c_ref[...], x_ref[...])
        sh = jnp.where(is_ctx, mod_ref[0:1, :], mod_ref[2:3, :])
        sc = jnp.where(is_ctx, mod_ref[1:2, :], mod_ref[3:4, :])
        hb = _rms_mod(xt, nw_ref[...], sh, sc).astype(BF16)
        h_ref[...] = hb
        p = _dot(hb, w_ref[...])
        p_ref[...] = p
        cs, sn = cos_ref[...], sin_ref[...]
        for h in range(RET_HEADS):
            q_cols = slice(RET_WIDTH + RET_DH * h, RET_WIDTH + RET_DH * (h + 1))
            k_cols = slice(2 * RET_WIDTH + RET_DH * h, 2 * RET_WIDTH + RET_DH * (h + 1))
            p_ref[:, q_cols] = _rope(p[:, q_cols], cs, sn)
            p_ref[:, k_cols] = _rope(p[:, k_cols] * K_SCALE, cs, sn)
        _s5_put_rows(u_ref, u_scr, p[:, 0:S5_WIDTH])

    return pl.pallas_call(
        body, name="norm_inproj", grid=(la // tm,),
        in_specs=[pl.BlockSpec((tm, D_MODEL), lambda i: (jnp.maximum(i - nct, 0), 0)),
                  pl.BlockSpec((tm, D_MODEL), lambda i: (jnp.minimum(i, nct - 1), 0)),
                  _full((1, D_MODEL)), _full((4, D_MODEL)), _full((D_MODEL, IN_COLS)),
                  pl.BlockSpec((tm, RET_DH), lambda i: (i, 0)), pl.BlockSpec((tm, RET_DH), lambda i: (i, 0))] + [ANY] * n,
        out_specs=[pl.BlockSpec((tm, IN_COLS), lambda i: (i, 0)), pl.BlockSpec((tm, D_MODEL), lambda i: (i, 0)),
                   pl.BlockSpec((S5_NB, tm // S5_T, S5_BW), lambda i: (0, i, 0))] + [ANY] * n,
        out_shape=[jax.ShapeDtypeStruct((la, IN_COLS), F32), jax.ShapeDtypeStruct((la, D_MODEL), BF16),
                   jax.ShapeDtypeStruct((S5_NB, la // S5_T, S5_BW), BF16)]
        + [jax.ShapeDtypeStruct(p.shape, p.dtype) for p in placed],
        input_output_aliases={7 + a: 3 + a for a in range(n)},
        scratch_shapes=[pltpu.SemaphoreType.DMA((n, 3)), pltpu.SemaphoreType.DMA((n, 3)),
                        pltpu.VMEM((tm // S5_T, S5_T, S5_WIDTH), F32)],
        compiler_params=_params(("arbitrary",)),
    )(x, ctx, n1w, mod4, w_in_b, cos_t, sin_t, *placed)


def _iota2(shape, dim):
    return lax.broadcasted_iota(jnp.int32, shape, dim)


def _group_mask(rows, cols, row_div, col_div):
    return jnp.where(_iota2((rows, cols), 0) // row_div == _iota2((rows, cols), 1) // col_div, 1.0, 0.0).astype(F32)


def _s5_gen_dir(lre, lim, lst, b_re, b_im, c_re, c_im):
    step = jnp.exp(lst)
    mag = jnp.exp(lre * step)
    ar = mag * jnp.cos(lim * step)
    ai = mag * jnp.sin(lim * step)
    den = lre * lre + lim * lim
    xr = ar - 1.0
    cr = (xr * lre + ai * lim) / den
    ci = (ai * lre - xr * lim) / den
    rexp = _group_mask(128, 8, S5_GROUP, 1)
    are, aie = _dot_hi(rexp, ar), _dot_hi(rexp, ai)
    cre, cie = _dot_hi(rexp, cr), _dot_hi(rexp, ci)
    bbr = cre * b_re - cie * b_im
    bbi = cre * b_im + cie * b_re
    gmask = _group_mask(128, 128, S5_GROUP, S5_GROUP)
    pr, pi = jnp.ones_like(are), jnp.zeros_like(are)
    xs, ys = [], []
    for t in range(S5_T + 1):
        if t < S5_T:
            xs.append(jnp.concatenate([bbr * pr - bbi * pi, bbr * pi + bbi * pr], axis=1))
        ys.append(jnp.concatenate([c_re * pr - c_im * pi, -(c_re * pi + c_im * pr)], axis=1))
        pr, pi = pr * are - pi * aie, pr * aie + pi * are
    gs = [_dot_nt_hi(x_t, ys[0]) * gmask for x_t in xs]
    r16, i16 = ar, ai
    for _ in range(4):
        r16, i16 = r16 * r16 - i16 * i16, 2.0 * r16 * i16
    return xs, ys, gs, jnp.concatenate([r16, i16], axis=1)


def _s5_expand(z):
    return jnp.concatenate([z] * 8, axis=1) * _group_mask(128, S5_SW, S5_GROUP, 128)


def _s5_contract(z):
    zm = z * _group_mask(128, S5_SW, S5_GROUP, 128)
    acc = zm[:, 0:128]
    for k in range(1, 8):
        acc = acc + zm[:, 128 * k:128 * (k + 1)]
    return acc


def _s5_param_specs():
    blk3 = lambda r, c: pl.BlockSpec((1, 1, r, c), lambda b, *_: (0, b, 0, 0))
    dir3 = lambda r, c: pl.BlockSpec((2, 1, r, c), lambda b, *_: (0, b, 0, 0))
    return [dir3(8, S5_STATE), dir3(8, S5_STATE), dir3(8, 1), blk3(128, S5_STATE), blk3(128, S5_STATE),
            blk3(128, S5_STATE), blk3(128, S5_STATE), blk3(1, 128)]


def _s5_gen(lre, lim, lst, b_re, b_im, c_re, c_im, dvec, placed, kinds):
    n = len(placed)
    shard_shapes = _gather_shard_shapes(placed, kinds)

    def body(lre_ref, lim_ref, lst_ref, bre_ref, bim_ref, cre_ref, cim_ref, d_ref, *rest):
        gg_ref, xw_ref, yw_ref, a16_ref = rest[n:n + 4]
        _behind(pl.program_id(0), S5_NB - 1,
                functools.partial(_gather_chip_copies, rest[n + 4:2 * n + 4], kinds, shard_shapes, *rest[2 * n + 4:]))
        eye = _group_mask(128, 128, 1, 1)
        g0 = eye * d_ref[0, 0]
        for dr in range(2):
            xs, ys, gs, a16 = _s5_gen_dir(lre_ref[dr, 0], lim_ref[dr, 0], lst_ref[dr, 0], bre_ref[0, 0],
                                          bim_ref[0, 0], cre_ref[0, 0], cim_ref[0, 0])
            a16_ref[0, dr] = a16
            for j in range(S5_T):
                xw_ref[0, dr, j] = xs[S5_T - 1 - j if dr == 0 else j]
                yw_ref[0, dr, j] = ys[j + 1 if dr == 0 else S5_T - j]
            g0 = g0 + gs[0]
            for t in range(1, S5_T):
                gg_ref[0, (S5_T - 1) + t if dr == 0 else (S5_T - 1) - t] = gs[t]
        gg_ref[0, S5_T - 1] = g0

    blk = pl.BlockSpec((1, 2, S5_T, 128, 128), lambda b: (b, 0, 0, 0, 0))
    return pl.pallas_call(
        body, name="s5_gen", grid=(S5_NB,),
        in_specs=_s5_param_specs() + [ANY] * n,
        out_specs=[pl.BlockSpec((1, 2 * S5_T - 1, 128, 128), lambda b: (b, 0, 0, 0)), blk, blk,
                   pl.BlockSpec((1, 2, 8, 128), lambda b: (b, 0, 0, 0))] + [ANY] * n,
        out_shape=[jax.ShapeDtypeStruct((S5_NB, 2 * S5_T - 1, 128, 128), F32),
                   jax.ShapeDtypeStruct((S5_NB, 2, S5_T, 128, 128), F32),
                   jax.ShapeDtypeStruct((S5_NB, 2, S5_T, 128, 128), F32),
                   jax.ShapeDtypeStruct((S5_NB, 2, 8, 128), F32)]
        + [jax.ShapeDtypeStruct(p.shape, p.dtype) for p in placed],
        input_output_aliases={8 + a: 4 + a for a in range(n)},
        scratch_shapes=[pltpu.SemaphoreType.DMA((n, 3)), pltpu.SemaphoreType.DMA((n, 3))],
        compiler_params=_params(("arbitrary",)),
    )(lre, lim, lst, b_re, b_im, c_re, c_im, dvec, *placed)


def _s5_fill_state_mat(w_scr, src_ref, dr):
    for j in range(S5_T):
        w_scr[128 * j:128 * (j + 1), :] = _s5_expand(src_ref[0, dr, j]).astype(BF16)


def _s5_fill_toeplitz(k_scr, gg_ref):
    for j in range(S5_T):
        for i in range(S5_T):
            k_scr[128 * j:128 * (j + 1), 128 * i:128 * (i + 1)] = gg_ref[0, i - j + (S5_T - 1)].astype(BF16)


S5_GEN_SPECS = [pl.BlockSpec((1, 2 * S5_T - 1, 128, 128), lambda b: (b, 0, 0, 0)),
                pl.BlockSpec((1, 2, S5_T, 128, 128), lambda b: (b, 0, 0, 0, 0))]


def _s5_gen_bwd(lre, lim, lst, b_re, b_im, c_re, c_im, dvec, dg, dx, dy, da16):
    def body(lre_ref, lim_ref, lst_ref, bre_ref, bim_ref, cre_ref, cim_ref, d_ref, dg_ref, dx_ref, dy_ref, da16_ref,
             glre_ref, glim_ref, glst_ref, gbre_ref, gbim_ref, gcre_ref, gcim_ref, gd_ref):
        eye = _group_mask(128, 128, 1, 1)
        gd_ref[0, 0] = jnp.sum(dg_ref[0, S5_T - 1] * eye, axis=0, keepdims=True)
        gb = [None, None, None, None]
        for dr in range(2):
            args = (lre_ref[dr, 0], lim_ref[dr, 0], lst_ref[dr, 0], bre_ref[0, 0], bim_ref[0, 0],
                    cre_ref[0, 0], cim_ref[0, 0])
            _, vjp = jax.vjp(_s5_gen_dir, *args)
            dxs = [dx_ref[0, dr, S5_T - 1 - t if dr == 0 else t] for t in range(S5_T)]
            dys = [jnp.zeros((128, 128), F32)] + [dy_ref[0, dr, t - 1 if dr == 0 else S5_T - t]
                                                  for t in range(1, S5_T + 1)]
            dgs = [dg_ref[0, (S5_T - 1) + t if dr == 0 else (S5_T - 1) - t] for t in range(S5_T)]
            g = vjp((dxs, dys, dgs, da16_ref[0, dr]))
            glre_ref[dr, 0] = g[0]
            glim_ref[dr, 0] = g[1]
            glst_ref[dr, 0] = g[2]
            for q in range(4):
                gb[q] = g[3 + q] if gb[q] is None else gb[q] + g[3 + q]
        gbre_ref[0, 0] = gb[0]
        gbim_ref[0, 0] = gb[1]
        gcre_ref[0, 0] = gb[2]
        gcim_ref[0, 0] = gb[3]

    shp = lambda a: jax.ShapeDtypeStruct(a.shape, F32)
    return pl.pallas_call(
        body, name="s5_gen_bwd", grid=(S5_NB,),
        in_specs=_s5_param_specs() + [
            pl.BlockSpec((1, 2 * S5_T - 1, 128, 128), lambda b: (b, 0, 0, 0)),
            pl.BlockSpec((1, 2, S5_T, 128, 128), lambda b: (b, 0, 0, 0, 0)),
            pl.BlockSpec((1, 2, S5_T, 128, 128), lambda b: (b, 0, 0, 0, 0)),
            pl.BlockSpec((1, 2, 8, 128), lambda b: (b, 0, 0, 0))],
        out_specs=_s5_param_specs(),
        out_shape=[shp(lre), shp(lim), shp(lst), shp(b_re), shp(b_im), shp(c_re), shp(c_im), shp(dvec)],
        compiler_params=_params(("parallel",)),
    )(lre, lim, lst, b_re, b_im, c_re, c_im, dvec, dg, dx, dy, da16)


def _s5_put_groups(o_ref, dr, val):
    for gi in range(8):
        o_ref[dr, :, gi, :] = val[:, 128 * gi:128 * (gi + 1)]


def _s5_get_groups(s_ref, dr, n=8):
    return jnp.concatenate([s_ref[dr, :, gi, :] for gi in range(n)], axis=1).astype(BF16)


def _s5_to_states(u3, blocks, name):
    cn = u3.shape[1]

    def body(u_ref, b_ref, o_ref, w_scr):
        u = u_ref[0]
        for dr in range(2):
            _s5_fill_state_mat(w_scr, b_ref, dr)
            _s5_put_groups(o_ref, dr, _dot(u, w_scr[...]))

    return pl.pallas_call(
        body, name=name, grid=(S5_NB,),
        in_specs=[pl.BlockSpec((1, cn, S5_BW), lambda b: (b, 0, 0)), S5_GEN_SPECS[1]],
        out_specs=pl.BlockSpec((2, cn, 8, 128), lambda b: (0, 0, b, 0)),
        out_shape=jax.ShapeDtypeStruct((2, cn, S5_GROUPS, 128), F32),
        scratch_shapes=[pltpu.VMEM((S5_BW, S5_SW), BF16)],
        compiler_params=_params(("parallel",)),
    )(u3, blocks)


def _s5_from_states(u3, gg, st, blocks, transposed, name):
    cn = u3.shape[1]

    def body(u_ref, g_ref, s_ref, b_ref, o_ref, k_scr, w_scr):
        u = u_ref[0]
        _s5_fill_toeplitz(k_scr, g_ref)
        y = _dot_nt(u, k_scr[...]) if transposed else _dot(u, k_scr[...])
        for dr in range(2):
            _s5_fill_state_mat(w_scr, b_ref, dr)
            y = y + _dot_nt(_s5_get_groups(s_ref, dr), w_scr[...])
        for i in range(S5_T):
            o_ref[:, i, :] = y[:, 128 * i:128 * (i + 1)]

    return pl.pallas_call(
        body, name=name, grid=(S5_NB,),
        in_specs=[pl.BlockSpec((1, cn, S5_BW), lambda b: (b, 0, 0)), S5_GEN_SPECS[0],
                  pl.BlockSpec((2, cn, 8, 128), lambda b: (0, 0, b, 0)), S5_GEN_SPECS[1]],
        out_specs=pl.BlockSpec((cn, S5_T, 128), lambda b: (0, 0, b)),
        out_shape=jax.ShapeDtypeStruct((cn, S5_T, S5_WIDTH), F32),
        scratch_shapes=[pltpu.VMEM((S5_BW, S5_BW), BF16), pltpu.VMEM((S5_BW, S5_SW), BF16)],
        compiler_params=_params(("parallel",)),
    )(u3, gg, st, blocks)


def _s5_a_forms(a):
    ra = pltpu.roll(a, S5_STATE, 1)
    low = _iota2(a.shape, 1) < S5_STATE
    return jnp.where(low, a, ra), jnp.where(low, -ra, a)


def _s5_scan(sloc, a16, ncc, placed, kinds):
    cn = sloc.shape[1]
    n = len(placed)
    shard_shapes = _gather_shard_shapes(placed, kinds)

    def body(s_ref, a_ref, *rest):
        h_ref = rest[n]
        sends, arrivals = _gather_chip_copies(rest[n + 1:2 * n + 1], kinds, shard_shapes, *rest[2 * n + 1:])
        for cp in sends:
            cp.start()
        forms = [_s5_a_forms(a_ref[dr]) for dr in range(2)]

        def step(s, hs):
            out = []
            for dr in range(2):
                arr, aii = forms[dr]
                h, rh = hs[dr]
                c = s if dr == 0 else jnp.where(s < ncc, ncc - 1 - s, cn - 1 - (s - ncc))
                h_ref[dr, c] = h
                sc = s_ref[dr, c]
                out.append((h * arr + rh * aii + sc, rh * arr - h * aii + pltpu.roll(sc, S5_STATE, 1)))
            return tuple(out)

        zero = jnp.zeros((S5_GROUPS, 128), F32)
        lax.fori_loop(0, cn, step, ((zero, zero), (zero, zero)), unroll=4)
        for cp in arrivals:
            cp.wait_recv()
        for cp in sends:
            cp.wait_send()

    vmem = pl.BlockSpec(memory_space=pltpu.VMEM)
    return pl.pallas_call(
        body, name="s5_scan",
        in_specs=[vmem, vmem] + [ANY] * n, out_specs=[vmem] + [ANY] * n,
        out_shape=[jax.ShapeDtypeStruct(sloc.shape, F32)] + [jax.ShapeDtypeStruct(p.shape, p.dtype) for p in placed],
        input_output_aliases={2 + a: 1 + a for a in range(n)},
        scratch_shapes=[pltpu.SemaphoreType.DMA((n, 3)), pltpu.SemaphoreType.DMA((n, 3))],
        compiler_params=_params(),
    )(sloc, a16, *placed)


def _s5_scan_bwd(e, hs, a16, ncc):
    cn = e.shape[1]

    def body(e_ref, h_ref, a_ref, ds_ref, da_ref):
        forms = [_s5_a_forms(a_ref[dr]) for dr in range(2)]
        low = _iota2((S5_GROUPS, 128), 1) < S5_STATE

        def step(s, carry):
            out = []
            r = cn - 1 - s
            for dr in range(2):
                arr, aii = forms[dr]
                g, rg, da = carry[dr]
                c = r if dr == 0 else jnp.where(r < ncc, ncc - 1 - r, cn - 1 - (r - ncc))
                ds_ref[dr, c] = g
                h = h_ref[dr, c]
                rh = pltpu.roll(h, S5_STATE, 1)
                da = da + jnp.where(low, g * h + rg * rh, g * rh - rg * h)
                ec = e_ref[dr, c]
                out.append((ec + g * arr - rg * aii, pltpu.roll(ec, S5_STATE, 1) + rg * arr + g * aii, da))
            return tuple(out)

        zero = jnp.zeros((S5_GROUPS, 128), F32)
        res = lax.fori_loop(0, cn, step, ((zero, zero, zero), (zero, zero, zero)), unroll=4)
        da_ref[0] = res[0][2]
        da_ref[1] = res[1][2]

    return pl.pallas_call(
        body, name="s5_scan_bwd",
        out_shape=[jax.ShapeDtypeStruct(e.shape, F32), jax.ShapeDtypeStruct((2, S5_GROUPS, 128), F32)],
        compiler_params=_params(),
    )(e, hs, a16)


def _s5_bwd_kb(p3, dy3):
    cn = p3.shape[1]
    half = S5_T // 2

    def body(u_ref, d_ref, o_ref):
        q = pl.program_id(1)

        @pl.when(q == 0)
        def _():
            o_ref[...] = jnp.zeros_like(o_ref)

        dk = _dot_tn(u_ref[0], d_ref[0])
        for j in range(S5_T):
            for i in range(half):
                o_ref[0, half * q + i - j + (S5_T - 1)] += dk[128 * j:128 * (j + 1), 128 * i:128 * (i + 1)]

    return pl.pallas_call(
        body, name="s5_bwd_kb", grid=(S5_NB, 2),
        in_specs=[pl.BlockSpec((1, cn, S5_BW), lambda b, q: (b, 0, 0)),
                  pl.BlockSpec((1, cn, S5_BW // 2), lambda b, q: (b, 0, q))],
        out_specs=pl.BlockSpec((1, 2 * S5_T - 1, 128, 128), lambda b, q: (b, 0, 0, 0)),
        out_shape=jax.ShapeDtypeStruct((S5_NB, 2 * S5_T - 1, 128, 128), F32),
        compiler_params=_params(("parallel", "arbitrary")),
    )(p3, dy3)


def _s5_bwd_w(u3, st, name):
    cn = u3.shape[1]

    def body(u_ref, s_ref, w_ref):
        dw = _dot_tn(u_ref[0], _s5_get_groups(s_ref, 0))
        for j in range(S5_T):
            w_ref[0, 0, j] = _s5_contract(dw[128 * j:128 * (j + 1), :])

    return pl.pallas_call(
        body, name=name, grid=(S5_NB, 2),
        in_specs=[pl.BlockSpec((1, cn, S5_BW), lambda b, q: (b, 0, 0)),
                  pl.BlockSpec((1, cn, 8, 128), lambda b, q: (q, 0, b, 0))],
        out_specs=pl.BlockSpec((1, 1, S5_T, 128, 128), lambda b, q: (b, q, 0, 0, 0)),
        out_shape=jax.ShapeDtypeStruct((S5_NB, 2, S5_T, 128, 128), F32),
        compiler_params=_params(("parallel", "parallel")),
    )(u3, st)


K_SCALE = RET_DH ** -0.5
G_COL = 16


def _ret_chunk_of(step, ncc, nch, rev):
    if not rev:
        return step
    return jnp.where(step < ncc, ncc - 1 - step, nch - 1 - (step - ncc))


def _ret_decay(ld, rev):
    c = _iota2((RET_CHUNK, RET_CHUNK), 0).astype(F32)
    m = _iota2((RET_CHUNK, RET_CHUNK), 1).astype(F32)
    diff = (m - c) if rev else (c - m)
    keep = (diff > 0) if rev else (diff >= 0)
    expo = jnp.maximum(diff, 0.0)
    dm = jnp.where(keep, jnp.exp(ld * expo), 0.0)
    xi_e = (RET_CHUNK - c) if rev else (c + 1.0)
    zeta_e = c if rev else (RET_CHUNK - 1.0 - c)
    return dm, expo, jnp.exp(ld * xi_e), xi_e, jnp.exp(ld * zeta_e), zeta_e


RET_TABLES = 7


def _ret_tables(ld2):
    def body(ld_ref, t_ref):
        dr, h = pl.program_id(0), pl.program_id(1)
        ldh = ld_ref[dr, h]
        for rev in (False, True):
            @pl.when(dr == int(rev))
            def _(rev=rev):
                dm, expo, xi, xi_e, zeta, zeta_e = _ret_decay(ldh, rev)
                t_ref[0, 0, 0] = dm
                t_ref[0, 0, 1] = dm * expo
                t_ref[0, 0, 2] = xi
                t_ref[0, 0, 3] = xi * xi_e
                t_ref[0, 0, 4] = zeta
                t_ref[0, 0, 5] = zeta * zeta_e
                t_ref[0, 0, 6] = jnp.zeros_like(dm) + jnp.exp(ldh * RET_CHUNK)

    return pl.pallas_call(
        body, name="ret_tables", grid=(2, RET_HEADS),
        in_specs=[pl.BlockSpec(memory_space=pltpu.SMEM)],
        out_specs=pl.BlockSpec((1, 1, RET_TABLES, RET_CHUNK, RET_CHUNK), lambda d, h: (d, h, 0, 0, 0)),
        out_shape=jax.ShapeDtypeStruct((2, RET_HEADS, RET_TABLES, RET_CHUNK, RET_CHUNK), F32),
        compiler_params=_params(("parallel", "parallel")),
    )(ld2)


def _ret_specs(nch, ncc, rev, step_of):
    chunk = lambda n: _ret_chunk_of(step_of(n), ncc, nch, rev)
    cols = [pl.BlockSpec((RET_CHUNK, RET_WIDTH), functools.partial(lambda n, cb: (chunk(n), cb), cb=cb))
            for cb in (1, 2, 3)]
    return cols, pl.BlockSpec((RET_CHUNK, RET_WIDTH), lambda n: (chunk(n), 0))


def _ret_scan(p_all, tabs, ncc):
    la = p_all.shape[0]
    nch = la // RET_CHUNK

    def body(t_ref, qf, kf, vf, qb, kb, vb, of_ref, ob_ref, ssf_ref, ssb_ref, s_scr):
        @pl.when(pl.program_id(0) == 0)
        def _():
            s_scr[...] = jnp.zeros_like(s_scr)

        for dr, (q_ref, k_ref, v_ref, o_ref, ss_ref) in enumerate(
                ((qf, kf, vf, of_ref, ssf_ref), (qb, kb, vb, ob_ref, ssb_ref))):
            for h in range(RET_HEADS):
                sl = slice(RET_DH * h, RET_DH * (h + 1))
                dm, xi, zeta = t_ref[dr, h, 0], t_ref[dr, h, 2, :, 0:RET_DH], t_ref[dr, h, 4, :, 0:RET_DH]
                q, k = q_ref[:, sl], k_ref[:, sl]
                vh = v_ref[:, sl].astype(BF16)
                s = s_scr[dr, h]
                ss_ref[0, h] = s
                sc = (_dot_nt(q.astype(BF16), k.astype(BF16)) * dm).astype(BF16)
                o_ref[:, sl] = _dot(sc, vh) + _dot((q * xi).astype(BF16), s.astype(BF16))
                s_scr[dr, h] = t_ref[dr, h, 6, 0:RET_DH, 0:RET_DH] * s + _dot_tn((k * zeta).astype(BF16), vh)

    in_f, out_f = _ret_specs(nch, ncc, False, lambda n: n)
    in_b, out_b = _ret_specs(nch, ncc, True, lambda n: n)
    ss_spec = pl.BlockSpec((1, RET_HEADS, RET_DH, RET_DH), lambda n: (n, 0, 0, 0))
    o_shape = jax.ShapeDtypeStruct((la, RET_WIDTH), F32)
    ss_shape = jax.ShapeDtypeStruct((nch, RET_HEADS, RET_DH, RET_DH), F32)
    return pl.pallas_call(
        body, name="ret_scan", grid=(nch,),
        in_specs=[_full(tabs.shape)] + in_f + in_b,
        out_specs=[out_f, out_b, ss_spec, ss_spec],
        out_shape=[o_shape, o_shape, ss_shape, ss_shape],
        scratch_shapes=[pltpu.VMEM((2, RET_HEADS, RET_DH, RET_DH), F32)],
        compiler_params=_params(("arbitrary",)),
    )(tabs, p_all, p_all, p_all, p_all, p_all, p_all)


def _ret_scan_bwd(p_all, tabs, ssf, ssb, dy_all, ncc):
    la = p_all.shape[0]
    nch = la // RET_CHUNK

    def body(t_ref, qf, kf, vf, dof, ssf_ref, qb, kb, vb, dob_, ssb_ref,
             dqf, dkf, dvf, dqb, dkb, dvb, dld_ref, ds_scr):
        @pl.when(pl.program_id(0) == 0)
        def _():
            ds_scr[...] = jnp.zeros_like(ds_scr)
            dld_ref[...] = jnp.zeros_like(dld_ref)

        for dr, (q_ref, k_ref, v_ref, do_ref, ss_ref, dq_ref, dk_ref, dv_ref) in enumerate(
                ((qf, kf, vf, dof, ssf_ref, dqf, dkf, dvf), (qb, kb, vb, dob_, ssb_ref, dqb, dkb, dvb))):
            on_ctx = _ret_chunk_of(nch - 1 - pl.program_id(0), ncc, nch, dr == 1) < ncc
            for h in range(RET_HEADS):
                sl = slice(RET_DH * h, RET_DH * (h + 1))
                dm, dm_d = t_ref[dr, h, 0], t_ref[dr, h, 1]
                xi, xi_d, zeta, zeta_d = [t_ref[dr, h, t, :, 0:RET_DH] for t in (2, 3, 4, 5)]
                gc = t_ref[dr, h, 6, 0:RET_DH, 0:RET_DH]
                q, k = q_ref[:, sl], k_ref[:, sl]
                q16, k16, v16 = q.astype(BF16), k.astype(BF16), v_ref[:, sl].astype(BF16)
                s = ss_ref[0, h]
                s16 = s.astype(BF16)
                ds_in = ds_scr[dr, h]
                ds16 = ds_in.astype(BF16)
                do16 = jnp.where(on_ctx, 0.0, do_ref[:, sl]).astype(BF16)
                qk = _dot_nt(q16, k16)
                dsv = _dot_nt(do16, v16)
                dsc = (dsv * dm).astype(BF16)
                sc16 = (qk * dm).astype(BF16)
                dos = _dot_nt(do16, s16)
                vds = _dot_nt(v16, ds16)
                dq_ref[:, sl] = _dot(dsc, k16) + dos * xi
                dk_ref[:, sl] = _dot_tn(dsc, q16) + vds * zeta
                dv_ref[:, sl] = _dot_tn(sc16, do16) + _dot((k * zeta).astype(BF16), ds16)
                ds_scr[dr, h] = _dot_tn((q * xi).astype(BF16), do16) + gc * ds_in
                dld = (jnp.sum(dsv * qk * dm_d) + jnp.sum(q * dos * xi_d + k * vds * zeta_d)
                       + RET_CHUNK * jnp.sum(gc * s * ds_in))
                dld_ref[dr, h] += dld

    back = lambda n: nch - 1 - n
    in_f, out_f = _ret_specs(nch, ncc, False, back)
    in_b, out_b = _ret_specs(nch, ncc, True, back)
    ss_spec = pl.BlockSpec((1, RET_HEADS, RET_DH, RET_DH), lambda n: (nch - 1 - n, 0, 0, 0))
    shp = jax.ShapeDtypeStruct((la, RET_WIDTH), F32)
    dy_spec = lambda rev: pl.BlockSpec(
        (RET_CHUNK, RET_WIDTH), lambda n: (jnp.maximum(_ret_chunk_of(nch - 1 - n, ncc, nch, rev) - ncc, 0), 0))
    return pl.pallas_call(
        body, name="ret_scan_bwd", grid=(nch,),
        in_specs=[_full(tabs.shape)] + in_f + [dy_spec(False), ss_spec] + in_b + [dy_spec(True), ss_spec],
        out_specs=[out_f, out_f, out_f, out_b, out_b, out_b, _full((2, RET_HEADS, 8, 128))],
        out_shape=[shp] * 6 + [jax.ShapeDtypeStruct((2, RET_HEADS, 8, 128), F32)],
        scratch_shapes=[pltpu.VMEM((2, RET_HEADS, RET_DH, RET_DH), F32)],
        compiler_params=_params(("arbitrary",)),
    )(tabs, p_all, p_all, p_all, dy_all, ssf, p_all, p_all, p_all, dy_all, ssb)


def _in_bwd(dqf, dkf, dvf, dqb, dkb, dvb, du, dg, cos_t, sin_t, w_in_b, x, ctx, n1w, mod4, dx1):
    l, lc = x.shape[0], ctx.shape[0]
    la = l + lc
    tm = TOK_TILE
    nct = lc // tm

    def body(dqf_ref, dkf_ref, dvf_ref, dqb_ref, dkb_ref, dvb_ref, du_ref, dg_ref, cos_ref, sin_ref,
             w_ref, x_ref, c_ref, nw_ref, mod_ref, dx1_ref, dp_ref, gx_ref, acc_ref):
        i = pl.program_id(0)
        is_ctx = i < nct

        @pl.when(i == 0)
        def _():
            acc_ref[...] = jnp.zeros_like(acc_ref)

        cs, sn = cos_ref[...], sin_ref[...]
        def piece(k, val):
            cols = slice(S5_WIDTH * k, S5_WIDTH * (k + 1))
            dp_ref[:, cols] = val.astype(BF16)
            return _dot_nt(dp_ref[:, cols], w_ref[:, cols])

        dh1 = piece(0, du_ref[...])
        dh1 = dh1 + piece(3, dvf_ref[...] + dvb_ref[...])
        dh1 = dh1 + piece(4, jnp.where(is_ctx, 0.0, dg_ref[...]))
        for k, (f_ref, b_ref, scale) in ((1, (dqf_ref, dqb_ref, 1.0)), (2, (dkf_ref, dkb_ref, K_SCALE))):
            heads = [_rope_t(f_ref[:, RET_DH * h:RET_DH * (h + 1)] + b_ref[:, RET_DH * h:RET_DH * (h + 1)], cs, sn) * scale
                     for h in range(RET_HEADS)]
            dh1 = dh1 + piece(k, jnp.concatenate(heads, axis=1))
        xt = jnp.where(is_ctx, c_ref[...], x_ref[...])
        sh = jnp.where(is_ctx, mod_ref[0:1, :], mod_ref[2:3, :])
        sc = jnp.where(is_ctx, mod_ref[1:2, :], mod_ref[3:4, :])
        _, vjp = jax.vjp(_rms_mod, xt, nw_ref[...], sh, sc)
        dx, dnw, dsh, dsc = vjp(dh1)
        gx_ref[...] = dx + dx1_ref[...]
        cf = jnp.where(is_ctx, 1.0, 0.0)
        acc_ref[0:1, :] += dnw
        acc_ref[1:2, :] += cf * dsh
        acc_ref[2:3, :] += cf * dsc
        acc_ref[3:4, :] += (1.0 - cf) * dsh
        acc_ref[4:5, :] += (1.0 - cf) * dsc

    row = pl.BlockSpec((tm, RET_WIDTH), lambda i: (i, 0))
    tab = pl.BlockSpec((tm, RET_DH), lambda i: (i, 0))
    xrow = pl.BlockSpec((tm, D_MODEL), lambda i: (jnp.maximum(i - nct, 0), 0))
    return pl.pallas_call(
        body, name="in_bwd", grid=(la // tm,),
        in_specs=[row] * 7 + [pl.BlockSpec((tm, RET_WIDTH), lambda i: (jnp.maximum(i - nct, 0), 0)),
                              tab, tab, _full((D_MODEL, IN_COLS)), xrow,
                              pl.BlockSpec((tm, D_MODEL), lambda i: (jnp.minimum(i, nct - 1), 0)),
                              _full((1, D_MODEL)), _full((4, D_MODEL)), xrow],
        out_specs=[pl.BlockSpec((tm, IN_COLS), lambda i: (i, 0)), xrow, _full((8, D_MODEL))],
        out_shape=[jax.ShapeDtypeStruct((la, IN_COLS), BF16), jax.ShapeDtypeStruct((l, D_MODEL), F32),
                   jax.ShapeDtypeStruct((8, D_MODEL), F32)],
        compiler_params=_params(("arbitrary",)),
    )(dqf, dkf, dvf, dqb, dkb, dvb, du, dg, cos_t, sin_t, w_in_b, x, ctx, n1w, mod4, dx1)


def _outproj_up(x, y_all, of, ob, p_all, w_glu_b, b_glu, w_out_b, mod3, n2w, w_up_b, nct):
    l = x.shape[0]
    tm = TOK_TILE

    def body(x_ref, y_ref, of_ref, ob_ref, g_ref, wg_ref, bg_ref, wo_ref, mod_ref, nw_ref, wu_ref,
             x1_ref, mix_ref, h2_ref, up_ref, mb_ref, yr_ref):
        yg = _gelu(y_ref[...])
        mb_ref[:, 0:S5_WIDTH] = (yg * _sigmoid(_dot(yg.astype(BF16), wg_ref[...]) + bg_ref[...])).astype(BF16)
        yr = of_ref[...] + ob_ref[...]
        yr_ref[...] = yr
        for h in range(RET_HEADS):
            sl = slice(RET_DH * h, RET_DH * (h + 1))
            mb_ref[:, S5_WIDTH + RET_DH * h:S5_WIDTH + RET_DH * (h + 1)] = (
                _head_norm_gate(yr[:, sl], g_ref[:, sl]).astype(BF16))
        mix = _dot(mb_ref[...], wo_ref[...])
        mix_ref[...] = mix
        x1 = x_ref[...] + mod_ref[0:1, :] * mix
        x1_ref[...] = x1
        h2 = _rms_mod(x1, nw_ref[...], mod_ref[1:2, :], mod_ref[2:3, :]).astype(BF16)
        h2_ref[...] = h2
        up_ref[...] = _dot(h2, wu_ref[...])

    row = lambda w: pl.BlockSpec((tm, w), lambda i: (i, 0))
    arow = pl.BlockSpec((tm, RET_WIDTH), lambda i: (i + nct, 0))
    return pl.pallas_call(
        body, name="outproj_up", grid=(l // tm,),
        in_specs=[row(D_MODEL), arow, arow, arow, pl.BlockSpec((tm, RET_WIDTH), lambda i: (i + nct, G_COL // 4)),
                  _full((S5_WIDTH, S5_WIDTH)), _full((1, S5_WIDTH)), _full((D_MODEL, D_MODEL)), _full((3, D_MODEL)),
                  _full((1, D_MODEL)), _full((D_MODEL, 2 * D_FF))],
        out_specs=[row(D_MODEL), row(D_MODEL), row(D_MODEL), row(2 * D_FF), row(D_MODEL), row(RET_WIDTH)],
        out_shape=[jax.ShapeDtypeStruct((l, D_MODEL), F32), jax.ShapeDtypeStruct((l, D_MODEL), F32),
                   jax.ShapeDtypeStruct((l, D_MODEL), BF16), jax.ShapeDtypeStruct((l, 2 * D_FF), F32),
                   jax.ShapeDtypeStruct((l, D_MODEL), BF16), jax.ShapeDtypeStruct((l, RET_WIDTH), F32)],
        compiler_params=_params(("parallel",)),
    )(x, y_all, of, ob, p_all, w_glu_b, b_glu, w_out_b, mod3, n2w, w_up_b)


HALO = 8


def _conv_taps(g, prev_row, next_row):
    t = g.shape[0]
    r = _iota2(g.shape, 0)
    gprev = jnp.where(r == 0, prev_row, pltpu.roll(g, 1, 0))
    gnext = jnp.where(r == t - 1, next_row, pltpu.roll(g, t - 1, 0))
    return gprev, gnext


def _ffn_loss(up, x1, conv_w, conv_b, w_down_b, gate, fnw, tgt):
    l = x1.shape[0]
    tm = TOK_TILE
    nt = l // tm
    hb = tm // HALO

    cw = 256

    def body(up_a, up_g, hp_ref, hn_ref, x1_ref, cw_ref, cb_ref, wd_ref, gate_ref, fn_ref, tgt_ref,
             act_ref, dx2_ref, ddn_ref, dact_ref, acc_ref, ddn_scr):
        step = pl.program_id(0)
        i = jnp.minimum(step, nt - 1)

        @pl.when(step == 0)
        def _():
            acc_ref[...] = jnp.zeros_like(acc_ref)
            ddn_scr[...] = jnp.zeros_like(ddn_scr)

        ddn_prev = ddn_scr[...]
        dn = jnp.zeros((tm, D_MODEL), F32)
        for c in range(D_FF // cw):
            cols = slice(cw * c, cw * (c + 1))
            g = up_g[:, cols]
            prev_row = jnp.where(i == 0, 0.0, hp_ref[HALO - 1:HALO, cols])
            next_row = jnp.where(i == nt - 1, 0.0, hn_ref[0:1, cols])
            gprev, gnext = _conv_taps(g, prev_row, next_row)
            gc = cb_ref[:, cols] + gprev * cw_ref[0:1, cols] + g * cw_ref[1:2, cols] + gnext * cw_ref[2:3, cols]
            act = (_gelu(gc) * up_a[:, cols]).astype(BF16)
            act_ref[:, cols] = act
            dn = dn + _dot(act, wd_ref[cols, :])
            dact_ref[:, cols] = _dot_nt(ddn_prev, wd_ref[cols, :])
        x2 = x1_ref[...] + gate_ref[...] * dn
        y, vjp = jax.vjp(_rms, x2, fn_ref[...])
        err = y - tgt_ref[...]
        dx2, dfn = vjp(err * (1.0 / D_MODEL))
        dx2_ref[...] = dx2
        ddn = (dx2 * gate_ref[...]).astype(BF16)
        ddn_ref[...] = ddn
        ddn_scr[...] = ddn
        live = step < nt
        acc_ref[0:1, :] += jnp.where(live, dfn, 0.0)
        acc_ref[1:2, :] += jnp.where(live, jnp.sum(dx2 * dn, axis=0, keepdims=True), 0.0)
        acc_ref[2:3, :] += jnp.where(live, (0.5 / D_MODEL) * jnp.sum(err * err), 0.0)

    tile = lambda s: jnp.minimum(s, nt - 1)
    row = lambda w, cb=0: pl.BlockSpec((tm, w), lambda s: (tile(s), cb))
    last = l // HALO - 1
    return pl.pallas_call(
        body, name="ffn_loss", grid=(nt + 1,),
        in_specs=[row(D_FF, 0), row(D_FF, 1),
                  pl.BlockSpec((HALO, D_FF), lambda s: (jnp.maximum(tile(s) * hb - 1, 0), 1)),
                  pl.BlockSpec((HALO, D_FF), lambda s: (jnp.minimum((tile(s) + 1) * hb, last), 1)),
                  row(D_MODEL), _full((3, D_FF)), _full((1, D_FF)), _full((D_FF, D_MODEL)),
                  _full((1, D_MODEL)), _full((1, D_MODEL)), row(D_MODEL)],
        out_specs=[row(D_FF), row(D_MODEL), row(D_MODEL),
                   pl.BlockSpec((tm, D_FF), lambda s: (jnp.maximum(s - 1, 0), 0)), _full((8, D_MODEL))],
        out_shape=[jax.ShapeDtypeStruct((l, D_FF), BF16), jax.ShapeDtypeStruct((l, D_MODEL), F32),
                   jax.ShapeDtypeStruct((l, D_MODEL), BF16), jax.ShapeDtypeStruct((l, D_FF), F32),
                   jax.ShapeDtypeStruct((8, D_MODEL), F32)],
        scratch_shapes=[pltpu.VMEM((tm, D_MODEL), BF16)],
        compiler_params=_params(("arbitrary",)),
    )(up, up, up, up, x1, conv_w, conv_b, w_down_b, gate, fnw, tgt)


def _convglu_bwd(up, dact, conv_w, conv_b):
    l = up.shape[0]
    tm = 128
    nt = l // tm
    hb = tm // HALO
    te = tm + 2 * HALO

    def body(a_ref, ap_ref, an_ref, g_ref, gp_ref, gn_ref, d_ref, dp_ref, dn_ref, cw_ref, cb_ref,
             dup_ref, acc_ref):
        i = pl.program_id(0)

        @pl.when(i == 0)
        def _():
            acc_ref[...] = jnp.zeros_like(acc_ref)

        def ext(p, c, n):
            return jnp.concatenate([jnp.where(i == 0, 0.0, p[...]), c[...], jnp.where(i == nt - 1, 0.0, n[...])], axis=0)

        ae, ge, de = ext(ap_ref, a_ref, an_ref), ext(gp_ref, g_ref, gn_ref), ext(dp_ref, d_ref, dn_ref)
        gprev = pltpu.roll(ge, 1, 0)
        gnext = pltpu.roll(ge, te - 1, 0)
        w0, w1, w2 = cw_ref[0:1, :], cw_ref[1:2, :], cw_ref[2:3, :]
        gce = cb_ref[...] + gprev * w0 + ge * w1 + gnext * w2
        gel, dgel = _gelu_and_grad(gce)
        dae = de * gel
        dgce = de * ae * dgel
        dge = dgce * w1 + pltpu.roll(dgce, te - 1, 0) * w0 + pltpu.roll(dgce, 1, 0) * w2
        mid = slice(HALO, HALO + tm)
        dup_ref[:, 0:D_FF] = dae[mid].astype(BF16)
        dup_ref[:, D_FF:2 * D_FF] = dge[mid].astype(BF16)
        dgc = dgce[mid]
        acc_ref[0:1, :] += jnp.sum(dgc * gprev[mid], axis=0, keepdims=True)
        acc_ref[1:2, :] += jnp.sum(dgc * ge[mid], axis=0, keepdims=True)
        acc_ref[2:3, :] += jnp.sum(dgc * gnext[mid], axis=0, keepdims=True)
        acc_ref[3:4, :] += jnp.sum(dgc, axis=0, keepdims=True)

    last = l // HALO - 1

    def trio(cb):
        return [pl.BlockSpec((tm, D_FF), lambda i: (i, cb)),
                pl.BlockSpec((HALO, D_FF), lambda i: (jnp.maximum(i * hb - 1, 0), cb)),
                pl.BlockSpec((HALO, D_FF), lambda i: (jnp.minimum((i + 1) * hb, last), cb))]

    return pl.pallas_call(
        body, name="convglu_bwd", grid=(nt,),
        in_specs=trio(0) + trio(1) + trio(0) + [_full((3, D_FF)), _full((1, D_FF))],
        out_specs=[pl.BlockSpec((tm, 2 * D_FF), lambda i: (i, 0)), _full((8, D_FF))],
        out_shape=[jax.ShapeDtypeStruct((l, 2 * D_FF), BF16), jax.ShapeDtypeStruct((8, D_FF), F32)],
        compiler_params=_params(("arbitrary",)),
    )(up, up, up, up, up, up, dact, dact, dact, conv_w, conv_b)


def _up_bwd(dup, w_up_b, w_out_b, x1, dx2, mix, mod3, n2w, y_all, y_ret, p_all, w_glu_b, b_glu, zero_rows, nct, pairs,
            kinds):
    l = x1.shape[0]
    tm = TOK_TILE
    nt = l // tm
    n = len(pairs)
    shapes = _rs_slot_shapes(pairs, kinds)
    n_out = 8

    def body(dup_ref, wu_ref, wo_ref, x1_ref, dx2_ref, mix_ref, mod_ref, nw_ref, y_ref, yr_ref, g_ref, wg_ref, bg_ref,
             zero_rows_ref, *rest):
        dx1_ref, dmixb_ref, acc_ref, dys_ref, dyr_ref, dg_ref, gw_ref, gb_ref = rest[n:n + n_out]
        send_sems, recv_sems, dy_scr = rest[2 * n + n_out:]
        step = pl.program_id(0)

        @pl.when(step == 0)
        def _():
            acc_ref[...] = jnp.zeros_like(acc_ref)
            gw_ref[...] = jnp.zeros_like(gw_ref)
            gb_ref[...] = jnp.zeros_like(gb_ref)

        _behind(step, nt - 1, functools.partial(_rs_chip_copies, rest[:n], rest[n + n_out:2 * n + n_out], kinds,
                                                shapes, send_sems, recv_sems))

        dh2 = _dot_nt(dup_ref[...], wu_ref[...])
        _, vjp = jax.vjp(_rms_mod, x1_ref[...], nw_ref[...], mod_ref[1:2, :], mod_ref[2:3, :])
        dx, dnw, dsh, dsc = vjp(dh2)
        dx1 = dx + dx2_ref[...]
        dx1_ref[...] = dx1
        dmixb = (dx1 * mod_ref[0:1, :]).astype(BF16)
        dmixb_ref[...] = dmixb
        dmix = _dot_nt(dmixb, wo_ref[...])
        acc_ref[0:1, :] += dnw
        acc_ref[1:2, :] += jnp.sum(dx1 * mix_ref[...], axis=0, keepdims=True)
        acc_ref[2:3, :] += dsh
        acc_ref[3:4, :] += dsc

        yg, dgel = _gelu_and_grad(y_ref[...])
        ygb = yg.astype(BF16)
        sg = _sigmoid(_dot(ygb, wg_ref[...]) + bg_ref[...])
        ds = dmix[:, 0:S5_WIDTH]
        dz = ds * yg * sg * (1.0 - sg)
        dzb = dz.astype(BF16)
        _s5_put_rows(dys_ref, dy_scr, (ds * sg + _dot_nt(dzb, wg_ref[...])) * dgel)
        gw_ref[...] += _dot_tn(ygb, dzb)
        gb_ref[...] += jnp.sum(dz, axis=0, keepdims=True)

        for h in range(RET_HEADS):
            sl = slice(RET_DH * h, RET_DH * (h + 1))
            _, hvjp = jax.vjp(_head_norm_gate, yr_ref[:, sl], g_ref[:, sl])
            dyr, dg = hvjp(dmix[:, S5_WIDTH + RET_DH * h:S5_WIDTH + RET_DH * (h + 1)])
            dyr_ref[:, sl] = dyr
            dg_ref[:, sl] = dg

    row = pl.BlockSpec((tm, D_MODEL), lambda i: (i, 0))
    half = pl.BlockSpec((tm, S5_WIDTH), lambda i: (i, 0))
    f32h = jax.ShapeDtypeStruct((l, RET_WIDTH), F32)
    return pl.pallas_call(
        body, name="up_bwd", grid=(nt,),
        in_specs=[pl.BlockSpec((tm, 2 * D_FF), lambda i: (i, 0)), _full((D_MODEL, 2 * D_FF)),
                  _full((D_MODEL, D_MODEL)), row, row, row, _full((3, D_MODEL)), _full((1, D_MODEL)),
                  pl.BlockSpec((tm, S5_WIDTH), lambda i: (i + nct, 0)), half,
                  pl.BlockSpec((tm, RET_WIDTH), lambda i: (i + nct, G_COL // 4)),
                  _full((S5_WIDTH, S5_WIDTH)), _full((1, S5_WIDTH)), ANY] + [ANY] * n,
        out_specs=[row, row, _full((8, D_MODEL)),
                   pl.BlockSpec((S5_NB, tm // S5_T, S5_BW), lambda i: (0, i + nct, 0)), half, half,
                   _full((S5_WIDTH, S5_WIDTH)),
                   _full((1, S5_WIDTH))] + [ANY] * n,
        out_shape=[jax.ShapeDtypeStruct((l, D_MODEL), F32), jax.ShapeDtypeStruct((l, D_MODEL), BF16),
                   jax.ShapeDtypeStruct((8, D_MODEL), F32), jax.ShapeDtypeStruct(zero_rows.shape, BF16), f32h, f32h,
                   jax.ShapeDtypeStruct((S5_WIDTH, S5_WIDTH), F32), jax.ShapeDtypeStruct((1, S5_WIDTH), F32)]
        + [jax.ShapeDtypeStruct((4,) + s, p.dtype) for s, p in zip(shapes, pairs)],
        input_output_aliases={13: 3},
        scratch_shapes=[pltpu.SemaphoreType.DMA((n, 3)), pltpu.SemaphoreType.DMA((n, 3)),
                        pltpu.VMEM((tm // S5_T, S5_T, S5_WIDTH), F32)],
        compiler_params=_params(("arbitrary",)),
    )(dup, w_up_b, w_out_b, x1, dx2, mix, mod3, n2w, y_all, y_ret, p_all, w_glu_b, b_glu, zero_rows, *pairs)


MOD_ROWS = 16
MOD_COLS = 6 * D_MODEL // 4


def _mod_fwd(c_all, c_ctx, w_mod_b, b_loc):
    def body(c_ref, cc_ref, w_ref, b_ref, m_ref, s_ref):
        cond = jnp.concatenate([c_ref[...], jnp.broadcast_to(cc_ref[...], (8, D_MODEL))], axis=0)
        s = _silu(cond).astype(BF16)
        s_ref[...] = s
        m_ref[...] = _dot(s, w_ref[...]) + b_ref[...]

    return pl.pallas_call(
        body, name="mod_fwd",
        out_shape=[jax.ShapeDtypeStruct((MOD_ROWS, MOD_COLS), F32), jax.ShapeDtypeStruct((MOD_ROWS, D_MODEL), BF16)],
        compiler_params=_params(),
    )(c_all, c_ctx, w_mod_b, b_loc)


def _mod_bwd_sum(dm_all):
    def body(d_ref, dm_ref, gb_ref):
        rows = [d_ref[k, 0:1, :] for k in range(8)]
        ctx_sum = d_ref[0, 1:2, :]
        for k in range(1, 8):
            ctx_sum = ctx_sum + d_ref[k, 1:2, :]
        gb = ctx_sum
        for k in range(8):
            gb = gb + rows[k]
        gb_ref[...] = gb
        dm_ref[...] = jnp.concatenate(rows + [ctx_sum] + [jnp.zeros((7, 6 * D_MODEL), F32)], axis=0)

    return pl.pallas_call(
        body, name="mod_bwd_sum",
        out_shape=[jax.ShapeDtypeStruct((MOD_ROWS, 6 * D_MODEL), F32), jax.ShapeDtypeStruct((1, 6 * D_MODEL), F32)],
        compiler_params=_params(),
    )(dm_all)


def _mod_bwd_w(dm_loc, s_b, c_ctx, w_mod_b):
    def body(d_ref, s_ref, cc_ref, w_ref, gw_ref, gc_ref):
        db = d_ref[...].astype(BF16)
        gw_ref[...] = _dot_tn(s_ref[...], db)
        ds = _dot_nt(db, w_ref[...])
        _, vjp = jax.vjp(_silu, cc_ref[...])
        gc_ref[...] = jnp.broadcast_to(vjp(ds[8:9, :])[0], (8, D_MODEL))

    return pl.pallas_call(
        body, name="mod_bwd_w",
        out_shape=[jax.ShapeDtypeStruct((D_MODEL, MOD_COLS), F32), jax.ShapeDtypeStruct((8, D_MODEL), F32)],
        compiler_params=_params(),
    )(dm_loc, s_b, c_ctx, w_mod_b)


def _adamw(w, g, m, v, name):
    r, c = w.shape
    tr = _pick(r, (256, 128, 64, 32, 16, 8))
    bc1 = 1.0 - ADAM_B1 ** ADAM_STEP
    bc2 = 1.0 - ADAM_B2 ** ADAM_STEP

    def body(w_ref, g_ref, m_ref, v_ref, d_ref, nm_ref, nv_ref):
        gg = g_ref[...]
        nm = ADAM_B1 * m_ref[...] + (1.0 - ADAM_B1) * gg
        nv = ADAM_B2 * v_ref[...] + (1.0 - ADAM_B2) * (gg * gg)
        nm_ref[...] = nm
        nv_ref[...] = nv
        d_ref[...] = -ADAM_LR * ((nm / bc1) / (jnp.sqrt(nv / bc2) + ADAM_EPS) + ADAM_WD * w_ref[...])

    blk = pl.BlockSpec((tr, c), lambda i: (i, 0))
    shp = jax.ShapeDtypeStruct((r, c), F32)
    return pl.pallas_call(
        body, name=name, grid=(r // tr,), in_specs=[blk] * 4, out_specs=[blk] * 3, out_shape=[shp] * 3,
        compiler_params=_params(("parallel",)),
    )(w, g, m, v)


def _sum_slots(a, name):
    n, r, c = a.shape
    tr = _pick(r, (376, 256, 208, 128, 64, 32, 16, 8))

    def body(a_ref, o_ref):
        acc = a_ref[0].astype(F32)
        for k in range(1, n):
            acc = acc + a_ref[k].astype(F32)
        o_ref[...] = acc

    return pl.pallas_call(
        body, name=name, grid=(r // tr,),
        in_specs=[pl.BlockSpec((n, tr, c), lambda i: (0, i, 0))],
        out_specs=pl.BlockSpec((tr, c), lambda i: (i, 0)),
        out_shape=jax.ShapeDtypeStruct((r, c), F32),
        compiler_params=_params(("parallel",)),
    )(a)


def _mesh_pos():
    return lax.axis_index("x"), lax.axis_index("y"), lax.axis_index("c")


def _gather8_phases(x_ref, out_ref, send_sems, recv_sems, local_sem, m_per):
    def parts():
        x, y, c = _mesh_pos()
        me, sibling = (x, y, c), (x, y, 1 - c)
        chips = [(1 - x, y), (x, 1 - y), (1 - x, 1 - y)]

        def rows(px, py, pc):
            return out_ref.at[pl.ds((4 * px + 2 * py + pc) * m_per, m_per), :]

        def copy(k, block, to, src=None):
            return pltpu.make_async_remote_copy(
                src_ref=rows(*block) if src is None else src, dst_ref=rows(*block),
                send_sem=send_sems.at[k], recv_sem=recv_sems.at[k], device_id=to, device_id_type=MESH_ID)

        mine = pltpu.make_async_copy(x_ref, rows(*me), local_sem)
        first = [copy(0, me, sibling, src=x_ref)]
        first += [copy(1 + j, me, (*chip, c), src=x_ref) for j, chip in enumerate(chips)]
        return me, sibling, chips, c, copy, mine, first

    def begin():
        *_, mine, first = parts()
        mine.start()
        for cp in first:
            cp.start()

    def finish():
        me, sibling, chips, c, copy, mine, first = parts()
        passed = [copy(4 + j, (*chip, c), sibling) for j, chip in enumerate(chips)]
        for j, chip in enumerate(chips):
            copy(1 + j, (*chip, c), me).wait_recv()
            passed[j].start()
        copy(0, sibling, me).wait_recv()
        for j, chip in enumerate(chips):
            copy(4 + j, (*chip, 1 - c), me).wait_recv()
        for cp in first + passed:
            cp.wait_send()
        mine.wait()

    return begin, finish


GATHER8_SCRATCH = (pltpu.SemaphoreType.DMA((7,)), pltpu.SemaphoreType.DMA((7,)), pltpu.SemaphoreType.DMA)


def _all_gather8(v, name):
    m_per, n = v.shape

    def body(x_ref, out_ref, send_sems, recv_sems, local_sem):
        begin, finish = _gather8_phases(x_ref, out_ref, send_sems, recv_sems, local_sem, m_per)
        begin()
        finish()

    return pl.pallas_call(
        body, name=name,
        out_shape=jax.ShapeDtypeStruct((8 * m_per, n), v.dtype),
        in_specs=[pl.BlockSpec(memory_space=pltpu.VMEM)],
        out_specs=pl.BlockSpec(memory_space=pltpu.VMEM),
        scratch_shapes=list(GATHER8_SCRATCH),
        compiler_params=_params(),
    )(v)


ANY = pl.BlockSpec(memory_space=pl.ANY)
def PEER_CHIPS(x, y):
    return [(x, 1 - y), (1 - x, y), (1 - x, 1 - y)]


def _shard_region(ref, kind, k, rl, cl, r0, nr, c0, nc):
    if kind == "col":
        return ref.at[pl.ds(r0, nr), pl.ds(k * cl + c0, nc)]
    return ref.at[pl.ds(k * rl + r0, nr), pl.ds(c0, nc)]


def _place_shard(w, kind, chip, name):
    rl, cl = w.shape
    tr = _pick(rl, (256, 128, 64))
    nt = rl // tr

    def body(chip_ref, w_ref, o_ref):
        o_ref[...] = w_ref[...].astype(BF16)

    o_map = (lambda i, chip_ref: (i, chip_ref[0])) if kind == "col" else (lambda i, chip_ref: (chip_ref[0] * nt + i, 0))
    return pl.pallas_call(
        body, name=name,
        grid_spec=pltpu.PrefetchScalarGridSpec(
            num_scalar_prefetch=1, grid=(nt,),
            in_specs=[pl.BlockSpec((tr, cl), lambda i, chip_ref: (i, 0))], out_specs=pl.BlockSpec((tr, cl), o_map)),
        out_shape=jax.ShapeDtypeStruct((rl, 4 * cl) if kind == "col" else (4 * rl, cl), BF16),
        compiler_params=_params(("parallel",)),
    )(chip.reshape(1), w)


def _gather_shard_shapes(placed, kinds):
    return [(p.shape[0], p.shape[1] // 4) if k == "col" else (p.shape[0] // 4, p.shape[1]) for p, k in zip(placed, kinds)]


def _gather_chip_copies(outs, kinds, shard_shapes, send_sems, recv_sems, with_arrivals=True):
    x, y, c = _mesh_pos()
    me = 2 * x + y
    sends, arrivals = [], []
    for a in range(len(outs)):
        rl, cl = shard_shapes[a]
        rh = rl // 2
        reg = functools.partial(_shard_region, outs[a], kinds[a], rl=rl, cl=cl, r0=c * rh, nr=rh, c0=0, nc=cl)
        for j, (px, py) in enumerate(PEER_CHIPS(x, y)):
            to = dict(send_sem=send_sems.at[a, j], recv_sem=recv_sems.at[a, j], device_id=(px, py, c),
                      device_id_type=MESH_ID)
            sends.append(pltpu.make_async_remote_copy(src_ref=reg(k=me), dst_ref=reg(k=me), **to))
            if with_arrivals:
                got = reg(k=2 * px + py)
                arrivals.append(pltpu.make_async_remote_copy(src_ref=got, dst_ref=got, **to))
    return sends, arrivals


def _gather_sibling_copies(outs, kinds, shard_shapes, send_sems, recv_sems):
    x, y, c = _mesh_pos()
    forwards, arrivals = [], []
    for a in range(len(outs)):
        rl, cl = shard_shapes[a]
        rh = rl // 2
        for j, (px, py) in enumerate(PEER_CHIPS(x, y)):
            to = dict(send_sem=send_sems.at[a, j], recv_sem=recv_sems.at[a, j], device_id=(x, y, 1 - c),
                      device_id_type=MESH_ID)
            reg = functools.partial(_shard_region, outs[a], kinds[a], k=2 * px + py, rl=rl, cl=cl, nr=rh, c0=0, nc=cl)
            forwards.append(pltpu.make_async_remote_copy(src_ref=reg(r0=c * rh), dst_ref=reg(r0=c * rh), **to))
            arrivals.append(pltpu.make_async_remote_copy(src_ref=reg(r0=(1 - c) * rh), dst_ref=reg(r0=(1 - c) * rh), **to))
    return forwards, arrivals


def _gather_sibling(placed, kinds, name):
    n = len(placed)
    shard_shapes = _gather_shard_shapes(placed, kinds)

    def body(*refs):
        forwards, from_sibling = _gather_sibling_copies(refs[n:2 * n], kinds, shard_shapes, *refs[2 * n:])
        for cp in forwards:
            cp.start()
        for cp in from_sibling:
            cp.wait_recv()
        for cp in forwards:
            cp.wait_send()

    return pl.pallas_call(
        body, name=name,
        out_shape=[jax.ShapeDtypeStruct(p.shape, p.dtype) for p in placed],
        in_specs=[ANY] * n, out_specs=[ANY] * n, input_output_aliases={a: a for a in range(n)},
        scratch_shapes=[pltpu.SemaphoreType.DMA((n, 3))] * 2,
        compiler_params=_params(),
    )(*placed)


def _half(kind, r, c):
    return (r // 2, c) if kind == "col" else (r, c // 2)


def _half_of(ref, kind, which):
    r, c = ref.shape
    hr, hc = _half(kind, r, c)
    return ref.at[pl.ds(which * hr, hr), :] if kind == "col" else ref.at[:, pl.ds(which * hc, hc)]


def _rs_sibling(grads, kinds, name):
    n = len(grads)

    def body(*refs):
        srcs, dsts = refs[:n], refs[n:2 * n]
        send_sems, recv_sems = refs[2 * n:]
        x, y, c = _mesh_pos()
        cps = [pltpu.make_async_remote_copy(src_ref=_half_of(srcs[a], kinds[a], 1 - c), dst_ref=dsts[a],
                                            send_sem=send_sems.at[a], recv_sem=recv_sems.at[a],
                                            device_id=(x, y, 1 - c), device_id_type=MESH_ID) for a in range(n)]
        for cp in cps:
            cp.start()
        for cp in cps:
            cp.wait()

    return pl.pallas_call(
        body, name=name,
        out_shape=[jax.ShapeDtypeStruct(_half(k, *g.shape), g.dtype) for g, k in zip(grads, kinds)],
        in_specs=[ANY] * n, out_specs=[ANY] * n,
        scratch_shapes=[pltpu.SemaphoreType.DMA((n,)), pltpu.SemaphoreType.DMA((n,))],
        compiler_params=_params(),
    )(*grads)


def _pair_sum(gf, rv, kind, ci, name):
    r, c = rv.shape
    tr = _pick(r, (128, 64, 32, 16, 8))
    nt = r // tr

    def body(ci_ref, g_ref, r_ref, o_ref):
        o_ref[...] = (g_ref[...] + r_ref[...]).astype(BF16)

    g_map = (lambda i, ci_ref: (ci_ref[0] * nt + i, 0)) if kind == "col" else (lambda i, ci_ref: (i, ci_ref[0]))
    blk = pl.BlockSpec((tr, c), lambda i, ci_ref: (i, 0))
    return pl.pallas_call(
        body, name=name,
        grid_spec=pltpu.PrefetchScalarGridSpec(num_scalar_prefetch=1, grid=(nt,),
                                               in_specs=[pl.BlockSpec((tr, c), g_map), blk], out_specs=blk),
        out_shape=jax.ShapeDtypeStruct((r, c), BF16),
        compiler_params=_params(("parallel",)),
    )(ci.reshape(1), gf, rv)


def _rs_slot_shapes(pairs, kinds):
    return [(p.shape[0], p.shape[1] // 4) if k == "col" else (p.shape[0] // 4, p.shape[1]) for p, k in zip(pairs, kinds)]


def _rs_chip_copies(srcs, dsts, kinds, shapes, send_sems, recv_sems, with_arrivals=True):
    x, y, c = _mesh_pos()
    me = 2 * x + y
    sends, arrivals = [], []
    for a in range(len(srcs)):
        rl, cl = shapes[a]
        reg = functools.partial(_shard_region, srcs[a], kinds[a], rl=rl, cl=cl, r0=0, nr=rl, c0=0, nc=cl)
        for j, (px, py) in enumerate(PEER_CHIPS(x, y)):
            to = dict(send_sem=send_sems.at[a, j], recv_sem=recv_sems.at[a, j], device_id=(px, py, c),
                      device_id_type=MESH_ID)
            sends.append(pltpu.make_async_remote_copy(src_ref=reg(k=2 * px + py), dst_ref=dsts[a].at[me], **to))
            if with_arrivals:
                slot = dsts[a].at[2 * px + py]
                arrivals.append(pltpu.make_async_remote_copy(src_ref=slot, dst_ref=slot, **to))
    return sends, arrivals


def _rs_chips(pairs, kinds):
    n = len(pairs)
    shapes = _rs_slot_shapes(pairs, kinds)

    def body(*refs):
        sends, arrivals = _rs_chip_copies(refs[:n], refs[n:2 * n], kinds, shapes, *refs[2 * n:])
        for cp in sends:
            cp.start()
        for cp in arrivals:
            cp.wait_recv()
        for cp in sends:
            cp.wait_send()

    return pl.pallas_call(
        body, name="rs_chips",
        out_shape=[jax.ShapeDtypeStruct((4,) + s, p.dtype) for s, p in zip(shapes, pairs)],
        in_specs=[ANY] * n, out_specs=[ANY] * n,
        scratch_shapes=[pltpu.SemaphoreType.DMA((n, 3)), pltpu.SemaphoreType.DMA((n, 3))],
        compiler_params=_params(),
    )(*pairs)


def _sum_chips(pair, got, kind, pos, name):
    _, r, c = got.shape
    tr = _pick(r, (256, 128, 64, 32, 16))
    nt = r // tr

    def body(pos_ref, own_ref, g1_ref, g2_ref, g3_ref, o_ref):
        o_ref[...] = ((own_ref[...].astype(F32) + g1_ref[0].astype(F32)) + g2_ref[0].astype(F32)) + g3_ref[0].astype(F32)

    if kind == "col":
        own_map = lambda i, p: (i, p[1])
        out_map = lambda i, p: (p[0] * nt + i, 0)
        out_shape = (2 * r, c)
    else:
        own_map = lambda i, p: (p[1] * nt + i, 0)
        out_map = lambda i, p: (i, p[0])
        out_shape = (r, 2 * c)
    peer = lambda m: pl.BlockSpec((1, tr, c), lambda i, p: (p[1] ^ m, i, 0))
    return pl.pallas_call(
        body, name=name,
        grid_spec=pltpu.PrefetchScalarGridSpec(
            num_scalar_prefetch=1, grid=(nt,),
            in_specs=[pl.BlockSpec((tr, c), own_map), peer(1), peer(2), peer(3)],
            out_specs=pl.BlockSpec((tr, c), out_map)),
        out_shape=jax.ShapeDtypeStruct(out_shape, F32),
        compiler_params=_params(("parallel",)),
    )(pos, pair, got, got, got)


def _rs_back(halves, kinds):
    n = len(halves)

    def body(*refs):
        outs = refs[n:2 * n]
        send_sems, recv_sems = refs[2 * n:]
        x, y, c = _mesh_pos()
        cps = []
        for a in range(n):
            mine = _half_of(outs[a], kinds[a], c)
            cps.append(pltpu.make_async_remote_copy(src_ref=mine, dst_ref=mine, send_sem=send_sems.at[a],
                                                    recv_sem=recv_sems.at[a], device_id=(x, y, 1 - c),
                                                    device_id_type=MESH_ID))
            cps[-1].start()
        for a in range(n):
            other = _half_of(outs[a], kinds[a], 1 - c)
            pltpu.make_async_remote_copy(src_ref=other, dst_ref=other, send_sem=send_sems.at[a],
                                         recv_sem=recv_sems.at[a], device_id=(x, y, 1 - c),
                                         device_id_type=MESH_ID).wait_recv()
        for cp in cps:
            cp.wait_send()

    return pl.pallas_call(
        body, name="rs_back",
        out_shape=[jax.ShapeDtypeStruct(h.shape, h.dtype) for h in halves],
        in_specs=[ANY] * n, out_specs=[ANY] * n, input_output_aliases={a: a for a in range(n)},
        scratch_shapes=[pltpu.SemaphoreType.DMA((n,)), pltpu.SemaphoreType.DMA((n,))],
        compiler_params=_params(),
    )(*halves)


def _rope_tables(l, lc):
    rows = l // GRID_W
    n_freq = RET_DH // 4
    inv_freq = ROPE_THETA ** (-jnp.arange(n_freq, dtype=F32) / n_freq)
    sign = jnp.tile(jnp.array([-1.0, 1.0], F32), n_freq)

    def half(n):
        ang = jnp.repeat(jnp.arange(n, dtype=F32)[:, None] * inv_freq, 2, axis=-1)
        return jnp.cos(ang), jnp.sin(ang) * sign

    (cr, sr), (cc, sc) = half(rows), half(GRID_W)
    grid = lambda r, c: jnp.concatenate([jnp.repeat(r, GRID_W, axis=0), jnp.tile(c, (rows, 1))], axis=-1)
    cos_t = jnp.concatenate([jnp.ones((lc, RET_DH), F32), grid(cr, cc)], axis=0)
    sin_t = jnp.concatenate([jnp.zeros((lc, RET_DH), F32), grid(sr, sc)], axis=0)
    return cos_t, sin_t


def _s5_pack(a):
    blk = lambda t: t.reshape(1, S5_NB, 128, S5_STATE)
    lre = jnp.stack([a["s5_lambda_re_f"][0], a["s5_lambda_re_b"][0]]).reshape(2, S5_NB, 8, S5_STATE)
    lim = jnp.stack([a["s5_lambda_im_f"][0], a["s5_lambda_im_b"][0]]).reshape(2, S5_NB, 8, S5_STATE)
    lst = jnp.stack([a["s5_log_step_f"][0], a["s5_log_step_b"][0]]).reshape(2, S5_NB, 8, 1)
    b_re = blk(a["s5_b_re"][0].transpose(0, 2, 1))
    b_im = blk(a["s5_b_im"][0].transpose(0, 2, 1))
    return (lre, lim, lst, b_re, b_im, blk(a["s5_c_re"][0]), blk(a["s5_c_im"][0]),
            a["s5_d"].reshape(1, S5_NB, 1, 128))


def _s5_unpack(g):
    glre, glim, glst, gbre, gbim, gcre, gcim, gd = g
    unb = lambda t: t.reshape(S5_GROUPS, S5_GROUP, S5_STATE).transpose(0, 2, 1)[None]
    return {
        "s5_lambda_re_f": glre[0].reshape(1, S5_GROUPS, S5_STATE), "s5_lambda_re_b": glre[1].reshape(1, S5_GROUPS, S5_STATE),
        "s5_lambda_im_f": glim[0].reshape(1, S5_GROUPS, S5_STATE), "s5_lambda_im_b": glim[1].reshape(1, S5_GROUPS, S5_STATE),
        "s5_log_step_f": glst[0].reshape(1, S5_GROUPS), "s5_log_step_b": glst[1].reshape(1, S5_GROUPS),
        "s5_b_re": unb(gbre), "s5_b_im": unb(gbim),
        "s5_c_re": gcre.reshape(1, S5_GROUPS, S5_GROUP, S5_STATE), "s5_c_im": gcim.reshape(1, S5_GROUPS, S5_GROUP, S5_STATE),
        "s5_d": gd.reshape(1, S5_WIDTH),
    }


def _local_step(a, early, late, mx, mc, conv_w, ci):
    x, ctx, tgt = a["x"][0], a["ctx"][0], a["loss_target"][0]
    l, lc = x.shape[0], ctx.shape[0]
    la = l + lc
    nct, ncc, nrc, cn = lc // TOK_TILE, lc // S5_T, lc // RET_CHUNK, la // S5_T
    n1w, n2w, fnw = a["norm1_w"], a["norm2_w"], a["final_norm_w"].reshape(1, D_MODEL)
    conv_b, b_glu = a["conv_b"], a["s5_b_glu"]
    ld2 = jnp.concatenate([a["ret_log_decay_f"], a["ret_log_decay_b"]], axis=0)
    mod4 = jnp.concatenate([mc[0:2], mx[0:2]], axis=0)
    mod3 = mx[2:5]
    gate5 = mx[5:6]
    cos_t, sin_t = _rope_tables(l, lc)
    s5p = _s5_pack(a)

    gg, xw, yw, a16, *early = _s5_gen(*s5p, early, EARLY_KINDS)
    wb = dict(zip(EARLY_NAMES, _gather_sibling(early, EARLY_KINDS, "gather_sibling_early")))
    p_all, h1b, p3, w_up_p = _norm_inproj(x, ctx, n1w, mod4, wb["w_in"], cos_t, sin_t, [late[1]], (LATE_KINDS[1],))
    sloc = _s5_to_states(p3, xw, "s5_state")
    a16s = a16.transpose(1, 0, 2, 3).reshape(2, S5_GROUPS, 128)
    hs, w_out_p, w_down_p = _s5_scan(sloc, a16s, ncc, [late[0], late[2]], (LATE_KINDS[0], LATE_KINDS[2]))
    y_all = _s5_from_states(p3, gg, hs, yw, False, "s5_out").reshape(la, S5_WIDTH)
    tabs = _ret_tables(ld2)
    of, ob, ssf, ssb = _ret_scan(p_all, tabs, nrc)
    wb = {**wb, **dict(zip(LATE_NAMES, _gather_sibling([w_out_p, w_up_p, w_down_p], LATE_KINDS, "gather_sibling_late")))}
    x1, mix, h2b, up, mixb, y_ret = _outproj_up(x, y_all, of, ob, p_all, wb["s5_w_glu"], b_glu, wb["w_out"],
                                                     mod3, n2w, wb["w_up"], nct)
    act, dx2, ddn, dact, acc_f = _ffn_loss(up, x1, conv_w, conv_b, wb["w_down"], gate5, fnw, tgt)

    g = {}
    g["w_down"] = _mm_tn(act, ddn, name="gw_down")
    dup, acc_c = _convglu_bwd(up, dact, conv_w, conv_b)
    g["w_up"] = _mm_tn(h2b, dup, name="gw_up")
    first = [g[n] for n in FIRST_GRADS]
    first_pairs = [_pair_sum(gf, rv, k, ci, "rs_pair_" + n)
                   for gf, rv, k, n in zip(first, _rs_sibling(first, FIRST_KINDS, "rs_sibling_first"), FIRST_KINDS, FIRST_GRADS)]
    dx1, dmixb, acc_2, dy3, dy_ret, dg, g["s5_w_glu"], g["s5_b_glu"], *first_got = _up_bwd(
        dup, wb["w_up"], wb["w_out"], x1, dx2, mix, mod3, n2w, y_all, y_ret, p_all, wb["s5_w_glu"], b_glu,
        jnp.zeros(p3.shape, BF16), nct, first_pairs, FIRST_KINDS)
    g["w_out"] = _mm_tn(mixb, dmixb, name="gw_out")

    e = _s5_to_states(dy3, yw, "s5_bwd_h")
    ds, da16 = _s5_scan_bwd(e, hs, a16s, ncc)
    du = _s5_from_states(dy3, gg, ds, xw, True, "s5_bwd_u").reshape(la, S5_WIDTH)
    dkb = _s5_bwd_kb(p3, dy3)
    dwst = _s5_bwd_w(p3, ds, "s5_bwd_wst")
    dwout = _s5_bwd_w(dy3, hs, "s5_bwd_wout")
    da16p = da16.reshape(2, S5_NB, 8, 128).transpose(1, 0, 2, 3)
    g.update(_s5_unpack(_s5_gen_bwd(*s5p, dkb, dwst, dwout, da16p)))

    dqf, dkf, dvf, dqb, dkb_, dvb, dld = _ret_scan_bwd(p_all, tabs, ssf, ssb, dy_ret, nrc)
    g["ret_log_decay_f"] = dld[0, :, 0, 0].reshape(1, RET_HEADS)
    g["ret_log_decay_b"] = dld[1, :, 0, 0].reshape(1, RET_HEADS)
    dp, grad_x, acc_1 = _in_bwd(dqf, dkf, dvf, dqb, dkb_, dvb, du, dg, cos_t, sin_t, wb["w_in"], x, ctx, n1w, mod4, dx1)
    g["norm1_w"], g["norm2_w"], g["final_norm_w"] = acc_1[0:1], acc_2[0:1], acc_f[0]
    g["conv_w"], g["conv_b"] = acc_c[0:3], acc_c[3:4]
    zero = jnp.zeros((1, D_MODEL), F32)
    dmx = jnp.concatenate([acc_1[3:5], acc_2[1:2], acc_2[2:4], acc_f[1:2]], axis=0)
    dmc = jnp.concatenate([acc_1[1:3], zero, zero, zero, zero], axis=0)
    dm_pair = jnp.concatenate([dmx.reshape(1, -1), dmc.reshape(1, -1), jnp.zeros((6, 6 * D_MODEL), F32)], axis=0)
    g["w_in"], dm_all = _mm_tn(h1b, dp, name="gw_in", gathered=dm_pair)
    return acc_f[2, 0], grad_x, g, dm_all, first_pairs, first_got


WEIGHT_NAMES = ("c_ctx", "w_mod", "b_mod", "norm1_w", "w_in", "s5_lambda_re_f", "s5_lambda_im_f", "s5_log_step_f",
                "s5_lambda_re_b", "s5_lambda_im_b", "s5_log_step_b", "s5_b_re", "s5_b_im", "s5_c_re", "s5_c_im",
                "s5_d", "s5_w_glu", "s5_b_glu", "ret_log_decay_f", "ret_log_decay_b", "w_out", "norm2_w", "w_up",
                "conv_w", "conv_b", "w_down", "final_norm_w")
BIG_NAMES = ("w_in", "w_out", "w_up", "w_down", "s5_w_glu")
BIG_KINDS = ("col", "row", "col", "row", "row")
EARLY_NAMES, EARLY_KINDS = ("w_in", "s5_w_glu"), ("col", "row")
LATE_NAMES, LATE_KINDS = ("w_out", "w_up", "w_down"), ("row", "col", "row")
FIRST_GRADS, FIRST_KINDS = ("w_down", "w_up"), ("row", "col")
LAST_GRADS, LAST_KINDS = ("w_in", "w_out", "s5_w_glu"), ("col", "row", "row")
SMALL_NAMES = ("norm1_w", "norm2_w", "final_norm_w", "conv_b", "conv_w", "s5_lambda_re_f", "s5_lambda_im_f",
               "s5_log_step_f", "s5_lambda_re_b", "s5_lambda_im_b", "s5_log_step_b", "s5_b_re", "s5_b_im", "s5_c_re",
               "s5_c_im", "s5_d", "s5_b_glu", "ret_log_decay_f", "ret_log_decay_b")
ROW = 1024
N_CHIPS = 4


def _pack_rows(parts):
    flat = jnp.concatenate([p.reshape(-1) for p in parts])
    n = flat.shape[0]
    rows = -(-n // (8 * ROW)) * 8
    return jnp.pad(flat, (0, rows * ROW - n)).reshape(rows, ROW)


def _unpack_rows(packed, shapes):
    flat = packed.reshape(-1)
    out, off = [], 0
    for s in shapes:
        n = math.prod(s)
        out.append(flat[off:off + n].reshape(s))
        off += n
    return out


def _step(a):
    xi, yi, ci = _mesh_pos()
    chip = 2 * xi + yi
    dev = 2 * chip + ci

    cw_loc = a["conv_w"].reshape(-1)
    small_in = jnp.concatenate([a["c"].reshape(-1), jnp.pad(cw_loc, (0, 24 * 128 - cw_loc.shape[0]))]).reshape(32, 128)
    sg = _all_gather8(small_in, "gather_cond").reshape(8, 32, 128)
    c_all = sg[:, 0:8].reshape(8, D_MODEL)
    conv_w = sg[0::2, 8:32].reshape(N_CHIPS, -1)[:, :cw_loc.shape[0]].reshape(N_CHIPS, 3, -1)
    conv_w = conv_w.transpose(1, 0, 2).reshape(3, D_FF)

    placed = {n: _place_shard(a[n][0], k, chip, "place_" + n) for n, k in zip(BIG_NAMES, BIG_KINDS)}
    early = [placed[n] for n in EARLY_NAMES]
    late = [placed[n] for n in LATE_NAMES]

    w_mod_b = a["w_mod"][0].astype(BF16)
    c_ctx = a["c_ctx"].reshape(1, D_MODEL)
    b_loc = lax.dynamic_slice_in_dim(a["b_mod"], chip * MOD_COLS, MOD_COLS, 1)
    m_loc, s_b = _mod_fwd(c_all, c_ctx, w_mod_b, b_loc)
    mg = _all_gather8(m_loc, "gather_mod").reshape(8, MOD_ROWS, MOD_COLS)
    m_full = mg[0::2].transpose(1, 0, 2).reshape(MOD_ROWS, 6 * D_MODEL)
    mx = lax.dynamic_slice_in_dim(m_full, dev, 1, 0).reshape(6, D_MODEL)
    mc = m_full[8].reshape(6, D_MODEL)

    loss_part, grad_x, g, dm_all, first_pairs, first_got = _local_step(a, early, late, mx, mc, conv_w, ci)
    loss = lax.psum(loss_part, ("x", "y", "c"))

    dm16, gb_mod = _mod_bwd_sum(dm_all.reshape(8, 8, 6 * D_MODEL))
    dm_loc = lax.dynamic_slice_in_dim(dm16, chip * MOD_COLS, MOD_COLS, 1)
    gw_mod, gcc = _mod_bwd_w(dm_loc, s_b, c_ctx, w_mod_b)

    small_parts = [g[n] for n in SMALL_NAMES] + [gcc[0]]
    small_shapes = [p.shape for p in small_parts]
    sp = _pack_rows(small_parts)
    tot = _sum_slots(_all_gather8(sp, "gather_small_grads").reshape(8, sp.shape[0], ROW), "sum_small_grads")
    small = dict(zip(SMALL_NAMES + ("c_ctx",), _unpack_rows(tot, small_shapes)))
    grads = {n: small[n].reshape(a[n].shape) for n in SMALL_NAMES if n != "conv_w"}
    grads["c_ctx"] = (0.5 * small["c_ctx"]).reshape(a["c_ctx"].shape)
    grads["conv_w"] = lax.dynamic_slice_in_dim(small["conv_w"], chip * (D_FF // N_CHIPS), D_FF // N_CHIPS, 1)[None]
    grads["b_mod"] = gb_mod
    grads["w_mod"] = gw_mod[None]

    last = [g[n] for n in LAST_GRADS]
    last_pairs = [_pair_sum(gf, rv, k, ci, "rs_pair_" + n)
                  for gf, rv, k, n in zip(last, _rs_sibling(last, LAST_KINDS, "rs_sibling_last"), LAST_KINDS, LAST_GRADS)]
    last_got = _rs_chips(last_pairs, LAST_KINDS)
    pos = jnp.stack([ci, chip])
    order = FIRST_GRADS + LAST_GRADS
    order_kinds = FIRST_KINDS + LAST_KINDS
    halves = [_sum_chips(p, t, k, pos, "rs_sum_" + n)
              for p, t, k, n in zip(first_pairs + last_pairs, list(first_got) + list(last_got), order_kinds, order)]
    for n, t in zip(order, _rs_back(halves, order_kinds)):
        grads[n] = t[None]

    delta, new_m, new_v = {}, {}, {}
    for n in BIG_NAMES + ("w_mod",):
        for dst, t in zip((delta, new_m, new_v), _adamw(a[n][0], grads[n][0], a["m_" + n][0], a["v_" + n][0], "adamw_" + n)):
            dst[n] = t[None]
    rest = [n for n in WEIGHT_NAMES if n not in BIG_NAMES and n != "w_mod"]
    shapes = [a[n].shape for n in rest]
    pr = lambda pre: _pack_rows([a[pre + n] for n in rest])
    for dst, t in zip((delta, new_m, new_v),
                      _adamw(pr(""), _pack_rows([grads[n] for n in rest]), pr("m_"), pr("v_"), "adamw_small")):
        dst.update(zip(rest, _unpack_rows(t, shapes)))

    return (loss, grad_x[None], *[grads[n] for n in WEIGHT_NAMES], *[delta[n] for n in WEIGHT_NAMES],
            *[new_m[n] for n in WEIGHT_NAMES], *[new_v[n] for n in WEIGHT_NAMES])


def kernel(x, c, ctx, c_ctx, w_mod, b_mod, norm1_w, w_in, s5_lambda_re_f, s5_lambda_im_f, s5_log_step_f, s5_lambda_re_b, s5_lambda_im_b, s5_log_step_b, s5_b_re, s5_b_im, s5_c_re, s5_c_im, s5_d, s5_w_glu, s5_b_glu, ret_log_decay_f, ret_log_decay_b, w_out, norm2_w, w_up, conv_w, conv_b, w_down, final_norm_w, loss_target, m_c_ctx, m_w_mod, m_b_mod, m_norm1_w, m_w_in, m_s5_lambda_re_f, m_s5_lambda_im_f, m_s5_log_step_f, m_s5_lambda_re_b, m_s5_lambda_im_b, m_s5_log_step_b, m_s5_b_re, m_s5_b_im, m_s5_c_re, m_s5_c_im, m_s5_d, m_s5_w_glu, m_s5_b_glu, m_ret_log_decay_f, m_ret_log_decay_b, m_w_out, m_norm2_w, m_w_up, m_conv_w, m_conv_b, m_w_down, m_final_norm_w, v_c_ctx, v_w_mod, v_b_mod, v_norm1_w, v_w_in, v_s5_lambda_re_f, v_s5_lambda_im_f, v_s5_log_step_f, v_s5_lambda_re_b, v_s5_lambda_im_b, v_s5_log_step_b, v_s5_b_re, v_s5_b_im, v_s5_c_re, v_s5_c_im, v_s5_d, v_s5_w_glu, v_s5_b_glu, v_ret_log_decay_f, v_ret_log_decay_b, v_w_out, v_norm2_w, v_w_up, v_conv_w, v_conv_b, v_w_down, v_final_norm_w):
    return _step(dict(locals()))
```

```python
import functools
import math

import jax
import jax.numpy as jnp
from jax import lax
from jax.experimental import pallas as pl
from jax.experimental.pallas import tpu as pltpu

F32 = jnp.float32
BF16 = jnp.bfloat16

D_MODEL = 1024
S5_WIDTH = 512
S5_GROUPS = 32
S5_GROUP = 16
S5_STATE = 64
RET_WIDTH = 512
RET_HEADS = 4
RET_DH = 128
RET_CHUNK = 256
GRID_W = 64
ROPE_THETA = 10000.0
D_FF = 2816
NORM_EPS = 1e-6
IN_COLS = S5_WIDTH + 4 * RET_WIDTH

S5_T = 16
S5_NB = 4
S5_BW = S5_T * 128
S5_SW = 8 * 2 * S5_STATE

ADAM_LR, ADAM_B1, ADAM_B2, ADAM_EPS, ADAM_WD, ADAM_STEP = 0.001, 0.9, 0.999, 1e-08, 0.01, 10

VMEM_LIMIT = 56 * 1024 * 1024
MM_TN_VMEM = 40 * 1024 * 1024
MESH_ID = pl.DeviceIdType.MESH


def _params(sem=None):
    return pltpu.CompilerParams(dimension_semantics=sem, vmem_limit_bytes=VMEM_LIMIT)


def _full(shape):
    n = len(shape)
    return pl.BlockSpec(shape, lambda *_: (0,) * n)


def _dot(a, b):
    return jnp.dot(a, b, preferred_element_type=F32)


def _dot_nt(a, b):
    return lax.dot_general(a, b, (((1,), (1,)), ((), ())), preferred_element_type=F32)


def _dot_tn(a, b):
    return lax.dot_general(a, b, (((0,), (0,)), ((), ())), preferred_element_type=F32)


def _dot_hi(a, b):
    return jnp.dot(a, b, preferred_element_type=F32, precision=lax.Precision.HIGHEST)


def _dot_nt_hi(a, b):
    return lax.dot_general(a, b, (((1,), (1,)), ((), ())), preferred_element_type=F32,
                           precision=lax.Precision.HIGHEST)


def _gelu(x):
    return 0.5 * x * (1.0 + jnp.tanh(0.7978845608028654 * (x + 0.044715 * (x * x * x))))


def _gelu_and_grad(x):
    c, ca = 0.7978845608028654, 0.7978845608028654 * 0.044715
    x2 = x * x
    t = jnp.tanh(x * (c + ca * x2))
    h = 0.5 * x
    return h + h * t, 0.5 + 0.5 * t + h * (1.0 - t * t) * (c + 3.0 * ca * x2)


def _sigmoid(x):
    return 1.0 / (1.0 + jnp.exp(-x))


def _silu(x):
    return x * _sigmoid(x)


def _rms_mod(x, nw, sh, sc):
    r = lax.rsqrt(jnp.mean(x * x, axis=-1, keepdims=True) + NORM_EPS)
    return (x * r * nw) * (1.0 + sc) + sh


def _rms(x, nw):
    r = lax.rsqrt(jnp.mean(x * x, axis=-1, keepdims=True) + NORM_EPS)
    return x * r * nw


def _head_norm_gate(y, g):
    mu = jnp.mean(y, axis=-1, keepdims=True)
    yc = y - mu
    var = jnp.mean(yc * yc, axis=-1, keepdims=True)
    return _silu(g) * (yc * lax.rsqrt(var + NORM_EPS))


def _swap_pairs(t):
    lane = lax.broadcasted_iota(jnp.int32, t.shape, 1)
    return jnp.where(lane % 2 == 0, pltpu.roll(t, RET_DH - 1, 1), pltpu.roll(t, 1, 1))


def _rope(t, cos_t, sin_t):
    return t * cos_t + _swap_pairs(t) * sin_t


def _rope_t(dt, cos_t, sin_t):
    return dt * cos_t + _swap_pairs(dt * sin_t)


def _pick(n, prefs):
    for p in prefs:
        if n % p == 0:
            return p
    return n


def _mm_tn(a, b, *, name, gathered=None):
    m, k = a.shape
    n = b.shape[1]
    tn = _pick(n, (1408, 1024, 1280, 512))
    fits = lambda t: 2 * (2 * t * k + 2 * t * tn + 4 * k * tn) <= MM_TN_VMEM
    tm = _pick(m, [t for t in (2816, 2048, 1024, 768, 512, 256) if fits(t)] + [128])
    nj, ni = n // tn, m // tm

    def product(a_ref, b_ref, o_ref):
        @pl.when(pl.program_id(1) == 0)
        def _():
            o_ref[...] = jnp.zeros_like(o_ref)
        o_ref[...] += _dot_tn(a_ref[...], b_ref[...])

    specs = dict(
        grid=(nj, ni),
        in_specs=[pl.BlockSpec((tm, k), lambda j, i: (i, 0)), pl.BlockSpec((tm, tn), lambda j, i: (i, j))],
        out_specs=pl.BlockSpec((k, tn), lambda j, i: (0, j)),
        out_shape=jax.ShapeDtypeStruct((k, n), F32))
    if gathered is None:
        def body(a_ref, b_ref, o_ref):
            product(a_ref, b_ref, o_ref)

        return pl.pallas_call(body, name=name, compiler_params=_params(("parallel", "arbitrary")), **specs)(a, b)

    ng = len(gathered)

    def body_gather(a_ref, b_ref, *rest):
        v_refs, o_ref, all_refs, sems = rest[:ng], rest[ng], rest[ng + 1:2 * ng + 1], rest[2 * ng + 1:]
        phases = [_gather8_phases(v_refs[q], all_refs[q], *sems[3 * q:3 * q + 3], gathered[q].shape[0])
                  for q in range(ng)]
        step = pl.program_id(0) * ni + pl.program_id(1)

        @pl.when(step == 0)
        def _():
            for begin, _ in phases:
                begin()

        product(a_ref, b_ref, o_ref)

        @pl.when(step == nj * ni - 1)
        def _():
            for _, finish in phases:
                finish()

    specs["in_specs"] = specs["in_specs"] + [ANY] * ng
    specs["out_specs"] = [specs["out_specs"]] + [ANY] * ng
    specs["out_shape"] = [specs["out_shape"]] + [jax.ShapeDtypeStruct((8 * v.shape[0], v.shape[1]), v.dtype)
                                                 for v in gathered]
    return pl.pallas_call(body_gather, name=name, scratch_shapes=list(GATHER8_SCRATCH) * ng,
                          compiler_params=_params(("arbitrary", "arbitrary")), **specs)(a, b, *gathered)


TOK_TILE = 256


def _behind(step, last, copies):
    @pl.when(step == 0)
    def _():
        for cp in copies(with_arrivals=False)[0]:
            cp.start()

    @pl.when(step == last)
    def _():
        sends, arrivals = copies()
        for cp in arrivals:
            cp.wait_recv()
        for cp in sends:
            cp.wait_send()


def _s5_put_rows(rows_ref, scr, val):
    nchunk = scr.shape[0]
    for c in range(nchunk):
        scr[c] = val[S5_T * c:S5_T * (c + 1), :]
    for b in range(S5_NB):
        for j in range(S5_T):
            rows_ref[b, :, 128 * j:128 * (j + 1)] = scr[:, j, 128 * b:128 * (b + 1)].astype(BF16)


def _norm_inproj(x, ctx, n1w, mod4, w_in_b, cos_t, sin_t, placed, kinds):
    l, lc = x.shape[0], ctx.shape[0]
    tm = TOK_TILE
    nct = lc // tm
    la = l + lc
    n = len(placed)
    shard_shapes = _gather_shard_shapes(placed, kinds)

    def body(x_ref, c_ref, nw_ref, mod_ref, w_ref, cos_ref, sin_ref, *rest):
        p_ref, h_ref, u_ref = rest[n:n + 3]
        send_sems, recv_sems, u_scr = rest[2 * n + 3:]
        _behind(pl.program_id(0), la // tm - 1,
                functools.partial(_gather_chip_copies, rest[n + 3:2 * n + 3], kinds, shard_shapes, send_sems, recv_sems))
        is_ctx = pl.program_id(0) < nct
        xt = jnp.where(is_ctx, c_ref[...], x_ref[...])
        sh = jnp.where(is_ctx, mod_ref[0:1, :], mod_ref[2:3, :])
        sc = jnp.where(is_ctx, mod_ref[1:2, :], mod_ref[3:4, :])
        hb = _rms_mod(xt, nw_ref[...], sh, sc).astype(BF16)
        h_ref[...] = hb
        p = _dot(hb, w_ref[...])
        p_ref[...] = p
        cs, sn = cos_ref[...], sin_ref[...]
        for h in range(RET_HEADS):
            q_cols = slice(RET_WIDTH + RET_DH * h, RET_WIDTH + RET_DH * (h + 1))
            k_cols = slice(2 * RET_WIDTH + RET_DH * h, 2 * RET_WIDTH + RET_DH * (h + 1))
            p_ref[:, q_cols] = _rope(p[:, q_cols], cs, sn)
            p_ref[:, k_cols] = _rope(p[:, k_cols] * K_SCALE, cs, sn)
        _s5_put_rows(u_ref, u_scr, p[:, 0:S5_WIDTH])

    return pl.pallas_call(
        body, name="norm_inproj", grid=(la // tm,),
        in_specs=[pl.BlockSpec((tm, D_MODEL), lambda i: (jnp.maximum(i - nct, 0), 0)),
                  pl.BlockSpec((tm, D_MODEL), lambda i: (jnp.minimum(i, nct - 1), 0)),
                  _full((1, D_MODEL)), _full((4, D_MODEL)), _full((D_MODEL, IN_COLS)),
                  pl.BlockSpec((tm, RET_DH), lambda i: (i, 0)), pl.BlockSpec((tm, RET_DH), lambda i: (i, 0))] + [ANY] * n,
        out_specs=[pl.BlockSpec((tm, IN_COLS), lambda i: (i, 0)), pl.BlockSpec((tm, D_MODEL), lambda i: (i, 0)),
                   pl.BlockSpec((S5_NB, tm // S5_T, S5_BW), lambda i: (0, i, 0))] + [ANY] * n,
        out_shape=[jax.ShapeDtypeStruct((la, IN_COLS), F32), jax.ShapeDtypeStruct((la, D_MODEL), BF16),
                   jax.ShapeDtypeStruct((S5_NB, la // S5_T, S5_BW), BF16)]
        + [jax.ShapeDtypeStruct(p.shape, p.dtype) for p in placed],
        input_output_aliases={7 + a: 3 + a for a in range(n)},
        scratch_shapes=[pltpu.SemaphoreType.DMA((n, 3)), pltpu.SemaphoreType.DMA((n, 3)),
                        pltpu.VMEM((tm // S5_T, S5_T, S5_WIDTH), F32)],
        compiler_params=_params(("arbitrary",)),
    )(x, ctx, n1w, mod4, w_in_b, cos_t, sin_t, *placed)


def _iota2(shape, dim):
    return lax.broadcasted_iota(jnp.int32, shape, dim)


def _group_mask(rows, cols, row_div, col_div):
    return jnp.where(_iota2((rows, cols), 0) // row_div == _iota2((rows, cols), 1) // col_div, 1.0, 0.0).astype(F32)


def _s5_gen_dir(lre, lim, lst, b_re, b_im, c_re, c_im):
    step = jnp.exp(lst)
    mag = jnp.exp(lre * step)
    ar = mag * jnp.cos(lim * step)
    ai = mag * jnp.sin(lim * step)
    den = lre * lre + lim * lim
    xr = ar - 1.0
    cr = (xr * lre + ai * lim) / den
    ci = (ai * lre - xr * lim) / den
    rexp = _group_mask(128, 8, S5_GROUP, 1)
    are, aie = _dot_hi(rexp, ar), _dot_hi(rexp, ai)
    cre, cie = _dot_hi(rexp, cr), _dot_hi(rexp, ci)
    bbr = cre * b_re - cie * b_im
    bbi = cre * b_im + cie * b_re
    gmask = _group_mask(128, 128, S5_GROUP, S5_GROUP)
    pr, pi = jnp.ones_like(are), jnp.zeros_like(are)
    xs, ys = [], []
    for t in range(S5_T + 1):
        if t < S5_T:
            xs.append(jnp.concatenate([bbr * pr - bbi * pi, bbr * pi + bbi * pr], axis=1))
        ys.append(jnp.concatenate([c_re * pr - c_im * pi, -(c_re * pi + c_im * pr)], axis=1))
        pr, pi = pr * are - pi * aie, pr * aie + pi * are
    gs = [_dot_nt_hi(x_t, ys[0]) * gmask for x_t in xs]
    r16, i16 = ar, ai
    for _ in range(4):
        r16, i16 = r16 * r16 - i16 * i16, 2.0 * r16 * i16
    return xs, ys, gs, jnp.concatenate([r16, i16], axis=1)


def _s5_expand(z):
    return jnp.concatenate([z] * 8, axis=1) * _group_mask(128, S5_SW, S5_GROUP, 128)


def _s5_contract(z):
    zm = z * _group_mask(128, S5_SW, S5_GROUP, 128)
    acc = zm[:, 0:128]
    for k in range(1, 8):
        acc = acc + zm[:, 128 * k:128 * (k + 1)]
    return acc


def _s5_param_specs():
    blk3 = lambda r, c: pl.BlockSpec((1, 1, r, c), lambda b, *_: (0, b, 0, 0))
    dir3 = lambda r, c: pl.BlockSpec((2, 1, r, c), lambda b, *_: (0, b, 0, 0))
    return [dir3(8, S5_STATE), dir3(8, S5_STATE), dir3(8, 1), blk3(128, S5_STATE), blk3(128, S5_STATE),
            blk3(128, S5_STATE), blk3(128, S5_STATE), blk3(1, 128)]


def _s5_gen(lre, lim, lst, b_re, b_im, c_re, c_im, dvec, placed, kinds):
    n = len(placed)
    shard_shapes = _gather_shard_shapes(placed, kinds)

    def body(lre_ref, lim_ref, lst_ref, bre_ref, bim_ref, cre_ref, cim_ref, d_ref, *rest):
        gg_ref, xw_ref, yw_ref, a16_ref = rest[n:n + 4]
        _behind(pl.program_id(0), S5_NB - 1,
                functools.partial(_gather_chip_copies, rest[n + 4:2 * n + 4], kinds, shard_shapes, *rest[2 * n + 4:]))
        eye = _group_mask(128, 128, 1, 1)
        g0 = eye * d_ref[0, 0]
        for dr in range(2):
            xs, ys, gs, a16 = _s5_gen_dir(lre_ref[dr, 0], lim_ref[dr, 0], lst_ref[dr, 0], bre_ref[0, 0],
                                          bim_ref[0, 0], cre_ref[0, 0], cim_ref[0, 0])
            a16_ref[0, dr] = a16
            for j in range(S5_T):
                xw_ref[0, dr, j] = xs[S5_T - 1 - j if dr == 0 else j]
                yw_ref[0, dr, j] = ys[j + 1 if dr == 0 else S5_T - j]
            g0 = g0 + gs[0]
            for t in range(1, S5_T):
                gg_ref[0, (S5_T - 1) + t if dr == 0 else (S5_T - 1) - t] = gs[t]
        gg_ref[0, S5_T - 1] = g0

    blk = pl.BlockSpec((1, 2, S5_T, 128, 128), lambda b: (b, 0, 0, 0, 0))
    return pl.pallas_call(
        body, name="s5_gen", grid=(S5_NB,),
        in_specs=_s5_param_specs() + [ANY] * n,
        out_specs=[pl.BlockSpec((1, 2 * S5_T - 1, 128, 128), lambda b: (b, 0, 0, 0)), blk, blk,
                   pl.BlockSpec((1, 2, 8, 128), lambda b: (b, 0, 0, 0))] + [ANY] * n,
        out_shape=[jax.ShapeDtypeStruct((S5_NB, 2 * S5_T - 1, 128, 128), F32),
                   jax.ShapeDtypeStruct((S5_NB, 2, S5_T, 128, 128), F32),
                   jax.ShapeDtypeStruct((S5_NB, 2, S5_T, 128, 128), F32),
                   jax.ShapeDtypeStruct((S5_NB, 2, 8, 128), F32)]
        + [jax.ShapeDtypeStruct(p.shape, p.dtype) for p in placed],
        input_output_aliases={8 + a: 4 + a for a in range(n)},
        scratch_shapes=[pltpu.SemaphoreType.DMA((n, 3)), pltpu.SemaphoreType.DMA((n, 3))],
        compiler_params=_params(("arbitrary",)),
    )(lre, lim, lst, b_re, b_im, c_re, c_im, dvec, *placed)


def _s5_fill_state_mat(w_scr, src_ref, dr):
    for j in range(S5_T):
        w_scr[128 * j:128 * (j + 1), :] = _s5_expand(src_ref[0, dr, j]).astype(BF16)


def _s5_fill_toeplitz(k_scr, gg_ref):
    for j in range(S5_T):
        for i in range(S5_T):
            k_scr[128 * j:128 * (j + 1), 128 * i:128 * (i + 1)] = gg_ref[0, i - j + (S5_T - 1)].astype(BF16)


S5_GEN_SPECS = [pl.BlockSpec((1, 2 * S5_T - 1, 128, 128), lambda b: (b, 0, 0, 0)),
                pl.BlockSpec((1, 2, S5_T, 128, 128), lambda b: (b, 0, 0, 0, 0))]


def _s5_gen_bwd(lre, lim, lst, b_re, b_im, c_re, c_im, dvec, dg, dx, dy, da16):
    def body(lre_ref, lim_ref, lst_ref, bre_ref, bim_ref, cre_ref, cim_ref, d_ref, dg_ref, dx_ref, dy_ref, da16_ref,
             glre_ref, glim_ref, glst_ref, gbre_ref, gbim_ref, gcre_ref, gcim_ref, gd_ref):
        eye = _group_mask(128, 128, 1, 1)
        gd_ref[0, 0] = jnp.sum(dg_ref[0, S5_T - 1] * eye, axis=0, keepdims=True)
        gb = [None, None, None, None]
        for dr in range(2):
            args = (lre_ref[dr, 0], lim_ref[dr, 0], lst_ref[dr, 0], bre_ref[0, 0], bim_ref[0, 0],
                    cre_ref[0, 0], cim_ref[0, 0])
            _, vjp = jax.vjp(_s5_gen_dir, *args)
            dxs = [dx_ref[0, dr, S5_T - 1 - t if dr == 0 else t] for t in range(S5_T)]
            dys = [jnp.zeros((128, 128), F32)] + [dy_ref[0, dr, t - 1 if dr == 0 else S5_T - t]
                                                  for t in range(1, S5_T + 1)]
            dgs = [dg_ref[0, (S5_T - 1) + t if dr == 0 else (S5_T - 1) - t] for t in range(S5_T)]
            g = vjp((dxs, dys, dgs, da16_ref[0, dr]))
            glre_ref[dr, 0] = g[0]
            glim_ref[dr, 0] = g[1]
            glst_ref[dr, 0] = g[2]
            for q in range(4):
                gb[q] = g[3 + q] if gb[q] is None else gb[q] + g[3 + q]
        gbre_ref[0, 0] = gb[0]
        gbim_ref[0, 0] = gb[1]
        gcre_ref[0, 0] = gb[2]
        gcim_ref[0, 0] = gb[3]

    shp = lambda a: jax.ShapeDtypeStruct(a.shape, F32)
    return pl.pallas_call(
        body, name="s5_gen_bwd", grid=(S5_NB,),
        in_specs=_s5_param_specs() + [
            pl.BlockSpec((1, 2 * S5_T - 1, 128, 128), lambda b: (b, 0, 0, 0)),
            pl.BlockSpec((1, 2, S5_T, 128, 128), lambda b: (b, 0, 0, 0, 0)),
            pl.BlockSpec((1, 2, S5_T, 128, 128), lambda b: (b, 0, 0, 0, 0)),
            pl.BlockSpec((1, 2, 8, 128), lambda b: (b, 0, 0, 0))],
        out_specs=_s5_param_specs(),
        out_shape=[shp(lre), shp(lim), shp(lst), shp(b_re), shp(b_im), shp(c_re), shp(c_im), shp(dvec)],
        compiler_params=_params(("parallel",)),
    )(lre, lim, lst, b_re, b_im, c_re, c_im, dvec, dg, dx, dy, da16)


def _s5_put_groups(o_ref, dr, val):
    for gi in range(8):
        o_ref[dr, :, gi, :] = val[:, 128 * gi:128 * (gi + 1)]


def _s5_get_groups(s_ref, dr, n=8):
    return jnp.concatenate([s_ref[dr, :, gi, :] for gi in range(n)], axis=1).astype(BF16)


def _s5_to_states(u3, blocks, name):
    cn = u3.shape[1]

    def body(u_ref, b_ref, o_ref, w_scr):
        u = u_ref[0]
        for dr in range(2):
            _s5_fill_state_mat(w_scr, b_ref, dr)
            _s5_put_groups(o_ref, dr, _dot(u, w_scr[...]))

    return pl.pallas_call(
        body, name=name, grid=(S5_NB,),
        in_specs=[pl.BlockSpec((1, cn, S5_BW), lambda b: (b, 0, 0)), S5_GEN_SPECS[1]],
        out_specs=pl.BlockSpec((2, cn, 8, 128), lambda b: (0, 0, b, 0)),
        out_shape=jax.ShapeDtypeStruct((2, cn, S5_GROUPS, 128), F32),
        scratch_shapes=[pltpu.VMEM((S5_BW, S5_SW), BF16)],
        compiler_params=_params(("parallel",)),
    )(u3, blocks)


def _s5_from_states(u3, gg, st, blocks, transposed, name):
    cn = u3.shape[1]

    def body(u_ref, g_ref, s_ref, b_ref, o_ref, k_scr, w_scr):
        u = u_ref[0]
        _s5_fill_toeplitz(k_scr, g_ref)
        y = _dot_nt(u, k_scr[...]) if transposed else _dot(u, k_scr[...])
        for dr in range(2):
            _s5_fill_state_mat(w_scr, b_ref, dr)
            y = y + _dot_nt(_s5_get_groups(s_ref, dr), w_scr[...])
        for i in range(S5_T):
            o_ref[:, i, :] = y[:, 128 * i:128 * (i + 1)]

    return pl.pallas_call(
        body, name=name, grid=(S5_NB,),
        in_specs=[pl.BlockSpec((1, cn, S5_BW), lambda b: (b, 0, 0)), S5_GEN_SPECS[0],
                  pl.BlockSpec((2, cn, 8, 128), lambda b: (0, 0, b, 0)), S5_GEN_SPECS[1]],
        out_specs=pl.BlockSpec((cn, S5_T, 128), lambda b: (0, 0, b)),
        out_shape=jax.ShapeDtypeStruct((cn, S5_T, S5_WIDTH), F32),
        scratch_shapes=[pltpu.VMEM((S5_BW, S5_BW), BF16), pltpu.VMEM((S5_BW, S5_SW), BF16)],
        compiler_params=_params(("parallel",)),
    )(u3, gg, st, blocks)


def _s5_a_forms(a):
    ra = pltpu.roll(a, S5_STATE, 1)
    low = _iota2(a.shape, 1) < S5_STATE
    return jnp.where(low, a, ra), jnp.where(low, -ra, a)


def _s5_scan(sloc, a16, ncc, placed, kinds):
    cn = sloc.shape[1]
    n = len(placed)
    shard_shapes = _gather_shard_shapes(placed, kinds)

    def body(s_ref, a_ref, *rest):
        h_ref = rest[n]
        sends, arrivals = _gather_chip_copies(rest[n + 1:2 * n + 1], kinds, shard_shapes, *rest[2 * n + 1:])
        for cp in sends:
            cp.start()
        forms = [_s5_a_forms(a_ref[dr]) for dr in range(2)]

        def step(s, hs):
            out = []
            for dr in range(2):
                arr, aii = forms[dr]
                h, rh = hs[dr]
                c = s if dr == 0 else jnp.where(s < ncc, ncc - 1 - s, cn - 1 - (s - ncc))
                h_ref[dr, c] = h
                sc = s_ref[dr, c]
                out.append((h * arr + rh * aii + sc, rh * arr - h * aii + pltpu.roll(sc, S5_STATE, 1)))
            return tuple(out)

        zero = jnp.zeros((S5_GROUPS, 128), F32)
        lax.fori_loop(0, cn, step, ((zero, zero), (zero, zero)), unroll=4)
        for cp in arrivals:
            cp.wait_recv()
        for cp in sends:
            cp.wait_send()

    vmem = pl.BlockSpec(memory_space=pltpu.VMEM)
    return pl.pallas_call(
        body, name="s5_scan",
        in_specs=[vmem, vmem] + [ANY] * n, out_specs=[vmem] + [ANY] * n,
        out_shape=[jax.ShapeDtypeStruct(sloc.shape, F32)] + [jax.ShapeDtypeStruct(p.shape, p.dtype) for p in placed],
        input_output_aliases={2 + a: 1 + a for a in range(n)},
        scratch_shapes=[pltpu.SemaphoreType.DMA((n, 3)), pltpu.SemaphoreType.DMA((n, 3))],
        compiler_params=_params(),
    )(sloc, a16, *placed)


def _s5_scan_bwd(e, hs, a16, ncc):
    cn = e.shape[1]

    def body(e_ref, h_ref, a_ref, ds_ref, da_ref):
        forms = [_s5_a_forms(a_ref[dr]) for dr in range(2)]
        low = _iota2((S5_GROUPS, 128), 1) < S5_STATE

        def step(s, carry):
            out = []
            r = cn - 1 - s
            for dr in range(2):
                arr, aii = forms[dr]
                g, rg, da = carry[dr]
                c = r if dr == 0 else jnp.where(r < ncc, ncc - 1 - r, cn - 1 - (r - ncc))
                ds_ref[dr, c] = g
                h = h_ref[dr, c]
                rh = pltpu.roll(h, S5_STATE, 1)
                da = da + jnp.where(low, g * h + rg * rh, g * rh - rg * h)
                ec = e_ref[dr, c]
                out.append((ec + g * arr - rg * aii, pltpu.roll(ec, S5_STATE, 1) + rg * arr + g * aii, da))
            return tuple(out)

        zero = jnp.zeros((S5_GROUPS, 128), F32)
        res = lax.fori_loop(0, cn, step, ((zero, zero, zero), (zero, zero, zero)), unroll=4)
        da_ref[0] = res[0][2]
        da_ref[1] = res[1][2]

    return pl.pallas_call(
        body, name="s5_scan_bwd",
        out_shape=[jax.ShapeDtypeStruct(e.shape, F32), jax.ShapeDtypeStruct((2, S5_GROUPS, 128), F32)],
        compiler_params=_params(),
    )(e, hs, a16)


def _s5_bwd_kb(p3, dy3):
    cn = p3.shape[1]
    half = S5_T // 2

    def body(u_ref, d_ref, o_ref):
        q = pl.program_id(1)

        @pl.when(q == 0)
        def _():
            o_ref[...] = jnp.zeros_like(o_ref)

        dk = _dot_tn(u_ref[0], d_ref[0])
        for j in range(S5_T):
            for i in range(half):
                o_ref[0, half * q + i - j + (S5_T - 1)] += dk[128 * j:128 * (j + 1), 128 * i:128 * (i + 1)]

    return pl.pallas_call(
        body, name="s5_bwd_kb", grid=(S5_NB, 2),
        in_specs=[pl.BlockSpec((1, cn, S5_BW), lambda b, q: (b, 0, 0)),
                  pl.BlockSpec((1, cn, S5_BW // 2), lambda b, q: (b, 0, q))],
        out_specs=pl.BlockSpec((1, 2 * S5_T - 1, 128, 128), lambda b, q: (b, 0, 0, 0)),
        out_shape=jax.ShapeDtypeStruct((S5_NB, 2 * S5_T - 1, 128, 128), F32),
        compiler_params=_params(("parallel", "arbitrary")),
    )(p3, dy3)


def _s5_bwd_w(u3, st, name):
    cn = u3.shape[1]

    def body(u_ref, s_ref, w_ref):
        dw = _dot_tn(u_ref[0], _s5_get_groups(s_ref, 0))
        for j in range(S5_T):
            w_ref[0, 0, j] = _s5_contract(dw[128 * j:128 * (j + 1), :])

    return pl.pallas_call(
        body, name=name, grid=(S5_NB, 2),
        in_specs=[pl.BlockSpec((1, cn, S5_BW), lambda b, q: (b, 0, 0)),
                  pl.BlockSpec((1, cn, 8, 128), lambda b, q: (q, 0, b, 0))],
        out_specs=pl.BlockSpec((1, 1, S5_T, 128, 128), lambda b, q: (b, q, 0, 0, 0)),
        out_shape=jax.ShapeDtypeStruct((S5_NB, 2, S5_T, 128, 128), F32),
        compiler_params=_params(("parallel", "parallel")),
    )(u3, st)


K_SCALE = RET_DH ** -0.5
G_COL = 16


def _ret_chunk_of(step, ncc, nch, rev):
    if not rev:
        return step
    return jnp.where(step < ncc, ncc - 1 - step, nch - 1 - (step - ncc))


def _ret_decay(ld, rev):
    c = _iota2((RET_CHUNK, RET_CHUNK), 0).astype(F32)
    m = _iota2((RET_CHUNK, RET_CHUNK), 1).astype(F32)
    diff = (m - c) if rev else (c - m)
    keep = (diff > 0) if rev else (diff >= 0)
    expo = jnp.maximum(diff, 0.0)
    dm = jnp.where(keep, jnp.exp(ld * expo), 0.0)
    xi_e = (RET_CHUNK - c) if rev else (c + 1.0)
    zeta_e = c if rev else (RET_CHUNK - 1.0 - c)
    return dm, expo, jnp.exp(ld * xi_e), xi_e, jnp.exp(ld * zeta_e), zeta_e


RET_TABLES = 7


def _ret_tables(ld2):
    def body(ld_ref, t_ref):
        dr, h = pl.program_id(0), pl.program_id(1)
        ldh = ld_ref[dr, h]
        for rev in (False, True):
            @pl.when(dr == int(rev))
            def _(rev=rev):
                dm, expo, xi, xi_e, zeta, zeta_e = _ret_decay(ldh, rev)
                t_ref[0, 0, 0] = dm
                t_ref[0, 0, 1] = dm * expo
                t_ref[0, 0, 2] = xi
                t_ref[0, 0, 3] = xi * xi_e
                t_ref[0, 0, 4] = zeta
                t_ref[0, 0, 5] = zeta * zeta_e
                t_ref[0, 0, 6] = jnp.zeros_like(dm) + jnp.exp(ldh * RET_CHUNK)

    return pl.pallas_call(
        body, name="ret_tables", grid=(2, RET_HEADS),
        in_specs=[pl.BlockSpec(memory_space=pltpu.SMEM)],
        out_specs=pl.BlockSpec((1, 1, RET_TABLES, RET_CHUNK, RET_CHUNK), lambda d, h: (d, h, 0, 0, 0)),
        out_shape=jax.ShapeDtypeStruct((2, RET_HEADS, RET_TABLES, RET_CHUNK, RET_CHUNK), F32),
        compiler_params=_params(("parallel", "parallel")),
    )(ld2)


def _ret_specs(nch, ncc, rev, step_of):
    chunk = lambda n: _ret_chunk_of(step_of(n), ncc, nch, rev)
    cols = [pl.BlockSpec((RET_CHUNK, RET_WIDTH), functools.partial(lambda n, cb: (chunk(n), cb), cb=cb))
            for cb in (1, 2, 3)]
    return cols, pl.BlockSpec((RET_CHUNK, RET_WIDTH), lambda n: (chunk(n), 0))


def _ret_scan(p_all, tabs, ncc):
    la = p_all.shape[0]
    nch = la // RET_CHUNK

    def body(t_ref, qf, kf, vf, qb, kb, vb, of_ref, ob_ref, ssf_ref, ssb_ref, s_scr):
        @pl.when(pl.program_id(0) == 0)
        def _():
            s_scr[...] = jnp.zeros_like(s_scr)

        for dr, (q_ref, k_ref, v_ref, o_ref, ss_ref) in enumerate(
                ((qf, kf, vf, of_ref, ssf_ref), (qb, kb, vb, ob_ref, ssb_ref))):
            for h in range(RET_HEADS):
                sl = slice(RET_DH * h, RET_DH * (h + 1))
                dm, xi, zeta = t_ref[dr, h, 0], t_ref[dr, h, 2, :, 0:RET_DH], t_ref[dr, h, 4, :, 0:RET_DH]
                q, k = q_ref[:, sl], k_ref[:, sl]
                vh = v_ref[:, sl].astype(BF16)
                s = s_scr[dr, h]
                ss_ref[0, h] = s
                sc = (_dot_nt(q.astype(BF16), k.astype(BF16)) * dm).astype(BF16)
                o_ref[:, sl] = _dot(sc, vh) + _dot((q * xi).astype(BF16), s.astype(BF16))
                s_scr[dr, h] = t_ref[dr, h, 6, 0:RET_DH, 0:RET_DH] * s + _dot_tn((k * zeta).astype(BF16), vh)

    in_f, out_f = _ret_specs(nch, ncc, False, lambda n: n)
    in_b, out_b = _ret_specs(nch, ncc, True, lambda n: n)
    ss_spec = pl.BlockSpec((1, RET_HEADS, RET_DH, RET_DH), lambda n: (n, 0, 0, 0))
    o_shape = jax.ShapeDtypeStruct((la, RET_WIDTH), F32)
    ss_shape = jax.ShapeDtypeStruct((nch, RET_HEADS, RET_DH, RET_DH), F32)
    return pl.pallas_call(
        body, name="ret_scan", grid=(nch,),
        in_specs=[_full(tabs.shape)] + in_f + in_b,
        out_specs=[out_f, out_b, ss_spec, ss_spec],
        out_shape=[o_shape, o_shape, ss_shape, ss_shape],
        scratch_shapes=[pltpu.VMEM((2, RET_HEADS, RET_DH, RET_DH), F32)],
        compiler_params=_params(("arbitrary",)),
    )(tabs, p_all, p_all, p_all, p_all, p_all, p_all)


def _ret_scan_bwd(p_all, tabs, ssf, ssb, dy_all, ncc):
    la = p_all.shape[0]
    nch = la // RET_CHUNK

    def body(t_ref, qf, kf, vf, dof, ssf_ref, qb, kb, vb, dob_, ssb_ref,
             dqf, dkf, dvf, dqb, dkb, dvb, dld_ref, ds_scr):
        @pl.when(pl.program_id(0) == 0)
        def _():
            ds_scr[...] = jnp.zeros_like(ds_scr)
            dld_ref[...] = jnp.zeros_like(dld_ref)

        for dr, (q_ref, k_ref, v_ref, do_ref, ss_ref, dq_ref, dk_ref, dv_ref) in enumerate(
                ((qf, kf, vf, dof, ssf_ref, dqf, dkf, dvf), (qb, kb, vb, dob_, ssb_ref, dqb, dkb, dvb))):
            on_ctx = _ret_chunk_of(nch - 1 - pl.program_id(0), ncc, nch, dr == 1) < ncc
            for h in range(RET_HEADS):
                sl = slice(RET_DH * h, RET_DH * (h + 1))
                dm, dm_d = t_ref[dr, h, 0], t_ref[dr, h, 1]
                xi, xi_d, zeta, zeta_d = [t_ref[dr, h, t, :, 0:RET_DH] for t in (2, 3, 4, 5)]
                gc = t_ref[dr, h, 6, 0:RET_DH, 0:RET_DH]
                q, k = q_ref[:, sl], k_ref[:, sl]
                q16, k16, v16 = q.astype(BF16), k.astype(BF16), v_ref[:, sl].astype(BF16)
                s = ss_ref[0, h]
                s16 = s.astype(BF16)
                ds_in = ds_scr[dr, h]
                ds16 = ds_in.astype(BF16)
                do16 = jnp.where(on_ctx, 0.0, do_ref[:, sl]).astype(BF16)
                qk = _dot_nt(q16, k16)
                dsv = _dot_nt(do16, v16)
                dsc = (dsv * dm).astype(BF16)
                sc16 = (qk * dm).astype(BF16)
                dos = _dot_nt(do16, s16)
                vds = _dot_nt(v16, ds16)
                dq_ref[:, sl] = _dot(dsc, k16) + dos * xi
                dk_ref[:, sl] = _dot_tn(dsc, q16) + vds * zeta
                dv_ref[:, sl] = _dot_tn(sc16, do16) + _dot((k * zeta).astype(BF16), ds16)
                ds_scr[dr, h] = _dot_tn((q * xi).astype(BF16), do16) + gc * ds_in
                dld = (jnp.sum(dsv * qk * dm_d) + jnp.sum(q * dos * xi_d + k * vds * zeta_d)
                       + RET_CHUNK * jnp.sum(gc * s * ds_in))
                dld_ref[dr, h] += dld

    back = lambda n: nch - 1 - n
    in_f, out_f = _ret_specs(nch, ncc, False, back)
    in_b, out_b = _ret_specs(nch, ncc, True, back)
    ss_spec = pl.BlockSpec((1, RET_HEADS, RET_DH, RET_DH), lambda n: (nch - 1 - n, 0, 0, 0))
    shp = jax.ShapeDtypeStruct((la, RET_WIDTH), F32)
    dy_spec = lambda rev: pl.BlockSpec(
        (RET_CHUNK, RET_WIDTH), lambda n: (jnp.maximum(_ret_chunk_of(nch - 1 - n, ncc, nch, rev) - ncc, 0), 0))
    return pl.pallas_call(
        body, name="ret_scan_bwd", grid=(nch,),
        in_specs=[_full(tabs.shape)] + in_f + [dy_spec(False), ss_spec] + in_b + [dy_spec(True), ss_spec],
        out_specs=[out_f, out_f, out_f, out_b, out_b, out_b, _full((2, RET_HEADS, 8, 128))],
        out_shape=[shp] * 6 + [jax.ShapeDtypeStruct((2, RET_HEADS, 8, 128), F32)],
        scratch_shapes=[pltpu.VMEM((2, RET_HEADS, RET_DH, RET_DH), F32)],
        compiler_params=_params(("arbitrary",)),
    )(tabs, p_all, p_all, p_all, dy_all, ssf, p_all, p_all, p_all, dy_all, ssb)


def _in_bwd(dqf, dkf, dvf, dqb, dkb, dvb, du, dg, cos_t, sin_t, w_in_b, x, ctx, n1w, mod4, dx1):
    l, lc = x.shape[0], ctx.shape[0]
    la = l + lc
    tm = TOK_TILE
    nct = lc // tm

    def body(dqf_ref, dkf_ref, dvf_ref, dqb_ref, dkb_ref, dvb_ref, du_ref, dg_ref, cos_ref, sin_ref,
             w_ref, x_ref, c_ref, nw_ref, mod_ref, dx1_ref, dp_ref, gx_ref, acc_ref):
        i = pl.program_id(0)
        is_ctx = i < nct

        @pl.when(i == 0)
        def _():
            acc_ref[...] = jnp.zeros_like(acc_ref)

        cs, sn = cos_ref[...], sin_ref[...]
        def piece(k, val):
            cols = slice(S5_WIDTH * k, S5_WIDTH * (k + 1))
            dp_ref[:, cols] = val.astype(BF16)
            return _dot_nt(dp_ref[:, cols], w_ref[:, cols])

        dh1 = piece(0, du_ref[...])
        dh1 = dh1 + piece(3, dvf_ref[...] + dvb_ref[...])
        dh1 = dh1 + piece(4, jnp.where(is_ctx, 0.0, dg_ref[...]))
        for k, (f_ref, b_ref, scale) in ((1, (dqf_ref, dqb_ref, 1.0)), (2, (dkf_ref, dkb_ref, K_SCALE))):
            heads = [_rope_t(f_ref[:, RET_DH * h:RET_DH * (h + 1)] + b_ref[:, RET_DH * h:RET_DH * (h + 1)], cs, sn) * scale
                     for h in range(RET_HEADS)]
            dh1 = dh1 + piece(k, jnp.concatenate(heads, axis=1))
        xt = jnp.where(is_ctx, c_ref[...], x_ref[...])
        sh = jnp.where(is_ctx, mod_ref[0:1, :], mod_ref[2:3, :])
        sc = jnp.where(is_ctx, mod_ref[1:2, :], mod_ref[3:4, :])
        _, vjp = jax.vjp(_rms_mod, xt, nw_ref[...], sh, sc)
        dx, dnw, dsh, dsc = vjp(dh1)
        gx_ref[...] = dx + dx1_ref[...]
        cf = jnp.where(is_ctx, 1.0, 0.0)
        acc_ref[0:1, :] += dnw
        acc_ref[1:2, :] += cf * dsh
        acc_ref[2:3, :] += cf * dsc
        acc_ref[3:4, :] += (1.0 - cf) * dsh
        acc_ref[4:5, :] += (1.0 - cf) * dsc

    row = pl.BlockSpec((tm, RET_WIDTH), lambda i: (i, 0))
    tab = pl.BlockSpec((tm, RET_DH), lambda i: (i, 0))
    xrow = pl.BlockSpec((tm, D_MODEL), lambda i: (jnp.maximum(i - nct, 0), 0))
    return pl.pallas_call(
        body, name="in_bwd", grid=(la // tm,),
        in_specs=[row] * 7 + [pl.BlockSpec((tm, RET_WIDTH), lambda i: (jnp.maximum(i - nct, 0), 0)),
                              tab, tab, _full((D_MODEL, IN_COLS)), xrow,
                              pl.BlockSpec((tm, D_MODEL), lambda i: (jnp.minimum(i, nct - 1), 0)),
                              _full((1, D_MODEL)), _full((4, D_MODEL)), xrow],
        out_specs=[pl.BlockSpec((tm, IN_COLS), lambda i: (i, 0)), xrow, _full((8, D_MODEL))],
        out_shape=[jax.ShapeDtypeStruct((la, IN_COLS), BF16), jax.ShapeDtypeStruct((l, D_MODEL), F32),
                   jax.ShapeDtypeStruct((8, D_MODEL), F32)],
        compiler_params=_params(("arbitrary",)),
    )(dqf, dkf, dvf, dqb, dkb, dvb, du, dg, cos_t, sin_t, w_in_b, x, ctx, n1w, mod4, dx1)


def _outproj_up(x, y_all, of, ob, p_all, w_glu_b, b_glu, w_out_b, mod3, n2w, w_up_b, nct):
    l = x.shape[0]
    tm = TOK_TILE

    def body(x_ref, y_ref, of_ref, ob_ref, g_ref, wg_ref, bg_ref, wo_ref, mod_ref, nw_ref, wu_ref,
             x1_ref, mix_ref, h2_ref, up_ref, mb_ref, yr_ref):
        yg = _gelu(y_ref[...])
        mb_ref[:, 0:S5_WIDTH] = (yg * _sigmoid(_dot(yg.astype(BF16), wg_ref[...]) + bg_ref[...])).astype(BF16)
        yr = of_ref[...] + ob_ref[...]
        yr_ref[...] = yr
        for h in range(RET_HEADS):
            sl = slice(RET_DH * h, RET_DH * (h + 1))
            mb_ref[:, S5_WIDTH + RET_DH * h:S5_WIDTH + RET_DH * (h + 1)] = (
                _head_norm_gate(yr[:, sl], g_ref[:, sl]).astype(BF16))
        mix = _dot(mb_ref[...], wo_ref[...])
        mix_ref[...] = mix
        x1 = x_ref[...] + mod_ref[0:1, :] * mix
        x1_ref[...] = x1
        h2 = _rms_mod(x1, nw_ref[...], mod_ref[1:2, :], mod_ref[2:3, :]).astype(BF16)
        h2_ref[...] = h2
        up_ref[...] = _dot(h2, wu_ref[...])

    row = lambda w: pl.BlockSpec((tm, w), lambda i: (i, 0))
    arow = pl.BlockSpec((tm, RET_WIDTH), lambda i: (i + nct, 0))
    return pl.pallas_call(
        body, name="outproj_up", grid=(l // tm,),
        in_specs=[row(D_MODEL), arow, arow, arow, pl.BlockSpec((tm, RET_WIDTH), lambda i: (i + nct, G_COL // 4)),
                  _full((S5_WIDTH, S5_WIDTH)), _full((1, S5_WIDTH)), _full((D_MODEL, D_MODEL)), _full((3, D_MODEL)),
                  _full((1, D_MODEL)), _full((D_MODEL, 2 * D_FF))],
        out_specs=[row(D_MODEL), row(D_MODEL), row(D_MODEL), row(2 * D_FF), row(D_MODEL), row(RET_WIDTH)],
        out_shape=[jax.ShapeDtypeStruct((l, D_MODEL), F32), jax.ShapeDtypeStruct((l, D_MODEL), F32),
                   jax.ShapeDtypeStruct((l, D_MODEL), BF16), jax.ShapeDtypeStruct((l, 2 * D_FF), F32),
                   jax.ShapeDtypeStruct((l, D_MODEL), BF16), jax.ShapeDtypeStruct((l, RET_WIDTH), F32)],
        compiler_params=_params(("parallel",)),
    )(x, y_all, of, ob, p_all, w_glu_b, b_glu, w_out_b, mod3, n2w, w_up_b)


HALO = 8


def _conv_taps(g, prev_row, next_row):
    t = g.shape[0]
    r = _iota2(g.shape, 0)
    gprev = jnp.where(r == 0, prev_row, pltpu.roll(g, 1, 0))
    gnext = jnp.where(r == t - 1, next_row, pltpu.roll(g, t - 1, 0))
    return gprev, gnext


def _ffn_loss(up, x1, conv_w, conv_b, w_down_b, gate, fnw, tgt):
    l = x1.shape[0]
    tm = TOK_TILE
    nt = l // tm
    hb = tm // HALO

    cw = 256

    def body(up_a, up_g, hp_ref, hn_ref, x1_ref, cw_ref, cb_ref, wd_ref, gate_ref, fn_ref, tgt_ref,
             act_ref, dx2_ref, ddn_ref, dact_ref, acc_ref, ddn_scr):
        step = pl.program_id(0)
        i = jnp.minimum(step, nt - 1)

        @pl.when(step == 0)
        def _():
            acc_ref[...] = jnp.zeros_like(acc_ref)
            ddn_scr[...] = jnp.zeros_like(ddn_scr)

        ddn_prev = ddn_scr[...]
        dn = jnp.zeros((tm, D_MODEL), F32)
        for c in range(D_FF // cw):
            cols = slice(cw * c, cw * (c + 1))
            g = up_g[:, cols]
            prev_row = jnp.where(i == 0, 0.0, hp_ref[HALO - 1:HALO, cols])
            next_row = jnp.where(i == nt - 1, 0.0, hn_ref[0:1, cols])
            gprev, gnext = _conv_taps(g, prev_row, next_row)
            gc = cb_ref[:, cols] + gprev * cw_ref[0:1, cols] + g * cw_ref[1:2, cols] + gnext * cw_ref[2:3, cols]
            act = (_gelu(gc) * up_a[:, cols]).astype(BF16)
            act_ref[:, cols] = act
            dn = dn + _dot(act, wd_ref[cols, :])
            dact_ref[:, cols] = _dot_nt(ddn_prev, wd_ref[cols, :])
        x2 = x1_ref[...] + gate_ref[...] * dn
        y, vjp = jax.vjp(_rms, x2, fn_ref[...])
        err = y - tgt_ref[...]
        dx2, dfn = vjp(err * (1.0 / D_MODEL))
        dx2_ref[...] = dx2
        ddn = (dx2 * gate_ref[...]).astype(BF16)
        ddn_ref[...] = ddn
        ddn_scr[...] = ddn
        live = step < nt
        acc_ref[0:1, :] += jnp.where(live, dfn, 0.0)
        acc_ref[1:2, :] += jnp.where(live, jnp.sum(dx2 * dn, axis=0, keepdims=True), 0.0)
        acc_ref[2:3, :] += jnp.where(live, (0.5 / D_MODEL) * jnp.sum(err * err), 0.0)

    tile = lambda s: jnp.minimum(s, nt - 1)
    row = lambda w, cb=0: pl.BlockSpec((tm, w), lambda s: (tile(s), cb))
    last = l // HALO - 1
    return pl.pallas_call(
        body, name="ffn_loss", grid=(nt + 1,),
        in_specs=[row(D_FF, 0), row(D_FF, 1),
                  pl.BlockSpec((HALO, D_FF), lambda s: (jnp.maximum(tile(s) * hb - 1, 0), 1)),
                  pl.BlockSpec((HALO, D_FF), lambda s: (jnp.minimum((tile(s) + 1) * hb, last), 1)),
                  row(D_MODEL), _full((3, D_FF)), _full((1, D_FF)), _full((D_FF, D_MODEL)),
                  _full((1, D_MODEL)), _full((1, D_MODEL)), row(D_MODEL)],
        out_specs=[row(D_FF), row(D_MODEL), row(D_MODEL),
                   pl.BlockSpec((tm, D_FF), lambda s: (jnp.maximum(s - 1, 0), 0)), _full((8, D_MODEL))],
        out_shape=[jax.ShapeDtypeStruct((l, D_FF), BF16), jax.ShapeDtypeStruct((l, D_MODEL), F32),
                   jax.ShapeDtypeStruct((l, D_MODEL), BF16), jax.ShapeDtypeStruct((l, D_FF), F32),
                   jax.ShapeDtypeStruct((8, D_MODEL), F32)],
        scratch_shapes=[pltpu.VMEM((tm, D_MODEL), BF16)],
        compiler_params=_params(("arbitrary",)),
    )(up, up, up, up, x1, conv_w, conv_b, w_down_b, gate, fnw, tgt)


def _convglu_bwd(up, dact, conv_w, conv_b):
    l = up.shape[0]
    tm = 128
    nt = l // tm
    hb = tm // HALO
    te = tm + 2 * HALO

    def body(a_ref, ap_ref, an_ref, g_ref, gp_ref, gn_ref, d_ref, dp_ref, dn_ref, cw_ref, cb_ref,
             dup_ref, acc_ref):
        i = pl.program_id(0)

        @pl.when(i == 0)
        def _():
            acc_ref[...] = jnp.zeros_like(acc_ref)

        def ext(p, c, n):
            return jnp.concatenate([jnp.where(i == 0, 0.0, p[...]), c[...], jnp.where(i == nt - 1, 0.0, n[...])], axis=0)

        ae, ge, de = ext(ap_ref, a_ref, an_ref), ext(gp_ref, g_ref, gn_ref), ext(dp_ref, d_ref, dn_ref)
        gprev = pltpu.roll(ge, 1, 0)
        gnext = pltpu.roll(ge, te - 1, 0)
        w0, w1, w2 = cw_ref[0:1, :], cw_ref[1:2, :], cw_ref[2:3, :]
        gce = cb_ref[...] + gprev * w0 + ge * w1 + gnext * w2
        gel, dgel = _gelu_and_grad(gce)
        dae = de * gel
        dgce = de * ae * dgel
        dge = dgce * w1 + pltpu.roll(dgce, te - 1, 0) * w0 + pltpu.roll(dgce, 1, 0) * w2
        mid = slice(HALO, HALO + tm)
        dup_ref[:, 0:D_FF] = dae[mid].astype(BF16)
        dup_ref[:, D_FF:2 * D_FF] = dge[mid].astype(BF16)
        dgc = dgce[mid]
        acc_ref[0:1, :] += jnp.sum(dgc * gprev[mid], axis=0, keepdims=True)
        acc_ref[1:2, :] += jnp.sum(dgc * ge[mid], axis=0, keepdims=True)
        acc_ref[2:3, :] += jnp.sum(dgc * gnext[mid], axis=0, keepdims=True)
        acc_ref[3:4, :] += jnp.sum(dgc, axis=0, keepdims=True)

    last = l // HALO - 1

    def trio(cb):
        return [pl.BlockSpec((tm, D_FF), lambda i: (i, cb)),
                pl.BlockSpec((HALO, D_FF), lambda i: (jnp.maximum(i * hb - 1, 0), cb)),
                pl.BlockSpec((HALO, D_FF), lambda i: (jnp.minimum((i + 1) * hb, last), cb))]

    return pl.pallas_call(
        body, name="convglu_bwd", grid=(nt,),
        in_specs=trio(0) + trio(1) + trio(0) + [_full((3, D_FF)), _full((1, D_FF))],
        out_specs=[pl.BlockSpec((tm, 2 * D_FF), lambda i: (i, 0)), _full((8, D_FF))],
        out_shape=[jax.ShapeDtypeStruct((l, 2 * D_FF), BF16), jax.ShapeDtypeStruct((8, D_FF), F32)],
        compiler_params=_params(("arbitrary",)),
    )(up, up, up, up, up, up, dact, dact, dact, conv_w, conv_b)


def _up_bwd(dup, w_up_b, w_out_b, x1, dx2, mix, mod3, n2w, y_all, y_ret, p_all, w_glu_b, b_glu, zero_rows, nct, pairs,
            kinds):
    l = x1.shape[0]
    tm = TOK_TILE
    nt = l // tm
    n = len(pairs)
    shapes = _rs_slot_shapes(pairs, kinds)
    n_out = 8

    def body(dup_ref, wu_ref, wo_ref, x1_ref, dx2_ref, mix_ref, mod_ref, nw_ref, y_ref, yr_ref, g_ref, wg_ref, bg_ref,
             zero_rows_ref, *rest):
        dx1_ref, dmixb_ref, acc_ref, dys_ref, dyr_ref, dg_ref, gw_ref, gb_ref = rest[n:n + n_out]
        send_sems, recv_sems, dy_scr = rest[2 * n + n_out:]
        step = pl.program_id(0)

        @pl.when(step == 0)
        def _():
            acc_ref[...] = jnp.zeros_like(acc_ref)
            gw_ref[...] = jnp.zeros_like(gw_ref)
            gb_ref[...] = jnp.zeros_like(gb_ref)

        _behind(step, nt - 1, functools.partial(_rs_chip_copies, rest[:n], rest[n + n_out:2 * n + n_out], kinds,
                                                shapes, send_sems, recv_sems))

        dh2 = _dot_nt(dup_ref[...], wu_ref[...])
        _, vjp = jax.vjp(_rms_mod, x1_ref[...], nw_ref[...], mod_ref[1:2, :], mod_ref[2:3, :])
        dx, dnw, dsh, dsc = vjp(dh2)
        dx1 = dx + dx2_ref[...]
        dx1_ref[...] = dx1
        dmixb = (dx1 * mod_ref[0:1, :]).astype(BF16)
        dmixb_ref[...] = dmixb
        dmix = _dot_nt(dmixb, wo_ref[...])
        acc_ref[0:1, :] += dnw
        acc_ref[1:2, :] += jnp.sum(dx1 * mix_ref[...], axis=0, keepdims=True)
        acc_ref[2:3, :] += dsh
        acc_ref[3:4, :] += dsc

        yg, dgel = _gelu_and_grad(y_ref[...])
        ygb = yg.astype(BF16)
        sg = _sigmoid(_dot(ygb, wg_ref[...]) + bg_ref[...])
        ds = dmix[:, 0:S5_WIDTH]
        dz = ds * yg * sg * (1.0 - sg)
        dzb = dz.astype(BF16)
        _s5_put_rows(dys_ref, dy_scr, (ds * sg + _dot_nt(dzb, wg_ref[...])) * dgel)
        gw_ref[...] += _dot_tn(ygb, dzb)
        gb_ref[...] += jnp.sum(dz, axis=0, keepdims=True)

        for h in range(RET_HEADS):
            sl = slice(RET_DH * h, RET_DH * (h + 1))
            _, hvjp = jax.vjp(_head_norm_gate, yr_ref[:, sl], g_ref[:, sl])
            dyr, dg = hvjp(dmix[:, S5_WIDTH + RET_DH * h:S5_WIDTH + RET_DH * (h + 1)])
            dyr_ref[:, sl] = dyr
            dg_ref[:, sl] = dg

    row = pl.BlockSpec((tm, D_MODEL), lambda i: (i, 0))
    half = pl.BlockSpec((tm, S5_WIDTH), lambda i: (i, 0))
    f32h = jax.ShapeDtypeStruct((l, RET_WIDTH), F32)
    return pl.pallas_call(
        body, name="up_bwd", grid=(nt,),
        in_specs=[pl.BlockSpec((tm, 2 * D_FF), lambda i: (i, 0)), _full((D_MODEL, 2 * D_FF)),
                  _full((D_MODEL, D_MODEL)), row, row, row, _full((3, D_MODEL)), _full((1, D_MODEL)),
                  pl.BlockSpec((tm, S5_WIDTH), lambda i: (i + nct, 0)), half,
                  pl.BlockSpec((tm, RET_WIDTH), lambda i: (i + nct, G_COL // 4)),
                  _full((S5_WIDTH, S5_WIDTH)), _full((1, S5_WIDTH)), ANY] + [ANY] * n,
        out_specs=[row, row, _full((8, D_MODEL)),
                   pl.BlockSpec((S5_NB, tm // S5_T, S5_BW), lambda i: (0, i + nct, 0)), half, half,
                   _full((S5_WIDTH, S5_WIDTH)),
                   _full((1, S5_WIDTH))] + [ANY] * n,
        out_shape=[jax.ShapeDtypeStruct((l, D_MODEL), F32), jax.ShapeDtypeStruct((l, D_MODEL), BF16),
                   jax.ShapeDtypeStruct((8, D_MODEL), F32), jax.ShapeDtypeStruct(zero_rows.shape, BF16), f32h, f32h,
                   jax.ShapeDtypeStruct((S5_WIDTH, S5_WIDTH), F32), jax.ShapeDtypeStruct((1, S5_WIDTH), F32)]
        + [jax.ShapeDtypeStruct((4,) + s, p.dtype) for s, p in zip(shapes, pairs)],
        input_output_aliases={13: 3},
        scratch_shapes=[pltpu.SemaphoreType.DMA((n, 3)), pltpu.SemaphoreType.DMA((n, 3)),
                        pltpu.VMEM((tm // S5_T, S5_T, S5_WIDTH), F32)],
        compiler_params=_params(("arbitrary",)),
    )(dup, w_up_b, w_out_b, x1, dx2, mix, mod3, n2w, y_all, y_ret, p_all, w_glu_b, b_glu, zero_rows, *pairs)


MOD_ROWS = 16
MOD_COLS = 6 * D_MODEL // 4


def _mod_fwd(c_all, c_ctx, w_mod_b, b_loc):
    def body(c_ref, cc_ref, w_ref, b_ref, m_ref, s_ref):
        cond = jnp.concatenate([c_ref[...], jnp.broadcast_to(cc_ref[...], (8, D_MODEL))], axis=0)
        s = _silu(cond).astype(BF16)
        s_ref[...] = s
        m_ref[...] = _dot(s, w_ref[...]) + b_ref[...]

    return pl.pallas_call(
        body, name="mod_fwd",
        out_shape=[jax.ShapeDtypeStruct((MOD_ROWS, MOD_COLS), F32), jax.ShapeDtypeStruct((MOD_ROWS, D_MODEL), BF16)],
        compiler_params=_params(),
    )(c_all, c_ctx, w_mod_b, b_loc)


def _mod_bwd_sum(dm_all):
    def body(d_ref, dm_ref, gb_ref):
        rows = [d_ref[k, 0:1, :] for k in range(8)]
        ctx_sum = d_ref[0, 1:2, :]
        for k in range(1, 8):
            ctx_sum = ctx_sum + d_ref[k, 1:2, :]
        gb = ctx_sum
        for k in range(8):
            gb = gb + rows[k]
        gb_ref[...] = gb
        dm_ref[...] = jnp.concatenate(rows + [ctx_sum] + [jnp.zeros((7, 6 * D_MODEL), F32)], axis=0)

    return pl.pallas_call(
        body, name="mod_bwd_sum",
        out_shape=[jax.ShapeDtypeStruct((MOD_ROWS, 6 * D_MODEL), F32), jax.ShapeDtypeStruct((1, 6 * D_MODEL), F32)],
        compiler_params=_params(),
    )(dm_all)


def _mod_bwd_w(dm_loc, s_b, c_ctx, w_mod_b):
    def body(d_ref, s_ref, cc_ref, w_ref, gw_ref, gc_ref):
        db = d_ref[...].astype(BF16)
        gw_ref[...] = _dot_tn(s_ref[...], db)
        ds = _dot_nt(db, w_ref[...])
        _, vjp = jax.vjp(_silu, cc_ref[...])
        gc_ref[...] = jnp.broadcast_to(vjp(ds[8:9, :])[0], (8, D_MODEL))

    return pl.pallas_call(
        body, name="mod_bwd_w",
        out_shape=[jax.ShapeDtypeStruct((D_MODEL, MOD_COLS), F32), jax.ShapeDtypeStruct((8, D_MODEL), F32)],
        compiler_params=_params(),
    )(dm_loc, s_b, c_ctx, w_mod_b)


def _adamw(w, g, m, v, name):
    r, c = w.shape
    tr = _pick(r, (256, 128, 64, 32, 16, 8))
    bc1 = 1.0 - ADAM_B1 ** ADAM_STEP
    bc2 = 1.0 - ADAM_B2 ** ADAM_STEP

    def body(w_ref, g_ref, m_ref, v_ref, d_ref, nm_ref, nv_ref):
        gg = g_ref[...]
        nm = ADAM_B1 * m_ref[...] + (1.0 - ADAM_B1) * gg
        nv = ADAM_B2 * v_ref[...] + (1.0 - ADAM_B2) * (gg * gg)
        nm_ref[...] = nm
        nv_ref[...] = nv
        d_ref[...] = -ADAM_LR * ((nm / bc1) / (jnp.sqrt(nv / bc2) + ADAM_EPS) + ADAM_WD * w_ref[...])

    blk = pl.BlockSpec((tr, c), lambda i: (i, 0))
    shp = jax.ShapeDtypeStruct((r, c), F32)
    return pl.pallas_call(
        body, name=name, grid=(r // tr,), in_specs=[blk] * 4, out_specs=[blk] * 3, out_shape=[shp] * 3,
        compiler_params=_params(("parallel",)),
    )(w, g, m, v)


def _sum_slots(a, name):
    n, r, c = a.shape
    tr = _pick(r, (376, 256, 208, 128, 64, 32, 16, 8))

    def body(a_ref, o_ref):
        acc = a_ref[0].astype(F32)
        for k in range(1, n):
            acc = acc + a_ref[k].astype(F32)
        o_ref[...] = acc

    return pl.pallas_call(
        body, name=name, grid=(r // tr,),
        in_specs=[pl.BlockSpec((n, tr, c), lambda i: (0, i, 0))],
        out_specs=pl.BlockSpec((tr, c), lambda i: (i, 0)),
        out_shape=jax.ShapeDtypeStruct((r, c), F32),
        compiler_params=_params(("parallel",)),
    )(a)


def _mesh_pos():
    return lax.axis_index("x"), lax.axis_index("y"), lax.axis_index("c")


def _gather8_phases(x_ref, out_ref, send_sems, recv_sems, local_sem, m_per):
    def parts():
        x, y, c = _mesh_pos()
        me, sibling = (x, y, c), (x, y, 1 - c)
        chips = [(1 - x, y), (x, 1 - y), (1 - x, 1 - y)]

        def rows(px, py, pc):
            return out_ref.at[pl.ds((4 * px + 2 * py + pc) * m_per, m_per), :]

        def copy(k, block, to, src=None):
            return pltpu.make_async_remote_copy(
                src_ref=rows(*block) if src is None else src, dst_ref=rows(*block),
                send_sem=send_sems.at[k], recv_sem=recv_sems.at[k], device_id=to, device_id_type=MESH_ID)

        mine = pltpu.make_async_copy(x_ref, rows(*me), local_sem)
        first = [copy(0, me, sibling, src=x_ref)]
        first += [copy(1 + j, me, (*chip, c), src=x_ref) for j, chip in enumerate(chips)]
        return me, sibling, chips, c, copy, mine, first

    def begin():
        *_, mine, first = parts()
        mine.start()
        for cp in first:
            cp.start()

    def finish():
        me, sibling, chips, c, copy, mine, first = parts()
        passed = [copy(4 + j, (*chip, c), sibling) for j, chip in enumerate(chips)]
        for j, chip in enumerate(chips):
            copy(1 + j, (*chip, c), me).wait_recv()
            passed[j].start()
        copy(0, sibling, me).wait_recv()
        for j, chip in enumerate(chips):
            copy(4 + j, (*chip, 1 - c), me).wait_recv()
        for cp in first + passed:
            cp.wait_send()
        mine.wait()

    return begin, finish


GATHER8_SCRATCH = (pltpu.SemaphoreType.DMA((7,)), pltpu.SemaphoreType.DMA((7,)), pltpu.SemaphoreType.DMA)


def _all_gather8(v, name):
    m_per, n = v.shape

    def body(x_ref, out_ref, send_sems, recv_sems, local_sem):
        begin, finish = _gather8_phases(x_ref, out_ref, send_sems, recv_sems, local_sem, m_per)
        begin()
        finish()

    return pl.pallas_call(
        body, name=name,
        out_shape=jax.ShapeDtypeStruct((8 * m_per, n), v.dtype),
        in_specs=[pl.BlockSpec(memory_space=pltpu.VMEM)],
        out_specs=pl.BlockSpec(memory_space=pltpu.VMEM),
        scratch_shapes=list(GATHER8_SCRATCH),
        compiler_params=_params(),
    )(v)


ANY = pl.BlockSpec(memory_space=pl.ANY)
def PEER_CHIPS(x, y):
    return [(x, 1 - y), (1 - x, y), (1 - x, 1 - y)]


def _shard_region(ref, kind, k, rl, cl, r0, nr, c0, nc):
    if kind == "col":
        return ref.at[pl.ds(r0, nr), pl.ds(k * cl + c0, nc)]
    return ref.at[pl.ds(k * rl + r0, nr), pl.ds(c0, nc)]


def _place_shard(w, kind, chip, name):
    rl, cl = w.shape
    tr = _pick(rl, (256, 128, 64))
    nt = rl // tr

    def body(chip_ref, w_ref, o_ref):
        o_ref[...] = w_ref[...].astype(BF16)

    o_map = (lambda i, chip_ref: (i, chip_ref[0])) if kind == "col" else (lambda i, chip_ref: (chip_ref[0] * nt + i, 0))
    return pl.pallas_call(
        body, name=name,
        grid_spec=pltpu.PrefetchScalarGridSpec(
            num_scalar_prefetch=1, grid=(nt,),
            in_specs=[pl.BlockSpec((tr, cl), lambda i, chip_ref: (i, 0))], out_specs=pl.BlockSpec((tr, cl), o_map)),
        out_shape=jax.ShapeDtypeStruct((rl, 4 * cl) if kind == "col" else (4 * rl, cl), BF16),
        compiler_params=_params(("parallel",)),
    )(chip.reshape(1), w)


def _gather_shard_shapes(placed, kinds):
    return [(p.shape[0], p.shape[1] // 4) if k == "col" else (p.shape[0] // 4, p.shape[1]) for p, k in zip(placed, kinds)]


def _gather_chip_copies(outs, kinds, shard_shapes, send_sems, recv_sems, with_arrivals=True):
    x, y, c = _mesh_pos()
    me = 2 * x + y
    sends, arrivals = [], []
    for a in range(len(outs)):
        rl, cl = shard_shapes[a]
        rh = rl // 2
        reg = functools.partial(_shard_region, outs[a], kinds[a], rl=rl, cl=cl, r0=c * rh, nr=rh, c0=0, nc=cl)
        for j, (px, py) in enumerate(PEER_CHIPS(x, y)):
            to = dict(send_sem=send_sems.at[a, j], recv_sem=recv_sems.at[a, j], device_id=(px, py, c),
                      device_id_type=MESH_ID)
            sends.append(pltpu.make_async_remote_copy(src_ref=reg(k=me), dst_ref=reg(k=me), **to))
            if with_arrivals:
                got = reg(k=2 * px + py)
                arrivals.append(pltpu.make_async_remote_copy(src_ref=got, dst_ref=got, **to))
    return sends, arrivals


def _gather_sibling_copies(outs, kinds, shard_shapes, send_sems, recv_sems):
    x, y, c = _mesh_pos()
    forwards, arrivals = [], []
    for a in range(len(outs)):
        rl, cl = shard_shapes[a]
        rh = rl // 2
        for j, (px, py) in enumerate(PEER_CHIPS(x, y)):
            to = dict(send_sem=send_sems.at[a, j], recv_sem=recv_sems.at[a, j], device_id=(x, y, 1 - c),
                      device_id_type=MESH_ID)
            reg = functools.partial(_shard_region, outs[a], kinds[a], k=2 * px + py, rl=rl, cl=cl, nr=rh, c0=0, nc=cl)
            forwards.append(pltpu.make_async_remote_copy(src_ref=reg(r0=c * rh), dst_ref=reg(r0=c * rh), **to))
            arrivals.append(pltpu.make_async_remote_copy(src_ref=reg(r0=(1 - c) * rh), dst_ref=reg(r0=(1 - c) * rh), **to))
    return forwards, arrivals


def _gather_sibling(placed, kinds, name):
    n = len(placed)
    shard_shapes = _gather_shard_shapes(placed, kinds)

    def body(*refs):
        forwards, from_sibling = _gather_sibling_copies(refs[n:2 * n], kinds, shard_shapes, *refs[2 * n:])
        for cp in forwards:
            cp.start()
        for cp in from_sibling:
            cp.wait_recv()
        for cp in forwards:
            cp.wait_send()

    return pl.pallas_call(
        body, name=name,
        out_shape=[jax.ShapeDtypeStruct(p.shape, p.dtype) for p in placed],
        in_specs=[ANY] * n, out_specs=[ANY] * n, input_output_aliases={a: a for a in range(n)},
        scratch_shapes=[pltpu.SemaphoreType.DMA((n, 3))] * 2,
        compiler_params=_params(),
    )(*placed)


def _half(kind, r, c):
    return (r // 2, c) if kind == "col" else (r, c // 2)


def _half_of(ref, kind, which):
    r, c = ref.shape
    hr, hc = _half(kind, r, c)
    return ref.at[pl.ds(which * hr, hr), :] if kind == "col" else ref.at[:, pl.ds(which * hc, hc)]


def _rs_sibling(grads, kinds, name):
    n = len(grads)

    def body(*refs):
        srcs, dsts = refs[:n], refs[n:2 * n]
        send_sems, recv_sems = refs[2 * n:]
        x, y, c = _mesh_pos()
        cps = [pltpu.make_async_remote_copy(src_ref=_half_of(srcs[a], kinds[a], 1 - c), dst_ref=dsts[a],
                                            send_sem=send_sems.at[a], recv_sem=recv_sems.at[a],
                                            device_id=(x, y, 1 - c), device_id_type=MESH_ID) for a in range(n)]
        for cp in cps:
            cp.start()
        for cp in cps:
            cp.wait()

    return pl.pallas_call(
        body, name=name,
        out_shape=[jax.ShapeDtypeStruct(_half(k, *g.shape), g.dtype) for g, k in zip(grads, kinds)],
        in_specs=[ANY] * n, out_specs=[ANY] * n,
        scratch_shapes=[pltpu.SemaphoreType.DMA((n,)), pltpu.SemaphoreType.DMA((n,))],
        compiler_params=_params(),
    )(*grads)


def _pair_sum(gf, rv, kind, ci, name):
    r, c = rv.shape
    tr = _pick(r, (128, 64, 32, 16, 8))
    nt = r // tr

    def body(ci_ref, g_ref, r_ref, o_ref):
        o_ref[...] = (g_ref[...] + r_ref[...]).astype(BF16)

    g_map = (lambda i, ci_ref: (ci_ref[0] * nt + i, 0)) if kind == "col" else (lambda i, ci_ref: (i, ci_ref[0]))
    blk = pl.BlockSpec((tr, c), lambda i, ci_ref: (i, 0))
    return pl.pallas_call(
        body, name=name,
        grid_spec=pltpu.PrefetchScalarGridSpec(num_scalar_prefetch=1, grid=(nt,),
                                               in_specs=[pl.BlockSpec((tr, c), g_map), blk], out_specs=blk),
        out_shape=jax.ShapeDtypeStruct((r, c), BF16),
        compiler_params=_params(("parallel",)),
    )(ci.reshape(1), gf, rv)


def _rs_slot_shapes(pairs, kinds):
    return [(p.shape[0], p.shape[1] // 4) if k == "col" else (p.shape[0] // 4, p.shape[1]) for p, k in zip(pairs, kinds)]


def _rs_chip_copies(srcs, dsts, kinds, shapes, send_sems, recv_sems, with_arrivals=True):
    x, y, c = _mesh_pos()
    me = 2 * x + y
    sends, arrivals = [], []
    for a in range(len(srcs)):
        rl, cl = shapes[a]
        reg = functools.partial(_shard_region, srcs[a], kinds[a], rl=rl, cl=cl, r0=0, nr=rl, c0=0, nc=cl)
        for j, (px, py) in enumerate(PEER_CHIPS(x, y)):
            to = dict(send_sem=send_sems.at[a, j], recv_sem=recv_sems.at[a, j], device_id=(px, py, c),
                      device_id_type=MESH_ID)
            sends.append(pltpu.make_async_remote_copy(src_ref=reg(k=2 * px + py), dst_ref=dsts[a].at[me], **to))
            if with_arrivals:
                slot = dsts[a].at[2 * px + py]
                arrivals.append(pltpu.make_async_remote_copy(src_ref=slot, dst_ref=slot, **to))
    return sends, arrivals


def _rs_chips(pairs, kinds):
    n = len(pairs)
    shapes = _rs_slot_shapes(pairs, kinds)

    def body(*refs):
        sends, arrivals = _rs_chip_copies(refs[:n], refs[n:2 * n], kinds, shapes, *refs[2 * n:])
        for cp in sends:
            cp.start()
        for cp in arrivals:
            cp.wait_recv()
        for cp in sends:
            cp.wait_send()

    return pl.pallas_call(
        body, name="rs_chips",
        out_shape=[jax.ShapeDtypeStruct((4,) + s, p.dtype) for s, p in zip(shapes, pairs)],
        in_specs=[ANY] * n, out_specs=[ANY] * n,
        scratch_shapes=[pltpu.SemaphoreType.DMA((n, 3)), pltpu.SemaphoreType.DMA((n, 3))],
        compiler_params=_params(),
    )(*pairs)


def _sum_chips(pair, got, kind, pos, name):
    _, r, c = got.shape
    tr = _pick(r, (256, 128, 64, 32, 16))
    nt = r // tr

    def body(pos_ref, own_ref, g1_ref, g2_ref, g3_ref, o_ref):
        o_ref[...] = ((own_ref[...].astype(F32) + g1_ref[0].astype(F32)) + g2_ref[0].astype(F32)) + g3_ref[0].astype(F32)

    if kind == "col":
        own_map = lambda i, p: (i, p[1])
        out_map = lambda i, p: (p[0] * nt + i, 0)
        out_shape = (2 * r, c)
    else:
        own_map = lambda i, p: (p[1] * nt + i, 0)
        out_map = lambda i, p: (i, p[0])
        out_shape = (r, 2 * c)
    peer = lambda m: pl.BlockSpec((1, tr, c), lambda i, p: (p[1] ^ m, i, 0))
    return pl.pallas_call(
        body, name=name,
        grid_spec=pltpu.PrefetchScalarGridSpec(
            num_scalar_prefetch=1, grid=(nt,),
            in_specs=[pl.BlockSpec((tr, c), own_map), peer(1), peer(2), peer(3)],
            out_specs=pl.BlockSpec((tr, c), out_map)),
        out_shape=jax.ShapeDtypeStruct(out_shape, F32),
        compiler_params=_params(("parallel",)),
    )(pos, pair, got, got, got)


def _rs_back(halves, kinds):
    n = len(halves)

    def body(*refs):
        outs = refs[n:2 * n]
        send_sems, recv_sems = refs[2 * n:]
        x, y, c = _mesh_pos()
        cps = []
        for a in range(n):
            mine = _half_of(outs[a], kinds[a], c)
            cps.append(pltpu.make_async_remote_copy(src_ref=mine, dst_ref=mine, send_sem=send_sems.at[a],
                                                    recv_sem=recv_sems.at[a], device_id=(x, y, 1 - c),
                                                    device_id_type=MESH_ID))
            cps[-1].start()
        for a in range(n):
            other = _half_of(outs[a], kinds[a], 1 - c)
            pltpu.make_async_remote_copy(src_ref=other, dst_ref=other, send_sem=send_sems.at[a],
                                         recv_sem=recv_sems.at[a], device_id=(x, y, 1 - c),
                                         device_id_type=MESH_ID).wait_recv()
        for cp in cps:
            cp.wait_send()

    return pl.pallas_call(
        body, name="rs_back",
        out_shape=[jax.ShapeDtypeStruct(h.shape, h.dtype) for h in halves],
        in_specs=[ANY] * n, out_specs=[ANY] * n, input_output_aliases={a: a for a in range(n)},
        scratch_shapes=[pltpu.SemaphoreType.DMA((n,)), pltpu.SemaphoreType.DMA((n,))],
        compiler_params=_params(),
    )(*halves)


def _rope_tables(l, lc):
    rows = l // GRID_W
    n_freq = RET_DH // 4
    inv_freq = ROPE_THETA ** (-jnp.arange(n_freq, dtype=F32) / n_freq)
    sign = jnp.tile(jnp.array([-1.0, 1.0], F32), n_freq)

    def half(n):
        ang = jnp.repeat(jnp.arange(n, dtype=F32)[:, None] * inv_freq, 2, axis=-1)
        return jnp.cos(ang), jnp.sin(ang) * sign

    (cr, sr), (cc, sc) = half(rows), half(GRID_W)
    grid = lambda r, c: jnp.concatenate([jnp.repeat(r, GRID_W, axis=0), jnp.tile(c, (rows, 1))], axis=-1)
    cos_t = jnp.concatenate([jnp.ones((lc, RET_DH), F32), grid(cr, cc)], axis=0)
    sin_t = jnp.concatenate([jnp.zeros((lc, RET_DH), F32), grid(sr, sc)], axis=0)
    return cos_t, sin_t


def _s5_pack(a):
    blk = lambda t: t.reshape(1, S5_NB, 128, S5_STATE)
    lre = jnp.stack([a["s5_lambda_re_f"][0], a["s5_lambda_re_b"][0]]).reshape(2, S5_NB, 8, S5_STATE)
    lim = jnp.stack([a["s5_lambda_im_f"][0], a["s5_lambda_im_b"][0]]).reshape(2, S5_NB, 8, S5_STATE)
    lst = jnp.stack([a["s5_log_step_f"][0], a["s5_log_step_b"][0]]).reshape(2, S5_NB, 8, 1)
    b_re = blk(a["s5_b_re"][0].transpose(0, 2, 1))
    b_im = blk(a["s5_b_im"][0].transpose(0, 2, 1))
    return (lre, lim, lst, b_re, b_im, blk(a["s5_c_re"][0]), blk(a["s5_c_im"][0]),
            a["s5_d"].reshape(1, S5_NB, 1, 128))


def _s5_unpack(g):
    glre, glim, glst, gbre, gbim, gcre, gcim, gd = g
    unb = lambda t: t.reshape(S5_GROUPS, S5_GROUP, S5_STATE).transpose(0, 2, 1)[None]
    return {
        "s5_lambda_re_f": glre[0].reshape(1, S5_GROUPS, S5_STATE), "s5_lambda_re_b": glre[1].reshape(1, S5_GROUPS, S5_STATE),
        "s5_lambda_im_f": glim[0].reshape(1, S5_GROUPS, S5_STATE), "s5_lambda_im_b": glim[1].reshape(1, S5_GROUPS, S5_STATE),
        "s5_log_step_f": glst[0].reshape(1, S5_GROUPS), "s5_log_step_b": glst[1].reshape(1, S5_GROUPS),
        "s5_b_re": unb(gbre), "s5_b_im": unb(gbim),
        "s5_c_re": gcre.reshape(1, S5_GROUPS, S5_GROUP, S5_STATE), "s5_c_im": gcim.reshape(1, S5_GROUPS, S5_GROUP, S5_STATE),
        "s5_d": gd.reshape(1, S5_WIDTH),
    }


def _local_step(a, early, late, mx, mc, conv_w, ci):
    x, ctx, tgt = a["x"][0], a["ctx"][0], a["loss_target"][0]
    l, lc = x.shape[0], ctx.shape[0]
    la = l + lc
    nct, ncc, nrc, cn = lc // TOK_TILE, lc // S5_T, lc // RET_CHUNK, la // S5_T
    n1w, n2w, fnw = a["norm1_w"], a["norm2_w"], a["final_norm_w"].reshape(1, D_MODEL)
    conv_b, b_glu = a["conv_b"], a["s5_b_glu"]
    ld2 = jnp.concatenate([a["ret_log_decay_f"], a["ret_log_decay_b"]], axis=0)
    mod4 = jnp.concatenate([mc[0:2], mx[0:2]], axis=0)
    mod3 = mx[2:5]
    gate5 = mx[5:6]
    cos_t, sin_t = _rope_tables(l, lc)
    s5p = _s5_pack(a)

    gg, xw, yw, a16, *early = _s5_gen(*s5p, early, EARLY_KINDS)
    wb = dict(zip(EARLY_NAMES, _gather_sibling(early, EARLY_KINDS, "gather_sibling_early")))
    p_all, h1b, p3, w_up_p = _norm_inproj(x, ctx, n1w, mod4, wb["w_in"], cos_t, sin_t, [late[1]], (LATE_KINDS[1],))
    sloc = _s5_to_states(p3, xw, "s5_state")
    a16s = a16.transpose(1, 0, 2, 3).reshape(2, S5_GROUPS, 128)
    hs, w_out_p, w_down_p = _s5_scan(sloc, a16s, ncc, [late[0], late[2]], (LATE_KINDS[0], LATE_KINDS[2]))
    y_all = _s5_from_states(p3, gg, hs, yw, False, "s5_out").reshape(la, S5_WIDTH)
    tabs = _ret_tables(ld2)
    of, ob, ssf, ssb = _ret_scan(p_all, tabs, nrc)
    wb = {**wb, **dict(zip(LATE_NAMES, _gather_sibling([w_out_p, w_up_p, w_down_p], LATE_KINDS, "gather_sibling_late")))}
    x1, mix, h2b, up, mixb, y_ret = _outproj_up(x, y_all, of, ob, p_all, wb["s5_w_glu"], b_glu, wb["w_out"],
                                                     mod3, n2w, wb["w_up"], nct)
    act, dx2, ddn, dact, acc_f = _ffn_loss(up, x1, conv_w, conv_b, wb["w_down"], gate5, fnw, tgt)

    g = {}
    g["w_down"] = _mm_tn(act, ddn, name="gw_down")
    dup, acc_c = _convglu_bwd(up, dact, conv_w, conv_b)
    g["w_up"] = _mm_tn(h2b, dup, name="gw_up")
    first = [g[n] for n in FIRST_GRADS]
    first_pairs = [_pair_sum(gf, rv, k, ci, "rs_pair_" + n)
                   for gf, rv, k, n in zip(first, _rs_sibling(first, FIRST_KINDS, "rs_sibling_first"), FIRST_KINDS, FIRST_GRADS)]
    dx1, dmixb, acc_2, dy3, dy_ret, dg, g["s5_w_glu"], g["s5_b_glu"], *first_got = _up_bwd(
        dup, wb["w_up"], wb["w_out"], x1, dx2, mix, mod3, n2w, y_all, y_ret, p_all, wb["s5_w_glu"], b_glu,
        jnp.zeros(p3.shape, BF16), nct, first_pairs, FIRST_KINDS)
    g["w_out"] = _mm_tn(mixb, dmixb, name="gw_out")

    e = _s5_to_states(dy3, yw, "s5_bwd_h")
    ds, da16 = _s5_scan_bwd(e, hs, a16s, ncc)
    du = _s5_from_states(dy3, gg, ds, xw, True, "s5_bwd_u").reshape(la, S5_WIDTH)
    dkb = _s5_bwd_kb(p3, dy3)
    dwst = _s5_bwd_w(p3, ds, "s5_bwd_wst")
    dwout = _s5_bwd_w(dy3, hs, "s5_bwd_wout")
    da16p = da16.reshape(2, S5_NB, 8, 128).transpose(1, 0, 2, 3)
    g.update(_s5_unpack(_s5_gen_bwd(*s5p, dkb, dwst, dwout, da16p)))

    dqf, dkf, dvf, dqb, dkb_, dvb, dld = _ret_scan_bwd(p_all, tabs, ssf, ssb, dy_ret, nrc)
    g["ret_log_decay_f"] = dld[0, :, 0, 0].reshape(1, RET_HEADS)
    g["ret_log_decay_b"] = dld[1, :, 0, 0].reshape(1, RET_HEADS)
    dp, grad_x, acc_1 = _in_bwd(dqf, dkf, dvf, dqb, dkb_, dvb, du, dg, cos_t, sin_t, wb["w_in"], x, ctx, n1w, mod4, dx1)
    g["norm1_w"], g["norm2_w"], g["final_norm_w"] = acc_1[0:1], acc_2[0:1], acc_f[0]
    g["conv_w"], g["conv_b"] = acc_c[0:3], acc_c[3:4]
    zero = jnp.zeros((1, D_MODEL), F32)
    dmx = jnp.concatenate([acc_1[3:5], acc_2[1:2], acc_2[2:4], acc_f[1:2]], axis=0)
    dmc = jnp.concatenate([acc_1[1:3], zero, zero, zero, zero], axis=0)
    dm_pair = jnp.concatenate([dmx.reshape(1, -1), dmc.reshape(1, -1), jnp.zeros((6, 6 * D_MODEL), F32)], axis=0)
    small = _pack_rows([g[n] for n in SMALL_NAMES])
    g["w_in"], dm_all, small_all = _mm_tn(h1b, dp, name="gw_in", gathered=[dm_pair, small])
    return acc_f[2, 0], grad_x, g, dm_all, small_all, first_pairs, first_got


WEIGHT_NAMES = ("c_ctx", "w_mod", "b_mod", "norm1_w", "w_in", "s5_lambda_re_f", "s5_lambda_im_f", "s5_log_step_f",
                "s5_lambda_re_b", "s5_lambda_im_b", "s5_log_step_b", "s5_b_re", "s5_b_im", "s5_c_re", "s5_c_im",
                "s5_d", "s5_w_glu", "s5_b_glu", "ret_log_decay_f", "ret_log_decay_b", "w_out", "norm2_w", "w_up",
                "conv_w", "conv_b", "w_down", "final_norm_w")
BIG_NAMES = ("w_in", "w_out", "w_up", "w_down", "s5_w_glu")
BIG_KINDS = ("col", "row", "col", "row", "row")
EARLY_NAMES, EARLY_KINDS = ("w_in", "s5_w_glu"), ("col", "row")
LATE_NAMES, LATE_KINDS = ("w_out", "w_up", "w_down"), ("row", "col", "row")
FIRST_GRADS, FIRST_KINDS = ("w_down", "w_up"), ("row", "col")
LAST_GRADS, LAST_KINDS = ("w_in", "w_out", "s5_w_glu"), ("col", "row", "row")
SMALL_NAMES = ("norm1_w", "norm2_w", "final_norm_w", "conv_b", "conv_w", "s5_lambda_re_f", "s5_lambda_im_f",
               "s5_log_step_f", "s5_lambda_re_b", "s5_lambda_im_b", "s5_log_step_b", "s5_b_re", "s5_b_im", "s5_c_re",
               "s5_c_im", "s5_d", "s5_b_glu", "ret_log_decay_f", "ret_log_decay_b")
ROW = 1024
N_CHIPS = 4


def _pack_rows(parts):
    flat = jnp.concatenate([p.reshape(-1) for p in parts])
    n = flat.shape[0]
    rows = -(-n // (8 * ROW)) * 8
    return jnp.pad(flat, (0, rows * ROW - n)).reshape(rows, ROW)


def _unpack_rows(packed, shapes):
    flat = packed.reshape(-1)
    out, off = [], 0
    for s in shapes:
        n = math.prod(s)
        out.append(flat[off:off + n].reshape(s))
        off += n
    return out


def _step(a):
    xi, yi, ci = _mesh_pos()
    chip = 2 * xi + yi
    dev = 2 * chip + ci

    cw_loc = a["conv_w"].reshape(-1)
    small_in = jnp.concatenate([a["c"].reshape(-1), jnp.pad(cw_loc, (0, 24 * 128 - cw_loc.shape[0]))]).reshape(32, 128)
    sg = _all_gather8(small_in, "gather_cond").reshape(8, 32, 128)
    c_all = sg[:, 0:8].reshape(8, D_MODEL)
    conv_w = sg[0::2, 8:32].reshape(N_CHIPS, -1)[:, :cw_loc.shape[0]].reshape(N_CHIPS, 3, -1)
    conv_w = conv_w.transpose(1, 0, 2).reshape(3, D_FF)

    placed = {n: _place_shard(a[n][0], k, chip, "place_" + n) for n, k in zip(BIG_NAMES, BIG_KINDS)}
    early = [placed[n] for n in EARLY_NAMES]
    late = [placed[n] for n in LATE_NAMES]

    w_mod_b = a["w_mod"][0].astype(BF16)
    c_ctx = a["c_ctx"].reshape(1, D_MODEL)
    b_loc = lax.dynamic_slice_in_dim(a["b_mod"], chip * MOD_COLS, MOD_COLS, 1)
    m_loc, s_b = _mod_fwd(c_all, c_ctx, w_mod_b, b_loc)
    mg = _all_gather8(m_loc, "gather_mod").reshape(8, MOD_ROWS, MOD_COLS)
    m_full = mg[0::2].transpose(1, 0, 2).reshape(MOD_ROWS, 6 * D_MODEL)
    mx = lax.dynamic_slice_in_dim(m_full, dev, 1, 0).reshape(6, D_MODEL)
    mc = m_full[8].reshape(6, D_MODEL)

    loss_part, grad_x, g, dm_all, small_all, first_pairs, first_got = _local_step(a, early, late, mx, mc, conv_w, ci)
    loss = lax.psum(loss_part, ("x", "y", "c"))

    dm16, gb_mod = _mod_bwd_sum(dm_all.reshape(8, 8, 6 * D_MODEL))
    dm_loc = lax.dynamic_slice_in_dim(dm16, chip * MOD_COLS, MOD_COLS, 1)
    gw_mod, gcc = _mod_bwd_w(dm_loc, s_b, c_ctx, w_mod_b)

    tot = _sum_slots(small_all.reshape(8, -1, ROW), "sum_small_grads")
    small = dict(zip(SMALL_NAMES, _unpack_rows(tot, [g[n].shape for n in SMALL_NAMES])))
    grads = {n: small[n].reshape(a[n].shape) for n in SMALL_NAMES if n != "conv_w"}
    gcc_tot = _sum_slots(_all_gather8(gcc, "gather_c_ctx").reshape(8, 8, D_MODEL), "sum_c_ctx")
    grads["c_ctx"] = (0.5 * gcc_tot[0]).reshape(a["c_ctx"].shape)
    grads["conv_w"] = lax.dynamic_slice_in_dim(small["conv_w"], chip * (D_FF // N_CHIPS), D_FF // N_CHIPS, 1)[None]
    grads["b_mod"] = gb_mod
    grads["w_mod"] = gw_mod[None]

    last = [g[n] for n in LAST_GRADS]
    last_pairs = [_pair_sum(gf, rv, k, ci, "rs_pair_" + n)
                  for gf, rv, k, n in zip(last, _rs_sibling(last, LAST_KINDS, "rs_sibling_last"), LAST_KINDS, LAST_GRADS)]
    last_got = _rs_chips(last_pairs, LAST_KINDS)
    pos = jnp.stack([ci, chip])
    order = FIRST_GRADS + LAST_GRADS
    order_kinds = FIRST_KINDS + LAST_KINDS
    halves = [_sum_chips(p, t, k, pos, "rs_sum_" + n)
              for p, t, k, n in zip(first_pairs + last_pairs, list(first_got) + list(last_got), order_kinds, order)]
    for n, t in zip(order, _rs_back(halves, order_kinds)):
        grads[n] = t[None]

    delta, new_m, new_v = {}, {}, {}
    for n in BIG_NAMES + ("w_mod",):
        for dst, t in zip((delta, new_m, new_v), _adamw(a[n][0], grads[n][0], a["m_" + n][0], a["v_" + n][0], "adamw_" + n)):
            dst[n] = t[None]
    rest = [n for n in WEIGHT_NAMES if n not in BIG_NAMES and n != "w_mod"]
    shapes = [a[n].shape for n in rest]
    pr = lambda pre: _pack_rows([a[pre + n] for n in rest])
    for dst, t in zip((delta, new_m, new_v),
                      _adamw(pr(""), _pack_rows([grads[n] for n in rest]), pr("m_"), pr("v_"), "adamw_small")):
        dst.update(zip(rest, _unpack_rows(t, shapes)))

    return (loss, grad_x[None], *[grads[n] for n in WEIGHT_NAMES], *[delta[n] for n in WEIGHT_NAMES],
            *[new_m[n] for n in WEIGHT_NAMES], *[new_v[n] for n in WEIGHT_NAMES])


def kernel(x, c, ctx, c_ctx, w_mod, b_mod, norm1_w, w_in, s5_lambda_re_f, s5_lambda_im_f, s5_log_step_f, s5_lambda_re_b, s5_lambda_im_b, s5_log_step_b, s5_b_re, s5_b_im, s5_c_re, s5_c_im, s5_d, s5_w_glu, s5_b_glu, ret_log_decay_f, ret_log_decay_b, w_out, norm2_w, w_up, conv_w, conv_b, w_down, final_norm_w, loss_target, m_c_ctx, m_w_mod, m_b_mod, m_norm1_w, m_w_in, m_s5_lambda_re_f, m_s5_lambda_im_f, m_s5_log_step_f, m_s5_lambda_re_b, m_s5_lambda_im_b, m_s5_log_step_b, m_s5_b_re, m_s5_b_im, m_s5_c_re, m_s5_c_im, m_s5_d, m_s5_w_glu, m_s5_b_glu, m_ret_log_decay_f, m_ret_log_decay_b, m_w_out, m_norm2_w, m_w_up, m_conv_w, m_conv_b, m_w_down, m_final_norm_w, v_c_ctx, v_w_mod, v_b_mod, v_norm1_w, v_w_in, v_s5_lambda_re_f, v_s5_lambda_im_f, v_s5_log_step_f, v_s5_lambda_re_b, v_s5_lambda_im_b, v_s5_log_step_b, v_s5_b_re, v_s5_b_im, v_s5_c_re, v_s5_c_im, v_s5_d, v_s5_w_glu, v_s5_b_glu, v_ret_log_decay_f, v_ret_log_decay_b, v_w_out, v_norm2_w, v_w_up, v_conv_w, v_conv_b, v_w_down, v_final_norm_w):
    return _step(dict(locals()))
```

```python
import functools
import math

import jax
import jax.numpy as jnp
from jax import lax
from jax.experimental import pallas as pl
from jax.experimental.pallas import tpu as pltpu

F32 = jnp.float32
BF16 = jnp.bfloat16

D_MODEL = 1024
S5_WIDTH = 512
S5_GROUPS = 32
S5_GROUP = 16
S5_STATE = 64
RET_WIDTH = 512
RET_HEADS = 4
RET_DH = 128
RET_CHUNK = 256
GRID_W = 64
ROPE_THETA = 10000.0
D_FF = 2816
NORM_EPS = 1e-6
IN_COLS = S5_WIDTH + 4 * RET_WIDTH

S5_T = 16
S5_NB = 4
S5_BW = S5_T * 128
S5_SW = 8 * 2 * S5_STATE

ADAM_LR, ADAM_B1, ADAM_B2, ADAM_EPS, ADAM_WD, ADAM_STEP = 0.001, 0.9, 0.999, 1e-08, 0.01, 10

VMEM_LIMIT = 56 * 1024 * 1024
MM_TN_VMEM = 40 * 1024 * 1024
MESH_ID = pl.DeviceIdType.MESH


def _params(sem=None):
    return pltpu.CompilerParams(dimension_semantics=sem, vmem_limit_bytes=VMEM_LIMIT)


def _full(shape):
    n = len(shape)
    return pl.BlockSpec(shape, lambda *_: (0,) * n)


def _dot(a, b):
    return jnp.dot(a, b, preferred_element_type=F32)


def _dot_nt(a, b):
    return lax.dot_general(a, b, (((1,), (1,)), ((), ())), preferred_element_type=F32)


def _dot_tn(a, b):
    return lax.dot_general(a, b, (((0,), (0,)), ((), ())), preferred_element_type=F32)


def _dot_hi(a, b):
    return jnp.dot(a, b, preferred_element_type=F32, precision=lax.Precision.HIGHEST)


def _dot_nt_hi(a, b):
    return lax.dot_general(a, b, (((1,), (1,)), ((), ())), preferred_element_type=F32,
                           precision=lax.Precision.HIGHEST)


def _gelu(x):
    return 0.5 * x * (1.0 + jnp.tanh(0.7978845608028654 * (x + 0.044715 * (x * x * x))))


def _gelu_and_grad(x):
    c, ca = 0.7978845608028654, 0.7978845608028654 * 0.044715
    x2 = x * x
    t = jnp.tanh(x * (c + ca * x2))
    h = 0.5 * x
    return h + h * t, 0.5 + 0.5 * t + h * (1.0 - t * t) * (c + 3.0 * ca * x2)


def _sigmoid(x):
    return 1.0 / (1.0 + jnp.exp(-x))


def _silu(x):
    return x * _sigmoid(x)


def _rms_mod(x, nw, sh, sc):
    r = lax.rsqrt(jnp.mean(x * x, axis=-1, keepdims=True) + NORM_EPS)
    return (x * r * nw) * (1.0 + sc) + sh


def _rms(x, nw):
    r = lax.rsqrt(jnp.mean(x * x, axis=-1, keepdims=True) + NORM_EPS)
    return x * r * nw


def _head_norm_gate(y, g):
    mu = jnp.mean(y, axis=-1, keepdims=True)
    yc = y - mu
    var = jnp.mean(yc * yc, axis=-1, keepdims=True)
    return _silu(g) * (yc * lax.rsqrt(var + NORM_EPS))


def _swap_pairs(t):
    lane = lax.broadcasted_iota(jnp.int32, t.shape, 1)
    return jnp.where(lane % 2 == 0, pltpu.roll(t, RET_DH - 1, 1), pltpu.roll(t, 1, 1))


def _rope(t, cos_t, sin_t):
    return t * cos_t + _swap_pairs(t) * sin_t


def _rope_t(dt, cos_t, sin_t):
    return dt * cos_t + _swap_pairs(dt * sin_t)


def _pick(n, prefs):
    for p in prefs:
        if n % p == 0:
            return p
    return n


def _mm_tn(a, b, *, name, gathered=None):
    m, k = a.shape
    n = b.shape[1]
    tn = _pick(n, (1408, 1024, 1280, 512))
    fits = lambda t: 2 * (2 * t * k + 2 * t * tn + 4 * k * tn) <= MM_TN_VMEM
    tm = _pick(m, [t for t in (2816, 2048, 1024, 768, 512, 256) if fits(t)] + [128])
    nj, ni = n // tn, m // tm

    def product(a_ref, b_ref, o_ref):
        @pl.when(pl.program_id(1) == 0)
        def _():
            o_ref[...] = jnp.zeros_like(o_ref)
        o_ref[...] += _dot_tn(a_ref[...], b_ref[...])

    specs = dict(
        grid=(nj, ni),
        in_specs=[pl.BlockSpec((tm, k), lambda j, i: (i, 0)), pl.BlockSpec((tm, tn), lambda j, i: (i, j))],
        out_specs=pl.BlockSpec((k, tn), lambda j, i: (0, j)),
        out_shape=jax.ShapeDtypeStruct((k, n), F32))
    if gathered is None:
        def body(a_ref, b_ref, o_ref):
            product(a_ref, b_ref, o_ref)

        return pl.pallas_call(body, name=name, compiler_params=_params(("parallel", "arbitrary")), **specs)(a, b)

    ng = len(gathered)

    def body_gather(a_ref, b_ref, *rest):
        v_refs, o_ref, all_refs, sems = rest[:ng], rest[ng], rest[ng + 1:2 * ng + 1], rest[2 * ng + 1:]
        phases = [_gather8_phases(v_refs[q], all_refs[q], *sems[3 * q:3 * q + 3], gathered[q].shape[0])
                  for q in range(ng)]
        step = pl.program_id(0) * ni + pl.program_id(1)

        @pl.when(step == 0)
        def _():
            for begin, _ in phases:
                begin()

        product(a_ref, b_ref, o_ref)

        @pl.when(step == nj * ni - 1)
        def _():
            for _, finish in phases:
                finish()

    specs["in_specs"] = specs["in_specs"] + [ANY] * ng
    specs["out_specs"] = [specs["out_specs"]] + [ANY] * ng
    specs["out_shape"] = [specs["out_shape"]] + [jax.ShapeDtypeStruct((8 * v.shape[0], v.shape[1]), v.dtype)
                                                 for v in gathered]
    return pl.pallas_call(body_gather, name=name, scratch_shapes=list(GATHER8_SCRATCH) * ng,
                          compiler_params=_params(("arbitrary", "arbitrary")), **specs)(a, b, *gathered)


TOK_TILE = 256


def _behind(step, last, copies):
    @pl.when(step == 0)
    def _():
        for cp in copies(with_arrivals=False)[0]:
            cp.start()

    @pl.when(step == last)
    def _():
        sends, arrivals = copies()
        for cp in arrivals:
            cp.wait_recv()
        for cp in sends:
            cp.wait_send()


def _s5_put_rows(rows_ref, scr, val):
    nchunk = scr.shape[0]
    for c in range(nchunk):
        scr[c] = val[S5_T * c:S5_T * (c + 1), :]
    for b in range(S5_NB):
        for j in range(S5_T):
            rows_ref[b, :, 128 * j:128 * (j + 1)] = scr[:, j, 128 * b:128 * (b + 1)].astype(BF16)


def _norm_inproj(x, ctx, n1w, mod4, w_in_b, cos_t, sin_t, placed, kinds):
    l, lc = x.shape[0], ctx.shape[0]
    tm = TOK_TILE
    nct = lc // tm
    la = l + lc
    n = len(placed)
    shard_shapes = _gather_shard_shapes(placed, kinds)

    def body(x_ref, c_ref, nw_ref, mod_ref, w_ref, cos_ref, sin_ref, *rest):
        p_ref, h_ref, u_ref = rest[n:n + 3]
        send_sems, recv_sems, u_scr = rest[2 * n + 3:]
        _behind(pl.program_id(0), la // tm - 1,
                functools.partial(_gather_chip_copies, rest[n + 3:2 * n + 3], kinds, shard_shapes, send_sems, recv_sems))
        is_ctx = pl.program_id(0) < nct
        xt = jnp.where(is_ctx, c_ref[...], x_ref[...])
        sh = jnp.where(is_ctx, mod_ref[0:1, :], mod_ref[2:3, :])
        sc = jnp.where(is_ctx, mod_ref[1:2, :], mod_ref[3:4, :])
        hb = _rms_mod(xt, nw_ref[...], sh, sc).astype(BF16)
        h_ref[...] = hb
        p = _dot(hb, w_ref[...])
        p_ref[...] = p
        cs, sn = cos_ref[...], sin_ref[...]
        for h in range(RET_HEADS):
            q_cols = slice(RET_WIDTH + RET_DH * h, RET_WIDTH + RET_DH * (h + 1))
            k_cols = slice(2 * RET_WIDTH + RET_DH * h, 2 * RET_WIDTH + RET_DH * (h + 1))
            p_ref[:, q_cols] = _rope(p[:, q_cols], cs, sn)
            p_ref[:, k_cols] = _rope(p[:, k_cols] * K_SCALE, cs, sn)
        _s5_put_rows(u_ref, u_scr, p[:, 0:S5_WIDTH])

    return pl.pallas_call(
        body, name="norm_inproj", grid=(la // tm,),
        in_specs=[pl.BlockSpec((tm, D_MODEL), lambda i: (jnp.maximum(i - nct, 0), 0)),
                  pl.BlockSpec((tm, D_MODEL), lambda i: (jnp.minimum(i, nct - 1), 0)),
                  _full((1, D_MODEL)), _full((4, D_MODEL)), _full((D_MODEL, IN_COLS)),
                  pl.BlockSpec((tm, RET_DH), lambda i: (i, 0)), pl.BlockSpec((tm, RET_DH), lambda i: (i, 0))] + [ANY] * n,
        out_specs=[pl.BlockSpec((tm, IN_COLS), lambda i: (i, 0)), pl.BlockSpec((tm, D_MODEL), lambda i: (i, 0)),
                   pl.BlockSpec((S5_NB, tm // S5_T, S5_BW), lambda i: (0, i, 0))] + [ANY] * n,
        out_shape=[jax.ShapeDtypeStruct((la, IN_COLS), F32), jax.ShapeDtypeStruct((la, D_MODEL), BF16),
                   jax.ShapeDtypeStruct((S5_NB, la // S5_T, S5_BW), BF16)]
        + [jax.ShapeDtypeStruct(p.shape, p.dtype) for p in placed],
        input_output_aliases={7 + a: 3 + a for a in range(n)},
        scratch_shapes=[pltpu.SemaphoreType.DMA((n, 3)), pltpu.SemaphoreType.DMA((n, 3)),
                        pltpu.VMEM((tm // S5_T, S5_T, S5_WIDTH), F32)],
        compiler_params=_params(("arbitrary",)),
    )(x, ctx, n1w, mod4, w_in_b, cos_t, sin_t, *placed)


def _iota2(shape, dim):
    return lax.broadcasted_iota(jnp.int32, shape, dim)


def _group_mask(rows, cols, row_div, col_div):
    return jnp.where(_iota2((rows, cols), 0) // row_div == _iota2((rows, cols), 1) // col_div, 1.0, 0.0).astype(F32)


def _s5_gen_dir(lre, lim, lst, b_re, b_im, c_re, c_im):
    step = jnp.exp(lst)
    mag = jnp.exp(lre * step)
    ar = mag * jnp.cos(lim * step)
    ai = mag * jnp.sin(lim * step)
    den = lre * lre + lim * lim
    xr = ar - 1.0
    cr = (xr * lre + ai * lim) / den
    ci = (ai * lre - xr * lim) / den
    rexp = _group_mask(128, 8, S5_GROUP, 1)
    are, aie = _dot_hi(rexp, ar), _dot_hi(rexp, ai)
    cre, cie = _dot_hi(rexp, cr), _dot_hi(rexp, ci)
    bbr = cre * b_re - cie * b_im
    bbi = cre * b_im + cie * b_re
    gmask = _group_mask(128, 128, S5_GROUP, S5_GROUP)
    pr, pi = jnp.ones_like(are), jnp.zeros_like(are)
    xs, ys = [], []
    for t in range(S5_T + 1):
        if t < S5_T:
            xs.append(jnp.concatenate([bbr * pr - bbi * pi, bbr * pi + bbi * pr], axis=1))
        ys.append(jnp.concatenate([c_re * pr - c_im * pi, -(c_re * pi + c_im * pr)], axis=1))
        pr, pi = pr * are - pi * aie, pr * aie + pi * are
    gs = [_dot_nt_hi(x_t, ys[0]) * gmask for x_t in xs]
    r16, i16 = ar, ai
    for _ in range(4):
        r16, i16 = r16 * r16 - i16 * i16, 2.0 * r16 * i16
    return xs, ys, gs, jnp.concatenate([r16, i16], axis=1)


def _s5_expand(z):
    return jnp.concatenate([z] * 8, axis=1) * _group_mask(128, S5_SW, S5_GROUP, 128)


def _s5_contract(z):
    zm = z * _group_mask(128, S5_SW, S5_GROUP, 128)
    acc = zm[:, 0:128]
    for k in range(1, 8):
        acc = acc + zm[:, 128 * k:128 * (k + 1)]
    return acc


def _s5_param_specs():
    blk3 = lambda r, c: pl.BlockSpec((1, 1, r, c), lambda b, *_: (0, b, 0, 0))
    dir3 = lambda r, c: pl.BlockSpec((2, 1, r, c), lambda b, *_: (0, b, 0, 0))
    return [dir3(8, S5_STATE), dir3(8, S5_STATE), dir3(8, 1), blk3(128, S5_STATE), blk3(128, S5_STATE),
            blk3(128, S5_STATE), blk3(128, S5_STATE), blk3(1, 128)]


def _s5_gen(lre, lim, lst, b_re, b_im, c_re, c_im, dvec, placed, kinds):
    n = len(placed)
    shard_shapes = _gather_shard_shapes(placed, kinds)

    def body(lre_ref, lim_ref, lst_ref, bre_ref, bim_ref, cre_ref, cim_ref, d_ref, *rest):
        gg_ref, xw_ref, yw_ref, a16_ref = rest[n:n + 4]
        _behind(pl.program_id(0), S5_NB - 1,
                functools.partial(_gather_chip_copies, rest[n + 4:2 * n + 4], kinds, shard_shapes, *rest[2 * n + 4:]))
        eye = _group_mask(128, 128, 1, 1)
        g0 = eye * d_ref[0, 0]
        for dr in range(2):
            xs, ys, gs, a16 = _s5_gen_dir(lre_ref[dr, 0], lim_ref[dr, 0], lst_ref[dr, 0], bre_ref[0, 0],
                                          bim_ref[0, 0], cre_ref[0, 0], cim_ref[0, 0])
            a16_ref[0, dr] = a16
            for j in range(S5_T):
                xw_ref[0, dr, j] = xs[S5_T - 1 - j if dr == 0 else j]
                yw_ref[0, dr, j] = ys[j + 1 if dr == 0 else S5_T - j]
            g0 = g0 + gs[0]
            for t in range(1, S5_T):
                gg_ref[0, (S5_T - 1) + t if dr == 0 else (S5_T - 1) - t] = gs[t]
        gg_ref[0, S5_T - 1] = g0

    blk = pl.BlockSpec((1, 2, S5_T, 128, 128), lambda b: (b, 0, 0, 0, 0))
    return pl.pallas_call(
        body, name="s5_gen", grid=(S5_NB,),
        in_specs=_s5_param_specs() + [ANY] * n,
        out_specs=[pl.BlockSpec((1, 2 * S5_T - 1, 128, 128), lambda b: (b, 0, 0, 0)), blk, blk,
                   pl.BlockSpec((1, 2, 8, 128), lambda b: (b, 0, 0, 0))] + [ANY] * n,
        out_shape=[jax.ShapeDtypeStruct((S5_NB, 2 * S5_T - 1, 128, 128), F32),
                   jax.ShapeDtypeStruct((S5_NB, 2, S5_T, 128, 128), F32),
                   jax.ShapeDtypeStruct((S5_NB, 2, S5_T, 128, 128), F32),
                   jax.ShapeDtypeStruct((S5_NB, 2, 8, 128), F32)]
        + [jax.ShapeDtypeStruct(p.shape, p.dtype) for p in placed],
        input_output_aliases={8 + a: 4 + a for a in range(n)},
        scratch_shapes=[pltpu.SemaphoreType.DMA((n, 3)), pltpu.SemaphoreType.DMA((n, 3))],
        compiler_params=_params(("arbitrary",)),
    )(lre, lim, lst, b_re, b_im, c_re, c_im, dvec, *placed)


def _s5_fill_state_mat(w_scr, src_ref, dr):
    for j in range(S5_T):
        w_scr[128 * j:128 * (j + 1), :] = _s5_expand(src_ref[0, dr, j]).astype(BF16)


def _s5_fill_toeplitz(k_scr, gg_ref):
    for j in range(S5_T):
        for i in range(S5_T):
            k_scr[128 * j:128 * (j + 1), 128 * i:128 * (i + 1)] = gg_ref[0, i - j + (S5_T - 1)].astype(BF16)


S5_GEN_SPECS = [pl.BlockSpec((1, 2 * S5_T - 1, 128, 128), lambda b: (b, 0, 0, 0)),
                pl.BlockSpec((1, 2, S5_T, 128, 128), lambda b: (b, 0, 0, 0, 0))]


def _s5_gen_bwd(lre, lim, lst, b_re, b_im, c_re, c_im, dvec, dg, dx, dy, da16):
    def body(lre_ref, lim_ref, lst_ref, bre_ref, bim_ref, cre_ref, cim_ref, d_ref, dg_ref, dx_ref, dy_ref, da16_ref,
             glre_ref, glim_ref, glst_ref, gbre_ref, gbim_ref, gcre_ref, gcim_ref, gd_ref):
        eye = _group_mask(128, 128, 1, 1)
        gd_ref[0, 0] = jnp.sum(dg_ref[0, S5_T - 1] * eye, axis=0, keepdims=True)
        gb = [None, None, None, None]
        for dr in range(2):
            args = (lre_ref[dr, 0], lim_ref[dr, 0], lst_ref[dr, 0], bre_ref[0, 0], bim_ref[0, 0],
                    cre_ref[0, 0], cim_ref[0, 0])
            _, vjp = jax.vjp(_s5_gen_dir, *args)
            dxs = [dx_ref[0, dr, S5_T - 1 - t if dr == 0 else t] for t in range(S5_T)]
            dys = [jnp.zeros((128, 128), F32)] + [dy_ref[0, dr, t - 1 if dr == 0 else S5_T - t]
                                                  for t in range(1, S5_T + 1)]
            dgs = [dg_ref[0, (S5_T - 1) + t if dr == 0 else (S5_T - 1) - t] for t in range(S5_T)]
            g = vjp((dxs, dys, dgs, da16_ref[0, dr]))
            glre_ref[dr, 0] = g[0]
            glim_ref[dr, 0] = g[1]
            glst_ref[dr, 0] = g[2]
            for q in range(4):
                gb[q] = g[3 + q] if gb[q] is None else gb[q] + g[3 + q]
        gbre_ref[0, 0] = gb[0]
        gbim_ref[0, 0] = gb[1]
        gcre_ref[0, 0] = gb[2]
        gcim_ref[0, 0] = gb[3]

    shp = lambda a: jax.ShapeDtypeStruct(a.shape, F32)
    return pl.pallas_call(
        body, name="s5_gen_bwd", grid=(S5_NB,),
        in_specs=_s5_param_specs() + [
            pl.BlockSpec((1, 2 * S5_T - 1, 128, 128), lambda b: (b, 0, 0, 0)),
            pl.BlockSpec((1, 2, S5_T, 128, 128), lambda b: (b, 0, 0, 0, 0)),
            pl.BlockSpec((1, 2, S5_T, 128, 128), lambda b: (b, 0, 0, 0, 0)),
            pl.BlockSpec((1, 2, 8, 128), lambda b: (b, 0, 0, 0))],
        out_specs=_s5_param_specs(),
        out_shape=[shp(lre), shp(lim), shp(lst), shp(b_re), shp(b_im), shp(c_re), shp(c_im), shp(dvec)],
        compiler_params=_params(("parallel",)),
    )(lre, lim, lst, b_re, b_im, c_re, c_im, dvec, dg, dx, dy, da16)


def _s5_put_groups(o_ref, dr, val):
    for gi in range(8):
        o_ref[dr, :, gi, :] = val[:, 128 * gi:128 * (gi + 1)]


def _s5_get_groups(s_ref, dr, n=8):
    return jnp.concatenate([s_ref[dr, :, gi, :] for gi in range(n)], axis=1).astype(BF16)


def _s5_to_states(u3, blocks, name):
    cn = u3.shape[1]

    def body(u_ref, b_ref, o_ref, w_scr):
        u = u_ref[0]
        for dr in range(2):
            _s5_fill_state_mat(w_scr, b_ref, dr)
            _s5_put_groups(o_ref, dr, _dot(u, w_scr[...]))

    return pl.pallas_call(
        body, name=name, grid=(S5_NB,),
        in_specs=[pl.BlockSpec((1, cn, S5_BW), lambda b: (b, 0, 0)), S5_GEN_SPECS[1]],
        out_specs=pl.BlockSpec((2, cn, 8, 128), lambda b: (0, 0, b, 0)),
        out_shape=jax.ShapeDtypeStruct((2, cn, S5_GROUPS, 128), F32),
        scratch_shapes=[pltpu.VMEM((S5_BW, S5_SW), BF16)],
        compiler_params=_params(("parallel",)),
    )(u3, blocks)


def _s5_from_states(u3, gg, st, blocks, transposed, name):
    cn = u3.shape[1]

    def body(u_ref, g_ref, s_ref, b_ref, o_ref, k_scr, w_scr):
        u = u_ref[0]
        _s5_fill_toeplitz(k_scr, g_ref)
        y = _dot_nt(u, k_scr[...]) if transposed else _dot(u, k_scr[...])
        for dr in range(2):
            _s5_fill_state_mat(w_scr, b_ref, dr)
            y = y + _dot_nt(_s5_get_groups(s_ref, dr), w_scr[...])
        for i in range(S5_T):
            o_ref[:, i, :] = y[:, 128 * i:128 * (i + 1)]

    return pl.pallas_call(
        body, name=name, grid=(S5_NB,),
        in_specs=[pl.BlockSpec((1, cn, S5_BW), lambda b: (b, 0, 0)), S5_GEN_SPECS[0],
                  pl.BlockSpec((2, cn, 8, 128), lambda b: (0, 0, b, 0)), S5_GEN_SPECS[1]],
        out_specs=pl.BlockSpec((cn, S5_T, 128), lambda b: (0, 0, b)),
        out_shape=jax.ShapeDtypeStruct((cn, S5_T, S5_WIDTH), F32),
        scratch_shapes=[pltpu.VMEM((S5_BW, S5_BW), BF16), pltpu.VMEM((S5_BW, S5_SW), BF16)],
        compiler_params=_params(("parallel",)),
    )(u3, gg, st, blocks)


def _s5_a_forms(a):
    ra = pltpu.roll(a, S5_STATE, 1)
    low = _iota2(a.shape, 1) < S5_STATE
    return jnp.where(low, a, ra), jnp.where(low, -ra, a)


def _s5_scan(sloc, a16, ncc, placed, kinds):
    cn = sloc.shape[1]
    n = len(placed)
    shard_shapes = _gather_shard_shapes(placed, kinds)

    def body(s_ref, a_ref, *rest):
        h_ref = rest[n]
        sends, arrivals = _gather_chip_copies(rest[n + 1:2 * n + 1], kinds, shard_shapes, *rest[2 * n + 1:])
        for cp in sends:
            cp.start()
        forms = [_s5_a_forms(a_ref[dr]) for dr in range(2)]

        def step(s, hs):
            out = []
            for dr in range(2):
                arr, aii = forms[dr]
                h, rh = hs[dr]
                c = s if dr == 0 else jnp.where(s < ncc, ncc - 1 - s, cn - 1 - (s - ncc))
                h_ref[dr, c] = h
                sc = s_ref[dr, c]
                out.append((h * arr + rh * aii + sc, rh * arr - h * aii + pltpu.roll(sc, S5_STATE, 1)))
            return tuple(out)

        zero = jnp.zeros((S5_GROUPS, 128), F32)
        lax.fori_loop(0, cn, step, ((zero, zero), (zero, zero)), unroll=4)
        for cp in arrivals:
            cp.wait_recv()
        for cp in sends:
            cp.wait_send()

    vmem = pl.BlockSpec(memory_space=pltpu.VMEM)
    return pl.pallas_call(
        body, name="s5_scan",
        in_specs=[vmem, vmem] + [ANY] * n, out_specs=[vmem] + [ANY] * n,
        out_shape=[jax.ShapeDtypeStruct(sloc.shape, F32)] + [jax.ShapeDtypeStruct(p.shape, p.dtype) for p in placed],
        input_output_aliases={2 + a: 1 + a for a in range(n)},
        scratch_shapes=[pltpu.SemaphoreType.DMA((n, 3)), pltpu.SemaphoreType.DMA((n, 3))],
        compiler_params=_params(),
    )(sloc, a16, *placed)


def _s5_scan_bwd(e, hs, a16, ncc):
    cn = e.shape[1]

    def body(e_ref, h_ref, a_ref, ds_ref, da_ref):
        forms = [_s5_a_forms(a_ref[dr]) for dr in range(2)]
        low = _iota2((S5_GROUPS, 128), 1) < S5_STATE

        def step(s, carry):
            out = []
            r = cn - 1 - s
            for dr in range(2):
                arr, aii = forms[dr]
                g, rg, da = carry[dr]
                c = r if dr == 0 else jnp.where(r < ncc, ncc - 1 - r, cn - 1 - (r - ncc))
                ds_ref[dr, c] = g
                h = h_ref[dr, c]
                rh = pltpu.roll(h, S5_STATE, 1)
                da = da + jnp.where(low, g * h + rg * rh, g * rh - rg * h)
                ec = e_ref[dr, c]
                out.append((ec + g * arr - rg * aii, pltpu.roll(ec, S5_STATE, 1) + rg * arr + g * aii, da))
            return tuple(out)

        zero = jnp.zeros((S5_GROUPS, 128), F32)
        res = lax.fori_loop(0, cn, step, ((zero, zero, zero), (zero, zero, zero)), unroll=4)
        da_ref[0] = res[0][2]
        da_ref[1] = res[1][2]

    return pl.pallas_call(
        body, name="s5_scan_bwd",
        out_shape=[jax.ShapeDtypeStruct(e.shape, F32), jax.ShapeDtypeStruct((2, S5_GROUPS, 128), F32)],
        compiler_params=_params(),
    )(e, hs, a16)


def _s5_bwd_kb(p3, dy3):
    cn = p3.shape[1]
    half = S5_T // 2

    def body(u_ref, d_ref, o_ref):
        q = pl.program_id(1)

        @pl.when(q == 0)
        def _():
            o_ref[...] = jnp.zeros_like(o_ref)

        dk = _dot_tn(u_ref[0], d_ref[0])
        for j in range(S5_T):
            for i in range(half):
                o_ref[0, half * q + i - j + (S5_T - 1)] += dk[128 * j:128 * (j + 1), 128 * i:128 * (i + 1)]

    return pl.pallas_call(
        body, name="s5_bwd_kb", grid=(S5_NB, 2),
        in_specs=[pl.BlockSpec((1, cn, S5_BW), lambda b, q: (b, 0, 0)),
                  pl.BlockSpec((1, cn, S5_BW // 2), lambda b, q: (b, 0, q))],
        out_specs=pl.BlockSpec((1, 2 * S5_T - 1, 128, 128), lambda b, q: (b, 0, 0, 0)),
        out_shape=jax.ShapeDtypeStruct((S5_NB, 2 * S5_T - 1, 128, 128), F32),
        compiler_params=_params(("parallel", "arbitrary")),
    )(p3, dy3)


def _s5_bwd_w(u3, st, name):
    cn = u3.shape[1]

    def body(u_ref, s_ref, w_ref):
        dw = _dot_tn(u_ref[0], _s5_get_groups(s_ref, 0))
        for j in range(S5_T):
            w_ref[0, 0, j] = _s5_contract(dw[128 * j:128 * (j + 1), :])

    return pl.pallas_call(
        body, name=name, grid=(S5_NB, 2),
        in_specs=[pl.BlockSpec((1, cn, S5_BW), lambda b, q: (b, 0, 0)),
                  pl.BlockSpec((1, cn, 8, 128), lambda b, q: (q, 0, b, 0))],
        out_specs=pl.BlockSpec((1, 1, S5_T, 128, 128), lambda b, q: (b, q, 0, 0, 0)),
        out_shape=jax.ShapeDtypeStruct((S5_NB, 2, S5_T, 128, 128), F32),
        compiler_params=_params(("parallel", "parallel")),
    )(u3, st)


K_SCALE = RET_DH ** -0.5
G_COL = 16


def _ret_chunk_of(step, ncc, nch, rev):
    if not rev:
        return step
    return jnp.where(step < ncc, ncc - 1 - step, nch - 1 - (step - ncc))


def _ret_decay(ld, rev):
    c = _iota2((RET_CHUNK, RET_CHUNK), 0).astype(F32)
    m = _iota2((RET_CHUNK, RET_CHUNK), 1).astype(F32)
    diff = (m - c) if rev else (c - m)
    keep = (diff > 0) if rev else (diff >= 0)
    expo = jnp.maximum(diff, 0.0)
    dm = jnp.where(keep, jnp.exp(ld * expo), 0.0)
    xi_e = (RET_CHUNK - c) if rev else (c + 1.0)
    zeta_e = c if rev else (RET_CHUNK - 1.0 - c)
    return dm, expo, jnp.exp(ld * xi_e), xi_e, jnp.exp(ld * zeta_e), zeta_e


RET_TABLES = 7


def _ret_tables(ld2):
    def body(ld_ref, t_ref):
        dr, h = pl.program_id(0), pl.program_id(1)
        ldh = ld_ref[dr, h]
        for rev in (False, True):
            @pl.when(dr == int(rev))
            def _(rev=rev):
                dm, expo, xi, xi_e, zeta, zeta_e = _ret_decay(ldh, rev)
                t_ref[0, 0, 0] = dm
                t_ref[0, 0, 1] = dm * expo
                t_ref[0, 0, 2] = xi
                t_ref[0, 0, 3] = xi * xi_e
                t_ref[0, 0, 4] = zeta
                t_ref[0, 0, 5] = zeta * zeta_e
                t_ref[0, 0, 6] = jnp.zeros_like(dm) + jnp.exp(ldh * RET_CHUNK)

    return pl.pallas_call(
        body, name="ret_tables", grid=(2, RET_HEADS),
        in_specs=[pl.BlockSpec(memory_space=pltpu.SMEM)],
        out_specs=pl.BlockSpec((1, 1, RET_TABLES, RET_CHUNK, RET_CHUNK), lambda d, h: (d, h, 0, 0, 0)),
        out_shape=jax.ShapeDtypeStruct((2, RET_HEADS, RET_TABLES, RET_CHUNK, RET_CHUNK), F32),
        compiler_params=_params(("parallel", "parallel")),
    )(ld2)


def _ret_specs(nch, ncc, rev, step_of):
    chunk = lambda n: _ret_chunk_of(step_of(n), ncc, nch, rev)
    cols = [pl.BlockSpec((RET_CHUNK, RET_WIDTH), functools.partial(lambda n, cb: (chunk(n), cb), cb=cb))
            for cb in (1, 2, 3)]
    return cols, pl.BlockSpec((RET_CHUNK, RET_WIDTH), lambda n: (chunk(n), 0))


def _ret_scan(p_all, tabs, ncc):
    la = p_all.shape[0]
    nch = la // RET_CHUNK

    def body(t_ref, qf, kf, vf, qb, kb, vb, of_ref, ob_ref, ssf_ref, ssb_ref, s_scr):
        @pl.when(pl.program_id(0) == 0)
        def _():
            s_scr[...] = jnp.zeros_like(s_scr)

        for dr, (q_ref, k_ref, v_ref, o_ref, ss_ref) in enumerate(
                ((qf, kf, vf, of_ref, ssf_ref), (qb, kb, vb, ob_ref, ssb_ref))):
            for h in range(RET_HEADS):
                sl = slice(RET_DH * h, RET_DH * (h + 1))
                dm, xi, zeta = t_ref[dr, h, 0], t_ref[dr, h, 2, :, 0:RET_DH], t_ref[dr, h, 4, :, 0:RET_DH]
                q, k = q_ref[:, sl], k_ref[:, sl]
                vh = v_ref[:, sl].astype(BF16)
                s = s_scr[dr, h]
                ss_ref[0, h] = s
                sc = (_dot_nt(q.astype(BF16), k.astype(BF16)) * dm).astype(BF16)
                o_ref[:, sl] = _dot(sc, vh) + _dot((q * xi).astype(BF16), s.astype(BF16))
                s_scr[dr, h] = t_ref[dr, h, 6, 0:RET_DH, 0:RET_DH] * s + _dot_tn((k * zeta).astype(BF16), vh)

    in_f, out_f = _ret_specs(nch, ncc, False, lambda n: n)
    in_b, out_b = _ret_specs(nch, ncc, True, lambda n: n)
    ss_spec = pl.BlockSpec((1, RET_HEADS, RET_DH, RET_DH), lambda n: (n, 0, 0, 0))
    o_shape = jax.ShapeDtypeStruct((la, RET_WIDTH), F32)
    ss_shape = jax.ShapeDtypeStruct((nch, RET_HEADS, RET_DH, RET_DH), F32)
    return pl.pallas_call(
        body, name="ret_scan", grid=(nch,),
        in_specs=[_full(tabs.shape)] + in_f + in_b,
        out_specs=[out_f, out_b, ss_spec, ss_spec],
        out_shape=[o_shape, o_shape, ss_shape, ss_shape],
        scratch_shapes=[pltpu.VMEM((2, RET_HEADS, RET_DH, RET_DH), F32)],
        compiler_params=_params(("arbitrary",)),
    )(tabs, p_all, p_all, p_all, p_all, p_all, p_all)


def _ret_scan_bwd(p_all, tabs, ssf, ssb, dy_all, ncc):
    la = p_all.shape[0]
    nch = la // RET_CHUNK

    def body(t_ref, qf, kf, vf, dof, ssf_ref, qb, kb, vb, dob_, ssb_ref,
             dqf, dkf, dvf, dqb, dkb, dvb, dld_ref, ds_scr):
        @pl.when(pl.program_id(0) == 0)
        def _():
            ds_scr[...] = jnp.zeros_like(ds_scr)
            dld_ref[...] = jnp.zeros_like(dld_ref)

        for dr, (q_ref, k_ref, v_ref, do_ref, ss_ref, dq_ref, dk_ref, dv_ref) in enumerate(
                ((qf, kf, vf, dof, ssf_ref, dqf, dkf, dvf), (qb, kb, vb, dob_, ssb_ref, dqb, dkb, dvb))):
            on_ctx = _ret_chunk_of(nch - 1 - pl.program_id(0), ncc, nch, dr == 1) < ncc
            for h in range(RET_HEADS):
                sl = slice(RET_DH * h, RET_DH * (h + 1))
                dm, dm_d = t_ref[dr, h, 0], t_ref[dr, h, 1]
                xi, xi_d, zeta, zeta_d = [t_ref[dr, h, t, :, 0:RET_DH] for t in (2, 3, 4, 5)]
                gc = t_ref[dr, h, 6, 0:RET_DH, 0:RET_DH]
                q, k = q_ref[:, sl], k_ref[:, sl]
                q16, k16, v16 = q.astype(BF16), k.astype(BF16), v_ref[:, sl].astype(BF16)
                s = ss_ref[0, h]
                s16 = s.astype(BF16)
                ds_in = ds_scr[dr, h]
                ds16 = ds_in.astype(BF16)
                do16 = jnp.where(on_ctx, 0.0, do_ref[:, sl]).astype(BF16)
                qk = _dot_nt(q16, k16)
                dsv = _dot_nt(do16, v16)
                dsc = (dsv * dm).astype(BF16)
                sc16 = (qk * dm).astype(BF16)
                dos = _dot_nt(do16, s16)
                vds = _dot_nt(v16, ds16)
                dq_ref[:, sl] = _dot(dsc, k16) + dos * xi
                dk_ref[:, sl] = _dot_tn(dsc, q16) + vds * zeta
                dv_ref[:, sl] = _dot_tn(sc16, do16) + _dot((k * zeta).astype(BF16), ds16)
                ds_scr[dr, h] = _dot_tn((q * xi).astype(BF16), do16) + gc * ds_in
                dld = (jnp.sum(dsv * qk * dm_d) + jnp.sum(q * dos * xi_d + k * vds * zeta_d)
                       + RET_CHUNK * jnp.sum(gc * s * ds_in))
                dld_ref[dr, h] += dld

    back = lambda n: nch - 1 - n
    in_f, out_f = _ret_specs(nch, ncc, False, back)
    in_b, out_b = _ret_specs(nch, ncc, True, back)
    ss_spec = pl.BlockSpec((1, RET_HEADS, RET_DH, RET_DH), lambda n: (nch - 1 - n, 0, 0, 0))
    shp = jax.ShapeDtypeStruct((la, RET_WIDTH), F32)
    dy_spec = lambda rev: pl.BlockSpec(
        (RET_CHUNK, RET_WIDTH), lambda n: (jnp.maximum(_ret_chunk_of(nch - 1 - n, ncc, nch, rev) - ncc, 0), 0))
    return pl.pallas_call(
        body, name="ret_scan_bwd", grid=(nch,),
        in_specs=[_full(tabs.shape)] + in_f + [dy_spec(False), ss_spec] + in_b + [dy_spec(True), ss_spec],
        out_specs=[out_f, out_f, out_f, out_b, out_b, out_b, _full((2, RET_HEADS, 8, 128))],
        out_shape=[shp] * 6 + [jax.ShapeDtypeStruct((2, RET_HEADS, 8, 128), F32)],
        scratch_shapes=[pltpu.VMEM((2, RET_HEADS, RET_DH, RET_DH), F32)],
        compiler_params=_params(("arbitrary",)),
    )(tabs, p_all, p_all, p_all, dy_all, ssf, p_all, p_all, p_all, dy_all, ssb)


def _in_bwd(dqf, dkf, dvf, dqb, dkb, dvb, du, dg, cos_t, sin_t, w_in_b, x, ctx, n1w, mod4, dx1):
    l, lc = x.shape[0], ctx.shape[0]
    la = l + lc
    tm = TOK_TILE
    nct = lc // tm

    def body(dqf_ref, dkf_ref, dvf_ref, dqb_ref, dkb_ref, dvb_ref, du_ref, dg_ref, cos_ref, sin_ref,
             w_ref, x_ref, c_ref, nw_ref, mod_ref, dx1_ref, dp_ref, gx_ref, acc_ref):
        i = pl.program_id(0)
        is_ctx = i < nct

        @pl.when(i == 0)
        def _():
            acc_ref[...] = jnp.zeros_like(acc_ref)

        cs, sn = cos_ref[...], sin_ref[...]
        def piece(k, val):
            cols = slice(S5_WIDTH * k, S5_WIDTH * (k + 1))
            dp_ref[:, cols] = val.astype(BF16)
            return _dot_nt(dp_ref[:, cols], w_ref[:, cols])

        dh1 = piece(0, du_ref[...])
        dh1 = dh1 + piece(3, dvf_ref[...] + dvb_ref[...])
        dh1 = dh1 + piece(4, jnp.where(is_ctx, 0.0, dg_ref[...]))
        for k, (f_ref, b_ref, scale) in ((1, (dqf_ref, dqb_ref, 1.0)), (2, (dkf_ref, dkb_ref, K_SCALE))):
            heads = [_rope_t(f_ref[:, RET_DH * h:RET_DH * (h + 1)] + b_ref[:, RET_DH * h:RET_DH * (h + 1)], cs, sn) * scale
                     for h in range(RET_HEADS)]
            dh1 = dh1 + piece(k, jnp.concatenate(heads, axis=1))
        xt = jnp.where(is_ctx, c_ref[...], x_ref[...])
        sh = jnp.where(is_ctx, mod_ref[0:1, :], mod_ref[2:3, :])
        sc = jnp.where(is_ctx, mod_ref[1:2, :], mod_ref[3:4, :])
        _, vjp = jax.vjp(_rms_mod, xt, nw_ref[...], sh, sc)
        dx, dnw, dsh, dsc = vjp(dh1)
        gx_ref[...] = dx + dx1_ref[...]
        cf = jnp.where(is_ctx, 1.0, 0.0)
        acc_ref[0:1, :] += dnw
        acc_ref[1:2, :] += cf * dsh
        acc_ref[2:3, :] += cf * dsc
        acc_ref[3:4, :] += (1.0 - cf) * dsh
        acc_ref[4:5, :] += (1.0 - cf) * dsc

    row = pl.BlockSpec((tm, RET_WIDTH), lambda i: (i, 0))
    tab = pl.BlockSpec((tm, RET_DH), lambda i: (i, 0))
    xrow = pl.BlockSpec((tm, D_MODEL), lambda i: (jnp.maximum(i - nct, 0), 0))
    return pl.pallas_call(
        body, name="in_bwd", grid=(la // tm,),
        in_specs=[row] * 7 + [pl.BlockSpec((tm, RET_WIDTH), lambda i: (jnp.maximum(i - nct, 0), 0)),
                              tab, tab, _full((D_MODEL, IN_COLS)), xrow,
                              pl.BlockSpec((tm, D_MODEL), lambda i: (jnp.minimum(i, nct - 1), 0)),
                              _full((1, D_MODEL)), _full((4, D_MODEL)), xrow],
        out_specs=[pl.BlockSpec((tm, IN_COLS), lambda i: (i, 0)), xrow, _full((8, D_MODEL))],
        out_shape=[jax.ShapeDtypeStruct((la, IN_COLS), BF16), jax.ShapeDtypeStruct((l, D_MODEL), F32),
                   jax.ShapeDtypeStruct((8, D_MODEL), F32)],
        compiler_params=_params(("arbitrary",)),
    )(dqf, dkf, dvf, dqb, dkb, dvb, du, dg, cos_t, sin_t, w_in_b, x, ctx, n1w, mod4, dx1)


def _outproj_up(x, y_all, of, ob, p_all, w_glu_b, b_glu, w_out_b, mod3, n2w, w_up_b, nct):
    l = x.shape[0]
    tm = TOK_TILE

    def body(x_ref, y_ref, of_ref, ob_ref, g_ref, wg_ref, bg_ref, wo_ref, mod_ref, nw_ref, wu_ref,
             x1_ref, mix_ref, h2_ref, up_ref, mb_ref, yr_ref):
        yg = _gelu(y_ref[...])
        mb_ref[:, 0:S5_WIDTH] = (yg * _sigmoid(_dot(yg.astype(BF16), wg_ref[...]) + bg_ref[...])).astype(BF16)
        yr = of_ref[...] + ob_ref[...]
        yr_ref[...] = yr
        for h in range(RET_HEADS):
            sl = slice(RET_DH * h, RET_DH * (h + 1))
            mb_ref[:, S5_WIDTH + RET_DH * h:S5_WIDTH + RET_DH * (h + 1)] = (
                _head_norm_gate(yr[:, sl], g_ref[:, sl]).astype(BF16))
        mix = _dot(mb_ref[...], wo_ref[...])
        mix_ref[...] = mix
        x1 = x_ref[...] + mod_ref[0:1, :] * mix
        x1_ref[...] = x1
        h2 = _rms_mod(x1, nw_ref[...], mod_ref[1:2, :], mod_ref[2:3, :]).astype(BF16)
        h2_ref[...] = h2
        up_ref[...] = _dot(h2, wu_ref[...])

    row = lambda w: pl.BlockSpec((tm, w), lambda i: (i, 0))
    arow = pl.BlockSpec((tm, RET_WIDTH), lambda i: (i + nct, 0))
    return pl.pallas_call(
        body, name="outproj_up", grid=(l // tm,),
        in_specs=[row(D_MODEL), arow, arow, arow, pl.BlockSpec((tm, RET_WIDTH), lambda i: (i + nct, G_COL // 4)),
                  _full((S5_WIDTH, S5_WIDTH)), _full((1, S5_WIDTH)), _full((D_MODEL, D_MODEL)), _full((3, D_MODEL)),
                  _full((1, D_MODEL)), _full((D_MODEL, 2 * D_FF))],
        out_specs=[row(D_MODEL), row(D_MODEL), row(D_MODEL), row(2 * D_FF), row(D_MODEL), row(RET_WIDTH)],
        out_shape=[jax.ShapeDtypeStruct((l, D_MODEL), F32), jax.ShapeDtypeStruct((l, D_MODEL), F32),
                   jax.ShapeDtypeStruct((l, D_MODEL), BF16), jax.ShapeDtypeStruct((l, 2 * D_FF), F32),
                   jax.ShapeDtypeStruct((l, D_MODEL), BF16), jax.ShapeDtypeStruct((l, RET_WIDTH), F32)],
        compiler_params=_params(("parallel",)),
    )(x, y_all, of, ob, p_all, w_glu_b, b_glu, w_out_b, mod3, n2w, w_up_b)


HALO = 8


def _conv_taps(g, prev_row, next_row):
    t = g.shape[0]
    r = _iota2(g.shape, 0)
    gprev = jnp.where(r == 0, prev_row, pltpu.roll(g, 1, 0))
    gnext = jnp.where(r == t - 1, next_row, pltpu.roll(g, t - 1, 0))
    return gprev, gnext


def _ffn_loss(up, x1, conv_w, conv_b, w_down_b, gate, fnw, tgt):
    l = x1.shape[0]
    tm = TOK_TILE
    nt = l // tm
    hb = tm // HALO

    cw = 256

    def body(up_a, up_g, hp_ref, hn_ref, x1_ref, cw_ref, cb_ref, wd_ref, gate_ref, fn_ref, tgt_ref,
             act_ref, dx2_ref, ddn_ref, dact_ref, acc_ref, ddn_scr):
        step = pl.program_id(0)
        i = jnp.minimum(step, nt - 1)

        @pl.when(step == 0)
        def _():
            acc_ref[...] = jnp.zeros_like(acc_ref)
            ddn_scr[...] = jnp.zeros_like(ddn_scr)

        ddn_prev = ddn_scr[...]
        dn = jnp.zeros((tm, D_MODEL), F32)
        for c in range(D_FF // cw):
            cols = slice(cw * c, cw * (c + 1))
            g = up_g[:, cols]
            prev_row = jnp.where(i == 0, 0.0, hp_ref[HALO - 1:HALO, cols])
            next_row = jnp.where(i == nt - 1, 0.0, hn_ref[0:1, cols])
            gprev, gnext = _conv_taps(g, prev_row, next_row)
            gc = cb_ref[:, cols] + gprev * cw_ref[0:1, cols] + g * cw_ref[1:2, cols] + gnext * cw_ref[2:3, cols]
            act = (_gelu(gc) * up_a[:, cols]).astype(BF16)
            act_ref[:, cols] = act
            dn = dn + _dot(act, wd_ref[cols, :])
            dact_ref[:, cols] = _dot_nt(ddn_prev, wd_ref[cols, :])
        x2 = x1_ref[...] + gate_ref[...] * dn
        y, vjp = jax.vjp(_rms, x2, fn_ref[...])
        err = y - tgt_ref[...]
        dx2, dfn = vjp(err * (1.0 / D_MODEL))
        dx2_ref[...] = dx2
        ddn = (dx2 * gate_ref[...]).astype(BF16)
        ddn_ref[...] = ddn
        ddn_scr[...] = ddn
        live = step < nt
        acc_ref[0:1, :] += jnp.where(live, dfn, 0.0)
        acc_ref[1:2, :] += jnp.where(live, jnp.sum(dx2 * dn, axis=0, keepdims=True), 0.0)
        acc_ref[2:3, :] += jnp.where(live, (0.5 / D_MODEL) * jnp.sum(err * err), 0.0)

    tile = lambda s: jnp.minimum(s, nt - 1)
    row = lambda w, cb=0: pl.BlockSpec((tm, w), lambda s: (tile(s), cb))
    last = l // HALO - 1
    return pl.pallas_call(
        body, name="ffn_loss", grid=(nt + 1,),
        in_specs=[row(D_FF, 0), row(D_FF, 1),
                  pl.BlockSpec((HALO, D_FF), lambda s: (jnp.maximum(tile(s) * hb - 1, 0), 1)),
                  pl.BlockSpec((HALO, D_FF), lambda s: (jnp.minimum((tile(s) + 1) * hb, last), 1)),
                  row(D_MODEL), _full((3, D_FF)), _full((1, D_FF)), _full((D_FF, D_MODEL)),
                  _full((1, D_MODEL)), _full((1, D_MODEL)), row(D_MODEL)],
        out_specs=[row(D_FF), row(D_MODEL), row(D_MODEL),
                   pl.BlockSpec((tm, D_FF), lambda s: (jnp.maximum(s - 1, 0), 0)), _full((8, D_MODEL))],
        out_shape=[jax.ShapeDtypeStruct((l, D_FF), BF16), jax.ShapeDtypeStruct((l, D_MODEL), F32),
                   jax.ShapeDtypeStruct((l, D_MODEL), BF16), jax.ShapeDtypeStruct((l, D_FF), F32),
                   jax.ShapeDtypeStruct((8, D_MODEL), F32)],
        scratch_shapes=[pltpu.VMEM((tm, D_MODEL), BF16)],
        compiler_params=_params(("arbitrary",)),
    )(up, up, up, up, x1, conv_w, conv_b, w_down_b, gate, fnw, tgt)


def _convglu_bwd(up, dact, conv_w, conv_b):
    l = up.shape[0]
    tm = 128
    nt = l // tm
    hb = tm // HALO
    te = tm + 2 * HALO

    def body(a_ref, ap_ref, an_ref, g_ref, gp_ref, gn_ref, d_ref, dp_ref, dn_ref, cw_ref, cb_ref,
             dup_ref, acc_ref):
        i = pl.program_id(0)

        @pl.when(i == 0)
        def _():
            acc_ref[...] = jnp.zeros_like(acc_ref)

        def ext(p, c, n):
            return jnp.concatenate([jnp.where(i == 0, 0.0, p[...]), c[...], jnp.where(i == nt - 1, 0.0, n[...])], axis=0)

        ae, ge, de = ext(ap_ref, a_ref, an_ref), ext(gp_ref, g_ref, gn_ref), ext(dp_ref, d_ref, dn_ref)
        gprev = pltpu.roll(ge, 1, 0)
        gnext = pltpu.roll(ge, te - 1, 0)
        w0, w1, w2 = cw_ref[0:1, :], cw_ref[1:2, :], cw_ref[2:3, :]
        gce = cb_ref[...] + gprev * w0 + ge * w1 + gnext * w2
        gel, dgel = _gelu_and_grad(gce)
        dae = de * gel
        dgce = de * ae * dgel
        dge = dgce * w1 + pltpu.roll(dgce, te - 1, 0) * w0 + pltpu.roll(dgce, 1, 0) * w2
        mid = slice(HALO, HALO + tm)
        dup_ref[:, 0:D_FF] = dae[mid].astype(BF16)
        dup_ref[:, D_FF:2 * D_FF] = dge[mid].astype(BF16)
        dgc = dgce[mid]
        acc_ref[0:1, :] += jnp.sum(dgc * gprev[mid], axis=0, keepdims=True)
        acc_ref[1:2, :] += jnp.sum(dgc * ge[mid], axis=0, keepdims=True)
        acc_ref[2:3, :] += jnp.sum(dgc * gnext[mid], axis=0, keepdims=True)
        acc_ref[3:4, :] += jnp.sum(dgc, axis=0, keepdims=True)

    last = l // HALO - 1

    def trio(cb):
        return [pl.BlockSpec((tm, D_FF), lambda i: (i, cb)),
                pl.BlockSpec((HALO, D_FF), lambda i: (jnp.maximum(i * hb - 1, 0), cb)),
                pl.BlockSpec((HALO, D_FF), lambda i: (jnp.minimum((i + 1) * hb, last), cb))]

    return pl.pallas_call(
        body, name="convglu_bwd", grid=(nt,),
        in_specs=trio(0) + trio(1) + trio(0) + [_full((3, D_FF)), _full((1, D_FF))],
        out_specs=[pl.BlockSpec((tm, 2 * D_FF), lambda i: (i, 0)), _full((8, D_FF))],
        out_shape=[jax.ShapeDtypeStruct((l, 2 * D_FF), BF16), jax.ShapeDtypeStruct((8, D_FF), F32)],
        compiler_params=_params(("arbitrary",)),
    )(up, up, up, up, up, up, dact, dact, dact, conv_w, conv_b)


def _up_bwd(dup, w_up_b, w_out_b, x1, dx2, mix, mod3, n2w, y_all, y_ret, p_all, w_glu_b, b_glu, zero_rows, nct, pairs,
            kinds):
    l = x1.shape[0]
    tm = TOK_TILE
    nt = l // tm
    n = len(pairs)
    shapes = _rs_slot_shapes(pairs, kinds)
    n_out = 8

    def body(dup_ref, wu_ref, wo_ref, x1_ref, dx2_ref, mix_ref, mod_ref, nw_ref, y_ref, yr_ref, g_ref, wg_ref, bg_ref,
             zero_rows_ref, *rest):
        dx1_ref, dmixb_ref, acc_ref, dys_ref, dyr_ref, dg_ref, gw_ref, gb_ref = rest[n:n + n_out]
        send_sems, recv_sems, dy_scr = rest[2 * n + n_out:]
        step = pl.program_id(0)

        @pl.when(step == 0)
        def _():
            acc_ref[...] = jnp.zeros_like(acc_ref)
            gw_ref[...] = jnp.zeros_like(gw_ref)
            gb_ref[...] = jnp.zeros_like(gb_ref)

        _behind(step, nt - 1, functools.partial(_rs_chip_copies, rest[:n], rest[n + n_out:2 * n + n_out], kinds,
                                                shapes, send_sems, recv_sems))

        dh2 = _dot_nt(dup_ref[...], wu_ref[...])
        _, vjp = jax.vjp(_rms_mod, x1_ref[...], nw_ref[...], mod_ref[1:2, :], mod_ref[2:3, :])
        dx, dnw, dsh, dsc = vjp(dh2)
        dx1 = dx + dx2_ref[...]
        dx1_ref[...] = dx1
        dmixb = (dx1 * mod_ref[0:1, :]).astype(BF16)
        dmixb_ref[...] = dmixb
        dmix = _dot_nt(dmixb, wo_ref[...])
        acc_ref[0:1, :] += dnw
        acc_ref[1:2, :] += jnp.sum(dx1 * mix_ref[...], axis=0, keepdims=True)
        acc_ref[2:3, :] += dsh
        acc_ref[3:4, :] += dsc

        yg, dgel = _gelu_and_grad(y_ref[...])
        ygb = yg.astype(BF16)
        sg = _sigmoid(_dot(ygb, wg_ref[...]) + bg_ref[...])
        ds = dmix[:, 0:S5_WIDTH]
        dz = ds * yg * sg * (1.0 - sg)
        dzb = dz.astype(BF16)
        _s5_put_rows(dys_ref, dy_scr, (ds * sg + _dot_nt(dzb, wg_ref[...])) * dgel)
        gw_ref[...] += _dot_tn(ygb, dzb)
        gb_ref[...] += jnp.sum(dz, axis=0, keepdims=True)

        for h in range(RET_HEADS):
            sl = slice(RET_DH * h, RET_DH * (h + 1))
            _, hvjp = jax.vjp(_head_norm_gate, yr_ref[:, sl], g_ref[:, sl])
            dyr, dg = hvjp(dmix[:, S5_WIDTH + RET_DH * h:S5_WIDTH + RET_DH * (h + 1)])
            dyr_ref[:, sl] = dyr
            dg_ref[:, sl] = dg

    row = pl.BlockSpec((tm, D_MODEL), lambda i: (i, 0))
    half = pl.BlockSpec((tm, S5_WIDTH), lambda i: (i, 0))
    f32h = jax.ShapeDtypeStruct((l, RET_WIDTH), F32)
    return pl.pallas_call(
        body, name="up_bwd", grid=(nt,),
        in_specs=[pl.BlockSpec((tm, 2 * D_FF), lambda i: (i, 0)), _full((D_MODEL, 2 * D_FF)),
                  _full((D_MODEL, D_MODEL)), row, row, row, _full((3, D_MODEL)), _full((1, D_MODEL)),
                  pl.BlockSpec((tm, S5_WIDTH), lambda i: (i + nct, 0)), half,
                  pl.BlockSpec((tm, RET_WIDTH), lambda i: (i + nct, G_COL // 4)),
                  _full((S5_WIDTH, S5_WIDTH)), _full((1, S5_WIDTH)), ANY] + [ANY] * n,
        out_specs=[row, row, _full((8, D_MODEL)),
                   pl.BlockSpec((S5_NB, tm // S5_T, S5_BW), lambda i: (0, i + nct, 0)), half, half,
                   _full((S5_WIDTH, S5_WIDTH)),
                   _full((1, S5_WIDTH))] + [ANY] * n,
        out_shape=[jax.ShapeDtypeStruct((l, D_MODEL), F32), jax.ShapeDtypeStruct((l, D_MODEL), BF16),
                   jax.ShapeDtypeStruct((8, D_MODEL), F32), jax.ShapeDtypeStruct(zero_rows.shape, BF16), f32h, f32h,
                   jax.ShapeDtypeStruct((S5_WIDTH, S5_WIDTH), F32), jax.ShapeDtypeStruct((1, S5_WIDTH), F32)]
        + [jax.ShapeDtypeStruct((4,) + s, p.dtype) for s, p in zip(shapes, pairs)],
        input_output_aliases={13: 3},
        scratch_shapes=[pltpu.SemaphoreType.DMA((n, 3)), pltpu.SemaphoreType.DMA((n, 3)),
                        pltpu.VMEM((tm // S5_T, S5_T, S5_WIDTH), F32)],
        compiler_params=_params(("arbitrary",)),
    )(dup, w_up_b, w_out_b, x1, dx2, mix, mod3, n2w, y_all, y_ret, p_all, w_glu_b, b_glu, zero_rows, *pairs)


MOD_ROWS = 16
MOD_COLS = 6 * D_MODEL // 4


def _mod_fwd(c_all, c_ctx, w_mod_b, b_loc):
    def body(c_ref, cc_ref, w_ref, b_ref, m_ref, s_ref):
        cond = jnp.concatenate([c_ref[...], jnp.broadcast_to(cc_ref[...], (8, D_MODEL))], axis=0)
        s = _silu(cond).astype(BF16)
        s_ref[...] = s
        m_ref[...] = _dot(s, w_ref[...]) + b_ref[...]

    return pl.pallas_call(
        body, name="mod_fwd",
        out_shape=[jax.ShapeDtypeStruct((MOD_ROWS, MOD_COLS), F32), jax.ShapeDtypeStruct((MOD_ROWS, D_MODEL), BF16)],
        compiler_params=_params(),
    )(c_all, c_ctx, w_mod_b, b_loc)


def _mod_bwd_sum(dm_all):
    def body(d_ref, dm_ref, gb_ref):
        rows = [d_ref[k, 0:1, :] for k in range(8)]
        ctx_sum = d_ref[0, 1:2, :]
        for k in range(1, 8):
            ctx_sum = ctx_sum + d_ref[k, 1:2, :]
        gb = ctx_sum
        for k in range(8):
            gb = gb + rows[k]
        gb_ref[...] = gb
        dm_ref[...] = jnp.concatenate(rows + [ctx_sum] + [jnp.zeros((7, 6 * D_MODEL), F32)], axis=0)

    return pl.pallas_call(
        body, name="mod_bwd_sum",
        out_shape=[jax.ShapeDtypeStruct((MOD_ROWS, 6 * D_MODEL), F32), jax.ShapeDtypeStruct((1, 6 * D_MODEL), F32)],
        compiler_params=_params(),
    )(dm_all)


def _mod_bwd_w(dm_loc, s_b, c_ctx, w_mod_b):
    def body(d_ref, s_ref, cc_ref, w_ref, gw_ref, gc_ref):
        db = d_ref[...].astype(BF16)
        gw_ref[...] = _dot_tn(s_ref[...], db)
        ds = _dot_nt(db, w_ref[...])
        _, vjp = jax.vjp(_silu, cc_ref[...])
        gc_ref[...] = jnp.broadcast_to(vjp(ds[8:9, :])[0], (8, D_MODEL))

    return pl.pallas_call(
        body, name="mod_bwd_w",
        out_shape=[jax.ShapeDtypeStruct((D_MODEL, MOD_COLS), F32), jax.ShapeDtypeStruct((8, D_MODEL), F32)],
        compiler_params=_params(),
    )(dm_loc, s_b, c_ctx, w_mod_b)


def _adamw(w, g, m, v, name):
    r, c = w.shape
    tr = _pick(r, (256, 128, 64, 32, 16, 8))
    bc1 = 1.0 - ADAM_B1 ** ADAM_STEP
    bc2 = 1.0 - ADAM_B2 ** ADAM_STEP

    def body(w_ref, g_ref, m_ref, v_ref, d_ref, nm_ref, nv_ref):
        gg = g_ref[...]
        nm = ADAM_B1 * m_ref[...] + (1.0 - ADAM_B1) * gg
        nv = ADAM_B2 * v_ref[...] + (1.0 - ADAM_B2) * (gg * gg)
        nm_ref[...] = nm
        nv_ref[...] = nv
        d_ref[...] = -ADAM_LR * ((nm / bc1) / (jnp.sqrt(nv / bc2) + ADAM_EPS) + ADAM_WD * w_ref[...])

    blk = pl.BlockSpec((tr, c), lambda i: (i, 0))
    shp = jax.ShapeDtypeStruct((r, c), F32)
    return pl.pallas_call(
        body, name=name, grid=(r // tr,), in_specs=[blk] * 4, out_specs=[blk] * 3, out_shape=[shp] * 3,
        compiler_params=_params(("parallel",)),
    )(w, g, m, v)


def _sum_slots(a, name):
    n, r, c = a.shape
    tr = _pick(r, (376, 256, 208, 128, 64, 32, 16, 8))

    def body(a_ref, o_ref):
        acc = a_ref[0].astype(F32)
        for k in range(1, n):
            acc = acc + a_ref[k].astype(F32)
        o_ref[...] = acc

    return pl.pallas_call(
        body, name=name, grid=(r // tr,),
        in_specs=[pl.BlockSpec((n, tr, c), lambda i: (0, i, 0))],
        out_specs=pl.BlockSpec((tr, c), lambda i: (i, 0)),
        out_shape=jax.ShapeDtypeStruct((r, c), F32),
        compiler_params=_params(("parallel",)),
    )(a)


def _mesh_pos():
    return lax.axis_index("x"), lax.axis_index("y"), lax.axis_index("c")


def _gather8_phases(x_ref, out_ref, send_sems, recv_sems, local_sem, m_per):
    def parts():
        x, y, c = _mesh_pos()
        me, sibling = (x, y, c), (x, y, 1 - c)
        chips = [(1 - x, y), (x, 1 - y), (1 - x, 1 - y)]

        def rows(px, py, pc):
            return out_ref.at[pl.ds((4 * px + 2 * py + pc) * m_per, m_per), :]

        def copy(k, block, to, src=None):
            return pltpu.make_async_remote_copy(
                src_ref=rows(*block) if src is None else src, dst_ref=rows(*block),
                send_sem=send_sems.at[k], recv_sem=recv_sems.at[k], device_id=to, device_id_type=MESH_ID)

        mine = pltpu.make_async_copy(x_ref, rows(*me), local_sem)
        first = [copy(0, me, sibling, src=x_ref)]
        first += [copy(1 + j, me, (*chip, c), src=x_ref) for j, chip in enumerate(chips)]
        return me, sibling, chips, c, copy, mine, first

    def begin():
        *_, mine, first = parts()
        mine.start()
        for cp in first:
            cp.start()

    def finish():
        me, sibling, chips, c, copy, mine, first = parts()
        passed = [copy(4 + j, (*chip, c), sibling) for j, chip in enumerate(chips)]
        for j, chip in enumerate(chips):
            copy(1 + j, (*chip, c), me).wait_recv()
            passed[j].start()
        copy(0, sibling, me).wait_recv()
        for j, chip in enumerate(chips):
            copy(4 + j, (*chip, 1 - c), me).wait_recv()
        for cp in first + passed:
            cp.wait_send()
        mine.wait()

    return begin, finish


GATHER8_SCRATCH = (pltpu.SemaphoreType.DMA((7,)), pltpu.SemaphoreType.DMA((7,)), pltpu.SemaphoreType.DMA)


def _all_gather8(v, name):
    m_per, n = v.shape

    def body(x_ref, out_ref, send_sems, recv_sems, local_sem):
        begin, finish = _gather8_phases(x_ref, out_ref, send_sems, recv_sems, local_sem, m_per)
        begin()
        finish()

    return pl.pallas_call(
        body, name=name,
        out_shape=jax.ShapeDtypeStruct((8 * m_per, n), v.dtype),
        in_specs=[pl.BlockSpec(memory_space=pltpu.VMEM)],
        out_specs=pl.BlockSpec(memory_space=pltpu.VMEM),
        scratch_shapes=list(GATHER8_SCRATCH),
        compiler_params=_params(),
    )(v)


ANY = pl.BlockSpec(memory_space=pl.ANY)
def PEER_CHIPS(x, y):
    return [(x, 1 - y), (1 - x, y), (1 - x, 1 - y)]


def _shard_region(ref, kind, k, rl, cl, r0, nr, c0, nc):
    if kind == "col":
        return ref.at[pl.ds(r0, nr), pl.ds(k * cl + c0, nc)]
    return ref.at[pl.ds(k * rl + r0, nr), pl.ds(c0, nc)]


def _place_shard(w, kind, chip, name):
    rl, cl = w.shape
    tr = _pick(rl, (256, 128, 64))
    nt = rl // tr

    def body(chip_ref, w_ref, o_ref):
        o_ref[...] = w_ref[...].astype(BF16)

    o_map = (lambda i, chip_ref: (i, chip_ref[0])) if kind == "col" else (lambda i, chip_ref: (chip_ref[0] * nt + i, 0))
    return pl.pallas_call(
        body, name=name,
        grid_spec=pltpu.PrefetchScalarGridSpec(
            num_scalar_prefetch=1, grid=(nt,),
            in_specs=[pl.BlockSpec((tr, cl), lambda i, chip_ref: (i, 0))], out_specs=pl.BlockSpec((tr, cl), o_map)),
        out_shape=jax.ShapeDtypeStruct((rl, 4 * cl) if kind == "col" else (4 * rl, cl), BF16),
        compiler_params=_params(("parallel",)),
    )(chip.reshape(1), w)


def _gather_shard_shapes(placed, kinds):
    return [(p.shape[0], p.shape[1] // 4) if k == "col" else (p.shape[0] // 4, p.shape[1]) for p, k in zip(placed, kinds)]


def _gather_chip_copies(outs, kinds, shard_shapes, send_sems, recv_sems, with_arrivals=True):
    x, y, c = _mesh_pos()
    me = 2 * x + y
    sends, arrivals = [], []
    for a in range(len(outs)):
        rl, cl = shard_shapes[a]
        rh = rl // 2
        reg = functools.partial(_shard_region, outs[a], kinds[a], rl=rl, cl=cl, r0=c * rh, nr=rh, c0=0, nc=cl)
        for j, (px, py) in enumerate(PEER_CHIPS(x, y)):
            to = dict(send_sem=send_sems.at[a, j], recv_sem=recv_sems.at[a, j], device_id=(px, py, c),
                      device_id_type=MESH_ID)
            sends.append(pltpu.make_async_remote_copy(src_ref=reg(k=me), dst_ref=reg(k=me), **to))
            if with_arrivals:
                got = reg(k=2 * px + py)
                arrivals.append(pltpu.make_async_remote_copy(src_ref=got, dst_ref=got, **to))
    return sends, arrivals


def _gather_sibling_copies(outs, kinds, shard_shapes, send_sems, recv_sems):
    x, y, c = _mesh_pos()
    forwards, arrivals = [], []
    for a in range(len(outs)):
        rl, cl = shard_shapes[a]
        rh = rl // 2
        for j, (px, py) in enumerate(PEER_CHIPS(x, y)):
            to = dict(send_sem=send_sems.at[a, j], recv_sem=recv_sems.at[a, j], device_id=(x, y, 1 - c),
                      device_id_type=MESH_ID)
            reg = functools.partial(_shard_region, outs[a], kinds[a], k=2 * px + py, rl=rl, cl=cl, nr=rh, c0=0, nc=cl)
            forwards.append(pltpu.make_async_remote_copy(src_ref=reg(r0=c * rh), dst_ref=reg(r0=c * rh), **to))
            arrivals.append(pltpu.make_async_remote_copy(src_ref=reg(r0=(1 - c) * rh), dst_ref=reg(r0=(1 - c) * rh), **to))
    return forwards, arrivals


def _gather_sibling(placed, kinds, name):
    n = len(placed)
    shard_shapes = _gather_shard_shapes(placed, kinds)

    def body(*refs):
        forwards, from_sibling = _gather_sibling_copies(refs[n:2 * n], kinds, shard_shapes, *refs[2 * n:])
        for cp in forwards:
            cp.start()
        for cp in from_sibling:
            cp.wait_recv()
        for cp in forwards:
            cp.wait_send()

    return pl.pallas_call(
        body, name=name,
        out_shape=[jax.ShapeDtypeStruct(p.shape, p.dtype) for p in placed],
        in_specs=[ANY] * n, out_specs=[ANY] * n, input_output_aliases={a: a for a in range(n)},
        scratch_shapes=[pltpu.SemaphoreType.DMA((n, 3))] * 2,
        compiler_params=_params(),
    )(*placed)


def _half(kind, r, c):
    return (r // 2, c) if kind == "col" else (r, c // 2)


def _half_of(ref, kind, which):
    r, c = ref.shape
    hr, hc = _half(kind, r, c)
    return ref.at[pl.ds(which * hr, hr), :] if kind == "col" else ref.at[:, pl.ds(which * hc, hc)]


def _rs_sibling(grads, kinds, name):
    n = len(grads)

    def body(*refs):
        srcs, dsts = refs[:n], refs[n:2 * n]
        send_sems, recv_sems = refs[2 * n:]
        x, y, c = _mesh_pos()
        cps = [pltpu.make_async_remote_copy(src_ref=_half_of(srcs[a], kinds[a], 1 - c), dst_ref=dsts[a],
                                            send_sem=send_sems.at[a], recv_sem=recv_sems.at[a],
                                            device_id=(x, y, 1 - c), device_id_type=MESH_ID) for a in range(n)]
        for cp in cps:
            cp.start()
        for cp in cps:
            cp.wait()

    return pl.pallas_call(
        body, name=name,
        out_shape=[jax.ShapeDtypeStruct(_half(k, *g.shape), g.dtype) for g, k in zip(grads, kinds)],
        in_specs=[ANY] * n, out_specs=[ANY] * n,
        scratch_shapes=[pltpu.SemaphoreType.DMA((n,)), pltpu.SemaphoreType.DMA((n,))],
        compiler_params=_params(),
    )(*grads)


def _pair_sum(gf, rv, kind, ci, name):
    r, c = rv.shape
    tr = _pick(r, (128, 64, 32, 16, 8))
    nt = r // tr

    def body(ci_ref, g_ref, r_ref, o_ref):
        o_ref[...] = (g_ref[...] + r_ref[...]).astype(BF16)

    g_map = (lambda i, ci_ref: (ci_ref[0] * nt + i, 0)) if kind == "col" else (lambda i, ci_ref: (i, ci_ref[0]))
    blk = pl.BlockSpec((tr, c), lambda i, ci_ref: (i, 0))
    return pl.pallas_call(
        body, name=name,
        grid_spec=pltpu.PrefetchScalarGridSpec(num_scalar_prefetch=1, grid=(nt,),
                                               in_specs=[pl.BlockSpec((tr, c), g_map), blk], out_specs=blk),
        out_shape=jax.ShapeDtypeStruct((r, c), BF16),
        compiler_params=_params(("parallel",)),
    )(ci.reshape(1), gf, rv)


def _rs_slot_shapes(pairs, kinds):
    return [(p.shape[0], p.shape[1] // 4) if k == "col" else (p.shape[0] // 4, p.shape[1]) for p, k in zip(pairs, kinds)]


def _rs_chip_copies(srcs, dsts, kinds, shapes, send_sems, recv_sems, with_arrivals=True):
    x, y, c = _mesh_pos()
    me = 2 * x + y
    sends, arrivals = [], []
    for a in range(len(srcs)):
        rl, cl = shapes[a]
        reg = functools.partial(_shard_region, srcs[a], kinds[a], rl=rl, cl=cl, r0=0, nr=rl, c0=0, nc=cl)
        for j, (px, py) in enumerate(PEER_CHIPS(x, y)):
            to = dict(send_sem=send_sems.at[a, j], recv_sem=recv_sems.at[a, j], device_id=(px, py, c),
                      device_id_type=MESH_ID)
            sends.append(pltpu.make_async_remote_copy(src_ref=reg(k=2 * px + py), dst_ref=dsts[a].at[me], **to))
            if with_arrivals:
                slot = dsts[a].at[2 * px + py]
                arrivals.append(pltpu.make_async_remote_copy(src_ref=slot, dst_ref=slot, **to))
    return sends, arrivals


def _rs_chips(pairs, kinds):
    n = len(pairs)
    shapes = _rs_slot_shapes(pairs, kinds)

    def body(*refs):
        sends, arrivals = _rs_chip_copies(refs[:n], refs[n:2 * n], kinds, shapes, *refs[2 * n:])
        for cp in sends:
            cp.start()
        for cp in arrivals:
            cp.wait_recv()
        for cp in sends:
            cp.wait_send()

    return pl.pallas_call(
        body, name="rs_chips",
        out_shape=[jax.ShapeDtypeStruct((4,) + s, p.dtype) for s, p in zip(shapes, pairs)],
        in_specs=[ANY] * n, out_specs=[ANY] * n,
        scratch_shapes=[pltpu.SemaphoreType.DMA((n, 3)), pltpu.SemaphoreType.DMA((n, 3))],
        compiler_params=_params(),
    )(*pairs)


def _sum_chips(pair, got, kind, pos, name):
    _, r, c = got.shape
    tr = _pick(r, (256, 128, 64, 32, 16))
    nt = r // tr

    def body(pos_ref, own_ref, g1_ref, g2_ref, g3_ref, o_ref):
        o_ref[...] = ((own_ref[...].astype(F32) + g1_ref[0].astype(F32)) + g2_ref[0].astype(F32)) + g3_ref[0].astype(F32)

    if kind == "col":
        own_map = lambda i, p: (i, p[1])
        out_map = lambda i, p: (p[0] * nt + i, 0)
        out_shape = (2 * r, c)
    else:
        own_map = lambda i, p: (p[1] * nt + i, 0)
        out_map = lambda i, p: (i, p[0])
        out_shape = (r, 2 * c)
    peer = lambda m: pl.BlockSpec((1, tr, c), lambda i, p: (p[1] ^ m, i, 0))
    return pl.pallas_call(
        body, name=name,
        grid_spec=pltpu.PrefetchScalarGridSpec(
            num_scalar_prefetch=1, grid=(nt,),
            in_specs=[pl.BlockSpec((tr, c), own_map), peer(1), peer(2), peer(3)],
            out_specs=pl.BlockSpec((tr, c), out_map)),
        out_shape=jax.ShapeDtypeStruct(out_shape, F32),
        compiler_params=_params(("parallel",)),
    )(pos, pair, got, got, got)


def _rs_back(halves, kinds):
    n = len(halves)

    def body(*refs):
        outs = refs[n:2 * n]
        send_sems, recv_sems = refs[2 * n:]
        x, y, c = _mesh_pos()
        cps = []
        for a in range(n):
            mine = _half_of(outs[a], kinds[a], c)
            cps.append(pltpu.make_async_remote_copy(src_ref=mine, dst_ref=mine, send_sem=send_sems.at[a],
                                                    recv_sem=recv_sems.at[a], device_id=(x, y, 1 - c),
                                                    device_id_type=MESH_ID))
            cps[-1].start()
        for a in range(n):
            other = _half_of(outs[a], kinds[a], 1 - c)
            pltpu.make_async_remote_copy(src_ref=other, dst_ref=other, send_sem=send_sems.at[a],
                                         recv_sem=recv_sems.at[a], device_id=(x, y, 1 - c),
                                         device_id_type=MESH_ID).wait_recv()
        for cp in cps:
            cp.wait_send()

    return pl.pallas_call(
        body, name="rs_back",
        out_shape=[jax.ShapeDtypeStruct(h.shape, h.dtype) for h in halves],
        in_specs=[ANY] * n, out_specs=[ANY] * n, input_output_aliases={a: a for a in range(n)},
        scratch_shapes=[pltpu.SemaphoreType.DMA((n,)), pltpu.SemaphoreType.DMA((n,))],
        compiler_params=_params(),
    )(*halves)


def _rope_tables(l, lc):
    rows = l // GRID_W
    n_freq = RET_DH // 4
    inv_freq = ROPE_THETA ** (-jnp.arange(n_freq, dtype=F32) / n_freq)
    sign = jnp.tile(jnp.array([-1.0, 1.0], F32), n_freq)

    def half(n):
        ang = jnp.repeat(jnp.arange(n, dtype=F32)[:, None] * inv_freq, 2, axis=-1)
        return jnp.cos(ang), jnp.sin(ang) * sign

    (cr, sr), (cc, sc) = half(rows), half(GRID_W)
    grid = lambda r, c: jnp.concatenate([jnp.repeat(r, GRID_W, axis=0), jnp.tile(c, (rows, 1))], axis=-1)
    cos_t = jnp.concatenate([jnp.ones((lc, RET_DH), F32), grid(cr, cc)], axis=0)
    sin_t = jnp.concatenate([jnp.zeros((lc, RET_DH), F32), grid(sr, sc)], axis=0)
    return cos_t, sin_t


def _s5_pack(a):
    blk = lambda t: t.reshape(1, S5_NB, 128, S5_STATE)
    lre = jnp.stack([a["s5_lambda_re_f"][0], a["s5_lambda_re_b"][0]]).reshape(2, S5_NB, 8, S5_STATE)
    lim = jnp.stack([a["s5_lambda_im_f"][0], a["s5_lambda_im_b"][0]]).reshape(2, S5_NB, 8, S5_STATE)
    lst = jnp.stack([a["s5_log_step_f"][0], a["s5_log_step_b"][0]]).reshape(2, S5_NB, 8, 1)
    b_re = blk(a["s5_b_re"][0].transpose(0, 2, 1))
    b_im = blk(a["s5_b_im"][0].transpose(0, 2, 1))
    return (lre, lim, lst, b_re, b_im, blk(a["s5_c_re"][0]), blk(a["s5_c_im"][0]),
            a["s5_d"].reshape(1, S5_NB, 1, 128))


def _s5_unpack(g):
    glre, glim, glst, gbre, gbim, gcre, gcim, gd = g
    unb = lambda t: t.reshape(S5_GROUPS, S5_GROUP, S5_STATE).transpose(0, 2, 1)[None]
    return {
        "s5_lambda_re_f": glre[0].reshape(1, S5_GROUPS, S5_STATE), "s5_lambda_re_b": glre[1].reshape(1, S5_GROUPS, S5_STATE),
        "s5_lambda_im_f": glim[0].reshape(1, S5_GROUPS, S5_STATE), "s5_lambda_im_b": glim[1].reshape(1, S5_GROUPS, S5_STATE),
        "s5_log_step_f": glst[0].reshape(1, S5_GROUPS), "s5_log_step_b": glst[1].reshape(1, S5_GROUPS),
        "s5_b_re": unb(gbre), "s5_b_im": unb(gbim),
        "s5_c_re": gcre.reshape(1, S5_GROUPS, S5_GROUP, S5_STATE), "s5_c_im": gcim.reshape(1, S5_GROUPS, S5_GROUP, S5_STATE),
        "s5_d": gd.reshape(1, S5_WIDTH),
    }


def _local_step(a, early, late, mx, mc, conv_w, ci):
    x, ctx, tgt = a["x"][0], a["ctx"][0], a["loss_target"][0]
    l, lc = x.shape[0], ctx.shape[0]
    la = l + lc
    nct, ncc, nrc, cn = lc // TOK_TILE, lc // S5_T, lc // RET_CHUNK, la // S5_T
    n1w, n2w, fnw = a["norm1_w"], a["norm2_w"], a["final_norm_w"].reshape(1, D_MODEL)
    conv_b, b_glu = a["conv_b"], a["s5_b_glu"]
    ld2 = jnp.concatenate([a["ret_log_decay_f"], a["ret_log_decay_b"]], axis=0)
    mod4 = jnp.concatenate([mc[0:2], mx[0:2]], axis=0)
    mod3 = mx[2:5]
    gate5 = mx[5:6]
    cos_t, sin_t = _rope_tables(l, lc)
    s5p = _s5_pack(a)

    gg, xw, yw, a16, *early = _s5_gen(*s5p, early, EARLY_KINDS)
    wb = dict(zip(EARLY_NAMES, _gather_sibling(early, EARLY_KINDS, "gather_sibling_early")))
    p_all, h1b, p3, w_up_p = _norm_inproj(x, ctx, n1w, mod4, wb["w_in"], cos_t, sin_t, [late[1]], (LATE_KINDS[1],))
    sloc = _s5_to_states(p3, xw, "s5_state")
    a16s = a16.transpose(1, 0, 2, 3).reshape(2, S5_GROUPS, 128)
    hs, w_out_p, w_down_p = _s5_scan(sloc, a16s, ncc, [late[0], late[2]], (LATE_KINDS[0], LATE_KINDS[2]))
    y_all = _s5_from_states(p3, gg, hs, yw, False, "s5_out").reshape(la, S5_WIDTH)
    tabs = _ret_tables(ld2)
    of, ob, ssf, ssb = _ret_scan(p_all, tabs, nrc)
    wb = {**wb, **dict(zip(LATE_NAMES, _gather_sibling([w_out_p, w_up_p, w_down_p], LATE_KINDS, "gather_sibling_late")))}
    x1, mix, h2b, up, mixb, y_ret = _outproj_up(x, y_all, of, ob, p_all, wb["s5_w_glu"], b_glu, wb["w_out"],
                                                     mod3, n2w, wb["w_up"], nct)
    act, dx2, ddn, dact, acc_f = _ffn_loss(up, x1, conv_w, conv_b, wb["w_down"], gate5, fnw, tgt)

    g = {}
    g["w_down"] = _mm_tn(act, ddn, name="gw_down")
    dup, acc_c = _convglu_bwd(up, dact, conv_w, conv_b)
    g["w_up"] = _mm_tn(h2b, dup, name="gw_up")
    first = [g[n] for n in FIRST_GRADS]
    first_pairs = [_pair_sum(gf, rv, k, ci, "rs_pair_" + n)
                   for gf, rv, k, n in zip(first, _rs_sibling(first, FIRST_KINDS, "rs_sibling_first"), FIRST_KINDS, FIRST_GRADS)]
    dx1, dmixb, acc_2, dy3, dy_ret, dg, g["s5_w_glu"], g["s5_b_glu"], *first_got = _up_bwd(
        dup, wb["w_up"], wb["w_out"], x1, dx2, mix, mod3, n2w, y_all, y_ret, p_all, wb["s5_w_glu"], b_glu,
        jnp.zeros(p3.shape, BF16), nct, first_pairs, FIRST_KINDS)
    g["w_out"] = _mm_tn(mixb, dmixb, name="gw_out")

    e = _s5_to_states(dy3, yw, "s5_bwd_h")
    ds, da16 = _s5_scan_bwd(e, hs, a16s, ncc)
    du = _s5_from_states(dy3, gg, ds, xw, True, "s5_bwd_u").reshape(la, S5_WIDTH)
    dkb = _s5_bwd_kb(p3, dy3)
    dwst = _s5_bwd_w(p3, ds, "s5_bwd_wst")
    dwout = _s5_bwd_w(dy3, hs, "s5_bwd_wout")
    da16p = da16.reshape(2, S5_NB, 8, 128).transpose(1, 0, 2, 3)
    g.update(_s5_unpack(_s5_gen_bwd(*s5p, dkb, dwst, dwout, da16p)))

    dqf, dkf, dvf, dqb, dkb_, dvb, dld = _ret_scan_bwd(p_all, tabs, ssf, ssb, dy_ret, nrc)
    g["ret_log_decay_f"] = dld[0, :, 0, 0].reshape(1, RET_HEADS)
    g["ret_log_decay_b"] = dld[1, :, 0, 0].reshape(1, RET_HEADS)
    dp, grad_x, acc_1 = _in_bwd(dqf, dkf, dvf, dqb, dkb_, dvb, du, dg, cos_t, sin_t, wb["w_in"], x, ctx, n1w, mod4, dx1)
    g["norm1_w"], g["norm2_w"], g["final_norm_w"] = acc_1[0:1], acc_2[0:1], acc_f[0]
    g["conv_w"], g["conv_b"] = acc_c[0:3], acc_c[3:4]
    zero = jnp.zeros((1, D_MODEL), F32)
    dmx = jnp.concatenate([acc_1[3:5], acc_2[1:2], acc_2[2:4], acc_f[1:2]], axis=0)
    dmc = jnp.concatenate([acc_1[1:3], zero, zero, zero, zero], axis=0)
    dm_pair = jnp.concatenate([dmx.reshape(1, -1), dmc.reshape(1, -1), jnp.zeros((6, 6 * D_MODEL), F32)], axis=0)
    small = _pack_rows([g[n] for n in SMALL_NAMES] + [acc_f[2, 0:1]])
    g["w_in"], dm_all, small_all = _mm_tn(h1b, dp, name="gw_in", gathered=[dm_pair, small])
    return grad_x, g, dm_all, small_all, first_pairs, first_got


WEIGHT_NAMES = ("c_ctx", "w_mod", "b_mod", "norm1_w", "w_in", "s5_lambda_re_f", "s5_lambda_im_f", "s5_log_step_f",
                "s5_lambda_re_b", "s5_lambda_im_b", "s5_log_step_b", "s5_b_re", "s5_b_im", "s5_c_re", "s5_c_im",
                "s5_d", "s5_w_glu", "s5_b_glu", "ret_log_decay_f", "ret_log_decay_b", "w_out", "norm2_w", "w_up",
                "conv_w", "conv_b", "w_down", "final_norm_w")
BIG_NAMES = ("w_in", "w_out", "w_up", "w_down", "s5_w_glu")
BIG_KINDS = ("col", "row", "col", "row", "row")
EARLY_NAMES, EARLY_KINDS = ("w_in", "s5_w_glu"), ("col", "row")
LATE_NAMES, LATE_KINDS = ("w_out", "w_up", "w_down"), ("row", "col", "row")
FIRST_GRADS, FIRST_KINDS = ("w_down", "w_up"), ("row", "col")
LAST_GRADS, LAST_KINDS = ("w_in", "w_out", "s5_w_glu"), ("col", "row", "row")
SMALL_NAMES = ("norm1_w", "norm2_w", "final_norm_w", "conv_b", "conv_w", "s5_lambda_re_f", "s5_lambda_im_f",
               "s5_log_step_f", "s5_lambda_re_b", "s5_lambda_im_b", "s5_log_step_b", "s5_b_re", "s5_b_im", "s5_c_re",
               "s5_c_im", "s5_d", "s5_b_glu", "ret_log_decay_f", "ret_log_decay_b")
ROW = 1024
N_CHIPS = 4


def _pack_rows(parts):
    flat = jnp.concatenate([p.reshape(-1) for p in parts])
    n = flat.shape[0]
    rows = -(-n // (8 * ROW)) * 8
    return jnp.pad(flat, (0, rows * ROW - n)).reshape(rows, ROW)


def _unpack_rows(packed, shapes):
    flat = packed.reshape(-1)
    out, off = [], 0
    for s in shapes:
        n = math.prod(s)
        out.append(flat[off:off + n].reshape(s))
        off += n
    return out


def _step(a):
    xi, yi, ci = _mesh_pos()
    chip = 2 * xi + yi
    dev = 2 * chip + ci

    cw_loc = a["conv_w"].reshape(-1)
    small_in = jnp.concatenate([a["c"].reshape(-1), jnp.pad(cw_loc, (0, 24 * 128 - cw_loc.shape[0]))]).reshape(32, 128)
    sg = _all_gather8(small_in, "gather_cond").reshape(8, 32, 128)
    c_all = sg[:, 0:8].reshape(8, D_MODEL)
    conv_w = sg[0::2, 8:32].reshape(N_CHIPS, -1)[:, :cw_loc.shape[0]].reshape(N_CHIPS, 3, -1)
    conv_w = conv_w.transpose(1, 0, 2).reshape(3, D_FF)

    placed = {n: _place_shard(a[n][0], k, chip, "place_" + n) for n, k in zip(BIG_NAMES, BIG_KINDS)}
    early = [placed[n] for n in EARLY_NAMES]
    late = [placed[n] for n in LATE_NAMES]

    w_mod_b = a["w_mod"][0].astype(BF16)
    c_ctx = a["c_ctx"].reshape(1, D_MODEL)
    b_loc = lax.dynamic_slice_in_dim(a["b_mod"], chip * MOD_COLS, MOD_COLS, 1)
    m_loc, s_b = _mod_fwd(c_all, c_ctx, w_mod_b, b_loc)
    mg = _all_gather8(m_loc, "gather_mod").reshape(8, MOD_ROWS, MOD_COLS)
    m_full = mg[0::2].transpose(1, 0, 2).reshape(MOD_ROWS, 6 * D_MODEL)
    mx = lax.dynamic_slice_in_dim(m_full, dev, 1, 0).reshape(6, D_MODEL)
    mc = m_full[8].reshape(6, D_MODEL)

    grad_x, g, dm_all, small_all, first_pairs, first_got = _local_step(a, early, late, mx, mc, conv_w, ci)

    dm16, gb_mod = _mod_bwd_sum(dm_all.reshape(8, 8, 6 * D_MODEL))
    dm_loc = lax.dynamic_slice_in_dim(dm16, chip * MOD_COLS, MOD_COLS, 1)
    gw_mod, gcc = _mod_bwd_w(dm_loc, s_b, c_ctx, w_mod_b)

    tot = _sum_slots(small_all.reshape(8, -1, ROW), "sum_small_grads")
    small = dict(zip(SMALL_NAMES + ("loss",), _unpack_rows(tot, [g[n].shape for n in SMALL_NAMES] + [(1,)])))
    loss = small["loss"].reshape(())
    grads = {n: small[n].reshape(a[n].shape) for n in SMALL_NAMES if n != "conv_w"}
    gcc_tot = _sum_slots(_all_gather8(gcc, "gather_c_ctx").reshape(8, 8, D_MODEL), "sum_c_ctx")
    grads["c_ctx"] = (0.5 * gcc_tot[0]).reshape(a["c_ctx"].shape)
    grads["conv_w"] = lax.dynamic_slice_in_dim(small["conv_w"], chip * (D_FF // N_CHIPS), D_FF // N_CHIPS, 1)[None]
    grads["b_mod"] = gb_mod
    grads["w_mod"] = gw_mod[None]

    last = [g[n] for n in LAST_GRADS]
    last_pairs = [_pair_sum(gf, rv, k, ci, "rs_pair_" + n)
                  for gf, rv, k, n in zip(last, _rs_sibling(last, LAST_KINDS, "rs_sibling_last"), LAST_KINDS, LAST_GRADS)]
    last_got = _rs_chips(last_pairs, LAST_KINDS)
    pos = jnp.stack([ci, chip])
    order = FIRST_GRADS + LAST_GRADS
    order_kinds = FIRST_KINDS + LAST_KINDS
    halves = [_sum_chips(p, t, k, pos, "rs_sum_" + n)
              for p, t, k, n in zip(first_pairs + last_pairs, list(first_got) + list(last_got), order_kinds, order)]
    for n, t in zip(order, _rs_back(halves, order_kinds)):
        grads[n] = t[None]

    delta, new_m, new_v = {}, {}, {}
    for n in BIG_NAMES + ("w_mod",):
        for dst, t in zip((delta, new_m, new_v), _adamw(a[n][0], grads[n][0], a["m_" + n][0], a["v_" + n][0], "adamw_" + n)):
            dst[n] = t[None]
    rest = [n for n in WEIGHT_NAMES if n not in BIG_NAMES and n != "w_mod"]
    shapes = [a[n].shape for n in rest]
    pr = lambda pre: _pack_rows([a[pre + n] for n in rest])
    for dst, t in zip((delta, new_m, new_v),
                      _adamw(pr(""), _pack_rows([grads[n] for n in rest]), pr("m_"), pr("v_"), "adamw_small")):
        dst.update(zip(rest, _unpack_rows(t, shapes)))

    return (loss, grad_x[None], *[grads[n] for n in WEIGHT_NAMES], *[delta[n] for n in WEIGHT_NAMES],
            *[new_m[n] for n in WEIGHT_NAMES], *[new_v[n] for n in WEIGHT_NAMES])


def kernel(x, c, ctx, c_ctx, w_mod, b_mod, norm1_w, w_in, s5_lambda_re_f, s5_lambda_im_f, s5_log_step_f, s5_lambda_re_b, s5_lambda_im_b, s5_log_step_b, s5_b_re, s5_b_im, s5_c_re, s5_c_im, s5_d, s5_w_glu, s5_b_glu, ret_log_decay_f, ret_log_decay_b, w_out, norm2_w, w_up, conv_w, conv_b, w_down, final_norm_w, loss_target, m_c_ctx, m_w_mod, m_b_mod, m_norm1_w, m_w_in, m_s5_lambda_re_f, m_s5_lambda_im_f, m_s5_log_step_f, m_s5_lambda_re_b, m_s5_lambda_im_b, m_s5_log_step_b, m_s5_b_re, m_s5_b_im, m_s5_c_re, m_s5_c_im, m_s5_d, m_s5_w_glu, m_s5_b_glu, m_ret_log_decay_f, m_ret_log_decay_b, m_w_out, m_norm2_w, m_w_up, m_conv_w, m_conv_b, m_w_down, m_final_norm_w, v_c_ctx, v_w_mod, v_b_mod, v_norm1_w, v_w_in, v_s5_lambda_re_f, v_s5_lambda_im_f, v_s5_log_step_f, v_s5_lambda_re_b, v_s5_lambda_im_b, v_s5_log_step_b, v_s5_b_re, v_s5_b_im, v_s5_c_re, v_s5_c_im, v_s5_d, v_s5_w_glu, v_s5_b_glu, v_ret_log_decay_f, v_ret_log_decay_b, v_w_out, v_norm2_w, v_w_up, v_conv_w, v_conv_b, v_w_down, v_final_norm_w):
    return _step(dict(locals()))
```

```python
import functools
import math

import jax
import jax.numpy as jnp
from jax import lax
from jax.experimental import pallas as pl
from jax.experimental.pallas import tpu as pltpu

F32 = jnp.float32
BF16 = jnp.bfloat16

D_MODEL = 1024
S5_WIDTH = 512
S5_GROUPS = 32
S5_GROUP = 16
S5_STATE = 64
RET_WIDTH = 512
RET_HEADS = 4
RET_DH = 128
RET_CHUNK = 256
GRID_W = 64
ROPE_THETA = 10000.0
D_FF = 2816
NORM_EPS = 1e-6
IN_COLS = S5_WIDTH + 4 * RET_WIDTH

S5_T = 16
S5_NB = 4
S5_BW = S5_T * 128
S5_SW = 8 * 2 * S5_STATE

ADAM_LR, ADAM_B1, ADAM_B2, ADAM_EPS, ADAM_WD, ADAM_STEP = 0.001, 0.9, 0.999, 1e-08, 0.01, 10

VMEM_LIMIT = 56 * 1024 * 1024
MM_TN_VMEM = 40 * 1024 * 1024
MESH_ID = pl.DeviceIdType.MESH


def _params(sem=None):
    return pltpu.CompilerParams(dimension_semantics=sem, vmem_limit_bytes=VMEM_LIMIT)


def _full(shape):
    n = len(shape)
    return pl.BlockSpec(shape, lambda *_: (0,) * n)


def _dot(a, b):
    return jnp.dot(a, b, preferred_element_type=F32)


def _dot_nt(a, b):
    return lax.dot_general(a, b, (((1,), (1,)), ((), ())), preferred_element_type=F32)


def _dot_tn(a, b):
    return lax.dot_general(a, b, (((0,), (0,)), ((), ())), preferred_element_type=F32)


def _dot_hi(a, b):
    return jnp.dot(a, b, preferred_element_type=F32, precision=lax.Precision.HIGHEST)


def _dot_nt_hi(a, b):
    return lax.dot_general(a, b, (((1,), (1,)), ((), ())), preferred_element_type=F32,
                           precision=lax.Precision.HIGHEST)


def _gelu(x):
    return 0.5 * x * (1.0 + jnp.tanh(0.7978845608028654 * (x + 0.044715 * (x * x * x))))


def _gelu_and_grad(x):
    c, ca = 0.7978845608028654, 0.7978845608028654 * 0.044715
    x2 = x * x
    t = jnp.tanh(x * (c + ca * x2))
    h = 0.5 * x
    return h + h * t, 0.5 + 0.5 * t + h * (1.0 - t * t) * (c + 3.0 * ca * x2)


def _sigmoid(x):
    return 1.0 / (1.0 + jnp.exp(-x))


def _silu(x):
    return x * _sigmoid(x)


def _rms_mod(x, nw, sh, sc):
    r = lax.rsqrt(jnp.mean(x * x, axis=-1, keepdims=True) + NORM_EPS)
    return (x * r * nw) * (1.0 + sc) + sh


def _rms(x, nw):
    r = lax.rsqrt(jnp.mean(x * x, axis=-1, keepdims=True) + NORM_EPS)
    return x * r * nw


def _head_norm_gate(y, g):
    mu = jnp.mean(y, axis=-1, keepdims=True)
    yc = y - mu
    var = jnp.mean(yc * yc, axis=-1, keepdims=True)
    return _silu(g) * (yc * lax.rsqrt(var + NORM_EPS))


def _swap_pairs(t):
    lane = lax.broadcasted_iota(jnp.int32, t.shape, 1)
    return jnp.where(lane % 2 == 0, pltpu.roll(t, RET_DH - 1, 1), pltpu.roll(t, 1, 1))


def _rope(t, cos_t, sin_t):
    return t * cos_t + _swap_pairs(t) * sin_t


def _rope_t(dt, cos_t, sin_t):
    return dt * cos_t + _swap_pairs(dt * sin_t)


def _pick(n, prefs):
    for p in prefs:
        if n % p == 0:
            return p
    return n


def _mm_tn(a, b, *, name, gathered=None):
    m, k = a.shape
    n = b.shape[1]
    tn = _pick(n, (1408, 1024, 1280, 512))
    fits = lambda t: 2 * (2 * t * k + 2 * t * tn + 4 * k * tn) <= MM_TN_VMEM
    tm = _pick(m, [t for t in (2816, 2048, 1024, 768, 512, 256) if fits(t)] + [128])
    nj, ni = n // tn, m // tm

    def product(a_ref, b_ref, o_ref):
        @pl.when(pl.program_id(1) == 0)
        def _():
            o_ref[...] = jnp.zeros_like(o_ref)
        o_ref[...] += _dot_tn(a_ref[...], b_ref[...])

    specs = dict(
        grid=(nj, ni),
        in_specs=[pl.BlockSpec((tm, k), lambda j, i: (i, 0)), pl.BlockSpec((tm, tn), lambda j, i: (i, j))],
        out_specs=pl.BlockSpec((k, tn), lambda j, i: (0, j)),
        out_shape=jax.ShapeDtypeStruct((k, n), F32))
    if gathered is None:
        def body(a_ref, b_ref, o_ref):
            product(a_ref, b_ref, o_ref)

        return pl.pallas_call(body, name=name, compiler_params=_params(("parallel", "arbitrary")), **specs)(a, b)

    ng = len(gathered)

    def body_gather(a_ref, b_ref, *rest):
        v_refs, o_ref, all_refs, sems = rest[:ng], rest[ng], rest[ng + 1:2 * ng + 1], rest[2 * ng + 1:]
        phases = [_gather8_phases(v_refs[q], all_refs[q], *sems[3 * q:3 * q + 3], gathered[q].shape[0])
                  for q in range(ng)]
        step = pl.program_id(0) * ni + pl.program_id(1)

        @pl.when(step == 0)
        def _():
            for begin, _ in phases:
                begin()

        product(a_ref, b_ref, o_ref)

        @pl.when(step == nj * ni - 1)
        def _():
            for _, finish in phases:
                finish()

    specs["in_specs"] = specs["in_specs"] + [ANY] * ng
    specs["out_specs"] = [specs["out_specs"]] + [ANY] * ng
    specs["out_shape"] = [specs["out_shape"]] + [jax.ShapeDtypeStruct((8 * v.shape[0], v.shape[1]), v.dtype)
                                                 for v in gathered]
    return pl.pallas_call(body_gather, name=name, scratch_shapes=list(GATHER8_SCRATCH) * ng,
                          compiler_params=_params(("arbitrary", "arbitrary")), **specs)(a, b, *gathered)


TOK_TILE = 256


def _behind(step, last, copies):
    @pl.when(step == 0)
    def _():
        for cp in copies(with_arrivals=False)[0]:
            cp.start()

    @pl.when(step == last)
    def _():
        sends, arrivals = copies()
        for cp in arrivals:
            cp.wait_recv()
        for cp in sends:
            cp.wait_send()


def _s5_put_rows(rows_ref, scr, val):
    nchunk = scr.shape[0]
    for c in range(nchunk):
        scr[c] = val[S5_T * c:S5_T * (c + 1), :]
    for b in range(S5_NB):
        for j in range(S5_T):
            rows_ref[b, :, 128 * j:128 * (j + 1)] = scr[:, j, 128 * b:128 * (b + 1)].astype(BF16)


def _norm_inproj(x, ctx, n1w, mod4, w_in_b, cos_t, sin_t, placed, kinds):
    l, lc = x.shape[0], ctx.shape[0]
    tm = TOK_TILE
    nct = lc // tm
    la = l + lc
    n = len(placed)
    shard_shapes = _gather_shard_shapes(placed, kinds)

    def body(x_ref, c_ref, nw_ref, mod_ref, w_ref, cos_ref, sin_ref, *rest):
        p_ref, h_ref, u_ref = rest[n:n + 3]
        send_sems, recv_sems, u_scr = rest[2 * n + 3:]
        _behind(pl.program_id(0), la // tm - 1,
                functools.partial(_gather_chip_copies, rest[n + 3:2 * n + 3], kinds, shard_shapes, send_sems, recv_sems))
        is_ctx = pl.program_id(0) < nct
        xt = jnp.where(is_ctx, c_ref[...], x_ref[...])
        sh = jnp.where(is_ctx, mod_ref[0:1, :], mod_ref[2:3, :])
        sc = jnp.where(is_ctx, mod_ref[1:2, :], mod_ref[3:4, :])
        hb = _rms_mod(xt, nw_ref[...], sh, sc).astype(BF16)
        h_ref[...] = hb
        p = _dot(hb, w_ref[...])
        p_ref[...] = p
        cs, sn = cos_ref[...], sin_ref[...]
        for h in range(RET_HEADS):
            q_cols = slice(RET_WIDTH + RET_DH * h, RET_WIDTH + RET_DH * (h + 1))
            k_cols = slice(2 * RET_WIDTH + RET_DH * h, 2 * RET_WIDTH + RET_DH * (h + 1))
            p_ref[:, q_cols] = _rope(p[:, q_cols], cs, sn)
            p_ref[:, k_cols] = _rope(p[:, k_cols] * K_SCALE, cs, sn)
        _s5_put_rows(u_ref, u_scr, p[:, 0:S5_WIDTH])

    return pl.pallas_call(
        body, name="norm_inproj", grid=(la // tm,),
        in_specs=[pl.BlockSpec((tm, D_MODEL), lambda i: (jnp.maximum(i - nct, 0), 0)),
                  pl.BlockSpec((tm, D_MODEL), lambda i: (jnp.minimum(i, nct - 1), 0)),
                  _full((1, D_MODEL)), _full((4, D_MODEL)), _full((D_MODEL, IN_COLS)),
                  pl.BlockSpec((tm, RET_DH), lambda i: (i, 0)), pl.BlockSpec((tm, RET_DH), lambda i: (i, 0))] + [ANY] * n,
        out_specs=[pl.BlockSpec((tm, IN_COLS), lambda i: (i, 0)), pl.BlockSpec((tm, D_MODEL), lambda i: (i, 0)),
                   pl.BlockSpec((S5_NB, tm // S5_T, S5_BW), lambda i: (0, i, 0))] + [ANY] * n,
        out_shape=[jax.ShapeDtypeStruct((la, IN_COLS), F32), jax.ShapeDtypeStruct((la, D_MODEL), BF16),
                   jax.ShapeDtypeStruct((S5_NB, la // S5_T, S5_BW), BF16)]
        + [jax.ShapeDtypeStruct(p.shape, p.dtype) for p in placed],
        input_output_aliases={7 + a: 3 + a for a in range(n)},
        scratch_shapes=[pltpu.SemaphoreType.DMA((n, 3)), pltpu.SemaphoreType.DMA((n, 3)),
                        pltpu.VMEM((tm // S5_T, S5_T, S5_WIDTH), F32)],
        compiler_params=_params(("arbitrary",)),
    )(x, ctx, n1w, mod4, w_in_b, cos_t, sin_t, *placed)


def _iota2(shape, dim):
    return lax.broadcasted_iota(jnp.int32, shape, dim)


def _group_mask(rows, cols, row_div, col_div):
    return jnp.where(_iota2((rows, cols), 0) // row_div == _iota2((rows, cols), 1) // col_div, 1.0, 0.0).astype(F32)


def _s5_gen_dir(lre, lim, lst, b_re, b_im, c_re, c_im):
    step = jnp.exp(lst)
    mag = jnp.exp(lre * step)
    ar = mag * jnp.cos(lim * step)
    ai = mag * jnp.sin(lim * step)
    den = lre * lre + lim * lim
    xr = ar - 1.0
    cr = (xr * lre + ai * lim) / den
    ci = (ai * lre - xr * lim) / den
    rexp = _group_mask(128, 8, S5_GROUP, 1)
    are, aie = _dot_hi(rexp, ar), _dot_hi(rexp, ai)
    cre, cie = _dot_hi(rexp, cr), _dot_hi(rexp, ci)
    bbr = cre * b_re - cie * b_im
    bbi = cre * b_im + cie * b_re
    gmask = _group_mask(128, 128, S5_GROUP, S5_GROUP)
    pr, pi = jnp.ones_like(are), jnp.zeros_like(are)
    xs, ys = [], []
    for t in range(S5_T + 1):
        if t < S5_T:
            xs.append(jnp.concatenate([bbr * pr - bbi * pi, bbr * pi + bbi * pr], axis=1))
        ys.append(jnp.concatenate([c_re * pr - c_im * pi, -(c_re * pi + c_im * pr)], axis=1))
        pr, pi = pr * are - pi * aie, pr * aie + pi * are
    gs = [_dot_nt_hi(x_t, ys[0]) * gmask for x_t in xs]
    r16, i16 = ar, ai
    for _ in range(4):
        r16, i16 = r16 * r16 - i16 * i16, 2.0 * r16 * i16
    return xs, ys, gs, jnp.concatenate([r16, i16], axis=1)


def _s5_expand(z):
    return jnp.concatenate([z] * 8, axis=1) * _group_mask(128, S5_SW, S5_GROUP, 128)


def _s5_contract(z):
    zm = z * _group_mask(128, S5_SW, S5_GROUP, 128)
    acc = zm[:, 0:128]
    for k in range(1, 8):
        acc = acc + zm[:, 128 * k:128 * (k + 1)]
    return acc


def _s5_param_specs():
    blk3 = lambda r, c: pl.BlockSpec((1, 1, r, c), lambda b, *_: (0, b, 0, 0))
    dir3 = lambda r, c: pl.BlockSpec((2, 1, r, c), lambda b, *_: (0, b, 0, 0))
    return [dir3(8, S5_STATE), dir3(8, S5_STATE), dir3(8, 1), blk3(128, S5_STATE), blk3(128, S5_STATE),
            blk3(128, S5_STATE), blk3(128, S5_STATE), blk3(1, 128)]


def _s5_gen(lre, lim, lst, b_re, b_im, c_re, c_im, dvec, placed, kinds):
    n = len(placed)
    shard_shapes = _gather_shard_shapes(placed, kinds)

    def body(lre_ref, lim_ref, lst_ref, bre_ref, bim_ref, cre_ref, cim_ref, d_ref, *rest):
        gg_ref, xw_ref, yw_ref, a16_ref = rest[n:n + 4]
        _behind(pl.program_id(0), S5_NB - 1,
                functools.partial(_gather_chip_copies, rest[n + 4:2 * n + 4], kinds, shard_shapes, *rest[2 * n + 4:]))
        eye = _group_mask(128, 128, 1, 1)
        g0 = eye * d_ref[0, 0]
        for dr in range(2):
            xs, ys, gs, a16 = _s5_gen_dir(lre_ref[dr, 0], lim_ref[dr, 0], lst_ref[dr, 0], bre_ref[0, 0],
                                          bim_ref[0, 0], cre_ref[0, 0], cim_ref[0, 0])
            a16_ref[0, dr] = a16
            for j in range(S5_T):
                xw_ref[0, dr, j] = xs[S5_T - 1 - j if dr == 0 else j]
                yw_ref[0, dr, j] = ys[j + 1 if dr == 0 else S5_T - j]
            g0 = g0 + gs[0]
            for t in range(1, S5_T):
                gg_ref[0, (S5_T - 1) + t if dr == 0 else (S5_T - 1) - t] = gs[t]
        gg_ref[0, S5_T - 1] = g0

    blk = pl.BlockSpec((1, 2, S5_T, 128, 128), lambda b: (b, 0, 0, 0, 0))
    return pl.pallas_call(
        body, name="s5_gen", grid=(S5_NB,),
        in_specs=_s5_param_specs() + [ANY] * n,
        out_specs=[pl.BlockSpec((1, 2 * S5_T - 1, 128, 128), lambda b: (b, 0, 0, 0)), blk, blk,
                   pl.BlockSpec((1, 2, 8, 128), lambda b: (b, 0, 0, 0))] + [ANY] * n,
        out_shape=[jax.ShapeDtypeStruct((S5_NB, 2 * S5_T - 1, 128, 128), F32),
                   jax.ShapeDtypeStruct((S5_NB, 2, S5_T, 128, 128), F32),
                   jax.ShapeDtypeStruct((S5_NB, 2, S5_T, 128, 128), F32),
                   jax.ShapeDtypeStruct((S5_NB, 2, 8, 128), F32)]
        + [jax.ShapeDtypeStruct(p.shape, p.dtype) for p in placed],
        input_output_aliases={8 + a: 4 + a for a in range(n)},
        scratch_shapes=[pltpu.SemaphoreType.DMA((n, 3)), pltpu.SemaphoreType.DMA((n, 3))],
        compiler_params=_params(("arbitrary",)),
    )(lre, lim, lst, b_re, b_im, c_re, c_im, dvec, *placed)


def _s5_fill_state_mat(w_scr, src_ref, dr):
    for j in range(S5_T):
        w_scr[128 * j:128 * (j + 1), :] = _s5_expand(src_ref[0, dr, j]).astype(BF16)


def _s5_fill_toeplitz(k_scr, gg_ref):
    for j in range(S5_T):
        for i in range(S5_T):
            k_scr[128 * j:128 * (j + 1), 128 * i:128 * (i + 1)] = gg_ref[0, i - j + (S5_T - 1)].astype(BF16)


S5_GEN_SPECS = [pl.BlockSpec((1, 2 * S5_T - 1, 128, 128), lambda b: (b, 0, 0, 0)),
                pl.BlockSpec((1, 2, S5_T, 128, 128), lambda b: (b, 0, 0, 0, 0))]


def _s5_gen_bwd(lre, lim, lst, b_re, b_im, c_re, c_im, dvec, dg, dx, dy, da16):
    def body(lre_ref, lim_ref, lst_ref, bre_ref, bim_ref, cre_ref, cim_ref, d_ref, dg_ref, dx_ref, dy_ref, da16_ref,
             glre_ref, glim_ref, glst_ref, gbre_ref, gbim_ref, gcre_ref, gcim_ref, gd_ref):
        eye = _group_mask(128, 128, 1, 1)
        gd_ref[0, 0] = jnp.sum(dg_ref[0, S5_T - 1] * eye, axis=0, keepdims=True)
        gb = [None, None, None, None]
        for dr in range(2):
            args = (lre_ref[dr, 0], lim_ref[dr, 0], lst_ref[dr, 0], bre_ref[0, 0], bim_ref[0, 0],
                    cre_ref[0, 0], cim_ref[0, 0])
            _, vjp = jax.vjp(_s5_gen_dir, *args)
            dxs = [dx_ref[0, dr, S5_T - 1 - t if dr == 0 else t] for t in range(S5_T)]
            dys = [jnp.zeros((128, 128), F32)] + [dy_ref[0, dr, t - 1 if dr == 0 else S5_T - t]
                                                  for t in range(1, S5_T + 1)]
            dgs = [dg_ref[0, (S5_T - 1) + t if dr == 0 else (S5_T - 1) - t] for t in range(S5_T)]
            g = vjp((dxs, dys, dgs, da16_ref[0, dr]))
            glre_ref[dr, 0] = g[0]
            glim_ref[dr, 0] = g[1]
            glst_ref[dr, 0] = g[2]
            for q in range(4):
                gb[q] = g[3 + q] if gb[q] is None else gb[q] + g[3 + q]
        gbre_ref[0, 0] = gb[0]
        gbim_ref[0, 0] = gb[1]
        gcre_ref[0, 0] = gb[2]
        gcim_ref[0, 0] = gb[3]

    shp = lambda a: jax.ShapeDtypeStruct(a.shape, F32)
    return pl.pallas_call(
        body, name="s5_gen_bwd", grid=(S5_NB,),
        in_specs=_s5_param_specs() + [
            pl.BlockSpec((1, 2 * S5_T - 1, 128, 128), lambda b: (b, 0, 0, 0)),
            pl.BlockSpec((1, 2, S5_T, 128, 128), lambda b: (b, 0, 0, 0, 0)),
            pl.BlockSpec((1, 2, S5_T, 128, 128), lambda b: (b, 0, 0, 0, 0)),
            pl.BlockSpec((1, 2, 8, 128), lambda b: (b, 0, 0, 0))],
        out_specs=_s5_param_specs(),
        out_shape=[shp(lre), shp(lim), shp(lst), shp(b_re), shp(b_im), shp(c_re), shp(c_im), shp(dvec)],
        compiler_params=_params(("parallel",)),
    )(lre, lim, lst, b_re, b_im, c_re, c_im, dvec, dg, dx, dy, da16)


def _s5_put_groups(o_ref, dr, val):
    for gi in range(8):
        o_ref[dr, :, gi, :] = val[:, 128 * gi:128 * (gi + 1)]


def _s5_get_groups(s_ref, dr, n=8):
    return jnp.concatenate([s_ref[dr, :, gi, :] for gi in range(n)], axis=1).astype(BF16)


def _s5_to_states(u3, blocks, name):
    cn = u3.shape[1]

    def body(u_ref, b_ref, o_ref, w_scr):
        u = u_ref[0]
        for dr in range(2):
            _s5_fill_state_mat(w_scr, b_ref, dr)
            _s5_put_groups(o_ref, dr, _dot(u, w_scr[...]))

    return pl.pallas_call(
        body, name=name, grid=(S5_NB,),
        in_specs=[pl.BlockSpec((1, cn, S5_BW), lambda b: (b, 0, 0)), S5_GEN_SPECS[1]],
        out_specs=pl.BlockSpec((2, cn, 8, 128), lambda b: (0, 0, b, 0)),
        out_shape=jax.ShapeDtypeStruct((2, cn, S5_GROUPS, 128), F32),
        scratch_shapes=[pltpu.VMEM((S5_BW, S5_SW), BF16)],
        compiler_params=_params(("parallel",)),
    )(u3, blocks)


def _s5_from_states(u3, gg, st, blocks, transposed, name):
    cn = u3.shape[1]

    def body(u_ref, g_ref, s_ref, b_ref, o_ref, k_scr, w_scr):
        u = u_ref[0]
        _s5_fill_toeplitz(k_scr, g_ref)
        y = _dot_nt(u, k_scr[...]) if transposed else _dot(u, k_scr[...])
        for dr in range(2):
            _s5_fill_state_mat(w_scr, b_ref, dr)
            y = y + _dot_nt(_s5_get_groups(s_ref, dr), w_scr[...])
        for i in range(S5_T):
            o_ref[:, i, :] = y[:, 128 * i:128 * (i + 1)]

    return pl.pallas_call(
        body, name=name, grid=(S5_NB,),
        in_specs=[pl.BlockSpec((1, cn, S5_BW), lambda b: (b, 0, 0)), S5_GEN_SPECS[0],
                  pl.BlockSpec((2, cn, 8, 128), lambda b: (0, 0, b, 0)), S5_GEN_SPECS[1]],
        out_specs=pl.BlockSpec((cn, S5_T, 128), lambda b: (0, 0, b)),
        out_shape=jax.ShapeDtypeStruct((cn, S5_T, S5_WIDTH), F32),
        scratch_shapes=[pltpu.VMEM((S5_BW, S5_BW), BF16), pltpu.VMEM((S5_BW, S5_SW), BF16)],
        compiler_params=_params(("parallel",)),
    )(u3, gg, st, blocks)


def _s5_a_forms(a):
    ra = pltpu.roll(a, S5_STATE, 1)
    low = _iota2(a.shape, 1) < S5_STATE
    return jnp.where(low, a, ra), jnp.where(low, -ra, a)


def _s5_scan(sloc, a16, ncc, placed, kinds):
    cn = sloc.shape[1]
    n = len(placed)
    shard_shapes = _gather_shard_shapes(placed, kinds)

    def body(s_ref, a_ref, *rest):
        h_ref = rest[n]
        sends, arrivals = _gather_chip_copies(rest[n + 1:2 * n + 1], kinds, shard_shapes, *rest[2 * n + 1:])
        for cp in sends:
            cp.start()
        forms = [_s5_a_forms(a_ref[dr]) for dr in range(2)]

        def step(s, hs):
            out = []
            for dr in range(2):
                arr, aii = forms[dr]
                h, rh = hs[dr]
                c = s if dr == 0 else jnp.where(s < ncc, ncc - 1 - s, cn - 1 - (s - ncc))
                h_ref[dr, c] = h
                sc = s_ref[dr, c]
                out.append((h * arr + rh * aii + sc, rh * arr - h * aii + pltpu.roll(sc, S5_STATE, 1)))
            return tuple(out)

        zero = jnp.zeros((S5_GROUPS, 128), F32)
        lax.fori_loop(0, cn, step, ((zero, zero), (zero, zero)), unroll=4)
        for cp in arrivals:
            cp.wait_recv()
        for cp in sends:
            cp.wait_send()

    vmem = pl.BlockSpec(memory_space=pltpu.VMEM)
    return pl.pallas_call(
        body, name="s5_scan",
        in_specs=[vmem, vmem] + [ANY] * n, out_specs=[vmem] + [ANY] * n,
        out_shape=[jax.ShapeDtypeStruct(sloc.shape, F32)] + [jax.ShapeDtypeStruct(p.shape, p.dtype) for p in placed],
        input_output_aliases={2 + a: 1 + a for a in range(n)},
        scratch_shapes=[pltpu.SemaphoreType.DMA((n, 3)), pltpu.SemaphoreType.DMA((n, 3))],
        compiler_params=_params(),
    )(sloc, a16, *placed)


def _s5_scan_bwd(e, hs, a16, ncc):
    cn = e.shape[1]

    def body(e_ref, h_ref, a_ref, ds_ref, da_ref):
        forms = [_s5_a_forms(a_ref[dr]) for dr in range(2)]
        low = _iota2((S5_GROUPS, 128), 1) < S5_STATE

        def step(s, carry):
            out = []
            r = cn - 1 - s
            for dr in range(2):
                arr, aii = forms[dr]
                g, rg, da = carry[dr]
                c = r if dr == 0 else jnp.where(r < ncc, ncc - 1 - r, cn - 1 - (r - ncc))
                ds_ref[dr, c] = g
                h = h_ref[dr, c]
                rh = pltpu.roll(h, S5_STATE, 1)
                da = da + jnp.where(low, g * h + rg * rh, g * rh - rg * h)
                ec = e_ref[dr, c]
                out.append((ec + g * arr - rg * aii, pltpu.roll(ec, S5_STATE, 1) + rg * arr + g * aii, da))
            return tuple(out)

        zero = jnp.zeros((S5_GROUPS, 128), F32)
        res = lax.fori_loop(0, cn, step, ((zero, zero, zero), (zero, zero, zero)), unroll=4)
        da_ref[0] = res[0][2]
        da_ref[1] = res[1][2]

    return pl.pallas_call(
        body, name="s5_scan_bwd",
        out_shape=[jax.ShapeDtypeStruct(e.shape, F32), jax.ShapeDtypeStruct((2, S5_GROUPS, 128), F32)],
        compiler_params=_params(),
    )(e, hs, a16)


def _s5_bwd_kb(p3, dy3):
    cn = p3.shape[1]
    half = S5_T // 2

    def body(u_ref, d_ref, o_ref):
        q = pl.program_id(1)

        @pl.when(q == 0)
        def _():
            o_ref[...] = jnp.zeros_like(o_ref)

        dk = _dot_tn(u_ref[0], d_ref[0])
        for j in range(S5_T):
            for i in range(half):
                o_ref[0, half * q + i - j + (S5_T - 1)] += dk[128 * j:128 * (j + 1), 128 * i:128 * (i + 1)]

    return pl.pallas_call(
        body, name="s5_bwd_kb", grid=(S5_NB, 2),
        in_specs=[pl.BlockSpec((1, cn, S5_BW), lambda b, q: (b, 0, 0)),
                  pl.BlockSpec((1, cn, S5_BW // 2), lambda b, q: (b, 0, q))],
        out_specs=pl.BlockSpec((1, 2 * S5_T - 1, 128, 128), lambda b, q: (b, 0, 0, 0)),
        out_shape=jax.ShapeDtypeStruct((S5_NB, 2 * S5_T - 1, 128, 128), F32),
        compiler_params=_params(("parallel", "arbitrary")),
    )(p3, dy3)


def _s5_bwd_w(u3, st, name):
    cn = u3.shape[1]

    def body(u_ref, s_ref, w_ref):
        dw = _dot_tn(u_ref[0], _s5_get_groups(s_ref, 0))
        for j in range(S5_T):
            w_ref[0, 0, j] = _s5_contract(dw[128 * j:128 * (j + 1), :])

    return pl.pallas_call(
        body, name=name, grid=(S5_NB, 2),
        in_specs=[pl.BlockSpec((1, cn, S5_BW), lambda b, q: (b, 0, 0)),
                  pl.BlockSpec((1, cn, 8, 128), lambda b, q: (q, 0, b, 0))],
        out_specs=pl.BlockSpec((1, 1, S5_T, 128, 128), lambda b, q: (b, q, 0, 0, 0)),
        out_shape=jax.ShapeDtypeStruct((S5_NB, 2, S5_T, 128, 128), F32),
        compiler_params=_params(("parallel", "parallel")),
    )(u3, st)


K_SCALE = RET_DH ** -0.5
G_COL = 16


def _ret_chunk_of(step, ncc, nch, rev):
    if not rev:
        return step
    return jnp.where(step < ncc, ncc - 1 - step, nch - 1 - (step - ncc))


def _ret_decay(ld, rev):
    c = _iota2((RET_CHUNK, RET_CHUNK), 0).astype(F32)
    m = _iota2((RET_CHUNK, RET_CHUNK), 1).astype(F32)
    diff = (m - c) if rev else (c - m)
    keep = (diff > 0) if rev else (diff >= 0)
    expo = jnp.maximum(diff, 0.0)
    dm = jnp.where(keep, jnp.exp(ld * expo), 0.0)
    xi_e = (RET_CHUNK - c) if rev else (c + 1.0)
    zeta_e = c if rev else (RET_CHUNK - 1.0 - c)
    return dm, expo, jnp.exp(ld * xi_e), xi_e, jnp.exp(ld * zeta_e), zeta_e


RET_TABLES = 7


def _ret_tables(ld2):
    def body(ld_ref, t_ref):
        dr, h = pl.program_id(0), pl.program_id(1)
        ldh = ld_ref[dr, h]
        for rev in (False, True):
            @pl.when(dr == int(rev))
            def _(rev=rev):
                dm, expo, xi, xi_e, zeta, zeta_e = _ret_decay(ldh, rev)
                t_ref[0, 0, 0] = dm
                t_ref[0, 0, 1] = dm * expo
                t_ref[0, 0, 2] = xi
                t_ref[0, 0, 3] = xi * xi_e
                t_ref[0, 0, 4] = zeta
                t_ref[0, 0, 5] = zeta * zeta_e
                t_ref[0, 0, 6] = jnp.zeros_like(dm) + jnp.exp(ldh * RET_CHUNK)

    return pl.pallas_call(
        body, name="ret_tables", grid=(2, RET_HEADS),
        in_specs=[pl.BlockSpec(memory_space=pltpu.SMEM)],
        out_specs=pl.BlockSpec((1, 1, RET_TABLES, RET_CHUNK, RET_CHUNK), lambda d, h: (d, h, 0, 0, 0)),
        out_shape=jax.ShapeDtypeStruct((2, RET_HEADS, RET_TABLES, RET_CHUNK, RET_CHUNK), F32),
        compiler_params=_params(("parallel", "parallel")),
    )(ld2)


def _ret_specs(nch, ncc, rev, step_of):
    chunk = lambda n: _ret_chunk_of(step_of(n), ncc, nch, rev)
    cols = [pl.BlockSpec((RET_CHUNK, RET_WIDTH), functools.partial(lambda n, cb: (chunk(n), cb), cb=cb))
            for cb in (1, 2, 3)]
    return cols, pl.BlockSpec((RET_CHUNK, RET_WIDTH), lambda n: (chunk(n), 0))


def _ret_scan(p_all, tabs, ncc):
    la = p_all.shape[0]
    nch = la // RET_CHUNK

    def body(t_ref, qf, kf, vf, qb, kb, vb, of_ref, ob_ref, ssf_ref, ssb_ref, s_scr):
        @pl.when(pl.program_id(0) == 0)
        def _():
            s_scr[...] = jnp.zeros_like(s_scr)

        for dr, (q_ref, k_ref, v_ref, o_ref, ss_ref) in enumerate(
                ((qf, kf, vf, of_ref, ssf_ref), (qb, kb, vb, ob_ref, ssb_ref))):
            for h in range(RET_HEADS):
                sl = slice(RET_DH * h, RET_DH * (h + 1))
                dm, xi, zeta = t_ref[dr, h, 0], t_ref[dr, h, 2, :, 0:RET_DH], t_ref[dr, h, 4, :, 0:RET_DH]
                q, k = q_ref[:, sl], k_ref[:, sl]
                vh = v_ref[:, sl].astype(BF16)
                s = s_scr[dr, h]
                ss_ref[0, h] = s
                sc = (_dot_nt(q.astype(BF16), k.astype(BF16)) * dm).astype(BF16)
                o_ref[:, sl] = _dot(sc, vh) + _dot((q * xi).astype(BF16), s.astype(BF16))
                s_scr[dr, h] = t_ref[dr, h, 6, 0:RET_DH, 0:RET_DH] * s + _dot_tn((k * zeta).astype(BF16), vh)

    in_f, out_f = _ret_specs(nch, ncc, False, lambda n: n)
    in_b, out_b = _ret_specs(nch, ncc, True, lambda n: n)
    ss_spec = pl.BlockSpec((1, RET_HEADS, RET_DH, RET_DH), lambda n: (n, 0, 0, 0))
    o_shape = jax.ShapeDtypeStruct((la, RET_WIDTH), F32)
    ss_shape = jax.ShapeDtypeStruct((nch, RET_HEADS, RET_DH, RET_DH), F32)
    return pl.pallas_call(
        body, name="ret_scan", grid=(nch,),
        in_specs=[_full(tabs.shape)] + in_f + in_b,
        out_specs=[out_f, out_b, ss_spec, ss_spec],
        out_shape=[o_shape, o_shape, ss_shape, ss_shape],
        scratch_shapes=[pltpu.VMEM((2, RET_HEADS, RET_DH, RET_DH), F32)],
        compiler_params=_params(("arbitrary",)),
    )(tabs, p_all, p_all, p_all, p_all, p_all, p_all)


def _ret_scan_bwd(p_all, tabs, ssf, ssb, dy_all, ncc):
    la = p_all.shape[0]
    nch = la // RET_CHUNK

    def body(t_ref, qf, kf, vf, dof, ssf_ref, qb, kb, vb, dob_, ssb_ref,
             dqf, dkf, dvf, dqb, dkb, dvb, dld_ref, ds_scr):
        @pl.when(pl.program_id(0) == 0)
        def _():
            ds_scr[...] = jnp.zeros_like(ds_scr)
            dld_ref[...] = jnp.zeros_like(dld_ref)

        for dr, (q_ref, k_ref, v_ref, do_ref, ss_ref, dq_ref, dk_ref, dv_ref) in enumerate(
                ((qf, kf, vf, dof, ssf_ref, dqf, dkf, dvf), (qb, kb, vb, dob_, ssb_ref, dqb, dkb, dvb))):
            on_ctx = _ret_chunk_of(nch - 1 - pl.program_id(0), ncc, nch, dr == 1) < ncc
            for h in range(RET_HEADS):
                sl = slice(RET_DH * h, RET_DH * (h + 1))
                dm, dm_d = t_ref[dr, h, 0], t_ref[dr, h, 1]
                xi, xi_d, zeta, zeta_d = [t_ref[dr, h, t, :, 0:RET_DH] for t in (2, 3, 4, 5)]
                gc = t_ref[dr, h, 6, 0:RET_DH, 0:RET_DH]
                q, k = q_ref[:, sl], k_ref[:, sl]
                q16, k16, v16 = q.astype(BF16), k.astype(BF16), v_ref[:, sl].astype(BF16)
                s = ss_ref[0, h]
                s16 = s.astype(BF16)
                ds_in = ds_scr[dr, h]
                ds16 = ds_in.astype(BF16)
                do16 = jnp.where(on_ctx, 0.0, do_ref[:, sl]).astype(BF16)
                qk = _dot_nt(q16, k16)
                dsv = _dot_nt(do16, v16)
                dsc = (dsv * dm).astype(BF16)
                sc16 = (qk * dm).astype(BF16)
                dos = _dot_nt(do16, s16)
                vds = _dot_nt(v16, ds16)
                dq_ref[:, sl] = _dot(dsc, k16) + dos * xi
                dk_ref[:, sl] = _dot_tn(dsc, q16) + vds * zeta
                dv_ref[:, sl] = _dot_tn(sc16, do16) + _dot((k * zeta).astype(BF16), ds16)
                ds_scr[dr, h] = _dot_tn((q * xi).astype(BF16), do16) + gc * ds_in
                dld = (jnp.sum(dsv * qk * dm_d) + jnp.sum(q * dos * xi_d + k * vds * zeta_d)
                       + RET_CHUNK * jnp.sum(gc * s * ds_in))
                dld_ref[dr, h] += dld

    back = lambda n: nch - 1 - n
    in_f, out_f = _ret_specs(nch, ncc, False, back)
    in_b, out_b = _ret_specs(nch, ncc, True, back)
    ss_spec = pl.BlockSpec((1, RET_HEADS, RET_DH, RET_DH), lambda n: (nch - 1 - n, 0, 0, 0))
    shp = jax.ShapeDtypeStruct((la, RET_WIDTH), F32)
    dy_spec = lambda rev: pl.BlockSpec(
        (RET_CHUNK, RET_WIDTH), lambda n: (jnp.maximum(_ret_chunk_of(nch - 1 - n, ncc, nch, rev) - ncc, 0), 0))
    return pl.pallas_call(
        body, name="ret_scan_bwd", grid=(nch,),
        in_specs=[_full(tabs.shape)] + in_f + [dy_spec(False), ss_spec] + in_b + [dy_spec(True), ss_spec],
        out_specs=[out_f, out_f, out_f, out_b, out_b, out_b, _full((2, RET_HEADS, 8, 128))],
        out_shape=[shp] * 6 + [jax.ShapeDtypeStruct((2, RET_HEADS, 8, 128), F32)],
        scratch_shapes=[pltpu.VMEM((2, RET_HEADS, RET_DH, RET_DH), F32)],
        compiler_params=_params(("arbitrary",)),
    )(tabs, p_all, p_all, p_all, dy_all, ssf, p_all, p_all, p_all, dy_all, ssb)


def _in_bwd(dqf, dkf, dvf, dqb, dkb, dvb, du, dg, cos_t, sin_t, w_in_b, x, ctx, n1w, mod4, dx1):
    l, lc = x.shape[0], ctx.shape[0]
    la = l + lc
    tm = TOK_TILE
    nct = lc // tm

    def body(dqf_ref, dkf_ref, dvf_ref, dqb_ref, dkb_ref, dvb_ref, du_ref, dg_ref, cos_ref, sin_ref,
             w_ref, x_ref, c_ref, nw_ref, mod_ref, dx1_ref, dp_ref, gx_ref, acc_ref):
        i = pl.program_id(0)
        is_ctx = i < nct

        @pl.when(i == 0)
        def _():
            acc_ref[...] = jnp.zeros_like(acc_ref)

        cs, sn = cos_ref[...], sin_ref[...]
        def piece(k, val):
            cols = slice(S5_WIDTH * k, S5_WIDTH * (k + 1))
            dp_ref[:, cols] = val.astype(BF16)
            return _dot_nt(dp_ref[:, cols], w_ref[:, cols])

        dh1 = piece(0, du_ref[...])
        dh1 = dh1 + piece(3, dvf_ref[...] + dvb_ref[...])
        dh1 = dh1 + piece(4, jnp.where(is_ctx, 0.0, dg_ref[...]))
        for k, (f_ref, b_ref, scale) in ((1, (dqf_ref, dqb_ref, 1.0)), (2, (dkf_ref, dkb_ref, K_SCALE))):
            heads = [_rope_t(f_ref[:, RET_DH * h:RET_DH * (h + 1)] + b_ref[:, RET_DH * h:RET_DH * (h + 1)], cs, sn) * scale
                     for h in range(RET_HEADS)]
            dh1 = dh1 + piece(k, jnp.concatenate(heads, axis=1))
        xt = jnp.where(is_ctx, c_ref[...], x_ref[...])
        sh = jnp.where(is_ctx, mod_ref[0:1, :], mod_ref[2:3, :])
        sc = jnp.where(is_ctx, mod_ref[1:2, :], mod_ref[3:4, :])
        _, vjp = jax.vjp(_rms_mod, xt, nw_ref[...], sh, sc)
        dx, dnw, dsh, dsc = vjp(dh1)
        gx_ref[...] = dx + dx1_ref[...]
        cf = jnp.where(is_ctx, 1.0, 0.0)
        acc_ref[0:1, :] += dnw
        acc_ref[1:2, :] += cf * dsh
        acc_ref[2:3, :] += cf * dsc
        acc_ref[3:4, :] += (1.0 - cf) * dsh
        acc_ref[4:5, :] += (1.0 - cf) * dsc

    row = pl.BlockSpec((tm, RET_WIDTH), lambda i: (i, 0))
    tab = pl.BlockSpec((tm, RET_DH), lambda i: (i, 0))
    xrow = pl.BlockSpec((tm, D_MODEL), lambda i: (jnp.maximum(i - nct, 0), 0))
    return pl.pallas_call(
        body, name="in_bwd", grid=(la // tm,),
        in_specs=[row] * 7 + [pl.BlockSpec((tm, RET_WIDTH), lambda i: (jnp.maximum(i - nct, 0), 0)),
                              tab, tab, _full((D_MODEL, IN_COLS)), xrow,
                              pl.BlockSpec((tm, D_MODEL), lambda i: (jnp.minimum(i, nct - 1), 0)),
                              _full((1, D_MODEL)), _full((4, D_MODEL)), xrow],
        out_specs=[pl.BlockSpec((tm, IN_COLS), lambda i: (i, 0)), xrow, _full((8, D_MODEL))],
        out_shape=[jax.ShapeDtypeStruct((la, IN_COLS), BF16), jax.ShapeDtypeStruct((l, D_MODEL), F32),
                   jax.ShapeDtypeStruct((8, D_MODEL), F32)],
        compiler_params=_params(("arbitrary",)),
    )(dqf, dkf, dvf, dqb, dkb, dvb, du, dg, cos_t, sin_t, w_in_b, x, ctx, n1w, mod4, dx1)


def _outproj_up(x, y_all, of, ob, p_all, w_glu_b, b_glu, w_out_b, mod3, n2w, w_up_b, nct):
    l = x.shape[0]
    tm = TOK_TILE

    def body(x_ref, y_ref, of_ref, ob_ref, g_ref, wg_ref, bg_ref, wo_ref, mod_ref, nw_ref, wu_ref,
             x1_ref, mix_ref, h2_ref, up_ref, mb_ref, yr_ref):
        yg = _gelu(y_ref[...])
        mb_ref[:, 0:S5_WIDTH] = (yg * _sigmoid(_dot(yg.astype(BF16), wg_ref[...]) + bg_ref[...])).astype(BF16)
        yr = of_ref[...] + ob_ref[...]
        yr_ref[...] = yr
        for h in range(RET_HEADS):
            sl = slice(RET_DH * h, RET_DH * (h + 1))
            mb_ref[:, S5_WIDTH + RET_DH * h:S5_WIDTH + RET_DH * (h + 1)] = (
                _head_norm_gate(yr[:, sl], g_ref[:, sl]).astype(BF16))
        mix = _dot(mb_ref[...], wo_ref[...])
        mix_ref[...] = mix
        x1 = x_ref[...] + mod_ref[0:1, :] * mix
        x1_ref[...] = x1
        h2 = _rms_mod(x1, nw_ref[...], mod_ref[1:2, :], mod_ref[2:3, :]).astype(BF16)
        h2_ref[...] = h2
        up_ref[...] = _dot(h2, wu_ref[...])

    row = lambda w: pl.BlockSpec((tm, w), lambda i: (i, 0))
    arow = pl.BlockSpec((tm, RET_WIDTH), lambda i: (i + nct, 0))
    return pl.pallas_call(
        body, name="outproj_up", grid=(l // tm,),
        in_specs=[row(D_MODEL), arow, arow, arow, pl.BlockSpec((tm, RET_WIDTH), lambda i: (i + nct, G_COL // 4)),
                  _full((S5_WIDTH, S5_WIDTH)), _full((1, S5_WIDTH)), _full((D_MODEL, D_MODEL)), _full((3, D_MODEL)),
                  _full((1, D_MODEL)), _full((D_MODEL, 2 * D_FF))],
        out_specs=[row(D_MODEL), row(D_MODEL), row(D_MODEL), row(2 * D_FF), row(D_MODEL), row(RET_WIDTH)],
        out_shape=[jax.ShapeDtypeStruct((l, D_MODEL), F32), jax.ShapeDtypeStruct((l, D_MODEL), F32),
                   jax.ShapeDtypeStruct((l, D_MODEL), BF16), jax.ShapeDtypeStruct((l, 2 * D_FF), F32),
                   jax.ShapeDtypeStruct((l, D_MODEL), BF16), jax.ShapeDtypeStruct((l, RET_WIDTH), F32)],
        compiler_params=_params(("parallel",)),
    )(x, y_all, of, ob, p_all, w_glu_b, b_glu, w_out_b, mod3, n2w, w_up_b)


HALO = 8


def _conv_taps(g, prev_row, next_row):
    t = g.shape[0]
    r = _iota2(g.shape, 0)
    gprev = jnp.where(r == 0, prev_row, pltpu.roll(g, 1, 0))
    gnext = jnp.where(r == t - 1, next_row, pltpu.roll(g, t - 1, 0))
    return gprev, gnext


def _ffn_loss(up, x1, conv_w, conv_b, w_down_b, gate, fnw, tgt):
    l = x1.shape[0]
    tm = TOK_TILE
    nt = l // tm
    hb = tm // HALO

    cw = 256

    def body(up_a, up_g, hp_ref, hn_ref, x1_ref, cw_ref, cb_ref, wd_ref, gate_ref, fn_ref, tgt_ref,
             act_ref, dx2_ref, ddn_ref, dact_ref, acc_ref, ddn_scr):
        step = pl.program_id(0)
        i = jnp.minimum(step, nt - 1)

        @pl.when(step == 0)
        def _():
            acc_ref[...] = jnp.zeros_like(acc_ref)
            ddn_scr[...] = jnp.zeros_like(ddn_scr)

        ddn_prev = ddn_scr[...]
        dn = jnp.zeros((tm, D_MODEL), F32)
        for c in range(D_FF // cw):
            cols = slice(cw * c, cw * (c + 1))
            g = up_g[:, cols]
            prev_row = jnp.where(i == 0, 0.0, hp_ref[HALO - 1:HALO, cols])
            next_row = jnp.where(i == nt - 1, 0.0, hn_ref[0:1, cols])
            gprev, gnext = _conv_taps(g, prev_row, next_row)
            gc = cb_ref[:, cols] + gprev * cw_ref[0:1, cols] + g * cw_ref[1:2, cols] + gnext * cw_ref[2:3, cols]
            act = (_gelu(gc) * up_a[:, cols]).astype(BF16)
            act_ref[:, cols] = act
            dn = dn + _dot(act, wd_ref[cols, :])
            dact_ref[:, cols] = _dot_nt(ddn_prev, wd_ref[cols, :])
        x2 = x1_ref[...] + gate_ref[...] * dn
        y, vjp = jax.vjp(_rms, x2, fn_ref[...])
        err = y - tgt_ref[...]
        dx2, dfn = vjp(err * (1.0 / D_MODEL))
        dx2_ref[...] = dx2
        ddn = (dx2 * gate_ref[...]).astype(BF16)
        ddn_ref[...] = ddn
        ddn_scr[...] = ddn
        live = step < nt
        acc_ref[0:1, :] += jnp.where(live, dfn, 0.0)
        acc_ref[1:2, :] += jnp.where(live, jnp.sum(dx2 * dn, axis=0, keepdims=True), 0.0)
        acc_ref[2:3, :] += jnp.where(live, (0.5 / D_MODEL) * jnp.sum(err * err), 0.0)

    tile = lambda s: jnp.minimum(s, nt - 1)
    row = lambda w, cb=0: pl.BlockSpec((tm, w), lambda s: (tile(s), cb))
    last = l // HALO - 1
    return pl.pallas_call(
        body, name="ffn_loss", grid=(nt + 1,),
        in_specs=[row(D_FF, 0), row(D_FF, 1),
                  pl.BlockSpec((HALO, D_FF), lambda s: (jnp.maximum(tile(s) * hb - 1, 0), 1)),
                  pl.BlockSpec((HALO, D_FF), lambda s: (jnp.minimum((tile(s) + 1) * hb, last), 1)),
                  row(D_MODEL), _full((3, D_FF)), _full((1, D_FF)), _full((D_FF, D_MODEL)),
                  _full((1, D_MODEL)), _full((1, D_MODEL)), row(D_MODEL)],
        out_specs=[row(D_FF), row(D_MODEL), row(D_MODEL),
                   pl.BlockSpec((tm, D_FF), lambda s: (jnp.maximum(s - 1, 0), 0)), _full((8, D_MODEL))],
        out_shape=[jax.ShapeDtypeStruct((l, D_FF), BF16), jax.ShapeDtypeStruct((l, D_MODEL), F32),
                   jax.ShapeDtypeStruct((l, D_MODEL), BF16), jax.ShapeDtypeStruct((l, D_FF), F32),
                   jax.ShapeDtypeStruct((8, D_MODEL), F32)],
        scratch_shapes=[pltpu.VMEM((tm, D_MODEL), BF16)],
        compiler_params=_params(("arbitrary",)),
    )(up, up, up, up, x1, conv_w, conv_b, w_down_b, gate, fnw, tgt)


def _convglu_bwd(up, dact, conv_w, conv_b):
    l = up.shape[0]
    tm = 128
    nt = l // tm
    hb = tm // HALO
    te = tm + 2 * HALO

    def body(a_ref, ap_ref, an_ref, g_ref, gp_ref, gn_ref, d_ref, dp_ref, dn_ref, cw_ref, cb_ref,
             dup_ref, acc_ref):
        i = pl.program_id(0)

        @pl.when(i == 0)
        def _():
            acc_ref[...] = jnp.zeros_like(acc_ref)

        def ext(p, c, n):
            return jnp.concatenate([jnp.where(i == 0, 0.0, p[...]), c[...], jnp.where(i == nt - 1, 0.0, n[...])], axis=0)

        ae, ge, de = ext(ap_ref, a_ref, an_ref), ext(gp_ref, g_ref, gn_ref), ext(dp_ref, d_ref, dn_ref)
        gprev = pltpu.roll(ge, 1, 0)
        gnext = pltpu.roll(ge, te - 1, 0)
        w0, w1, w2 = cw_ref[0:1, :], cw_ref[1:2, :], cw_ref[2:3, :]
        gce = cb_ref[...] + gprev * w0 + ge * w1 + gnext * w2
        gel, dgel = _gelu_and_grad(gce)
        dae = de * gel
        dgce = de * ae * dgel
        dge = dgce * w1 + pltpu.roll(dgce, te - 1, 0) * w0 + pltpu.roll(dgce, 1, 0) * w2
        mid = slice(HALO, HALO + tm)
        dup_ref[:, 0:D_FF] = dae[mid].astype(BF16)
        dup_ref[:, D_FF:2 * D_FF] = dge[mid].astype(BF16)
        dgc = dgce[mid]
        acc_ref[0:1, :] += jnp.sum(dgc * gprev[mid], axis=0, keepdims=True)
        acc_ref[1:2, :] += jnp.sum(dgc * ge[mid], axis=0, keepdims=True)
        acc_ref[2:3, :] += jnp.sum(dgc * gnext[mid], axis=0, keepdims=True)
        acc_ref[3:4, :] += jnp.sum(dgc, axis=0, keepdims=True)

    last = l // HALO - 1

    def trio(cb):
        return [pl.BlockSpec((tm, D_FF), lambda i: (i, cb)),
                pl.BlockSpec((HALO, D_FF), lambda i: (jnp.maximum(i * hb - 1, 0), cb)),
                pl.BlockSpec((HALO, D_FF), lambda i: (jnp.minimum((i + 1) * hb, last), cb))]

    return pl.pallas_call(
        body, name="convglu_bwd", grid=(nt,),
        in_specs=trio(0) + trio(1) + trio(0) + [_full((3, D_FF)), _full((1, D_FF))],
        out_specs=[pl.BlockSpec((tm, 2 * D_FF), lambda i: (i, 0)), _full((8, D_FF))],
        out_shape=[jax.ShapeDtypeStruct((l, 2 * D_FF), BF16), jax.ShapeDtypeStruct((8, D_FF), F32)],
        compiler_params=_params(("arbitrary",)),
    )(up, up, up, up, up, up, dact, dact, dact, conv_w, conv_b)


def _up_bwd(dup, w_up_b, w_out_b, x1, dx2, mix, mod3, n2w, y_all, y_ret, p_all, w_glu_b, b_glu, zero_rows, nct, pairs,
            kinds):
    l = x1.shape[0]
    tm = TOK_TILE
    nt = l // tm
    n = len(pairs)
    shapes = _rs_slot_shapes(pairs, kinds)
    n_out = 8

    def body(dup_ref, wu_ref, wo_ref, x1_ref, dx2_ref, mix_ref, mod_ref, nw_ref, y_ref, yr_ref, g_ref, wg_ref, bg_ref,
             zero_rows_ref, *rest):
        dx1_ref, dmixb_ref, acc_ref, dys_ref, dyr_ref, dg_ref, gw_ref, gb_ref = rest[n:n + n_out]
        send_sems, recv_sems, dy_scr = rest[2 * n + n_out:]
        step = pl.program_id(0)

        @pl.when(step == 0)
        def _():
            acc_ref[...] = jnp.zeros_like(acc_ref)
            gw_ref[...] = jnp.zeros_like(gw_ref)
            gb_ref[...] = jnp.zeros_like(gb_ref)

        _behind(step, nt - 1, functools.partial(_rs_chip_copies, rest[:n], rest[n + n_out:2 * n + n_out], kinds,
                                                shapes, send_sems, recv_sems))

        dh2 = _dot_nt(dup_ref[...], wu_ref[...])
        _, vjp = jax.vjp(_rms_mod, x1_ref[...], nw_ref[...], mod_ref[1:2, :], mod_ref[2:3, :])
        dx, dnw, dsh, dsc = vjp(dh2)
        dx1 = dx + dx2_ref[...]
        dx1_ref[...] = dx1
        dmixb = (dx1 * mod_ref[0:1, :]).astype(BF16)
        dmixb_ref[...] = dmixb
        dmix = _dot_nt(dmixb, wo_ref[...])
        acc_ref[0:1, :] += dnw
        acc_ref[1:2, :] += jnp.sum(dx1 * mix_ref[...], axis=0, keepdims=True)
        acc_ref[2:3, :] += dsh
        acc_ref[3:4, :] += dsc

        yg, dgel = _gelu_and_grad(y_ref[...])
        ygb = yg.astype(BF16)
        sg = _sigmoid(_dot(ygb, wg_ref[...]) + bg_ref[...])
        ds = dmix[:, 0:S5_WIDTH]
        dz = ds * yg * sg * (1.0 - sg)
        dzb = dz.astype(BF16)
        _s5_put_rows(dys_ref, dy_scr, (ds * sg + _dot_nt(dzb, wg_ref[...])) * dgel)
        gw_ref[...] += _dot_tn(ygb, dzb)
        gb_ref[...] += jnp.sum(dz, axis=0, keepdims=True)

        for h in range(RET_HEADS):
            sl = slice(RET_DH * h, RET_DH * (h + 1))
            _, hvjp = jax.vjp(_head_norm_gate, yr_ref[:, sl], g_ref[:, sl])
            dyr, dg = hvjp(dmix[:, S5_WIDTH + RET_DH * h:S5_WIDTH + RET_DH * (h + 1)])
            dyr_ref[:, sl] = dyr
            dg_ref[:, sl] = dg

    row = pl.BlockSpec((tm, D_MODEL), lambda i: (i, 0))
    half = pl.BlockSpec((tm, S5_WIDTH), lambda i: (i, 0))
    f32h = jax.ShapeDtypeStruct((l, RET_WIDTH), F32)
    return pl.pallas_call(
        body, name="up_bwd", grid=(nt,),
        in_specs=[pl.BlockSpec((tm, 2 * D_FF), lambda i: (i, 0)), _full((D_MODEL, 2 * D_FF)),
                  _full((D_MODEL, D_MODEL)), row, row, row, _full((3, D_MODEL)), _full((1, D_MODEL)),
                  pl.BlockSpec((tm, S5_WIDTH), lambda i: (i + nct, 0)), half,
                  pl.BlockSpec((tm, RET_WIDTH), lambda i: (i + nct, G_COL // 4)),
                  _full((S5_WIDTH, S5_WIDTH)), _full((1, S5_WIDTH)), ANY] + [ANY] * n,
        out_specs=[row, row, _full((8, D_MODEL)),
                   pl.BlockSpec((S5_NB, tm // S5_T, S5_BW), lambda i: (0, i + nct, 0)), half, half,
                   _full((S5_WIDTH, S5_WIDTH)),
                   _full((1, S5_WIDTH))] + [ANY] * n,
        out_shape=[jax.ShapeDtypeStruct((l, D_MODEL), F32), jax.ShapeDtypeStruct((l, D_MODEL), BF16),
                   jax.ShapeDtypeStruct((8, D_MODEL), F32), jax.ShapeDtypeStruct(zero_rows.shape, BF16), f32h, f32h,
                   jax.ShapeDtypeStruct((S5_WIDTH, S5_WIDTH), F32), jax.ShapeDtypeStruct((1, S5_WIDTH), F32)]
        + [jax.ShapeDtypeStruct((4,) + s, p.dtype) for s, p in zip(shapes, pairs)],
        input_output_aliases={13: 3},
        scratch_shapes=[pltpu.SemaphoreType.DMA((n, 3)), pltpu.SemaphoreType.DMA((n, 3)),
                        pltpu.VMEM((tm // S5_T, S5_T, S5_WIDTH), F32)],
        compiler_params=_params(("arbitrary",)),
    )(dup, w_up_b, w_out_b, x1, dx2, mix, mod3, n2w, y_all, y_ret, p_all, w_glu_b, b_glu, zero_rows, *pairs)


MOD_ROWS = 16
MOD_COLS = 6 * D_MODEL // 4


def _mod_fwd(c_all, c_ctx, w_mod_b, b_loc):
    def body(c_ref, cc_ref, w_ref, b_ref, m_ref, s_ref):
        cond = jnp.concatenate([c_ref[...], jnp.broadcast_to(cc_ref[...], (8, D_MODEL))], axis=0)
        s = _silu(cond).astype(BF16)
        s_ref[...] = s
        m_ref[...] = _dot(s, w_ref[...]) + b_ref[...]

    return pl.pallas_call(
        body, name="mod_fwd",
        out_shape=[jax.ShapeDtypeStruct((MOD_ROWS, MOD_COLS), F32), jax.ShapeDtypeStruct((MOD_ROWS, D_MODEL), BF16)],
        compiler_params=_params(),
    )(c_all, c_ctx, w_mod_b, b_loc)


def _mod_bwd_sum(dm_all):
    def body(d_ref, dm_ref, gb_ref):
        rows = [d_ref[k, 0:1, :] for k in range(8)]
        ctx_sum = d_ref[0, 1:2, :]
        for k in range(1, 8):
            ctx_sum = ctx_sum + d_ref[k, 1:2, :]
        gb = ctx_sum
        for k in range(8):
            gb = gb + rows[k]
        gb_ref[...] = gb
        dm_ref[...] = jnp.concatenate(rows + [ctx_sum] + [jnp.zeros((7, 6 * D_MODEL), F32)], axis=0)

    return pl.pallas_call(
        body, name="mod_bwd_sum",
        out_shape=[jax.ShapeDtypeStruct((MOD_ROWS, 6 * D_MODEL), F32), jax.ShapeDtypeStruct((1, 6 * D_MODEL), F32)],
        compiler_params=_params(),
    )(dm_all)


def _mod_bwd_w(dm_loc, s_b, c_ctx, w_mod_b):
    def body(d_ref, s_ref, cc_ref, w_ref, gw_ref, gc_ref):
        db = d_ref[...].astype(BF16)
        gw_ref[...] = _dot_tn(s_ref[...], db)
        ds = _dot_nt(db, w_ref[...])
        _, vjp = jax.vjp(_silu, cc_ref[...])
        gc_ref[...] = jnp.broadcast_to(vjp(ds[8:9, :])[0], (8, D_MODEL))

    return pl.pallas_call(
        body, name="mod_bwd_w",
        out_shape=[jax.ShapeDtypeStruct((D_MODEL, MOD_COLS), F32), jax.ShapeDtypeStruct((8, D_MODEL), F32)],
        compiler_params=_params(),
    )(dm_loc, s_b, c_ctx, w_mod_b)


def _adamw(w, g, m, v, name, gathered=()):
    r, c = w.shape
    tr = _pick(r, (256, 128, 64, 32, 16, 8))
    bc1 = 1.0 - ADAM_B1 ** ADAM_STEP
    bc2 = 1.0 - ADAM_B2 ** ADAM_STEP
    ng = len(gathered)

    def body(w_ref, g_ref, m_ref, v_ref, *rest):
        v_refs, (d_ref, nm_ref, nv_ref) = rest[:ng], rest[ng:ng + 3]
        all_refs, sems = rest[ng + 3:2 * ng + 3], rest[2 * ng + 3:]
        phases = [_gather8_phases(v_refs[q], all_refs[q], *sems[3 * q:3 * q + 3], gathered[q].shape[0])
                  for q in range(ng)]
        if ng:
            @pl.when(pl.program_id(0) == 0)
            def _():
                for begin, _ in phases:
                    begin()

        gg = g_ref[...]
        nm = ADAM_B1 * m_ref[...] + (1.0 - ADAM_B1) * gg
        nv = ADAM_B2 * v_ref[...] + (1.0 - ADAM_B2) * (gg * gg)
        nm_ref[...] = nm
        nv_ref[...] = nv
        d_ref[...] = -ADAM_LR * ((nm / bc1) / (jnp.sqrt(nv / bc2) + ADAM_EPS) + ADAM_WD * w_ref[...])

        if ng:
            @pl.when(pl.program_id(0) == r // tr - 1)
            def _():
                for _, finish in phases:
                    finish()

    blk = pl.BlockSpec((tr, c), lambda i: (i, 0))
    shp = jax.ShapeDtypeStruct((r, c), F32)
    return pl.pallas_call(
        body, name=name, grid=(r // tr,), in_specs=[blk] * 4 + [ANY] * ng, out_specs=[blk] * 3 + [ANY] * ng,
        out_shape=[shp] * 3 + [jax.ShapeDtypeStruct((8 * t.shape[0], t.shape[1]), t.dtype) for t in gathered],
        scratch_shapes=list(GATHER8_SCRATCH) * ng,
        compiler_params=_params(("arbitrary",) if ng else ("parallel",)),
    )(w, g, m, v, *gathered)


def _sum_slots(a, name):
    n, r, c = a.shape
    tr = _pick(r, (376, 256, 208, 128, 64, 32, 16, 8))

    def body(a_ref, o_ref):
        acc = a_ref[0].astype(F32)
        for k in range(1, n):
            acc = acc + a_ref[k].astype(F32)
        o_ref[...] = acc

    return pl.pallas_call(
        body, name=name, grid=(r // tr,),
        in_specs=[pl.BlockSpec((n, tr, c), lambda i: (0, i, 0))],
        out_specs=pl.BlockSpec((tr, c), lambda i: (i, 0)),
        out_shape=jax.ShapeDtypeStruct((r, c), F32),
        compiler_params=_params(("parallel",)),
    )(a)


def _mesh_pos():
    return lax.axis_index("x"), lax.axis_index("y"), lax.axis_index("c")


def _gather8_phases(x_ref, out_ref, send_sems, recv_sems, local_sem, m_per):
    def parts():
        x, y, c = _mesh_pos()
        me, sibling = (x, y, c), (x, y, 1 - c)
        chips = [(1 - x, y), (x, 1 - y), (1 - x, 1 - y)]

        def rows(px, py, pc):
            return out_ref.at[pl.ds((4 * px + 2 * py + pc) * m_per, m_per), :]

        def copy(k, block, to, src=None):
            return pltpu.make_async_remote_copy(
                src_ref=rows(*block) if src is None else src, dst_ref=rows(*block),
                send_sem=send_sems.at[k], recv_sem=recv_sems.at[k], device_id=to, device_id_type=MESH_ID)

        mine = pltpu.make_async_copy(x_ref, rows(*me), local_sem)
        first = [copy(0, me, sibling, src=x_ref)]
        first += [copy(1 + j, me, (*chip, c), src=x_ref) for j, chip in enumerate(chips)]
        return me, sibling, chips, c, copy, mine, first

    def begin():
        *_, mine, first = parts()
        mine.start()
        for cp in first:
            cp.start()

    def finish():
        me, sibling, chips, c, copy, mine, first = parts()
        passed = [copy(4 + j, (*chip, c), sibling) for j, chip in enumerate(chips)]
        for j, chip in enumerate(chips):
            copy(1 + j, (*chip, c), me).wait_recv()
            passed[j].start()
        copy(0, sibling, me).wait_recv()
        for j, chip in enumerate(chips):
            copy(4 + j, (*chip, 1 - c), me).wait_recv()
        for cp in first + passed:
            cp.wait_send()
        mine.wait()

    return begin, finish


GATHER8_SCRATCH = (pltpu.SemaphoreType.DMA((7,)), pltpu.SemaphoreType.DMA((7,)), pltpu.SemaphoreType.DMA)


def _all_gather8(v, name):
    m_per, n = v.shape

    def body(x_ref, out_ref, send_sems, recv_sems, local_sem):
        begin, finish = _gather8_phases(x_ref, out_ref, send_sems, recv_sems, local_sem, m_per)
        begin()
        finish()

    return pl.pallas_call(
        body, name=name,
        out_shape=jax.ShapeDtypeStruct((8 * m_per, n), v.dtype),
        in_specs=[pl.BlockSpec(memory_space=pltpu.VMEM)],
        out_specs=pl.BlockSpec(memory_space=pltpu.VMEM),
        scratch_shapes=list(GATHER8_SCRATCH),
        compiler_params=_params(),
    )(v)


ANY = pl.BlockSpec(memory_space=pl.ANY)
def PEER_CHIPS(x, y):
    return [(x, 1 - y), (1 - x, y), (1 - x, 1 - y)]


def _shard_region(ref, kind, k, rl, cl, r0, nr, c0, nc):
    if kind == "col":
        return ref.at[pl.ds(r0, nr), pl.ds(k * cl + c0, nc)]
    return ref.at[pl.ds(k * rl + r0, nr), pl.ds(c0, nc)]


def _place_shard(w, kind, chip, name):
    rl, cl = w.shape
    tr = _pick(rl, (256, 128, 64))
    nt = rl // tr

    def body(chip_ref, w_ref, o_ref):
        o_ref[...] = w_ref[...].astype(BF16)

    o_map = (lambda i, chip_ref: (i, chip_ref[0])) if kind == "col" else (lambda i, chip_ref: (chip_ref[0] * nt + i, 0))
    return pl.pallas_call(
        body, name=name,
        grid_spec=pltpu.PrefetchScalarGridSpec(
            num_scalar_prefetch=1, grid=(nt,),
            in_specs=[pl.BlockSpec((tr, cl), lambda i, chip_ref: (i, 0))], out_specs=pl.BlockSpec((tr, cl), o_map)),
        out_shape=jax.ShapeDtypeStruct((rl, 4 * cl) if kind == "col" else (4 * rl, cl), BF16),
        compiler_params=_params(("parallel",)),
    )(chip.reshape(1), w)


def _gather_shard_shapes(placed, kinds):
    return [(p.shape[0], p.shape[1] // 4) if k == "col" else (p.shape[0] // 4, p.shape[1]) for p, k in zip(placed, kinds)]


def _gather_chip_copies(outs, kinds, shard_shapes, send_sems, recv_sems, with_arrivals=True):
    x, y, c = _mesh_pos()
    me = 2 * x + y
    sends, arrivals = [], []
    for a in range(len(outs)):
        rl, cl = shard_shapes[a]
        rh = rl // 2
        reg = functools.partial(_shard_region, outs[a], kinds[a], rl=rl, cl=cl, r0=c * rh, nr=rh, c0=0, nc=cl)
        for j, (px, py) in enumerate(PEER_CHIPS(x, y)):
            to = dict(send_sem=send_sems.at[a, j], recv_sem=recv_sems.at[a, j], device_id=(px, py, c),
                      device_id_type=MESH_ID)
            sends.append(pltpu.make_async_remote_copy(src_ref=reg(k=me), dst_ref=reg(k=me), **to))
            if with_arrivals:
                got = reg(k=2 * px + py)
                arrivals.append(pltpu.make_async_remote_copy(src_ref=got, dst_ref=got, **to))
    return sends, arrivals


def _gather_sibling_copies(outs, kinds, shard_shapes, send_sems, recv_sems):
    x, y, c = _mesh_pos()
    forwards, arrivals = [], []
    for a in range(len(outs)):
        rl, cl = shard_shapes[a]
        rh = rl // 2
        for j, (px, py) in enumerate(PEER_CHIPS(x, y)):
            to = dict(send_sem=send_sems.at[a, j], recv_sem=recv_sems.at[a, j], device_id=(x, y, 1 - c),
                      device_id_type=MESH_ID)
            reg = functools.partial(_shard_region, outs[a], kinds[a], k=2 * px + py, rl=rl, cl=cl, nr=rh, c0=0, nc=cl)
            forwards.append(pltpu.make_async_remote_copy(src_ref=reg(r0=c * rh), dst_ref=reg(r0=c * rh), **to))
            arrivals.append(pltpu.make_async_remote_copy(src_ref=reg(r0=(1 - c) * rh), dst_ref=reg(r0=(1 - c) * rh), **to))
    return forwards, arrivals


def _gather_sibling(placed, kinds, name):
    n = len(placed)
    shard_shapes = _gather_shard_shapes(placed, kinds)

    def body(*refs):
        forwards, from_sibling = _gather_sibling_copies(refs[n:2 * n], kinds, shard_shapes, *refs[2 * n:])
        for cp in forwards:
            cp.start()
        for cp in from_sibling:
            cp.wait_recv()
        for cp in forwards:
            cp.wait_send()

    return pl.pallas_call(
        body, name=name,
        out_shape=[jax.ShapeDtypeStruct(p.shape, p.dtype) for p in placed],
        in_specs=[ANY] * n, out_specs=[ANY] * n, input_output_aliases={a: a for a in range(n)},
        scratch_shapes=[pltpu.SemaphoreType.DMA((n, 3))] * 2,
        compiler_params=_params(),
    )(*placed)


def _half(kind, r, c):
    return (r // 2, c) if kind == "col" else (r, c // 2)


def _half_of(ref, kind, which):
    r, c = ref.shape
    hr, hc = _half(kind, r, c)
    return ref.at[pl.ds(which * hr, hr), :] if kind == "col" else ref.at[:, pl.ds(which * hc, hc)]


def _rs_sibling(grads, kinds, name):
    n = len(grads)

    def body(*refs):
        srcs, dsts = refs[:n], refs[n:2 * n]
        send_sems, recv_sems = refs[2 * n:]
        x, y, c = _mesh_pos()
        cps = [pltpu.make_async_remote_copy(src_ref=_half_of(srcs[a], kinds[a], 1 - c), dst_ref=dsts[a],
                                            send_sem=send_sems.at[a], recv_sem=recv_sems.at[a],
                                            device_id=(x, y, 1 - c), device_id_type=MESH_ID) for a in range(n)]
        for cp in cps:
            cp.start()
        for cp in cps:
            cp.wait()

    return pl.pallas_call(
        body, name=name,
        out_shape=[jax.ShapeDtypeStruct(_half(k, *g.shape), g.dtype) for g, k in zip(grads, kinds)],
        in_specs=[ANY] * n, out_specs=[ANY] * n,
        scratch_shapes=[pltpu.SemaphoreType.DMA((n,)), pltpu.SemaphoreType.DMA((n,))],
        compiler_params=_params(),
    )(*grads)


def _pair_sum(gf, rv, kind, ci, name):
    r, c = rv.shape
    tr = _pick(r, (128, 64, 32, 16, 8))
    nt = r // tr

    def body(ci_ref, g_ref, r_ref, o_ref):
        o_ref[...] = (g_ref[...] + r_ref[...]).astype(BF16)

    g_map = (lambda i, ci_ref: (ci_ref[0] * nt + i, 0)) if kind == "col" else (lambda i, ci_ref: (i, ci_ref[0]))
    blk = pl.BlockSpec((tr, c), lambda i, ci_ref: (i, 0))
    return pl.pallas_call(
        body, name=name,
        grid_spec=pltpu.PrefetchScalarGridSpec(num_scalar_prefetch=1, grid=(nt,),
                                               in_specs=[pl.BlockSpec((tr, c), g_map), blk], out_specs=blk),
        out_shape=jax.ShapeDtypeStruct((r, c), BF16),
        compiler_params=_params(("parallel",)),
    )(ci.reshape(1), gf, rv)


def _rs_slot_shapes(pairs, kinds):
    return [(p.shape[0], p.shape[1] // 4) if k == "col" else (p.shape[0] // 4, p.shape[1]) for p, k in zip(pairs, kinds)]


def _rs_chip_copies(srcs, dsts, kinds, shapes, send_sems, recv_sems, with_arrivals=True):
    x, y, c = _mesh_pos()
    me = 2 * x + y
    sends, arrivals = [], []
    for a in range(len(srcs)):
        rl, cl = shapes[a]
        reg = functools.partial(_shard_region, srcs[a], kinds[a], rl=rl, cl=cl, r0=0, nr=rl, c0=0, nc=cl)
        for j, (px, py) in enumerate(PEER_CHIPS(x, y)):
            to = dict(send_sem=send_sems.at[a, j], recv_sem=recv_sems.at[a, j], device_id=(px, py, c),
                      device_id_type=MESH_ID)
            sends.append(pltpu.make_async_remote_copy(src_ref=reg(k=2 * px + py), dst_ref=dsts[a].at[me], **to))
            if with_arrivals:
                slot = dsts[a].at[2 * px + py]
                arrivals.append(pltpu.make_async_remote_copy(src_ref=slot, dst_ref=slot, **to))
    return sends, arrivals


def _rs_chips(pairs, kinds):
    n = len(pairs)
    shapes = _rs_slot_shapes(pairs, kinds)

    def body(*refs):
        sends, arrivals = _rs_chip_copies(refs[:n], refs[n:2 * n], kinds, shapes, *refs[2 * n:])
        for cp in sends:
            cp.start()
        for cp in arrivals:
            cp.wait_recv()
        for cp in sends:
            cp.wait_send()

    return pl.pallas_call(
        body, name="rs_chips",
        out_shape=[jax.ShapeDtypeStruct((4,) + s, p.dtype) for s, p in zip(shapes, pairs)],
        in_specs=[ANY] * n, out_specs=[ANY] * n,
        scratch_shapes=[pltpu.SemaphoreType.DMA((n, 3)), pltpu.SemaphoreType.DMA((n, 3))],
        compiler_params=_params(),
    )(*pairs)


def _sum_chips(pair, got, kind, pos, name):
    _, r, c = got.shape
    tr = _pick(r, (256, 128, 64, 32, 16))
    nt = r // tr

    def body(pos_ref, own_ref, g1_ref, g2_ref, g3_ref, o_ref):
        o_ref[...] = ((own_ref[...].astype(F32) + g1_ref[0].astype(F32)) + g2_ref[0].astype(F32)) + g3_ref[0].astype(F32)

    if kind == "col":
        own_map = lambda i, p: (i, p[1])
        out_map = lambda i, p: (p[0] * nt + i, 0)
        out_shape = (2 * r, c)
    else:
        own_map = lambda i, p: (p[1] * nt + i, 0)
        out_map = lambda i, p: (i, p[0])
        out_shape = (r, 2 * c)
    peer = lambda m: pl.BlockSpec((1, tr, c), lambda i, p: (p[1] ^ m, i, 0))
    return pl.pallas_call(
        body, name=name,
        grid_spec=pltpu.PrefetchScalarGridSpec(
            num_scalar_prefetch=1, grid=(nt,),
            in_specs=[pl.BlockSpec((tr, c), own_map), peer(1), peer(2), peer(3)],
            out_specs=pl.BlockSpec((tr, c), out_map)),
        out_shape=jax.ShapeDtypeStruct(out_shape, F32),
        compiler_params=_params(("parallel",)),
    )(pos, pair, got, got, got)


def _rs_back(halves, kinds):
    n = len(halves)

    def body(*refs):
        outs = refs[n:2 * n]
        send_sems, recv_sems = refs[2 * n:]
        x, y, c = _mesh_pos()
        cps = []
        for a in range(n):
            mine = _half_of(outs[a], kinds[a], c)
            cps.append(pltpu.make_async_remote_copy(src_ref=mine, dst_ref=mine, send_sem=send_sems.at[a],
                                                    recv_sem=recv_sems.at[a], device_id=(x, y, 1 - c),
                                                    device_id_type=MESH_ID))
            cps[-1].start()
        for a in range(n):
            other = _half_of(outs[a], kinds[a], 1 - c)
            pltpu.make_async_remote_copy(src_ref=other, dst_ref=other, send_sem=send_sems.at[a],
                                         recv_sem=recv_sems.at[a], device_id=(x, y, 1 - c),
                                         device_id_type=MESH_ID).wait_recv()
        for cp in cps:
            cp.wait_send()

    return pl.pallas_call(
        body, name="rs_back",
        out_shape=[jax.ShapeDtypeStruct(h.shape, h.dtype) for h in halves],
        in_specs=[ANY] * n, out_specs=[ANY] * n, input_output_aliases={a: a for a in range(n)},
        scratch_shapes=[pltpu.SemaphoreType.DMA((n,)), pltpu.SemaphoreType.DMA((n,))],
        compiler_params=_params(),
    )(*halves)


def _rope_tables(l, lc):
    rows = l // GRID_W
    n_freq = RET_DH // 4
    inv_freq = ROPE_THETA ** (-jnp.arange(n_freq, dtype=F32) / n_freq)
    sign = jnp.tile(jnp.array([-1.0, 1.0], F32), n_freq)

    def half(n):
        ang = jnp.repeat(jnp.arange(n, dtype=F32)[:, None] * inv_freq, 2, axis=-1)
        return jnp.cos(ang), jnp.sin(ang) * sign

    (cr, sr), (cc, sc) = half(rows), half(GRID_W)
    grid = lambda r, c: jnp.concatenate([jnp.repeat(r, GRID_W, axis=0), jnp.tile(c, (rows, 1))], axis=-1)
    cos_t = jnp.concatenate([jnp.ones((lc, RET_DH), F32), grid(cr, cc)], axis=0)
    sin_t = jnp.concatenate([jnp.zeros((lc, RET_DH), F32), grid(sr, sc)], axis=0)
    return cos_t, sin_t


def _s5_pack(a):
    blk = lambda t: t.reshape(1, S5_NB, 128, S5_STATE)
    lre = jnp.stack([a["s5_lambda_re_f"][0], a["s5_lambda_re_b"][0]]).reshape(2, S5_NB, 8, S5_STATE)
    lim = jnp.stack([a["s5_lambda_im_f"][0], a["s5_lambda_im_b"][0]]).reshape(2, S5_NB, 8, S5_STATE)
    lst = jnp.stack([a["s5_log_step_f"][0], a["s5_log_step_b"][0]]).reshape(2, S5_NB, 8, 1)
    b_re = blk(a["s5_b_re"][0].transpose(0, 2, 1))
    b_im = blk(a["s5_b_im"][0].transpose(0, 2, 1))
    return (lre, lim, lst, b_re, b_im, blk(a["s5_c_re"][0]), blk(a["s5_c_im"][0]),
            a["s5_d"].reshape(1, S5_NB, 1, 128))


def _s5_unpack(g):
    glre, glim, glst, gbre, gbim, gcre, gcim, gd = g
    unb = lambda t: t.reshape(S5_GROUPS, S5_GROUP, S5_STATE).transpose(0, 2, 1)[None]
    return {
        "s5_lambda_re_f": glre[0].reshape(1, S5_GROUPS, S5_STATE), "s5_lambda_re_b": glre[1].reshape(1, S5_GROUPS, S5_STATE),
        "s5_lambda_im_f": glim[0].reshape(1, S5_GROUPS, S5_STATE), "s5_lambda_im_b": glim[1].reshape(1, S5_GROUPS, S5_STATE),
        "s5_log_step_f": glst[0].reshape(1, S5_GROUPS), "s5_log_step_b": glst[1].reshape(1, S5_GROUPS),
        "s5_b_re": unb(gbre), "s5_b_im": unb(gbim),
        "s5_c_re": gcre.reshape(1, S5_GROUPS, S5_GROUP, S5_STATE), "s5_c_im": gcim.reshape(1, S5_GROUPS, S5_GROUP, S5_STATE),
        "s5_d": gd.reshape(1, S5_WIDTH),
    }


def _local_step(a, early, late, mx, mc, conv_w, ci):
    x, ctx, tgt = a["x"][0], a["ctx"][0], a["loss_target"][0]
    l, lc = x.shape[0], ctx.shape[0]
    la = l + lc
    nct, ncc, nrc, cn = lc // TOK_TILE, lc // S5_T, lc // RET_CHUNK, la // S5_T
    n1w, n2w, fnw = a["norm1_w"], a["norm2_w"], a["final_norm_w"].reshape(1, D_MODEL)
    conv_b, b_glu = a["conv_b"], a["s5_b_glu"]
    ld2 = jnp.concatenate([a["ret_log_decay_f"], a["ret_log_decay_b"]], axis=0)
    mod4 = jnp.concatenate([mc[0:2], mx[0:2]], axis=0)
    mod3 = mx[2:5]
    gate5 = mx[5:6]
    cos_t, sin_t = _rope_tables(l, lc)
    s5p = _s5_pack(a)

    gg, xw, yw, a16, *early = _s5_gen(*s5p, early, EARLY_KINDS)
    wb = dict(zip(EARLY_NAMES, _gather_sibling(early, EARLY_KINDS, "gather_sibling_early")))
    p_all, h1b, p3, w_up_p = _norm_inproj(x, ctx, n1w, mod4, wb["w_in"], cos_t, sin_t, [late[1]], (LATE_KINDS[1],))
    sloc = _s5_to_states(p3, xw, "s5_state")
    a16s = a16.transpose(1, 0, 2, 3).reshape(2, S5_GROUPS, 128)
    hs, w_out_p, w_down_p = _s5_scan(sloc, a16s, ncc, [late[0], late[2]], (LATE_KINDS[0], LATE_KINDS[2]))
    y_all = _s5_from_states(p3, gg, hs, yw, False, "s5_out").reshape(la, S5_WIDTH)
    tabs = _ret_tables(ld2)
    of, ob, ssf, ssb = _ret_scan(p_all, tabs, nrc)
    wb = {**wb, **dict(zip(LATE_NAMES, _gather_sibling([w_out_p, w_up_p, w_down_p], LATE_KINDS, "gather_sibling_late")))}
    x1, mix, h2b, up, mixb, y_ret = _outproj_up(x, y_all, of, ob, p_all, wb["s5_w_glu"], b_glu, wb["w_out"],
                                                     mod3, n2w, wb["w_up"], nct)
    act, dx2, ddn, dact, acc_f = _ffn_loss(up, x1, conv_w, conv_b, wb["w_down"], gate5, fnw, tgt)

    g = {}
    g["w_down"] = _mm_tn(act, ddn, name="gw_down")
    dup, acc_c = _convglu_bwd(up, dact, conv_w, conv_b)
    g["w_up"] = _mm_tn(h2b, dup, name="gw_up")
    first = [g[n] for n in FIRST_GRADS]
    first_pairs = [_pair_sum(gf, rv, k, ci, "rs_pair_" + n)
                   for gf, rv, k, n in zip(first, _rs_sibling(first, FIRST_KINDS, "rs_sibling_first"), FIRST_KINDS, FIRST_GRADS)]
    dx1, dmixb, acc_2, dy3, dy_ret, dg, g["s5_w_glu"], g["s5_b_glu"], *first_got = _up_bwd(
        dup, wb["w_up"], wb["w_out"], x1, dx2, mix, mod3, n2w, y_all, y_ret, p_all, wb["s5_w_glu"], b_glu,
        jnp.zeros(p3.shape, BF16), nct, first_pairs, FIRST_KINDS)
    g["w_out"] = _mm_tn(mixb, dmixb, name="gw_out")

    e = _s5_to_states(dy3, yw, "s5_bwd_h")
    ds, da16 = _s5_scan_bwd(e, hs, a16s, ncc)
    du = _s5_from_states(dy3, gg, ds, xw, True, "s5_bwd_u").reshape(la, S5_WIDTH)
    dkb = _s5_bwd_kb(p3, dy3)
    dwst = _s5_bwd_w(p3, ds, "s5_bwd_wst")
    dwout = _s5_bwd_w(dy3, hs, "s5_bwd_wout")
    da16p = da16.reshape(2, S5_NB, 8, 128).transpose(1, 0, 2, 3)
    g.update(_s5_unpack(_s5_gen_bwd(*s5p, dkb, dwst, dwout, da16p)))

    dqf, dkf, dvf, dqb, dkb_, dvb, dld = _ret_scan_bwd(p_all, tabs, ssf, ssb, dy_ret, nrc)
    g["ret_log_decay_f"] = dld[0, :, 0, 0].reshape(1, RET_HEADS)
    g["ret_log_decay_b"] = dld[1, :, 0, 0].reshape(1, RET_HEADS)
    dp, grad_x, acc_1 = _in_bwd(dqf, dkf, dvf, dqb, dkb_, dvb, du, dg, cos_t, sin_t, wb["w_in"], x, ctx, n1w, mod4, dx1)
    g["norm1_w"], g["norm2_w"], g["final_norm_w"] = acc_1[0:1], acc_2[0:1], acc_f[0]
    g["conv_w"], g["conv_b"] = acc_c[0:3], acc_c[3:4]
    zero = jnp.zeros((1, D_MODEL), F32)
    dmx = jnp.concatenate([acc_1[3:5], acc_2[1:2], acc_2[2:4], acc_f[1:2]], axis=0)
    dmc = jnp.concatenate([acc_1[1:3], zero, zero, zero, zero], axis=0)
    dm_pair = jnp.concatenate([dmx.reshape(1, -1), dmc.reshape(1, -1), jnp.zeros((6, 6 * D_MODEL), F32)], axis=0)
    small = _pack_rows([g[n] for n in SMALL_NAMES] + [acc_f[2, 0:1]])
    g["w_in"], dm_all, small_all = _mm_tn(h1b, dp, name="gw_in", gathered=[dm_pair, small])
    return grad_x, g, dm_all, small_all, first_pairs, first_got


WEIGHT_NAMES = ("c_ctx", "w_mod", "b_mod", "norm1_w", "w_in", "s5_lambda_re_f", "s5_lambda_im_f", "s5_log_step_f",
                "s5_lambda_re_b", "s5_lambda_im_b", "s5_log_step_b", "s5_b_re", "s5_b_im", "s5_c_re", "s5_c_im",
                "s5_d", "s5_w_glu", "s5_b_glu", "ret_log_decay_f", "ret_log_decay_b", "w_out", "norm2_w", "w_up",
                "conv_w", "conv_b", "w_down", "final_norm_w")
BIG_NAMES = ("w_in", "w_out", "w_up", "w_down", "s5_w_glu")
BIG_KINDS = ("col", "row", "col", "row", "row")
EARLY_NAMES, EARLY_KINDS = ("w_in", "s5_w_glu"), ("col", "row")
LATE_NAMES, LATE_KINDS = ("w_out", "w_up", "w_down"), ("row", "col", "row")
FIRST_GRADS, FIRST_KINDS = ("w_down", "w_up"), ("row", "col")
LAST_GRADS, LAST_KINDS = ("w_in", "w_out", "s5_w_glu"), ("col", "row", "row")
SMALL_NAMES = ("norm1_w", "norm2_w", "final_norm_w", "conv_b", "conv_w", "s5_lambda_re_f", "s5_lambda_im_f",
               "s5_log_step_f", "s5_lambda_re_b", "s5_lambda_im_b", "s5_log_step_b", "s5_b_re", "s5_b_im", "s5_c_re",
               "s5_c_im", "s5_d", "s5_b_glu", "ret_log_decay_f", "ret_log_decay_b")
ROW = 1024
N_CHIPS = 4


def _pack_rows(parts):
    flat = jnp.concatenate([p.reshape(-1) for p in parts])
    n = flat.shape[0]
    rows = -(-n // (8 * ROW)) * 8
    return jnp.pad(flat, (0, rows * ROW - n)).reshape(rows, ROW)


def _unpack_rows(packed, shapes):
    flat = packed.reshape(-1)
    out, off = [], 0
    for s in shapes:
        n = math.prod(s)
        out.append(flat[off:off + n].reshape(s))
        off += n
    return out


def _step(a):
    xi, yi, ci = _mesh_pos()
    chip = 2 * xi + yi
    dev = 2 * chip + ci

    cw_loc = a["conv_w"].reshape(-1)
    small_in = jnp.concatenate([a["c"].reshape(-1), jnp.pad(cw_loc, (0, 24 * 128 - cw_loc.shape[0]))]).reshape(32, 128)
    sg = _all_gather8(small_in, "gather_cond").reshape(8, 32, 128)
    c_all = sg[:, 0:8].reshape(8, D_MODEL)
    conv_w = sg[0::2, 8:32].reshape(N_CHIPS, -1)[:, :cw_loc.shape[0]].reshape(N_CHIPS, 3, -1)
    conv_w = conv_w.transpose(1, 0, 2).reshape(3, D_FF)

    placed = {n: _place_shard(a[n][0], k, chip, "place_" + n) for n, k in zip(BIG_NAMES, BIG_KINDS)}
    early = [placed[n] for n in EARLY_NAMES]
    late = [placed[n] for n in LATE_NAMES]

    w_mod_b = a["w_mod"][0].astype(BF16)
    c_ctx = a["c_ctx"].reshape(1, D_MODEL)
    b_loc = lax.dynamic_slice_in_dim(a["b_mod"], chip * MOD_COLS, MOD_COLS, 1)
    m_loc, s_b = _mod_fwd(c_all, c_ctx, w_mod_b, b_loc)
    mg = _all_gather8(m_loc, "gather_mod").reshape(8, MOD_ROWS, MOD_COLS)
    m_full = mg[0::2].transpose(1, 0, 2).reshape(MOD_ROWS, 6 * D_MODEL)
    mx = lax.dynamic_slice_in_dim(m_full, dev, 1, 0).reshape(6, D_MODEL)
    mc = m_full[8].reshape(6, D_MODEL)

    grad_x, g, dm_all, small_all, first_pairs, first_got = _local_step(a, early, late, mx, mc, conv_w, ci)

    dm16, gb_mod = _mod_bwd_sum(dm_all.reshape(8, 8, 6 * D_MODEL))
    dm_loc = lax.dynamic_slice_in_dim(dm16, chip * MOD_COLS, MOD_COLS, 1)
    gw_mod, gcc = _mod_bwd_w(dm_loc, s_b, c_ctx, w_mod_b)
    *mod_new, gcc_all = _adamw(a["w_mod"][0], gw_mod, a["m_w_mod"][0], a["v_w_mod"][0], "adamw_w_mod", gathered=[gcc])

    tot = _sum_slots(small_all.reshape(8, -1, ROW), "sum_small_grads")
    small = dict(zip(SMALL_NAMES + ("loss",), _unpack_rows(tot, [g[n].shape for n in SMALL_NAMES] + [(1,)])))
    loss = small["loss"].reshape(())
    grads = {n: small[n].reshape(a[n].shape) for n in SMALL_NAMES if n != "conv_w"}
    gcc_tot = _sum_slots(gcc_all.reshape(8, 8, D_MODEL), "sum_c_ctx")
    grads["c_ctx"] = (0.5 * gcc_tot[0]).reshape(a["c_ctx"].shape)
    grads["conv_w"] = lax.dynamic_slice_in_dim(small["conv_w"], chip * (D_FF // N_CHIPS), D_FF // N_CHIPS, 1)[None]
    grads["b_mod"] = gb_mod
    grads["w_mod"] = gw_mod[None]

    last = [g[n] for n in LAST_GRADS]
    last_pairs = [_pair_sum(gf, rv, k, ci, "rs_pair_" + n)
                  for gf, rv, k, n in zip(last, _rs_sibling(last, LAST_KINDS, "rs_sibling_last"), LAST_KINDS, LAST_GRADS)]
    last_got = _rs_chips(last_pairs, LAST_KINDS)
    pos = jnp.stack([ci, chip])
    order = FIRST_GRADS + LAST_GRADS
    order_kinds = FIRST_KINDS + LAST_KINDS
    halves = [_sum_chips(p, t, k, pos, "rs_sum_" + n)
              for p, t, k, n in zip(first_pairs + last_pairs, list(first_got) + list(last_got), order_kinds, order)]
    for n, t in zip(order, _rs_back(halves, order_kinds)):
        grads[n] = t[None]

    delta, new_m, new_v = {}, {}, {}
    for n in BIG_NAMES:
        for dst, t in zip((delta, new_m, new_v), _adamw(a[n][0], grads[n][0], a["m_" + n][0], a["v_" + n][0], "adamw_" + n)):
            dst[n] = t[None]
    for dst, t in zip((delta, new_m, new_v), mod_new):
        dst["w_mod"] = t[None]
    rest = [n for n in WEIGHT_NAMES if n not in BIG_NAMES and n != "w_mod"]
    shapes = [a[n].shape for n in rest]
    pr = lambda pre: _pack_rows([a[pre + n] for n in rest])
    for dst, t in zip((delta, new_m, new_v),
                      _adamw(pr(""), _pack_rows([grads[n] for n in rest]), pr("m_"), pr("v_"), "adamw_small")):
        dst.update(zip(rest, _unpack_rows(t, shapes)))

    return (loss, grad_x[None], *[grads[n] for n in WEIGHT_NAMES], *[delta[n] for n in WEIGHT_NAMES],
            *[new_m[n] for n in WEIGHT_NAMES], *[new_v[n] for n in WEIGHT_NAMES])


def kernel(x, c, ctx, c_ctx, w_mod, b_mod, norm1_w, w_in, s5_lambda_re_f, s5_lambda_im_f, s5_log_step_f, s5_lambda_re_b, s5_lambda_im_b, s5_log_step_b, s5_b_re, s5_b_im, s5_c_re, s5_c_im, s5_d, s5_w_glu, s5_b_glu, ret_log_decay_f, ret_log_decay_b, w_out, norm2_w, w_up, conv_w, conv_b, w_down, final_norm_w, loss_target, m_c_ctx, m_w_mod, m_b_mod, m_norm1_w, m_w_in, m_s5_lambda_re_f, m_s5_lambda_im_f, m_s5_log_step_f, m_s5_lambda_re_b, m_s5_lambda_im_b, m_s5_log_step_b, m_s5_b_re, m_s5_b_im, m_s5_c_re, m_s5_c_im, m_s5_d, m_s5_w_glu, m_s5_b_glu, m_ret_log_decay_f, m_ret_log_decay_b, m_w_out, m_norm2_w, m_w_up, m_conv_w, m_conv_b, m_w_down, m_final_norm_w, v_c_ctx, v_w_mod, v_b_mod, v_norm1_w, v_w_in, v_s5_lambda_re_f, v_s5_lambda_im_f, v_s5_log_step_f, v_s5_lambda_re_b, v_s5_lambda_im_b, v_s5_log_step_b, v_s5_b_re, v_s5_b_im, v_s5_c_re, v_s5_c_im, v_s5_d, v_s5_w_glu, v_s5_b_glu, v_ret_log_decay_f, v_ret_log_decay_b, v_w_out, v_norm2_w, v_w_up, v_conv_w, v_conv_b, v_w_down, v_final_norm_w):
    return _step(dict(locals()))
```

```python
import functools
import math

import jax
import jax.numpy as jnp
from jax import lax
from jax.experimental import pallas as pl
from jax.experimental.pallas import tpu as pltpu

F32 = jnp.float32
BF16 = jnp.bfloat16

D_MODEL = 1024
S5_WIDTH = 512
S5_GROUPS = 32
S5_GROUP = 16
S5_STATE = 64
RET_WIDTH = 512
RET_HEADS = 4
RET_DH = 128
RET_CHUNK = 256
GRID_W = 64
ROPE_THETA = 10000.0
D_FF = 2816
NORM_EPS = 1e-6
IN_COLS = S5_WIDTH + 4 * RET_WIDTH

S5_T = 16
S5_NB = 4
S5_BW = S5_T * 128
S5_SW = 8 * 2 * S5_STATE

ADAM_LR, ADAM_B1, ADAM_B2, ADAM_EPS, ADAM_WD, ADAM_STEP = 0.001, 0.9, 0.999, 1e-08, 0.01, 10

VMEM_LIMIT = 56 * 1024 * 1024
MM_TN_VMEM = 40 * 1024 * 1024
MESH_ID = pl.DeviceIdType.MESH


def _params(sem=None):
    return pltpu.CompilerParams(dimension_semantics=sem, vmem_limit_bytes=VMEM_LIMIT)


def _full(shape):
    n = len(shape)
    return pl.BlockSpec(shape, lambda *_: (0,) * n)


def _dot(a, b):
    return jnp.dot(a, b, preferred_element_type=F32)


def _dot_nt(a, b):
    return lax.dot_general(a, b, (((1,), (1,)), ((), ())), preferred_element_type=F32)


def _dot_tn(a, b):
    return lax.dot_general(a, b, (((0,), (0,)), ((), ())), preferred_element_type=F32)


def _dot_hi(a, b):
    return jnp.dot(a, b, preferred_element_type=F32, precision=lax.Precision.HIGHEST)


def _dot_nt_hi(a, b):
    return lax.dot_general(a, b, (((1,), (1,)), ((), ())), preferred_element_type=F32,
                           precision=lax.Precision.HIGHEST)


def _gelu(x):
    return 0.5 * x * (1.0 + jnp.tanh(0.7978845608028654 * (x + 0.044715 * (x * x * x))))


def _gelu_and_grad(x):
    c, ca = 0.7978845608028654, 0.7978845608028654 * 0.044715
    x2 = x * x
    t = jnp.tanh(x * (c + ca * x2))
    h = 0.5 * x
    return h + h * t, 0.5 + 0.5 * t + h * (1.0 - t * t) * (c + 3.0 * ca * x2)


def _sigmoid(x):
    return 1.0 / (1.0 + jnp.exp(-x))


def _silu(x):
    return x * _sigmoid(x)


def _rms_mod(x, nw, sh, sc):
    r = lax.rsqrt(jnp.mean(x * x, axis=-1, keepdims=True) + NORM_EPS)
    return (x * r * nw) * (1.0 + sc) + sh


def _rms(x, nw):
    r = lax.rsqrt(jnp.mean(x * x, axis=-1, keepdims=True) + NORM_EPS)
    return x * r * nw


def _head_norm_gate(y, g):
    mu = jnp.mean(y, axis=-1, keepdims=True)
    yc = y - mu
    var = jnp.mean(yc * yc, axis=-1, keepdims=True)
    return _silu(g) * (yc * lax.rsqrt(var + NORM_EPS))


def _swap_pairs(t):
    lane = lax.broadcasted_iota(jnp.int32, t.shape, 1)
    return jnp.where(lane % 2 == 0, pltpu.roll(t, RET_DH - 1, 1), pltpu.roll(t, 1, 1))


def _rope(t, cos_t, sin_t):
    return t * cos_t + _swap_pairs(t) * sin_t


def _rope_t(dt, cos_t, sin_t):
    return dt * cos_t + _swap_pairs(dt * sin_t)


def _pick(n, prefs):
    for p in prefs:
        if n % p == 0:
            return p
    return n


def _mm_tn(a, b, *, name, gathered=None):
    m, k = a.shape
    n = b.shape[1]
    tn = _pick(n, (1408, 1024, 1280, 512))
    fits = lambda t: 2 * (2 * t * k + 2 * t * tn + 4 * k * tn) <= MM_TN_VMEM
    tm = _pick(m, [t for t in (2816, 2048, 1024, 768, 512, 256) if fits(t)] + [128])
    nj, ni = n // tn, m // tm

    def product(a_ref, b_ref, o_ref):
        @pl.when(pl.program_id(1) == 0)
        def _():
            o_ref[...] = jnp.zeros_like(o_ref)
        o_ref[...] += _dot_tn(a_ref[...], b_ref[...])

    specs = dict(
        grid=(nj, ni),
        in_specs=[pl.BlockSpec((tm, k), lambda j, i: (i, 0)), pl.BlockSpec((tm, tn), lambda j, i: (i, j))],
        out_specs=pl.BlockSpec((k, tn), lambda j, i: (0, j)),
        out_shape=jax.ShapeDtypeStruct((k, n), F32))
    if gathered is None:
        def body(a_ref, b_ref, o_ref):
            product(a_ref, b_ref, o_ref)

        return pl.pallas_call(body, name=name, compiler_params=_params(("parallel", "arbitrary")), **specs)(a, b)

    ng = len(gathered)

    def body_gather(a_ref, b_ref, *rest):
        v_refs, o_ref, all_refs, sems = rest[:ng], rest[ng], rest[ng + 1:2 * ng + 1], rest[2 * ng + 1:]
        phases = [_gather8_phases(v_refs[q], all_refs[q], *sems[3 * q:3 * q + 3], gathered[q].shape[0])
                  for q in range(ng)]
        step = pl.program_id(0) * ni + pl.program_id(1)

        @pl.when(step == 0)
        def _():
            for begin, _ in phases:
                begin()

        product(a_ref, b_ref, o_ref)

        @pl.when(step == nj * ni - 1)
        def _():
            for _, finish in phases:
                finish()

    specs["in_specs"] = specs["in_specs"] + [ANY] * ng
    specs["out_specs"] = [specs["out_specs"]] + [ANY] * ng
    specs["out_shape"] = [specs["out_shape"]] + [jax.ShapeDtypeStruct((8 * v.shape[0], v.shape[1]), v.dtype)
                                                 for v in gathered]
    return pl.pallas_call(body_gather, name=name, scratch_shapes=list(GATHER8_SCRATCH) * ng,
                          compiler_params=_params(("arbitrary", "arbitrary")), **specs)(a, b, *gathered)


def _mm_tn_swap(a, b, extra, extra_kinds, *, name):
    m, k = a.shape
    n = b.shape[1]
    tn = _pick(n, (1408, 1024, 1280, 512))
    fits = lambda t: 2 * (2 * t * k + 2 * t * tn + 4 * k * tn) <= MM_TN_VMEM
    tm = _pick(m, [t for t in (2816, 2048, 1024, 768, 512, 256) if fits(t)] + [128])
    nj, ni, ne, hr = n // tn, m // tm, len(extra), k // 2

    def body(a_ref, b_ref, *rest):
        srcs, o_ref, got_ref, dsts = rest[:ne], rest[ne], rest[ne + 1], rest[ne + 2:2 * ne + 2]
        send_sems, recv_sems, xsend_sems, xrecv_sems, stage = rest[2 * ne + 2:]
        x, y, c = _mesh_pos()
        sibling = (x, y, 1 - c)

        def push(j):
            return pltpu.make_async_remote_copy(
                src_ref=stage, dst_ref=got_ref.at[:, pl.ds(j * tn, tn)],
                send_sem=send_sems.at[j], recv_sem=recv_sems.at[j], device_id=sibling, device_id_type=MESH_ID)

        def swaps():
            return [pltpu.make_async_remote_copy(
                src_ref=_half_of(srcs[q], extra_kinds[q], 1 - c), dst_ref=dsts[q], send_sem=xsend_sems.at[q],
                recv_sem=xrecv_sems.at[q], device_id=sibling, device_id_type=MESH_ID) for q in range(ne)]

        pj, pi = pl.program_id(0), pl.program_id(1)

        @pl.when((pj == 0) & (pi == 0))
        def _():
            for cp in swaps():
                cp.start()

        @pl.when(pi == 0)
        def _():
            o_ref[...] = jnp.zeros_like(o_ref)
        o_ref[...] += _dot_tn(a_ref[...], b_ref[...])

        for j in range(nj):
            @pl.when((pj == j) & (pi == ni - 1))
            def _():
                if j > 0:
                    push(j - 1).wait_send()
                stage[...] = o_ref[pl.ds((1 - c) * hr, hr), :]
                push(j).start()
                if j == nj - 1:
                    push(j).wait_send()
                    for jj in range(nj):
                        push(jj).wait_recv()
                    for cp in swaps():
                        cp.wait()

    f32 = lambda shape: jax.ShapeDtypeStruct(shape, F32)
    return pl.pallas_call(
        body, name=name, grid=(nj, ni),
        in_specs=[pl.BlockSpec((tm, k), lambda j, i: (i, 0)), pl.BlockSpec((tm, tn), lambda j, i: (i, j))] + [ANY] * ne,
        out_specs=[pl.BlockSpec((k, tn), lambda j, i: (0, j)), ANY] + [ANY] * ne,
        out_shape=[f32((k, n)), f32((hr, n))] + [f32(_half(kd, *t.shape)) for t, kd in zip(extra, extra_kinds)],
        scratch_shapes=[pltpu.SemaphoreType.DMA((nj,)), pltpu.SemaphoreType.DMA((nj,)),
                        pltpu.SemaphoreType.DMA((ne,)), pltpu.SemaphoreType.DMA((ne,)), pltpu.VMEM((hr, tn), F32)],
        compiler_params=_params(("arbitrary", "arbitrary")),
    )(a, b, *extra)


TOK_TILE = 256


def _behind(step, last, copies):
    @pl.when(step == 0)
    def _():
        for cp in copies(with_arrivals=False)[0]:
            cp.start()

    @pl.when(step == last)
    def _():
        sends, arrivals = copies()
        for cp in arrivals:
            cp.wait_recv()
        for cp in sends:
            cp.wait_send()


def _s5_put_rows(rows_ref, scr, val):
    nchunk = scr.shape[0]
    for c in range(nchunk):
        scr[c] = val[S5_T * c:S5_T * (c + 1), :]
    for b in range(S5_NB):
        for j in range(S5_T):
            rows_ref[b, :, 128 * j:128 * (j + 1)] = scr[:, j, 128 * b:128 * (b + 1)].astype(BF16)


def _norm_inproj(x, ctx, n1w, mod4, w_in_b, cos_t, sin_t, placed, kinds):
    l, lc = x.shape[0], ctx.shape[0]
    tm = TOK_TILE
    nct = lc // tm
    la = l + lc
    n = len(placed)
    shard_shapes = _gather_shard_shapes(placed, kinds)

    def body(x_ref, c_ref, nw_ref, mod_ref, w_ref, cos_ref, sin_ref, *rest):
        p_ref, h_ref, u_ref = rest[n:n + 3]
        send_sems, recv_sems, u_scr = rest[2 * n + 3:]
        _behind(pl.program_id(0), la // tm - 1,
                functools.partial(_gather_chip_copies, rest[n + 3:2 * n + 3], kinds, shard_shapes, send_sems, recv_sems))
        is_ctx = pl.program_id(0) < nct
        xt = jnp.where(is_ctx, c_ref[...], x_ref[...])
        sh = jnp.where(is_ctx, mod_ref[0:1, :], mod_ref[2:3, :])
        sc = jnp.where(is_ctx, mod_ref[1:2, :], mod_ref[3:4, :])
        hb = _rms_mod(xt, nw_ref[...], sh, sc).astype(BF16)
        h_ref[...] = hb
        p = _dot(hb, w_ref[...])
        p_ref[...] = p
        cs, sn = cos_ref[...], sin_ref[...]
        for h in range(RET_HEADS):
            q_cols = slice(RET_WIDTH + RET_DH * h, RET_WIDTH + RET_DH * (h + 1))
            k_cols = slice(2 * RET_WIDTH + RET_DH * h, 2 * RET_WIDTH + RET_DH * (h + 1))
            p_ref[:, q_cols] = _rope(p[:, q_cols], cs, sn)
            p_ref[:, k_cols] = _rope(p[:, k_cols] * K_SCALE, cs, sn)
        _s5_put_rows(u_ref, u_scr, p[:, 0:S5_WIDTH])

    return pl.pallas_call(
        body, name="norm_inproj", grid=(la // tm,),
        in_specs=[pl.BlockSpec((tm, D_MODEL), lambda i: (jnp.maximum(i - nct, 0), 0)),
                  pl.BlockSpec((tm, D_MODEL), lambda i: (jnp.minimum(i, nct - 1), 0)),
                  _full((1, D_MODEL)), _full((4, D_MODEL)), _full((D_MODEL, IN_COLS)),
                  pl.BlockSpec((tm, RET_DH), lambda i: (i, 0)), pl.BlockSpec((tm, RET_DH), lambda i: (i, 0))] + [ANY] * n,
        out_specs=[pl.BlockSpec((tm, IN_COLS), lambda i: (i, 0)), pl.BlockSpec((tm, D_MODEL), lambda i: (i, 0)),
                   pl.BlockSpec((S5_NB, tm // S5_T, S5_BW), lambda i: (0, i, 0))] + [ANY] * n,
        out_shape=[jax.ShapeDtypeStruct((la, IN_COLS), F32), jax.ShapeDtypeStruct((la, D_MODEL), BF16),
                   jax.ShapeDtypeStruct((S5_NB, la // S5_T, S5_BW), BF16)]
        + [jax.ShapeDtypeStruct(p.shape, p.dtype) for p in placed],
        input_output_aliases={7 + a: 3 + a for a in range(n)},
        scratch_shapes=[pltpu.SemaphoreType.DMA((n, 3)), pltpu.SemaphoreType.DMA((n, 3)),
                        pltpu.VMEM((tm // S5_T, S5_T, S5_WIDTH), F32)],
        compiler_params=_params(("arbitrary",)),
    )(x, ctx, n1w, mod4, w_in_b, cos_t, sin_t, *placed)


def _iota2(shape, dim):
    return lax.broadcasted_iota(jnp.int32, shape, dim)


def _group_mask(rows, cols, row_div, col_div):
    return jnp.where(_iota2((rows, cols), 0) // row_div == _iota2((rows, cols), 1) // col_div, 1.0, 0.0).astype(F32)


def _s5_gen_dir(lre, lim, lst, b_re, b_im, c_re, c_im):
    step = jnp.exp(lst)
    mag = jnp.exp(lre * step)
    ar = mag * jnp.cos(lim * step)
    ai = mag * jnp.sin(lim * step)
    den = lre * lre + lim * lim
    xr = ar - 1.0
    cr = (xr * lre + ai * lim) / den
    ci = (ai * lre - xr * lim) / den
    rexp = _group_mask(128, 8, S5_GROUP, 1)
    are, aie = _dot_hi(rexp, ar), _dot_hi(rexp, ai)
    cre, cie = _dot_hi(rexp, cr), _dot_hi(rexp, ci)
    bbr = cre * b_re - cie * b_im
    bbi = cre * b_im + cie * b_re
    gmask = _group_mask(128, 128, S5_GROUP, S5_GROUP)
    pr, pi = jnp.ones_like(are), jnp.zeros_like(are)
    xs, ys = [], []
    for t in range(S5_T + 1):
        if t < S5_T:
            xs.append(jnp.concatenate([bbr * pr - bbi * pi, bbr * pi + bbi * pr], axis=1))
        ys.append(jnp.concatenate([c_re * pr - c_im * pi, -(c_re * pi + c_im * pr)], axis=1))
        pr, pi = pr * are - pi * aie, pr * aie + pi * are
    gs = [_dot_nt_hi(x_t, ys[0]) * gmask for x_t in xs]
    r16, i16 = ar, ai
    for _ in range(4):
        r16, i16 = r16 * r16 - i16 * i16, 2.0 * r16 * i16
    return xs, ys, gs, jnp.concatenate([r16, i16], axis=1)


def _s5_expand(z):
    return jnp.concatenate([z] * 8, axis=1) * _group_mask(128, S5_SW, S5_GROUP, 128)


def _s5_contract(z):
    zm = z * _group_mask(128, S5_SW, S5_GROUP, 128)
    acc = zm[:, 0:128]
    for k in range(1, 8):
        acc = acc + zm[:, 128 * k:128 * (k + 1)]
    return acc


def _s5_param_specs():
    blk3 = lambda r, c: pl.BlockSpec((1, 1, r, c), lambda b, *_: (0, b, 0, 0))
    dir3 = lambda r, c: pl.BlockSpec((2, 1, r, c), lambda b, *_: (0, b, 0, 0))
    return [dir3(8, S5_STATE), dir3(8, S5_STATE), dir3(8, 1), blk3(128, S5_STATE), blk3(128, S5_STATE),
            blk3(128, S5_STATE), blk3(128, S5_STATE), blk3(1, 128)]


def _s5_gen(lre, lim, lst, b_re, b_im, c_re, c_im, dvec, placed, kinds):
    n = len(placed)
    shard_shapes = _gather_shard_shapes(placed, kinds)

    def body(lre_ref, lim_ref, lst_ref, bre_ref, bim_ref, cre_ref, cim_ref, d_ref, *rest):
        gg_ref, xw_ref, yw_ref, a16_ref = rest[n:n + 4]
        _behind(pl.program_id(0), S5_NB - 1,
                functools.partial(_gather_chip_copies, rest[n + 4:2 * n + 4], kinds, shard_shapes, *rest[2 * n + 4:]))
        eye = _group_mask(128, 128, 1, 1)
        g0 = eye * d_ref[0, 0]
        for dr in range(2):
            xs, ys, gs, a16 = _s5_gen_dir(lre_ref[dr, 0], lim_ref[dr, 0], lst_ref[dr, 0], bre_ref[0, 0],
                                          bim_ref[0, 0], cre_ref[0, 0], cim_ref[0, 0])
            a16_ref[0, dr] = a16
            for j in range(S5_T):
                xw_ref[0, dr, j] = xs[S5_T - 1 - j if dr == 0 else j]
                yw_ref[0, dr, j] = ys[j + 1 if dr == 0 else S5_T - j]
            g0 = g0 + gs[0]
            for t in range(1, S5_T):
                gg_ref[0, (S5_T - 1) + t if dr == 0 else (S5_T - 1) - t] = gs[t]
        gg_ref[0, S5_T - 1] = g0

    blk = pl.BlockSpec((1, 2, S5_T, 128, 128), lambda b: (b, 0, 0, 0, 0))
    return pl.pallas_call(
        body, name="s5_gen", grid=(S5_NB,),
        in_specs=_s5_param_specs() + [ANY] * n,
        out_specs=[pl.BlockSpec((1, 2 * S5_T - 1, 128, 128), lambda b: (b, 0, 0, 0)), blk, blk,
                   pl.BlockSpec((1, 2, 8, 128), lambda b: (b, 0, 0, 0))] + [ANY] * n,
        out_shape=[jax.ShapeDtypeStruct((S5_NB, 2 * S5_T - 1, 128, 128), F32),
                   jax.ShapeDtypeStruct((S5_NB, 2, S5_T, 128, 128), F32),
                   jax.ShapeDtypeStruct((S5_NB, 2, S5_T, 128, 128), F32),
                   jax.ShapeDtypeStruct((S5_NB, 2, 8, 128), F32)]
        + [jax.ShapeDtypeStruct(p.shape, p.dtype) for p in placed],
        input_output_aliases={8 + a: 4 + a for a in range(n)},
        scratch_shapes=[pltpu.SemaphoreType.DMA((n, 3)), pltpu.SemaphoreType.DMA((n, 3))],
        compiler_params=_params(("arbitrary",)),
    )(lre, lim, lst, b_re, b_im, c_re, c_im, dvec, *placed)


def _s5_fill_state_mat(w_scr, src_ref, dr):
    for j in range(S5_T):
        w_scr[128 * j:128 * (j + 1), :] = _s5_expand(src_ref[0, dr, j]).astype(BF16)


def _s5_fill_toeplitz(k_scr, gg_ref):
    for j in range(S5_T):
        for i in range(S5_T):
            k_scr[128 * j:128 * (j + 1), 128 * i:128 * (i + 1)] = gg_ref[0, i - j + (S5_T - 1)].astype(BF16)


S5_GEN_SPECS = [pl.BlockSpec((1, 2 * S5_T - 1, 128, 128), lambda b: (b, 0, 0, 0)),
                pl.BlockSpec((1, 2, S5_T, 128, 128), lambda b: (b, 0, 0, 0, 0))]


def _s5_gen_bwd(lre, lim, lst, b_re, b_im, c_re, c_im, dvec, dg, dx, dy, da16):
    def body(lre_ref, lim_ref, lst_ref, bre_ref, bim_ref, cre_ref, cim_ref, d_ref, dg_ref, dx_ref, dy_ref, da16_ref,
             glre_ref, glim_ref, glst_ref, gbre_ref, gbim_ref, gcre_ref, gcim_ref, gd_ref):
        eye = _group_mask(128, 128, 1, 1)
        gd_ref[0, 0] = jnp.sum(dg_ref[0, S5_T - 1] * eye, axis=0, keepdims=True)
        gb = [None, None, None, None]
        for dr in range(2):
            args = (lre_ref[dr, 0], lim_ref[dr, 0], lst_ref[dr, 0], bre_ref[0, 0], bim_ref[0, 0],
                    cre_ref[0, 0], cim_ref[0, 0])
            _, vjp = jax.vjp(_s5_gen_dir, *args)
            dxs = [dx_ref[0, dr, S5_T - 1 - t if dr == 0 else t] for t in range(S5_T)]
            dys = [jnp.zeros((128, 128), F32)] + [dy_ref[0, dr, t - 1 if dr == 0 else S5_T - t]
                                                  for t in range(1, S5_T + 1)]
            dgs = [dg_ref[0, (S5_T - 1) + t if dr == 0 else (S5_T - 1) - t] for t in range(S5_T)]
            g = vjp((dxs, dys, dgs, da16_ref[0, dr]))
            glre_ref[dr, 0] = g[0]
            glim_ref[dr, 0] = g[1]
            glst_ref[dr, 0] = g[2]
            for q in range(4):
                gb[q] = g[3 + q] if gb[q] is None else gb[q] + g[3 + q]
        gbre_ref[0, 0] = gb[0]
        gbim_ref[0, 0] = gb[1]
        gcre_ref[0, 0] = gb[2]
        gcim_ref[0, 0] = gb[3]

    shp = lambda a: jax.ShapeDtypeStruct(a.shape, F32)
    return pl.pallas_call(
        body, name="s5_gen_bwd", grid=(S5_NB,),
        in_specs=_s5_param_specs() + [
            pl.BlockSpec((1, 2 * S5_T - 1, 128, 128), lambda b: (b, 0, 0, 0)),
            pl.BlockSpec((1, 2, S5_T, 128, 128), lambda b: (b, 0, 0, 0, 0)),
            pl.BlockSpec((1, 2, S5_T, 128, 128), lambda b: (b, 0, 0, 0, 0)),
            pl.BlockSpec((1, 2, 8, 128), lambda b: (b, 0, 0, 0))],
        out_specs=_s5_param_specs(),
        out_shape=[shp(lre), shp(lim), shp(lst), shp(b_re), shp(b_im), shp(c_re), shp(c_im), shp(dvec)],
        compiler_params=_params(("parallel",)),
    )(lre, lim, lst, b_re, b_im, c_re, c_im, dvec, dg, dx, dy, da16)


def _s5_put_groups(o_ref, dr, val):
    for gi in range(8):
        o_ref[dr, :, gi, :] = val[:, 128 * gi:128 * (gi + 1)]


def _s5_get_groups(s_ref, dr, n=8):
    return jnp.concatenate([s_ref[dr, :, gi, :] for gi in range(n)], axis=1).astype(BF16)


def _s5_to_states(u3, blocks, name):
    cn = u3.shape[1]

    def body(u_ref, b_ref, o_ref, w_scr):
        u = u_ref[0]
        for dr in range(2):
            _s5_fill_state_mat(w_scr, b_ref, dr)
            _s5_put_groups(o_ref, dr, _dot(u, w_scr[...]))

    return pl.pallas_call(
        body, name=name, grid=(S5_NB,),
        in_specs=[pl.BlockSpec((1, cn, S5_BW), lambda b: (b, 0, 0)), S5_GEN_SPECS[1]],
        out_specs=pl.BlockSpec((2, cn, 8, 128), lambda b: (0, 0, b, 0)),
        out_shape=jax.ShapeDtypeStruct((2, cn, S5_GROUPS, 128), F32),
        scratch_shapes=[pltpu.VMEM((S5_BW, S5_SW), BF16)],
        compiler_params=_params(("parallel",)),
    )(u3, blocks)


def _s5_from_states(u3, gg, st, blocks, transposed, name):
    cn = u3.shape[1]

    def body(u_ref, g_ref, s_ref, b_ref, o_ref, k_scr, w_scr):
        u = u_ref[0]
        _s5_fill_toeplitz(k_scr, g_ref)
        y = _dot_nt(u, k_scr[...]) if transposed else _dot(u, k_scr[...])
        for dr in range(2):
            _s5_fill_state_mat(w_scr, b_ref, dr)
            y = y + _dot_nt(_s5_get_groups(s_ref, dr), w_scr[...])
        for i in range(S5_T):
            o_ref[:, i, :] = y[:, 128 * i:128 * (i + 1)]

    return pl.pallas_call(
        body, name=name, grid=(S5_NB,),
        in_specs=[pl.BlockSpec((1, cn, S5_BW), lambda b: (b, 0, 0)), S5_GEN_SPECS[0],
                  pl.BlockSpec((2, cn, 8, 128), lambda b: (0, 0, b, 0)), S5_GEN_SPECS[1]],
        out_specs=pl.BlockSpec((cn, S5_T, 128), lambda b: (0, 0, b)),
        out_shape=jax.ShapeDtypeStruct((cn, S5_T, S5_WIDTH), F32),
        scratch_shapes=[pltpu.VMEM((S5_BW, S5_BW), BF16), pltpu.VMEM((S5_BW, S5_SW), BF16)],
        compiler_params=_params(("parallel",)),
    )(u3, gg, st, blocks)


def _s5_a_forms(a):
    ra = pltpu.roll(a, S5_STATE, 1)
    low = _iota2(a.shape, 1) < S5_STATE
    return jnp.where(low, a, ra), jnp.where(low, -ra, a)


def _s5_scan(sloc, a16, ncc, placed, kinds):
    cn = sloc.shape[1]
    n = len(placed)
    shard_shapes = _gather_shard_shapes(placed, kinds)

    def body(s_ref, a_ref, *rest):
        h_ref = rest[n]
        sends, arrivals = _gather_chip_copies(rest[n + 1:2 * n + 1], kinds, shard_shapes, *rest[2 * n + 1:])
        for cp in sends:
            cp.start()
        forms = [_s5_a_forms(a_ref[dr]) for dr in range(2)]

        def step(s, hs):
            out = []
            for dr in range(2):
                arr, aii = forms[dr]
                h, rh = hs[dr]
                c = s if dr == 0 else jnp.where(s < ncc, ncc - 1 - s, cn - 1 - (s - ncc))
                h_ref[dr, c] = h
                sc = s_ref[dr, c]
                out.append((h * arr + rh * aii + sc, rh * arr - h * aii + pltpu.roll(sc, S5_STATE, 1)))
            return tuple(out)

        zero = jnp.zeros((S5_GROUPS, 128), F32)
        lax.fori_loop(0, cn, step, ((zero, zero), (zero, zero)), unroll=4)
        for cp in arrivals:
            cp.wait_recv()
        for cp in sends:
            cp.wait_send()

    vmem = pl.BlockSpec(memory_space=pltpu.VMEM)
    return pl.pallas_call(
        body, name="s5_scan",
        in_specs=[vmem, vmem] + [ANY] * n, out_specs=[vmem] + [ANY] * n,
        out_shape=[jax.ShapeDtypeStruct(sloc.shape, F32)] + [jax.ShapeDtypeStruct(p.shape, p.dtype) for p in placed],
        input_output_aliases={2 + a: 1 + a for a in range(n)},
        scratch_shapes=[pltpu.SemaphoreType.DMA((n, 3)), pltpu.SemaphoreType.DMA((n, 3))],
        compiler_params=_params(),
    )(sloc, a16, *placed)


def _s5_scan_bwd(e, hs, a16, ncc):
    cn = e.shape[1]

    def body(e_ref, h_ref, a_ref, ds_ref, da_ref):
        forms = [_s5_a_forms(a_ref[dr]) for dr in range(2)]
        low = _iota2((S5_GROUPS, 128), 1) < S5_STATE

        def step(s, carry):
            out = []
            r = cn - 1 - s
            for dr in range(2):
                arr, aii = forms[dr]
                g, rg, da = carry[dr]
                c = r if dr == 0 else jnp.where(r < ncc, ncc - 1 - r, cn - 1 - (r - ncc))
                ds_ref[dr, c] = g
                h = h_ref[dr, c]
                rh = pltpu.roll(h, S5_STATE, 1)
                da = da + jnp.where(low, g * h + rg * rh, g * rh - rg * h)
                ec = e_ref[dr, c]
                out.append((ec + g * arr - rg * aii, pltpu.roll(ec, S5_STATE, 1) + rg * arr + g * aii, da))
            return tuple(out)

        zero = jnp.zeros((S5_GROUPS, 128), F32)
        res = lax.fori_loop(0, cn, step, ((zero, zero, zero), (zero, zero, zero)), unroll=4)
        da_ref[0] = res[0][2]
        da_ref[1] = res[1][2]

    return pl.pallas_call(
        body, name="s5_scan_bwd",
        out_shape=[jax.ShapeDtypeStruct(e.shape, F32), jax.ShapeDtypeStruct((2, S5_GROUPS, 128), F32)],
        compiler_params=_params(),
    )(e, hs, a16)


def _s5_bwd_kb(p3, dy3):
    cn = p3.shape[1]
    half = S5_T // 2

    def body(u_ref, d_ref, o_ref):
        q = pl.program_id(1)

        @pl.when(q == 0)
        def _():
            o_ref[...] = jnp.zeros_like(o_ref)

        dk = _dot_tn(u_ref[0], d_ref[0])
        for j in range(S5_T):
            for i in range(half):
                o_ref[0, half * q + i - j + (S5_T - 1)] += dk[128 * j:128 * (j + 1), 128 * i:128 * (i + 1)]

    return pl.pallas_call(
        body, name="s5_bwd_kb", grid=(S5_NB, 2),
        in_specs=[pl.BlockSpec((1, cn, S5_BW), lambda b, q: (b, 0, 0)),
                  pl.BlockSpec((1, cn, S5_BW // 2), lambda b, q: (b, 0, q))],
        out_specs=pl.BlockSpec((1, 2 * S5_T - 1, 128, 128), lambda b, q: (b, 0, 0, 0)),
        out_shape=jax.ShapeDtypeStruct((S5_NB, 2 * S5_T - 1, 128, 128), F32),
        compiler_params=_params(("parallel", "arbitrary")),
    )(p3, dy3)


def _s5_bwd_w(u3, st, name):
    cn = u3.shape[1]

    def body(u_ref, s_ref, w_ref):
        dw = _dot_tn(u_ref[0], _s5_get_groups(s_ref, 0))
        for j in range(S5_T):
            w_ref[0, 0, j] = _s5_contract(dw[128 * j:128 * (j + 1), :])

    return pl.pallas_call(
        body, name=name, grid=(S5_NB, 2),
        in_specs=[pl.BlockSpec((1, cn, S5_BW), lambda b, q: (b, 0, 0)),
                  pl.BlockSpec((1, cn, 8, 128), lambda b, q: (q, 0, b, 0))],
        out_specs=pl.BlockSpec((1, 1, S5_T, 128, 128), lambda b, q: (b, q, 0, 0, 0)),
        out_shape=jax.ShapeDtypeStruct((S5_NB, 2, S5_T, 128, 128), F32),
        compiler_params=_params(("parallel", "parallel")),
    )(u3, st)


K_SCALE = RET_DH ** -0.5
G_COL = 16


def _ret_chunk_of(step, ncc, nch, rev):
    if not rev:
        return step
    return jnp.where(step < ncc, ncc - 1 - step, nch - 1 - (step - ncc))


def _ret_decay(ld, rev):
    c = _iota2((RET_CHUNK, RET_CHUNK), 0).astype(F32)
    m = _iota2((RET_CHUNK, RET_CHUNK), 1).astype(F32)
    diff = (m - c) if rev else (c - m)
    keep = (diff > 0) if rev else (diff >= 0)
    expo = jnp.maximum(diff, 0.0)
    dm = jnp.where(keep, jnp.exp(ld * expo), 0.0)
    xi_e = (RET_CHUNK - c) if rev else (c + 1.0)
    zeta_e = c if rev else (RET_CHUNK - 1.0 - c)
    return dm, expo, jnp.exp(ld * xi_e), xi_e, jnp.exp(ld * zeta_e), zeta_e


RET_TABLES = 7


def _ret_tables(ld2):
    def body(ld_ref, t_ref):
        dr, h = pl.program_id(0), pl.program_id(1)
        ldh = ld_ref[dr, h]
        for rev in (False, True):
            @pl.when(dr == int(rev))
            def _(rev=rev):
                dm, expo, xi, xi_e, zeta, zeta_e = _ret_decay(ldh, rev)
                t_ref[0, 0, 0] = dm
                t_ref[0, 0, 1] = dm * expo
                t_ref[0, 0, 2] = xi
                t_ref[0, 0, 3] = xi * xi_e
                t_ref[0, 0, 4] = zeta
                t_ref[0, 0, 5] = zeta * zeta_e
                t_ref[0, 0, 6] = jnp.zeros_like(dm) + jnp.exp(ldh * RET_CHUNK)

    return pl.pallas_call(
        body, name="ret_tables", grid=(2, RET_HEADS),
        in_specs=[pl.BlockSpec(memory_space=pltpu.SMEM)],
        out_specs=pl.BlockSpec((1, 1, RET_TABLES, RET_CHUNK, RET_CHUNK), lambda d, h: (d, h, 0, 0, 0)),
        out_shape=jax.ShapeDtypeStruct((2, RET_HEADS, RET_TABLES, RET_CHUNK, RET_CHUNK), F32),
        compiler_params=_params(("parallel", "parallel")),
    )(ld2)


def _ret_specs(nch, ncc, rev, step_of):
    chunk = lambda n: _ret_chunk_of(step_of(n), ncc, nch, rev)
    cols = [pl.BlockSpec((RET_CHUNK, RET_WIDTH), functools.partial(lambda n, cb: (chunk(n), cb), cb=cb))
            for cb in (1, 2, 3)]
    return cols, pl.BlockSpec((RET_CHUNK, RET_WIDTH), lambda n: (chunk(n), 0))


def _ret_scan(p_all, tabs, ncc):
    la = p_all.shape[0]
    nch = la // RET_CHUNK

    def body(t_ref, qf, kf, vf, qb, kb, vb, of_ref, ob_ref, ssf_ref, ssb_ref, s_scr):
        @pl.when(pl.program_id(0) == 0)
        def _():
            s_scr[...] = jnp.zeros_like(s_scr)

        for dr, (q_ref, k_ref, v_ref, o_ref, ss_ref) in enumerate(
                ((qf, kf, vf, of_ref, ssf_ref), (qb, kb, vb, ob_ref, ssb_ref))):
            for h in range(RET_HEADS):
                sl = slice(RET_DH * h, RET_DH * (h + 1))
                dm, xi, zeta = t_ref[dr, h, 0], t_ref[dr, h, 2, :, 0:RET_DH], t_ref[dr, h, 4, :, 0:RET_DH]
                q, k = q_ref[:, sl], k_ref[:, sl]
                vh = v_ref[:, sl].astype(BF16)
                s = s_scr[dr, h]
                ss_ref[0, h] = s
                sc = (_dot_nt(q.astype(BF16), k.astype(BF16)) * dm).astype(BF16)
                o_ref[:, sl] = _dot(sc, vh) + _dot((q * xi).astype(BF16), s.astype(BF16))
                s_scr[dr, h] = t_ref[dr, h, 6, 0:RET_DH, 0:RET_DH] * s + _dot_tn((k * zeta).astype(BF16), vh)

    in_f, out_f = _ret_specs(nch, ncc, False, lambda n: n)
    in_b, out_b = _ret_specs(nch, ncc, True, lambda n: n)
    ss_spec = pl.BlockSpec((1, RET_HEADS, RET_DH, RET_DH), lambda n: (n, 0, 0, 0))
    o_shape = jax.ShapeDtypeStruct((la, RET_WIDTH), F32)
    ss_shape = jax.ShapeDtypeStruct((nch, RET_HEADS, RET_DH, RET_DH), F32)
    return pl.pallas_call(
        body, name="ret_scan", grid=(nch,),
        in_specs=[_full(tabs.shape)] + in_f + in_b,
        out_specs=[out_f, out_b, ss_spec, ss_spec],
        out_shape=[o_shape, o_shape, ss_shape, ss_shape],
        scratch_shapes=[pltpu.VMEM((2, RET_HEADS, RET_DH, RET_DH), F32)],
        compiler_params=_params(("arbitrary",)),
    )(tabs, p_all, p_all, p_all, p_all, p_all, p_all)


def _ret_scan_bwd(p_all, tabs, ssf, ssb, dy_all, ncc):
    la = p_all.shape[0]
    nch = la // RET_CHUNK

    def body(t_ref, qf, kf, vf, dof, ssf_ref, qb, kb, vb, dob_, ssb_ref,
             dqf, dkf, dvf, dqb, dkb, dvb, dld_ref, ds_scr):
        @pl.when(pl.program_id(0) == 0)
        def _():
            ds_scr[...] = jnp.zeros_like(ds_scr)
            dld_ref[...] = jnp.zeros_like(dld_ref)

        for dr, (q_ref, k_ref, v_ref, do_ref, ss_ref, dq_ref, dk_ref, dv_ref) in enumerate(
                ((qf, kf, vf, dof, ssf_ref, dqf, dkf, dvf), (qb, kb, vb, dob_, ssb_ref, dqb, dkb, dvb))):
            on_ctx = _ret_chunk_of(nch - 1 - pl.program_id(0), ncc, nch, dr == 1) < ncc
            for h in range(RET_HEADS):
                sl = slice(RET_DH * h, RET_DH * (h + 1))
                dm, dm_d = t_ref[dr, h, 0], t_ref[dr, h, 1]
                xi, xi_d, zeta, zeta_d = [t_ref[dr, h, t, :, 0:RET_DH] for t in (2, 3, 4, 5)]
                gc = t_ref[dr, h, 6, 0:RET_DH, 0:RET_DH]
                q, k = q_ref[:, sl], k_ref[:, sl]
                q16, k16, v16 = q.astype(BF16), k.astype(BF16), v_ref[:, sl].astype(BF16)
                s = ss_ref[0, h]
                s16 = s.astype(BF16)
                ds_in = ds_scr[dr, h]
                ds16 = ds_in.astype(BF16)
                do16 = jnp.where(on_ctx, 0.0, do_ref[:, sl]).astype(BF16)
                qk = _dot_nt(q16, k16)
                dsv = _dot_nt(do16, v16)
                dsc = (dsv * dm).astype(BF16)
                sc16 = (qk * dm).astype(BF16)
                dos = _dot_nt(do16, s16)
                vds = _dot_nt(v16, ds16)
                dq_ref[:, sl] = _dot(dsc, k16) + dos * xi
                dk_ref[:, sl] = _dot_tn(dsc, q16) + vds * zeta
                dv_ref[:, sl] = _dot_tn(sc16, do16) + _dot((k * zeta).astype(BF16), ds16)
                ds_scr[dr, h] = _dot_tn((q * xi).astype(BF16), do16) + gc * ds_in
                dld = (jnp.sum(dsv * qk * dm_d) + jnp.sum(q * dos * xi_d + k * vds * zeta_d)
                       + RET_CHUNK * jnp.sum(gc * s * ds_in))
                dld_ref[dr, h] += dld

    back = lambda n: nch - 1 - n
    in_f, out_f = _ret_specs(nch, ncc, False, back)
    in_b, out_b = _ret_specs(nch, ncc, True, back)
    ss_spec = pl.BlockSpec((1, RET_HEADS, RET_DH, RET_DH), lambda n: (nch - 1 - n, 0, 0, 0))
    shp = jax.ShapeDtypeStruct((la, RET_WIDTH), F32)
    dy_spec = lambda rev: pl.BlockSpec(
        (RET_CHUNK, RET_WIDTH), lambda n: (jnp.maximum(_ret_chunk_of(nch - 1 - n, ncc, nch, rev) - ncc, 0), 0))
    return pl.pallas_call(
        body, name="ret_scan_bwd", grid=(nch,),
        in_specs=[_full(tabs.shape)] + in_f + [dy_spec(False), ss_spec] + in_b + [dy_spec(True), ss_spec],
        out_specs=[out_f, out_f, out_f, out_b, out_b, out_b, _full((2, RET_HEADS, 8, 128))],
        out_shape=[shp] * 6 + [jax.ShapeDtypeStruct((2, RET_HEADS, 8, 128), F32)],
        scratch_shapes=[pltpu.VMEM((2, RET_HEADS, RET_DH, RET_DH), F32)],
        compiler_params=_params(("arbitrary",)),
    )(tabs, p_all, p_all, p_all, dy_all, ssf, p_all, p_all, p_all, dy_all, ssb)


def _in_bwd(dqf, dkf, dvf, dqb, dkb, dvb, du, dg, cos_t, sin_t, w_in_b, x, ctx, n1w, mod4, dx1):
    l, lc = x.shape[0], ctx.shape[0]
    la = l + lc
    tm = TOK_TILE
    nct = lc // tm

    def body(dqf_ref, dkf_ref, dvf_ref, dqb_ref, dkb_ref, dvb_ref, du_ref, dg_ref, cos_ref, sin_ref,
             w_ref, x_ref, c_ref, nw_ref, mod_ref, dx1_ref, dp_ref, gx_ref, acc_ref):
        i = pl.program_id(0)
        is_ctx = i < nct

        @pl.when(i == 0)
        def _():
            acc_ref[...] = jnp.zeros_like(acc_ref)

        cs, sn = cos_ref[...], sin_ref[...]
        def piece(k, val):
            cols = slice(S5_WIDTH * k, S5_WIDTH * (k + 1))
            dp_ref[:, cols] = val.astype(BF16)
            return _dot_nt(dp_ref[:, cols], w_ref[:, cols])

        dh1 = piece(0, du_ref[...])
        dh1 = dh1 + piece(3, dvf_ref[...] + dvb_ref[...])
        dh1 = dh1 + piece(4, jnp.where(is_ctx, 0.0, dg_ref[...]))
        for k, (f_ref, b_ref, scale) in ((1, (dqf_ref, dqb_ref, 1.0)), (2, (dkf_ref, dkb_ref, K_SCALE))):
            heads = [_rope_t(f_ref[:, RET_DH * h:RET_DH * (h + 1)] + b_ref[:, RET_DH * h:RET_DH * (h + 1)], cs, sn) * scale
                     for h in range(RET_HEADS)]
            dh1 = dh1 + piece(k, jnp.concatenate(heads, axis=1))
        xt = jnp.where(is_ctx, c_ref[...], x_ref[...])
        sh = jnp.where(is_ctx, mod_ref[0:1, :], mod_ref[2:3, :])
        sc = jnp.where(is_ctx, mod_ref[1:2, :], mod_ref[3:4, :])
        _, vjp = jax.vjp(_rms_mod, xt, nw_ref[...], sh, sc)
        dx, dnw, dsh, dsc = vjp(dh1)
        gx_ref[...] = dx + dx1_ref[...]
        cf = jnp.where(is_ctx, 1.0, 0.0)
        acc_ref[0:1, :] += dnw
        acc_ref[1:2, :] += cf * dsh
        acc_ref[2:3, :] += cf * dsc
        acc_ref[3:4, :] += (1.0 - cf) * dsh
        acc_ref[4:5, :] += (1.0 - cf) * dsc

    row = pl.BlockSpec((tm, RET_WIDTH), lambda i: (i, 0))
    tab = pl.BlockSpec((tm, RET_DH), lambda i: (i, 0))
    xrow = pl.BlockSpec((tm, D_MODEL), lambda i: (jnp.maximum(i - nct, 0), 0))
    return pl.pallas_call(
        body, name="in_bwd", grid=(la // tm,),
        in_specs=[row] * 7 + [pl.BlockSpec((tm, RET_WIDTH), lambda i: (jnp.maximum(i - nct, 0), 0)),
                              tab, tab, _full((D_MODEL, IN_COLS)), xrow,
                              pl.BlockSpec((tm, D_MODEL), lambda i: (jnp.minimum(i, nct - 1), 0)),
                              _full((1, D_MODEL)), _full((4, D_MODEL)), xrow],
        out_specs=[pl.BlockSpec((tm, IN_COLS), lambda i: (i, 0)), xrow, _full((8, D_MODEL))],
        out_shape=[jax.ShapeDtypeStruct((la, IN_COLS), BF16), jax.ShapeDtypeStruct((l, D_MODEL), F32),
                   jax.ShapeDtypeStruct((8, D_MODEL), F32)],
        compiler_params=_params(("arbitrary",)),
    )(dqf, dkf, dvf, dqb, dkb, dvb, du, dg, cos_t, sin_t, w_in_b, x, ctx, n1w, mod4, dx1)


def _outproj_up(x, y_all, of, ob, p_all, w_glu_b, b_glu, w_out_b, mod3, n2w, w_up_b, nct):
    l = x.shape[0]
    tm = TOK_TILE

    def body(x_ref, y_ref, of_ref, ob_ref, g_ref, wg_ref, bg_ref, wo_ref, mod_ref, nw_ref, wu_ref,
             x1_ref, mix_ref, h2_ref, up_ref, mb_ref, yr_ref):
        yg = _gelu(y_ref[...])
        mb_ref[:, 0:S5_WIDTH] = (yg * _sigmoid(_dot(yg.astype(BF16), wg_ref[...]) + bg_ref[...])).astype(BF16)
        yr = of_ref[...] + ob_ref[...]
        yr_ref[...] = yr
        for h in range(RET_HEADS):
            sl = slice(RET_DH * h, RET_DH * (h + 1))
            mb_ref[:, S5_WIDTH + RET_DH * h:S5_WIDTH + RET_DH * (h + 1)] = (
                _head_norm_gate(yr[:, sl], g_ref[:, sl]).astype(BF16))
        mix = _dot(mb_ref[...], wo_ref[...])
        mix_ref[...] = mix
        x1 = x_ref[...] + mod_ref[0:1, :] * mix
        x1_ref[...] = x1
        h2 = _rms_mod(x1, nw_ref[...], mod_ref[1:2, :], mod_ref[2:3, :]).astype(BF16)
        h2_ref[...] = h2
        up_ref[...] = _dot(h2, wu_ref[...])

    row = lambda w: pl.BlockSpec((tm, w), lambda i: (i, 0))
    arow = pl.BlockSpec((tm, RET_WIDTH), lambda i: (i + nct, 0))
    return pl.pallas_call(
        body, name="outproj_up", grid=(l // tm,),
        in_specs=[row(D_MODEL), arow, arow, arow, pl.BlockSpec((tm, RET_WIDTH), lambda i: (i + nct, G_COL // 4)),
                  _full((S5_WIDTH, S5_WIDTH)), _full((1, S5_WIDTH)), _full((D_MODEL, D_MODEL)), _full((3, D_MODEL)),
                  _full((1, D_MODEL)), _full((D_MODEL, 2 * D_FF))],
        out_specs=[row(D_MODEL), row(D_MODEL), row(D_MODEL), row(2 * D_FF), row(D_MODEL), row(RET_WIDTH)],
        out_shape=[jax.ShapeDtypeStruct((l, D_MODEL), F32), jax.ShapeDtypeStruct((l, D_MODEL), F32),
                   jax.ShapeDtypeStruct((l, D_MODEL), BF16), jax.ShapeDtypeStruct((l, 2 * D_FF), F32),
                   jax.ShapeDtypeStruct((l, D_MODEL), BF16), jax.ShapeDtypeStruct((l, RET_WIDTH), F32)],
        compiler_params=_params(("parallel",)),
    )(x, y_all, of, ob, p_all, w_glu_b, b_glu, w_out_b, mod3, n2w, w_up_b)


HALO = 8


def _conv_taps(g, prev_row, next_row):
    t = g.shape[0]
    r = _iota2(g.shape, 0)
    gprev = jnp.where(r == 0, prev_row, pltpu.roll(g, 1, 0))
    gnext = jnp.where(r == t - 1, next_row, pltpu.roll(g, t - 1, 0))
    return gprev, gnext


def _ffn_loss(up, x1, conv_w, conv_b, w_down_b, gate, fnw, tgt):
    l = x1.shape[0]
    tm = TOK_TILE
    nt = l // tm
    hb = tm // HALO

    cw = 256

    def body(up_a, up_g, hp_ref, hn_ref, x1_ref, cw_ref, cb_ref, wd_ref, gate_ref, fn_ref, tgt_ref,
             act_ref, dx2_ref, ddn_ref, dact_ref, acc_ref, ddn_scr):
        step = pl.program_id(0)
        i = jnp.minimum(step, nt - 1)

        @pl.when(step == 0)
        def _():
            acc_ref[...] = jnp.zeros_like(acc_ref)
            ddn_scr[...] = jnp.zeros_like(ddn_scr)

        ddn_prev = ddn_scr[...]
        dn = jnp.zeros((tm, D_MODEL), F32)
        for c in range(D_FF // cw):
            cols = slice(cw * c, cw * (c + 1))
            g = up_g[:, cols]
            prev_row = jnp.where(i == 0, 0.0, hp_ref[HALO - 1:HALO, cols])
            next_row = jnp.where(i == nt - 1, 0.0, hn_ref[0:1, cols])
            gprev, gnext = _conv_taps(g, prev_row, next_row)
            gc = cb_ref[:, cols] + gprev * cw_ref[0:1, cols] + g * cw_ref[1:2, cols] + gnext * cw_ref[2:3, cols]
            act = (_gelu(gc) * up_a[:, cols]).astype(BF16)
            act_ref[:, cols] = act
            dn = dn + _dot(act, wd_ref[cols, :])
            dact_ref[:, cols] = _dot_nt(ddn_prev, wd_ref[cols, :])
        x2 = x1_ref[...] + gate_ref[...] * dn
        y, vjp = jax.vjp(_rms, x2, fn_ref[...])
        err = y - tgt_ref[...]
        dx2, dfn = vjp(err * (1.0 / D_MODEL))
        dx2_ref[...] = dx2
        ddn = (dx2 * gate_ref[...]).astype(BF16)
        ddn_ref[...] = ddn
        ddn_scr[...] = ddn
        live = step < nt
        acc_ref[0:1, :] += jnp.where(live, dfn, 0.0)
        acc_ref[1:2, :] += jnp.where(live, jnp.sum(dx2 * dn, axis=0, keepdims=True), 0.0)
        acc_ref[2:3, :] += jnp.where(live, (0.5 / D_MODEL) * jnp.sum(err * err), 0.0)

    tile = lambda s: jnp.minimum(s, nt - 1)
    row = lambda w, cb=0: pl.BlockSpec((tm, w), lambda s: (tile(s), cb))
    last = l // HALO - 1
    return pl.pallas_call(
        body, name="ffn_loss", grid=(nt + 1,),
        in_specs=[row(D_FF, 0), row(D_FF, 1),
                  pl.BlockSpec((HALO, D_FF), lambda s: (jnp.maximum(tile(s) * hb - 1, 0), 1)),
                  pl.BlockSpec((HALO, D_FF), lambda s: (jnp.minimum((tile(s) + 1) * hb, last), 1)),
                  row(D_MODEL), _full((3, D_FF)), _full((1, D_FF)), _full((D_FF, D_MODEL)),
                  _full((1, D_MODEL)), _full((1, D_MODEL)), row(D_MODEL)],
        out_specs=[row(D_FF), row(D_MODEL), row(D_MODEL),
                   pl.BlockSpec((tm, D_FF), lambda s: (jnp.maximum(s - 1, 0), 0)), _full((8, D_MODEL))],
        out_shape=[jax.ShapeDtypeStruct((l, D_FF), BF16), jax.ShapeDtypeStruct((l, D_MODEL), F32),
                   jax.ShapeDtypeStruct((l, D_MODEL), BF16), jax.ShapeDtypeStruct((l, D_FF), F32),
                   jax.ShapeDtypeStruct((8, D_MODEL), F32)],
        scratch_shapes=[pltpu.VMEM((tm, D_MODEL), BF16)],
        compiler_params=_params(("arbitrary",)),
    )(up, up, up, up, x1, conv_w, conv_b, w_down_b, gate, fnw, tgt)


def _convglu_bwd(up, dact, conv_w, conv_b):
    l = up.shape[0]
    tm = 128
    nt = l // tm
    hb = tm // HALO
    te = tm + 2 * HALO

    def body(a_ref, ap_ref, an_ref, g_ref, gp_ref, gn_ref, d_ref, dp_ref, dn_ref, cw_ref, cb_ref,
             dup_ref, acc_ref):
        i = pl.program_id(0)

        @pl.when(i == 0)
        def _():
            acc_ref[...] = jnp.zeros_like(acc_ref)

        def ext(p, c, n):
            return jnp.concatenate([jnp.where(i == 0, 0.0, p[...]), c[...], jnp.where(i == nt - 1, 0.0, n[...])], axis=0)

        ae, ge, de = ext(ap_ref, a_ref, an_ref), ext(gp_ref, g_ref, gn_ref), ext(dp_ref, d_ref, dn_ref)
        gprev = pltpu.roll(ge, 1, 0)
        gnext = pltpu.roll(ge, te - 1, 0)
        w0, w1, w2 = cw_ref[0:1, :], cw_ref[1:2, :], cw_ref[2:3, :]
        gce = cb_ref[...] + gprev * w0 + ge * w1 + gnext * w2
        gel, dgel = _gelu_and_grad(gce)
        dae = de * gel
        dgce = de * ae * dgel
        dge = dgce * w1 + pltpu.roll(dgce, te - 1, 0) * w0 + pltpu.roll(dgce, 1, 0) * w2
        mid = slice(HALO, HALO + tm)
        dup_ref[:, 0:D_FF] = dae[mid].astype(BF16)
        dup_ref[:, D_FF:2 * D_FF] = dge[mid].astype(BF16)
        dgc = dgce[mid]
        acc_ref[0:1, :] += jnp.sum(dgc * gprev[mid], axis=0, keepdims=True)
        acc_ref[1:2, :] += jnp.sum(dgc * ge[mid], axis=0, keepdims=True)
        acc_ref[2:3, :] += jnp.sum(dgc * gnext[mid], axis=0, keepdims=True)
        acc_ref[3:4, :] += jnp.sum(dgc, axis=0, keepdims=True)

    last = l // HALO - 1

    def trio(cb):
        return [pl.BlockSpec((tm, D_FF), lambda i: (i, cb)),
                pl.BlockSpec((HALO, D_FF), lambda i: (jnp.maximum(i * hb - 1, 0), cb)),
                pl.BlockSpec((HALO, D_FF), lambda i: (jnp.minimum((i + 1) * hb, last), cb))]

    return pl.pallas_call(
        body, name="convglu_bwd", grid=(nt,),
        in_specs=trio(0) + trio(1) + trio(0) + [_full((3, D_FF)), _full((1, D_FF))],
        out_specs=[pl.BlockSpec((tm, 2 * D_FF), lambda i: (i, 0)), _full((8, D_FF))],
        out_shape=[jax.ShapeDtypeStruct((l, 2 * D_FF), BF16), jax.ShapeDtypeStruct((8, D_FF), F32)],
        compiler_params=_params(("arbitrary",)),
    )(up, up, up, up, up, up, dact, dact, dact, conv_w, conv_b)


def _up_bwd(dup, w_up_b, w_out_b, x1, dx2, mix, mod3, n2w, y_all, y_ret, p_all, w_glu_b, b_glu, zero_rows, nct, pairs,
            kinds):
    l = x1.shape[0]
    tm = TOK_TILE
    nt = l // tm
    n = len(pairs)
    shapes = _rs_slot_shapes(pairs, kinds)
    n_out = 8

    def body(dup_ref, wu_ref, wo_ref, x1_ref, dx2_ref, mix_ref, mod_ref, nw_ref, y_ref, yr_ref, g_ref, wg_ref, bg_ref,
             zero_rows_ref, *rest):
        dx1_ref, dmixb_ref, acc_ref, dys_ref, dyr_ref, dg_ref, gw_ref, gb_ref = rest[n:n + n_out]
        send_sems, recv_sems, dy_scr = rest[2 * n + n_out:]
        step = pl.program_id(0)

        @pl.when(step == 0)
        def _():
            acc_ref[...] = jnp.zeros_like(acc_ref)
            gw_ref[...] = jnp.zeros_like(gw_ref)
            gb_ref[...] = jnp.zeros_like(gb_ref)

        _behind(step, nt - 1, functools.partial(_rs_chip_copies, rest[:n], rest[n + n_out:2 * n + n_out], kinds,
                                                shapes, send_sems, recv_sems))

        dh2 = _dot_nt(dup_ref[...], wu_ref[...])
        _, vjp = jax.vjp(_rms_mod, x1_ref[...], nw_ref[...], mod_ref[1:2, :], mod_ref[2:3, :])
        dx, dnw, dsh, dsc = vjp(dh2)
        dx1 = dx + dx2_ref[...]
        dx1_ref[...] = dx1
        dmixb = (dx1 * mod_ref[0:1, :]).astype(BF16)
        dmixb_ref[...] = dmixb
        dmix = _dot_nt(dmixb, wo_ref[...])
        acc_ref[0:1, :] += dnw
        acc_ref[1:2, :] += jnp.sum(dx1 * mix_ref[...], axis=0, keepdims=True)
        acc_ref[2:3, :] += dsh
        acc_ref[3:4, :] += dsc

        yg, dgel = _gelu_and_grad(y_ref[...])
        ygb = yg.astype(BF16)
        sg = _sigmoid(_dot(ygb, wg_ref[...]) + bg_ref[...])
        ds = dmix[:, 0:S5_WIDTH]
        dz = ds * yg * sg * (1.0 - sg)
        dzb = dz.astype(BF16)
        _s5_put_rows(dys_ref, dy_scr, (ds * sg + _dot_nt(dzb, wg_ref[...])) * dgel)
        gw_ref[...] += _dot_tn(ygb, dzb)
        gb_ref[...] += jnp.sum(dz, axis=0, keepdims=True)

        for h in range(RET_HEADS):
            sl = slice(RET_DH * h, RET_DH * (h + 1))
            _, hvjp = jax.vjp(_head_norm_gate, yr_ref[:, sl], g_ref[:, sl])
            dyr, dg = hvjp(dmix[:, S5_WIDTH + RET_DH * h:S5_WIDTH + RET_DH * (h + 1)])
            dyr_ref[:, sl] = dyr
            dg_ref[:, sl] = dg

    row = pl.BlockSpec((tm, D_MODEL), lambda i: (i, 0))
    half = pl.BlockSpec((tm, S5_WIDTH), lambda i: (i, 0))
    f32h = jax.ShapeDtypeStruct((l, RET_WIDTH), F32)
    return pl.pallas_call(
        body, name="up_bwd", grid=(nt,),
        in_specs=[pl.BlockSpec((tm, 2 * D_FF), lambda i: (i, 0)), _full((D_MODEL, 2 * D_FF)),
                  _full((D_MODEL, D_MODEL)), row, row, row, _full((3, D_MODEL)), _full((1, D_MODEL)),
                  pl.BlockSpec((tm, S5_WIDTH), lambda i: (i + nct, 0)), half,
                  pl.BlockSpec((tm, RET_WIDTH), lambda i: (i + nct, G_COL // 4)),
                  _full((S5_WIDTH, S5_WIDTH)), _full((1, S5_WIDTH)), ANY] + [ANY] * n,
        out_specs=[row, row, _full((8, D_MODEL)),
                   pl.BlockSpec((S5_NB, tm // S5_T, S5_BW), lambda i: (0, i + nct, 0)), half, half,
                   _full((S5_WIDTH, S5_WIDTH)),
                   _full((1, S5_WIDTH))] + [ANY] * n,
        out_shape=[jax.ShapeDtypeStruct((l, D_MODEL), F32), jax.ShapeDtypeStruct((l, D_MODEL), BF16),
                   jax.ShapeDtypeStruct((8, D_MODEL), F32), jax.ShapeDtypeStruct(zero_rows.shape, BF16), f32h, f32h,
                   jax.ShapeDtypeStruct((S5_WIDTH, S5_WIDTH), F32), jax.ShapeDtypeStruct((1, S5_WIDTH), F32)]
        + [jax.ShapeDtypeStruct((4,) + s, p.dtype) for s, p in zip(shapes, pairs)],
        input_output_aliases={13: 3},
        scratch_shapes=[pltpu.SemaphoreType.DMA((n, 3)), pltpu.SemaphoreType.DMA((n, 3)),
                        pltpu.VMEM((tm // S5_T, S5_T, S5_WIDTH), F32)],
        compiler_params=_params(("arbitrary",)),
    )(dup, w_up_b, w_out_b, x1, dx2, mix, mod3, n2w, y_all, y_ret, p_all, w_glu_b, b_glu, zero_rows, *pairs)


MOD_ROWS = 16
MOD_COLS = 6 * D_MODEL // 4


def _mod_fwd(c_all, c_ctx, w_mod_b, b_loc):
    def body(c_ref, cc_ref, w_ref, b_ref, m_ref, s_ref):
        cond = jnp.concatenate([c_ref[...], jnp.broadcast_to(cc_ref[...], (8, D_MODEL))], axis=0)
        s = _silu(cond).astype(BF16)
        s_ref[...] = s
        m_ref[...] = _dot(s, w_ref[...]) + b_ref[...]

    return pl.pallas_call(
        body, name="mod_fwd",
        out_shape=[jax.ShapeDtypeStruct((MOD_ROWS, MOD_COLS), F32), jax.ShapeDtypeStruct((MOD_ROWS, D_MODEL), BF16)],
        compiler_params=_params(),
    )(c_all, c_ctx, w_mod_b, b_loc)


def _mod_bwd_sum(dm_all):
    def body(d_ref, dm_ref, gb_ref):
        rows = [d_ref[k, 0:1, :] for k in range(8)]
        ctx_sum = d_ref[0, 1:2, :]
        for k in range(1, 8):
            ctx_sum = ctx_sum + d_ref[k, 1:2, :]
        gb = ctx_sum
        for k in range(8):
            gb = gb + rows[k]
        gb_ref[...] = gb
        dm_ref[...] = jnp.concatenate(rows + [ctx_sum] + [jnp.zeros((7, 6 * D_MODEL), F32)], axis=0)

    return pl.pallas_call(
        body, name="mod_bwd_sum",
        out_shape=[jax.ShapeDtypeStruct((MOD_ROWS, 6 * D_MODEL), F32), jax.ShapeDtypeStruct((1, 6 * D_MODEL), F32)],
        compiler_params=_params(),
    )(dm_all)


def _mod_bwd_w(dm_loc, s_b, c_ctx, w_mod_b):
    def body(d_ref, s_ref, cc_ref, w_ref, gw_ref, gc_ref):
        db = d_ref[...].astype(BF16)
        gw_ref[...] = _dot_tn(s_ref[...], db)
        ds = _dot_nt(db, w_ref[...])
        _, vjp = jax.vjp(_silu, cc_ref[...])
        gc_ref[...] = jnp.broadcast_to(vjp(ds[8:9, :])[0], (8, D_MODEL))

    return pl.pallas_call(
        body, name="mod_bwd_w",
        out_shape=[jax.ShapeDtypeStruct((D_MODEL, MOD_COLS), F32), jax.ShapeDtypeStruct((8, D_MODEL), F32)],
        compiler_params=_params(),
    )(dm_loc, s_b, c_ctx, w_mod_b)


def _adamw(w, g, m, v, name):
    r, c = w.shape
    tr = _pick(r, (256, 128, 64, 32, 16, 8))
    bc1 = 1.0 - ADAM_B1 ** ADAM_STEP
    bc2 = 1.0 - ADAM_B2 ** ADAM_STEP

    def body(w_ref, g_ref, m_ref, v_ref, d_ref, nm_ref, nv_ref):
        gg = g_ref[...]
        nm = ADAM_B1 * m_ref[...] + (1.0 - ADAM_B1) * gg
        nv = ADAM_B2 * v_ref[...] + (1.0 - ADAM_B2) * (gg * gg)
        nm_ref[...] = nm
        nv_ref[...] = nv
        d_ref[...] = -ADAM_LR * ((nm / bc1) / (jnp.sqrt(nv / bc2) + ADAM_EPS) + ADAM_WD * w_ref[...])

    blk = pl.BlockSpec((tr, c), lambda i: (i, 0))
    shp = jax.ShapeDtypeStruct((r, c), F32)
    return pl.pallas_call(
        body, name=name, grid=(r // tr,), in_specs=[blk] * 4, out_specs=[blk] * 3, out_shape=[shp] * 3,
        compiler_params=_params(("parallel",)),
    )(w, g, m, v)


def _sum_slots(a, name):
    n, r, c = a.shape
    tr = _pick(r, (376, 256, 208, 128, 64, 32, 16, 8))

    def body(a_ref, o_ref):
        acc = a_ref[0].astype(F32)
        for k in range(1, n):
            acc = acc + a_ref[k].astype(F32)
        o_ref[...] = acc

    return pl.pallas_call(
        body, name=name, grid=(r // tr,),
        in_specs=[pl.BlockSpec((n, tr, c), lambda i: (0, i, 0))],
        out_specs=pl.BlockSpec((tr, c), lambda i: (i, 0)),
        out_shape=jax.ShapeDtypeStruct((r, c), F32),
        compiler_params=_params(("parallel",)),
    )(a)


def _mesh_pos():
    return lax.axis_index("x"), lax.axis_index("y"), lax.axis_index("c")


def _gather8_phases(x_ref, out_ref, send_sems, recv_sems, local_sem, m_per):
    def parts():
        x, y, c = _mesh_pos()
        me, sibling = (x, y, c), (x, y, 1 - c)
        chips = [(1 - x, y), (x, 1 - y), (1 - x, 1 - y)]

        def rows(px, py, pc):
            return out_ref.at[pl.ds((4 * px + 2 * py + pc) * m_per, m_per), :]

        def copy(k, block, to, src=None):
            return pltpu.make_async_remote_copy(
                src_ref=rows(*block) if src is None else src, dst_ref=rows(*block),
                send_sem=send_sems.at[k], recv_sem=recv_sems.at[k], device_id=to, device_id_type=MESH_ID)

        mine = pltpu.make_async_copy(x_ref, rows(*me), local_sem)
        first = [copy(0, me, sibling, src=x_ref)]
        first += [copy(1 + j, me, (*chip, c), src=x_ref) for j, chip in enumerate(chips)]
        return me, sibling, chips, c, copy, mine, first

    def begin():
        *_, mine, first = parts()
        mine.start()
        for cp in first:
            cp.start()

    def finish():
        me, sibling, chips, c, copy, mine, first = parts()
        passed = [copy(4 + j, (*chip, c), sibling) for j, chip in enumerate(chips)]
        for j, chip in enumerate(chips):
            copy(1 + j, (*chip, c), me).wait_recv()
            passed[j].start()
        copy(0, sibling, me).wait_recv()
        for j, chip in enumerate(chips):
            copy(4 + j, (*chip, 1 - c), me).wait_recv()
        for cp in first + passed:
            cp.wait_send()
        mine.wait()

    return begin, finish


GATHER8_SCRATCH = (pltpu.SemaphoreType.DMA((7,)), pltpu.SemaphoreType.DMA((7,)), pltpu.SemaphoreType.DMA)


def _all_gather8(v, name):
    m_per, n = v.shape

    def body(x_ref, out_ref, send_sems, recv_sems, local_sem):
        begin, finish = _gather8_phases(x_ref, out_ref, send_sems, recv_sems, local_sem, m_per)
        begin()
        finish()

    return pl.pallas_call(
        body, name=name,
        out_shape=jax.ShapeDtypeStruct((8 * m_per, n), v.dtype),
        in_specs=[pl.BlockSpec(memory_space=pltpu.VMEM)],
        out_specs=pl.BlockSpec(memory_space=pltpu.VMEM),
        scratch_shapes=list(GATHER8_SCRATCH),
        compiler_params=_params(),
    )(v)


ANY = pl.BlockSpec(memory_space=pl.ANY)
def PEER_CHIPS(x, y):
    return [(x, 1 - y), (1 - x, y), (1 - x, 1 - y)]


def _shard_region(ref, kind, k, rl, cl, r0, nr, c0, nc):
    if kind == "col":
        return ref.at[pl.ds(r0, nr), pl.ds(k * cl + c0, nc)]
    return ref.at[pl.ds(k * rl + r0, nr), pl.ds(c0, nc)]


def _place_shard(w, kind, chip, name):
    rl, cl = w.shape
    tr = _pick(rl, (256, 128, 64))
    nt = rl // tr

    def body(chip_ref, w_ref, o_ref):
        o_ref[...] = w_ref[...].astype(BF16)

    o_map = (lambda i, chip_ref: (i, chip_ref[0])) if kind == "col" else (lambda i, chip_ref: (chip_ref[0] * nt + i, 0))
    return pl.pallas_call(
        body, name=name,
        grid_spec=pltpu.PrefetchScalarGridSpec(
            num_scalar_prefetch=1, grid=(nt,),
            in_specs=[pl.BlockSpec((tr, cl), lambda i, chip_ref: (i, 0))], out_specs=pl.BlockSpec((tr, cl), o_map)),
        out_shape=jax.ShapeDtypeStruct((rl, 4 * cl) if kind == "col" else (4 * rl, cl), BF16),
        compiler_params=_params(("parallel",)),
    )(chip.reshape(1), w)


def _gather_shard_shapes(placed, kinds):
    return [(p.shape[0], p.shape[1] // 4) if k == "col" else (p.shape[0] // 4, p.shape[1]) for p, k in zip(placed, kinds)]


def _gather_chip_copies(outs, kinds, shard_shapes, send_sems, recv_sems, with_arrivals=True):
    x, y, c = _mesh_pos()
    me = 2 * x + y
    sends, arrivals = [], []
    for a in range(len(outs)):
        rl, cl = shard_shapes[a]
        rh = rl // 2
        reg = functools.partial(_shard_region, outs[a], kinds[a], rl=rl, cl=cl, r0=c * rh, nr=rh, c0=0, nc=cl)
        for j, (px, py) in enumerate(PEER_CHIPS(x, y)):
            to = dict(send_sem=send_sems.at[a, j], recv_sem=recv_sems.at[a, j], device_id=(px, py, c),
                      device_id_type=MESH_ID)
            sends.append(pltpu.make_async_remote_copy(src_ref=reg(k=me), dst_ref=reg(k=me), **to))
            if with_arrivals:
                got = reg(k=2 * px + py)
                arrivals.append(pltpu.make_async_remote_copy(src_ref=got, dst_ref=got, **to))
    return sends, arrivals


def _gather_sibling_copies(outs, kinds, shard_shapes, send_sems, recv_sems):
    x, y, c = _mesh_pos()
    forwards, arrivals = [], []
    for a in range(len(outs)):
        rl, cl = shard_shapes[a]
        rh = rl // 2
        for j, (px, py) in enumerate(PEER_CHIPS(x, y)):
            to = dict(send_sem=send_sems.at[a, j], recv_sem=recv_sems.at[a, j], device_id=(x, y, 1 - c),
                      device_id_type=MESH_ID)
            reg = functools.partial(_shard_region, outs[a], kinds[a], k=2 * px + py, rl=rl, cl=cl, nr=rh, c0=0, nc=cl)
            forwards.append(pltpu.make_async_remote_copy(src_ref=reg(r0=c * rh), dst_ref=reg(r0=c * rh), **to))
            arrivals.append(pltpu.make_async_remote_copy(src_ref=reg(r0=(1 - c) * rh), dst_ref=reg(r0=(1 - c) * rh), **to))
    return forwards, arrivals


def _gather_sibling(placed, kinds, name):
    n = len(placed)
    shard_shapes = _gather_shard_shapes(placed, kinds)

    def body(*refs):
        forwards, from_sibling = _gather_sibling_copies(refs[n:2 * n], kinds, shard_shapes, *refs[2 * n:])
        for cp in forwards:
            cp.start()
        for cp in from_sibling:
            cp.wait_recv()
        for cp in forwards:
            cp.wait_send()

    return pl.pallas_call(
        body, name=name,
        out_shape=[jax.ShapeDtypeStruct(p.shape, p.dtype) for p in placed],
        in_specs=[ANY] * n, out_specs=[ANY] * n, input_output_aliases={a: a for a in range(n)},
        scratch_shapes=[pltpu.SemaphoreType.DMA((n, 3))] * 2,
        compiler_params=_params(),
    )(*placed)


def _half(kind, r, c):
    return (r // 2, c) if kind == "col" else (r, c // 2)


def _half_of(ref, kind, which):
    r, c = ref.shape
    hr, hc = _half(kind, r, c)
    return ref.at[pl.ds(which * hr, hr), :] if kind == "col" else ref.at[:, pl.ds(which * hc, hc)]


def _rs_sibling(grads, kinds, name):
    n = len(grads)

    def body(*refs):
        srcs, dsts = refs[:n], refs[n:2 * n]
        send_sems, recv_sems = refs[2 * n:]
        x, y, c = _mesh_pos()
        cps = [pltpu.make_async_remote_copy(src_ref=_half_of(srcs[a], kinds[a], 1 - c), dst_ref=dsts[a],
                                            send_sem=send_sems.at[a], recv_sem=recv_sems.at[a],
                                            device_id=(x, y, 1 - c), device_id_type=MESH_ID) for a in range(n)]
        for cp in cps:
            cp.start()
        for cp in cps:
            cp.wait()

    return pl.pallas_call(
        body, name=name,
        out_shape=[jax.ShapeDtypeStruct(_half(k, *g.shape), g.dtype) for g, k in zip(grads, kinds)],
        in_specs=[ANY] * n, out_specs=[ANY] * n,
        scratch_shapes=[pltpu.SemaphoreType.DMA((n,)), pltpu.SemaphoreType.DMA((n,))],
        compiler_params=_params(),
    )(*grads)


def _pair_sum(gf, rv, kind, ci, name):
    r, c = rv.shape
    tr = _pick(r, (128, 64, 32, 16, 8))
    nt = r // tr

    def body(ci_ref, g_ref, r_ref, o_ref):
        o_ref[...] = (g_ref[...] + r_ref[...]).astype(BF16)

    g_map = (lambda i, ci_ref: (ci_ref[0] * nt + i, 0)) if kind == "col" else (lambda i, ci_ref: (i, ci_ref[0]))
    blk = pl.BlockSpec((tr, c), lambda i, ci_ref: (i, 0))
    return pl.pallas_call(
        body, name=name,
        grid_spec=pltpu.PrefetchScalarGridSpec(num_scalar_prefetch=1, grid=(nt,),
                                               in_specs=[pl.BlockSpec((tr, c), g_map), blk], out_specs=blk),
        out_shape=jax.ShapeDtypeStruct((r, c), BF16),
        compiler_params=_params(("parallel",)),
    )(ci.reshape(1), gf, rv)


def _rs_slot_shapes(pairs, kinds):
    return [(p.shape[0], p.shape[1] // 4) if k == "col" else (p.shape[0] // 4, p.shape[1]) for p, k in zip(pairs, kinds)]


def _rs_chip_copies(srcs, dsts, kinds, shapes, send_sems, recv_sems, with_arrivals=True):
    x, y, c = _mesh_pos()
    me = 2 * x + y
    sends, arrivals = [], []
    for a in range(len(srcs)):
        rl, cl = shapes[a]
        reg = functools.partial(_shard_region, srcs[a], kinds[a], rl=rl, cl=cl, r0=0, nr=rl, c0=0, nc=cl)
        for j, (px, py) in enumerate(PEER_CHIPS(x, y)):
            to = dict(send_sem=send_sems.at[a, j], recv_sem=recv_sems.at[a, j], device_id=(px, py, c),
                      device_id_type=MESH_ID)
            sends.append(pltpu.make_async_remote_copy(src_ref=reg(k=2 * px + py), dst_ref=dsts[a].at[me], **to))
            if with_arrivals:
                slot = dsts[a].at[2 * px + py]
                arrivals.append(pltpu.make_async_remote_copy(src_ref=slot, dst_ref=slot, **to))
    return sends, arrivals


def _rs_chips(pairs, kinds):
    n = len(pairs)
    shapes = _rs_slot_shapes(pairs, kinds)

    def body(*refs):
        sends, arrivals = _rs_chip_copies(refs[:n], refs[n:2 * n], kinds, shapes, *refs[2 * n:])
        for cp in sends:
            cp.start()
        for cp in arrivals:
            cp.wait_recv()
        for cp in sends:
            cp.wait_send()

    return pl.pallas_call(
        body, name="rs_chips",
        out_shape=[jax.ShapeDtypeStruct((4,) + s, p.dtype) for s, p in zip(shapes, pairs)],
        in_specs=[ANY] * n, out_specs=[ANY] * n,
        scratch_shapes=[pltpu.SemaphoreType.DMA((n, 3)), pltpu.SemaphoreType.DMA((n, 3))],
        compiler_params=_params(),
    )(*pairs)


def _sum_chips(pair, got, kind, pos, name):
    _, r, c = got.shape
    tr = _pick(r, (256, 128, 64, 32, 16))
    nt = r // tr

    def body(pos_ref, own_ref, g1_ref, g2_ref, g3_ref, o_ref):
        o_ref[...] = ((own_ref[...].astype(F32) + g1_ref[0].astype(F32)) + g2_ref[0].astype(F32)) + g3_ref[0].astype(F32)

    if kind == "col":
        own_map = lambda i, p: (i, p[1])
        out_map = lambda i, p: (p[0] * nt + i, 0)
        out_shape = (2 * r, c)
    else:
        own_map = lambda i, p: (p[1] * nt + i, 0)
        out_map = lambda i, p: (i, p[0])
        out_shape = (r, 2 * c)
    peer = lambda m: pl.BlockSpec((1, tr, c), lambda i, p: (p[1] ^ m, i, 0))
    return pl.pallas_call(
        body, name=name,
        grid_spec=pltpu.PrefetchScalarGridSpec(
            num_scalar_prefetch=1, grid=(nt,),
            in_specs=[pl.BlockSpec((tr, c), own_map), peer(1), peer(2), peer(3)],
            out_specs=pl.BlockSpec((tr, c), out_map)),
        out_shape=jax.ShapeDtypeStruct(out_shape, F32),
        compiler_params=_params(("parallel",)),
    )(pos, pair, got, got, got)


def _rs_back(halves, kinds):
    n = len(halves)

    def body(*refs):
        outs = refs[n:2 * n]
        send_sems, recv_sems = refs[2 * n:]
        x, y, c = _mesh_pos()
        cps = []
        for a in range(n):
            mine = _half_of(outs[a], kinds[a], c)
            cps.append(pltpu.make_async_remote_copy(src_ref=mine, dst_ref=mine, send_sem=send_sems.at[a],
                                                    recv_sem=recv_sems.at[a], device_id=(x, y, 1 - c),
                                                    device_id_type=MESH_ID))
            cps[-1].start()
        for a in range(n):
            other = _half_of(outs[a], kinds[a], 1 - c)
            pltpu.make_async_remote_copy(src_ref=other, dst_ref=other, send_sem=send_sems.at[a],
                                         recv_sem=recv_sems.at[a], device_id=(x, y, 1 - c),
                                         device_id_type=MESH_ID).wait_recv()
        for cp in cps:
            cp.wait_send()

    return pl.pallas_call(
        body, name="rs_back",
        out_shape=[jax.ShapeDtypeStruct(h.shape, h.dtype) for h in halves],
        in_specs=[ANY] * n, out_specs=[ANY] * n, input_output_aliases={a: a for a in range(n)},
        scratch_shapes=[pltpu.SemaphoreType.DMA((n,)), pltpu.SemaphoreType.DMA((n,))],
        compiler_params=_params(),
    )(*halves)


def _rope_tables(l, lc):
    rows = l // GRID_W
    n_freq = RET_DH // 4
    inv_freq = ROPE_THETA ** (-jnp.arange(n_freq, dtype=F32) / n_freq)
    sign = jnp.tile(jnp.array([-1.0, 1.0], F32), n_freq)

    def half(n):
        ang = jnp.repeat(jnp.arange(n, dtype=F32)[:, None] * inv_freq, 2, axis=-1)
        return jnp.cos(ang), jnp.sin(ang) * sign

    (cr, sr), (cc, sc) = half(rows), half(GRID_W)
    grid = lambda r, c: jnp.concatenate([jnp.repeat(r, GRID_W, axis=0), jnp.tile(c, (rows, 1))], axis=-1)
    cos_t = jnp.concatenate([jnp.ones((lc, RET_DH), F32), grid(cr, cc)], axis=0)
    sin_t = jnp.concatenate([jnp.zeros((lc, RET_DH), F32), grid(sr, sc)], axis=0)
    return cos_t, sin_t


def _s5_pack(a):
    blk = lambda t: t.reshape(1, S5_NB, 128, S5_STATE)
    lre = jnp.stack([a["s5_lambda_re_f"][0], a["s5_lambda_re_b"][0]]).reshape(2, S5_NB, 8, S5_STATE)
    lim = jnp.stack([a["s5_lambda_im_f"][0], a["s5_lambda_im_b"][0]]).reshape(2, S5_NB, 8, S5_STATE)
    lst = jnp.stack([a["s5_log_step_f"][0], a["s5_log_step_b"][0]]).reshape(2, S5_NB, 8, 1)
    b_re = blk(a["s5_b_re"][0].transpose(0, 2, 1))
    b_im = blk(a["s5_b_im"][0].transpose(0, 2, 1))
    return (lre, lim, lst, b_re, b_im, blk(a["s5_c_re"][0]), blk(a["s5_c_im"][0]),
            a["s5_d"].reshape(1, S5_NB, 1, 128))


def _s5_unpack(g):
    glre, glim, glst, gbre, gbim, gcre, gcim, gd = g
    unb = lambda t: t.reshape(S5_GROUPS, S5_GROUP, S5_STATE).transpose(0, 2, 1)[None]
    return {
        "s5_lambda_re_f": glre[0].reshape(1, S5_GROUPS, S5_STATE), "s5_lambda_re_b": glre[1].reshape(1, S5_GROUPS, S5_STATE),
        "s5_lambda_im_f": glim[0].reshape(1, S5_GROUPS, S5_STATE), "s5_lambda_im_b": glim[1].reshape(1, S5_GROUPS, S5_STATE),
        "s5_log_step_f": glst[0].reshape(1, S5_GROUPS), "s5_log_step_b": glst[1].reshape(1, S5_GROUPS),
        "s5_b_re": unb(gbre), "s5_b_im": unb(gbim),
        "s5_c_re": gcre.reshape(1, S5_GROUPS, S5_GROUP, S5_STATE), "s5_c_im": gcim.reshape(1, S5_GROUPS, S5_GROUP, S5_STATE),
        "s5_d": gd.reshape(1, S5_WIDTH),
    }


def _local_step(a, early, late, mx, mc, conv_w, ci):
    x, ctx, tgt = a["x"][0], a["ctx"][0], a["loss_target"][0]
    l, lc = x.shape[0], ctx.shape[0]
    la = l + lc
    nct, ncc, nrc, cn = lc // TOK_TILE, lc // S5_T, lc // RET_CHUNK, la // S5_T
    n1w, n2w, fnw = a["norm1_w"], a["norm2_w"], a["final_norm_w"].reshape(1, D_MODEL)
    conv_b, b_glu = a["conv_b"], a["s5_b_glu"]
    ld2 = jnp.concatenate([a["ret_log_decay_f"], a["ret_log_decay_b"]], axis=0)
    mod4 = jnp.concatenate([mc[0:2], mx[0:2]], axis=0)
    mod3 = mx[2:5]
    gate5 = mx[5:6]
    cos_t, sin_t = _rope_tables(l, lc)
    s5p = _s5_pack(a)

    gg, xw, yw, a16, *early = _s5_gen(*s5p, early, EARLY_KINDS)
    wb = dict(zip(EARLY_NAMES, _gather_sibling(early, EARLY_KINDS, "gather_sibling_early")))
    p_all, h1b, p3, w_up_p = _norm_inproj(x, ctx, n1w, mod4, wb["w_in"], cos_t, sin_t, [late[1]], (LATE_KINDS[1],))
    sloc = _s5_to_states(p3, xw, "s5_state")
    a16s = a16.transpose(1, 0, 2, 3).reshape(2, S5_GROUPS, 128)
    hs, w_out_p, w_down_p = _s5_scan(sloc, a16s, ncc, [late[0], late[2]], (LATE_KINDS[0], LATE_KINDS[2]))
    y_all = _s5_from_states(p3, gg, hs, yw, False, "s5_out").reshape(la, S5_WIDTH)
    tabs = _ret_tables(ld2)
    of, ob, ssf, ssb = _ret_scan(p_all, tabs, nrc)
    wb = {**wb, **dict(zip(LATE_NAMES, _gather_sibling([w_out_p, w_up_p, w_down_p], LATE_KINDS, "gather_sibling_late")))}
    x1, mix, h2b, up, mixb, y_ret = _outproj_up(x, y_all, of, ob, p_all, wb["s5_w_glu"], b_glu, wb["w_out"],
                                                     mod3, n2w, wb["w_up"], nct)
    act, dx2, ddn, dact, acc_f = _ffn_loss(up, x1, conv_w, conv_b, wb["w_down"], gate5, fnw, tgt)

    g = {}
    g["w_down"] = _mm_tn(act, ddn, name="gw_down")
    dup, acc_c = _convglu_bwd(up, dact, conv_w, conv_b)
    g["w_up"], got_up, got_down = _mm_tn_swap(h2b, dup, [g["w_down"]], ["row"], name="gw_up")
    first = [g[n] for n in FIRST_GRADS]
    first_pairs = [_pair_sum(gf, rv, k, ci, "rs_pair_" + n)
                   for gf, rv, k, n in zip(first, (got_down, got_up), FIRST_KINDS, FIRST_GRADS)]
    dx1, dmixb, acc_2, dy3, dy_ret, dg, g["s5_w_glu"], g["s5_b_glu"], *first_got = _up_bwd(
        dup, wb["w_up"], wb["w_out"], x1, dx2, mix, mod3, n2w, y_all, y_ret, p_all, wb["s5_w_glu"], b_glu,
        jnp.zeros(p3.shape, BF16), nct, first_pairs, FIRST_KINDS)
    g["w_out"] = _mm_tn(mixb, dmixb, name="gw_out")

    e = _s5_to_states(dy3, yw, "s5_bwd_h")
    ds, da16 = _s5_scan_bwd(e, hs, a16s, ncc)
    du = _s5_from_states(dy3, gg, ds, xw, True, "s5_bwd_u").reshape(la, S5_WIDTH)
    dkb = _s5_bwd_kb(p3, dy3)
    dwst = _s5_bwd_w(p3, ds, "s5_bwd_wst")
    dwout = _s5_bwd_w(dy3, hs, "s5_bwd_wout")
    da16p = da16.reshape(2, S5_NB, 8, 128).transpose(1, 0, 2, 3)
    g.update(_s5_unpack(_s5_gen_bwd(*s5p, dkb, dwst, dwout, da16p)))

    dqf, dkf, dvf, dqb, dkb_, dvb, dld = _ret_scan_bwd(p_all, tabs, ssf, ssb, dy_ret, nrc)
    g["ret_log_decay_f"] = dld[0, :, 0, 0].reshape(1, RET_HEADS)
    g["ret_log_decay_b"] = dld[1, :, 0, 0].reshape(1, RET_HEADS)
    dp, grad_x, acc_1 = _in_bwd(dqf, dkf, dvf, dqb, dkb_, dvb, du, dg, cos_t, sin_t, wb["w_in"], x, ctx, n1w, mod4, dx1)
    g["norm1_w"], g["norm2_w"], g["final_norm_w"] = acc_1[0:1], acc_2[0:1], acc_f[0]
    g["conv_w"], g["conv_b"] = acc_c[0:3], acc_c[3:4]
    zero = jnp.zeros((1, D_MODEL), F32)
    dmx = jnp.concatenate([acc_1[3:5], acc_2[1:2], acc_2[2:4], acc_f[1:2]], axis=0)
    dmc = jnp.concatenate([acc_1[1:3], zero, zero, zero, zero], axis=0)
    dm_pair = jnp.concatenate([dmx.reshape(1, -1), dmc.reshape(1, -1), jnp.zeros((6, 6 * D_MODEL), F32)], axis=0)
    small = _pack_rows([g[n] for n in SMALL_NAMES] + [acc_f[2, 0:1]])
    g["w_in"], dm_all, small_all = _mm_tn(h1b, dp, name="gw_in", gathered=[dm_pair, small])
    return grad_x, g, dm_all, small_all, first_pairs, first_got


WEIGHT_NAMES = ("c_ctx", "w_mod", "b_mod", "norm1_w", "w_in", "s5_lambda_re_f", "s5_lambda_im_f", "s5_log_step_f",
                "s5_lambda_re_b", "s5_lambda_im_b", "s5_log_step_b", "s5_b_re", "s5_b_im", "s5_c_re", "s5_c_im",
                "s5_d", "s5_w_glu", "s5_b_glu", "ret_log_decay_f", "ret_log_decay_b", "w_out", "norm2_w", "w_up",
                "conv_w", "conv_b", "w_down", "final_norm_w")
BIG_NAMES = ("w_in", "w_out", "w_up", "w_down", "s5_w_glu")
BIG_KINDS = ("col", "row", "col", "row", "row")
EARLY_NAMES, EARLY_KINDS = ("w_in", "s5_w_glu"), ("col", "row")
LATE_NAMES, LATE_KINDS = ("w_out", "w_up", "w_down"), ("row", "col", "row")
FIRST_GRADS, FIRST_KINDS = ("w_down", "w_up"), ("row", "col")
LAST_GRADS, LAST_KINDS = ("w_in", "w_out", "s5_w_glu"), ("col", "row", "row")
SMALL_NAMES = ("norm1_w", "norm2_w", "final_norm_w", "conv_b", "conv_w", "s5_lambda_re_f", "s5_lambda_im_f",
               "s5_log_step_f", "s5_lambda_re_b", "s5_lambda_im_b", "s5_log_step_b", "s5_b_re", "s5_b_im", "s5_c_re",
               "s5_c_im", "s5_d", "s5_b_glu", "ret_log_decay_f", "ret_log_decay_b")
ROW = 1024
N_CHIPS = 4


def _pack_rows(parts):
    flat = jnp.concatenate([p.reshape(-1) for p in parts])
    n = flat.shape[0]
    rows = -(-n // (8 * ROW)) * 8
    return jnp.pad(flat, (0, rows * ROW - n)).reshape(rows, ROW)


def _unpack_rows(packed, shapes):
    flat = packed.reshape(-1)
    out, off = [], 0
    for s in shapes:
        n = math.prod(s)
        out.append(flat[off:off + n].reshape(s))
        off += n
    return out


def _step(a):
    xi, yi, ci = _mesh_pos()
    chip = 2 * xi + yi
    dev = 2 * chip + ci

    cw_loc = a["conv_w"].reshape(-1)
    small_in = jnp.concatenate([a["c"].reshape(-1), jnp.pad(cw_loc, (0, 24 * 128 - cw_loc.shape[0]))]).reshape(32, 128)
    sg = _all_gather8(small_in, "gather_cond").reshape(8, 32, 128)
    c_all = sg[:, 0:8].reshape(8, D_MODEL)
    conv_w = sg[0::2, 8:32].reshape(N_CHIPS, -1)[:, :cw_loc.shape[0]].reshape(N_CHIPS, 3, -1)
    conv_w = conv_w.transpose(1, 0, 2).reshape(3, D_FF)

    placed = {n: _place_shard(a[n][0], k, chip, "place_" + n) for n, k in zip(BIG_NAMES, BIG_KINDS)}
    early = [placed[n] for n in EARLY_NAMES]
    late = [placed[n] for n in LATE_NAMES]

    w_mod_b = a["w_mod"][0].astype(BF16)
    c_ctx = a["c_ctx"].reshape(1, D_MODEL)
    b_loc = lax.dynamic_slice_in_dim(a["b_mod"], chip * MOD_COLS, MOD_COLS, 1)
    m_loc, s_b = _mod_fwd(c_all, c_ctx, w_mod_b, b_loc)
    mg = _all_gather8(m_loc, "gather_mod").reshape(8, MOD_ROWS, MOD_COLS)
    m_full = mg[0::2].transpose(1, 0, 2).reshape(MOD_ROWS, 6 * D_MODEL)
    mx = lax.dynamic_slice_in_dim(m_full, dev, 1, 0).reshape(6, D_MODEL)
    mc = m_full[8].reshape(6, D_MODEL)

    grad_x, g, dm_all, small_all, first_pairs, first_got = _local_step(a, early, late, mx, mc, conv_w, ci)

    dm16, gb_mod = _mod_bwd_sum(dm_all.reshape(8, 8, 6 * D_MODEL))
    dm_loc = lax.dynamic_slice_in_dim(dm16, chip * MOD_COLS, MOD_COLS, 1)
    gw_mod, gcc = _mod_bwd_w(dm_loc, s_b, c_ctx, w_mod_b)

    tot = _sum_slots(small_all.reshape(8, -1, ROW), "sum_small_grads")
    small = dict(zip(SMALL_NAMES + ("loss",), _unpack_rows(tot, [g[n].shape for n in SMALL_NAMES] + [(1,)])))
    loss = small["loss"].reshape(())
    grads = {n: small[n].reshape(a[n].shape) for n in SMALL_NAMES if n != "conv_w"}
    gcc_tot = _sum_slots(_all_gather8(gcc, "gather_c_ctx").reshape(8, 8, D_MODEL), "sum_c_ctx")
    grads["c_ctx"] = (0.5 * gcc_tot[0]).reshape(a["c_ctx"].shape)
    grads["conv_w"] = lax.dynamic_slice_in_dim(small["conv_w"], chip * (D_FF // N_CHIPS), D_FF // N_CHIPS, 1)[None]
    grads["b_mod"] = gb_mod
    grads["w_mod"] = gw_mod[None]

    last = [g[n] for n in LAST_GRADS]
    last_pairs = [_pair_sum(gf, rv, k, ci, "rs_pair_" + n)
                  for gf, rv, k, n in zip(last, _rs_sibling(last, LAST_KINDS, "rs_sibling_last"), LAST_KINDS, LAST_GRADS)]
    last_got = _rs_chips(last_pairs, LAST_KINDS)
    pos = jnp.stack([ci, chip])
    order = FIRST_GRADS + LAST_GRADS
    order_kinds = FIRST_KINDS + LAST_KINDS
    halves = [_sum_chips(p, t, k, pos, "rs_sum_" + n)
              for p, t, k, n in zip(first_pairs + last_pairs, list(first_got) + list(last_got), order_kinds, order)]
    for n, t in zip(order, _rs_back(halves, order_kinds)):
        grads[n] = t[None]

    delta, new_m, new_v = {}, {}, {}
    for n in BIG_NAMES + ("w_mod",):
        for dst, t in zip((delta, new_m, new_v), _adamw(a[n][0], grads[n][0], a["m_" + n][0], a["v_" + n][0], "adamw_" + n)):
            dst[n] = t[None]
    rest = [n for n in WEIGHT_NAMES if n not in BIG_NAMES and n != "w_mod"]
    shapes = [a[n].shape for n in rest]
    pr = lambda pre: _pack_rows([a[pre + n] for n in rest])
    for dst, t in zip((delta, new_m, new_v),
                      _adamw(pr(""), _pack_rows([grads[n] for n in rest]), pr("m_"), pr("v_"), "adamw_small")):
        dst.update(zip(rest, _unpack_rows(t, shapes)))

    return (loss, grad_x[None], *[grads[n] for n in WEIGHT_NAMES], *[delta[n] for n in WEIGHT_NAMES],
            *[new_m[n] for n in WEIGHT_NAMES], *[new_v[n] for n in WEIGHT_NAMES])


def kernel(x, c, ctx, c_ctx, w_mod, b_mod, norm1_w, w_in, s5_lambda_re_f, s5_lambda_im_f, s5_log_step_f, s5_lambda_re_b, s5_lambda_im_b, s5_log_step_b, s5_b_re, s5_b_im, s5_c_re, s5_c_im, s5_d, s5_w_glu, s5_b_glu, ret_log_decay_f, ret_log_decay_b, w_out, norm2_w, w_up, conv_w, conv_b, w_down, final_norm_w, loss_target, m_c_ctx, m_w_mod, m_b_mod, m_norm1_w, m_w_in, m_s5_lambda_re_f, m_s5_lambda_im_f, m_s5_log_step_f, m_s5_lambda_re_b, m_s5_lambda_im_b, m_s5_log_step_b, m_s5_b_re, m_s5_b_im, m_s5_c_re, m_s5_c_im, m_s5_d, m_s5_w_glu, m_s5_b_glu, m_ret_log_decay_f, m_ret_log_decay_b, m_w_out, m_norm2_w, m_w_up, m_conv_w, m_conv_b, m_w_down, m_final_norm_w, v_c_ctx, v_w_mod, v_b_mod, v_norm1_w, v_w_in, v_s5_lambda_re_f, v_s5_lambda_im_f, v_s5_log_step_f, v_s5_lambda_re_b, v_s5_lambda_im_b, v_s5_log_step_b, v_s5_b_re, v_s5_b_im, v_s5_c_re, v_s5_c_im, v_s5_d, v_s5_w_glu, v_s5_b_glu, v_ret_log_decay_f, v_ret_log_decay_b, v_w_out, v_norm2_w, v_w_up, v_conv_w, v_conv_b, v_w_down, v_final_norm_w):
    return _step(dict(locals()))
```
